```python
import math
import jax, jax.numpy as jnp
from jax import lax
import numpy as np

D_MODEL = 1024
BATCH = 16
SEQ = 2048
DEPTH = 1

D_FF = 2816
CHUNK = 128
D_A = 512
SGU_GROUPS = 4
SGU_GROUP_DIM = D_A // SGU_GROUPS
HEAD_DIM = 64
N_Q_HEADS = 8
N_KV_HEADS = 2
Q_PER_KV = N_Q_HEADS // N_KV_HEADS
WINDOW = 128
BLOCK = 128
D_B = N_Q_HEADS * HEAD_DIM
SPLITS = (D_A, D_A, D_B, N_KV_HEADS * HEAD_DIM, N_KV_HEADS * HEAD_DIM, D_MODEL, D_MODEL)
IN_COLS = sum(SPLITS)
N_MOD = 9
EPS = 1e-6
NEG = -1e30

kernel_name = "hybrid_gmlp_swa_sink_macaron_adaln"


def rms_norm(x, g):
    xf = x.astype(jnp.float32)
    y = xf * lax.rsqrt(jnp.mean(xf * xf, axis=-1, keepdims=True) + EPS)
    return (y * g.astype(jnp.float32)).astype(x.dtype)


def layer_norm(x, g, b):
    xf = x.astype(jnp.float32)
    mu = jnp.mean(xf, axis=-1, keepdims=True)
    var = jnp.mean(jnp.square(xf - mu), axis=-1, keepdims=True)
    y = (xf - mu) * lax.rsqrt(var + EPS)
    return (y * g.astype(jnp.float32) + b.astype(jnp.float32)).astype(x.dtype)


def modulate(xn, shift, scale):
    return xn * (1 + scale[:, None, :]) + shift[:, None, :]


def swiglu(x, w_gate, w_up, w_down):
    return (jax.nn.silu(x @ w_gate) * (x @ w_up)) @ w_down


def gmlp_sgu(u, v, g_ln, b_ln, w_s, b_s):
    bsz, seq, _ = v.shape
    n_chunks = seq // CHUNK
    v = layer_norm(v, g_ln, b_ln)
    vg = v.reshape(bsz, n_chunks, CHUNK, SGU_GROUPS, SGU_GROUP_DIM)
    causal = jnp.tril(jnp.ones((CHUNK, CHUNK), dtype=bool))
    ws = jnp.where(causal[None], w_s, jnp.zeros_like(w_s))
    z = jnp.einsum('gts,bnsgc->bntgc', ws, vg)
    z = z + b_s.T[None, None, :, :, None]
    return u * z.reshape(bsz, seq, D_A)


def sliding_window_attention(q, k, v, g_q, g_k, sinks):
    bsz, seq = q.shape[0], q.shape[1]
    nb = seq // BLOCK
    q = rms_norm(q, g_q)
    k = rms_norm(k, g_k)
    pad = ((0, 0), (BLOCK, 0), (0, 0), (0, 0))
    k_prev = jnp.pad(k, pad)[:, :seq]
    v_prev = jnp.pad(v, pad)[:, :seq]
    kb = jnp.concatenate([k_prev.reshape(bsz, nb, BLOCK, N_KV_HEADS, HEAD_DIM),
                          k.reshape(bsz, nb, BLOCK, N_KV_HEADS, HEAD_DIM)], axis=2)
    vb = jnp.concatenate([v_prev.reshape(bsz, nb, BLOCK, N_KV_HEADS, HEAD_DIM),
                          v.reshape(bsz, nb, BLOCK, N_KV_HEADS, HEAD_DIM)], axis=2)
    qb = q.reshape(bsz, nb, BLOCK, N_KV_HEADS, Q_PER_KV, HEAD_DIM)
    scores = jnp.einsum('bnqhgd,bnkhd->bnhgqk', qb, kb).astype(jnp.float32)
    scores = scores * (HEAD_DIM ** -0.5)
    blk = jnp.arange(nb)[:, None, None]
    qpos = blk * BLOCK + jnp.arange(BLOCK)[None, :, None]
    kpos = (blk - 1) * BLOCK + jnp.arange(2 * BLOCK)[None, None, :]
    diff = qpos - kpos
    valid = (diff >= 0) & (diff < WINDOW) & (kpos >= 0)
    scores = jnp.where(valid[None, :, None, None], scores, NEG)
    sink = jnp.broadcast_to(
        sinks.astype(jnp.float32).reshape(1, 1, N_KV_HEADS, Q_PER_KV, 1, 1),
        scores.shape[:-1] + (1,))
    probs = jax.nn.softmax(jnp.concatenate([scores, sink], axis=-1), axis=-1)[..., :-1]
    out = jnp.einsum('bnhgqk,bnkhd->bnqhgd', probs.astype(v.dtype), vb)
    return out.reshape(bsz, seq, D_B)


def _fwd_setup_inputs(seed: int = 0) -> dict:
    key = jax.random.key(seed)
    ks = jax.random.split(key, 26)
    f32 = jnp.float32
    L = DEPTH

    def nrm(k, shape, fan_in):
        return jax.random.normal(k, shape, f32) * (fan_in ** -0.5)

    def gain(k, shape):
        return 1.0 + 0.02 * jax.random.normal(k, shape, f32)

    return {
        "x": jax.random.normal(ks[0], (BATCH, SEQ, D_MODEL), f32),
        "c": jax.random.normal(ks[1], (BATCH, D_MODEL), f32),
        "w_ada": nrm(ks[2], (L, D_MODEL, N_MOD * D_MODEL), D_MODEL) * 0.5,
        "b_ada": 0.01 * jax.random.normal(ks[3], (L, N_MOD * D_MODEL), f32),
        "g_norm1": gain(ks[4], (L, D_MODEL)),
        "ffn1_w_gate": nrm(ks[5], (L, D_MODEL, D_FF), D_MODEL),
        "ffn1_w_up": nrm(ks[6], (L, D_MODEL, D_FF), D_MODEL),
        "ffn1_w_down": nrm(ks[7], (L, D_FF, D_MODEL), D_FF),
        "g_norm2": gain(ks[8], (L, D_MODEL)),
        "w_in": nrm(ks[9], (L, D_MODEL, IN_COLS), D_MODEL),
        "g_sgu_ln": gain(ks[10], (L, D_A)),
        "b_sgu_ln": 0.01 * jax.random.normal(ks[11], (L, D_A), f32),
        "w_spatial": nrm(ks[12], (L, SGU_GROUPS, CHUNK, CHUNK), CHUNK),
        "b_spatial": 1.0 + 0.02 * jax.random.normal(ks[13], (L, SGU_GROUPS, CHUNK), f32),
        "g_q": gain(ks[14], (L, HEAD_DIM)),
        "g_k": gain(ks[15], (L, HEAD_DIM)),
        "attn_sinks": jax.random.normal(ks[16], (L, N_Q_HEADS), f32),
        "w_branch_a": nrm(ks[17], (L, D_A, D_MODEL), D_A),
        "w_branch_b": nrm(ks[18], (L, D_B, D_MODEL), D_B),
        "w_out": nrm(ks[19], (L, D_MODEL, D_MODEL), D_MODEL),
        "g_norm3": gain(ks[20], (L, D_MODEL)),
        "ffn2_w_gate": nrm(ks[21], (L, D_MODEL, D_FF), D_MODEL),
        "ffn2_w_up": nrm(ks[22], (L, D_MODEL, D_FF), D_MODEL),
        "ffn2_w_down": nrm(ks[23], (L, D_FF, D_MODEL), D_FF),
    }


def _fwd_reference(x, c, w_ada, b_ada, g_norm1, ffn1_w_gate, ffn1_w_up, ffn1_w_down,
              g_norm2, w_in, g_sgu_ln, b_sgu_ln, w_spatial, b_spatial, g_q, g_k,
              attn_sinks, w_branch_a, w_branch_b, w_out, g_norm3,
              ffn2_w_gate, ffn2_w_up, ffn2_w_down):
    bsz, seq, _ = x.shape
    cond = jax.nn.silu(c)
    split_idx = list(np.cumsum(SPLITS)[:-1])
    h = x
    for l in range(DEPTH):
        mods = cond @ w_ada[l] + b_ada[l]
        sh1, sc1, ga1, sh2, sc2, ga2, sh3, sc3, ga3 = jnp.split(mods, N_MOD, axis=-1)

        xn = modulate(rms_norm(h, g_norm1[l]), sh1, sc1)
        h = h + 0.5 * ga1[:, None, :] * swiglu(xn, ffn1_w_gate[l], ffn1_w_up[l], ffn1_w_down[l])

        xn = modulate(rms_norm(h, g_norm2[l]), sh2, sc2)
        proj = xn @ w_in[l]
        a_u, a_v, q, k, v, gate_a, gate_b = jnp.split(proj, split_idx, axis=-1)

        y_a = gmlp_sgu(jax.nn.gelu(a_u), jax.nn.gelu(a_v), g_sgu_ln[l], b_sgu_ln[l],
                       w_spatial[l], b_spatial[l]) @ w_branch_a[l]

        y_b = sliding_window_attention(
            q.reshape(bsz, seq, N_Q_HEADS, HEAD_DIM),
            k.reshape(bsz, seq, N_KV_HEADS, HEAD_DIM),
            v.reshape(bsz, seq, N_KV_HEADS, HEAD_DIM),
            g_q[l], g_k[l], attn_sinks[l]) @ w_branch_b[l]

        merged = jax.nn.sigmoid(gate_a) * y_a + jax.nn.sigmoid(gate_b) * y_b
        h = h + ga2[:, None, :] * (merged @ w_out[l])

        xn = modulate(rms_norm(h, g_norm3[l]), sh3, sc3)
        h = h + 0.5 * ga3[:, None, :] * swiglu(xn, ffn2_w_gate[l], ffn2_w_up[l], ffn2_w_down[l])
    return h


import jax as _jax
import jax.numpy as _jnp

TWIN_FORMAT = 'train_step'
FWD_PARAMS = ['x', 'c', 'w_ada', 'b_ada', 'g_norm1', 'ffn1_w_gate', 'ffn1_w_up', 'ffn1_w_down', 'g_norm2', 'w_in', 'g_sgu_ln', 'b_sgu_ln', 'w_spatial', 'b_spatial', 'g_q', 'g_k', 'attn_sinks', 'w_branch_a', 'w_branch_b', 'w_out', 'g_norm3', 'ffn2_w_gate', 'ffn2_w_up', 'ffn2_w_down']
TWIN_WEIGHTS = ['w_ada', 'b_ada', 'g_norm1', 'ffn1_w_gate', 'ffn1_w_up', 'ffn1_w_down', 'g_norm2', 'w_in', 'g_sgu_ln', 'b_sgu_ln', 'w_spatial', 'b_spatial', 'g_q', 'g_k', 'attn_sinks', 'w_branch_a', 'w_branch_b', 'w_out', 'g_norm3', 'ffn2_w_gate', 'ffn2_w_up', 'ffn2_w_down']
TWIN_DIFF_INPUT = 'x'
TWIN_INPUTS = ['x', 'c', 'w_ada', 'b_ada', 'g_norm1', 'ffn1_w_gate', 'ffn1_w_up', 'ffn1_w_down', 'g_norm2', 'w_in', 'g_sgu_ln', 'b_sgu_ln', 'w_spatial', 'b_spatial', 'g_q', 'g_k', 'attn_sinks', 'w_branch_a', 'w_branch_b', 'w_out', 'g_norm3', 'ffn2_w_gate', 'ffn2_w_up', 'ffn2_w_down', 'loss_target', 'm_w_ada', 'm_b_ada', 'm_g_norm1', 'm_ffn1_w_gate', 'm_ffn1_w_up', 'm_ffn1_w_down', 'm_g_norm2', 'm_w_in', 'm_g_sgu_ln', 'm_b_sgu_ln', 'm_w_spatial', 'm_b_spatial', 'm_g_q', 'm_g_k', 'm_attn_sinks', 'm_w_branch_a', 'm_w_branch_b', 'm_w_out', 'm_g_norm3', 'm_ffn2_w_gate', 'm_ffn2_w_up', 'm_ffn2_w_down', 'v_w_ada', 'v_b_ada', 'v_g_norm1', 'v_ffn1_w_gate', 'v_ffn1_w_up', 'v_ffn1_w_down', 'v_g_norm2', 'v_w_in', 'v_g_sgu_ln', 'v_b_sgu_ln', 'v_w_spatial', 'v_b_spatial', 'v_g_q', 'v_g_k', 'v_attn_sinks', 'v_w_branch_a', 'v_w_branch_b', 'v_w_out', 'v_g_norm3', 'v_ffn2_w_gate', 'v_ffn2_w_up', 'v_ffn2_w_down']
TWIN_OUTPUTS = ['loss', 'grad_x', 'grad_w_ada', 'grad_b_ada', 'grad_g_norm1', 'grad_ffn1_w_gate', 'grad_ffn1_w_up', 'grad_ffn1_w_down', 'grad_g_norm2', 'grad_w_in', 'grad_g_sgu_ln', 'grad_b_sgu_ln', 'grad_w_spatial', 'grad_b_spatial', 'grad_g_q', 'grad_g_k', 'grad_attn_sinks', 'grad_w_branch_a', 'grad_w_branch_b', 'grad_w_out', 'grad_g_norm3', 'grad_ffn2_w_gate', 'grad_ffn2_w_up', 'grad_ffn2_w_down', 'delta_w_ada', 'delta_b_ada', 'delta_g_norm1', 'delta_ffn1_w_gate', 'delta_ffn1_w_up', 'delta_ffn1_w_down', 'delta_g_norm2', 'delta_w_in', 'delta_g_sgu_ln', 'delta_b_sgu_ln', 'delta_w_spatial', 'delta_b_spatial', 'delta_g_q', 'delta_g_k', 'delta_attn_sinks', 'delta_w_branch_a', 'delta_w_branch_b', 'delta_w_out', 'delta_g_norm3', 'delta_ffn2_w_gate', 'delta_ffn2_w_up', 'delta_ffn2_w_down', 'new_m_w_ada', 'new_m_b_ada', 'new_m_g_norm1', 'new_m_ffn1_w_gate', 'new_m_ffn1_w_up', 'new_m_ffn1_w_down', 'new_m_g_norm2', 'new_m_w_in', 'new_m_g_sgu_ln', 'new_m_b_sgu_ln', 'new_m_w_spatial', 'new_m_b_spatial', 'new_m_g_q', 'new_m_g_k', 'new_m_attn_sinks', 'new_m_w_branch_a', 'new_m_w_branch_b', 'new_m_w_out', 'new_m_g_norm3', 'new_m_ffn2_w_gate', 'new_m_ffn2_w_up', 'new_m_ffn2_w_down', 'new_v_w_ada', 'new_v_b_ada', 'new_v_g_norm1', 'new_v_ffn1_w_gate', 'new_v_ffn1_w_up', 'new_v_ffn1_w_down', 'new_v_g_norm2', 'new_v_w_in', 'new_v_g_sgu_ln', 'new_v_b_sgu_ln', 'new_v_w_spatial', 'new_v_b_spatial', 'new_v_g_q', 'new_v_g_k', 'new_v_attn_sinks', 'new_v_w_branch_a', 'new_v_w_branch_b', 'new_v_w_out', 'new_v_g_norm3', 'new_v_ffn2_w_gate', 'new_v_ffn2_w_up', 'new_v_ffn2_w_down']
TWIN_LEAF_KINDS = {'loss': 'loss', 'grad_x': 'grad_x', 'grad_w_ada': 'grad_w', 'grad_b_ada': 'grad_w', 'grad_g_norm1': 'grad_w', 'grad_ffn1_w_gate': 'grad_w', 'grad_ffn1_w_up': 'grad_w', 'grad_ffn1_w_down': 'grad_w', 'grad_g_norm2': 'grad_w', 'grad_w_in': 'grad_w', 'grad_g_sgu_ln': 'grad_w', 'grad_b_sgu_ln': 'grad_w', 'grad_w_spatial': 'grad_w', 'grad_b_spatial': 'grad_w', 'grad_g_q': 'grad_w', 'grad_g_k': 'grad_w', 'grad_attn_sinks': 'grad_w', 'grad_w_branch_a': 'grad_w', 'grad_w_branch_b': 'grad_w', 'grad_w_out': 'grad_w', 'grad_g_norm3': 'grad_w', 'grad_ffn2_w_gate': 'grad_w', 'grad_ffn2_w_up': 'grad_w', 'grad_ffn2_w_down': 'grad_w', 'delta_w_ada': 'delta_w', 'delta_b_ada': 'delta_w', 'delta_g_norm1': 'delta_w', 'delta_ffn1_w_gate': 'delta_w', 'delta_ffn1_w_up': 'delta_w', 'delta_ffn1_w_down': 'delta_w', 'delta_g_norm2': 'delta_w', 'delta_w_in': 'delta_w', 'delta_g_sgu_ln': 'delta_w', 'delta_b_sgu_ln': 'delta_w', 'delta_w_spatial': 'delta_w', 'delta_b_spatial': 'delta_w', 'delta_g_q': 'delta_w', 'delta_g_k': 'delta_w', 'delta_attn_sinks': 'delta_w', 'delta_w_branch_a': 'delta_w', 'delta_w_branch_b': 'delta_w', 'delta_w_out': 'delta_w', 'delta_g_norm3': 'delta_w', 'delta_ffn2_w_gate': 'delta_w', 'delta_ffn2_w_up': 'delta_w', 'delta_ffn2_w_down': 'delta_w', 'new_m_w_ada': 'new_m', 'new_m_b_ada': 'new_m', 'new_m_g_norm1': 'new_m', 'new_m_ffn1_w_gate': 'new_m', 'new_m_ffn1_w_up': 'new_m', 'new_m_ffn1_w_down': 'new_m', 'new_m_g_norm2': 'new_m', 'new_m_w_in': 'new_m', 'new_m_g_sgu_ln': 'new_m', 'new_m_b_sgu_ln': 'new_m', 'new_m_w_spatial': 'new_m', 'new_m_b_spatial': 'new_m', 'new_m_g_q': 'new_m', 'new_m_g_k': 'new_m', 'new_m_attn_sinks': 'new_m', 'new_m_w_branch_a': 'new_m', 'new_m_w_branch_b': 'new_m', 'new_m_w_out': 'new_m', 'new_m_g_norm3': 'new_m', 'new_m_ffn2_w_gate': 'new_m', 'new_m_ffn2_w_up': 'new_m', 'new_m_ffn2_w_down': 'new_m', 'new_v_w_ada': 'new_v', 'new_v_b_ada': 'new_v', 'new_v_g_norm1': 'new_v', 'new_v_ffn1_w_gate': 'new_v', 'new_v_ffn1_w_up': 'new_v', 'new_v_ffn1_w_down': 'new_v', 'new_v_g_norm2': 'new_v', 'new_v_w_in': 'new_v', 'new_v_g_sgu_ln': 'new_v', 'new_v_b_sgu_ln': 'new_v', 'new_v_w_spatial': 'new_v', 'new_v_b_spatial': 'new_v', 'new_v_g_q': 'new_v', 'new_v_g_k': 'new_v', 'new_v_attn_sinks': 'new_v', 'new_v_w_branch_a': 'new_v', 'new_v_w_branch_b': 'new_v', 'new_v_w_out': 'new_v', 'new_v_g_norm3': 'new_v', 'new_v_ffn2_w_gate': 'new_v', 'new_v_ffn2_w_up': 'new_v', 'new_v_ffn2_w_down': 'new_v'}


def _forward(args):
    return _fwd_reference(*[args[k] for k in FWD_PARAMS])


def _output_shape():
    out = _jax.eval_shape(lambda: _forward(_fwd_setup_inputs(0)))
    return out.shape, out.dtype

N_MICROBATCH = 1
ADAM_LR = 0.001
ADAM_B1 = 0.9
ADAM_B2 = 0.999
ADAM_EPS = 1e-08
ADAM_WD = 0.01
ADAM_STEP = 10
PER_EXAMPLE_BATCH_AXIS = {'x': 0, 'c': 0, 'loss_target': 0}
SHARED_INPUTS = []
_WEIGHT_DTYPES = {'w_ada': _jnp.float32, 'b_ada': _jnp.float32, 'g_norm1': _jnp.float32, 'ffn1_w_gate': _jnp.float32, 'ffn1_w_up': _jnp.float32, 'ffn1_w_down': _jnp.float32, 'g_norm2': _jnp.float32, 'w_in': _jnp.float32, 'g_sgu_ln': _jnp.float32, 'b_sgu_ln': _jnp.float32, 'w_spatial': _jnp.float32, 'b_spatial': _jnp.float32, 'g_q': _jnp.float32, 'g_k': _jnp.float32, 'attn_sinks': _jnp.float32, 'w_branch_a': _jnp.float32, 'w_branch_b': _jnp.float32, 'w_out': _jnp.float32, 'g_norm3': _jnp.float32, 'ffn2_w_gate': _jnp.float32, 'ffn2_w_up': _jnp.float32, 'ffn2_w_down': _jnp.float32}
MOMENT_SCALE = {'w_ada': 3.617752e-01, 'b_ada': 7.616074e-01, 'g_norm1': 7.515064e-01, 'ffn1_w_gate': 2.197570e-02, 'ffn1_w_up': 2.112829e-02, 'ffn1_w_down': 3.429790e-02, 'g_norm2': 8.275465e-01, 'w_in': 9.150437e-02, 'g_sgu_ln': 4.676429e-01, 'b_sgu_ln': 6.369388e-02, 'w_spatial': 5.039719e-02, 'b_spatial': 9.063384e-01, 'g_q': 1.521138e-01, 'g_k': 1.514851e-01, 'attn_sinks': 7.896832e-02, 'w_branch_a': 1.425713e-01, 'w_branch_b': 9.870712e-02, 'w_out': 1.351198e-01, 'g_norm3': 8.263692e-01, 'ffn2_w_gate': 2.241116e-02, 'ffn2_w_up': 2.100902e-02, 'ffn2_w_down': 3.431637e-02}


def _to_microbatches(a, axis):
    t = _jnp.moveaxis(a, axis, 0)
    t = t.reshape((N_MICROBATCH, t.shape[0] // N_MICROBATCH) + t.shape[1:])
    return _jnp.moveaxis(t, 1, axis + 1)


def setup_inputs(seed: int = 0) -> dict:
    inp = _fwd_setup_inputs(seed)
    key = _jax.random.fold_in(_jax.random.key(seed), 7919)
    shape, _ = _output_shape()
    out = dict(inp)
    out["loss_target"] = _jax.random.normal(_jax.random.fold_in(key, 0), shape, _jnp.float32)
    for i, name in enumerate(TWIN_WEIGHTS):
        w = inp[name].astype(_jnp.float32)
        if MOMENT_SCALE is None:
            s = _jnp.sqrt(_jnp.mean(_jnp.square(w)) + 1e-30)
        else:
            s = MOMENT_SCALE[name]
        km, kv = _jax.random.split(_jax.random.fold_in(key, i + 1))
        out[name] = w
        out["m_" + name] = s * _jax.random.normal(km, w.shape, _jnp.float32)
        out["v_" + name] = (s * s) * _jax.random.uniform(kv, w.shape, _jnp.float32, 0.5, 1.5)
    if N_MICROBATCH > 1:
        for name, axis in PER_EXAMPLE_BATCH_AXIS.items():
            out[name] = _to_microbatches(out[name], axis)
    return {'x': out['x'], 'c': out['c'], 'w_ada': out['w_ada'], 'b_ada': out['b_ada'], 'g_norm1': out['g_norm1'], 'ffn1_w_gate': out['ffn1_w_gate'], 'ffn1_w_up': out['ffn1_w_up'], 'ffn1_w_down': out['ffn1_w_down'], 'g_norm2': out['g_norm2'], 'w_in': out['w_in'], 'g_sgu_ln': out['g_sgu_ln'], 'b_sgu_ln': out['b_sgu_ln'], 'w_spatial': out['w_spatial'], 'b_spatial': out['b_spatial'], 'g_q': out['g_q'], 'g_k': out['g_k'], 'attn_sinks': out['attn_sinks'], 'w_branch_a': out['w_branch_a'], 'w_branch_b': out['w_branch_b'], 'w_out': out['w_out'], 'g_norm3': out['g_norm3'], 'ffn2_w_gate': out['ffn2_w_gate'], 'ffn2_w_up': out['ffn2_w_up'], 'ffn2_w_down': out['ffn2_w_down'], 'loss_target': out['loss_target'], 'm_w_ada': out['m_w_ada'], 'm_b_ada': out['m_b_ada'], 'm_g_norm1': out['m_g_norm1'], 'm_ffn1_w_gate': out['m_ffn1_w_gate'], 'm_ffn1_w_up': out['m_ffn1_w_up'], 'm_ffn1_w_down': out['m_ffn1_w_down'], 'm_g_norm2': out['m_g_norm2'], 'm_w_in': out['m_w_in'], 'm_g_sgu_ln': out['m_g_sgu_ln'], 'm_b_sgu_ln': out['m_b_sgu_ln'], 'm_w_spatial': out['m_w_spatial'], 'm_b_spatial': out['m_b_spatial'], 'm_g_q': out['m_g_q'], 'm_g_k': out['m_g_k'], 'm_attn_sinks': out['m_attn_sinks'], 'm_w_branch_a': out['m_w_branch_a'], 'm_w_branch_b': out['m_w_branch_b'], 'm_w_out': out['m_w_out'], 'm_g_norm3': out['m_g_norm3'], 'm_ffn2_w_gate': out['m_ffn2_w_gate'], 'm_ffn2_w_up': out['m_ffn2_w_up'], 'm_ffn2_w_down': out['m_ffn2_w_down'], 'v_w_ada': out['v_w_ada'], 'v_b_ada': out['v_b_ada'], 'v_g_norm1': out['v_g_norm1'], 'v_ffn1_w_gate': out['v_ffn1_w_gate'], 'v_ffn1_w_up': out['v_ffn1_w_up'], 'v_ffn1_w_down': out['v_ffn1_w_down'], 'v_g_norm2': out['v_g_norm2'], 'v_w_in': out['v_w_in'], 'v_g_sgu_ln': out['v_g_sgu_ln'], 'v_b_sgu_ln': out['v_b_sgu_ln'], 'v_w_spatial': out['v_w_spatial'], 'v_b_spatial': out['v_b_spatial'], 'v_g_q': out['v_g_q'], 'v_g_k': out['v_g_k'], 'v_attn_sinks': out['v_attn_sinks'], 'v_w_branch_a': out['v_w_branch_a'], 'v_w_branch_b': out['v_w_branch_b'], 'v_w_out': out['v_w_out'], 'v_g_norm3': out['v_g_norm3'], 'v_ffn2_w_gate': out['v_ffn2_w_gate'], 'v_ffn2_w_up': out['v_ffn2_w_up'], 'v_ffn2_w_down': out['v_ffn2_w_down']}


def _loss(weights, diff, rest, loss_target):
    with _jax.named_scope("forward"):
        args = {**rest, TWIN_DIFF_INPUT: diff, **{k: w.astype(_WEIGHT_DTYPES[k]) for k, w in weights.items()}}
        y = _forward(args)
    with _jax.named_scope("loss_head"):
        err = _jnp.square(y.astype(_jnp.float32) - loss_target)
        return 0.5 * _jnp.sum(_jnp.mean(err, axis=-1)) if err.ndim else 0.5 * err


def _adamw(w, g, m, v):
    m = ADAM_B1 * m + (1.0 - ADAM_B1) * g
    v = ADAM_B2 * v + (1.0 - ADAM_B2) * _jnp.square(g)
    m_hat = m / (1.0 - ADAM_B1 ** ADAM_STEP)
    v_hat = v / (1.0 - ADAM_B2 ** ADAM_STEP)
    delta = -ADAM_LR * (m_hat / (_jnp.sqrt(v_hat) + ADAM_EPS) + ADAM_WD * w)
    return delta, m, v


def reference(x, c, w_ada, b_ada, g_norm1, ffn1_w_gate, ffn1_w_up, ffn1_w_down, g_norm2, w_in, g_sgu_ln, b_sgu_ln, w_spatial, b_spatial, g_q, g_k, attn_sinks, w_branch_a, w_branch_b, w_out, g_norm3, ffn2_w_gate, ffn2_w_up, ffn2_w_down, loss_target, m_w_ada, m_b_ada, m_g_norm1, m_ffn1_w_gate, m_ffn1_w_up, m_ffn1_w_down, m_g_norm2, m_w_in, m_g_sgu_ln, m_b_sgu_ln, m_w_spatial, m_b_spatial, m_g_q, m_g_k, m_attn_sinks, m_w_branch_a, m_w_branch_b, m_w_out, m_g_norm3, m_ffn2_w_gate, m_ffn2_w_up, m_ffn2_w_down, v_w_ada, v_b_ada, v_g_norm1, v_ffn1_w_gate, v_ffn1_w_up, v_ffn1_w_down, v_g_norm2, v_w_in, v_g_sgu_ln, v_b_sgu_ln, v_w_spatial, v_b_spatial, v_g_q, v_g_k, v_attn_sinks, v_w_branch_a, v_w_branch_b, v_w_out, v_g_norm3, v_ffn2_w_gate, v_ffn2_w_up, v_ffn2_w_down):
    given = dict(x=x, c=c, w_ada=w_ada, b_ada=b_ada, g_norm1=g_norm1, ffn1_w_gate=ffn1_w_gate, ffn1_w_up=ffn1_w_up, ffn1_w_down=ffn1_w_down, g_norm2=g_norm2, w_in=w_in, g_sgu_ln=g_sgu_ln, b_sgu_ln=b_sgu_ln, w_spatial=w_spatial, b_spatial=b_spatial, g_q=g_q, g_k=g_k, attn_sinks=attn_sinks, w_branch_a=w_branch_a, w_branch_b=w_branch_b, w_out=w_out, g_norm3=g_norm3, ffn2_w_gate=ffn2_w_gate, ffn2_w_up=ffn2_w_up, ffn2_w_down=ffn2_w_down, loss_target=loss_target, m_w_ada=m_w_ada, m_b_ada=m_b_ada, m_g_norm1=m_g_norm1, m_ffn1_w_gate=m_ffn1_w_gate, m_ffn1_w_up=m_ffn1_w_up, m_ffn1_w_down=m_ffn1_w_down, m_g_norm2=m_g_norm2, m_w_in=m_w_in, m_g_sgu_ln=m_g_sgu_ln, m_b_sgu_ln=m_b_sgu_ln, m_w_spatial=m_w_spatial, m_b_spatial=m_b_spatial, m_g_q=m_g_q, m_g_k=m_g_k, m_attn_sinks=m_attn_sinks, m_w_branch_a=m_w_branch_a, m_w_branch_b=m_w_branch_b, m_w_out=m_w_out, m_g_norm3=m_g_norm3, m_ffn2_w_gate=m_ffn2_w_gate, m_ffn2_w_up=m_ffn2_w_up, m_ffn2_w_down=m_ffn2_w_down, v_w_ada=v_w_ada, v_b_ada=v_b_ada, v_g_norm1=v_g_norm1, v_ffn1_w_gate=v_ffn1_w_gate, v_ffn1_w_up=v_ffn1_w_up, v_ffn1_w_down=v_ffn1_w_down, v_g_norm2=v_g_norm2, v_w_in=v_w_in, v_g_sgu_ln=v_g_sgu_ln, v_b_sgu_ln=v_b_sgu_ln, v_w_spatial=v_w_spatial, v_b_spatial=v_b_spatial, v_g_q=v_g_q, v_g_k=v_g_k, v_attn_sinks=v_attn_sinks, v_w_branch_a=v_w_branch_a, v_w_branch_b=v_w_branch_b, v_w_out=v_w_out, v_g_norm3=v_g_norm3, v_ffn2_w_gate=v_ffn2_w_gate, v_ffn2_w_up=v_ffn2_w_up, v_ffn2_w_down=v_ffn2_w_down)
    weights = {n: given[n] for n in TWIN_WEIGHTS}
    shared = {n: given[n] for n in SHARED_INPUTS}
    per_example = {n: given[n] for n in ['x', 'c']}
    grad_fn = _jax.value_and_grad(_loss, argnums=(0, 1))

    def one_microbatch(ex, loss_target):
        ex = dict(ex)
        diff = ex.pop(TWIN_DIFF_INPUT)
        return grad_fn(weights, diff, {**shared, **ex}, loss_target)

    if N_MICROBATCH == 1:
        loss, (grad_w, grad_x) = one_microbatch(per_example, given["loss_target"])
    else:
        def body(carry, xs):
            loss_sum, grad_sum = carry
            l_k, (gw_k, gx_k) = one_microbatch(xs[0], xs[1])
            with _jax.named_scope("update"):
                return (loss_sum + l_k, _jax.tree.map(_jnp.add, grad_sum, gw_k)), gx_k

        init = (_jnp.zeros((), _jnp.float32), _jax.tree.map(_jnp.zeros_like, weights))
        (loss, grad_w), grad_x = _jax.lax.scan(body, init, (per_example, given["loss_target"]))
    with _jax.named_scope("update"):
        delta_w, new_m, new_v = {}, {}, {}
        for n in TWIN_WEIGHTS:
            delta_w[n], new_m[n], new_v[n] = _adamw(weights[n], grad_w[n], given["m_" + n], given["v_" + n])
    return (loss, grad_x, *[grad_w[n] for n in TWIN_WEIGHTS], *[delta_w[n] for n in TWIN_WEIGHTS],
            *[new_m[n] for n in TWIN_WEIGHTS], *[new_v[n] for n in TWIN_WEIGHTS])
```

```python
import functools
import math

import jax
import jax.numpy as jnp
from jax import lax
from jax.experimental import pallas as pl
from jax.experimental.pallas import tpu as pltpu

F32 = jnp.float32
BF16 = jnp.bfloat16
EPS = 1e-6
NEG = -1e30
N_DEV = 8
N_QUAD = 4
D_A = 512
N_GROUPS = 4
CHUNK = 128
HEAD_DIM = 64
N_KV = 2
Q_PER_KV = 4
N_Q = N_KV * Q_PER_KV
D_B = N_Q * HEAD_DIM
QKV_W = D_B + 2 * N_KV * HEAD_DIM
K_OFF = D_B
V_OFF = D_B + N_KV * HEAD_DIM
ATT_SCALE = HEAD_DIM ** -0.5
GELU_K = math.sqrt(2.0 / math.pi)
GELU_C = 0.044715
ADAM_LR, ADAM_B1, ADAM_B2, ADAM_EPS, ADAM_WD, ADAM_STEP = 0.001, 0.9, 0.999, 1e-08, 0.01, 10
VMEM_LIMIT_BYTES = 56 * 1024 * 1024
MESH = pl.DeviceIdType.MESH
NT = (((1,), (1,)), ((), ()))
TN = (((0,), (0,)), ((), ()))


def _cparams(*sem):
    return pltpu.CompilerParams(dimension_semantics=sem, vmem_limit_bytes=VMEM_LIMIT_BYTES)


def _dot(a, b):
    return jnp.dot(a, b, preferred_element_type=F32)


def _dg(a, b, dims):
    return lax.dot_general(a, b, dims, preferred_element_type=F32)


def _gelu_parts(x):
    t = jnp.tanh(GELU_K * (x + GELU_C * x * x * x))
    return 0.5 * x * (1.0 + t), t


def _dgelu(x, t):
    return 0.5 * (1.0 + t) + 0.5 * x * (1.0 - t * t) * (GELU_K * (1.0 + 3.0 * GELU_C * x * x))


def _row_block(rows, target):
    b = min(rows, target)
    while rows % b or b % 8:
        b -= 1
    return b


def norm_mod_fwd(h, g, sc, sh, *, seq, tm, name):
    T, D = h.shape
    B, tps = T // seq, seq // tm

    def body(h_ref, g_ref, sc_ref, sh_ref, o_ref):
        x = h_ref[...]
        r = lax.rsqrt(jnp.mean(x * x, axis=-1, keepdims=True) + EPS)
        y = x * r * g_ref[...]
        o_ref[...] = (y * (1.0 + sc_ref[0]) + sh_ref[0]).astype(o_ref.dtype)

    per_seq = pl.BlockSpec((1, 1, D), lambda i: (i // tps, 0, 0))
    return pl.pallas_call(
        body, name=name, grid=(T // tm,),
        in_specs=[pl.BlockSpec((tm, D), lambda i: (i, 0)), pl.BlockSpec((1, D), lambda i: (0, 0)), per_seq, per_seq],
        out_specs=pl.BlockSpec((tm, D), lambda i: (i, 0)),
        out_shape=jax.ShapeDtypeStruct((T, D), BF16),
        compiler_params=_cparams("parallel"),
    )(h, g, sc.reshape(B, 1, D), sh.reshape(B, 1, D))


def ffn_up(xn, wg, wu, *, tm, name):
    T, D = xn.shape
    nq, _, fs = wg.shape

    def body(x_ref, wg_ref, wu_ref, g_ref, u_ref, a_ref):
        x = x_ref[...]
        g = _dot(x, wg_ref[0])
        u = _dot(x, wu_ref[0])
        g_ref[0] = g.astype(BF16)
        u_ref[0] = u.astype(BF16)
        a_ref[0] = (g * jax.nn.sigmoid(g) * u).astype(BF16)

    w_spec = pl.BlockSpec((1, D, fs), lambda q, i: (q, 0, 0))
    o_spec = pl.BlockSpec((1, tm, fs), lambda q, i: (q, i, 0))
    o_shape = jax.ShapeDtypeStruct((nq, T, fs), BF16)
    return pl.pallas_call(
        body, name=name, grid=(nq, T // tm),
        in_specs=[pl.BlockSpec((tm, D), lambda q, i: (i, 0)), w_spec, w_spec],
        out_specs=[o_spec, o_spec, o_spec], out_shape=[o_shape, o_shape, o_shape],
        compiler_params=_cparams("parallel", "parallel"),
    )(xn, wg, wu)


def ffn_down(a, wd, hin, ga, target=None, *, seq, tm, name):
    nq, T, fs = a.shape
    D = wd.shape[-1]
    B, tps, nt = T // seq, seq // tm, T // tm
    with_loss = target is not None

    def body(*refs):
        if with_loss:
            a_ref, wd_ref, h_ref, ga_ref, t_ref, o_ref, f_ref, l_ref = refs
        else:
            a_ref, wd_ref, h_ref, ga_ref, o_ref, f_ref = refs
        f = _dot(a_ref[0], wd_ref[0])
        for q in range(1, nq):
            f = f + _dot(a_ref[q], wd_ref[q])
        f_ref[...] = f.astype(BF16)
        y = h_ref[...] + (0.5 * ga_ref[0]) * f
        if with_loss:
            e = y - t_ref[...]
            o_ref[...] = e * (1.0 / D)
            l_ref[...] = jnp.full(l_ref.shape, 0.5 * jnp.sum(e * e) * (1.0 / D), F32)
        else:
            o_ref[...] = y

    tile = pl.BlockSpec((tm, D), lambda i: (i, 0))
    in_specs = [pl.BlockSpec((nq, tm, fs), lambda i: (0, i, 0)), pl.BlockSpec((nq, fs, D), lambda i: (0, 0, 0)),
                tile, pl.BlockSpec((1, 1, D), lambda i: (i // tps, 0, 0))]
    out_specs = [tile, tile]
    out_shape = [jax.ShapeDtypeStruct((T, D), F32), jax.ShapeDtypeStruct((T, D), BF16)]
    args = [a, wd, hin, ga.reshape(B, 1, D)]
    if with_loss:
        in_specs.append(tile)
        args.append(target)
        out_specs.append(pl.BlockSpec((1, 8, 128), lambda i: (i, 0, 0)))
        out_shape.append(jax.ShapeDtypeStruct((nt, 8, 128), F32))
    return pl.pallas_call(body, name=name, grid=(nt,), in_specs=in_specs, out_specs=out_specs, out_shape=out_shape,
                          compiler_params=_cparams("parallel"))(*args)


def proj_fwd(xn, ws, *, tm, name):
    T, D = xn.shape
    n = len(ws)

    def body(*refs):
        x = refs[0][...]
        for j in range(n):
            refs[1 + n + j][...] = _dot(x, refs[1 + j][...])

    return pl.pallas_call(
        body, name=name, grid=(T // tm,),
        in_specs=[pl.BlockSpec((tm, D), lambda i: (i, 0))] + [pl.BlockSpec(w.shape, lambda i: (0, 0)) for w in ws],
        out_specs=[pl.BlockSpec((tm, w.shape[1]), lambda i: (i, 0)) for w in ws],
        out_shape=[jax.ShapeDtypeStruct((T, w.shape[1]), F32) for w in ws],
        compiler_params=_cparams("parallel"),
    )(xn, *ws)


def _layer_norm_parts(v):
    mu = jnp.mean(v, axis=-1, keepdims=True)
    d = v - mu
    rstd = lax.rsqrt(jnp.mean(d * d, axis=-1, keepdims=True) + EPS)
    return d * rstd, rstd


def _causal():
    row = lax.broadcasted_iota(jnp.int32, (CHUNK, CHUNK), 0)
    col = lax.broadcasted_iota(jnp.int32, (CHUNK, CHUNK), 1)
    return row >= col


def sgu_fwd(puv, gln, bln, ws, bs_t, *, cpb, name):
    T = puv.shape[0]
    gd = D_A // N_GROUPS
    tm = cpb * CHUNK

    def body(p_ref, gln_ref, bln_ref, ws_ref, bs_ref, o_ref):
        causal = _causal()
        wm = [jnp.where(causal, ws_ref[g], 0.0).astype(BF16) for g in range(N_GROUPS)]
        for j in range(cpb):
            rows = slice(j * CHUNK, (j + 1) * CHUNK)
            u, _ = _gelu_parts(p_ref[rows, 0:D_A])
            vv, _ = _gelu_parts(p_ref[rows, D_A:2 * D_A])
            vhat, _ = _layer_norm_parts(vv)
            vn = (vhat * gln_ref[...] + bln_ref[...]).astype(BF16)
            for g in range(N_GROUPS):
                cols = slice(g * gd, (g + 1) * gd)
                z = _dot(wm[g], vn[:, cols]) + bs_ref[:, g:g + 1]
                o_ref[rows, cols] = (u[:, cols] * z).astype(BF16)

    full = lambda a: pl.BlockSpec(a.shape, lambda i: (0,) * a.ndim)
    return pl.pallas_call(
        body, name=name, grid=(T // tm,),
        in_specs=[pl.BlockSpec((tm, 2 * D_A), lambda i: (i, 0)), full(gln), full(bln), full(ws), full(bs_t)],
        out_specs=pl.BlockSpec((tm, D_A), lambda i: (i, 0)),
        out_shape=jax.ShapeDtypeStruct((T, D_A), BF16),
        compiler_params=_cparams("parallel"),
    )(puv, gln, bln, ws, bs_t)


def _rms_parts(x):
    r = lax.rsqrt(jnp.mean(x * x, axis=-1, keepdims=True) + EPS)
    return x * r, r


def _attn_mask(i):
    qi = lax.broadcasted_iota(jnp.int32, (CHUNK, 2 * CHUNK), 0)
    kj = lax.broadcasted_iota(jnp.int32, (CHUNK, 2 * CHUNK), 1)
    diff = qi + CHUNK - kj
    return (diff >= 0) & (diff < CHUNK) & ((kj >= CHUNK) | (i > 0))


def _attn_probs(qn, kk, valid, sink):
    s = _dg(qn, kk, NT) * ATT_SCALE
    s = jnp.where(valid, s, NEG)
    m = jnp.maximum(jnp.max(s, axis=-1, keepdims=True), sink)
    p = jnp.exp(s - m)
    es = jnp.exp(sink - m)
    inv = 1.0 / (jnp.sum(p, axis=-1, keepdims=True) + es)
    return p * inv, es * inv


def attn_fwd(pqkv, gq, gk, sinks, *, seq, name):
    T = pqkv.shape[0]
    nb = seq // CHUNK

    def body(p_ref, gq_ref, gk_ref, sink_ref, o_ref, kpad, vpad):
        kpad[0:CHUNK, :] = jnp.zeros((CHUNK, HEAD_DIM), BF16)
        vpad[0:CHUNK, :] = jnp.zeros((CHUNK, HEAD_DIM), BF16)
        for h in range(N_KV):
            nk, _ = _rms_parts(p_ref[:, K_OFF + h * HEAD_DIM:K_OFF + (h + 1) * HEAD_DIM])
            kpad[CHUNK:CHUNK + seq, :] = (nk * gk_ref[...]).astype(BF16)
            vpad[CHUNK:CHUNK + seq, :] = p_ref[:, V_OFF + h * HEAD_DIM:V_OFF + (h + 1) * HEAD_DIM].astype(BF16)

            def block(i, carry):
                r0 = pl.multiple_of(i * CHUNK, CHUNK)
                kk = kpad[pl.ds(r0, 2 * CHUNK), :]
                vv = vpad[pl.ds(r0, 2 * CHUNK), :]
                valid = _attn_mask(i)
                for g in range(Q_PER_KV):
                    hq = h * Q_PER_KV + g
                    cols = slice(hq * HEAD_DIM, (hq + 1) * HEAD_DIM)
                    nq, _ = _rms_parts(p_ref[pl.ds(r0, CHUNK), cols])
                    qn = (nq * gq_ref[...]).astype(BF16)
                    probs, _ = _attn_probs(qn, kk, valid, sink_ref[0, hq])
                    o_ref[pl.ds(r0, CHUNK), cols] = _dot(probs.astype(BF16), vv).astype(BF16)
                return carry

            lax.fori_loop(0, nb, block, 0)

    return pl.pallas_call(
        body, name=name, grid=(T // seq,),
        in_specs=[pl.BlockSpec((seq, QKV_W), lambda b: (b, 0)), pl.BlockSpec(gq.shape, lambda b: (0, 0)),
                  pl.BlockSpec(gk.shape, lambda b: (0, 0)), pl.BlockSpec(memory_space=pltpu.SMEM)],
        out_specs=pl.BlockSpec((seq, D_B), lambda b: (b, 0)),
        out_shape=jax.ShapeDtypeStruct((T, D_B), BF16),
        scratch_shapes=[pltpu.VMEM((seq + CHUNK, HEAD_DIM), BF16), pltpu.VMEM((seq + CHUNK, HEAD_DIM), BF16)],
        compiler_params=_cparams("parallel"),
    )(pqkv, gq, gk, sinks)


def mixer_out_fwd(sgu, att, pg, hin, ga, wa, wb, wo, *, seq, tm, name):
    T, D = hin.shape
    B, tps = T // seq, seq // tm

    def body(s_ref, t_ref, pg_ref, h_ref, ga_ref, wa_ref, wb_ref, wo_ref, ya_ref, yb_ref, mg_ref, m_ref, ho_ref):
        ya = _dot(s_ref[...], wa_ref[...])
        yb = _dot(t_ref[...], wb_ref[...])
        merged = jax.nn.sigmoid(pg_ref[:, 0:D]) * ya + jax.nn.sigmoid(pg_ref[:, D:2 * D]) * yb
        mg = merged.astype(BF16)
        m = _dot(mg, wo_ref[...])
        ya_ref[...] = ya.astype(BF16)
        yb_ref[...] = yb.astype(BF16)
        mg_ref[...] = mg
        m_ref[...] = m.astype(BF16)
        ho_ref[...] = h_ref[...] + ga_ref[0] * m

    tile = lambda w: pl.BlockSpec((tm, w), lambda i: (i, 0))
    full = lambda a: pl.BlockSpec(a.shape, lambda i: (0, 0))
    b16 = jax.ShapeDtypeStruct((T, D), BF16)
    return pl.pallas_call(
        body, name=name, grid=(T // tm,),
        in_specs=[tile(D_A), tile(D_B), tile(2 * D), tile(D), pl.BlockSpec((1, 1, D), lambda i: (i // tps, 0, 0)),
                  full(wa), full(wb), full(wo)],
        out_specs=[tile(D)] * 5, out_shape=[b16, b16, b16, b16, jax.ShapeDtypeStruct((T, D), F32)],
        compiler_params=_cparams("parallel"),
    )(sgu, att, pg, hin, ga.reshape(B, 1, D), wa, wb, wo)


def gate_bwd(dh, f, ga, scale, *, seq, tm, name):
    T, D = dh.shape
    B, tps = T // seq, seq // tm

    def body(dh_ref, f_ref, ga_ref, df_ref, dga_ref):
        i = pl.program_id(0)
        d = dh_ref[...]
        df_ref[...] = ((scale * ga_ref[0]) * d).astype(BF16)
        part = scale * jnp.sum(d * f_ref[...].astype(F32), axis=0, keepdims=True)

        @pl.when(i % tps == 0)
        def _():
            dga_ref[0] = part

        @pl.when(i % tps != 0)
        def _():
            dga_ref[0] += part

    tile = pl.BlockSpec((tm, D), lambda i: (i, 0))
    per_seq = pl.BlockSpec((1, 1, D), lambda i: (i // tps, 0, 0))
    df, dga = pl.pallas_call(
        body, name=name, grid=(T // tm,), in_specs=[tile, tile, per_seq], out_specs=[tile, per_seq],
        out_shape=[jax.ShapeDtypeStruct((T, D), BF16), jax.ShapeDtypeStruct((B, 1, D), F32)],
        compiler_params=_cparams("arbitrary"),
    )(dh, f, ga.reshape(B, 1, D))
    return df, dga.reshape(B, D)


def ffn_bwd_act(df, wd, g, u, *, tm, name):
    T, D = df.shape
    nq, fs, _ = wd.shape

    def body(df_ref, wd_ref, g_ref, u_ref, dg_ref, du_ref):
        da = _dg(df_ref[...], wd_ref[0], NT)
        gg = g_ref[0].astype(F32)
        sg = jax.nn.sigmoid(gg)
        du_ref[0] = (da * (gg * sg)).astype(BF16)
        dg_ref[0] = (da * u_ref[0].astype(F32) * (sg * (1.0 + gg * (1.0 - sg)))).astype(BF16)

    act = pl.BlockSpec((1, tm, fs), lambda q, i: (q, i, 0))
    o_shape = jax.ShapeDtypeStruct((nq, T, fs), BF16)
    return pl.pallas_call(
        body, name=name, grid=(nq, T // tm),
        in_specs=[pl.BlockSpec((tm, D), lambda q, i: (i, 0)), pl.BlockSpec((1, fs, D), lambda q, i: (q, 0, 0)), act, act],
        out_specs=[act, act], out_shape=[o_shape, o_shape],
        compiler_params=_cparams("parallel", "parallel"),
    )(df, wd, g, u)


def mm_tn(x, ys, *, tt, name):
    nq = max([a.shape[0] for a in [x] + list(ys) if a.ndim == 3] + [1])
    T, K = x.shape[-2:]
    n = len(ys)
    nt = T // tt

    def body(*refs):
        x_ref, y_refs, o_refs, accs = refs[0], refs[1:1 + n], refs[1 + n:1 + 2 * n], refs[1 + 2 * n:]
        t = pl.program_id(1)
        xv = x_ref[0] if x.ndim == 3 else x_ref[...]
        for j in range(n):
            yv = y_refs[j][0] if ys[j].ndim == 3 else y_refs[j][...]
            part = _dg(xv, yv, TN)

            @pl.when(t == 0)
            def _():
                accs[j][...] = part

            @pl.when(t != 0)
            def _():
                accs[j][...] += part

        @pl.when(t == nt - 1)
        def _():
            for j in range(n):
                if ys[j].ndim == 3 or x.ndim == 3:
                    o_refs[j][0] = accs[j][...].astype(BF16)
                else:
                    o_refs[j][...] = accs[j][...].astype(BF16)

    def spec(a):
        if a.ndim == 3:
            return pl.BlockSpec((1, tt, a.shape[2]), lambda q, t: (q, t, 0))
        return pl.BlockSpec((tt, a.shape[1]), lambda q, t: (t, 0))

    out_specs, out_shape = [], []
    for y in ys:
        N = y.shape[-1]
        if y.ndim == 3 or x.ndim == 3:
            out_specs.append(pl.BlockSpec((1, K, N), lambda q, t: (q, 0, 0)))
            out_shape.append(jax.ShapeDtypeStruct((nq, K, N), BF16))
        else:
            out_specs.append(pl.BlockSpec((K, N), lambda q, t: (0, 0)))
            out_shape.append(jax.ShapeDtypeStruct((K, N), BF16))
    return pl.pallas_call(
        body, name=name, grid=(nq, nt), in_specs=[spec(x)] + [spec(y) for y in ys],
        out_specs=out_specs, out_shape=out_shape,
        scratch_shapes=[pltpu.VMEM((K, y.shape[-1]), F32) for y in ys],
        compiler_params=_cparams("parallel", "arbitrary"),
    )(x, *ys)


def dx_norm_bwd(dys, ws, hin, dh_out, g, sc, *, seq, tm, name):
    T, D = hin.shape
    B, tps = T // seq, seq // tm
    n = len(dys)

    def body(*refs):
        dy_refs, w_refs = refs[:n], refs[n:2 * n]
        h_ref, dho_ref, g_ref, sc_ref, dhi_ref, dsh_ref, dsc_ref, dgn_ref = refs[2 * n:]
        i = pl.program_id(0)
        dxn = None
        for j in range(n):
            if dys[j].ndim == 3:
                for q in range(dys[j].shape[0]):
                    part = _dg(dy_refs[j][q], w_refs[j][q], NT)
                    dxn = part if dxn is None else dxn + part
            else:
                part = _dg(dy_refs[j][...], w_refs[j][...], NT)
                dxn = part if dxn is None else dxn + part
        x = h_ref[...]
        nx, r = _rms_parts(x)
        gain = g_ref[...]
        one_sc = 1.0 + sc_ref[0]
        dsh = jnp.sum(dxn, axis=0, keepdims=True)
        dsc = jnp.sum(dxn * (nx * gain), axis=0, keepdims=True)
        dgn = jnp.sum(dxn * one_sc * nx, axis=0, keepdims=True)
        dn = dxn * one_sc * gain
        dhi_ref[...] = dho_ref[...] + r * (dn - nx * jnp.mean(dn * nx, axis=-1, keepdims=True))

        @pl.when(i % tps == 0)
        def _():
            dsh_ref[0] = dsh
            dsc_ref[0] = dsc

        @pl.when(i % tps != 0)
        def _():
            dsh_ref[0] += dsh
            dsc_ref[0] += dsc

        @pl.when(i == 0)
        def _():
            dgn_ref[...] = dgn

        @pl.when(i != 0)
        def _():
            dgn_ref[...] += dgn

    def dy_spec(a):
        if a.ndim == 3:
            return pl.BlockSpec((a.shape[0], tm, a.shape[2]), lambda i: (0, i, 0))
        return pl.BlockSpec((tm, a.shape[1]), lambda i: (i, 0))

    tile = pl.BlockSpec((tm, D), lambda i: (i, 0))
    per_seq = pl.BlockSpec((1, 1, D), lambda i: (i // tps, 0, 0))
    row = pl.BlockSpec((1, D), lambda i: (0, 0))
    dhi, dsh, dsc, dgn = pl.pallas_call(
        body, name=name, grid=(T // tm,),
        in_specs=[dy_spec(a) for a in dys] + [pl.BlockSpec(w.shape, lambda i, nd=w.ndim: (0,) * nd) for w in ws]
        + [tile, tile, row, per_seq],
        out_specs=[tile, per_seq, per_seq, row],
        out_shape=[jax.ShapeDtypeStruct((T, D), F32), jax.ShapeDtypeStruct((B, 1, D), F32),
                   jax.ShapeDtypeStruct((B, 1, D), F32), jax.ShapeDtypeStruct((1, D), F32)],
        compiler_params=_cparams("arbitrary"),
    )(*dys, *ws, hin, dh_out, g, sc.reshape(B, 1, D))
    return dhi, dsh.reshape(B, D), dsc.reshape(B, D), dgn


def mixer_out_bwd(dm, ya, yb, pg, wa, wb, wo, *, tm, name):
    T, D = dm.shape

    def body(dm_ref, ya_ref, yb_ref, pg_ref, wa_ref, wb_ref, wo_ref, dg_ref, dya_ref, dyb_ref, ds_ref, dt_ref):
        dmg = _dg(dm_ref[...], wo_ref[...], NT)
        sa = jax.nn.sigmoid(pg_ref[:, 0:D])
        sb = jax.nn.sigmoid(pg_ref[:, D:2 * D])
        dg_ref[:, 0:D] = (dmg * ya_ref[...].astype(F32) * (sa * (1.0 - sa))).astype(BF16)
        dg_ref[:, D:2 * D] = (dmg * yb_ref[...].astype(F32) * (sb * (1.0 - sb))).astype(BF16)
        dya = (dmg * sa).astype(BF16)
        dyb = (dmg * sb).astype(BF16)
        dya_ref[...] = dya
        dyb_ref[...] = dyb
        ds_ref[...] = _dg(dya, wa_ref[...], NT)
        dt_ref[...] = _dg(dyb, wb_ref[...], NT)

    tile = lambda w: pl.BlockSpec((tm, w), lambda i: (i, 0))
    full = lambda a: pl.BlockSpec(a.shape, lambda i: (0, 0))
    return pl.pallas_call(
        body, name=name, grid=(T // tm,),
        in_specs=[tile(D), tile(D), tile(D), tile(2 * D), full(wa), full(wb), full(wo)],
        out_specs=[tile(2 * D), tile(D), tile(D), tile(D_A), tile(D_B)],
        out_shape=[jax.ShapeDtypeStruct((T, 2 * D), BF16), jax.ShapeDtypeStruct((T, D), BF16),
                   jax.ShapeDtypeStruct((T, D), BF16), jax.ShapeDtypeStruct((T, D_A), F32),
                   jax.ShapeDtypeStruct((T, D_B), F32)],
        compiler_params=_cparams("parallel"),
    )(dm, ya, yb, pg, wa, wb, wo)


def sgu_bwd(puv, dsgu, gln, bln, ws, bs_t, *, cpb, name):
    T = puv.shape[0]
    gd = D_A // N_GROUPS
    tm = cpb * CHUNK

    def body(p_ref, ds_ref, gln_ref, bln_ref, ws_ref, bs_ref, d_ref, dws_ref, dz_ref, dgl_ref, dbl_ref):
        i = pl.program_id(0)
        causal = _causal()
        wf = [jnp.where(causal, ws_ref[g], 0.0) for g in range(N_GROUPS)]
        wm = [w.astype(BF16) for w in wf]
        wt = [w.T.astype(BF16) for w in wf]
        dws = [jnp.zeros((CHUNK, CHUNK), F32) for _ in range(N_GROUPS)]
        dzs = jnp.zeros((CHUNK, D_A), F32)
        dgl = jnp.zeros((1, D_A), F32)
        dbl = jnp.zeros((1, D_A), F32)
        for j in range(cpb):
            rows = slice(j * CHUNK, (j + 1) * CHUNK)
            au = p_ref[rows, 0:D_A]
            av = p_ref[rows, D_A:2 * D_A]
            u, tu = _gelu_parts(au)
            vv, tv = _gelu_parts(av)
            vhat, rstd = _layer_norm_parts(vv)
            vn = (vhat * gln_ref[...] + bln_ref[...]).astype(BF16)
            dsg = ds_ref[rows, :]
            dz = dsg * u
            dzb = dz.astype(BF16)
            zs, dvns = [], []
            for g in range(N_GROUPS):
                cols = slice(g * gd, (g + 1) * gd)
                zs.append(_dot(wm[g], vn[:, cols]) + bs_ref[:, g:g + 1])
                dws[g] = dws[g] + _dg(dzb[:, cols], vn[:, cols], NT)
                dvns.append(_dot(wt[g], dzb[:, cols]))
            z = jnp.concatenate(zs, axis=1)
            dvn = jnp.concatenate(dvns, axis=1)
            d_ref[rows, 0:D_A] = (dsg * z * _dgelu(au, tu)).astype(BF16)
            dvh = dvn * gln_ref[...]
            dvv = rstd * (dvh - jnp.mean(dvh, axis=-1, keepdims=True)
                          - vhat * jnp.mean(dvh * vhat, axis=-1, keepdims=True))
            d_ref[rows, D_A:2 * D_A] = (dvv * _dgelu(av, tv)).astype(BF16)
            dzs = dzs + dz
            dgl = dgl + jnp.sum(dvn * vhat, axis=0, keepdims=True)
            dbl = dbl + jnp.sum(dvn, axis=0, keepdims=True)

        @pl.when(i == 0)
        def _():
            for g in range(N_GROUPS):
                dws_ref[g] = jnp.where(causal, dws[g], 0.0)
            dz_ref[...] = dzs
            dgl_ref[...] = dgl
            dbl_ref[...] = dbl

        @pl.when(i != 0)
        def _():
            for g in range(N_GROUPS):
                dws_ref[g] += jnp.where(causal, dws[g], 0.0)
            dz_ref[...] += dzs
            dgl_ref[...] += dgl
            dbl_ref[...] += dbl

    full = lambda a: pl.BlockSpec(a.shape, lambda i: (0,) * a.ndim)
    acc = lambda s: pl.BlockSpec(s, lambda i: (0,) * len(s))
    return pl.pallas_call(
        body, name=name, grid=(T // tm,),
        in_specs=[pl.BlockSpec((tm, 2 * D_A), lambda i: (i, 0)), pl.BlockSpec((tm, D_A), lambda i: (i, 0)),
                  full(gln), full(bln), full(ws), full(bs_t)],
        out_specs=[pl.BlockSpec((tm, 2 * D_A), lambda i: (i, 0)), acc((N_GROUPS, CHUNK, CHUNK)), acc((CHUNK, D_A)),
                   acc((1, D_A)), acc((1, D_A))],
        out_shape=[jax.ShapeDtypeStruct((T, 2 * D_A), BF16), jax.ShapeDtypeStruct((N_GROUPS, CHUNK, CHUNK), F32),
                   jax.ShapeDtypeStruct((CHUNK, D_A), F32), jax.ShapeDtypeStruct((1, D_A), F32),
                   jax.ShapeDtypeStruct((1, D_A), F32)],
        compiler_params=_cparams("arbitrary"),
    )(puv, dsgu, gln, bln, ws, bs_t)


def attn_bwd(pqkv, datt, gq, gk, sinks, *, seq, name):
    T = pqkv.shape[0]
    nb = seq // CHUNK

    def body(p_ref, do_ref, gq_ref, gk_ref, sink_ref, d_ref, dgq_ref, dgk_ref, dsk_ref, kpad, vpad, dkacc, dvacc, dgq_s, dsk_s):
        b = pl.program_id(0)
        kpad[0:CHUNK, :] = jnp.zeros((CHUNK, HEAD_DIM), BF16)
        vpad[0:CHUNK, :] = jnp.zeros((CHUNK, HEAD_DIM), BF16)
        dgq_s[...] = jnp.zeros_like(dgq_s)
        dsk_s[...] = jnp.zeros_like(dsk_s)
        dgk = jnp.zeros((1, HEAD_DIM), F32)
        lane = lax.broadcasted_iota(jnp.int32, (1, 128), 1)
        for h in range(N_KV):
            kcols = slice(K_OFF + h * HEAD_DIM, K_OFF + (h + 1) * HEAD_DIM)
            vcols = slice(V_OFF + h * HEAD_DIM, V_OFF + (h + 1) * HEAD_DIM)
            nk, rk = _rms_parts(p_ref[:, kcols])
            kpad[CHUNK:CHUNK + seq, :] = (nk * gk_ref[...]).astype(BF16)
            vpad[CHUNK:CHUNK + seq, :] = p_ref[:, vcols].astype(BF16)
            dkacc[...] = jnp.zeros_like(dkacc)
            dvacc[...] = jnp.zeros_like(dvacc)

            def block(i, carry):
                r0 = pl.multiple_of(i * CHUNK, CHUNK)
                kk = kpad[pl.ds(r0, 2 * CHUNK), :]
                vv = vpad[pl.ds(r0, 2 * CHUNK), :]
                valid = _attn_mask(i)
                for g in range(Q_PER_KV):
                    hq = h * Q_PER_KV + g
                    cols = slice(hq * HEAD_DIM, (hq + 1) * HEAD_DIM)
                    nq, rq = _rms_parts(p_ref[pl.ds(r0, CHUNK), cols])
                    qn = (nq * gq_ref[...]).astype(BF16)
                    probs, psink = _attn_probs(qn, kk, valid, sink_ref[0, hq])
                    do = do_ref[pl.ds(r0, CHUNK), cols].astype(BF16)
                    dp = _dg(do, vv, NT)
                    dr = jnp.sum(probs * dp, axis=-1, keepdims=True)
                    ds = (probs * (dp - dr) * ATT_SCALE).astype(BF16)
                    dsk_s[...] += jnp.where(lane == hq, -jnp.sum(psink * dr), 0.0)
                    dqn = _dot(ds, kk)
                    dkacc[pl.ds(r0, 2 * CHUNK), :] += _dg(ds, qn, TN)
                    dvacc[pl.ds(r0, 2 * CHUNK), :] += _dg(probs.astype(BF16), do, TN)
                    dn = dqn * gq_ref[...]
                    dq = rq * (dn - nq * jnp.mean(dn * nq, axis=-1, keepdims=True))
                    d_ref[pl.ds(r0, CHUNK), cols] = dq.astype(BF16)
                    dgq_s[...] += jnp.sum(dqn * nq, axis=0, keepdims=True)
                return carry

            lax.fori_loop(0, nb, block, 0)
            dkn = dkacc[CHUNK:CHUNK + seq, :]
            dn = dkn * gk_ref[...]
            d_ref[:, kcols] = (rk * (dn - nk * jnp.mean(dn * nk, axis=-1, keepdims=True))).astype(BF16)
            d_ref[:, vcols] = dvacc[CHUNK:CHUNK + seq, :].astype(BF16)
            dgk = dgk + jnp.sum(dkn * nk, axis=0, keepdims=True)

        @pl.when(b == 0)
        def _():
            dgq_ref[...] = dgq_s[...]
            dgk_ref[...] = dgk
            dsk_ref[...] = jnp.broadcast_to(dsk_s[...], dsk_ref.shape)

        @pl.when(b != 0)
        def _():
            dgq_ref[...] += dgq_s[...]
            dgk_ref[...] += dgk
            dsk_ref[...] += jnp.broadcast_to(dsk_s[...], dsk_ref.shape)

    acc = lambda s: pl.BlockSpec(s, lambda b: (0, 0))
    return pl.pallas_call(
        body, name=name, grid=(T // seq,),
        in_specs=[pl.BlockSpec((seq, QKV_W), lambda b: (b, 0)), pl.BlockSpec((seq, D_B), lambda b: (b, 0)),
                  pl.BlockSpec(gq.shape, lambda b: (0, 0)), pl.BlockSpec(gk.shape, lambda b: (0, 0)),
                  pl.BlockSpec(memory_space=pltpu.SMEM)],
        out_specs=[pl.BlockSpec((seq, QKV_W), lambda b: (b, 0)), acc((1, HEAD_DIM)), acc((1, HEAD_DIM)), acc((8, 128))],
        out_shape=[jax.ShapeDtypeStruct((T, QKV_W), BF16), jax.ShapeDtypeStruct((1, HEAD_DIM), F32),
                   jax.ShapeDtypeStruct((1, HEAD_DIM), F32), jax.ShapeDtypeStruct((8, 128), F32)],
        scratch_shapes=[pltpu.VMEM((seq + CHUNK, HEAD_DIM), BF16), pltpu.VMEM((seq + CHUNK, HEAD_DIM), BF16),
                        pltpu.VMEM((seq + CHUNK, HEAD_DIM), F32), pltpu.VMEM((seq + CHUNK, HEAD_DIM), F32),
                        pltpu.VMEM((1, HEAD_DIM), F32), pltpu.VMEM((1, 128), F32)],
        compiler_params=_cparams("arbitrary"),
    )(pqkv, datt, gq, gk, sinks)


def ada_fwd(c_all, w, b, *, name):
    nb, D = c_all.shape
    N = w.shape[1]
    tn = N // 3

    def body(c_ref, w_ref, b_ref, o_ref):
        c = c_ref[...]
        cond = (c * jax.nn.sigmoid(c)).astype(BF16)
        o_ref[...] = _dot(cond, w_ref[...].astype(BF16)) + b_ref[...]

    return pl.pallas_call(
        body, name=name, grid=(3,),
        in_specs=[pl.BlockSpec((nb, D), lambda j: (0, 0)), pl.BlockSpec((D, tn), lambda j: (0, j)),
                  pl.BlockSpec((1, tn), lambda j: (0, j))],
        out_specs=pl.BlockSpec((nb, tn), lambda j: (0, j)),
        out_shape=jax.ShapeDtypeStruct((nb, N), F32), compiler_params=_cparams("parallel"),
    )(c_all, w, b)


def ada_bwd(c_all, dmods_all, dmods_mine, *, name):
    nb, D = c_all.shape
    NA = dmods_all.shape[1]
    N = dmods_mine.shape[1]
    tn = N // 3

    def body(c_ref, da_ref, dm_ref, db_ref, dw_ref):
        c = c_ref[...]
        cond = (c * jax.nn.sigmoid(c)).astype(BF16)
        dw_ref[...] = _dg(cond, dm_ref[...].astype(BF16), TN)
        db_ref[...] = jnp.sum(da_ref[...], axis=0, keepdims=True)

    return pl.pallas_call(
        body, name=name, grid=(3,),
        in_specs=[pl.BlockSpec((nb, D), lambda j: (0, 0)), pl.BlockSpec((nb, NA), lambda j: (0, 0)),
                  pl.BlockSpec((nb, tn), lambda j: (0, j))],
        out_specs=[pl.BlockSpec((1, NA), lambda j: (0, 0)), pl.BlockSpec((D, tn), lambda j: (0, j))],
        out_shape=[jax.ShapeDtypeStruct((1, NA), F32), jax.ShapeDtypeStruct((D, N), F32)],
        compiler_params=_cparams("arbitrary"),
    )(c_all, dmods_all, dmods_mine)


def adamw(w, g, m, v, *, name):
    R, C = w.shape
    rb = _row_block(R, max(8, (1 << 18) // C))
    c1 = 1.0 / (1.0 - ADAM_B1 ** ADAM_STEP)
    c2 = 1.0 / (1.0 - ADAM_B2 ** ADAM_STEP)

    def body(w_ref, g_ref, m_ref, v_ref, d_ref, mo_ref, vo_ref):
        gg = g_ref[...]
        mn = ADAM_B1 * m_ref[...] + (1.0 - ADAM_B1) * gg
        vn = ADAM_B2 * v_ref[...] + (1.0 - ADAM_B2) * (gg * gg)
        mo_ref[...] = mn
        vo_ref[...] = vn
        d_ref[...] = -ADAM_LR * ((mn * c1) / (jnp.sqrt(vn * c2) + ADAM_EPS) + ADAM_WD * w_ref[...])

    blk = pl.BlockSpec((rb, C), lambda i: (i, 0))
    shp = jax.ShapeDtypeStruct((R, C), F32)
    return pl.pallas_call(body, name=name, grid=(R // rb,), in_specs=[blk] * 4, out_specs=[blk] * 3,
                          out_shape=[shp] * 3, compiler_params=_cparams("parallel"))(w, g, m, v)


def _place():
    return lax.axis_index("x"), lax.axis_index("y"), lax.axis_index("c")


def _flip(v, bit):
    return 1 - v if bit else v


def all_gather8(x, *, name, reduce=False):
    r, n = x.shape

    def body(x_ref, o_ref, *rest):
        if reduce:
            buf, send_sems, recv_sems, lsem = rest
        else:
            buf = o_ref
            send_sems, recv_sems, lsem = rest
        mx, my, mc = _place()
        me = 4 * mx + 2 * my + mc
        mine = pltpu.make_async_copy(x_ref, buf.at[me], lsem)
        mine.start()
        sends, recvs = [], []
        for k in range(1, N_DEV):
            peer = (_flip(mx, k & 4), _flip(my, k & 2), _flip(mc, k & 1))
            pidx = 4 * peer[0] + 2 * peer[1] + peer[2]
            sends.append(pltpu.make_async_remote_copy(
                src_ref=x_ref, dst_ref=buf.at[me], send_sem=send_sems.at[k - 1], recv_sem=recv_sems.at[k - 1],
                device_id=peer, device_id_type=MESH))
            recvs.append(pltpu.make_async_remote_copy(
                src_ref=x_ref, dst_ref=buf.at[pidx], send_sem=send_sems.at[k - 1], recv_sem=recv_sems.at[k - 1],
                device_id=peer, device_id_type=MESH))
        for cp in sends:
            cp.start()
        for cp in recvs:
            cp.wait_recv()
        for cp in sends:
            cp.wait_send()
        mine.wait()
        if reduce:
            total = buf[0]
            for d in range(1, N_DEV):
                total = total + buf[d]
            o_ref[...] = total

    scratch = [pltpu.SemaphoreType.DMA((N_DEV - 1,)), pltpu.SemaphoreType.DMA((N_DEV - 1,)), pltpu.SemaphoreType.DMA]
    if reduce:
        scratch = [pltpu.VMEM((N_DEV, r, n), F32)] + scratch
        out_shape = jax.ShapeDtypeStruct((r, n), F32)
    else:
        out_shape = jax.ShapeDtypeStruct((N_DEV, r, n), F32)
    return pl.pallas_call(
        body, name=name, out_shape=out_shape, in_specs=[pl.BlockSpec(memory_space=pltpu.VMEM)],
        out_specs=pl.BlockSpec(memory_space=pltpu.VMEM), scratch_shapes=scratch,
        compiler_params=pltpu.CompilerParams(vmem_limit_bytes=VMEM_LIMIT_BYTES),
    )(x)


ANY = pl.BlockSpec(memory_space=pl.ANY)


def gather_weights(shards, *, name):
    n = len(shards)

    def body(*refs):
        ins, outs = refs[:n], refs[n:2 * n]
        send_sems, recv_sems, lsems = refs[2 * n:]
        mx, my, mc = _place()
        q = 2 * mx + my
        sibling = (mx, my, 1 - mc)
        chips = [(_flip(mx, k & 2), _flip(my, k & 1)) for k in range(1, N_QUAD)]
        local, sends = [], []
        for p in range(n):
            hr = shards[p].shape[0] // 2
            mine = pl.ds(mc * hr, hr)
            cp = pltpu.make_async_copy(ins[p], outs[p].at[q], lsems.at[p])
            cp.start()
            local.append(cp)
            for k, chip in enumerate(chips):
                cp = pltpu.make_async_remote_copy(
                    src_ref=ins[p].at[mine, :], dst_ref=outs[p].at[q, mine, :],
                    send_sem=send_sems.at[6 * p + k], recv_sem=recv_sems.at[6 * p + k],
                    device_id=(chip[0], chip[1], mc), device_id_type=MESH)
                cp.start()
                sends.append(cp)
        for p in range(n):
            hr = shards[p].shape[0] // 2
            mine = pl.ds(mc * hr, hr)
            for k, chip in enumerate(chips):
                qk = 2 * chip[0] + chip[1]
                landed = outs[p].at[qk, mine, :]
                pltpu.make_async_remote_copy(
                    src_ref=ins[p].at[mine, :], dst_ref=landed,
                    send_sem=send_sems.at[6 * p + k], recv_sem=recv_sems.at[6 * p + k],
                    device_id=(chip[0], chip[1], mc), device_id_type=MESH).wait_recv()
                cp = pltpu.make_async_remote_copy(
                    src_ref=landed, dst_ref=landed,
                    send_sem=send_sems.at[6 * p + 3 + k], recv_sem=recv_sems.at[6 * p + 3 + k],
                    device_id=sibling, device_id_type=MESH)
                cp.start()
                sends.append(cp)
        for p in range(n):
            hr = shards[p].shape[0] // 2
            other = pl.ds((1 - mc) * hr, hr)
            for k, chip in enumerate(chips):
                qk = 2 * chip[0] + chip[1]
                theirs = outs[p].at[qk, other, :]
                pltpu.make_async_remote_copy(
                    src_ref=theirs, dst_ref=theirs,
                    send_sem=send_sems.at[6 * p + 3 + k], recv_sem=recv_sems.at[6 * p + 3 + k],
                    device_id=sibling, device_id_type=MESH).wait_recv()
        for cp in sends:
            cp.wait_send()
        for cp in local:
            cp.wait()

    return pl.pallas_call(
        body, name=name, in_specs=[ANY] * n, out_specs=[ANY] * n,
        out_shape=[jax.ShapeDtypeStruct((N_QUAD,) + s.shape, s.dtype) for s in shards],
        scratch_shapes=[pltpu.SemaphoreType.DMA((6 * n,)), pltpu.SemaphoreType.DMA((6 * n,)),
                        pltpu.SemaphoreType.DMA((n,))],
    )(*shards)


def rs_pair_exchange(gs, *, name):
    n = len(gs)

    def body(*refs):
        ins, outs = refs[:n], refs[n:2 * n]
        send_sems, recv_sems = refs[2 * n:]
        mx, my, mc = _place()
        sibling = (mx, my, 1 - mc)
        cps = []
        for p in range(n):
            hr = gs[p].shape[1] // 2
            cp = pltpu.make_async_remote_copy(
                src_ref=ins[p].at[:, pl.ds((1 - mc) * hr, hr), :], dst_ref=outs[p],
                send_sem=send_sems.at[p], recv_sem=recv_sems.at[p], device_id=sibling, device_id_type=MESH)
            cp.start()
            cps.append(cp)
        for cp in cps:
            cp.wait()

    return pl.pallas_call(
        body, name=name, in_specs=[ANY] * n, out_specs=[ANY] * n,
        out_shape=[jax.ShapeDtypeStruct((g.shape[0], g.shape[1] // 2, g.shape[2]), g.dtype) for g in gs],
        scratch_shapes=[pltpu.SemaphoreType.DMA((n,)), pltpu.SemaphoreType.DMA((n,))],
    )(*gs)


def rs_chip_exchange(ss, *, name):
    n = len(ss)

    def body(*refs):
        ins, outs = refs[:n], refs[n:2 * n]
        send_sems, recv_sems = refs[2 * n:]
        mx, my, mc = _place()
        chips = [(_flip(mx, k & 2), _flip(my, k & 1)) for k in range(1, N_QUAD)]
        cps = []
        for p in range(n):
            for k, chip in enumerate(chips):
                cp = pltpu.make_async_remote_copy(
                    src_ref=ins[p].at[2 * chip[0] + chip[1]], dst_ref=outs[p].at[k],
                    send_sem=send_sems.at[3 * p + k], recv_sem=recv_sems.at[3 * p + k],
                    device_id=(chip[0], chip[1], mc), device_id_type=MESH)
                cp.start()
                cps.append(cp)
        for cp in cps:
            cp.wait()

    return pl.pallas_call(
        body, name=name, in_specs=[ANY] * n, out_specs=[ANY] * n,
        out_shape=[jax.ShapeDtypeStruct((N_QUAD - 1,) + s.shape[1:], s.dtype) for s in ss],
        scratch_shapes=[pltpu.SemaphoreType.DMA((3 * n,)), pltpu.SemaphoreType.DMA((3 * n,))],
    )(*ss)


def rs_pair_share(hs, *, name):
    n = len(hs)

    def body(*refs):
        ins, outs = refs[:n], refs[n:2 * n]
        send_sems, recv_sems, lsems = refs[2 * n:]
        mx, my, mc = _place()
        sibling = (mx, my, 1 - mc)
        cps, local = [], []
        for p in range(n):
            hr = hs[p].shape[0]
            mine = outs[p].at[pl.ds(mc * hr, hr), :]
            cp = pltpu.make_async_copy(ins[p], mine, lsems.at[p])
            cp.start()
            local.append(cp)
            cp = pltpu.make_async_remote_copy(
                src_ref=ins[p], dst_ref=mine, send_sem=send_sems.at[p], recv_sem=recv_sems.at[p],
                device_id=sibling, device_id_type=MESH)
            cp.start()
            cps.append(cp)
        for p in range(n):
            hr = hs[p].shape[0]
            theirs = outs[p].at[pl.ds((1 - mc) * hr, hr), :]
            pltpu.make_async_remote_copy(
                src_ref=ins[p], dst_ref=theirs, send_sem=send_sems.at[p], recv_sem=recv_sems.at[p],
                device_id=sibling, device_id_type=MESH).wait_recv()
        for cp in cps:
            cp.wait_send()
        for cp in local:
            cp.wait()

    return pl.pallas_call(
        body, name=name, in_specs=[ANY] * n, out_specs=[ANY] * n,
        out_shape=[jax.ShapeDtypeStruct((2 * h.shape[0], h.shape[1]), h.dtype) for h in hs],
        scratch_shapes=[pltpu.SemaphoreType.DMA((n,)), pltpu.SemaphoreType.DMA((n,)), pltpu.SemaphoreType.DMA((n,))],
    )(*hs)


def pair_sum(g, recv, mc, *, name):
    nq, R, C = g.shape
    hr = R // 2
    rb = _row_block(hr, 256)
    nb = hr // rb

    def body(s_ref, g_ref, r_ref, o_ref):
        o_ref[...] = (g_ref[...].astype(F32) + r_ref[...].astype(F32)).astype(BF16)

    return pl.pallas_call(
        body, name=name, out_shape=jax.ShapeDtypeStruct((nq, hr, C), BF16),
        grid_spec=pltpu.PrefetchScalarGridSpec(
            num_scalar_prefetch=1, grid=(nq, nb),
            in_specs=[pl.BlockSpec((1, rb, C), lambda q, i, s: (q, s[0] * nb + i, 0)),
                      pl.BlockSpec((1, rb, C), lambda q, i, s: (q, i, 0))],
            out_specs=pl.BlockSpec((1, rb, C), lambda q, i, s: (q, i, 0))),
        compiler_params=_cparams("parallel", "parallel"),
    )(mc, g, recv)


def chip_sum(s, recv, quad, *, name):
    nq, hr, C = s.shape
    rb = _row_block(hr, 256)

    def body(q_ref, s_ref, r_ref, o_ref):
        total = s_ref[0].astype(F32)
        for k in range(N_QUAD - 1):
            total = total + r_ref[k].astype(F32)
        o_ref[...] = total

    return pl.pallas_call(
        body, name=name, out_shape=jax.ShapeDtypeStruct((hr, C), F32),
        grid_spec=pltpu.PrefetchScalarGridSpec(
            num_scalar_prefetch=1, grid=(hr // rb,),
            in_specs=[pl.BlockSpec((1, rb, C), lambda i, q: (q[0], i, 0)),
                      pl.BlockSpec((N_QUAD - 1, rb, C), lambda i, q: (0, i, 0))],
            out_specs=pl.BlockSpec((rb, C), lambda i, q: (i, 0))),
        compiler_params=_cparams("parallel"),
    )(quad, s, recv)


def reduce_scatter(gs, mc, quad, tag):
    n = len(gs)
    recv1 = rs_pair_exchange(gs, name=f"rs_pair_exchange_{tag}")
    sums = [pair_sum(gs[p], recv1[p], mc, name=f"pair_sum_{tag}_{p}") for p in range(n)]
    recv2 = rs_chip_exchange(sums, name=f"rs_chip_exchange_{tag}")
    halves = [chip_sum(sums[p], recv2[p], quad, name=f"chip_sum_{tag}_{p}") for p in range(n)]
    return rs_pair_share(halves, name=f"rs_pair_share_{tag}")


def _stack_cols(w_full):
    K, N = w_full.shape
    return w_full.reshape(K, N_QUAD, N // N_QUAD).transpose(1, 0, 2)


def _unstack_cols(w4):
    nq, K, n = w4.shape
    return w4.transpose(1, 0, 2).reshape(K, nq * n)


def kernel(x, c, w_ada, b_ada, g_norm1, ffn1_w_gate, ffn1_w_up, ffn1_w_down, g_norm2, w_in, g_sgu_ln, b_sgu_ln, w_spatial, b_spatial, g_q, g_k, attn_sinks, w_branch_a, w_branch_b, w_out, g_norm3, ffn2_w_gate, ffn2_w_up, ffn2_w_down, loss_target, m_w_ada, m_b_ada, m_g_norm1, m_ffn1_w_gate, m_ffn1_w_up, m_ffn1_w_down, m_g_norm2, m_w_in, m_g_sgu_ln, m_b_sgu_ln, m_w_spatial, m_b_spatial, m_g_q, m_g_k, m_attn_sinks, m_w_branch_a, m_w_branch_b, m_w_out, m_g_norm3, m_ffn2_w_gate, m_ffn2_w_up, m_ffn2_w_down, v_w_ada, v_b_ada, v_g_norm1, v_ffn1_w_gate, v_ffn1_w_up, v_ffn1_w_down, v_g_norm2, v_w_in, v_g_sgu_ln, v_b_sgu_ln, v_w_spatial, v_b_spatial, v_g_q, v_g_k, v_attn_sinks, v_w_branch_a, v_w_branch_b, v_w_out, v_g_norm3, v_ffn2_w_gate, v_ffn2_w_up, v_ffn2_w_down):
    B, S, D = x.shape
    T = B * S
    n_mod = b_ada.shape[1] // D
    mx, my, mc = _place()
    quad = 2 * mx + my
    mc_arr = jnp.reshape(mc, (1,)).astype(jnp.int32)
    quad_arr = jnp.reshape(quad, (1,)).astype(jnp.int32)
    tm = min(512, S)
    tm_small = min(256, S)
    cpb = min(4, S // CHUNK)

    xt = x.reshape(T, D)
    tgt = loss_target.reshape(T, D)

    c_all = all_gather8(c, name="gather_cond").reshape(N_DEV * B, D)
    n_ada = w_ada.shape[2]
    b_mine = lax.dynamic_slice(b_ada, (0, quad * n_ada), (1, n_ada))
    mods_part = ada_fwd(c_all, w_ada[0], b_mine, name="ada_fwd")
    mods_parts = all_gather8(mods_part, name="gather_mods")
    mods = jnp.concatenate([mods_parts[2 * j] for j in range(N_QUAD)], axis=1)
    me = 4 * mx + 2 * my + mc
    mods = lax.dynamic_slice(mods, (me * B, 0), (B, n_mod * D))
    sh1, sc1, ga1, sh2, sc2, ga2, sh3, sc3, ga3 = [mods[:, j * D:(j + 1) * D] for j in range(n_mod)]

    big = [ffn1_w_gate, ffn1_w_up, ffn1_w_down, w_in, w_branch_a, w_branch_b, w_out, ffn2_w_gate, ffn2_w_up, ffn2_w_down]
    gathered = gather_weights([w[0].astype(BF16) for w in big], name="gather_weights")
    wg1, wu1, wd1, win4, wa4, wb4, wo4, wg3, wu3, wd3 = gathered
    win = _unstack_cols(win4)
    w_uv, w_qkv, w_gt = win[:, 0:2 * D_A], win[:, 2 * D_A:2 * D_A + QKV_W], win[:, 2 * D_A + QKV_W:]
    wa, wb = _unstack_cols(wa4), _unstack_cols(wb4)
    wo = wo4.reshape(D, D)
    bs_t = b_spatial[0].T
    ws = w_spatial[0]

    xn1 = norm_mod_fwd(xt, g_norm1, sc1, sh1, seq=S, tm=tm, name="norm1")
    g1, u1, a1 = ffn_up(xn1, wg1, wu1, tm=tm, name="ffn1_up")
    h1, f1 = ffn_down(a1, wd1, xt, ga1, seq=S, tm=tm, name="ffn1_down")
    xn2 = norm_mod_fwd(h1, g_norm2, sc2, sh2, seq=S, tm=tm, name="norm2")
    puv, pqkv, pgt = proj_fwd(xn2, [w_uv, w_qkv, w_gt], tm=tm_small, name="proj")
    sgu = sgu_fwd(puv, g_sgu_ln, b_sgu_ln, ws, bs_t, cpb=cpb, name="sgu_fwd")
    att = attn_fwd(pqkv, g_q, g_k, attn_sinks, seq=S, name="attn_fwd")
    ya, yb, merged, mm, h2 = mixer_out_fwd(sgu, att, pgt, h1, ga2, wa, wb, wo, seq=S, tm=tm_small, name="mixer_out")
    xn3 = norm_mod_fwd(h2, g_norm3, sc3, sh3, seq=S, tm=tm, name="norm3")
    g3, u3, a3 = ffn_up(xn3, wg3, wu3, tm=tm, name="ffn2_up")
    dy, f3, lparts = ffn_down(a3, wd3, h2, ga3, tgt, seq=S, tm=tm, name="ffn2_down_loss")
    loss = lax.psum(jnp.sum(lparts[:, 0, 0]), ("x", "y", "c"))

    df3, dga3 = gate_bwd(dy, f3, ga3, 0.5, seq=S, tm=tm, name="ffn2_gate_bwd")
    dg3, du3 = ffn_bwd_act(df3, wd3, g3, u3, tm=tm, name="ffn2_act_bwd")
    (dwd3,) = mm_tn(a3, [df3], tt=tm, name="ffn2_dwd")
    dwg3, dwu3 = mm_tn(xn3, [dg3, du3], tt=tm, name="ffn2_dwgu")
    dh2, dsh3, dsc3, dgn3 = dx_norm_bwd([dg3, du3], [wg3, wu3], h2, dy, g_norm3, sc3, seq=S, tm=tm_small, name="ffn2_dx")

    dm, dga2 = gate_bwd(dh2, mm, ga2, 1.0, seq=S, tm=tm, name="mixer_gate_bwd")
    dgt, dya, dyb, dsgu, datt = mixer_out_bwd(dm, ya, yb, pgt, wa, wb, wo, tm=tm_small, name="mixer_out_bwd")
    (dwo,) = mm_tn(merged, [dm], tt=tm, name="mixer_dwo")
    (dwa,) = mm_tn(sgu, [dya], tt=tm, name="mixer_dwa")
    (dwb,) = mm_tn(att, [dyb], tt=tm, name="mixer_dwb")
    duv, dws, dzs, dgln, dbln = sgu_bwd(puv, dsgu, g_sgu_ln, b_sgu_ln, ws, bs_t, cpb=cpb, name="sgu_bwd")
    dqkv, dgq, dgk, dsk = attn_bwd(pqkv, datt, g_q, g_k, attn_sinks, seq=S, name="attn_bwd")
    dwin_parts = mm_tn(xn2, [duv, dqkv, dgt], tt=tm, name="mixer_dwin")
    dh1, dsh2, dsc2, dgn2 = dx_norm_bwd([duv, dqkv, dgt], [w_uv, w_qkv, w_gt], h1, dh2, g_norm2, sc2, seq=S,
                                        tm=tm_small, name="mixer_dx")

    df1, dga1 = gate_bwd(dh1, f1, ga1, 0.5, seq=S, tm=tm, name="ffn1_gate_bwd")
    dg1, du1 = ffn_bwd_act(df1, wd1, g1, u1, tm=tm, name="ffn1_act_bwd")
    (dwd1,) = mm_tn(a1, [df1], tt=tm, name="ffn1_dwd")
    dwg1, dwu1 = mm_tn(xn1, [dg1, du1], tt=tm, name="ffn1_dwgu")
    dx, dsh1, dsc1, dgn1 = dx_norm_bwd([dg1, du1], [wg1, wu1], xt, dh1, g_norm1, sc1, seq=S, tm=tm_small, name="ffn1_dx")

    dmods = jnp.concatenate([dsh1, dsc1, dga1, dsh2, dsc2, dga2, dsh3, dsc3, dga3], axis=1)
    dmods_all = all_gather8(dmods, name="gather_dmods").reshape(N_DEV * B, n_mod * D)
    dmods_mine = lax.dynamic_slice(dmods_all, (0, quad * n_ada), (N_DEV * B, n_ada))
    db_ada, dw_ada = ada_bwd(c_all, dmods_all, dmods_mine, name="ada_bwd")

    db_s = dzs.reshape(CHUNK, N_GROUPS, D_A // N_GROUPS).sum(axis=2).T.reshape(1, N_GROUPS * CHUNK)
    tail = jnp.concatenate([db_s, dgq, dgk, dsk[0:1, 0:N_Q]], axis=1)
    tail = jnp.pad(tail, ((0, 0), (0, (-tail.shape[1]) % D)))
    lnrow = jnp.concatenate([dgln, dbln], axis=1)
    lnrow = jnp.pad(lnrow, ((0, 0), (0, (-lnrow.shape[1]) % D)))
    packed = jnp.concatenate([dgn1, dgn2, dgn3, lnrow.reshape(-1, D), tail.reshape(-1, D), dws.reshape(-1, D)], axis=0)
    n_rows = packed.shape[0]
    packed = jnp.pad(packed, ((0, (-n_rows) % 8), (0, 0)))
    small = all_gather8(packed, name="reduce_small", reduce=True)
    ln_rows = lnrow.size // D
    tail_rows = tail.size // D
    r = 3
    g_gn1, g_gn2, g_gn3 = small[0:1], small[1:2], small[2:3]
    ln_flat = small[r:r + ln_rows].reshape(1, -1)
    r += ln_rows
    tail_flat = small[r:r + tail_rows].reshape(1, -1)
    r += tail_rows
    g_ws = small[r:r + dws.size // D].reshape(w_spatial.shape)
    g_gln, g_bln = ln_flat[:, 0:D_A], ln_flat[:, D_A:2 * D_A]
    o = N_GROUPS * CHUNK
    g_bs = tail_flat[:, 0:o].reshape(b_spatial.shape)
    g_gq, g_gk, g_sk = tail_flat[:, o:o + HEAD_DIM], tail_flat[:, o + HEAD_DIM:o + 2 * HEAD_DIM], tail_flat[:, o + 2 * HEAD_DIM:o + 2 * HEAD_DIM + N_Q]

    dwin4 = _stack_cols(jnp.concatenate(dwin_parts, axis=1))
    dwa4, dwb4 = _stack_cols(dwa), _stack_cols(dwb)
    dwo4 = dwo.reshape(N_QUAD, D // N_QUAD, D)
    reduced = reduce_scatter([dwg1, dwu1, dwd1, dwin4, dwa4, dwb4, dwo4, dwg3, dwu3, dwd3], mc_arr, quad_arr, "big")
    big_grads = [g[None] for g in reduced]

    def update(name, w, g, m, v):
        shape = w.shape
        two_d = lambda a: a.reshape(-1, shape[-1])
        d, mn, vn = adamw(two_d(w), two_d(g), two_d(m), two_d(v), name=f"adamw_{name}")
        return g.reshape(shape), d.reshape(shape), mn.reshape(shape), vn.reshape(shape)

    names = ["w_ada", "b_ada", "g_norm1", "ffn1_w_gate", "ffn1_w_up", "ffn1_w_down", "g_norm2", "w_in", "g_sgu_ln",
             "b_sgu_ln", "w_spatial", "b_spatial", "g_q", "g_k", "attn_sinks", "w_branch_a", "w_branch_b", "w_out",
             "g_norm3", "ffn2_w_gate", "ffn2_w_up", "ffn2_w_down"]
    weights = dict(zip(names, [w_ada, b_ada, g_norm1, ffn1_w_gate, ffn1_w_up, ffn1_w_down, g_norm2, w_in, g_sgu_ln,
                               b_sgu_ln, w_spatial, b_spatial, g_q, g_k, attn_sinks, w_branch_a, w_branch_b, w_out,
                               g_norm3, ffn2_w_gate, ffn2_w_up, ffn2_w_down]))
    m_in = dict(zip(names, [m_w_ada, m_b_ada, m_g_norm1, m_ffn1_w_gate, m_ffn1_w_up, m_ffn1_w_down, m_g_norm2, m_w_in,
                            m_g_sgu_ln, m_b_sgu_ln, m_w_spatial, m_b_spatial, m_g_q, m_g_k, m_attn_sinks, m_w_branch_a,
                            m_w_branch_b, m_w_out, m_g_norm3, m_ffn2_w_gate, m_ffn2_w_up, m_ffn2_w_down]))
    v_in = dict(zip(names, [v_w_ada, v_b_ada, v_g_norm1, v_ffn1_w_gate, v_ffn1_w_up, v_ffn1_w_down, v_g_norm2, v_w_in,
                            v_g_sgu_ln, v_b_sgu_ln, v_w_spatial, v_b_spatial, v_g_q, v_g_k, v_attn_sinks, v_w_branch_a,
                            v_w_branch_b, v_w_out, v_g_norm3, v_ffn2_w_gate, v_ffn2_w_up, v_ffn2_w_down]))
    grads = {
        "w_ada": dw_ada[None], "b_ada": db_ada, "g_norm1": g_gn1, "g_norm2": g_gn2, "g_norm3": g_gn3,
        "g_sgu_ln": g_gln, "b_sgu_ln": g_bln, "w_spatial": g_ws, "b_spatial": g_bs, "g_q": g_gq, "g_k": g_gk,
        "attn_sinks": g_sk,
    }
    for nm, g in zip(["ffn1_w_gate", "ffn1_w_up", "ffn1_w_down", "w_in", "w_branch_a", "w_branch_b", "w_out",
                      "ffn2_w_gate", "ffn2_w_up", "ffn2_w_down"], big_grads):
        grads[nm] = g

    small_names = ["b_ada", "g_norm1", "g_norm2", "g_norm3", "g_sgu_ln", "b_sgu_ln", "w_spatial", "b_spatial", "g_q",
                   "g_k", "attn_sinks"]

    def pack(d):
        flat = jnp.concatenate([d[nm].reshape(-1) for nm in small_names])
        return jnp.pad(flat, (0, (-flat.size) % (8 * 128))).reshape(-1, 128)

    sd, sm, sv = adamw(pack(weights), pack(grads), pack(m_in), pack(v_in), name="adamw_small")
    delta, new_m, new_v = {}, {}, {}
    off = 0
    for nm in small_names:
        size, shape = weights[nm].size, weights[nm].shape
        delta[nm] = sd.reshape(-1)[off:off + size].reshape(shape)
        new_m[nm] = sm.reshape(-1)[off:off + size].reshape(shape)
        new_v[nm] = sv.reshape(-1)[off:off + size].reshape(shape)
        off += size
    for nm in names:
        if nm not in small_names:
            grads[nm], delta[nm], new_m[nm], new_v[nm] = update(nm, weights[nm], grads[nm], m_in[nm], v_in[nm])
        else:
            grads[nm] = grads[nm].reshape(weights[nm].shape)

    return (loss, dx.reshape(B, S, D), *[grads[nm] for nm in names], *[delta[nm] for nm in names],
            *[new_m[nm] for nm in names], *[new_v[nm] for nm in names])
```

```python
import functools
import math
import operator

import jax
import jax.numpy as jnp
from jax import lax
from jax.experimental import pallas as pl
from jax.experimental.pallas import tpu as pltpu

F32 = jnp.float32
BF16 = jnp.bfloat16
EPS = 1e-6
NEG = -1e30
N_DEV = 8
N_QUAD = 4
D_A = 512
N_GROUPS = 4
CHUNK = 128
HEAD_DIM = 64
N_KV = 2
Q_PER_KV = 4
N_Q = N_KV * Q_PER_KV
D_B = N_Q * HEAD_DIM
QKV_W = D_B + 2 * N_KV * HEAD_DIM
K_OFF = D_B
V_OFF = D_B + N_KV * HEAD_DIM
ATT_SCALE = HEAD_DIM ** -0.5
GELU_K = math.sqrt(2.0 / math.pi)
GELU_C = 0.044715
ADAM_LR, ADAM_B1, ADAM_B2, ADAM_EPS, ADAM_WD, ADAM_STEP = 0.001, 0.9, 0.999, 1e-08, 0.01, 10
VMEM_LIMIT_BYTES = 56 * 1024 * 1024
MESH = pl.DeviceIdType.MESH
NT = (((1,), (1,)), ((), ()))
TN = (((0,), (0,)), ((), ()))
ANY = pl.BlockSpec(memory_space=pl.ANY)


def _dot(a, b):
    return jnp.dot(a, b, preferred_element_type=F32)


def _dg(a, b, dims):
    return lax.dot_general(a, b, dims, preferred_element_type=F32)


def _gelu_parts(x):
    t = jnp.tanh(GELU_K * (x + GELU_C * x * x * x))
    return 0.5 * x * (1.0 + t), t


def _dgelu(x, t):
    return 0.5 * (1.0 + t) + 0.5 * x * (1.0 - t * t) * (GELU_K * (1.0 + 3.0 * GELU_C * x * x))


def _row_block(rows, target):
    b = min(rows, target)
    while rows % b or b % 8:
        b -= 1
    return b


class Comm:
    def __init__(self, *, srcs=(), bufs=(), land_shapes=(), n_sems, start, finish):
        self.srcs, self.bufs, self.land_shapes = list(srcs), list(bufs), list(land_shapes)
        self.n_sems, self.start, self.finish = n_sems, start, finish
        self.bufs_out, self.lands = None, None


def _call(body, *, name, grid, in_specs, out_specs, out_shape, args, scratch_shapes=(), comms=()):
    in_specs, out_specs, out_shape = list(in_specs), list(out_specs), list(out_shape)
    args, scratch = list(args), list(scratch_shapes)
    n_in, n_out, n_scr = len(args), len(out_shape), len(scratch)
    aliases, layout = {}, []
    for cm in comms:
        i0, o0, s0 = len(args), len(out_shape), len(scratch)
        args += cm.srcs + cm.bufs
        in_specs += [ANY] * (len(cm.srcs) + len(cm.bufs))
        out_shape += [jax.ShapeDtypeStruct(b.shape, b.dtype) for b in cm.bufs] + cm.land_shapes
        out_specs += [ANY] * (len(cm.bufs) + len(cm.land_shapes))
        for j in range(len(cm.bufs)):
            aliases[i0 + len(cm.srcs) + j] = o0 + j
        scratch += [pltpu.SemaphoreType.DMA((cm.n_sems,)), pltpu.SemaphoreType.DMA((cm.n_sems,))]
        layout.append((i0, o0, s0))
    n_in_all, n_out_all = len(args), len(out_shape)

    def wrapped(*refs):
        ins, outs, scr = refs[:n_in_all], refs[n_in_all:n_in_all + n_out_all], refs[n_in_all + n_out_all:]
        parts = []
        for cm, (i0, o0, s0) in zip(comms, layout):
            nb = len(cm.bufs)
            parts.append((ins[i0:i0 + len(cm.srcs)], outs[o0:o0 + nb], outs[o0 + nb:o0 + nb + len(cm.land_shapes)],
                          scr[s0], scr[s0 + 1]))
        if comms and grid:
            ids = [pl.program_id(d) for d in range(len(grid))]
            first = functools.reduce(operator.and_, [i == 0 for i in ids])
            last = functools.reduce(operator.and_, [i == g - 1 for i, g in zip(ids, grid)])

            @pl.when(first)
            def _():
                for cm, p in zip(comms, parts):
                    cm.start(*p)
        elif comms:
            for cm, p in zip(comms, parts):
                cm.start(*p)
        if body is not None:
            body(*ins[:n_in], *outs[:n_out], *scr[:n_scr])
        if comms and grid:
            @pl.when(last)
            def _():
                for cm, p in zip(comms, parts):
                    cm.finish(*p)
        elif comms:
            for cm, p in zip(comms, parts):
                cm.finish(*p)

    kwargs = dict(grid=grid) if grid else {}
    sem = ("arbitrary",) * len(grid)
    res = pl.pallas_call(
        wrapped, name=name, in_specs=in_specs, out_specs=out_specs, out_shape=out_shape, scratch_shapes=scratch,
        input_output_aliases=aliases,
        compiler_params=pltpu.CompilerParams(dimension_semantics=sem, vmem_limit_bytes=VMEM_LIMIT_BYTES), **kwargs,
    )(*args)
    for cm, (i0, o0, s0) in zip(comms, layout):
        nb = len(cm.bufs)
        cm.bufs_out = list(res[o0:o0 + nb])
        cm.lands = list(res[o0 + nb:o0 + nb + len(cm.land_shapes)])
    return list(res[:n_out])


def run_comms(comms, *, name):
    _call(None, name=name, grid=(), in_specs=[], out_specs=[], out_shape=[], args=[], comms=comms)


def norm_mod_fwd(h, g, sc, sh, *, seq, tm, name, comms=()):
    T, D = h.shape
    B, tps = T // seq, seq // tm

    def body(h_ref, g_ref, sc_ref, sh_ref, o_ref):
        x = h_ref[...]
        r = lax.rsqrt(jnp.mean(x * x, axis=-1, keepdims=True) + EPS)
        y = x * r * g_ref[...]
        o_ref[...] = (y * (1.0 + sc_ref[0]) + sh_ref[0]).astype(o_ref.dtype)

    per_seq = pl.BlockSpec((1, 1, D), lambda i: (i // tps, 0, 0))
    return _call(
        body, name=name, grid=(T // tm,),
        in_specs=[pl.BlockSpec((tm, D), lambda i: (i, 0)), pl.BlockSpec((1, D), lambda i: (0, 0)), per_seq, per_seq],
        out_specs=[pl.BlockSpec((tm, D), lambda i: (i, 0))], out_shape=[jax.ShapeDtypeStruct((T, D), BF16)],
        args=[h, g, sc.reshape(B, 1, D), sh.reshape(B, 1, D)], comms=comms)[0]


def ffn_up(xn, wg, wu, *, tm, name, comms=()):
    T, D = xn.shape
    nq, _, fs = wg.shape

    def body(x_ref, wg_ref, wu_ref, g_ref, u_ref, a_ref):
        x = x_ref[...]
        g = _dot(x, wg_ref[0])
        u = _dot(x, wu_ref[0])
        g_ref[0] = g.astype(BF16)
        u_ref[0] = u.astype(BF16)
        a_ref[0] = (g * jax.nn.sigmoid(g) * u).astype(BF16)

    w_spec = pl.BlockSpec((1, D, fs), lambda q, i: (q, 0, 0))
    o_spec = pl.BlockSpec((1, tm, fs), lambda q, i: (q, i, 0))
    o_shape = jax.ShapeDtypeStruct((nq, T, fs), BF16)
    return _call(
        body, name=name, grid=(nq, T // tm),
        in_specs=[pl.BlockSpec((tm, D), lambda q, i: (i, 0)), w_spec, w_spec],
        out_specs=[o_spec, o_spec, o_spec], out_shape=[o_shape, o_shape, o_shape], args=[xn, wg, wu], comms=comms)


def ffn_down(a, wd, hin, ga, target=None, *, seq, tm, name, comms=()):
    nq, T, fs = a.shape
    D = wd.shape[-1]
    B, tps, nt = T // seq, seq // tm, T // tm
    with_loss = target is not None

    def body(*refs):
        if with_loss:
            a_ref, wd_ref, h_ref, ga_ref, t_ref, o_ref, f_ref, l_ref = refs
        else:
            a_ref, wd_ref, h_ref, ga_ref, o_ref, f_ref = refs
        f = _dot(a_ref[0], wd_ref[0])
        for q in range(1, nq):
            f = f + _dot(a_ref[q], wd_ref[q])
        f_ref[...] = f.astype(BF16)
        y = h_ref[...] + (0.5 * ga_ref[0]) * f
        if with_loss:
            e = y - t_ref[...]
            o_ref[...] = e * (1.0 / D)
            l_ref[...] = jnp.full(l_ref.shape, 0.5 * jnp.sum(e * e) * (1.0 / D), F32)
        else:
            o_ref[...] = y

    tile = pl.BlockSpec((tm, D), lambda i: (i, 0))
    in_specs = [pl.BlockSpec((nq, tm, fs), lambda i: (0, i, 0)), pl.BlockSpec((nq, fs, D), lambda i: (0, 0, 0)),
                tile, pl.BlockSpec((1, 1, D), lambda i: (i // tps, 0, 0))]
    out_specs = [tile, tile]
    out_shape = [jax.ShapeDtypeStruct((T, D), F32), jax.ShapeDtypeStruct((T, D), BF16)]
    args = [a, wd, hin, ga.reshape(B, 1, D)]
    if with_loss:
        in_specs.append(tile)
        args.append(target)
        out_specs.append(pl.BlockSpec((1, 8, 128), lambda i: (i, 0, 0)))
        out_shape.append(jax.ShapeDtypeStruct((nt, 8, 128), F32))
    return _call(body, name=name, grid=(nt,), in_specs=in_specs, out_specs=out_specs, out_shape=out_shape, args=args,
                 comms=comms)


def proj_fwd(xn, ws, *, tm, name, comms=()):
    T, D = xn.shape
    n = len(ws)

    def body(*refs):
        x = refs[0][...]
        for j in range(n):
            refs[1 + n + j][...] = _dot(x, refs[1 + j][...])

    return _call(
        body, name=name, grid=(T // tm,),
        in_specs=[pl.BlockSpec((tm, D), lambda i: (i, 0))] + [pl.BlockSpec(w.shape, lambda i: (0, 0)) for w in ws],
        out_specs=[pl.BlockSpec((tm, w.shape[1]), lambda i: (i, 0)) for w in ws],
        out_shape=[jax.ShapeDtypeStruct((T, w.shape[1]), F32) for w in ws], args=[xn, *ws], comms=comms)


def _layer_norm_parts(v):
    mu = jnp.mean(v, axis=-1, keepdims=True)
    d = v - mu
    rstd = lax.rsqrt(jnp.mean(d * d, axis=-1, keepdims=True) + EPS)
    return d * rstd, rstd


def _causal():
    row = lax.broadcasted_iota(jnp.int32, (CHUNK, CHUNK), 0)
    col = lax.broadcasted_iota(jnp.int32, (CHUNK, CHUNK), 1)
    return row >= col


def sgu_fwd(puv, gln, bln, ws, bs_t, *, cpb, name, comms=()):
    T = puv.shape[0]
    gd = D_A // N_GROUPS
    tm = cpb * CHUNK

    def body(p_ref, gln_ref, bln_ref, ws_ref, bs_ref, o_ref):
        causal = _causal()
        wm = [jnp.where(causal, ws_ref[g], 0.0).astype(BF16) for g in range(N_GROUPS)]
        for j in range(cpb):
            rows = slice(j * CHUNK, (j + 1) * CHUNK)
            u, _ = _gelu_parts(p_ref[rows, 0:D_A])
            vv, _ = _gelu_parts(p_ref[rows, D_A:2 * D_A])
            vhat, _ = _layer_norm_parts(vv)
            vn = (vhat * gln_ref[...] + bln_ref[...]).astype(BF16)
            for g in range(N_GROUPS):
                cols = slice(g * gd, (g + 1) * gd)
                z = _dot(wm[g], vn[:, cols]) + bs_ref[:, g:g + 1]
                o_ref[rows, cols] = (u[:, cols] * z).astype(BF16)

    full = lambda a: pl.BlockSpec(a.shape, lambda i: (0,) * a.ndim)
    return _call(
        body, name=name, grid=(T // tm,),
        in_specs=[pl.BlockSpec((tm, 2 * D_A), lambda i: (i, 0)), full(gln), full(bln), full(ws), full(bs_t)],
        out_specs=[pl.BlockSpec((tm, D_A), lambda i: (i, 0))], out_shape=[jax.ShapeDtypeStruct((T, D_A), BF16)],
        args=[puv, gln, bln, ws, bs_t], comms=comms)[0]


def _rms_parts(x):
    r = lax.rsqrt(jnp.mean(x * x, axis=-1, keepdims=True) + EPS)
    return x * r, r


def _attn_mask(i):
    qi = lax.broadcasted_iota(jnp.int32, (CHUNK, 2 * CHUNK), 0)
    kj = lax.broadcasted_iota(jnp.int32, (CHUNK, 2 * CHUNK), 1)
    diff = qi + CHUNK - kj
    return (diff >= 0) & (diff < CHUNK) & ((kj >= CHUNK) | (i > 0))


def _attn_probs(qn, kk, valid, sink):
    s = _dg(qn, kk, NT) * ATT_SCALE
    s = jnp.where(valid, s, NEG)
    m = jnp.maximum(jnp.max(s, axis=-1, keepdims=True), sink)
    p = jnp.exp(s - m)
    es = jnp.exp(sink - m)
    inv = 1.0 / (jnp.sum(p, axis=-1, keepdims=True) + es)
    return p * inv, es * inv


def attn_fwd(pqkv, gq, gk, sinks, *, seq, name, comms=()):
    T = pqkv.shape[0]
    nb = seq // CHUNK

    def body(p_ref, gq_ref, gk_ref, sink_ref, o_ref, kpad, vpad):
        kpad[0:CHUNK, :] = jnp.zeros((CHUNK, HEAD_DIM), BF16)
        vpad[0:CHUNK, :] = jnp.zeros((CHUNK, HEAD_DIM), BF16)
        for h in range(N_KV):
            nk, _ = _rms_parts(p_ref[:, K_OFF + h * HEAD_DIM:K_OFF + (h + 1) * HEAD_DIM])
            kpad[CHUNK:CHUNK + seq, :] = (nk * gk_ref[...]).astype(BF16)
            vpad[CHUNK:CHUNK + seq, :] = p_ref[:, V_OFF + h * HEAD_DIM:V_OFF + (h + 1) * HEAD_DIM].astype(BF16)

            def block(i, carry):
                r0 = pl.multiple_of(i * CHUNK, CHUNK)
                kk = kpad[pl.ds(r0, 2 * CHUNK), :]
                vv = vpad[pl.ds(r0, 2 * CHUNK), :]
                valid = _attn_mask(i)
                for g in range(Q_PER_KV):
                    hq = h * Q_PER_KV + g
                    cols = slice(hq * HEAD_DIM, (hq + 1) * HEAD_DIM)
                    nq, _ = _rms_parts(p_ref[pl.ds(r0, CHUNK), cols])
                    qn = (nq * gq_ref[...]).astype(BF16)
                    probs, _ = _attn_probs(qn, kk, valid, sink_ref[0, hq])
                    o_ref[pl.ds(r0, CHUNK), cols] = _dot(probs.astype(BF16), vv).astype(BF16)
                return carry

            lax.fori_loop(0, nb, block, 0)

    return _call(
        body, name=name, grid=(T // seq,),
        in_specs=[pl.BlockSpec((seq, QKV_W), lambda b: (b, 0)), pl.BlockSpec(gq.shape, lambda b: (0, 0)),
                  pl.BlockSpec(gk.shape, lambda b: (0, 0)), pl.BlockSpec(memory_space=pltpu.SMEM)],
        out_specs=[pl.BlockSpec((seq, D_B), lambda b: (b, 0))], out_shape=[jax.ShapeDtypeStruct((T, D_B), BF16)],
        scratch_shapes=[pltpu.VMEM((seq + CHUNK, HEAD_DIM), BF16), pltpu.VMEM((seq + CHUNK, HEAD_DIM), BF16)],
        args=[pqkv, gq, gk, sinks], comms=comms)[0]


def mixer_out_fwd(sgu, att, pg, hin, ga, wa, wb, wo, *, seq, tm, name, comms=()):
    T, D = hin.shape
    B, tps = T // seq, seq // tm

    def body(s_ref, t_ref, pg_ref, h_ref, ga_ref, wa_ref, wb_ref, wo_ref, ya_ref, yb_ref, mg_ref, m_ref, ho_ref):
        ya = _dot(s_ref[...], wa_ref[...])
        yb = _dot(t_ref[...], wb_ref[...])
        merged = jax.nn.sigmoid(pg_ref[:, 0:D]) * ya + jax.nn.sigmoid(pg_ref[:, D:2 * D]) * yb
        mg = merged.astype(BF16)
        m = _dot(mg, wo_ref[...])
        ya_ref[...] = ya.astype(BF16)
        yb_ref[...] = yb.astype(BF16)
        mg_ref[...] = mg
        m_ref[...] = m.astype(BF16)
        ho_ref[...] = h_ref[...] + ga_ref[0] * m

    tile = lambda w: pl.BlockSpec((tm, w), lambda i: (i, 0))
    full = lambda a: pl.BlockSpec(a.shape, lambda i: (0, 0))
    b16 = jax.ShapeDtypeStruct((T, D), BF16)
    return _call(
        body, name=name, grid=(T // tm,),
        in_specs=[tile(D_A), tile(D_B), tile(2 * D), tile(D), pl.BlockSpec((1, 1, D), lambda i: (i // tps, 0, 0)),
                  full(wa), full(wb), full(wo)],
        out_specs=[tile(D)] * 5, out_shape=[b16, b16, b16, b16, jax.ShapeDtypeStruct((T, D), F32)],
        args=[sgu, att, pg, hin, ga.reshape(B, 1, D), wa, wb, wo], comms=comms)


def gate_bwd(dh, f, ga, scale, *, seq, tm, name, comms=()):
    T, D = dh.shape
    B, tps = T // seq, seq // tm

    def body(dh_ref, f_ref, ga_ref, df_ref, dga_ref):
        i = pl.program_id(0)
        d = dh_ref[...]
        df_ref[...] = ((scale * ga_ref[0]) * d).astype(BF16)
        part = scale * jnp.sum(d * f_ref[...].astype(F32), axis=0, keepdims=True)

        @pl.when(i % tps == 0)
        def _():
            dga_ref[0] = part

        @pl.when(i % tps != 0)
        def _():
            dga_ref[0] += part

    tile = pl.BlockSpec((tm, D), lambda i: (i, 0))
    per_seq = pl.BlockSpec((1, 1, D), lambda i: (i // tps, 0, 0))
    df, dga = _call(
        body, name=name, grid=(T // tm,), in_specs=[tile, tile, per_seq], out_specs=[tile, per_seq],
        out_shape=[jax.ShapeDtypeStruct((T, D), BF16), jax.ShapeDtypeStruct((B, 1, D), F32)],
        args=[dh, f, ga.reshape(B, 1, D)], comms=comms)
    return df, dga.reshape(B, D)


def ffn_bwd_act(df, wd, g, u, *, tm, name, comms=()):
    T, D = df.shape
    nq, fs, _ = wd.shape

    def body(df_ref, wd_ref, g_ref, u_ref, dg_ref, du_ref):
        da = _dg(df_ref[...], wd_ref[0], NT)
        gg = g_ref[0].astype(F32)
        sg = jax.nn.sigmoid(gg)
        du_ref[0] = (da * (gg * sg)).astype(BF16)
        dg_ref[0] = (da * u_ref[0].astype(F32) * (sg * (1.0 + gg * (1.0 - sg)))).astype(BF16)

    act = pl.BlockSpec((1, tm, fs), lambda q, i: (q, i, 0))
    o_shape = jax.ShapeDtypeStruct((nq, T, fs), BF16)
    return _call(
        body, name=name, grid=(nq, T // tm),
        in_specs=[pl.BlockSpec((tm, D), lambda q, i: (i, 0)), pl.BlockSpec((1, fs, D), lambda q, i: (q, 0, 0)), act, act],
        out_specs=[act, act], out_shape=[o_shape, o_shape], args=[df, wd, g, u], comms=comms)


def mm_tn(x, ys, *, tt, name, comms=()):
    nq = max([a.shape[0] for a in [x] + list(ys) if a.ndim == 3] + [1])
    T, K = x.shape[-2:]
    n = len(ys)
    nt = T // tt

    def body(*refs):
        x_ref, y_refs, o_refs, accs = refs[0], refs[1:1 + n], refs[1 + n:1 + 2 * n], refs[1 + 2 * n:]
        t = pl.program_id(1)
        xv = x_ref[0] if x.ndim == 3 else x_ref[...]
        for j in range(n):
            yv = y_refs[j][0] if ys[j].ndim == 3 else y_refs[j][...]
            part = _dg(xv, yv, TN)

            @pl.when(t == 0)
            def _():
                accs[j][...] = part

            @pl.when(t != 0)
            def _():
                accs[j][...] += part

        @pl.when(t == nt - 1)
        def _():
            for j in range(n):
                if ys[j].ndim == 3 or x.ndim == 3:
                    o_refs[j][0] = accs[j][...].astype(BF16)
                else:
                    o_refs[j][...] = accs[j][...].astype(BF16)

    def spec(a):
        if a.ndim == 3:
            return pl.BlockSpec((1, tt, a.shape[2]), lambda q, t: (q, t, 0))
        return pl.BlockSpec((tt, a.shape[1]), lambda q, t: (t, 0))

    out_specs, out_shape = [], []
    for y in ys:
        N = y.shape[-1]
        if y.ndim == 3 or x.ndim == 3:
            out_specs.append(pl.BlockSpec((1, K, N), lambda q, t: (q, 0, 0)))
            out_shape.append(jax.ShapeDtypeStruct((nq, K, N), BF16))
        else:
            out_specs.append(pl.BlockSpec((K, N), lambda q, t: (0, 0)))
            out_shape.append(jax.ShapeDtypeStruct((K, N), BF16))
    return _call(
        body, name=name, grid=(nq, nt), in_specs=[spec(x)] + [spec(y) for y in ys],
        out_specs=out_specs, out_shape=out_shape, scratch_shapes=[pltpu.VMEM((K, y.shape[-1]), F32) for y in ys],
        args=[x, *ys], comms=comms)


def dx_norm_bwd(dys, ws, hin, dh_out, g, sc, *, seq, tm, name, comms=()):
    T, D = hin.shape
    B, tps = T // seq, seq // tm
    n = len(dys)

    def body(*refs):
        dy_refs, w_refs = refs[:n], refs[n:2 * n]
        h_ref, dho_ref, g_ref, sc_ref, dhi_ref, dsh_ref, dsc_ref, dgn_ref = refs[2 * n:]
        i = pl.program_id(0)
        dxn = None
        for j in range(n):
            if dys[j].ndim == 3:
                for q in range(dys[j].shape[0]):
                    part = _dg(dy_refs[j][q], w_refs[j][q], NT)
                    dxn = part if dxn is None else dxn + part
            else:
                part = _dg(dy_refs[j][...], w_refs[j][...], NT)
                dxn = part if dxn is None else dxn + part
        x = h_ref[...]
        nx, r = _rms_parts(x)
        gain = g_ref[...]
        one_sc = 1.0 + sc_ref[0]
        dsh = jnp.sum(dxn, axis=0, keepdims=True)
        dsc = jnp.sum(dxn * (nx * gain), axis=0, keepdims=True)
        dgn = jnp.sum(dxn * one_sc * nx, axis=0, keepdims=True)
        dn = dxn * one_sc * gain
        dhi_ref[...] = dho_ref[...] + r * (dn - nx * jnp.mean(dn * nx, axis=-1, keepdims=True))

        @pl.when(i % tps == 0)
        def _():
            dsh_ref[0] = dsh
            dsc_ref[0] = dsc

        @pl.when(i % tps != 0)
        def _():
            dsh_ref[0] += dsh
            dsc_ref[0] += dsc

        @pl.when(i == 0)
        def _():
            dgn_ref[...] = dgn

        @pl.when(i != 0)
        def _():
            dgn_ref[...] += dgn

    def dy_spec(a):
        if a.ndim == 3:
            return pl.BlockSpec((a.shape[0], tm, a.shape[2]), lambda i: (0, i, 0))
        return pl.BlockSpec((tm, a.shape[1]), lambda i: (i, 0))

    tile = pl.BlockSpec((tm, D), lambda i: (i, 0))
    per_seq = pl.BlockSpec((1, 1, D), lambda i: (i // tps, 0, 0))
    row = pl.BlockSpec((1, D), lambda i: (0, 0))
    dhi, dsh, dsc, dgn = _call(
        body, name=name, grid=(T // tm,),
        in_specs=[dy_spec(a) for a in dys] + [pl.BlockSpec(w.shape, lambda i, nd=w.ndim: (0,) * nd) for w in ws]
        + [tile, tile, row, per_seq],
        out_specs=[tile, per_seq, per_seq, row],
        out_shape=[jax.ShapeDtypeStruct((T, D), F32), jax.ShapeDtypeStruct((B, 1, D), F32),
                   jax.ShapeDtypeStruct((B, 1, D), F32), jax.ShapeDtypeStruct((1, D), F32)],
        args=[*dys, *ws, hin, dh_out, g, sc.reshape(B, 1, D)], comms=comms)
    return dhi, dsh.reshape(B, D), dsc.reshape(B, D), dgn


def mixer_out_bwd(dm, ya, yb, pg, wa, wb, wo, *, tm, name, comms=()):
    T, D = dm.shape

    def body(dm_ref, ya_ref, yb_ref, pg_ref, wa_ref, wb_ref, wo_ref, dg_ref, dya_ref, dyb_ref, ds_ref, dt_ref):
        dmg = _dg(dm_ref[...], wo_ref[...], NT)
        sa = jax.nn.sigmoid(pg_ref[:, 0:D])
        sb = jax.nn.sigmoid(pg_ref[:, D:2 * D])
        dg_ref[:, 0:D] = (dmg * ya_ref[...].astype(F32) * (sa * (1.0 - sa))).astype(BF16)
        dg_ref[:, D:2 * D] = (dmg * yb_ref[...].astype(F32) * (sb * (1.0 - sb))).astype(BF16)
        dya = (dmg * sa).astype(BF16)
        dyb = (dmg * sb).astype(BF16)
        dya_ref[...] = dya
        dyb_ref[...] = dyb
        ds_ref[...] = _dg(dya, wa_ref[...], NT)
        dt_ref[...] = _dg(dyb, wb_ref[...], NT)

    tile = lambda w: pl.BlockSpec((tm, w), lambda i: (i, 0))
    full = lambda a: pl.BlockSpec(a.shape, lambda i: (0, 0))
    return _call(
        body, name=name, grid=(T // tm,),
        in_specs=[tile(D), tile(D), tile(D), tile(2 * D), full(wa), full(wb), full(wo)],
        out_specs=[tile(2 * D), tile(D), tile(D), tile(D_A), tile(D_B)],
        out_shape=[jax.ShapeDtypeStruct((T, 2 * D), BF16), jax.ShapeDtypeStruct((T, D), BF16),
                   jax.ShapeDtypeStruct((T, D), BF16), jax.ShapeDtypeStruct((T, D_A), F32),
                   jax.ShapeDtypeStruct((T, D_B), F32)],
        args=[dm, ya, yb, pg, wa, wb, wo], comms=comms)


def sgu_bwd(puv, dsgu, gln, bln, ws, bs_t, *, cpb, name, comms=()):
    T = puv.shape[0]
    gd = D_A // N_GROUPS
    tm = cpb * CHUNK

    def body(p_ref, ds_ref, gln_ref, bln_ref, ws_ref, bs_ref, d_ref, dws_ref, dz_ref, dgl_ref, dbl_ref):
        i = pl.program_id(0)
        causal = _causal()
        wf = [jnp.where(causal, ws_ref[g], 0.0) for g in range(N_GROUPS)]
        wm = [w.astype(BF16) for w in wf]
        wt = [w.T.astype(BF16) for w in wf]
        dws = [jnp.zeros((CHUNK, CHUNK), F32) for _ in range(N_GROUPS)]
        dzs = jnp.zeros((CHUNK, D_A), F32)
        dgl = jnp.zeros((1, D_A), F32)
        dbl = jnp.zeros((1, D_A), F32)
        for j in range(cpb):
            rows = slice(j * CHUNK, (j + 1) * CHUNK)
            au = p_ref[rows, 0:D_A]
            av = p_ref[rows, D_A:2 * D_A]
            u, tu = _gelu_parts(au)
            vv, tv = _gelu_parts(av)
            vhat, rstd = _layer_norm_parts(vv)
            vn = (vhat * gln_ref[...] + bln_ref[...]).astype(BF16)
            dsg = ds_ref[rows, :]
            dz = dsg * u
            dzb = dz.astype(BF16)
            zs, dvns = [], []
            for g in range(N_GROUPS):
                cols = slice(g * gd, (g + 1) * gd)
                zs.append(_dot(wm[g], vn[:, cols]) + bs_ref[:, g:g + 1])
                dws[g] = dws[g] + _dg(dzb[:, cols], vn[:, cols], NT)
                dvns.append(_dot(wt[g], dzb[:, cols]))
            z = jnp.concatenate(zs, axis=1)
            dvn = jnp.concatenate(dvns, axis=1)
            d_ref[rows, 0:D_A] = (dsg * z * _dgelu(au, tu)).astype(BF16)
            dvh = dvn * gln_ref[...]
            dvv = rstd * (dvh - jnp.mean(dvh, axis=-1, keepdims=True)
                          - vhat * jnp.mean(dvh * vhat, axis=-1, keepdims=True))
            d_ref[rows, D_A:2 * D_A] = (dvv * _dgelu(av, tv)).astype(BF16)
            dzs = dzs + dz
            dgl = dgl + jnp.sum(dvn * vhat, axis=0, keepdims=True)
            dbl = dbl + jnp.sum(dvn, axis=0, keepdims=True)

        @pl.when(i == 0)
        def _():
            for g in range(N_GROUPS):
                dws_ref[g] = jnp.where(causal, dws[g], 0.0)
            dz_ref[...] = dzs
            dgl_ref[...] = dgl
            dbl_ref[...] = dbl

        @pl.when(i != 0)
        def _():
            for g in range(N_GROUPS):
                dws_ref[g] += jnp.where(causal, dws[g], 0.0)
            dz_ref[...] += dzs
            dgl_ref[...] += dgl
            dbl_ref[...] += dbl

    full = lambda a: pl.BlockSpec(a.shape, lambda i: (0,) * a.ndim)
    acc = lambda s: pl.BlockSpec(s, lambda i: (0,) * len(s))
    return _call(
        body, name=name, grid=(T // tm,),
        in_specs=[pl.BlockSpec((tm, 2 * D_A), lambda i: (i, 0)), pl.BlockSpec((tm, D_A), lambda i: (i, 0)),
                  full(gln), full(bln), full(ws), full(bs_t)],
        out_specs=[pl.BlockSpec((tm, 2 * D_A), lambda i: (i, 0)), acc((N_GROUPS, CHUNK, CHUNK)), acc((CHUNK, D_A)),
                   acc((1, D_A)), acc((1, D_A))],
        out_shape=[jax.ShapeDtypeStruct((T, 2 * D_A), BF16), jax.ShapeDtypeStruct((N_GROUPS, CHUNK, CHUNK), F32),
                   jax.ShapeDtypeStruct((CHUNK, D_A), F32), jax.ShapeDtypeStruct((1, D_A), F32),
                   jax.ShapeDtypeStruct((1, D_A), F32)],
        args=[puv, dsgu, gln, bln, ws, bs_t], comms=comms)


def attn_bwd(pqkv, datt, gq, gk, sinks, *, seq, name, comms=()):
    T = pqkv.shape[0]
    nb = seq // CHUNK

    def body(p_ref, do_ref, gq_ref, gk_ref, sink_ref, d_ref, dgq_ref, dgk_ref, dsk_ref, kpad, vpad, dkacc, dvacc, dgq_s, dsk_s):
        b = pl.program_id(0)
        kpad[0:CHUNK, :] = jnp.zeros((CHUNK, HEAD_DIM), BF16)
        vpad[0:CHUNK, :] = jnp.zeros((CHUNK, HEAD_DIM), BF16)
        dgq_s[...] = jnp.zeros_like(dgq_s)
        dsk_s[...] = jnp.zeros_like(dsk_s)
        dgk = jnp.zeros((1, HEAD_DIM), F32)
        lane = lax.broadcasted_iota(jnp.int32, (1, 128), 1)
        for h in range(N_KV):
            kcols = slice(K_OFF + h * HEAD_DIM, K_OFF + (h + 1) * HEAD_DIM)
            vcols = slice(V_OFF + h * HEAD_DIM, V_OFF + (h + 1) * HEAD_DIM)
            nk, rk = _rms_parts(p_ref[:, kcols])
            kpad[CHUNK:CHUNK + seq, :] = (nk * gk_ref[...]).astype(BF16)
            vpad[CHUNK:CHUNK + seq, :] = p_ref[:, vcols].astype(BF16)
            dkacc[...] = jnp.zeros_like(dkacc)
            dvacc[...] = jnp.zeros_like(dvacc)

            def block(i, carry):
                r0 = pl.multiple_of(i * CHUNK, CHUNK)
                kk = kpad[pl.ds(r0, 2 * CHUNK), :]
                vv = vpad[pl.ds(r0, 2 * CHUNK), :]
                valid = _attn_mask(i)
                for g in range(Q_PER_KV):
                    hq = h * Q_PER_KV + g
                    cols = slice(hq * HEAD_DIM, (hq + 1) * HEAD_DIM)
                    nq, rq = _rms_parts(p_ref[pl.ds(r0, CHUNK), cols])
                    qn = (nq * gq_ref[...]).astype(BF16)
                    probs, psink = _attn_probs(qn, kk, valid, sink_ref[0, hq])
                    do = do_ref[pl.ds(r0, CHUNK), cols].astype(BF16)
                    dp = _dg(do, vv, NT)
                    dr = jnp.sum(probs * dp, axis=-1, keepdims=True)
                    ds = (probs * (dp - dr) * ATT_SCALE).astype(BF16)
                    dsk_s[...] += jnp.where(lane == hq, -jnp.sum(psink * dr), 0.0)
                    dqn = _dot(ds, kk)
                    dkacc[pl.ds(r0, 2 * CHUNK), :] += _dg(ds, qn, TN)
                    dvacc[pl.ds(r0, 2 * CHUNK), :] += _dg(probs.astype(BF16), do, TN)
                    dn = dqn * gq_ref[...]
                    dq = rq * (dn - nq * jnp.mean(dn * nq, axis=-1, keepdims=True))
                    d_ref[pl.ds(r0, CHUNK), cols] = dq.astype(BF16)
                    dgq_s[...] += jnp.sum(dqn * nq, axis=0, keepdims=True)
                return carry

            lax.fori_loop(0, nb, block, 0)
            dkn = dkacc[CHUNK:CHUNK + seq, :]
            dn = dkn * gk_ref[...]
            d_ref[:, kcols] = (rk * (dn - nk * jnp.mean(dn * nk, axis=-1, keepdims=True))).astype(BF16)
            d_ref[:, vcols] = dvacc[CHUNK:CHUNK + seq, :].astype(BF16)
            dgk = dgk + jnp.sum(dkn * nk, axis=0, keepdims=True)

        @pl.when(b == 0)
        def _():
            dgq_ref[...] = dgq_s[...]
            dgk_ref[...] = dgk
            dsk_ref[...] = jnp.broadcast_to(dsk_s[...], dsk_ref.shape)

        @pl.when(b != 0)
        def _():
            dgq_ref[...] += dgq_s[...]
            dgk_ref[...] += dgk
            dsk_ref[...] += jnp.broadcast_to(dsk_s[...], dsk_ref.shape)

    acc = lambda s: pl.BlockSpec(s, lambda b: (0, 0))
    return _call(
        body, name=name, grid=(T // seq,),
        in_specs=[pl.BlockSpec((seq, QKV_W), lambda b: (b, 0)), pl.BlockSpec((seq, D_B), lambda b: (b, 0)),
                  pl.BlockSpec(gq.shape, lambda b: (0, 0)), pl.BlockSpec(gk.shape, lambda b: (0, 0)),
                  pl.BlockSpec(memory_space=pltpu.SMEM)],
        out_specs=[pl.BlockSpec((seq, QKV_W), lambda b: (b, 0)), acc((1, HEAD_DIM)), acc((1, HEAD_DIM)), acc((8, 128))],
        out_shape=[jax.ShapeDtypeStruct((T, QKV_W), BF16), jax.ShapeDtypeStruct((1, HEAD_DIM), F32),
                   jax.ShapeDtypeStruct((1, HEAD_DIM), F32), jax.ShapeDtypeStruct((8, 128), F32)],
        scratch_shapes=[pltpu.VMEM((seq + CHUNK, HEAD_DIM), BF16), pltpu.VMEM((seq + CHUNK, HEAD_DIM), BF16),
                        pltpu.VMEM((seq + CHUNK, HEAD_DIM), F32), pltpu.VMEM((seq + CHUNK, HEAD_DIM), F32),
                        pltpu.VMEM((1, HEAD_DIM), F32), pltpu.VMEM((1, 128), F32)],
        args=[pqkv, datt, gq, gk, sinks], comms=comms)


def ada_fwd(c_all, w, b, *, name):
    nb, D = c_all.shape
    N = w.shape[1]
    tn = N // 3

    def body(c_ref, w_ref, b_ref, o_ref):
        c = c_ref[...]
        cond = (c * jax.nn.sigmoid(c)).astype(BF16)
        o_ref[...] = _dot(cond, w_ref[...].astype(BF16)) + b_ref[...]

    return _call(
        body, name=name, grid=(3,),
        in_specs=[pl.BlockSpec((nb, D), lambda j: (0, 0)), pl.BlockSpec((D, tn), lambda j: (0, j)),
                  pl.BlockSpec((1, tn), lambda j: (0, j))],
        out_specs=[pl.BlockSpec((nb, tn), lambda j: (0, j))], out_shape=[jax.ShapeDtypeStruct((nb, N), F32)],
        args=[c_all, w, b])[0]


def ada_bwd(c_all, dmods_all, dmods_mine, *, name):
    nb, D = c_all.shape
    NA = dmods_all.shape[1]
    N = dmods_mine.shape[1]
    tn = N // 3

    def body(c_ref, da_ref, dm_ref, db_ref, dw_ref):
        c = c_ref[...]
        cond = (c * jax.nn.sigmoid(c)).astype(BF16)
        dw_ref[...] = _dg(cond, dm_ref[...].astype(BF16), TN)
        db_ref[...] = jnp.sum(da_ref[...], axis=0, keepdims=True)

    return _call(
        body, name=name, grid=(3,),
        in_specs=[pl.BlockSpec((nb, D), lambda j: (0, 0)), pl.BlockSpec((nb, NA), lambda j: (0, 0)),
                  pl.BlockSpec((nb, tn), lambda j: (0, j))],
        out_specs=[pl.BlockSpec((1, NA), lambda j: (0, 0)), pl.BlockSpec((D, tn), lambda j: (0, j))],
        out_shape=[jax.ShapeDtypeStruct((1, NA), F32), jax.ShapeDtypeStruct((D, N), F32)],
        args=[c_all, dmods_all, dmods_mine])


def adamw(w, g, m, v, *, name):
    R, C = w.shape
    rb = _row_block(R, max(8, (1 << 18) // C))
    c1 = 1.0 / (1.0 - ADAM_B1 ** ADAM_STEP)
    c2 = 1.0 / (1.0 - ADAM_B2 ** ADAM_STEP)

    def body(w_ref, g_ref, m_ref, v_ref, d_ref, mo_ref, vo_ref):
        gg = g_ref[...]
        mn = ADAM_B1 * m_ref[...] + (1.0 - ADAM_B1) * gg
        vn = ADAM_B2 * v_ref[...] + (1.0 - ADAM_B2) * (gg * gg)
        mo_ref[...] = mn
        vo_ref[...] = vn
        d_ref[...] = -ADAM_LR * ((mn * c1) / (jnp.sqrt(vn * c2) + ADAM_EPS) + ADAM_WD * w_ref[...])

    blk = pl.BlockSpec((rb, C), lambda i: (i, 0))
    shp = jax.ShapeDtypeStruct((R, C), F32)
    return _call(body, name=name, grid=(R // rb,), in_specs=[blk] * 4, out_specs=[blk] * 3, out_shape=[shp] * 3,
                 args=[w, g, m, v])


def _place():
    return lax.axis_index("x"), lax.axis_index("y"), lax.axis_index("c")


def _flip(v, bit):
    return 1 - v if bit else v


def _other_chips(mx, my):
    return [(_flip(mx, k & 2), _flip(my, k & 1)) for k in range(1, N_QUAD)]


def _remote(src, dst, send_sem, recv_sem, peer):
    return pltpu.make_async_remote_copy(src_ref=src, dst_ref=dst, send_sem=send_sem, recv_sem=recv_sem,
                                        device_id=peer, device_id_type=MESH)


def all_gather8(x, *, name, reduce=False):
    r, n = x.shape

    def body(x_ref, o_ref, *rest):
        if reduce:
            buf, send_sems, recv_sems, lsem = rest
        else:
            buf = o_ref
            send_sems, recv_sems, lsem = rest
        mx, my, mc = _place()
        me = 4 * mx + 2 * my + mc
        mine = pltpu.make_async_copy(x_ref, buf.at[me], lsem)
        mine.start()
        sends, recvs = [], []
        for k in range(1, N_DEV):
            peer = (_flip(mx, k & 4), _flip(my, k & 2), _flip(mc, k & 1))
            pidx = 4 * peer[0] + 2 * peer[1] + peer[2]
            sends.append(_remote(x_ref, buf.at[me], send_sems.at[k - 1], recv_sems.at[k - 1], peer))
            recvs.append(_remote(x_ref, buf.at[pidx], send_sems.at[k - 1], recv_sems.at[k - 1], peer))
        for cp in sends:
            cp.start()
        for cp in recvs:
            cp.wait_recv()
        for cp in sends:
            cp.wait_send()
        mine.wait()
        if reduce:
            total = buf[0]
            for d in range(1, N_DEV):
                total = total + buf[d]
            o_ref[...] = total

    scratch = [pltpu.SemaphoreType.DMA((N_DEV - 1,)), pltpu.SemaphoreType.DMA((N_DEV - 1,)), pltpu.SemaphoreType.DMA]
    if reduce:
        scratch = [pltpu.VMEM((N_DEV, r, n), F32)] + scratch
        out_shape = jax.ShapeDtypeStruct((r, n), F32)
    else:
        out_shape = jax.ShapeDtypeStruct((N_DEV, r, n), F32)
    return pl.pallas_call(
        body, name=name, out_shape=out_shape, in_specs=[pl.BlockSpec(memory_space=pltpu.VMEM)],
        out_specs=pl.BlockSpec(memory_space=pltpu.VMEM), scratch_shapes=scratch,
        compiler_params=pltpu.CompilerParams(vmem_limit_bytes=VMEM_LIMIT_BYTES),
    )(x)


def cast_into_stack(w, quad, *, name):
    R, C = w.shape
    rb = _row_block(R, 256)

    def body(q_ref, w_ref, o_ref):
        o_ref[0] = w_ref[...].astype(BF16)

    return pl.pallas_call(
        body, name=name, out_shape=jax.ShapeDtypeStruct((N_QUAD, R, C), BF16),
        grid_spec=pltpu.PrefetchScalarGridSpec(
            num_scalar_prefetch=1, grid=(R // rb,),
            in_specs=[pl.BlockSpec((rb, C), lambda i, q: (i, 0))],
            out_specs=pl.BlockSpec((1, rb, C), lambda i, q: (q[0], i, 0))),
        compiler_params=pltpu.CompilerParams(dimension_semantics=("arbitrary",), vmem_limit_bytes=VMEM_LIMIT_BYTES),
    )(quad, w)


def gather_comm(stacks):
    n = len(stacks)

    def copies(bufs, ss, rs):
        mx, my, mc = _place()
        q = 2 * mx + my
        sibling = (mx, my, 1 - mc)
        ici_send, ici_recv, fwd_send, fwd_recv = [], [], [], []
        for p in range(n):
            hr = stacks[p].shape[1] // 2
            mine, other = pl.ds(mc * hr, hr), pl.ds((1 - mc) * hr, hr)
            for k, chip in enumerate(_other_chips(mx, my)):
                qk = 2 * chip[0] + chip[1]
                peer = (chip[0], chip[1], mc)
                own, landed, theirs = bufs[p].at[q, mine, :], bufs[p].at[qk, mine, :], bufs[p].at[qk, other, :]
                ici_send.append(_remote(own, own, ss.at[6 * p + k], rs.at[6 * p + k], peer))
                ici_recv.append(_remote(own, landed, ss.at[6 * p + k], rs.at[6 * p + k], peer))
                fwd_send.append(_remote(landed, landed, ss.at[6 * p + 3 + k], rs.at[6 * p + 3 + k], sibling))
                fwd_recv.append(_remote(theirs, theirs, ss.at[6 * p + 3 + k], rs.at[6 * p + 3 + k], sibling))
        return ici_send, ici_recv, fwd_send, fwd_recv

    def start(srcs, bufs, lands, ss, rs):
        for cp in copies(bufs, ss, rs)[0]:
            cp.start()

    def finish(srcs, bufs, lands, ss, rs):
        ici_send, ici_recv, fwd_send, fwd_recv = copies(bufs, ss, rs)
        for arrived, onward in zip(ici_recv, fwd_send):
            arrived.wait_recv()
            onward.start()
        for cp in fwd_recv:
            cp.wait_recv()
        for cp in ici_send + fwd_send:
            cp.wait_send()

    return Comm(bufs=stacks, n_sems=6 * n, start=start, finish=finish)


def rs_pair_comm(gs):
    n = len(gs)

    def copies(srcs, lands, ss, rs):
        mx, my, mc = _place()
        out = []
        for p in range(n):
            hr = gs[p].shape[1] // 2
            out.append(_remote(srcs[p].at[:, pl.ds((1 - mc) * hr, hr), :], lands[p], ss.at[p], rs.at[p], (mx, my, 1 - mc)))
        return out

    def start(srcs, bufs, lands, ss, rs):
        for cp in copies(srcs, lands, ss, rs):
            cp.start()

    def finish(srcs, bufs, lands, ss, rs):
        for cp in copies(srcs, lands, ss, rs):
            cp.wait()

    return Comm(srcs=gs, land_shapes=[jax.ShapeDtypeStruct((g.shape[0], g.shape[1] // 2, g.shape[2]), g.dtype) for g in gs],
                n_sems=n, start=start, finish=finish)


def rs_chip_comm(ss_):
    n = len(ss_)

    def copies(srcs, lands, ss, rs):
        mx, my, mc = _place()
        out = []
        for p in range(n):
            for k, chip in enumerate(_other_chips(mx, my)):
                out.append(_remote(srcs[p].at[2 * chip[0] + chip[1]], lands[p].at[k], ss.at[3 * p + k], rs.at[3 * p + k],
                                   (chip[0], chip[1], mc)))
        return out

    def start(srcs, bufs, lands, ss, rs):
        for cp in copies(srcs, lands, ss, rs):
            cp.start()

    def finish(srcs, bufs, lands, ss, rs):
        for cp in copies(srcs, lands, ss, rs):
            cp.wait()

    return Comm(srcs=ss_, land_shapes=[jax.ShapeDtypeStruct((N_QUAD - 1,) + s.shape[1:], s.dtype) for s in ss_],
                n_sems=3 * n, start=start, finish=finish)


def rs_share_comm(fulls):
    n = len(fulls)

    def copies(bufs, ss, rs):
        mx, my, mc = _place()
        send, recv = [], []
        for p in range(n):
            hr = fulls[p].shape[0] // 2
            mine, other = bufs[p].at[pl.ds(mc * hr, hr), :], bufs[p].at[pl.ds((1 - mc) * hr, hr), :]
            send.append(_remote(mine, mine, ss.at[p], rs.at[p], (mx, my, 1 - mc)))
            recv.append(_remote(other, other, ss.at[p], rs.at[p], (mx, my, 1 - mc)))
        return send, recv

    def start(srcs, bufs, lands, ss, rs):
        for cp in copies(bufs, ss, rs)[0]:
            cp.start()

    def finish(srcs, bufs, lands, ss, rs):
        send, recv = copies(bufs, ss, rs)
        for cp in recv:
            cp.wait_recv()
        for cp in send:
            cp.wait_send()

    return Comm(bufs=fulls, n_sems=n, start=start, finish=finish)


def pair_sum(g, recv, mc, *, name):
    nq, R, C = g.shape
    hr = R // 2
    rb = _row_block(hr, 256)
    nb = hr // rb

    def body(s_ref, g_ref, r_ref, o_ref):
        o_ref[...] = (g_ref[...].astype(F32) + r_ref[...].astype(F32)).astype(BF16)

    return pl.pallas_call(
        body, name=name, out_shape=jax.ShapeDtypeStruct((nq, hr, C), BF16),
        grid_spec=pltpu.PrefetchScalarGridSpec(
            num_scalar_prefetch=1, grid=(nq, nb),
            in_specs=[pl.BlockSpec((1, rb, C), lambda q, i, s: (q, s[0] * nb + i, 0)),
                      pl.BlockSpec((1, rb, C), lambda q, i, s: (q, i, 0))],
            out_specs=pl.BlockSpec((1, rb, C), lambda q, i, s: (q, i, 0))),
        compiler_params=pltpu.CompilerParams(dimension_semantics=("arbitrary", "arbitrary"),
                                             vmem_limit_bytes=VMEM_LIMIT_BYTES),
    )(mc, g, recv)


def chip_sum(s, recv, quad_mc, *, name):
    nq, hr, C = s.shape
    rb = _row_block(hr, 256)
    nb = hr // rb

    def body(q_ref, s_ref, r_ref, o_ref):
        total = s_ref[0].astype(F32)
        for k in range(N_QUAD - 1):
            total = total + r_ref[k].astype(F32)
        o_ref[...] = total

    return pl.pallas_call(
        body, name=name, out_shape=jax.ShapeDtypeStruct((2 * hr, C), F32),
        grid_spec=pltpu.PrefetchScalarGridSpec(
            num_scalar_prefetch=1, grid=(nb,),
            in_specs=[pl.BlockSpec((1, rb, C), lambda i, q: (q[0], i, 0)),
                      pl.BlockSpec((N_QUAD - 1, rb, C), lambda i, q: (0, i, 0))],
            out_specs=pl.BlockSpec((rb, C), lambda i, q: (q[1] * nb + i, 0))),
        compiler_params=pltpu.CompilerParams(dimension_semantics=("arbitrary",), vmem_limit_bytes=VMEM_LIMIT_BYTES),
    )(quad_mc, s, recv)


def _stack_cols(w_full):
    K, N = w_full.shape
    return w_full.reshape(K, N_QUAD, N // N_QUAD).transpose(1, 0, 2)


def _unstack_cols(w4):
    nq, K, n = w4.shape
    return w4.transpose(1, 0, 2).reshape(K, nq * n)


def kernel(x, c, w_ada, b_ada, g_norm1, ffn1_w_gate, ffn1_w_up, ffn1_w_down, g_norm2, w_in, g_sgu_ln, b_sgu_ln, w_spatial, b_spatial, g_q, g_k, attn_sinks, w_branch_a, w_branch_b, w_out, g_norm3, ffn2_w_gate, ffn2_w_up, ffn2_w_down, loss_target, m_w_ada, m_b_ada, m_g_norm1, m_ffn1_w_gate, m_ffn1_w_up, m_ffn1_w_down, m_g_norm2, m_w_in, m_g_sgu_ln, m_b_sgu_ln, m_w_spatial, m_b_spatial, m_g_q, m_g_k, m_attn_sinks, m_w_branch_a, m_w_branch_b, m_w_out, m_g_norm3, m_ffn2_w_gate, m_ffn2_w_up, m_ffn2_w_down, v_w_ada, v_b_ada, v_g_norm1, v_ffn1_w_gate, v_ffn1_w_up, v_ffn1_w_down, v_g_norm2, v_w_in, v_g_sgu_ln, v_b_sgu_ln, v_w_spatial, v_b_spatial, v_g_q, v_g_k, v_attn_sinks, v_w_branch_a, v_w_branch_b, v_w_out, v_g_norm3, v_ffn2_w_gate, v_ffn2_w_up, v_ffn2_w_down):
    B, S, D = x.shape
    T = B * S
    n_mod = b_ada.shape[1] // D
    mx, my, mc = _place()
    quad = 2 * mx + my
    mc_arr = jnp.reshape(mc, (1,)).astype(jnp.int32)
    quad_arr = jnp.reshape(quad, (1,)).astype(jnp.int32)
    quad_mc = jnp.stack([quad, mc]).astype(jnp.int32)
    tm = min(512, S)
    tm_small = min(256, S)
    cpb = min(4, S // CHUNK)

    xt = x.reshape(T, D)
    tgt = loss_target.reshape(T, D)

    stack = lambda w, nm: cast_into_stack(w[0], quad_arr, name=f"stack_{nm}")
    gather1 = gather_comm([stack(ffn1_w_gate, "wg1"), stack(ffn1_w_up, "wu1"), stack(ffn1_w_down, "wd1")])
    run_comms([gather1], name="gather_ffn1")
    wg1, wu1, wd1 = gather1.bufs_out
    gather_mix = gather_comm([stack(w_in, "win"), stack(w_branch_a, "wa"), stack(w_branch_b, "wb"), stack(w_out, "wo")])
    gather3a = gather_comm([stack(ffn2_w_gate, "wg3"), stack(ffn2_w_up, "wu3")])
    gather3b = gather_comm([stack(ffn2_w_down, "wd3")])

    c_all = all_gather8(c, name="gather_cond").reshape(N_DEV * B, D)
    n_ada = w_ada.shape[2]
    b_mine = lax.dynamic_slice(b_ada, (0, quad * n_ada), (1, n_ada))
    mods_part = ada_fwd(c_all, w_ada[0], b_mine, name="ada_fwd")
    mods_parts = all_gather8(mods_part, name="gather_mods")
    mods = jnp.concatenate([mods_parts[2 * j] for j in range(N_QUAD)], axis=1)
    me = 4 * mx + 2 * my + mc
    mods = lax.dynamic_slice(mods, (me * B, 0), (B, n_mod * D))
    sh1, sc1, ga1, sh2, sc2, ga2, sh3, sc3, ga3 = [mods[:, j * D:(j + 1) * D] for j in range(n_mod)]
    bs_t = b_spatial[0].T
    ws = w_spatial[0]

    xn1 = norm_mod_fwd(xt, g_norm1, sc1, sh1, seq=S, tm=tm, name="norm1")
    g1, u1, a1 = ffn_up(xn1, wg1, wu1, tm=tm, name="ffn1_up", comms=[gather_mix])
    win4, wa4, wb4, wo4 = gather_mix.bufs_out
    win = _unstack_cols(win4)
    w_uv, w_qkv, w_gt = win[:, 0:2 * D_A], win[:, 2 * D_A:2 * D_A + QKV_W], win[:, 2 * D_A + QKV_W:]
    wa, wb = _unstack_cols(wa4), _unstack_cols(wb4)
    wo = wo4.reshape(D, D)
    h1, f1 = ffn_down(a1, wd1, xt, ga1, seq=S, tm=tm, name="ffn1_down")
    xn2 = norm_mod_fwd(h1, g_norm2, sc2, sh2, seq=S, tm=tm, name="norm2")
    puv, pqkv, pgt = proj_fwd(xn2, [w_uv, w_qkv, w_gt], tm=tm_small, name="proj")
    sgu = sgu_fwd(puv, g_sgu_ln, b_sgu_ln, ws, bs_t, cpb=cpb, name="sgu_fwd")
    att = attn_fwd(pqkv, g_q, g_k, attn_sinks, seq=S, name="attn_fwd", comms=[gather3a])
    wg3, wu3 = gather3a.bufs_out
    ya, yb, merged, mm, h2 = mixer_out_fwd(sgu, att, pgt, h1, ga2, wa, wb, wo, seq=S, tm=tm_small, name="mixer_out",
                                           comms=[gather3b])
    (wd3,) = gather3b.bufs_out
    xn3 = norm_mod_fwd(h2, g_norm3, sc3, sh3, seq=S, tm=tm, name="norm3")
    g3, u3, a3 = ffn_up(xn3, wg3, wu3, tm=tm, name="ffn2_up")
    dy, f3, lparts = ffn_down(a3, wd3, h2, ga3, tgt, seq=S, tm=tm, name="ffn2_down_loss")
    loss = lax.psum(jnp.sum(lparts[:, 0, 0]), ("x", "y", "c"))

    def sums_of(pair, tag):
        return [pair_sum(g, r, mc_arr, name=f"pair_sum_{tag}_{p}") for p, (g, r) in enumerate(zip(pair.srcs, pair.lands))]

    def halves_of(chip, tag):
        return [chip_sum(s, r, quad_mc, name=f"chip_sum_{tag}_{p}") for p, (s, r) in enumerate(zip(chip.srcs, chip.lands))]

    df3, dga3 = gate_bwd(dy, f3, ga3, 0.5, seq=S, tm=tm, name="ffn2_gate_bwd")
    dg3, du3 = ffn_bwd_act(df3, wd3, g3, u3, tm=tm, name="ffn2_act_bwd")
    (dwd3,) = mm_tn(a3, [df3], tt=tm, name="ffn2_dwd")
    dwg3, dwu3 = mm_tn(xn3, [dg3, du3], tt=tm, name="ffn2_dwgu")
    pair3 = rs_pair_comm([dwg3, dwu3, dwd3])
    dh2, dsh3, dsc3, dgn3 = dx_norm_bwd([dg3, du3], [wg3, wu3], h2, dy, g_norm3, sc3, seq=S, tm=tm_small, name="ffn2_dx",
                                        comms=[pair3])
    chip3 = rs_chip_comm(sums_of(pair3, "ffn2"))

    dm, dga2 = gate_bwd(dh2, mm, ga2, 1.0, seq=S, tm=tm, name="mixer_gate_bwd")
    dgt, dya, dyb, dsgu, datt = mixer_out_bwd(dm, ya, yb, pgt, wa, wb, wo, tm=tm_small, name="mixer_out_bwd")
    (dwo,) = mm_tn(merged, [dm], tt=tm, name="mixer_dwo")
    (dwa,) = mm_tn(sgu, [dya], tt=tm, name="mixer_dwa")
    (dwb,) = mm_tn(att, [dyb], tt=tm, name="mixer_dwb")
    duv, dws, dzs, dgln, dbln = sgu_bwd(puv, dsgu, g_sgu_ln, b_sgu_ln, ws, bs_t, cpb=cpb, name="sgu_bwd")
    dqkv, dgq, dgk, dsk = attn_bwd(pqkv, datt, g_q, g_k, attn_sinks, seq=S, name="attn_bwd", comms=[chip3])
    share3 = rs_share_comm(halves_of(chip3, "ffn2"))
    dwin_parts = mm_tn(xn2, [duv, dqkv, dgt], tt=tm, name="mixer_dwin")
    dwin4 = _stack_cols(jnp.concatenate(dwin_parts, axis=1))
    pair_mix = rs_pair_comm([dwin4, _stack_cols(dwa), _stack_cols(dwb), dwo.reshape(N_QUAD, D // N_QUAD, D)])
    dh1, dsh2, dsc2, dgn2 = dx_norm_bwd([duv, dqkv, dgt], [w_uv, w_qkv, w_gt], h1, dh2, g_norm2, sc2, seq=S,
                                        tm=tm_small, name="mixer_dx", comms=[pair_mix])
    chip_mix = rs_chip_comm(sums_of(pair_mix, "mix"))

    df1, dga1 = gate_bwd(dh1, f1, ga1, 0.5, seq=S, tm=tm, name="ffn1_gate_bwd")
    dg1, du1 = ffn_bwd_act(df1, wd1, g1, u1, tm=tm, name="ffn1_act_bwd", comms=[share3])
    (dwd1,) = mm_tn(a1, [df1], tt=tm, name="ffn1_dwd")
    dwg1, dwu1 = mm_tn(xn1, [dg1, du1], tt=tm, name="ffn1_dwgu", comms=[chip_mix])
    pair1 = rs_pair_comm([dwg1, dwu1, dwd1])
    run_comms([pair1], name="rs_pair_ffn1")
    chip1 = rs_chip_comm(sums_of(pair1, "ffn1"))
    dx, dsh1, dsc1, dgn1 = dx_norm_bwd([dg1, du1], [wg1, wu1], xt, dh1, g_norm1, sc1, seq=S, tm=tm_small, name="ffn1_dx",
                                       comms=[chip1])
    share_rest = rs_share_comm(halves_of(chip_mix, "mix") + halves_of(chip1, "ffn1"))
    run_comms([share_rest], name="rs_share_rest")
    r_win, r_wa, r_wb, r_wo, r_wg1, r_wu1, r_wd1 = share_rest.bufs_out
    r_wg3, r_wu3, r_wd3 = share3.bufs_out

    dmods = jnp.concatenate([dsh1, dsc1, dga1, dsh2, dsc2, dga2, dsh3, dsc3, dga3], axis=1)
    dmods_all = all_gather8(dmods, name="gather_dmods").reshape(N_DEV * B, n_mod * D)
    dmods_mine = lax.dynamic_slice(dmods_all, (0, quad * n_ada), (N_DEV * B, n_ada))
    db_ada, dw_ada = ada_bwd(c_all, dmods_all, dmods_mine, name="ada_bwd")

    db_s = dzs.reshape(CHUNK, N_GROUPS, D_A // N_GROUPS).sum(axis=2).T.reshape(1, N_GROUPS * CHUNK)
    tail = jnp.concatenate([db_s, dgq, dgk, dsk[0:1, 0:N_Q]], axis=1)
    tail = jnp.pad(tail, ((0, 0), (0, (-tail.shape[1]) % D)))
    lnrow = jnp.concatenate([dgln, dbln], axis=1)
    lnrow = jnp.pad(lnrow, ((0, 0), (0, (-lnrow.shape[1]) % D)))
    packed = jnp.concatenate([dgn1, dgn2, dgn3, lnrow.reshape(-1, D), tail.reshape(-1, D), dws.reshape(-1, D)], axis=0)
    n_rows = packed.shape[0]
    packed = jnp.pad(packed, ((0, (-n_rows) % 8), (0, 0)))
    small = all_gather8(packed, name="reduce_small", reduce=True)
    ln_rows = lnrow.size // D
    tail_rows = tail.size // D
    r = 3
    g_gn1, g_gn2, g_gn3 = small[0:1], small[1:2], small[2:3]
    ln_flat = small[r:r + ln_rows].reshape(1, -1)
    r += ln_rows
    tail_flat = small[r:r + tail_rows].reshape(1, -1)
    r += tail_rows
    g_ws = small[r:r + dws.size // D].reshape(w_spatial.shape)
    g_gln, g_bln = ln_flat[:, 0:D_A], ln_flat[:, D_A:2 * D_A]
    o = N_GROUPS * CHUNK
    g_bs = tail_flat[:, 0:o].reshape(b_spatial.shape)
    g_gq, g_gk, g_sk = tail_flat[:, o:o + HEAD_DIM], tail_flat[:, o + HEAD_DIM:o + 2 * HEAD_DIM], tail_flat[:, o + 2 * HEAD_DIM:o + 2 * HEAD_DIM + N_Q]

    def update(name, w, g, m, v):
        shape = w.shape
        two_d = lambda a: a.reshape(-1, shape[-1])
        d, mn, vn = adamw(two_d(w), two_d(g), two_d(m), two_d(v), name=f"adamw_{name}")
        return g.reshape(shape), d.reshape(shape), mn.reshape(shape), vn.reshape(shape)

    names = ["w_ada", "b_ada", "g_norm1", "ffn1_w_gate", "ffn1_w_up", "ffn1_w_down", "g_norm2", "w_in", "g_sgu_ln",
             "b_sgu_ln", "w_spatial", "b_spatial", "g_q", "g_k", "attn_sinks", "w_branch_a", "w_branch_b", "w_out",
             "g_norm3", "ffn2_w_gate", "ffn2_w_up", "ffn2_w_down"]
    weights = dict(zip(names, [w_ada, b_ada, g_norm1, ffn1_w_gate, ffn1_w_up, ffn1_w_down, g_norm2, w_in, g_sgu_ln,
                               b_sgu_ln, w_spatial, b_spatial, g_q, g_k, attn_sinks, w_branch_a, w_branch_b, w_out,
                               g_norm3, ffn2_w_gate, ffn2_w_up, ffn2_w_down]))
    m_in = dict(zip(names, [m_w_ada, m_b_ada, m_g_norm1, m_ffn1_w_gate, m_ffn1_w_up, m_ffn1_w_down, m_g_norm2, m_w_in,
                            m_g_sgu_ln, m_b_sgu_ln, m_w_spatial, m_b_spatial, m_g_q, m_g_k, m_attn_sinks, m_w_branch_a,
                            m_w_branch_b, m_w_out, m_g_norm3, m_ffn2_w_gate, m_ffn2_w_up, m_ffn2_w_down]))
    v_in = dict(zip(names, [v_w_ada, v_b_ada, v_g_norm1, v_ffn1_w_gate, v_ffn1_w_up, v_ffn1_w_down, v_g_norm2, v_w_in,
                            v_g_sgu_ln, v_b_sgu_ln, v_w_spatial, v_b_spatial, v_g_q, v_g_k, v_attn_sinks, v_w_branch_a,
                            v_w_branch_b, v_w_out, v_g_norm3, v_ffn2_w_gate, v_ffn2_w_up, v_ffn2_w_down]))
    grads = {
        "w_ada": dw_ada[None], "b_ada": db_ada, "g_norm1": g_gn1, "g_norm2": g_gn2, "g_norm3": g_gn3,
        "g_sgu_ln": g_gln, "b_sgu_ln": g_bln, "w_spatial": g_ws, "b_spatial": g_bs, "g_q": g_gq, "g_k": g_gk,
        "attn_sinks": g_sk,
        "ffn1_w_gate": r_wg1[None], "ffn1_w_up": r_wu1[None], "ffn1_w_down": r_wd1[None], "w_in": r_win[None],
        "w_branch_a": r_wa[None], "w_branch_b": r_wb[None], "w_out": r_wo[None],
        "ffn2_w_gate": r_wg3[None], "ffn2_w_up": r_wu3[None], "ffn2_w_down": r_wd3[None],
    }

    small_names = ["b_ada", "g_norm1", "g_norm2", "g_norm3", "g_sgu_ln", "b_sgu_ln", "w_spatial", "b_spatial", "g_q",
                   "g_k", "attn_sinks"]

    def pack(d):
        flat = jnp.concatenate([d[nm].reshape(-1) for nm in small_names])
        return jnp.pad(flat, (0, (-flat.size) % (8 * 128))).reshape(-1, 128)

    sd, sm, sv = adamw(pack(weights), pack(grads), pack(m_in), pack(v_in), name="adamw_small")
    delta, new_m, new_v = {}, {}, {}
    off = 0
    for nm in small_names:
        size, shape = weights[nm].size, weights[nm].shape
        delta[nm] = sd.reshape(-1)[off:off + size].reshape(shape)
        new_m[nm] = sm.reshape(-1)[off:off + size].reshape(shape)
        new_v[nm] = sv.reshape(-1)[off:off + size].reshape(shape)
        off += size
    for nm in names:
        if nm not in small_names:
            grads[nm], delta[nm], new_m[nm], new_v[nm] = update(nm, weights[nm], grads[nm], m_in[nm], v_in[nm])
        else:
            grads[nm] = grads[nm].reshape(weights[nm].shape)

    return (loss, dx.reshape(B, S, D), *[grads[nm] for nm in names], *[delta[nm] for nm in names],
            *[new_m[nm] for nm in names], *[new_v[nm] for nm in names])
```

```python
import functools
import math
import operator

import jax
import jax.numpy as jnp
from jax import lax
from jax.experimental import pallas as pl
from jax.experimental.pallas import tpu as pltpu

F32 = jnp.float32
BF16 = jnp.bfloat16
EPS = 1e-6
NEG = -1e30
N_DEV = 8
N_QUAD = 4
D_A = 512
N_GROUPS = 4
CHUNK = 128
HEAD_DIM = 64
N_KV = 2
Q_PER_KV = 4
N_Q = N_KV * Q_PER_KV
D_B = N_Q * HEAD_DIM
QKV_W = D_B + 2 * N_KV * HEAD_DIM
K_OFF = D_B
V_OFF = D_B + N_KV * HEAD_DIM
ATT_SCALE = HEAD_DIM ** -0.5
GELU_K = math.sqrt(2.0 / math.pi)
GELU_C = 0.044715
ADAM_LR, ADAM_B1, ADAM_B2, ADAM_EPS, ADAM_WD, ADAM_STEP = 0.001, 0.9, 0.999, 1e-08, 0.01, 10
VMEM_LIMIT_BYTES = 56 * 1024 * 1024
MESH = pl.DeviceIdType.MESH
NT = (((1,), (1,)), ((), ()))
TN = (((0,), (0,)), ((), ()))
ANY = pl.BlockSpec(memory_space=pl.ANY)


def _dot(a, b):
    return jnp.dot(a, b, preferred_element_type=F32)


def _dg(a, b, dims):
    return lax.dot_general(a, b, dims, preferred_element_type=F32)


def _gelu_parts(x):
    t = jnp.tanh(GELU_K * (x + GELU_C * x * x * x))
    return 0.5 * x * (1.0 + t), t


def _dgelu(x, t):
    return 0.5 * (1.0 + t) + 0.5 * x * (1.0 - t * t) * (GELU_K * (1.0 + 3.0 * GELU_C * x * x))


def _row_block(rows, target):
    b = min(rows, target)
    while rows % b or b % 8:
        b -= 1
    return b


class Comm:
    def __init__(self, *, srcs=(), bufs=(), land_shapes=(), n_sems, start, finish):
        self.srcs, self.bufs, self.land_shapes = list(srcs), list(bufs), list(land_shapes)
        self.n_sems, self.start, self.finish = n_sems, start, finish
        self.bufs_out, self.lands = None, None


def _call(body, *, name, grid, in_specs, out_specs, out_shape, args, scratch_shapes=(), comms=()):
    in_specs, out_specs, out_shape = list(in_specs), list(out_specs), list(out_shape)
    args, scratch = list(args), list(scratch_shapes)
    n_in, n_out, n_scr = len(args), len(out_shape), len(scratch)
    aliases, layout = {}, []
    for cm in comms:
        i0, o0, s0 = len(args), len(out_shape), len(scratch)
        args += cm.srcs + cm.bufs
        in_specs += [ANY] * (len(cm.srcs) + len(cm.bufs))
        out_shape += [jax.ShapeDtypeStruct(b.shape, b.dtype) for b in cm.bufs] + cm.land_shapes
        out_specs += [ANY] * (len(cm.bufs) + len(cm.land_shapes))
        for j in range(len(cm.bufs)):
            aliases[i0 + len(cm.srcs) + j] = o0 + j
        scratch += [pltpu.SemaphoreType.DMA((cm.n_sems,)), pltpu.SemaphoreType.DMA((cm.n_sems,))]
        layout.append((i0, o0, s0))
    n_in_all, n_out_all = len(args), len(out_shape)

    def wrapped(*refs):
        ins, outs, scr = refs[:n_in_all], refs[n_in_all:n_in_all + n_out_all], refs[n_in_all + n_out_all:]
        parts = []
        for cm, (i0, o0, s0) in zip(comms, layout):
            nb = len(cm.bufs)
            parts.append((ins[i0:i0 + len(cm.srcs)], outs[o0:o0 + nb], outs[o0 + nb:o0 + nb + len(cm.land_shapes)],
                          scr[s0], scr[s0 + 1]))
        if comms and grid:
            ids = [pl.program_id(d) for d in range(len(grid))]
            first = functools.reduce(operator.and_, [i == 0 for i in ids])
            last = functools.reduce(operator.and_, [i == g - 1 for i, g in zip(ids, grid)])

            @pl.when(first)
            def _():
                for cm, p in zip(comms, parts):
                    cm.start(*p)
        elif comms:
            for cm, p in zip(comms, parts):
                cm.start(*p)
        if body is not None:
            body(*ins[:n_in], *outs[:n_out], *scr[:n_scr])
        if comms and grid:
            @pl.when(last)
            def _():
                for cm, p in zip(comms, parts):
                    cm.finish(*p)
        elif comms:
            for cm, p in zip(comms, parts):
                cm.finish(*p)

    kwargs = dict(grid=grid) if grid else {}
    sem = ("arbitrary",) * len(grid)
    res = pl.pallas_call(
        wrapped, name=name, in_specs=in_specs, out_specs=out_specs, out_shape=out_shape, scratch_shapes=scratch,
        input_output_aliases=aliases,
        compiler_params=pltpu.CompilerParams(dimension_semantics=sem, vmem_limit_bytes=VMEM_LIMIT_BYTES), **kwargs,
    )(*args)
    for cm, (i0, o0, s0) in zip(comms, layout):
        nb = len(cm.bufs)
        cm.bufs_out = list(res[o0:o0 + nb])
        cm.lands = list(res[o0 + nb:o0 + nb + len(cm.land_shapes)])
    return list(res[:n_out])


def run_comms(comms, *, name):
    _call(None, name=name, grid=(), in_specs=[], out_specs=[], out_shape=[], args=[], comms=comms)


def norm_mod_fwd(h, g, sc, sh, *, seq, tm, name, comms=()):
    T, D = h.shape
    B, tps = T // seq, seq // tm

    def body(h_ref, g_ref, sc_ref, sh_ref, o_ref):
        x = h_ref[...]
        r = lax.rsqrt(jnp.mean(x * x, axis=-1, keepdims=True) + EPS)
        y = x * r * g_ref[...]
        o_ref[...] = (y * (1.0 + sc_ref[0]) + sh_ref[0]).astype(o_ref.dtype)

    per_seq = pl.BlockSpec((1, 1, D), lambda i: (i // tps, 0, 0))
    return _call(
        body, name=name, grid=(T // tm,),
        in_specs=[pl.BlockSpec((tm, D), lambda i: (i, 0)), pl.BlockSpec((1, D), lambda i: (0, 0)), per_seq, per_seq],
        out_specs=[pl.BlockSpec((tm, D), lambda i: (i, 0))], out_shape=[jax.ShapeDtypeStruct((T, D), BF16)],
        args=[h, g, sc.reshape(B, 1, D), sh.reshape(B, 1, D)], comms=comms)[0]


def ffn_up(xn, wg, wu, *, tm, name, comms=()):
    T, D = xn.shape
    nq, fs, _ = wg.shape

    def body(x_ref, wg_ref, wu_ref, g_ref, u_ref, a_ref):
        x = x_ref[...]
        g = _dg(x, wg_ref[0], NT)
        u = _dg(x, wu_ref[0], NT)
        g_ref[0] = g.astype(BF16)
        u_ref[0] = u.astype(BF16)
        a_ref[0] = (g * jax.nn.sigmoid(g) * u).astype(BF16)

    w_spec = pl.BlockSpec((1, fs, D), lambda q, i: (q, 0, 0))
    o_spec = pl.BlockSpec((1, tm, fs), lambda q, i: (q, i, 0))
    o_shape = jax.ShapeDtypeStruct((nq, T, fs), BF16)
    return _call(
        body, name=name, grid=(nq, T // tm),
        in_specs=[pl.BlockSpec((tm, D), lambda q, i: (i, 0)), w_spec, w_spec],
        out_specs=[o_spec, o_spec, o_spec], out_shape=[o_shape, o_shape, o_shape], args=[xn, wg, wu], comms=comms)


def ffn_down(a, wd, hin, ga, target=None, *, seq, tm, name, comms=()):
    nq, T, fs = a.shape
    D = wd.shape[-1]
    B, tps, nt = T // seq, seq // tm, T // tm
    with_loss = target is not None

    def body(*refs):
        if with_loss:
            a_ref, wd_ref, h_ref, ga_ref, t_ref, o_ref, f_ref, l_ref = refs
        else:
            a_ref, wd_ref, h_ref, ga_ref, o_ref, f_ref = refs
        f = _dot(a_ref[0], wd_ref[0])
        for q in range(1, nq):
            f = f + _dot(a_ref[q], wd_ref[q])
        f_ref[...] = f.astype(BF16)
        y = h_ref[...] + (0.5 * ga_ref[0]) * f
        if with_loss:
            e = y - t_ref[...]
            o_ref[...] = e * (1.0 / D)
            l_ref[...] = jnp.full(l_ref.shape, 0.5 * jnp.sum(e * e) * (1.0 / D), F32)
        else:
            o_ref[...] = y

    tile = pl.BlockSpec((tm, D), lambda i: (i, 0))
    in_specs = [pl.BlockSpec((nq, tm, fs), lambda i: (0, i, 0)), pl.BlockSpec((nq, fs, D), lambda i: (0, 0, 0)),
                tile, pl.BlockSpec((1, 1, D), lambda i: (i // tps, 0, 0))]
    out_specs = [tile, tile]
    out_shape = [jax.ShapeDtypeStruct((T, D), F32), jax.ShapeDtypeStruct((T, D), BF16)]
    args = [a, wd, hin, ga.reshape(B, 1, D)]
    if with_loss:
        in_specs.append(tile)
        args.append(target)
        out_specs.append(pl.BlockSpec((1, 8, 128), lambda i: (i, 0, 0)))
        out_shape.append(jax.ShapeDtypeStruct((nt, 8, 128), F32))
    return _call(body, name=name, grid=(nt,), in_specs=in_specs, out_specs=out_specs, out_shape=out_shape, args=args,
                 comms=comms)


def proj_fwd(xn, ws, *, tm, name, comms=()):
    T, D = xn.shape
    n = len(ws)

    def body(*refs):
        x = refs[0][...]
        for j in range(n):
            refs[1 + n + j][...] = _dg(x, refs[1 + j][...], NT)

    return _call(
        body, name=name, grid=(T // tm,),
        in_specs=[pl.BlockSpec((tm, D), lambda i: (i, 0))] + [pl.BlockSpec(w.shape, lambda i: (0, 0)) for w in ws],
        out_specs=[pl.BlockSpec((tm, w.shape[0]), lambda i: (i, 0)) for w in ws],
        out_shape=[jax.ShapeDtypeStruct((T, w.shape[0]), F32) for w in ws], args=[xn, *ws], comms=comms)


def _layer_norm_parts(v):
    mu = jnp.mean(v, axis=-1, keepdims=True)
    d = v - mu
    rstd = lax.rsqrt(jnp.mean(d * d, axis=-1, keepdims=True) + EPS)
    return d * rstd, rstd


def _causal():
    row = lax.broadcasted_iota(jnp.int32, (CHUNK, CHUNK), 0)
    col = lax.broadcasted_iota(jnp.int32, (CHUNK, CHUNK), 1)
    return row >= col


def sgu_fwd(puv, gln, bln, ws, bs_t, *, cpb, name, comms=()):
    T = puv.shape[0]
    gd = D_A // N_GROUPS
    tm = cpb * CHUNK

    def body(p_ref, gln_ref, bln_ref, ws_ref, bs_ref, o_ref):
        causal = _causal()
        wm = [jnp.where(causal, ws_ref[g], 0.0).astype(BF16) for g in range(N_GROUPS)]
        for j in range(cpb):
            rows = slice(j * CHUNK, (j + 1) * CHUNK)
            u, _ = _gelu_parts(p_ref[rows, 0:D_A])
            vv, _ = _gelu_parts(p_ref[rows, D_A:2 * D_A])
            vhat, _ = _layer_norm_parts(vv)
            vn = (vhat * gln_ref[...] + bln_ref[...]).astype(BF16)
            for g in range(N_GROUPS):
                cols = slice(g * gd, (g + 1) * gd)
                z = _dot(wm[g], vn[:, cols]) + bs_ref[:, g:g + 1]
                o_ref[rows, cols] = (u[:, cols] * z).astype(BF16)

    full = lambda a: pl.BlockSpec(a.shape, lambda i: (0,) * a.ndim)
    return _call(
        body, name=name, grid=(T // tm,),
        in_specs=[pl.BlockSpec((tm, 2 * D_A), lambda i: (i, 0)), full(gln), full(bln), full(ws), full(bs_t)],
        out_specs=[pl.BlockSpec((tm, D_A), lambda i: (i, 0))], out_shape=[jax.ShapeDtypeStruct((T, D_A), BF16)],
        args=[puv, gln, bln, ws, bs_t], comms=comms)[0]


def _rms_parts(x):
    r = lax.rsqrt(jnp.mean(x * x, axis=-1, keepdims=True) + EPS)
    return x * r, r


def _attn_mask(i):
    qi = lax.broadcasted_iota(jnp.int32, (CHUNK, 2 * CHUNK), 0)
    kj = lax.broadcasted_iota(jnp.int32, (CHUNK, 2 * CHUNK), 1)
    diff = qi + CHUNK - kj
    return (diff >= 0) & (diff < CHUNK) & ((kj >= CHUNK) | (i > 0))


def _attn_probs(qn, kk, valid, sink):
    s = _dg(qn, kk, NT) * ATT_SCALE
    s = jnp.where(valid, s, NEG)
    m = jnp.maximum(jnp.max(s, axis=-1, keepdims=True), sink)
    p = jnp.exp(s - m)
    es = jnp.exp(sink - m)
    inv = 1.0 / (jnp.sum(p, axis=-1, keepdims=True) + es)
    return p * inv, es * inv


def attn_fwd(pqkv, gq, gk, sinks, *, seq, name, comms=()):
    T = pqkv.shape[0]
    nb = seq // CHUNK

    def body(p_ref, gq_ref, gk_ref, sink_ref, o_ref, kpad, vpad):
        kpad[0:CHUNK, :] = jnp.zeros((CHUNK, HEAD_DIM), BF16)
        vpad[0:CHUNK, :] = jnp.zeros((CHUNK, HEAD_DIM), BF16)
        for h in range(N_KV):
            nk, _ = _rms_parts(p_ref[:, K_OFF + h * HEAD_DIM:K_OFF + (h + 1) * HEAD_DIM])
            kpad[CHUNK:CHUNK + seq, :] = (nk * gk_ref[...]).astype(BF16)
            vpad[CHUNK:CHUNK + seq, :] = p_ref[:, V_OFF + h * HEAD_DIM:V_OFF + (h + 1) * HEAD_DIM].astype(BF16)

            def block(i, carry):
                r0 = pl.multiple_of(i * CHUNK, CHUNK)
                kk = kpad[pl.ds(r0, 2 * CHUNK), :]
                vv = vpad[pl.ds(r0, 2 * CHUNK), :]
                valid = _attn_mask(i)
                for g in range(Q_PER_KV):
                    hq = h * Q_PER_KV + g
                    cols = slice(hq * HEAD_DIM, (hq + 1) * HEAD_DIM)
                    nq, _ = _rms_parts(p_ref[pl.ds(r0, CHUNK), cols])
                    qn = (nq * gq_ref[...]).astype(BF16)
                    probs, _ = _attn_probs(qn, kk, valid, sink_ref[0, hq])
                    o_ref[pl.ds(r0, CHUNK), cols] = _dot(probs.astype(BF16), vv).astype(BF16)
                return carry

            lax.fori_loop(0, nb, block, 0)

    return _call(
        body, name=name, grid=(T // seq,),
        in_specs=[pl.BlockSpec((seq, QKV_W), lambda b: (b, 0)), pl.BlockSpec(gq.shape, lambda b: (0, 0)),
                  pl.BlockSpec(gk.shape, lambda b: (0, 0)), pl.BlockSpec(memory_space=pltpu.SMEM)],
        out_specs=[pl.BlockSpec((seq, D_B), lambda b: (b, 0))], out_shape=[jax.ShapeDtypeStruct((T, D_B), BF16)],
        scratch_shapes=[pltpu.VMEM((seq + CHUNK, HEAD_DIM), BF16), pltpu.VMEM((seq + CHUNK, HEAD_DIM), BF16)],
        args=[pqkv, gq, gk, sinks], comms=comms)[0]


def mixer_out_fwd(sgu, att, pg, hin, ga, wa, wb, wo, *, seq, tm, name, comms=()):
    T, D = hin.shape
    B, tps = T // seq, seq // tm

    def body(s_ref, t_ref, pg_ref, h_ref, ga_ref, wa_ref, wb_ref, wo_ref, ya_ref, yb_ref, mg_ref, m_ref, ho_ref):
        ya = _dot(s_ref[...], wa_ref[...])
        yb = _dot(t_ref[...], wb_ref[...])
        merged = jax.nn.sigmoid(pg_ref[:, 0:D]) * ya + jax.nn.sigmoid(pg_ref[:, D:2 * D]) * yb
        mg = merged.astype(BF16)
        m = _dot(mg, wo_ref[...])
        ya_ref[...] = ya.astype(BF16)
        yb_ref[...] = yb.astype(BF16)
        mg_ref[...] = mg
        m_ref[...] = m.astype(BF16)
        ho_ref[...] = h_ref[...] + ga_ref[0] * m

    tile = lambda w: pl.BlockSpec((tm, w), lambda i: (i, 0))
    full = lambda a: pl.BlockSpec(a.shape, lambda i: (0, 0))
    b16 = jax.ShapeDtypeStruct((T, D), BF16)
    return _call(
        body, name=name, grid=(T // tm,),
        in_specs=[tile(D_A), tile(D_B), tile(2 * D), tile(D), pl.BlockSpec((1, 1, D), lambda i: (i // tps, 0, 0)),
                  full(wa), full(wb), full(wo)],
        out_specs=[tile(D)] * 5, out_shape=[b16, b16, b16, b16, jax.ShapeDtypeStruct((T, D), F32)],
        args=[sgu, att, pg, hin, ga.reshape(B, 1, D), wa, wb, wo], comms=comms)


def gate_bwd(dh, f, ga, scale, *, seq, tm, name, comms=()):
    T, D = dh.shape
    B, tps = T // seq, seq // tm

    def body(dh_ref, f_ref, ga_ref, df_ref, dga_ref):
        i = pl.program_id(0)
        d = dh_ref[...]
        df_ref[...] = ((scale * ga_ref[0]) * d).astype(BF16)
        part = scale * jnp.sum(d * f_ref[...].astype(F32), axis=0, keepdims=True)

        @pl.when(i % tps == 0)
        def _():
            dga_ref[0] = part

        @pl.when(i % tps != 0)
        def _():
            dga_ref[0] += part

    tile = pl.BlockSpec((tm, D), lambda i: (i, 0))
    per_seq = pl.BlockSpec((1, 1, D), lambda i: (i // tps, 0, 0))
    df, dga = _call(
        body, name=name, grid=(T // tm,), in_specs=[tile, tile, per_seq], out_specs=[tile, per_seq],
        out_shape=[jax.ShapeDtypeStruct((T, D), BF16), jax.ShapeDtypeStruct((B, 1, D), F32)],
        args=[dh, f, ga.reshape(B, 1, D)], comms=comms)
    return df, dga.reshape(B, D)


def ffn_bwd_act(df, wd, g, u, *, tm, name, comms=()):
    T, D = df.shape
    nq, fs, _ = wd.shape

    def body(df_ref, wd_ref, g_ref, u_ref, dg_ref, du_ref):
        da = _dg(df_ref[...], wd_ref[0], NT)
        gg = g_ref[0].astype(F32)
        sg = jax.nn.sigmoid(gg)
        du_ref[0] = (da * (gg * sg)).astype(BF16)
        dg_ref[0] = (da * u_ref[0].astype(F32) * (sg * (1.0 + gg * (1.0 - sg)))).astype(BF16)

    act = pl.BlockSpec((1, tm, fs), lambda q, i: (q, i, 0))
    o_shape = jax.ShapeDtypeStruct((nq, T, fs), BF16)
    return _call(
        body, name=name, grid=(nq, T // tm),
        in_specs=[pl.BlockSpec((tm, D), lambda q, i: (i, 0)), pl.BlockSpec((1, fs, D), lambda q, i: (q, 0, 0)), act, act],
        out_specs=[act, act], out_shape=[o_shape, o_shape], args=[df, wd, g, u], comms=comms)


def mm_tn(pairs, *, tt, name, comms=()):
    n = len(pairs)
    flat = [a for pair in pairs for a in pair]
    nq = max([a.shape[0] for a in flat if a.ndim == 3] + [1])
    T = flat[0].shape[-2]
    nt = T // tt
    stacked = [x.ndim == 3 or y.ndim == 3 for x, y in pairs]

    def body(*refs):
        in_refs, o_refs, accs = refs[:2 * n], refs[2 * n:3 * n], refs[3 * n:]
        t = pl.program_id(1)
        for j, (x, y) in enumerate(pairs):
            xv = in_refs[2 * j][0] if x.ndim == 3 else in_refs[2 * j][...]
            yv = in_refs[2 * j + 1][0] if y.ndim == 3 else in_refs[2 * j + 1][...]
            part = _dg(xv, yv, TN)

            @pl.when(t == 0)
            def _():
                accs[j][...] = part

            @pl.when(t != 0)
            def _():
                accs[j][...] += part

        @pl.when(t == nt - 1)
        def _():
            for j in range(n):
                if stacked[j]:
                    o_refs[j][0] = accs[j][...].astype(BF16)
                else:
                    o_refs[j][...] = accs[j][...].astype(BF16)

    def spec(a):
        if a.ndim == 3:
            return pl.BlockSpec((1, tt, a.shape[2]), lambda q, t: (q, t, 0))
        return pl.BlockSpec((tt, a.shape[1]), lambda q, t: (t, 0))

    out_specs, out_shape = [], []
    for j, (x, y) in enumerate(pairs):
        K, N = x.shape[-1], y.shape[-1]
        if stacked[j]:
            out_specs.append(pl.BlockSpec((1, K, N), lambda q, t: (q, 0, 0)))
            out_shape.append(jax.ShapeDtypeStruct((nq, K, N), BF16))
        else:
            out_specs.append(pl.BlockSpec((K, N), lambda q, t: (0, 0)))
            out_shape.append(jax.ShapeDtypeStruct((K, N), BF16))
    return _call(
        body, name=name, grid=(nq, nt), in_specs=[spec(a) for a in flat],
        out_specs=out_specs, out_shape=out_shape,
        scratch_shapes=[pltpu.VMEM((x.shape[-1], y.shape[-1]), F32) for x, y in pairs], args=flat, comms=comms)


def dx_norm_bwd(dys, ws, hin, dh_out, g, sc, *, seq, tm, name, comms=()):
    T, D = hin.shape
    B, tps = T // seq, seq // tm
    n = len(dys)

    def body(*refs):
        dy_refs, w_refs = refs[:n], refs[n:2 * n]
        h_ref, dho_ref, g_ref, sc_ref, dhi_ref, dsh_ref, dsc_ref, dgn_ref = refs[2 * n:]
        i = pl.program_id(0)
        dxn = None
        for j in range(n):
            if dys[j].ndim == 3:
                for q in range(dys[j].shape[0]):
                    part = _dot(dy_refs[j][q], w_refs[j][q])
                    dxn = part if dxn is None else dxn + part
            else:
                part = _dot(dy_refs[j][...], w_refs[j][...])
                dxn = part if dxn is None else dxn + part
        x = h_ref[...]
        nx, r = _rms_parts(x)
        gain = g_ref[...]
        one_sc = 1.0 + sc_ref[0]
        dsh = jnp.sum(dxn, axis=0, keepdims=True)
        dsc = jnp.sum(dxn * (nx * gain), axis=0, keepdims=True)
        dgn = jnp.sum(dxn * one_sc * nx, axis=0, keepdims=True)
        dn = dxn * one_sc * gain
        dhi_ref[...] = dho_ref[...] + r * (dn - nx * jnp.mean(dn * nx, axis=-1, keepdims=True))

        @pl.when(i % tps == 0)
        def _():
            dsh_ref[0] = dsh
            dsc_ref[0] = dsc

        @pl.when(i % tps != 0)
        def _():
            dsh_ref[0] += dsh
            dsc_ref[0] += dsc

        @pl.when(i == 0)
        def _():
            dgn_ref[...] = dgn

        @pl.when(i != 0)
        def _():
            dgn_ref[...] += dgn

    def dy_spec(a):
        if a.ndim == 3:
            return pl.BlockSpec((a.shape[0], tm, a.shape[2]), lambda i: (0, i, 0))
        return pl.BlockSpec((tm, a.shape[1]), lambda i: (i, 0))

    tile = pl.BlockSpec((tm, D), lambda i: (i, 0))
    per_seq = pl.BlockSpec((1, 1, D), lambda i: (i // tps, 0, 0))
    row = pl.BlockSpec((1, D), lambda i: (0, 0))
    dhi, dsh, dsc, dgn = _call(
        body, name=name, grid=(T // tm,),
        in_specs=[dy_spec(a) for a in dys] + [pl.BlockSpec(w.shape, lambda i, nd=w.ndim: (0,) * nd) for w in ws]
        + [tile, tile, row, per_seq],
        out_specs=[tile, per_seq, per_seq, row],
        out_shape=[jax.ShapeDtypeStruct((T, D), F32), jax.ShapeDtypeStruct((B, 1, D), F32),
                   jax.ShapeDtypeStruct((B, 1, D), F32), jax.ShapeDtypeStruct((1, D), F32)],
        args=[*dys, *ws, hin, dh_out, g, sc.reshape(B, 1, D)], comms=comms)
    return dhi, dsh.reshape(B, D), dsc.reshape(B, D), dgn


def mixer_out_bwd(dm, ya, yb, pg, wa, wb, wo, *, tm, name, comms=()):
    T, D = dm.shape

    def body(dm_ref, ya_ref, yb_ref, pg_ref, wa_ref, wb_ref, wo_ref, dg_ref, dya_ref, dyb_ref, ds_ref, dt_ref):
        dmg = _dg(dm_ref[...], wo_ref[...], NT)
        sa = jax.nn.sigmoid(pg_ref[:, 0:D])
        sb = jax.nn.sigmoid(pg_ref[:, D:2 * D])
        dg_ref[:, 0:D] = (dmg * ya_ref[...].astype(F32) * (sa * (1.0 - sa))).astype(BF16)
        dg_ref[:, D:2 * D] = (dmg * yb_ref[...].astype(F32) * (sb * (1.0 - sb))).astype(BF16)
        dya = (dmg * sa).astype(BF16)
        dyb = (dmg * sb).astype(BF16)
        dya_ref[...] = dya
        dyb_ref[...] = dyb
        ds_ref[...] = _dg(dya, wa_ref[...], NT)
        dt_ref[...] = _dg(dyb, wb_ref[...], NT)

    tile = lambda w: pl.BlockSpec((tm, w), lambda i: (i, 0))
    full = lambda a: pl.BlockSpec(a.shape, lambda i: (0, 0))
    return _call(
        body, name=name, grid=(T // tm,),
        in_specs=[tile(D), tile(D), tile(D), tile(2 * D), full(wa), full(wb), full(wo)],
        out_specs=[tile(2 * D), tile(D), tile(D), tile(D_A), tile(D_B)],
        out_shape=[jax.ShapeDtypeStruct((T, 2 * D), BF16), jax.ShapeDtypeStruct((T, D), BF16),
                   jax.ShapeDtypeStruct((T, D), BF16), jax.ShapeDtypeStruct((T, D_A), F32),
                   jax.ShapeDtypeStruct((T, D_B), F32)],
        args=[dm, ya, yb, pg, wa, wb, wo], comms=comms)


def sgu_bwd(puv, dsgu, gln, bln, ws, bs_t, *, cpb, name, comms=()):
    T = puv.shape[0]
    gd = D_A // N_GROUPS
    tm = cpb * CHUNK

    def body(p_ref, ds_ref, gln_ref, bln_ref, ws_ref, bs_ref, d_ref, dws_ref, dz_ref, dgl_ref, dbl_ref):
        i = pl.program_id(0)
        causal = _causal()
        wf = [jnp.where(causal, ws_ref[g], 0.0) for g in range(N_GROUPS)]
        wm = [w.astype(BF16) for w in wf]
        wt = [w.T.astype(BF16) for w in wf]
        dws = [jnp.zeros((CHUNK, CHUNK), F32) for _ in range(N_GROUPS)]
        dzs = jnp.zeros((CHUNK, D_A), F32)
        dgl = jnp.zeros((1, D_A), F32)
        dbl = jnp.zeros((1, D_A), F32)
        for j in range(cpb):
            rows = slice(j * CHUNK, (j + 1) * CHUNK)
            au = p_ref[rows, 0:D_A]
            av = p_ref[rows, D_A:2 * D_A]
            u, tu = _gelu_parts(au)
            vv, tv = _gelu_parts(av)
            vhat, rstd = _layer_norm_parts(vv)
            vn = (vhat * gln_ref[...] + bln_ref[...]).astype(BF16)
            dsg = ds_ref[rows, :]
            dz = dsg * u
            dzb = dz.astype(BF16)
            zs, dvns = [], []
            for g in range(N_GROUPS):
                cols = slice(g * gd, (g + 1) * gd)
                zs.append(_dot(wm[g], vn[:, cols]) + bs_ref[:, g:g + 1])
                dws[g] = dws[g] + _dg(dzb[:, cols], vn[:, cols], NT)
                dvns.append(_dot(wt[g], dzb[:, cols]))
            z = jnp.concatenate(zs, axis=1)
            dvn = jnp.concatenate(dvns, axis=1)
            d_ref[rows, 0:D_A] = (dsg * z * _dgelu(au, tu)).astype(BF16)
            dvh = dvn * gln_ref[...]
            dvv = rstd * (dvh - jnp.mean(dvh, axis=-1, keepdims=True)
                          - vhat * jnp.mean(dvh * vhat, axis=-1, keepdims=True))
            d_ref[rows, D_A:2 * D_A] = (dvv * _dgelu(av, tv)).astype(BF16)
            dzs = dzs + dz
            dgl = dgl + jnp.sum(dvn * vhat, axis=0, keepdims=True)
            dbl = dbl + jnp.sum(dvn, axis=0, keepdims=True)

        @pl.when(i == 0)
        def _():
            for g in range(N_GROUPS):
                dws_ref[g] = jnp.where(causal, dws[g], 0.0)
            dz_ref[...] = dzs
            dgl_ref[...] = dgl
            dbl_ref[...] = dbl

        @pl.when(i != 0)
        def _():
            for g in range(N_GROUPS):
                dws_ref[g] += jnp.where(causal, dws[g], 0.0)
            dz_ref[...] += dzs
            dgl_ref[...] += dgl
            dbl_ref[...] += dbl

    full = lambda a: pl.BlockSpec(a.shape, lambda i: (0,) * a.ndim)
    acc = lambda s: pl.BlockSpec(s, lambda i: (0,) * len(s))
    return _call(
        body, name=name, grid=(T // tm,),
        in_specs=[pl.BlockSpec((tm, 2 * D_A), lambda i: (i, 0)), pl.BlockSpec((tm, D_A), lambda i: (i, 0)),
                  full(gln), full(bln), full(ws), full(bs_t)],
        out_specs=[pl.BlockSpec((tm, 2 * D_A), lambda i: (i, 0)), acc((N_GROUPS, CHUNK, CHUNK)), acc((CHUNK, D_A)),
                   acc((1, D_A)), acc((1, D_A))],
        out_shape=[jax.ShapeDtypeStruct((T, 2 * D_A), BF16), jax.ShapeDtypeStruct((N_GROUPS, CHUNK, CHUNK), F32),
                   jax.ShapeDtypeStruct((CHUNK, D_A), F32), jax.ShapeDtypeStruct((1, D_A), F32),
                   jax.ShapeDtypeStruct((1, D_A), F32)],
        args=[puv, dsgu, gln, bln, ws, bs_t], comms=comms)


def attn_bwd(pqkv, datt, gq, gk, sinks, *, seq, name, comms=()):
    T = pqkv.shape[0]
    nb = seq // CHUNK

    def body(p_ref, do_ref, gq_ref, gk_ref, sink_ref, d_ref, dgq_ref, dgk_ref, dsk_ref, kpad, vpad, dkacc, dvacc, dgq_s, dsk_s):
        b = pl.program_id(0)
        kpad[0:CHUNK, :] = jnp.zeros((CHUNK, HEAD_DIM), BF16)
        vpad[0:CHUNK, :] = jnp.zeros((CHUNK, HEAD_DIM), BF16)
        dgq_s[...] = jnp.zeros_like(dgq_s)
        dsk_s[...] = jnp.zeros_like(dsk_s)
        dgk = jnp.zeros((1, HEAD_DIM), F32)
        lane = lax.broadcasted_iota(jnp.int32, (1, 128), 1)
        for h in range(N_KV):
            kcols = slice(K_OFF + h * HEAD_DIM, K_OFF + (h + 1) * HEAD_DIM)
            vcols = slice(V_OFF + h * HEAD_DIM, V_OFF + (h + 1) * HEAD_DIM)
            nk, rk = _rms_parts(p_ref[:, kcols])
            kpad[CHUNK:CHUNK + seq, :] = (nk * gk_ref[...]).astype(BF16)
            vpad[CHUNK:CHUNK + seq, :] = p_ref[:, vcols].astype(BF16)
            dkacc[...] = jnp.zeros_like(dkacc)
            dvacc[...] = jnp.zeros_like(dvacc)

            def block(i, carry):
                r0 = pl.multiple_of(i * CHUNK, CHUNK)
                kk = kpad[pl.ds(r0, 2 * CHUNK), :]
                vv = vpad[pl.ds(r0, 2 * CHUNK), :]
                valid = _attn_mask(i)
                for g in range(Q_PER_KV):
                    hq = h * Q_PER_KV + g
                    cols = slice(hq * HEAD_DIM, (hq + 1) * HEAD_DIM)
                    nq, rq = _rms_parts(p_ref[pl.ds(r0, CHUNK), cols])
                    qn = (nq * gq_ref[...]).astype(BF16)
                    probs, psink = _attn_probs(qn, kk, valid, sink_ref[0, hq])
                    do = do_ref[pl.ds(r0, CHUNK), cols].astype(BF16)
                    dp = _dg(do, vv, NT)
                    dr = jnp.sum(probs * dp, axis=-1, keepdims=True)
                    ds = (probs * (dp - dr) * ATT_SCALE).astype(BF16)
                    dsk_s[...] += jnp.where(lane == hq, -jnp.sum(psink * dr), 0.0)
                    dqn = _dot(ds, kk)
                    dkacc[pl.ds(r0, 2 * CHUNK), :] += _dg(ds, qn, TN)
                    dvacc[pl.ds(r0, 2 * CHUNK), :] += _dg(probs.astype(BF16), do, TN)
                    dn = dqn * gq_ref[...]
                    dq = rq * (dn - nq * jnp.mean(dn * nq, axis=-1, keepdims=True))
                    d_ref[pl.ds(r0, CHUNK), cols] = dq.astype(BF16)
                    dgq_s[...] += jnp.sum(dqn * nq, axis=0, keepdims=True)
                return carry

            lax.fori_loop(0, nb, block, 0)
            dkn = dkacc[CHUNK:CHUNK + seq, :]
            dn = dkn * gk_ref[...]
            d_ref[:, kcols] = (rk * (dn - nk * jnp.mean(dn * nk, axis=-1, keepdims=True))).astype(BF16)
            d_ref[:, vcols] = dvacc[CHUNK:CHUNK + seq, :].astype(BF16)
            dgk = dgk + jnp.sum(dkn * nk, axis=0, keepdims=True)

        @pl.when(b == 0)
        def _():
            dgq_ref[...] = dgq_s[...]
            dgk_ref[...] = dgk
            dsk_ref[...] = jnp.broadcast_to(dsk_s[...], dsk_ref.shape)

        @pl.when(b != 0)
        def _():
            dgq_ref[...] += dgq_s[...]
            dgk_ref[...] += dgk
            dsk_ref[...] += jnp.broadcast_to(dsk_s[...], dsk_ref.shape)

    acc = lambda s: pl.BlockSpec(s, lambda b: (0, 0))
    return _call(
        body, name=name, grid=(T // seq,),
        in_specs=[pl.BlockSpec((seq, QKV_W), lambda b: (b, 0)), pl.BlockSpec((seq, D_B), lambda b: (b, 0)),
                  pl.BlockSpec(gq.shape, lambda b: (0, 0)), pl.BlockSpec(gk.shape, lambda b: (0, 0)),
                  pl.BlockSpec(memory_space=pltpu.SMEM)],
        out_specs=[pl.BlockSpec((seq, QKV_W), lambda b: (b, 0)), acc((1, HEAD_DIM)), acc((1, HEAD_DIM)), acc((8, 128))],
        out_shape=[jax.ShapeDtypeStruct((T, QKV_W), BF16), jax.ShapeDtypeStruct((1, HEAD_DIM), F32),
                   jax.ShapeDtypeStruct((1, HEAD_DIM), F32), jax.ShapeDtypeStruct((8, 128), F32)],
        scratch_shapes=[pltpu.VMEM((seq + CHUNK, HEAD_DIM), BF16), pltpu.VMEM((seq + CHUNK, HEAD_DIM), BF16),
                        pltpu.VMEM((seq + CHUNK, HEAD_DIM), F32), pltpu.VMEM((seq + CHUNK, HEAD_DIM), F32),
                        pltpu.VMEM((1, HEAD_DIM), F32), pltpu.VMEM((1, 128), F32)],
        args=[pqkv, datt, gq, gk, sinks], comms=comms)


def ada_fwd(c_all, w, b, *, name):
    nb, D = c_all.shape
    N = w.shape[1]
    tn = N // 3

    def body(c_ref, w_ref, b_ref, o_ref):
        c = c_ref[...]
        cond = (c * jax.nn.sigmoid(c)).astype(BF16)
        o_ref[...] = _dot(cond, w_ref[...].astype(BF16)) + b_ref[...]

    return _call(
        body, name=name, grid=(3,),
        in_specs=[pl.BlockSpec((nb, D), lambda j: (0, 0)), pl.BlockSpec((D, tn), lambda j: (0, j)),
                  pl.BlockSpec((1, tn), lambda j: (0, j))],
        out_specs=[pl.BlockSpec((nb, tn), lambda j: (0, j))], out_shape=[jax.ShapeDtypeStruct((nb, N), F32)],
        args=[c_all, w, b])[0]


def ada_bwd(c_all, dmods_all, dmods_mine, *, name):
    nb, D = c_all.shape
    NA = dmods_all.shape[1]
    N = dmods_mine.shape[1]
    tn = N // 3

    def body(c_ref, da_ref, dm_ref, db_ref, dw_ref):
        c = c_ref[...]
        cond = (c * jax.nn.sigmoid(c)).astype(BF16)
        dw_ref[...] = _dg(cond, dm_ref[...].astype(BF16), TN)
        db_ref[...] = jnp.sum(da_ref[...], axis=0, keepdims=True)

    return _call(
        body, name=name, grid=(3,),
        in_specs=[pl.BlockSpec((nb, D), lambda j: (0, 0)), pl.BlockSpec((nb, NA), lambda j: (0, 0)),
                  pl.BlockSpec((nb, tn), lambda j: (0, j))],
        out_specs=[pl.BlockSpec((1, NA), lambda j: (0, 0)), pl.BlockSpec((D, tn), lambda j: (0, j))],
        out_shape=[jax.ShapeDtypeStruct((1, NA), F32), jax.ShapeDtypeStruct((D, N), F32)],
        args=[c_all, dmods_all, dmods_mine])


def adamw(w, g, m, v, *, name):
    R, C = w.shape
    rb = _row_block(R, max(8, (1 << 18) // C))
    c1 = 1.0 / (1.0 - ADAM_B1 ** ADAM_STEP)
    c2 = 1.0 / (1.0 - ADAM_B2 ** ADAM_STEP)

    def body(w_ref, g_ref, m_ref, v_ref, d_ref, mo_ref, vo_ref):
        gg = g_ref[...]
        mn = ADAM_B1 * m_ref[...] + (1.0 - ADAM_B1) * gg
        vn = ADAM_B2 * v_ref[...] + (1.0 - ADAM_B2) * (gg * gg)
        mo_ref[...] = mn
        vo_ref[...] = vn
        d_ref[...] = -ADAM_LR * ((mn * c1) / (jnp.sqrt(vn * c2) + ADAM_EPS) + ADAM_WD * w_ref[...])

    blk = pl.BlockSpec((rb, C), lambda i: (i, 0))
    shp = jax.ShapeDtypeStruct((R, C), F32)
    return _call(body, name=name, grid=(R // rb,), in_specs=[blk] * 4, out_specs=[blk] * 3, out_shape=[shp] * 3,
                 args=[w, g, m, v])


def _place():
    return lax.axis_index("x"), lax.axis_index("y"), lax.axis_index("c")


def _flip(v, bit):
    return 1 - v if bit else v


def _other_chips(mx, my):
    return [(_flip(mx, k & 2), _flip(my, k & 1)) for k in range(1, N_QUAD)]


def _remote(src, dst, send_sem, recv_sem, peer):
    return pltpu.make_async_remote_copy(src_ref=src, dst_ref=dst, send_sem=send_sem, recv_sem=recv_sem,
                                        device_id=peer, device_id_type=MESH)


def all_gather8(x, *, name, reduce=False):
    r, n = x.shape

    def body(x_ref, o_ref, *rest):
        if reduce:
            buf, send_sems, recv_sems, lsem = rest
        else:
            buf = o_ref
            send_sems, recv_sems, lsem = rest
        mx, my, mc = _place()
        me = 4 * mx + 2 * my + mc
        mine = pltpu.make_async_copy(x_ref, buf.at[me], lsem)
        mine.start()
        sends, recvs = [], []
        for k in range(1, N_DEV):
            peer = (_flip(mx, k & 4), _flip(my, k & 2), _flip(mc, k & 1))
            pidx = 4 * peer[0] + 2 * peer[1] + peer[2]
            sends.append(_remote(x_ref, buf.at[me], send_sems.at[k - 1], recv_sems.at[k - 1], peer))
            recvs.append(_remote(x_ref, buf.at[pidx], send_sems.at[k - 1], recv_sems.at[k - 1], peer))
        for cp in sends:
            cp.start()
        for cp in recvs:
            cp.wait_recv()
        for cp in sends:
            cp.wait_send()
        mine.wait()
        if reduce:
            total = buf[0]
            for d in range(1, N_DEV):
                total = total + buf[d]
            o_ref[...] = total

    scratch = [pltpu.SemaphoreType.DMA((N_DEV - 1,)), pltpu.SemaphoreType.DMA((N_DEV - 1,)), pltpu.SemaphoreType.DMA]
    if reduce:
        scratch = [pltpu.VMEM((N_DEV, r, n), F32)] + scratch
        out_shape = jax.ShapeDtypeStruct((r, n), F32)
    else:
        out_shape = jax.ShapeDtypeStruct((N_DEV, r, n), F32)
    return pl.pallas_call(
        body, name=name, out_shape=out_shape, in_specs=[pl.BlockSpec(memory_space=pltpu.VMEM)],
        out_specs=pl.BlockSpec(memory_space=pltpu.VMEM), scratch_shapes=scratch,
        compiler_params=pltpu.CompilerParams(vmem_limit_bytes=VMEM_LIMIT_BYTES),
    )(x)


def cast_into_stack(w, quad, *, name):
    R, C = w.shape
    rb = _row_block(R, 256)

    def body(q_ref, w_ref, o_ref):
        o_ref[0] = w_ref[...].astype(BF16)

    return pl.pallas_call(
        body, name=name, out_shape=jax.ShapeDtypeStruct((N_QUAD, R, C), BF16),
        grid_spec=pltpu.PrefetchScalarGridSpec(
            num_scalar_prefetch=1, grid=(R // rb,),
            in_specs=[pl.BlockSpec((rb, C), lambda i, q: (i, 0))],
            out_specs=pl.BlockSpec((1, rb, C), lambda i, q: (q[0], i, 0))),
        compiler_params=pltpu.CompilerParams(dimension_semantics=("arbitrary",), vmem_limit_bytes=VMEM_LIMIT_BYTES),
    )(quad, w)


def gather_comm(stacks):
    n = len(stacks)

    def copies(bufs, ss, rs):
        mx, my, mc = _place()
        q = 2 * mx + my
        sibling = (mx, my, 1 - mc)
        ici_send, ici_recv, fwd_send, fwd_recv = [], [], [], []
        for p in range(n):
            hr = stacks[p].shape[1] // 2
            mine, other = pl.ds(mc * hr, hr), pl.ds((1 - mc) * hr, hr)
            for k, chip in enumerate(_other_chips(mx, my)):
                qk = 2 * chip[0] + chip[1]
                peer = (chip[0], chip[1], mc)
                own, landed, theirs = bufs[p].at[q, mine, :], bufs[p].at[qk, mine, :], bufs[p].at[qk, other, :]
                ici_send.append(_remote(own, own, ss.at[6 * p + k], rs.at[6 * p + k], peer))
                ici_recv.append(_remote(own, landed, ss.at[6 * p + k], rs.at[6 * p + k], peer))
                fwd_send.append(_remote(landed, landed, ss.at[6 * p + 3 + k], rs.at[6 * p + 3 + k], sibling))
                fwd_recv.append(_remote(theirs, theirs, ss.at[6 * p + 3 + k], rs.at[6 * p + 3 + k], sibling))
        return ici_send, ici_recv, fwd_send, fwd_recv

    def start(srcs, bufs, lands, ss, rs):
        for cp in copies(bufs, ss, rs)[0]:
            cp.start()

    def finish(srcs, bufs, lands, ss, rs):
        ici_send, ici_recv, fwd_send, fwd_recv = copies(bufs, ss, rs)
        for arrived, onward in zip(ici_recv, fwd_send):
            arrived.wait_recv()
            onward.start()
        for cp in fwd_recv:
            cp.wait_recv()
        for cp in ici_send + fwd_send:
            cp.wait_send()

    return Comm(bufs=stacks, n_sems=6 * n, start=start, finish=finish)


def rs_pair_comm(gs):
    n = len(gs)

    def copies(srcs, lands, ss, rs):
        mx, my, mc = _place()
        out = []
        for p in range(n):
            hr = gs[p].shape[1] // 2
            out.append(_remote(srcs[p].at[:, pl.ds((1 - mc) * hr, hr), :], lands[p], ss.at[p], rs.at[p], (mx, my, 1 - mc)))
        return out

    def start(srcs, bufs, lands, ss, rs):
        for cp in copies(srcs, lands, ss, rs):
            cp.start()

    def finish(srcs, bufs, lands, ss, rs):
        for cp in copies(srcs, lands, ss, rs):
            cp.wait()

    return Comm(srcs=gs, land_shapes=[jax.ShapeDtypeStruct((g.shape[0], g.shape[1] // 2, g.shape[2]), g.dtype) for g in gs],
                n_sems=n, start=start, finish=finish)


def rs_chip_comm(ss_):
    n = len(ss_)

    def copies(srcs, lands, ss, rs):
        mx, my, mc = _place()
        out = []
        for p in range(n):
            for k, chip in enumerate(_other_chips(mx, my)):
                out.append(_remote(srcs[p].at[2 * chip[0] + chip[1]], lands[p].at[k], ss.at[3 * p + k], rs.at[3 * p + k],
                                   (chip[0], chip[1], mc)))
        return out

    def start(srcs, bufs, lands, ss, rs):
        for cp in copies(srcs, lands, ss, rs):
            cp.start()

    def finish(srcs, bufs, lands, ss, rs):
        for cp in copies(srcs, lands, ss, rs):
            cp.wait()

    return Comm(srcs=ss_, land_shapes=[jax.ShapeDtypeStruct((N_QUAD - 1,) + s.shape[1:], s.dtype) for s in ss_],
                n_sems=3 * n, start=start, finish=finish)


def rs_share_comm(fulls):
    n = len(fulls)

    def copies(bufs, ss, rs):
        mx, my, mc = _place()
        send, recv = [], []
        for p in range(n):
            hr = fulls[p].shape[0] // 2
            mine, other = bufs[p].at[pl.ds(mc * hr, hr), :], bufs[p].at[pl.ds((1 - mc) * hr, hr), :]
            send.append(_remote(mine, mine, ss.at[p], rs.at[p], (mx, my, 1 - mc)))
            recv.append(_remote(other, other, ss.at[p], rs.at[p], (mx, my, 1 - mc)))
        return send, recv

    def start(srcs, bufs, lands, ss, rs):
        for cp in copies(bufs, ss, rs)[0]:
            cp.start()

    def finish(srcs, bufs, lands, ss, rs):
        send, recv = copies(bufs, ss, rs)
        for cp in recv:
            cp.wait_recv()
        for cp in send:
            cp.wait_send()

    return Comm(bufs=fulls, n_sems=n, start=start, finish=finish)


def pair_sum(g, recv, mc, *, name):
    nq, R, C = g.shape
    hr = R // 2
    rb = _row_block(hr, 256)
    nb = hr // rb

    def body(s_ref, g_ref, r_ref, o_ref):
        o_ref[...] = (g_ref[...].astype(F32) + r_ref[...].astype(F32)).astype(BF16)

    return pl.pallas_call(
        body, name=name, out_shape=jax.ShapeDtypeStruct((nq, hr, C), BF16),
        grid_spec=pltpu.PrefetchScalarGridSpec(
            num_scalar_prefetch=1, grid=(nq, nb),
            in_specs=[pl.BlockSpec((1, rb, C), lambda q, i, s: (q, s[0] * nb + i, 0)),
                      pl.BlockSpec((1, rb, C), lambda q, i, s: (q, i, 0))],
            out_specs=pl.BlockSpec((1, rb, C), lambda q, i, s: (q, i, 0))),
        compiler_params=pltpu.CompilerParams(dimension_semantics=("arbitrary", "arbitrary"),
                                             vmem_limit_bytes=VMEM_LIMIT_BYTES),
    )(mc, g, recv)


def chip_sum(s, recv, quad_mc, *, name):
    nq, hr, C = s.shape
    rb = _row_block(hr, 256)
    nb = hr // rb

    def body(q_ref, s_ref, r_ref, o_ref):
        total = s_ref[0].astype(F32)
        for k in range(N_QUAD - 1):
            total = total + r_ref[k].astype(F32)
        o_ref[...] = total

    return pl.pallas_call(
        body, name=name, out_shape=jax.ShapeDtypeStruct((2 * hr, C), F32),
        grid_spec=pltpu.PrefetchScalarGridSpec(
            num_scalar_prefetch=1, grid=(nb,),
            in_specs=[pl.BlockSpec((1, rb, C), lambda i, q: (q[0], i, 0)),
                      pl.BlockSpec((N_QUAD - 1, rb, C), lambda i, q: (0, i, 0))],
            out_specs=pl.BlockSpec((rb, C), lambda i, q: (q[1] * nb + i, 0))),
        compiler_params=pltpu.CompilerParams(dimension_semantics=("arbitrary",), vmem_limit_bytes=VMEM_LIMIT_BYTES),
    )(quad_mc, s, recv)


def _stack_cols(w_full):
    K, N = w_full.shape
    return w_full.reshape(K, N_QUAD, N // N_QUAD).transpose(1, 0, 2)


def _unstack_cols(w4):
    nq, K, n = w4.shape
    return w4.transpose(1, 0, 2).reshape(K, nq * n)


def kernel(x, c, w_ada, b_ada, g_norm1, ffn1_w_gate, ffn1_w_up, ffn1_w_down, g_norm2, w_in, g_sgu_ln, b_sgu_ln, w_spatial, b_spatial, g_q, g_k, attn_sinks, w_branch_a, w_branch_b, w_out, g_norm3, ffn2_w_gate, ffn2_w_up, ffn2_w_down, loss_target, m_w_ada, m_b_ada, m_g_norm1, m_ffn1_w_gate, m_ffn1_w_up, m_ffn1_w_down, m_g_norm2, m_w_in, m_g_sgu_ln, m_b_sgu_ln, m_w_spatial, m_b_spatial, m_g_q, m_g_k, m_attn_sinks, m_w_branch_a, m_w_branch_b, m_w_out, m_g_norm3, m_ffn2_w_gate, m_ffn2_w_up, m_ffn2_w_down, v_w_ada, v_b_ada, v_g_norm1, v_ffn1_w_gate, v_ffn1_w_up, v_ffn1_w_down, v_g_norm2, v_w_in, v_g_sgu_ln, v_b_sgu_ln, v_w_spatial, v_b_spatial, v_g_q, v_g_k, v_attn_sinks, v_w_branch_a, v_w_branch_b, v_w_out, v_g_norm3, v_ffn2_w_gate, v_ffn2_w_up, v_ffn2_w_down):
    B, S, D = x.shape
    T = B * S
    n_mod = b_ada.shape[1] // D
    mx, my, mc = _place()
    quad = 2 * mx + my
    mc_arr = jnp.reshape(mc, (1,)).astype(jnp.int32)
    quad_arr = jnp.reshape(quad, (1,)).astype(jnp.int32)
    quad_mc = jnp.stack([quad, mc]).astype(jnp.int32)
    tm = min(512, S)
    tm_small = min(256, S)
    cpb = min(4, S // CHUNK)

    xt = x.reshape(T, D)
    tgt = loss_target.reshape(T, D)

    tr = lambda a: jnp.swapaxes(a, 1, 2)
    stack = lambda w, nm: cast_into_stack(w[0], quad_arr, name=f"stack_{nm}")
    gather1 = gather_comm([stack(tr(ffn1_w_gate), "wg1"), stack(tr(ffn1_w_up), "wu1"), stack(ffn1_w_down, "wd1")])
    run_comms([gather1], name="gather_ffn1")
    wg1, wu1, wd1 = gather1.bufs_out
    gather_mix = gather_comm([stack(tr(w_in), "win"), stack(w_branch_a, "wa"), stack(w_branch_b, "wb"),
                              stack(w_out, "wo")])
    gather3a = gather_comm([stack(tr(ffn2_w_gate), "wg3"), stack(tr(ffn2_w_up), "wu3")])
    gather3b = gather_comm([stack(ffn2_w_down, "wd3")])

    c_all = all_gather8(c, name="gather_cond").reshape(N_DEV * B, D)
    n_ada = w_ada.shape[2]
    b_mine = lax.dynamic_slice(b_ada, (0, quad * n_ada), (1, n_ada))
    mods_part = ada_fwd(c_all, w_ada[0], b_mine, name="ada_fwd")
    mods_parts = all_gather8(mods_part, name="gather_mods")
    mods = jnp.concatenate([mods_parts[2 * j] for j in range(N_QUAD)], axis=1)
    me = 4 * mx + 2 * my + mc
    mods = lax.dynamic_slice(mods, (me * B, 0), (B, n_mod * D))
    sh1, sc1, ga1, sh2, sc2, ga2, sh3, sc3, ga3 = [mods[:, j * D:(j + 1) * D] for j in range(n_mod)]
    bs_t = b_spatial[0].T
    ws = w_spatial[0]

    xn1 = norm_mod_fwd(xt, g_norm1, sc1, sh1, seq=S, tm=tm, name="norm1")
    g1, u1, a1 = ffn_up(xn1, wg1, wu1, tm=tm, name="ffn1_up", comms=[gather_mix])
    win4, wa4, wb4, wo4 = gather_mix.bufs_out
    win = win4.reshape(-1, D)
    w_uv, w_qkv, w_gt = win[0:2 * D_A], win[2 * D_A:2 * D_A + QKV_W], win[2 * D_A + QKV_W:]
    wa, wb = _unstack_cols(wa4), _unstack_cols(wb4)
    wo = wo4.reshape(D, D)
    h1, f1 = ffn_down(a1, wd1, xt, ga1, seq=S, tm=tm, name="ffn1_down")
    xn2 = norm_mod_fwd(h1, g_norm2, sc2, sh2, seq=S, tm=tm, name="norm2")
    puv, pqkv, pgt = proj_fwd(xn2, [w_uv, w_qkv, w_gt], tm=tm_small, name="proj")
    sgu = sgu_fwd(puv, g_sgu_ln, b_sgu_ln, ws, bs_t, cpb=cpb, name="sgu_fwd")
    att = attn_fwd(pqkv, g_q, g_k, attn_sinks, seq=S, name="attn_fwd", comms=[gather3a])
    wg3, wu3 = gather3a.bufs_out
    ya, yb, merged, mm, h2 = mixer_out_fwd(sgu, att, pgt, h1, ga2, wa, wb, wo, seq=S, tm=tm_small, name="mixer_out",
                                           comms=[gather3b])
    (wd3,) = gather3b.bufs_out
    xn3 = norm_mod_fwd(h2, g_norm3, sc3, sh3, seq=S, tm=tm, name="norm3")
    g3, u3, a3 = ffn_up(xn3, wg3, wu3, tm=tm, name="ffn2_up")
    dy, f3, lparts = ffn_down(a3, wd3, h2, ga3, tgt, seq=S, tm=tm, name="ffn2_down_loss")
    loss_part = jnp.sum(lparts[:, 0, 0])

    def sums_of(pair, tag):
        return [pair_sum(g, r, mc_arr, name=f"pair_sum_{tag}_{p}") for p, (g, r) in enumerate(zip(pair.srcs, pair.lands))]

    def halves_of(chip, tag):
        return [chip_sum(s, r, quad_mc, name=f"chip_sum_{tag}_{p}") for p, (s, r) in enumerate(zip(chip.srcs, chip.lands))]

    df3, dga3 = gate_bwd(dy, f3, ga3, 0.5, seq=S, tm=tm, name="ffn2_gate_bwd")
    dg3, du3 = ffn_bwd_act(df3, wd3, g3, u3, tm=tm, name="ffn2_act_bwd")
    (dwd3,) = mm_tn([(a3, df3)], tt=tm, name="ffn2_dwd")
    dwg3, dwu3 = mm_tn([(dg3, xn3), (du3, xn3)], tt=tm, name="ffn2_dwgu")
    pair3 = rs_pair_comm([dwg3, dwu3, dwd3])
    dh2, dsh3, dsc3, dgn3 = dx_norm_bwd([dg3, du3], [wg3, wu3], h2, dy, g_norm3, sc3, seq=S, tm=tm_small, name="ffn2_dx",
                                        comms=[pair3])
    chip3 = rs_chip_comm(sums_of(pair3, "ffn2"))

    dm, dga2 = gate_bwd(dh2, mm, ga2, 1.0, seq=S, tm=tm, name="mixer_gate_bwd")
    dgt, dya, dyb, dsgu, datt = mixer_out_bwd(dm, ya, yb, pgt, wa, wb, wo, tm=tm_small, name="mixer_out_bwd")
    (dwo,) = mm_tn([(merged, dm)], tt=tm, name="mixer_dwo")
    (dwa,) = mm_tn([(sgu, dya)], tt=tm, name="mixer_dwa")
    (dwb,) = mm_tn([(att, dyb)], tt=tm, name="mixer_dwb")
    duv, dws, dzs, dgln, dbln = sgu_bwd(puv, dsgu, g_sgu_ln, b_sgu_ln, ws, bs_t, cpb=cpb, name="sgu_bwd")
    dqkv, dgq, dgk, dsk = attn_bwd(pqkv, datt, g_q, g_k, attn_sinks, seq=S, name="attn_bwd", comms=[chip3])
    share3 = rs_share_comm(halves_of(chip3, "ffn2"))
    dwin_parts = mm_tn([(duv, xn2), (dqkv, xn2), (dgt, xn2)], tt=tm, name="mixer_dwin")
    dwin4 = jnp.concatenate(dwin_parts, axis=0).reshape(win4.shape)
    pair_mix = rs_pair_comm([dwin4, _stack_cols(dwa), _stack_cols(dwb), dwo.reshape(N_QUAD, D // N_QUAD, D)])
    dh1, dsh2, dsc2, dgn2 = dx_norm_bwd([duv, dqkv, dgt], [w_uv, w_qkv, w_gt], h1, dh2, g_norm2, sc2, seq=S,
                                        tm=tm_small, name="mixer_dx", comms=[pair_mix])
    chip_mix = rs_chip_comm(sums_of(pair_mix, "mix"))

    df1, dga1 = gate_bwd(dh1, f1, ga1, 0.5, seq=S, tm=tm, name="ffn1_gate_bwd")
    dg1, du1 = ffn_bwd_act(df1, wd1, g1, u1, tm=tm, name="ffn1_act_bwd", comms=[share3])
    (dwd1,) = mm_tn([(a1, df1)], tt=tm, name="ffn1_dwd")
    dwg1, dwu1 = mm_tn([(dg1, xn1), (du1, xn1)], tt=tm, name="ffn1_dwgu", comms=[chip_mix])
    pair1 = rs_pair_comm([dwg1, dwu1, dwd1])
    run_comms([pair1], name="rs_pair_ffn1")
    chip1 = rs_chip_comm(sums_of(pair1, "ffn1"))
    dx, dsh1, dsc1, dgn1 = dx_norm_bwd([dg1, du1], [wg1, wu1], xt, dh1, g_norm1, sc1, seq=S, tm=tm_small, name="ffn1_dx",
                                       comms=[chip1])
    share_rest = rs_share_comm(halves_of(chip_mix, "mix") + halves_of(chip1, "ffn1"))
    run_comms([share_rest], name="rs_share_rest")
    r_win, r_wa, r_wb, r_wo, r_wg1, r_wu1, r_wd1 = share_rest.bufs_out
    r_wg3, r_wu3, r_wd3 = share3.bufs_out

    dmods = jnp.concatenate([dsh1, dsc1, dga1, dsh2, dsc2, dga2, dsh3, dsc3, dga3], axis=1)
    dmods_all = all_gather8(dmods, name="gather_dmods").reshape(N_DEV * B, n_mod * D)
    dmods_mine = lax.dynamic_slice(dmods_all, (0, quad * n_ada), (N_DEV * B, n_ada))
    db_ada, dw_ada = ada_bwd(c_all, dmods_all, dmods_mine, name="ada_bwd")

    db_s = dzs.reshape(CHUNK, N_GROUPS, D_A // N_GROUPS).sum(axis=2).T.reshape(1, N_GROUPS * CHUNK)
    tail = jnp.concatenate([db_s, dgq, dgk, dsk[0:1, 0:N_Q], loss_part.reshape(1, 1)], axis=1)
    tail = jnp.pad(tail, ((0, 0), (0, (-tail.shape[1]) % D)))
    lnrow = jnp.concatenate([dgln, dbln], axis=1)
    lnrow = jnp.pad(lnrow, ((0, 0), (0, (-lnrow.shape[1]) % D)))
    packed = jnp.concatenate([dgn1, dgn2, dgn3, lnrow.reshape(-1, D), tail.reshape(-1, D), dws.reshape(-1, D)], axis=0)
    n_rows = packed.shape[0]
    packed = jnp.pad(packed, ((0, (-n_rows) % 8), (0, 0)))
    small = all_gather8(packed, name="reduce_small", reduce=True)
    ln_rows = lnrow.size // D
    tail_rows = tail.size // D
    r = 3
    g_gn1, g_gn2, g_gn3 = small[0:1], small[1:2], small[2:3]
    ln_flat = small[r:r + ln_rows].reshape(1, -1)
    r += ln_rows
    tail_flat = small[r:r + tail_rows].reshape(1, -1)
    r += tail_rows
    g_ws = small[r:r + dws.size // D].reshape(w_spatial.shape)
    g_gln, g_bln = ln_flat[:, 0:D_A], ln_flat[:, D_A:2 * D_A]
    o = N_GROUPS * CHUNK
    g_bs = tail_flat[:, 0:o].reshape(b_spatial.shape)
    g_gq, g_gk, g_sk = tail_flat[:, o:o + HEAD_DIM], tail_flat[:, o + HEAD_DIM:o + 2 * HEAD_DIM], tail_flat[:, o + 2 * HEAD_DIM:o + 2 * HEAD_DIM + N_Q]
    loss = tail_flat[0, o + 2 * HEAD_DIM + N_Q]

    transposed = ("ffn1_w_gate", "ffn1_w_up", "w_in", "ffn2_w_gate", "ffn2_w_up")

    def update(name, w, g, m, v):
        if name in transposed:
            w, m, v = tr(w), tr(m), tr(v)
        shape = w.shape
        two_d = lambda a: a.reshape(-1, shape[-1])
        res = adamw(two_d(w), two_d(g), two_d(m), two_d(v), name=f"adamw_{name}")
        res = [a.reshape(shape) for a in [g] + list(res)]
        return [tr(a) for a in res] if name in transposed else res

    names = ["w_ada", "b_ada", "g_norm1", "ffn1_w_gate", "ffn1_w_up", "ffn1_w_down", "g_norm2", "w_in", "g_sgu_ln",
             "b_sgu_ln", "w_spatial", "b_spatial", "g_q", "g_k", "attn_sinks", "w_branch_a", "w_branch_b", "w_out",
             "g_norm3", "ffn2_w_gate", "ffn2_w_up", "ffn2_w_down"]
    weights = dict(zip(names, [w_ada, b_ada, g_norm1, ffn1_w_gate, ffn1_w_up, ffn1_w_down, g_norm2, w_in, g_sgu_ln,
                               b_sgu_ln, w_spatial, b_spatial, g_q, g_k, attn_sinks, w_branch_a, w_branch_b, w_out,
                               g_norm3, ffn2_w_gate, ffn2_w_up, ffn2_w_down]))
    m_in = dict(zip(names, [m_w_ada, m_b_ada, m_g_norm1, m_ffn1_w_gate, m_ffn1_w_up, m_ffn1_w_down, m_g_norm2, m_w_in,
                            m_g_sgu_ln, m_b_sgu_ln, m_w_spatial, m_b_spatial, m_g_q, m_g_k, m_attn_sinks, m_w_branch_a,
                            m_w_branch_b, m_w_out, m_g_norm3, m_ffn2_w_gate, m_ffn2_w_up, m_ffn2_w_down]))
    v_in = dict(zip(names, [v_w_ada, v_b_ada, v_g_norm1, v_ffn1_w_gate, v_ffn1_w_up, v_ffn1_w_down, v_g_norm2, v_w_in,
                            v_g_sgu_ln, v_b_sgu_ln, v_w_spatial, v_b_spatial, v_g_q, v_g_k, v_attn_sinks, v_w_branch_a,
                            v_w_branch_b, v_w_out, v_g_norm3, v_ffn2_w_gate, v_ffn2_w_up, v_ffn2_w_down]))
    grads = {
        "w_ada": dw_ada[None], "b_ada": db_ada, "g_norm1": g_gn1, "g_norm2": g_gn2, "g_norm3": g_gn3,
        "g_sgu_ln": g_gln, "b_sgu_ln": g_bln, "w_spatial": g_ws, "b_spatial": g_bs, "g_q": g_gq, "g_k": g_gk,
        "attn_sinks": g_sk,
        "ffn1_w_gate": r_wg1[None], "ffn1_w_up": r_wu1[None], "ffn1_w_down": r_wd1[None], "w_in": r_win[None],
        "w_branch_a": r_wa[None], "w_branch_b": r_wb[None], "w_out": r_wo[None],
        "ffn2_w_gate": r_wg3[None], "ffn2_w_up": r_wu3[None], "ffn2_w_down": r_wd3[None],
    }

    small_names = ["b_ada", "g_norm1", "g_norm2", "g_norm3", "g_sgu_ln", "b_sgu_ln", "w_spatial", "b_spatial", "g_q",
                   "g_k", "attn_sinks"]

    def pack(d):
        flat = jnp.concatenate([d[nm].reshape(-1) for nm in small_names])
        return jnp.pad(flat, (0, (-flat.size) % (8 * 128))).reshape(-1, 128)

    sd, sm, sv = adamw(pack(weights), pack(grads), pack(m_in), pack(v_in), name="adamw_small")
    delta, new_m, new_v = {}, {}, {}
    off = 0
    for nm in small_names:
        size, shape = weights[nm].size, weights[nm].shape
        delta[nm] = sd.reshape(-1)[off:off + size].reshape(shape)
        new_m[nm] = sm.reshape(-1)[off:off + size].reshape(shape)
        new_v[nm] = sv.reshape(-1)[off:off + size].reshape(shape)
        off += size
    for nm in names:
        if nm not in small_names:
            grads[nm], delta[nm], new_m[nm], new_v[nm] = update(nm, weights[nm], grads[nm], m_in[nm], v_in[nm])
        else:
            grads[nm] = grads[nm].reshape(weights[nm].shape)

    return (loss, dx.reshape(B, S, D), *[grads[nm] for nm in names], *[delta[nm] for nm in names],
            *[new_m[nm] for nm in names], *[new_v[nm] for nm in names])
```

```python
import functools
import math
import operator

import jax
import jax.numpy as jnp
from jax import lax
from jax.experimental import pallas as pl
from jax.experimental.pallas import tpu as pltpu

F32 = jnp.float32
BF16 = jnp.bfloat16
EPS = 1e-6
NEG = -1e30
N_DEV = 8
N_QUAD = 4
D_A = 512
N_GROUPS = 4
CHUNK = 128
HEAD_DIM = 64
N_KV = 2
Q_PER_KV = 4
N_Q = N_KV * Q_PER_KV
D_B = N_Q * HEAD_DIM
QKV_W = D_B + 2 * N_KV * HEAD_DIM
K_OFF = D_B
V_OFF = D_B + N_KV * HEAD_DIM
ATT_SCALE = HEAD_DIM ** -0.5
GELU_K = math.sqrt(2.0 / math.pi)
GELU_C = 0.044715
ADAM_LR, ADAM_B1, ADAM_B2, ADAM_EPS, ADAM_WD, ADAM_STEP = 0.001, 0.9, 0.999, 1e-08, 0.01, 10
VMEM_LIMIT_BYTES = 56 * 1024 * 1024
MESH = pl.DeviceIdType.MESH
NT = (((1,), (1,)), ((), ()))
TN = (((0,), (0,)), ((), ()))
ANY = pl.BlockSpec(memory_space=pl.ANY)


def _dot(a, b):
    return jnp.dot(a, b, preferred_element_type=F32)


def _dg(a, b, dims):
    return lax.dot_general(a, b, dims, preferred_element_type=F32)


def _gelu_parts(x):
    t = jnp.tanh(GELU_K * (x + GELU_C * x * x * x))
    return 0.5 * x * (1.0 + t), t


def _dgelu(x, t):
    return 0.5 * (1.0 + t) + 0.5 * x * (1.0 - t * t) * (GELU_K * (1.0 + 3.0 * GELU_C * x * x))


def _row_block(rows, target):
    b = min(rows, target)
    while rows % b or b % 8:
        b -= 1
    return b


class Comm:
    def __init__(self, *, srcs=(), bufs=(), land_shapes=(), n_sems, start, finish):
        self.srcs, self.bufs, self.land_shapes = list(srcs), list(bufs), list(land_shapes)
        self.n_sems, self.start, self.finish = n_sems, start, finish
        self.bufs_out, self.lands = None, None


def _call(body, *, name, grid, in_specs, out_specs, out_shape, args, scratch_shapes=(), comms=()):
    in_specs, out_specs, out_shape = list(in_specs), list(out_specs), list(out_shape)
    args, scratch = list(args), list(scratch_shapes)
    n_in, n_out, n_scr = len(args), len(out_shape), len(scratch)
    aliases, layout = {}, []
    for cm in comms:
        i0, o0, s0 = len(args), len(out_shape), len(scratch)
        args += cm.srcs + cm.bufs
        in_specs += [ANY] * (len(cm.srcs) + len(cm.bufs))
        out_shape += [jax.ShapeDtypeStruct(b.shape, b.dtype) for b in cm.bufs] + cm.land_shapes
        out_specs += [ANY] * (len(cm.bufs) + len(cm.land_shapes))
        for j in range(len(cm.bufs)):
            aliases[i0 + len(cm.srcs) + j] = o0 + j
        scratch += [pltpu.SemaphoreType.DMA((cm.n_sems,)), pltpu.SemaphoreType.DMA((cm.n_sems,))]
        layout.append((i0, o0, s0))
    n_in_all, n_out_all = len(args), len(out_shape)

    def wrapped(*refs):
        ins, outs, scr = refs[:n_in_all], refs[n_in_all:n_in_all + n_out_all], refs[n_in_all + n_out_all:]
        parts = []
        for cm, (i0, o0, s0) in zip(comms, layout):
            nb = len(cm.bufs)
            parts.append((ins[i0:i0 + len(cm.srcs)], outs[o0:o0 + nb], outs[o0 + nb:o0 + nb + len(cm.land_shapes)],
                          scr[s0], scr[s0 + 1]))
        if comms and grid:
            ids = [pl.program_id(d) for d in range(len(grid))]
            first = functools.reduce(operator.and_, [i == 0 for i in ids])
            last = functools.reduce(operator.and_, [i == g - 1 for i, g in zip(ids, grid)])

            @pl.when(first)
            def _():
                for cm, p in zip(comms, parts):
                    cm.start(*p)
        elif comms:
            for cm, p in zip(comms, parts):
                cm.start(*p)
        if body is not None:
            body(*ins[:n_in], *outs[:n_out], *scr[:n_scr])
        if comms and grid:
            @pl.when(last)
            def _():
                for cm, p in zip(comms, parts):
                    cm.finish(*p)
        elif comms:
            for cm, p in zip(comms, parts):
                cm.finish(*p)

    kwargs = dict(grid=grid) if grid else {}
    sem = ("arbitrary",) * len(grid)
    res = pl.pallas_call(
        wrapped, name=name, in_specs=in_specs, out_specs=out_specs, out_shape=out_shape, scratch_shapes=scratch,
        input_output_aliases=aliases,
        compiler_params=pltpu.CompilerParams(dimension_semantics=sem, vmem_limit_bytes=VMEM_LIMIT_BYTES), **kwargs,
    )(*args)
    for cm, (i0, o0, s0) in zip(comms, layout):
        nb = len(cm.bufs)
        cm.bufs_out = list(res[o0:o0 + nb])
        cm.lands = list(res[o0 + nb:o0 + nb + len(cm.land_shapes)])
    return list(res[:n_out])


def run_comms(comms, *, name):
    _call(None, name=name, grid=(), in_specs=[], out_specs=[], out_shape=[], args=[], comms=comms)


def norm_mod_fwd(h, g, sc, sh, *, seq, tm, name, comms=()):
    T, D = h.shape
    B, tps = T // seq, seq // tm

    def body(h_ref, g_ref, sc_ref, sh_ref, o_ref):
        x = h_ref[...]
        r = lax.rsqrt(jnp.mean(x * x, axis=-1, keepdims=True) + EPS)
        y = x * r * g_ref[...]
        o_ref[...] = (y * (1.0 + sc_ref[0]) + sh_ref[0]).astype(o_ref.dtype)

    per_seq = pl.BlockSpec((1, 1, D), lambda i: (i // tps, 0, 0))
    return _call(
        body, name=name, grid=(T // tm,),
        in_specs=[pl.BlockSpec((tm, D), lambda i: (i, 0)), pl.BlockSpec((1, D), lambda i: (0, 0)), per_seq, per_seq],
        out_specs=[pl.BlockSpec((tm, D), lambda i: (i, 0))], out_shape=[jax.ShapeDtypeStruct((T, D), BF16)],
        args=[h, g, sc.reshape(B, 1, D), sh.reshape(B, 1, D)], comms=comms)[0]


def ffn_up(xn, wg, wu, *, tm, name, comms=()):
    T, D = xn.shape
    nq, fs, _ = wg.shape

    def body(x_ref, wg_ref, wu_ref, g_ref, u_ref, a_ref):
        x = x_ref[...]
        g = _dg(x, wg_ref[0], NT)
        u = _dg(x, wu_ref[0], NT)
        g_ref[0] = g.astype(BF16)
        u_ref[0] = u.astype(BF16)
        a_ref[0] = (g * jax.nn.sigmoid(g) * u).astype(BF16)

    w_spec = pl.BlockSpec((1, fs, D), lambda q, i: (q, 0, 0))
    o_spec = pl.BlockSpec((1, tm, fs), lambda q, i: (q, i, 0))
    o_shape = jax.ShapeDtypeStruct((nq, T, fs), BF16)
    return _call(
        body, name=name, grid=(nq, T // tm),
        in_specs=[pl.BlockSpec((tm, D), lambda q, i: (i, 0)), w_spec, w_spec],
        out_specs=[o_spec, o_spec, o_spec], out_shape=[o_shape, o_shape, o_shape], args=[xn, wg, wu], comms=comms)


def ffn_down(a, wd, hin, ga, target=None, *, seq, tm, name, comms=()):
    nq, T, fs = a.shape
    D = wd.shape[-1]
    B, tps, nt = T // seq, seq // tm, T // tm
    with_loss = target is not None

    def body(*refs):
        if with_loss:
            a_ref, wd_ref, h_ref, ga_ref, t_ref, o_ref, f_ref, l_ref = refs
        else:
            a_ref, wd_ref, h_ref, ga_ref, o_ref, f_ref = refs
        f = _dot(a_ref[0], wd_ref[0])
        for q in range(1, nq):
            f = f + _dot(a_ref[q], wd_ref[q])
        f_ref[...] = f.astype(BF16)
        y = h_ref[...] + (0.5 * ga_ref[0]) * f
        if with_loss:
            e = y - t_ref[...]
            o_ref[...] = e * (1.0 / D)
            l_ref[...] = jnp.full(l_ref.shape, 0.5 * jnp.sum(e * e) * (1.0 / D), F32)
        else:
            o_ref[...] = y

    tile = pl.BlockSpec((tm, D), lambda i: (i, 0))
    in_specs = [pl.BlockSpec((nq, tm, fs), lambda i: (0, i, 0)), pl.BlockSpec((nq, fs, D), lambda i: (0, 0, 0)),
                tile, pl.BlockSpec((1, 1, D), lambda i: (i // tps, 0, 0))]
    out_specs = [tile, tile]
    out_shape = [jax.ShapeDtypeStruct((T, D), F32), jax.ShapeDtypeStruct((T, D), BF16)]
    args = [a, wd, hin, ga.reshape(B, 1, D)]
    if with_loss:
        in_specs.append(tile)
        args.append(target)
        out_specs.append(pl.BlockSpec((1, 8, 128), lambda i: (i, 0, 0)))
        out_shape.append(jax.ShapeDtypeStruct((nt, 8, 128), F32))
    return _call(body, name=name, grid=(nt,), in_specs=in_specs, out_specs=out_specs, out_shape=out_shape, args=args,
                 comms=comms)


def proj_fwd(xn, ws, *, tm, name, comms=()):
    T, D = xn.shape
    n = len(ws)

    def body(*refs):
        x = refs[0][...]
        for j in range(n):
            refs[1 + n + j][...] = _dg(x, refs[1 + j][...], NT)

    return _call(
        body, name=name, grid=(T // tm,),
        in_specs=[pl.BlockSpec((tm, D), lambda i: (i, 0))] + [pl.BlockSpec(w.shape, lambda i: (0, 0)) for w in ws],
        out_specs=[pl.BlockSpec((tm, w.shape[0]), lambda i: (i, 0)) for w in ws],
        out_shape=[jax.ShapeDtypeStruct((T, w.shape[0]), F32) for w in ws], args=[xn, *ws], comms=comms)


def _layer_norm_parts(v):
    mu = jnp.mean(v, axis=-1, keepdims=True)
    d = v - mu
    rstd = lax.rsqrt(jnp.mean(d * d, axis=-1, keepdims=True) + EPS)
    return d * rstd, rstd


def _causal():
    row = lax.broadcasted_iota(jnp.int32, (CHUNK, CHUNK), 0)
    col = lax.broadcasted_iota(jnp.int32, (CHUNK, CHUNK), 1)
    return row >= col


def sgu_fwd(puv, gln, bln, ws, bs_t, *, cpb, name, comms=()):
    T = puv.shape[0]
    gd = D_A // N_GROUPS
    tm = cpb * CHUNK

    def body(p_ref, gln_ref, bln_ref, ws_ref, bs_ref, o_ref):
        causal = _causal()
        wm = [jnp.where(causal, ws_ref[g], 0.0).astype(BF16) for g in range(N_GROUPS)]
        for j in range(cpb):
            rows = slice(j * CHUNK, (j + 1) * CHUNK)
            u, _ = _gelu_parts(p_ref[rows, 0:D_A])
            vv, _ = _gelu_parts(p_ref[rows, D_A:2 * D_A])
            vhat, _ = _layer_norm_parts(vv)
            vn = (vhat * gln_ref[...] + bln_ref[...]).astype(BF16)
            for g in range(N_GROUPS):
                cols = slice(g * gd, (g + 1) * gd)
                z = _dot(wm[g], vn[:, cols]) + bs_ref[:, g:g + 1]
                o_ref[rows, cols] = (u[:, cols] * z).astype(BF16)

    full = lambda a: pl.BlockSpec(a.shape, lambda i: (0,) * a.ndim)
    return _call(
        body, name=name, grid=(T // tm,),
        in_specs=[pl.BlockSpec((tm, 2 * D_A), lambda i: (i, 0)), full(gln), full(bln), full(ws), full(bs_t)],
        out_specs=[pl.BlockSpec((tm, D_A), lambda i: (i, 0))], out_shape=[jax.ShapeDtypeStruct((T, D_A), BF16)],
        args=[puv, gln, bln, ws, bs_t], comms=comms)[0]


def _rms_parts(x):
    r = lax.rsqrt(jnp.mean(x * x, axis=-1, keepdims=True) + EPS)
    return x * r, r


KV_W = Q_PER_KV * HEAD_DIM
KV2 = N_KV * HEAD_DIM
STACK = Q_PER_KV * CHUNK


def _iota(shape, dim):
    return lax.broadcasted_iota(jnp.int32, shape, dim)


def _head_mean_matrix(n):
    return jnp.where((_iota((n, n), 0) >> 6) == (_iota((n, n), 1) >> 6), 1.0 / HEAD_DIM, 0.0).astype(BF16)


def _repeat_matrix(h):
    return jnp.where(_iota((KV2, KV_W), 0) == h * HEAD_DIM + (_iota((KV2, KV_W), 1) & (HEAD_DIM - 1)), 1.0, 0.0).astype(BF16)


def _fold_matrix(h):
    j, l = _iota((KV_W, KV2), 0), _iota((KV_W, KV2), 1)
    return jnp.where(((j & (HEAD_DIM - 1)) == (l & (HEAD_DIM - 1))) & ((l >> 6) == h), 1.0, 0.0).astype(BF16)


def _dot_split(x, m):
    hi = x.astype(BF16)
    lo = (x - hi.astype(F32)).astype(BF16)
    return _dot(hi, m) + _dot(lo, m)


def _head_rms(x, mean_matrix):
    r = lax.rsqrt(_dot_split(x * x, mean_matrix) + EPS)
    return x * r, r


def _stack_heads(x):
    head = _iota(x.shape, 1) >> 6
    return jnp.concatenate([jnp.where(head == g, x, 0.0).astype(BF16) for g in range(Q_PER_KV)], axis=0)


def _unstack_heads(y):
    head = _iota((CHUNK, KV_W), 1) >> 6
    out = jnp.where(head == 0, y[0:CHUNK], 0.0)
    for g in range(1, Q_PER_KV):
        out = out + jnp.where(head == g, y[g * CHUNK:(g + 1) * CHUNK], 0.0)
    return out


def _attn_mask(i):
    qi = _iota((STACK, 2 * CHUNK), 0) & (CHUNK - 1)
    kj = _iota((STACK, 2 * CHUNK), 1)
    diff = qi + CHUNK - kj
    return (diff >= 0) & (diff < CHUNK) & ((kj >= CHUNK) | (i > 0))


def _sink_column(sink_ref, h):
    blk = _iota((STACK, 1), 0) >> 7
    col = jnp.full((STACK, 1), sink_ref[0, h * Q_PER_KV], F32)
    for g in range(1, Q_PER_KV):
        col = jnp.where(blk == g, sink_ref[0, h * Q_PER_KV + g], col)
    return col


def _attn_probs(qs, kk, valid, sink):
    s = _dg(qs, kk, NT) * ATT_SCALE
    s = jnp.where(valid, s, NEG)
    m = jnp.maximum(jnp.max(s, axis=-1, keepdims=True), sink)
    p = jnp.exp(s - m)
    es = jnp.exp(sink - m)
    inv = 1.0 / (jnp.sum(p, axis=-1, keepdims=True) + es)
    return p * inv, es * inv


def _fill_keys(p_ref, gk_ref, krep, vrep, seq):
    mean_k = _head_mean_matrix(KV2)
    reps = [_repeat_matrix(h) for h in range(N_KV)]
    for h in range(N_KV):
        krep[h][0:CHUNK, :] = jnp.zeros((CHUNK, KV_W), BF16)
        vrep[h][0:CHUNK, :] = jnp.zeros((CHUNK, KV_W), BF16)
    rows = min(seq, 4 * CHUNK)

    def piece(j, carry):
        r0 = pl.multiple_of(j * rows, CHUNK)
        nk, _ = _head_rms(p_ref[pl.ds(r0, rows), K_OFF:K_OFF + KV2], mean_k)
        kn = (nk * gk_ref[...]).astype(BF16)
        v = p_ref[pl.ds(r0, rows), V_OFF:V_OFF + KV2].astype(BF16)
        r1 = pl.multiple_of(r0 + CHUNK, CHUNK)
        for h in range(N_KV):
            krep[h][pl.ds(r1, rows), :] = _dot(kn, reps[h]).astype(BF16)
            vrep[h][pl.ds(r1, rows), :] = _dot(v, reps[h]).astype(BF16)
        return carry

    lax.fori_loop(0, seq // rows, piece, 0)


def attn_fwd(pqkv, gq_t, gk_t, sinks, *, seq, name, comms=()):
    T = pqkv.shape[0]
    nb = seq // CHUNK

    def body(p_ref, gq_ref, gk_ref, sink_ref, o_ref, k0, k1, v0, v1):
        krep, vrep = [k0, k1], [v0, v1]
        _fill_keys(p_ref, gk_ref, krep, vrep, seq)
        mean_q = _head_mean_matrix(D_B)

        def block(i, carry):
            r0 = pl.multiple_of(i * CHUNK, CHUNK)
            nq, _ = _head_rms(p_ref[pl.ds(r0, CHUNK), 0:D_B], mean_q)
            qn = nq * gq_ref[...]
            valid = _attn_mask(i)
            for h in range(N_KV):
                qs = _stack_heads(qn[:, h * KV_W:(h + 1) * KV_W])
                kk = krep[h][pl.ds(r0, 2 * CHUNK), :]
                vv = vrep[h][pl.ds(r0, 2 * CHUNK), :]
                probs, _ = _attn_probs(qs, kk, valid, _sink_column(sink_ref, h))
                out = _unstack_heads(_dot(probs.astype(BF16), vv))
                o_ref[pl.ds(r0, CHUNK), h * KV_W:(h + 1) * KV_W] = out.astype(BF16)
            return carry

        lax.fori_loop(0, nb, block, 0)

    return _call(
        body, name=name, grid=(T // seq,),
        in_specs=[pl.BlockSpec((seq, QKV_W), lambda b: (b, 0)), pl.BlockSpec(gq_t.shape, lambda b: (0, 0)),
                  pl.BlockSpec(gk_t.shape, lambda b: (0, 0)), pl.BlockSpec(memory_space=pltpu.SMEM)],
        out_specs=[pl.BlockSpec((seq, D_B), lambda b: (b, 0))], out_shape=[jax.ShapeDtypeStruct((T, D_B), BF16)],
        scratch_shapes=[pltpu.VMEM((seq + CHUNK, KV_W), BF16)] * 4,
        args=[pqkv, gq_t, gk_t, sinks], comms=comms)[0]


def mixer_out_fwd(sgu, att, pg, hin, ga, wa, wb, wo, *, seq, tm, name, comms=()):
    T, D = hin.shape
    B, tps = T // seq, seq // tm

    def body(s_ref, t_ref, pg_ref, h_ref, ga_ref, wa_ref, wb_ref, wo_ref, ya_ref, yb_ref, mg_ref, m_ref, ho_ref):
        ya = _dot(s_ref[...], wa_ref[...])
        yb = _dot(t_ref[...], wb_ref[...])
        merged = jax.nn.sigmoid(pg_ref[:, 0:D]) * ya + jax.nn.sigmoid(pg_ref[:, D:2 * D]) * yb
        mg = merged.astype(BF16)
        m = _dot(mg, wo_ref[...])
        ya_ref[...] = ya.astype(BF16)
        yb_ref[...] = yb.astype(BF16)
        mg_ref[...] = mg
        m_ref[...] = m.astype(BF16)
        ho_ref[...] = h_ref[...] + ga_ref[0] * m

    tile = lambda w: pl.BlockSpec((tm, w), lambda i: (i, 0))
    full = lambda a: pl.BlockSpec(a.shape, lambda i: (0, 0))
    b16 = jax.ShapeDtypeStruct((T, D), BF16)
    return _call(
        body, name=name, grid=(T // tm,),
        in_specs=[tile(D_A), tile(D_B), tile(2 * D), tile(D), pl.BlockSpec((1, 1, D), lambda i: (i // tps, 0, 0)),
                  full(wa), full(wb), full(wo)],
        out_specs=[tile(D)] * 5, out_shape=[b16, b16, b16, b16, jax.ShapeDtypeStruct((T, D), F32)],
        args=[sgu, att, pg, hin, ga.reshape(B, 1, D), wa, wb, wo], comms=comms)


def gate_bwd(dh, f, ga, scale, *, seq, tm, name, comms=()):
    T, D = dh.shape
    B, tps = T // seq, seq // tm

    def body(dh_ref, f_ref, ga_ref, df_ref, dga_ref):
        i = pl.program_id(0)
        d = dh_ref[...]
        df_ref[...] = ((scale * ga_ref[0]) * d).astype(BF16)
        part = scale * jnp.sum(d * f_ref[...].astype(F32), axis=0, keepdims=True)

        @pl.when(i % tps == 0)
        def _():
            dga_ref[0] = part

        @pl.when(i % tps != 0)
        def _():
            dga_ref[0] += part

    tile = pl.BlockSpec((tm, D), lambda i: (i, 0))
    per_seq = pl.BlockSpec((1, 1, D), lambda i: (i // tps, 0, 0))
    df, dga = _call(
        body, name=name, grid=(T // tm,), in_specs=[tile, tile, per_seq], out_specs=[tile, per_seq],
        out_shape=[jax.ShapeDtypeStruct((T, D), BF16), jax.ShapeDtypeStruct((B, 1, D), F32)],
        args=[dh, f, ga.reshape(B, 1, D)], comms=comms)
    return df, dga.reshape(B, D)


def ffn_bwd_act(df, wd, g, u, *, tm, name, comms=()):
    T, D = df.shape
    nq, fs, _ = wd.shape

    def body(df_ref, wd_ref, g_ref, u_ref, dg_ref, du_ref):
        da = _dg(df_ref[...], wd_ref[0], NT)
        gg = g_ref[0].astype(F32)
        sg = jax.nn.sigmoid(gg)
        du_ref[0] = (da * (gg * sg)).astype(BF16)
        dg_ref[0] = (da * u_ref[0].astype(F32) * (sg * (1.0 + gg * (1.0 - sg)))).astype(BF16)

    act = pl.BlockSpec((1, tm, fs), lambda q, i: (q, i, 0))
    o_shape = jax.ShapeDtypeStruct((nq, T, fs), BF16)
    return _call(
        body, name=name, grid=(nq, T // tm),
        in_specs=[pl.BlockSpec((tm, D), lambda q, i: (i, 0)), pl.BlockSpec((1, fs, D), lambda q, i: (q, 0, 0)), act, act],
        out_specs=[act, act], out_shape=[o_shape, o_shape], args=[df, wd, g, u], comms=comms)


def mm_tn(pairs, *, tt, name, comms=()):
    n = len(pairs)
    flat = [a for pair in pairs for a in pair]
    nq = max([a.shape[0] for a in flat if a.ndim == 3] + [1])
    T = flat[0].shape[-2]
    nt = T // tt
    stacked = [x.ndim == 3 or y.ndim == 3 for x, y in pairs]

    def body(*refs):
        in_refs, o_refs, accs = refs[:2 * n], refs[2 * n:3 * n], refs[3 * n:]
        t = pl.program_id(1)
        for j, (x, y) in enumerate(pairs):
            xv = in_refs[2 * j][0] if x.ndim == 3 else in_refs[2 * j][...]
            yv = in_refs[2 * j + 1][0] if y.ndim == 3 else in_refs[2 * j + 1][...]
            part = _dg(xv, yv, TN)

            @pl.when(t == 0)
            def _():
                accs[j][...] = part

            @pl.when(t != 0)
            def _():
                accs[j][...] += part

        @pl.when(t == nt - 1)
        def _():
            for j in range(n):
                if stacked[j]:
                    o_refs[j][0] = accs[j][...].astype(BF16)
                else:
                    o_refs[j][...] = accs[j][...].astype(BF16)

    def spec(a):
        if a.ndim == 3:
            return pl.BlockSpec((1, tt, a.shape[2]), lambda q, t: (q, t, 0))
        return pl.BlockSpec((tt, a.shape[1]), lambda q, t: (t, 0))

    out_specs, out_shape = [], []
    for j, (x, y) in enumerate(pairs):
        K, N = x.shape[-1], y.shape[-1]
        if stacked[j]:
            out_specs.append(pl.BlockSpec((1, K, N), lambda q, t: (q, 0, 0)))
            out_shape.append(jax.ShapeDtypeStruct((nq, K, N), BF16))
        else:
            out_specs.append(pl.BlockSpec((K, N), lambda q, t: (0, 0)))
            out_shape.append(jax.ShapeDtypeStruct((K, N), BF16))
    return _call(
        body, name=name, grid=(nq, nt), in_specs=[spec(a) for a in flat],
        out_specs=out_specs, out_shape=out_shape,
        scratch_shapes=[pltpu.VMEM((x.shape[-1], y.shape[-1]), F32) for x, y in pairs], args=flat, comms=comms)


def dx_norm_bwd(dys, ws, hin, dh_out, g, sc, *, seq, tm, name, comms=()):
    T, D = hin.shape
    B, tps = T // seq, seq // tm
    n = len(dys)

    def body(*refs):
        dy_refs, w_refs = refs[:n], refs[n:2 * n]
        h_ref, dho_ref, g_ref, sc_ref, dhi_ref, dsh_ref, dsc_ref, dgn_ref = refs[2 * n:]
        i = pl.program_id(0)
        dxn = None
        for j in range(n):
            if dys[j].ndim == 3:
                for q in range(dys[j].shape[0]):
                    part = _dot(dy_refs[j][q], w_refs[j][q])
                    dxn = part if dxn is None else dxn + part
            else:
                part = _dot(dy_refs[j][...], w_refs[j][...])
                dxn = part if dxn is None else dxn + part
        x = h_ref[...]
        nx, r = _rms_parts(x)
        gain = g_ref[...]
        one_sc = 1.0 + sc_ref[0]
        dsh = jnp.sum(dxn, axis=0, keepdims=True)
        dsc = jnp.sum(dxn * (nx * gain), axis=0, keepdims=True)
        dgn = jnp.sum(dxn * one_sc * nx, axis=0, keepdims=True)
        dn = dxn * one_sc * gain
        dhi_ref[...] = dho_ref[...] + r * (dn - nx * jnp.mean(dn * nx, axis=-1, keepdims=True))

        @pl.when(i % tps == 0)
        def _():
            dsh_ref[0] = dsh
            dsc_ref[0] = dsc

        @pl.when(i % tps != 0)
        def _():
            dsh_ref[0] += dsh
            dsc_ref[0] += dsc

        @pl.when(i == 0)
        def _():
            dgn_ref[...] = dgn

        @pl.when(i != 0)
        def _():
            dgn_ref[...] += dgn

    def dy_spec(a):
        if a.ndim == 3:
            return pl.BlockSpec((a.shape[0], tm, a.shape[2]), lambda i: (0, i, 0))
        return pl.BlockSpec((tm, a.shape[1]), lambda i: (i, 0))

    tile = pl.BlockSpec((tm, D), lambda i: (i, 0))
    per_seq = pl.BlockSpec((1, 1, D), lambda i: (i // tps, 0, 0))
    row = pl.BlockSpec((1, D), lambda i: (0, 0))
    dhi, dsh, dsc, dgn = _call(
        body, name=name, grid=(T // tm,),
        in_specs=[dy_spec(a) for a in dys] + [pl.BlockSpec(w.shape, lambda i, nd=w.ndim: (0,) * nd) for w in ws]
        + [tile, tile, row, per_seq],
        out_specs=[tile, per_seq, per_seq, row],
        out_shape=[jax.ShapeDtypeStruct((T, D), F32), jax.ShapeDtypeStruct((B, 1, D), F32),
                   jax.ShapeDtypeStruct((B, 1, D), F32), jax.ShapeDtypeStruct((1, D), F32)],
        args=[*dys, *ws, hin, dh_out, g, sc.reshape(B, 1, D)], comms=comms)
    return dhi, dsh.reshape(B, D), dsc.reshape(B, D), dgn


def mixer_out_bwd(dm, ya, yb, pg, wa, wb, wo, *, tm, name, comms=()):
    T, D = dm.shape

    def body(dm_ref, ya_ref, yb_ref, pg_ref, wa_ref, wb_ref, wo_ref, dg_ref, dya_ref, dyb_ref, ds_ref, dt_ref):
        dmg = _dg(dm_ref[...], wo_ref[...], NT)
        sa = jax.nn.sigmoid(pg_ref[:, 0:D])
        sb = jax.nn.sigmoid(pg_ref[:, D:2 * D])
        dg_ref[:, 0:D] = (dmg * ya_ref[...].astype(F32) * (sa * (1.0 - sa))).astype(BF16)
        dg_ref[:, D:2 * D] = (dmg * yb_ref[...].astype(F32) * (sb * (1.0 - sb))).astype(BF16)
        dya = (dmg * sa).astype(BF16)
        dyb = (dmg * sb).astype(BF16)
        dya_ref[...] = dya
        dyb_ref[...] = dyb
        ds_ref[...] = _dg(dya, wa_ref[...], NT)
        dt_ref[...] = _dg(dyb, wb_ref[...], NT)

    tile = lambda w: pl.BlockSpec((tm, w), lambda i: (i, 0))
    full = lambda a: pl.BlockSpec(a.shape, lambda i: (0, 0))
    return _call(
        body, name=name, grid=(T // tm,),
        in_specs=[tile(D), tile(D), tile(D), tile(2 * D), full(wa), full(wb), full(wo)],
        out_specs=[tile(2 * D), tile(D), tile(D), tile(D_A), tile(D_B)],
        out_shape=[jax.ShapeDtypeStruct((T, 2 * D), BF16), jax.ShapeDtypeStruct((T, D), BF16),
                   jax.ShapeDtypeStruct((T, D), BF16), jax.ShapeDtypeStruct((T, D_A), F32),
                   jax.ShapeDtypeStruct((T, D_B), F32)],
        args=[dm, ya, yb, pg, wa, wb, wo], comms=comms)


def sgu_bwd(puv, dsgu, gln, bln, ws, bs_t, *, cpb, name, comms=()):
    T = puv.shape[0]
    gd = D_A // N_GROUPS
    tm = cpb * CHUNK

    def body(p_ref, ds_ref, gln_ref, bln_ref, ws_ref, bs_ref, d_ref, dws_ref, dz_ref, dgl_ref, dbl_ref):
        i = pl.program_id(0)
        causal = _causal()
        wf = [jnp.where(causal, ws_ref[g], 0.0) for g in range(N_GROUPS)]
        wm = [w.astype(BF16) for w in wf]
        wt = [w.T.astype(BF16) for w in wf]
        dws = [jnp.zeros((CHUNK, CHUNK), F32) for _ in range(N_GROUPS)]
        dzs = jnp.zeros((CHUNK, D_A), F32)
        dgl = jnp.zeros((1, D_A), F32)
        dbl = jnp.zeros((1, D_A), F32)
        for j in range(cpb):
            rows = slice(j * CHUNK, (j + 1) * CHUNK)
            au = p_ref[rows, 0:D_A]
            av = p_ref[rows, D_A:2 * D_A]
            u, tu = _gelu_parts(au)
            vv, tv = _gelu_parts(av)
            vhat, rstd = _layer_norm_parts(vv)
            vn = (vhat * gln_ref[...] + bln_ref[...]).astype(BF16)
            dsg = ds_ref[rows, :]
            dz = dsg * u
            dzb = dz.astype(BF16)
            zs, dvns = [], []
            for g in range(N_GROUPS):
                cols = slice(g * gd, (g + 1) * gd)
                zs.append(_dot(wm[g], vn[:, cols]) + bs_ref[:, g:g + 1])
                dws[g] = dws[g] + _dg(dzb[:, cols], vn[:, cols], NT)
                dvns.append(_dot(wt[g], dzb[:, cols]))
            z = jnp.concatenate(zs, axis=1)
            dvn = jnp.concatenate(dvns, axis=1)
            d_ref[rows, 0:D_A] = (dsg * z * _dgelu(au, tu)).astype(BF16)
            dvh = dvn * gln_ref[...]
            dvv = rstd * (dvh - jnp.mean(dvh, axis=-1, keepdims=True)
                          - vhat * jnp.mean(dvh * vhat, axis=-1, keepdims=True))
            d_ref[rows, D_A:2 * D_A] = (dvv * _dgelu(av, tv)).astype(BF16)
            dzs = dzs + dz
            dgl = dgl + jnp.sum(dvn * vhat, axis=0, keepdims=True)
            dbl = dbl + jnp.sum(dvn, axis=0, keepdims=True)

        @pl.when(i == 0)
        def _():
            for g in range(N_GROUPS):
                dws_ref[g] = jnp.where(causal, dws[g], 0.0)
            dz_ref[...] = dzs
            dgl_ref[...] = dgl
            dbl_ref[...] = dbl

        @pl.when(i != 0)
        def _():
            for g in range(N_GROUPS):
                dws_ref[g] += jnp.where(causal, dws[g], 0.0)
            dz_ref[...] += dzs
            dgl_ref[...] += dgl
            dbl_ref[...] += dbl

    full = lambda a: pl.BlockSpec(a.shape, lambda i: (0,) * a.ndim)
    acc = lambda s: pl.BlockSpec(s, lambda i: (0,) * len(s))
    return _call(
        body, name=name, grid=(T // tm,),
        in_specs=[pl.BlockSpec((tm, 2 * D_A), lambda i: (i, 0)), pl.BlockSpec((tm, D_A), lambda i: (i, 0)),
                  full(gln), full(bln), full(ws), full(bs_t)],
        out_specs=[pl.BlockSpec((tm, 2 * D_A), lambda i: (i, 0)), acc((N_GROUPS, CHUNK, CHUNK)), acc((CHUNK, D_A)),
                   acc((1, D_A)), acc((1, D_A))],
        out_shape=[jax.ShapeDtypeStruct((T, 2 * D_A), BF16), jax.ShapeDtypeStruct((N_GROUPS, CHUNK, CHUNK), F32),
                   jax.ShapeDtypeStruct((CHUNK, D_A), F32), jax.ShapeDtypeStruct((1, D_A), F32),
                   jax.ShapeDtypeStruct((1, D_A), F32)],
        args=[puv, dsgu, gln, bln, ws, bs_t], comms=comms)


def attn_bwd(pqkv, datt, gq_t, gk_t, sinks, *, seq, name, comms=()):
    T = pqkv.shape[0]
    nb = seq // CHUNK

    def body(p_ref, do_ref, gq_ref, gk_ref, sink_ref, d_ref, dgq_ref, dgk_ref, dsk_ref,
             k0, k1, v0, v1, dk0, dk1, dv0, dv1, dgq_s, dsk_s):
        b = pl.program_id(0)
        krep, vrep, dkacc, dvacc = [k0, k1], [v0, v1], [dk0, dk1], [dv0, dv1]
        _fill_keys(p_ref, gk_ref, krep, vrep, seq)
        for h in range(N_KV):
            dkacc[h][...] = jnp.zeros_like(dkacc[h])
            dvacc[h][...] = jnp.zeros_like(dvacc[h])
        dgq_s[...] = jnp.zeros_like(dgq_s)
        dsk_s[...] = jnp.zeros_like(dsk_s)
        mean_q = _head_mean_matrix(D_B)
        lane = _iota((1, 128), 1)

        def block(i, carry):
            r0 = pl.multiple_of(i * CHUNK, CHUNK)
            nq, rq = _head_rms(p_ref[pl.ds(r0, CHUNK), 0:D_B], mean_q)
            qn = nq * gq_ref[...]
            valid = _attn_mask(i)
            dqn_parts = []
            for h in range(N_KV):
                cols = slice(h * KV_W, (h + 1) * KV_W)
                qs = _stack_heads(qn[:, cols])
                kk = krep[h][pl.ds(r0, 2 * CHUNK), :]
                vv = vrep[h][pl.ds(r0, 2 * CHUNK), :]
                probs, psink = _attn_probs(qs, kk, valid, _sink_column(sink_ref, h))
                dos = _stack_heads(do_ref[pl.ds(r0, CHUNK), cols])
                dp = _dg(dos, vv, NT)
                dr = jnp.sum(probs * dp, axis=-1, keepdims=True)
                ds = (probs * (dp - dr) * ATT_SCALE).astype(BF16)
                dsink = psink * dr
                for g in range(Q_PER_KV):
                    dsk_s[...] += jnp.where(lane == h * Q_PER_KV + g, -jnp.sum(dsink[g * CHUNK:(g + 1) * CHUNK]), 0.0)
                dqn_parts.append(_unstack_heads(_dot(ds, kk)))
                dkacc[h][pl.ds(r0, 2 * CHUNK), :] += _dg(ds, qs, TN)
                dvacc[h][pl.ds(r0, 2 * CHUNK), :] += _dg(probs.astype(BF16), dos, TN)
            dqn = jnp.concatenate(dqn_parts, axis=1)
            dn = dqn * gq_ref[...]
            d_ref[pl.ds(r0, CHUNK), 0:D_B] = (rq * (dn - nq * _dot_split(dn * nq, mean_q))).astype(BF16)
            dgq_s[...] += jnp.sum(dqn * nq, axis=0, keepdims=True)
            return carry

        lax.fori_loop(0, nb, block, 0)

        mean_k = _head_mean_matrix(KV2)
        folds = [_fold_matrix(h) for h in range(N_KV)]
        rows = min(seq, 4 * CHUNK)

        def piece(j, dgk):
            r0 = pl.multiple_of(j * rows, CHUNK)
            r1 = pl.multiple_of(r0 + CHUNK, CHUNK)
            dkn = _dot_split(dkacc[0][pl.ds(r1, rows), :], folds[0]) + _dot_split(dkacc[1][pl.ds(r1, rows), :], folds[1])
            dv = _dot_split(dvacc[0][pl.ds(r1, rows), :], folds[0]) + _dot_split(dvacc[1][pl.ds(r1, rows), :], folds[1])
            nk, rk = _head_rms(p_ref[pl.ds(r0, rows), K_OFF:K_OFF + KV2], mean_k)
            dn = dkn * gk_ref[...]
            d_ref[pl.ds(r0, rows), K_OFF:K_OFF + KV2] = (rk * (dn - nk * _dot_split(dn * nk, mean_k))).astype(BF16)
            d_ref[pl.ds(r0, rows), V_OFF:V_OFF + KV2] = dv.astype(BF16)
            return dgk + jnp.sum(dkn * nk, axis=0, keepdims=True)

        dgk = lax.fori_loop(0, seq // rows, piece, jnp.zeros((1, KV2), F32))

        @pl.when(b == 0)
        def _():
            dgq_ref[...] = dgq_s[...]
            dgk_ref[...] = dgk
            dsk_ref[...] = jnp.broadcast_to(dsk_s[...], dsk_ref.shape)

        @pl.when(b != 0)
        def _():
            dgq_ref[...] += dgq_s[...]
            dgk_ref[...] += dgk
            dsk_ref[...] += jnp.broadcast_to(dsk_s[...], dsk_ref.shape)

    acc = lambda s: pl.BlockSpec(s, lambda b: (0, 0))
    return _call(
        body, name=name, grid=(T // seq,),
        in_specs=[pl.BlockSpec((seq, QKV_W), lambda b: (b, 0)), pl.BlockSpec((seq, D_B), lambda b: (b, 0)),
                  pl.BlockSpec(gq_t.shape, lambda b: (0, 0)), pl.BlockSpec(gk_t.shape, lambda b: (0, 0)),
                  pl.BlockSpec(memory_space=pltpu.SMEM)],
        out_specs=[pl.BlockSpec((seq, QKV_W), lambda b: (b, 0)), acc((1, D_B)), acc((1, KV2)), acc((8, 128))],
        out_shape=[jax.ShapeDtypeStruct((T, QKV_W), BF16), jax.ShapeDtypeStruct((1, D_B), F32),
                   jax.ShapeDtypeStruct((1, KV2), F32), jax.ShapeDtypeStruct((8, 128), F32)],
        scratch_shapes=[pltpu.VMEM((seq + CHUNK, KV_W), BF16)] * 4 + [pltpu.VMEM((seq + CHUNK, KV_W), F32)] * 4
        + [pltpu.VMEM((1, D_B), F32), pltpu.VMEM((1, 128), F32)],
        args=[pqkv, datt, gq_t, gk_t, sinks], comms=comms)


def ada_fwd(c_all, w, b, *, name):
    nb, D = c_all.shape
    N = w.shape[1]
    tn = N // 3

    def body(c_ref, w_ref, b_ref, o_ref):
        c = c_ref[...]
        cond = (c * jax.nn.sigmoid(c)).astype(BF16)
        o_ref[...] = _dot(cond, w_ref[...].astype(BF16)) + b_ref[...]

    return _call(
        body, name=name, grid=(3,),
        in_specs=[pl.BlockSpec((nb, D), lambda j: (0, 0)), pl.BlockSpec((D, tn), lambda j: (0, j)),
                  pl.BlockSpec((1, tn), lambda j: (0, j))],
        out_specs=[pl.BlockSpec((nb, tn), lambda j: (0, j))], out_shape=[jax.ShapeDtypeStruct((nb, N), F32)],
        args=[c_all, w, b])[0]


def ada_bwd(c_all, dmods_all, dmods_mine, *, name):
    nb, D = c_all.shape
    NA = dmods_all.shape[1]
    N = dmods_mine.shape[1]
    tn = N // 3

    def body(c_ref, da_ref, dm_ref, db_ref, dw_ref):
        c = c_ref[...]
        cond = (c * jax.nn.sigmoid(c)).astype(BF16)
        dw_ref[...] = _dg(cond, dm_ref[...].astype(BF16), TN)
        db_ref[...] = jnp.sum(da_ref[...], axis=0, keepdims=True)

    return _call(
        body, name=name, grid=(3,),
        in_specs=[pl.BlockSpec((nb, D), lambda j: (0, 0)), pl.BlockSpec((nb, NA), lambda j: (0, 0)),
                  pl.BlockSpec((nb, tn), lambda j: (0, j))],
        out_specs=[pl.BlockSpec((1, NA), lambda j: (0, 0)), pl.BlockSpec((D, tn), lambda j: (0, j))],
        out_shape=[jax.ShapeDtypeStruct((1, NA), F32), jax.ShapeDtypeStruct((D, N), F32)],
        args=[c_all, dmods_all, dmods_mine])


def adamw(w, g, m, v, *, name):
    R, C = w.shape
    rb = _row_block(R, max(8, (1 << 18) // C))
    c1 = 1.0 / (1.0 - ADAM_B1 ** ADAM_STEP)
    c2 = 1.0 / (1.0 - ADAM_B2 ** ADAM_STEP)

    def body(w_ref, g_ref, m_ref, v_ref, d_ref, mo_ref, vo_ref):
        gg = g_ref[...]
        mn = ADAM_B1 * m_ref[...] + (1.0 - ADAM_B1) * gg
        vn = ADAM_B2 * v_ref[...] + (1.0 - ADAM_B2) * (gg * gg)
        mo_ref[...] = mn
        vo_ref[...] = vn
        d_ref[...] = -ADAM_LR * ((mn * c1) / (jnp.sqrt(vn * c2) + ADAM_EPS) + ADAM_WD * w_ref[...])

    blk = pl.BlockSpec((rb, C), lambda i: (i, 0))
    shp = jax.ShapeDtypeStruct((R, C), F32)
    return _call(body, name=name, grid=(R // rb,), in_specs=[blk] * 4, out_specs=[blk] * 3, out_shape=[shp] * 3,
                 args=[w, g, m, v])


def _place():
    return lax.axis_index("x"), lax.axis_index("y"), lax.axis_index("c")


def _flip(v, bit):
    return 1 - v if bit else v


def _other_chips(mx, my):
    return [(_flip(mx, k & 2), _flip(my, k & 1)) for k in range(1, N_QUAD)]


def _remote(src, dst, send_sem, recv_sem, peer):
    return pltpu.make_async_remote_copy(src_ref=src, dst_ref=dst, send_sem=send_sem, recv_sem=recv_sem,
                                        device_id=peer, device_id_type=MESH)


def all_gather8(x, *, name, reduce=False):
    r, n = x.shape

    def body(x_ref, o_ref, *rest):
        if reduce:
            buf, send_sems, recv_sems, lsem = rest
        else:
            buf = o_ref
            send_sems, recv_sems, lsem = rest
        mx, my, mc = _place()
        me = 4 * mx + 2 * my + mc
        mine = pltpu.make_async_copy(x_ref, buf.at[me], lsem)
        mine.start()
        sends, recvs = [], []
        for k in range(1, N_DEV):
            peer = (_flip(mx, k & 4), _flip(my, k & 2), _flip(mc, k & 1))
            pidx = 4 * peer[0] + 2 * peer[1] + peer[2]
            sends.append(_remote(x_ref, buf.at[me], send_sems.at[k - 1], recv_sems.at[k - 1], peer))
            recvs.append(_remote(x_ref, buf.at[pidx], send_sems.at[k - 1], recv_sems.at[k - 1], peer))
        for cp in sends:
            cp.start()
        for cp in recvs:
            cp.wait_recv()
        for cp in sends:
            cp.wait_send()
        mine.wait()
        if reduce:
            total = buf[0]
            for d in range(1, N_DEV):
                total = total + buf[d]
            o_ref[...] = total

    scratch = [pltpu.SemaphoreType.DMA((N_DEV - 1,)), pltpu.SemaphoreType.DMA((N_DEV - 1,)), pltpu.SemaphoreType.DMA]
    if reduce:
        scratch = [pltpu.VMEM((N_DEV, r, n), F32)] + scratch
        out_shape = jax.ShapeDtypeStruct((r, n), F32)
    else:
        out_shape = jax.ShapeDtypeStruct((N_DEV, r, n), F32)
    return pl.pallas_call(
        body, name=name, out_shape=out_shape, in_specs=[pl.BlockSpec(memory_space=pltpu.VMEM)],
        out_specs=pl.BlockSpec(memory_space=pltpu.VMEM), scratch_shapes=scratch,
        compiler_params=pltpu.CompilerParams(vmem_limit_bytes=VMEM_LIMIT_BYTES),
    )(x)


def cast_into_stack(w, quad, *, name):
    R, C = w.shape
    rb = _row_block(R, 256)

    def body(q_ref, w_ref, o_ref):
        o_ref[0] = w_ref[...].astype(BF16)

    return pl.pallas_call(
        body, name=name, out_shape=jax.ShapeDtypeStruct((N_QUAD, R, C), BF16),
        grid_spec=pltpu.PrefetchScalarGridSpec(
            num_scalar_prefetch=1, grid=(R // rb,),
            in_specs=[pl.BlockSpec((rb, C), lambda i, q: (i, 0))],
            out_specs=pl.BlockSpec((1, rb, C), lambda i, q: (q[0], i, 0))),
        compiler_params=pltpu.CompilerParams(dimension_semantics=("arbitrary",), vmem_limit_bytes=VMEM_LIMIT_BYTES),
    )(quad, w)


def gather_comm(stacks):
    n = len(stacks)

    def copies(bufs, ss, rs):
        mx, my, mc = _place()
        q = 2 * mx + my
        sibling = (mx, my, 1 - mc)
        ici_send, ici_recv, fwd_send, fwd_recv = [], [], [], []
        for p in range(n):
            hr = stacks[p].shape[1] // 2
            mine, other = pl.ds(mc * hr, hr), pl.ds((1 - mc) * hr, hr)
            for k, chip in enumerate(_other_chips(mx, my)):
                qk = 2 * chip[0] + chip[1]
                peer = (chip[0], chip[1], mc)
                own, landed, theirs = bufs[p].at[q, mine, :], bufs[p].at[qk, mine, :], bufs[p].at[qk, other, :]
                ici_send.append(_remote(own, own, ss.at[6 * p + k], rs.at[6 * p + k], peer))
                ici_recv.append(_remote(own, landed, ss.at[6 * p + k], rs.at[6 * p + k], peer))
                fwd_send.append(_remote(landed, landed, ss.at[6 * p + 3 + k], rs.at[6 * p + 3 + k], sibling))
                fwd_recv.append(_remote(theirs, theirs, ss.at[6 * p + 3 + k], rs.at[6 * p + 3 + k], sibling))
        return ici_send, ici_recv, fwd_send, fwd_recv

    def start(srcs, bufs, lands, ss, rs):
        for cp in copies(bufs, ss, rs)[0]:
            cp.start()

    def finish(srcs, bufs, lands, ss, rs):
        ici_send, ici_recv, fwd_send, fwd_recv = copies(bufs, ss, rs)
        for arrived, onward in zip(ici_recv, fwd_send):
            arrived.wait_recv()
            onward.start()
        for cp in fwd_recv:
            cp.wait_recv()
        for cp in ici_send + fwd_send:
            cp.wait_send()

    return Comm(bufs=stacks, n_sems=6 * n, start=start, finish=finish)


def rs_pair_comm(gs):
    n = len(gs)

    def copies(srcs, lands, ss, rs):
        mx, my, mc = _place()
        out = []
        for p in range(n):
            hr = gs[p].shape[1] // 2
            out.append(_remote(srcs[p].at[:, pl.ds((1 - mc) * hr, hr), :], lands[p], ss.at[p], rs.at[p], (mx, my, 1 - mc)))
        return out

    def start(srcs, bufs, lands, ss, rs):
        for cp in copies(srcs, lands, ss, rs):
            cp.start()

    def finish(srcs, bufs, lands, ss, rs):
        for cp in copies(srcs, lands, ss, rs):
            cp.wait()

    return Comm(srcs=gs, land_shapes=[jax.ShapeDtypeStruct((g.shape[0], g.shape[1] // 2, g.shape[2]), g.dtype) for g in gs],
                n_sems=n, start=start, finish=finish)


def rs_chip_comm(ss_):
    n = len(ss_)

    def copies(srcs, lands, ss, rs):
        mx, my, mc = _place()
        out = []
        for p in range(n):
            for k, chip in enumerate(_other_chips(mx, my)):
                out.append(_remote(srcs[p].at[2 * chip[0] + chip[1]], lands[p].at[k], ss.at[3 * p + k], rs.at[3 * p + k],
                                   (chip[0], chip[1], mc)))
        return out

    def start(srcs, bufs, lands, ss, rs):
        for cp in copies(srcs, lands, ss, rs):
            cp.start()

    def finish(srcs, bufs, lands, ss, rs):
        for cp in copies(srcs, lands, ss, rs):
            cp.wait()

    return Comm(srcs=ss_, land_shapes=[jax.ShapeDtypeStruct((N_QUAD - 1,) + s.shape[1:], s.dtype) for s in ss_],
                n_sems=3 * n, start=start, finish=finish)


def rs_share_comm(fulls):
    n = len(fulls)

    def copies(bufs, ss, rs):
        mx, my, mc = _place()
        send, recv = [], []
        for p in range(n):
            hr = fulls[p].shape[0] // 2
            mine, other = bufs[p].at[pl.ds(mc * hr, hr), :], bufs[p].at[pl.ds((1 - mc) * hr, hr), :]
            send.append(_remote(mine, mine, ss.at[p], rs.at[p], (mx, my, 1 - mc)))
            recv.append(_remote(other, other, ss.at[p], rs.at[p], (mx, my, 1 - mc)))
        return send, recv

    def start(srcs, bufs, lands, ss, rs):
        for cp in copies(bufs, ss, rs)[0]:
            cp.start()

    def finish(srcs, bufs, lands, ss, rs):
        send, recv = copies(bufs, ss, rs)
        for cp in recv:
            cp.wait_recv()
        for cp in send:
            cp.wait_send()

    return Comm(bufs=fulls, n_sems=n, start=start, finish=finish)


def pair_sum(g, recv, mc, *, name):
    nq, R, C = g.shape
    hr = R // 2
    rb = _row_block(hr, 256)
    nb = hr // rb

    def body(s_ref, g_ref, r_ref, o_ref):
        o_ref[...] = (g_ref[...].astype(F32) + r_ref[...].astype(F32)).astype(BF16)

    return pl.pallas_call(
        body, name=name, out_shape=jax.ShapeDtypeStruct((nq, hr, C), BF16),
        grid_spec=pltpu.PrefetchScalarGridSpec(
            num_scalar_prefetch=1, grid=(nq, nb),
            in_specs=[pl.BlockSpec((1, rb, C), lambda q, i, s: (q, s[0] * nb + i, 0)),
                      pl.BlockSpec((1, rb, C), lambda q, i, s: (q, i, 0))],
            out_specs=pl.BlockSpec((1, rb, C), lambda q, i, s: (q, i, 0))),
        compiler_params=pltpu.CompilerParams(dimension_semantics=("arbitrary", "arbitrary"),
                                             vmem_limit_bytes=VMEM_LIMIT_BYTES),
    )(mc, g, recv)


def chip_sum(s, recv, quad_mc, *, name):
    nq, hr, C = s.shape
    rb = _row_block(hr, 256)
    nb = hr // rb

    def body(q_ref, s_ref, r_ref, o_ref):
        total = s_ref[0].astype(F32)
        for k in range(N_QUAD - 1):
            total = total + r_ref[k].astype(F32)
        o_ref[...] = total

    return pl.pallas_call(
        body, name=name, out_shape=jax.ShapeDtypeStruct((2 * hr, C), F32),
        grid_spec=pltpu.PrefetchScalarGridSpec(
            num_scalar_prefetch=1, grid=(nb,),
            in_specs=[pl.BlockSpec((1, rb, C), lambda i, q: (q[0], i, 0)),
                      pl.BlockSpec((N_QUAD - 1, rb, C), lambda i, q: (0, i, 0))],
            out_specs=pl.BlockSpec((rb, C), lambda i, q: (q[1] * nb + i, 0))),
        compiler_params=pltpu.CompilerParams(dimension_semantics=("arbitrary",), vmem_limit_bytes=VMEM_LIMIT_BYTES),
    )(quad_mc, s, recv)


def _stack_cols(w_full):
    K, N = w_full.shape
    return w_full.reshape(K, N_QUAD, N // N_QUAD).transpose(1, 0, 2)


def _unstack_cols(w4):
    nq, K, n = w4.shape
    return w4.transpose(1, 0, 2).reshape(K, nq * n)


def kernel(x, c, w_ada, b_ada, g_norm1, ffn1_w_gate, ffn1_w_up, ffn1_w_down, g_norm2, w_in, g_sgu_ln, b_sgu_ln, w_spatial, b_spatial, g_q, g_k, attn_sinks, w_branch_a, w_branch_b, w_out, g_norm3, ffn2_w_gate, ffn2_w_up, ffn2_w_down, loss_target, m_w_ada, m_b_ada, m_g_norm1, m_ffn1_w_gate, m_ffn1_w_up, m_ffn1_w_down, m_g_norm2, m_w_in, m_g_sgu_ln, m_b_sgu_ln, m_w_spatial, m_b_spatial, m_g_q, m_g_k, m_attn_sinks, m_w_branch_a, m_w_branch_b, m_w_out, m_g_norm3, m_ffn2_w_gate, m_ffn2_w_up, m_ffn2_w_down, v_w_ada, v_b_ada, v_g_norm1, v_ffn1_w_gate, v_ffn1_w_up, v_ffn1_w_down, v_g_norm2, v_w_in, v_g_sgu_ln, v_b_sgu_ln, v_w_spatial, v_b_spatial, v_g_q, v_g_k, v_attn_sinks, v_w_branch_a, v_w_branch_b, v_w_out, v_g_norm3, v_ffn2_w_gate, v_ffn2_w_up, v_ffn2_w_down):
    B, S, D = x.shape
    T = B * S
    n_mod = b_ada.shape[1] // D
    mx, my, mc = _place()
    quad = 2 * mx + my
    mc_arr = jnp.reshape(mc, (1,)).astype(jnp.int32)
    quad_arr = jnp.reshape(quad, (1,)).astype(jnp.int32)
    quad_mc = jnp.stack([quad, mc]).astype(jnp.int32)
    tm = min(512, S)
    tm_small = min(256, S)
    cpb = min(4, S // CHUNK)

    xt = x.reshape(T, D)
    tgt = loss_target.reshape(T, D)

    tr = lambda a: jnp.swapaxes(a, 1, 2)
    stack = lambda w, nm: cast_into_stack(w[0], quad_arr, name=f"stack_{nm}")
    gather1 = gather_comm([stack(tr(ffn1_w_gate), "wg1"), stack(tr(ffn1_w_up), "wu1"), stack(ffn1_w_down, "wd1")])
    run_comms([gather1], name="gather_ffn1")
    wg1, wu1, wd1 = gather1.bufs_out
    gather_mix = gather_comm([stack(tr(w_in), "win"), stack(w_branch_a, "wa"), stack(w_branch_b, "wb"),
                              stack(w_out, "wo")])
    gather3a = gather_comm([stack(tr(ffn2_w_gate), "wg3"), stack(tr(ffn2_w_up), "wu3")])
    gather3b = gather_comm([stack(ffn2_w_down, "wd3")])

    c_all = all_gather8(c, name="gather_cond").reshape(N_DEV * B, D)
    n_ada = w_ada.shape[2]
    b_mine = lax.dynamic_slice(b_ada, (0, quad * n_ada), (1, n_ada))
    mods_part = ada_fwd(c_all, w_ada[0], b_mine, name="ada_fwd")
    mods_parts = all_gather8(mods_part, name="gather_mods")
    mods = jnp.concatenate([mods_parts[2 * j] for j in range(N_QUAD)], axis=1)
    me = 4 * mx + 2 * my + mc
    mods = lax.dynamic_slice(mods, (me * B, 0), (B, n_mod * D))
    sh1, sc1, ga1, sh2, sc2, ga2, sh3, sc3, ga3 = [mods[:, j * D:(j + 1) * D] for j in range(n_mod)]
    bs_t = b_spatial[0].T
    ws = w_spatial[0]

    xn1 = norm_mod_fwd(xt, g_norm1, sc1, sh1, seq=S, tm=tm, name="norm1")
    g1, u1, a1 = ffn_up(xn1, wg1, wu1, tm=tm, name="ffn1_up", comms=[gather_mix])
    win4, wa4, wb4, wo4 = gather_mix.bufs_out
    win = win4.reshape(-1, D)
    w_uv, w_qkv, w_gt = win[0:2 * D_A], win[2 * D_A:2 * D_A + QKV_W], win[2 * D_A + QKV_W:]
    wa, wb = _unstack_cols(wa4), _unstack_cols(wb4)
    wo = wo4.reshape(D, D)
    h1, f1 = ffn_down(a1, wd1, xt, ga1, seq=S, tm=tm, name="ffn1_down")
    xn2 = norm_mod_fwd(h1, g_norm2, sc2, sh2, seq=S, tm=tm, name="norm2")
    puv, pqkv, pgt = proj_fwd(xn2, [w_uv, w_qkv, w_gt], tm=tm_small, name="proj")
    sgu = sgu_fwd(puv, g_sgu_ln, b_sgu_ln, ws, bs_t, cpb=cpb, name="sgu_fwd")
    gq_t, gk_t = jnp.tile(g_q, (1, N_Q)), jnp.tile(g_k, (1, N_KV))
    att = attn_fwd(pqkv, gq_t, gk_t, attn_sinks, seq=S, name="attn_fwd", comms=[gather3a])
    wg3, wu3 = gather3a.bufs_out
    ya, yb, merged, mm, h2 = mixer_out_fwd(sgu, att, pgt, h1, ga2, wa, wb, wo, seq=S, tm=tm_small, name="mixer_out",
                                           comms=[gather3b])
    (wd3,) = gather3b.bufs_out
    xn3 = norm_mod_fwd(h2, g_norm3, sc3, sh3, seq=S, tm=tm, name="norm3")
    g3, u3, a3 = ffn_up(xn3, wg3, wu3, tm=tm, name="ffn2_up")
    dy, f3, lparts = ffn_down(a3, wd3, h2, ga3, tgt, seq=S, tm=tm, name="ffn2_down_loss")
    loss_part = jnp.sum(lparts[:, 0, 0])

    def sums_of(pair, tag):
        return [pair_sum(g, r, mc_arr, name=f"pair_sum_{tag}_{p}") for p, (g, r) in enumerate(zip(pair.srcs, pair.lands))]

    def halves_of(chip, tag):
        return [chip_sum(s, r, quad_mc, name=f"chip_sum_{tag}_{p}") for p, (s, r) in enumerate(zip(chip.srcs, chip.lands))]

    df3, dga3 = gate_bwd(dy, f3, ga3, 0.5, seq=S, tm=tm, name="ffn2_gate_bwd")
    dg3, du3 = ffn_bwd_act(df3, wd3, g3, u3, tm=tm, name="ffn2_act_bwd")
    (dwd3,) = mm_tn([(a3, df3)], tt=tm, name="ffn2_dwd")
    dwg3, dwu3 = mm_tn([(dg3, xn3), (du3, xn3)], tt=tm, name="ffn2_dwgu")
    pair3 = rs_pair_comm([dwg3, dwu3, dwd3])
    dh2, dsh3, dsc3, dgn3 = dx_norm_bwd([dg3, du3], [wg3, wu3], h2, dy, g_norm3, sc3, seq=S, tm=tm_small, name="ffn2_dx",
                                        comms=[pair3])
    chip3 = rs_chip_comm(sums_of(pair3, "ffn2"))

    dm, dga2 = gate_bwd(dh2, mm, ga2, 1.0, seq=S, tm=tm, name="mixer_gate_bwd")
    dgt, dya, dyb, dsgu, datt = mixer_out_bwd(dm, ya, yb, pgt, wa, wb, wo, tm=tm_small, name="mixer_out_bwd")
    (dwo,) = mm_tn([(merged, dm)], tt=tm, name="mixer_dwo")
    (dwa,) = mm_tn([(sgu, dya)], tt=tm, name="mixer_dwa")
    (dwb,) = mm_tn([(att, dyb)], tt=tm, name="mixer_dwb")
    duv, dws, dzs, dgln, dbln = sgu_bwd(puv, dsgu, g_sgu_ln, b_sgu_ln, ws, bs_t, cpb=cpb, name="sgu_bwd")
    dqkv, dgq_h, dgk_h, dsk = attn_bwd(pqkv, datt, gq_t, gk_t, attn_sinks, seq=S, name="attn_bwd", comms=[chip3])
    dgq = dgq_h.reshape(N_Q, HEAD_DIM).sum(axis=0, keepdims=True)
    dgk = dgk_h.reshape(N_KV, HEAD_DIM).sum(axis=0, keepdims=True)
    share3 = rs_share_comm(halves_of(chip3, "ffn2"))
    dwin_parts = mm_tn([(duv, xn2), (dqkv, xn2), (dgt, xn2)], tt=tm, name="mixer_dwin")
    dwin4 = jnp.concatenate(dwin_parts, axis=0).reshape(win4.shape)
    pair_mix = rs_pair_comm([dwin4, _stack_cols(dwa), _stack_cols(dwb), dwo.reshape(N_QUAD, D // N_QUAD, D)])
    dh1, dsh2, dsc2, dgn2 = dx_norm_bwd([duv, dqkv, dgt], [w_uv, w_qkv, w_gt], h1, dh2, g_norm2, sc2, seq=S,
                                        tm=tm_small, name="mixer_dx", comms=[pair_mix])
    chip_mix = rs_chip_comm(sums_of(pair_mix, "mix"))

    df1, dga1 = gate_bwd(dh1, f1, ga1, 0.5, seq=S, tm=tm, name="ffn1_gate_bwd")
    dg1, du1 = ffn_bwd_act(df1, wd1, g1, u1, tm=tm, name="ffn1_act_bwd", comms=[share3])
    (dwd1,) = mm_tn([(a1, df1)], tt=tm, name="ffn1_dwd")
    dwg1, dwu1 = mm_tn([(dg1, xn1), (du1, xn1)], tt=tm, name="ffn1_dwgu", comms=[chip_mix])
    pair1 = rs_pair_comm([dwg1, dwu1, dwd1])
    run_comms([pair1], name="rs_pair_ffn1")
    chip1 = rs_chip_comm(sums_of(pair1, "ffn1"))
    dx, dsh1, dsc1, dgn1 = dx_norm_bwd([dg1, du1], [wg1, wu1], xt, dh1, g_norm1, sc1, seq=S, tm=tm_small, name="ffn1_dx",
                                       comms=[chip1])
    share_rest = rs_share_comm(halves_of(chip_mix, "mix") + halves_of(chip1, "ffn1"))
    run_comms([share_rest], name="rs_share_rest")
    r_win, r_wa, r_wb, r_wo, r_wg1, r_wu1, r_wd1 = share_rest.bufs_out
    r_wg3, r_wu3, r_wd3 = share3.bufs_out

    dmods = jnp.concatenate([dsh1, dsc1, dga1, dsh2, dsc2, dga2, dsh3, dsc3, dga3], axis=1)
    dmods_all = all_gather8(dmods, name="gather_dmods").reshape(N_DEV * B, n_mod * D)
    dmods_mine = lax.dynamic_slice(dmods_all, (0, quad * n_ada), (N_DEV * B, n_ada))
    db_ada, dw_ada = ada_bwd(c_all, dmods_all, dmods_mine, name="ada_bwd")

    db_s = dzs.reshape(CHUNK, N_GROUPS, D_A // N_GROUPS).sum(axis=2).T.reshape(1, N_GROUPS * CHUNK)
    tail = jnp.concatenate([db_s, dgq, dgk, dsk[0:1, 0:N_Q], loss_part.reshape(1, 1)], axis=1)
    tail = jnp.pad(tail, ((0, 0), (0, (-tail.shape[1]) % D)))
    lnrow = jnp.concatenate([dgln, dbln], axis=1)
    lnrow = jnp.pad(lnrow, ((0, 0), (0, (-lnrow.shape[1]) % D)))
    packed = jnp.concatenate([dgn1, dgn2, dgn3, lnrow.reshape(-1, D), tail.reshape(-1, D), dws.reshape(-1, D)], axis=0)
    n_rows = packed.shape[0]
    packed = jnp.pad(packed, ((0, (-n_rows) % 8), (0, 0)))
    small = all_gather8(packed, name="reduce_small", reduce=True)
    ln_rows = lnrow.size // D
    tail_rows = tail.size // D
    r = 3
    g_gn1, g_gn2, g_gn3 = small[0:1], small[1:2], small[2:3]
    ln_flat = small[r:r + ln_rows].reshape(1, -1)
    r += ln_rows
    tail_flat = small[r:r + tail_rows].reshape(1, -1)
    r += tail_rows
    g_ws = small[r:r + dws.size // D].reshape(w_spatial.shape)
    g_gln, g_bln = ln_flat[:, 0:D_A], ln_flat[:, D_A:2 * D_A]
    o = N_GROUPS * CHUNK
    g_bs = tail_flat[:, 0:o].reshape(b_spatial.shape)
    g_gq, g_gk, g_sk = tail_flat[:, o:o + HEAD_DIM], tail_flat[:, o + HEAD_DIM:o + 2 * HEAD_DIM], tail_flat[:, o + 2 * HEAD_DIM:o + 2 * HEAD_DIM + N_Q]
    loss = tail_flat[0, o + 2 * HEAD_DIM + N_Q]

    transposed = ("ffn1_w_gate", "ffn1_w_up", "w_in", "ffn2_w_gate", "ffn2_w_up")

    def update(name, w, g, m, v):
        if name in transposed:
            w, m, v = tr(w), tr(m), tr(v)
        shape = w.shape
        two_d = lambda a: a.reshape(-1, shape[-1])
        res = adamw(two_d(w), two_d(g), two_d(m), two_d(v), name=f"adamw_{name}")
        res = [a.reshape(shape) for a in [g] + list(res)]
        return [tr(a) for a in res] if name in transposed else res

    names = ["w_ada", "b_ada", "g_norm1", "ffn1_w_gate", "ffn1_w_up", "ffn1_w_down", "g_norm2", "w_in", "g_sgu_ln",
             "b_sgu_ln", "w_spatial", "b_spatial", "g_q", "g_k", "attn_sinks", "w_branch_a", "w_branch_b", "w_out",
             "g_norm3", "ffn2_w_gate", "ffn2_w_up", "ffn2_w_down"]
    weights = dict(zip(names, [w_ada, b_ada, g_norm1, ffn1_w_gate, ffn1_w_up, ffn1_w_down, g_norm2, w_in, g_sgu_ln,
                               b_sgu_ln, w_spatial, b_spatial, g_q, g_k, attn_sinks, w_branch_a, w_branch_b, w_out,
                               g_norm3, ffn2_w_gate, ffn2_w_up, ffn2_w_down]))
    m_in = dict(zip(names, [m_w_ada, m_b_ada, m_g_norm1, m_ffn1_w_gate, m_ffn1_w_up, m_ffn1_w_down, m_g_norm2, m_w_in,
                            m_g_sgu_ln, m_b_sgu_ln, m_w_spatial, m_b_spatial, m_g_q, m_g_k, m_attn_sinks, m_w_branch_a,
                            m_w_branch_b, m_w_out, m_g_norm3, m_ffn2_w_gate, m_ffn2_w_up, m_ffn2_w_down]))
    v_in = dict(zip(names, [v_w_ada, v_b_ada, v_g_norm1, v_ffn1_w_gate, v_ffn1_w_up, v_ffn1_w_down, v_g_norm2, v_w_in,
                            v_g_sgu_ln, v_b_sgu_ln, v_w_spatial, v_b_spatial, v_g_q, v_g_k, v_attn_sinks, v_w_branch_a,
                            v_w_branch_b, v_w_out, v_g_norm3, v_ffn2_w_gate, v_ffn2_w_up, v_ffn2_w_down]))
    grads = {
        "w_ada": dw_ada[None], "b_ada": db_ada, "g_norm1": g_gn1, "g_norm2": g_gn2, "g_norm3": g_gn3,
        "g_sgu_ln": g_gln, "b_sgu_ln": g_bln, "w_spatial": g_ws, "b_spatial": g_bs, "g_q": g_gq, "g_k": g_gk,
        "attn_sinks": g_sk,
        "ffn1_w_gate": r_wg1[None], "ffn1_w_up": r_wu1[None], "ffn1_w_down": r_wd1[None], "w_in": r_win[None],
        "w_branch_a": r_wa[None], "w_branch_b": r_wb[None], "w_out": r_wo[None],
        "ffn2_w_gate": r_wg3[None], "ffn2_w_up": r_wu3[None], "ffn2_w_down": r_wd3[None],
    }

    small_names = ["b_ada", "g_norm1", "g_norm2", "g_norm3", "g_sgu_ln", "b_sgu_ln", "w_spatial", "b_spatial", "g_q",
                   "g_k", "attn_sinks"]

    def pack(d):
        flat = jnp.concatenate([d[nm].reshape(-1) for nm in small_names])
        return jnp.pad(flat, (0, (-flat.size) % (8 * 128))).reshape(-1, 128)

    sd, sm, sv = adamw(pack(weights), pack(grads), pack(m_in), pack(v_in), name="adamw_small")
    delta, new_m, new_v = {}, {}, {}
    off = 0
    for nm in small_names:
        size, shape = weights[nm].size, weights[nm].shape
        delta[nm] = sd.reshape(-1)[off:off + size].reshape(shape)
        new_m[nm] = sm.reshape(-1)[off:off + size].reshape(shape)
        new_v[nm] = sv.reshape(-1)[off:off + size].reshape(shape)
        off += size
    for nm in names:
        if nm not in small_names:
            grads[nm], delta[nm], new_m[nm], new_v[nm] = update(nm, weights[nm], grads[nm], m_in[nm], v_in[nm])
        else:
            grads[nm] = grads[nm].reshape(weights[nm].shape)

    return (loss, dx.reshape(B, S, D), *[grads[nm] for nm in names], *[delta[nm] for nm in names],
            *[new_m[nm] for nm in names], *[new_v[nm] for nm in names])
```

```python
import functools
import math
import operator

import jax
import jax.numpy as jnp
from jax import lax
from jax.experimental import pallas as pl
from jax.experimental.pallas import tpu as pltpu

F32 = jnp.float32
BF16 = jnp.bfloat16
EPS = 1e-6
NEG = -1e30
N_DEV = 8
N_QUAD = 4
D_A = 512
N_GROUPS = 4
CHUNK = 128
HEAD_DIM = 64
N_KV = 2
Q_PER_KV = 4
N_Q = N_KV * Q_PER_KV
D_B = N_Q * HEAD_DIM
QKV_W = D_B + 2 * N_KV * HEAD_DIM
K_OFF = D_B
V_OFF = D_B + N_KV * HEAD_DIM
ATT_SCALE = HEAD_DIM ** -0.5
GELU_K = math.sqrt(2.0 / math.pi)
GELU_C = 0.044715
ADAM_LR, ADAM_B1, ADAM_B2, ADAM_EPS, ADAM_WD, ADAM_STEP = 0.001, 0.9, 0.999, 1e-08, 0.01, 10
VMEM_LIMIT_BYTES = 56 * 1024 * 1024
MESH = pl.DeviceIdType.MESH
NT = (((1,), (1,)), ((), ()))
TN = (((0,), (0,)), ((), ()))
ANY = pl.BlockSpec(memory_space=pl.ANY)


def _dot(a, b):
    return jnp.dot(a, b, preferred_element_type=F32)


def _dg(a, b, dims):
    return lax.dot_general(a, b, dims, preferred_element_type=F32)


def _gelu_parts(x):
    t = jnp.tanh(GELU_K * (x + GELU_C * x * x * x))
    return 0.5 * x * (1.0 + t), t


def _dgelu(x, t):
    return 0.5 * (1.0 + t) + 0.5 * x * (1.0 - t * t) * (GELU_K * (1.0 + 3.0 * GELU_C * x * x))


def _row_block(rows, target):
    b = min(rows, target)
    while rows % b or b % 8:
        b -= 1
    return b


def _resident(a):
    return pl.BlockSpec(a.shape, lambda *_: (0,) * a.ndim, pipeline_mode=pl.Buffered(1))


class Comm:
    def __init__(self, *, srcs=(), bufs=(), land_shapes=(), n_sems, start, finish):
        self.srcs, self.bufs, self.land_shapes = list(srcs), list(bufs), list(land_shapes)
        self.n_sems, self.start, self.finish = n_sems, start, finish
        self.bufs_out, self.lands = None, None


def _call(body, *, name, grid, in_specs, out_specs, out_shape, args, scratch_shapes=(), comms=()):
    in_specs, out_specs, out_shape = list(in_specs), list(out_specs), list(out_shape)
    args, scratch = list(args), list(scratch_shapes)
    n_in, n_out, n_scr = len(args), len(out_shape), len(scratch)
    aliases, layout = {}, []
    for cm in comms:
        i0, o0, s0 = len(args), len(out_shape), len(scratch)
        args += cm.srcs + cm.bufs
        in_specs += [ANY] * (len(cm.srcs) + len(cm.bufs))
        out_shape += [jax.ShapeDtypeStruct(b.shape, b.dtype) for b in cm.bufs] + cm.land_shapes
        out_specs += [ANY] * (len(cm.bufs) + len(cm.land_shapes))
        for j in range(len(cm.bufs)):
            aliases[i0 + len(cm.srcs) + j] = o0 + j
        scratch += [pltpu.SemaphoreType.DMA((cm.n_sems,)), pltpu.SemaphoreType.DMA((cm.n_sems,))]
        layout.append((i0, o0, s0))
    n_in_all, n_out_all = len(args), len(out_shape)

    def wrapped(*refs):
        ins, outs, scr = refs[:n_in_all], refs[n_in_all:n_in_all + n_out_all], refs[n_in_all + n_out_all:]
        parts = []
        for cm, (i0, o0, s0) in zip(comms, layout):
            nb = len(cm.bufs)
            parts.append((ins[i0:i0 + len(cm.srcs)], outs[o0:o0 + nb], outs[o0 + nb:o0 + nb + len(cm.land_shapes)],
                          scr[s0], scr[s0 + 1]))
        if comms and grid:
            ids = [pl.program_id(d) for d in range(len(grid))]
            first = functools.reduce(operator.and_, [i == 0 for i in ids])
            last = functools.reduce(operator.and_, [i == g - 1 for i, g in zip(ids, grid)])

            @pl.when(first)
            def _():
                for cm, p in zip(comms, parts):
                    cm.start(*p)
        elif comms:
            for cm, p in zip(comms, parts):
                cm.start(*p)
        if body is not None:
            body(*ins[:n_in], *outs[:n_out], *scr[:n_scr])
        if comms and grid:
            @pl.when(last)
            def _():
                for cm, p in zip(comms, parts):
                    cm.finish(*p)
        elif comms:
            for cm, p in zip(comms, parts):
                cm.finish(*p)

    kwargs = dict(grid=grid) if grid else {}
    sem = ("arbitrary",) * len(grid)
    res = pl.pallas_call(
        wrapped, name=name, in_specs=in_specs, out_specs=out_specs, out_shape=out_shape, scratch_shapes=scratch,
        input_output_aliases=aliases,
        compiler_params=pltpu.CompilerParams(dimension_semantics=sem, vmem_limit_bytes=VMEM_LIMIT_BYTES), **kwargs,
    )(*args)
    for cm, (i0, o0, s0) in zip(comms, layout):
        nb = len(cm.bufs)
        cm.bufs_out = list(res[o0:o0 + nb])
        cm.lands = list(res[o0 + nb:o0 + nb + len(cm.land_shapes)])
    return list(res[:n_out])


def run_comms(comms, *, name):
    _call(None, name=name, grid=(), in_specs=[], out_specs=[], out_shape=[], args=[], comms=comms)


def norm_mod_fwd(h, g, sc, sh, *, seq, tm, name, comms=()):
    T, D = h.shape
    B, tps = T // seq, seq // tm

    def body(h_ref, g_ref, sc_ref, sh_ref, o_ref):
        x = h_ref[...]
        r = lax.rsqrt(jnp.mean(x * x, axis=-1, keepdims=True) + EPS)
        y = x * r * g_ref[...]
        o_ref[...] = (y * (1.0 + sc_ref[0]) + sh_ref[0]).astype(o_ref.dtype)

    per_seq = pl.BlockSpec((1, 1, D), lambda i: (i // tps, 0, 0))
    return _call(
        body, name=name, grid=(T // tm,),
        in_specs=[pl.BlockSpec((tm, D), lambda i: (i, 0)), pl.BlockSpec((1, D), lambda i: (0, 0)), per_seq, per_seq],
        out_specs=[pl.BlockSpec((tm, D), lambda i: (i, 0))], out_shape=[jax.ShapeDtypeStruct((T, D), BF16)],
        args=[h, g, sc.reshape(B, 1, D), sh.reshape(B, 1, D)], comms=comms)[0]


def ffn_up(xn, wg, wu, *, tm, name, comms=()):
    T, D = xn.shape
    nq, fs, _ = wg.shape

    def body(x_ref, wg_ref, wu_ref, s_ref, q_ref, a_ref):
        x = x_ref[...]
        g = _dg(x, wg_ref[0], NT)
        u = _dg(x, wu_ref[0], NT)
        sg = jax.nn.sigmoid(g)
        s = g * sg
        s_ref[0] = s.astype(BF16)
        q_ref[0] = (u * (sg + s * (1.0 - sg))).astype(BF16)
        a_ref[0] = (s * u).astype(BF16)

    w_spec = pl.BlockSpec((1, fs, D), lambda q, i: (q, 0, 0))
    o_spec = pl.BlockSpec((1, tm, fs), lambda q, i: (q, i, 0))
    o_shape = jax.ShapeDtypeStruct((nq, T, fs), BF16)
    return _call(
        body, name=name, grid=(nq, T // tm),
        in_specs=[pl.BlockSpec((tm, D), lambda q, i: (i, 0)), w_spec, w_spec],
        out_specs=[o_spec, o_spec, o_spec], out_shape=[o_shape, o_shape, o_shape], args=[xn, wg, wu], comms=comms)


def ffn_down(a, wd, hin, ga, target=None, *, seq, tm, name, comms=()):
    nq, T, fs = a.shape
    D = wd.shape[-1]
    B, tps, nt = T // seq, seq // tm, T // tm
    with_loss = target is not None

    def body(*refs):
        if with_loss:
            a_ref, wd_ref, h_ref, ga_ref, t_ref, o_ref, f_ref, l_ref = refs
        else:
            a_ref, wd_ref, h_ref, ga_ref, o_ref, f_ref = refs
        f = _dot(a_ref[0], wd_ref[0])
        for q in range(1, nq):
            f = f + _dot(a_ref[q], wd_ref[q])
        f_ref[...] = f.astype(BF16)
        y = h_ref[...] + (0.5 * ga_ref[0]) * f
        if with_loss:
            e = y - t_ref[...]
            o_ref[...] = e * (1.0 / D)
            l_ref[...] = jnp.full(l_ref.shape, 0.5 * jnp.sum(e * e) * (1.0 / D), F32)
        else:
            o_ref[...] = y

    tile = pl.BlockSpec((tm, D), lambda i: (i, 0))
    in_specs = [pl.BlockSpec((nq, tm, fs), lambda i: (0, i, 0)), _resident(wd),
                tile, pl.BlockSpec((1, 1, D), lambda i: (i // tps, 0, 0))]
    out_specs = [tile, tile]
    out_shape = [jax.ShapeDtypeStruct((T, D), F32), jax.ShapeDtypeStruct((T, D), BF16)]
    args = [a, wd, hin, ga.reshape(B, 1, D)]
    if with_loss:
        in_specs.append(tile)
        args.append(target)
        out_specs.append(pl.BlockSpec((1, 8, 128), lambda i: (i, 0, 0)))
        out_shape.append(jax.ShapeDtypeStruct((nt, 8, 128), F32))
    return _call(body, name=name, grid=(nt,), in_specs=in_specs, out_specs=out_specs, out_shape=out_shape, args=args,
                 comms=comms)


def proj_fwd(xn, ws, *, tm, name, comms=()):
    T, D = xn.shape
    n = len(ws)

    def body(*refs):
        x = refs[0][...]
        for j in range(n):
            refs[1 + n + j][...] = _dg(x, refs[1 + j][...], NT)

    return _call(
        body, name=name, grid=(T // tm,),
        in_specs=[pl.BlockSpec((tm, D), lambda i: (i, 0))] + [pl.BlockSpec(w.shape, lambda i: (0, 0)) for w in ws],
        out_specs=[pl.BlockSpec((tm, w.shape[0]), lambda i: (i, 0)) for w in ws],
        out_shape=[jax.ShapeDtypeStruct((T, w.shape[0]), F32) for w in ws], args=[xn, *ws], comms=comms)


def _layer_norm_parts(v):
    mu = jnp.mean(v, axis=-1, keepdims=True)
    d = v - mu
    rstd = lax.rsqrt(jnp.mean(d * d, axis=-1, keepdims=True) + EPS)
    return d * rstd, rstd


def _causal():
    row = lax.broadcasted_iota(jnp.int32, (CHUNK, CHUNK), 0)
    col = lax.broadcasted_iota(jnp.int32, (CHUNK, CHUNK), 1)
    return row >= col


def sgu_fwd(puv, gln, bln, ws, bs_t, *, cpb, name, comms=()):
    T = puv.shape[0]
    gd = D_A // N_GROUPS
    tm = cpb * CHUNK

    def body(p_ref, gln_ref, bln_ref, ws_ref, bs_ref, o_ref):
        causal = _causal()
        wm = [jnp.where(causal, ws_ref[g], 0.0).astype(BF16) for g in range(N_GROUPS)]
        for j in range(cpb):
            rows = slice(j * CHUNK, (j + 1) * CHUNK)
            u, _ = _gelu_parts(p_ref[rows, 0:D_A])
            vv, _ = _gelu_parts(p_ref[rows, D_A:2 * D_A])
            vhat, _ = _layer_norm_parts(vv)
            vn = (vhat * gln_ref[...] + bln_ref[...]).astype(BF16)
            for g in range(N_GROUPS):
                cols = slice(g * gd, (g + 1) * gd)
                z = _dot(wm[g], vn[:, cols]) + bs_ref[:, g:g + 1]
                o_ref[rows, cols] = (u[:, cols] * z).astype(BF16)

    full = lambda a: pl.BlockSpec(a.shape, lambda i: (0,) * a.ndim)
    return _call(
        body, name=name, grid=(T // tm,),
        in_specs=[pl.BlockSpec((tm, 2 * D_A), lambda i: (i, 0)), full(gln), full(bln), full(ws), full(bs_t)],
        out_specs=[pl.BlockSpec((tm, D_A), lambda i: (i, 0))], out_shape=[jax.ShapeDtypeStruct((T, D_A), BF16)],
        args=[puv, gln, bln, ws, bs_t], comms=comms)[0]


def _rms_parts(x):
    r = lax.rsqrt(jnp.mean(x * x, axis=-1, keepdims=True) + EPS)
    return x * r, r


KV_W = Q_PER_KV * HEAD_DIM
KV2 = N_KV * HEAD_DIM
STACK = Q_PER_KV * CHUNK


def _iota(shape, dim):
    return lax.broadcasted_iota(jnp.int32, shape, dim)


def _head_mean_matrix(n):
    return jnp.where((_iota((n, n), 0) >> 6) == (_iota((n, n), 1) >> 6), 1.0 / HEAD_DIM, 0.0).astype(BF16)


def _repeat_matrix(h):
    return jnp.where(_iota((KV2, KV_W), 0) == h * HEAD_DIM + (_iota((KV2, KV_W), 1) & (HEAD_DIM - 1)), 1.0, 0.0).astype(BF16)


def _fold_matrix(h):
    j, l = _iota((KV_W, KV2), 0), _iota((KV_W, KV2), 1)
    return jnp.where(((j & (HEAD_DIM - 1)) == (l & (HEAD_DIM - 1))) & ((l >> 6) == h), 1.0, 0.0).astype(BF16)


def _dot_split(x, m):
    hi = x.astype(BF16)
    lo = (x - hi.astype(F32)).astype(BF16)
    return _dot(hi, m) + _dot(lo, m)


def _head_rms(x, mean_matrix):
    r = lax.rsqrt(_dot_split(x * x, mean_matrix) + EPS)
    return x * r, r


def _stack_heads(x):
    head = _iota(x.shape, 1) >> 6
    return jnp.concatenate([jnp.where(head == g, x, 0.0).astype(BF16) for g in range(Q_PER_KV)], axis=0)


def _unstack_heads(y):
    head = _iota((CHUNK, KV_W), 1) >> 6
    out = jnp.where(head == 0, y[0:CHUNK], 0.0)
    for g in range(1, Q_PER_KV):
        out = out + jnp.where(head == g, y[g * CHUNK:(g + 1) * CHUNK], 0.0)
    return out


def _attn_mask(i):
    qi = _iota((STACK, 2 * CHUNK), 0) & (CHUNK - 1)
    kj = _iota((STACK, 2 * CHUNK), 1)
    diff = qi + CHUNK - kj
    return (diff >= 0) & (diff < CHUNK) & ((kj >= CHUNK) | (i > 0))


def _sink_column(sink_ref, h):
    blk = _iota((STACK, 1), 0) >> 7
    col = jnp.full((STACK, 1), sink_ref[0, h * Q_PER_KV], F32)
    for g in range(1, Q_PER_KV):
        col = jnp.where(blk == g, sink_ref[0, h * Q_PER_KV + g], col)
    return col


def _attn_probs(qs, kk, valid, sink):
    s = _dg(qs, kk, NT) * ATT_SCALE
    s = jnp.where(valid, s, NEG)
    m = jnp.maximum(jnp.max(s, axis=-1, keepdims=True), sink)
    p = jnp.exp(s - m)
    es = jnp.exp(sink - m)
    inv = 1.0 / (jnp.sum(p, axis=-1, keepdims=True) + es)
    return p * inv, es * inv


def _fill_keys(p_ref, gk_ref, krep, vrep, seq):
    mean_k = _head_mean_matrix(KV2)
    reps = [_repeat_matrix(h) for h in range(N_KV)]
    for h in range(N_KV):
        krep[h][0:CHUNK, :] = jnp.zeros((CHUNK, KV_W), BF16)
        vrep[h][0:CHUNK, :] = jnp.zeros((CHUNK, KV_W), BF16)
    rows = min(seq, 4 * CHUNK)

    def piece(j, carry):
        r0 = pl.multiple_of(j * rows, CHUNK)
        nk, _ = _head_rms(p_ref[pl.ds(r0, rows), K_OFF:K_OFF + KV2], mean_k)
        kn = (nk * gk_ref[...]).astype(BF16)
        v = p_ref[pl.ds(r0, rows), V_OFF:V_OFF + KV2].astype(BF16)
        r1 = pl.multiple_of(r0 + CHUNK, CHUNK)
        for h in range(N_KV):
            krep[h][pl.ds(r1, rows), :] = _dot(kn, reps[h]).astype(BF16)
            vrep[h][pl.ds(r1, rows), :] = _dot(v, reps[h]).astype(BF16)
        return carry

    lax.fori_loop(0, seq // rows, piece, 0)


def attn_fwd(pqkv, gq_t, gk_t, sinks, *, seq, name, comms=()):
    T = pqkv.shape[0]
    nb = seq // CHUNK

    def body(p_ref, gq_ref, gk_ref, sink_ref, o_ref, k0, k1, v0, v1):
        krep, vrep = [k0, k1], [v0, v1]
        _fill_keys(p_ref, gk_ref, krep, vrep, seq)
        mean_q = _head_mean_matrix(D_B)

        def block(i, carry):
            r0 = pl.multiple_of(i * CHUNK, CHUNK)
            nq, _ = _head_rms(p_ref[pl.ds(r0, CHUNK), 0:D_B], mean_q)
            qn = nq * gq_ref[...]
            valid = _attn_mask(i)
            for h in range(N_KV):
                qs = _stack_heads(qn[:, h * KV_W:(h + 1) * KV_W])
                kk = krep[h][pl.ds(r0, 2 * CHUNK), :]
                vv = vrep[h][pl.ds(r0, 2 * CHUNK), :]
                probs, _ = _attn_probs(qs, kk, valid, _sink_column(sink_ref, h))
                out = _unstack_heads(_dot(probs.astype(BF16), vv))
                o_ref[pl.ds(r0, CHUNK), h * KV_W:(h + 1) * KV_W] = out.astype(BF16)
            return carry

        lax.fori_loop(0, nb, block, 0)

    return _call(
        body, name=name, grid=(T // seq,),
        in_specs=[pl.BlockSpec((seq, QKV_W), lambda b: (b, 0)), pl.BlockSpec(gq_t.shape, lambda b: (0, 0)),
                  pl.BlockSpec(gk_t.shape, lambda b: (0, 0)), pl.BlockSpec(memory_space=pltpu.SMEM)],
        out_specs=[pl.BlockSpec((seq, D_B), lambda b: (b, 0))], out_shape=[jax.ShapeDtypeStruct((T, D_B), BF16)],
        scratch_shapes=[pltpu.VMEM((seq + CHUNK, KV_W), BF16)] * 4,
        args=[pqkv, gq_t, gk_t, sinks], comms=comms)[0]


def mixer_out_fwd(sgu, att, pg, hin, ga, wa, wb, wo, *, seq, tm, name, comms=()):
    T, D = hin.shape
    B, tps = T // seq, seq // tm

    def body(s_ref, t_ref, pg_ref, h_ref, ga_ref, wa_ref, wb_ref, wo_ref, ya_ref, yb_ref, mg_ref, m_ref, ho_ref):
        ya = _dot(s_ref[...], wa_ref[...])
        yb = _dot(t_ref[...], wb_ref[...])
        merged = jax.nn.sigmoid(pg_ref[:, 0:D]) * ya + jax.nn.sigmoid(pg_ref[:, D:2 * D]) * yb
        mg = merged.astype(BF16)
        m = _dot(mg, wo_ref[...])
        ya_ref[...] = ya.astype(BF16)
        yb_ref[...] = yb.astype(BF16)
        mg_ref[...] = mg
        m_ref[...] = m.astype(BF16)
        ho_ref[...] = h_ref[...] + ga_ref[0] * m

    tile = lambda w: pl.BlockSpec((tm, w), lambda i: (i, 0))
    full = lambda a: pl.BlockSpec(a.shape, lambda i: (0, 0))
    b16 = jax.ShapeDtypeStruct((T, D), BF16)
    return _call(
        body, name=name, grid=(T // tm,),
        in_specs=[tile(D_A), tile(D_B), tile(2 * D), tile(D), pl.BlockSpec((1, 1, D), lambda i: (i // tps, 0, 0)),
                  full(wa), full(wb), full(wo)],
        out_specs=[tile(D)] * 5, out_shape=[b16, b16, b16, b16, jax.ShapeDtypeStruct((T, D), F32)],
        args=[sgu, att, pg, hin, ga.reshape(B, 1, D), wa, wb, wo], comms=comms)


def gate_bwd(dh, f, ga, scale, *, seq, tm, name, comms=()):
    T, D = dh.shape
    B, tps = T // seq, seq // tm

    def body(dh_ref, f_ref, ga_ref, df_ref, dga_ref):
        i = pl.program_id(0)
        d = dh_ref[...]
        df_ref[...] = ((scale * ga_ref[0]) * d).astype(BF16)
        part = scale * jnp.sum(d * f_ref[...].astype(F32), axis=0, keepdims=True)

        @pl.when(i % tps == 0)
        def _():
            dga_ref[0] = part

        @pl.when(i % tps != 0)
        def _():
            dga_ref[0] += part

    tile = pl.BlockSpec((tm, D), lambda i: (i, 0))
    per_seq = pl.BlockSpec((1, 1, D), lambda i: (i // tps, 0, 0))
    df, dga = _call(
        body, name=name, grid=(T // tm,), in_specs=[tile, tile, per_seq], out_specs=[tile, per_seq],
        out_shape=[jax.ShapeDtypeStruct((T, D), BF16), jax.ShapeDtypeStruct((B, 1, D), F32)],
        args=[dh, f, ga.reshape(B, 1, D)], comms=comms)
    return df, dga.reshape(B, D)


def ffn_bwd_act(df, wd, s, q, *, tm, name, comms=()):
    T, D = df.shape
    nq, fs, _ = wd.shape

    def body(df_ref, wd_ref, s_ref, q_ref, dg_ref, du_ref):
        da = _dg(df_ref[...], wd_ref[0], NT)
        du_ref[0] = (da * s_ref[0].astype(F32)).astype(BF16)
        dg_ref[0] = (da * q_ref[0].astype(F32)).astype(BF16)

    act = pl.BlockSpec((1, tm, fs), lambda q, i: (q, i, 0))
    o_shape = jax.ShapeDtypeStruct((nq, T, fs), BF16)
    return _call(
        body, name=name, grid=(nq, T // tm),
        in_specs=[pl.BlockSpec((tm, D), lambda q, i: (i, 0)), pl.BlockSpec((1, fs, D), lambda q, i: (q, 0, 0)), act, act],
        out_specs=[act, act], out_shape=[o_shape, o_shape], args=[df, wd, s, q], comms=comms)


def mm_tn(pairs, *, tt, name, comms=()):
    n = len(pairs)
    flat = []
    for pair in pairs:
        for a in pair:
            if not any(a is b for b in flat):
                flat.append(a)
    where = lambda a: [k for k, b in enumerate(flat) if a is b][0]
    nq = max([a.shape[0] for a in flat if a.ndim == 3] + [1])
    T = flat[0].shape[-2]
    tt = min(tt, T)
    nt = T // tt
    stacked = [x.ndim == 3 or y.ndim == 3 for x, y in pairs]

    def body(*refs):
        in_refs, o_refs, accs = refs[:len(flat)], refs[len(flat):len(flat) + n], refs[len(flat) + n:]
        t = pl.program_id(1)
        value = lambda a: in_refs[where(a)][0] if a.ndim == 3 else in_refs[where(a)][...]

        def put(j, v):
            if stacked[j]:
                o_refs[j][0] = v.astype(BF16)
            else:
                o_refs[j][...] = v.astype(BF16)

        for j, (x, y) in enumerate(pairs):
            part = _dg(value(x), value(y), TN)
            if nt == 1:
                put(j, part)
                continue

            @pl.when(t == 0)
            def _():
                accs[j][...] = part

            @pl.when(t != 0)
            def _():
                accs[j][...] += part

        if nt > 1:
            @pl.when(t == nt - 1)
            def _():
                for j in range(n):
                    put(j, accs[j][...])

    def spec(a):
        if a.ndim == 3:
            return pl.BlockSpec((1, tt, a.shape[2]), lambda q, t: (q, t, 0))
        return pl.BlockSpec((tt, a.shape[1]), lambda q, t: (t, 0))

    out_specs, out_shape = [], []
    for j, (x, y) in enumerate(pairs):
        K, N = x.shape[-1], y.shape[-1]
        if stacked[j]:
            out_specs.append(pl.BlockSpec((1, K, N), lambda q, t: (q, 0, 0)))
            out_shape.append(jax.ShapeDtypeStruct((nq, K, N), BF16))
        else:
            out_specs.append(pl.BlockSpec((K, N), lambda q, t: (0, 0)))
            out_shape.append(jax.ShapeDtypeStruct((K, N), BF16))
    return _call(
        body, name=name, grid=(nq, nt), in_specs=[spec(a) for a in flat],
        out_specs=out_specs, out_shape=out_shape,
        scratch_shapes=[pltpu.VMEM((x.shape[-1], y.shape[-1]), F32) for x, y in pairs] if nt > 1 else [],
        args=flat, comms=comms)


def dx_norm_bwd(dys, ws, hin, dh_out, g, sc, *, seq, tm, name, comms=()):
    T, D = hin.shape
    B, tps = T // seq, seq // tm
    n = len(dys)

    def body(*refs):
        dy_refs, w_refs = refs[:n], refs[n:2 * n]
        h_ref, dho_ref, g_ref, sc_ref, dhi_ref, dsh_ref, dsc_ref, dgn_ref = refs[2 * n:]
        i = pl.program_id(0)
        dxn = None
        for j in range(n):
            if dys[j].ndim == 3:
                for q in range(dys[j].shape[0]):
                    part = _dot(dy_refs[j][q], w_refs[j][q])
                    dxn = part if dxn is None else dxn + part
            else:
                part = _dot(dy_refs[j][...], w_refs[j][...])
                dxn = part if dxn is None else dxn + part
        x = h_ref[...]
        nx, r = _rms_parts(x)
        gain = g_ref[...]
        one_sc = 1.0 + sc_ref[0]
        dsh = jnp.sum(dxn, axis=0, keepdims=True)
        dsc = jnp.sum(dxn * (nx * gain), axis=0, keepdims=True)
        dgn = jnp.sum(dxn * one_sc * nx, axis=0, keepdims=True)
        dn = dxn * one_sc * gain
        dhi_ref[...] = dho_ref[...] + r * (dn - nx * jnp.mean(dn * nx, axis=-1, keepdims=True))

        @pl.when(i % tps == 0)
        def _():
            dsh_ref[0] = dsh
            dsc_ref[0] = dsc

        @pl.when(i % tps != 0)
        def _():
            dsh_ref[0] += dsh
            dsc_ref[0] += dsc

        @pl.when(i == 0)
        def _():
            dgn_ref[...] = dgn

        @pl.when(i != 0)
        def _():
            dgn_ref[...] += dgn

    def dy_spec(a):
        if a.ndim == 3:
            return pl.BlockSpec((a.shape[0], tm, a.shape[2]), lambda i: (0, i, 0))
        return pl.BlockSpec((tm, a.shape[1]), lambda i: (i, 0))

    tile = pl.BlockSpec((tm, D), lambda i: (i, 0))
    per_seq = pl.BlockSpec((1, 1, D), lambda i: (i // tps, 0, 0))
    row = pl.BlockSpec((1, D), lambda i: (0, 0))
    dhi, dsh, dsc, dgn = _call(
        body, name=name, grid=(T // tm,),
        in_specs=[dy_spec(a) for a in dys] + [_resident(w) for w in ws] + [tile, tile, row, per_seq],
        out_specs=[tile, per_seq, per_seq, row],
        out_shape=[jax.ShapeDtypeStruct((T, D), F32), jax.ShapeDtypeStruct((B, 1, D), F32),
                   jax.ShapeDtypeStruct((B, 1, D), F32), jax.ShapeDtypeStruct((1, D), F32)],
        args=[*dys, *ws, hin, dh_out, g, sc.reshape(B, 1, D)], comms=comms)
    return dhi, dsh.reshape(B, D), dsc.reshape(B, D), dgn


def mixer_out_bwd(dm, ya, yb, pg, wa, wb, wo, *, tm, name, comms=()):
    T, D = dm.shape

    def body(dm_ref, ya_ref, yb_ref, pg_ref, wa_ref, wb_ref, wo_ref, dg_ref, dya_ref, dyb_ref, ds_ref, dt_ref):
        dmg = _dg(dm_ref[...], wo_ref[...], NT)
        sa = jax.nn.sigmoid(pg_ref[:, 0:D])
        sb = jax.nn.sigmoid(pg_ref[:, D:2 * D])
        dg_ref[:, 0:D] = (dmg * ya_ref[...].astype(F32) * (sa * (1.0 - sa))).astype(BF16)
        dg_ref[:, D:2 * D] = (dmg * yb_ref[...].astype(F32) * (sb * (1.0 - sb))).astype(BF16)
        dya = (dmg * sa).astype(BF16)
        dyb = (dmg * sb).astype(BF16)
        dya_ref[...] = dya
        dyb_ref[...] = dyb
        ds_ref[...] = _dg(dya, wa_ref[...], NT)
        dt_ref[...] = _dg(dyb, wb_ref[...], NT)

    tile = lambda w: pl.BlockSpec((tm, w), lambda i: (i, 0))
    full = lambda a: pl.BlockSpec(a.shape, lambda i: (0, 0))
    return _call(
        body, name=name, grid=(T // tm,),
        in_specs=[tile(D), tile(D), tile(D), tile(2 * D), full(wa), full(wb), full(wo)],
        out_specs=[tile(2 * D), tile(D), tile(D), tile(D_A), tile(D_B)],
        out_shape=[jax.ShapeDtypeStruct((T, 2 * D), BF16), jax.ShapeDtypeStruct((T, D), BF16),
                   jax.ShapeDtypeStruct((T, D), BF16), jax.ShapeDtypeStruct((T, D_A), F32),
                   jax.ShapeDtypeStruct((T, D_B), F32)],
        args=[dm, ya, yb, pg, wa, wb, wo], comms=comms)


def sgu_bwd(puv, dsgu, gln, bln, ws, bs_t, *, cpb, name, comms=()):
    T = puv.shape[0]
    gd = D_A // N_GROUPS
    tm = cpb * CHUNK

    def body(p_ref, ds_ref, gln_ref, bln_ref, ws_ref, bs_ref, d_ref, dws_ref, dz_ref, dgl_ref, dbl_ref):
        i = pl.program_id(0)
        causal = _causal()
        wf = [jnp.where(causal, ws_ref[g], 0.0) for g in range(N_GROUPS)]
        wm = [w.astype(BF16) for w in wf]
        wt = [w.T.astype(BF16) for w in wf]
        dws = [jnp.zeros((CHUNK, CHUNK), F32) for _ in range(N_GROUPS)]
        dzs = jnp.zeros((CHUNK, D_A), F32)
        dgl = jnp.zeros((1, D_A), F32)
        dbl = jnp.zeros((1, D_A), F32)
        for j in range(cpb):
            rows = slice(j * CHUNK, (j + 1) * CHUNK)
            au = p_ref[rows, 0:D_A]
            av = p_ref[rows, D_A:2 * D_A]
            u, tu = _gelu_parts(au)
            vv, tv = _gelu_parts(av)
            vhat, rstd = _layer_norm_parts(vv)
            vn = (vhat * gln_ref[...] + bln_ref[...]).astype(BF16)
            dsg = ds_ref[rows, :]
            dz = dsg * u
            dzb = dz.astype(BF16)
            zs, dvns = [], []
            for g in range(N_GROUPS):
                cols = slice(g * gd, (g + 1) * gd)
                zs.append(_dot(wm[g], vn[:, cols]) + bs_ref[:, g:g + 1])
                dws[g] = dws[g] + _dg(dzb[:, cols], vn[:, cols], NT)
                dvns.append(_dot(wt[g], dzb[:, cols]))
            z = jnp.concatenate(zs, axis=1)
            dvn = jnp.concatenate(dvns, axis=1)
            d_ref[rows, 0:D_A] = (dsg * z * _dgelu(au, tu)).astype(BF16)
            dvh = dvn * gln_ref[...]
            dvv = rstd * (dvh - jnp.mean(dvh, axis=-1, keepdims=True)
                          - vhat * jnp.mean(dvh * vhat, axis=-1, keepdims=True))
            d_ref[rows, D_A:2 * D_A] = (dvv * _dgelu(av, tv)).astype(BF16)
            dzs = dzs + dz
            dgl = dgl + jnp.sum(dvn * vhat, axis=0, keepdims=True)
            dbl = dbl + jnp.sum(dvn, axis=0, keepdims=True)

        @pl.when(i == 0)
        def _():
            for g in range(N_GROUPS):
                dws_ref[g] = jnp.where(causal, dws[g], 0.0)
            dz_ref[...] = dzs
            dgl_ref[...] = dgl
            dbl_ref[...] = dbl

        @pl.when(i != 0)
        def _():
            for g in range(N_GROUPS):
                dws_ref[g] += jnp.where(causal, dws[g], 0.0)
            dz_ref[...] += dzs
            dgl_ref[...] += dgl
            dbl_ref[...] += dbl

    full = lambda a: pl.BlockSpec(a.shape, lambda i: (0,) * a.ndim)
    acc = lambda s: pl.BlockSpec(s, lambda i: (0,) * len(s))
    return _call(
        body, name=name, grid=(T // tm,),
        in_specs=[pl.BlockSpec((tm, 2 * D_A), lambda i: (i, 0)), pl.BlockSpec((tm, D_A), lambda i: (i, 0)),
                  full(gln), full(bln), full(ws), full(bs_t)],
        out_specs=[pl.BlockSpec((tm, 2 * D_A), lambda i: (i, 0)), acc((N_GROUPS, CHUNK, CHUNK)), acc((CHUNK, D_A)),
                   acc((1, D_A)), acc((1, D_A))],
        out_shape=[jax.ShapeDtypeStruct((T, 2 * D_A), BF16), jax.ShapeDtypeStruct((N_GROUPS, CHUNK, CHUNK), F32),
                   jax.ShapeDtypeStruct((CHUNK, D_A), F32), jax.ShapeDtypeStruct((1, D_A), F32),
                   jax.ShapeDtypeStruct((1, D_A), F32)],
        args=[puv, dsgu, gln, bln, ws, bs_t], comms=comms)


def attn_bwd(pqkv, datt, gq_t, gk_t, sinks, *, seq, name, comms=()):
    T = pqkv.shape[0]
    nb = seq // CHUNK

    def body(p_ref, do_ref, gq_ref, gk_ref, sink_ref, d_ref, dgq_ref, dgk_ref, dsk_ref,
             k0, k1, v0, v1, dk0, dk1, dv0, dv1, dgq_s, dsk_s):
        b = pl.program_id(0)
        krep, vrep, dkacc, dvacc = [k0, k1], [v0, v1], [dk0, dk1], [dv0, dv1]
        _fill_keys(p_ref, gk_ref, krep, vrep, seq)
        for h in range(N_KV):
            dkacc[h][...] = jnp.zeros_like(dkacc[h])
            dvacc[h][...] = jnp.zeros_like(dvacc[h])
        dgq_s[...] = jnp.zeros_like(dgq_s)
        dsk_s[...] = jnp.zeros_like(dsk_s)
        mean_q = _head_mean_matrix(D_B)
        lane = _iota((1, 128), 1)

        def block(i, carry):
            r0 = pl.multiple_of(i * CHUNK, CHUNK)
            nq, rq = _head_rms(p_ref[pl.ds(r0, CHUNK), 0:D_B], mean_q)
            qn = nq * gq_ref[...]
            valid = _attn_mask(i)
            dqn_parts = []
            for h in range(N_KV):
                cols = slice(h * KV_W, (h + 1) * KV_W)
                qs = _stack_heads(qn[:, cols])
                kk = krep[h][pl.ds(r0, 2 * CHUNK), :]
                vv = vrep[h][pl.ds(r0, 2 * CHUNK), :]
                probs, psink = _attn_probs(qs, kk, valid, _sink_column(sink_ref, h))
                dos = _stack_heads(do_ref[pl.ds(r0, CHUNK), cols])
                dp = _dg(dos, vv, NT)
                dr = jnp.sum(probs * dp, axis=-1, keepdims=True)
                ds = (probs * (dp - dr) * ATT_SCALE).astype(BF16)
                dsink = psink * dr
                for g in range(Q_PER_KV):
                    dsk_s[...] += jnp.where(lane == h * Q_PER_KV + g, -jnp.sum(dsink[g * CHUNK:(g + 1) * CHUNK]), 0.0)
                dqn_parts.append(_unstack_heads(_dot(ds, kk)))
                dkacc[h][pl.ds(r0, 2 * CHUNK), :] += _dg(ds, qs, TN)
                dvacc[h][pl.ds(r0, 2 * CHUNK), :] += _dg(probs.astype(BF16), dos, TN)
            dqn = jnp.concatenate(dqn_parts, axis=1)
            dn = dqn * gq_ref[...]
            d_ref[pl.ds(r0, CHUNK), 0:D_B] = (rq * (dn - nq * _dot_split(dn * nq, mean_q))).astype(BF16)
            dgq_s[...] += jnp.sum(dqn * nq, axis=0, keepdims=True)
            return carry

        lax.fori_loop(0, nb, block, 0)

        mean_k = _head_mean_matrix(KV2)
        folds = [_fold_matrix(h) for h in range(N_KV)]
        rows = min(seq, 4 * CHUNK)

        def piece(j, dgk):
            r0 = pl.multiple_of(j * rows, CHUNK)
            r1 = pl.multiple_of(r0 + CHUNK, CHUNK)
            dkn = _dot_split(dkacc[0][pl.ds(r1, rows), :], folds[0]) + _dot_split(dkacc[1][pl.ds(r1, rows), :], folds[1])
            dv = _dot_split(dvacc[0][pl.ds(r1, rows), :], folds[0]) + _dot_split(dvacc[1][pl.ds(r1, rows), :], folds[1])
            nk, rk = _head_rms(p_ref[pl.ds(r0, rows), K_OFF:K_OFF + KV2], mean_k)
            dn = dkn * gk_ref[...]
            d_ref[pl.ds(r0, rows), K_OFF:K_OFF + KV2] = (rk * (dn - nk * _dot_split(dn * nk, mean_k))).astype(BF16)
            d_ref[pl.ds(r0, rows), V_OFF:V_OFF + KV2] = dv.astype(BF16)
            return dgk + jnp.sum(dkn * nk, axis=0, keepdims=True)

        dgk = lax.fori_loop(0, seq // rows, piece, jnp.zeros((1, KV2), F32))

        @pl.when(b == 0)
        def _():
            dgq_ref[...] = dgq_s[...]
            dgk_ref[...] = dgk
            dsk_ref[...] = jnp.broadcast_to(dsk_s[...], dsk_ref.shape)

        @pl.when(b != 0)
        def _():
            dgq_ref[...] += dgq_s[...]
            dgk_ref[...] += dgk
            dsk_ref[...] += jnp.broadcast_to(dsk_s[...], dsk_ref.shape)

    acc = lambda s: pl.BlockSpec(s, lambda b: (0, 0))
    return _call(
        body, name=name, grid=(T // seq,),
        in_specs=[pl.BlockSpec((seq, QKV_W), lambda b: (b, 0)), pl.BlockSpec((seq, D_B), lambda b: (b, 0)),
                  pl.BlockSpec(gq_t.shape, lambda b: (0, 0)), pl.BlockSpec(gk_t.shape, lambda b: (0, 0)),
                  pl.BlockSpec(memory_space=pltpu.SMEM)],
        out_specs=[pl.BlockSpec((seq, QKV_W), lambda b: (b, 0)), acc((1, D_B)), acc((1, KV2)), acc((8, 128))],
        out_shape=[jax.ShapeDtypeStruct((T, QKV_W), BF16), jax.ShapeDtypeStruct((1, D_B), F32),
                   jax.ShapeDtypeStruct((1, KV2), F32), jax.ShapeDtypeStruct((8, 128), F32)],
        scratch_shapes=[pltpu.VMEM((seq + CHUNK, KV_W), BF16)] * 4 + [pltpu.VMEM((seq + CHUNK, KV_W), F32)] * 4
        + [pltpu.VMEM((1, D_B), F32), pltpu.VMEM((1, 128), F32)],
        args=[pqkv, datt, gq_t, gk_t, sinks], comms=comms)


def ada_fwd(c_all, w, b, *, name):
    nb, D = c_all.shape
    N = w.shape[1]
    tn = N // 3

    def body(c_ref, w_ref, b_ref, o_ref):
        c = c_ref[...]
        cond = (c * jax.nn.sigmoid(c)).astype(BF16)
        o_ref[...] = _dot(cond, w_ref[...].astype(BF16)) + b_ref[...]

    return _call(
        body, name=name, grid=(3,),
        in_specs=[pl.BlockSpec((nb, D), lambda j: (0, 0)), pl.BlockSpec((D, tn), lambda j: (0, j)),
                  pl.BlockSpec((1, tn), lambda j: (0, j))],
        out_specs=[pl.BlockSpec((nb, tn), lambda j: (0, j))], out_shape=[jax.ShapeDtypeStruct((nb, N), F32)],
        args=[c_all, w, b])[0]


def ada_bwd(c_all, dmods_all, dmods_mine, *, name):
    nb, D = c_all.shape
    NA = dmods_all.shape[1]
    N = dmods_mine.shape[1]
    tn = N // 3

    def body(c_ref, da_ref, dm_ref, db_ref, dw_ref):
        c = c_ref[...]
        cond = (c * jax.nn.sigmoid(c)).astype(BF16)
        dw_ref[...] = _dg(cond, dm_ref[...].astype(BF16), TN)
        db_ref[...] = jnp.sum(da_ref[...], axis=0, keepdims=True)

    return _call(
        body, name=name, grid=(3,),
        in_specs=[pl.BlockSpec((nb, D), lambda j: (0, 0)), pl.BlockSpec((nb, NA), lambda j: (0, 0)),
                  pl.BlockSpec((nb, tn), lambda j: (0, j))],
        out_specs=[pl.BlockSpec((1, NA), lambda j: (0, 0)), pl.BlockSpec((D, tn), lambda j: (0, j))],
        out_shape=[jax.ShapeDtypeStruct((1, NA), F32), jax.ShapeDtypeStruct((D, N), F32)],
        args=[c_all, dmods_all, dmods_mine])


def adamw(w, g, m, v, *, name):
    R, C = w.shape
    rb = _row_block(R, max(8, (1 << 18) // C))
    c1 = 1.0 / (1.0 - ADAM_B1 ** ADAM_STEP)
    c2 = 1.0 / (1.0 - ADAM_B2 ** ADAM_STEP)

    def body(w_ref, g_ref, m_ref, v_ref, d_ref, mo_ref, vo_ref):
        gg = g_ref[...]
        mn = ADAM_B1 * m_ref[...] + (1.0 - ADAM_B1) * gg
        vn = ADAM_B2 * v_ref[...] + (1.0 - ADAM_B2) * (gg * gg)
        mo_ref[...] = mn
        vo_ref[...] = vn
        d_ref[...] = -ADAM_LR * ((mn * c1) / (jnp.sqrt(vn * c2) + ADAM_EPS) + ADAM_WD * w_ref[...])

    blk = pl.BlockSpec((rb, C), lambda i: (i, 0))
    shp = jax.ShapeDtypeStruct((R, C), F32)
    return _call(body, name=name, grid=(R // rb,), in_specs=[blk] * 4, out_specs=[blk] * 3, out_shape=[shp] * 3,
                 args=[w, g, m, v])


def _place():
    return lax.axis_index("x"), lax.axis_index("y"), lax.axis_index("c")


def _flip(v, bit):
    return 1 - v if bit else v


def _other_chips(mx, my):
    return [(_flip(mx, k & 2), _flip(my, k & 1)) for k in range(1, N_QUAD)]


def _remote(src, dst, send_sem, recv_sem, peer):
    return pltpu.make_async_remote_copy(src_ref=src, dst_ref=dst, send_sem=send_sem, recv_sem=recv_sem,
                                        device_id=peer, device_id_type=MESH)


def all_gather8(x, *, name, reduce=False):
    r, n = x.shape

    def body(x_ref, o_ref, *rest):
        if reduce:
            buf, send_sems, recv_sems, lsem = rest
        else:
            buf = o_ref
            send_sems, recv_sems, lsem = rest
        mx, my, mc = _place()
        me = 4 * mx + 2 * my + mc
        mine = pltpu.make_async_copy(x_ref, buf.at[me], lsem)
        mine.start()
        sends, recvs = [], []
        for k in range(1, N_DEV):
            peer = (_flip(mx, k & 4), _flip(my, k & 2), _flip(mc, k & 1))
            pidx = 4 * peer[0] + 2 * peer[1] + peer[2]
            sends.append(_remote(x_ref, buf.at[me], send_sems.at[k - 1], recv_sems.at[k - 1], peer))
            recvs.append(_remote(x_ref, buf.at[pidx], send_sems.at[k - 1], recv_sems.at[k - 1], peer))
        for cp in sends:
            cp.start()
        for cp in recvs:
            cp.wait_recv()
        for cp in sends:
            cp.wait_send()
        mine.wait()
        if reduce:
            total = buf[0]
            for d in range(1, N_DEV):
                total = total + buf[d]
            o_ref[...] = total

    scratch = [pltpu.SemaphoreType.DMA((N_DEV - 1,)), pltpu.SemaphoreType.DMA((N_DEV - 1,)), pltpu.SemaphoreType.DMA]
    if reduce:
        scratch = [pltpu.VMEM((N_DEV, r, n), F32)] + scratch
        out_shape = jax.ShapeDtypeStruct((r, n), F32)
    else:
        out_shape = jax.ShapeDtypeStruct((N_DEV, r, n), F32)
    return pl.pallas_call(
        body, name=name, out_shape=out_shape, in_specs=[pl.BlockSpec(memory_space=pltpu.VMEM)],
        out_specs=pl.BlockSpec(memory_space=pltpu.VMEM), scratch_shapes=scratch,
        compiler_params=pltpu.CompilerParams(vmem_limit_bytes=VMEM_LIMIT_BYTES),
    )(x)


def cast_into_stack(w, quad, *, name):
    R, C = w.shape
    rb = _row_block(R, 256)

    def body(q_ref, w_ref, o_ref):
        o_ref[0] = w_ref[...].astype(BF16)

    return pl.pallas_call(
        body, name=name, out_shape=jax.ShapeDtypeStruct((N_QUAD, R, C), BF16),
        grid_spec=pltpu.PrefetchScalarGridSpec(
            num_scalar_prefetch=1, grid=(R // rb,),
            in_specs=[pl.BlockSpec((rb, C), lambda i, q: (i, 0))],
            out_specs=pl.BlockSpec((1, rb, C), lambda i, q: (q[0], i, 0))),
        compiler_params=pltpu.CompilerParams(dimension_semantics=("arbitrary",), vmem_limit_bytes=VMEM_LIMIT_BYTES),
    )(quad, w)


def gather_comm(stacks):
    n = len(stacks)

    def copies(bufs, ss, rs):
        mx, my, mc = _place()
        q = 2 * mx + my
        sibling = (mx, my, 1 - mc)
        ici_send, ici_recv, fwd_send, fwd_recv = [], [], [], []
        for p in range(n):
            hr = stacks[p].shape[1] // 2
            mine, other = pl.ds(mc * hr, hr), pl.ds((1 - mc) * hr, hr)
            for k, chip in enumerate(_other_chips(mx, my)):
                qk = 2 * chip[0] + chip[1]
                peer = (chip[0], chip[1], mc)
                own, landed, theirs = bufs[p].at[q, mine, :], bufs[p].at[qk, mine, :], bufs[p].at[qk, other, :]
                ici_send.append(_remote(own, own, ss.at[6 * p + k], rs.at[6 * p + k], peer))
                ici_recv.append(_remote(own, landed, ss.at[6 * p + k], rs.at[6 * p + k], peer))
                fwd_send.append(_remote(landed, landed, ss.at[6 * p + 3 + k], rs.at[6 * p + 3 + k], sibling))
                fwd_recv.append(_remote(theirs, theirs, ss.at[6 * p + 3 + k], rs.at[6 * p + 3 + k], sibling))
        return ici_send, ici_recv, fwd_send, fwd_recv

    def start(srcs, bufs, lands, ss, rs):
        for cp in copies(bufs, ss, rs)[0]:
            cp.start()

    def finish(srcs, bufs, lands, ss, rs):
        ici_send, ici_recv, fwd_send, fwd_recv = copies(bufs, ss, rs)
        for arrived, onward in zip(ici_recv, fwd_send):
            arrived.wait_recv()
            onward.start()
        for cp in fwd_recv:
            cp.wait_recv()
        for cp in ici_send + fwd_send:
            cp.wait_send()

    return Comm(bufs=stacks, n_sems=6 * n, start=start, finish=finish)


def rs_pair_comm(gs):
    n = len(gs)

    def copies(srcs, lands, ss, rs):
        mx, my, mc = _place()
        out = []
        for p in range(n):
            hr = gs[p].shape[1] // 2
            out.append(_remote(srcs[p].at[:, pl.ds((1 - mc) * hr, hr), :], lands[p], ss.at[p], rs.at[p], (mx, my, 1 - mc)))
        return out

    def start(srcs, bufs, lands, ss, rs):
        for cp in copies(srcs, lands, ss, rs):
            cp.start()

    def finish(srcs, bufs, lands, ss, rs):
        for cp in copies(srcs, lands, ss, rs):
            cp.wait()

    return Comm(srcs=gs, land_shapes=[jax.ShapeDtypeStruct((g.shape[0], g.shape[1] // 2, g.shape[2]), g.dtype) for g in gs],
                n_sems=n, start=start, finish=finish)


def rs_chip_comm(ss_):
    n = len(ss_)

    def copies(srcs, lands, ss, rs):
        mx, my, mc = _place()
        out = []
        for p in range(n):
            for k, chip in enumerate(_other_chips(mx, my)):
                out.append(_remote(srcs[p].at[2 * chip[0] + chip[1]], lands[p].at[k], ss.at[3 * p + k], rs.at[3 * p + k],
                                   (chip[0], chip[1], mc)))
        return out

    def start(srcs, bufs, lands, ss, rs):
        for cp in copies(srcs, lands, ss, rs):
            cp.start()

    def finish(srcs, bufs, lands, ss, rs):
        for cp in copies(srcs, lands, ss, rs):
            cp.wait()

    return Comm(srcs=ss_, land_shapes=[jax.ShapeDtypeStruct((N_QUAD - 1,) + s.shape[1:], s.dtype) for s in ss_],
                n_sems=3 * n, start=start, finish=finish)


def rs_share_comm(fulls):
    n = len(fulls)

    def copies(bufs, ss, rs):
        mx, my, mc = _place()
        send, recv = [], []
        for p in range(n):
            hr = fulls[p].shape[0] // 2
            mine, other = bufs[p].at[pl.ds(mc * hr, hr), :], bufs[p].at[pl.ds((1 - mc) * hr, hr), :]
            send.append(_remote(mine, mine, ss.at[p], rs.at[p], (mx, my, 1 - mc)))
            recv.append(_remote(other, other, ss.at[p], rs.at[p], (mx, my, 1 - mc)))
        return send, recv

    def start(srcs, bufs, lands, ss, rs):
        for cp in copies(bufs, ss, rs)[0]:
            cp.start()

    def finish(srcs, bufs, lands, ss, rs):
        send, recv = copies(bufs, ss, rs)
        for cp in recv:
            cp.wait_recv()
        for cp in send:
            cp.wait_send()

    return Comm(bufs=fulls, n_sems=n, start=start, finish=finish)


def pair_sum(g, recv, mc, *, name):
    nq, R, C = g.shape
    hr = R // 2
    rb = _row_block(hr, 256)
    nb = hr // rb

    def body(s_ref, g_ref, r_ref, o_ref):
        o_ref[...] = (g_ref[...].astype(F32) + r_ref[...].astype(F32)).astype(BF16)

    return pl.pallas_call(
        body, name=name, out_shape=jax.ShapeDtypeStruct((nq, hr, C), BF16),
        grid_spec=pltpu.PrefetchScalarGridSpec(
            num_scalar_prefetch=1, grid=(nq, nb),
            in_specs=[pl.BlockSpec((1, rb, C), lambda q, i, s: (q, s[0] * nb + i, 0)),
                      pl.BlockSpec((1, rb, C), lambda q, i, s: (q, i, 0))],
            out_specs=pl.BlockSpec((1, rb, C), lambda q, i, s: (q, i, 0))),
        compiler_params=pltpu.CompilerParams(dimension_semantics=("arbitrary", "arbitrary"),
                                             vmem_limit_bytes=VMEM_LIMIT_BYTES),
    )(mc, g, recv)


def chip_sum(s, recv, quad_mc, *, name):
    nq, hr, C = s.shape
    rb = _row_block(hr, 256)
    nb = hr // rb

    def body(q_ref, s_ref, r_ref, o_ref):
        total = s_ref[0].astype(F32)
        for k in range(N_QUAD - 1):
            total = total + r_ref[k].astype(F32)
        o_ref[...] = total

    return pl.pallas_call(
        body, name=name, out_shape=jax.ShapeDtypeStruct((2 * hr, C), F32),
        grid_spec=pltpu.PrefetchScalarGridSpec(
            num_scalar_prefetch=1, grid=(nb,),
            in_specs=[pl.BlockSpec((1, rb, C), lambda i, q: (q[0], i, 0)),
                      pl.BlockSpec((N_QUAD - 1, rb, C), lambda i, q: (0, i, 0))],
            out_specs=pl.BlockSpec((rb, C), lambda i, q: (q[1] * nb + i, 0))),
        compiler_params=pltpu.CompilerParams(dimension_semantics=("arbitrary",), vmem_limit_bytes=VMEM_LIMIT_BYTES),
    )(quad_mc, s, recv)


def _stack_cols(w_full):
    K, N = w_full.shape
    return w_full.reshape(K, N_QUAD, N // N_QUAD).transpose(1, 0, 2)


def _unstack_cols(w4):
    nq, K, n = w4.shape
    return w4.transpose(1, 0, 2).reshape(K, nq * n)


def kernel(x, c, w_ada, b_ada, g_norm1, ffn1_w_gate, ffn1_w_up, ffn1_w_down, g_norm2, w_in, g_sgu_ln, b_sgu_ln, w_spatial, b_spatial, g_q, g_k, attn_sinks, w_branch_a, w_branch_b, w_out, g_norm3, ffn2_w_gate, ffn2_w_up, ffn2_w_down, loss_target, m_w_ada, m_b_ada, m_g_norm1, m_ffn1_w_gate, m_ffn1_w_up, m_ffn1_w_down, m_g_norm2, m_w_in, m_g_sgu_ln, m_b_sgu_ln, m_w_spatial, m_b_spatial, m_g_q, m_g_k, m_attn_sinks, m_w_branch_a, m_w_branch_b, m_w_out, m_g_norm3, m_ffn2_w_gate, m_ffn2_w_up, m_ffn2_w_down, v_w_ada, v_b_ada, v_g_norm1, v_ffn1_w_gate, v_ffn1_w_up, v_ffn1_w_down, v_g_norm2, v_w_in, v_g_sgu_ln, v_b_sgu_ln, v_w_spatial, v_b_spatial, v_g_q, v_g_k, v_attn_sinks, v_w_branch_a, v_w_branch_b, v_w_out, v_g_norm3, v_ffn2_w_gate, v_ffn2_w_up, v_ffn2_w_down):
    B, S, D = x.shape
    T = B * S
    n_mod = b_ada.shape[1] // D
    mx, my, mc = _place()
    quad = 2 * mx + my
    mc_arr = jnp.reshape(mc, (1,)).astype(jnp.int32)
    quad_arr = jnp.reshape(quad, (1,)).astype(jnp.int32)
    quad_mc = jnp.stack([quad, mc]).astype(jnp.int32)
    tm = min(512, S)
    tm_small = min(256, S)
    tm_big = min(1024, S)
    tt_half = min(2048, T)
    cpb = min(4, S // CHUNK)

    xt = x.reshape(T, D)
    tgt = loss_target.reshape(T, D)

    tr = lambda a: jnp.swapaxes(a, 1, 2)
    stack = lambda w, nm: cast_into_stack(w[0], quad_arr, name=f"stack_{nm}")
    gather1 = gather_comm([stack(tr(ffn1_w_gate), "wg1"), stack(tr(ffn1_w_up), "wu1"), stack(ffn1_w_down, "wd1")])
    run_comms([gather1], name="gather_ffn1")
    wg1, wu1, wd1 = gather1.bufs_out
    gather_mix = gather_comm([stack(tr(w_in), "win"), stack(w_branch_a, "wa"), stack(w_branch_b, "wb"),
                              stack(w_out, "wo")])
    gather3a = gather_comm([stack(tr(ffn2_w_gate), "wg3"), stack(tr(ffn2_w_up), "wu3")])
    gather3b = gather_comm([stack(ffn2_w_down, "wd3")])

    c_all = all_gather8(c, name="gather_cond").reshape(N_DEV * B, D)
    n_ada = w_ada.shape[2]
    b_mine = lax.dynamic_slice(b_ada, (0, quad * n_ada), (1, n_ada))
    mods_part = ada_fwd(c_all, w_ada[0], b_mine, name="ada_fwd")
    mods_parts = all_gather8(mods_part, name="gather_mods")
    mods = jnp.concatenate([mods_parts[2 * j] for j in range(N_QUAD)], axis=1)
    me = 4 * mx + 2 * my + mc
    mods = lax.dynamic_slice(mods, (me * B, 0), (B, n_mod * D))
    sh1, sc1, ga1, sh2, sc2, ga2, sh3, sc3, ga3 = [mods[:, j * D:(j + 1) * D] for j in range(n_mod)]
    bs_t = b_spatial[0].T
    ws = w_spatial[0]

    xn1 = norm_mod_fwd(xt, g_norm1, sc1, sh1, seq=S, tm=tm, name="norm1")
    g1, u1, a1 = ffn_up(xn1, wg1, wu1, tm=tm_big, name="ffn1_up", comms=[gather_mix])
    win4, wa4, wb4, wo4 = gather_mix.bufs_out
    win = win4.reshape(-1, D)
    w_uv, w_qkv, w_gt = win[0:2 * D_A], win[2 * D_A:2 * D_A + QKV_W], win[2 * D_A + QKV_W:]
    wa, wb = _unstack_cols(wa4), _unstack_cols(wb4)
    wo = wo4.reshape(D, D)
    h1, f1 = ffn_down(a1, wd1, xt, ga1, seq=S, tm=tm, name="ffn1_down")
    xn2 = norm_mod_fwd(h1, g_norm2, sc2, sh2, seq=S, tm=tm, name="norm2")
    puv, pqkv, pgt = proj_fwd(xn2, [w_uv, w_qkv, w_gt], tm=tm_small, name="proj")
    sgu = sgu_fwd(puv, g_sgu_ln, b_sgu_ln, ws, bs_t, cpb=cpb, name="sgu_fwd")
    gq_t, gk_t = jnp.tile(g_q, (1, N_Q)), jnp.tile(g_k, (1, N_KV))
    att = attn_fwd(pqkv, gq_t, gk_t, attn_sinks, seq=S, name="attn_fwd", comms=[gather3a])
    wg3, wu3 = gather3a.bufs_out
    ya, yb, merged, mm, h2 = mixer_out_fwd(sgu, att, pgt, h1, ga2, wa, wb, wo, seq=S, tm=tm_small, name="mixer_out",
                                           comms=[gather3b])
    (wd3,) = gather3b.bufs_out
    xn3 = norm_mod_fwd(h2, g_norm3, sc3, sh3, seq=S, tm=tm, name="norm3")
    g3, u3, a3 = ffn_up(xn3, wg3, wu3, tm=tm_big, name="ffn2_up")
    dy, f3, lparts = ffn_down(a3, wd3, h2, ga3, tgt, seq=S, tm=tm, name="ffn2_down_loss")
    loss_part = jnp.sum(lparts[:, 0, 0])

    def sums_of(pair, tag):
        return [pair_sum(g, r, mc_arr, name=f"pair_sum_{tag}_{p}") for p, (g, r) in enumerate(zip(pair.srcs, pair.lands))]

    def halves_of(chip, tag):
        return [chip_sum(s, r, quad_mc, name=f"chip_sum_{tag}_{p}") for p, (s, r) in enumerate(zip(chip.srcs, chip.lands))]

    df3, dga3 = gate_bwd(dy, f3, ga3, 0.5, seq=S, tm=tm, name="ffn2_gate_bwd")
    dg3, du3 = ffn_bwd_act(df3, wd3, g3, u3, tm=tm_big, name="ffn2_act_bwd")
    (dwd3,) = mm_tn([(a3, df3)], tt=T, name="ffn2_dwd")
    dwg3, dwu3 = mm_tn([(dg3, xn3), (du3, xn3)], tt=tt_half, name="ffn2_dwgu")
    pair3 = rs_pair_comm([dwg3, dwu3, dwd3])
    dh2, dsh3, dsc3, dgn3 = dx_norm_bwd([dg3, du3], [wg3, wu3], h2, dy, g_norm3, sc3, seq=S, tm=tm, name="ffn2_dx",
                                        comms=[pair3])
    chip3 = rs_chip_comm(sums_of(pair3, "ffn2"))

    dm, dga2 = gate_bwd(dh2, mm, ga2, 1.0, seq=S, tm=tm, name="mixer_gate_bwd")
    dgt, dya, dyb, dsgu, datt = mixer_out_bwd(dm, ya, yb, pgt, wa, wb, wo, tm=tm_small, name="mixer_out_bwd")
    (dwo,) = mm_tn([(merged, dm)], tt=tt_half, name="mixer_dwo")
    (dwa,) = mm_tn([(sgu, dya)], tt=tt_half, name="mixer_dwa")
    (dwb,) = mm_tn([(att, dyb)], tt=tt_half, name="mixer_dwb")
    duv, dws, dzs, dgln, dbln = sgu_bwd(puv, dsgu, g_sgu_ln, b_sgu_ln, ws, bs_t, cpb=cpb, name="sgu_bwd")
    dqkv, dgq_h, dgk_h, dsk = attn_bwd(pqkv, datt, gq_t, gk_t, attn_sinks, seq=S, name="attn_bwd", comms=[chip3])
    dgq = dgq_h.reshape(N_Q, HEAD_DIM).sum(axis=0, keepdims=True)
    dgk = dgk_h.reshape(N_KV, HEAD_DIM).sum(axis=0, keepdims=True)
    share3 = rs_share_comm(halves_of(chip3, "ffn2"))
    dwin_parts = mm_tn([(duv, xn2), (dqkv, xn2), (dgt, xn2)], tt=tm, name="mixer_dwin")
    dwin4 = jnp.concatenate(dwin_parts, axis=0).reshape(win4.shape)
    pair_mix = rs_pair_comm([dwin4, _stack_cols(dwa), _stack_cols(dwb), dwo.reshape(N_QUAD, D // N_QUAD, D)])
    dh1, dsh2, dsc2, dgn2 = dx_norm_bwd([duv, dqkv, dgt], [w_uv, w_qkv, w_gt], h1, dh2, g_norm2, sc2, seq=S,
                                        tm=tm, name="mixer_dx", comms=[pair_mix])
    chip_mix = rs_chip_comm(sums_of(pair_mix, "mix"))

    df1, dga1 = gate_bwd(dh1, f1, ga1, 0.5, seq=S, tm=tm, name="ffn1_gate_bwd")
    dg1, du1 = ffn_bwd_act(df1, wd1, g1, u1, tm=tm_big, name="ffn1_act_bwd", comms=[share3])
    (dwd1,) = mm_tn([(a1, df1)], tt=T, name="ffn1_dwd")
    dwg1, dwu1 = mm_tn([(dg1, xn1), (du1, xn1)], tt=tt_half, name="ffn1_dwgu", comms=[chip_mix])
    pair1 = rs_pair_comm([dwg1, dwu1, dwd1])
    run_comms([pair1], name="rs_pair_ffn1")
    chip1 = rs_chip_comm(sums_of(pair1, "ffn1"))
    dx, dsh1, dsc1, dgn1 = dx_norm_bwd([dg1, du1], [wg1, wu1], xt, dh1, g_norm1, sc1, seq=S, tm=tm, name="ffn1_dx",
                                       comms=[chip1])
    share_rest = rs_share_comm(halves_of(chip_mix, "mix") + halves_of(chip1, "ffn1"))
    run_comms([share_rest], name="rs_share_rest")
    r_win, r_wa, r_wb, r_wo, r_wg1, r_wu1, r_wd1 = share_rest.bufs_out
    r_wg3, r_wu3, r_wd3 = share3.bufs_out

    dmods = jnp.concatenate([dsh1, dsc1, dga1, dsh2, dsc2, dga2, dsh3, dsc3, dga3], axis=1)
    dmods_all = all_gather8(dmods, name="gather_dmods").reshape(N_DEV * B, n_mod * D)
    dmods_mine = lax.dynamic_slice(dmods_all, (0, quad * n_ada), (N_DEV * B, n_ada))
    db_ada, dw_ada = ada_bwd(c_all, dmods_all, dmods_mine, name="ada_bwd")

    db_s = dzs.reshape(CHUNK, N_GROUPS, D_A // N_GROUPS).sum(axis=2).T.reshape(1, N_GROUPS * CHUNK)
    tail = jnp.concatenate([db_s, dgq, dgk, dsk[0:1, 0:N_Q], loss_part.reshape(1, 1)], axis=1)
    tail = jnp.pad(tail, ((0, 0), (0, (-tail.shape[1]) % D)))
    lnrow = jnp.concatenate([dgln, dbln], axis=1)
    lnrow = jnp.pad(lnrow, ((0, 0), (0, (-lnrow.shape[1]) % D)))
    packed = jnp.concatenate([dgn1, dgn2, dgn3, lnrow.reshape(-1, D), tail.reshape(-1, D), dws.reshape(-1, D)], axis=0)
    n_rows = packed.shape[0]
    packed = jnp.pad(packed, ((0, (-n_rows) % 8), (0, 0)))
    small = all_gather8(packed, name="reduce_small", reduce=True)
    ln_rows = lnrow.size // D
    tail_rows = tail.size // D
    r = 3
    g_gn1, g_gn2, g_gn3 = small[0:1], small[1:2], small[2:3]
    ln_flat = small[r:r + ln_rows].reshape(1, -1)
    r += ln_rows
    tail_flat = small[r:r + tail_rows].reshape(1, -1)
    r += tail_rows
    g_ws = small[r:r + dws.size // D].reshape(w_spatial.shape)
    g_gln, g_bln = ln_flat[:, 0:D_A], ln_flat[:, D_A:2 * D_A]
    o = N_GROUPS * CHUNK
    g_bs = tail_flat[:, 0:o].reshape(b_spatial.shape)
    g_gq, g_gk, g_sk = tail_flat[:, o:o + HEAD_DIM], tail_flat[:, o + HEAD_DIM:o + 2 * HEAD_DIM], tail_flat[:, o + 2 * HEAD_DIM:o + 2 * HEAD_DIM + N_Q]
    loss = tail_flat[0, o + 2 * HEAD_DIM + N_Q]

    transposed = ("ffn1_w_gate", "ffn1_w_up", "w_in", "ffn2_w_gate", "ffn2_w_up")

    def update(name, w, g, m, v):
        if name in transposed:
            w, m, v = tr(w), tr(m), tr(v)
        shape = w.shape
        two_d = lambda a: a.reshape(-1, shape[-1])
        res = adamw(two_d(w), two_d(g), two_d(m), two_d(v), name=f"adamw_{name}")
        res = [a.reshape(shape) for a in [g] + list(res)]
        return [tr(a) for a in res] if name in transposed else res

    names = ["w_ada", "b_ada", "g_norm1", "ffn1_w_gate", "ffn1_w_up", "ffn1_w_down", "g_norm2", "w_in", "g_sgu_ln",
             "b_sgu_ln", "w_spatial", "b_spatial", "g_q", "g_k", "attn_sinks", "w_branch_a", "w_branch_b", "w_out",
             "g_norm3", "ffn2_w_gate", "ffn2_w_up", "ffn2_w_down"]
    weights = dict(zip(names, [w_ada, b_ada, g_norm1, ffn1_w_gate, ffn1_w_up, ffn1_w_down, g_norm2, w_in, g_sgu_ln,
                               b_sgu_ln, w_spatial, b_spatial, g_q, g_k, attn_sinks, w_branch_a, w_branch_b, w_out,
                               g_norm3, ffn2_w_gate, ffn2_w_up, ffn2_w_down]))
    m_in = dict(zip(names, [m_w_ada, m_b_ada, m_g_norm1, m_ffn1_w_gate, m_ffn1_w_up, m_ffn1_w_down, m_g_norm2, m_w_in,
                            m_g_sgu_ln, m_b_sgu_ln, m_w_spatial, m_b_spatial, m_g_q, m_g_k, m_attn_sinks, m_w_branch_a,
                            m_w_branch_b, m_w_out, m_g_norm3, m_ffn2_w_gate, m_ffn2_w_up, m_ffn2_w_down]))
    v_in = dict(zip(names, [v_w_ada, v_b_ada, v_g_norm1, v_ffn1_w_gate, v_ffn1_w_up, v_ffn1_w_down, v_g_norm2, v_w_in,
                            v_g_sgu_ln, v_b_sgu_ln, v_w_spatial, v_b_spatial, v_g_q, v_g_k, v_attn_sinks, v_w_branch_a,
                            v_w_branch_b, v_w_out, v_g_norm3, v_ffn2_w_gate, v_ffn2_w_up, v_ffn2_w_down]))
    grads = {
        "w_ada": dw_ada[None], "b_ada": db_ada, "g_norm1": g_gn1, "g_norm2": g_gn2, "g_norm3": g_gn3,
        "g_sgu_ln": g_gln, "b_sgu_ln": g_bln, "w_spatial": g_ws, "b_spatial": g_bs, "g_q": g_gq, "g_k": g_gk,
        "attn_sinks": g_sk,
        "ffn1_w_gate": r_wg1[None], "ffn1_w_up": r_wu1[None], "ffn1_w_down": r_wd1[None], "w_in": r_win[None],
        "w_branch_a": r_wa[None], "w_branch_b": r_wb[None], "w_out": r_wo[None],
        "ffn2_w_gate": r_wg3[None], "ffn2_w_up": r_wu3[None], "ffn2_w_down": r_wd3[None],
    }

    small_names = ["b_ada", "g_norm1", "g_norm2", "g_norm3", "g_sgu_ln", "b_sgu_ln", "w_spatial", "b_spatial", "g_q",
                   "g_k", "attn_sinks"]

    def pack(d):
        flat = jnp.concatenate([d[nm].reshape(-1) for nm in small_names])
        return jnp.pad(flat, (0, (-flat.size) % (8 * 128))).reshape(-1, 128)

    sd, sm, sv = adamw(pack(weights), pack(grads), pack(m_in), pack(v_in), name="adamw_small")
    delta, new_m, new_v = {}, {}, {}
    off = 0
    for nm in small_names:
        size, shape = weights[nm].size, weights[nm].shape
        delta[nm] = sd.reshape(-1)[off:off + size].reshape(shape)
        new_m[nm] = sm.reshape(-1)[off:off + size].reshape(shape)
        new_v[nm] = sv.reshape(-1)[off:off + size].reshape(shape)
        off += size
    for nm in names:
        if nm not in small_names:
            grads[nm], delta[nm], new_m[nm], new_v[nm] = update(nm, weights[nm], grads[nm], m_in[nm], v_in[nm])
        else:
            grads[nm] = grads[nm].reshape(weights[nm].shape)

    return (loss, dx.reshape(B, S, D), *[grads[nm] for nm in names], *[delta[nm] for nm in names],
            *[new_m[nm] for nm in names], *[new_v[nm] for nm in names])
```

```python
import functools
import math
import operator

import jax
import jax.numpy as jnp
from jax import lax
from jax.experimental import pallas as pl
from jax.experimental.pallas import tpu as pltpu

F32 = jnp.float32
BF16 = jnp.bfloat16
EPS = 1e-6
NEG = -1e30
N_DEV = 8
N_QUAD = 4
D_A = 512
N_GROUPS = 4
CHUNK = 128
HEAD_DIM = 64
N_KV = 2
Q_PER_KV = 4
N_Q = N_KV * Q_PER_KV
D_B = N_Q * HEAD_DIM
QKV_W = D_B + 2 * N_KV * HEAD_DIM
K_OFF = D_B
V_OFF = D_B + N_KV * HEAD_DIM
ATT_SCALE = HEAD_DIM ** -0.5
GELU_K = math.sqrt(2.0 / math.pi)
GELU_C = 0.044715
ADAM_LR, ADAM_B1, ADAM_B2, ADAM_EPS, ADAM_WD, ADAM_STEP = 0.001, 0.9, 0.999, 1e-08, 0.01, 10
VMEM_LIMIT_BYTES = 56 * 1024 * 1024
MESH = pl.DeviceIdType.MESH
NT = (((1,), (1,)), ((), ()))
TN = (((0,), (0,)), ((), ()))
ANY = pl.BlockSpec(memory_space=pl.ANY)


def _dot(a, b):
    return jnp.dot(a, b, preferred_element_type=F32)


def _dg(a, b, dims):
    return lax.dot_general(a, b, dims, preferred_element_type=F32)


def _gelu_parts(x):
    t = jnp.tanh(GELU_K * (x + GELU_C * x * x * x))
    return 0.5 * x * (1.0 + t), t


def _dgelu(x, t):
    return 0.5 * (1.0 + t) + 0.5 * x * (1.0 - t * t) * (GELU_K * (1.0 + 3.0 * GELU_C * x * x))


def _row_block(rows, target):
    b = min(rows, target)
    while rows % b or b % 8:
        b -= 1
    return b


def _resident(a):
    return pl.BlockSpec(a.shape, lambda *_: (0,) * a.ndim, pipeline_mode=pl.Buffered(1))


class Comm:
    def __init__(self, *, srcs=(), bufs=(), land_shapes=(), n_sems, start, finish):
        self.srcs, self.bufs, self.land_shapes = list(srcs), list(bufs), list(land_shapes)
        self.n_sems, self.start, self.finish = n_sems, start, finish
        self.bufs_out, self.lands = None, None


def _call(body, *, name, grid, in_specs, out_specs, out_shape, args, scratch_shapes=(), comms=()):
    in_specs, out_specs, out_shape = list(in_specs), list(out_specs), list(out_shape)
    args, scratch = list(args), list(scratch_shapes)
    n_in, n_out, n_scr = len(args), len(out_shape), len(scratch)
    aliases, layout = {}, []
    for cm in comms:
        i0, o0, s0 = len(args), len(out_shape), len(scratch)
        args += cm.srcs + cm.bufs
        in_specs += [ANY] * (len(cm.srcs) + len(cm.bufs))
        out_shape += [jax.ShapeDtypeStruct(b.shape, b.dtype) for b in cm.bufs] + cm.land_shapes
        out_specs += [ANY] * (len(cm.bufs) + len(cm.land_shapes))
        for j in range(len(cm.bufs)):
            aliases[i0 + len(cm.srcs) + j] = o0 + j
        scratch += [pltpu.SemaphoreType.DMA((cm.n_sems,)), pltpu.SemaphoreType.DMA((cm.n_sems,))]
        layout.append((i0, o0, s0))
    n_in_all, n_out_all = len(args), len(out_shape)

    def wrapped(*refs):
        ins, outs, scr = refs[:n_in_all], refs[n_in_all:n_in_all + n_out_all], refs[n_in_all + n_out_all:]
        parts = []
        for cm, (i0, o0, s0) in zip(comms, layout):
            nb = len(cm.bufs)
            parts.append((ins[i0:i0 + len(cm.srcs)], outs[o0:o0 + nb], outs[o0 + nb:o0 + nb + len(cm.land_shapes)],
                          scr[s0], scr[s0 + 1]))
        if comms and grid:
            ids = [pl.program_id(d) for d in range(len(grid))]
            first = functools.reduce(operator.and_, [i == 0 for i in ids])
            last = functools.reduce(operator.and_, [i == g - 1 for i, g in zip(ids, grid)])

            @pl.when(first)
            def _():
                for cm, p in zip(comms, parts):
                    cm.start(*p)
        elif comms:
            for cm, p in zip(comms, parts):
                cm.start(*p)
        if body is not None:
            body(*ins[:n_in], *outs[:n_out], *scr[:n_scr])
        if comms and grid:
            @pl.when(last)
            def _():
                for cm, p in zip(comms, parts):
                    cm.finish(*p)
        elif comms:
            for cm, p in zip(comms, parts):
                cm.finish(*p)

    kwargs = dict(grid=grid) if grid else {}
    sem = ("arbitrary",) * len(grid)
    res = pl.pallas_call(
        wrapped, name=name, in_specs=in_specs, out_specs=out_specs, out_shape=out_shape, scratch_shapes=scratch,
        input_output_aliases=aliases,
        compiler_params=pltpu.CompilerParams(dimension_semantics=sem, vmem_limit_bytes=VMEM_LIMIT_BYTES), **kwargs,
    )(*args)
    for cm, (i0, o0, s0) in zip(comms, layout):
        nb = len(cm.bufs)
        cm.bufs_out = list(res[o0:o0 + nb])
        cm.lands = list(res[o0 + nb:o0 + nb + len(cm.land_shapes)])
    return list(res[:n_out])


def run_comms(comms, *, name):
    _call(None, name=name, grid=(), in_specs=[], out_specs=[], out_shape=[], args=[], comms=comms)


def norm_mod_fwd(h, g, sc, sh, *, seq, tm, name, comms=()):
    T, D = h.shape
    B, tps = T // seq, seq // tm

    def body(h_ref, g_ref, sc_ref, sh_ref, o_ref):
        x = h_ref[...]
        r = lax.rsqrt(jnp.mean(x * x, axis=-1, keepdims=True) + EPS)
        y = x * r * g_ref[...]
        o_ref[...] = (y * (1.0 + sc_ref[0]) + sh_ref[0]).astype(o_ref.dtype)

    per_seq = pl.BlockSpec((1, 1, D), lambda i: (i // tps, 0, 0))
    return _call(
        body, name=name, grid=(T // tm,),
        in_specs=[pl.BlockSpec((tm, D), lambda i: (i, 0)), pl.BlockSpec((1, D), lambda i: (0, 0)), per_seq, per_seq],
        out_specs=[pl.BlockSpec((tm, D), lambda i: (i, 0))], out_shape=[jax.ShapeDtypeStruct((T, D), BF16)],
        args=[h, g, sc.reshape(B, 1, D), sh.reshape(B, 1, D)], comms=comms)[0]


def ffn_up(xn, wg, wu, *, tm, name, comms=()):
    T, D = xn.shape
    nq, fs, _ = wg.shape

    def body(x_ref, wg_ref, wu_ref, s_ref, q_ref, a_ref):
        x = x_ref[...]
        g = _dg(x, wg_ref[0], NT)
        u = _dg(x, wu_ref[0], NT)
        sg = jax.nn.sigmoid(g)
        s = g * sg
        s_ref[0] = s.astype(BF16)
        q_ref[0] = (u * (sg + s * (1.0 - sg))).astype(BF16)
        a_ref[0] = (s * u).astype(BF16)

    w_spec = pl.BlockSpec((1, fs, D), lambda q, i: (q, 0, 0))
    o_spec = pl.BlockSpec((1, tm, fs), lambda q, i: (q, i, 0))
    o_shape = jax.ShapeDtypeStruct((nq, T, fs), BF16)
    return _call(
        body, name=name, grid=(nq, T // tm),
        in_specs=[pl.BlockSpec((tm, D), lambda q, i: (i, 0)), w_spec, w_spec],
        out_specs=[o_spec, o_spec, o_spec], out_shape=[o_shape, o_shape, o_shape], args=[xn, wg, wu], comms=comms)


def ffn_down(a, wd, hin, ga, target=None, *, seq, tm, name, comms=()):
    nq, T, fs = a.shape
    D = wd.shape[-1]
    B, tps, nt = T // seq, seq // tm, T // tm
    with_loss = target is not None

    def body(*refs):
        if with_loss:
            a_ref, wd_ref, h_ref, ga_ref, t_ref, o_ref, f_ref, l_ref = refs
        else:
            a_ref, wd_ref, h_ref, ga_ref, o_ref, f_ref = refs
        f = _dot(a_ref[0], wd_ref[0])
        for q in range(1, nq):
            f = f + _dot(a_ref[q], wd_ref[q])
        f_ref[...] = f.astype(BF16)
        y = h_ref[...] + (0.5 * ga_ref[0]) * f
        if with_loss:
            e = y - t_ref[...]
            o_ref[...] = e * (1.0 / D)
            l_ref[...] = jnp.full(l_ref.shape, 0.5 * jnp.sum(e * e) * (1.0 / D), F32)
        else:
            o_ref[...] = y

    tile = pl.BlockSpec((tm, D), lambda i: (i, 0))
    in_specs = [pl.BlockSpec((nq, tm, fs), lambda i: (0, i, 0)), _resident(wd),
                tile, pl.BlockSpec((1, 1, D), lambda i: (i // tps, 0, 0))]
    out_specs = [tile, tile]
    out_shape = [jax.ShapeDtypeStruct((T, D), F32), jax.ShapeDtypeStruct((T, D), BF16)]
    args = [a, wd, hin, ga.reshape(B, 1, D)]
    if with_loss:
        in_specs.append(tile)
        args.append(target)
        out_specs.append(pl.BlockSpec((1, 8, 128), lambda i: (i, 0, 0)))
        out_shape.append(jax.ShapeDtypeStruct((nt, 8, 128), F32))
    return _call(body, name=name, grid=(nt,), in_specs=in_specs, out_specs=out_specs, out_shape=out_shape, args=args,
                 comms=comms)


def proj_fwd(xn, ws, *, tm, name, comms=()):
    T, D = xn.shape
    n = len(ws)

    def body(*refs):
        x = refs[0][...]
        for j in range(n):
            refs[1 + n + j][...] = _dg(x, refs[1 + j][...], NT)

    return _call(
        body, name=name, grid=(T // tm,),
        in_specs=[pl.BlockSpec((tm, D), lambda i: (i, 0))] + [pl.BlockSpec(w.shape, lambda i: (0, 0)) for w in ws],
        out_specs=[pl.BlockSpec((tm, w.shape[0]), lambda i: (i, 0)) for w in ws],
        out_shape=[jax.ShapeDtypeStruct((T, w.shape[0]), F32) for w in ws], args=[xn, *ws], comms=comms)


def _layer_norm_parts(v):
    mu = jnp.mean(v, axis=-1, keepdims=True)
    d = v - mu
    rstd = lax.rsqrt(jnp.mean(d * d, axis=-1, keepdims=True) + EPS)
    return d * rstd, rstd


def _causal():
    row = lax.broadcasted_iota(jnp.int32, (CHUNK, CHUNK), 0)
    col = lax.broadcasted_iota(jnp.int32, (CHUNK, CHUNK), 1)
    return row >= col


def sgu_fwd(puv, gln, bln, ws, bs_t, *, cpb, name, comms=()):
    T = puv.shape[0]
    gd = D_A // N_GROUPS
    tm = cpb * CHUNK

    def body(p_ref, gln_ref, bln_ref, ws_ref, bs_ref, o_ref):
        causal = _causal()
        wm = [jnp.where(causal, ws_ref[g], 0.0).astype(BF16) for g in range(N_GROUPS)]
        for j in range(cpb):
            rows = slice(j * CHUNK, (j + 1) * CHUNK)
            u, _ = _gelu_parts(p_ref[rows, 0:D_A])
            vv, _ = _gelu_parts(p_ref[rows, D_A:2 * D_A])
            vhat, _ = _layer_norm_parts(vv)
            vn = (vhat * gln_ref[...] + bln_ref[...]).astype(BF16)
            for g in range(N_GROUPS):
                cols = slice(g * gd, (g + 1) * gd)
                z = _dot(wm[g], vn[:, cols]) + bs_ref[:, g:g + 1]
                o_ref[rows, cols] = (u[:, cols] * z).astype(BF16)

    full = lambda a: pl.BlockSpec(a.shape, lambda i: (0,) * a.ndim)
    return _call(
        body, name=name, grid=(T // tm,),
        in_specs=[pl.BlockSpec((tm, 2 * D_A), lambda i: (i, 0)), full(gln), full(bln), full(ws), full(bs_t)],
        out_specs=[pl.BlockSpec((tm, D_A), lambda i: (i, 0))], out_shape=[jax.ShapeDtypeStruct((T, D_A), BF16)],
        args=[puv, gln, bln, ws, bs_t], comms=comms)[0]


def _rms_parts(x):
    r = lax.rsqrt(jnp.mean(x * x, axis=-1, keepdims=True) + EPS)
    return x * r, r


KV_W = Q_PER_KV * HEAD_DIM
KV2 = N_KV * HEAD_DIM
STACK = Q_PER_KV * CHUNK


def _iota(shape, dim):
    return lax.broadcasted_iota(jnp.int32, shape, dim)


def _head_mean_matrix(n):
    return jnp.where((_iota((n, n), 0) >> 6) == (_iota((n, n), 1) >> 6), 1.0 / HEAD_DIM, 0.0).astype(BF16)


def _repeat_matrix(h):
    return jnp.where(_iota((KV2, KV_W), 0) == h * HEAD_DIM + (_iota((KV2, KV_W), 1) & (HEAD_DIM - 1)), 1.0, 0.0).astype(BF16)


def _fold_matrix(h):
    j, l = _iota((KV_W, KV2), 0), _iota((KV_W, KV2), 1)
    return jnp.where(((j & (HEAD_DIM - 1)) == (l & (HEAD_DIM - 1))) & ((l >> 6) == h), 1.0, 0.0).astype(BF16)


def _dot_split(x, m):
    hi = x.astype(BF16)
    lo = (x - hi.astype(F32)).astype(BF16)
    return _dot(hi, m) + _dot(lo, m)


def _head_rms(x, mean_matrix):
    r = lax.rsqrt(_dot_split(x * x, mean_matrix) + EPS)
    return x * r, r


def _stack_heads(x):
    head = _iota(x.shape, 1) >> 6
    return jnp.concatenate([jnp.where(head == g, x, 0.0).astype(BF16) for g in range(Q_PER_KV)], axis=0)


def _unstack_heads(y):
    head = _iota((CHUNK, KV_W), 1) >> 6
    out = jnp.where(head == 0, y[0:CHUNK], 0.0)
    for g in range(1, Q_PER_KV):
        out = out + jnp.where(head == g, y[g * CHUNK:(g + 1) * CHUNK], 0.0)
    return out


def _attn_mask(i):
    qi = _iota((STACK, 2 * CHUNK), 0) & (CHUNK - 1)
    kj = _iota((STACK, 2 * CHUNK), 1)
    diff = qi + CHUNK - kj
    return (diff >= 0) & (diff < CHUNK) & ((kj >= CHUNK) | (i > 0))


def _sink_column(sink_ref, h):
    blk = _iota((STACK, 1), 0) >> 7
    col = jnp.full((STACK, 1), sink_ref[0, h * Q_PER_KV], F32)
    for g in range(1, Q_PER_KV):
        col = jnp.where(blk == g, sink_ref[0, h * Q_PER_KV + g], col)
    return col


def _attn_probs(qs, kk, valid, sink):
    s = _dg(qs, kk, NT) * ATT_SCALE
    s = jnp.where(valid, s, NEG)
    m = jnp.maximum(jnp.max(s, axis=-1, keepdims=True), sink)
    p = jnp.exp(s - m)
    es = jnp.exp(sink - m)
    inv = 1.0 / (jnp.sum(p, axis=-1, keepdims=True) + es)
    return p * inv, es * inv


def _fill_keys(p_ref, gk_ref, krep, vrep, seq):
    mean_k = _head_mean_matrix(KV2)
    reps = [_repeat_matrix(h) for h in range(N_KV)]
    for h in range(N_KV):
        krep[h][0:CHUNK, :] = jnp.zeros((CHUNK, KV_W), BF16)
        vrep[h][0:CHUNK, :] = jnp.zeros((CHUNK, KV_W), BF16)
    rows = min(seq, 4 * CHUNK)

    def piece(j, carry):
        r0 = pl.multiple_of(j * rows, CHUNK)
        nk, _ = _head_rms(p_ref[pl.ds(r0, rows), K_OFF:K_OFF + KV2], mean_k)
        kn = (nk * gk_ref[...]).astype(BF16)
        v = p_ref[pl.ds(r0, rows), V_OFF:V_OFF + KV2].astype(BF16)
        r1 = pl.multiple_of(r0 + CHUNK, CHUNK)
        for h in range(N_KV):
            krep[h][pl.ds(r1, rows), :] = _dot(kn, reps[h]).astype(BF16)
            vrep[h][pl.ds(r1, rows), :] = _dot(v, reps[h]).astype(BF16)
        return carry

    lax.fori_loop(0, seq // rows, piece, 0)


def attn_fwd(pqkv, gq_t, gk_t, sinks, *, seq, name, comms=()):
    T = pqkv.shape[0]
    nb = seq // CHUNK

    def body(p_ref, gq_ref, gk_ref, sink_ref, o_ref, k0, k1, v0, v1):
        krep, vrep = [k0, k1], [v0, v1]
        _fill_keys(p_ref, gk_ref, krep, vrep, seq)
        mean_q = _head_mean_matrix(D_B)

        def block(i, carry):
            r0 = pl.multiple_of(i * CHUNK, CHUNK)
            nq, _ = _head_rms(p_ref[pl.ds(r0, CHUNK), 0:D_B], mean_q)
            qn = nq * gq_ref[...]
            valid = _attn_mask(i)
            for h in range(N_KV):
                qs = _stack_heads(qn[:, h * KV_W:(h + 1) * KV_W])
                kk = krep[h][pl.ds(r0, 2 * CHUNK), :]
                vv = vrep[h][pl.ds(r0, 2 * CHUNK), :]
                probs, _ = _attn_probs(qs, kk, valid, _sink_column(sink_ref, h))
                out = _unstack_heads(_dot(probs.astype(BF16), vv))
                o_ref[pl.ds(r0, CHUNK), h * KV_W:(h + 1) * KV_W] = out.astype(BF16)
            return carry

        lax.fori_loop(0, nb, block, 0)

    return _call(
        body, name=name, grid=(T // seq,),
        in_specs=[pl.BlockSpec((seq, QKV_W), lambda b: (b, 0)), pl.BlockSpec(gq_t.shape, lambda b: (0, 0)),
                  pl.BlockSpec(gk_t.shape, lambda b: (0, 0)), pl.BlockSpec(memory_space=pltpu.SMEM)],
        out_specs=[pl.BlockSpec((seq, D_B), lambda b: (b, 0))], out_shape=[jax.ShapeDtypeStruct((T, D_B), BF16)],
        scratch_shapes=[pltpu.VMEM((seq + CHUNK, KV_W), BF16)] * 4,
        args=[pqkv, gq_t, gk_t, sinks], comms=comms)[0]


def mixer_out_fwd(sgu, att, pg, hin, ga, wa, wb, wo, *, seq, tm, name, comms=()):
    T, D = hin.shape
    B, tps = T // seq, seq // tm

    def body(s_ref, t_ref, pg_ref, h_ref, ga_ref, wa_ref, wb_ref, wo_ref, ya_ref, yb_ref, mg_ref, m_ref, ho_ref):
        ya = _dot(s_ref[...], wa_ref[...])
        yb = _dot(t_ref[...], wb_ref[...])
        merged = jax.nn.sigmoid(pg_ref[:, 0:D]) * ya + jax.nn.sigmoid(pg_ref[:, D:2 * D]) * yb
        mg = merged.astype(BF16)
        m = _dot(mg, wo_ref[...])
        ya_ref[...] = ya.astype(BF16)
        yb_ref[...] = yb.astype(BF16)
        mg_ref[...] = mg
        m_ref[...] = m.astype(BF16)
        ho_ref[...] = h_ref[...] + ga_ref[0] * m

    tile = lambda w: pl.BlockSpec((tm, w), lambda i: (i, 0))
    full = lambda a: pl.BlockSpec(a.shape, lambda i: (0, 0))
    b16 = jax.ShapeDtypeStruct((T, D), BF16)
    return _call(
        body, name=name, grid=(T // tm,),
        in_specs=[tile(D_A), tile(D_B), tile(2 * D), tile(D), pl.BlockSpec((1, 1, D), lambda i: (i // tps, 0, 0)),
                  full(wa), full(wb), full(wo)],
        out_specs=[tile(D)] * 5, out_shape=[b16, b16, b16, b16, jax.ShapeDtypeStruct((T, D), F32)],
        args=[sgu, att, pg, hin, ga.reshape(B, 1, D), wa, wb, wo], comms=comms)


def gate_bwd(dh, f, ga, scale, *, seq, tm, name, comms=()):
    T, D = dh.shape
    B, tps = T // seq, seq // tm

    def body(dh_ref, f_ref, ga_ref, df_ref, dga_ref):
        i = pl.program_id(0)
        d = dh_ref[...]
        df_ref[...] = ((scale * ga_ref[0]) * d).astype(BF16)
        part = scale * jnp.sum(d * f_ref[...].astype(F32), axis=0, keepdims=True)

        @pl.when(i % tps == 0)
        def _():
            dga_ref[0] = part

        @pl.when(i % tps != 0)
        def _():
            dga_ref[0] += part

    tile = pl.BlockSpec((tm, D), lambda i: (i, 0))
    per_seq = pl.BlockSpec((1, 1, D), lambda i: (i // tps, 0, 0))
    df, dga = _call(
        body, name=name, grid=(T // tm,), in_specs=[tile, tile, per_seq], out_specs=[tile, per_seq],
        out_shape=[jax.ShapeDtypeStruct((T, D), BF16), jax.ShapeDtypeStruct((B, 1, D), F32)],
        args=[dh, f, ga.reshape(B, 1, D)], comms=comms)
    return df, dga.reshape(B, D)


def ffn_bwd_act(df, wd, s, q, *, tm, name, comms=()):
    T, D = df.shape
    nq, fs, _ = wd.shape

    def body(df_ref, wd_ref, s_ref, q_ref, dg_ref, du_ref):
        da = _dg(df_ref[...], wd_ref[0], NT)
        du_ref[0] = (da * s_ref[0].astype(F32)).astype(BF16)
        dg_ref[0] = (da * q_ref[0].astype(F32)).astype(BF16)

    act = pl.BlockSpec((1, tm, fs), lambda q, i: (q, i, 0))
    o_shape = jax.ShapeDtypeStruct((nq, T, fs), BF16)
    return _call(
        body, name=name, grid=(nq, T // tm),
        in_specs=[pl.BlockSpec((tm, D), lambda q, i: (i, 0)), pl.BlockSpec((1, fs, D), lambda q, i: (q, 0, 0)), act, act],
        out_specs=[act, act], out_shape=[o_shape, o_shape], args=[df, wd, s, q], comms=comms)


def mm_tn(pairs, *, tt, name, comms=()):
    n = len(pairs)
    flat = []
    for pair in pairs:
        for a in pair:
            if not any(a is b for b in flat):
                flat.append(a)
    where = lambda a: [k for k, b in enumerate(flat) if a is b][0]
    nq = max([a.shape[0] for a in flat if a.ndim == 3] + [1])
    T = flat[0].shape[-2]
    tt = min(tt, T)
    nt = T // tt
    stacked = [x.ndim == 3 or y.ndim == 3 for x, y in pairs]

    def body(*refs):
        in_refs, o_refs, accs = refs[:len(flat)], refs[len(flat):len(flat) + n], refs[len(flat) + n:]
        t = pl.program_id(1)
        value = lambda a: in_refs[where(a)][0] if a.ndim == 3 else in_refs[where(a)][...]

        def put(j, v):
            if stacked[j]:
                o_refs[j][0] = v.astype(BF16)
            else:
                o_refs[j][...] = v.astype(BF16)

        for j, (x, y) in enumerate(pairs):
            part = _dg(value(x), value(y), TN)
            if nt == 1:
                put(j, part)
                continue

            @pl.when(t == 0)
            def _():
                accs[j][...] = part

            @pl.when(t != 0)
            def _():
                accs[j][...] += part

        if nt > 1:
            @pl.when(t == nt - 1)
            def _():
                for j in range(n):
                    put(j, accs[j][...])

    def spec(a):
        if a.ndim == 3:
            return pl.BlockSpec((1, tt, a.shape[2]), lambda q, t: (q, t, 0))
        return pl.BlockSpec((tt, a.shape[1]), lambda q, t: (t, 0))

    out_specs, out_shape = [], []
    for j, (x, y) in enumerate(pairs):
        K, N = x.shape[-1], y.shape[-1]
        if stacked[j]:
            out_specs.append(pl.BlockSpec((1, K, N), lambda q, t: (q, 0, 0)))
            out_shape.append(jax.ShapeDtypeStruct((nq, K, N), BF16))
        else:
            out_specs.append(pl.BlockSpec((K, N), lambda q, t: (0, 0)))
            out_shape.append(jax.ShapeDtypeStruct((K, N), BF16))
    return _call(
        body, name=name, grid=(nq, nt), in_specs=[spec(a) for a in flat],
        out_specs=out_specs, out_shape=out_shape,
        scratch_shapes=[pltpu.VMEM((x.shape[-1], y.shape[-1]), F32) for x, y in pairs] if nt > 1 else [],
        args=flat, comms=comms)


def dx_norm_bwd(dys, ws, hin, dh_out, g, sc, *, seq, tm, name, comms=()):
    T, D = hin.shape
    B, tps = T // seq, seq // tm
    n = len(dys)

    def body(*refs):
        dy_refs, w_refs = refs[:n], refs[n:2 * n]
        h_ref, dho_ref, g_ref, sc_ref, dhi_ref, dsh_ref, dsc_ref, dgn_ref = refs[2 * n:]
        i = pl.program_id(0)
        dxn = None
        for j in range(n):
            if dys[j].ndim == 3:
                for q in range(dys[j].shape[0]):
                    part = _dot(dy_refs[j][q], w_refs[j][q])
                    dxn = part if dxn is None else dxn + part
            else:
                part = _dot(dy_refs[j][...], w_refs[j][...])
                dxn = part if dxn is None else dxn + part
        x = h_ref[...]
        nx, r = _rms_parts(x)
        gain = g_ref[...]
        one_sc = 1.0 + sc_ref[0]
        dsh = jnp.sum(dxn, axis=0, keepdims=True)
        dsc = jnp.sum(dxn * (nx * gain), axis=0, keepdims=True)
        dgn = jnp.sum(dxn * one_sc * nx, axis=0, keepdims=True)
        dn = dxn * one_sc * gain
        dhi_ref[...] = dho_ref[...] + r * (dn - nx * jnp.mean(dn * nx, axis=-1, keepdims=True))

        @pl.when(i % tps == 0)
        def _():
            dsh_ref[0] = dsh
            dsc_ref[0] = dsc

        @pl.when(i % tps != 0)
        def _():
            dsh_ref[0] += dsh
            dsc_ref[0] += dsc

        @pl.when(i == 0)
        def _():
            dgn_ref[...] = dgn

        @pl.when(i != 0)
        def _():
            dgn_ref[...] += dgn

    def dy_spec(a):
        if a.ndim == 3:
            return pl.BlockSpec((a.shape[0], tm, a.shape[2]), lambda i: (0, i, 0))
        return pl.BlockSpec((tm, a.shape[1]), lambda i: (i, 0))

    tile = pl.BlockSpec((tm, D), lambda i: (i, 0))
    per_seq = pl.BlockSpec((1, 1, D), lambda i: (i // tps, 0, 0))
    row = pl.BlockSpec((1, D), lambda i: (0, 0))
    dhi, dsh, dsc, dgn = _call(
        body, name=name, grid=(T // tm,),
        in_specs=[dy_spec(a) for a in dys] + [_resident(w) for w in ws] + [tile, tile, row, per_seq],
        out_specs=[tile, per_seq, per_seq, row],
        out_shape=[jax.ShapeDtypeStruct((T, D), F32), jax.ShapeDtypeStruct((B, 1, D), F32),
                   jax.ShapeDtypeStruct((B, 1, D), F32), jax.ShapeDtypeStruct((1, D), F32)],
        args=[*dys, *ws, hin, dh_out, g, sc.reshape(B, 1, D)], comms=comms)
    return dhi, dsh.reshape(B, D), dsc.reshape(B, D), dgn


def mixer_out_bwd(dm, ya, yb, pg, wa, wb, wo, *, tm, name, comms=()):
    T, D = dm.shape

    def body(dm_ref, ya_ref, yb_ref, pg_ref, wa_ref, wb_ref, wo_ref, dg_ref, dya_ref, dyb_ref, ds_ref, dt_ref):
        dmg = _dg(dm_ref[...], wo_ref[...], NT)
        sa = jax.nn.sigmoid(pg_ref[:, 0:D])
        sb = jax.nn.sigmoid(pg_ref[:, D:2 * D])
        dg_ref[:, 0:D] = (dmg * ya_ref[...].astype(F32) * (sa * (1.0 - sa))).astype(BF16)
        dg_ref[:, D:2 * D] = (dmg * yb_ref[...].astype(F32) * (sb * (1.0 - sb))).astype(BF16)
        dya = (dmg * sa).astype(BF16)
        dyb = (dmg * sb).astype(BF16)
        dya_ref[...] = dya
        dyb_ref[...] = dyb
        ds_ref[...] = _dg(dya, wa_ref[...], NT)
        dt_ref[...] = _dg(dyb, wb_ref[...], NT)

    tile = lambda w: pl.BlockSpec((tm, w), lambda i: (i, 0))
    full = lambda a: pl.BlockSpec(a.shape, lambda i: (0, 0))
    return _call(
        body, name=name, grid=(T // tm,),
        in_specs=[tile(D), tile(D), tile(D), tile(2 * D), full(wa), full(wb), full(wo)],
        out_specs=[tile(2 * D), tile(D), tile(D), tile(D_A), tile(D_B)],
        out_shape=[jax.ShapeDtypeStruct((T, 2 * D), BF16), jax.ShapeDtypeStruct((T, D), BF16),
                   jax.ShapeDtypeStruct((T, D), BF16), jax.ShapeDtypeStruct((T, D_A), F32),
                   jax.ShapeDtypeStruct((T, D_B), F32)],
        args=[dm, ya, yb, pg, wa, wb, wo], comms=comms)


def sgu_bwd(puv, dsgu, gln, bln, ws, bs_t, *, cpb, name, comms=()):
    T = puv.shape[0]
    gd = D_A // N_GROUPS
    tm = cpb * CHUNK

    def body(p_ref, ds_ref, gln_ref, bln_ref, ws_ref, bs_ref, d_ref, dws_ref, dz_ref, dgl_ref, dbl_ref):
        i = pl.program_id(0)
        causal = _causal()
        wf = [jnp.where(causal, ws_ref[g], 0.0) for g in range(N_GROUPS)]
        wm = [w.astype(BF16) for w in wf]
        wt = [w.T.astype(BF16) for w in wf]
        dws = [jnp.zeros((CHUNK, CHUNK), F32) for _ in range(N_GROUPS)]
        dzs = jnp.zeros((CHUNK, D_A), F32)
        dgl = jnp.zeros((1, D_A), F32)
        dbl = jnp.zeros((1, D_A), F32)
        for j in range(cpb):
            rows = slice(j * CHUNK, (j + 1) * CHUNK)
            au = p_ref[rows, 0:D_A]
            av = p_ref[rows, D_A:2 * D_A]
            u, tu = _gelu_parts(au)
            vv, tv = _gelu_parts(av)
            vhat, rstd = _layer_norm_parts(vv)
            vn = (vhat * gln_ref[...] + bln_ref[...]).astype(BF16)
            dsg = ds_ref[rows, :]
            dz = dsg * u
            dzb = dz.astype(BF16)
            zs, dvns = [], []
            for g in range(N_GROUPS):
                cols = slice(g * gd, (g + 1) * gd)
                zs.append(_dot(wm[g], vn[:, cols]) + bs_ref[:, g:g + 1])
                dws[g] = dws[g] + _dg(dzb[:, cols], vn[:, cols], NT)
                dvns.append(_dot(wt[g], dzb[:, cols]))
            z = jnp.concatenate(zs, axis=1)
            dvn = jnp.concatenate(dvns, axis=1)
            d_ref[rows, 0:D_A] = (dsg * z * _dgelu(au, tu)).astype(BF16)
            dvh = dvn * gln_ref[...]
            dvv = rstd * (dvh - jnp.mean(dvh, axis=-1, keepdims=True)
                          - vhat * jnp.mean(dvh * vhat, axis=-1, keepdims=True))
            d_ref[rows, D_A:2 * D_A] = (dvv * _dgelu(av, tv)).astype(BF16)
            dzs = dzs + dz
            dgl = dgl + jnp.sum(dvn * vhat, axis=0, keepdims=True)
            dbl = dbl + jnp.sum(dvn, axis=0, keepdims=True)

        @pl.when(i == 0)
        def _():
            for g in range(N_GROUPS):
                dws_ref[g] = jnp.where(causal, dws[g], 0.0)
            dz_ref[...] = dzs
            dgl_ref[...] = dgl
            dbl_ref[...] = dbl

        @pl.when(i != 0)
        def _():
            for g in range(N_GROUPS):
                dws_ref[g] += jnp.where(causal, dws[g], 0.0)
            dz_ref[...] += dzs
            dgl_ref[...] += dgl
            dbl_ref[...] += dbl

    full = lambda a: pl.BlockSpec(a.shape, lambda i: (0,) * a.ndim)
    acc = lambda s: pl.BlockSpec(s, lambda i: (0,) * len(s))
    return _call(
        body, name=name, grid=(T // tm,),
        in_specs=[pl.BlockSpec((tm, 2 * D_A), lambda i: (i, 0)), pl.BlockSpec((tm, D_A), lambda i: (i, 0)),
                  full(gln), full(bln), full(ws), full(bs_t)],
        out_specs=[pl.BlockSpec((tm, 2 * D_A), lambda i: (i, 0)), acc((N_GROUPS, CHUNK, CHUNK)), acc((CHUNK, D_A)),
                   acc((1, D_A)), acc((1, D_A))],
        out_shape=[jax.ShapeDtypeStruct((T, 2 * D_A), BF16), jax.ShapeDtypeStruct((N_GROUPS, CHUNK, CHUNK), F32),
                   jax.ShapeDtypeStruct((CHUNK, D_A), F32), jax.ShapeDtypeStruct((1, D_A), F32),
                   jax.ShapeDtypeStruct((1, D_A), F32)],
        args=[puv, dsgu, gln, bln, ws, bs_t], comms=comms)


def attn_bwd(pqkv, datt, gq_t, gk_t, sinks, *, seq, name, comms=()):
    T = pqkv.shape[0]
    nb = seq // CHUNK

    def body(p_ref, do_ref, gq_ref, gk_ref, sink_ref, d_ref, dgq_ref, dgk_ref, dsk_ref,
             k0, k1, v0, v1, dk0, dk1, dv0, dv1, dgq_s, dsk_s):
        b = pl.program_id(0)
        krep, vrep, dkacc, dvacc = [k0, k1], [v0, v1], [dk0, dk1], [dv0, dv1]
        _fill_keys(p_ref, gk_ref, krep, vrep, seq)
        for h in range(N_KV):
            dkacc[h][...] = jnp.zeros_like(dkacc[h])
            dvacc[h][...] = jnp.zeros_like(dvacc[h])
        dgq_s[...] = jnp.zeros_like(dgq_s)
        dsk_s[...] = jnp.zeros_like(dsk_s)
        mean_q = _head_mean_matrix(D_B)
        lane = _iota((1, 128), 1)

        def block(i, carry):
            r0 = pl.multiple_of(i * CHUNK, CHUNK)
            nq, rq = _head_rms(p_ref[pl.ds(r0, CHUNK), 0:D_B], mean_q)
            qn = nq * gq_ref[...]
            valid = _attn_mask(i)
            dqn_parts = []
            for h in range(N_KV):
                cols = slice(h * KV_W, (h + 1) * KV_W)
                qs = _stack_heads(qn[:, cols])
                kk = krep[h][pl.ds(r0, 2 * CHUNK), :]
                vv = vrep[h][pl.ds(r0, 2 * CHUNK), :]
                probs, psink = _attn_probs(qs, kk, valid, _sink_column(sink_ref, h))
                dos = _stack_heads(do_ref[pl.ds(r0, CHUNK), cols])
                dp = _dg(dos, vv, NT)
                dr = jnp.sum(probs * dp, axis=-1, keepdims=True)
                ds = (probs * (dp - dr) * ATT_SCALE).astype(BF16)
                dsink = psink * dr
                for g in range(Q_PER_KV):
                    dsk_s[...] += jnp.where(lane == h * Q_PER_KV + g, -jnp.sum(dsink[g * CHUNK:(g + 1) * CHUNK]), 0.0)
                dqn_parts.append(_unstack_heads(_dot(ds, kk)))
                dkacc[h][pl.ds(r0, 2 * CHUNK), :] += _dg(ds, qs, TN)
                dvacc[h][pl.ds(r0, 2 * CHUNK), :] += _dg(probs.astype(BF16), dos, TN)
            dqn = jnp.concatenate(dqn_parts, axis=1)
            dn = dqn * gq_ref[...]
            d_ref[pl.ds(r0, CHUNK), 0:D_B] = (rq * (dn - nq * _dot_split(dn * nq, mean_q))).astype(BF16)
            dgq_s[...] += jnp.sum(dqn * nq, axis=0, keepdims=True)
            return carry

        lax.fori_loop(0, nb, block, 0)

        mean_k = _head_mean_matrix(KV2)
        folds = [_fold_matrix(h) for h in range(N_KV)]
        rows = min(seq, 4 * CHUNK)

        def piece(j, dgk):
            r0 = pl.multiple_of(j * rows, CHUNK)
            r1 = pl.multiple_of(r0 + CHUNK, CHUNK)
            dkn = _dot_split(dkacc[0][pl.ds(r1, rows), :], folds[0]) + _dot_split(dkacc[1][pl.ds(r1, rows), :], folds[1])
            dv = _dot_split(dvacc[0][pl.ds(r1, rows), :], folds[0]) + _dot_split(dvacc[1][pl.ds(r1, rows), :], folds[1])
            nk, rk = _head_rms(p_ref[pl.ds(r0, rows), K_OFF:K_OFF + KV2], mean_k)
            dn = dkn * gk_ref[...]
            d_ref[pl.ds(r0, rows), K_OFF:K_OFF + KV2] = (rk * (dn - nk * _dot_split(dn * nk, mean_k))).astype(BF16)
            d_ref[pl.ds(r0, rows), V_OFF:V_OFF + KV2] = dv.astype(BF16)
            return dgk + jnp.sum(dkn * nk, axis=0, keepdims=True)

        dgk = lax.fori_loop(0, seq // rows, piece, jnp.zeros((1, KV2), F32))

        @pl.when(b == 0)
        def _():
            dgq_ref[...] = dgq_s[...]
            dgk_ref[...] = dgk
            dsk_ref[...] = jnp.broadcast_to(dsk_s[...], dsk_ref.shape)

        @pl.when(b != 0)
        def _():
            dgq_ref[...] += dgq_s[...]
            dgk_ref[...] += dgk
            dsk_ref[...] += jnp.broadcast_to(dsk_s[...], dsk_ref.shape)

    acc = lambda s: pl.BlockSpec(s, lambda b: (0, 0))
    return _call(
        body, name=name, grid=(T // seq,),
        in_specs=[pl.BlockSpec((seq, QKV_W), lambda b: (b, 0)), pl.BlockSpec((seq, D_B), lambda b: (b, 0)),
                  pl.BlockSpec(gq_t.shape, lambda b: (0, 0)), pl.BlockSpec(gk_t.shape, lambda b: (0, 0)),
                  pl.BlockSpec(memory_space=pltpu.SMEM)],
        out_specs=[pl.BlockSpec((seq, QKV_W), lambda b: (b, 0)), acc((1, D_B)), acc((1, KV2)), acc((8, 128))],
        out_shape=[jax.ShapeDtypeStruct((T, QKV_W), BF16), jax.ShapeDtypeStruct((1, D_B), F32),
                   jax.ShapeDtypeStruct((1, KV2), F32), jax.ShapeDtypeStruct((8, 128), F32)],
        scratch_shapes=[pltpu.VMEM((seq + CHUNK, KV_W), BF16)] * 4 + [pltpu.VMEM((seq + CHUNK, KV_W), F32)] * 4
        + [pltpu.VMEM((1, D_B), F32), pltpu.VMEM((1, 128), F32)],
        args=[pqkv, datt, gq_t, gk_t, sinks], comms=comms)


def ada_fwd(c_all, w, b, *, name):
    nb, D = c_all.shape
    N = w.shape[1]
    tn = N // 3

    def body(c_ref, w_ref, b_ref, o_ref):
        c = c_ref[...]
        cond = (c * jax.nn.sigmoid(c)).astype(BF16)
        o_ref[...] = _dot(cond, w_ref[...].astype(BF16)) + b_ref[...]

    return _call(
        body, name=name, grid=(3,),
        in_specs=[pl.BlockSpec((nb, D), lambda j: (0, 0)), pl.BlockSpec((D, tn), lambda j: (0, j)),
                  pl.BlockSpec((1, tn), lambda j: (0, j))],
        out_specs=[pl.BlockSpec((nb, tn), lambda j: (0, j))], out_shape=[jax.ShapeDtypeStruct((nb, N), F32)],
        args=[c_all, w, b])[0]


def ada_bwd(c_all, dmods_all, dmods_mine, *, name):
    nb, D = c_all.shape
    NA = dmods_all.shape[1]
    N = dmods_mine.shape[1]
    tn = N // 3

    def body(c_ref, da_ref, dm_ref, db_ref, dw_ref):
        c = c_ref[...]
        cond = (c * jax.nn.sigmoid(c)).astype(BF16)
        dw_ref[...] = _dg(cond, dm_ref[...].astype(BF16), TN)
        db_ref[...] = jnp.sum(da_ref[...], axis=0, keepdims=True)

    return _call(
        body, name=name, grid=(3,),
        in_specs=[pl.BlockSpec((nb, D), lambda j: (0, 0)), pl.BlockSpec((nb, NA), lambda j: (0, 0)),
                  pl.BlockSpec((nb, tn), lambda j: (0, j))],
        out_specs=[pl.BlockSpec((1, NA), lambda j: (0, 0)), pl.BlockSpec((D, tn), lambda j: (0, j))],
        out_shape=[jax.ShapeDtypeStruct((1, NA), F32), jax.ShapeDtypeStruct((D, N), F32)],
        args=[c_all, dmods_all, dmods_mine])


def adamw(w, g, m, v, *, name):
    R, C = w.shape
    rb = _row_block(R, max(8, (1 << 18) // C))
    c1 = 1.0 / (1.0 - ADAM_B1 ** ADAM_STEP)
    c2 = 1.0 / (1.0 - ADAM_B2 ** ADAM_STEP)

    def body(w_ref, g_ref, m_ref, v_ref, d_ref, mo_ref, vo_ref):
        gg = g_ref[...]
        mn = ADAM_B1 * m_ref[...] + (1.0 - ADAM_B1) * gg
        vn = ADAM_B2 * v_ref[...] + (1.0 - ADAM_B2) * (gg * gg)
        mo_ref[...] = mn
        vo_ref[...] = vn
        d_ref[...] = -ADAM_LR * ((mn * c1) / (jnp.sqrt(vn * c2) + ADAM_EPS) + ADAM_WD * w_ref[...])

    blk = pl.BlockSpec((rb, C), lambda i: (i, 0))
    shp = jax.ShapeDtypeStruct((R, C), F32)
    return _call(body, name=name, grid=(R // rb,), in_specs=[blk] * 4, out_specs=[blk] * 3, out_shape=[shp] * 3,
                 args=[w, g, m, v])


def _place():
    return lax.axis_index("x"), lax.axis_index("y"), lax.axis_index("c")


def _flip(v, bit):
    return 1 - v if bit else v


def _other_chips(mx, my):
    return [(_flip(mx, k & 2), _flip(my, k & 1)) for k in range(1, N_QUAD)]


def _remote(src, dst, send_sem, recv_sem, peer):
    return pltpu.make_async_remote_copy(src_ref=src, dst_ref=dst, send_sem=send_sem, recv_sem=recv_sem,
                                        device_id=peer, device_id_type=MESH)


def all_gather8(x, *, name, reduce=False):
    r, n = x.shape

    def body(x_ref, o_ref, *rest):
        if reduce:
            buf, send_sems, recv_sems, lsem = rest
        else:
            buf = o_ref
            send_sems, recv_sems, lsem = rest
        mx, my, mc = _place()
        me = 4 * mx + 2 * my + mc
        mine = pltpu.make_async_copy(x_ref, buf.at[me], lsem)
        mine.start()
        sends, recvs = [], []
        for k in range(1, N_DEV):
            peer = (_flip(mx, k & 4), _flip(my, k & 2), _flip(mc, k & 1))
            pidx = 4 * peer[0] + 2 * peer[1] + peer[2]
            sends.append(_remote(x_ref, buf.at[me], send_sems.at[k - 1], recv_sems.at[k - 1], peer))
            recvs.append(_remote(x_ref, buf.at[pidx], send_sems.at[k - 1], recv_sems.at[k - 1], peer))
        for cp in sends:
            cp.start()
        for cp in recvs:
            cp.wait_recv()
        for cp in sends:
            cp.wait_send()
        mine.wait()
        if reduce:
            total = buf[0]
            for d in range(1, N_DEV):
                total = total + buf[d]
            o_ref[...] = total

    scratch = [pltpu.SemaphoreType.DMA((N_DEV - 1,)), pltpu.SemaphoreType.DMA((N_DEV - 1,)), pltpu.SemaphoreType.DMA]
    if reduce:
        scratch = [pltpu.VMEM((N_DEV, r, n), F32)] + scratch
        out_shape = jax.ShapeDtypeStruct((r, n), F32)
    else:
        out_shape = jax.ShapeDtypeStruct((N_DEV, r, n), F32)
    return pl.pallas_call(
        body, name=name, out_shape=out_shape, in_specs=[pl.BlockSpec(memory_space=pltpu.VMEM)],
        out_specs=pl.BlockSpec(memory_space=pltpu.VMEM), scratch_shapes=scratch,
        compiler_params=pltpu.CompilerParams(vmem_limit_bytes=VMEM_LIMIT_BYTES),
    )(x)


def cast_into_stack(w, quad, *, name):
    R, C = w.shape
    rb = _row_block(R, 256)

    def body(q_ref, w_ref, o_ref):
        o_ref[0] = w_ref[...].astype(BF16)

    return pl.pallas_call(
        body, name=name, out_shape=jax.ShapeDtypeStruct((N_QUAD, R, C), BF16),
        grid_spec=pltpu.PrefetchScalarGridSpec(
            num_scalar_prefetch=1, grid=(R // rb,),
            in_specs=[pl.BlockSpec((rb, C), lambda i, q: (i, 0))],
            out_specs=pl.BlockSpec((1, rb, C), lambda i, q: (q[0], i, 0))),
        compiler_params=pltpu.CompilerParams(dimension_semantics=("arbitrary",), vmem_limit_bytes=VMEM_LIMIT_BYTES),
    )(quad, w)


def gather_comm(stacks):
    n = len(stacks)

    def copies(bufs, ss, rs, only_sends=False):
        mx, my, mc = _place()
        q = 2 * mx + my
        sibling = (mx, my, 1 - mc)
        ici_send, ici_recv, fwd_send, fwd_recv = [], [], [], []
        for p in range(n):
            hr = stacks[p].shape[1] // 2
            mine, other = pl.ds(mc * hr, hr), pl.ds((1 - mc) * hr, hr)
            for k, chip in enumerate(_other_chips(mx, my)):
                qk = 2 * chip[0] + chip[1]
                peer = (chip[0], chip[1], mc)
                own, landed, theirs = bufs[p].at[q, mine, :], bufs[p].at[qk, mine, :], bufs[p].at[qk, other, :]
                ici_send.append(_remote(own, own, ss.at[6 * p + k], rs.at[6 * p + k], peer))
                if only_sends:
                    continue
                ici_recv.append(_remote(own, landed, ss.at[6 * p + k], rs.at[6 * p + k], peer))
                fwd_send.append(_remote(landed, landed, ss.at[6 * p + 3 + k], rs.at[6 * p + 3 + k], sibling))
                fwd_recv.append(_remote(theirs, theirs, ss.at[6 * p + 3 + k], rs.at[6 * p + 3 + k], sibling))
        return ici_send, ici_recv, fwd_send, fwd_recv

    def start(srcs, bufs, lands, ss, rs):
        for cp in copies(bufs, ss, rs, only_sends=True)[0]:
            cp.start()

    def finish(srcs, bufs, lands, ss, rs):
        ici_send, ici_recv, fwd_send, fwd_recv = copies(bufs, ss, rs)
        for arrived, onward in zip(ici_recv, fwd_send):
            arrived.wait_recv()
            onward.start()
        for cp in fwd_recv:
            cp.wait_recv()
        for cp in ici_send + fwd_send:
            cp.wait_send()

    return Comm(bufs=stacks, n_sems=6 * n, start=start, finish=finish)


def rs_pair_comm(gs):
    n = len(gs)

    def copies(srcs, lands, ss, rs):
        mx, my, mc = _place()
        out = []
        for p in range(n):
            hr = gs[p].shape[1] // 2
            out.append(_remote(srcs[p].at[:, pl.ds((1 - mc) * hr, hr), :], lands[p], ss.at[p], rs.at[p], (mx, my, 1 - mc)))
        return out

    def start(srcs, bufs, lands, ss, rs):
        for cp in copies(srcs, lands, ss, rs):
            cp.start()

    def finish(srcs, bufs, lands, ss, rs):
        for cp in copies(srcs, lands, ss, rs):
            cp.wait()

    return Comm(srcs=gs, land_shapes=[jax.ShapeDtypeStruct((g.shape[0], g.shape[1] // 2, g.shape[2]), g.dtype) for g in gs],
                n_sems=n, start=start, finish=finish)


def rs_chip_comm(ss_):
    n = len(ss_)

    def copies(srcs, lands, ss, rs):
        mx, my, mc = _place()
        out = []
        for p in range(n):
            for k, chip in enumerate(_other_chips(mx, my)):
                out.append(_remote(srcs[p].at[2 * chip[0] + chip[1]], lands[p].at[k], ss.at[3 * p + k], rs.at[3 * p + k],
                                   (chip[0], chip[1], mc)))
        return out

    def start(srcs, bufs, lands, ss, rs):
        for cp in copies(srcs, lands, ss, rs):
            cp.start()

    def finish(srcs, bufs, lands, ss, rs):
        for cp in copies(srcs, lands, ss, rs):
            cp.wait()

    return Comm(srcs=ss_, land_shapes=[jax.ShapeDtypeStruct((N_QUAD - 1,) + s.shape[1:], s.dtype) for s in ss_],
                n_sems=3 * n, start=start, finish=finish)


def rs_share_comm(fulls):
    n = len(fulls)

    def copies(bufs, ss, rs, only_sends=False):
        mx, my, mc = _place()
        send, recv = [], []
        for p in range(n):
            hr = fulls[p].shape[0] // 2
            mine, other = bufs[p].at[pl.ds(mc * hr, hr), :], bufs[p].at[pl.ds((1 - mc) * hr, hr), :]
            send.append(_remote(mine, mine, ss.at[p], rs.at[p], (mx, my, 1 - mc)))
            if not only_sends:
                recv.append(_remote(other, other, ss.at[p], rs.at[p], (mx, my, 1 - mc)))
        return send, recv

    def start(srcs, bufs, lands, ss, rs):
        for cp in copies(bufs, ss, rs, only_sends=True)[0]:
            cp.start()

    def finish(srcs, bufs, lands, ss, rs):
        send, recv = copies(bufs, ss, rs)
        for cp in recv:
            cp.wait_recv()
        for cp in send:
            cp.wait_send()

    return Comm(bufs=fulls, n_sems=n, start=start, finish=finish)


def pair_sum(g, recv, mc, *, name):
    nq, R, C = g.shape
    hr = R // 2
    rb = _row_block(hr, 256)
    nb = hr // rb

    def body(s_ref, g_ref, r_ref, o_ref):
        o_ref[...] = (g_ref[...].astype(F32) + r_ref[...].astype(F32)).astype(BF16)

    return pl.pallas_call(
        body, name=name, out_shape=jax.ShapeDtypeStruct((nq, hr, C), BF16),
        grid_spec=pltpu.PrefetchScalarGridSpec(
            num_scalar_prefetch=1, grid=(nq, nb),
            in_specs=[pl.BlockSpec((1, rb, C), lambda q, i, s: (q, s[0] * nb + i, 0)),
                      pl.BlockSpec((1, rb, C), lambda q, i, s: (q, i, 0))],
            out_specs=pl.BlockSpec((1, rb, C), lambda q, i, s: (q, i, 0))),
        compiler_params=pltpu.CompilerParams(dimension_semantics=("arbitrary", "arbitrary"),
                                             vmem_limit_bytes=VMEM_LIMIT_BYTES),
    )(mc, g, recv)


def chip_sum(s, recv, quad_mc, *, name):
    nq, hr, C = s.shape
    rb = _row_block(hr, 256)
    nb = hr // rb

    def body(q_ref, s_ref, r_ref, o_ref):
        total = s_ref[0].astype(F32)
        for k in range(N_QUAD - 1):
            total = total + r_ref[k].astype(F32)
        o_ref[...] = total

    return pl.pallas_call(
        body, name=name, out_shape=jax.ShapeDtypeStruct((2 * hr, C), F32),
        grid_spec=pltpu.PrefetchScalarGridSpec(
            num_scalar_prefetch=1, grid=(nb,),
            in_specs=[pl.BlockSpec((1, rb, C), lambda i, q: (q[0], i, 0)),
                      pl.BlockSpec((N_QUAD - 1, rb, C), lambda i, q: (0, i, 0))],
            out_specs=pl.BlockSpec((rb, C), lambda i, q: (q[1] * nb + i, 0))),
        compiler_params=pltpu.CompilerParams(dimension_semantics=("arbitrary",), vmem_limit_bytes=VMEM_LIMIT_BYTES),
    )(quad_mc, s, recv)


def _stack_cols(w_full):
    K, N = w_full.shape
    return w_full.reshape(K, N_QUAD, N // N_QUAD).transpose(1, 0, 2)


def _unstack_cols(w4):
    nq, K, n = w4.shape
    return w4.transpose(1, 0, 2).reshape(K, nq * n)


def kernel(x, c, w_ada, b_ada, g_norm1, ffn1_w_gate, ffn1_w_up, ffn1_w_down, g_norm2, w_in, g_sgu_ln, b_sgu_ln, w_spatial, b_spatial, g_q, g_k, attn_sinks, w_branch_a, w_branch_b, w_out, g_norm3, ffn2_w_gate, ffn2_w_up, ffn2_w_down, loss_target, m_w_ada, m_b_ada, m_g_norm1, m_ffn1_w_gate, m_ffn1_w_up, m_ffn1_w_down, m_g_norm2, m_w_in, m_g_sgu_ln, m_b_sgu_ln, m_w_spatial, m_b_spatial, m_g_q, m_g_k, m_attn_sinks, m_w_branch_a, m_w_branch_b, m_w_out, m_g_norm3, m_ffn2_w_gate, m_ffn2_w_up, m_ffn2_w_down, v_w_ada, v_b_ada, v_g_norm1, v_ffn1_w_gate, v_ffn1_w_up, v_ffn1_w_down, v_g_norm2, v_w_in, v_g_sgu_ln, v_b_sgu_ln, v_w_spatial, v_b_spatial, v_g_q, v_g_k, v_attn_sinks, v_w_branch_a, v_w_branch_b, v_w_out, v_g_norm3, v_ffn2_w_gate, v_ffn2_w_up, v_ffn2_w_down):
    B, S, D = x.shape
    T = B * S
    n_mod = b_ada.shape[1] // D
    mx, my, mc = _place()
    quad = 2 * mx + my
    mc_arr = jnp.reshape(mc, (1,)).astype(jnp.int32)
    quad_arr = jnp.reshape(quad, (1,)).astype(jnp.int32)
    quad_mc = jnp.stack([quad, mc]).astype(jnp.int32)
    tm = min(512, S)
    tm_small = min(256, S)
    tm_big = min(1024, S)
    tt_half = min(2048, T)
    cpb = min(4, S // CHUNK)

    xt = x.reshape(T, D)
    tgt = loss_target.reshape(T, D)

    tr = lambda a: jnp.swapaxes(a, 1, 2)
    stack = lambda w, nm: cast_into_stack(w[0], quad_arr, name=f"stack_{nm}")
    gather_gu1 = gather_comm([stack(tr(ffn1_w_gate), "wg1"), stack(tr(ffn1_w_up), "wu1")])
    run_comms([gather_gu1], name="gather_ffn1")
    wg1, wu1 = gather_gu1.bufs_out
    gather_wd1 = gather_comm([stack(ffn1_w_down, "wd1")])
    gather_win = gather_comm([stack(tr(w_in), "win")])
    gather_abo = gather_comm([stack(w_branch_a, "wa"), stack(w_branch_b, "wb"), stack(w_out, "wo")])
    gather_wg3 = gather_comm([stack(tr(ffn2_w_gate), "wg3")])
    gather_wu3 = gather_comm([stack(tr(ffn2_w_up), "wu3")])
    gather_wd3 = gather_comm([stack(ffn2_w_down, "wd3")])

    c_all = all_gather8(c, name="gather_cond").reshape(N_DEV * B, D)
    n_ada = w_ada.shape[2]
    b_mine = lax.dynamic_slice(b_ada, (0, quad * n_ada), (1, n_ada))
    mods_part = ada_fwd(c_all, w_ada[0], b_mine, name="ada_fwd")
    mods_parts = all_gather8(mods_part, name="gather_mods")
    mods = jnp.concatenate([mods_parts[2 * j] for j in range(N_QUAD)], axis=1)
    me = 4 * mx + 2 * my + mc
    mods = lax.dynamic_slice(mods, (me * B, 0), (B, n_mod * D))
    sh1, sc1, ga1, sh2, sc2, ga2, sh3, sc3, ga3 = [mods[:, j * D:(j + 1) * D] for j in range(n_mod)]
    bs_t = b_spatial[0].T
    ws = w_spatial[0]

    xn1 = norm_mod_fwd(xt, g_norm1, sc1, sh1, seq=S, tm=tm, name="norm1")
    g1, u1, a1 = ffn_up(xn1, wg1, wu1, tm=tm_big, name="ffn1_up", comms=[gather_wd1, gather_win])
    (wd1,), (win4,) = gather_wd1.bufs_out, gather_win.bufs_out
    win = win4.reshape(-1, D)
    w_uv, w_qkv, w_gt = win[0:2 * D_A], win[2 * D_A:2 * D_A + QKV_W], win[2 * D_A + QKV_W:]
    h1, f1 = ffn_down(a1, wd1, xt, ga1, seq=S, tm=tm, name="ffn1_down", comms=[gather_abo])
    wa4, wb4, wo4 = gather_abo.bufs_out
    wa, wb = _unstack_cols(wa4), _unstack_cols(wb4)
    wo = wo4.reshape(D, D)
    xn2 = norm_mod_fwd(h1, g_norm2, sc2, sh2, seq=S, tm=tm, name="norm2")
    puv, pqkv, pgt = proj_fwd(xn2, [w_uv, w_qkv, w_gt], tm=tm_small, name="proj", comms=[gather_wg3])
    (wg3,) = gather_wg3.bufs_out
    sgu = sgu_fwd(puv, g_sgu_ln, b_sgu_ln, ws, bs_t, cpb=cpb, name="sgu_fwd")
    gq_t, gk_t = jnp.tile(g_q, (1, N_Q)), jnp.tile(g_k, (1, N_KV))
    att = attn_fwd(pqkv, gq_t, gk_t, attn_sinks, seq=S, name="attn_fwd", comms=[gather_wu3])
    (wu3,) = gather_wu3.bufs_out
    ya, yb, merged, mm, h2 = mixer_out_fwd(sgu, att, pgt, h1, ga2, wa, wb, wo, seq=S, tm=tm_small, name="mixer_out",
                                           comms=[gather_wd3])
    (wd3,) = gather_wd3.bufs_out
    xn3 = norm_mod_fwd(h2, g_norm3, sc3, sh3, seq=S, tm=tm, name="norm3")
    g3, u3, a3 = ffn_up(xn3, wg3, wu3, tm=tm_big, name="ffn2_up")
    dy, f3, lparts = ffn_down(a3, wd3, h2, ga3, tgt, seq=S, tm=tm, name="ffn2_down_loss")
    loss_part = jnp.sum(lparts[:, 0, 0])

    def sums_of(pair, tag):
        return [pair_sum(g, r, mc_arr, name=f"pair_sum_{tag}_{p}") for p, (g, r) in enumerate(zip(pair.srcs, pair.lands))]

    def halves_of(chip, tag):
        return [chip_sum(s, r, quad_mc, name=f"chip_sum_{tag}_{p}") for p, (s, r) in enumerate(zip(chip.srcs, chip.lands))]

    df3, dga3 = gate_bwd(dy, f3, ga3, 0.5, seq=S, tm=tm, name="ffn2_gate_bwd")
    dg3, du3 = ffn_bwd_act(df3, wd3, g3, u3, tm=tm_big, name="ffn2_act_bwd")
    (dwd3,) = mm_tn([(a3, df3)], tt=T, name="ffn2_dwd")
    pair_wd3 = rs_pair_comm([dwd3])
    dwg3, dwu3 = mm_tn([(dg3, xn3), (du3, xn3)], tt=tt_half, name="ffn2_dwgu", comms=[pair_wd3])
    chip_wd3 = rs_chip_comm(sums_of(pair_wd3, "wd3"))
    pair_gu3 = rs_pair_comm([dwg3, dwu3])
    dh2, dsh3, dsc3, dgn3 = dx_norm_bwd([dg3, du3], [wg3, wu3], h2, dy, g_norm3, sc3, seq=S, tm=tm, name="ffn2_dx",
                                        comms=[chip_wd3, pair_gu3])
    sum_wg3, sum_wu3 = sums_of(pair_gu3, "gu3")
    chip_wg3, chip_wu3 = rs_chip_comm([sum_wg3]), rs_chip_comm([sum_wu3])

    dm, dga2 = gate_bwd(dh2, mm, ga2, 1.0, seq=S, tm=tm, name="mixer_gate_bwd")
    dgt, dya, dyb, dsgu, datt = mixer_out_bwd(dm, ya, yb, pgt, wa, wb, wo, tm=tm_small, name="mixer_out_bwd",
                                              comms=[chip_wg3])
    (dwo,) = mm_tn([(merged, dm)], tt=tt_half, name="mixer_dwo")
    (dwa,) = mm_tn([(sgu, dya)], tt=tt_half, name="mixer_dwa")
    (dwb,) = mm_tn([(att, dyb)], tt=tt_half, name="mixer_dwb")
    pair_abo = rs_pair_comm([_stack_cols(dwa), _stack_cols(dwb), dwo.reshape(N_QUAD, D // N_QUAD, D)])
    duv, dws, dzs, dgln, dbln = sgu_bwd(puv, dsgu, g_sgu_ln, b_sgu_ln, ws, bs_t, cpb=cpb, name="sgu_bwd",
                                        comms=[pair_abo])
    chip_abo = rs_chip_comm(sums_of(pair_abo, "abo"))
    dqkv, dgq_h, dgk_h, dsk = attn_bwd(pqkv, datt, gq_t, gk_t, attn_sinks, seq=S, name="attn_bwd",
                                       comms=[chip_wu3, chip_abo])
    dgq = dgq_h.reshape(N_Q, HEAD_DIM).sum(axis=0, keepdims=True)
    dgk = dgk_h.reshape(N_KV, HEAD_DIM).sum(axis=0, keepdims=True)
    share3 = rs_share_comm(halves_of(chip_wg3, "wg3") + halves_of(chip_wu3, "wu3") + halves_of(chip_wd3, "wd3"))
    dwin_parts = mm_tn([(duv, xn2), (dqkv, xn2), (dgt, xn2)], tt=tm, name="mixer_dwin", comms=[share3])
    r_wg3, r_wu3, r_wd3 = share3.bufs_out
    pair_win = rs_pair_comm([jnp.concatenate(dwin_parts, axis=0).reshape(win4.shape)])
    dh1, dsh2, dsc2, dgn2 = dx_norm_bwd([duv, dqkv, dgt], [w_uv, w_qkv, w_gt], h1, dh2, g_norm2, sc2, seq=S,
                                        tm=tm, name="mixer_dx", comms=[pair_win])
    chip_win = rs_chip_comm(sums_of(pair_win, "win"))

    df1, dga1 = gate_bwd(dh1, f1, ga1, 0.5, seq=S, tm=tm, name="ffn1_gate_bwd")
    dg1, du1 = ffn_bwd_act(df1, wd1, g1, u1, tm=tm_big, name="ffn1_act_bwd", comms=[chip_win])
    dwg1, dwu1 = mm_tn([(dg1, xn1), (du1, xn1)], tt=tt_half, name="ffn1_dwgu")
    pair_gu1 = rs_pair_comm([dwg1, dwu1])
    (dwd1,) = mm_tn([(a1, df1)], tt=T, name="ffn1_dwd", comms=[pair_gu1])
    chip_gu1 = rs_chip_comm(sums_of(pair_gu1, "gu1"))
    pair_wd1 = rs_pair_comm([dwd1])
    dx, dsh1, dsc1, dgn1 = dx_norm_bwd([dg1, du1], [wg1, wu1], xt, dh1, g_norm1, sc1, seq=S, tm=tm, name="ffn1_dx",
                                       comms=[chip_gu1, pair_wd1])
    chip_wd1 = rs_chip_comm(sums_of(pair_wd1, "wd1"))
    run_comms([chip_wd1], name="rs_chip_wd1")
    share_rest = rs_share_comm(halves_of(chip_win, "win") + halves_of(chip_abo, "abo") + halves_of(chip_gu1, "gu1")
                               + halves_of(chip_wd1, "wd1"))
    run_comms([share_rest], name="rs_share_rest")
    r_win, r_wa, r_wb, r_wo, r_wg1, r_wu1, r_wd1 = share_rest.bufs_out

    dmods = jnp.concatenate([dsh1, dsc1, dga1, dsh2, dsc2, dga2, dsh3, dsc3, dga3], axis=1)
    dmods_all = all_gather8(dmods, name="gather_dmods").reshape(N_DEV * B, n_mod * D)
    dmods_mine = lax.dynamic_slice(dmods_all, (0, quad * n_ada), (N_DEV * B, n_ada))
    db_ada, dw_ada = ada_bwd(c_all, dmods_all, dmods_mine, name="ada_bwd")

    db_s = dzs.reshape(CHUNK, N_GROUPS, D_A // N_GROUPS).sum(axis=2).T.reshape(1, N_GROUPS * CHUNK)
    tail = jnp.concatenate([db_s, dgq, dgk, dsk[0:1, 0:N_Q], loss_part.reshape(1, 1)], axis=1)
    tail = jnp.pad(tail, ((0, 0), (0, (-tail.shape[1]) % D)))
    lnrow = jnp.concatenate([dgln, dbln], axis=1)
    lnrow = jnp.pad(lnrow, ((0, 0), (0, (-lnrow.shape[1]) % D)))
    packed = jnp.concatenate([dgn1, dgn2, dgn3, lnrow.reshape(-1, D), tail.reshape(-1, D), dws.reshape(-1, D)], axis=0)
    n_rows = packed.shape[0]
    packed = jnp.pad(packed, ((0, (-n_rows) % 8), (0, 0)))
    small = all_gather8(packed, name="reduce_small", reduce=True)
    ln_rows = lnrow.size // D
    tail_rows = tail.size // D
    r = 3
    g_gn1, g_gn2, g_gn3 = small[0:1], small[1:2], small[2:3]
    ln_flat = small[r:r + ln_rows].reshape(1, -1)
    r += ln_rows
    tail_flat = small[r:r + tail_rows].reshape(1, -1)
    r += tail_rows
    g_ws = small[r:r + dws.size // D].reshape(w_spatial.shape)
    g_gln, g_bln = ln_flat[:, 0:D_A], ln_flat[:, D_A:2 * D_A]
    o = N_GROUPS * CHUNK
    g_bs = tail_flat[:, 0:o].reshape(b_spatial.shape)
    g_gq, g_gk, g_sk = tail_flat[:, o:o + HEAD_DIM], tail_flat[:, o + HEAD_DIM:o + 2 * HEAD_DIM], tail_flat[:, o + 2 * HEAD_DIM:o + 2 * HEAD_DIM + N_Q]
    loss = tail_flat[0, o + 2 * HEAD_DIM + N_Q]

    transposed = ("ffn1_w_gate", "ffn1_w_up", "w_in", "ffn2_w_gate", "ffn2_w_up")

    def update(name, w, g, m, v):
        if name in transposed:
            w, m, v = tr(w), tr(m), tr(v)
        shape = w.shape
        two_d = lambda a: a.reshape(-1, shape[-1])
        res = adamw(two_d(w), two_d(g), two_d(m), two_d(v), name=f"adamw_{name}")
        res = [a.reshape(shape) for a in [g] + list(res)]
        return [tr(a) for a in res] if name in transposed else res

    names = ["w_ada", "b_ada", "g_norm1", "ffn1_w_gate", "ffn1_w_up", "ffn1_w_down", "g_norm2", "w_in", "g_sgu_ln",
             "b_sgu_ln", "w_spatial", "b_spatial", "g_q", "g_k", "attn_sinks", "w_branch_a", "w_branch_b", "w_out",
             "g_norm3", "ffn2_w_gate", "ffn2_w_up", "ffn2_w_down"]
    weights = dict(zip(names, [w_ada, b_ada, g_norm1, ffn1_w_gate, ffn1_w_up, ffn1_w_down, g_norm2, w_in, g_sgu_ln,
                               b_sgu_ln, w_spatial, b_spatial, g_q, g_k, attn_sinks, w_branch_a, w_branch_b, w_out,
                               g_norm3, ffn2_w_gate, ffn2_w_up, ffn2_w_down]))
    m_in = dict(zip(names, [m_w_ada, m_b_ada, m_g_norm1, m_ffn1_w_gate, m_ffn1_w_up, m_ffn1_w_down, m_g_norm2, m_w_in,
                            m_g_sgu_ln, m_b_sgu_ln, m_w_spatial, m_b_spatial, m_g_q, m_g_k, m_attn_sinks, m_w_branch_a,
                            m_w_branch_b, m_w_out, m_g_norm3, m_ffn2_w_gate, m_ffn2_w_up, m_ffn2_w_down]))
    v_in = dict(zip(names, [v_w_ada, v_b_ada, v_g_norm1, v_ffn1_w_gate, v_ffn1_w_up, v_ffn1_w_down, v_g_norm2, v_w_in,
                            v_g_sgu_ln, v_b_sgu_ln, v_w_spatial, v_b_spatial, v_g_q, v_g_k, v_attn_sinks, v_w_branch_a,
                            v_w_branch_b, v_w_out, v_g_norm3, v_ffn2_w_gate, v_ffn2_w_up, v_ffn2_w_down]))
    grads = {
        "w_ada": dw_ada[None], "b_ada": db_ada, "g_norm1": g_gn1, "g_norm2": g_gn2, "g_norm3": g_gn3,
        "g_sgu_ln": g_gln, "b_sgu_ln": g_bln, "w_spatial": g_ws, "b_spatial": g_bs, "g_q": g_gq, "g_k": g_gk,
        "attn_sinks": g_sk,
        "ffn1_w_gate": r_wg1[None], "ffn1_w_up": r_wu1[None], "ffn1_w_down": r_wd1[None], "w_in": r_win[None],
        "w_branch_a": r_wa[None], "w_branch_b": r_wb[None], "w_out": r_wo[None],
        "ffn2_w_gate": r_wg3[None], "ffn2_w_up": r_wu3[None], "ffn2_w_down": r_wd3[None],
    }

    small_names = ["b_ada", "g_norm1", "g_norm2", "g_norm3", "g_sgu_ln", "b_sgu_ln", "w_spatial", "b_spatial", "g_q",
                   "g_k", "attn_sinks"]

    def pack(d):
        flat = jnp.concatenate([d[nm].reshape(-1) for nm in small_names])
        return jnp.pad(flat, (0, (-flat.size) % (8 * 128))).reshape(-1, 128)

    sd, sm, sv = adamw(pack(weights), pack(grads), pack(m_in), pack(v_in), name="adamw_small")
    delta, new_m, new_v = {}, {}, {}
    off = 0
    for nm in small_names:
        size, shape = weights[nm].size, weights[nm].shape
        delta[nm] = sd.reshape(-1)[off:off + size].reshape(shape)
        new_m[nm] = sm.reshape(-1)[off:off + size].reshape(shape)
        new_v[nm] = sv.reshape(-1)[off:off + size].reshape(shape)
        off += size
    for nm in names:
        if nm not in small_names:
            grads[nm], delta[nm], new_m[nm], new_v[nm] = update(nm, weights[nm], grads[nm], m_in[nm], v_in[nm])
        else:
            grads[nm] = grads[nm].reshape(weights[nm].shape)

    return (loss, dx.reshape(B, S, D), *[grads[nm] for nm in names], *[delta[nm] for nm in names],
            *[new_m[nm] for nm in names], *[new_v[nm] for nm in names])
```

```python
import functools
import math
import operator

import jax
import jax.numpy as jnp
from jax import lax
from jax.experimental import pallas as pl
from jax.experimental.pallas import tpu as pltpu

F32 = jnp.float32
BF16 = jnp.bfloat16
EPS = 1e-6
NEG = -1e30
N_DEV = 8
N_QUAD = 4
D_A = 512
N_GROUPS = 4
CHUNK = 128
HEAD_DIM = 64
N_KV = 2
Q_PER_KV = 4
N_Q = N_KV * Q_PER_KV
D_B = N_Q * HEAD_DIM
QKV_W = D_B + 2 * N_KV * HEAD_DIM
K_OFF = D_B
V_OFF = D_B + N_KV * HEAD_DIM
ATT_SCALE = HEAD_DIM ** -0.5
GELU_K = math.sqrt(2.0 / math.pi)
GELU_C = 0.044715
ADAM_LR, ADAM_B1, ADAM_B2, ADAM_EPS, ADAM_WD, ADAM_STEP = 0.001, 0.9, 0.999, 1e-08, 0.01, 10
VMEM_LIMIT_BYTES = 56 * 1024 * 1024
MESH = pl.DeviceIdType.MESH
NT = (((1,), (1,)), ((), ()))
TN = (((0,), (0,)), ((), ()))
ANY = pl.BlockSpec(memory_space=pl.ANY)


def _dot(a, b):
    return jnp.dot(a, b, preferred_element_type=F32)


def _dg(a, b, dims):
    return lax.dot_general(a, b, dims, preferred_element_type=F32)


def _gelu_parts(x):
    t = jnp.tanh(GELU_K * (x + GELU_C * x * x * x))
    return 0.5 * x * (1.0 + t), t


def _dgelu(x, t):
    return 0.5 * (1.0 + t) + 0.5 * x * (1.0 - t * t) * (GELU_K * (1.0 + 3.0 * GELU_C * x * x))


def _row_block(rows, target):
    b = min(rows, target)
    while rows % b or b % 8:
        b -= 1
    return b


def _resident(a):
    return pl.BlockSpec(a.shape, lambda *_: (0,) * a.ndim, pipeline_mode=pl.Buffered(1))


class Comm:
    def __init__(self, *, srcs=(), bufs=(), land_shapes=(), n_sems, start, finish):
        self.srcs, self.bufs, self.land_shapes = list(srcs), list(bufs), list(land_shapes)
        self.n_sems, self.start, self.finish = n_sems, start, finish
        self.bufs_out, self.lands = None, None


def _call(body, *, name, grid, in_specs, out_specs, out_shape, args, scratch_shapes=(), comms=(), prefetch=()):
    prefetch = list(prefetch)
    in_specs, out_specs, out_shape = list(in_specs), list(out_specs), list(out_shape)
    args, scratch = list(args), list(scratch_shapes)
    n_in, n_out, n_scr = len(args), len(out_shape), len(scratch)
    aliases, layout = {}, []
    for cm in comms:
        i0, o0, s0 = len(args), len(out_shape), len(scratch)
        args += cm.srcs + cm.bufs
        in_specs += [ANY] * (len(cm.srcs) + len(cm.bufs))
        out_shape += [jax.ShapeDtypeStruct(b.shape, b.dtype) for b in cm.bufs] + cm.land_shapes
        out_specs += [ANY] * (len(cm.bufs) + len(cm.land_shapes))
        for j in range(len(cm.bufs)):
            aliases[len(prefetch) + i0 + len(cm.srcs) + j] = o0 + j
        scratch += [pltpu.SemaphoreType.DMA((cm.n_sems,)), pltpu.SemaphoreType.DMA((cm.n_sems,))]
        layout.append((i0, o0, s0))
    n_in_all, n_out_all = len(args), len(out_shape)

    def wrapped(*refs):
        scalars, refs = refs[:len(prefetch)], refs[len(prefetch):]
        ins, outs, scr = refs[:n_in_all], refs[n_in_all:n_in_all + n_out_all], refs[n_in_all + n_out_all:]
        parts = []
        for cm, (i0, o0, s0) in zip(comms, layout):
            nb = len(cm.bufs)
            parts.append((ins[i0:i0 + len(cm.srcs)], outs[o0:o0 + nb], outs[o0 + nb:o0 + nb + len(cm.land_shapes)],
                          scr[s0], scr[s0 + 1]))
        if comms and grid:
            ids = [pl.program_id(d) for d in range(len(grid))]
            first = functools.reduce(operator.and_, [i == 0 for i in ids])
            last = functools.reduce(operator.and_, [i == g - 1 for i, g in zip(ids, grid)])

            @pl.when(first)
            def _():
                for cm, p in zip(comms, parts):
                    cm.start(*p)
        elif comms:
            for cm, p in zip(comms, parts):
                cm.start(*p)
        if body is not None:
            body(*scalars, *ins[:n_in], *outs[:n_out], *scr[:n_scr])
        if comms and grid:
            @pl.when(last)
            def _():
                for cm, p in zip(comms, parts):
                    cm.finish(*p)
        elif comms:
            for cm, p in zip(comms, parts):
                cm.finish(*p)

    if prefetch:
        kwargs = dict(grid_spec=pltpu.PrefetchScalarGridSpec(
            num_scalar_prefetch=len(prefetch), grid=grid, in_specs=in_specs, out_specs=out_specs, scratch_shapes=scratch))
    else:
        kwargs = dict(in_specs=in_specs, out_specs=out_specs, scratch_shapes=scratch, **(dict(grid=grid) if grid else {}))
    sem = ("arbitrary",) * len(grid)
    res = pl.pallas_call(
        wrapped, name=name, out_shape=out_shape, input_output_aliases=aliases,
        compiler_params=pltpu.CompilerParams(dimension_semantics=sem, vmem_limit_bytes=VMEM_LIMIT_BYTES), **kwargs,
    )(*prefetch, *args)
    for cm, (i0, o0, s0) in zip(comms, layout):
        nb = len(cm.bufs)
        cm.bufs_out = list(res[o0:o0 + nb])
        cm.lands = list(res[o0 + nb:o0 + nb + len(cm.land_shapes)])
    return list(res[:n_out])


def run_comms(comms, *, name):
    _call(None, name=name, grid=(), in_specs=[], out_specs=[], out_shape=[], args=[], comms=comms)


def norm_mod_fwd(h, g, sc, sh, *, seq, tm, name, comms=()):
    T, D = h.shape
    B, tps = T // seq, seq // tm

    def body(h_ref, g_ref, sc_ref, sh_ref, o_ref):
        x = h_ref[...]
        r = lax.rsqrt(jnp.mean(x * x, axis=-1, keepdims=True) + EPS)
        y = x * r * g_ref[...]
        o_ref[...] = (y * (1.0 + sc_ref[0]) + sh_ref[0]).astype(o_ref.dtype)

    per_seq = pl.BlockSpec((1, 1, D), lambda i: (i // tps, 0, 0))
    return _call(
        body, name=name, grid=(T // tm,),
        in_specs=[pl.BlockSpec((tm, D), lambda i: (i, 0)), pl.BlockSpec((1, D), lambda i: (0, 0)), per_seq, per_seq],
        out_specs=[pl.BlockSpec((tm, D), lambda i: (i, 0))], out_shape=[jax.ShapeDtypeStruct((T, D), BF16)],
        args=[h, g, sc.reshape(B, 1, D), sh.reshape(B, 1, D)], comms=comms)[0]


def ffn_up(xn, wg, wu, *, tm, name, comms=()):
    T, D = xn.shape
    nq, fs, _ = wg.shape

    def body(x_ref, wg_ref, wu_ref, s_ref, q_ref, a_ref):
        x = x_ref[...]
        g = _dg(x, wg_ref[0], NT)
        u = _dg(x, wu_ref[0], NT)
        sg = jax.nn.sigmoid(g)
        s = g * sg
        s_ref[0] = s.astype(BF16)
        q_ref[0] = (u * (sg + s * (1.0 - sg))).astype(BF16)
        a_ref[0] = (s * u).astype(BF16)

    w_spec = pl.BlockSpec((1, fs, D), lambda q, i: (q, 0, 0))
    o_spec = pl.BlockSpec((1, tm, fs), lambda q, i: (q, i, 0))
    o_shape = jax.ShapeDtypeStruct((nq, T, fs), BF16)
    return _call(
        body, name=name, grid=(nq, T // tm),
        in_specs=[pl.BlockSpec((tm, D), lambda q, i: (i, 0)), w_spec, w_spec],
        out_specs=[o_spec, o_spec, o_spec], out_shape=[o_shape, o_shape, o_shape], args=[xn, wg, wu], comms=comms)


def ffn_down(a, wd, hin, ga, target=None, *, seq, tm, name, comms=()):
    nq, T, fs = a.shape
    D = wd.shape[-1]
    B, tps, nt = T // seq, seq // tm, T // tm
    with_loss = target is not None

    def body(*refs):
        if with_loss:
            a_ref, wd_ref, h_ref, ga_ref, t_ref, o_ref, f_ref, l_ref = refs
        else:
            a_ref, wd_ref, h_ref, ga_ref, o_ref, f_ref = refs
        f = _dot(a_ref[0], wd_ref[0])
        for q in range(1, nq):
            f = f + _dot(a_ref[q], wd_ref[q])
        f_ref[...] = f.astype(BF16)
        y = h_ref[...] + (0.5 * ga_ref[0]) * f
        if with_loss:
            e = y - t_ref[...]
            o_ref[...] = e * (1.0 / D)
            l_ref[...] = jnp.full(l_ref.shape, 0.5 * jnp.sum(e * e) * (1.0 / D), F32)
        else:
            o_ref[...] = y

    tile = pl.BlockSpec((tm, D), lambda i: (i, 0))
    in_specs = [pl.BlockSpec((nq, tm, fs), lambda i: (0, i, 0)), _resident(wd),
                tile, pl.BlockSpec((1, 1, D), lambda i: (i // tps, 0, 0))]
    out_specs = [tile, tile]
    out_shape = [jax.ShapeDtypeStruct((T, D), F32), jax.ShapeDtypeStruct((T, D), BF16)]
    args = [a, wd, hin, ga.reshape(B, 1, D)]
    if with_loss:
        in_specs.append(tile)
        args.append(target)
        out_specs.append(pl.BlockSpec((1, 8, 128), lambda i: (i, 0, 0)))
        out_shape.append(jax.ShapeDtypeStruct((nt, 8, 128), F32))
    return _call(body, name=name, grid=(nt,), in_specs=in_specs, out_specs=out_specs, out_shape=out_shape, args=args,
                 comms=comms)


def proj_fwd(xn, ws, *, tm, name, comms=()):
    T, D = xn.shape
    n = len(ws)

    def body(*refs):
        x = refs[0][...]
        for j in range(n):
            refs[1 + n + j][...] = _dg(x, refs[1 + j][...], NT)

    return _call(
        body, name=name, grid=(T // tm,),
        in_specs=[pl.BlockSpec((tm, D), lambda i: (i, 0))] + [pl.BlockSpec(w.shape, lambda i: (0, 0)) for w in ws],
        out_specs=[pl.BlockSpec((tm, w.shape[0]), lambda i: (i, 0)) for w in ws],
        out_shape=[jax.ShapeDtypeStruct((T, w.shape[0]), F32) for w in ws], args=[xn, *ws], comms=comms)


def _layer_norm_parts(v):
    mu = jnp.mean(v, axis=-1, keepdims=True)
    d = v - mu
    rstd = lax.rsqrt(jnp.mean(d * d, axis=-1, keepdims=True) + EPS)
    return d * rstd, rstd


def _causal():
    row = lax.broadcasted_iota(jnp.int32, (CHUNK, CHUNK), 0)
    col = lax.broadcasted_iota(jnp.int32, (CHUNK, CHUNK), 1)
    return row >= col


def sgu_fwd(puv, gln, bln, ws, bs_t, *, cpb, name, comms=()):
    T = puv.shape[0]
    gd = D_A // N_GROUPS
    tm = cpb * CHUNK

    def body(p_ref, gln_ref, bln_ref, ws_ref, bs_ref, o_ref):
        causal = _causal()
        wm = [jnp.where(causal, ws_ref[g], 0.0).astype(BF16) for g in range(N_GROUPS)]
        for j in range(cpb):
            rows = slice(j * CHUNK, (j + 1) * CHUNK)
            u, _ = _gelu_parts(p_ref[rows, 0:D_A])
            vv, _ = _gelu_parts(p_ref[rows, D_A:2 * D_A])
            vhat, _ = _layer_norm_parts(vv)
            vn = (vhat * gln_ref[...] + bln_ref[...]).astype(BF16)
            for g in range(N_GROUPS):
                cols = slice(g * gd, (g + 1) * gd)
                z = _dot(wm[g], vn[:, cols]) + bs_ref[:, g:g + 1]
                o_ref[rows, cols] = (u[:, cols] * z).astype(BF16)

    full = lambda a: pl.BlockSpec(a.shape, lambda i: (0,) * a.ndim)
    return _call(
        body, name=name, grid=(T // tm,),
        in_specs=[pl.BlockSpec((tm, 2 * D_A), lambda i: (i, 0)), full(gln), full(bln), full(ws), full(bs_t)],
        out_specs=[pl.BlockSpec((tm, D_A), lambda i: (i, 0))], out_shape=[jax.ShapeDtypeStruct((T, D_A), BF16)],
        args=[puv, gln, bln, ws, bs_t], comms=comms)[0]


def _rms_parts(x):
    r = lax.rsqrt(jnp.mean(x * x, axis=-1, keepdims=True) + EPS)
    return x * r, r


KV_W = Q_PER_KV * HEAD_DIM
KV2 = N_KV * HEAD_DIM
STACK = Q_PER_KV * CHUNK


def _iota(shape, dim):
    return lax.broadcasted_iota(jnp.int32, shape, dim)


def _head_mean_matrix(n):
    return jnp.where((_iota((n, n), 0) >> 6) == (_iota((n, n), 1) >> 6), 1.0 / HEAD_DIM, 0.0).astype(BF16)


def _repeat_matrix(h):
    return jnp.where(_iota((KV2, KV_W), 0) == h * HEAD_DIM + (_iota((KV2, KV_W), 1) & (HEAD_DIM - 1)), 1.0, 0.0).astype(BF16)


def _fold_matrix(h):
    j, l = _iota((KV_W, KV2), 0), _iota((KV_W, KV2), 1)
    return jnp.where(((j & (HEAD_DIM - 1)) == (l & (HEAD_DIM - 1))) & ((l >> 6) == h), 1.0, 0.0).astype(BF16)


def _dot_split(x, m):
    hi = x.astype(BF16)
    lo = (x - hi.astype(F32)).astype(BF16)
    return _dot(hi, m) + _dot(lo, m)


def _head_rms(x, mean_matrix):
    r = lax.rsqrt(_dot_split(x * x, mean_matrix) + EPS)
    return x * r, r


def _stack_heads(x):
    head = _iota(x.shape, 1) >> 6
    return jnp.concatenate([jnp.where(head == g, x, 0.0).astype(BF16) for g in range(Q_PER_KV)], axis=0)


def _unstack_heads(y):
    head = _iota((CHUNK, KV_W), 1) >> 6
    out = jnp.where(head == 0, y[0:CHUNK], 0.0)
    for g in range(1, Q_PER_KV):
        out = out + jnp.where(head == g, y[g * CHUNK:(g + 1) * CHUNK], 0.0)
    return out


def _attn_mask(i):
    qi = _iota((STACK, 2 * CHUNK), 0) & (CHUNK - 1)
    kj = _iota((STACK, 2 * CHUNK), 1)
    diff = qi + CHUNK - kj
    return (diff >= 0) & (diff < CHUNK) & ((kj >= CHUNK) | (i > 0))


def _sink_column(sink_ref, h):
    blk = _iota((STACK, 1), 0) >> 7
    col = jnp.full((STACK, 1), sink_ref[0, h * Q_PER_KV], F32)
    for g in range(1, Q_PER_KV):
        col = jnp.where(blk == g, sink_ref[0, h * Q_PER_KV + g], col)
    return col


def _attn_probs(qs, kk, valid, sink):
    s = _dg(qs, kk, NT) * ATT_SCALE
    s = jnp.where(valid, s, NEG)
    m = jnp.maximum(jnp.max(s, axis=-1, keepdims=True), sink)
    p = jnp.exp(s - m)
    es = jnp.exp(sink - m)
    inv = 1.0 / (jnp.sum(p, axis=-1, keepdims=True) + es)
    return p * inv, es * inv


def _fill_keys(p_ref, gk_ref, krep, vrep, seq):
    mean_k = _head_mean_matrix(KV2)
    reps = [_repeat_matrix(h) for h in range(N_KV)]
    for h in range(N_KV):
        krep[h][0:CHUNK, :] = jnp.zeros((CHUNK, KV_W), BF16)
        vrep[h][0:CHUNK, :] = jnp.zeros((CHUNK, KV_W), BF16)
    rows = min(seq, 4 * CHUNK)

    def piece(j, carry):
        r0 = pl.multiple_of(j * rows, CHUNK)
        nk, _ = _head_rms(p_ref[pl.ds(r0, rows), K_OFF:K_OFF + KV2], mean_k)
        kn = (nk * gk_ref[...]).astype(BF16)
        v = p_ref[pl.ds(r0, rows), V_OFF:V_OFF + KV2].astype(BF16)
        r1 = pl.multiple_of(r0 + CHUNK, CHUNK)
        for h in range(N_KV):
            krep[h][pl.ds(r1, rows), :] = _dot(kn, reps[h]).astype(BF16)
            vrep[h][pl.ds(r1, rows), :] = _dot(v, reps[h]).astype(BF16)
        return carry

    lax.fori_loop(0, seq // rows, piece, 0)


def attn_fwd(pqkv, gq_t, gk_t, sinks, *, seq, name, comms=()):
    T = pqkv.shape[0]
    nb = seq // CHUNK

    def body(p_ref, gq_ref, gk_ref, sink_ref, o_ref, k0, k1, v0, v1):
        krep, vrep = [k0, k1], [v0, v1]
        _fill_keys(p_ref, gk_ref, krep, vrep, seq)
        mean_q = _head_mean_matrix(D_B)

        def block(i, carry):
            r0 = pl.multiple_of(i * CHUNK, CHUNK)
            nq, _ = _head_rms(p_ref[pl.ds(r0, CHUNK), 0:D_B], mean_q)
            qn = nq * gq_ref[...]
            valid = _attn_mask(i)
            for h in range(N_KV):
                qs = _stack_heads(qn[:, h * KV_W:(h + 1) * KV_W])
                kk = krep[h][pl.ds(r0, 2 * CHUNK), :]
                vv = vrep[h][pl.ds(r0, 2 * CHUNK), :]
                probs, _ = _attn_probs(qs, kk, valid, _sink_column(sink_ref, h))
                out = _unstack_heads(_dot(probs.astype(BF16), vv))
                o_ref[pl.ds(r0, CHUNK), h * KV_W:(h + 1) * KV_W] = out.astype(BF16)
            return carry

        lax.fori_loop(0, nb, block, 0)

    return _call(
        body, name=name, grid=(T // seq,),
        in_specs=[pl.BlockSpec((seq, QKV_W), lambda b: (b, 0)), pl.BlockSpec(gq_t.shape, lambda b: (0, 0)),
                  pl.BlockSpec(gk_t.shape, lambda b: (0, 0)), pl.BlockSpec(memory_space=pltpu.SMEM)],
        out_specs=[pl.BlockSpec((seq, D_B), lambda b: (b, 0))], out_shape=[jax.ShapeDtypeStruct((T, D_B), BF16)],
        scratch_shapes=[pltpu.VMEM((seq + CHUNK, KV_W), BF16)] * 4,
        args=[pqkv, gq_t, gk_t, sinks], comms=comms)[0]


def mixer_out_fwd(sgu, att, pg, hin, ga, wa, wb, wo, *, seq, tm, name, comms=()):
    T, D = hin.shape
    B, tps = T // seq, seq // tm

    def body(s_ref, t_ref, pg_ref, h_ref, ga_ref, wa_ref, wb_ref, wo_ref, ya_ref, yb_ref, mg_ref, m_ref, ho_ref):
        ya = _dot(s_ref[...], wa_ref[...])
        yb = _dot(t_ref[...], wb_ref[...])
        merged = jax.nn.sigmoid(pg_ref[:, 0:D]) * ya + jax.nn.sigmoid(pg_ref[:, D:2 * D]) * yb
        mg = merged.astype(BF16)
        m = _dot(mg, wo_ref[...])
        ya_ref[...] = ya.astype(BF16)
        yb_ref[...] = yb.astype(BF16)
        mg_ref[...] = mg
        m_ref[...] = m.astype(BF16)
        ho_ref[...] = h_ref[...] + ga_ref[0] * m

    tile = lambda w: pl.BlockSpec((tm, w), lambda i: (i, 0))
    full = lambda a: pl.BlockSpec(a.shape, lambda i: (0, 0))
    b16 = jax.ShapeDtypeStruct((T, D), BF16)
    return _call(
        body, name=name, grid=(T // tm,),
        in_specs=[tile(D_A), tile(D_B), tile(2 * D), tile(D), pl.BlockSpec((1, 1, D), lambda i: (i // tps, 0, 0)),
                  full(wa), full(wb), full(wo)],
        out_specs=[tile(D)] * 5, out_shape=[b16, b16, b16, b16, jax.ShapeDtypeStruct((T, D), F32)],
        args=[sgu, att, pg, hin, ga.reshape(B, 1, D), wa, wb, wo], comms=comms)


def gate_bwd(dh, f, ga, scale, *, seq, tm, name, comms=()):
    T, D = dh.shape
    B, tps = T // seq, seq // tm

    def body(dh_ref, f_ref, ga_ref, df_ref, dga_ref):
        i = pl.program_id(0)
        d = dh_ref[...]
        df_ref[...] = ((scale * ga_ref[0]) * d).astype(BF16)
        part = scale * jnp.sum(d * f_ref[...].astype(F32), axis=0, keepdims=True)

        @pl.when(i % tps == 0)
        def _():
            dga_ref[0] = part

        @pl.when(i % tps != 0)
        def _():
            dga_ref[0] += part

    tile = pl.BlockSpec((tm, D), lambda i: (i, 0))
    per_seq = pl.BlockSpec((1, 1, D), lambda i: (i // tps, 0, 0))
    df, dga = _call(
        body, name=name, grid=(T // tm,), in_specs=[tile, tile, per_seq], out_specs=[tile, per_seq],
        out_shape=[jax.ShapeDtypeStruct((T, D), BF16), jax.ShapeDtypeStruct((B, 1, D), F32)],
        args=[dh, f, ga.reshape(B, 1, D)], comms=comms)
    return df, dga.reshape(B, D)


def ffn_bwd_act(df, wd, s, q, *, tm, name, comms=()):
    T, D = df.shape
    nq, fs, _ = wd.shape

    def body(df_ref, wd_ref, s_ref, q_ref, dg_ref, du_ref):
        da = _dg(df_ref[...], wd_ref[0], NT)
        du_ref[0] = (da * s_ref[0].astype(F32)).astype(BF16)
        dg_ref[0] = (da * q_ref[0].astype(F32)).astype(BF16)

    act = pl.BlockSpec((1, tm, fs), lambda q, i: (q, i, 0))
    o_shape = jax.ShapeDtypeStruct((nq, T, fs), BF16)
    return _call(
        body, name=name, grid=(nq, T // tm),
        in_specs=[pl.BlockSpec((tm, D), lambda q, i: (i, 0)), pl.BlockSpec((1, fs, D), lambda q, i: (q, 0, 0)), act, act],
        out_specs=[act, act], out_shape=[o_shape, o_shape], args=[df, wd, s, q], comms=comms)


def mm_tn(pairs, *, tt, name, comms=()):
    n = len(pairs)
    flat = []
    for pair in pairs:
        for a in pair:
            if not any(a is b for b in flat):
                flat.append(a)
    where = lambda a: [k for k, b in enumerate(flat) if a is b][0]
    nq = max([a.shape[0] for a in flat if a.ndim == 3] + [1])
    T = flat[0].shape[-2]
    tt = min(tt, T)
    nt = T // tt
    stacked = [x.ndim == 3 or y.ndim == 3 for x, y in pairs]

    def body(*refs):
        in_refs, o_refs, accs = refs[:len(flat)], refs[len(flat):len(flat) + n], refs[len(flat) + n:]
        t = pl.program_id(1)
        value = lambda a: in_refs[where(a)][0] if a.ndim == 3 else in_refs[where(a)][...]

        def put(j, v):
            if stacked[j]:
                o_refs[j][0] = v.astype(BF16)
            else:
                o_refs[j][...] = v.astype(BF16)

        for j, (x, y) in enumerate(pairs):
            part = _dg(value(x), value(y), TN)
            if nt == 1:
                put(j, part)
                continue

            @pl.when(t == 0)
            def _():
                accs[j][...] = part

            @pl.when(t != 0)
            def _():
                accs[j][...] += part

        if nt > 1:
            @pl.when(t == nt - 1)
            def _():
                for j in range(n):
                    put(j, accs[j][...])

    def spec(a):
        if a.ndim == 3:
            return pl.BlockSpec((1, tt, a.shape[2]), lambda q, t: (q, t, 0))
        return pl.BlockSpec((tt, a.shape[1]), lambda q, t: (t, 0))

    out_specs, out_shape = [], []
    for j, (x, y) in enumerate(pairs):
        K, N = x.shape[-1], y.shape[-1]
        if stacked[j]:
            out_specs.append(pl.BlockSpec((1, K, N), lambda q, t: (q, 0, 0)))
            out_shape.append(jax.ShapeDtypeStruct((nq, K, N), BF16))
        else:
            out_specs.append(pl.BlockSpec((K, N), lambda q, t: (0, 0)))
            out_shape.append(jax.ShapeDtypeStruct((K, N), BF16))
    return _call(
        body, name=name, grid=(nq, nt), in_specs=[spec(a) for a in flat],
        out_specs=out_specs, out_shape=out_shape,
        scratch_shapes=[pltpu.VMEM((x.shape[-1], y.shape[-1]), F32) for x, y in pairs] if nt > 1 else [],
        args=flat, comms=comms)


def dx_norm_bwd(dys, ws, hin, dh_out, g, sc, *, seq, tm, name, comms=()):
    T, D = hin.shape
    B, tps = T // seq, seq // tm
    n = len(dys)

    def body(*refs):
        dy_refs, w_refs = refs[:n], refs[n:2 * n]
        h_ref, dho_ref, g_ref, sc_ref, dhi_ref, dsh_ref, dsc_ref, dgn_ref = refs[2 * n:]
        i = pl.program_id(0)
        dxn = None
        for j in range(n):
            if dys[j].ndim == 3:
                for q in range(dys[j].shape[0]):
                    part = _dot(dy_refs[j][q], w_refs[j][q])
                    dxn = part if dxn is None else dxn + part
            else:
                part = _dot(dy_refs[j][...], w_refs[j][...])
                dxn = part if dxn is None else dxn + part
        x = h_ref[...]
        nx, r = _rms_parts(x)
        gain = g_ref[...]
        one_sc = 1.0 + sc_ref[0]
        dsh = jnp.sum(dxn, axis=0, keepdims=True)
        dsc = jnp.sum(dxn * (nx * gain), axis=0, keepdims=True)
        dgn = jnp.sum(dxn * one_sc * nx, axis=0, keepdims=True)
        dn = dxn * one_sc * gain
        dhi_ref[...] = dho_ref[...] + r * (dn - nx * jnp.mean(dn * nx, axis=-1, keepdims=True))

        @pl.when(i % tps == 0)
        def _():
            dsh_ref[0] = dsh
            dsc_ref[0] = dsc

        @pl.when(i % tps != 0)
        def _():
            dsh_ref[0] += dsh
            dsc_ref[0] += dsc

        @pl.when(i == 0)
        def _():
            dgn_ref[...] = dgn

        @pl.when(i != 0)
        def _():
            dgn_ref[...] += dgn

    def dy_spec(a):
        if a.ndim == 3:
            return pl.BlockSpec((a.shape[0], tm, a.shape[2]), lambda i: (0, i, 0))
        return pl.BlockSpec((tm, a.shape[1]), lambda i: (i, 0))

    tile = pl.BlockSpec((tm, D), lambda i: (i, 0))
    per_seq = pl.BlockSpec((1, 1, D), lambda i: (i // tps, 0, 0))
    row = pl.BlockSpec((1, D), lambda i: (0, 0))
    dhi, dsh, dsc, dgn = _call(
        body, name=name, grid=(T // tm,),
        in_specs=[dy_spec(a) for a in dys] + [_resident(w) for w in ws] + [tile, tile, row, per_seq],
        out_specs=[tile, per_seq, per_seq, row],
        out_shape=[jax.ShapeDtypeStruct((T, D), F32), jax.ShapeDtypeStruct((B, 1, D), F32),
                   jax.ShapeDtypeStruct((B, 1, D), F32), jax.ShapeDtypeStruct((1, D), F32)],
        args=[*dys, *ws, hin, dh_out, g, sc.reshape(B, 1, D)], comms=comms)
    return dhi, dsh.reshape(B, D), dsc.reshape(B, D), dgn


def mixer_out_bwd(dm, ya, yb, pg, wa, wb, wo, *, tm, name, comms=()):
    T, D = dm.shape

    def body(dm_ref, ya_ref, yb_ref, pg_ref, wa_ref, wb_ref, wo_ref, dg_ref, dya_ref, dyb_ref, ds_ref, dt_ref):
        dmg = _dg(dm_ref[...], wo_ref[...], NT)
        sa = jax.nn.sigmoid(pg_ref[:, 0:D])
        sb = jax.nn.sigmoid(pg_ref[:, D:2 * D])
        dg_ref[:, 0:D] = (dmg * ya_ref[...].astype(F32) * (sa * (1.0 - sa))).astype(BF16)
        dg_ref[:, D:2 * D] = (dmg * yb_ref[...].astype(F32) * (sb * (1.0 - sb))).astype(BF16)
        dya = (dmg * sa).astype(BF16)
        dyb = (dmg * sb).astype(BF16)
        dya_ref[...] = dya
        dyb_ref[...] = dyb
        ds_ref[...] = _dg(dya, wa_ref[...], NT)
        dt_ref[...] = _dg(dyb, wb_ref[...], NT)

    tile = lambda w: pl.BlockSpec((tm, w), lambda i: (i, 0))
    full = lambda a: pl.BlockSpec(a.shape, lambda i: (0, 0))
    return _call(
        body, name=name, grid=(T // tm,),
        in_specs=[tile(D), tile(D), tile(D), tile(2 * D), full(wa), full(wb), full(wo)],
        out_specs=[tile(2 * D), tile(D), tile(D), tile(D_A), tile(D_B)],
        out_shape=[jax.ShapeDtypeStruct((T, 2 * D), BF16), jax.ShapeDtypeStruct((T, D), BF16),
                   jax.ShapeDtypeStruct((T, D), BF16), jax.ShapeDtypeStruct((T, D_A), F32),
                   jax.ShapeDtypeStruct((T, D_B), F32)],
        args=[dm, ya, yb, pg, wa, wb, wo], comms=comms)


def sgu_bwd(puv, dsgu, gln, bln, ws, bs_t, *, cpb, name, comms=()):
    T = puv.shape[0]
    gd = D_A // N_GROUPS
    tm = cpb * CHUNK

    def body(p_ref, ds_ref, gln_ref, bln_ref, ws_ref, bs_ref, d_ref, dws_ref, dz_ref, dgl_ref, dbl_ref):
        i = pl.program_id(0)
        causal = _causal()
        wf = [jnp.where(causal, ws_ref[g], 0.0) for g in range(N_GROUPS)]
        wm = [w.astype(BF16) for w in wf]
        wt = [w.T.astype(BF16) for w in wf]
        dws = [jnp.zeros((CHUNK, CHUNK), F32) for _ in range(N_GROUPS)]
        dzs = jnp.zeros((CHUNK, D_A), F32)
        dgl = jnp.zeros((1, D_A), F32)
        dbl = jnp.zeros((1, D_A), F32)
        for j in range(cpb):
            rows = slice(j * CHUNK, (j + 1) * CHUNK)
            au = p_ref[rows, 0:D_A]
            av = p_ref[rows, D_A:2 * D_A]
            u, tu = _gelu_parts(au)
            vv, tv = _gelu_parts(av)
            vhat, rstd = _layer_norm_parts(vv)
            vn = (vhat * gln_ref[...] + bln_ref[...]).astype(BF16)
            dsg = ds_ref[rows, :]
            dz = dsg * u
            dzb = dz.astype(BF16)
            zs, dvns = [], []
            for g in range(N_GROUPS):
                cols = slice(g * gd, (g + 1) * gd)
                zs.append(_dot(wm[g], vn[:, cols]) + bs_ref[:, g:g + 1])
                dws[g] = dws[g] + _dg(dzb[:, cols], vn[:, cols], NT)
                dvns.append(_dot(wt[g], dzb[:, cols]))
            z = jnp.concatenate(zs, axis=1)
            dvn = jnp.concatenate(dvns, axis=1)
            d_ref[rows, 0:D_A] = (dsg * z * _dgelu(au, tu)).astype(BF16)
            dvh = dvn * gln_ref[...]
            dvv = rstd * (dvh - jnp.mean(dvh, axis=-1, keepdims=True)
                          - vhat * jnp.mean(dvh * vhat, axis=-1, keepdims=True))
            d_ref[rows, D_A:2 * D_A] = (dvv * _dgelu(av, tv)).astype(BF16)
            dzs = dzs + dz
            dgl = dgl + jnp.sum(dvn * vhat, axis=0, keepdims=True)
            dbl = dbl + jnp.sum(dvn, axis=0, keepdims=True)

        @pl.when(i == 0)
        def _():
            for g in range(N_GROUPS):
                dws_ref[g] = jnp.where(causal, dws[g], 0.0)
            dz_ref[...] = dzs
            dgl_ref[...] = dgl
            dbl_ref[...] = dbl

        @pl.when(i != 0)
        def _():
            for g in range(N_GROUPS):
                dws_ref[g] += jnp.where(causal, dws[g], 0.0)
            dz_ref[...] += dzs
            dgl_ref[...] += dgl
            dbl_ref[...] += dbl

    full = lambda a: pl.BlockSpec(a.shape, lambda i: (0,) * a.ndim)
    acc = lambda s: pl.BlockSpec(s, lambda i: (0,) * len(s))
    return _call(
        body, name=name, grid=(T // tm,),
        in_specs=[pl.BlockSpec((tm, 2 * D_A), lambda i: (i, 0)), pl.BlockSpec((tm, D_A), lambda i: (i, 0)),
                  full(gln), full(bln), full(ws), full(bs_t)],
        out_specs=[pl.BlockSpec((tm, 2 * D_A), lambda i: (i, 0)), acc((N_GROUPS, CHUNK, CHUNK)), acc((CHUNK, D_A)),
                   acc((1, D_A)), acc((1, D_A))],
        out_shape=[jax.ShapeDtypeStruct((T, 2 * D_A), BF16), jax.ShapeDtypeStruct((N_GROUPS, CHUNK, CHUNK), F32),
                   jax.ShapeDtypeStruct((CHUNK, D_A), F32), jax.ShapeDtypeStruct((1, D_A), F32),
                   jax.ShapeDtypeStruct((1, D_A), F32)],
        args=[puv, dsgu, gln, bln, ws, bs_t], comms=comms)


def attn_bwd(pqkv, datt, gq_t, gk_t, sinks, *, seq, name, comms=()):
    T = pqkv.shape[0]
    nb = seq // CHUNK

    def body(p_ref, do_ref, gq_ref, gk_ref, sink_ref, d_ref, dgq_ref, dgk_ref, dsk_ref,
             k0, k1, v0, v1, dk0, dk1, dv0, dv1, dgq_s, dsk_s):
        b = pl.program_id(0)
        krep, vrep, dkacc, dvacc = [k0, k1], [v0, v1], [dk0, dk1], [dv0, dv1]
        _fill_keys(p_ref, gk_ref, krep, vrep, seq)
        for h in range(N_KV):
            dkacc[h][...] = jnp.zeros_like(dkacc[h])
            dvacc[h][...] = jnp.zeros_like(dvacc[h])
        dgq_s[...] = jnp.zeros_like(dgq_s)
        dsk_s[...] = jnp.zeros_like(dsk_s)
        mean_q = _head_mean_matrix(D_B)
        lane = _iota((1, 128), 1)

        def block(i, carry):
            r0 = pl.multiple_of(i * CHUNK, CHUNK)
            nq, rq = _head_rms(p_ref[pl.ds(r0, CHUNK), 0:D_B], mean_q)
            qn = nq * gq_ref[...]
            valid = _attn_mask(i)
            dqn_parts = []
            for h in range(N_KV):
                cols = slice(h * KV_W, (h + 1) * KV_W)
                qs = _stack_heads(qn[:, cols])
                kk = krep[h][pl.ds(r0, 2 * CHUNK), :]
                vv = vrep[h][pl.ds(r0, 2 * CHUNK), :]
                probs, psink = _attn_probs(qs, kk, valid, _sink_column(sink_ref, h))
                dos = _stack_heads(do_ref[pl.ds(r0, CHUNK), cols])
                dp = _dg(dos, vv, NT)
                dr = jnp.sum(probs * dp, axis=-1, keepdims=True)
                ds = (probs * (dp - dr) * ATT_SCALE).astype(BF16)
                dsink = psink * dr
                for g in range(Q_PER_KV):
                    dsk_s[...] += jnp.where(lane == h * Q_PER_KV + g, -jnp.sum(dsink[g * CHUNK:(g + 1) * CHUNK]), 0.0)
                dqn_parts.append(_unstack_heads(_dot(ds, kk)))
                dkacc[h][pl.ds(r0, 2 * CHUNK), :] += _dg(ds, qs, TN)
                dvacc[h][pl.ds(r0, 2 * CHUNK), :] += _dg(probs.astype(BF16), dos, TN)
            dqn = jnp.concatenate(dqn_parts, axis=1)
            dn = dqn * gq_ref[...]
            d_ref[pl.ds(r0, CHUNK), 0:D_B] = (rq * (dn - nq * _dot_split(dn * nq, mean_q))).astype(BF16)
            dgq_s[...] += jnp.sum(dqn * nq, axis=0, keepdims=True)
            return carry

        lax.fori_loop(0, nb, block, 0)

        mean_k = _head_mean_matrix(KV2)
        folds = [_fold_matrix(h) for h in range(N_KV)]
        rows = min(seq, 4 * CHUNK)

        def piece(j, dgk):
            r0 = pl.multiple_of(j * rows, CHUNK)
            r1 = pl.multiple_of(r0 + CHUNK, CHUNK)
            dkn = _dot_split(dkacc[0][pl.ds(r1, rows), :], folds[0]) + _dot_split(dkacc[1][pl.ds(r1, rows), :], folds[1])
            dv = _dot_split(dvacc[0][pl.ds(r1, rows), :], folds[0]) + _dot_split(dvacc[1][pl.ds(r1, rows), :], folds[1])
            nk, rk = _head_rms(p_ref[pl.ds(r0, rows), K_OFF:K_OFF + KV2], mean_k)
            dn = dkn * gk_ref[...]
            d_ref[pl.ds(r0, rows), K_OFF:K_OFF + KV2] = (rk * (dn - nk * _dot_split(dn * nk, mean_k))).astype(BF16)
            d_ref[pl.ds(r0, rows), V_OFF:V_OFF + KV2] = dv.astype(BF16)
            return dgk + jnp.sum(dkn * nk, axis=0, keepdims=True)

        dgk = lax.fori_loop(0, seq // rows, piece, jnp.zeros((1, KV2), F32))

        @pl.when(b == 0)
        def _():
            dgq_ref[...] = dgq_s[...]
            dgk_ref[...] = dgk
            dsk_ref[...] = jnp.broadcast_to(dsk_s[...], dsk_ref.shape)

        @pl.when(b != 0)
        def _():
            dgq_ref[...] += dgq_s[...]
            dgk_ref[...] += dgk
            dsk_ref[...] += jnp.broadcast_to(dsk_s[...], dsk_ref.shape)

    acc = lambda s: pl.BlockSpec(s, lambda b: (0, 0))
    return _call(
        body, name=name, grid=(T // seq,),
        in_specs=[pl.BlockSpec((seq, QKV_W), lambda b: (b, 0)), pl.BlockSpec((seq, D_B), lambda b: (b, 0)),
                  pl.BlockSpec(gq_t.shape, lambda b: (0, 0)), pl.BlockSpec(gk_t.shape, lambda b: (0, 0)),
                  pl.BlockSpec(memory_space=pltpu.SMEM)],
        out_specs=[pl.BlockSpec((seq, QKV_W), lambda b: (b, 0)), acc((1, D_B)), acc((1, KV2)), acc((8, 128))],
        out_shape=[jax.ShapeDtypeStruct((T, QKV_W), BF16), jax.ShapeDtypeStruct((1, D_B), F32),
                   jax.ShapeDtypeStruct((1, KV2), F32), jax.ShapeDtypeStruct((8, 128), F32)],
        scratch_shapes=[pltpu.VMEM((seq + CHUNK, KV_W), BF16)] * 4 + [pltpu.VMEM((seq + CHUNK, KV_W), F32)] * 4
        + [pltpu.VMEM((1, D_B), F32), pltpu.VMEM((1, 128), F32)],
        args=[pqkv, datt, gq_t, gk_t, sinks], comms=comms)


def ada_fwd(c_all, w, b, *, name):
    nb, D = c_all.shape
    N = w.shape[1]
    tn = N // 3

    def body(c_ref, w_ref, b_ref, o_ref):
        c = c_ref[...]
        cond = (c * jax.nn.sigmoid(c)).astype(BF16)
        o_ref[...] = _dot(cond, w_ref[...].astype(BF16)) + b_ref[...]

    return _call(
        body, name=name, grid=(3,),
        in_specs=[pl.BlockSpec((nb, D), lambda j: (0, 0)), pl.BlockSpec((D, tn), lambda j: (0, j)),
                  pl.BlockSpec((1, tn), lambda j: (0, j))],
        out_specs=[pl.BlockSpec((nb, tn), lambda j: (0, j))], out_shape=[jax.ShapeDtypeStruct((nb, N), F32)],
        args=[c_all, w, b])[0]


def ada_bwd(c_all, dmods_all, dmods_mine, gain_rows, *, name, comms=()):
    nb, D = c_all.shape
    NA = dmods_all.shape[1]
    N = dmods_mine.shape[1]
    tn = N // 3

    def body(c_ref, da_ref, dm_ref, gr_ref, db_ref, dw_ref, dgn_ref):
        c = c_ref[...]
        cond = (c * jax.nn.sigmoid(c)).astype(BF16)
        dw_ref[...] = _dg(cond, dm_ref[...].astype(BF16), TN)
        db_ref[...] = jnp.sum(da_ref[...], axis=0, keepdims=True)
        total = gr_ref[0:1, :]
        for d in range(1, N_DEV):
            total = total + gr_ref[d:d + 1, :]
        dgn_ref[...] = total

    return _call(
        body, name=name, grid=(3,),
        in_specs=[pl.BlockSpec((nb, D), lambda j: (0, 0)), pl.BlockSpec((nb, NA), lambda j: (0, 0)),
                  pl.BlockSpec((nb, tn), lambda j: (0, j)), pl.BlockSpec((N_DEV, D), lambda j: (0, 0))],
        out_specs=[pl.BlockSpec((1, NA), lambda j: (0, 0)), pl.BlockSpec((D, tn), lambda j: (0, j)),
                   pl.BlockSpec((1, D), lambda j: (0, 0))],
        out_shape=[jax.ShapeDtypeStruct((1, NA), F32), jax.ShapeDtypeStruct((D, N), F32),
                   jax.ShapeDtypeStruct((1, D), F32)],
        args=[c_all, dmods_all, dmods_mine, gain_rows], comms=comms)


def adamw(w, g, m, v, *, name):
    R, C = w.shape
    rb = _row_block(R, max(8, (1 << 18) // C))
    c1 = 1.0 / (1.0 - ADAM_B1 ** ADAM_STEP)
    c2 = 1.0 / (1.0 - ADAM_B2 ** ADAM_STEP)

    def body(w_ref, g_ref, m_ref, v_ref, d_ref, mo_ref, vo_ref):
        gg = g_ref[...]
        mn = ADAM_B1 * m_ref[...] + (1.0 - ADAM_B1) * gg
        vn = ADAM_B2 * v_ref[...] + (1.0 - ADAM_B2) * (gg * gg)
        mo_ref[...] = mn
        vo_ref[...] = vn
        d_ref[...] = -ADAM_LR * ((mn * c1) / (jnp.sqrt(vn * c2) + ADAM_EPS) + ADAM_WD * w_ref[...])

    blk = pl.BlockSpec((rb, C), lambda i: (i, 0))
    shp = jax.ShapeDtypeStruct((R, C), F32)
    return _call(body, name=name, grid=(R // rb,), in_specs=[blk] * 4, out_specs=[blk] * 3, out_shape=[shp] * 3,
                 args=[w, g, m, v])


def _place():
    return lax.axis_index("x"), lax.axis_index("y"), lax.axis_index("c")


def _flip(v, bit):
    return 1 - v if bit else v


def _other_chips(mx, my):
    return [(_flip(mx, k & 2), _flip(my, k & 1)) for k in range(1, N_QUAD)]


def _remote(src, dst, send_sem, recv_sem, peer):
    return pltpu.make_async_remote_copy(src_ref=src, dst_ref=dst, send_sem=send_sem, recv_sem=recv_sem,
                                        device_id=peer, device_id_type=MESH)


def all_gather8(x, *, name, reduce=False, comms=()):
    r, n = x.shape

    def body(x_ref, o_ref, *rest):
        if reduce:
            buf, send_sems, recv_sems, lsem = rest
        else:
            buf = o_ref
            send_sems, recv_sems, lsem = rest
        mx, my, mc = _place()
        me = 4 * mx + 2 * my + mc
        mine = pltpu.make_async_copy(x_ref, buf.at[me], lsem)
        mine.start()
        sends, recvs = [], []
        for k in range(1, N_DEV):
            peer = (_flip(mx, k & 4), _flip(my, k & 2), _flip(mc, k & 1))
            pidx = 4 * peer[0] + 2 * peer[1] + peer[2]
            sends.append(_remote(x_ref, buf.at[me], send_sems.at[k - 1], recv_sems.at[k - 1], peer))
            recvs.append(_remote(x_ref, buf.at[pidx], send_sems.at[k - 1], recv_sems.at[k - 1], peer))
        for cp in sends:
            cp.start()
        for cp in recvs:
            cp.wait_recv()
        for cp in sends:
            cp.wait_send()
        mine.wait()
        if reduce:
            total = buf[0]
            for d in range(1, N_DEV):
                total = total + buf[d]
            o_ref[...] = total

    scratch = [pltpu.SemaphoreType.DMA((N_DEV - 1,)), pltpu.SemaphoreType.DMA((N_DEV - 1,)), pltpu.SemaphoreType.DMA]
    if reduce:
        scratch = [pltpu.VMEM((N_DEV, r, n), F32)] + scratch
        out_shape = jax.ShapeDtypeStruct((r, n), F32)
    else:
        out_shape = jax.ShapeDtypeStruct((N_DEV, r, n), F32)
    vmem = pl.BlockSpec(memory_space=pltpu.VMEM)
    return _call(body, name=name, grid=(), in_specs=[vmem], out_specs=[vmem], out_shape=[out_shape], args=[x],
                 scratch_shapes=scratch, comms=comms)[0]


def ag8_comm(x):
    def copies(srcs, lands, ss, rs, only_sends=False):
        mx, my, mc = _place()
        me = 4 * mx + 2 * my + mc
        local = pltpu.make_async_copy(srcs[0], lands[0].at[me], ss.at[N_DEV - 1])
        sends, recvs = [], []
        for k in range(1, N_DEV):
            peer = (_flip(mx, k & 4), _flip(my, k & 2), _flip(mc, k & 1))
            sends.append(_remote(srcs[0], lands[0].at[me], ss.at[k - 1], rs.at[k - 1], peer))
            if not only_sends:
                recvs.append(_remote(srcs[0], lands[0].at[4 * peer[0] + 2 * peer[1] + peer[2]], ss.at[k - 1], rs.at[k - 1], peer))
        return local, sends, recvs

    def start(srcs, bufs, lands, ss, rs):
        local, sends, _ = copies(srcs, lands, ss, rs, only_sends=True)
        local.start()
        for cp in sends:
            cp.start()

    def finish(srcs, bufs, lands, ss, rs):
        local, sends, recvs = copies(srcs, lands, ss, rs)
        for cp in recvs:
            cp.wait_recv()
        for cp in sends:
            cp.wait_send()
        local.wait()

    return Comm(srcs=[x], land_shapes=[jax.ShapeDtypeStruct((N_DEV,) + x.shape, x.dtype)], n_sems=N_DEV,
                start=start, finish=finish)


def sum8(stacked, *, name):
    _, r, n = stacked.shape

    def body(s_ref, o_ref):
        total = s_ref[0]
        for d in range(1, N_DEV):
            total = total + s_ref[d]
        o_ref[...] = total

    return _call(body, name=name, grid=(), in_specs=[pl.BlockSpec(memory_space=pltpu.VMEM)],
                 out_specs=[pl.BlockSpec(memory_space=pltpu.VMEM)], out_shape=[jax.ShapeDtypeStruct((r, n), F32)],
                 args=[stacked])[0]


def cast_into_stacks(ws, quad, *, name, comms=()):
    steps = 4

    def body(q_ref, *refs):
        for w_ref, o_ref in zip(refs[:len(ws)], refs[len(ws):]):
            o_ref[0] = w_ref[...].astype(BF16)

    return _call(
        body, name=name, grid=(steps,), prefetch=[quad],
        in_specs=[pl.BlockSpec((w.shape[0] // steps, w.shape[1]), lambda i, q: (i, 0)) for w in ws],
        out_specs=[pl.BlockSpec((1, w.shape[0] // steps, w.shape[1]), lambda i, q: (q[0], i, 0)) for w in ws],
        out_shape=[jax.ShapeDtypeStruct((N_QUAD,) + w.shape, BF16) for w in ws], args=list(ws), comms=comms)


def gather_comm(stacks):
    n = len(stacks)

    def copies(bufs, ss, rs, only_sends=False):
        mx, my, mc = _place()
        q = 2 * mx + my
        sibling = (mx, my, 1 - mc)
        ici_send, ici_recv, fwd_send, fwd_recv = [], [], [], []
        for p in range(n):
            hr = stacks[p].shape[1] // 2
            mine, other = pl.ds(mc * hr, hr), pl.ds((1 - mc) * hr, hr)
            for k, chip in enumerate(_other_chips(mx, my)):
                qk = 2 * chip[0] + chip[1]
                peer = (chip[0], chip[1], mc)
                own, landed, theirs = bufs[p].at[q, mine, :], bufs[p].at[qk, mine, :], bufs[p].at[qk, other, :]
                ici_send.append(_remote(own, own, ss.at[6 * p + k], rs.at[6 * p + k], peer))
                if only_sends:
                    continue
                ici_recv.append(_remote(own, landed, ss.at[6 * p + k], rs.at[6 * p + k], peer))
                fwd_send.append(_remote(landed, landed, ss.at[6 * p + 3 + k], rs.at[6 * p + 3 + k], sibling))
                fwd_recv.append(_remote(theirs, theirs, ss.at[6 * p + 3 + k], rs.at[6 * p + 3 + k], sibling))
        return ici_send, ici_recv, fwd_send, fwd_recv

    def start(srcs, bufs, lands, ss, rs):
        for cp in copies(bufs, ss, rs, only_sends=True)[0]:
            cp.start()

    def finish(srcs, bufs, lands, ss, rs):
        ici_send, ici_recv, fwd_send, fwd_recv = copies(bufs, ss, rs)
        for arrived, onward in zip(ici_recv, fwd_send):
            arrived.wait_recv()
            onward.start()
        for cp in fwd_recv:
            cp.wait_recv()
        for cp in ici_send + fwd_send:
            cp.wait_send()

    return Comm(bufs=stacks, n_sems=6 * n, start=start, finish=finish)


def rs_pair_comm(gs):
    n = len(gs)

    def copies(srcs, lands, ss, rs):
        mx, my, mc = _place()
        out = []
        for p in range(n):
            hr = gs[p].shape[1] // 2
            out.append(_remote(srcs[p].at[:, pl.ds((1 - mc) * hr, hr), :], lands[p], ss.at[p], rs.at[p], (mx, my, 1 - mc)))
        return out

    def start(srcs, bufs, lands, ss, rs):
        for cp in copies(srcs, lands, ss, rs):
            cp.start()

    def finish(srcs, bufs, lands, ss, rs):
        for cp in copies(srcs, lands, ss, rs):
            cp.wait()

    return Comm(srcs=gs, land_shapes=[jax.ShapeDtypeStruct((g.shape[0], g.shape[1] // 2, g.shape[2]), g.dtype) for g in gs],
                n_sems=n, start=start, finish=finish)


def rs_chip_comm(ss_):
    n = len(ss_)

    def copies(srcs, lands, ss, rs):
        mx, my, mc = _place()
        out = []
        for p in range(n):
            for k, chip in enumerate(_other_chips(mx, my)):
                out.append(_remote(srcs[p].at[2 * chip[0] + chip[1]], lands[p].at[k], ss.at[3 * p + k], rs.at[3 * p + k],
                                   (chip[0], chip[1], mc)))
        return out

    def start(srcs, bufs, lands, ss, rs):
        for cp in copies(srcs, lands, ss, rs):
            cp.start()

    def finish(srcs, bufs, lands, ss, rs):
        for cp in copies(srcs, lands, ss, rs):
            cp.wait()

    return Comm(srcs=ss_, land_shapes=[jax.ShapeDtypeStruct((N_QUAD - 1,) + s.shape[1:], s.dtype) for s in ss_],
                n_sems=3 * n, start=start, finish=finish)


def rs_share_comm(fulls):
    n = len(fulls)

    def copies(bufs, ss, rs, only_sends=False):
        mx, my, mc = _place()
        send, recv = [], []
        for p in range(n):
            hr = fulls[p].shape[0] // 2
            mine, other = bufs[p].at[pl.ds(mc * hr, hr), :], bufs[p].at[pl.ds((1 - mc) * hr, hr), :]
            send.append(_remote(mine, mine, ss.at[p], rs.at[p], (mx, my, 1 - mc)))
            if not only_sends:
                recv.append(_remote(other, other, ss.at[p], rs.at[p], (mx, my, 1 - mc)))
        return send, recv

    def start(srcs, bufs, lands, ss, rs):
        for cp in copies(bufs, ss, rs, only_sends=True)[0]:
            cp.start()

    def finish(srcs, bufs, lands, ss, rs):
        send, recv = copies(bufs, ss, rs)
        for cp in recv:
            cp.wait_recv()
        for cp in send:
            cp.wait_send()

    return Comm(bufs=fulls, n_sems=n, start=start, finish=finish)


def pair_sum(g, recv, mc, *, name):
    nq, R, C = g.shape
    hr = R // 2
    rb = _row_block(hr, 256)
    nb = hr // rb

    def body(s_ref, g_ref, r_ref, o_ref):
        o_ref[...] = (g_ref[...].astype(F32) + r_ref[...].astype(F32)).astype(BF16)

    return pl.pallas_call(
        body, name=name, out_shape=jax.ShapeDtypeStruct((nq, hr, C), BF16),
        grid_spec=pltpu.PrefetchScalarGridSpec(
            num_scalar_prefetch=1, grid=(nq, nb),
            in_specs=[pl.BlockSpec((1, rb, C), lambda q, i, s: (q, s[0] * nb + i, 0)),
                      pl.BlockSpec((1, rb, C), lambda q, i, s: (q, i, 0))],
            out_specs=pl.BlockSpec((1, rb, C), lambda q, i, s: (q, i, 0))),
        compiler_params=pltpu.CompilerParams(dimension_semantics=("arbitrary", "arbitrary"),
                                             vmem_limit_bytes=VMEM_LIMIT_BYTES),
    )(mc, g, recv)


def chip_sum(s, recv, quad_mc, *, name):
    nq, hr, C = s.shape
    rb = _row_block(hr, 256)
    nb = hr // rb

    def body(q_ref, s_ref, r_ref, o_ref):
        total = s_ref[0].astype(F32)
        for k in range(N_QUAD - 1):
            total = total + r_ref[k].astype(F32)
        o_ref[...] = total

    return pl.pallas_call(
        body, name=name, out_shape=jax.ShapeDtypeStruct((2 * hr, C), F32),
        grid_spec=pltpu.PrefetchScalarGridSpec(
            num_scalar_prefetch=1, grid=(nb,),
            in_specs=[pl.BlockSpec((1, rb, C), lambda i, q: (q[0], i, 0)),
                      pl.BlockSpec((N_QUAD - 1, rb, C), lambda i, q: (0, i, 0))],
            out_specs=pl.BlockSpec((rb, C), lambda i, q: (q[1] * nb + i, 0))),
        compiler_params=pltpu.CompilerParams(dimension_semantics=("arbitrary",), vmem_limit_bytes=VMEM_LIMIT_BYTES),
    )(quad_mc, s, recv)


def _stack_cols(w_full):
    K, N = w_full.shape
    return w_full.reshape(K, N_QUAD, N // N_QUAD).transpose(1, 0, 2)


def _unstack_cols(w4):
    nq, K, n = w4.shape
    return w4.transpose(1, 0, 2).reshape(K, nq * n)


def kernel(x, c, w_ada, b_ada, g_norm1, ffn1_w_gate, ffn1_w_up, ffn1_w_down, g_norm2, w_in, g_sgu_ln, b_sgu_ln, w_spatial, b_spatial, g_q, g_k, attn_sinks, w_branch_a, w_branch_b, w_out, g_norm3, ffn2_w_gate, ffn2_w_up, ffn2_w_down, loss_target, m_w_ada, m_b_ada, m_g_norm1, m_ffn1_w_gate, m_ffn1_w_up, m_ffn1_w_down, m_g_norm2, m_w_in, m_g_sgu_ln, m_b_sgu_ln, m_w_spatial, m_b_spatial, m_g_q, m_g_k, m_attn_sinks, m_w_branch_a, m_w_branch_b, m_w_out, m_g_norm3, m_ffn2_w_gate, m_ffn2_w_up, m_ffn2_w_down, v_w_ada, v_b_ada, v_g_norm1, v_ffn1_w_gate, v_ffn1_w_up, v_ffn1_w_down, v_g_norm2, v_w_in, v_g_sgu_ln, v_b_sgu_ln, v_w_spatial, v_b_spatial, v_g_q, v_g_k, v_attn_sinks, v_w_branch_a, v_w_branch_b, v_w_out, v_g_norm3, v_ffn2_w_gate, v_ffn2_w_up, v_ffn2_w_down):
    B, S, D = x.shape
    T = B * S
    n_mod = b_ada.shape[1] // D
    mx, my, mc = _place()
    quad = 2 * mx + my
    mc_arr = jnp.reshape(mc, (1,)).astype(jnp.int32)
    quad_arr = jnp.reshape(quad, (1,)).astype(jnp.int32)
    quad_mc = jnp.stack([quad, mc]).astype(jnp.int32)
    tm = min(512, S)
    tm_small = min(256, S)
    tm_big = min(1024, S)
    tt_half = min(2048, T)
    cpb = min(4, S // CHUNK)

    xt = x.reshape(T, D)
    tgt = loss_target.reshape(T, D)

    tr = lambda a: jnp.swapaxes(a, 1, 2)
    stack_wg1, stack_wu1 = cast_into_stacks([tr(ffn1_w_gate)[0], tr(ffn1_w_up)[0]], quad_arr, name="stack_gu1")
    gather_wg1, gather_wu1 = gather_comm([stack_wg1]), gather_comm([stack_wu1])
    later = [ffn1_w_down, tr(w_in), w_branch_a, w_branch_b, w_out, tr(ffn2_w_gate), tr(ffn2_w_up), ffn2_w_down]
    stacks = cast_into_stacks([w[0] for w in later], quad_arr, name="stack_rest", comms=[gather_wg1])
    (wg1,) = gather_wg1.bufs_out
    gather_wd1, gather_win = gather_comm([stacks[0]]), gather_comm([stacks[1]])
    gather_abo = gather_comm(stacks[2:5])
    gather_wg3, gather_wu3, gather_wd3 = gather_comm([stacks[5]]), gather_comm([stacks[6]]), gather_comm([stacks[7]])

    c_all = all_gather8(c, name="gather_cond").reshape(N_DEV * B, D)
    n_ada = w_ada.shape[2]
    b_mine = lax.dynamic_slice(b_ada, (0, quad * n_ada), (1, n_ada))
    mods_part = ada_fwd(c_all, w_ada[0], b_mine, name="ada_fwd")
    mods_parts = all_gather8(mods_part, name="gather_mods", comms=[gather_wu1])
    (wu1,) = gather_wu1.bufs_out
    mods = jnp.concatenate([mods_parts[2 * j] for j in range(N_QUAD)], axis=1)
    me = 4 * mx + 2 * my + mc
    mods = lax.dynamic_slice(mods, (me * B, 0), (B, n_mod * D))
    sh1, sc1, ga1, sh2, sc2, ga2, sh3, sc3, ga3 = [mods[:, j * D:(j + 1) * D] for j in range(n_mod)]
    bs_t = b_spatial[0].T
    ws = w_spatial[0]

    xn1 = norm_mod_fwd(xt, g_norm1, sc1, sh1, seq=S, tm=tm, name="norm1")
    g1, u1, a1 = ffn_up(xn1, wg1, wu1, tm=tm_big, name="ffn1_up", comms=[gather_wd1, gather_win])
    (wd1,), (win4,) = gather_wd1.bufs_out, gather_win.bufs_out
    win = win4.reshape(-1, D)
    w_uv, w_qkv, w_gt = win[0:2 * D_A], win[2 * D_A:2 * D_A + QKV_W], win[2 * D_A + QKV_W:]
    h1, f1 = ffn_down(a1, wd1, xt, ga1, seq=S, tm=tm, name="ffn1_down", comms=[gather_abo])
    wa4, wb4, wo4 = gather_abo.bufs_out
    wa, wb = _unstack_cols(wa4), _unstack_cols(wb4)
    wo = wo4.reshape(D, D)
    xn2 = norm_mod_fwd(h1, g_norm2, sc2, sh2, seq=S, tm=tm, name="norm2")
    puv, pqkv, pgt = proj_fwd(xn2, [w_uv, w_qkv, w_gt], tm=tm_small, name="proj", comms=[gather_wg3])
    (wg3,) = gather_wg3.bufs_out
    sgu = sgu_fwd(puv, g_sgu_ln, b_sgu_ln, ws, bs_t, cpb=cpb, name="sgu_fwd")
    gq_t, gk_t = jnp.tile(g_q, (1, N_Q)), jnp.tile(g_k, (1, N_KV))
    att = attn_fwd(pqkv, gq_t, gk_t, attn_sinks, seq=S, name="attn_fwd", comms=[gather_wu3])
    (wu3,) = gather_wu3.bufs_out
    ya, yb, merged, mm, h2 = mixer_out_fwd(sgu, att, pgt, h1, ga2, wa, wb, wo, seq=S, tm=tm_small, name="mixer_out",
                                           comms=[gather_wd3])
    (wd3,) = gather_wd3.bufs_out
    xn3 = norm_mod_fwd(h2, g_norm3, sc3, sh3, seq=S, tm=tm, name="norm3")
    g3, u3, a3 = ffn_up(xn3, wg3, wu3, tm=tm_big, name="ffn2_up")
    dy, f3, lparts = ffn_down(a3, wd3, h2, ga3, tgt, seq=S, tm=tm, name="ffn2_down_loss")
    loss_part = jnp.sum(lparts[:, 0, 0])

    def sums_of(pair, tag):
        return [pair_sum(g, r, mc_arr, name=f"pair_sum_{tag}_{p}") for p, (g, r) in enumerate(zip(pair.srcs, pair.lands))]

    def halves_of(chip, tag):
        return [chip_sum(s, r, quad_mc, name=f"chip_sum_{tag}_{p}") for p, (s, r) in enumerate(zip(chip.srcs, chip.lands))]

    df3, dga3 = gate_bwd(dy, f3, ga3, 0.5, seq=S, tm=tm, name="ffn2_gate_bwd")
    dg3, du3 = ffn_bwd_act(df3, wd3, g3, u3, tm=tm_big, name="ffn2_act_bwd")
    (dwd3,) = mm_tn([(a3, df3)], tt=T, name="ffn2_dwd")
    pair_wd3 = rs_pair_comm([dwd3])
    dwg3, dwu3 = mm_tn([(dg3, xn3), (du3, xn3)], tt=tt_half, name="ffn2_dwgu", comms=[pair_wd3])
    chip_wd3 = rs_chip_comm(sums_of(pair_wd3, "wd3"))
    pair_gu3 = rs_pair_comm([dwg3, dwu3])
    dh2, dsh3, dsc3, dgn3 = dx_norm_bwd([dg3, du3], [wg3, wu3], h2, dy, g_norm3, sc3, seq=S, tm=tm, name="ffn2_dx",
                                        comms=[chip_wd3, pair_gu3])
    sum_wg3, sum_wu3 = sums_of(pair_gu3, "gu3")
    chip_wg3, chip_wu3 = rs_chip_comm([sum_wg3]), rs_chip_comm([sum_wu3])

    dm, dga2 = gate_bwd(dh2, mm, ga2, 1.0, seq=S, tm=tm, name="mixer_gate_bwd")
    dgt, dya, dyb, dsgu, datt = mixer_out_bwd(dm, ya, yb, pgt, wa, wb, wo, tm=tm_small, name="mixer_out_bwd",
                                              comms=[chip_wg3])
    (dwo,) = mm_tn([(merged, dm)], tt=tt_half, name="mixer_dwo")
    (dwa,) = mm_tn([(sgu, dya)], tt=tt_half, name="mixer_dwa")
    (dwb,) = mm_tn([(att, dyb)], tt=tt_half, name="mixer_dwb")
    pair_abo = rs_pair_comm([_stack_cols(dwa), _stack_cols(dwb), dwo.reshape(N_QUAD, D // N_QUAD, D)])
    duv, dws, dzs, dgln, dbln = sgu_bwd(puv, dsgu, g_sgu_ln, b_sgu_ln, ws, bs_t, cpb=cpb, name="sgu_bwd",
                                        comms=[pair_abo])
    chip_abo = rs_chip_comm(sums_of(pair_abo, "abo"))
    dqkv, dgq_h, dgk_h, dsk = attn_bwd(pqkv, datt, gq_t, gk_t, attn_sinks, seq=S, name="attn_bwd",
                                       comms=[chip_wu3, chip_abo])
    dgq = dgq_h.reshape(N_Q, HEAD_DIM).sum(axis=0, keepdims=True)
    dgk = dgk_h.reshape(N_KV, HEAD_DIM).sum(axis=0, keepdims=True)
    share3 = rs_share_comm(halves_of(chip_wg3, "wg3") + halves_of(chip_wu3, "wu3") + halves_of(chip_wd3, "wd3"))
    dwin_parts = mm_tn([(duv, xn2), (dqkv, xn2), (dgt, xn2)], tt=tm, name="mixer_dwin", comms=[share3])
    r_wg3, r_wu3, r_wd3 = share3.bufs_out
    pair_win = rs_pair_comm([jnp.concatenate(dwin_parts, axis=0).reshape(win4.shape)])
    dh1, dsh2, dsc2, dgn2 = dx_norm_bwd([duv, dqkv, dgt], [w_uv, w_qkv, w_gt], h1, dh2, g_norm2, sc2, seq=S,
                                        tm=tm, name="mixer_dx", comms=[pair_win])
    chip_win = rs_chip_comm(sums_of(pair_win, "win"))

    df1, dga1 = gate_bwd(dh1, f1, ga1, 0.5, seq=S, tm=tm, name="ffn1_gate_bwd")
    dg1, du1 = ffn_bwd_act(df1, wd1, g1, u1, tm=tm_big, name="ffn1_act_bwd", comms=[chip_win])
    share_mix = rs_share_comm(halves_of(chip_win, "win") + halves_of(chip_abo, "abo"))

    db_s = dzs.reshape(CHUNK, N_GROUPS, D_A // N_GROUPS).sum(axis=2).T.reshape(1, N_GROUPS * CHUNK)
    tail = jnp.concatenate([db_s, dgq, dgk, dsk[0:1, 0:N_Q], loss_part.reshape(1, 1)], axis=1)
    tail = jnp.pad(tail, ((0, 0), (0, (-tail.shape[1]) % D)))
    lnrow = jnp.concatenate([dgln, dbln], axis=1)
    lnrow = jnp.pad(lnrow, ((0, 0), (0, (-lnrow.shape[1]) % D)))
    packed = jnp.concatenate([dgn2, dgn3, lnrow.reshape(-1, D), tail.reshape(-1, D), dws.reshape(-1, D)], axis=0)
    n_rows = packed.shape[0]
    packed = jnp.pad(packed, ((0, (-n_rows) % 8), (0, 0)))
    gather_small = ag8_comm(packed)

    dwg1, dwu1 = mm_tn([(dg1, xn1), (du1, xn1)], tt=tt_half, name="ffn1_dwgu", comms=[share_mix, gather_small])
    r_win, r_wa, r_wb, r_wo = share_mix.bufs_out
    small = sum8(gather_small.lands[0], name="sum_small")
    pair_gu1 = rs_pair_comm([dwg1, dwu1])
    (dwd1,) = mm_tn([(a1, df1)], tt=T, name="ffn1_dwd", comms=[pair_gu1])
    chip_gu1 = rs_chip_comm(sums_of(pair_gu1, "gu1"))
    pair_wd1 = rs_pair_comm([dwd1])
    dx, dsh1, dsc1, dgn1 = dx_norm_bwd([dg1, du1], [wg1, wu1], xt, dh1, g_norm1, sc1, seq=S, tm=tm, name="ffn1_dx",
                                       comms=[chip_gu1, pair_wd1])
    chip_wd1 = rs_chip_comm(sums_of(pair_wd1, "wd1"))
    share_gu1 = rs_share_comm(halves_of(chip_gu1, "gu1"))

    dmods = jnp.concatenate([dsh1, dsc1, dga1, dsh2, dsc2, dga2, dsh3, dsc3, dga3], axis=1)
    dmods = jnp.concatenate([dmods, jnp.pad(dgn1, ((0, 0), (0, (n_mod - 1) * D)))], axis=0)
    gathered = all_gather8(dmods, name="gather_dmods", comms=[chip_wd1, share_gu1])
    r_wg1, r_wu1 = share_gu1.bufs_out
    dmods_all = gathered[:, 0:B].reshape(N_DEV * B, n_mod * D)
    dmods_mine = lax.dynamic_slice(dmods_all, (0, quad * n_ada), (N_DEV * B, n_ada))
    share_wd1 = rs_share_comm(halves_of(chip_wd1, "wd1"))
    db_ada, dw_ada, g_gn1 = ada_bwd(c_all, dmods_all, dmods_mine, gathered[:, B, 0:D], name="ada_bwd", comms=[share_wd1])
    (r_wd1,) = share_wd1.bufs_out

    ln_rows = lnrow.size // D
    tail_rows = tail.size // D
    r = 2
    g_gn2, g_gn3 = small[0:1], small[1:2]
    ln_flat = small[r:r + ln_rows].reshape(1, -1)
    r += ln_rows
    tail_flat = small[r:r + tail_rows].reshape(1, -1)
    r += tail_rows
    g_ws = small[r:r + dws.size // D].reshape(w_spatial.shape)
    g_gln, g_bln = ln_flat[:, 0:D_A], ln_flat[:, D_A:2 * D_A]
    o = N_GROUPS * CHUNK
    g_bs = tail_flat[:, 0:o].reshape(b_spatial.shape)
    g_gq, g_gk, g_sk = tail_flat[:, o:o + HEAD_DIM], tail_flat[:, o + HEAD_DIM:o + 2 * HEAD_DIM], tail_flat[:, o + 2 * HEAD_DIM:o + 2 * HEAD_DIM + N_Q]
    loss = tail_flat[0, o + 2 * HEAD_DIM + N_Q]

    transposed = ("ffn1_w_gate", "ffn1_w_up", "w_in", "ffn2_w_gate", "ffn2_w_up")

    def update(name, w, g, m, v):
        if name in transposed:
            w, m, v = tr(w), tr(m), tr(v)
        shape = w.shape
        two_d = lambda a: a.reshape(-1, shape[-1])
        res = adamw(two_d(w), two_d(g), two_d(m), two_d(v), name=f"adamw_{name}")
        res = [a.reshape(shape) for a in [g] + list(res)]
        return [tr(a) for a in res] if name in transposed else res

    names = ["w_ada", "b_ada", "g_norm1", "ffn1_w_gate", "ffn1_w_up", "ffn1_w_down", "g_norm2", "w_in", "g_sgu_ln",
             "b_sgu_ln", "w_spatial", "b_spatial", "g_q", "g_k", "attn_sinks", "w_branch_a", "w_branch_b", "w_out",
             "g_norm3", "ffn2_w_gate", "ffn2_w_up", "ffn2_w_down"]
    weights = dict(zip(names, [w_ada, b_ada, g_norm1, ffn1_w_gate, ffn1_w_up, ffn1_w_down, g_norm2, w_in, g_sgu_ln,
                               b_sgu_ln, w_spatial, b_spatial, g_q, g_k, attn_sinks, w_branch_a, w_branch_b, w_out,
                               g_norm3, ffn2_w_gate, ffn2_w_up, ffn2_w_down]))
    m_in = dict(zip(names, [m_w_ada, m_b_ada, m_g_norm1, m_ffn1_w_gate, m_ffn1_w_up, m_ffn1_w_down, m_g_norm2, m_w_in,
                            m_g_sgu_ln, m_b_sgu_ln, m_w_spatial, m_b_spatial, m_g_q, m_g_k, m_attn_sinks, m_w_branch_a,
                            m_w_branch_b, m_w_out, m_g_norm3, m_ffn2_w_gate, m_ffn2_w_up, m_ffn2_w_down]))
    v_in = dict(zip(names, [v_w_ada, v_b_ada, v_g_norm1, v_ffn1_w_gate, v_ffn1_w_up, v_ffn1_w_down, v_g_norm2, v_w_in,
                            v_g_sgu_ln, v_b_sgu_ln, v_w_spatial, v_b_spatial, v_g_q, v_g_k, v_attn_sinks, v_w_branch_a,
                            v_w_branch_b, v_w_out, v_g_norm3, v_ffn2_w_gate, v_ffn2_w_up, v_ffn2_w_down]))
    grads = {
        "w_ada": dw_ada[None], "b_ada": db_ada, "g_norm1": g_gn1, "g_norm2": g_gn2, "g_norm3": g_gn3,
        "g_sgu_ln": g_gln, "b_sgu_ln": g_bln, "w_spatial": g_ws, "b_spatial": g_bs, "g_q": g_gq, "g_k": g_gk,
        "attn_sinks": g_sk,
        "ffn1_w_gate": r_wg1[None], "ffn1_w_up": r_wu1[None], "ffn1_w_down": r_wd1[None], "w_in": r_win[None],
        "w_branch_a": r_wa[None], "w_branch_b": r_wb[None], "w_out": r_wo[None],
        "ffn2_w_gate": r_wg3[None], "ffn2_w_up": r_wu3[None], "ffn2_w_down": r_wd3[None],
    }

    small_names = ["b_ada", "g_norm1", "g_norm2", "g_norm3", "g_sgu_ln", "b_sgu_ln", "w_spatial", "b_spatial", "g_q",
                   "g_k", "attn_sinks"]

    def pack(d):
        flat = jnp.concatenate([d[nm].reshape(-1) for nm in small_names])
        return jnp.pad(flat, (0, (-flat.size) % (8 * 128))).reshape(-1, 128)

    sd, sm, sv = adamw(pack(weights), pack(grads), pack(m_in), pack(v_in), name="adamw_small")
    delta, new_m, new_v = {}, {}, {}
    off = 0
    for nm in small_names:
        size, shape = weights[nm].size, weights[nm].shape
        delta[nm] = sd.reshape(-1)[off:off + size].reshape(shape)
        new_m[nm] = sm.reshape(-1)[off:off + size].reshape(shape)
        new_v[nm] = sv.reshape(-1)[off:off + size].reshape(shape)
        off += size
    for nm in names:
        if nm not in small_names:
            grads[nm], delta[nm], new_m[nm], new_v[nm] = update(nm, weights[nm], grads[nm], m_in[nm], v_in[nm])
        else:
            grads[nm] = grads[nm].reshape(weights[nm].shape)

    return (loss, dx.reshape(B, S, D), *[grads[nm] for nm in names], *[delta[nm] for nm in names],
            *[new_m[nm] for nm in names], *[new_v[nm] for nm in names])
```

```python
import functools
import math
import operator

import jax
import jax.numpy as jnp
from jax import lax
from jax.experimental import pallas as pl
from jax.experimental.pallas import tpu as pltpu

F32 = jnp.float32
BF16 = jnp.bfloat16
EPS = 1e-6
NEG = -1e30
N_DEV = 8
N_QUAD = 4
D_A = 512
N_GROUPS = 4
CHUNK = 128
HEAD_DIM = 64
N_KV = 2
Q_PER_KV = 4
N_Q = N_KV * Q_PER_KV
D_B = N_Q * HEAD_DIM
QKV_W = D_B + 2 * N_KV * HEAD_DIM
K_OFF = D_B
V_OFF = D_B + N_KV * HEAD_DIM
ATT_SCALE = HEAD_DIM ** -0.5
GELU_K = math.sqrt(2.0 / math.pi)
GELU_C = 0.044715
ADAM_LR, ADAM_B1, ADAM_B2, ADAM_EPS, ADAM_WD, ADAM_STEP = 0.001, 0.9, 0.999, 1e-08, 0.01, 10
VMEM_LIMIT_BYTES = 56 * 1024 * 1024
MESH = pl.DeviceIdType.MESH
NT = (((1,), (1,)), ((), ()))
TN = (((0,), (0,)), ((), ()))
ANY = pl.BlockSpec(memory_space=pl.ANY)


def _dot(a, b):
    return jnp.dot(a, b, preferred_element_type=F32)


def _dg(a, b, dims):
    return lax.dot_general(a, b, dims, preferred_element_type=F32)


def _gelu_parts(x):
    t = jnp.tanh(GELU_K * (x + GELU_C * x * x * x))
    return 0.5 * x * (1.0 + t), t


def _dgelu(x, t):
    return 0.5 * (1.0 + t) + 0.5 * x * (1.0 - t * t) * (GELU_K * (1.0 + 3.0 * GELU_C * x * x))


def _row_block(rows, target):
    b = min(rows, target)
    while rows % b or b % 8:
        b -= 1
    return b


def _resident(a):
    return pl.BlockSpec(a.shape, lambda *_: (0,) * a.ndim, pipeline_mode=pl.Buffered(1))


class Comm:
    def __init__(self, *, srcs=(), bufs=(), land_shapes=(), n_sems, start, finish):
        self.srcs, self.bufs, self.land_shapes = list(srcs), list(bufs), list(land_shapes)
        self.n_sems, self.start, self.finish = n_sems, start, finish
        self.bufs_out, self.lands = None, None


def _call(body, *, name, grid, in_specs, out_specs, out_shape, args, scratch_shapes=(), comms=(), prefetch=()):
    prefetch = list(prefetch)
    in_specs, out_specs, out_shape = list(in_specs), list(out_specs), list(out_shape)
    args, scratch = list(args), list(scratch_shapes)
    n_in, n_out, n_scr = len(args), len(out_shape), len(scratch)
    aliases, layout = {}, []
    for cm in comms:
        i0, o0, s0 = len(args), len(out_shape), len(scratch)
        args += cm.srcs + cm.bufs
        in_specs += [ANY] * (len(cm.srcs) + len(cm.bufs))
        out_shape += [jax.ShapeDtypeStruct(b.shape, b.dtype) for b in cm.bufs] + cm.land_shapes
        out_specs += [ANY] * (len(cm.bufs) + len(cm.land_shapes))
        for j in range(len(cm.bufs)):
            aliases[len(prefetch) + i0 + len(cm.srcs) + j] = o0 + j
        scratch += [pltpu.SemaphoreType.DMA((cm.n_sems,)), pltpu.SemaphoreType.DMA((cm.n_sems,))]
        layout.append((i0, o0, s0))
    n_in_all, n_out_all = len(args), len(out_shape)

    def wrapped(*refs):
        scalars, refs = refs[:len(prefetch)], refs[len(prefetch):]
        ins, outs, scr = refs[:n_in_all], refs[n_in_all:n_in_all + n_out_all], refs[n_in_all + n_out_all:]
        parts = []
        for cm, (i0, o0, s0) in zip(comms, layout):
            nb = len(cm.bufs)
            parts.append((ins[i0:i0 + len(cm.srcs)], outs[o0:o0 + nb], outs[o0 + nb:o0 + nb + len(cm.land_shapes)],
                          scr[s0], scr[s0 + 1]))
        if comms and grid:
            ids = [pl.program_id(d) for d in range(len(grid))]
            first = functools.reduce(operator.and_, [i == 0 for i in ids])
            last = functools.reduce(operator.and_, [i == g - 1 for i, g in zip(ids, grid)])

            @pl.when(first)
            def _():
                for cm, p in zip(comms, parts):
                    cm.start(*p)
        elif comms:
            for cm, p in zip(comms, parts):
                cm.start(*p)
        if body is not None:
            body(*scalars, *ins[:n_in], *outs[:n_out], *scr[:n_scr])
        if comms and grid:
            @pl.when(last)
            def _():
                for cm, p in zip(comms, parts):
                    cm.finish(*p)
        elif comms:
            for cm, p in zip(comms, parts):
                cm.finish(*p)

    if prefetch:
        kwargs = dict(grid_spec=pltpu.PrefetchScalarGridSpec(
            num_scalar_prefetch=len(prefetch), grid=grid, in_specs=in_specs, out_specs=out_specs, scratch_shapes=scratch))
    else:
        kwargs = dict(in_specs=in_specs, out_specs=out_specs, scratch_shapes=scratch, **(dict(grid=grid) if grid else {}))
    sem = ("arbitrary",) * len(grid)
    res = pl.pallas_call(
        wrapped, name=name, out_shape=out_shape, input_output_aliases=aliases,
        compiler_params=pltpu.CompilerParams(dimension_semantics=sem, vmem_limit_bytes=VMEM_LIMIT_BYTES), **kwargs,
    )(*prefetch, *args)
    for cm, (i0, o0, s0) in zip(comms, layout):
        nb = len(cm.bufs)
        cm.bufs_out = list(res[o0:o0 + nb])
        cm.lands = list(res[o0 + nb:o0 + nb + len(cm.land_shapes)])
    return list(res[:n_out])


def run_comms(comms, *, name):
    _call(None, name=name, grid=(), in_specs=[], out_specs=[], out_shape=[], args=[], comms=comms)


def norm_mod_fwd(h, g, sc, sh, *, seq, tm, name, comms=()):
    T, D = h.shape
    B, tps = T // seq, seq // tm

    def body(h_ref, g_ref, sc_ref, sh_ref, o_ref):
        x = h_ref[...]
        r = lax.rsqrt(jnp.mean(x * x, axis=-1, keepdims=True) + EPS)
        y = x * r * g_ref[...]
        o_ref[...] = (y * (1.0 + sc_ref[0]) + sh_ref[0]).astype(o_ref.dtype)

    per_seq = pl.BlockSpec((1, 1, D), lambda i: (i // tps, 0, 0))
    return _call(
        body, name=name, grid=(T // tm,),
        in_specs=[pl.BlockSpec((tm, D), lambda i: (i, 0)), pl.BlockSpec((1, D), lambda i: (0, 0)), per_seq, per_seq],
        out_specs=[pl.BlockSpec((tm, D), lambda i: (i, 0))], out_shape=[jax.ShapeDtypeStruct((T, D), BF16)],
        args=[h, g, sc.reshape(B, 1, D), sh.reshape(B, 1, D)], comms=comms)[0]


def ffn_up(xn, wg, wu, *, tm, name, comms=()):
    T, D = xn.shape
    nq, fs, _ = wg.shape

    def body(x_ref, wg_ref, wu_ref, s_ref, q_ref, a_ref):
        x = x_ref[...]
        g = _dg(x, wg_ref[0], NT)
        u = _dg(x, wu_ref[0], NT)
        sg = jax.nn.sigmoid(g)
        s = g * sg
        s_ref[0] = s.astype(BF16)
        q_ref[0] = (u * (sg + s * (1.0 - sg))).astype(BF16)
        a_ref[0] = (s * u).astype(BF16)

    w_spec = pl.BlockSpec((1, fs, D), lambda q, i: (q, 0, 0))
    o_spec = pl.BlockSpec((1, tm, fs), lambda q, i: (q, i, 0))
    o_shape = jax.ShapeDtypeStruct((nq, T, fs), BF16)
    return _call(
        body, name=name, grid=(nq, T // tm),
        in_specs=[pl.BlockSpec((tm, D), lambda q, i: (i, 0)), w_spec, w_spec],
        out_specs=[o_spec, o_spec, o_spec], out_shape=[o_shape, o_shape, o_shape], args=[xn, wg, wu], comms=comms)


def ffn_down(a, wd, hin, ga, target=None, *, seq, tm, name, comms=()):
    nq, T, fs = a.shape
    D = wd.shape[-1]
    B, tps, nt = T // seq, seq // tm, T // tm
    with_loss = target is not None

    def body(*refs):
        if with_loss:
            a_ref, wd_ref, h_ref, ga_ref, t_ref, o_ref, f_ref, l_ref = refs
        else:
            a_ref, wd_ref, h_ref, ga_ref, o_ref, f_ref = refs
        f = _dot(a_ref[0], wd_ref[0])
        for q in range(1, nq):
            f = f + _dot(a_ref[q], wd_ref[q])
        f_ref[...] = f.astype(BF16)
        y = h_ref[...] + (0.5 * ga_ref[0]) * f
        if with_loss:
            e = y - t_ref[...]
            o_ref[...] = e * (1.0 / D)
            l_ref[...] = jnp.full(l_ref.shape, 0.5 * jnp.sum(e * e) * (1.0 / D), F32)
        else:
            o_ref[...] = y

    tile = pl.BlockSpec((tm, D), lambda i: (i, 0))
    in_specs = [pl.BlockSpec((nq, tm, fs), lambda i: (0, i, 0)), _resident(wd),
                tile, pl.BlockSpec((1, 1, D), lambda i: (i // tps, 0, 0))]
    out_specs = [tile, tile]
    out_shape = [jax.ShapeDtypeStruct((T, D), F32), jax.ShapeDtypeStruct((T, D), BF16)]
    args = [a, wd, hin, ga.reshape(B, 1, D)]
    if with_loss:
        in_specs.append(tile)
        args.append(target)
        out_specs.append(pl.BlockSpec((1, 8, 128), lambda i: (i, 0, 0)))
        out_shape.append(jax.ShapeDtypeStruct((nt, 8, 128), F32))
    return _call(body, name=name, grid=(nt,), in_specs=in_specs, out_specs=out_specs, out_shape=out_shape, args=args,
                 comms=comms)


def proj_fwd(xn, ws, *, tm, name, comms=()):
    T, D = xn.shape
    n = len(ws)

    def body(*refs):
        x = refs[0][...]
        for j in range(n):
            refs[1 + n + j][...] = _dg(x, refs[1 + j][...], NT)

    return _call(
        body, name=name, grid=(T // tm,),
        in_specs=[pl.BlockSpec((tm, D), lambda i: (i, 0))] + [pl.BlockSpec(w.shape, lambda i: (0, 0)) for w in ws],
        out_specs=[pl.BlockSpec((tm, w.shape[0]), lambda i: (i, 0)) for w in ws],
        out_shape=[jax.ShapeDtypeStruct((T, w.shape[0]), F32) for w in ws], args=[xn, *ws], comms=comms)


def _layer_norm_parts(v):
    mu = jnp.mean(v, axis=-1, keepdims=True)
    d = v - mu
    rstd = lax.rsqrt(jnp.mean(d * d, axis=-1, keepdims=True) + EPS)
    return d * rstd, rstd


def _causal():
    row = lax.broadcasted_iota(jnp.int32, (CHUNK, CHUNK), 0)
    col = lax.broadcasted_iota(jnp.int32, (CHUNK, CHUNK), 1)
    return row >= col


def sgu_fwd(puv, gln, bln, ws, bs_t, *, cpb, name, comms=()):
    T = puv.shape[0]
    gd = D_A // N_GROUPS
    tm = cpb * CHUNK

    def body(p_ref, gln_ref, bln_ref, ws_ref, bs_ref, o_ref):
        causal = _causal()
        wm = [jnp.where(causal, ws_ref[g], 0.0).astype(BF16) for g in range(N_GROUPS)]
        for j in range(cpb):
            rows = slice(j * CHUNK, (j + 1) * CHUNK)
            u, _ = _gelu_parts(p_ref[rows, 0:D_A])
            vv, _ = _gelu_parts(p_ref[rows, D_A:2 * D_A])
            vhat, _ = _layer_norm_parts(vv)
            vn = (vhat * gln_ref[...] + bln_ref[...]).astype(BF16)
            for g in range(N_GROUPS):
                cols = slice(g * gd, (g + 1) * gd)
                z = _dot(wm[g], vn[:, cols]) + bs_ref[:, g:g + 1]
                o_ref[rows, cols] = (u[:, cols] * z).astype(BF16)

    full = lambda a: pl.BlockSpec(a.shape, lambda i: (0,) * a.ndim)
    return _call(
        body, name=name, grid=(T // tm,),
        in_specs=[pl.BlockSpec((tm, 2 * D_A), lambda i: (i, 0)), full(gln), full(bln), full(ws), full(bs_t)],
        out_specs=[pl.BlockSpec((tm, D_A), lambda i: (i, 0))], out_shape=[jax.ShapeDtypeStruct((T, D_A), BF16)],
        args=[puv, gln, bln, ws, bs_t], comms=comms)[0]


def _rms_parts(x):
    r = lax.rsqrt(jnp.mean(x * x, axis=-1, keepdims=True) + EPS)
    return x * r, r


KV_W = Q_PER_KV * HEAD_DIM
KV2 = N_KV * HEAD_DIM
STACK = Q_PER_KV * CHUNK


def _iota(shape, dim):
    return lax.broadcasted_iota(jnp.int32, shape, dim)


def _head_mean_matrix(n):
    return jnp.where((_iota((n, n), 0) >> 6) == (_iota((n, n), 1) >> 6), 1.0 / HEAD_DIM, 0.0).astype(BF16)


def _repeat_matrix(h):
    return jnp.where(_iota((KV2, KV_W), 0) == h * HEAD_DIM + (_iota((KV2, KV_W), 1) & (HEAD_DIM - 1)), 1.0, 0.0).astype(BF16)


def _fold_matrix(h):
    j, l = _iota((KV_W, KV2), 0), _iota((KV_W, KV2), 1)
    return jnp.where(((j & (HEAD_DIM - 1)) == (l & (HEAD_DIM - 1))) & ((l >> 6) == h), 1.0, 0.0).astype(BF16)


def _dot_split(x, m):
    hi = x.astype(BF16)
    lo = (x - hi.astype(F32)).astype(BF16)
    return _dot(hi, m) + _dot(lo, m)


def _head_rms(x, mean_matrix):
    r = lax.rsqrt(_dot_split(x * x, mean_matrix) + EPS)
    return x * r, r


def _stack_heads(x):
    head = _iota(x.shape, 1) >> 6
    return jnp.concatenate([jnp.where(head == g, x, 0.0).astype(BF16) for g in range(Q_PER_KV)], axis=0)


def _unstack_heads(y):
    head = _iota((CHUNK, KV_W), 1) >> 6
    out = jnp.where(head == 0, y[0:CHUNK], 0.0)
    for g in range(1, Q_PER_KV):
        out = out + jnp.where(head == g, y[g * CHUNK:(g + 1) * CHUNK], 0.0)
    return out


def _attn_mask(i):
    qi = _iota((STACK, 2 * CHUNK), 0) & (CHUNK - 1)
    kj = _iota((STACK, 2 * CHUNK), 1)
    diff = qi + CHUNK - kj
    return (diff >= 0) & (diff < CHUNK) & ((kj >= CHUNK) | (i > 0))


def _sink_column(sink_ref, h):
    blk = _iota((STACK, 1), 0) >> 7
    col = jnp.full((STACK, 1), sink_ref[0, h * Q_PER_KV], F32)
    for g in range(1, Q_PER_KV):
        col = jnp.where(blk == g, sink_ref[0, h * Q_PER_KV + g], col)
    return col


def _attn_probs(qs, kk, valid, sink):
    s = _dg(qs, kk, NT) * ATT_SCALE
    s = jnp.where(valid, s, NEG)
    m = jnp.maximum(jnp.max(s, axis=-1, keepdims=True), sink)
    p = jnp.exp(s - m)
    es = jnp.exp(sink - m)
    inv = 1.0 / (jnp.sum(p, axis=-1, keepdims=True) + es)
    return p * inv, es * inv


def _fill_keys(p_ref, gk_ref, krep, vrep, seq):
    mean_k = _head_mean_matrix(KV2)
    reps = [_repeat_matrix(h) for h in range(N_KV)]
    for h in range(N_KV):
        krep[h][0:CHUNK, :] = jnp.zeros((CHUNK, KV_W), BF16)
        vrep[h][0:CHUNK, :] = jnp.zeros((CHUNK, KV_W), BF16)
    rows = min(seq, 4 * CHUNK)

    def piece(j, carry):
        r0 = pl.multiple_of(j * rows, CHUNK)
        nk, _ = _head_rms(p_ref[pl.ds(r0, rows), K_OFF:K_OFF + KV2], mean_k)
        kn = (nk * gk_ref[...]).astype(BF16)
        v = p_ref[pl.ds(r0, rows), V_OFF:V_OFF + KV2].astype(BF16)
        r1 = pl.multiple_of(r0 + CHUNK, CHUNK)
        for h in range(N_KV):
            krep[h][pl.ds(r1, rows), :] = _dot(kn, reps[h]).astype(BF16)
            vrep[h][pl.ds(r1, rows), :] = _dot(v, reps[h]).astype(BF16)
        return carry

    lax.fori_loop(0, seq // rows, piece, 0)


def attn_fwd(pqkv, gq_t, gk_t, sinks, *, seq, name, comms=()):
    T = pqkv.shape[0]
    nb = seq // CHUNK

    def body(p_ref, gq_ref, gk_ref, sink_ref, o_ref, k0, k1, v0, v1):
        krep, vrep = [k0, k1], [v0, v1]
        _fill_keys(p_ref, gk_ref, krep, vrep, seq)
        mean_q = _head_mean_matrix(D_B)

        def block(i, carry):
            r0 = pl.multiple_of(i * CHUNK, CHUNK)
            nq, _ = _head_rms(p_ref[pl.ds(r0, CHUNK), 0:D_B], mean_q)
            qn = nq * gq_ref[...]
            valid = _attn_mask(i)
            for h in range(N_KV):
                qs = _stack_heads(qn[:, h * KV_W:(h + 1) * KV_W])
                kk = krep[h][pl.ds(r0, 2 * CHUNK), :]
                vv = vrep[h][pl.ds(r0, 2 * CHUNK), :]
                probs, _ = _attn_probs(qs, kk, valid, _sink_column(sink_ref, h))
                out = _unstack_heads(_dot(probs.astype(BF16), vv))
                o_ref[pl.ds(r0, CHUNK), h * KV_W:(h + 1) * KV_W] = out.astype(BF16)
            return carry

        lax.fori_loop(0, nb, block, 0)

    return _call(
        body, name=name, grid=(T // seq,),
        in_specs=[pl.BlockSpec((seq, QKV_W), lambda b: (b, 0)), pl.BlockSpec(gq_t.shape, lambda b: (0, 0)),
                  pl.BlockSpec(gk_t.shape, lambda b: (0, 0)), pl.BlockSpec(memory_space=pltpu.SMEM)],
        out_specs=[pl.BlockSpec((seq, D_B), lambda b: (b, 0))], out_shape=[jax.ShapeDtypeStruct((T, D_B), BF16)],
        scratch_shapes=[pltpu.VMEM((seq + CHUNK, KV_W), BF16)] * 4,
        args=[pqkv, gq_t, gk_t, sinks], comms=comms)[0]


def mixer_out_fwd(sgu, att, pg, hin, ga, wa, wb, wo, *, seq, tm, name, comms=()):
    T, D = hin.shape
    B, tps = T // seq, seq // tm

    def body(s_ref, t_ref, pg_ref, h_ref, ga_ref, wa_ref, wb_ref, wo_ref, ya_ref, yb_ref, mg_ref, m_ref, ho_ref):
        ya = _dot(s_ref[...], wa_ref[...])
        yb = _dot(t_ref[...], wb_ref[...])
        merged = jax.nn.sigmoid(pg_ref[:, 0:D]) * ya + jax.nn.sigmoid(pg_ref[:, D:2 * D]) * yb
        mg = merged.astype(BF16)
        m = _dot(mg, wo_ref[...])
        ya_ref[...] = ya.astype(BF16)
        yb_ref[...] = yb.astype(BF16)
        mg_ref[...] = mg
        m_ref[...] = m.astype(BF16)
        ho_ref[...] = h_ref[...] + ga_ref[0] * m

    tile = lambda w: pl.BlockSpec((tm, w), lambda i: (i, 0))
    full = lambda a: pl.BlockSpec(a.shape, lambda i: (0, 0))
    b16 = jax.ShapeDtypeStruct((T, D), BF16)
    return _call(
        body, name=name, grid=(T // tm,),
        in_specs=[tile(D_A), tile(D_B), tile(2 * D), tile(D), pl.BlockSpec((1, 1, D), lambda i: (i // tps, 0, 0)),
                  full(wa), full(wb), full(wo)],
        out_specs=[tile(D)] * 5, out_shape=[b16, b16, b16, b16, jax.ShapeDtypeStruct((T, D), F32)],
        args=[sgu, att, pg, hin, ga.reshape(B, 1, D), wa, wb, wo], comms=comms)


def ffn_bwd_act(dh, f, ga, wd, s, q, *, seq, tm, name, comms=()):
    T, D = dh.shape
    nq, fs, _ = wd.shape
    B, tps = T // seq, seq // tm

    def body(dh_ref, f_ref, ga_ref, wd_ref, s_ref, q_ref, dg_ref, du_ref, df_ref, dga_ref):
        i = pl.program_id(0)
        d = dh_ref[...]
        df = ((0.5 * ga_ref[0]) * d).astype(BF16)
        df_ref[...] = df
        part = 0.5 * jnp.sum(d * f_ref[...].astype(F32), axis=0, keepdims=True)
        for j in range(nq):
            da = _dg(df, wd_ref[j], NT)
            du_ref[j] = (da * s_ref[j].astype(F32)).astype(BF16)
            dg_ref[j] = (da * q_ref[j].astype(F32)).astype(BF16)

        @pl.when(i % tps == 0)
        def _():
            dga_ref[0] = part

        @pl.when(i % tps != 0)
        def _():
            dga_ref[0] += part

    tile = pl.BlockSpec((tm, D), lambda i: (i, 0))
    per_seq = pl.BlockSpec((1, 1, D), lambda i: (i // tps, 0, 0))
    act = pl.BlockSpec((nq, tm, fs), lambda i: (0, i, 0))
    o_shape = jax.ShapeDtypeStruct((nq, T, fs), BF16)
    dg, du, df, dga = _call(
        body, name=name, grid=(T // tm,), in_specs=[tile, tile, per_seq, _resident(wd), act, act],
        out_specs=[act, act, tile, per_seq],
        out_shape=[o_shape, o_shape, jax.ShapeDtypeStruct((T, D), BF16), jax.ShapeDtypeStruct((B, 1, D), F32)],
        args=[dh, f, ga.reshape(B, 1, D), wd, s, q], comms=comms)
    return dg, du, df, dga.reshape(B, D)


def mm_tn(pairs, *, tt, name, comms=()):
    n = len(pairs)
    flat = []
    for pair in pairs:
        for a in pair:
            if not any(a is b for b in flat):
                flat.append(a)
    where = lambda a: [k for k, b in enumerate(flat) if a is b][0]
    nq = max([a.shape[0] for a in flat if a.ndim == 3] + [1])
    T = flat[0].shape[-2]
    tt = min(tt, T)
    nt = T // tt
    stacked = [x.ndim == 3 or y.ndim == 3 for x, y in pairs]

    def body(*refs):
        in_refs, o_refs, accs = refs[:len(flat)], refs[len(flat):len(flat) + n], refs[len(flat) + n:]
        t = pl.program_id(1)
        value = lambda a: in_refs[where(a)][0] if a.ndim == 3 else in_refs[where(a)][...]

        def put(j, v):
            if stacked[j]:
                o_refs[j][0] = v.astype(BF16)
            else:
                o_refs[j][...] = v.astype(BF16)

        for j, (x, y) in enumerate(pairs):
            part = _dg(value(x), value(y), TN)
            if nt == 1:
                put(j, part)
                continue

            @pl.when(t == 0)
            def _():
                accs[j][...] = part

            @pl.when(t != 0)
            def _():
                accs[j][...] += part

        if nt > 1:
            @pl.when(t == nt - 1)
            def _():
                for j in range(n):
                    put(j, accs[j][...])

    def spec(a):
        if a.ndim == 3:
            return pl.BlockSpec((1, tt, a.shape[2]), lambda q, t: (q, t, 0))
        return pl.BlockSpec((tt, a.shape[1]), lambda q, t: (t, 0))

    out_specs, out_shape = [], []
    for j, (x, y) in enumerate(pairs):
        K, N = x.shape[-1], y.shape[-1]
        if stacked[j]:
            out_specs.append(pl.BlockSpec((1, K, N), lambda q, t: (q, 0, 0)))
            out_shape.append(jax.ShapeDtypeStruct((nq, K, N), BF16))
        else:
            out_specs.append(pl.BlockSpec((K, N), lambda q, t: (0, 0)))
            out_shape.append(jax.ShapeDtypeStruct((K, N), BF16))
    return _call(
        body, name=name, grid=(nq, nt), in_specs=[spec(a) for a in flat],
        out_specs=out_specs, out_shape=out_shape,
        scratch_shapes=[pltpu.VMEM((x.shape[-1], y.shape[-1]), F32) for x, y in pairs] if nt > 1 else [],
        args=flat, comms=comms)


def dx_norm_bwd(dys, ws, hin, dh_out, g, sc, *, seq, tm, name, comms=()):
    T, D = hin.shape
    B, tps = T // seq, seq // tm
    n = len(dys)

    def body(*refs):
        dy_refs, w_refs = refs[:n], refs[n:2 * n]
        h_ref, dho_ref, g_ref, sc_ref, dhi_ref, dsh_ref, dsc_ref, dgn_ref = refs[2 * n:]
        i = pl.program_id(0)
        dxn = None
        for j in range(n):
            if dys[j].ndim == 3:
                for q in range(dys[j].shape[0]):
                    part = _dot(dy_refs[j][q], w_refs[j][q])
                    dxn = part if dxn is None else dxn + part
            else:
                part = _dot(dy_refs[j][...], w_refs[j][...])
                dxn = part if dxn is None else dxn + part
        x = h_ref[...]
        nx, r = _rms_parts(x)
        gain = g_ref[...]
        one_sc = 1.0 + sc_ref[0]
        dsh = jnp.sum(dxn, axis=0, keepdims=True)
        dsc = jnp.sum(dxn * (nx * gain), axis=0, keepdims=True)
        dgn = jnp.sum(dxn * one_sc * nx, axis=0, keepdims=True)
        dn = dxn * one_sc * gain
        dhi_ref[...] = dho_ref[...] + r * (dn - nx * jnp.mean(dn * nx, axis=-1, keepdims=True))

        @pl.when(i % tps == 0)
        def _():
            dsh_ref[0] = dsh
            dsc_ref[0] = dsc

        @pl.when(i % tps != 0)
        def _():
            dsh_ref[0] += dsh
            dsc_ref[0] += dsc

        @pl.when(i == 0)
        def _():
            dgn_ref[...] = dgn

        @pl.when(i != 0)
        def _():
            dgn_ref[...] += dgn

    def dy_spec(a):
        if a.ndim == 3:
            return pl.BlockSpec((a.shape[0], tm, a.shape[2]), lambda i: (0, i, 0))
        return pl.BlockSpec((tm, a.shape[1]), lambda i: (i, 0))

    tile = pl.BlockSpec((tm, D), lambda i: (i, 0))
    per_seq = pl.BlockSpec((1, 1, D), lambda i: (i // tps, 0, 0))
    row = pl.BlockSpec((1, D), lambda i: (0, 0))
    dhi, dsh, dsc, dgn = _call(
        body, name=name, grid=(T // tm,),
        in_specs=[dy_spec(a) for a in dys] + [_resident(w) for w in ws] + [tile, tile, row, per_seq],
        out_specs=[tile, per_seq, per_seq, row],
        out_shape=[jax.ShapeDtypeStruct((T, D), F32), jax.ShapeDtypeStruct((B, 1, D), F32),
                   jax.ShapeDtypeStruct((B, 1, D), F32), jax.ShapeDtypeStruct((1, D), F32)],
        args=[*dys, *ws, hin, dh_out, g, sc.reshape(B, 1, D)], comms=comms)
    return dhi, dsh.reshape(B, D), dsc.reshape(B, D), dgn


def mixer_out_bwd(dh, m, ga, ya, yb, pg, wa, wb, wo, *, seq, tm, name, comms=()):
    T, D = dh.shape
    B, tps = T // seq, seq // tm

    def body(dh_ref, m_ref, ga_ref, ya_ref, yb_ref, pg_ref, wa_ref, wb_ref, wo_ref,
             dg_ref, dya_ref, dyb_ref, ds_ref, dt_ref, dm_ref, dga_ref):
        i = pl.program_id(0)
        d = dh_ref[...]
        dm = (ga_ref[0] * d).astype(BF16)
        dm_ref[...] = dm
        part = jnp.sum(d * m_ref[...].astype(F32), axis=0, keepdims=True)

        @pl.when(i % tps == 0)
        def _():
            dga_ref[0] = part

        @pl.when(i % tps != 0)
        def _():
            dga_ref[0] += part

        dmg = _dg(dm, wo_ref[...], NT)
        sa = jax.nn.sigmoid(pg_ref[:, 0:D])
        sb = jax.nn.sigmoid(pg_ref[:, D:2 * D])
        dg_ref[:, 0:D] = (dmg * ya_ref[...].astype(F32) * (sa * (1.0 - sa))).astype(BF16)
        dg_ref[:, D:2 * D] = (dmg * yb_ref[...].astype(F32) * (sb * (1.0 - sb))).astype(BF16)
        dya = (dmg * sa).astype(BF16)
        dyb = (dmg * sb).astype(BF16)
        dya_ref[...] = dya
        dyb_ref[...] = dyb
        ds_ref[...] = _dg(dya, wa_ref[...], NT)
        dt_ref[...] = _dg(dyb, wb_ref[...], NT)

    tile = lambda w: pl.BlockSpec((tm, w), lambda i: (i, 0))
    per_seq = pl.BlockSpec((1, 1, D), lambda i: (i // tps, 0, 0))
    dg, dya, dyb, ds, dt, dm, dga = _call(
        body, name=name, grid=(T // tm,),
        in_specs=[tile(D), tile(D), per_seq, tile(D), tile(D), tile(2 * D), _resident(wa), _resident(wb), _resident(wo)],
        out_specs=[tile(2 * D), tile(D), tile(D), tile(D_A), tile(D_B), tile(D), per_seq],
        out_shape=[jax.ShapeDtypeStruct((T, 2 * D), BF16), jax.ShapeDtypeStruct((T, D), BF16),
                   jax.ShapeDtypeStruct((T, D), BF16), jax.ShapeDtypeStruct((T, D_A), F32),
                   jax.ShapeDtypeStruct((T, D_B), F32), jax.ShapeDtypeStruct((T, D), BF16),
                   jax.ShapeDtypeStruct((B, 1, D), F32)],
        args=[dh, m, ga.reshape(B, 1, D), ya, yb, pg, wa, wb, wo], comms=comms)
    return dg, dya, dyb, ds, dt, dm, dga.reshape(B, D)


def sgu_bwd(puv, dsgu, gln, bln, ws, bs_t, *, cpb, name, comms=()):
    T = puv.shape[0]
    gd = D_A // N_GROUPS
    tm = cpb * CHUNK

    def body(p_ref, ds_ref, gln_ref, bln_ref, ws_ref, bs_ref, d_ref, dws_ref, dz_ref, dgl_ref, dbl_ref):
        i = pl.program_id(0)
        causal = _causal()
        wf = [jnp.where(causal, ws_ref[g], 0.0) for g in range(N_GROUPS)]
        wm = [w.astype(BF16) for w in wf]
        wt = [w.T.astype(BF16) for w in wf]
        dws = [jnp.zeros((CHUNK, CHUNK), F32) for _ in range(N_GROUPS)]
        dzs = jnp.zeros((CHUNK, D_A), F32)
        dgl = jnp.zeros((1, D_A), F32)
        dbl = jnp.zeros((1, D_A), F32)
        for j in range(cpb):
            rows = slice(j * CHUNK, (j + 1) * CHUNK)
            au = p_ref[rows, 0:D_A]
            av = p_ref[rows, D_A:2 * D_A]
            u, tu = _gelu_parts(au)
            vv, tv = _gelu_parts(av)
            vhat, rstd = _layer_norm_parts(vv)
            vn = (vhat * gln_ref[...] + bln_ref[...]).astype(BF16)
            dsg = ds_ref[rows, :]
            dz = dsg * u
            dzb = dz.astype(BF16)
            zs, dvns = [], []
            for g in range(N_GROUPS):
                cols = slice(g * gd, (g + 1) * gd)
                zs.append(_dot(wm[g], vn[:, cols]) + bs_ref[:, g:g + 1])
                dws[g] = dws[g] + _dg(dzb[:, cols], vn[:, cols], NT)
                dvns.append(_dot(wt[g], dzb[:, cols]))
            z = jnp.concatenate(zs, axis=1)
            dvn = jnp.concatenate(dvns, axis=1)
            d_ref[rows, 0:D_A] = (dsg * z * _dgelu(au, tu)).astype(BF16)
            dvh = dvn * gln_ref[...]
            dvv = rstd * (dvh - jnp.mean(dvh, axis=-1, keepdims=True)
                          - vhat * jnp.mean(dvh * vhat, axis=-1, keepdims=True))
            d_ref[rows, D_A:2 * D_A] = (dvv * _dgelu(av, tv)).astype(BF16)
            dzs = dzs + dz
            dgl = dgl + jnp.sum(dvn * vhat, axis=0, keepdims=True)
            dbl = dbl + jnp.sum(dvn, axis=0, keepdims=True)

        @pl.when(i == 0)
        def _():
            for g in range(N_GROUPS):
                dws_ref[g] = jnp.where(causal, dws[g], 0.0)
            dz_ref[...] = dzs
            dgl_ref[...] = dgl
            dbl_ref[...] = dbl

        @pl.when(i != 0)
        def _():
            for g in range(N_GROUPS):
                dws_ref[g] += jnp.where(causal, dws[g], 0.0)
            dz_ref[...] += dzs
            dgl_ref[...] += dgl
            dbl_ref[...] += dbl

    full = lambda a: pl.BlockSpec(a.shape, lambda i: (0,) * a.ndim)
    acc = lambda s: pl.BlockSpec(s, lambda i: (0,) * len(s))
    return _call(
        body, name=name, grid=(T // tm,),
        in_specs=[pl.BlockSpec((tm, 2 * D_A), lambda i: (i, 0)), pl.BlockSpec((tm, D_A), lambda i: (i, 0)),
                  full(gln), full(bln), full(ws), full(bs_t)],
        out_specs=[pl.BlockSpec((tm, 2 * D_A), lambda i: (i, 0)), acc((N_GROUPS, CHUNK, CHUNK)), acc((CHUNK, D_A)),
                   acc((1, D_A)), acc((1, D_A))],
        out_shape=[jax.ShapeDtypeStruct((T, 2 * D_A), BF16), jax.ShapeDtypeStruct((N_GROUPS, CHUNK, CHUNK), F32),
                   jax.ShapeDtypeStruct((CHUNK, D_A), F32), jax.ShapeDtypeStruct((1, D_A), F32),
                   jax.ShapeDtypeStruct((1, D_A), F32)],
        args=[puv, dsgu, gln, bln, ws, bs_t], comms=comms)


def attn_bwd(pqkv, datt, gq_t, gk_t, sinks, *, seq, name, comms=()):
    T = pqkv.shape[0]
    nb = seq // CHUNK

    def body(p_ref, do_ref, gq_ref, gk_ref, sink_ref, d_ref, dgq_ref, dgk_ref, dsk_ref,
             k0, k1, v0, v1, dk0, dk1, dv0, dv1, dgq_s, dsk_s):
        b = pl.program_id(0)
        krep, vrep, dkacc, dvacc = [k0, k1], [v0, v1], [dk0, dk1], [dv0, dv1]
        _fill_keys(p_ref, gk_ref, krep, vrep, seq)
        for h in range(N_KV):
            dkacc[h][...] = jnp.zeros_like(dkacc[h])
            dvacc[h][...] = jnp.zeros_like(dvacc[h])
        dgq_s[...] = jnp.zeros_like(dgq_s)
        dsk_s[...] = jnp.zeros_like(dsk_s)
        mean_q = _head_mean_matrix(D_B)
        lane = _iota((1, 128), 1)

        def block(i, carry):
            r0 = pl.multiple_of(i * CHUNK, CHUNK)
            nq, rq = _head_rms(p_ref[pl.ds(r0, CHUNK), 0:D_B], mean_q)
            qn = nq * gq_ref[...]
            valid = _attn_mask(i)
            dqn_parts = []
            for h in range(N_KV):
                cols = slice(h * KV_W, (h + 1) * KV_W)
                qs = _stack_heads(qn[:, cols])
                kk = krep[h][pl.ds(r0, 2 * CHUNK), :]
                vv = vrep[h][pl.ds(r0, 2 * CHUNK), :]
                probs, psink = _attn_probs(qs, kk, valid, _sink_column(sink_ref, h))
                dos = _stack_heads(do_ref[pl.ds(r0, CHUNK), cols])
                dp = _dg(dos, vv, NT)
                dr = jnp.sum(probs * dp, axis=-1, keepdims=True)
                ds = (probs * (dp - dr) * ATT_SCALE).astype(BF16)
                dsink = psink * dr
                for g in range(Q_PER_KV):
                    dsk_s[...] += jnp.where(lane == h * Q_PER_KV + g, -jnp.sum(dsink[g * CHUNK:(g + 1) * CHUNK]), 0.0)
                dqn_parts.append(_unstack_heads(_dot(ds, kk)))
                dkacc[h][pl.ds(r0, 2 * CHUNK), :] += _dg(ds, qs, TN)
                dvacc[h][pl.ds(r0, 2 * CHUNK), :] += _dg(probs.astype(BF16), dos, TN)
            dqn = jnp.concatenate(dqn_parts, axis=1)
            dn = dqn * gq_ref[...]
            d_ref[pl.ds(r0, CHUNK), 0:D_B] = (rq * (dn - nq * _dot_split(dn * nq, mean_q))).astype(BF16)
            dgq_s[...] += jnp.sum(dqn * nq, axis=0, keepdims=True)
            return carry

        lax.fori_loop(0, nb, block, 0)

        mean_k = _head_mean_matrix(KV2)
        folds = [_fold_matrix(h) for h in range(N_KV)]
        rows = min(seq, 4 * CHUNK)

        def piece(j, dgk):
            r0 = pl.multiple_of(j * rows, CHUNK)
            r1 = pl.multiple_of(r0 + CHUNK, CHUNK)
            dkn = _dot_split(dkacc[0][pl.ds(r1, rows), :], folds[0]) + _dot_split(dkacc[1][pl.ds(r1, rows), :], folds[1])
            dv = _dot_split(dvacc[0][pl.ds(r1, rows), :], folds[0]) + _dot_split(dvacc[1][pl.ds(r1, rows), :], folds[1])
            nk, rk = _head_rms(p_ref[pl.ds(r0, rows), K_OFF:K_OFF + KV2], mean_k)
            dn = dkn * gk_ref[...]
            d_ref[pl.ds(r0, rows), K_OFF:K_OFF + KV2] = (rk * (dn - nk * _dot_split(dn * nk, mean_k))).astype(BF16)
            d_ref[pl.ds(r0, rows), V_OFF:V_OFF + KV2] = dv.astype(BF16)
            return dgk + jnp.sum(dkn * nk, axis=0, keepdims=True)

        dgk = lax.fori_loop(0, seq // rows, piece, jnp.zeros((1, KV2), F32))

        @pl.when(b == 0)
        def _():
            dgq_ref[...] = dgq_s[...]
            dgk_ref[...] = dgk
            dsk_ref[...] = jnp.broadcast_to(dsk_s[...], dsk_ref.shape)

        @pl.when(b != 0)
        def _():
            dgq_ref[...] += dgq_s[...]
            dgk_ref[...] += dgk
            dsk_ref[...] += jnp.broadcast_to(dsk_s[...], dsk_ref.shape)

    acc = lambda s: pl.BlockSpec(s, lambda b: (0, 0))
    return _call(
        body, name=name, grid=(T // seq,),
        in_specs=[pl.BlockSpec((seq, QKV_W), lambda b: (b, 0)), pl.BlockSpec((seq, D_B), lambda b: (b, 0)),
                  pl.BlockSpec(gq_t.shape, lambda b: (0, 0)), pl.BlockSpec(gk_t.shape, lambda b: (0, 0)),
                  pl.BlockSpec(memory_space=pltpu.SMEM)],
        out_specs=[pl.BlockSpec((seq, QKV_W), lambda b: (b, 0)), acc((1, D_B)), acc((1, KV2)), acc((8, 128))],
        out_shape=[jax.ShapeDtypeStruct((T, QKV_W), BF16), jax.ShapeDtypeStruct((1, D_B), F32),
                   jax.ShapeDtypeStruct((1, KV2), F32), jax.ShapeDtypeStruct((8, 128), F32)],
        scratch_shapes=[pltpu.VMEM((seq + CHUNK, KV_W), BF16)] * 4 + [pltpu.VMEM((seq + CHUNK, KV_W), F32)] * 4
        + [pltpu.VMEM((1, D_B), F32), pltpu.VMEM((1, 128), F32)],
        args=[pqkv, datt, gq_t, gk_t, sinks], comms=comms)


def ada_fwd(c_all, w, b, *, name):
    nb, D = c_all.shape
    N = w.shape[1]
    tn = N // 3

    def body(c_ref, w_ref, b_ref, o_ref):
        c = c_ref[...]
        cond = (c * jax.nn.sigmoid(c)).astype(BF16)
        o_ref[...] = _dot(cond, w_ref[...].astype(BF16)) + b_ref[...]

    return _call(
        body, name=name, grid=(3,),
        in_specs=[pl.BlockSpec((nb, D), lambda j: (0, 0)), pl.BlockSpec((D, tn), lambda j: (0, j)),
                  pl.BlockSpec((1, tn), lambda j: (0, j))],
        out_specs=[pl.BlockSpec((nb, tn), lambda j: (0, j))], out_shape=[jax.ShapeDtypeStruct((nb, N), F32)],
        args=[c_all, w, b])[0]


def ada_bwd(c_all, dmods_all, dmods_mine, gain_rows, *, name, comms=()):
    nb, D = c_all.shape
    NA = dmods_all.shape[1]
    N = dmods_mine.shape[1]
    tn = N // 3

    def body(c_ref, da_ref, dm_ref, gr_ref, db_ref, dw_ref, dgn_ref):
        c = c_ref[...]
        cond = (c * jax.nn.sigmoid(c)).astype(BF16)
        dw_ref[...] = _dg(cond, dm_ref[...].astype(BF16), TN)
        db_ref[...] = jnp.sum(da_ref[...], axis=0, keepdims=True)
        total = gr_ref[0:1, :]
        for d in range(1, N_DEV):
            total = total + gr_ref[d:d + 1, :]
        dgn_ref[...] = total

    return _call(
        body, name=name, grid=(3,),
        in_specs=[pl.BlockSpec((nb, D), lambda j: (0, 0)), pl.BlockSpec((nb, NA), lambda j: (0, 0)),
                  pl.BlockSpec((nb, tn), lambda j: (0, j)), pl.BlockSpec((N_DEV, D), lambda j: (0, 0))],
        out_specs=[pl.BlockSpec((1, NA), lambda j: (0, 0)), pl.BlockSpec((D, tn), lambda j: (0, j)),
                   pl.BlockSpec((1, D), lambda j: (0, 0))],
        out_shape=[jax.ShapeDtypeStruct((1, NA), F32), jax.ShapeDtypeStruct((D, N), F32),
                   jax.ShapeDtypeStruct((1, D), F32)],
        args=[c_all, dmods_all, dmods_mine, gain_rows], comms=comms)


def adamw(w, g, m, v, *, name):
    R, C = w.shape
    rb = _row_block(R, max(8, (1 << 18) // C))
    c1 = 1.0 / (1.0 - ADAM_B1 ** ADAM_STEP)
    c2 = 1.0 / (1.0 - ADAM_B2 ** ADAM_STEP)

    def body(w_ref, g_ref, m_ref, v_ref, d_ref, mo_ref, vo_ref):
        gg = g_ref[...]
        mn = ADAM_B1 * m_ref[...] + (1.0 - ADAM_B1) * gg
        vn = ADAM_B2 * v_ref[...] + (1.0 - ADAM_B2) * (gg * gg)
        mo_ref[...] = mn
        vo_ref[...] = vn
        d_ref[...] = -ADAM_LR * ((mn * c1) / (jnp.sqrt(vn * c2) + ADAM_EPS) + ADAM_WD * w_ref[...])

    blk = pl.BlockSpec((rb, C), lambda i: (i, 0))
    shp = jax.ShapeDtypeStruct((R, C), F32)
    return _call(body, name=name, grid=(R // rb,), in_specs=[blk] * 4, out_specs=[blk] * 3, out_shape=[shp] * 3,
                 args=[w, g, m, v])


def _place():
    return lax.axis_index("x"), lax.axis_index("y"), lax.axis_index("c")


def _flip(v, bit):
    return 1 - v if bit else v


def _other_chips(mx, my):
    return [(_flip(mx, k & 2), _flip(my, k & 1)) for k in range(1, N_QUAD)]


def _remote(src, dst, send_sem, recv_sem, peer):
    return pltpu.make_async_remote_copy(src_ref=src, dst_ref=dst, send_sem=send_sem, recv_sem=recv_sem,
                                        device_id=peer, device_id_type=MESH)


def all_gather8(x, *, name, reduce=False, comms=()):
    r, n = x.shape

    def body(x_ref, o_ref, *rest):
        if reduce:
            buf, send_sems, recv_sems, lsem = rest
        else:
            buf = o_ref
            send_sems, recv_sems, lsem = rest
        mx, my, mc = _place()
        me = 4 * mx + 2 * my + mc
        mine = pltpu.make_async_copy(x_ref, buf.at[me], lsem)
        mine.start()
        sends, recvs = [], []
        for k in range(1, N_DEV):
            peer = (_flip(mx, k & 4), _flip(my, k & 2), _flip(mc, k & 1))
            pidx = 4 * peer[0] + 2 * peer[1] + peer[2]
            sends.append(_remote(x_ref, buf.at[me], send_sems.at[k - 1], recv_sems.at[k - 1], peer))
            recvs.append(_remote(x_ref, buf.at[pidx], send_sems.at[k - 1], recv_sems.at[k - 1], peer))
        for cp in sends:
            cp.start()
        for cp in recvs:
            cp.wait_recv()
        for cp in sends:
            cp.wait_send()
        mine.wait()
        if reduce:
            total = buf[0]
            for d in range(1, N_DEV):
                total = total + buf[d]
            o_ref[...] = total

    scratch = [pltpu.SemaphoreType.DMA((N_DEV - 1,)), pltpu.SemaphoreType.DMA((N_DEV - 1,)), pltpu.SemaphoreType.DMA]
    if reduce:
        scratch = [pltpu.VMEM((N_DEV, r, n), F32)] + scratch
        out_shape = jax.ShapeDtypeStruct((r, n), F32)
    else:
        out_shape = jax.ShapeDtypeStruct((N_DEV, r, n), F32)
    vmem = pl.BlockSpec(memory_space=pltpu.VMEM)
    return _call(body, name=name, grid=(), in_specs=[vmem], out_specs=[vmem], out_shape=[out_shape], args=[x],
                 scratch_shapes=scratch, comms=comms)[0]


def ag8_comm(x):
    def copies(srcs, lands, ss, rs, only_sends=False):
        mx, my, mc = _place()
        me = 4 * mx + 2 * my + mc
        local = pltpu.make_async_copy(srcs[0], lands[0].at[me], ss.at[N_DEV - 1])
        sends, recvs = [], []
        for k in range(1, N_DEV):
            peer = (_flip(mx, k & 4), _flip(my, k & 2), _flip(mc, k & 1))
            sends.append(_remote(srcs[0], lands[0].at[me], ss.at[k - 1], rs.at[k - 1], peer))
            if not only_sends:
                recvs.append(_remote(srcs[0], lands[0].at[4 * peer[0] + 2 * peer[1] + peer[2]], ss.at[k - 1], rs.at[k - 1], peer))
        return local, sends, recvs

    def start(srcs, bufs, lands, ss, rs):
        local, sends, _ = copies(srcs, lands, ss, rs, only_sends=True)
        local.start()
        for cp in sends:
            cp.start()

    def finish(srcs, bufs, lands, ss, rs):
        local, sends, recvs = copies(srcs, lands, ss, rs)
        for cp in recvs:
            cp.wait_recv()
        for cp in sends:
            cp.wait_send()
        local.wait()

    return Comm(srcs=[x], land_shapes=[jax.ShapeDtypeStruct((N_DEV,) + x.shape, x.dtype)], n_sems=N_DEV,
                start=start, finish=finish)


def sum8(stacked, *, name):
    _, r, n = stacked.shape

    def body(s_ref, o_ref):
        total = s_ref[0]
        for d in range(1, N_DEV):
            total = total + s_ref[d]
        o_ref[...] = total

    return _call(body, name=name, grid=(), in_specs=[pl.BlockSpec(memory_space=pltpu.VMEM)],
                 out_specs=[pl.BlockSpec(memory_space=pltpu.VMEM)], out_shape=[jax.ShapeDtypeStruct((r, n), F32)],
                 args=[stacked])[0]


def cast_into_stacks(ws, quad, *, name, comms=()):
    steps = 4

    def body(q_ref, *refs):
        for w_ref, o_ref in zip(refs[:len(ws)], refs[len(ws):]):
            o_ref[0] = w_ref[...].astype(BF16)

    return _call(
        body, name=name, grid=(steps,), prefetch=[quad],
        in_specs=[pl.BlockSpec((w.shape[0] // steps, w.shape[1]), lambda i, q: (i, 0)) for w in ws],
        out_specs=[pl.BlockSpec((1, w.shape[0] // steps, w.shape[1]), lambda i, q: (q[0], i, 0)) for w in ws],
        out_shape=[jax.ShapeDtypeStruct((N_QUAD,) + w.shape, BF16) for w in ws], args=list(ws), comms=comms)


def gather_comm(stacks):
    n = len(stacks)

    def copies(bufs, ss, rs, only_sends=False):
        mx, my, mc = _place()
        q = 2 * mx + my
        sibling = (mx, my, 1 - mc)
        ici_send, ici_recv, fwd_send, fwd_recv = [], [], [], []
        for p in range(n):
            hr = stacks[p].shape[1] // 2
            mine, other = pl.ds(mc * hr, hr), pl.ds((1 - mc) * hr, hr)
            for k, chip in enumerate(_other_chips(mx, my)):
                qk = 2 * chip[0] + chip[1]
                peer = (chip[0], chip[1], mc)
                own, landed, theirs = bufs[p].at[q, mine, :], bufs[p].at[qk, mine, :], bufs[p].at[qk, other, :]
                ici_send.append(_remote(own, own, ss.at[6 * p + k], rs.at[6 * p + k], peer))
                if only_sends:
                    continue
                ici_recv.append(_remote(own, landed, ss.at[6 * p + k], rs.at[6 * p + k], peer))
                fwd_send.append(_remote(landed, landed, ss.at[6 * p + 3 + k], rs.at[6 * p + 3 + k], sibling))
                fwd_recv.append(_remote(theirs, theirs, ss.at[6 * p + 3 + k], rs.at[6 * p + 3 + k], sibling))
        return ici_send, ici_recv, fwd_send, fwd_recv

    def start(srcs, bufs, lands, ss, rs):
        for cp in copies(bufs, ss, rs, only_sends=True)[0]:
            cp.start()

    def finish(srcs, bufs, lands, ss, rs):
        ici_send, ici_recv, fwd_send, fwd_recv = copies(bufs, ss, rs)
        for arrived, onward in zip(ici_recv, fwd_send):
            arrived.wait_recv()
            onward.start()
        for cp in fwd_recv:
            cp.wait_recv()
        for cp in ici_send + fwd_send:
            cp.wait_send()

    return Comm(bufs=stacks, n_sems=6 * n, start=start, finish=finish)


def rs_pair_comm(gs):
    n = len(gs)

    def copies(srcs, lands, ss, rs):
        mx, my, mc = _place()
        out = []
        for p in range(n):
            hr = gs[p].shape[1] // 2
            out.append(_remote(srcs[p].at[:, pl.ds((1 - mc) * hr, hr), :], lands[p], ss.at[p], rs.at[p], (mx, my, 1 - mc)))
        return out

    def start(srcs, bufs, lands, ss, rs):
        for cp in copies(srcs, lands, ss, rs):
            cp.start()

    def finish(srcs, bufs, lands, ss, rs):
        for cp in copies(srcs, lands, ss, rs):
            cp.wait()

    return Comm(srcs=gs, land_shapes=[jax.ShapeDtypeStruct((g.shape[0], g.shape[1] // 2, g.shape[2]), g.dtype) for g in gs],
                n_sems=n, start=start, finish=finish)


def rs_chip_comm(ss_):
    n = len(ss_)

    def copies(srcs, lands, ss, rs):
        mx, my, mc = _place()
        out = []
        for p in range(n):
            for k, chip in enumerate(_other_chips(mx, my)):
                out.append(_remote(srcs[p].at[2 * chip[0] + chip[1]], lands[p].at[k], ss.at[3 * p + k], rs.at[3 * p + k],
                                   (chip[0], chip[1], mc)))
        return out

    def start(srcs, bufs, lands, ss, rs):
        for cp in copies(srcs, lands, ss, rs):
            cp.start()

    def finish(srcs, bufs, lands, ss, rs):
        for cp in copies(srcs, lands, ss, rs):
            cp.wait()

    return Comm(srcs=ss_, land_shapes=[jax.ShapeDtypeStruct((N_QUAD - 1,) + s.shape[1:], s.dtype) for s in ss_],
                n_sems=3 * n, start=start, finish=finish)


def rs_share_comm(fulls):
    n = len(fulls)

    def copies(bufs, ss, rs, only_sends=False):
        mx, my, mc = _place()
        send, recv = [], []
        for p in range(n):
            hr = fulls[p].shape[0] // 2
            mine, other = bufs[p].at[pl.ds(mc * hr, hr), :], bufs[p].at[pl.ds((1 - mc) * hr, hr), :]
            send.append(_remote(mine, mine, ss.at[p], rs.at[p], (mx, my, 1 - mc)))
            if not only_sends:
                recv.append(_remote(other, other, ss.at[p], rs.at[p], (mx, my, 1 - mc)))
        return send, recv

    def start(srcs, bufs, lands, ss, rs):
        for cp in copies(bufs, ss, rs, only_sends=True)[0]:
            cp.start()

    def finish(srcs, bufs, lands, ss, rs):
        send, recv = copies(bufs, ss, rs)
        for cp in recv:
            cp.wait_recv()
        for cp in send:
            cp.wait_send()

    return Comm(bufs=fulls, n_sems=n, start=start, finish=finish)


def pair_sum(g, recv, mc, *, name):
    nq, R, C = g.shape
    hr = R // 2
    rb = _row_block(hr, 256)
    nb = hr // rb

    def body(s_ref, g_ref, r_ref, o_ref):
        o_ref[...] = (g_ref[...].astype(F32) + r_ref[...].astype(F32)).astype(BF16)

    return pl.pallas_call(
        body, name=name, out_shape=jax.ShapeDtypeStruct((nq, hr, C), BF16),
        grid_spec=pltpu.PrefetchScalarGridSpec(
            num_scalar_prefetch=1, grid=(nq, nb),
            in_specs=[pl.BlockSpec((1, rb, C), lambda q, i, s: (q, s[0] * nb + i, 0)),
                      pl.BlockSpec((1, rb, C), lambda q, i, s: (q, i, 0))],
            out_specs=pl.BlockSpec((1, rb, C), lambda q, i, s: (q, i, 0))),
        compiler_params=pltpu.CompilerParams(dimension_semantics=("arbitrary", "arbitrary"),
                                             vmem_limit_bytes=VMEM_LIMIT_BYTES),
    )(mc, g, recv)


def chip_sum(s, recv, quad_mc, *, name):
    nq, hr, C = s.shape
    rb = _row_block(hr, 256)
    nb = hr // rb

    def body(q_ref, s_ref, r_ref, o_ref):
        total = s_ref[0].astype(F32)
        for k in range(N_QUAD - 1):
            total = total + r_ref[k].astype(F32)
        o_ref[...] = total

    return pl.pallas_call(
        body, name=name, out_shape=jax.ShapeDtypeStruct((2 * hr, C), F32),
        grid_spec=pltpu.PrefetchScalarGridSpec(
            num_scalar_prefetch=1, grid=(nb,),
            in_specs=[pl.BlockSpec((1, rb, C), lambda i, q: (q[0], i, 0)),
                      pl.BlockSpec((N_QUAD - 1, rb, C), lambda i, q: (0, i, 0))],
            out_specs=pl.BlockSpec((rb, C), lambda i, q: (q[1] * nb + i, 0))),
        compiler_params=pltpu.CompilerParams(dimension_semantics=("arbitrary",), vmem_limit_bytes=VMEM_LIMIT_BYTES),
    )(quad_mc, s, recv)


def _stack_cols(w_full):
    K, N = w_full.shape
    return w_full.reshape(K, N_QUAD, N // N_QUAD).transpose(1, 0, 2)


def _unstack_cols(w4):
    nq, K, n = w4.shape
    return w4.transpose(1, 0, 2).reshape(K, nq * n)


def kernel(x, c, w_ada, b_ada, g_norm1, ffn1_w_gate, ffn1_w_up, ffn1_w_down, g_norm2, w_in, g_sgu_ln, b_sgu_ln, w_spatial, b_spatial, g_q, g_k, attn_sinks, w_branch_a, w_branch_b, w_out, g_norm3, ffn2_w_gate, ffn2_w_up, ffn2_w_down, loss_target, m_w_ada, m_b_ada, m_g_norm1, m_ffn1_w_gate, m_ffn1_w_up, m_ffn1_w_down, m_g_norm2, m_w_in, m_g_sgu_ln, m_b_sgu_ln, m_w_spatial, m_b_spatial, m_g_q, m_g_k, m_attn_sinks, m_w_branch_a, m_w_branch_b, m_w_out, m_g_norm3, m_ffn2_w_gate, m_ffn2_w_up, m_ffn2_w_down, v_w_ada, v_b_ada, v_g_norm1, v_ffn1_w_gate, v_ffn1_w_up, v_ffn1_w_down, v_g_norm2, v_w_in, v_g_sgu_ln, v_b_sgu_ln, v_w_spatial, v_b_spatial, v_g_q, v_g_k, v_attn_sinks, v_w_branch_a, v_w_branch_b, v_w_out, v_g_norm3, v_ffn2_w_gate, v_ffn2_w_up, v_ffn2_w_down):
    B, S, D = x.shape
    T = B * S
    n_mod = b_ada.shape[1] // D
    mx, my, mc = _place()
    quad = 2 * mx + my
    mc_arr = jnp.reshape(mc, (1,)).astype(jnp.int32)
    quad_arr = jnp.reshape(quad, (1,)).astype(jnp.int32)
    quad_mc = jnp.stack([quad, mc]).astype(jnp.int32)
    tm = min(512, S)
    tm_small = min(256, S)
    tm_big = min(1024, S)
    tt_half = min(2048, T)
    cpb = min(4, S // CHUNK)

    xt = x.reshape(T, D)
    tgt = loss_target.reshape(T, D)

    tr = lambda a: jnp.swapaxes(a, 1, 2)
    stack_wg1, stack_wu1 = cast_into_stacks([tr(ffn1_w_gate)[0], tr(ffn1_w_up)[0]], quad_arr, name="stack_gu1")
    gather_wg1, gather_wu1 = gather_comm([stack_wg1]), gather_comm([stack_wu1])
    later = [ffn1_w_down, tr(w_in), w_branch_a, w_branch_b, w_out, tr(ffn2_w_gate), tr(ffn2_w_up), ffn2_w_down]
    stacks = cast_into_stacks([w[0] for w in later], quad_arr, name="stack_rest", comms=[gather_wg1])
    (wg1,) = gather_wg1.bufs_out
    gather_wd1, gather_win = gather_comm([stacks[0]]), gather_comm([stacks[1]])
    gather_abo = gather_comm(stacks[2:5])
    gather_wg3, gather_wu3, gather_wd3 = gather_comm([stacks[5]]), gather_comm([stacks[6]]), gather_comm([stacks[7]])

    c_all = all_gather8(c, name="gather_cond").reshape(N_DEV * B, D)
    n_ada = w_ada.shape[2]
    b_mine = lax.dynamic_slice(b_ada, (0, quad * n_ada), (1, n_ada))
    mods_part = ada_fwd(c_all, w_ada[0], b_mine, name="ada_fwd")
    mods_parts = all_gather8(mods_part, name="gather_mods", comms=[gather_wu1])
    (wu1,) = gather_wu1.bufs_out
    mods = jnp.concatenate([mods_parts[2 * j] for j in range(N_QUAD)], axis=1)
    me = 4 * mx + 2 * my + mc
    mods = lax.dynamic_slice(mods, (me * B, 0), (B, n_mod * D))
    sh1, sc1, ga1, sh2, sc2, ga2, sh3, sc3, ga3 = [mods[:, j * D:(j + 1) * D] for j in range(n_mod)]
    bs_t = b_spatial[0].T
    ws = w_spatial[0]

    xn1 = norm_mod_fwd(xt, g_norm1, sc1, sh1, seq=S, tm=tm, name="norm1")
    g1, u1, a1 = ffn_up(xn1, wg1, wu1, tm=tm_big, name="ffn1_up", comms=[gather_wd1, gather_win])
    (wd1,), (win4,) = gather_wd1.bufs_out, gather_win.bufs_out
    win = win4.reshape(-1, D)
    w_uv, w_qkv, w_gt = win[0:2 * D_A], win[2 * D_A:2 * D_A + QKV_W], win[2 * D_A + QKV_W:]
    h1, f1 = ffn_down(a1, wd1, xt, ga1, seq=S, tm=tm, name="ffn1_down", comms=[gather_abo])
    wa4, wb4, wo4 = gather_abo.bufs_out
    wa, wb = _unstack_cols(wa4), _unstack_cols(wb4)
    wo = wo4.reshape(D, D)
    xn2 = norm_mod_fwd(h1, g_norm2, sc2, sh2, seq=S, tm=tm, name="norm2")
    puv, pqkv, pgt = proj_fwd(xn2, [w_uv, w_qkv, w_gt], tm=tm_small, name="proj", comms=[gather_wg3])
    (wg3,) = gather_wg3.bufs_out
    sgu = sgu_fwd(puv, g_sgu_ln, b_sgu_ln, ws, bs_t, cpb=cpb, name="sgu_fwd")
    gq_t, gk_t = jnp.tile(g_q, (1, N_Q)), jnp.tile(g_k, (1, N_KV))
    att = attn_fwd(pqkv, gq_t, gk_t, attn_sinks, seq=S, name="attn_fwd", comms=[gather_wu3])
    (wu3,) = gather_wu3.bufs_out
    ya, yb, merged, mm, h2 = mixer_out_fwd(sgu, att, pgt, h1, ga2, wa, wb, wo, seq=S, tm=tm_small, name="mixer_out",
                                           comms=[gather_wd3])
    (wd3,) = gather_wd3.bufs_out
    xn3 = norm_mod_fwd(h2, g_norm3, sc3, sh3, seq=S, tm=tm, name="norm3")
    g3, u3, a3 = ffn_up(xn3, wg3, wu3, tm=tm_big, name="ffn2_up")
    dy, f3, lparts = ffn_down(a3, wd3, h2, ga3, tgt, seq=S, tm=tm, name="ffn2_down_loss")
    loss_part = jnp.sum(lparts[:, 0, 0])

    def sums_of(pair, tag):
        return [pair_sum(g, r, mc_arr, name=f"pair_sum_{tag}_{p}") for p, (g, r) in enumerate(zip(pair.srcs, pair.lands))]

    def halves_of(chip, tag):
        return [chip_sum(s, r, quad_mc, name=f"chip_sum_{tag}_{p}") for p, (s, r) in enumerate(zip(chip.srcs, chip.lands))]

    dg3, du3, df3, dga3 = ffn_bwd_act(dy, f3, ga3, wd3, g3, u3, seq=S, tm=tm, name="ffn2_act_bwd")
    (dwd3,) = mm_tn([(a3, df3)], tt=T, name="ffn2_dwd")
    pair_wd3 = rs_pair_comm([dwd3])
    dwg3, dwu3 = mm_tn([(dg3, xn3), (du3, xn3)], tt=tt_half, name="ffn2_dwgu", comms=[pair_wd3])
    chip_wd3 = rs_chip_comm(sums_of(pair_wd3, "wd3"))
    pair_gu3 = rs_pair_comm([dwg3, dwu3])
    dh2, dsh3, dsc3, dgn3 = dx_norm_bwd([dg3, du3], [wg3, wu3], h2, dy, g_norm3, sc3, seq=S, tm=tm, name="ffn2_dx",
                                        comms=[chip_wd3, pair_gu3])
    sum_wg3, sum_wu3 = sums_of(pair_gu3, "gu3")
    chip_wg3, chip_wu3 = rs_chip_comm([sum_wg3]), rs_chip_comm([sum_wu3])

    dgt, dya, dyb, dsgu, datt, dm, dga2 = mixer_out_bwd(dh2, mm, ga2, ya, yb, pgt, wa, wb, wo, seq=S, tm=tm_small,
                                                        name="mixer_out_bwd", comms=[chip_wg3])
    (dwo,) = mm_tn([(merged, dm)], tt=tt_half, name="mixer_dwo")
    (dwa,) = mm_tn([(sgu, dya)], tt=tt_half, name="mixer_dwa")
    (dwb,) = mm_tn([(att, dyb)], tt=tt_half, name="mixer_dwb")
    pair_abo = rs_pair_comm([_stack_cols(dwa), _stack_cols(dwb), dwo.reshape(N_QUAD, D // N_QUAD, D)])
    duv, dws, dzs, dgln, dbln = sgu_bwd(puv, dsgu, g_sgu_ln, b_sgu_ln, ws, bs_t, cpb=cpb, name="sgu_bwd",
                                        comms=[pair_abo])
    chip_abo = rs_chip_comm(sums_of(pair_abo, "abo"))
    dqkv, dgq_h, dgk_h, dsk = attn_bwd(pqkv, datt, gq_t, gk_t, attn_sinks, seq=S, name="attn_bwd",
                                       comms=[chip_wu3, chip_abo])
    dgq = dgq_h.reshape(N_Q, HEAD_DIM).sum(axis=0, keepdims=True)
    dgk = dgk_h.reshape(N_KV, HEAD_DIM).sum(axis=0, keepdims=True)
    share3 = rs_share_comm(halves_of(chip_wg3, "wg3") + halves_of(chip_wu3, "wu3") + halves_of(chip_wd3, "wd3"))
    dwin_uv, dwin_qkv = mm_tn([(duv, xn2), (dqkv, xn2)], tt=tt_half, name="mixer_dwin_a", comms=[share3])
    (dwin_gt,) = mm_tn([(dgt, xn2)], tt=tt_half, name="mixer_dwin_b")
    dwin_parts = [dwin_uv, dwin_qkv, dwin_gt]
    r_wg3, r_wu3, r_wd3 = share3.bufs_out
    pair_win = rs_pair_comm([jnp.concatenate(dwin_parts, axis=0).reshape(win4.shape)])
    dh1, dsh2, dsc2, dgn2 = dx_norm_bwd([duv, dqkv, dgt], [w_uv, w_qkv, w_gt], h1, dh2, g_norm2, sc2, seq=S,
                                        tm=tm, name="mixer_dx", comms=[pair_win])
    chip_win = rs_chip_comm(sums_of(pair_win, "win"))

    dg1, du1, df1, dga1 = ffn_bwd_act(dh1, f1, ga1, wd1, g1, u1, seq=S, tm=tm, name="ffn1_act_bwd", comms=[chip_win])
    share_mix = rs_share_comm(halves_of(chip_win, "win") + halves_of(chip_abo, "abo"))

    db_s = dzs.reshape(CHUNK, N_GROUPS, D_A // N_GROUPS).sum(axis=2).T.reshape(1, N_GROUPS * CHUNK)
    tail = jnp.concatenate([db_s, dgq, dgk, dsk[0:1, 0:N_Q], loss_part.reshape(1, 1)], axis=1)
    tail = jnp.pad(tail, ((0, 0), (0, (-tail.shape[1]) % D)))
    lnrow = jnp.concatenate([dgln, dbln], axis=1)
    lnrow = jnp.pad(lnrow, ((0, 0), (0, (-lnrow.shape[1]) % D)))
    packed = jnp.concatenate([dgn2, dgn3, lnrow.reshape(-1, D), tail.reshape(-1, D), dws.reshape(-1, D)], axis=0)
    n_rows = packed.shape[0]
    packed = jnp.pad(packed, ((0, (-n_rows) % 8), (0, 0)))
    gather_small = ag8_comm(packed)

    dwg1, dwu1 = mm_tn([(dg1, xn1), (du1, xn1)], tt=tt_half, name="ffn1_dwgu", comms=[share_mix, gather_small])
    r_win, r_wa, r_wb, r_wo = share_mix.bufs_out
    small = sum8(gather_small.lands[0], name="sum_small")
    pair_gu1 = rs_pair_comm([dwg1, dwu1])
    (dwd1,) = mm_tn([(a1, df1)], tt=T, name="ffn1_dwd", comms=[pair_gu1])
    chip_gu1 = rs_chip_comm(sums_of(pair_gu1, "gu1"))
    pair_wd1 = rs_pair_comm([dwd1])
    dx, dsh1, dsc1, dgn1 = dx_norm_bwd([dg1, du1], [wg1, wu1], xt, dh1, g_norm1, sc1, seq=S, tm=tm, name="ffn1_dx",
                                       comms=[chip_gu1, pair_wd1])
    chip_wd1 = rs_chip_comm(sums_of(pair_wd1, "wd1"))
    share_gu1 = rs_share_comm(halves_of(chip_gu1, "gu1"))

    dmods = jnp.concatenate([dsh1, dsc1, dga1, dsh2, dsc2, dga2, dsh3, dsc3, dga3], axis=1)
    dmods = jnp.concatenate([dmods, jnp.pad(dgn1, ((0, 0), (0, (n_mod - 1) * D)))], axis=0)
    gathered = all_gather8(dmods, name="gather_dmods", comms=[chip_wd1, share_gu1])
    r_wg1, r_wu1 = share_gu1.bufs_out
    dmods_all = gathered[:, 0:B].reshape(N_DEV * B, n_mod * D)
    dmods_mine = lax.dynamic_slice(dmods_all, (0, quad * n_ada), (N_DEV * B, n_ada))
    share_wd1 = rs_share_comm(halves_of(chip_wd1, "wd1"))
    db_ada, dw_ada, g_gn1 = ada_bwd(c_all, dmods_all, dmods_mine, gathered[:, B, 0:D], name="ada_bwd", comms=[share_wd1])
    (r_wd1,) = share_wd1.bufs_out

    ln_rows = lnrow.size // D
    tail_rows = tail.size // D
    r = 2
    g_gn2, g_gn3 = small[0:1], small[1:2]
    ln_flat = small[r:r + ln_rows].reshape(1, -1)
    r += ln_rows
    tail_flat = small[r:r + tail_rows].reshape(1, -1)
    r += tail_rows
    g_ws = small[r:r + dws.size // D].reshape(w_spatial.shape)
    g_gln, g_bln = ln_flat[:, 0:D_A], ln_flat[:, D_A:2 * D_A]
    o = N_GROUPS * CHUNK
    g_bs = tail_flat[:, 0:o].reshape(b_spatial.shape)
    g_gq, g_gk, g_sk = tail_flat[:, o:o + HEAD_DIM], tail_flat[:, o + HEAD_DIM:o + 2 * HEAD_DIM], tail_flat[:, o + 2 * HEAD_DIM:o + 2 * HEAD_DIM + N_Q]
    loss = tail_flat[0, o + 2 * HEAD_DIM + N_Q]

    transposed = ("ffn1_w_gate", "ffn1_w_up", "w_in", "ffn2_w_gate", "ffn2_w_up")

    def update(name, w, g, m, v):
        if name in transposed:
            w, m, v = tr(w), tr(m), tr(v)
        shape = w.shape
        two_d = lambda a: a.reshape(-1, shape[-1])
        res = adamw(two_d(w), two_d(g), two_d(m), two_d(v), name=f"adamw_{name}")
        res = [a.reshape(shape) for a in [g] + list(res)]
        return [tr(a) for a in res] if name in transposed else res

    names = ["w_ada", "b_ada", "g_norm1", "ffn1_w_gate", "ffn1_w_up", "ffn1_w_down", "g_norm2", "w_in", "g_sgu_ln",
             "b_sgu_ln", "w_spatial", "b_spatial", "g_q", "g_k", "attn_sinks", "w_branch_a", "w_branch_b", "w_out",
             "g_norm3", "ffn2_w_gate", "ffn2_w_up", "ffn2_w_down"]
    weights = dict(zip(names, [w_ada, b_ada, g_norm1, ffn1_w_gate, ffn1_w_up, ffn1_w_down, g_norm2, w_in, g_sgu_ln,
                               b_sgu_ln, w_spatial, b_spatial, g_q, g_k, attn_sinks, w_branch_a, w_branch_b, w_out,
                               g_norm3, ffn2_w_gate, ffn2_w_up, ffn2_w_down]))
    m_in = dict(zip(names, [m_w_ada, m_b_ada, m_g_norm1, m_ffn1_w_gate, m_ffn1_w_up, m_ffn1_w_down, m_g_norm2, m_w_in,
                            m_g_sgu_ln, m_b_sgu_ln, m_w_spatial, m_b_spatial, m_g_q, m_g_k, m_attn_sinks, m_w_branch_a,
                            m_w_branch_b, m_w_out, m_g_norm3, m_ffn2_w_gate, m_ffn2_w_up, m_ffn2_w_down]))
    v_in = dict(zip(names, [v_w_ada, v_b_ada, v_g_norm1, v_ffn1_w_gate, v_ffn1_w_up, v_ffn1_w_down, v_g_norm2, v_w_in,
                            v_g_sgu_ln, v_b_sgu_ln, v_w_spatial, v_b_spatial, v_g_q, v_g_k, v_attn_sinks, v_w_branch_a,
                            v_w_branch_b, v_w_out, v_g_norm3, v_ffn2_w_gate, v_ffn2_w_up, v_ffn2_w_down]))
    grads = {
        "w_ada": dw_ada[None], "b_ada": db_ada, "g_norm1": g_gn1, "g_norm2": g_gn2, "g_norm3": g_gn3,
        "g_sgu_ln": g_gln, "b_sgu_ln": g_bln, "w_spatial": g_ws, "b_spatial": g_bs, "g_q": g_gq, "g_k": g_gk,
        "attn_sinks": g_sk,
        "ffn1_w_gate": r_wg1[None], "ffn1_w_up": r_wu1[None], "ffn1_w_down": r_wd1[None], "w_in": r_win[None],
        "w_branch_a": r_wa[None], "w_branch_b": r_wb[None], "w_out": r_wo[None],
        "ffn2_w_gate": r_wg3[None], "ffn2_w_up": r_wu3[None], "ffn2_w_down": r_wd3[None],
    }

    small_names = ["b_ada", "g_norm1", "g_norm2", "g_norm3", "g_sgu_ln", "b_sgu_ln", "w_spatial", "b_spatial", "g_q",
                   "g_k", "attn_sinks"]

    def pack(d):
        flat = jnp.concatenate([d[nm].reshape(-1) for nm in small_names])
        return jnp.pad(flat, (0, (-flat.size) % (8 * 128))).reshape(-1, 128)

    sd, sm, sv = adamw(pack(weights), pack(grads), pack(m_in), pack(v_in), name="adamw_small")
    delta, new_m, new_v = {}, {}, {}
    off = 0
    for nm in small_names:
        size, shape = weights[nm].size, weights[nm].shape
        delta[nm] = sd.reshape(-1)[off:off + size].reshape(shape)
        new_m[nm] = sm.reshape(-1)[off:off + size].reshape(shape)
        new_v[nm] = sv.reshape(-1)[off:off + size].reshape(shape)
        off += size
    for nm in names:
        if nm not in small_names:
            grads[nm], delta[nm], new_m[nm], new_v[nm] = update(nm, weights[nm], grads[nm], m_in[nm], v_in[nm])
        else:
            grads[nm] = grads[nm].reshape(weights[nm].shape)

    return (loss, dx.reshape(B, S, D), *[grads[nm] for nm in names], *[delta[nm] for nm in names],
            *[new_m[nm] for nm in names], *[new_v[nm] for nm in names])
```

```python
import functools
import math
import operator

import jax
import jax.numpy as jnp
from jax import lax
from jax.experimental import pallas as pl
from jax.experimental.pallas import tpu as pltpu

F32 = jnp.float32
BF16 = jnp.bfloat16
EPS = 1e-6
NEG = -1e30
N_DEV = 8
N_QUAD = 4
D_A = 512
N_GROUPS = 4
CHUNK = 128
HEAD_DIM = 64
N_KV = 2
Q_PER_KV = 4
N_Q = N_KV * Q_PER_KV
D_B = N_Q * HEAD_DIM
QKV_W = D_B + 2 * N_KV * HEAD_DIM
K_OFF = D_B
V_OFF = D_B + N_KV * HEAD_DIM
ATT_SCALE = HEAD_DIM ** -0.5
GELU_K = math.sqrt(2.0 / math.pi)
GELU_C = 0.044715
ADAM_LR, ADAM_B1, ADAM_B2, ADAM_EPS, ADAM_WD, ADAM_STEP = 0.001, 0.9, 0.999, 1e-08, 0.01, 10
VMEM_LIMIT_BYTES = 56 * 1024 * 1024
MESH = pl.DeviceIdType.MESH
NT = (((1,), (1,)), ((), ()))
TN = (((0,), (0,)), ((), ()))
ANY = pl.BlockSpec(memory_space=pl.ANY)


def _dot(a, b):
    return jnp.dot(a, b, preferred_element_type=F32)


def _dg(a, b, dims):
    return lax.dot_general(a, b, dims, preferred_element_type=F32)


def _gelu_parts(x):
    t = jnp.tanh(GELU_K * (x + GELU_C * x * x * x))
    return 0.5 * x * (1.0 + t), t


def _dgelu(x, t):
    return 0.5 * (1.0 + t) + 0.5 * x * (1.0 - t * t) * (GELU_K * (1.0 + 3.0 * GELU_C * x * x))


def _row_block(rows, target):
    b = min(rows, target)
    while rows % b or b % 8:
        b -= 1
    return b


def _resident(a):
    return pl.BlockSpec(a.shape, lambda *_: (0,) * a.ndim, pipeline_mode=pl.Buffered(1))


class Comm:
    def __init__(self, *, srcs=(), bufs=(), land_shapes=(), n_sems, start, finish):
        self.srcs, self.bufs, self.land_shapes = list(srcs), list(bufs), list(land_shapes)
        self.n_sems, self.start, self.finish = n_sems, start, finish
        self.bufs_out, self.lands = None, None


def _call(body, *, name, grid, in_specs, out_specs, out_shape, args, scratch_shapes=(), comms=(), prefetch=()):
    prefetch = list(prefetch)
    in_specs, out_specs, out_shape = list(in_specs), list(out_specs), list(out_shape)
    args, scratch = list(args), list(scratch_shapes)
    n_in, n_out, n_scr = len(args), len(out_shape), len(scratch)
    aliases, layout = {}, []
    for cm in comms:
        i0, o0, s0 = len(args), len(out_shape), len(scratch)
        args += cm.srcs + cm.bufs
        in_specs += [ANY] * (len(cm.srcs) + len(cm.bufs))
        out_shape += [jax.ShapeDtypeStruct(b.shape, b.dtype) for b in cm.bufs] + cm.land_shapes
        out_specs += [ANY] * (len(cm.bufs) + len(cm.land_shapes))
        for j in range(len(cm.bufs)):
            aliases[len(prefetch) + i0 + len(cm.srcs) + j] = o0 + j
        scratch += [pltpu.SemaphoreType.DMA((cm.n_sems,)), pltpu.SemaphoreType.DMA((cm.n_sems,))]
        layout.append((i0, o0, s0))
    n_in_all, n_out_all = len(args), len(out_shape)

    def wrapped(*refs):
        scalars, refs = refs[:len(prefetch)], refs[len(prefetch):]
        ins, outs, scr = refs[:n_in_all], refs[n_in_all:n_in_all + n_out_all], refs[n_in_all + n_out_all:]
        parts = []
        for cm, (i0, o0, s0) in zip(comms, layout):
            nb = len(cm.bufs)
            parts.append((ins[i0:i0 + len(cm.srcs)], outs[o0:o0 + nb], outs[o0 + nb:o0 + nb + len(cm.land_shapes)],
                          scr[s0], scr[s0 + 1]))
        if comms and grid:
            ids = [pl.program_id(d) for d in range(len(grid))]
            first = functools.reduce(operator.and_, [i == 0 for i in ids])
            last = functools.reduce(operator.and_, [i == g - 1 for i, g in zip(ids, grid)])

            @pl.when(first)
            def _():
                for cm, p in zip(comms, parts):
                    cm.start(*p)
        elif comms:
            for cm, p in zip(comms, parts):
                cm.start(*p)
        if body is not None:
            body(*scalars, *ins[:n_in], *outs[:n_out], *scr[:n_scr])
        if comms and grid:
            @pl.when(last)
            def _():
                for cm, p in zip(comms, parts):
                    cm.finish(*p)
        elif comms:
            for cm, p in zip(comms, parts):
                cm.finish(*p)

    if prefetch:
        kwargs = dict(grid_spec=pltpu.PrefetchScalarGridSpec(
            num_scalar_prefetch=len(prefetch), grid=grid, in_specs=in_specs, out_specs=out_specs, scratch_shapes=scratch))
    else:
        kwargs = dict(in_specs=in_specs, out_specs=out_specs, scratch_shapes=scratch, **(dict(grid=grid) if grid else {}))
    sem = ("arbitrary",) * len(grid)
    res = pl.pallas_call(
        wrapped, name=name, out_shape=out_shape, input_output_aliases=aliases,
        compiler_params=pltpu.CompilerParams(dimension_semantics=sem, vmem_limit_bytes=VMEM_LIMIT_BYTES), **kwargs,
    )(*prefetch, *args)
    for cm, (i0, o0, s0) in zip(comms, layout):
        nb = len(cm.bufs)
        cm.bufs_out = list(res[o0:o0 + nb])
        cm.lands = list(res[o0 + nb:o0 + nb + len(cm.land_shapes)])
    return list(res[:n_out])


def run_comms(comms, *, name):
    _call(None, name=name, grid=(), in_specs=[], out_specs=[], out_shape=[], args=[], comms=comms)


def norm_mod_fwd(h, g, sc, sh, *, seq, tm, name, comms=()):
    T, D = h.shape
    B, tps = T // seq, seq // tm

    def body(h_ref, g_ref, sc_ref, sh_ref, o_ref):
        x = h_ref[...]
        r = lax.rsqrt(jnp.mean(x * x, axis=-1, keepdims=True) + EPS)
        y = x * r * g_ref[...]
        o_ref[...] = (y * (1.0 + sc_ref[0]) + sh_ref[0]).astype(o_ref.dtype)

    per_seq = pl.BlockSpec((1, 1, D), lambda i: (i // tps, 0, 0))
    return _call(
        body, name=name, grid=(T // tm,),
        in_specs=[pl.BlockSpec((tm, D), lambda i: (i, 0)), pl.BlockSpec((1, D), lambda i: (0, 0)), per_seq, per_seq],
        out_specs=[pl.BlockSpec((tm, D), lambda i: (i, 0))], out_shape=[jax.ShapeDtypeStruct((T, D), BF16)],
        args=[h, g, sc.reshape(B, 1, D), sh.reshape(B, 1, D)], comms=comms)[0]


def ffn_up(xn, wg, wu, *, tm, name, comms=()):
    T, D = xn.shape
    nq, fs, _ = wg.shape

    def body(x_ref, wg_ref, wu_ref, s_ref, q_ref, a_ref):
        x = x_ref[...]
        g = _dg(x, wg_ref[0], NT)
        u = _dg(x, wu_ref[0], NT)
        sg = jax.nn.sigmoid(g)
        s = g * sg
        s_ref[0] = s.astype(BF16)
        q_ref[0] = (u * (sg + s * (1.0 - sg))).astype(BF16)
        a_ref[0] = (s * u).astype(BF16)

    w_spec = pl.BlockSpec((1, fs, D), lambda q, i: (q, 0, 0))
    o_spec = pl.BlockSpec((1, tm, fs), lambda q, i: (q, i, 0))
    o_shape = jax.ShapeDtypeStruct((nq, T, fs), BF16)
    return _call(
        body, name=name, grid=(nq, T // tm),
        in_specs=[pl.BlockSpec((tm, D), lambda q, i: (i, 0)), w_spec, w_spec],
        out_specs=[o_spec, o_spec, o_spec], out_shape=[o_shape, o_shape, o_shape], args=[xn, wg, wu], comms=comms)


def _norm_mod(y, g_ref, sc_ref, sh_ref):
    nx, _ = _rms_parts(y)
    return ((nx * g_ref[...]) * (1.0 + sc_ref[0]) + sh_ref[0]).astype(BF16)


def ffn_down(a, wd, hin, ga, *, target=None, norm=None, seq, tm, name, comms=()):
    nq, T, fs = a.shape
    D = wd.shape[-1]
    B, tps, nt = T // seq, seq // tm, T // tm

    def body(a_ref, wd_ref, h_ref, ga_ref, *rest):
        rest = list(rest)
        t_ref = rest.pop(0) if target is not None else None
        norm_refs = [rest.pop(0) for _ in range(3)] if norm is not None else None
        o_ref, f_ref = rest[0], rest[1]
        f = _dot(a_ref[0], wd_ref[0])
        for q in range(1, nq):
            f = f + _dot(a_ref[q], wd_ref[q])
        f_ref[...] = f.astype(BF16)
        y = h_ref[...] + (0.5 * ga_ref[0]) * f
        if target is not None:
            e = y - t_ref[...]
            o_ref[...] = e * (1.0 / D)
            rest[2][...] = jnp.full(rest[2].shape, 0.5 * jnp.sum(e * e) * (1.0 / D), F32)
        else:
            o_ref[...] = y
        if norm is not None:
            rest[2][...] = _norm_mod(y, *norm_refs)

    tile = pl.BlockSpec((tm, D), lambda i: (i, 0))
    per_seq = pl.BlockSpec((1, 1, D), lambda i: (i // tps, 0, 0))
    in_specs = [pl.BlockSpec((nq, tm, fs), lambda i: (0, i, 0)), _resident(wd), tile, per_seq]
    out_specs = [tile, tile]
    out_shape = [jax.ShapeDtypeStruct((T, D), F32), jax.ShapeDtypeStruct((T, D), BF16)]
    args = [a, wd, hin, ga.reshape(B, 1, D)]
    if target is not None:
        in_specs.append(tile)
        args.append(target)
        out_specs.append(pl.BlockSpec((1, 8, 128), lambda i: (i, 0, 0)))
        out_shape.append(jax.ShapeDtypeStruct((nt, 8, 128), F32))
    if norm is not None:
        in_specs += [pl.BlockSpec((1, D), lambda i: (0, 0)), per_seq, per_seq]
        args += [norm[0], norm[1].reshape(B, 1, D), norm[2].reshape(B, 1, D)]
        out_specs.append(tile)
        out_shape.append(jax.ShapeDtypeStruct((T, D), BF16))
    return _call(body, name=name, grid=(nt,), in_specs=in_specs, out_specs=out_specs, out_shape=out_shape, args=args,
                 comms=comms)


def proj_fwd(xn, ws, *, tm, name, comms=()):
    T, D = xn.shape
    n = len(ws)

    def body(*refs):
        x = refs[0][...]
        for j in range(n):
            refs[1 + n + j][...] = _dg(x, refs[1 + j][...], NT).astype(BF16)

    return _call(
        body, name=name, grid=(T // tm,),
        in_specs=[pl.BlockSpec((tm, D), lambda i: (i, 0))] + [pl.BlockSpec(w.shape, lambda i: (0, 0)) for w in ws],
        out_specs=[pl.BlockSpec((tm, w.shape[0]), lambda i: (i, 0)) for w in ws],
        out_shape=[jax.ShapeDtypeStruct((T, w.shape[0]), BF16) for w in ws], args=[xn, *ws], comms=comms)


def _layer_norm_parts(v):
    mu = jnp.mean(v, axis=-1, keepdims=True)
    d = v - mu
    rstd = lax.rsqrt(jnp.mean(d * d, axis=-1, keepdims=True) + EPS)
    return d * rstd, rstd


def _causal():
    row = lax.broadcasted_iota(jnp.int32, (CHUNK, CHUNK), 0)
    col = lax.broadcasted_iota(jnp.int32, (CHUNK, CHUNK), 1)
    return row >= col


def sgu_fwd(puv, gln, bln, ws, bs_t, *, cpb, name, comms=()):
    T = puv.shape[0]
    gd = D_A // N_GROUPS
    tm = cpb * CHUNK

    def body(p_ref, gln_ref, bln_ref, ws_ref, bs_ref, o_ref):
        causal = _causal()
        wm = [jnp.where(causal, ws_ref[g], 0.0).astype(BF16) for g in range(N_GROUPS)]
        for j in range(cpb):
            rows = slice(j * CHUNK, (j + 1) * CHUNK)
            u, _ = _gelu_parts(p_ref[rows, 0:D_A].astype(F32))
            vv, _ = _gelu_parts(p_ref[rows, D_A:2 * D_A].astype(F32))
            vhat, _ = _layer_norm_parts(vv)
            vn = (vhat * gln_ref[...] + bln_ref[...]).astype(BF16)
            for g in range(N_GROUPS):
                cols = slice(g * gd, (g + 1) * gd)
                z = _dot(wm[g], vn[:, cols]) + bs_ref[:, g:g + 1]
                o_ref[rows, cols] = (u[:, cols] * z).astype(BF16)

    full = lambda a: pl.BlockSpec(a.shape, lambda i: (0,) * a.ndim)
    return _call(
        body, name=name, grid=(T // tm,),
        in_specs=[pl.BlockSpec((tm, 2 * D_A), lambda i: (i, 0)), full(gln), full(bln), full(ws), full(bs_t)],
        out_specs=[pl.BlockSpec((tm, D_A), lambda i: (i, 0))], out_shape=[jax.ShapeDtypeStruct((T, D_A), BF16)],
        args=[puv, gln, bln, ws, bs_t], comms=comms)[0]


def _rms_parts(x):
    r = lax.rsqrt(jnp.mean(x * x, axis=-1, keepdims=True) + EPS)
    return x * r, r


KV_W = Q_PER_KV * HEAD_DIM
KV2 = N_KV * HEAD_DIM
STACK = Q_PER_KV * CHUNK


def _iota(shape, dim):
    return lax.broadcasted_iota(jnp.int32, shape, dim)


def _head_mean_matrix(n):
    return jnp.where((_iota((n, n), 0) >> 6) == (_iota((n, n), 1) >> 6), 1.0 / HEAD_DIM, 0.0).astype(BF16)


def _repeat_matrix(h):
    return jnp.where(_iota((KV2, KV_W), 0) == h * HEAD_DIM + (_iota((KV2, KV_W), 1) & (HEAD_DIM - 1)), 1.0, 0.0).astype(BF16)


def _fold_matrix(h):
    j, l = _iota((KV_W, KV2), 0), _iota((KV_W, KV2), 1)
    return jnp.where(((j & (HEAD_DIM - 1)) == (l & (HEAD_DIM - 1))) & ((l >> 6) == h), 1.0, 0.0).astype(BF16)


def _dot_split(x, m):
    hi = x.astype(BF16)
    lo = (x - hi.astype(F32)).astype(BF16)
    return _dot(hi, m) + _dot(lo, m)


def _head_rms(x, mean_matrix):
    r = lax.rsqrt(_dot_split(x * x, mean_matrix) + EPS)
    return x * r, r


def _stack_heads(x):
    head = _iota(x.shape, 1) >> 6
    return jnp.concatenate([jnp.where(head == g, x, 0.0).astype(BF16) for g in range(Q_PER_KV)], axis=0)


def _unstack_heads(y):
    head = _iota((CHUNK, KV_W), 1) >> 6
    out = jnp.where(head == 0, y[0:CHUNK], 0.0)
    for g in range(1, Q_PER_KV):
        out = out + jnp.where(head == g, y[g * CHUNK:(g + 1) * CHUNK], 0.0)
    return out


SOFTMAX_ROWS = 32


def _fill_mask_bias(bias_ref):
    qi = _iota((CHUNK, 2 * CHUNK), 0)
    kj = _iota((CHUNK, 2 * CHUNK), 1)
    diff = qi + CHUNK - kj
    window = (diff >= 0) & (diff < CHUNK)
    bias_ref[0] = jnp.where(window & (kj >= CHUNK), 0.0, NEG)
    bias_ref[1] = jnp.where(window, 0.0, NEG)


def _softmax_rows(s_ref, bias_ref, first, sink_ref, h, c):
    rows = pl.ds(pl.multiple_of(c * SOFTMAX_ROWS, SOFTMAX_ROWS), SOFTMAX_ROWS)
    in_block = pl.ds(pl.multiple_of((c % (CHUNK // SOFTMAX_ROWS)) * SOFTMAX_ROWS, SOFTMAX_ROWS), SOFTMAX_ROWS)
    sink = sink_ref[0, h * Q_PER_KV + c // (CHUNK // SOFTMAX_ROWS)]
    s = s_ref[rows, :] + bias_ref[first, in_block, :]
    m = jnp.maximum(jnp.max(s, axis=-1, keepdims=True), sink)
    p = jnp.exp(s - m)
    es = jnp.exp(sink - m)
    inv = 1.0 / (jnp.sum(p, axis=-1, keepdims=True) + es)
    return rows, p * inv, es * inv


def _fill_keys(p_ref, gk_ref, krep, vrep, seq):
    mean_k = _head_mean_matrix(KV2)
    reps = [_repeat_matrix(h) for h in range(N_KV)]
    for h in range(N_KV):
        krep[h][0:CHUNK, :] = jnp.zeros((CHUNK, KV_W), BF16)
        vrep[h][0:CHUNK, :] = jnp.zeros((CHUNK, KV_W), BF16)
    rows = min(seq, 4 * CHUNK)

    def piece(j, carry):
        r0 = pl.multiple_of(j * rows, CHUNK)
        nk, _ = _head_rms(p_ref[pl.ds(r0, rows), K_OFF:K_OFF + KV2].astype(F32), mean_k)
        kn = (nk * gk_ref[...]).astype(BF16)
        v = p_ref[pl.ds(r0, rows), V_OFF:V_OFF + KV2].astype(BF16)
        r1 = pl.multiple_of(r0 + CHUNK, CHUNK)
        for h in range(N_KV):
            krep[h][pl.ds(r1, rows), :] = _dot(kn, reps[h]).astype(BF16)
            vrep[h][pl.ds(r1, rows), :] = _dot(v, reps[h]).astype(BF16)
        return carry

    lax.fori_loop(0, seq // rows, piece, 0)


def attn_fwd(pqkv, gq_t, gk_t, sinks, *, seq, name, comms=()):
    T = pqkv.shape[0]
    nb = seq // CHUNK

    def body(p_ref, gq_ref, gk_ref, sink_ref, o_ref, k0, k1, v0, v1, bias_ref, s_ref, pr_ref):
        krep, vrep = [k0, k1], [v0, v1]
        _fill_keys(p_ref, gk_ref, krep, vrep, seq)
        _fill_mask_bias(bias_ref)
        mean_q = _head_mean_matrix(D_B)

        def block(i, carry):
            r0 = pl.multiple_of(i * CHUNK, CHUNK)
            first = jnp.minimum(i, 1)
            nq, _ = _head_rms(p_ref[pl.ds(r0, CHUNK), 0:D_B].astype(F32), mean_q)
            qn = nq * (gq_ref[...] * ATT_SCALE)
            for h in range(N_KV):
                qs = _stack_heads(qn[:, h * KV_W:(h + 1) * KV_W])
                s_ref[...] = _dg(qs, krep[h][pl.ds(r0, 2 * CHUNK), :], NT)

                def rows_of(c, carry2):
                    rows, probs, _ = _softmax_rows(s_ref, bias_ref, first, sink_ref, h, c)
                    pr_ref[rows, :] = probs.astype(BF16)
                    return carry2

                lax.fori_loop(0, STACK // SOFTMAX_ROWS, rows_of, 0, unroll=True)
                out = _unstack_heads(_dot(pr_ref[...], vrep[h][pl.ds(r0, 2 * CHUNK), :]))
                o_ref[pl.ds(r0, CHUNK), h * KV_W:(h + 1) * KV_W] = out.astype(BF16)
            return carry

        lax.fori_loop(0, nb, block, 0)

    return _call(
        body, name=name, grid=(T // seq,),
        in_specs=[pl.BlockSpec((seq, QKV_W), lambda b: (b, 0)), pl.BlockSpec(gq_t.shape, lambda b: (0, 0)),
                  pl.BlockSpec(gk_t.shape, lambda b: (0, 0)), pl.BlockSpec(memory_space=pltpu.SMEM)],
        out_specs=[pl.BlockSpec((seq, D_B), lambda b: (b, 0))], out_shape=[jax.ShapeDtypeStruct((T, D_B), BF16)],
        scratch_shapes=[pltpu.VMEM((seq + CHUNK, KV_W), BF16)] * 4
        + [pltpu.VMEM((2, CHUNK, 2 * CHUNK), F32), pltpu.VMEM((STACK, 2 * CHUNK), F32), pltpu.VMEM((STACK, 2 * CHUNK), BF16)],
        args=[pqkv, gq_t, gk_t, sinks], comms=comms)[0]


def mixer_out_fwd(sgu, att, pg, hin, ga, wa, wb, wo, norm, *, seq, tm, name, comms=()):
    T, D = hin.shape
    B, tps = T // seq, seq // tm

    def body(s_ref, t_ref, pg_ref, h_ref, ga_ref, wa_ref, wb_ref, wo_ref, g_ref, sc_ref, sh_ref,
             ya_ref, yb_ref, mg_ref, m_ref, ho_ref, xn_ref):
        ya = _dot(s_ref[...], wa_ref[...])
        yb = _dot(t_ref[...], wb_ref[...])
        merged = (jax.nn.sigmoid(pg_ref[:, 0:D].astype(F32)) * ya
                  + jax.nn.sigmoid(pg_ref[:, D:2 * D].astype(F32)) * yb)
        mg = merged.astype(BF16)
        m = _dot(mg, wo_ref[...])
        ya_ref[...] = ya.astype(BF16)
        yb_ref[...] = yb.astype(BF16)
        mg_ref[...] = mg
        m_ref[...] = m.astype(BF16)
        y = h_ref[...] + ga_ref[0] * m
        ho_ref[...] = y
        xn_ref[...] = _norm_mod(y, g_ref, sc_ref, sh_ref)

    tile = lambda w: pl.BlockSpec((tm, w), lambda i: (i, 0))
    per_seq = pl.BlockSpec((1, 1, D), lambda i: (i // tps, 0, 0))
    b16 = jax.ShapeDtypeStruct((T, D), BF16)
    return _call(
        body, name=name, grid=(T // tm,),
        in_specs=[tile(D_A), tile(D_B), tile(2 * D), tile(D), per_seq, _resident(wa), _resident(wb), _resident(wo),
                  pl.BlockSpec((1, D), lambda i: (0, 0)), per_seq, per_seq],
        out_specs=[tile(D)] * 6, out_shape=[b16, b16, b16, b16, jax.ShapeDtypeStruct((T, D), F32), b16],
        args=[sgu, att, pg, hin, ga.reshape(B, 1, D), wa, wb, wo, norm[0], norm[1].reshape(B, 1, D),
              norm[2].reshape(B, 1, D)], comms=comms)


def ffn_bwd_act(dh, f, ga, wd, s, q, *, seq, tm, name, comms=()):
    T, D = dh.shape
    nq, fs, _ = wd.shape
    B, tps = T // seq, seq // tm

    def body(dh_ref, f_ref, ga_ref, wd_ref, s_ref, q_ref, dg_ref, du_ref, df_ref, dga_ref):
        i = pl.program_id(0)
        d = dh_ref[...]
        df = ((0.5 * ga_ref[0]) * d).astype(BF16)
        df_ref[...] = df
        part = 0.5 * jnp.sum(d * f_ref[...].astype(F32), axis=0, keepdims=True)
        for j in range(nq):
            da = _dg(df, wd_ref[j], NT)
            du_ref[j] = (da * s_ref[j].astype(F32)).astype(BF16)
            dg_ref[j] = (da * q_ref[j].astype(F32)).astype(BF16)

        @pl.when(i % tps == 0)
        def _():
            dga_ref[0] = part

        @pl.when(i % tps != 0)
        def _():
            dga_ref[0] += part

    tile = pl.BlockSpec((tm, D), lambda i: (i, 0))
    per_seq = pl.BlockSpec((1, 1, D), lambda i: (i // tps, 0, 0))
    act = pl.BlockSpec((nq, tm, fs), lambda i: (0, i, 0))
    o_shape = jax.ShapeDtypeStruct((nq, T, fs), BF16)
    dg, du, df, dga = _call(
        body, name=name, grid=(T // tm,), in_specs=[tile, tile, per_seq, _resident(wd), act, act],
        out_specs=[act, act, tile, per_seq],
        out_shape=[o_shape, o_shape, jax.ShapeDtypeStruct((T, D), BF16), jax.ShapeDtypeStruct((B, 1, D), F32)],
        args=[dh, f, ga.reshape(B, 1, D), wd, s, q], comms=comms)
    return dg, du, df, dga.reshape(B, D)


def mm_tn(pairs, *, tt, name, comms=()):
    n = len(pairs)
    flat = []
    for pair in pairs:
        for a in pair:
            if not any(a is b for b in flat):
                flat.append(a)
    where = lambda a: [k for k, b in enumerate(flat) if a is b][0]
    nq = max([a.shape[0] for a in flat if a.ndim == 3] + [1])
    T = flat[0].shape[-2]
    tt = min(tt, T)
    nt = T // tt
    stacked = [x.ndim == 3 or y.ndim == 3 for x, y in pairs]

    def body(*refs):
        in_refs, o_refs, accs = refs[:len(flat)], refs[len(flat):len(flat) + n], refs[len(flat) + n:]
        t = pl.program_id(1)
        value = lambda a: in_refs[where(a)][0] if a.ndim == 3 else in_refs[where(a)][...]

        def put(j, v):
            if stacked[j]:
                o_refs[j][0] = v.astype(BF16)
            else:
                o_refs[j][...] = v.astype(BF16)

        for j, (x, y) in enumerate(pairs):
            part = _dg(value(x), value(y), TN)
            if nt == 1:
                put(j, part)
                continue

            @pl.when(t == 0)
            def _():
                accs[j][...] = part

            @pl.when(t != 0)
            def _():
                accs[j][...] += part

        if nt > 1:
            @pl.when(t == nt - 1)
            def _():
                for j in range(n):
                    put(j, accs[j][...])

    def spec(a):
        if a.ndim == 3:
            return pl.BlockSpec((1, tt, a.shape[2]), lambda q, t: (q, t, 0))
        return pl.BlockSpec((tt, a.shape[1]), lambda q, t: (t, 0))

    out_specs, out_shape = [], []
    for j, (x, y) in enumerate(pairs):
        K, N = x.shape[-1], y.shape[-1]
        if stacked[j]:
            out_specs.append(pl.BlockSpec((1, K, N), lambda q, t: (q, 0, 0)))
            out_shape.append(jax.ShapeDtypeStruct((nq, K, N), BF16))
        else:
            out_specs.append(pl.BlockSpec((K, N), lambda q, t: (0, 0)))
            out_shape.append(jax.ShapeDtypeStruct((K, N), BF16))
    return _call(
        body, name=name, grid=(nq, nt), in_specs=[spec(a) for a in flat],
        out_specs=out_specs, out_shape=out_shape,
        scratch_shapes=[pltpu.VMEM((x.shape[-1], y.shape[-1]), F32) for x, y in pairs] if nt > 1 else [],
        args=flat, comms=comms)


def dx_norm_bwd(dys, ws, hin, dh_out, g, sc, *, seq, tm, name, comms=()):
    T, D = hin.shape
    B, tps = T // seq, seq // tm
    n = len(dys)

    def body(*refs):
        dy_refs, w_refs = refs[:n], refs[n:2 * n]
        h_ref, dho_ref, g_ref, sc_ref, dhi_ref, dsh_ref, dsc_ref, dgn_ref = refs[2 * n:]
        i = pl.program_id(0)
        dxn = None
        for j in range(n):
            if dys[j].ndim == 3:
                for q in range(dys[j].shape[0]):
                    part = _dot(dy_refs[j][q], w_refs[j][q])
                    dxn = part if dxn is None else dxn + part
            else:
                part = _dot(dy_refs[j][...], w_refs[j][...])
                dxn = part if dxn is None else dxn + part
        x = h_ref[...]
        nx, r = _rms_parts(x)
        gain = g_ref[...]
        one_sc = 1.0 + sc_ref[0]
        dsh = jnp.sum(dxn, axis=0, keepdims=True)
        dsc = jnp.sum(dxn * (nx * gain), axis=0, keepdims=True)
        dgn = jnp.sum(dxn * one_sc * nx, axis=0, keepdims=True)
        dn = dxn * one_sc * gain
        dhi_ref[...] = dho_ref[...] + r * (dn - nx * jnp.mean(dn * nx, axis=-1, keepdims=True))

        @pl.when(i % tps == 0)
        def _():
            dsh_ref[0] = dsh
            dsc_ref[0] = dsc

        @pl.when(i % tps != 0)
        def _():
            dsh_ref[0] += dsh
            dsc_ref[0] += dsc

        @pl.when(i == 0)
        def _():
            dgn_ref[...] = dgn

        @pl.when(i != 0)
        def _():
            dgn_ref[...] += dgn

    def dy_spec(a):
        if a.ndim == 3:
            return pl.BlockSpec((a.shape[0], tm, a.shape[2]), lambda i: (0, i, 0))
        return pl.BlockSpec((tm, a.shape[1]), lambda i: (i, 0))

    tile = pl.BlockSpec((tm, D), lambda i: (i, 0))
    per_seq = pl.BlockSpec((1, 1, D), lambda i: (i // tps, 0, 0))
    row = pl.BlockSpec((1, D), lambda i: (0, 0))
    dhi, dsh, dsc, dgn = _call(
        body, name=name, grid=(T // tm,),
        in_specs=[dy_spec(a) for a in dys] + [_resident(w) for w in ws] + [tile, tile, row, per_seq],
        out_specs=[tile, per_seq, per_seq, row],
        out_shape=[jax.ShapeDtypeStruct((T, D), F32), jax.ShapeDtypeStruct((B, 1, D), F32),
                   jax.ShapeDtypeStruct((B, 1, D), F32), jax.ShapeDtypeStruct((1, D), F32)],
        args=[*dys, *ws, hin, dh_out, g, sc.reshape(B, 1, D)], comms=comms)
    return dhi, dsh.reshape(B, D), dsc.reshape(B, D), dgn


def mixer_out_bwd(dh, m, ga, ya, yb, pg, wa, wb, wo, *, seq, tm, name, comms=()):
    T, D = dh.shape
    B, tps = T // seq, seq // tm

    def body(dh_ref, m_ref, ga_ref, ya_ref, yb_ref, pg_ref, wa_ref, wb_ref, wo_ref,
             dg_ref, dya_ref, dyb_ref, ds_ref, dt_ref, dm_ref, dga_ref):
        i = pl.program_id(0)
        d = dh_ref[...]
        dm = (ga_ref[0] * d).astype(BF16)
        dm_ref[...] = dm
        part = jnp.sum(d * m_ref[...].astype(F32), axis=0, keepdims=True)

        @pl.when(i % tps == 0)
        def _():
            dga_ref[0] = part

        @pl.when(i % tps != 0)
        def _():
            dga_ref[0] += part

        dmg = _dg(dm, wo_ref[...], NT)
        sa = jax.nn.sigmoid(pg_ref[:, 0:D].astype(F32))
        sb = jax.nn.sigmoid(pg_ref[:, D:2 * D].astype(F32))
        dg_ref[:, 0:D] = (dmg * ya_ref[...].astype(F32) * (sa * (1.0 - sa))).astype(BF16)
        dg_ref[:, D:2 * D] = (dmg * yb_ref[...].astype(F32) * (sb * (1.0 - sb))).astype(BF16)
        dya = (dmg * sa).astype(BF16)
        dyb = (dmg * sb).astype(BF16)
        dya_ref[...] = dya
        dyb_ref[...] = dyb
        ds_ref[...] = _dg(dya, wa_ref[...], NT).astype(BF16)
        dt_ref[...] = _dg(dyb, wb_ref[...], NT).astype(BF16)

    tile = lambda w: pl.BlockSpec((tm, w), lambda i: (i, 0))
    per_seq = pl.BlockSpec((1, 1, D), lambda i: (i // tps, 0, 0))
    dg, dya, dyb, ds, dt, dm, dga = _call(
        body, name=name, grid=(T // tm,),
        in_specs=[tile(D), tile(D), per_seq, tile(D), tile(D), tile(2 * D), _resident(wa), _resident(wb), _resident(wo)],
        out_specs=[tile(2 * D), tile(D), tile(D), tile(D_A), tile(D_B), tile(D), per_seq],
        out_shape=[jax.ShapeDtypeStruct((T, 2 * D), BF16), jax.ShapeDtypeStruct((T, D), BF16),
                   jax.ShapeDtypeStruct((T, D), BF16), jax.ShapeDtypeStruct((T, D_A), BF16),
                   jax.ShapeDtypeStruct((T, D_B), BF16), jax.ShapeDtypeStruct((T, D), BF16),
                   jax.ShapeDtypeStruct((B, 1, D), F32)],
        args=[dh, m, ga.reshape(B, 1, D), ya, yb, pg, wa, wb, wo], comms=comms)
    return dg, dya, dyb, ds, dt, dm, dga.reshape(B, D)


def sgu_bwd(puv, dsgu, gln, bln, ws, bs_t, *, cpb, name, comms=()):
    T = puv.shape[0]
    gd = D_A // N_GROUPS
    tm = cpb * CHUNK

    def body(p_ref, ds_ref, gln_ref, bln_ref, ws_ref, bs_ref, d_ref, dws_ref, dz_ref, dgl_ref, dbl_ref):
        i = pl.program_id(0)
        causal = _causal()
        wf = [jnp.where(causal, ws_ref[g], 0.0) for g in range(N_GROUPS)]
        wm = [w.astype(BF16) for w in wf]
        wt = [w.T.astype(BF16) for w in wf]
        dws = [jnp.zeros((CHUNK, CHUNK), F32) for _ in range(N_GROUPS)]
        dzs = jnp.zeros((CHUNK, D_A), F32)
        dgl = jnp.zeros((1, D_A), F32)
        dbl = jnp.zeros((1, D_A), F32)
        for j in range(cpb):
            rows = slice(j * CHUNK, (j + 1) * CHUNK)
            au = p_ref[rows, 0:D_A].astype(F32)
            av = p_ref[rows, D_A:2 * D_A].astype(F32)
            u, tu = _gelu_parts(au)
            vv, tv = _gelu_parts(av)
            vhat, rstd = _layer_norm_parts(vv)
            vn = (vhat * gln_ref[...] + bln_ref[...]).astype(BF16)
            dsg = ds_ref[rows, :].astype(F32)
            dz = dsg * u
            dzb = dz.astype(BF16)
            zs, dvns = [], []
            for g in range(N_GROUPS):
                cols = slice(g * gd, (g + 1) * gd)
                zs.append(_dot(wm[g], vn[:, cols]) + bs_ref[:, g:g + 1])
                dws[g] = dws[g] + _dg(dzb[:, cols], vn[:, cols], NT)
                dvns.append(_dot(wt[g], dzb[:, cols]))
            z = jnp.concatenate(zs, axis=1)
            dvn = jnp.concatenate(dvns, axis=1)
            d_ref[rows, 0:D_A] = (dsg * z * _dgelu(au, tu)).astype(BF16)
            dvh = dvn * gln_ref[...]
            dvv = rstd * (dvh - jnp.mean(dvh, axis=-1, keepdims=True)
                          - vhat * jnp.mean(dvh * vhat, axis=-1, keepdims=True))
            d_ref[rows, D_A:2 * D_A] = (dvv * _dgelu(av, tv)).astype(BF16)
            dzs = dzs + dz
            dgl = dgl + jnp.sum(dvn * vhat, axis=0, keepdims=True)
            dbl = dbl + jnp.sum(dvn, axis=0, keepdims=True)

        @pl.when(i == 0)
        def _():
            for g in range(N_GROUPS):
                dws_ref[g] = jnp.where(causal, dws[g], 0.0)
            dz_ref[...] = dzs
            dgl_ref[...] = dgl
            dbl_ref[...] = dbl

        @pl.when(i != 0)
        def _():
            for g in range(N_GROUPS):
                dws_ref[g] += jnp.where(causal, dws[g], 0.0)
            dz_ref[...] += dzs
            dgl_ref[...] += dgl
            dbl_ref[...] += dbl

    full = lambda a: pl.BlockSpec(a.shape, lambda i: (0,) * a.ndim)
    acc = lambda s: pl.BlockSpec(s, lambda i: (0,) * len(s))
    return _call(
        body, name=name, grid=(T // tm,),
        in_specs=[pl.BlockSpec((tm, 2 * D_A), lambda i: (i, 0)), pl.BlockSpec((tm, D_A), lambda i: (i, 0)),
                  full(gln), full(bln), full(ws), full(bs_t)],
        out_specs=[pl.BlockSpec((tm, 2 * D_A), lambda i: (i, 0)), acc((N_GROUPS, CHUNK, CHUNK)), acc((CHUNK, D_A)),
                   acc((1, D_A)), acc((1, D_A))],
        out_shape=[jax.ShapeDtypeStruct((T, 2 * D_A), BF16), jax.ShapeDtypeStruct((N_GROUPS, CHUNK, CHUNK), F32),
                   jax.ShapeDtypeStruct((CHUNK, D_A), F32), jax.ShapeDtypeStruct((1, D_A), F32),
                   jax.ShapeDtypeStruct((1, D_A), F32)],
        args=[puv, dsgu, gln, bln, ws, bs_t], comms=comms)


def attn_bwd(pqkv, datt, gq_t, gk_t, sinks, *, seq, name, comms=()):
    T = pqkv.shape[0]
    nb = seq // CHUNK

    def body(p_ref, do_ref, gq_ref, gk_ref, sink_ref, d_ref, dgq_ref, dgk_ref, dsk_ref,
             k0, k1, v0, v1, dk0, dk1, dv0, dv1, dgq_s, dsk_s, bias_ref, s_ref, dp_ref, pr_ref, ds_ref):
        b = pl.program_id(0)
        krep, vrep, dkacc, dvacc = [k0, k1], [v0, v1], [dk0, dk1], [dv0, dv1]
        _fill_keys(p_ref, gk_ref, krep, vrep, seq)
        _fill_mask_bias(bias_ref)
        for h in range(N_KV):
            dkacc[h][...] = jnp.zeros_like(dkacc[h])
            dvacc[h][...] = jnp.zeros_like(dvacc[h])
        dgq_s[...] = jnp.zeros_like(dgq_s)
        dsk_s[...] = jnp.zeros_like(dsk_s)
        mean_q = _head_mean_matrix(D_B)
        lane = _iota((1, 128), 1)

        def block(i, carry):
            r0 = pl.multiple_of(i * CHUNK, CHUNK)
            first = jnp.minimum(i, 1)
            nq, rq = _head_rms(p_ref[pl.ds(r0, CHUNK), 0:D_B].astype(F32), mean_q)
            qn = nq * (gq_ref[...] * ATT_SCALE)
            dqn_parts = []
            for h in range(N_KV):
                cols = slice(h * KV_W, (h + 1) * KV_W)
                qs = _stack_heads(qn[:, cols])
                kk = krep[h][pl.ds(r0, 2 * CHUNK), :]
                dos = _stack_heads(do_ref[pl.ds(r0, CHUNK), cols].astype(F32))
                s_ref[...] = _dg(qs, kk, NT)
                dp_ref[...] = _dg(dos, vrep[h][pl.ds(r0, 2 * CHUNK), :], NT)

                def rows_of(c, carry2):
                    rows, probs, psink = _softmax_rows(s_ref, bias_ref, first, sink_ref, h, c)
                    dp = dp_ref[rows, :]
                    dr = jnp.sum(probs * dp, axis=-1, keepdims=True)
                    pr_ref[rows, :] = probs.astype(BF16)
                    ds_ref[rows, :] = (probs * (dp - dr)).astype(BF16)
                    head = h * Q_PER_KV + c // (CHUNK // SOFTMAX_ROWS)
                    dsk_s[...] += jnp.where(lane == head, -jnp.sum(psink * dr), 0.0)
                    return carry2

                lax.fori_loop(0, STACK // SOFTMAX_ROWS, rows_of, 0, unroll=True)
                dqn_parts.append(_unstack_heads(_dot(ds_ref[...], kk)))
                dkacc[h][pl.ds(r0, 2 * CHUNK), :] += _dg(ds_ref[...], qs, TN)
                dvacc[h][pl.ds(r0, 2 * CHUNK), :] += _dg(pr_ref[...], dos, TN)
            dqn = jnp.concatenate(dqn_parts, axis=1) * ATT_SCALE
            dn = dqn * gq_ref[...]
            d_ref[pl.ds(r0, CHUNK), 0:D_B] = (rq * (dn - nq * _dot_split(dn * nq, mean_q))).astype(BF16)
            dgq_s[...] += jnp.sum(dqn * nq, axis=0, keepdims=True)
            return carry

        lax.fori_loop(0, nb, block, 0)

        mean_k = _head_mean_matrix(KV2)
        folds = [_fold_matrix(h) for h in range(N_KV)]
        rows = min(seq, 4 * CHUNK)

        def piece(j, dgk):
            r0 = pl.multiple_of(j * rows, CHUNK)
            r1 = pl.multiple_of(r0 + CHUNK, CHUNK)
            dkn = _dot_split(dkacc[0][pl.ds(r1, rows), :], folds[0]) + _dot_split(dkacc[1][pl.ds(r1, rows), :], folds[1])
            dv = _dot_split(dvacc[0][pl.ds(r1, rows), :], folds[0]) + _dot_split(dvacc[1][pl.ds(r1, rows), :], folds[1])
            nk, rk = _head_rms(p_ref[pl.ds(r0, rows), K_OFF:K_OFF + KV2].astype(F32), mean_k)
            dn = dkn * gk_ref[...]
            d_ref[pl.ds(r0, rows), K_OFF:K_OFF + KV2] = (rk * (dn - nk * _dot_split(dn * nk, mean_k))).astype(BF16)
            d_ref[pl.ds(r0, rows), V_OFF:V_OFF + KV2] = dv.astype(BF16)
            return dgk + jnp.sum(dkn * nk, axis=0, keepdims=True)

        dgk = lax.fori_loop(0, seq // rows, piece, jnp.zeros((1, KV2), F32))

        @pl.when(b == 0)
        def _():
            dgq_ref[...] = dgq_s[...]
            dgk_ref[...] = dgk
            dsk_ref[...] = jnp.broadcast_to(dsk_s[...], dsk_ref.shape)

        @pl.when(b != 0)
        def _():
            dgq_ref[...] += dgq_s[...]
            dgk_ref[...] += dgk
            dsk_ref[...] += jnp.broadcast_to(dsk_s[...], dsk_ref.shape)

    acc = lambda s: pl.BlockSpec(s, lambda b: (0, 0))
    return _call(
        body, name=name, grid=(T // seq,),
        in_specs=[pl.BlockSpec((seq, QKV_W), lambda b: (b, 0)), pl.BlockSpec((seq, D_B), lambda b: (b, 0)),
                  pl.BlockSpec(gq_t.shape, lambda b: (0, 0)), pl.BlockSpec(gk_t.shape, lambda b: (0, 0)),
                  pl.BlockSpec(memory_space=pltpu.SMEM)],
        out_specs=[pl.BlockSpec((seq, QKV_W), lambda b: (b, 0)), acc((1, D_B)), acc((1, KV2)), acc((8, 128))],
        out_shape=[jax.ShapeDtypeStruct((T, QKV_W), BF16), jax.ShapeDtypeStruct((1, D_B), F32),
                   jax.ShapeDtypeStruct((1, KV2), F32), jax.ShapeDtypeStruct((8, 128), F32)],
        scratch_shapes=[pltpu.VMEM((seq + CHUNK, KV_W), BF16)] * 4 + [pltpu.VMEM((seq + CHUNK, KV_W), F32)] * 4
        + [pltpu.VMEM((1, D_B), F32), pltpu.VMEM((1, 128), F32), pltpu.VMEM((2, CHUNK, 2 * CHUNK), F32)]
        + [pltpu.VMEM((STACK, 2 * CHUNK), F32)] * 2 + [pltpu.VMEM((STACK, 2 * CHUNK), BF16)] * 2,
        args=[pqkv, datt, gq_t, gk_t, sinks], comms=comms)


def ada_fwd(c_all, w, b, *, name):
    nb, D = c_all.shape
    N = w.shape[1]
    tn = N // 3

    def body(c_ref, w_ref, b_ref, o_ref):
        c = c_ref[...]
        cond = (c * jax.nn.sigmoid(c)).astype(BF16)
        o_ref[...] = _dot(cond, w_ref[...].astype(BF16)) + b_ref[...]

    return _call(
        body, name=name, grid=(3,),
        in_specs=[pl.BlockSpec((nb, D), lambda j: (0, 0)), pl.BlockSpec((D, tn), lambda j: (0, j)),
                  pl.BlockSpec((1, tn), lambda j: (0, j))],
        out_specs=[pl.BlockSpec((nb, tn), lambda j: (0, j))], out_shape=[jax.ShapeDtypeStruct((nb, N), F32)],
        args=[c_all, w, b])[0]


def ada_bwd(c_all, dmods_all, dmods_mine, gain_rows, *, name, comms=()):
    nb, D = c_all.shape
    NA = dmods_all.shape[1]
    N = dmods_mine.shape[1]
    tn = N // 3

    def body(c_ref, da_ref, dm_ref, gr_ref, db_ref, dw_ref, dgn_ref):
        c = c_ref[...]
        cond = (c * jax.nn.sigmoid(c)).astype(BF16)
        dw_ref[...] = _dg(cond, dm_ref[...].astype(BF16), TN)
        db_ref[...] = jnp.sum(da_ref[...], axis=0, keepdims=True)
        total = gr_ref[0:1, :]
        for d in range(1, N_DEV):
            total = total + gr_ref[d:d + 1, :]
        dgn_ref[...] = total

    return _call(
        body, name=name, grid=(3,),
        in_specs=[pl.BlockSpec((nb, D), lambda j: (0, 0)), pl.BlockSpec((nb, NA), lambda j: (0, 0)),
                  pl.BlockSpec((nb, tn), lambda j: (0, j)), pl.BlockSpec((N_DEV, D), lambda j: (0, 0))],
        out_specs=[pl.BlockSpec((1, NA), lambda j: (0, 0)), pl.BlockSpec((D, tn), lambda j: (0, j)),
                   pl.BlockSpec((1, D), lambda j: (0, 0))],
        out_shape=[jax.ShapeDtypeStruct((1, NA), F32), jax.ShapeDtypeStruct((D, N), F32),
                   jax.ShapeDtypeStruct((1, D), F32)],
        args=[c_all, dmods_all, dmods_mine, gain_rows], comms=comms)


def adamw(w, g, m, v, *, name):
    R, C = w.shape
    rb = _row_block(R, max(8, (1 << 18) // C))
    c1 = 1.0 / (1.0 - ADAM_B1 ** ADAM_STEP)
    c2 = 1.0 / (1.0 - ADAM_B2 ** ADAM_STEP)

    def body(w_ref, g_ref, m_ref, v_ref, d_ref, mo_ref, vo_ref):
        gg = g_ref[...]
        mn = ADAM_B1 * m_ref[...] + (1.0 - ADAM_B1) * gg
        vn = ADAM_B2 * v_ref[...] + (1.0 - ADAM_B2) * (gg * gg)
        mo_ref[...] = mn
        vo_ref[...] = vn
        d_ref[...] = -ADAM_LR * ((mn * c1) / (jnp.sqrt(vn * c2) + ADAM_EPS) + ADAM_WD * w_ref[...])

    blk = pl.BlockSpec((rb, C), lambda i: (i, 0))
    shp = jax.ShapeDtypeStruct((R, C), F32)
    return _call(body, name=name, grid=(R // rb,), in_specs=[blk] * 4, out_specs=[blk] * 3, out_shape=[shp] * 3,
                 args=[w, g, m, v])


def _place():
    return lax.axis_index("x"), lax.axis_index("y"), lax.axis_index("c")


def _flip(v, bit):
    return 1 - v if bit else v


def _other_chips(mx, my):
    return [(_flip(mx, k & 2), _flip(my, k & 1)) for k in range(1, N_QUAD)]


def _remote(src, dst, send_sem, recv_sem, peer):
    return pltpu.make_async_remote_copy(src_ref=src, dst_ref=dst, send_sem=send_sem, recv_sem=recv_sem,
                                        device_id=peer, device_id_type=MESH)


def all_gather8(x, *, name, reduce=False, comms=()):
    r, n = x.shape

    def body(x_ref, o_ref, *rest):
        if reduce:
            buf, send_sems, recv_sems, lsem = rest
        else:
            buf = o_ref
            send_sems, recv_sems, lsem = rest
        mx, my, mc = _place()
        me = 4 * mx + 2 * my + mc
        mine = pltpu.make_async_copy(x_ref, buf.at[me], lsem)
        mine.start()
        sends, recvs = [], []
        for k in range(1, N_DEV):
            peer = (_flip(mx, k & 4), _flip(my, k & 2), _flip(mc, k & 1))
            pidx = 4 * peer[0] + 2 * peer[1] + peer[2]
            sends.append(_remote(x_ref, buf.at[me], send_sems.at[k - 1], recv_sems.at[k - 1], peer))
            recvs.append(_remote(x_ref, buf.at[pidx], send_sems.at[k - 1], recv_sems.at[k - 1], peer))
        for cp in sends:
            cp.start()
        for cp in recvs:
            cp.wait_recv()
        for cp in sends:
            cp.wait_send()
        mine.wait()
        if reduce:
            total = buf[0]
            for d in range(1, N_DEV):
                total = total + buf[d]
            o_ref[...] = total

    scratch = [pltpu.SemaphoreType.DMA((N_DEV - 1,)), pltpu.SemaphoreType.DMA((N_DEV - 1,)), pltpu.SemaphoreType.DMA]
    if reduce:
        scratch = [pltpu.VMEM((N_DEV, r, n), F32)] + scratch
        out_shape = jax.ShapeDtypeStruct((r, n), F32)
    else:
        out_shape = jax.ShapeDtypeStruct((N_DEV, r, n), F32)
    vmem = pl.BlockSpec(memory_space=pltpu.VMEM)
    return _call(body, name=name, grid=(), in_specs=[vmem], out_specs=[vmem], out_shape=[out_shape], args=[x],
                 scratch_shapes=scratch, comms=comms)[0]


def ag8_comm(x):
    def copies(srcs, lands, ss, rs, only_sends=False):
        mx, my, mc = _place()
        me = 4 * mx + 2 * my + mc
        local = pltpu.make_async_copy(srcs[0], lands[0].at[me], ss.at[N_DEV - 1])
        sends, recvs = [], []
        for k in range(1, N_DEV):
            peer = (_flip(mx, k & 4), _flip(my, k & 2), _flip(mc, k & 1))
            sends.append(_remote(srcs[0], lands[0].at[me], ss.at[k - 1], rs.at[k - 1], peer))
            if not only_sends:
                recvs.append(_remote(srcs[0], lands[0].at[4 * peer[0] + 2 * peer[1] + peer[2]], ss.at[k - 1], rs.at[k - 1], peer))
        return local, sends, recvs

    def start(srcs, bufs, lands, ss, rs):
        local, sends, _ = copies(srcs, lands, ss, rs, only_sends=True)
        local.start()
        for cp in sends:
            cp.start()

    def finish(srcs, bufs, lands, ss, rs):
        local, sends, recvs = copies(srcs, lands, ss, rs)
        for cp in recvs:
            cp.wait_recv()
        for cp in sends:
            cp.wait_send()
        local.wait()

    return Comm(srcs=[x], land_shapes=[jax.ShapeDtypeStruct((N_DEV,) + x.shape, x.dtype)], n_sems=N_DEV,
                start=start, finish=finish)


def sum8(stacked, *, name):
    _, r, n = stacked.shape

    def body(s_ref, o_ref):
        total = s_ref[0]
        for d in range(1, N_DEV):
            total = total + s_ref[d]
        o_ref[...] = total

    return _call(body, name=name, grid=(), in_specs=[pl.BlockSpec(memory_space=pltpu.VMEM)],
                 out_specs=[pl.BlockSpec(memory_space=pltpu.VMEM)], out_shape=[jax.ShapeDtypeStruct((r, n), F32)],
                 args=[stacked])[0]


def cast_into_stacks(ws, quad, *, name, comms=()):
    steps = 4

    def body(q_ref, *refs):
        for w_ref, o_ref in zip(refs[:len(ws)], refs[len(ws):]):
            o_ref[0] = w_ref[...].astype(BF16)

    return _call(
        body, name=name, grid=(steps,), prefetch=[quad],
        in_specs=[pl.BlockSpec((w.shape[0] // steps, w.shape[1]), lambda i, q: (i, 0)) for w in ws],
        out_specs=[pl.BlockSpec((1, w.shape[0] // steps, w.shape[1]), lambda i, q: (q[0], i, 0)) for w in ws],
        out_shape=[jax.ShapeDtypeStruct((N_QUAD,) + w.shape, BF16) for w in ws], args=list(ws), comms=comms)


def gather_comm(stacks):
    n = len(stacks)

    def copies(bufs, ss, rs, only_sends=False):
        mx, my, mc = _place()
        q = 2 * mx + my
        sibling = (mx, my, 1 - mc)
        ici_send, ici_recv, fwd_send, fwd_recv = [], [], [], []
        for p in range(n):
            hr = stacks[p].shape[1] // 2
            mine, other = pl.ds(mc * hr, hr), pl.ds((1 - mc) * hr, hr)
            for k, chip in enumerate(_other_chips(mx, my)):
                qk = 2 * chip[0] + chip[1]
                peer = (chip[0], chip[1], mc)
                own, landed, theirs = bufs[p].at[q, mine, :], bufs[p].at[qk, mine, :], bufs[p].at[qk, other, :]
                ici_send.append(_remote(own, own, ss.at[6 * p + k], rs.at[6 * p + k], peer))
                if only_sends:
                    continue
                ici_recv.append(_remote(own, landed, ss.at[6 * p + k], rs.at[6 * p + k], peer))
                fwd_send.append(_remote(landed, landed, ss.at[6 * p + 3 + k], rs.at[6 * p + 3 + k], sibling))
                fwd_recv.append(_remote(theirs, theirs, ss.at[6 * p + 3 + k], rs.at[6 * p + 3 + k], sibling))
        return ici_send, ici_recv, fwd_send, fwd_recv

    def start(srcs, bufs, lands, ss, rs):
        for cp in copies(bufs, ss, rs, only_sends=True)[0]:
            cp.start()

    def finish(srcs, bufs, lands, ss, rs):
        ici_send, ici_recv, fwd_send, fwd_recv = copies(bufs, ss, rs)
        for arrived, onward in zip(ici_recv, fwd_send):
            arrived.wait_recv()
            onward.start()
        for cp in fwd_recv:
            cp.wait_recv()
        for cp in ici_send + fwd_send:
            cp.wait_send()

    return Comm(bufs=stacks, n_sems=6 * n, start=start, finish=finish)


def rs_pair_comm(gs):
    n = len(gs)

    def copies(srcs, lands, ss, rs):
        mx, my, mc = _place()
        out = []
        for p in range(n):
            hr = gs[p].shape[1] // 2
            out.append(_remote(srcs[p].at[:, pl.ds((1 - mc) * hr, hr), :], lands[p], ss.at[p], rs.at[p], (mx, my, 1 - mc)))
        return out

    def start(srcs, bufs, lands, ss, rs):
        for cp in copies(srcs, lands, ss, rs):
            cp.start()

    def finish(srcs, bufs, lands, ss, rs):
        for cp in copies(srcs, lands, ss, rs):
            cp.wait()

    return Comm(srcs=gs, land_shapes=[jax.ShapeDtypeStruct((g.shape[0], g.shape[1] // 2, g.shape[2]), g.dtype) for g in gs],
                n_sems=n, start=start, finish=finish)


def rs_chip_comm(ss_):
    n = len(ss_)

    def copies(srcs, lands, ss, rs):
        mx, my, mc = _place()
        out = []
        for p in range(n):
            for k, chip in enumerate(_other_chips(mx, my)):
                out.append(_remote(srcs[p].at[2 * chip[0] + chip[1]], lands[p].at[k], ss.at[3 * p + k], rs.at[3 * p + k],
                                   (chip[0], chip[1], mc)))
        return out

    def start(srcs, bufs, lands, ss, rs):
        for cp in copies(srcs, lands, ss, rs):
            cp.start()

    def finish(srcs, bufs, lands, ss, rs):
        for cp in copies(srcs, lands, ss, rs):
            cp.wait()

    return Comm(srcs=ss_, land_shapes=[jax.ShapeDtypeStruct((N_QUAD - 1,) + s.shape[1:], s.dtype) for s in ss_],
                n_sems=3 * n, start=start, finish=finish)


def rs_share_comm(fulls):
    n = len(fulls)

    def copies(bufs, ss, rs, only_sends=False):
        mx, my, mc = _place()
        send, recv = [], []
        for p in range(n):
            hr = fulls[p].shape[0] // 2
            mine, other = bufs[p].at[pl.ds(mc * hr, hr), :], bufs[p].at[pl.ds((1 - mc) * hr, hr), :]
            send.append(_remote(mine, mine, ss.at[p], rs.at[p], (mx, my, 1 - mc)))
            if not only_sends:
                recv.append(_remote(other, other, ss.at[p], rs.at[p], (mx, my, 1 - mc)))
        return send, recv

    def start(srcs, bufs, lands, ss, rs):
        for cp in copies(bufs, ss, rs, only_sends=True)[0]:
            cp.start()

    def finish(srcs, bufs, lands, ss, rs):
        send, recv = copies(bufs, ss, rs)
        for cp in recv:
            cp.wait_recv()
        for cp in send:
            cp.wait_send()

    return Comm(bufs=fulls, n_sems=n, start=start, finish=finish)


def pair_sum(g, recv, mc, *, name):
    nq, R, C = g.shape
    hr = R // 2
    rb = _row_block(hr, 512)
    nb = hr // rb

    def body(s_ref, g_ref, r_ref, o_ref):
        o_ref[...] = (g_ref[...].astype(F32) + r_ref[...].astype(F32)).astype(BF16)

    return pl.pallas_call(
        body, name=name, out_shape=jax.ShapeDtypeStruct((nq, hr, C), BF16),
        grid_spec=pltpu.PrefetchScalarGridSpec(
            num_scalar_prefetch=1, grid=(nq, nb),
            in_specs=[pl.BlockSpec((1, rb, C), lambda q, i, s: (q, s[0] * nb + i, 0)),
                      pl.BlockSpec((1, rb, C), lambda q, i, s: (q, i, 0))],
            out_specs=pl.BlockSpec((1, rb, C), lambda q, i, s: (q, i, 0))),
        compiler_params=pltpu.CompilerParams(dimension_semantics=("arbitrary", "arbitrary"),
                                             vmem_limit_bytes=VMEM_LIMIT_BYTES),
    )(mc, g, recv)


def chip_sum(s, recv, quad_mc, *, name):
    nq, hr, C = s.shape
    rb = _row_block(hr, 256)
    nb = hr // rb

    def body(q_ref, s_ref, r_ref, o_ref):
        total = s_ref[0].astype(F32)
        for k in range(N_QUAD - 1):
            total = total + r_ref[k].astype(F32)
        o_ref[...] = total

    return pl.pallas_call(
        body, name=name, out_shape=jax.ShapeDtypeStruct((2 * hr, C), F32),
        grid_spec=pltpu.PrefetchScalarGridSpec(
            num_scalar_prefetch=1, grid=(nb,),
            in_specs=[pl.BlockSpec((1, rb, C), lambda i, q: (q[0], i, 0)),
                      pl.BlockSpec((N_QUAD - 1, rb, C), lambda i, q: (0, i, 0))],
            out_specs=pl.BlockSpec((rb, C), lambda i, q: (q[1] * nb + i, 0))),
        compiler_params=pltpu.CompilerParams(dimension_semantics=("arbitrary",), vmem_limit_bytes=VMEM_LIMIT_BYTES),
    )(quad_mc, s, recv)


def _stack_cols(w_full):
    K, N = w_full.shape
    return w_full.reshape(K, N_QUAD, N // N_QUAD).transpose(1, 0, 2)


def _unstack_cols(w4):
    nq, K, n = w4.shape
    return w4.transpose(1, 0, 2).reshape(K, nq * n)


def kernel(x, c, w_ada, b_ada, g_norm1, ffn1_w_gate, ffn1_w_up, ffn1_w_down, g_norm2, w_in, g_sgu_ln, b_sgu_ln, w_spatial, b_spatial, g_q, g_k, attn_sinks, w_branch_a, w_branch_b, w_out, g_norm3, ffn2_w_gate, ffn2_w_up, ffn2_w_down, loss_target, m_w_ada, m_b_ada, m_g_norm1, m_ffn1_w_gate, m_ffn1_w_up, m_ffn1_w_down, m_g_norm2, m_w_in, m_g_sgu_ln, m_b_sgu_ln, m_w_spatial, m_b_spatial, m_g_q, m_g_k, m_attn_sinks, m_w_branch_a, m_w_branch_b, m_w_out, m_g_norm3, m_ffn2_w_gate, m_ffn2_w_up, m_ffn2_w_down, v_w_ada, v_b_ada, v_g_norm1, v_ffn1_w_gate, v_ffn1_w_up, v_ffn1_w_down, v_g_norm2, v_w_in, v_g_sgu_ln, v_b_sgu_ln, v_w_spatial, v_b_spatial, v_g_q, v_g_k, v_attn_sinks, v_w_branch_a, v_w_branch_b, v_w_out, v_g_norm3, v_ffn2_w_gate, v_ffn2_w_up, v_ffn2_w_down):
    B, S, D = x.shape
    T = B * S
    n_mod = b_ada.shape[1] // D
    mx, my, mc = _place()
    quad = 2 * mx + my
    mc_arr = jnp.reshape(mc, (1,)).astype(jnp.int32)
    quad_arr = jnp.reshape(quad, (1,)).astype(jnp.int32)
    quad_mc = jnp.stack([quad, mc]).astype(jnp.int32)
    tm = min(512, S)
    tm_small = min(256, S)
    tm_big = min(1024, S)
    tt_half = min(2048, T)
    cpb = min(4, S // CHUNK)

    xt = x.reshape(T, D)
    tgt = loss_target.reshape(T, D)

    tr = lambda a: jnp.swapaxes(a, 1, 2)
    stack_wg1, stack_wu1 = cast_into_stacks([tr(ffn1_w_gate)[0], tr(ffn1_w_up)[0]], quad_arr, name="stack_gu1")
    gather_wg1, gather_wu1 = gather_comm([stack_wg1]), gather_comm([stack_wu1])
    later = [ffn1_w_down, tr(w_in), w_branch_a, w_branch_b, w_out, tr(ffn2_w_gate), tr(ffn2_w_up), ffn2_w_down]
    stacks = cast_into_stacks([w[0] for w in later], quad_arr, name="stack_rest", comms=[gather_wg1])
    (wg1,) = gather_wg1.bufs_out
    gather_wd1, gather_win = gather_comm([stacks[0]]), gather_comm([stacks[1]])
    gather_abo = gather_comm(stacks[2:5])
    gather_wg3, gather_wu3, gather_wd3 = gather_comm([stacks[5]]), gather_comm([stacks[6]]), gather_comm([stacks[7]])

    c_all = all_gather8(c, name="gather_cond").reshape(N_DEV * B, D)
    n_ada = w_ada.shape[2]
    b_mine = lax.dynamic_slice(b_ada, (0, quad * n_ada), (1, n_ada))
    mods_part = ada_fwd(c_all, w_ada[0], b_mine, name="ada_fwd")
    mods_parts = all_gather8(mods_part, name="gather_mods", comms=[gather_wu1])
    (wu1,) = gather_wu1.bufs_out
    mods = jnp.concatenate([mods_parts[2 * j] for j in range(N_QUAD)], axis=1)
    me = 4 * mx + 2 * my + mc
    mods = lax.dynamic_slice(mods, (me * B, 0), (B, n_mod * D))
    sh1, sc1, ga1, sh2, sc2, ga2, sh3, sc3, ga3 = [mods[:, j * D:(j + 1) * D] for j in range(n_mod)]
    bs_t = b_spatial[0].T
    ws = w_spatial[0]

    xn1 = norm_mod_fwd(xt, g_norm1, sc1, sh1, seq=S, tm=tm, name="norm1")
    g1, u1, a1 = ffn_up(xn1, wg1, wu1, tm=tm_big, name="ffn1_up", comms=[gather_wd1, gather_win])
    (wd1,), (win4,) = gather_wd1.bufs_out, gather_win.bufs_out
    win = win4.reshape(-1, D)
    w_uv, w_qkv, w_gt = win[0:2 * D_A], win[2 * D_A:2 * D_A + QKV_W], win[2 * D_A + QKV_W:]
    h1, f1, xn2 = ffn_down(a1, wd1, xt, ga1, norm=(g_norm2, sc2, sh2), seq=S, tm=tm, name="ffn1_down",
                           comms=[gather_abo])
    wa4, wb4, wo4 = gather_abo.bufs_out
    wa, wb = _unstack_cols(wa4), _unstack_cols(wb4)
    wo = wo4.reshape(D, D)
    puv, pqkv, pgt = proj_fwd(xn2, [w_uv, w_qkv, w_gt], tm=tm_small, name="proj", comms=[gather_wg3])
    (wg3,) = gather_wg3.bufs_out
    sgu = sgu_fwd(puv, g_sgu_ln, b_sgu_ln, ws, bs_t, cpb=cpb, name="sgu_fwd")
    gq_t, gk_t = jnp.tile(g_q, (1, N_Q)), jnp.tile(g_k, (1, N_KV))
    att = attn_fwd(pqkv, gq_t, gk_t, attn_sinks, seq=S, name="attn_fwd", comms=[gather_wu3])
    (wu3,) = gather_wu3.bufs_out
    ya, yb, merged, mm, h2, xn3 = mixer_out_fwd(sgu, att, pgt, h1, ga2, wa, wb, wo, (g_norm3, sc3, sh3), seq=S,
                                                tm=tm_small, name="mixer_out", comms=[gather_wd3])
    (wd3,) = gather_wd3.bufs_out
    g3, u3, a3 = ffn_up(xn3, wg3, wu3, tm=tm_big, name="ffn2_up")
    dy, f3, lparts = ffn_down(a3, wd3, h2, ga3, target=tgt, seq=S, tm=tm, name="ffn2_down_loss")
    loss_part = jnp.sum(lparts[:, 0, 0])

    def sums_of(pair, tag):
        return [pair_sum(g, r, mc_arr, name=f"pair_sum_{tag}_{p}") for p, (g, r) in enumerate(zip(pair.srcs, pair.lands))]

    def halves_of(chip, tag):
        return [chip_sum(s, r, quad_mc, name=f"chip_sum_{tag}_{p}") for p, (s, r) in enumerate(zip(chip.srcs, chip.lands))]

    dg3, du3, df3, dga3 = ffn_bwd_act(dy, f3, ga3, wd3, g3, u3, seq=S, tm=tm, name="ffn2_act_bwd")
    (dwd3,) = mm_tn([(a3, df3)], tt=T, name="ffn2_dwd")
    pair_wd3 = rs_pair_comm([dwd3])
    dwg3, dwu3 = mm_tn([(dg3, xn3), (du3, xn3)], tt=tt_half, name="ffn2_dwgu", comms=[pair_wd3])
    chip_wd3 = rs_chip_comm(sums_of(pair_wd3, "wd3"))
    pair_gu3 = rs_pair_comm([dwg3, dwu3])
    dh2, dsh3, dsc3, dgn3 = dx_norm_bwd([dg3, du3], [wg3, wu3], h2, dy, g_norm3, sc3, seq=S, tm=tm, name="ffn2_dx",
                                        comms=[chip_wd3, pair_gu3])
    sum_wg3, sum_wu3 = sums_of(pair_gu3, "gu3")
    chip_wg3, chip_wu3 = rs_chip_comm([sum_wg3]), rs_chip_comm([sum_wu3])

    dgt, dya, dyb, dsgu, datt, dm, dga2 = mixer_out_bwd(dh2, mm, ga2, ya, yb, pgt, wa, wb, wo, seq=S, tm=tm_small,
                                                        name="mixer_out_bwd", comms=[chip_wg3])
    (dwo,) = mm_tn([(merged, dm)], tt=tt_half, name="mixer_dwo")
    (dwa,) = mm_tn([(sgu, dya)], tt=tt_half, name="mixer_dwa")
    (dwb,) = mm_tn([(att, dyb)], tt=tt_half, name="mixer_dwb")
    pair_abo = rs_pair_comm([_stack_cols(dwa), _stack_cols(dwb), dwo.reshape(N_QUAD, D // N_QUAD, D)])
    duv, dws, dzs, dgln, dbln = sgu_bwd(puv, dsgu, g_sgu_ln, b_sgu_ln, ws, bs_t, cpb=cpb, name="sgu_bwd",
                                        comms=[pair_abo])
    chip_abo = rs_chip_comm(sums_of(pair_abo, "abo"))
    dqkv, dgq_h, dgk_h, dsk = attn_bwd(pqkv, datt, gq_t, gk_t, attn_sinks, seq=S, name="attn_bwd",
                                       comms=[chip_wu3, chip_abo])
    dgq = dgq_h.reshape(N_Q, HEAD_DIM).sum(axis=0, keepdims=True)
    dgk = dgk_h.reshape(N_KV, HEAD_DIM).sum(axis=0, keepdims=True)
    share3 = rs_share_comm(halves_of(chip_wg3, "wg3") + halves_of(chip_wu3, "wu3") + halves_of(chip_wd3, "wd3"))
    dwin_uv, dwin_qkv = mm_tn([(duv, xn2), (dqkv, xn2)], tt=tt_half, name="mixer_dwin_a", comms=[share3])
    (dwin_gt,) = mm_tn([(dgt, xn2)], tt=tt_half, name="mixer_dwin_b")
    dwin_parts = [dwin_uv, dwin_qkv, dwin_gt]
    r_wg3, r_wu3, r_wd3 = share3.bufs_out
    pair_win = rs_pair_comm([jnp.concatenate(dwin_parts, axis=0).reshape(win4.shape)])
    dh1, dsh2, dsc2, dgn2 = dx_norm_bwd([duv, dqkv, dgt], [w_uv, w_qkv, w_gt], h1, dh2, g_norm2, sc2, seq=S,
                                        tm=tm, name="mixer_dx", comms=[pair_win])
    chip_win = rs_chip_comm(sums_of(pair_win, "win"))

    dg1, du1, df1, dga1 = ffn_bwd_act(dh1, f1, ga1, wd1, g1, u1, seq=S, tm=tm, name="ffn1_act_bwd", comms=[chip_win])
    share_mix = rs_share_comm(halves_of(chip_win, "win") + halves_of(chip_abo, "abo"))

    db_s = dzs.reshape(CHUNK, N_GROUPS, D_A // N_GROUPS).sum(axis=2).T.reshape(1, N_GROUPS * CHUNK)
    tail = jnp.concatenate([db_s, dgq, dgk, dsk[0:1, 0:N_Q], loss_part.reshape(1, 1)], axis=1)
    tail = jnp.pad(tail, ((0, 0), (0, (-tail.shape[1]) % D)))
    lnrow = jnp.concatenate([dgln, dbln], axis=1)
    lnrow = jnp.pad(lnrow, ((0, 0), (0, (-lnrow.shape[1]) % D)))
    packed = jnp.concatenate([dgn2, dgn3, lnrow.reshape(-1, D), tail.reshape(-1, D), dws.reshape(-1, D)], axis=0)
    n_rows = packed.shape[0]
    packed = jnp.pad(packed, ((0, (-n_rows) % 8), (0, 0)))
    gather_small = ag8_comm(packed)

    dwg1, dwu1 = mm_tn([(dg1, xn1), (du1, xn1)], tt=tt_half, name="ffn1_dwgu", comms=[share_mix, gather_small])
    r_win, r_wa, r_wb, r_wo = share_mix.bufs_out
    small = sum8(gather_small.lands[0], name="sum_small")
    pair_gu1 = rs_pair_comm([dwg1, dwu1])
    (dwd1,) = mm_tn([(a1, df1)], tt=T, name="ffn1_dwd", comms=[pair_gu1])
    chip_gu1 = rs_chip_comm(sums_of(pair_gu1, "gu1"))
    pair_wd1 = rs_pair_comm([dwd1])
    dx, dsh1, dsc1, dgn1 = dx_norm_bwd([dg1, du1], [wg1, wu1], xt, dh1, g_norm1, sc1, seq=S, tm=tm, name="ffn1_dx",
                                       comms=[chip_gu1, pair_wd1])
    chip_wd1 = rs_chip_comm(sums_of(pair_wd1, "wd1"))
    share_gu1 = rs_share_comm(halves_of(chip_gu1, "gu1"))

    dmods = jnp.concatenate([dsh1, dsc1, dga1, dsh2, dsc2, dga2, dsh3, dsc3, dga3], axis=1)
    dmods = jnp.concatenate([dmods, jnp.pad(dgn1, ((0, 0), (0, (n_mod - 1) * D)))], axis=0)
    gathered = all_gather8(dmods, name="gather_dmods", comms=[chip_wd1, share_gu1])
    r_wg1, r_wu1 = share_gu1.bufs_out
    dmods_all = gathered[:, 0:B].reshape(N_DEV * B, n_mod * D)
    dmods_mine = lax.dynamic_slice(dmods_all, (0, quad * n_ada), (N_DEV * B, n_ada))
    share_wd1 = rs_share_comm(halves_of(chip_wd1, "wd1"))
    db_ada, dw_ada, g_gn1 = ada_bwd(c_all, dmods_all, dmods_mine, gathered[:, B, 0:D], name="ada_bwd", comms=[share_wd1])
    (r_wd1,) = share_wd1.bufs_out

    ln_rows = lnrow.size // D
    tail_rows = tail.size // D
    r = 2
    g_gn2, g_gn3 = small[0:1], small[1:2]
    ln_flat = small[r:r + ln_rows].reshape(1, -1)
    r += ln_rows
    tail_flat = small[r:r + tail_rows].reshape(1, -1)
    r += tail_rows
    g_ws = small[r:r + dws.size // D].reshape(w_spatial.shape)
    g_gln, g_bln = ln_flat[:, 0:D_A], ln_flat[:, D_A:2 * D_A]
    o = N_GROUPS * CHUNK
    g_bs = tail_flat[:, 0:o].reshape(b_spatial.shape)
    g_gq, g_gk, g_sk = tail_flat[:, o:o + HEAD_DIM], tail_flat[:, o + HEAD_DIM:o + 2 * HEAD_DIM], tail_flat[:, o + 2 * HEAD_DIM:o + 2 * HEAD_DIM + N_Q]
    loss = tail_flat[0, o + 2 * HEAD_DIM + N_Q]

    transposed = ("ffn1_w_gate", "ffn1_w_up", "w_in", "ffn2_w_gate", "ffn2_w_up")

    def update(name, w, g, m, v):
        if name in transposed:
            w, m, v = tr(w), tr(m), tr(v)
        shape = w.shape
        two_d = lambda a: a.reshape(-1, shape[-1])
        res = adamw(two_d(w), two_d(g), two_d(m), two_d(v), name=f"adamw_{name}")
        res = [a.reshape(shape) for a in [g] + list(res)]
        return [tr(a) for a in res] if name in transposed else res

    names = ["w_ada", "b_ada", "g_norm1", "ffn1_w_gate", "ffn1_w_up", "ffn1_w_down", "g_norm2", "w_in", "g_sgu_ln",
             "b_sgu_ln", "w_spatial", "b_spatial", "g_q", "g_k", "attn_sinks", "w_branch_a", "w_branch_b", "w_out",
             "g_norm3", "ffn2_w_gate", "ffn2_w_up", "ffn2_w_down"]
    weights = dict(zip(names, [w_ada, b_ada, g_norm1, ffn1_w_gate, ffn1_w_up, ffn1_w_down, g_norm2, w_in, g_sgu_ln,
                               b_sgu_ln, w_spatial, b_spatial, g_q, g_k, attn_sinks, w_branch_a, w_branch_b, w_out,
                               g_norm3, ffn2_w_gate, ffn2_w_up, ffn2_w_down]))
    m_in = dict(zip(names, [m_w_ada, m_b_ada, m_g_norm1, m_ffn1_w_gate, m_ffn1_w_up, m_ffn1_w_down, m_g_norm2, m_w_in,
                            m_g_sgu_ln, m_b_sgu_ln, m_w_spatial, m_b_spatial, m_g_q, m_g_k, m_attn_sinks, m_w_branch_a,
                            m_w_branch_b, m_w_out, m_g_norm3, m_ffn2_w_gate, m_ffn2_w_up, m_ffn2_w_down]))
    v_in = dict(zip(names, [v_w_ada, v_b_ada, v_g_norm1, v_ffn1_w_gate, v_ffn1_w_up, v_ffn1_w_down, v_g_norm2, v_w_in,
                            v_g_sgu_ln, v_b_sgu_ln, v_w_spatial, v_b_spatial, v_g_q, v_g_k, v_attn_sinks, v_w_branch_a,
                            v_w_branch_b, v_w_out, v_g_norm3, v_ffn2_w_gate, v_ffn2_w_up, v_ffn2_w_down]))
    grads = {
        "w_ada": dw_ada[None], "b_ada": db_ada, "g_norm1": g_gn1, "g_norm2": g_gn2, "g_norm3": g_gn3,
        "g_sgu_ln": g_gln, "b_sgu_ln": g_bln, "w_spatial": g_ws, "b_spatial": g_bs, "g_q": g_gq, "g_k": g_gk,
        "attn_sinks": g_sk,
        "ffn1_w_gate": r_wg1[None], "ffn1_w_up": r_wu1[None], "ffn1_w_down": r_wd1[None], "w_in": r_win[None],
        "w_branch_a": r_wa[None], "w_branch_b": r_wb[None], "w_out": r_wo[None],
        "ffn2_w_gate": r_wg3[None], "ffn2_w_up": r_wu3[None], "ffn2_w_down": r_wd3[None],
    }

    small_names = ["b_ada", "g_norm1", "g_norm2", "g_norm3", "g_sgu_ln", "b_sgu_ln", "w_spatial", "b_spatial", "g_q",
                   "g_k", "attn_sinks"]

    def pack(d):
        flat = jnp.concatenate([d[nm].reshape(-1) for nm in small_names])
        return jnp.pad(flat, (0, (-flat.size) % (8 * 128))).reshape(-1, 128)

    sd, sm, sv = adamw(pack(weights), pack(grads), pack(m_in), pack(v_in), name="adamw_small")
    delta, new_m, new_v = {}, {}, {}
    off = 0
    for nm in small_names:
        size, shape = weights[nm].size, weights[nm].shape
        delta[nm] = sd.reshape(-1)[off:off + size].reshape(shape)
        new_m[nm] = sm.reshape(-1)[off:off + size].reshape(shape)
        new_v[nm] = sv.reshape(-1)[off:off + size].reshape(shape)
        off += size
    for nm in names:
        if nm not in small_names:
            grads[nm], delta[nm], new_m[nm], new_v[nm] = update(nm, weights[nm], grads[nm], m_in[nm], v_in[nm])
        else:
            grads[nm] = grads[nm].reshape(weights[nm].shape)

    return (loss, dx.reshape(B, S, D), *[grads[nm] for nm in names], *[delta[nm] for nm in names],
            *[new_m[nm] for nm in names], *[new_v[nm] for nm in names])
```

```python
import functools
import math
import operator

import jax
import jax.numpy as jnp
from jax import lax
from jax.experimental import pallas as pl
from jax.experimental.pallas import tpu as pltpu

F32 = jnp.float32
BF16 = jnp.bfloat16
EPS = 1e-6
NEG = -1e30
N_DEV = 8
N_QUAD = 4
D_A = 512
N_GROUPS = 4
CHUNK = 128
HEAD_DIM = 64
N_KV = 2
Q_PER_KV = 4
N_Q = N_KV * Q_PER_KV
D_B = N_Q * HEAD_DIM
QKV_W = D_B + 2 * N_KV * HEAD_DIM
K_OFF = D_B
V_OFF = D_B + N_KV * HEAD_DIM
ATT_SCALE = HEAD_DIM ** -0.5
GELU_K = math.sqrt(2.0 / math.pi)
GELU_C = 0.044715
ADAM_LR, ADAM_B1, ADAM_B2, ADAM_EPS, ADAM_WD, ADAM_STEP = 0.001, 0.9, 0.999, 1e-08, 0.01, 10
VMEM_LIMIT_BYTES = 56 * 1024 * 1024
MESH = pl.DeviceIdType.MESH
NT = (((1,), (1,)), ((), ()))
TN = (((0,), (0,)), ((), ()))
ANY = pl.BlockSpec(memory_space=pl.ANY)


def _dot(a, b):
    return jnp.dot(a, b, preferred_element_type=F32)


def _dg(a, b, dims):
    return lax.dot_general(a, b, dims, preferred_element_type=F32)


def _gelu_parts(x):
    t = jnp.tanh(GELU_K * (x + GELU_C * x * x * x))
    return 0.5 * x * (1.0 + t), t


def _dgelu(x, t):
    return 0.5 * (1.0 + t) + 0.5 * x * (1.0 - t * t) * (GELU_K * (1.0 + 3.0 * GELU_C * x * x))


def _row_block(rows, target):
    b = min(rows, target)
    while rows % b or b % 8:
        b -= 1
    return b


def _resident(a):
    return pl.BlockSpec(a.shape, lambda *_: (0,) * a.ndim, pipeline_mode=pl.Buffered(1))


class Comm:
    def __init__(self, *, srcs=(), bufs=(), land_shapes=(), n_sems, start, finish):
        self.srcs, self.bufs, self.land_shapes = list(srcs), list(bufs), list(land_shapes)
        self.n_sems, self.start, self.finish = n_sems, start, finish
        self.bufs_out, self.lands = None, None


def _call(body, *, name, grid, in_specs, out_specs, out_shape, args, scratch_shapes=(), comms=(), prefetch=()):
    prefetch = list(prefetch)
    in_specs, out_specs, out_shape = list(in_specs), list(out_specs), list(out_shape)
    args, scratch = list(args), list(scratch_shapes)
    n_in, n_out, n_scr = len(args), len(out_shape), len(scratch)
    aliases, layout = {}, []
    for cm in comms:
        i0, o0, s0 = len(args), len(out_shape), len(scratch)
        args += cm.srcs + cm.bufs
        in_specs += [ANY] * (len(cm.srcs) + len(cm.bufs))
        out_shape += [jax.ShapeDtypeStruct(b.shape, b.dtype) for b in cm.bufs] + cm.land_shapes
        out_specs += [ANY] * (len(cm.bufs) + len(cm.land_shapes))
        for j in range(len(cm.bufs)):
            aliases[len(prefetch) + i0 + len(cm.srcs) + j] = o0 + j
        scratch += [pltpu.SemaphoreType.DMA((cm.n_sems,)), pltpu.SemaphoreType.DMA((cm.n_sems,))]
        layout.append((i0, o0, s0))
    n_in_all, n_out_all = len(args), len(out_shape)

    def wrapped(*refs):
        scalars, refs = refs[:len(prefetch)], refs[len(prefetch):]
        ins, outs, scr = refs[:n_in_all], refs[n_in_all:n_in_all + n_out_all], refs[n_in_all + n_out_all:]
        parts = []
        for cm, (i0, o0, s0) in zip(comms, layout):
            nb = len(cm.bufs)
            parts.append((ins[i0:i0 + len(cm.srcs)], outs[o0:o0 + nb], outs[o0 + nb:o0 + nb + len(cm.land_shapes)],
                          scr[s0], scr[s0 + 1]))
        if comms and grid:
            ids = [pl.program_id(d) for d in range(len(grid))]
            first = functools.reduce(operator.and_, [i == 0 for i in ids])
            last = functools.reduce(operator.and_, [i == g - 1 for i, g in zip(ids, grid)])

            @pl.when(first)
            def _():
                for cm, p in zip(comms, parts):
                    cm.start(*p)
        elif comms:
            for cm, p in zip(comms, parts):
                cm.start(*p)
        if body is not None:
            body(*scalars, *ins[:n_in], *outs[:n_out], *scr[:n_scr])
        if comms and grid:
            @pl.when(last)
            def _():
                for cm, p in zip(comms, parts):
                    cm.finish(*p)
        elif comms:
            for cm, p in zip(comms, parts):
                cm.finish(*p)

    if prefetch:
        kwargs = dict(grid_spec=pltpu.PrefetchScalarGridSpec(
            num_scalar_prefetch=len(prefetch), grid=grid, in_specs=in_specs, out_specs=out_specs, scratch_shapes=scratch))
    else:
        kwargs = dict(in_specs=in_specs, out_specs=out_specs, scratch_shapes=scratch, **(dict(grid=grid) if grid else {}))
    sem = ("arbitrary",) * len(grid)
    res = pl.pallas_call(
        wrapped, name=name, out_shape=out_shape, input_output_aliases=aliases,
        compiler_params=pltpu.CompilerParams(dimension_semantics=sem, vmem_limit_bytes=VMEM_LIMIT_BYTES), **kwargs,
    )(*prefetch, *args)
    for cm, (i0, o0, s0) in zip(comms, layout):
        nb = len(cm.bufs)
        cm.bufs_out = list(res[o0:o0 + nb])
        cm.lands = list(res[o0 + nb:o0 + nb + len(cm.land_shapes)])
    return list(res[:n_out])


def run_comms(comms, *, name):
    _call(None, name=name, grid=(), in_specs=[], out_specs=[], out_shape=[], args=[], comms=comms)


def norm_mod_fwd(h, g, sc, sh, *, seq, tm, name, comms=()):
    T, D = h.shape
    B, tps = T // seq, seq // tm

    def body(h_ref, g_ref, sc_ref, sh_ref, o_ref):
        x = h_ref[...]
        r = lax.rsqrt(jnp.mean(x * x, axis=-1, keepdims=True) + EPS)
        y = x * r * g_ref[...]
        o_ref[...] = (y * (1.0 + sc_ref[0]) + sh_ref[0]).astype(o_ref.dtype)

    per_seq = pl.BlockSpec((1, 1, D), lambda i: (i // tps, 0, 0))
    return _call(
        body, name=name, grid=(T // tm,),
        in_specs=[pl.BlockSpec((tm, D), lambda i: (i, 0)), pl.BlockSpec((1, D), lambda i: (0, 0)), per_seq, per_seq],
        out_specs=[pl.BlockSpec((tm, D), lambda i: (i, 0))], out_shape=[jax.ShapeDtypeStruct((T, D), BF16)],
        args=[h, g, sc.reshape(B, 1, D), sh.reshape(B, 1, D)], comms=comms)[0]


def ffn_up(xn, wg, wu, *, tm, name, comms=()):
    T, D = xn.shape
    nq, fs, _ = wg.shape

    def body(x_ref, wg_ref, wu_ref, s_ref, q_ref, a_ref):
        x = x_ref[...]
        g = _dg(x, wg_ref[0], NT)
        u = _dg(x, wu_ref[0], NT)
        sg = jax.nn.sigmoid(g)
        s = g * sg
        s_ref[0] = s.astype(BF16)
        q_ref[0] = (u * (sg + s * (1.0 - sg))).astype(BF16)
        a_ref[0] = (s * u).astype(BF16)

    w_spec = pl.BlockSpec((1, fs, D), lambda q, i: (q, 0, 0))
    o_spec = pl.BlockSpec((1, tm, fs), lambda q, i: (q, i, 0))
    o_shape = jax.ShapeDtypeStruct((nq, T, fs), BF16)
    return _call(
        body, name=name, grid=(nq, T // tm),
        in_specs=[pl.BlockSpec((tm, D), lambda q, i: (i, 0)), w_spec, w_spec],
        out_specs=[o_spec, o_spec, o_spec], out_shape=[o_shape, o_shape, o_shape], args=[xn, wg, wu], comms=comms)


def _norm_mod(y, g_ref, sc_ref, sh_ref):
    nx, _ = _rms_parts(y)
    return ((nx * g_ref[...]) * (1.0 + sc_ref[0]) + sh_ref[0]).astype(BF16)


def ffn_down(a, wd, hin, ga, *, target=None, norm=None, seq, tm, name, comms=()):
    nq, T, fs = a.shape
    D = wd.shape[-1]
    B, tps, nt = T // seq, seq // tm, T // tm

    def body(a_ref, wd_ref, h_ref, ga_ref, *rest):
        rest = list(rest)
        t_ref = rest.pop(0) if target is not None else None
        norm_refs = [rest.pop(0) for _ in range(3)] if norm is not None else None
        o_ref, f_ref = rest[0], rest[1]
        f = _dot(a_ref[0], wd_ref[0])
        for q in range(1, nq):
            f = f + _dot(a_ref[q], wd_ref[q])
        f_ref[...] = f.astype(BF16)
        y = h_ref[...] + (0.5 * ga_ref[0]) * f
        if target is not None:
            e = y - t_ref[...]
            o_ref[...] = e * (1.0 / D)
            rest[2][...] = jnp.full(rest[2].shape, 0.5 * jnp.sum(e * e) * (1.0 / D), F32)
        else:
            o_ref[...] = y
        if norm is not None:
            rest[2][...] = _norm_mod(y, *norm_refs)

    tile = pl.BlockSpec((tm, D), lambda i: (i, 0))
    per_seq = pl.BlockSpec((1, 1, D), lambda i: (i // tps, 0, 0))
    in_specs = [pl.BlockSpec((nq, tm, fs), lambda i: (0, i, 0)), _resident(wd), tile, per_seq]
    out_specs = [tile, tile]
    out_shape = [jax.ShapeDtypeStruct((T, D), F32), jax.ShapeDtypeStruct((T, D), BF16)]
    args = [a, wd, hin, ga.reshape(B, 1, D)]
    if target is not None:
        in_specs.append(tile)
        args.append(target)
        out_specs.append(pl.BlockSpec((1, 8, 128), lambda i: (i, 0, 0)))
        out_shape.append(jax.ShapeDtypeStruct((nt, 8, 128), F32))
    if norm is not None:
        in_specs += [pl.BlockSpec((1, D), lambda i: (0, 0)), per_seq, per_seq]
        args += [norm[0], norm[1].reshape(B, 1, D), norm[2].reshape(B, 1, D)]
        out_specs.append(tile)
        out_shape.append(jax.ShapeDtypeStruct((T, D), BF16))
    return _call(body, name=name, grid=(nt,), in_specs=in_specs, out_specs=out_specs, out_shape=out_shape, args=args,
                 comms=comms)


def proj_fwd(xn, ws, *, tm, name, comms=()):
    T, D = xn.shape
    n = len(ws)

    def body(*refs):
        x = refs[0][...]
        for j in range(n):
            refs[1 + n + j][...] = _dg(x, refs[1 + j][...], NT).astype(BF16)

    return _call(
        body, name=name, grid=(T // tm,),
        in_specs=[pl.BlockSpec((tm, D), lambda i: (i, 0))] + [pl.BlockSpec(w.shape, lambda i: (0, 0)) for w in ws],
        out_specs=[pl.BlockSpec((tm, w.shape[0]), lambda i: (i, 0)) for w in ws],
        out_shape=[jax.ShapeDtypeStruct((T, w.shape[0]), BF16) for w in ws], args=[xn, *ws], comms=comms)


def _layer_norm_parts(v):
    mu = jnp.mean(v, axis=-1, keepdims=True)
    d = v - mu
    rstd = lax.rsqrt(jnp.mean(d * d, axis=-1, keepdims=True) + EPS)
    return d * rstd, rstd


def _causal():
    row = lax.broadcasted_iota(jnp.int32, (CHUNK, CHUNK), 0)
    col = lax.broadcasted_iota(jnp.int32, (CHUNK, CHUNK), 1)
    return row >= col


def sgu_fwd(puv, gln, bln, ws, bs_t, *, cpb, name, comms=()):
    T = puv.shape[0]
    gd = D_A // N_GROUPS
    tm = cpb * CHUNK

    def body(p_ref, gln_ref, bln_ref, ws_ref, bs_ref, o_ref):
        causal = _causal()
        wm = [jnp.where(causal, ws_ref[g], 0.0).astype(BF16) for g in range(N_GROUPS)]
        for j in range(cpb):
            rows = slice(j * CHUNK, (j + 1) * CHUNK)
            u, _ = _gelu_parts(p_ref[rows, 0:D_A].astype(F32))
            vv, _ = _gelu_parts(p_ref[rows, D_A:2 * D_A].astype(F32))
            vhat, _ = _layer_norm_parts(vv)
            vn = (vhat * gln_ref[...] + bln_ref[...]).astype(BF16)
            for g in range(N_GROUPS):
                cols = slice(g * gd, (g + 1) * gd)
                z = _dot(wm[g], vn[:, cols]) + bs_ref[:, g:g + 1]
                o_ref[rows, cols] = (u[:, cols] * z).astype(BF16)

    full = lambda a: pl.BlockSpec(a.shape, lambda i: (0,) * a.ndim)
    return _call(
        body, name=name, grid=(T // tm,),
        in_specs=[pl.BlockSpec((tm, 2 * D_A), lambda i: (i, 0)), full(gln), full(bln), full(ws), full(bs_t)],
        out_specs=[pl.BlockSpec((tm, D_A), lambda i: (i, 0))], out_shape=[jax.ShapeDtypeStruct((T, D_A), BF16)],
        args=[puv, gln, bln, ws, bs_t], comms=comms)[0]


def _rms_parts(x):
    r = lax.rsqrt(jnp.mean(x * x, axis=-1, keepdims=True) + EPS)
    return x * r, r


KV_W = Q_PER_KV * HEAD_DIM
KV2 = N_KV * HEAD_DIM
STACK = Q_PER_KV * CHUNK


def _iota(shape, dim):
    return lax.broadcasted_iota(jnp.int32, shape, dim)


def _head_mean_matrix(n):
    return jnp.where((_iota((n, n), 0) >> 6) == (_iota((n, n), 1) >> 6), 1.0 / HEAD_DIM, 0.0).astype(BF16)


def _repeat_matrix(h):
    return jnp.where(_iota((KV2, KV_W), 0) == h * HEAD_DIM + (_iota((KV2, KV_W), 1) & (HEAD_DIM - 1)), 1.0, 0.0).astype(BF16)


def _fold_matrix(h):
    j, l = _iota((KV_W, KV2), 0), _iota((KV_W, KV2), 1)
    return jnp.where(((j & (HEAD_DIM - 1)) == (l & (HEAD_DIM - 1))) & ((l >> 6) == h), 1.0, 0.0).astype(BF16)


def _dot_split(x, m):
    hi = x.astype(BF16)
    lo = (x - hi.astype(F32)).astype(BF16)
    return _dot(hi, m) + _dot(lo, m)


def _head_rms(x, mean_matrix):
    r = lax.rsqrt(_dot_split(x * x, mean_matrix) + EPS)
    return x * r, r


def _stack_heads(x):
    head = _iota(x.shape, 1) >> 6
    return jnp.concatenate([jnp.where(head == g, x, 0.0).astype(BF16) for g in range(Q_PER_KV)], axis=0)


def _unstack_heads(y):
    head = _iota((CHUNK, KV_W), 1) >> 6
    out = jnp.where(head == 0, y[0:CHUNK], 0.0)
    for g in range(1, Q_PER_KV):
        out = out + jnp.where(head == g, y[g * CHUNK:(g + 1) * CHUNK], 0.0)
    return out


SOFTMAX_ROWS = 32


def _fill_mask_bias(bias_ref):
    qi = _iota((CHUNK, 2 * CHUNK), 0)
    kj = _iota((CHUNK, 2 * CHUNK), 1)
    diff = qi + CHUNK - kj
    window = (diff >= 0) & (diff < CHUNK)
    bias_ref[0] = jnp.where(window & (kj >= CHUNK), 0.0, NEG)
    bias_ref[1] = jnp.where(window, 0.0, NEG)


def _softmax_rows(s_ref, bias_ref, first, sink_ref, h, c):
    rows = pl.ds(pl.multiple_of(c * SOFTMAX_ROWS, SOFTMAX_ROWS), SOFTMAX_ROWS)
    in_block = pl.ds(pl.multiple_of((c % (CHUNK // SOFTMAX_ROWS)) * SOFTMAX_ROWS, SOFTMAX_ROWS), SOFTMAX_ROWS)
    sink = sink_ref[0, h * Q_PER_KV + c // (CHUNK // SOFTMAX_ROWS)]
    s = s_ref[rows, :] + bias_ref[first, in_block, :]
    m = jnp.maximum(jnp.max(s, axis=-1, keepdims=True), sink)
    p = jnp.exp(s - m)
    es = jnp.exp(sink - m)
    inv = 1.0 / (jnp.sum(p, axis=-1, keepdims=True) + es)
    return rows, p * inv, es * inv


def _fill_keys(p_ref, gk_ref, krep, vrep, seq):
    mean_k = _head_mean_matrix(KV2)
    reps = [_repeat_matrix(h) for h in range(N_KV)]
    for h in range(N_KV):
        krep[h][0:CHUNK, :] = jnp.zeros((CHUNK, KV_W), BF16)
        vrep[h][0:CHUNK, :] = jnp.zeros((CHUNK, KV_W), BF16)
    rows = min(seq, 4 * CHUNK)

    def piece(j, carry):
        r0 = pl.multiple_of(j * rows, CHUNK)
        nk, _ = _head_rms(p_ref[pl.ds(r0, rows), K_OFF:K_OFF + KV2].astype(F32), mean_k)
        kn = (nk * gk_ref[...]).astype(BF16)
        v = p_ref[pl.ds(r0, rows), V_OFF:V_OFF + KV2].astype(BF16)
        r1 = pl.multiple_of(r0 + CHUNK, CHUNK)
        for h in range(N_KV):
            krep[h][pl.ds(r1, rows), :] = _dot(kn, reps[h]).astype(BF16)
            vrep[h][pl.ds(r1, rows), :] = _dot(v, reps[h]).astype(BF16)
        return carry

    lax.fori_loop(0, seq // rows, piece, 0)


def attn_fwd(pqkv, gq_t, gk_t, sinks, *, seq, name, comms=()):
    T = pqkv.shape[0]
    nb = seq // CHUNK

    def body(p_ref, gq_ref, gk_ref, sink_ref, o_ref, k0, k1, v0, v1, bias_ref, s_ref, pr_ref):
        krep, vrep = [k0, k1], [v0, v1]
        _fill_keys(p_ref, gk_ref, krep, vrep, seq)
        _fill_mask_bias(bias_ref)
        mean_q = _head_mean_matrix(D_B)

        def block(i, carry):
            r0 = pl.multiple_of(i * CHUNK, CHUNK)
            first = jnp.minimum(i, 1)
            nq, _ = _head_rms(p_ref[pl.ds(r0, CHUNK), 0:D_B].astype(F32), mean_q)
            qn = nq * (gq_ref[...] * ATT_SCALE)
            for h in range(N_KV):
                qs = _stack_heads(qn[:, h * KV_W:(h + 1) * KV_W])
                s_ref[...] = _dg(qs, krep[h][pl.ds(r0, 2 * CHUNK), :], NT)

                def rows_of(c, carry2):
                    rows, probs, _ = _softmax_rows(s_ref, bias_ref, first, sink_ref, h, c)
                    pr_ref[rows, :] = probs.astype(BF16)
                    return carry2

                lax.fori_loop(0, STACK // SOFTMAX_ROWS, rows_of, 0, unroll=True)
                out = _unstack_heads(_dot(pr_ref[...], vrep[h][pl.ds(r0, 2 * CHUNK), :]))
                o_ref[pl.ds(r0, CHUNK), h * KV_W:(h + 1) * KV_W] = out.astype(BF16)
            return carry

        lax.fori_loop(0, nb, block, 0)

    return _call(
        body, name=name, grid=(T // seq,),
        in_specs=[pl.BlockSpec((seq, QKV_W), lambda b: (b, 0)), pl.BlockSpec(gq_t.shape, lambda b: (0, 0)),
                  pl.BlockSpec(gk_t.shape, lambda b: (0, 0)), pl.BlockSpec(memory_space=pltpu.SMEM)],
        out_specs=[pl.BlockSpec((seq, D_B), lambda b: (b, 0))], out_shape=[jax.ShapeDtypeStruct((T, D_B), BF16)],
        scratch_shapes=[pltpu.VMEM((seq + CHUNK, KV_W), BF16)] * 4
        + [pltpu.VMEM((2, CHUNK, 2 * CHUNK), F32), pltpu.VMEM((STACK, 2 * CHUNK), F32), pltpu.VMEM((STACK, 2 * CHUNK), BF16)],
        args=[pqkv, gq_t, gk_t, sinks], comms=comms)[0]


def mixer_out_fwd(sgu, att, pg, hin, ga, wa, wb, wo, norm, *, seq, tm, name, comms=()):
    T, D = hin.shape
    B, tps = T // seq, seq // tm

    def body(s_ref, t_ref, pg_ref, h_ref, ga_ref, wa_ref, wb_ref, wo_ref, g_ref, sc_ref, sh_ref,
             ya_ref, yb_ref, mg_ref, m_ref, ho_ref, xn_ref):
        ya = _dot(s_ref[...], wa_ref[...])
        yb = _dot(t_ref[...], wb_ref[...])
        merged = (jax.nn.sigmoid(pg_ref[:, 0:D].astype(F32)) * ya
                  + jax.nn.sigmoid(pg_ref[:, D:2 * D].astype(F32)) * yb)
        mg = merged.astype(BF16)
        m = _dot(mg, wo_ref[...])
        ya_ref[...] = ya.astype(BF16)
        yb_ref[...] = yb.astype(BF16)
        mg_ref[...] = mg
        m_ref[...] = m.astype(BF16)
        y = h_ref[...] + ga_ref[0] * m
        ho_ref[...] = y
        xn_ref[...] = _norm_mod(y, g_ref, sc_ref, sh_ref)

    tile = lambda w: pl.BlockSpec((tm, w), lambda i: (i, 0))
    per_seq = pl.BlockSpec((1, 1, D), lambda i: (i // tps, 0, 0))
    b16 = jax.ShapeDtypeStruct((T, D), BF16)
    return _call(
        body, name=name, grid=(T // tm,),
        in_specs=[tile(D_A), tile(D_B), tile(2 * D), tile(D), per_seq, _resident(wa), _resident(wb), _resident(wo),
                  pl.BlockSpec((1, D), lambda i: (0, 0)), per_seq, per_seq],
        out_specs=[tile(D)] * 6, out_shape=[b16, b16, b16, b16, jax.ShapeDtypeStruct((T, D), F32), b16],
        args=[sgu, att, pg, hin, ga.reshape(B, 1, D), wa, wb, wo, norm[0], norm[1].reshape(B, 1, D),
              norm[2].reshape(B, 1, D)], comms=comms)


def ffn_bwd_act(dh, f, ga, wd, s, q, *, seq, tm, name, comms=()):
    T, D = dh.shape
    nq, fs, _ = wd.shape
    B, tps = T // seq, seq // tm

    def body(dh_ref, f_ref, ga_ref, wd_ref, s_ref, q_ref, dg_ref, du_ref, df_ref, dga_ref):
        i = pl.program_id(0)
        d = dh_ref[...]
        df = ((0.5 * ga_ref[0]) * d).astype(BF16)
        df_ref[...] = df
        part = 0.5 * jnp.sum(d * f_ref[...].astype(F32), axis=0, keepdims=True)
        for j in range(nq):
            da = _dg(df, wd_ref[j], NT)
            du_ref[j] = (da * s_ref[j].astype(F32)).astype(BF16)
            dg_ref[j] = (da * q_ref[j].astype(F32)).astype(BF16)

        @pl.when(i % tps == 0)
        def _():
            dga_ref[0] = part

        @pl.when(i % tps != 0)
        def _():
            dga_ref[0] += part

    tile = pl.BlockSpec((tm, D), lambda i: (i, 0))
    per_seq = pl.BlockSpec((1, 1, D), lambda i: (i // tps, 0, 0))
    act = pl.BlockSpec((nq, tm, fs), lambda i: (0, i, 0))
    o_shape = jax.ShapeDtypeStruct((nq, T, fs), BF16)
    dg, du, df, dga = _call(
        body, name=name, grid=(T // tm,), in_specs=[tile, tile, per_seq, _resident(wd), act, act],
        out_specs=[act, act, tile, per_seq],
        out_shape=[o_shape, o_shape, jax.ShapeDtypeStruct((T, D), BF16), jax.ShapeDtypeStruct((B, 1, D), F32)],
        args=[dh, f, ga.reshape(B, 1, D), wd, s, q], comms=comms)
    return dg, du, df, dga.reshape(B, D)


def mm_tn(pairs, *, tt, name, comms=()):
    n = len(pairs)
    flat = []
    for pair in pairs:
        for a in pair:
            if not any(a is b for b in flat):
                flat.append(a)
    where = lambda a: [k for k, b in enumerate(flat) if a is b][0]
    nq = max([a.shape[0] for a in flat if a.ndim == 3] + [1])
    T = flat[0].shape[-2]
    tt = min(tt, T)
    nt = T // tt
    stacked = [x.ndim == 3 or y.ndim == 3 for x, y in pairs]

    def body(*refs):
        in_refs, o_refs, accs = refs[:len(flat)], refs[len(flat):len(flat) + n], refs[len(flat) + n:]
        t = pl.program_id(1)
        value = lambda a: in_refs[where(a)][0] if a.ndim == 3 else in_refs[where(a)][...]

        def put(j, v):
            if stacked[j]:
                o_refs[j][0] = v.astype(BF16)
            else:
                o_refs[j][...] = v.astype(BF16)

        for j, (x, y) in enumerate(pairs):
            part = _dg(value(x), value(y), TN)
            if nt == 1:
                put(j, part)
                continue

            @pl.when(t == 0)
            def _():
                accs[j][...] = part

            @pl.when(t != 0)
            def _():
                accs[j][...] += part

        if nt > 1:
            @pl.when(t == nt - 1)
            def _():
                for j in range(n):
                    put(j, accs[j][...])

    def spec(a):
        if a.ndim == 3:
            return pl.BlockSpec((1, tt, a.shape[2]), lambda q, t: (q, t, 0))
        return pl.BlockSpec((tt, a.shape[1]), lambda q, t: (t, 0))

    out_specs, out_shape = [], []
    for j, (x, y) in enumerate(pairs):
        K, N = x.shape[-1], y.shape[-1]
        if stacked[j]:
            out_specs.append(pl.BlockSpec((1, K, N), lambda q, t: (q, 0, 0)))
            out_shape.append(jax.ShapeDtypeStruct((nq, K, N), BF16))
        else:
            out_specs.append(pl.BlockSpec((K, N), lambda q, t: (0, 0)))
            out_shape.append(jax.ShapeDtypeStruct((K, N), BF16))
    return _call(
        body, name=name, grid=(nq, nt), in_specs=[spec(a) for a in flat],
        out_specs=out_specs, out_shape=out_shape,
        scratch_shapes=[pltpu.VMEM((x.shape[-1], y.shape[-1]), F32) for x, y in pairs] if nt > 1 else [],
        args=flat, comms=comms)


def dx_norm_bwd(dys, ws, hin, dh_out, g, sc, *, seq, tm, name, comms=()):
    T, D = hin.shape
    B, tps = T // seq, seq // tm
    n = len(dys)

    def body(*refs):
        dy_refs, w_refs = refs[:n], refs[n:2 * n]
        h_ref, dho_ref, g_ref, sc_ref, dhi_ref, dsh_ref, dsc_ref, dgn_ref = refs[2 * n:]
        i = pl.program_id(0)
        dxn = None
        for j in range(n):
            if dys[j].ndim == 3:
                for q in range(dys[j].shape[0]):
                    part = _dot(dy_refs[j][q], w_refs[j][q])
                    dxn = part if dxn is None else dxn + part
            else:
                part = _dot(dy_refs[j][...], w_refs[j][...])
                dxn = part if dxn is None else dxn + part
        x = h_ref[...]
        nx, r = _rms_parts(x)
        gain = g_ref[...]
        one_sc = 1.0 + sc_ref[0]
        dsh = jnp.sum(dxn, axis=0, keepdims=True)
        dsc = jnp.sum(dxn * (nx * gain), axis=0, keepdims=True)
        dgn = jnp.sum(dxn * one_sc * nx, axis=0, keepdims=True)
        dn = dxn * one_sc * gain
        dhi_ref[...] = dho_ref[...] + r * (dn - nx * jnp.mean(dn * nx, axis=-1, keepdims=True))

        @pl.when(i % tps == 0)
        def _():
            dsh_ref[0] = dsh
            dsc_ref[0] = dsc

        @pl.when(i % tps != 0)
        def _():
            dsh_ref[0] += dsh
            dsc_ref[0] += dsc

        @pl.when(i == 0)
        def _():
            dgn_ref[...] = dgn

        @pl.when(i != 0)
        def _():
            dgn_ref[...] += dgn

    def dy_spec(a):
        if a.ndim == 3:
            return pl.BlockSpec((a.shape[0], tm, a.shape[2]), lambda i: (0, i, 0))
        return pl.BlockSpec((tm, a.shape[1]), lambda i: (i, 0))

    tile = pl.BlockSpec((tm, D), lambda i: (i, 0))
    per_seq = pl.BlockSpec((1, 1, D), lambda i: (i // tps, 0, 0))
    row = pl.BlockSpec((1, D), lambda i: (0, 0))
    dhi, dsh, dsc, dgn = _call(
        body, name=name, grid=(T // tm,),
        in_specs=[dy_spec(a) for a in dys] + [_resident(w) for w in ws] + [tile, tile, row, per_seq],
        out_specs=[tile, per_seq, per_seq, row],
        out_shape=[jax.ShapeDtypeStruct((T, D), F32), jax.ShapeDtypeStruct((B, 1, D), F32),
                   jax.ShapeDtypeStruct((B, 1, D), F32), jax.ShapeDtypeStruct((1, D), F32)],
        args=[*dys, *ws, hin, dh_out, g, sc.reshape(B, 1, D)], comms=comms)
    return dhi, dsh.reshape(B, D), dsc.reshape(B, D), dgn


def mixer_out_bwd(dh, m, ga, ya, yb, pg, wa, wb, wo, *, seq, tm, name, comms=()):
    T, D = dh.shape
    B, tps = T // seq, seq // tm

    def body(dh_ref, m_ref, ga_ref, ya_ref, yb_ref, pg_ref, wa_ref, wb_ref, wo_ref,
             dg_ref, dya_ref, dyb_ref, ds_ref, dt_ref, dm_ref, dga_ref):
        i = pl.program_id(0)
        d = dh_ref[...]
        dm = (ga_ref[0] * d).astype(BF16)
        dm_ref[...] = dm
        part = jnp.sum(d * m_ref[...].astype(F32), axis=0, keepdims=True)

        @pl.when(i % tps == 0)
        def _():
            dga_ref[0] = part

        @pl.when(i % tps != 0)
        def _():
            dga_ref[0] += part

        dmg = _dg(dm, wo_ref[...], NT)
        sa = jax.nn.sigmoid(pg_ref[:, 0:D].astype(F32))
        sb = jax.nn.sigmoid(pg_ref[:, D:2 * D].astype(F32))
        dg_ref[:, 0:D] = (dmg * ya_ref[...].astype(F32) * (sa * (1.0 - sa))).astype(BF16)
        dg_ref[:, D:2 * D] = (dmg * yb_ref[...].astype(F32) * (sb * (1.0 - sb))).astype(BF16)
        dya = (dmg * sa).astype(BF16)
        dyb = (dmg * sb).astype(BF16)
        dya_ref[...] = dya
        dyb_ref[...] = dyb
        ds_ref[...] = _dg(dya, wa_ref[...], NT).astype(BF16)
        dt_ref[...] = _dg(dyb, wb_ref[...], NT).astype(BF16)

    tile = lambda w: pl.BlockSpec((tm, w), lambda i: (i, 0))
    per_seq = pl.BlockSpec((1, 1, D), lambda i: (i // tps, 0, 0))
    dg, dya, dyb, ds, dt, dm, dga = _call(
        body, name=name, grid=(T // tm,),
        in_specs=[tile(D), tile(D), per_seq, tile(D), tile(D), tile(2 * D), _resident(wa), _resident(wb), _resident(wo)],
        out_specs=[tile(2 * D), tile(D), tile(D), tile(D_A), tile(D_B), tile(D), per_seq],
        out_shape=[jax.ShapeDtypeStruct((T, 2 * D), BF16), jax.ShapeDtypeStruct((T, D), BF16),
                   jax.ShapeDtypeStruct((T, D), BF16), jax.ShapeDtypeStruct((T, D_A), BF16),
                   jax.ShapeDtypeStruct((T, D_B), BF16), jax.ShapeDtypeStruct((T, D), BF16),
                   jax.ShapeDtypeStruct((B, 1, D), F32)],
        args=[dh, m, ga.reshape(B, 1, D), ya, yb, pg, wa, wb, wo], comms=comms)
    return dg, dya, dyb, ds, dt, dm, dga.reshape(B, D)


def sgu_bwd(puv, dsgu, gln, bln, ws, bs_t, *, cpb, name, comms=()):
    T = puv.shape[0]
    gd = D_A // N_GROUPS
    tm = cpb * CHUNK

    def body(p_ref, ds_ref, gln_ref, bln_ref, ws_ref, bs_ref, d_ref, dws_ref, dz_ref, dgl_ref, dbl_ref):
        i = pl.program_id(0)
        causal = _causal()
        wf = [jnp.where(causal, ws_ref[g], 0.0) for g in range(N_GROUPS)]
        wm = [w.astype(BF16) for w in wf]
        wt = [w.T.astype(BF16) for w in wf]
        dws = [jnp.zeros((CHUNK, CHUNK), F32) for _ in range(N_GROUPS)]
        dzs = jnp.zeros((CHUNK, D_A), F32)
        dgl = jnp.zeros((1, D_A), F32)
        dbl = jnp.zeros((1, D_A), F32)
        for j in range(cpb):
            rows = slice(j * CHUNK, (j + 1) * CHUNK)
            au = p_ref[rows, 0:D_A].astype(F32)
            av = p_ref[rows, D_A:2 * D_A].astype(F32)
            u, tu = _gelu_parts(au)
            vv, tv = _gelu_parts(av)
            vhat, rstd = _layer_norm_parts(vv)
            vn = (vhat * gln_ref[...] + bln_ref[...]).astype(BF16)
            dsg = ds_ref[rows, :].astype(F32)
            dz = dsg * u
            dzb = dz.astype(BF16)
            zs, dvns = [], []
            for g in range(N_GROUPS):
                cols = slice(g * gd, (g + 1) * gd)
                zs.append(_dot(wm[g], vn[:, cols]) + bs_ref[:, g:g + 1])
                dws[g] = dws[g] + _dg(dzb[:, cols], vn[:, cols], NT)
                dvns.append(_dot(wt[g], dzb[:, cols]))
            z = jnp.concatenate(zs, axis=1)
            dvn = jnp.concatenate(dvns, axis=1)
            d_ref[rows, 0:D_A] = (dsg * z * _dgelu(au, tu)).astype(BF16)
            dvh = dvn * gln_ref[...]
            dvv = rstd * (dvh - jnp.mean(dvh, axis=-1, keepdims=True)
                          - vhat * jnp.mean(dvh * vhat, axis=-1, keepdims=True))
            d_ref[rows, D_A:2 * D_A] = (dvv * _dgelu(av, tv)).astype(BF16)
            dzs = dzs + dz
            dgl = dgl + jnp.sum(dvn * vhat, axis=0, keepdims=True)
            dbl = dbl + jnp.sum(dvn, axis=0, keepdims=True)

        @pl.when(i == 0)
        def _():
            for g in range(N_GROUPS):
                dws_ref[g] = jnp.where(causal, dws[g], 0.0)
            dz_ref[...] = dzs
            dgl_ref[...] = dgl
            dbl_ref[...] = dbl

        @pl.when(i != 0)
        def _():
            for g in range(N_GROUPS):
                dws_ref[g] += jnp.where(causal, dws[g], 0.0)
            dz_ref[...] += dzs
            dgl_ref[...] += dgl
            dbl_ref[...] += dbl

    full = lambda a: pl.BlockSpec(a.shape, lambda i: (0,) * a.ndim)
    acc = lambda s: pl.BlockSpec(s, lambda i: (0,) * len(s))
    return _call(
        body, name=name, grid=(T // tm,),
        in_specs=[pl.BlockSpec((tm, 2 * D_A), lambda i: (i, 0)), pl.BlockSpec((tm, D_A), lambda i: (i, 0)),
                  full(gln), full(bln), full(ws), full(bs_t)],
        out_specs=[pl.BlockSpec((tm, 2 * D_A), lambda i: (i, 0)), acc((N_GROUPS, CHUNK, CHUNK)), acc((CHUNK, D_A)),
                   acc((1, D_A)), acc((1, D_A))],
        out_shape=[jax.ShapeDtypeStruct((T, 2 * D_A), BF16), jax.ShapeDtypeStruct((N_GROUPS, CHUNK, CHUNK), F32),
                   jax.ShapeDtypeStruct((CHUNK, D_A), F32), jax.ShapeDtypeStruct((1, D_A), F32),
                   jax.ShapeDtypeStruct((1, D_A), F32)],
        args=[puv, dsgu, gln, bln, ws, bs_t], comms=comms)


def attn_bwd(pqkv, datt, gq_t, gk_t, sinks, *, seq, name, comms=()):
    T = pqkv.shape[0]
    nb = seq // CHUNK

    def body(p_ref, do_ref, gq_ref, gk_ref, sink_ref, d_ref, dgq_ref, dgk_ref, dsk_ref,
             k0, k1, v0, v1, dk0, dk1, dv0, dv1, dgq_s, dsk_s, bias_ref, s_ref, dp_ref, pr_ref, ds_ref):
        b = pl.program_id(0)
        krep, vrep, dkacc, dvacc = [k0, k1], [v0, v1], [dk0, dk1], [dv0, dv1]
        _fill_keys(p_ref, gk_ref, krep, vrep, seq)
        _fill_mask_bias(bias_ref)
        for h in range(N_KV):
            dkacc[h][...] = jnp.zeros_like(dkacc[h])
            dvacc[h][...] = jnp.zeros_like(dvacc[h])
        dgq_s[...] = jnp.zeros_like(dgq_s)
        dsk_s[...] = jnp.zeros_like(dsk_s)
        mean_q = _head_mean_matrix(D_B)
        lane = _iota((1, 128), 1)

        def block(i, carry):
            r0 = pl.multiple_of(i * CHUNK, CHUNK)
            first = jnp.minimum(i, 1)
            nq, rq = _head_rms(p_ref[pl.ds(r0, CHUNK), 0:D_B].astype(F32), mean_q)
            qn = nq * (gq_ref[...] * ATT_SCALE)
            dqn_parts = []
            for h in range(N_KV):
                cols = slice(h * KV_W, (h + 1) * KV_W)
                qs = _stack_heads(qn[:, cols])
                kk = krep[h][pl.ds(r0, 2 * CHUNK), :]
                dos = _stack_heads(do_ref[pl.ds(r0, CHUNK), cols].astype(F32))
                s_ref[...] = _dg(qs, kk, NT)
                dp_ref[...] = _dg(dos, vrep[h][pl.ds(r0, 2 * CHUNK), :], NT)

                def rows_of(c, carry2):
                    rows, probs, psink = _softmax_rows(s_ref, bias_ref, first, sink_ref, h, c)
                    dp = dp_ref[rows, :]
                    dr = jnp.sum(probs * dp, axis=-1, keepdims=True)
                    pr_ref[rows, :] = probs.astype(BF16)
                    ds_ref[rows, :] = (probs * (dp - dr)).astype(BF16)
                    head = h * Q_PER_KV + c // (CHUNK // SOFTMAX_ROWS)
                    dsk_s[...] += jnp.where(lane == head, -jnp.sum(psink * dr), 0.0)
                    return carry2

                lax.fori_loop(0, STACK // SOFTMAX_ROWS, rows_of, 0, unroll=True)
                dqn_parts.append(_unstack_heads(_dot(ds_ref[...], kk)))
                dkacc[h][pl.ds(r0, 2 * CHUNK), :] += _dg(ds_ref[...], qs, TN)
                dvacc[h][pl.ds(r0, 2 * CHUNK), :] += _dg(pr_ref[...], dos, TN)
            dqn = jnp.concatenate(dqn_parts, axis=1) * ATT_SCALE
            dn = dqn * gq_ref[...]
            d_ref[pl.ds(r0, CHUNK), 0:D_B] = (rq * (dn - nq * _dot_split(dn * nq, mean_q))).astype(BF16)
            dgq_s[...] += jnp.sum(dqn * nq, axis=0, keepdims=True)
            return carry

        lax.fori_loop(0, nb, block, 0)

        mean_k = _head_mean_matrix(KV2)
        folds = [_fold_matrix(h) for h in range(N_KV)]
        rows = min(seq, 4 * CHUNK)

        def piece(j, dgk):
            r0 = pl.multiple_of(j * rows, CHUNK)
            r1 = pl.multiple_of(r0 + CHUNK, CHUNK)
            dkn = _dot_split(dkacc[0][pl.ds(r1, rows), :], folds[0]) + _dot_split(dkacc[1][pl.ds(r1, rows), :], folds[1])
            dv = _dot_split(dvacc[0][pl.ds(r1, rows), :], folds[0]) + _dot_split(dvacc[1][pl.ds(r1, rows), :], folds[1])
            nk, rk = _head_rms(p_ref[pl.ds(r0, rows), K_OFF:K_OFF + KV2].astype(F32), mean_k)
            dn = dkn * gk_ref[...]
            d_ref[pl.ds(r0, rows), K_OFF:K_OFF + KV2] = (rk * (dn - nk * _dot_split(dn * nk, mean_k))).astype(BF16)
            d_ref[pl.ds(r0, rows), V_OFF:V_OFF + KV2] = dv.astype(BF16)
            return dgk + jnp.sum(dkn * nk, axis=0, keepdims=True)

        dgk = lax.fori_loop(0, seq // rows, piece, jnp.zeros((1, KV2), F32))

        @pl.when(b == 0)
        def _():
            dgq_ref[...] = dgq_s[...]
            dgk_ref[...] = dgk
            dsk_ref[...] = jnp.broadcast_to(dsk_s[...], dsk_ref.shape)

        @pl.when(b != 0)
        def _():
            dgq_ref[...] += dgq_s[...]
            dgk_ref[...] += dgk
            dsk_ref[...] += jnp.broadcast_to(dsk_s[...], dsk_ref.shape)

    acc = lambda s: pl.BlockSpec(s, lambda b: (0, 0))
    return _call(
        body, name=name, grid=(T // seq,),
        in_specs=[pl.BlockSpec((seq, QKV_W), lambda b: (b, 0)), pl.BlockSpec((seq, D_B), lambda b: (b, 0)),
                  pl.BlockSpec(gq_t.shape, lambda b: (0, 0)), pl.BlockSpec(gk_t.shape, lambda b: (0, 0)),
                  pl.BlockSpec(memory_space=pltpu.SMEM)],
        out_specs=[pl.BlockSpec((seq, QKV_W), lambda b: (b, 0)), acc((1, D_B)), acc((1, KV2)), acc((8, 128))],
        out_shape=[jax.ShapeDtypeStruct((T, QKV_W), BF16), jax.ShapeDtypeStruct((1, D_B), F32),
                   jax.ShapeDtypeStruct((1, KV2), F32), jax.ShapeDtypeStruct((8, 128), F32)],
        scratch_shapes=[pltpu.VMEM((seq + CHUNK, KV_W), BF16)] * 4 + [pltpu.VMEM((seq + CHUNK, KV_W), F32)] * 4
        + [pltpu.VMEM((1, D_B), F32), pltpu.VMEM((1, 128), F32), pltpu.VMEM((2, CHUNK, 2 * CHUNK), F32)]
        + [pltpu.VMEM((STACK, 2 * CHUNK), F32)] * 2 + [pltpu.VMEM((STACK, 2 * CHUNK), BF16)] * 2,
        args=[pqkv, datt, gq_t, gk_t, sinks], comms=comms)


def ada_fwd(c_all, w, b, *, name):
    nb, D = c_all.shape
    N = w.shape[1]
    tn = N // 3

    def body(c_ref, w_ref, b_ref, o_ref):
        c = c_ref[...]
        cond = (c * jax.nn.sigmoid(c)).astype(BF16)
        o_ref[...] = _dot(cond, w_ref[...].astype(BF16)) + b_ref[...]

    return _call(
        body, name=name, grid=(3,),
        in_specs=[pl.BlockSpec((nb, D), lambda j: (0, 0)), pl.BlockSpec((D, tn), lambda j: (0, j)),
                  pl.BlockSpec((1, tn), lambda j: (0, j))],
        out_specs=[pl.BlockSpec((nb, tn), lambda j: (0, j))], out_shape=[jax.ShapeDtypeStruct((nb, N), F32)],
        args=[c_all, w, b])[0]


def ada_bwd(c_all, dmods_all, dmods_mine, gain_rows, *, name, comms=()):
    nb, D = c_all.shape
    NA = dmods_all.shape[1]
    N = dmods_mine.shape[1]
    tn = N // 3

    def body(c_ref, da_ref, dm_ref, gr_ref, db_ref, dw_ref, dgn_ref):
        c = c_ref[...]
        cond = (c * jax.nn.sigmoid(c)).astype(BF16)
        dw_ref[...] = _dg(cond, dm_ref[...].astype(BF16), TN)
        db_ref[...] = jnp.sum(da_ref[...], axis=0, keepdims=True)
        total = gr_ref[0:1, :]
        for d in range(1, N_DEV):
            total = total + gr_ref[d:d + 1, :]
        dgn_ref[...] = total

    return _call(
        body, name=name, grid=(3,),
        in_specs=[pl.BlockSpec((nb, D), lambda j: (0, 0)), pl.BlockSpec((nb, NA), lambda j: (0, 0)),
                  pl.BlockSpec((nb, tn), lambda j: (0, j)), pl.BlockSpec((N_DEV, D), lambda j: (0, 0))],
        out_specs=[pl.BlockSpec((1, NA), lambda j: (0, 0)), pl.BlockSpec((D, tn), lambda j: (0, j)),
                   pl.BlockSpec((1, D), lambda j: (0, 0))],
        out_shape=[jax.ShapeDtypeStruct((1, NA), F32), jax.ShapeDtypeStruct((D, N), F32),
                   jax.ShapeDtypeStruct((1, D), F32)],
        args=[c_all, dmods_all, dmods_mine, gain_rows], comms=comms)


def adamw(items, *, steps, name, comms=()):
    n = len(items)
    c1 = 1.0 / (1.0 - ADAM_B1 ** ADAM_STEP)
    c2 = 1.0 / (1.0 - ADAM_B2 ** ADAM_STEP)

    def body(*refs):
        for j in range(n):
            w_ref, g_ref, m_ref, v_ref = refs[4 * j:4 * j + 4]
            d_ref, mo_ref, vo_ref = refs[4 * n + 3 * j:4 * n + 3 * j + 3]
            gg = g_ref[...]
            mn = ADAM_B1 * m_ref[...] + (1.0 - ADAM_B1) * gg
            vn = ADAM_B2 * v_ref[...] + (1.0 - ADAM_B2) * (gg * gg)
            mo_ref[...] = mn
            vo_ref[...] = vn
            d_ref[...] = -ADAM_LR * ((mn * c1) / (jnp.sqrt(vn * c2) + ADAM_EPS) + ADAM_WD * w_ref[...])

    in_specs, out_specs, out_shape, args = [], [], [], []
    for item in items:
        R, C = item[0].shape
        blk = pl.BlockSpec((R // steps, C), lambda i: (i, 0))
        in_specs += [blk] * 4
        out_specs += [blk] * 3
        out_shape += [jax.ShapeDtypeStruct((R, C), F32)] * 3
        args += list(item)
    res = _call(body, name=name, grid=(steps,), in_specs=in_specs, out_specs=out_specs, out_shape=out_shape, args=args,
                comms=comms)
    return [tuple(res[3 * j:3 * j + 3]) for j in range(n)]


def _place():
    return lax.axis_index("x"), lax.axis_index("y"), lax.axis_index("c")


def _flip(v, bit):
    return 1 - v if bit else v


def _other_chips(mx, my):
    return [(_flip(mx, k & 2), _flip(my, k & 1)) for k in range(1, N_QUAD)]


def _remote(src, dst, send_sem, recv_sem, peer):
    return pltpu.make_async_remote_copy(src_ref=src, dst_ref=dst, send_sem=send_sem, recv_sem=recv_sem,
                                        device_id=peer, device_id_type=MESH)


def all_gather8(x, *, name, reduce=False, comms=()):
    r, n = x.shape

    def body(x_ref, o_ref, *rest):
        if reduce:
            buf, send_sems, recv_sems, lsem = rest
        else:
            buf = o_ref
            send_sems, recv_sems, lsem = rest
        mx, my, mc = _place()
        me = 4 * mx + 2 * my + mc
        mine = pltpu.make_async_copy(x_ref, buf.at[me], lsem)
        mine.start()
        sends, recvs = [], []
        for k in range(1, N_DEV):
            peer = (_flip(mx, k & 4), _flip(my, k & 2), _flip(mc, k & 1))
            pidx = 4 * peer[0] + 2 * peer[1] + peer[2]
            sends.append(_remote(x_ref, buf.at[me], send_sems.at[k - 1], recv_sems.at[k - 1], peer))
            recvs.append(_remote(x_ref, buf.at[pidx], send_sems.at[k - 1], recv_sems.at[k - 1], peer))
        for cp in sends:
            cp.start()
        for cp in recvs:
            cp.wait_recv()
        for cp in sends:
            cp.wait_send()
        mine.wait()
        if reduce:
            total = buf[0]
            for d in range(1, N_DEV):
                total = total + buf[d]
            o_ref[...] = total

    scratch = [pltpu.SemaphoreType.DMA((N_DEV - 1,)), pltpu.SemaphoreType.DMA((N_DEV - 1,)), pltpu.SemaphoreType.DMA]
    if reduce:
        scratch = [pltpu.VMEM((N_DEV, r, n), F32)] + scratch
        out_shape = jax.ShapeDtypeStruct((r, n), F32)
    else:
        out_shape = jax.ShapeDtypeStruct((N_DEV, r, n), F32)
    vmem = pl.BlockSpec(memory_space=pltpu.VMEM)
    return _call(body, name=name, grid=(), in_specs=[vmem], out_specs=[vmem], out_shape=[out_shape], args=[x],
                 scratch_shapes=scratch, comms=comms)[0]


def ag8_comm(x):
    def copies(srcs, lands, ss, rs, only_sends=False):
        mx, my, mc = _place()
        me = 4 * mx + 2 * my + mc
        local = pltpu.make_async_copy(srcs[0], lands[0].at[me], ss.at[N_DEV - 1])
        sends, recvs = [], []
        for k in range(1, N_DEV):
            peer = (_flip(mx, k & 4), _flip(my, k & 2), _flip(mc, k & 1))
            sends.append(_remote(srcs[0], lands[0].at[me], ss.at[k - 1], rs.at[k - 1], peer))
            if not only_sends:
                recvs.append(_remote(srcs[0], lands[0].at[4 * peer[0] + 2 * peer[1] + peer[2]], ss.at[k - 1], rs.at[k - 1], peer))
        return local, sends, recvs

    def start(srcs, bufs, lands, ss, rs):
        local, sends, _ = copies(srcs, lands, ss, rs, only_sends=True)
        local.start()
        for cp in sends:
            cp.start()

    def finish(srcs, bufs, lands, ss, rs):
        local, sends, recvs = copies(srcs, lands, ss, rs)
        for cp in recvs:
            cp.wait_recv()
        for cp in sends:
            cp.wait_send()
        local.wait()

    return Comm(srcs=[x], land_shapes=[jax.ShapeDtypeStruct((N_DEV,) + x.shape, x.dtype)], n_sems=N_DEV,
                start=start, finish=finish)


def sum8(stacked, *, name):
    _, r, n = stacked.shape

    def body(s_ref, o_ref):
        total = s_ref[0]
        for d in range(1, N_DEV):
            total = total + s_ref[d]
        o_ref[...] = total

    return _call(body, name=name, grid=(), in_specs=[pl.BlockSpec(memory_space=pltpu.VMEM)],
                 out_specs=[pl.BlockSpec(memory_space=pltpu.VMEM)], out_shape=[jax.ShapeDtypeStruct((r, n), F32)],
                 args=[stacked])[0]


def cast_into_stacks(ws, quad, *, name, comms=()):
    steps = 4

    def body(q_ref, *refs):
        for w_ref, o_ref in zip(refs[:len(ws)], refs[len(ws):]):
            o_ref[0] = w_ref[...].astype(BF16)

    return _call(
        body, name=name, grid=(steps,), prefetch=[quad],
        in_specs=[pl.BlockSpec((w.shape[0] // steps, w.shape[1]), lambda i, q: (i, 0)) for w in ws],
        out_specs=[pl.BlockSpec((1, w.shape[0] // steps, w.shape[1]), lambda i, q: (q[0], i, 0)) for w in ws],
        out_shape=[jax.ShapeDtypeStruct((N_QUAD,) + w.shape, BF16) for w in ws], args=list(ws), comms=comms)


def gather_comm(stacks):
    n = len(stacks)

    def copies(bufs, ss, rs, only_sends=False):
        mx, my, mc = _place()
        q = 2 * mx + my
        sibling = (mx, my, 1 - mc)
        ici_send, ici_recv, fwd_send, fwd_recv = [], [], [], []
        for p in range(n):
            hr = stacks[p].shape[1] // 2
            mine, other = pl.ds(mc * hr, hr), pl.ds((1 - mc) * hr, hr)
            for k, chip in enumerate(_other_chips(mx, my)):
                qk = 2 * chip[0] + chip[1]
                peer = (chip[0], chip[1], mc)
                own, landed, theirs = bufs[p].at[q, mine, :], bufs[p].at[qk, mine, :], bufs[p].at[qk, other, :]
                ici_send.append(_remote(own, own, ss.at[6 * p + k], rs.at[6 * p + k], peer))
                if only_sends:
                    continue
                ici_recv.append(_remote(own, landed, ss.at[6 * p + k], rs.at[6 * p + k], peer))
                fwd_send.append(_remote(landed, landed, ss.at[6 * p + 3 + k], rs.at[6 * p + 3 + k], sibling))
                fwd_recv.append(_remote(theirs, theirs, ss.at[6 * p + 3 + k], rs.at[6 * p + 3 + k], sibling))
        return ici_send, ici_recv, fwd_send, fwd_recv

    def start(srcs, bufs, lands, ss, rs):
        for cp in copies(bufs, ss, rs, only_sends=True)[0]:
            cp.start()

    def finish(srcs, bufs, lands, ss, rs):
        ici_send, ici_recv, fwd_send, fwd_recv = copies(bufs, ss, rs)
        for arrived, onward in zip(ici_recv, fwd_send):
            arrived.wait_recv()
            onward.start()
        for cp in fwd_recv:
            cp.wait_recv()
        for cp in ici_send + fwd_send:
            cp.wait_send()

    return Comm(bufs=stacks, n_sems=6 * n, start=start, finish=finish)


def rs_pair_comm(gs):
    n = len(gs)

    def copies(srcs, lands, ss, rs):
        mx, my, mc = _place()
        out = []
        for p in range(n):
            hr = gs[p].shape[1] // 2
            out.append(_remote(srcs[p].at[:, pl.ds((1 - mc) * hr, hr), :], lands[p], ss.at[p], rs.at[p], (mx, my, 1 - mc)))
        return out

    def start(srcs, bufs, lands, ss, rs):
        for cp in copies(srcs, lands, ss, rs):
            cp.start()

    def finish(srcs, bufs, lands, ss, rs):
        for cp in copies(srcs, lands, ss, rs):
            cp.wait()

    return Comm(srcs=gs, land_shapes=[jax.ShapeDtypeStruct((g.shape[0], g.shape[1] // 2, g.shape[2]), g.dtype) for g in gs],
                n_sems=n, start=start, finish=finish)


def rs_chip_comm(ss_):
    n = len(ss_)

    def copies(srcs, lands, ss, rs):
        mx, my, mc = _place()
        out = []
        for p in range(n):
            for k, chip in enumerate(_other_chips(mx, my)):
                out.append(_remote(srcs[p].at[2 * chip[0] + chip[1]], lands[p].at[k], ss.at[3 * p + k], rs.at[3 * p + k],
                                   (chip[0], chip[1], mc)))
        return out

    def start(srcs, bufs, lands, ss, rs):
        for cp in copies(srcs, lands, ss, rs):
            cp.start()

    def finish(srcs, bufs, lands, ss, rs):
        for cp in copies(srcs, lands, ss, rs):
            cp.wait()

    return Comm(srcs=ss_, land_shapes=[jax.ShapeDtypeStruct((N_QUAD - 1,) + s.shape[1:], s.dtype) for s in ss_],
                n_sems=3 * n, start=start, finish=finish)


def rs_share_comm(fulls):
    n = len(fulls)

    def copies(bufs, ss, rs, only_sends=False):
        mx, my, mc = _place()
        send, recv = [], []
        for p in range(n):
            hr = fulls[p].shape[0] // 2
            mine, other = bufs[p].at[pl.ds(mc * hr, hr), :], bufs[p].at[pl.ds((1 - mc) * hr, hr), :]
            send.append(_remote(mine, mine, ss.at[p], rs.at[p], (mx, my, 1 - mc)))
            if not only_sends:
                recv.append(_remote(other, other, ss.at[p], rs.at[p], (mx, my, 1 - mc)))
        return send, recv

    def start(srcs, bufs, lands, ss, rs):
        for cp in copies(bufs, ss, rs, only_sends=True)[0]:
            cp.start()

    def finish(srcs, bufs, lands, ss, rs):
        send, recv = copies(bufs, ss, rs)
        for cp in recv:
            cp.wait_recv()
        for cp in send:
            cp.wait_send()

    return Comm(bufs=fulls, n_sems=n, start=start, finish=finish)


def pair_sum(g, recv, mc, *, name):
    nq, R, C = g.shape
    hr = R // 2
    rb = _row_block(hr, 512)
    nb = hr // rb

    def body(s_ref, g_ref, r_ref, o_ref):
        o_ref[...] = (g_ref[...].astype(F32) + r_ref[...].astype(F32)).astype(BF16)

    return pl.pallas_call(
        body, name=name, out_shape=jax.ShapeDtypeStruct((nq, hr, C), BF16),
        grid_spec=pltpu.PrefetchScalarGridSpec(
            num_scalar_prefetch=1, grid=(nq, nb),
            in_specs=[pl.BlockSpec((1, rb, C), lambda q, i, s: (q, s[0] * nb + i, 0)),
                      pl.BlockSpec((1, rb, C), lambda q, i, s: (q, i, 0))],
            out_specs=pl.BlockSpec((1, rb, C), lambda q, i, s: (q, i, 0))),
        compiler_params=pltpu.CompilerParams(dimension_semantics=("arbitrary", "arbitrary"),
                                             vmem_limit_bytes=VMEM_LIMIT_BYTES),
    )(mc, g, recv)


def chip_sum(s, recv, quad_mc, *, name):
    nq, hr, C = s.shape
    rb = _row_block(hr, 256)
    nb = hr // rb

    def body(q_ref, s_ref, r_ref, o_ref):
        total = s_ref[0].astype(F32)
        for k in range(N_QUAD - 1):
            total = total + r_ref[k].astype(F32)
        o_ref[...] = total

    return pl.pallas_call(
        body, name=name, out_shape=jax.ShapeDtypeStruct((2 * hr, C), F32),
        grid_spec=pltpu.PrefetchScalarGridSpec(
            num_scalar_prefetch=1, grid=(nb,),
            in_specs=[pl.BlockSpec((1, rb, C), lambda i, q: (q[0], i, 0)),
                      pl.BlockSpec((N_QUAD - 1, rb, C), lambda i, q: (0, i, 0))],
            out_specs=pl.BlockSpec((rb, C), lambda i, q: (q[1] * nb + i, 0))),
        compiler_params=pltpu.CompilerParams(dimension_semantics=("arbitrary",), vmem_limit_bytes=VMEM_LIMIT_BYTES),
    )(quad_mc, s, recv)


def _stack_cols(w_full):
    K, N = w_full.shape
    return w_full.reshape(K, N_QUAD, N // N_QUAD).transpose(1, 0, 2)


def _unstack_cols(w4):
    nq, K, n = w4.shape
    return w4.transpose(1, 0, 2).reshape(K, nq * n)


def kernel(x, c, w_ada, b_ada, g_norm1, ffn1_w_gate, ffn1_w_up, ffn1_w_down, g_norm2, w_in, g_sgu_ln, b_sgu_ln, w_spatial, b_spatial, g_q, g_k, attn_sinks, w_branch_a, w_branch_b, w_out, g_norm3, ffn2_w_gate, ffn2_w_up, ffn2_w_down, loss_target, m_w_ada, m_b_ada, m_g_norm1, m_ffn1_w_gate, m_ffn1_w_up, m_ffn1_w_down, m_g_norm2, m_w_in, m_g_sgu_ln, m_b_sgu_ln, m_w_spatial, m_b_spatial, m_g_q, m_g_k, m_attn_sinks, m_w_branch_a, m_w_branch_b, m_w_out, m_g_norm3, m_ffn2_w_gate, m_ffn2_w_up, m_ffn2_w_down, v_w_ada, v_b_ada, v_g_norm1, v_ffn1_w_gate, v_ffn1_w_up, v_ffn1_w_down, v_g_norm2, v_w_in, v_g_sgu_ln, v_b_sgu_ln, v_w_spatial, v_b_spatial, v_g_q, v_g_k, v_attn_sinks, v_w_branch_a, v_w_branch_b, v_w_out, v_g_norm3, v_ffn2_w_gate, v_ffn2_w_up, v_ffn2_w_down):
    B, S, D = x.shape
    T = B * S
    n_mod = b_ada.shape[1] // D
    mx, my, mc = _place()
    quad = 2 * mx + my
    mc_arr = jnp.reshape(mc, (1,)).astype(jnp.int32)
    quad_arr = jnp.reshape(quad, (1,)).astype(jnp.int32)
    quad_mc = jnp.stack([quad, mc]).astype(jnp.int32)
    tm = min(512, S)
    tm_small = min(256, S)
    tm_big = min(1024, S)
    tt_half = min(2048, T)
    cpb = min(4, S // CHUNK)

    xt = x.reshape(T, D)
    tgt = loss_target.reshape(T, D)

    tr = lambda a: jnp.swapaxes(a, 1, 2)
    stack_wg1, stack_wu1 = cast_into_stacks([tr(ffn1_w_gate)[0], tr(ffn1_w_up)[0]], quad_arr, name="stack_gu1")
    gather_wg1, gather_wu1 = gather_comm([stack_wg1]), gather_comm([stack_wu1])
    later = [ffn1_w_down, tr(w_in), w_branch_a, w_branch_b, w_out, tr(ffn2_w_gate), tr(ffn2_w_up), ffn2_w_down]
    stacks = cast_into_stacks([w[0] for w in later], quad_arr, name="stack_rest", comms=[gather_wg1])
    (wg1,) = gather_wg1.bufs_out
    gather_wd1, gather_win = gather_comm([stacks[0]]), gather_comm([stacks[1]])
    gather_abo = gather_comm(stacks[2:5])
    gather_wg3, gather_wu3, gather_wd3 = gather_comm([stacks[5]]), gather_comm([stacks[6]]), gather_comm([stacks[7]])

    c_all = all_gather8(c, name="gather_cond").reshape(N_DEV * B, D)
    n_ada = w_ada.shape[2]
    b_mine = lax.dynamic_slice(b_ada, (0, quad * n_ada), (1, n_ada))
    mods_part = ada_fwd(c_all, w_ada[0], b_mine, name="ada_fwd")
    mods_parts = all_gather8(mods_part, name="gather_mods", comms=[gather_wu1])
    (wu1,) = gather_wu1.bufs_out
    mods = jnp.concatenate([mods_parts[2 * j] for j in range(N_QUAD)], axis=1)
    me = 4 * mx + 2 * my + mc
    mods = lax.dynamic_slice(mods, (me * B, 0), (B, n_mod * D))
    sh1, sc1, ga1, sh2, sc2, ga2, sh3, sc3, ga3 = [mods[:, j * D:(j + 1) * D] for j in range(n_mod)]
    bs_t = b_spatial[0].T
    ws = w_spatial[0]

    xn1 = norm_mod_fwd(xt, g_norm1, sc1, sh1, seq=S, tm=tm, name="norm1")
    g1, u1, a1 = ffn_up(xn1, wg1, wu1, tm=tm_big, name="ffn1_up", comms=[gather_wd1, gather_abo])
    (wd1,), (wa4, wb4, wo4) = gather_wd1.bufs_out, gather_abo.bufs_out
    wa, wb = _unstack_cols(wa4), _unstack_cols(wb4)
    wo = wo4.reshape(D, D)
    h1, f1, xn2 = ffn_down(a1, wd1, xt, ga1, norm=(g_norm2, sc2, sh2), seq=S, tm=tm, name="ffn1_down",
                           comms=[gather_win])
    (win4,) = gather_win.bufs_out
    win = win4.reshape(-1, D)
    w_uv, w_qkv, w_gt = win[0:2 * D_A], win[2 * D_A:2 * D_A + QKV_W], win[2 * D_A + QKV_W:]
    puv, pqkv, pgt = proj_fwd(xn2, [w_uv, w_qkv, w_gt], tm=tm_small, name="proj", comms=[gather_wg3])
    (wg3,) = gather_wg3.bufs_out
    sgu = sgu_fwd(puv, g_sgu_ln, b_sgu_ln, ws, bs_t, cpb=cpb, name="sgu_fwd")
    gq_t, gk_t = jnp.tile(g_q, (1, N_Q)), jnp.tile(g_k, (1, N_KV))
    att = attn_fwd(pqkv, gq_t, gk_t, attn_sinks, seq=S, name="attn_fwd", comms=[gather_wu3])
    (wu3,) = gather_wu3.bufs_out
    ya, yb, merged, mm, h2, xn3 = mixer_out_fwd(sgu, att, pgt, h1, ga2, wa, wb, wo, (g_norm3, sc3, sh3), seq=S,
                                                tm=tm_small, name="mixer_out", comms=[gather_wd3])
    (wd3,) = gather_wd3.bufs_out
    g3, u3, a3 = ffn_up(xn3, wg3, wu3, tm=tm_big, name="ffn2_up")
    dy, f3, lparts = ffn_down(a3, wd3, h2, ga3, target=tgt, seq=S, tm=tm, name="ffn2_down_loss")
    loss_part = jnp.sum(lparts[:, 0, 0])

    def sums_of(pair, tag):
        return [pair_sum(g, r, mc_arr, name=f"pair_sum_{tag}_{p}") for p, (g, r) in enumerate(zip(pair.srcs, pair.lands))]

    def halves_of(chip, tag):
        return [chip_sum(s, r, quad_mc, name=f"chip_sum_{tag}_{p}") for p, (s, r) in enumerate(zip(chip.srcs, chip.lands))]

    dg3, du3, df3, dga3 = ffn_bwd_act(dy, f3, ga3, wd3, g3, u3, seq=S, tm=tm, name="ffn2_act_bwd")
    (dwd3,) = mm_tn([(a3, df3)], tt=T, name="ffn2_dwd")
    pair_wd3 = rs_pair_comm([dwd3])
    dwg3, dwu3 = mm_tn([(dg3, xn3), (du3, xn3)], tt=tt_half, name="ffn2_dwgu", comms=[pair_wd3])
    chip_wd3 = rs_chip_comm(sums_of(pair_wd3, "wd3"))
    pair_gu3 = rs_pair_comm([dwg3, dwu3])
    dh2, dsh3, dsc3, dgn3 = dx_norm_bwd([dg3, du3], [wg3, wu3], h2, dy, g_norm3, sc3, seq=S, tm=tm, name="ffn2_dx",
                                        comms=[chip_wd3, pair_gu3])
    sum_wg3, sum_wu3 = sums_of(pair_gu3, "gu3")
    chip_wg3, chip_wu3 = rs_chip_comm([sum_wg3]), rs_chip_comm([sum_wu3])

    dgt, dya, dyb, dsgu, datt, dm, dga2 = mixer_out_bwd(dh2, mm, ga2, ya, yb, pgt, wa, wb, wo, seq=S, tm=tm_small,
                                                        name="mixer_out_bwd", comms=[chip_wg3])
    (dwo,) = mm_tn([(merged, dm)], tt=tt_half, name="mixer_dwo")
    (dwa,) = mm_tn([(sgu, dya)], tt=tt_half, name="mixer_dwa")
    (dwb,) = mm_tn([(att, dyb)], tt=tt_half, name="mixer_dwb")
    pair_abo = rs_pair_comm([_stack_cols(dwa), _stack_cols(dwb), dwo.reshape(N_QUAD, D // N_QUAD, D)])
    duv, dws, dzs, dgln, dbln = sgu_bwd(puv, dsgu, g_sgu_ln, b_sgu_ln, ws, bs_t, cpb=cpb, name="sgu_bwd",
                                        comms=[pair_abo])
    chip_abo = rs_chip_comm(sums_of(pair_abo, "abo"))
    dqkv, dgq_h, dgk_h, dsk = attn_bwd(pqkv, datt, gq_t, gk_t, attn_sinks, seq=S, name="attn_bwd",
                                       comms=[chip_wu3, chip_abo])
    dgq = dgq_h.reshape(N_Q, HEAD_DIM).sum(axis=0, keepdims=True)
    dgk = dgk_h.reshape(N_KV, HEAD_DIM).sum(axis=0, keepdims=True)
    share3 = rs_share_comm(halves_of(chip_wg3, "wg3") + halves_of(chip_wu3, "wu3") + halves_of(chip_wd3, "wd3"))
    dwin_uv, dwin_qkv = mm_tn([(duv, xn2), (dqkv, xn2)], tt=tt_half, name="mixer_dwin_a", comms=[share3])
    (dwin_gt,) = mm_tn([(dgt, xn2)], tt=tt_half, name="mixer_dwin_b")
    dwin_parts = [dwin_uv, dwin_qkv, dwin_gt]
    r_wg3, r_wu3, r_wd3 = share3.bufs_out
    pair_win = rs_pair_comm([jnp.concatenate(dwin_parts, axis=0).reshape(win4.shape)])
    dh1, dsh2, dsc2, dgn2 = dx_norm_bwd([duv, dqkv, dgt], [w_uv, w_qkv, w_gt], h1, dh2, g_norm2, sc2, seq=S,
                                        tm=tm, name="mixer_dx", comms=[pair_win])
    chip_win = rs_chip_comm(sums_of(pair_win, "win"))

    dg1, du1, df1, dga1 = ffn_bwd_act(dh1, f1, ga1, wd1, g1, u1, seq=S, tm=tm, name="ffn1_act_bwd", comms=[chip_win])
    share_mix = rs_share_comm(halves_of(chip_win, "win") + halves_of(chip_abo, "abo"))

    db_s = dzs.reshape(CHUNK, N_GROUPS, D_A // N_GROUPS).sum(axis=2).T.reshape(1, N_GROUPS * CHUNK)
    tail = jnp.concatenate([db_s, dgq, dgk, dsk[0:1, 0:N_Q], loss_part.reshape(1, 1)], axis=1)
    tail = jnp.pad(tail, ((0, 0), (0, (-tail.shape[1]) % D)))
    lnrow = jnp.concatenate([dgln, dbln], axis=1)
    lnrow = jnp.pad(lnrow, ((0, 0), (0, (-lnrow.shape[1]) % D)))
    packed = jnp.concatenate([dgn2, dgn3, lnrow.reshape(-1, D), tail.reshape(-1, D), dws.reshape(-1, D)], axis=0)
    n_rows = packed.shape[0]
    packed = jnp.pad(packed, ((0, (-n_rows) % 8), (0, 0)))
    gather_small = ag8_comm(packed)

    dwg1, dwu1 = mm_tn([(dg1, xn1), (du1, xn1)], tt=tt_half, name="ffn1_dwgu", comms=[share_mix, gather_small])
    r_win, r_wa, r_wb, r_wo = share_mix.bufs_out
    small = sum8(gather_small.lands[0], name="sum_small")
    pair_gu1 = rs_pair_comm([dwg1, dwu1])
    (dwd1,) = mm_tn([(a1, df1)], tt=T, name="ffn1_dwd", comms=[pair_gu1])
    chip_gu1 = rs_chip_comm(sums_of(pair_gu1, "gu1"))
    pair_wd1 = rs_pair_comm([dwd1])
    dx, dsh1, dsc1, dgn1 = dx_norm_bwd([dg1, du1], [wg1, wu1], xt, dh1, g_norm1, sc1, seq=S, tm=tm, name="ffn1_dx",
                                       comms=[chip_gu1, pair_wd1])
    chip_wd1 = rs_chip_comm(sums_of(pair_wd1, "wd1"))
    share_gu1 = rs_share_comm(halves_of(chip_gu1, "gu1"))

    names = ["w_ada", "b_ada", "g_norm1", "ffn1_w_gate", "ffn1_w_up", "ffn1_w_down", "g_norm2", "w_in", "g_sgu_ln",
             "b_sgu_ln", "w_spatial", "b_spatial", "g_q", "g_k", "attn_sinks", "w_branch_a", "w_branch_b", "w_out",
             "g_norm3", "ffn2_w_gate", "ffn2_w_up", "ffn2_w_down"]
    weights = dict(zip(names, [w_ada, b_ada, g_norm1, ffn1_w_gate, ffn1_w_up, ffn1_w_down, g_norm2, w_in, g_sgu_ln,
                               b_sgu_ln, w_spatial, b_spatial, g_q, g_k, attn_sinks, w_branch_a, w_branch_b, w_out,
                               g_norm3, ffn2_w_gate, ffn2_w_up, ffn2_w_down]))
    m_in = dict(zip(names, [m_w_ada, m_b_ada, m_g_norm1, m_ffn1_w_gate, m_ffn1_w_up, m_ffn1_w_down, m_g_norm2, m_w_in,
                            m_g_sgu_ln, m_b_sgu_ln, m_w_spatial, m_b_spatial, m_g_q, m_g_k, m_attn_sinks, m_w_branch_a,
                            m_w_branch_b, m_w_out, m_g_norm3, m_ffn2_w_gate, m_ffn2_w_up, m_ffn2_w_down]))
    v_in = dict(zip(names, [v_w_ada, v_b_ada, v_g_norm1, v_ffn1_w_gate, v_ffn1_w_up, v_ffn1_w_down, v_g_norm2, v_w_in,
                            v_g_sgu_ln, v_b_sgu_ln, v_w_spatial, v_b_spatial, v_g_q, v_g_k, v_attn_sinks, v_w_branch_a,
                            v_w_branch_b, v_w_out, v_g_norm3, v_ffn2_w_gate, v_ffn2_w_up, v_ffn2_w_down]))
    transposed = ("ffn1_w_gate", "ffn1_w_up", "w_in", "ffn2_w_gate", "ffn2_w_up")
    grads, delta, new_m, new_v = {}, {}, {}, {}

    def update(group, steps, name, comms=()):
        form = lambda nm, a: (tr(a) if nm in transposed else a)[0].reshape(group[nm].shape)
        res = adamw([(form(nm, weights[nm]), g, form(nm, m_in[nm]), form(nm, v_in[nm])) for nm, g in group.items()],
                    steps=steps, name=name, comms=comms)
        back = lambda nm, a: tr(a[None]) if nm in transposed else a.reshape(weights[nm].shape)
        for (nm, g), (d, mn, vn) in zip(group.items(), res):
            grads[nm], delta[nm], new_m[nm], new_v[nm] = back(nm, g), back(nm, d), back(nm, mn), back(nm, vn)

    update({"ffn2_w_gate": r_wg3, "ffn2_w_up": r_wu3, "ffn2_w_down": r_wd3, "w_out": r_wo, "w_branch_a": r_wa,
            "w_branch_b": r_wb}, 8, "adamw_early", comms=[chip_wd1, share_gu1])
    r_wg1, r_wu1 = share_gu1.bufs_out

    dmods = jnp.concatenate([dsh1, dsc1, dga1, dsh2, dsc2, dga2, dsh3, dsc3, dga3], axis=1)
    dmods = jnp.concatenate([dmods, jnp.pad(dgn1, ((0, 0), (0, (n_mod - 1) * D)))], axis=0)
    gathered = all_gather8(dmods, name="gather_dmods")
    dmods_all = gathered[:, 0:B].reshape(N_DEV * B, n_mod * D)
    dmods_mine = lax.dynamic_slice(dmods_all, (0, quad * n_ada), (N_DEV * B, n_ada))
    share_wd1 = rs_share_comm(halves_of(chip_wd1, "wd1"))
    db_ada, dw_ada, g_gn1 = ada_bwd(c_all, dmods_all, dmods_mine, gathered[:, B, 0:D], name="ada_bwd", comms=[share_wd1])
    (r_wd1,) = share_wd1.bufs_out

    ln_rows = lnrow.size // D
    tail_rows = tail.size // D
    r = 2
    g_gn2, g_gn3 = small[0:1], small[1:2]
    ln_flat = small[r:r + ln_rows].reshape(1, -1)
    r += ln_rows
    tail_flat = small[r:r + tail_rows].reshape(1, -1)
    r += tail_rows
    g_ws = small[r:r + dws.size // D].reshape(w_spatial.shape)
    g_gln, g_bln = ln_flat[:, 0:D_A], ln_flat[:, D_A:2 * D_A]
    o = N_GROUPS * CHUNK
    g_bs = tail_flat[:, 0:o].reshape(b_spatial.shape)
    g_gq, g_gk, g_sk = tail_flat[:, o:o + HEAD_DIM], tail_flat[:, o + HEAD_DIM:o + 2 * HEAD_DIM], tail_flat[:, o + 2 * HEAD_DIM:o + 2 * HEAD_DIM + N_Q]
    loss = tail_flat[0, o + 2 * HEAD_DIM + N_Q]

    update({"w_ada": dw_ada}, 16, "adamw_w_ada")
    update({"ffn1_w_gate": r_wg1, "ffn1_w_up": r_wu1, "ffn1_w_down": r_wd1, "w_in": r_win}, 8, "adamw_late")

    small_grads = {"b_ada": db_ada, "g_norm1": g_gn1, "g_norm2": g_gn2, "g_norm3": g_gn3, "g_sgu_ln": g_gln,
                   "b_sgu_ln": g_bln, "w_spatial": g_ws, "b_spatial": g_bs, "g_q": g_gq, "g_k": g_gk,
                   "attn_sinks": g_sk}

    def pack(d):
        flat = jnp.concatenate([d[nm].reshape(-1) for nm in small_grads])
        return jnp.pad(flat, (0, (-flat.size) % (8 * 128))).reshape(-1, 128)

    ((sd, sm, sv),) = adamw([(pack(weights), pack(small_grads), pack(m_in), pack(v_in))], steps=1, name="adamw_small")
    off = 0
    for nm, g in small_grads.items():
        size, shape = weights[nm].size, weights[nm].shape
        grads[nm] = g.reshape(shape)
        delta[nm] = sd.reshape(-1)[off:off + size].reshape(shape)
        new_m[nm] = sm.reshape(-1)[off:off + size].reshape(shape)
        new_v[nm] = sv.reshape(-1)[off:off + size].reshape(shape)
        off += size

    return (loss, dx.reshape(B, S, D), *[grads[nm] for nm in names], *[delta[nm] for nm in names],
            *[new_m[nm] for nm in names], *[new_v[nm] for nm in names])
```

```python
import functools
import math
import operator

import jax
import jax.numpy as jnp
from jax import lax
from jax.experimental import pallas as pl
from jax.experimental.pallas import tpu as pltpu

F32 = jnp.float32
BF16 = jnp.bfloat16
EPS = 1e-6
NEG = -1e30
N_DEV = 8
N_QUAD = 4
D_A = 512
N_GROUPS = 4
CHUNK = 128
HEAD_DIM = 64
N_KV = 2
Q_PER_KV = 4
N_Q = N_KV * Q_PER_KV
D_B = N_Q * HEAD_DIM
QKV_W = D_B + 2 * N_KV * HEAD_DIM
K_OFF = D_B
V_OFF = D_B + N_KV * HEAD_DIM
ATT_SCALE = HEAD_DIM ** -0.5
GELU_K = math.sqrt(2.0 / math.pi)
GELU_C = 0.044715
ADAM_LR, ADAM_B1, ADAM_B2, ADAM_EPS, ADAM_WD, ADAM_STEP = 0.001, 0.9, 0.999, 1e-08, 0.01, 10
VMEM_LIMIT_BYTES = 56 * 1024 * 1024
MESH = pl.DeviceIdType.MESH
NT = (((1,), (1,)), ((), ()))
TN = (((0,), (0,)), ((), ()))
ANY = pl.BlockSpec(memory_space=pl.ANY)


def _dot(a, b):
    return jnp.dot(a, b, preferred_element_type=F32)


def _dg(a, b, dims):
    return lax.dot_general(a, b, dims, preferred_element_type=F32)


def _gelu_parts(x):
    t = jnp.tanh(GELU_K * (x + GELU_C * x * x * x))
    return 0.5 * x * (1.0 + t), t


def _dgelu(x, t):
    return 0.5 * (1.0 + t) + 0.5 * x * (1.0 - t * t) * (GELU_K * (1.0 + 3.0 * GELU_C * x * x))


def _row_block(rows, target):
    b = min(rows, target)
    while rows % b or b % 8:
        b -= 1
    return b


def _mod_spec(mod, tps):
    mods, j = mod
    return pl.BlockSpec((1, None, 1, mods.shape[3]), lambda i: (i // tps, j, 0, 0))


def _resident(a):
    return pl.BlockSpec(a.shape, lambda *_: (0,) * a.ndim, pipeline_mode=pl.Buffered(1))


class Comm:
    def __init__(self, *, srcs=(), bufs=(), land_shapes=(), n_sems, start, finish):
        self.srcs, self.bufs, self.land_shapes = list(srcs), list(bufs), list(land_shapes)
        self.n_sems, self.start, self.finish = n_sems, start, finish
        self.bufs_out, self.lands = None, None


def _call(body, *, name, grid, in_specs, out_specs, out_shape, args, scratch_shapes=(), comms=(), prefetch=()):
    prefetch = list(prefetch)
    in_specs, out_specs, out_shape = list(in_specs), list(out_specs), list(out_shape)
    args, scratch = list(args), list(scratch_shapes)
    n_in, n_out, n_scr = len(args), len(out_shape), len(scratch)
    aliases, layout = {}, []
    for cm in comms:
        i0, o0, s0 = len(args), len(out_shape), len(scratch)
        args += cm.srcs + cm.bufs
        in_specs += [ANY] * (len(cm.srcs) + len(cm.bufs))
        out_shape += [jax.ShapeDtypeStruct(b.shape, b.dtype) for b in cm.bufs] + cm.land_shapes
        out_specs += [ANY] * (len(cm.bufs) + len(cm.land_shapes))
        for j in range(len(cm.bufs)):
            aliases[len(prefetch) + i0 + len(cm.srcs) + j] = o0 + j
        scratch += [pltpu.SemaphoreType.DMA((cm.n_sems,)), pltpu.SemaphoreType.DMA((cm.n_sems,))]
        layout.append((i0, o0, s0))
    n_in_all, n_out_all = len(args), len(out_shape)

    def wrapped(*refs):
        scalars, refs = refs[:len(prefetch)], refs[len(prefetch):]
        ins, outs, scr = refs[:n_in_all], refs[n_in_all:n_in_all + n_out_all], refs[n_in_all + n_out_all:]
        parts = []
        for cm, (i0, o0, s0) in zip(comms, layout):
            nb = len(cm.bufs)
            parts.append((ins[i0:i0 + len(cm.srcs)], outs[o0:o0 + nb], outs[o0 + nb:o0 + nb + len(cm.land_shapes)],
                          scr[s0], scr[s0 + 1]))
        if comms and grid:
            ids = [pl.program_id(d) for d in range(len(grid))]
            first = functools.reduce(operator.and_, [i == 0 for i in ids])
            last = functools.reduce(operator.and_, [i == g - 1 for i, g in zip(ids, grid)])

            @pl.when(first)
            def _():
                for cm, p in zip(comms, parts):
                    cm.start(*p)
        elif comms:
            for cm, p in zip(comms, parts):
                cm.start(*p)
        if body is not None:
            body(*scalars, *ins[:n_in], *outs[:n_out], *scr[:n_scr])
        if comms and grid:
            @pl.when(last)
            def _():
                for cm, p in zip(comms, parts):
                    cm.finish(*p)
        elif comms:
            for cm, p in zip(comms, parts):
                cm.finish(*p)

    if prefetch:
        kwargs = dict(grid_spec=pltpu.PrefetchScalarGridSpec(
            num_scalar_prefetch=len(prefetch), grid=grid, in_specs=in_specs, out_specs=out_specs, scratch_shapes=scratch))
    else:
        kwargs = dict(in_specs=in_specs, out_specs=out_specs, scratch_shapes=scratch, **(dict(grid=grid) if grid else {}))
    sem = ("arbitrary",) * len(grid)
    res = pl.pallas_call(
        wrapped, name=name, out_shape=out_shape, input_output_aliases=aliases,
        compiler_params=pltpu.CompilerParams(dimension_semantics=sem, vmem_limit_bytes=VMEM_LIMIT_BYTES), **kwargs,
    )(*prefetch, *args)
    for cm, (i0, o0, s0) in zip(comms, layout):
        nb = len(cm.bufs)
        cm.bufs_out = list(res[o0:o0 + nb])
        cm.lands = list(res[o0 + nb:o0 + nb + len(cm.land_shapes)])
    return list(res[:n_out])


def run_comms(comms, *, name):
    _call(None, name=name, grid=(), in_specs=[], out_specs=[], out_shape=[], args=[], comms=comms)


def norm_mod_fwd(h, g, sc, sh, *, seq, tm, name, comms=()):
    T, D = h.shape
    B, tps = T // seq, seq // tm

    def body(h_ref, g_ref, sc_ref, sh_ref, o_ref):
        x = h_ref[...]
        r = lax.rsqrt(jnp.mean(x * x, axis=-1, keepdims=True) + EPS)
        y = x * r * g_ref[...]
        o_ref[...] = (y * (1.0 + sc_ref[0]) + sh_ref[0]).astype(o_ref.dtype)

    return _call(
        body, name=name, grid=(T // tm,),
        in_specs=[pl.BlockSpec((tm, D), lambda i: (i, 0)), pl.BlockSpec((1, D), lambda i: (0, 0)),
                  _mod_spec(sc, tps), _mod_spec(sh, tps)],
        out_specs=[pl.BlockSpec((tm, D), lambda i: (i, 0))], out_shape=[jax.ShapeDtypeStruct((T, D), BF16)],
        args=[h, g, sc[0], sh[0]], comms=comms)[0]


def ffn_up(xn, wg, wu, *, tm, name, comms=()):
    T, D = xn.shape
    nq, fs, _ = wg.shape

    def body(x_ref, wg_ref, wu_ref, s_ref, q_ref, a_ref):
        x = x_ref[...]
        g = _dg(x, wg_ref[0], NT)
        u = _dg(x, wu_ref[0], NT)
        sg = jax.nn.sigmoid(g)
        s = g * sg
        s_ref[0] = s.astype(BF16)
        q_ref[0] = (u * (sg + s * (1.0 - sg))).astype(BF16)
        a_ref[0] = (s * u).astype(BF16)

    w_spec = pl.BlockSpec((1, fs, D), lambda q, i: (q, 0, 0))
    o_spec = pl.BlockSpec((1, tm, fs), lambda q, i: (q, i, 0))
    o_shape = jax.ShapeDtypeStruct((nq, T, fs), BF16)
    return _call(
        body, name=name, grid=(nq, T // tm),
        in_specs=[pl.BlockSpec((tm, D), lambda q, i: (i, 0)), w_spec, w_spec],
        out_specs=[o_spec, o_spec, o_spec], out_shape=[o_shape, o_shape, o_shape], args=[xn, wg, wu], comms=comms)


def _norm_mod(y, g_ref, sc_ref, sh_ref):
    nx, _ = _rms_parts(y)
    return ((nx * g_ref[...]) * (1.0 + sc_ref[0]) + sh_ref[0]).astype(BF16)


def ffn_down(a, wd, hin, ga, *, target=None, norm=None, seq, tm, name, comms=()):
    nq, T, fs = a.shape
    D = wd.shape[-1]
    B, tps, nt = T // seq, seq // tm, T // tm

    def body(a_ref, wd_ref, h_ref, ga_ref, *rest):
        rest = list(rest)
        t_ref = rest.pop(0) if target is not None else None
        norm_refs = [rest.pop(0) for _ in range(3)] if norm is not None else None
        o_ref, f_ref = rest[0], rest[1]
        f = _dot(a_ref[0], wd_ref[0])
        for q in range(1, nq):
            f = f + _dot(a_ref[q], wd_ref[q])
        f_ref[...] = f.astype(BF16)
        y = h_ref[...] + (0.5 * ga_ref[0]) * f
        if target is not None:
            e = y - t_ref[...]
            o_ref[...] = e * (1.0 / D)
            rest[2][...] = jnp.full(rest[2].shape, 0.5 * jnp.sum(e * e) * (1.0 / D), F32)
        else:
            o_ref[...] = y
        if norm is not None:
            rest[2][...] = _norm_mod(y, *norm_refs)

    tile = pl.BlockSpec((tm, D), lambda i: (i, 0))
    in_specs = [pl.BlockSpec((nq, tm, fs), lambda i: (0, i, 0)), _resident(wd), tile, _mod_spec(ga, tps)]
    out_specs = [tile, tile]
    out_shape = [jax.ShapeDtypeStruct((T, D), F32), jax.ShapeDtypeStruct((T, D), BF16)]
    args = [a, wd, hin, ga[0]]
    if target is not None:
        in_specs.append(tile)
        args.append(target)
        out_specs.append(pl.BlockSpec((1, 8, 128), lambda i: (i, 0, 0)))
        out_shape.append(jax.ShapeDtypeStruct((nt, 8, 128), F32))
    if norm is not None:
        in_specs += [pl.BlockSpec((1, D), lambda i: (0, 0)), _mod_spec(norm[1], tps), _mod_spec(norm[2], tps)]
        args += [norm[0], norm[1][0], norm[2][0]]
        out_specs.append(tile)
        out_shape.append(jax.ShapeDtypeStruct((T, D), BF16))
    return _call(body, name=name, grid=(nt,), in_specs=in_specs, out_specs=out_specs, out_shape=out_shape, args=args,
                 comms=comms)


def proj_fwd(xn, ws, *, tm, name, comms=()):
    T, D = xn.shape
    n = len(ws)

    def body(*refs):
        x = refs[0][...]
        for j in range(n):
            refs[1 + n + j][...] = _dg(x, refs[1 + j][...], NT).astype(BF16)

    return _call(
        body, name=name, grid=(T // tm,),
        in_specs=[pl.BlockSpec((tm, D), lambda i: (i, 0))] + [pl.BlockSpec(w.shape, lambda i: (0, 0)) for w in ws],
        out_specs=[pl.BlockSpec((tm, w.shape[0]), lambda i: (i, 0)) for w in ws],
        out_shape=[jax.ShapeDtypeStruct((T, w.shape[0]), BF16) for w in ws], args=[xn, *ws], comms=comms)


def _layer_norm_parts(v):
    mu = jnp.mean(v, axis=-1, keepdims=True)
    d = v - mu
    rstd = lax.rsqrt(jnp.mean(d * d, axis=-1, keepdims=True) + EPS)
    return d * rstd, rstd


def _causal():
    row = lax.broadcasted_iota(jnp.int32, (CHUNK, CHUNK), 0)
    col = lax.broadcasted_iota(jnp.int32, (CHUNK, CHUNK), 1)
    return row >= col


def sgu_fwd(puv, gln, bln, ws, bs_t, *, cpb, name, comms=()):
    T = puv.shape[0]
    gd = D_A // N_GROUPS
    tm = cpb * CHUNK

    def body(p_ref, gln_ref, bln_ref, ws_ref, bs_ref, o_ref):
        causal = _causal()
        wm = [jnp.where(causal, ws_ref[g], 0.0).astype(BF16) for g in range(N_GROUPS)]
        for j in range(cpb):
            rows = slice(j * CHUNK, (j + 1) * CHUNK)
            u, _ = _gelu_parts(p_ref[rows, 0:D_A].astype(F32))
            vv, _ = _gelu_parts(p_ref[rows, D_A:2 * D_A].astype(F32))
            vhat, _ = _layer_norm_parts(vv)
            vn = (vhat * gln_ref[...] + bln_ref[...]).astype(BF16)
            for g in range(N_GROUPS):
                cols = slice(g * gd, (g + 1) * gd)
                z = _dot(wm[g], vn[:, cols]) + bs_ref[:, g:g + 1]
                o_ref[rows, cols] = (u[:, cols] * z).astype(BF16)

    full = lambda a: pl.BlockSpec(a.shape, lambda i: (0,) * a.ndim)
    return _call(
        body, name=name, grid=(T // tm,),
        in_specs=[pl.BlockSpec((tm, 2 * D_A), lambda i: (i, 0)), full(gln), full(bln), full(ws), full(bs_t)],
        out_specs=[pl.BlockSpec((tm, D_A), lambda i: (i, 0))], out_shape=[jax.ShapeDtypeStruct((T, D_A), BF16)],
        args=[puv, gln, bln, ws, bs_t], comms=comms)[0]


def _rms_parts(x):
    r = lax.rsqrt(jnp.mean(x * x, axis=-1, keepdims=True) + EPS)
    return x * r, r


KV_W = Q_PER_KV * HEAD_DIM
KV2 = N_KV * HEAD_DIM
STACK = Q_PER_KV * CHUNK


def _iota(shape, dim):
    return lax.broadcasted_iota(jnp.int32, shape, dim)


def _head_mean_matrix(n):
    return jnp.where((_iota((n, n), 0) >> 6) == (_iota((n, n), 1) >> 6), 1.0 / HEAD_DIM, 0.0).astype(BF16)


def _repeat_matrix(h):
    return jnp.where(_iota((KV2, KV_W), 0) == h * HEAD_DIM + (_iota((KV2, KV_W), 1) & (HEAD_DIM - 1)), 1.0, 0.0).astype(BF16)


def _fold_matrix(h):
    j, l = _iota((KV_W, KV2), 0), _iota((KV_W, KV2), 1)
    return jnp.where(((j & (HEAD_DIM - 1)) == (l & (HEAD_DIM - 1))) & ((l >> 6) == h), 1.0, 0.0).astype(BF16)


def _dot_split(x, m):
    hi = x.astype(BF16)
    lo = (x - hi.astype(F32)).astype(BF16)
    return _dot(hi, m) + _dot(lo, m)


def _head_rms(x, mean_matrix):
    r = lax.rsqrt(_dot_split(x * x, mean_matrix) + EPS)
    return x * r, r


def _stack_heads(x):
    head = _iota(x.shape, 1) >> 6
    return jnp.concatenate([jnp.where(head == g, x, 0.0).astype(BF16) for g in range(Q_PER_KV)], axis=0)


def _unstack_heads(y):
    head = _iota((CHUNK, KV_W), 1) >> 6
    out = jnp.where(head == 0, y[0:CHUNK], 0.0)
    for g in range(1, Q_PER_KV):
        out = out + jnp.where(head == g, y[g * CHUNK:(g + 1) * CHUNK], 0.0)
    return out


SOFTMAX_ROWS = 32


def _fill_mask_bias(bias_ref):
    qi = _iota((CHUNK, 2 * CHUNK), 0)
    kj = _iota((CHUNK, 2 * CHUNK), 1)
    diff = qi + CHUNK - kj
    window = (diff >= 0) & (diff < CHUNK)
    bias_ref[0] = jnp.where(window & (kj >= CHUNK), 0.0, NEG)
    bias_ref[1] = jnp.where(window, 0.0, NEG)


def _softmax_rows(s_ref, bias_ref, first, sink_ref, h, c):
    rows = pl.ds(pl.multiple_of(c * SOFTMAX_ROWS, SOFTMAX_ROWS), SOFTMAX_ROWS)
    in_block = pl.ds(pl.multiple_of((c % (CHUNK // SOFTMAX_ROWS)) * SOFTMAX_ROWS, SOFTMAX_ROWS), SOFTMAX_ROWS)
    sink = sink_ref[0, h * Q_PER_KV + c // (CHUNK // SOFTMAX_ROWS)]
    s = s_ref[rows, :] + bias_ref[first, in_block, :]
    m = jnp.maximum(jnp.max(s, axis=-1, keepdims=True), sink)
    p = jnp.exp(s - m)
    es = jnp.exp(sink - m)
    inv = 1.0 / (jnp.sum(p, axis=-1, keepdims=True) + es)
    return rows, p * inv, es * inv


def _fill_keys(p_ref, gk_ref, krep, vrep, seq):
    mean_k = _head_mean_matrix(KV2)
    reps = [_repeat_matrix(h) for h in range(N_KV)]
    for h in range(N_KV):
        krep[h][0:CHUNK, :] = jnp.zeros((CHUNK, KV_W), BF16)
        vrep[h][0:CHUNK, :] = jnp.zeros((CHUNK, KV_W), BF16)
    rows = min(seq, 4 * CHUNK)

    def piece(j, carry):
        r0 = pl.multiple_of(j * rows, CHUNK)
        nk, _ = _head_rms(p_ref[pl.ds(r0, rows), K_OFF:K_OFF + KV2].astype(F32), mean_k)
        kn = (nk * gk_ref[...]).astype(BF16)
        v = p_ref[pl.ds(r0, rows), V_OFF:V_OFF + KV2].astype(BF16)
        r1 = pl.multiple_of(r0 + CHUNK, CHUNK)
        for h in range(N_KV):
            krep[h][pl.ds(r1, rows), :] = _dot(kn, reps[h]).astype(BF16)
            vrep[h][pl.ds(r1, rows), :] = _dot(v, reps[h]).astype(BF16)
        return carry

    lax.fori_loop(0, seq // rows, piece, 0)


def attn_fwd(pqkv, gq_t, gk_t, sinks, *, seq, name, comms=()):
    T = pqkv.shape[0]
    nb = seq // CHUNK

    def body(p_ref, gq_ref, gk_ref, sink_ref, o_ref, k0, k1, v0, v1, bias_ref, s_ref, pr_ref):
        krep, vrep = [k0, k1], [v0, v1]
        _fill_keys(p_ref, gk_ref, krep, vrep, seq)
        _fill_mask_bias(bias_ref)
        mean_q = _head_mean_matrix(D_B)

        def block(i, carry):
            r0 = pl.multiple_of(i * CHUNK, CHUNK)
            first = jnp.minimum(i, 1)
            nq, _ = _head_rms(p_ref[pl.ds(r0, CHUNK), 0:D_B].astype(F32), mean_q)
            qn = nq * (gq_ref[...] * ATT_SCALE)
            for h in range(N_KV):
                qs = _stack_heads(qn[:, h * KV_W:(h + 1) * KV_W])
                s_ref[...] = _dg(qs, krep[h][pl.ds(r0, 2 * CHUNK), :], NT)

                def rows_of(c, carry2):
                    rows, probs, _ = _softmax_rows(s_ref, bias_ref, first, sink_ref, h, c)
                    pr_ref[rows, :] = probs.astype(BF16)
                    return carry2

                lax.fori_loop(0, STACK // SOFTMAX_ROWS, rows_of, 0, unroll=True)
                out = _unstack_heads(_dot(pr_ref[...], vrep[h][pl.ds(r0, 2 * CHUNK), :]))
                o_ref[pl.ds(r0, CHUNK), h * KV_W:(h + 1) * KV_W] = out.astype(BF16)
            return carry

        lax.fori_loop(0, nb, block, 0)

    return _call(
        body, name=name, grid=(T // seq,),
        in_specs=[pl.BlockSpec((seq, QKV_W), lambda b: (b, 0)), pl.BlockSpec(gq_t.shape, lambda b: (0, 0)),
                  pl.BlockSpec(gk_t.shape, lambda b: (0, 0)), pl.BlockSpec(memory_space=pltpu.SMEM)],
        out_specs=[pl.BlockSpec((seq, D_B), lambda b: (b, 0))], out_shape=[jax.ShapeDtypeStruct((T, D_B), BF16)],
        scratch_shapes=[pltpu.VMEM((seq + CHUNK, KV_W), BF16)] * 4
        + [pltpu.VMEM((2, CHUNK, 2 * CHUNK), F32), pltpu.VMEM((STACK, 2 * CHUNK), F32), pltpu.VMEM((STACK, 2 * CHUNK), BF16)],
        args=[pqkv, gq_t, gk_t, sinks], comms=comms)[0]


def mixer_out_fwd(sgu, att, pg, hin, ga, wa, wb, wo, norm, *, seq, tm, name, comms=()):
    T, D = hin.shape
    B, tps = T // seq, seq // tm

    def body(s_ref, t_ref, pg_ref, h_ref, ga_ref, wa_ref, wb_ref, wo_ref, g_ref, sc_ref, sh_ref,
             ya_ref, yb_ref, mg_ref, m_ref, ho_ref, xn_ref):
        ya = _dot(s_ref[...], wa_ref[...])
        yb = _dot(t_ref[...], wb_ref[...])
        merged = (jax.nn.sigmoid(pg_ref[:, 0:D].astype(F32)) * ya
                  + jax.nn.sigmoid(pg_ref[:, D:2 * D].astype(F32)) * yb)
        mg = merged.astype(BF16)
        m = _dot(mg, wo_ref[...])
        ya_ref[...] = ya.astype(BF16)
        yb_ref[...] = yb.astype(BF16)
        mg_ref[...] = mg
        m_ref[...] = m.astype(BF16)
        y = h_ref[...] + ga_ref[0] * m
        ho_ref[...] = y
        xn_ref[...] = _norm_mod(y, g_ref, sc_ref, sh_ref)

    tile = lambda w: pl.BlockSpec((tm, w), lambda i: (i, 0))
    b16 = jax.ShapeDtypeStruct((T, D), BF16)
    return _call(
        body, name=name, grid=(T // tm,),
        in_specs=[tile(D_A), tile(D_B), tile(2 * D), tile(D), _mod_spec(ga, tps), _resident(wa), _resident(wb),
                  _resident(wo), pl.BlockSpec((1, D), lambda i: (0, 0)), _mod_spec(norm[1], tps), _mod_spec(norm[2], tps)],
        out_specs=[tile(D)] * 6, out_shape=[b16, b16, b16, b16, jax.ShapeDtypeStruct((T, D), F32), b16],
        args=[sgu, att, pg, hin, ga[0], wa, wb, wo, norm[0], norm[1][0], norm[2][0]], comms=comms)


def ffn_bwd_act(dh, f, ga, wd, s, q, *, seq, tm, name, comms=()):
    T, D = dh.shape
    nq, fs, _ = wd.shape
    B, tps = T // seq, seq // tm

    def body(dh_ref, f_ref, ga_ref, wd_ref, s_ref, q_ref, dg_ref, du_ref, df_ref, dga_ref):
        i = pl.program_id(0)
        d = dh_ref[...]
        df = ((0.5 * ga_ref[0]) * d).astype(BF16)
        df_ref[...] = df
        part = 0.5 * jnp.sum(d * f_ref[...].astype(F32), axis=0, keepdims=True)
        for j in range(nq):
            da = _dg(df, wd_ref[j], NT)
            du_ref[j] = (da * s_ref[j].astype(F32)).astype(BF16)
            dg_ref[j] = (da * q_ref[j].astype(F32)).astype(BF16)

        @pl.when(i % tps == 0)
        def _():
            dga_ref[0] = part

        @pl.when(i % tps != 0)
        def _():
            dga_ref[0] += part

    tile = pl.BlockSpec((tm, D), lambda i: (i, 0))
    per_seq = pl.BlockSpec((1, 1, D), lambda i: (i // tps, 0, 0))
    act = pl.BlockSpec((nq, tm, fs), lambda i: (0, i, 0))
    o_shape = jax.ShapeDtypeStruct((nq, T, fs), BF16)
    dg, du, df, dga = _call(
        body, name=name, grid=(T // tm,), in_specs=[tile, tile, _mod_spec(ga, tps), _resident(wd), act, act],
        out_specs=[act, act, tile, per_seq],
        out_shape=[o_shape, o_shape, jax.ShapeDtypeStruct((T, D), BF16), jax.ShapeDtypeStruct((B, 1, D), F32)],
        args=[dh, f, ga[0], wd, s, q], comms=comms)
    return dg, du, df, dga.reshape(B, D)


def mm_tn(pairs, *, tt, name, comms=()):
    n = len(pairs)
    flat = []
    for pair in pairs:
        for a in pair:
            if not any(a is b for b in flat):
                flat.append(a)
    where = lambda a: [k for k, b in enumerate(flat) if a is b][0]
    nq = max([a.shape[0] for a in flat if a.ndim == 3] + [1])
    T = flat[0].shape[-2]
    tt = min(tt, T)
    nt = T // tt
    stacked = [x.ndim == 3 or y.ndim == 3 for x, y in pairs]

    def body(*refs):
        in_refs, o_refs, accs = refs[:len(flat)], refs[len(flat):len(flat) + n], refs[len(flat) + n:]
        t = pl.program_id(1)
        value = lambda a: in_refs[where(a)][0] if a.ndim == 3 else in_refs[where(a)][...]

        def put(j, v):
            if stacked[j]:
                o_refs[j][0] = v.astype(BF16)
            else:
                o_refs[j][...] = v.astype(BF16)

        for j, (x, y) in enumerate(pairs):
            part = _dg(value(x), value(y), TN)
            if nt == 1:
                put(j, part)
                continue

            @pl.when(t == 0)
            def _():
                accs[j][...] = part

            @pl.when(t != 0)
            def _():
                accs[j][...] += part

        if nt > 1:
            @pl.when(t == nt - 1)
            def _():
                for j in range(n):
                    put(j, accs[j][...])

    def spec(a):
        if a.ndim == 3:
            return pl.BlockSpec((1, tt, a.shape[2]), lambda q, t: (q, t, 0))
        return pl.BlockSpec((tt, a.shape[1]), lambda q, t: (t, 0))

    out_specs, out_shape = [], []
    for j, (x, y) in enumerate(pairs):
        K, N = x.shape[-1], y.shape[-1]
        if stacked[j]:
            out_specs.append(pl.BlockSpec((1, K, N), lambda q, t: (q, 0, 0)))
            out_shape.append(jax.ShapeDtypeStruct((nq, K, N), BF16))
        else:
            out_specs.append(pl.BlockSpec((K, N), lambda q, t: (0, 0)))
            out_shape.append(jax.ShapeDtypeStruct((K, N), BF16))
    return _call(
        body, name=name, grid=(nq, nt), in_specs=[spec(a) for a in flat],
        out_specs=out_specs, out_shape=out_shape,
        scratch_shapes=[pltpu.VMEM((x.shape[-1], y.shape[-1]), F32) for x, y in pairs] if nt > 1 else [],
        args=flat, comms=comms)


def dx_norm_bwd(dys, ws, hin, dh_out, g, sc, *, seq, tm, name, comms=()):
    T, D = hin.shape
    B, tps = T // seq, seq // tm
    n = len(dys)

    def body(*refs):
        dy_refs, w_refs = refs[:n], refs[n:2 * n]
        h_ref, dho_ref, g_ref, sc_ref, dhi_ref, dsh_ref, dsc_ref, dgn_ref = refs[2 * n:]
        i = pl.program_id(0)
        dxn = None
        for j in range(n):
            if dys[j].ndim == 3:
                for q in range(dys[j].shape[0]):
                    part = _dot(dy_refs[j][q], w_refs[j][q])
                    dxn = part if dxn is None else dxn + part
            else:
                part = _dot(dy_refs[j][...], w_refs[j][...])
                dxn = part if dxn is None else dxn + part
        x = h_ref[...]
        nx, r = _rms_parts(x)
        gain = g_ref[...]
        one_sc = 1.0 + sc_ref[0]
        dsh = jnp.sum(dxn, axis=0, keepdims=True)
        dsc = jnp.sum(dxn * (nx * gain), axis=0, keepdims=True)
        dgn = jnp.sum(dxn * one_sc * nx, axis=0, keepdims=True)
        dn = dxn * one_sc * gain
        dhi_ref[...] = dho_ref[...] + r * (dn - nx * jnp.mean(dn * nx, axis=-1, keepdims=True))

        @pl.when(i % tps == 0)
        def _():
            dsh_ref[0] = dsh
            dsc_ref[0] = dsc

        @pl.when(i % tps != 0)
        def _():
            dsh_ref[0] += dsh
            dsc_ref[0] += dsc

        @pl.when(i == 0)
        def _():
            dgn_ref[...] = dgn

        @pl.when(i != 0)
        def _():
            dgn_ref[...] += dgn

    def dy_spec(a):
        if a.ndim == 3:
            return pl.BlockSpec((a.shape[0], tm, a.shape[2]), lambda i: (0, i, 0))
        return pl.BlockSpec((tm, a.shape[1]), lambda i: (i, 0))

    tile = pl.BlockSpec((tm, D), lambda i: (i, 0))
    per_seq = pl.BlockSpec((1, 1, D), lambda i: (i // tps, 0, 0))
    row = pl.BlockSpec((1, D), lambda i: (0, 0))
    dhi, dsh, dsc, dgn = _call(
        body, name=name, grid=(T // tm,),
        in_specs=[dy_spec(a) for a in dys] + [_resident(w) for w in ws] + [tile, tile, row, _mod_spec(sc, tps)],
        out_specs=[tile, per_seq, per_seq, row],
        out_shape=[jax.ShapeDtypeStruct((T, D), F32), jax.ShapeDtypeStruct((B, 1, D), F32),
                   jax.ShapeDtypeStruct((B, 1, D), F32), jax.ShapeDtypeStruct((1, D), F32)],
        args=[*dys, *ws, hin, dh_out, g, sc[0]], comms=comms)
    return dhi, dsh.reshape(B, D), dsc.reshape(B, D), dgn


def mixer_out_bwd(dh, m, ga, ya, yb, pg, wa, wb, wo, *, seq, tm, name, comms=()):
    T, D = dh.shape
    B, tps = T // seq, seq // tm

    def body(dh_ref, m_ref, ga_ref, ya_ref, yb_ref, pg_ref, wa_ref, wb_ref, wo_ref,
             dg_ref, dya_ref, dyb_ref, ds_ref, dt_ref, dm_ref, dga_ref):
        i = pl.program_id(0)
        d = dh_ref[...]
        dm = (ga_ref[0] * d).astype(BF16)
        dm_ref[...] = dm
        part = jnp.sum(d * m_ref[...].astype(F32), axis=0, keepdims=True)

        @pl.when(i % tps == 0)
        def _():
            dga_ref[0] = part

        @pl.when(i % tps != 0)
        def _():
            dga_ref[0] += part

        dmg = _dg(dm, wo_ref[...], NT)
        sa = jax.nn.sigmoid(pg_ref[:, 0:D].astype(F32))
        sb = jax.nn.sigmoid(pg_ref[:, D:2 * D].astype(F32))
        dg_ref[:, 0:D] = (dmg * ya_ref[...].astype(F32) * (sa * (1.0 - sa))).astype(BF16)
        dg_ref[:, D:2 * D] = (dmg * yb_ref[...].astype(F32) * (sb * (1.0 - sb))).astype(BF16)
        dya = (dmg * sa).astype(BF16)
        dyb = (dmg * sb).astype(BF16)
        dya_ref[...] = dya
        dyb_ref[...] = dyb
        ds_ref[...] = _dg(dya, wa_ref[...], NT).astype(BF16)
        dt_ref[...] = _dg(dyb, wb_ref[...], NT).astype(BF16)

    tile = lambda w: pl.BlockSpec((tm, w), lambda i: (i, 0))
    per_seq = pl.BlockSpec((1, 1, D), lambda i: (i // tps, 0, 0))
    dg, dya, dyb, ds, dt, dm, dga = _call(
        body, name=name, grid=(T // tm,),
        in_specs=[tile(D), tile(D), _mod_spec(ga, tps), tile(D), tile(D), tile(2 * D), _resident(wa), _resident(wb),
                  _resident(wo)],
        out_specs=[tile(2 * D), tile(D), tile(D), tile(D_A), tile(D_B), tile(D), per_seq],
        out_shape=[jax.ShapeDtypeStruct((T, 2 * D), BF16), jax.ShapeDtypeStruct((T, D), BF16),
                   jax.ShapeDtypeStruct((T, D), BF16), jax.ShapeDtypeStruct((T, D_A), BF16),
                   jax.ShapeDtypeStruct((T, D_B), BF16), jax.ShapeDtypeStruct((T, D), BF16),
                   jax.ShapeDtypeStruct((B, 1, D), F32)],
        args=[dh, m, ga[0], ya, yb, pg, wa, wb, wo], comms=comms)
    return dg, dya, dyb, ds, dt, dm, dga.reshape(B, D)


def sgu_bwd(puv, dsgu, gln, bln, ws, bs_t, *, cpb, name, comms=()):
    T = puv.shape[0]
    gd = D_A // N_GROUPS
    tm = cpb * CHUNK

    def body(p_ref, ds_ref, gln_ref, bln_ref, ws_ref, bs_ref, d_ref, dws_ref, dz_ref, dgl_ref, dbl_ref):
        i = pl.program_id(0)
        causal = _causal()
        wf = [jnp.where(causal, ws_ref[g], 0.0) for g in range(N_GROUPS)]
        wm = [w.astype(BF16) for w in wf]
        wt = [w.T.astype(BF16) for w in wf]
        dws = [jnp.zeros((CHUNK, CHUNK), F32) for _ in range(N_GROUPS)]
        dzs = jnp.zeros((CHUNK, D_A), F32)
        dgl = jnp.zeros((1, D_A), F32)
        dbl = jnp.zeros((1, D_A), F32)
        for j in range(cpb):
            rows = slice(j * CHUNK, (j + 1) * CHUNK)
            au = p_ref[rows, 0:D_A].astype(F32)
            av = p_ref[rows, D_A:2 * D_A].astype(F32)
            u, tu = _gelu_parts(au)
            vv, tv = _gelu_parts(av)
            vhat, rstd = _layer_norm_parts(vv)
            vn = (vhat * gln_ref[...] + bln_ref[...]).astype(BF16)
            dsg = ds_ref[rows, :].astype(F32)
            dz = dsg * u
            dzb = dz.astype(BF16)
            zs, dvns = [], []
            for g in range(N_GROUPS):
                cols = slice(g * gd, (g + 1) * gd)
                zs.append(_dot(wm[g], vn[:, cols]) + bs_ref[:, g:g + 1])
                dws[g] = dws[g] + _dg(dzb[:, cols], vn[:, cols], NT)
                dvns.append(_dot(wt[g], dzb[:, cols]))
            z = jnp.concatenate(zs, axis=1)
            dvn = jnp.concatenate(dvns, axis=1)
            d_ref[rows, 0:D_A] = (dsg * z * _dgelu(au, tu)).astype(BF16)
            dvh = dvn * gln_ref[...]
            dvv = rstd * (dvh - jnp.mean(dvh, axis=-1, keepdims=True)
                          - vhat * jnp.mean(dvh * vhat, axis=-1, keepdims=True))
            d_ref[rows, D_A:2 * D_A] = (dvv * _dgelu(av, tv)).astype(BF16)
            dzs = dzs + dz
            dgl = dgl + jnp.sum(dvn * vhat, axis=0, keepdims=True)
            dbl = dbl + jnp.sum(dvn, axis=0, keepdims=True)

        @pl.when(i == 0)
        def _():
            for g in range(N_GROUPS):
                dws_ref[g] = jnp.where(causal, dws[g], 0.0)
            dz_ref[...] = dzs
            dgl_ref[...] = dgl
            dbl_ref[...] = dbl

        @pl.when(i != 0)
        def _():
            for g in range(N_GROUPS):
                dws_ref[g] += jnp.where(causal, dws[g], 0.0)
            dz_ref[...] += dzs
            dgl_ref[...] += dgl
            dbl_ref[...] += dbl

    full = lambda a: pl.BlockSpec(a.shape, lambda i: (0,) * a.ndim)
    acc = lambda s: pl.BlockSpec(s, lambda i: (0,) * len(s))
    return _call(
        body, name=name, grid=(T // tm,),
        in_specs=[pl.BlockSpec((tm, 2 * D_A), lambda i: (i, 0)), pl.BlockSpec((tm, D_A), lambda i: (i, 0)),
                  full(gln), full(bln), full(ws), full(bs_t)],
        out_specs=[pl.BlockSpec((tm, 2 * D_A), lambda i: (i, 0)), acc((N_GROUPS, CHUNK, CHUNK)), acc((CHUNK, D_A)),
                   acc((1, D_A)), acc((1, D_A))],
        out_shape=[jax.ShapeDtypeStruct((T, 2 * D_A), BF16), jax.ShapeDtypeStruct((N_GROUPS, CHUNK, CHUNK), F32),
                   jax.ShapeDtypeStruct((CHUNK, D_A), F32), jax.ShapeDtypeStruct((1, D_A), F32),
                   jax.ShapeDtypeStruct((1, D_A), F32)],
        args=[puv, dsgu, gln, bln, ws, bs_t], comms=comms)


def attn_bwd(pqkv, datt, gq_t, gk_t, sinks, *, seq, name, comms=()):
    T = pqkv.shape[0]
    nb = seq // CHUNK

    def body(p_ref, do_ref, gq_ref, gk_ref, sink_ref, d_ref, dgq_ref, dgk_ref, dsk_ref,
             k0, k1, v0, v1, dk0, dk1, dv0, dv1, dgq_s, dsk_s, bias_ref, s_ref, dp_ref, pr_ref, ds_ref):
        b = pl.program_id(0)
        krep, vrep, dkacc, dvacc = [k0, k1], [v0, v1], [dk0, dk1], [dv0, dv1]
        _fill_keys(p_ref, gk_ref, krep, vrep, seq)
        _fill_mask_bias(bias_ref)
        for h in range(N_KV):
            dkacc[h][...] = jnp.zeros_like(dkacc[h])
            dvacc[h][...] = jnp.zeros_like(dvacc[h])
        dgq_s[...] = jnp.zeros_like(dgq_s)
        dsk_s[...] = jnp.zeros_like(dsk_s)
        mean_q = _head_mean_matrix(D_B)
        lane = _iota((1, 128), 1)

        def block(i, carry):
            r0 = pl.multiple_of(i * CHUNK, CHUNK)
            first = jnp.minimum(i, 1)
            nq, rq = _head_rms(p_ref[pl.ds(r0, CHUNK), 0:D_B].astype(F32), mean_q)
            qn = nq * (gq_ref[...] * ATT_SCALE)
            dqn_parts = []
            for h in range(N_KV):
                cols = slice(h * KV_W, (h + 1) * KV_W)
                qs = _stack_heads(qn[:, cols])
                kk = krep[h][pl.ds(r0, 2 * CHUNK), :]
                dos = _stack_heads(do_ref[pl.ds(r0, CHUNK), cols].astype(F32))
                s_ref[...] = _dg(qs, kk, NT)
                dp_ref[...] = _dg(dos, vrep[h][pl.ds(r0, 2 * CHUNK), :], NT)

                def rows_of(c, carry2):
                    rows, probs, psink = _softmax_rows(s_ref, bias_ref, first, sink_ref, h, c)
                    dp = dp_ref[rows, :]
                    dr = jnp.sum(probs * dp, axis=-1, keepdims=True)
                    pr_ref[rows, :] = probs.astype(BF16)
                    ds_ref[rows, :] = (probs * (dp - dr)).astype(BF16)
                    head = h * Q_PER_KV + c // (CHUNK // SOFTMAX_ROWS)
                    dsk_s[...] += jnp.where(lane == head, -jnp.sum(psink * dr), 0.0)
                    return carry2

                lax.fori_loop(0, STACK // SOFTMAX_ROWS, rows_of, 0, unroll=True)
                dqn_parts.append(_unstack_heads(_dot(ds_ref[...], kk)))
                dkacc[h][pl.ds(r0, 2 * CHUNK), :] += _dg(ds_ref[...], qs, TN)
                dvacc[h][pl.ds(r0, 2 * CHUNK), :] += _dg(pr_ref[...], dos, TN)
            dqn = jnp.concatenate(dqn_parts, axis=1) * ATT_SCALE
            dn = dqn * gq_ref[...]
            d_ref[pl.ds(r0, CHUNK), 0:D_B] = (rq * (dn - nq * _dot_split(dn * nq, mean_q))).astype(BF16)
            dgq_s[...] += jnp.sum(dqn * nq, axis=0, keepdims=True)
            return carry

        lax.fori_loop(0, nb, block, 0)

        mean_k = _head_mean_matrix(KV2)
        folds = [_fold_matrix(h) for h in range(N_KV)]
        rows = min(seq, 4 * CHUNK)

        def piece(j, dgk):
            r0 = pl.multiple_of(j * rows, CHUNK)
            r1 = pl.multiple_of(r0 + CHUNK, CHUNK)
            dkn = _dot_split(dkacc[0][pl.ds(r1, rows), :], folds[0]) + _dot_split(dkacc[1][pl.ds(r1, rows), :], folds[1])
            dv = _dot_split(dvacc[0][pl.ds(r1, rows), :], folds[0]) + _dot_split(dvacc[1][pl.ds(r1, rows), :], folds[1])
            nk, rk = _head_rms(p_ref[pl.ds(r0, rows), K_OFF:K_OFF + KV2].astype(F32), mean_k)
            dn = dkn * gk_ref[...]
            d_ref[pl.ds(r0, rows), K_OFF:K_OFF + KV2] = (rk * (dn - nk * _dot_split(dn * nk, mean_k))).astype(BF16)
            d_ref[pl.ds(r0, rows), V_OFF:V_OFF + KV2] = dv.astype(BF16)
            return dgk + jnp.sum(dkn * nk, axis=0, keepdims=True)

        dgk = lax.fori_loop(0, seq // rows, piece, jnp.zeros((1, KV2), F32))

        @pl.when(b == 0)
        def _():
            dgq_ref[...] = dgq_s[...]
            dgk_ref[...] = dgk
            dsk_ref[...] = jnp.broadcast_to(dsk_s[...], dsk_ref.shape)

        @pl.when(b != 0)
        def _():
            dgq_ref[...] += dgq_s[...]
            dgk_ref[...] += dgk
            dsk_ref[...] += jnp.broadcast_to(dsk_s[...], dsk_ref.shape)

    acc = lambda s: pl.BlockSpec(s, lambda b: (0, 0))
    return _call(
        body, name=name, grid=(T // seq,),
        in_specs=[pl.BlockSpec((seq, QKV_W), lambda b: (b, 0)), pl.BlockSpec((seq, D_B), lambda b: (b, 0)),
                  pl.BlockSpec(gq_t.shape, lambda b: (0, 0)), pl.BlockSpec(gk_t.shape, lambda b: (0, 0)),
                  pl.BlockSpec(memory_space=pltpu.SMEM)],
        out_specs=[pl.BlockSpec((seq, QKV_W), lambda b: (b, 0)), acc((1, D_B)), acc((1, KV2)), acc((8, 128))],
        out_shape=[jax.ShapeDtypeStruct((T, QKV_W), BF16), jax.ShapeDtypeStruct((1, D_B), F32),
                   jax.ShapeDtypeStruct((1, KV2), F32), jax.ShapeDtypeStruct((8, 128), F32)],
        scratch_shapes=[pltpu.VMEM((seq + CHUNK, KV_W), BF16)] * 4 + [pltpu.VMEM((seq + CHUNK, KV_W), F32)] * 4
        + [pltpu.VMEM((1, D_B), F32), pltpu.VMEM((1, 128), F32), pltpu.VMEM((2, CHUNK, 2 * CHUNK), F32)]
        + [pltpu.VMEM((STACK, 2 * CHUNK), F32)] * 2 + [pltpu.VMEM((STACK, 2 * CHUNK), BF16)] * 2,
        args=[pqkv, datt, gq_t, gk_t, sinks], comms=comms)


def ada_fwd(c_all, w, b, *, name):
    nb, D = c_all.shape
    N = w.shape[1]
    tn = N // 3

    def body(c_ref, w_ref, b_ref, o_ref):
        c = c_ref[...]
        cond = (c * jax.nn.sigmoid(c)).astype(BF16)
        o_ref[...] = _dot(cond, w_ref[...].astype(BF16)) + b_ref[...]

    return _call(
        body, name=name, grid=(3,),
        in_specs=[pl.BlockSpec((nb, D), lambda j: (0, 0)), pl.BlockSpec((D, tn), lambda j: (0, j)),
                  pl.BlockSpec((1, tn), lambda j: (0, j))],
        out_specs=[pl.BlockSpec((nb, tn), lambda j: (0, j))], out_shape=[jax.ShapeDtypeStruct((nb, N), F32)],
        args=[c_all, w, b])[0]


def ada_bwd(c_all, dmods_all, dmods_mine, gain_rows, *, name, comms=()):
    nb, D = c_all.shape
    NA = dmods_all.shape[1]
    N = dmods_mine.shape[1]
    tn = N // 3

    def body(c_ref, da_ref, dm_ref, gr_ref, db_ref, dw_ref, dgn_ref):
        c = c_ref[...]
        cond = (c * jax.nn.sigmoid(c)).astype(BF16)
        dw_ref[...] = _dg(cond, dm_ref[...].astype(BF16), TN)
        db_ref[...] = jnp.sum(da_ref[...], axis=0, keepdims=True)
        total = gr_ref[0:1, :]
        for d in range(1, N_DEV):
            total = total + gr_ref[d:d + 1, :]
        dgn_ref[...] = total

    return _call(
        body, name=name, grid=(3,),
        in_specs=[pl.BlockSpec((nb, D), lambda j: (0, 0)), pl.BlockSpec((nb, NA), lambda j: (0, 0)),
                  pl.BlockSpec((nb, tn), lambda j: (0, j)), pl.BlockSpec((N_DEV, D), lambda j: (0, 0))],
        out_specs=[pl.BlockSpec((1, NA), lambda j: (0, 0)), pl.BlockSpec((D, tn), lambda j: (0, j)),
                   pl.BlockSpec((1, D), lambda j: (0, 0))],
        out_shape=[jax.ShapeDtypeStruct((1, NA), F32), jax.ShapeDtypeStruct((D, N), F32),
                   jax.ShapeDtypeStruct((1, D), F32)],
        args=[c_all, dmods_all, dmods_mine, gain_rows], comms=comms)


def adamw(items, *, steps, name, comms=()):
    n = len(items)
    c1 = 1.0 / (1.0 - ADAM_B1 ** ADAM_STEP)
    c2 = 1.0 / (1.0 - ADAM_B2 ** ADAM_STEP)

    def body(*refs):
        for j in range(n):
            w_ref, g_ref, m_ref, v_ref = refs[4 * j:4 * j + 4]
            d_ref, mo_ref, vo_ref = refs[4 * n + 3 * j:4 * n + 3 * j + 3]
            gg = g_ref[...]
            mn = ADAM_B1 * m_ref[...] + (1.0 - ADAM_B1) * gg
            vn = ADAM_B2 * v_ref[...] + (1.0 - ADAM_B2) * (gg * gg)
            mo_ref[...] = mn
            vo_ref[...] = vn
            d_ref[...] = -ADAM_LR * ((mn * c1) / (jnp.sqrt(vn * c2) + ADAM_EPS) + ADAM_WD * w_ref[...])

    in_specs, out_specs, out_shape, args = [], [], [], []
    for item in items:
        R, C = item[0].shape
        blk = pl.BlockSpec((R // steps, C), lambda i: (i, 0))
        in_specs += [blk] * 4
        out_specs += [blk] * 3
        out_shape += [jax.ShapeDtypeStruct((R, C), F32)] * 3
        args += list(item)
    res = _call(body, name=name, grid=(steps,), in_specs=in_specs, out_specs=out_specs, out_shape=out_shape, args=args,
                comms=comms)
    return [tuple(res[3 * j:3 * j + 3]) for j in range(n)]


def _place():
    return lax.axis_index("x"), lax.axis_index("y"), lax.axis_index("c")


def _flip(v, bit):
    return 1 - v if bit else v


def _other_chips(mx, my):
    return [(_flip(mx, k & 2), _flip(my, k & 1)) for k in range(1, N_QUAD)]


def _remote(src, dst, send_sem, recv_sem, peer):
    return pltpu.make_async_remote_copy(src_ref=src, dst_ref=dst, send_sem=send_sem, recv_sem=recv_sem,
                                        device_id=peer, device_id_type=MESH)


def all_gather8(x, *, name, reduce=False, comms=()):
    r, n = x.shape

    def body(x_ref, o_ref, *rest):
        if reduce:
            buf, send_sems, recv_sems, lsem = rest
        else:
            buf = o_ref
            send_sems, recv_sems, lsem = rest
        mx, my, mc = _place()
        me = 4 * mx + 2 * my + mc
        mine = pltpu.make_async_copy(x_ref, buf.at[me], lsem)
        mine.start()
        sends, recvs = [], []
        for k in range(1, N_DEV):
            peer = (_flip(mx, k & 4), _flip(my, k & 2), _flip(mc, k & 1))
            pidx = 4 * peer[0] + 2 * peer[1] + peer[2]
            sends.append(_remote(x_ref, buf.at[me], send_sems.at[k - 1], recv_sems.at[k - 1], peer))
            recvs.append(_remote(x_ref, buf.at[pidx], send_sems.at[k - 1], recv_sems.at[k - 1], peer))
        for cp in sends:
            cp.start()
        for cp in recvs:
            cp.wait_recv()
        for cp in sends:
            cp.wait_send()
        mine.wait()
        if reduce:
            total = buf[0]
            for d in range(1, N_DEV):
                total = total + buf[d]
            o_ref[...] = total

    scratch = [pltpu.SemaphoreType.DMA((N_DEV - 1,)), pltpu.SemaphoreType.DMA((N_DEV - 1,)), pltpu.SemaphoreType.DMA]
    if reduce:
        scratch = [pltpu.VMEM((N_DEV, r, n), F32)] + scratch
        out_shape = jax.ShapeDtypeStruct((r, n), F32)
    else:
        out_shape = jax.ShapeDtypeStruct((N_DEV, r, n), F32)
    vmem = pl.BlockSpec(memory_space=pltpu.VMEM)
    return _call(body, name=name, grid=(), in_specs=[vmem], out_specs=[vmem], out_shape=[out_shape], args=[x],
                 scratch_shapes=scratch, comms=comms)[0]


def ag8_comm(x):
    def copies(srcs, lands, ss, rs, only_sends=False):
        mx, my, mc = _place()
        me = 4 * mx + 2 * my + mc
        local = pltpu.make_async_copy(srcs[0], lands[0].at[me], ss.at[N_DEV - 1])
        sends, recvs = [], []
        for k in range(1, N_DEV):
            peer = (_flip(mx, k & 4), _flip(my, k & 2), _flip(mc, k & 1))
            sends.append(_remote(srcs[0], lands[0].at[me], ss.at[k - 1], rs.at[k - 1], peer))
            if not only_sends:
                recvs.append(_remote(srcs[0], lands[0].at[4 * peer[0] + 2 * peer[1] + peer[2]], ss.at[k - 1], rs.at[k - 1], peer))
        return local, sends, recvs

    def start(srcs, bufs, lands, ss, rs):
        local, sends, _ = copies(srcs, lands, ss, rs, only_sends=True)
        local.start()
        for cp in sends:
            cp.start()

    def finish(srcs, bufs, lands, ss, rs):
        local, sends, recvs = copies(srcs, lands, ss, rs)
        for cp in recvs:
            cp.wait_recv()
        for cp in sends:
            cp.wait_send()
        local.wait()

    return Comm(srcs=[x], land_shapes=[jax.ShapeDtypeStruct((N_DEV,) + x.shape, x.dtype)], n_sems=N_DEV,
                start=start, finish=finish)


def sum8(stacked, *, name):
    _, r, n = stacked.shape

    def body(s_ref, o_ref):
        total = s_ref[0]
        for d in range(1, N_DEV):
            total = total + s_ref[d]
        o_ref[...] = total

    return _call(body, name=name, grid=(), in_specs=[pl.BlockSpec(memory_space=pltpu.VMEM)],
                 out_specs=[pl.BlockSpec(memory_space=pltpu.VMEM)], out_shape=[jax.ShapeDtypeStruct((r, n), F32)],
                 args=[stacked])[0]


def cast_into_stacks(ws, quad, *, name, comms=()):
    steps = 4

    def body(q_ref, *refs):
        for w_ref, o_ref in zip(refs[:len(ws)], refs[len(ws):]):
            o_ref[0] = w_ref[...].astype(BF16)

    return _call(
        body, name=name, grid=(steps,), prefetch=[quad],
        in_specs=[pl.BlockSpec((w.shape[0] // steps, w.shape[1]), lambda i, q: (i, 0)) for w in ws],
        out_specs=[pl.BlockSpec((1, w.shape[0] // steps, w.shape[1]), lambda i, q: (q[0], i, 0)) for w in ws],
        out_shape=[jax.ShapeDtypeStruct((N_QUAD,) + w.shape, BF16) for w in ws], args=list(ws), comms=comms)


def gather_comm(stacks):
    n = len(stacks)

    def copies(bufs, ss, rs, only_sends=False):
        mx, my, mc = _place()
        q = 2 * mx + my
        sibling = (mx, my, 1 - mc)
        ici_send, ici_recv, fwd_send, fwd_recv = [], [], [], []
        for p in range(n):
            hr = stacks[p].shape[1] // 2
            mine, other = pl.ds(mc * hr, hr), pl.ds((1 - mc) * hr, hr)
            for k, chip in enumerate(_other_chips(mx, my)):
                qk = 2 * chip[0] + chip[1]
                peer = (chip[0], chip[1], mc)
                own, landed, theirs = bufs[p].at[q, mine, :], bufs[p].at[qk, mine, :], bufs[p].at[qk, other, :]
                ici_send.append(_remote(own, own, ss.at[6 * p + k], rs.at[6 * p + k], peer))
                if only_sends:
                    continue
                ici_recv.append(_remote(own, landed, ss.at[6 * p + k], rs.at[6 * p + k], peer))
                fwd_send.append(_remote(landed, landed, ss.at[6 * p + 3 + k], rs.at[6 * p + 3 + k], sibling))
                fwd_recv.append(_remote(theirs, theirs, ss.at[6 * p + 3 + k], rs.at[6 * p + 3 + k], sibling))
        return ici_send, ici_recv, fwd_send, fwd_recv

    def start(srcs, bufs, lands, ss, rs):
        for cp in copies(bufs, ss, rs, only_sends=True)[0]:
            cp.start()

    def finish(srcs, bufs, lands, ss, rs):
        ici_send, ici_recv, fwd_send, fwd_recv = copies(bufs, ss, rs)
        for arrived, onward in zip(ici_recv, fwd_send):
            arrived.wait_recv()
            onward.start()
        for cp in fwd_recv:
            cp.wait_recv()
        for cp in ici_send + fwd_send:
            cp.wait_send()

    return Comm(bufs=stacks, n_sems=6 * n, start=start, finish=finish)


def rs_pair_comm(gs):
    n = len(gs)

    def copies(srcs, lands, ss, rs):
        mx, my, mc = _place()
        out = []
        for p in range(n):
            hr = gs[p].shape[1] // 2
            out.append(_remote(srcs[p].at[:, pl.ds((1 - mc) * hr, hr), :], lands[p], ss.at[p], rs.at[p], (mx, my, 1 - mc)))
        return out

    def start(srcs, bufs, lands, ss, rs):
        for cp in copies(srcs, lands, ss, rs):
            cp.start()

    def finish(srcs, bufs, lands, ss, rs):
        for cp in copies(srcs, lands, ss, rs):
            cp.wait()

    return Comm(srcs=gs, land_shapes=[jax.ShapeDtypeStruct((g.shape[0], g.shape[1] // 2, g.shape[2]), g.dtype) for g in gs],
                n_sems=n, start=start, finish=finish)


def rs_chip_comm(ss_):
    n = len(ss_)

    def copies(srcs, lands, ss, rs):
        mx, my, mc = _place()
        out = []
        for p in range(n):
            for k, chip in enumerate(_other_chips(mx, my)):
                out.append(_remote(srcs[p].at[2 * chip[0] + chip[1]], lands[p].at[k], ss.at[3 * p + k], rs.at[3 * p + k],
                                   (chip[0], chip[1], mc)))
        return out

    def start(srcs, bufs, lands, ss, rs):
        for cp in copies(srcs, lands, ss, rs):
            cp.start()

    def finish(srcs, bufs, lands, ss, rs):
        for cp in copies(srcs, lands, ss, rs):
            cp.wait()

    return Comm(srcs=ss_, land_shapes=[jax.ShapeDtypeStruct((N_QUAD - 1,) + s.shape[1:], s.dtype) for s in ss_],
                n_sems=3 * n, start=start, finish=finish)


def rs_share_comm(fulls):
    n = len(fulls)

    def copies(bufs, ss, rs, only_sends=False):
        mx, my, mc = _place()
        send, recv = [], []
        for p in range(n):
            hr = fulls[p].shape[0] // 2
            mine, other = bufs[p].at[pl.ds(mc * hr, hr), :], bufs[p].at[pl.ds((1 - mc) * hr, hr), :]
            send.append(_remote(mine, mine, ss.at[p], rs.at[p], (mx, my, 1 - mc)))
            if not only_sends:
                recv.append(_remote(other, other, ss.at[p], rs.at[p], (mx, my, 1 - mc)))
        return send, recv

    def start(srcs, bufs, lands, ss, rs):
        for cp in copies(bufs, ss, rs, only_sends=True)[0]:
            cp.start()

    def finish(srcs, bufs, lands, ss, rs):
        send, recv = copies(bufs, ss, rs)
        for cp in recv:
            cp.wait_recv()
        for cp in send:
            cp.wait_send()

    return Comm(bufs=fulls, n_sems=n, start=start, finish=finish)


def pair_sum(g, recv, mc, *, name):
    nq, R, C = g.shape
    hr = R // 2
    rb = _row_block(hr, 512)
    nb = hr // rb

    def body(s_ref, g_ref, r_ref, o_ref):
        o_ref[...] = (g_ref[...].astype(F32) + r_ref[...].astype(F32)).astype(BF16)

    return pl.pallas_call(
        body, name=name, out_shape=jax.ShapeDtypeStruct((nq, hr, C), BF16),
        grid_spec=pltpu.PrefetchScalarGridSpec(
            num_scalar_prefetch=1, grid=(nq, nb),
            in_specs=[pl.BlockSpec((1, rb, C), lambda q, i, s: (q, s[0] * nb + i, 0)),
                      pl.BlockSpec((1, rb, C), lambda q, i, s: (q, i, 0))],
            out_specs=pl.BlockSpec((1, rb, C), lambda q, i, s: (q, i, 0))),
        compiler_params=pltpu.CompilerParams(dimension_semantics=("arbitrary", "arbitrary"),
                                             vmem_limit_bytes=VMEM_LIMIT_BYTES),
    )(mc, g, recv)


def chip_sum(s, recv, quad_mc, *, name):
    nq, hr, C = s.shape
    rb = _row_block(hr, 256)
    nb = hr // rb

    def body(q_ref, s_ref, r_ref, o_ref):
        total = s_ref[0].astype(F32)
        for k in range(N_QUAD - 1):
            total = total + r_ref[k].astype(F32)
        o_ref[...] = total

    return pl.pallas_call(
        body, name=name, out_shape=jax.ShapeDtypeStruct((2 * hr, C), F32),
        grid_spec=pltpu.PrefetchScalarGridSpec(
            num_scalar_prefetch=1, grid=(nb,),
            in_specs=[pl.BlockSpec((1, rb, C), lambda i, q: (q[0], i, 0)),
                      pl.BlockSpec((N_QUAD - 1, rb, C), lambda i, q: (0, i, 0))],
            out_specs=pl.BlockSpec((rb, C), lambda i, q: (q[1] * nb + i, 0))),
        compiler_params=pltpu.CompilerParams(dimension_semantics=("arbitrary",), vmem_limit_bytes=VMEM_LIMIT_BYTES),
    )(quad_mc, s, recv)


def _stack_cols(w_full):
    K, N = w_full.shape
    return w_full.reshape(K, N_QUAD, N // N_QUAD).transpose(1, 0, 2)


def _unstack_cols(w4):
    nq, K, n = w4.shape
    return w4.transpose(1, 0, 2).reshape(K, nq * n)


def kernel(x, c, w_ada, b_ada, g_norm1, ffn1_w_gate, ffn1_w_up, ffn1_w_down, g_norm2, w_in, g_sgu_ln, b_sgu_ln, w_spatial, b_spatial, g_q, g_k, attn_sinks, w_branch_a, w_branch_b, w_out, g_norm3, ffn2_w_gate, ffn2_w_up, ffn2_w_down, loss_target, m_w_ada, m_b_ada, m_g_norm1, m_ffn1_w_gate, m_ffn1_w_up, m_ffn1_w_down, m_g_norm2, m_w_in, m_g_sgu_ln, m_b_sgu_ln, m_w_spatial, m_b_spatial, m_g_q, m_g_k, m_attn_sinks, m_w_branch_a, m_w_branch_b, m_w_out, m_g_norm3, m_ffn2_w_gate, m_ffn2_w_up, m_ffn2_w_down, v_w_ada, v_b_ada, v_g_norm1, v_ffn1_w_gate, v_ffn1_w_up, v_ffn1_w_down, v_g_norm2, v_w_in, v_g_sgu_ln, v_b_sgu_ln, v_w_spatial, v_b_spatial, v_g_q, v_g_k, v_attn_sinks, v_w_branch_a, v_w_branch_b, v_w_out, v_g_norm3, v_ffn2_w_gate, v_ffn2_w_up, v_ffn2_w_down):
    B, S, D = x.shape
    T = B * S
    n_mod = b_ada.shape[1] // D
    mx, my, mc = _place()
    quad = 2 * mx + my
    mc_arr = jnp.reshape(mc, (1,)).astype(jnp.int32)
    quad_arr = jnp.reshape(quad, (1,)).astype(jnp.int32)
    quad_mc = jnp.stack([quad, mc]).astype(jnp.int32)
    tm = min(512, S)
    tm_small = min(256, S)
    tm_big = min(1024, S)
    tt_half = min(2048, T)
    cpb = min(4, S // CHUNK)

    xt = x.reshape(T, D)
    tgt = loss_target.reshape(T, D)

    tr = lambda a: jnp.swapaxes(a, 1, 2)
    stack_wg1, stack_wu1 = cast_into_stacks([tr(ffn1_w_gate)[0], tr(ffn1_w_up)[0]], quad_arr, name="stack_gu1")
    gather_wg1, gather_wu1 = gather_comm([stack_wg1]), gather_comm([stack_wu1])
    later = [ffn1_w_down, tr(w_in), w_branch_a, w_branch_b, w_out, tr(ffn2_w_gate), tr(ffn2_w_up), ffn2_w_down]
    stacks = cast_into_stacks([w[0] for w in later], quad_arr, name="stack_rest", comms=[gather_wg1])
    (wg1,) = gather_wg1.bufs_out
    gather_wd1, gather_win = gather_comm([stacks[0]]), gather_comm([stacks[1]])
    gather_abo = gather_comm(stacks[2:5])
    gather_wg3, gather_wu3, gather_wd3 = gather_comm([stacks[5]]), gather_comm([stacks[6]]), gather_comm([stacks[7]])

    c_all = all_gather8(c, name="gather_cond").reshape(N_DEV * B, D)
    n_ada = w_ada.shape[2]
    b_mine = lax.dynamic_slice(b_ada, (0, quad * n_ada), (1, n_ada))
    mods_part = ada_fwd(c_all, w_ada[0], b_mine, name="ada_fwd")
    mods_parts = all_gather8(mods_part, name="gather_mods", comms=[gather_wu1])
    (wu1,) = gather_wu1.bufs_out
    mods = jnp.concatenate([mods_parts[2 * j] for j in range(N_QUAD)], axis=1)
    me = 4 * mx + 2 * my + mc
    mods = lax.dynamic_slice(mods, (me * B, 0), (B, n_mod * D))
    mods = mods.reshape(B, n_mod, 1, D)
    sh1, sc1, ga1, sh2, sc2, ga2, sh3, sc3, ga3 = [(mods, j) for j in range(n_mod)]
    bs_t = b_spatial[0].T
    ws = w_spatial[0]

    xn1 = norm_mod_fwd(xt, g_norm1, sc1, sh1, seq=S, tm=tm, name="norm1")
    g1, u1, a1 = ffn_up(xn1, wg1, wu1, tm=tm_big, name="ffn1_up", comms=[gather_wd1, gather_abo])
    (wd1,), (wa4, wb4, wo4) = gather_wd1.bufs_out, gather_abo.bufs_out
    wa, wb = _unstack_cols(wa4), _unstack_cols(wb4)
    wo = wo4.reshape(D, D)
    h1, f1, xn2 = ffn_down(a1, wd1, xt, ga1, norm=(g_norm2, sc2, sh2), seq=S, tm=tm, name="ffn1_down",
                           comms=[gather_win])
    (win4,) = gather_win.bufs_out
    win = win4.reshape(-1, D)
    w_uv, w_qkv, w_gt = win[0:2 * D_A], win[2 * D_A:2 * D_A + QKV_W], win[2 * D_A + QKV_W:]
    puv, pqkv, pgt = proj_fwd(xn2, [w_uv, w_qkv, w_gt], tm=tm_small, name="proj", comms=[gather_wg3])
    (wg3,) = gather_wg3.bufs_out
    sgu = sgu_fwd(puv, g_sgu_ln, b_sgu_ln, ws, bs_t, cpb=cpb, name="sgu_fwd")
    gq_t, gk_t = jnp.tile(g_q, (1, N_Q)), jnp.tile(g_k, (1, N_KV))
    att = attn_fwd(pqkv, gq_t, gk_t, attn_sinks, seq=S, name="attn_fwd", comms=[gather_wu3])
    (wu3,) = gather_wu3.bufs_out
    ya, yb, merged, mm, h2, xn3 = mixer_out_fwd(sgu, att, pgt, h1, ga2, wa, wb, wo, (g_norm3, sc3, sh3), seq=S,
                                                tm=tm_small, name="mixer_out", comms=[gather_wd3])
    (wd3,) = gather_wd3.bufs_out
    g3, u3, a3 = ffn_up(xn3, wg3, wu3, tm=tm_big, name="ffn2_up")
    dy, f3, lparts = ffn_down(a3, wd3, h2, ga3, target=tgt, seq=S, tm=tm, name="ffn2_down_loss")
    loss_part = jnp.sum(lparts[:, 0, 0])

    def sums_of(pair, tag):
        return [pair_sum(g, r, mc_arr, name=f"pair_sum_{tag}_{p}") for p, (g, r) in enumerate(zip(pair.srcs, pair.lands))]

    def halves_of(chip, tag):
        return [chip_sum(s, r, quad_mc, name=f"chip_sum_{tag}_{p}") for p, (s, r) in enumerate(zip(chip.srcs, chip.lands))]

    dg3, du3, df3, dga3 = ffn_bwd_act(dy, f3, ga3, wd3, g3, u3, seq=S, tm=tm, name="ffn2_act_bwd")
    (dwd3,) = mm_tn([(a3, df3)], tt=T, name="ffn2_dwd")
    pair_wd3 = rs_pair_comm([dwd3])
    dwg3, dwu3 = mm_tn([(dg3, xn3), (du3, xn3)], tt=tt_half, name="ffn2_dwgu", comms=[pair_wd3])
    chip_wd3 = rs_chip_comm(sums_of(pair_wd3, "wd3"))
    pair_gu3 = rs_pair_comm([dwg3, dwu3])
    dh2, dsh3, dsc3, dgn3 = dx_norm_bwd([dg3, du3], [wg3, wu3], h2, dy, g_norm3, sc3, seq=S, tm=tm, name="ffn2_dx",
                                        comms=[chip_wd3, pair_gu3])
    sum_wg3, sum_wu3 = sums_of(pair_gu3, "gu3")
    chip_wg3, chip_wu3 = rs_chip_comm([sum_wg3]), rs_chip_comm([sum_wu3])

    dgt, dya, dyb, dsgu, datt, dm, dga2 = mixer_out_bwd(dh2, mm, ga2, ya, yb, pgt, wa, wb, wo, seq=S, tm=tm_small,
                                                        name="mixer_out_bwd", comms=[chip_wg3])
    (dwo,) = mm_tn([(merged, dm)], tt=tt_half, name="mixer_dwo")
    (dwa,) = mm_tn([(sgu, dya)], tt=tt_half, name="mixer_dwa")
    (dwb,) = mm_tn([(att, dyb)], tt=tt_half, name="mixer_dwb")
    pair_abo = rs_pair_comm([_stack_cols(dwa), _stack_cols(dwb), dwo.reshape(N_QUAD, D // N_QUAD, D)])
    duv, dws, dzs, dgln, dbln = sgu_bwd(puv, dsgu, g_sgu_ln, b_sgu_ln, ws, bs_t, cpb=cpb, name="sgu_bwd",
                                        comms=[pair_abo])
    chip_abo = rs_chip_comm(sums_of(pair_abo, "abo"))
    dqkv, dgq_h, dgk_h, dsk = attn_bwd(pqkv, datt, gq_t, gk_t, attn_sinks, seq=S, name="attn_bwd",
                                       comms=[chip_wu3, chip_abo])
    dgq = dgq_h.reshape(N_Q, HEAD_DIM).sum(axis=0, keepdims=True)
    dgk = dgk_h.reshape(N_KV, HEAD_DIM).sum(axis=0, keepdims=True)
    share3 = rs_share_comm(halves_of(chip_wg3, "wg3") + halves_of(chip_wu3, "wu3") + halves_of(chip_wd3, "wd3"))
    dwin_uv, dwin_qkv = mm_tn([(duv, xn2), (dqkv, xn2)], tt=tt_half, name="mixer_dwin_a", comms=[share3])
    (dwin_gt,) = mm_tn([(dgt, xn2)], tt=tt_half, name="mixer_dwin_b")
    dwin_parts = [dwin_uv, dwin_qkv, dwin_gt]
    r_wg3, r_wu3, r_wd3 = share3.bufs_out
    pair_win = rs_pair_comm([jnp.concatenate(dwin_parts, axis=0).reshape(win4.shape)])
    dh1, dsh2, dsc2, dgn2 = dx_norm_bwd([duv, dqkv, dgt], [w_uv, w_qkv, w_gt], h1, dh2, g_norm2, sc2, seq=S,
                                        tm=tm, name="mixer_dx", comms=[pair_win])
    chip_win = rs_chip_comm(sums_of(pair_win, "win"))

    dg1, du1, df1, dga1 = ffn_bwd_act(dh1, f1, ga1, wd1, g1, u1, seq=S, tm=tm, name="ffn1_act_bwd", comms=[chip_win])
    share_mix = rs_share_comm(halves_of(chip_win, "win") + halves_of(chip_abo, "abo"))

    db_s = dzs.reshape(CHUNK, N_GROUPS, D_A // N_GROUPS).sum(axis=2).T.reshape(1, N_GROUPS * CHUNK)
    tail = jnp.concatenate([db_s, dgq, dgk, dsk[0:1, 0:N_Q], loss_part.reshape(1, 1)], axis=1)
    tail = jnp.pad(tail, ((0, 0), (0, (-tail.shape[1]) % D)))
    lnrow = jnp.concatenate([dgln, dbln], axis=1)
    lnrow = jnp.pad(lnrow, ((0, 0), (0, (-lnrow.shape[1]) % D)))
    packed = jnp.concatenate([dgn2, dgn3, lnrow.reshape(-1, D), tail.reshape(-1, D), dws.reshape(-1, D)], axis=0)
    n_rows = packed.shape[0]
    packed = jnp.pad(packed, ((0, (-n_rows) % 8), (0, 0)))
    gather_small = ag8_comm(packed)

    dwg1, dwu1 = mm_tn([(dg1, xn1), (du1, xn1)], tt=tt_half, name="ffn1_dwgu", comms=[share_mix, gather_small])
    r_win, r_wa, r_wb, r_wo = share_mix.bufs_out
    small = sum8(gather_small.lands[0], name="sum_small")
    pair_gu1 = rs_pair_comm([dwg1, dwu1])
    (dwd1,) = mm_tn([(a1, df1)], tt=T, name="ffn1_dwd", comms=[pair_gu1])
    chip_gu1 = rs_chip_comm(sums_of(pair_gu1, "gu1"))
    pair_wd1 = rs_pair_comm([dwd1])
    dx, dsh1, dsc1, dgn1 = dx_norm_bwd([dg1, du1], [wg1, wu1], xt, dh1, g_norm1, sc1, seq=S, tm=tm, name="ffn1_dx",
                                       comms=[chip_gu1, pair_wd1])
    chip_wd1 = rs_chip_comm(sums_of(pair_wd1, "wd1"))
    share_gu1 = rs_share_comm(halves_of(chip_gu1, "gu1"))

    names = ["w_ada", "b_ada", "g_norm1", "ffn1_w_gate", "ffn1_w_up", "ffn1_w_down", "g_norm2", "w_in", "g_sgu_ln",
             "b_sgu_ln", "w_spatial", "b_spatial", "g_q", "g_k", "attn_sinks", "w_branch_a", "w_branch_b", "w_out",
             "g_norm3", "ffn2_w_gate", "ffn2_w_up", "ffn2_w_down"]
    weights = dict(zip(names, [w_ada, b_ada, g_norm1, ffn1_w_gate, ffn1_w_up, ffn1_w_down, g_norm2, w_in, g_sgu_ln,
                               b_sgu_ln, w_spatial, b_spatial, g_q, g_k, attn_sinks, w_branch_a, w_branch_b, w_out,
                               g_norm3, ffn2_w_gate, ffn2_w_up, ffn2_w_down]))
    m_in = dict(zip(names, [m_w_ada, m_b_ada, m_g_norm1, m_ffn1_w_gate, m_ffn1_w_up, m_ffn1_w_down, m_g_norm2, m_w_in,
                            m_g_sgu_ln, m_b_sgu_ln, m_w_spatial, m_b_spatial, m_g_q, m_g_k, m_attn_sinks, m_w_branch_a,
                            m_w_branch_b, m_w_out, m_g_norm3, m_ffn2_w_gate, m_ffn2_w_up, m_ffn2_w_down]))
    v_in = dict(zip(names, [v_w_ada, v_b_ada, v_g_norm1, v_ffn1_w_gate, v_ffn1_w_up, v_ffn1_w_down, v_g_norm2, v_w_in,
                            v_g_sgu_ln, v_b_sgu_ln, v_w_spatial, v_b_spatial, v_g_q, v_g_k, v_attn_sinks, v_w_branch_a,
                            v_w_branch_b, v_w_out, v_g_norm3, v_ffn2_w_gate, v_ffn2_w_up, v_ffn2_w_down]))
    transposed = ("ffn1_w_gate", "ffn1_w_up", "w_in", "ffn2_w_gate", "ffn2_w_up")
    grads, delta, new_m, new_v = {}, {}, {}, {}

    def update(group, steps, name, comms=()):
        form = lambda nm, a: (tr(a) if nm in transposed else a)[0].reshape(group[nm].shape)
        res = adamw([(form(nm, weights[nm]), g, form(nm, m_in[nm]), form(nm, v_in[nm])) for nm, g in group.items()],
                    steps=steps, name=name, comms=comms)
        back = lambda nm, a: tr(a[None]) if nm in transposed else a.reshape(weights[nm].shape)
        for (nm, g), (d, mn, vn) in zip(group.items(), res):
            grads[nm], delta[nm], new_m[nm], new_v[nm] = back(nm, g), back(nm, d), back(nm, mn), back(nm, vn)

    update({"ffn2_w_gate": r_wg3, "ffn2_w_up": r_wu3, "ffn2_w_down": r_wd3, "w_out": r_wo, "w_branch_a": r_wa,
            "w_branch_b": r_wb}, 8, "adamw_early", comms=[chip_wd1, share_gu1])
    r_wg1, r_wu1 = share_gu1.bufs_out

    dmods = jnp.concatenate([dsh1, dsc1, dga1, dsh2, dsc2, dga2, dsh3, dsc3, dga3], axis=1)
    dmods = jnp.concatenate([dmods, jnp.pad(dgn1, ((0, 0), (0, (n_mod - 1) * D)))], axis=0)
    gathered = all_gather8(dmods, name="gather_dmods")
    dmods_all = gathered[:, 0:B].reshape(N_DEV * B, n_mod * D)
    dmods_mine = lax.dynamic_slice(dmods_all, (0, quad * n_ada), (N_DEV * B, n_ada))
    share_wd1 = rs_share_comm(halves_of(chip_wd1, "wd1"))
    db_ada, dw_ada, g_gn1 = ada_bwd(c_all, dmods_all, dmods_mine, gathered[:, B, 0:D], name="ada_bwd", comms=[share_wd1])
    (r_wd1,) = share_wd1.bufs_out

    ln_rows = lnrow.size // D
    tail_rows = tail.size // D
    r = 2
    g_gn2, g_gn3 = small[0:1], small[1:2]
    ln_flat = small[r:r + ln_rows].reshape(1, -1)
    r += ln_rows
    tail_flat = small[r:r + tail_rows].reshape(1, -1)
    r += tail_rows
    g_ws = small[r:r + dws.size // D].reshape(w_spatial.shape)
    g_gln, g_bln = ln_flat[:, 0:D_A], ln_flat[:, D_A:2 * D_A]
    o = N_GROUPS * CHUNK
    g_bs = tail_flat[:, 0:o].reshape(b_spatial.shape)
    g_gq, g_gk, g_sk = tail_flat[:, o:o + HEAD_DIM], tail_flat[:, o + HEAD_DIM:o + 2 * HEAD_DIM], tail_flat[:, o + 2 * HEAD_DIM:o + 2 * HEAD_DIM + N_Q]
    loss = tail_flat[0, o + 2 * HEAD_DIM + N_Q]

    update({"w_ada": dw_ada}, 16, "adamw_w_ada")
    update({"ffn1_w_gate": r_wg1, "ffn1_w_up": r_wu1, "ffn1_w_down": r_wd1, "w_in": r_win}, 8, "adamw_late")

    update({"b_ada": db_ada, "g_norm1": g_gn1, "g_norm2": g_gn2, "g_norm3": g_gn3, "g_sgu_ln": g_gln,
            "b_sgu_ln": g_bln, "w_spatial": g_ws.reshape(-1, CHUNK), "b_spatial": g_bs.reshape(N_GROUPS, CHUNK),
            "g_q": g_gq, "g_k": g_gk, "attn_sinks": g_sk}, 1, "adamw_small")

    return (loss, dx.reshape(B, S, D), *[grads[nm] for nm in names], *[delta[nm] for nm in names],
            *[new_m[nm] for nm in names], *[new_v[nm] for nm in names])
```

```python
import functools
import math
import operator

import jax
import jax.numpy as jnp
from jax import lax
from jax.experimental import pallas as pl
from jax.experimental.pallas import tpu as pltpu

F32 = jnp.float32
BF16 = jnp.bfloat16
EPS = 1e-6
NEG = -1e30
N_DEV = 8
N_QUAD = 4
D_A = 512
N_GROUPS = 4
CHUNK = 128
HEAD_DIM = 64
N_KV = 2
Q_PER_KV = 4
N_Q = N_KV * Q_PER_KV
D_B = N_Q * HEAD_DIM
QKV_W = D_B + 2 * N_KV * HEAD_DIM
K_OFF = D_B
V_OFF = D_B + N_KV * HEAD_DIM
ATT_SCALE = HEAD_DIM ** -0.5
GELU_K = math.sqrt(2.0 / math.pi)
GELU_C = 0.044715
ADAM_LR, ADAM_B1, ADAM_B2, ADAM_EPS, ADAM_WD, ADAM_STEP = 0.001, 0.9, 0.999, 1e-08, 0.01, 10
VMEM_LIMIT_BYTES = 56 * 1024 * 1024
MESH = pl.DeviceIdType.MESH
NT = (((1,), (1,)), ((), ()))
TN = (((0,), (0,)), ((), ()))
ANY = pl.BlockSpec(memory_space=pl.ANY)


def _dot(a, b):
    return jnp.dot(a, b, preferred_element_type=F32)


def _dg(a, b, dims):
    return lax.dot_general(a, b, dims, preferred_element_type=F32)


def _gelu_parts(x):
    t = jnp.tanh(GELU_K * (x + GELU_C * x * x * x))
    return 0.5 * x * (1.0 + t), t


def _dgelu(x, t):
    return 0.5 * (1.0 + t) + 0.5 * x * (1.0 - t * t) * (GELU_K * (1.0 + 3.0 * GELU_C * x * x))


def _row_block(rows, target):
    b = min(rows, target)
    while rows % b or b % 8:
        b -= 1
    return b


def _mod_spec(mod, tps):
    mods, j = mod
    return pl.BlockSpec((1, None, 1, mods.shape[3]), lambda i: (i // tps, j, 0, 0))


def _resident(a):
    return pl.BlockSpec(a.shape, lambda *_: (0,) * a.ndim, pipeline_mode=pl.Buffered(1))


class Comm:
    def __init__(self, *, srcs=(), bufs=(), land_shapes=(), n_sems, start, finish):
        self.srcs, self.bufs, self.land_shapes = list(srcs), list(bufs), list(land_shapes)
        self.n_sems, self.start, self.finish = n_sems, start, finish
        self.bufs_out, self.lands = None, None


def _call(body, *, name, grid, in_specs, out_specs, out_shape, args, scratch_shapes=(), comms=(), prefetch=()):
    prefetch = list(prefetch)
    in_specs, out_specs, out_shape = list(in_specs), list(out_specs), list(out_shape)
    args, scratch = list(args), list(scratch_shapes)
    n_in, n_out, n_scr = len(args), len(out_shape), len(scratch)
    aliases, layout = {}, []
    for cm in comms:
        i0, o0, s0 = len(args), len(out_shape), len(scratch)
        args += cm.srcs + cm.bufs
        in_specs += [ANY] * (len(cm.srcs) + len(cm.bufs))
        out_shape += [jax.ShapeDtypeStruct(b.shape, b.dtype) for b in cm.bufs] + cm.land_shapes
        out_specs += [ANY] * (len(cm.bufs) + len(cm.land_shapes))
        for j in range(len(cm.bufs)):
            aliases[len(prefetch) + i0 + len(cm.srcs) + j] = o0 + j
        scratch += [pltpu.SemaphoreType.DMA((cm.n_sems,)), pltpu.SemaphoreType.DMA((cm.n_sems,))]
        layout.append((i0, o0, s0))
    n_in_all, n_out_all = len(args), len(out_shape)

    def wrapped(*refs):
        scalars, refs = refs[:len(prefetch)], refs[len(prefetch):]
        ins, outs, scr = refs[:n_in_all], refs[n_in_all:n_in_all + n_out_all], refs[n_in_all + n_out_all:]
        parts = []
        for cm, (i0, o0, s0) in zip(comms, layout):
            nb = len(cm.bufs)
            parts.append((ins[i0:i0 + len(cm.srcs)], outs[o0:o0 + nb], outs[o0 + nb:o0 + nb + len(cm.land_shapes)],
                          scr[s0], scr[s0 + 1]))
        if comms and grid:
            ids = [pl.program_id(d) for d in range(len(grid))]
            first = functools.reduce(operator.and_, [i == 0 for i in ids])
            last = functools.reduce(operator.and_, [i == g - 1 for i, g in zip(ids, grid)])

            @pl.when(first)
            def _():
                for cm, p in zip(comms, parts):
                    cm.start(*p)
        elif comms:
            for cm, p in zip(comms, parts):
                cm.start(*p)
        if body is not None:
            body(*scalars, *ins[:n_in], *outs[:n_out], *scr[:n_scr])
        if comms and grid:
            @pl.when(last)
            def _():
                for cm, p in zip(comms, parts):
                    cm.finish(*p)
        elif comms:
            for cm, p in zip(comms, parts):
                cm.finish(*p)

    if prefetch:
        kwargs = dict(grid_spec=pltpu.PrefetchScalarGridSpec(
            num_scalar_prefetch=len(prefetch), grid=grid, in_specs=in_specs, out_specs=out_specs, scratch_shapes=scratch))
    else:
        kwargs = dict(in_specs=in_specs, out_specs=out_specs, scratch_shapes=scratch, **(dict(grid=grid) if grid else {}))
    sem = ("arbitrary",) * len(grid)
    res = pl.pallas_call(
        wrapped, name=name, out_shape=out_shape, input_output_aliases=aliases,
        compiler_params=pltpu.CompilerParams(dimension_semantics=sem, vmem_limit_bytes=VMEM_LIMIT_BYTES), **kwargs,
    )(*prefetch, *args)
    for cm, (i0, o0, s0) in zip(comms, layout):
        nb = len(cm.bufs)
        cm.bufs_out = list(res[o0:o0 + nb])
        cm.lands = list(res[o0 + nb:o0 + nb + len(cm.land_shapes)])
    return list(res[:n_out])


def run_comms(comms, *, name):
    _call(None, name=name, grid=(), in_specs=[], out_specs=[], out_shape=[], args=[], comms=comms)


def norm_mod_fwd(h, g, sc, sh, *, seq, tm, name, comms=()):
    T, D = h.shape
    B, tps = T // seq, seq // tm

    def body(h_ref, g_ref, sc_ref, sh_ref, o_ref):
        x = h_ref[...]
        r = lax.rsqrt(jnp.mean(x * x, axis=-1, keepdims=True) + EPS)
        y = x * r * g_ref[...]
        o_ref[...] = (y * (1.0 + sc_ref[0]) + sh_ref[0]).astype(o_ref.dtype)

    return _call(
        body, name=name, grid=(T // tm,),
        in_specs=[pl.BlockSpec((tm, D), lambda i: (i, 0)), pl.BlockSpec((1, D), lambda i: (0, 0)),
                  _mod_spec(sc, tps), _mod_spec(sh, tps)],
        out_specs=[pl.BlockSpec((tm, D), lambda i: (i, 0))], out_shape=[jax.ShapeDtypeStruct((T, D), BF16)],
        args=[h, g, sc[0], sh[0]], comms=comms)[0]


def ffn_up(xn, wg, wu, *, tm, name, comms=()):
    T, D = xn.shape
    nq, fs, _ = wg.shape

    def body(x_ref, wg_ref, wu_ref, s_ref, q_ref, a_ref):
        x = x_ref[...]
        g = _dg(x, wg_ref[0], NT)
        u = _dg(x, wu_ref[0], NT)
        sg = jax.nn.sigmoid(g)
        s = g * sg
        s_ref[0] = s.astype(BF16)
        q_ref[0] = (u * (sg + s * (1.0 - sg))).astype(BF16)
        a_ref[0] = (s * u).astype(BF16)

    w_spec = pl.BlockSpec((1, fs, D), lambda q, i: (q, 0, 0))
    o_spec = pl.BlockSpec((1, tm, fs), lambda q, i: (q, i, 0))
    o_shape = jax.ShapeDtypeStruct((nq, T, fs), BF16)
    return _call(
        body, name=name, grid=(nq, T // tm),
        in_specs=[pl.BlockSpec((tm, D), lambda q, i: (i, 0)), w_spec, w_spec],
        out_specs=[o_spec, o_spec, o_spec], out_shape=[o_shape, o_shape, o_shape], args=[xn, wg, wu], comms=comms)


def _norm_mod(y, g_ref, sc_ref, sh_ref):
    nx, _ = _rms_parts(y)
    return ((nx * g_ref[...]) * (1.0 + sc_ref[0]) + sh_ref[0]).astype(BF16)


def ffn_down(a, wd, hin, ga, *, target=None, norm=None, seq, tm, name, comms=()):
    nq, T, fs = a.shape
    D = wd.shape[-1]
    B, tps, nt = T // seq, seq // tm, T // tm

    def body(a_ref, wd_ref, h_ref, ga_ref, *rest):
        rest = list(rest)
        t_ref = rest.pop(0) if target is not None else None
        norm_refs = [rest.pop(0) for _ in range(3)] if norm is not None else None
        o_ref, f_ref = rest[0], rest[1]
        f = _dot(a_ref[0], wd_ref[0])
        for q in range(1, nq):
            f = f + _dot(a_ref[q], wd_ref[q])
        f_ref[...] = f.astype(BF16)
        y = h_ref[...] + (0.5 * ga_ref[0]) * f
        if target is not None:
            e = y - t_ref[...]
            o_ref[...] = e * (1.0 / D)
            rest[2][...] = jnp.full(rest[2].shape, 0.5 * jnp.sum(e * e) * (1.0 / D), F32)
        else:
            o_ref[...] = y
        if norm is not None:
            rest[2][...] = _norm_mod(y, *norm_refs)

    tile = pl.BlockSpec((tm, D), lambda i: (i, 0))
    in_specs = [pl.BlockSpec((nq, tm, fs), lambda i: (0, i, 0)), _resident(wd), tile, _mod_spec(ga, tps)]
    out_specs = [tile, tile]
    out_shape = [jax.ShapeDtypeStruct((T, D), F32), jax.ShapeDtypeStruct((T, D), BF16)]
    args = [a, wd, hin, ga[0]]
    if target is not None:
        in_specs.append(tile)
        args.append(target)
        out_specs.append(pl.BlockSpec((1, 8, 128), lambda i: (i, 0, 0)))
        out_shape.append(jax.ShapeDtypeStruct((nt, 8, 128), F32))
    if norm is not None:
        in_specs += [pl.BlockSpec((1, D), lambda i: (0, 0)), _mod_spec(norm[1], tps), _mod_spec(norm[2], tps)]
        args += [norm[0], norm[1][0], norm[2][0]]
        out_specs.append(tile)
        out_shape.append(jax.ShapeDtypeStruct((T, D), BF16))
    return _call(body, name=name, grid=(nt,), in_specs=in_specs, out_specs=out_specs, out_shape=out_shape, args=args,
                 comms=comms)


def proj_fwd(xn, ws, *, tm, name, comms=()):
    T, D = xn.shape
    n = len(ws)

    def body(*refs):
        x = refs[0][...]
        for j in range(n):
            refs[1 + n + j][...] = _dg(x, refs[1 + j][...], NT).astype(BF16)

    return _call(
        body, name=name, grid=(T // tm,),
        in_specs=[pl.BlockSpec((tm, D), lambda i: (i, 0))] + [pl.BlockSpec(w.shape, lambda i: (0, 0)) for w in ws],
        out_specs=[pl.BlockSpec((tm, w.shape[0]), lambda i: (i, 0)) for w in ws],
        out_shape=[jax.ShapeDtypeStruct((T, w.shape[0]), BF16) for w in ws], args=[xn, *ws], comms=comms)


def _layer_norm_parts(v):
    mu = jnp.mean(v, axis=-1, keepdims=True)
    d = v - mu
    rstd = lax.rsqrt(jnp.mean(d * d, axis=-1, keepdims=True) + EPS)
    return d * rstd, rstd


def _causal():
    row = lax.broadcasted_iota(jnp.int32, (CHUNK, CHUNK), 0)
    col = lax.broadcasted_iota(jnp.int32, (CHUNK, CHUNK), 1)
    return row >= col


def sgu_fwd(puv, gln, bln, ws, bs_t, *, cpb, name, comms=()):
    T = puv.shape[0]
    gd = D_A // N_GROUPS
    tm = cpb * CHUNK

    def body(p_ref, gln_ref, bln_ref, ws_ref, bs_ref, o_ref):
        causal = _causal()
        wm = [jnp.where(causal, ws_ref[g], 0.0).astype(BF16) for g in range(N_GROUPS)]
        for j in range(cpb):
            rows = slice(j * CHUNK, (j + 1) * CHUNK)
            u, _ = _gelu_parts(p_ref[rows, 0:D_A].astype(F32))
            vv, _ = _gelu_parts(p_ref[rows, D_A:2 * D_A].astype(F32))
            vhat, _ = _layer_norm_parts(vv)
            vn = (vhat * gln_ref[...] + bln_ref[...]).astype(BF16)
            for g in range(N_GROUPS):
                cols = slice(g * gd, (g + 1) * gd)
                z = _dot(wm[g], vn[:, cols]) + bs_ref[:, g:g + 1]
                o_ref[rows, cols] = (u[:, cols] * z).astype(BF16)

    full = lambda a: pl.BlockSpec(a.shape, lambda i: (0,) * a.ndim)
    return _call(
        body, name=name, grid=(T // tm,),
        in_specs=[pl.BlockSpec((tm, 2 * D_A), lambda i: (i, 0)), full(gln), full(bln), full(ws), full(bs_t)],
        out_specs=[pl.BlockSpec((tm, D_A), lambda i: (i, 0))], out_shape=[jax.ShapeDtypeStruct((T, D_A), BF16)],
        args=[puv, gln, bln, ws, bs_t], comms=comms)[0]


def _rms_parts(x):
    r = lax.rsqrt(jnp.mean(x * x, axis=-1, keepdims=True) + EPS)
    return x * r, r


KV_W = Q_PER_KV * HEAD_DIM
KV2 = N_KV * HEAD_DIM
STACK = Q_PER_KV * CHUNK


def _iota(shape, dim):
    return lax.broadcasted_iota(jnp.int32, shape, dim)


def _head_mean_matrix(n):
    return jnp.where((_iota((n, n), 0) >> 6) == (_iota((n, n), 1) >> 6), 1.0 / HEAD_DIM, 0.0).astype(BF16)


def _repeat_matrix(h):
    return jnp.where(_iota((KV2, KV_W), 0) == h * HEAD_DIM + (_iota((KV2, KV_W), 1) & (HEAD_DIM - 1)), 1.0, 0.0).astype(BF16)


def _fold_matrix(h):
    j, l = _iota((KV_W, KV2), 0), _iota((KV_W, KV2), 1)
    return jnp.where(((j & (HEAD_DIM - 1)) == (l & (HEAD_DIM - 1))) & ((l >> 6) == h), 1.0, 0.0).astype(BF16)


def _dot_split(x, m):
    hi = x.astype(BF16)
    lo = (x - hi.astype(F32)).astype(BF16)
    return _dot(hi, m) + _dot(lo, m)


def _head_rms(x, mean_matrix):
    r = lax.rsqrt(_dot_split(x * x, mean_matrix) + EPS)
    return x * r, r


def _stack_heads(x):
    head = _iota(x.shape, 1) >> 6
    return jnp.concatenate([jnp.where(head == g, x, 0.0).astype(BF16) for g in range(Q_PER_KV)], axis=0)


def _unstack_heads(y):
    head = _iota((CHUNK, KV_W), 1) >> 6
    out = jnp.where(head == 0, y[0:CHUNK], 0.0)
    for g in range(1, Q_PER_KV):
        out = out + jnp.where(head == g, y[g * CHUNK:(g + 1) * CHUNK], 0.0)
    return out


SOFTMAX_ROWS = 32


def _fill_mask_bias(bias_ref):
    qi = _iota((CHUNK, 2 * CHUNK), 0)
    kj = _iota((CHUNK, 2 * CHUNK), 1)
    diff = qi + CHUNK - kj
    window = (diff >= 0) & (diff < CHUNK)
    bias_ref[0] = jnp.where(window & (kj >= CHUNK), 0.0, NEG)
    bias_ref[1] = jnp.where(window, 0.0, NEG)


def _softmax_rows(s_ref, bias_ref, first, sink_ref, h, c):
    rows = pl.ds(pl.multiple_of(c * SOFTMAX_ROWS, SOFTMAX_ROWS), SOFTMAX_ROWS)
    in_block = pl.ds(pl.multiple_of((c % (CHUNK // SOFTMAX_ROWS)) * SOFTMAX_ROWS, SOFTMAX_ROWS), SOFTMAX_ROWS)
    sink = sink_ref[0, h * Q_PER_KV + c // (CHUNK // SOFTMAX_ROWS)]
    s = s_ref[rows, :] + bias_ref[first, in_block, :]
    m = jnp.maximum(jnp.max(s, axis=-1, keepdims=True), sink)
    p = jnp.exp(s - m)
    es = jnp.exp(sink - m)
    inv = 1.0 / (jnp.sum(p, axis=-1, keepdims=True) + es)
    return rows, p * inv, es * inv


def _fill_keys(p_ref, gk_ref, krep, vrep, seq):
    mean_k = _head_mean_matrix(KV2)
    reps = [_repeat_matrix(h) for h in range(N_KV)]
    for h in range(N_KV):
        krep[h][0:CHUNK, :] = jnp.zeros((CHUNK, KV_W), BF16)
        vrep[h][0:CHUNK, :] = jnp.zeros((CHUNK, KV_W), BF16)
    rows = min(seq, 4 * CHUNK)

    def piece(j, carry):
        r0 = pl.multiple_of(j * rows, CHUNK)
        nk, _ = _head_rms(p_ref[pl.ds(r0, rows), K_OFF:K_OFF + KV2].astype(F32), mean_k)
        kn = (nk * gk_ref[...]).astype(BF16)
        v = p_ref[pl.ds(r0, rows), V_OFF:V_OFF + KV2].astype(BF16)
        r1 = pl.multiple_of(r0 + CHUNK, CHUNK)
        for h in range(N_KV):
            krep[h][pl.ds(r1, rows), :] = _dot(kn, reps[h]).astype(BF16)
            vrep[h][pl.ds(r1, rows), :] = _dot(v, reps[h]).astype(BF16)
        return carry

    lax.fori_loop(0, seq // rows, piece, 0)


def attn_fwd(pqkv, gq_t, gk_t, sinks, *, seq, name, comms=()):
    T = pqkv.shape[0]
    nb = seq // CHUNK

    def body(p_ref, gq_ref, gk_ref, sink_ref, o_ref, k0, k1, v0, v1, bias_ref, s_ref, pr_ref):
        krep, vrep = [k0, k1], [v0, v1]
        _fill_keys(p_ref, gk_ref, krep, vrep, seq)
        _fill_mask_bias(bias_ref)
        mean_q = _head_mean_matrix(D_B)

        def block(i, carry):
            r0 = pl.multiple_of(i * CHUNK, CHUNK)
            first = jnp.minimum(i, 1)
            nq, _ = _head_rms(p_ref[pl.ds(r0, CHUNK), 0:D_B].astype(F32), mean_q)
            qn = nq * (gq_ref[...] * ATT_SCALE)
            for h in range(N_KV):
                qs = _stack_heads(qn[:, h * KV_W:(h + 1) * KV_W])
                s_ref[...] = _dg(qs, krep[h][pl.ds(r0, 2 * CHUNK), :], NT)

                def rows_of(c, carry2):
                    rows, probs, _ = _softmax_rows(s_ref, bias_ref, first, sink_ref, h, c)
                    pr_ref[rows, :] = probs.astype(BF16)
                    return carry2

                lax.fori_loop(0, STACK // SOFTMAX_ROWS, rows_of, 0, unroll=True)
                out = _unstack_heads(_dot(pr_ref[...], vrep[h][pl.ds(r0, 2 * CHUNK), :]))
                o_ref[pl.ds(r0, CHUNK), h * KV_W:(h + 1) * KV_W] = out.astype(BF16)
            return carry

        lax.fori_loop(0, nb, block, 0)

    return _call(
        body, name=name, grid=(T // seq,),
        in_specs=[pl.BlockSpec((seq, QKV_W), lambda b: (b, 0)), pl.BlockSpec(gq_t.shape, lambda b: (0, 0)),
                  pl.BlockSpec(gk_t.shape, lambda b: (0, 0)), pl.BlockSpec(memory_space=pltpu.SMEM)],
        out_specs=[pl.BlockSpec((seq, D_B), lambda b: (b, 0))], out_shape=[jax.ShapeDtypeStruct((T, D_B), BF16)],
        scratch_shapes=[pltpu.VMEM((seq + CHUNK, KV_W), BF16)] * 4
        + [pltpu.VMEM((2, CHUNK, 2 * CHUNK), F32), pltpu.VMEM((STACK, 2 * CHUNK), F32), pltpu.VMEM((STACK, 2 * CHUNK), BF16)],
        args=[pqkv, gq_t, gk_t, sinks], comms=comms)[0]


def mixer_out_fwd(sgu, att, pg, hin, ga, wa, wb, wo, norm, *, seq, tm, name, comms=()):
    T, D = hin.shape
    B, tps = T // seq, seq // tm

    def body(s_ref, t_ref, pg_ref, h_ref, ga_ref, wa_ref, wb_ref, wo_ref, g_ref, sc_ref, sh_ref,
             ya_ref, yb_ref, mg_ref, m_ref, ho_ref, xn_ref):
        ya = _dot(s_ref[...], wa_ref[...])
        yb = _dot(t_ref[...], wb_ref[...])
        merged = (jax.nn.sigmoid(pg_ref[:, 0:D].astype(F32)) * ya
                  + jax.nn.sigmoid(pg_ref[:, D:2 * D].astype(F32)) * yb)
        mg = merged.astype(BF16)
        m = _dot(mg, wo_ref[...])
        ya_ref[...] = ya.astype(BF16)
        yb_ref[...] = yb.astype(BF16)
        mg_ref[...] = mg
        m_ref[...] = m.astype(BF16)
        y = h_ref[...] + ga_ref[0] * m
        ho_ref[...] = y
        xn_ref[...] = _norm_mod(y, g_ref, sc_ref, sh_ref)

    tile = lambda w: pl.BlockSpec((tm, w), lambda i: (i, 0))
    b16 = jax.ShapeDtypeStruct((T, D), BF16)
    return _call(
        body, name=name, grid=(T // tm,),
        in_specs=[tile(D_A), tile(D_B), tile(2 * D), tile(D), _mod_spec(ga, tps), _resident(wa), _resident(wb),
                  _resident(wo), pl.BlockSpec((1, D), lambda i: (0, 0)), _mod_spec(norm[1], tps), _mod_spec(norm[2], tps)],
        out_specs=[tile(D)] * 6, out_shape=[b16, b16, b16, b16, jax.ShapeDtypeStruct((T, D), F32), b16],
        args=[sgu, att, pg, hin, ga[0], wa, wb, wo, norm[0], norm[1][0], norm[2][0]], comms=comms)


def ffn_bwd_act(dh, f, ga, wd, s, q, *, seq, tm, name, comms=()):
    T, D = dh.shape
    nq, fs, _ = wd.shape
    B, tps = T // seq, seq // tm

    def body(dh_ref, f_ref, ga_ref, wd_ref, s_ref, q_ref, dg_ref, du_ref, df_ref, dga_ref):
        i = pl.program_id(0)
        d = dh_ref[...]
        df = ((0.5 * ga_ref[0]) * d).astype(BF16)
        df_ref[...] = df
        part = 0.5 * jnp.sum(d * f_ref[...].astype(F32), axis=0, keepdims=True)
        for j in range(nq):
            da = _dg(df, wd_ref[j], NT)
            du_ref[j] = (da * s_ref[j].astype(F32)).astype(BF16)
            dg_ref[j] = (da * q_ref[j].astype(F32)).astype(BF16)

        @pl.when(i % tps == 0)
        def _():
            dga_ref[0] = part

        @pl.when(i % tps != 0)
        def _():
            dga_ref[0] += part

    tile = pl.BlockSpec((tm, D), lambda i: (i, 0))
    per_seq = pl.BlockSpec((1, 1, D), lambda i: (i // tps, 0, 0))
    act = pl.BlockSpec((nq, tm, fs), lambda i: (0, i, 0))
    o_shape = jax.ShapeDtypeStruct((nq, T, fs), BF16)
    dg, du, df, dga = _call(
        body, name=name, grid=(T // tm,), in_specs=[tile, tile, _mod_spec(ga, tps), _resident(wd), act, act],
        out_specs=[act, act, tile, per_seq],
        out_shape=[o_shape, o_shape, jax.ShapeDtypeStruct((T, D), BF16), jax.ShapeDtypeStruct((B, 1, D), F32)],
        args=[dh, f, ga[0], wd, s, q], comms=comms)
    return dg, du, df, dga.reshape(B, D)


def mm_tn(pairs, *, tt, name, comms=()):
    n = len(pairs)
    flat = []
    for pair in pairs:
        for a in pair:
            if not any(a is b for b in flat):
                flat.append(a)
    where = lambda a: [k for k, b in enumerate(flat) if a is b][0]
    nq = max([a.shape[0] for a in flat if a.ndim == 3] + [1])
    T = flat[0].shape[-2]
    tt = min(tt, T)
    nt = T // tt
    stacked = [x.ndim == 3 or y.ndim == 3 for x, y in pairs]

    def body(*refs):
        in_refs, o_refs, accs = refs[:len(flat)], refs[len(flat):len(flat) + n], refs[len(flat) + n:]
        t = pl.program_id(1)
        value = lambda a: in_refs[where(a)][0] if a.ndim == 3 else in_refs[where(a)][...]

        def put(j, v):
            if stacked[j]:
                o_refs[j][0] = v.astype(BF16)
            else:
                o_refs[j][...] = v.astype(BF16)

        for j, (x, y) in enumerate(pairs):
            part = _dg(value(x), value(y), TN)
            if nt == 1:
                put(j, part)
                continue

            @pl.when(t == 0)
            def _():
                accs[j][...] = part

            @pl.when(t != 0)
            def _():
                accs[j][...] += part

        if nt > 1:
            @pl.when(t == nt - 1)
            def _():
                for j in range(n):
                    put(j, accs[j][...])

    def spec(a):
        if a.ndim == 3:
            return pl.BlockSpec((1, tt, a.shape[2]), lambda q, t: (q, t, 0))
        return pl.BlockSpec((tt, a.shape[1]), lambda q, t: (t, 0))

    out_specs, out_shape = [], []
    for j, (x, y) in enumerate(pairs):
        K, N = x.shape[-1], y.shape[-1]
        if stacked[j]:
            out_specs.append(pl.BlockSpec((1, K, N), lambda q, t: (q, 0, 0)))
            out_shape.append(jax.ShapeDtypeStruct((nq, K, N), BF16))
        else:
            out_specs.append(pl.BlockSpec((K, N), lambda q, t: (0, 0)))
            out_shape.append(jax.ShapeDtypeStruct((K, N), BF16))
    return _call(
        body, name=name, grid=(nq, nt), in_specs=[spec(a) for a in flat],
        out_specs=out_specs, out_shape=out_shape,
        scratch_shapes=[pltpu.VMEM((x.shape[-1], y.shape[-1]), F32) for x, y in pairs] if nt > 1 else [],
        args=flat, comms=comms)


def dx_norm_bwd(dys, ws, hin, dh_out, g, sc, *, seq, tm, name, comms=()):
    T, D = hin.shape
    B, tps = T // seq, seq // tm
    n = len(dys)

    def body(*refs):
        dy_refs, w_refs = refs[:n], refs[n:2 * n]
        h_ref, dho_ref, g_ref, sc_ref, dhi_ref, dsh_ref, dsc_ref, dgn_ref = refs[2 * n:]
        i = pl.program_id(0)
        dxn = None
        for j in range(n):
            if dys[j].ndim == 3:
                for q in range(dys[j].shape[0]):
                    part = _dot(dy_refs[j][q], w_refs[j][q])
                    dxn = part if dxn is None else dxn + part
            else:
                part = _dot(dy_refs[j][...], w_refs[j][...])
                dxn = part if dxn is None else dxn + part
        x = h_ref[...]
        nx, r = _rms_parts(x)
        gain = g_ref[...]
        one_sc = 1.0 + sc_ref[0]
        dsh = jnp.sum(dxn, axis=0, keepdims=True)
        dsc = jnp.sum(dxn * (nx * gain), axis=0, keepdims=True)
        dgn = jnp.sum(dxn * one_sc * nx, axis=0, keepdims=True)
        dn = dxn * one_sc * gain
        dhi_ref[...] = dho_ref[...] + r * (dn - nx * jnp.mean(dn * nx, axis=-1, keepdims=True))

        @pl.when(i % tps == 0)
        def _():
            dsh_ref[0] = dsh
            dsc_ref[0] = dsc

        @pl.when(i % tps != 0)
        def _():
            dsh_ref[0] += dsh
            dsc_ref[0] += dsc

        @pl.when(i == 0)
        def _():
            dgn_ref[...] = dgn

        @pl.when(i != 0)
        def _():
            dgn_ref[...] += dgn

    def dy_spec(a):
        if a.ndim == 3:
            return pl.BlockSpec((a.shape[0], tm, a.shape[2]), lambda i: (0, i, 0))
        return pl.BlockSpec((tm, a.shape[1]), lambda i: (i, 0))

    tile = pl.BlockSpec((tm, D), lambda i: (i, 0))
    per_seq = pl.BlockSpec((1, 1, D), lambda i: (i // tps, 0, 0))
    row = pl.BlockSpec((1, D), lambda i: (0, 0))
    dhi, dsh, dsc, dgn = _call(
        body, name=name, grid=(T // tm,),
        in_specs=[dy_spec(a) for a in dys] + [_resident(w) for w in ws] + [tile, tile, row, _mod_spec(sc, tps)],
        out_specs=[tile, per_seq, per_seq, row],
        out_shape=[jax.ShapeDtypeStruct((T, D), F32), jax.ShapeDtypeStruct((B, 1, D), F32),
                   jax.ShapeDtypeStruct((B, 1, D), F32), jax.ShapeDtypeStruct((1, D), F32)],
        args=[*dys, *ws, hin, dh_out, g, sc[0]], comms=comms)
    return dhi, dsh.reshape(B, D), dsc.reshape(B, D), dgn


def mixer_out_bwd(dh, m, ga, ya, yb, pg, wa, wb, wo, *, seq, tm, name, comms=()):
    T, D = dh.shape
    B, tps = T // seq, seq // tm

    def body(dh_ref, m_ref, ga_ref, ya_ref, yb_ref, pg_ref, wa_ref, wb_ref, wo_ref,
             dg_ref, dya_ref, dyb_ref, ds_ref, dt_ref, dm_ref, dga_ref):
        i = pl.program_id(0)
        d = dh_ref[...]
        dm = (ga_ref[0] * d).astype(BF16)
        dm_ref[...] = dm
        part = jnp.sum(d * m_ref[...].astype(F32), axis=0, keepdims=True)

        @pl.when(i % tps == 0)
        def _():
            dga_ref[0] = part

        @pl.when(i % tps != 0)
        def _():
            dga_ref[0] += part

        dmg = _dg(dm, wo_ref[...], NT)
        sa = jax.nn.sigmoid(pg_ref[:, 0:D].astype(F32))
        sb = jax.nn.sigmoid(pg_ref[:, D:2 * D].astype(F32))
        dg_ref[:, 0:D] = (dmg * ya_ref[...].astype(F32) * (sa * (1.0 - sa))).astype(BF16)
        dg_ref[:, D:2 * D] = (dmg * yb_ref[...].astype(F32) * (sb * (1.0 - sb))).astype(BF16)
        dya = (dmg * sa).astype(BF16)
        dyb = (dmg * sb).astype(BF16)
        dya_ref[...] = dya
        dyb_ref[...] = dyb
        ds_ref[...] = _dg(dya, wa_ref[...], NT).astype(BF16)
        dt_ref[...] = _dg(dyb, wb_ref[...], NT).astype(BF16)

    tile = lambda w: pl.BlockSpec((tm, w), lambda i: (i, 0))
    per_seq = pl.BlockSpec((1, 1, D), lambda i: (i // tps, 0, 0))
    dg, dya, dyb, ds, dt, dm, dga = _call(
        body, name=name, grid=(T // tm,),
        in_specs=[tile(D), tile(D), _mod_spec(ga, tps), tile(D), tile(D), tile(2 * D), _resident(wa), _resident(wb),
                  _resident(wo)],
        out_specs=[tile(2 * D), tile(D), tile(D), tile(D_A), tile(D_B), tile(D), per_seq],
        out_shape=[jax.ShapeDtypeStruct((T, 2 * D), BF16), jax.ShapeDtypeStruct((T, D), BF16),
                   jax.ShapeDtypeStruct((T, D), BF16), jax.ShapeDtypeStruct((T, D_A), BF16),
                   jax.ShapeDtypeStruct((T, D_B), BF16), jax.ShapeDtypeStruct((T, D), BF16),
                   jax.ShapeDtypeStruct((B, 1, D), F32)],
        args=[dh, m, ga[0], ya, yb, pg, wa, wb, wo], comms=comms)
    return dg, dya, dyb, ds, dt, dm, dga.reshape(B, D)


def sgu_bwd(puv, dsgu, gln, bln, ws, bs_t, *, cpb, name, comms=()):
    T = puv.shape[0]
    gd = D_A // N_GROUPS
    tm = cpb * CHUNK

    def body(p_ref, ds_ref, gln_ref, bln_ref, ws_ref, bs_ref, d_ref, dws_ref, dz_ref, dgl_ref, dbl_ref):
        i = pl.program_id(0)
        causal = _causal()
        wf = [jnp.where(causal, ws_ref[g], 0.0) for g in range(N_GROUPS)]
        wm = [w.astype(BF16) for w in wf]
        wt = [w.T.astype(BF16) for w in wf]
        dws = [jnp.zeros((CHUNK, CHUNK), F32) for _ in range(N_GROUPS)]
        dzs = jnp.zeros((CHUNK, D_A), F32)
        dgl = jnp.zeros((1, D_A), F32)
        dbl = jnp.zeros((1, D_A), F32)
        for j in range(cpb):
            rows = slice(j * CHUNK, (j + 1) * CHUNK)
            au = p_ref[rows, 0:D_A].astype(F32)
            av = p_ref[rows, D_A:2 * D_A].astype(F32)
            u, tu = _gelu_parts(au)
            vv, tv = _gelu_parts(av)
            vhat, rstd = _layer_norm_parts(vv)
            vn = (vhat * gln_ref[...] + bln_ref[...]).astype(BF16)
            dsg = ds_ref[rows, :].astype(F32)
            dz = dsg * u
            dzb = dz.astype(BF16)
            zs, dvns = [], []
            for g in range(N_GROUPS):
                cols = slice(g * gd, (g + 1) * gd)
                zs.append(_dot(wm[g], vn[:, cols]) + bs_ref[:, g:g + 1])
                dws[g] = dws[g] + _dg(dzb[:, cols], vn[:, cols], NT)
                dvns.append(_dot(wt[g], dzb[:, cols]))
            z = jnp.concatenate(zs, axis=1)
            dvn = jnp.concatenate(dvns, axis=1)
            d_ref[rows, 0:D_A] = (dsg * z * _dgelu(au, tu)).astype(BF16)
            dvh = dvn * gln_ref[...]
            dvv = rstd * (dvh - jnp.mean(dvh, axis=-1, keepdims=True)
                          - vhat * jnp.mean(dvh * vhat, axis=-1, keepdims=True))
            d_ref[rows, D_A:2 * D_A] = (dvv * _dgelu(av, tv)).astype(BF16)
            dzs = dzs + dz
            dgl = dgl + jnp.sum(dvn * vhat, axis=0, keepdims=True)
            dbl = dbl + jnp.sum(dvn, axis=0, keepdims=True)

        @pl.when(i == 0)
        def _():
            for g in range(N_GROUPS):
                dws_ref[g] = jnp.where(causal, dws[g], 0.0)
            dz_ref[...] = dzs
            dgl_ref[...] = dgl
            dbl_ref[...] = dbl

        @pl.when(i != 0)
        def _():
            for g in range(N_GROUPS):
                dws_ref[g] += jnp.where(causal, dws[g], 0.0)
            dz_ref[...] += dzs
            dgl_ref[...] += dgl
            dbl_ref[...] += dbl

    full = lambda a: pl.BlockSpec(a.shape, lambda i: (0,) * a.ndim)
    acc = lambda s: pl.BlockSpec(s, lambda i: (0,) * len(s))
    return _call(
        body, name=name, grid=(T // tm,),
        in_specs=[pl.BlockSpec((tm, 2 * D_A), lambda i: (i, 0)), pl.BlockSpec((tm, D_A), lambda i: (i, 0)),
                  full(gln), full(bln), full(ws), full(bs_t)],
        out_specs=[pl.BlockSpec((tm, 2 * D_A), lambda i: (i, 0)), acc((N_GROUPS, CHUNK, CHUNK)), acc((CHUNK, D_A)),
                   acc((1, D_A)), acc((1, D_A))],
        out_shape=[jax.ShapeDtypeStruct((T, 2 * D_A), BF16), jax.ShapeDtypeStruct((N_GROUPS, CHUNK, CHUNK), F32),
                   jax.ShapeDtypeStruct((CHUNK, D_A), F32), jax.ShapeDtypeStruct((1, D_A), F32),
                   jax.ShapeDtypeStruct((1, D_A), F32)],
        args=[puv, dsgu, gln, bln, ws, bs_t], comms=comms)


def attn_bwd(pqkv, datt, gq_t, gk_t, sinks, *, seq, name, comms=()):
    T = pqkv.shape[0]
    nb = seq // CHUNK

    def body(p_ref, do_ref, gq_ref, gk_ref, sink_ref, d_ref, dgq_ref, dgk_ref, dsk_ref,
             k0, k1, v0, v1, dk0, dk1, dv0, dv1, dgq_s, dsk_s, bias_ref, s_ref, dp_ref, pr_ref, ds_ref):
        b = pl.program_id(0)
        krep, vrep, dkacc, dvacc = [k0, k1], [v0, v1], [dk0, dk1], [dv0, dv1]
        _fill_keys(p_ref, gk_ref, krep, vrep, seq)
        _fill_mask_bias(bias_ref)
        for h in range(N_KV):
            dkacc[h][...] = jnp.zeros_like(dkacc[h])
            dvacc[h][...] = jnp.zeros_like(dvacc[h])
        dgq_s[...] = jnp.zeros_like(dgq_s)
        dsk_s[...] = jnp.zeros_like(dsk_s)
        mean_q = _head_mean_matrix(D_B)
        lane = _iota((1, 128), 1)

        def block(i, carry):
            r0 = pl.multiple_of(i * CHUNK, CHUNK)
            first = jnp.minimum(i, 1)
            nq, rq = _head_rms(p_ref[pl.ds(r0, CHUNK), 0:D_B].astype(F32), mean_q)
            qn = nq * (gq_ref[...] * ATT_SCALE)
            dqn_parts = []
            for h in range(N_KV):
                cols = slice(h * KV_W, (h + 1) * KV_W)
                qs = _stack_heads(qn[:, cols])
                kk = krep[h][pl.ds(r0, 2 * CHUNK), :]
                dos = _stack_heads(do_ref[pl.ds(r0, CHUNK), cols].astype(F32))
                s_ref[...] = _dg(qs, kk, NT)
                dp_ref[...] = _dg(dos, vrep[h][pl.ds(r0, 2 * CHUNK), :], NT)

                def rows_of(c, carry2):
                    rows, probs, psink = _softmax_rows(s_ref, bias_ref, first, sink_ref, h, c)
                    dp = dp_ref[rows, :]
                    dr = jnp.sum(probs * dp, axis=-1, keepdims=True)
                    pr_ref[rows, :] = probs.astype(BF16)
                    ds_ref[rows, :] = (probs * (dp - dr)).astype(BF16)
                    head = h * Q_PER_KV + c // (CHUNK // SOFTMAX_ROWS)
                    dsk_s[...] += jnp.where(lane == head, -jnp.sum(psink * dr), 0.0)
                    return carry2

                lax.fori_loop(0, STACK // SOFTMAX_ROWS, rows_of, 0, unroll=True)
                dqn_parts.append(_unstack_heads(_dot(ds_ref[...], kk)))
                dkacc[h][pl.ds(r0, 2 * CHUNK), :] += _dg(ds_ref[...], qs, TN)
                dvacc[h][pl.ds(r0, 2 * CHUNK), :] += _dg(pr_ref[...], dos, TN)
            dqn = jnp.concatenate(dqn_parts, axis=1) * ATT_SCALE
            dn = dqn * gq_ref[...]
            d_ref[pl.ds(r0, CHUNK), 0:D_B] = (rq * (dn - nq * _dot_split(dn * nq, mean_q))).astype(BF16)
            dgq_s[...] += jnp.sum(dqn * nq, axis=0, keepdims=True)
            return carry

        lax.fori_loop(0, nb, block, 0)

        mean_k = _head_mean_matrix(KV2)
        folds = [_fold_matrix(h) for h in range(N_KV)]
        rows = min(seq, 4 * CHUNK)

        def piece(j, dgk):
            r0 = pl.multiple_of(j * rows, CHUNK)
            r1 = pl.multiple_of(r0 + CHUNK, CHUNK)
            dkn = _dot_split(dkacc[0][pl.ds(r1, rows), :], folds[0]) + _dot_split(dkacc[1][pl.ds(r1, rows), :], folds[1])
            dv = _dot_split(dvacc[0][pl.ds(r1, rows), :], folds[0]) + _dot_split(dvacc[1][pl.ds(r1, rows), :], folds[1])
            nk, rk = _head_rms(p_ref[pl.ds(r0, rows), K_OFF:K_OFF + KV2].astype(F32), mean_k)
            dn = dkn * gk_ref[...]
            d_ref[pl.ds(r0, rows), K_OFF:K_OFF + KV2] = (rk * (dn - nk * _dot_split(dn * nk, mean_k))).astype(BF16)
            d_ref[pl.ds(r0, rows), V_OFF:V_OFF + KV2] = dv.astype(BF16)
            return dgk + jnp.sum(dkn * nk, axis=0, keepdims=True)

        dgk = lax.fori_loop(0, seq // rows, piece, jnp.zeros((1, KV2), F32))

        @pl.when(b == 0)
        def _():
            dgq_ref[...] = dgq_s[...]
            dgk_ref[...] = dgk
            dsk_ref[...] = jnp.broadcast_to(dsk_s[...], dsk_ref.shape)

        @pl.when(b != 0)
        def _():
            dgq_ref[...] += dgq_s[...]
            dgk_ref[...] += dgk
            dsk_ref[...] += jnp.broadcast_to(dsk_s[...], dsk_ref.shape)

    acc = lambda s: pl.BlockSpec(s, lambda b: (0, 0))
    return _call(
        body, name=name, grid=(T // seq,),
        in_specs=[pl.BlockSpec((seq, QKV_W), lambda b: (b, 0)), pl.BlockSpec((seq, D_B), lambda b: (b, 0)),
                  pl.BlockSpec(gq_t.shape, lambda b: (0, 0)), pl.BlockSpec(gk_t.shape, lambda b: (0, 0)),
                  pl.BlockSpec(memory_space=pltpu.SMEM)],
        out_specs=[pl.BlockSpec((seq, QKV_W), lambda b: (b, 0)), acc((1, D_B)), acc((1, KV2)), acc((8, 128))],
        out_shape=[jax.ShapeDtypeStruct((T, QKV_W), BF16), jax.ShapeDtypeStruct((1, D_B), F32),
                   jax.ShapeDtypeStruct((1, KV2), F32), jax.ShapeDtypeStruct((8, 128), F32)],
        scratch_shapes=[pltpu.VMEM((seq + CHUNK, KV_W), BF16)] * 4 + [pltpu.VMEM((seq + CHUNK, KV_W), F32)] * 4
        + [pltpu.VMEM((1, D_B), F32), pltpu.VMEM((1, 128), F32), pltpu.VMEM((2, CHUNK, 2 * CHUNK), F32)]
        + [pltpu.VMEM((STACK, 2 * CHUNK), F32)] * 2 + [pltpu.VMEM((STACK, 2 * CHUNK), BF16)] * 2,
        args=[pqkv, datt, gq_t, gk_t, sinks], comms=comms)


def ada_fwd(c_all, w, b, *, name):
    nb, D = c_all.shape
    N = w.shape[1]
    tn = N // 3

    def body(c_ref, w_ref, b_ref, o_ref):
        c = c_ref[...]
        cond = (c * jax.nn.sigmoid(c)).astype(BF16)
        o_ref[...] = _dot(cond, w_ref[...].astype(BF16)) + b_ref[...]

    return _call(
        body, name=name, grid=(3,),
        in_specs=[pl.BlockSpec((nb, D), lambda j: (0, 0)), pl.BlockSpec((D, tn), lambda j: (0, j)),
                  pl.BlockSpec((1, tn), lambda j: (0, j))],
        out_specs=[pl.BlockSpec((nb, tn), lambda j: (0, j))], out_shape=[jax.ShapeDtypeStruct((nb, N), F32)],
        args=[c_all, w, b])[0]


def ada_bwd(c_all, dmods_all, dmods_mine, gain_rows, *, name, comms=()):
    nb, D = c_all.shape
    NA = dmods_all.shape[1]
    N = dmods_mine.shape[1]
    tn = N // 3

    def body(c_ref, da_ref, dm_ref, gr_ref, db_ref, dw_ref, dgn_ref):
        c = c_ref[...]
        cond = (c * jax.nn.sigmoid(c)).astype(BF16)
        dw_ref[...] = _dg(cond, dm_ref[...].astype(BF16), TN)
        db_ref[...] = jnp.sum(da_ref[...], axis=0, keepdims=True)
        total = gr_ref[0:1, :]
        for d in range(1, N_DEV):
            total = total + gr_ref[d:d + 1, :]
        dgn_ref[...] = total

    return _call(
        body, name=name, grid=(3,),
        in_specs=[pl.BlockSpec((nb, D), lambda j: (0, 0)), pl.BlockSpec((nb, NA), lambda j: (0, 0)),
                  pl.BlockSpec((nb, tn), lambda j: (0, j)), pl.BlockSpec((N_DEV, D), lambda j: (0, 0))],
        out_specs=[pl.BlockSpec((1, NA), lambda j: (0, 0)), pl.BlockSpec((D, tn), lambda j: (0, j)),
                   pl.BlockSpec((1, D), lambda j: (0, 0))],
        out_shape=[jax.ShapeDtypeStruct((1, NA), F32), jax.ShapeDtypeStruct((D, N), F32),
                   jax.ShapeDtypeStruct((1, D), F32)],
        args=[c_all, dmods_all, dmods_mine, gain_rows], comms=comms)


def adamw(items, *, steps, name, comms=()):
    n = len(items)
    c1 = 1.0 / (1.0 - ADAM_B1 ** ADAM_STEP)
    c2 = 1.0 / (1.0 - ADAM_B2 ** ADAM_STEP)

    def body(*refs):
        for j in range(n):
            w_ref, g_ref, m_ref, v_ref = refs[4 * j:4 * j + 4]
            d_ref, mo_ref, vo_ref = refs[4 * n + 3 * j:4 * n + 3 * j + 3]
            gg = g_ref[...]
            mn = ADAM_B1 * m_ref[...] + (1.0 - ADAM_B1) * gg
            vn = ADAM_B2 * v_ref[...] + (1.0 - ADAM_B2) * (gg * gg)
            mo_ref[...] = mn
            vo_ref[...] = vn
            d_ref[...] = -ADAM_LR * ((mn * c1) / (jnp.sqrt(vn * c2) + ADAM_EPS) + ADAM_WD * w_ref[...])

    in_specs, out_specs, out_shape, args = [], [], [], []
    for item in items:
        R, C = item[0].shape
        blk = pl.BlockSpec((R // steps, C), lambda i: (i, 0))
        in_specs += [blk] * 4
        out_specs += [blk] * 3
        out_shape += [jax.ShapeDtypeStruct((R, C), F32)] * 3
        args += list(item)
    res = _call(body, name=name, grid=(steps,), in_specs=in_specs, out_specs=out_specs, out_shape=out_shape, args=args,
                comms=comms)
    return [tuple(res[3 * j:3 * j + 3]) for j in range(n)]


def _place():
    return lax.axis_index("x"), lax.axis_index("y"), lax.axis_index("c")


def _flip(v, bit):
    return 1 - v if bit else v


def _other_chips(mx, my):
    return [(_flip(mx, k & 2), _flip(my, k & 1)) for k in range(1, N_QUAD)]


def _remote(src, dst, send_sem, recv_sem, peer):
    return pltpu.make_async_remote_copy(src_ref=src, dst_ref=dst, send_sem=send_sem, recv_sem=recv_sem,
                                        device_id=peer, device_id_type=MESH)


def all_gather8(x, *, name, reduce=False, comms=()):
    r, n = x.shape

    def body(x_ref, o_ref, *rest):
        if reduce:
            buf, send_sems, recv_sems, lsem = rest
        else:
            buf = o_ref
            send_sems, recv_sems, lsem = rest
        mx, my, mc = _place()
        me = 4 * mx + 2 * my + mc
        mine = pltpu.make_async_copy(x_ref, buf.at[me], lsem)
        mine.start()
        sends, recvs = [], []
        for k in range(1, N_DEV):
            peer = (_flip(mx, k & 4), _flip(my, k & 2), _flip(mc, k & 1))
            pidx = 4 * peer[0] + 2 * peer[1] + peer[2]
            sends.append(_remote(x_ref, buf.at[me], send_sems.at[k - 1], recv_sems.at[k - 1], peer))
            recvs.append(_remote(x_ref, buf.at[pidx], send_sems.at[k - 1], recv_sems.at[k - 1], peer))
        for cp in sends:
            cp.start()
        for cp in recvs:
            cp.wait_recv()
        for cp in sends:
            cp.wait_send()
        mine.wait()
        if reduce:
            total = buf[0]
            for d in range(1, N_DEV):
                total = total + buf[d]
            o_ref[...] = total

    scratch = [pltpu.SemaphoreType.DMA((N_DEV - 1,)), pltpu.SemaphoreType.DMA((N_DEV - 1,)), pltpu.SemaphoreType.DMA]
    if reduce:
        scratch = [pltpu.VMEM((N_DEV, r, n), F32)] + scratch
        out_shape = jax.ShapeDtypeStruct((r, n), F32)
    else:
        out_shape = jax.ShapeDtypeStruct((N_DEV, r, n), F32)
    vmem = pl.BlockSpec(memory_space=pltpu.VMEM)
    return _call(body, name=name, grid=(), in_specs=[vmem], out_specs=[vmem], out_shape=[out_shape], args=[x],
                 scratch_shapes=scratch, comms=comms)[0]


def ag8_comm(x):
    def copies(srcs, lands, ss, rs, only_sends=False):
        mx, my, mc = _place()
        me = 4 * mx + 2 * my + mc
        local = pltpu.make_async_copy(srcs[0], lands[0].at[me], ss.at[N_DEV - 1])
        sends, recvs = [], []
        for k in range(1, N_DEV):
            peer = (_flip(mx, k & 4), _flip(my, k & 2), _flip(mc, k & 1))
            sends.append(_remote(srcs[0], lands[0].at[me], ss.at[k - 1], rs.at[k - 1], peer))
            if not only_sends:
                recvs.append(_remote(srcs[0], lands[0].at[4 * peer[0] + 2 * peer[1] + peer[2]], ss.at[k - 1], rs.at[k - 1], peer))
        return local, sends, recvs

    def start(srcs, bufs, lands, ss, rs):
        local, sends, _ = copies(srcs, lands, ss, rs, only_sends=True)
        local.start()
        for cp in sends:
            cp.start()

    def finish(srcs, bufs, lands, ss, rs):
        local, sends, recvs = copies(srcs, lands, ss, rs)
        for cp in recvs:
            cp.wait_recv()
        for cp in sends:
            cp.wait_send()
        local.wait()

    return Comm(srcs=[x], land_shapes=[jax.ShapeDtypeStruct((N_DEV,) + x.shape, x.dtype)], n_sems=N_DEV,
                start=start, finish=finish)


def sum8(stacked, *, name):
    _, r, n = stacked.shape

    def body(s_ref, o_ref):
        total = s_ref[0]
        for d in range(1, N_DEV):
            total = total + s_ref[d]
        o_ref[...] = total

    return _call(body, name=name, grid=(), in_specs=[pl.BlockSpec(memory_space=pltpu.VMEM)],
                 out_specs=[pl.BlockSpec(memory_space=pltpu.VMEM)], out_shape=[jax.ShapeDtypeStruct((r, n), F32)],
                 args=[stacked])[0]


def cast_into_stacks(ws, quad, *, name, comms=()):
    steps = 4

    def body(q_ref, *refs):
        for w_ref, o_ref in zip(refs[:len(ws)], refs[len(ws):]):
            o_ref[0] = w_ref[...].astype(BF16)

    return _call(
        body, name=name, grid=(steps,), prefetch=[quad],
        in_specs=[pl.BlockSpec((w.shape[0] // steps, w.shape[1]), lambda i, q: (i, 0)) for w in ws],
        out_specs=[pl.BlockSpec((1, w.shape[0] // steps, w.shape[1]), lambda i, q: (q[0], i, 0)) for w in ws],
        out_shape=[jax.ShapeDtypeStruct((N_QUAD,) + w.shape, BF16) for w in ws], args=list(ws), comms=comms)


def gather_comm(stacks):
    n = len(stacks)

    def copies(bufs, ss, rs, only_sends=False):
        mx, my, mc = _place()
        q = 2 * mx + my
        sibling = (mx, my, 1 - mc)
        ici_send, ici_recv, fwd_send, fwd_recv = [], [], [], []
        for p in range(n):
            hr = stacks[p].shape[1] // 2
            mine, other = pl.ds(mc * hr, hr), pl.ds((1 - mc) * hr, hr)
            for k, chip in enumerate(_other_chips(mx, my)):
                qk = 2 * chip[0] + chip[1]
                peer = (chip[0], chip[1], mc)
                own, landed, theirs = bufs[p].at[q, mine, :], bufs[p].at[qk, mine, :], bufs[p].at[qk, other, :]
                ici_send.append(_remote(own, own, ss.at[6 * p + k], rs.at[6 * p + k], peer))
                if only_sends:
                    continue
                ici_recv.append(_remote(own, landed, ss.at[6 * p + k], rs.at[6 * p + k], peer))
                fwd_send.append(_remote(landed, landed, ss.at[6 * p + 3 + k], rs.at[6 * p + 3 + k], sibling))
                fwd_recv.append(_remote(theirs, theirs, ss.at[6 * p + 3 + k], rs.at[6 * p + 3 + k], sibling))
        return ici_send, ici_recv, fwd_send, fwd_recv

    def start(srcs, bufs, lands, ss, rs):
        for cp in copies(bufs, ss, rs, only_sends=True)[0]:
            cp.start()

    def finish(srcs, bufs, lands, ss, rs):
        ici_send, ici_recv, fwd_send, fwd_recv = copies(bufs, ss, rs)
        for arrived, onward in zip(ici_recv, fwd_send):
            arrived.wait_recv()
            onward.start()
        for cp in fwd_recv:
            cp.wait_recv()
        for cp in ici_send + fwd_send:
            cp.wait_send()

    return Comm(bufs=stacks, n_sems=6 * n, start=start, finish=finish)


def rs_pair_comm(gs):
    n = len(gs)

    def copies(srcs, lands, ss, rs):
        mx, my, mc = _place()
        out = []
        for p in range(n):
            hr = gs[p].shape[1] // 2
            out.append(_remote(srcs[p].at[:, pl.ds((1 - mc) * hr, hr), :], lands[p], ss.at[p], rs.at[p], (mx, my, 1 - mc)))
        return out

    def start(srcs, bufs, lands, ss, rs):
        for cp in copies(srcs, lands, ss, rs):
            cp.start()

    def finish(srcs, bufs, lands, ss, rs):
        for cp in copies(srcs, lands, ss, rs):
            cp.wait()

    return Comm(srcs=gs, land_shapes=[jax.ShapeDtypeStruct((g.shape[0], g.shape[1] // 2, g.shape[2]), g.dtype) for g in gs],
                n_sems=n, start=start, finish=finish)


def rs_chip_comm(ss_):
    n = len(ss_)

    def copies(srcs, lands, ss, rs):
        mx, my, mc = _place()
        out = []
        for p in range(n):
            for k, chip in enumerate(_other_chips(mx, my)):
                out.append(_remote(srcs[p].at[2 * chip[0] + chip[1]], lands[p].at[k], ss.at[3 * p + k], rs.at[3 * p + k],
                                   (chip[0], chip[1], mc)))
        return out

    def start(srcs, bufs, lands, ss, rs):
        for cp in copies(srcs, lands, ss, rs):
            cp.start()

    def finish(srcs, bufs, lands, ss, rs):
        for cp in copies(srcs, lands, ss, rs):
            cp.wait()

    return Comm(srcs=ss_, land_shapes=[jax.ShapeDtypeStruct((N_QUAD - 1,) + s.shape[1:], s.dtype) for s in ss_],
                n_sems=3 * n, start=start, finish=finish)


def rs_share_comm(fulls):
    n = len(fulls)

    def copies(bufs, ss, rs, only_sends=False):
        mx, my, mc = _place()
        send, recv = [], []
        for p in range(n):
            hr = fulls[p].shape[0] // 2
            mine, other = bufs[p].at[pl.ds(mc * hr, hr), :], bufs[p].at[pl.ds((1 - mc) * hr, hr), :]
            send.append(_remote(mine, mine, ss.at[p], rs.at[p], (mx, my, 1 - mc)))
            if not only_sends:
                recv.append(_remote(other, other, ss.at[p], rs.at[p], (mx, my, 1 - mc)))
        return send, recv

    def start(srcs, bufs, lands, ss, rs):
        for cp in copies(bufs, ss, rs, only_sends=True)[0]:
            cp.start()

    def finish(srcs, bufs, lands, ss, rs):
        send, recv = copies(bufs, ss, rs)
        for cp in recv:
            cp.wait_recv()
        for cp in send:
            cp.wait_send()

    return Comm(bufs=fulls, n_sems=n, start=start, finish=finish)


def pair_sum(g, recv, mc, *, name):
    nq, R, C = g.shape
    hr = R // 2
    rb = _row_block(hr, 512)
    nb = hr // rb

    def body(s_ref, g_ref, r_ref, o_ref):
        o_ref[...] = (g_ref[...].astype(F32) + r_ref[...].astype(F32)).astype(BF16)

    return pl.pallas_call(
        body, name=name, out_shape=jax.ShapeDtypeStruct((nq, hr, C), BF16),
        grid_spec=pltpu.PrefetchScalarGridSpec(
            num_scalar_prefetch=1, grid=(nq, nb),
            in_specs=[pl.BlockSpec((1, rb, C), lambda q, i, s: (q, s[0] * nb + i, 0)),
                      pl.BlockSpec((1, rb, C), lambda q, i, s: (q, i, 0))],
            out_specs=pl.BlockSpec((1, rb, C), lambda q, i, s: (q, i, 0))),
        compiler_params=pltpu.CompilerParams(dimension_semantics=("arbitrary", "arbitrary"),
                                             vmem_limit_bytes=VMEM_LIMIT_BYTES),
    )(mc, g, recv)


def chip_sum(s, recv, quad_mc, *, name):
    nq, hr, C = s.shape
    rb = _row_block(hr, 256)
    nb = hr // rb

    def body(q_ref, s_ref, r_ref, o_ref):
        total = s_ref[0].astype(F32)
        for k in range(N_QUAD - 1):
            total = total + r_ref[k].astype(F32)
        o_ref[...] = total

    return pl.pallas_call(
        body, name=name, out_shape=jax.ShapeDtypeStruct((2 * hr, C), F32),
        grid_spec=pltpu.PrefetchScalarGridSpec(
            num_scalar_prefetch=1, grid=(nb,),
            in_specs=[pl.BlockSpec((1, rb, C), lambda i, q: (q[0], i, 0)),
                      pl.BlockSpec((N_QUAD - 1, rb, C), lambda i, q: (0, i, 0))],
            out_specs=pl.BlockSpec((rb, C), lambda i, q: (q[1] * nb + i, 0))),
        compiler_params=pltpu.CompilerParams(dimension_semantics=("arbitrary",), vmem_limit_bytes=VMEM_LIMIT_BYTES),
    )(quad_mc, s, recv)


def _stack_cols(w_full):
    K, N = w_full.shape
    return w_full.reshape(K, N_QUAD, N // N_QUAD).transpose(1, 0, 2)


def _unstack_cols(w4):
    nq, K, n = w4.shape
    return w4.transpose(1, 0, 2).reshape(K, nq * n)


def kernel(x, c, w_ada, b_ada, g_norm1, ffn1_w_gate, ffn1_w_up, ffn1_w_down, g_norm2, w_in, g_sgu_ln, b_sgu_ln, w_spatial, b_spatial, g_q, g_k, attn_sinks, w_branch_a, w_branch_b, w_out, g_norm3, ffn2_w_gate, ffn2_w_up, ffn2_w_down, loss_target, m_w_ada, m_b_ada, m_g_norm1, m_ffn1_w_gate, m_ffn1_w_up, m_ffn1_w_down, m_g_norm2, m_w_in, m_g_sgu_ln, m_b_sgu_ln, m_w_spatial, m_b_spatial, m_g_q, m_g_k, m_attn_sinks, m_w_branch_a, m_w_branch_b, m_w_out, m_g_norm3, m_ffn2_w_gate, m_ffn2_w_up, m_ffn2_w_down, v_w_ada, v_b_ada, v_g_norm1, v_ffn1_w_gate, v_ffn1_w_up, v_ffn1_w_down, v_g_norm2, v_w_in, v_g_sgu_ln, v_b_sgu_ln, v_w_spatial, v_b_spatial, v_g_q, v_g_k, v_attn_sinks, v_w_branch_a, v_w_branch_b, v_w_out, v_g_norm3, v_ffn2_w_gate, v_ffn2_w_up, v_ffn2_w_down):
    B, S, D = x.shape
    T = B * S
    n_mod = b_ada.shape[1] // D
    mx, my, mc = _place()
    quad = 2 * mx + my
    mc_arr = jnp.reshape(mc, (1,)).astype(jnp.int32)
    quad_arr = jnp.reshape(quad, (1,)).astype(jnp.int32)
    quad_mc = jnp.stack([quad, mc]).astype(jnp.int32)
    tm = min(512, S)
    tm_small = min(256, S)
    tm_big = min(1024, S)
    tt_half = min(2048, T)
    cpb = min(4, S // CHUNK)

    xt = x.reshape(T, D)
    tgt = loss_target.reshape(T, D)

    tr = lambda a: jnp.swapaxes(a, 1, 2)
    stack_wg1, stack_wu1 = cast_into_stacks([tr(ffn1_w_gate)[0], tr(ffn1_w_up)[0]], quad_arr, name="stack_gu1")
    gather_wg1, gather_wu1 = gather_comm([stack_wg1]), gather_comm([stack_wu1])

    c_all = all_gather8(c, name="gather_cond", comms=[gather_wg1]).reshape(N_DEV * B, D)
    (wg1,) = gather_wg1.bufs_out
    n_ada = w_ada.shape[2]
    b_mine = lax.dynamic_slice(b_ada, (0, quad * n_ada), (1, n_ada))
    mods_part = ada_fwd(c_all, w_ada[0], b_mine, name="ada_fwd")
    gather_mods = ag8_comm(mods_part)
    later = [ffn1_w_down, tr(w_in), w_branch_a, w_branch_b, w_out, tr(ffn2_w_gate), tr(ffn2_w_up), ffn2_w_down]
    stacks = cast_into_stacks([w[0] for w in later], quad_arr, name="stack_rest", comms=[gather_wu1, gather_mods])
    (wu1,), (mods_parts,) = gather_wu1.bufs_out, gather_mods.lands
    gather_wd1, gather_win = gather_comm([stacks[0]]), gather_comm([stacks[1]])
    gather_abo = gather_comm(stacks[2:5])
    gather_wg3, gather_wu3, gather_wd3 = gather_comm([stacks[5]]), gather_comm([stacks[6]]), gather_comm([stacks[7]])
    mods = jnp.concatenate([mods_parts[2 * j] for j in range(N_QUAD)], axis=1)
    me = 4 * mx + 2 * my + mc
    mods = lax.dynamic_slice(mods, (me * B, 0), (B, n_mod * D))
    mods = mods.reshape(B, n_mod, 1, D)
    sh1, sc1, ga1, sh2, sc2, ga2, sh3, sc3, ga3 = [(mods, j) for j in range(n_mod)]
    bs_t = b_spatial[0].T
    ws = w_spatial[0]

    xn1 = norm_mod_fwd(xt, g_norm1, sc1, sh1, seq=S, tm=tm, name="norm1")
    g1, u1, a1 = ffn_up(xn1, wg1, wu1, tm=tm_big, name="ffn1_up", comms=[gather_wd1, gather_abo])
    (wd1,), (wa4, wb4, wo4) = gather_wd1.bufs_out, gather_abo.bufs_out
    wa, wb = _unstack_cols(wa4), _unstack_cols(wb4)
    wo = wo4.reshape(D, D)
    h1, f1, xn2 = ffn_down(a1, wd1, xt, ga1, norm=(g_norm2, sc2, sh2), seq=S, tm=tm, name="ffn1_down",
                           comms=[gather_win])
    (win4,) = gather_win.bufs_out
    win = win4.reshape(-1, D)
    w_uv, w_qkv, w_gt = win[0:2 * D_A], win[2 * D_A:2 * D_A + QKV_W], win[2 * D_A + QKV_W:]
    puv, pqkv, pgt = proj_fwd(xn2, [w_uv, w_qkv, w_gt], tm=tm_small, name="proj", comms=[gather_wg3])
    (wg3,) = gather_wg3.bufs_out
    sgu = sgu_fwd(puv, g_sgu_ln, b_sgu_ln, ws, bs_t, cpb=cpb, name="sgu_fwd")
    gq_t, gk_t = jnp.tile(g_q, (1, N_Q)), jnp.tile(g_k, (1, N_KV))
    att = attn_fwd(pqkv, gq_t, gk_t, attn_sinks, seq=S, name="attn_fwd", comms=[gather_wu3])
    (wu3,) = gather_wu3.bufs_out
    ya, yb, merged, mm, h2, xn3 = mixer_out_fwd(sgu, att, pgt, h1, ga2, wa, wb, wo, (g_norm3, sc3, sh3), seq=S,
                                                tm=tm_small, name="mixer_out", comms=[gather_wd3])
    (wd3,) = gather_wd3.bufs_out
    g3, u3, a3 = ffn_up(xn3, wg3, wu3, tm=tm_big, name="ffn2_up")
    dy, f3, lparts = ffn_down(a3, wd3, h2, ga3, target=tgt, seq=S, tm=tm, name="ffn2_down_loss")
    loss_part = jnp.sum(lparts[:, 0, 0])

    def sums_of(pair, tag):
        return [pair_sum(g, r, mc_arr, name=f"pair_sum_{tag}_{p}") for p, (g, r) in enumerate(zip(pair.srcs, pair.lands))]

    def halves_of(chip, tag):
        return [chip_sum(s, r, quad_mc, name=f"chip_sum_{tag}_{p}") for p, (s, r) in enumerate(zip(chip.srcs, chip.lands))]

    dg3, du3, df3, dga3 = ffn_bwd_act(dy, f3, ga3, wd3, g3, u3, seq=S, tm=tm, name="ffn2_act_bwd")
    (dwd3,) = mm_tn([(a3, df3)], tt=T, name="ffn2_dwd")
    pair_wd3 = rs_pair_comm([dwd3])
    dwg3, dwu3 = mm_tn([(dg3, xn3), (du3, xn3)], tt=tt_half, name="ffn2_dwgu", comms=[pair_wd3])
    chip_wd3 = rs_chip_comm(sums_of(pair_wd3, "wd3"))
    pair_gu3 = rs_pair_comm([dwg3, dwu3])
    dh2, dsh3, dsc3, dgn3 = dx_norm_bwd([dg3, du3], [wg3, wu3], h2, dy, g_norm3, sc3, seq=S, tm=tm, name="ffn2_dx",
                                        comms=[chip_wd3, pair_gu3])
    sum_wg3, sum_wu3 = sums_of(pair_gu3, "gu3")
    chip_wg3, chip_wu3 = rs_chip_comm([sum_wg3]), rs_chip_comm([sum_wu3])

    dgt, dya, dyb, dsgu, datt, dm, dga2 = mixer_out_bwd(dh2, mm, ga2, ya, yb, pgt, wa, wb, wo, seq=S, tm=tm_small,
                                                        name="mixer_out_bwd", comms=[chip_wg3])
    (dwo,) = mm_tn([(merged, dm)], tt=tt_half, name="mixer_dwo")
    (dwa,) = mm_tn([(sgu, dya)], tt=tt_half, name="mixer_dwa")
    (dwb,) = mm_tn([(att, dyb)], tt=tt_half, name="mixer_dwb")
    pair_abo = rs_pair_comm([_stack_cols(dwa), _stack_cols(dwb), dwo.reshape(N_QUAD, D // N_QUAD, D)])
    duv, dws, dzs, dgln, dbln = sgu_bwd(puv, dsgu, g_sgu_ln, b_sgu_ln, ws, bs_t, cpb=cpb, name="sgu_bwd",
                                        comms=[pair_abo])
    chip_abo = rs_chip_comm(sums_of(pair_abo, "abo"))
    dqkv, dgq_h, dgk_h, dsk = attn_bwd(pqkv, datt, gq_t, gk_t, attn_sinks, seq=S, name="attn_bwd",
                                       comms=[chip_wu3, chip_abo])
    dgq = dgq_h.reshape(N_Q, HEAD_DIM).sum(axis=0, keepdims=True)
    dgk = dgk_h.reshape(N_KV, HEAD_DIM).sum(axis=0, keepdims=True)
    share3 = rs_share_comm(halves_of(chip_wg3, "wg3") + halves_of(chip_wu3, "wu3") + halves_of(chip_wd3, "wd3"))
    dwin_uv, dwin_qkv = mm_tn([(duv, xn2), (dqkv, xn2)], tt=tt_half, name="mixer_dwin_a", comms=[share3])
    (dwin_gt,) = mm_tn([(dgt, xn2)], tt=tt_half, name="mixer_dwin_b")
    dwin_parts = [dwin_uv, dwin_qkv, dwin_gt]
    r_wg3, r_wu3, r_wd3 = share3.bufs_out
    pair_win = rs_pair_comm([jnp.concatenate(dwin_parts, axis=0).reshape(win4.shape)])
    dh1, dsh2, dsc2, dgn2 = dx_norm_bwd([duv, dqkv, dgt], [w_uv, w_qkv, w_gt], h1, dh2, g_norm2, sc2, seq=S,
                                        tm=tm, name="mixer_dx", comms=[pair_win])
    chip_win = rs_chip_comm(sums_of(pair_win, "win"))

    dg1, du1, df1, dga1 = ffn_bwd_act(dh1, f1, ga1, wd1, g1, u1, seq=S, tm=tm, name="ffn1_act_bwd", comms=[chip_win])
    share_mix = rs_share_comm(halves_of(chip_win, "win") + halves_of(chip_abo, "abo"))

    db_s = dzs.reshape(CHUNK, N_GROUPS, D_A // N_GROUPS).sum(axis=2).T.reshape(1, N_GROUPS * CHUNK)
    tail = jnp.concatenate([db_s, dgq, dgk, dsk[0:1, 0:N_Q], loss_part.reshape(1, 1)], axis=1)
    tail = jnp.pad(tail, ((0, 0), (0, (-tail.shape[1]) % D)))
    lnrow = jnp.concatenate([dgln, dbln], axis=1)
    lnrow = jnp.pad(lnrow, ((0, 0), (0, (-lnrow.shape[1]) % D)))
    packed = jnp.concatenate([dgn2, dgn3, lnrow.reshape(-1, D), tail.reshape(-1, D), dws.reshape(-1, D)], axis=0)
    n_rows = packed.shape[0]
    packed = jnp.pad(packed, ((0, (-n_rows) % 8), (0, 0)))
    gather_small = ag8_comm(packed)

    dwg1, dwu1 = mm_tn([(dg1, xn1), (du1, xn1)], tt=tt_half, name="ffn1_dwgu", comms=[share_mix, gather_small])
    r_win, r_wa, r_wb, r_wo = share_mix.bufs_out
    small = sum8(gather_small.lands[0], name="sum_small")
    pair_gu1 = rs_pair_comm([dwg1, dwu1])
    (dwd1,) = mm_tn([(a1, df1)], tt=T, name="ffn1_dwd", comms=[pair_gu1])
    chip_gu1 = rs_chip_comm(sums_of(pair_gu1, "gu1"))
    pair_wd1 = rs_pair_comm([dwd1])
    dx, dsh1, dsc1, dgn1 = dx_norm_bwd([dg1, du1], [wg1, wu1], xt, dh1, g_norm1, sc1, seq=S, tm=tm, name="ffn1_dx",
                                       comms=[chip_gu1, pair_wd1])
    chip_wd1 = rs_chip_comm(sums_of(pair_wd1, "wd1"))
    share_gu1 = rs_share_comm(halves_of(chip_gu1, "gu1"))

    names = ["w_ada", "b_ada", "g_norm1", "ffn1_w_gate", "ffn1_w_up", "ffn1_w_down", "g_norm2", "w_in", "g_sgu_ln",
             "b_sgu_ln", "w_spatial", "b_spatial", "g_q", "g_k", "attn_sinks", "w_branch_a", "w_branch_b", "w_out",
             "g_norm3", "ffn2_w_gate", "ffn2_w_up", "ffn2_w_down"]
    weights = dict(zip(names, [w_ada, b_ada, g_norm1, ffn1_w_gate, ffn1_w_up, ffn1_w_down, g_norm2, w_in, g_sgu_ln,
                               b_sgu_ln, w_spatial, b_spatial, g_q, g_k, attn_sinks, w_branch_a, w_branch_b, w_out,
                               g_norm3, ffn2_w_gate, ffn2_w_up, ffn2_w_down]))
    m_in = dict(zip(names, [m_w_ada, m_b_ada, m_g_norm1, m_ffn1_w_gate, m_ffn1_w_up, m_ffn1_w_down, m_g_norm2, m_w_in,
                            m_g_sgu_ln, m_b_sgu_ln, m_w_spatial, m_b_spatial, m_g_q, m_g_k, m_attn_sinks, m_w_branch_a,
                            m_w_branch_b, m_w_out, m_g_norm3, m_ffn2_w_gate, m_ffn2_w_up, m_ffn2_w_down]))
    v_in = dict(zip(names, [v_w_ada, v_b_ada, v_g_norm1, v_ffn1_w_gate, v_ffn1_w_up, v_ffn1_w_down, v_g_norm2, v_w_in,
                            v_g_sgu_ln, v_b_sgu_ln, v_w_spatial, v_b_spatial, v_g_q, v_g_k, v_attn_sinks, v_w_branch_a,
                            v_w_branch_b, v_w_out, v_g_norm3, v_ffn2_w_gate, v_ffn2_w_up, v_ffn2_w_down]))
    transposed = ("ffn1_w_gate", "ffn1_w_up", "w_in", "ffn2_w_gate", "ffn2_w_up")
    grads, delta, new_m, new_v = {}, {}, {}, {}

    def update(group, steps, name, comms=()):
        form = lambda nm, a: (tr(a) if nm in transposed else a)[0].reshape(group[nm].shape)
        res = adamw([(form(nm, weights[nm]), g, form(nm, m_in[nm]), form(nm, v_in[nm])) for nm, g in group.items()],
                    steps=steps, name=name, comms=comms)
        back = lambda nm, a: tr(a[None]) if nm in transposed else a.reshape(weights[nm].shape)
        for (nm, g), (d, mn, vn) in zip(group.items(), res):
            grads[nm], delta[nm], new_m[nm], new_v[nm] = back(nm, g), back(nm, d), back(nm, mn), back(nm, vn)

    dmods = jnp.concatenate([dsh1, dsc1, dga1, dsh2, dsc2, dga2, dsh3, dsc3, dga3], axis=1)
    dmods = jnp.concatenate([dmods, jnp.pad(dgn1, ((0, 0), (0, (n_mod - 1) * D)))], axis=0)
    gather_dmods = ag8_comm(dmods)
    update({"ffn2_w_gate": r_wg3, "ffn2_w_up": r_wu3, "ffn2_w_down": r_wd3, "w_out": r_wo, "w_branch_a": r_wa,
            "w_branch_b": r_wb}, 8, "adamw_early", comms=[chip_wd1, share_gu1, gather_dmods])
    r_wg1, r_wu1 = share_gu1.bufs_out
    (gathered,) = gather_dmods.lands
    dmods_all = gathered[:, 0:B].reshape(N_DEV * B, n_mod * D)
    dmods_mine = lax.dynamic_slice(dmods_all, (0, quad * n_ada), (N_DEV * B, n_ada))
    share_wd1 = rs_share_comm(halves_of(chip_wd1, "wd1"))
    db_ada, dw_ada, g_gn1 = ada_bwd(c_all, dmods_all, dmods_mine, gathered[:, B, 0:D], name="ada_bwd", comms=[share_wd1])
    (r_wd1,) = share_wd1.bufs_out

    ln_rows = lnrow.size // D
    tail_rows = tail.size // D
    r = 2
    g_gn2, g_gn3 = small[0:1], small[1:2]
    ln_flat = small[r:r + ln_rows].reshape(1, -1)
    r += ln_rows
    tail_flat = small[r:r + tail_rows].reshape(1, -1)
    r += tail_rows
    g_ws = small[r:r + dws.size // D].reshape(w_spatial.shape)
    g_gln, g_bln = ln_flat[:, 0:D_A], ln_flat[:, D_A:2 * D_A]
    o = N_GROUPS * CHUNK
    g_bs = tail_flat[:, 0:o].reshape(b_spatial.shape)
    g_gq, g_gk, g_sk = tail_flat[:, o:o + HEAD_DIM], tail_flat[:, o + HEAD_DIM:o + 2 * HEAD_DIM], tail_flat[:, o + 2 * HEAD_DIM:o + 2 * HEAD_DIM + N_Q]
    loss = tail_flat[0, o + 2 * HEAD_DIM + N_Q]

    update({"w_ada": dw_ada}, 16, "adamw_w_ada")
    update({"ffn1_w_gate": r_wg1, "ffn1_w_up": r_wu1, "ffn1_w_down": r_wd1, "w_in": r_win}, 8, "adamw_late")

    update({"b_ada": db_ada, "g_norm1": g_gn1, "g_norm2": g_gn2, "g_norm3": g_gn3, "g_sgu_ln": g_gln,
            "b_sgu_ln": g_bln, "w_spatial": g_ws.reshape(-1, CHUNK), "b_spatial": g_bs.reshape(N_GROUPS, CHUNK),
            "g_q": g_gq, "g_k": g_gk, "attn_sinks": g_sk}, 1, "adamw_small")

    return (loss, dx.reshape(B, S, D), *[grads[nm] for nm in names], *[delta[nm] for nm in names],
            *[new_m[nm] for nm in names], *[new_v[nm] for nm in names])
```

```python
import functools
import math
import operator

import jax
import jax.numpy as jnp
from jax import lax
from jax.experimental import pallas as pl
from jax.experimental.pallas import tpu as pltpu

F32 = jnp.float32
BF16 = jnp.bfloat16
EPS = 1e-6
NEG = -1e30
N_DEV = 8
N_QUAD = 4
D_A = 512
N_GROUPS = 4
CHUNK = 128
HEAD_DIM = 64
N_KV = 2
Q_PER_KV = 4
N_Q = N_KV * Q_PER_KV
D_B = N_Q * HEAD_DIM
QKV_W = D_B + 2 * N_KV * HEAD_DIM
K_OFF = D_B
V_OFF = D_B + N_KV * HEAD_DIM
ATT_SCALE = HEAD_DIM ** -0.5
GELU_K = math.sqrt(2.0 / math.pi)
GELU_C = 0.044715
ADAM_LR, ADAM_B1, ADAM_B2, ADAM_EPS, ADAM_WD, ADAM_STEP = 0.001, 0.9, 0.999, 1e-08, 0.01, 10
VMEM_LIMIT_BYTES = 56 * 1024 * 1024
MESH = pl.DeviceIdType.MESH
NT = (((1,), (1,)), ((), ()))
TN = (((0,), (0,)), ((), ()))
ANY = pl.BlockSpec(memory_space=pl.ANY)


def _dot(a, b):
    return jnp.dot(a, b, preferred_element_type=F32)


def _dg(a, b, dims):
    return lax.dot_general(a, b, dims, preferred_element_type=F32)


def _gelu_parts(x):
    t = jnp.tanh(GELU_K * (x + GELU_C * x * x * x))
    return 0.5 * x * (1.0 + t), t


def _dgelu(x, t):
    return 0.5 * (1.0 + t) + 0.5 * x * (1.0 - t * t) * (GELU_K * (1.0 + 3.0 * GELU_C * x * x))


def _row_block(rows, target):
    b = min(rows, target)
    while rows % b or b % 8:
        b -= 1
    return b


def _mod_spec(mod, tps):
    mods, j = mod
    return pl.BlockSpec((1, None, 1, mods.shape[3]), lambda i: (i // tps, j, 0, 0))


def _resident(a):
    return pl.BlockSpec(a.shape, lambda *_: (0,) * a.ndim, pipeline_mode=pl.Buffered(1))


class Comm:
    def __init__(self, *, srcs=(), bufs=(), land_shapes=(), n_sems, start, finish):
        self.srcs, self.bufs, self.land_shapes = list(srcs), list(bufs), list(land_shapes)
        self.n_sems, self.start, self.finish = n_sems, start, finish
        self.bufs_out, self.lands = None, None


def _call(body, *, name, grid, in_specs, out_specs, out_shape, args, scratch_shapes=(), comms=(), prefetch=()):
    prefetch = list(prefetch)
    in_specs, out_specs, out_shape = list(in_specs), list(out_specs), list(out_shape)
    args, scratch = list(args), list(scratch_shapes)
    n_in, n_out, n_scr = len(args), len(out_shape), len(scratch)
    aliases, layout = {}, []
    for cm in comms:
        i0, o0, s0 = len(args), len(out_shape), len(scratch)
        args += cm.srcs + cm.bufs
        in_specs += [ANY] * (len(cm.srcs) + len(cm.bufs))
        out_shape += [jax.ShapeDtypeStruct(b.shape, b.dtype) for b in cm.bufs] + cm.land_shapes
        out_specs += [ANY] * (len(cm.bufs) + len(cm.land_shapes))
        for j in range(len(cm.bufs)):
            aliases[len(prefetch) + i0 + len(cm.srcs) + j] = o0 + j
        scratch += [pltpu.SemaphoreType.DMA((cm.n_sems,)), pltpu.SemaphoreType.DMA((cm.n_sems,))]
        layout.append((i0, o0, s0))
    n_in_all, n_out_all = len(args), len(out_shape)

    def wrapped(*refs):
        scalars, refs = refs[:len(prefetch)], refs[len(prefetch):]
        ins, outs, scr = refs[:n_in_all], refs[n_in_all:n_in_all + n_out_all], refs[n_in_all + n_out_all:]
        parts = []
        for cm, (i0, o0, s0) in zip(comms, layout):
            nb = len(cm.bufs)
            parts.append((ins[i0:i0 + len(cm.srcs)], outs[o0:o0 + nb], outs[o0 + nb:o0 + nb + len(cm.land_shapes)],
                          scr[s0], scr[s0 + 1]))
        if comms and grid:
            ids = [pl.program_id(d) for d in range(len(grid))]
            first = functools.reduce(operator.and_, [i == 0 for i in ids])
            last = functools.reduce(operator.and_, [i == g - 1 for i, g in zip(ids, grid)])

            @pl.when(first)
            def _():
                for cm, p in zip(comms, parts):
                    cm.start(*p)
        elif comms:
            for cm, p in zip(comms, parts):
                cm.start(*p)
        if body is not None:
            body(*scalars, *ins[:n_in], *outs[:n_out], *scr[:n_scr])
        if comms and grid:
            @pl.when(last)
            def _():
                for cm, p in zip(comms, parts):
                    cm.finish(*p)
        elif comms:
            for cm, p in zip(comms, parts):
                cm.finish(*p)

    if prefetch:
        kwargs = dict(grid_spec=pltpu.PrefetchScalarGridSpec(
            num_scalar_prefetch=len(prefetch), grid=grid, in_specs=in_specs, out_specs=out_specs, scratch_shapes=scratch))
    else:
        kwargs = dict(in_specs=in_specs, out_specs=out_specs, scratch_shapes=scratch, **(dict(grid=grid) if grid else {}))
    sem = ("arbitrary",) * len(grid)
    res = pl.pallas_call(
        wrapped, name=name, out_shape=out_shape, input_output_aliases=aliases,
        compiler_params=pltpu.CompilerParams(dimension_semantics=sem, vmem_limit_bytes=VMEM_LIMIT_BYTES), **kwargs,
    )(*prefetch, *args)
    for cm, (i0, o0, s0) in zip(comms, layout):
        nb = len(cm.bufs)
        cm.bufs_out = list(res[o0:o0 + nb])
        cm.lands = list(res[o0 + nb:o0 + nb + len(cm.land_shapes)])
    return list(res[:n_out])


def run_comms(comms, *, name):
    _call(None, name=name, grid=(), in_specs=[], out_specs=[], out_shape=[], args=[], comms=comms)


def norm_mod_fwd(h, g, sc, sh, *, seq, tm, name, comms=()):
    T, D = h.shape
    B, tps = T // seq, seq // tm

    def body(h_ref, g_ref, sc_ref, sh_ref, o_ref):
        x = h_ref[...]
        r = lax.rsqrt(jnp.mean(x * x, axis=-1, keepdims=True) + EPS)
        y = x * r * g_ref[...]
        o_ref[...] = (y * (1.0 + sc_ref[0]) + sh_ref[0]).astype(o_ref.dtype)

    return _call(
        body, name=name, grid=(T // tm,),
        in_specs=[pl.BlockSpec((tm, D), lambda i: (i, 0)), pl.BlockSpec((1, D), lambda i: (0, 0)),
                  _mod_spec(sc, tps), _mod_spec(sh, tps)],
        out_specs=[pl.BlockSpec((tm, D), lambda i: (i, 0))], out_shape=[jax.ShapeDtypeStruct((T, D), BF16)],
        args=[h, g, sc[0], sh[0]], comms=comms)[0]


def ffn_up(xn, wg, wu, *, tm, name, comms=()):
    T, D = xn.shape
    nq, fs, _ = wg.shape

    def body(x_ref, wg_ref, wu_ref, s_ref, q_ref, a_ref):
        x = x_ref[...]
        g = _dg(x, wg_ref[0], NT)
        u = _dg(x, wu_ref[0], NT)
        sg = jax.nn.sigmoid(g)
        s = g * sg
        s_ref[0] = s.astype(BF16)
        q_ref[0] = (u * (sg + s * (1.0 - sg))).astype(BF16)
        a_ref[0] = (s * u).astype(BF16)

    w_spec = pl.BlockSpec((1, fs, D), lambda q, i: (q, 0, 0))
    o_spec = pl.BlockSpec((1, tm, fs), lambda q, i: (q, i, 0))
    o_shape = jax.ShapeDtypeStruct((nq, T, fs), BF16)
    return _call(
        body, name=name, grid=(nq, T // tm),
        in_specs=[pl.BlockSpec((tm, D), lambda q, i: (i, 0)), w_spec, w_spec],
        out_specs=[o_spec, o_spec, o_spec], out_shape=[o_shape, o_shape, o_shape], args=[xn, wg, wu], comms=comms)


def _norm_mod(y, g_ref, sc_ref, sh_ref):
    nx, _ = _rms_parts(y)
    return ((nx * g_ref[...]) * (1.0 + sc_ref[0]) + sh_ref[0]).astype(BF16)


def ffn_down(a, wd, hin, ga, *, target=None, norm=None, seq, tm, name, comms=()):
    nq, T, fs = a.shape
    D = wd.shape[-1]
    B, tps, nt = T // seq, seq // tm, T // tm

    def body(a_ref, wd_ref, h_ref, ga_ref, *rest):
        rest = list(rest)
        t_ref = rest.pop(0) if target is not None else None
        norm_refs = [rest.pop(0) for _ in range(3)] if norm is not None else None
        o_ref, f_ref = rest[0], rest[1]
        f = _dot(a_ref[0], wd_ref[0])
        for q in range(1, nq):
            f = f + _dot(a_ref[q], wd_ref[q])
        f_ref[...] = f.astype(BF16)
        y = h_ref[...] + (0.5 * ga_ref[0]) * f
        if target is not None:
            e = y - t_ref[...]
            o_ref[...] = e * (1.0 / D)
            rest[2][...] = jnp.full(rest[2].shape, 0.5 * jnp.sum(e * e) * (1.0 / D), F32)
        else:
            o_ref[...] = y
        if norm is not None:
            rest[2][...] = _norm_mod(y, *norm_refs)

    tile = pl.BlockSpec((tm, D), lambda i: (i, 0))
    in_specs = [pl.BlockSpec((nq, tm, fs), lambda i: (0, i, 0)), _resident(wd), tile, _mod_spec(ga, tps)]
    out_specs = [tile, tile]
    out_shape = [jax.ShapeDtypeStruct((T, D), F32), jax.ShapeDtypeStruct((T, D), BF16)]
    args = [a, wd, hin, ga[0]]
    if target is not None:
        in_specs.append(tile)
        args.append(target)
        out_specs.append(pl.BlockSpec((1, 8, 128), lambda i: (i, 0, 0)))
        out_shape.append(jax.ShapeDtypeStruct((nt, 8, 128), F32))
    if norm is not None:
        in_specs += [pl.BlockSpec((1, D), lambda i: (0, 0)), _mod_spec(norm[1], tps), _mod_spec(norm[2], tps)]
        args += [norm[0], norm[1][0], norm[2][0]]
        out_specs.append(tile)
        out_shape.append(jax.ShapeDtypeStruct((T, D), BF16))
    return _call(body, name=name, grid=(nt,), in_specs=in_specs, out_specs=out_specs, out_shape=out_shape, args=args,
                 comms=comms)


def proj_fwd(xn, ws, *, tm, name, comms=()):
    T, D = xn.shape
    n = len(ws)

    def body(*refs):
        x = refs[0][...]
        for j in range(n):
            refs[1 + n + j][...] = _dg(x, refs[1 + j][...], NT).astype(BF16)

    return _call(
        body, name=name, grid=(T // tm,),
        in_specs=[pl.BlockSpec((tm, D), lambda i: (i, 0))] + [pl.BlockSpec(w.shape, lambda i: (0, 0)) for w in ws],
        out_specs=[pl.BlockSpec((tm, w.shape[0]), lambda i: (i, 0)) for w in ws],
        out_shape=[jax.ShapeDtypeStruct((T, w.shape[0]), BF16) for w in ws], args=[xn, *ws], comms=comms)


def _layer_norm_parts(v):
    mu = jnp.mean(v, axis=-1, keepdims=True)
    d = v - mu
    rstd = lax.rsqrt(jnp.mean(d * d, axis=-1, keepdims=True) + EPS)
    return d * rstd, rstd


def _causal():
    row = lax.broadcasted_iota(jnp.int32, (CHUNK, CHUNK), 0)
    col = lax.broadcasted_iota(jnp.int32, (CHUNK, CHUNK), 1)
    return row >= col


def sgu_fwd(puv, gln, bln, ws, bs_t, *, cpb, name, comms=()):
    T = puv.shape[0]
    gd = D_A // N_GROUPS
    tm = cpb * CHUNK

    def body(p_ref, gln_ref, bln_ref, ws_ref, bs_ref, o_ref):
        causal = _causal()
        wm = [jnp.where(causal, ws_ref[g], 0.0).astype(BF16) for g in range(N_GROUPS)]
        for j in range(cpb):
            rows = slice(j * CHUNK, (j + 1) * CHUNK)
            u, _ = _gelu_parts(p_ref[rows, 0:D_A].astype(F32))
            vv, _ = _gelu_parts(p_ref[rows, D_A:2 * D_A].astype(F32))
            vhat, _ = _layer_norm_parts(vv)
            vn = (vhat * gln_ref[...] + bln_ref[...]).astype(BF16)
            for g in range(N_GROUPS):
                cols = slice(g * gd, (g + 1) * gd)
                z = _dot(wm[g], vn[:, cols]) + bs_ref[:, g:g + 1]
                o_ref[rows, cols] = (u[:, cols] * z).astype(BF16)

    full = lambda a: pl.BlockSpec(a.shape, lambda i: (0,) * a.ndim)
    return _call(
        body, name=name, grid=(T // tm,),
        in_specs=[pl.BlockSpec((tm, 2 * D_A), lambda i: (i, 0)), full(gln), full(bln), full(ws), full(bs_t)],
        out_specs=[pl.BlockSpec((tm, D_A), lambda i: (i, 0))], out_shape=[jax.ShapeDtypeStruct((T, D_A), BF16)],
        args=[puv, gln, bln, ws, bs_t], comms=comms)[0]


def _rms_parts(x):
    r = lax.rsqrt(jnp.mean(x * x, axis=-1, keepdims=True) + EPS)
    return x * r, r


KV_W = Q_PER_KV * HEAD_DIM
KV2 = N_KV * HEAD_DIM
STACK = Q_PER_KV * CHUNK


def _iota(shape, dim):
    return lax.broadcasted_iota(jnp.int32, shape, dim)


def _head_mean_matrix(n):
    return jnp.where((_iota((n, n), 0) >> 6) == (_iota((n, n), 1) >> 6), 1.0 / HEAD_DIM, 0.0).astype(BF16)


def _repeat_matrix(h):
    return jnp.where(_iota((KV2, KV_W), 0) == h * HEAD_DIM + (_iota((KV2, KV_W), 1) & (HEAD_DIM - 1)), 1.0, 0.0).astype(BF16)


def _fold_matrix(h):
    j, l = _iota((KV_W, KV2), 0), _iota((KV_W, KV2), 1)
    return jnp.where(((j & (HEAD_DIM - 1)) == (l & (HEAD_DIM - 1))) & ((l >> 6) == h), 1.0, 0.0).astype(BF16)


def _dot_split(x, m):
    hi = x.astype(BF16)
    lo = (x - hi.astype(F32)).astype(BF16)
    return _dot(hi, m) + _dot(lo, m)


def _head_rms(x, mean_matrix):
    r = lax.rsqrt(_dot_split(x * x, mean_matrix) + EPS)
    return x * r, r


def _stack_heads(x):
    head = _iota(x.shape, 1) >> 6
    return jnp.concatenate([jnp.where(head == g, x, 0.0).astype(BF16) for g in range(Q_PER_KV)], axis=0)


def _unstack_heads(y):
    head = _iota((CHUNK, KV_W), 1) >> 6
    out = jnp.where(head == 0, y[0:CHUNK], 0.0)
    for g in range(1, Q_PER_KV):
        out = out + jnp.where(head == g, y[g * CHUNK:(g + 1) * CHUNK], 0.0)
    return out


SOFTMAX_ROWS = 32


def _fill_mask_bias(bias_ref):
    qi = _iota((CHUNK, 2 * CHUNK), 0)
    kj = _iota((CHUNK, 2 * CHUNK), 1)
    diff = qi + CHUNK - kj
    window = (diff >= 0) & (diff < CHUNK)
    bias_ref[0] = jnp.where(window & (kj >= CHUNK), 0.0, NEG)
    bias_ref[1] = jnp.where(window, 0.0, NEG)


def _softmax_rows(s_ref, bias_ref, first, sink_ref, h, c):
    rows = pl.ds(pl.multiple_of(c * SOFTMAX_ROWS, SOFTMAX_ROWS), SOFTMAX_ROWS)
    in_block = pl.ds(pl.multiple_of((c % (CHUNK // SOFTMAX_ROWS)) * SOFTMAX_ROWS, SOFTMAX_ROWS), SOFTMAX_ROWS)
    sink = sink_ref[0, h * Q_PER_KV + c // (CHUNK // SOFTMAX_ROWS)]
    s = s_ref[rows, :] + bias_ref[first, in_block, :]
    m = jnp.maximum(jnp.max(s, axis=-1, keepdims=True), sink)
    p = jnp.exp(s - m)
    es = jnp.exp(sink - m)
    inv = 1.0 / (jnp.sum(p, axis=-1, keepdims=True) + es)
    return rows, p * inv, es * inv


def _fill_keys(p_ref, gk_ref, krep, vrep, seq):
    mean_k = _head_mean_matrix(KV2)
    reps = [_repeat_matrix(h) for h in range(N_KV)]
    for h in range(N_KV):
        krep[h][0:CHUNK, :] = jnp.zeros((CHUNK, KV_W), BF16)
        vrep[h][0:CHUNK, :] = jnp.zeros((CHUNK, KV_W), BF16)
    rows = min(seq, 4 * CHUNK)

    def piece(j, carry):
        r0 = pl.multiple_of(j * rows, CHUNK)
        nk, _ = _head_rms(p_ref[pl.ds(r0, rows), K_OFF:K_OFF + KV2].astype(F32), mean_k)
        kn = (nk * gk_ref[...]).astype(BF16)
        v = p_ref[pl.ds(r0, rows), V_OFF:V_OFF + KV2].astype(BF16)
        r1 = pl.multiple_of(r0 + CHUNK, CHUNK)
        for h in range(N_KV):
            krep[h][pl.ds(r1, rows), :] = _dot(kn, reps[h]).astype(BF16)
            vrep[h][pl.ds(r1, rows), :] = _dot(v, reps[h]).astype(BF16)
        return carry

    lax.fori_loop(0, seq // rows, piece, 0)


def attn_fwd(pqkv, gq_t, gk_t, sinks, *, seq, name, comms=()):
    T = pqkv.shape[0]
    nb = seq // CHUNK

    def body(p_ref, gq_ref, gk_ref, sink_ref, o_ref, k0, k1, v0, v1, bias_ref, s_ref, pr_ref):
        krep, vrep = [k0, k1], [v0, v1]
        _fill_keys(p_ref, gk_ref, krep, vrep, seq)
        _fill_mask_bias(bias_ref)
        mean_q = _head_mean_matrix(D_B)

        def block(i, carry):
            r0 = pl.multiple_of(i * CHUNK, CHUNK)
            first = jnp.minimum(i, 1)
            nq, _ = _head_rms(p_ref[pl.ds(r0, CHUNK), 0:D_B].astype(F32), mean_q)
            qn = nq * (gq_ref[...] * ATT_SCALE)
            for h in range(N_KV):
                qs = _stack_heads(qn[:, h * KV_W:(h + 1) * KV_W])
                s_ref[...] = _dg(qs, krep[h][pl.ds(r0, 2 * CHUNK), :], NT)

                def rows_of(c, carry2):
                    rows, probs, _ = _softmax_rows(s_ref, bias_ref, first, sink_ref, h, c)
                    pr_ref[rows, :] = probs.astype(BF16)
                    return carry2

                lax.fori_loop(0, STACK // SOFTMAX_ROWS, rows_of, 0, unroll=True)
                out = _unstack_heads(_dot(pr_ref[...], vrep[h][pl.ds(r0, 2 * CHUNK), :]))
                o_ref[pl.ds(r0, CHUNK), h * KV_W:(h + 1) * KV_W] = out.astype(BF16)
            return carry

        lax.fori_loop(0, nb, block, 0)

    return _call(
        body, name=name, grid=(T // seq,),
        in_specs=[pl.BlockSpec((seq, QKV_W), lambda b: (b, 0)), pl.BlockSpec(gq_t.shape, lambda b: (0, 0)),
                  pl.BlockSpec(gk_t.shape, lambda b: (0, 0)), pl.BlockSpec(memory_space=pltpu.SMEM)],
        out_specs=[pl.BlockSpec((seq, D_B), lambda b: (b, 0))], out_shape=[jax.ShapeDtypeStruct((T, D_B), BF16)],
        scratch_shapes=[pltpu.VMEM((seq + CHUNK, KV_W), BF16)] * 4
        + [pltpu.VMEM((2, CHUNK, 2 * CHUNK), F32), pltpu.VMEM((STACK, 2 * CHUNK), F32), pltpu.VMEM((STACK, 2 * CHUNK), BF16)],
        args=[pqkv, gq_t, gk_t, sinks], comms=comms)[0]


def mixer_out_fwd(sgu, att, pg, hin, ga, wa, wb, wo, norm, *, seq, tm, name, comms=()):
    T, D = hin.shape
    B, tps = T // seq, seq // tm

    def body(s_ref, t_ref, pg_ref, h_ref, ga_ref, wa_ref, wb_ref, wo_ref, g_ref, sc_ref, sh_ref,
             ya_ref, yb_ref, mg_ref, m_ref, ho_ref, xn_ref):
        ya = _dot(s_ref[...], wa_ref[...])
        yb = _dot(t_ref[...], wb_ref[...])
        merged = (jax.nn.sigmoid(pg_ref[:, 0:D].astype(F32)) * ya
                  + jax.nn.sigmoid(pg_ref[:, D:2 * D].astype(F32)) * yb)
        mg = merged.astype(BF16)
        m = _dot(mg, wo_ref[...])
        ya_ref[...] = ya.astype(BF16)
        yb_ref[...] = yb.astype(BF16)
        mg_ref[...] = mg
        m_ref[...] = m.astype(BF16)
        y = h_ref[...] + ga_ref[0] * m
        ho_ref[...] = y
        xn_ref[...] = _norm_mod(y, g_ref, sc_ref, sh_ref)

    tile = lambda w: pl.BlockSpec((tm, w), lambda i: (i, 0))
    b16 = jax.ShapeDtypeStruct((T, D), BF16)
    return _call(
        body, name=name, grid=(T // tm,),
        in_specs=[tile(D_A), tile(D_B), tile(2 * D), tile(D), _mod_spec(ga, tps), _resident(wa), _resident(wb),
                  _resident(wo), pl.BlockSpec((1, D), lambda i: (0, 0)), _mod_spec(norm[1], tps), _mod_spec(norm[2], tps)],
        out_specs=[tile(D)] * 6, out_shape=[b16, b16, b16, b16, jax.ShapeDtypeStruct((T, D), F32), b16],
        args=[sgu, att, pg, hin, ga[0], wa, wb, wo, norm[0], norm[1][0], norm[2][0]], comms=comms)


def ffn_bwd_act(dh, f, ga, wd, s, q, *, seq, tm, name, comms=()):
    T, D = dh.shape
    nq, fs, _ = wd.shape
    B, tps = T // seq, seq // tm

    def body(dh_ref, f_ref, ga_ref, wd_ref, s_ref, q_ref, dg_ref, du_ref, df_ref, dga_ref):
        i = pl.program_id(0)
        d = dh_ref[...]
        df = ((0.5 * ga_ref[0]) * d).astype(BF16)
        df_ref[...] = df
        part = 0.5 * jnp.sum(d * f_ref[...].astype(F32), axis=0, keepdims=True)
        for j in range(nq):
            da = _dg(df, wd_ref[j], NT)
            du_ref[j] = (da * s_ref[j].astype(F32)).astype(BF16)
            dg_ref[j] = (da * q_ref[j].astype(F32)).astype(BF16)

        @pl.when(i % tps == 0)
        def _():
            dga_ref[0] = part

        @pl.when(i % tps != 0)
        def _():
            dga_ref[0] += part

    tile = pl.BlockSpec((tm, D), lambda i: (i, 0))
    per_seq = pl.BlockSpec((1, 1, D), lambda i: (i // tps, 0, 0))
    act = pl.BlockSpec((nq, tm, fs), lambda i: (0, i, 0))
    o_shape = jax.ShapeDtypeStruct((nq, T, fs), BF16)
    dg, du, df, dga = _call(
        body, name=name, grid=(T // tm,), in_specs=[tile, tile, _mod_spec(ga, tps), _resident(wd), act, act],
        out_specs=[act, act, tile, per_seq],
        out_shape=[o_shape, o_shape, jax.ShapeDtypeStruct((T, D), BF16), jax.ShapeDtypeStruct((B, 1, D), F32)],
        args=[dh, f, ga[0], wd, s, q], comms=comms)
    return dg, du, df, dga.reshape(B, D)


def mm_tn(pairs, *, tt, name, comms=()):
    n = len(pairs)
    flat = []
    for pair in pairs:
        for a in pair:
            if not any(a is b for b in flat):
                flat.append(a)
    where = lambda a: [k for k, b in enumerate(flat) if a is b][0]
    nq = max([a.shape[0] for a in flat if a.ndim == 3] + [1])
    T = flat[0].shape[-2]
    tt = min(tt, T)
    nt = T // tt
    stacked = [x.ndim == 3 or y.ndim == 3 for x, y in pairs]

    def body(*refs):
        in_refs, o_refs, accs = refs[:len(flat)], refs[len(flat):len(flat) + n], refs[len(flat) + n:]
        t = pl.program_id(1)
        value = lambda a: in_refs[where(a)][0] if a.ndim == 3 else in_refs[where(a)][...]

        def put(j, v):
            if stacked[j]:
                o_refs[j][0] = v.astype(BF16)
            else:
                o_refs[j][...] = v.astype(BF16)

        for j, (x, y) in enumerate(pairs):
            part = _dg(value(x), value(y), TN)
            if nt == 1:
                put(j, part)
                continue

            @pl.when(t == 0)
            def _():
                accs[j][...] = part

            @pl.when(t != 0)
            def _():
                accs[j][...] += part

        if nt > 1:
            @pl.when(t == nt - 1)
            def _():
                for j in range(n):
                    put(j, accs[j][...])

    def spec(a):
        if a.ndim == 3:
            return pl.BlockSpec((1, tt, a.shape[2]), lambda q, t: (q, t, 0))
        return pl.BlockSpec((tt, a.shape[1]), lambda q, t: (t, 0))

    out_specs, out_shape = [], []
    for j, (x, y) in enumerate(pairs):
        K, N = x.shape[-1], y.shape[-1]
        if stacked[j]:
            out_specs.append(pl.BlockSpec((1, K, N), lambda q, t: (q, 0, 0)))
            out_shape.append(jax.ShapeDtypeStruct((nq, K, N), BF16))
        else:
            out_specs.append(pl.BlockSpec((K, N), lambda q, t: (0, 0)))
            out_shape.append(jax.ShapeDtypeStruct((K, N), BF16))
    return _call(
        body, name=name, grid=(nq, nt), in_specs=[spec(a) for a in flat],
        out_specs=out_specs, out_shape=out_shape,
        scratch_shapes=[pltpu.VMEM((x.shape[-1], y.shape[-1]), F32) for x, y in pairs] if nt > 1 else [],
        args=flat, comms=comms)


def dx_norm_bwd(dys, ws, hin, dh_out, g, sc, *, seq, tm, name, comms=()):
    T, D = hin.shape
    B, tps = T // seq, seq // tm
    n = len(dys)

    def body(*refs):
        dy_refs, w_refs = refs[:n], refs[n:2 * n]
        h_ref, dho_ref, g_ref, sc_ref, dhi_ref, dsh_ref, dsc_ref, dgn_ref = refs[2 * n:]
        i = pl.program_id(0)
        dxn = None
        for j in range(n):
            if dys[j].ndim == 3:
                for q in range(dys[j].shape[0]):
                    part = _dot(dy_refs[j][q], w_refs[j][q])
                    dxn = part if dxn is None else dxn + part
            else:
                part = _dot(dy_refs[j][...], w_refs[j][...])
                dxn = part if dxn is None else dxn + part
        x = h_ref[...]
        nx, r = _rms_parts(x)
        gain = g_ref[...]
        one_sc = 1.0 + sc_ref[0]
        dsh = jnp.sum(dxn, axis=0, keepdims=True)
        dsc = jnp.sum(dxn * (nx * gain), axis=0, keepdims=True)
        dgn = jnp.sum(dxn * one_sc * nx, axis=0, keepdims=True)
        dn = dxn * one_sc * gain
        dhi_ref[...] = dho_ref[...] + r * (dn - nx * jnp.mean(dn * nx, axis=-1, keepdims=True))

        @pl.when(i % tps == 0)
        def _():
            dsh_ref[0] = dsh
            dsc_ref[0] = dsc

        @pl.when(i % tps != 0)
        def _():
            dsh_ref[0] += dsh
            dsc_ref[0] += dsc

        @pl.when(i == 0)
        def _():
            dgn_ref[...] = dgn

        @pl.when(i != 0)
        def _():
            dgn_ref[...] += dgn

    def dy_spec(a):
        if a.ndim == 3:
            return pl.BlockSpec((a.shape[0], tm, a.shape[2]), lambda i: (0, i, 0))
        return pl.BlockSpec((tm, a.shape[1]), lambda i: (i, 0))

    tile = pl.BlockSpec((tm, D), lambda i: (i, 0))
    per_seq = pl.BlockSpec((1, 1, D), lambda i: (i // tps, 0, 0))
    row = pl.BlockSpec((1, D), lambda i: (0, 0))
    dhi, dsh, dsc, dgn = _call(
        body, name=name, grid=(T // tm,),
        in_specs=[dy_spec(a) for a in dys] + [_resident(w) for w in ws] + [tile, tile, row, _mod_spec(sc, tps)],
        out_specs=[tile, per_seq, per_seq, row],
        out_shape=[jax.ShapeDtypeStruct((T, D), F32), jax.ShapeDtypeStruct((B, 1, D), F32),
                   jax.ShapeDtypeStruct((B, 1, D), F32), jax.ShapeDtypeStruct((1, D), F32)],
        args=[*dys, *ws, hin, dh_out, g, sc[0]], comms=comms)
    return dhi, dsh.reshape(B, D), dsc.reshape(B, D), dgn


def mixer_out_bwd(dh, m, ga, ya, yb, pg, wa, wb, wo, *, seq, tm, name, comms=()):
    T, D = dh.shape
    B, tps = T // seq, seq // tm

    def body(dh_ref, m_ref, ga_ref, ya_ref, yb_ref, pg_ref, wa_ref, wb_ref, wo_ref,
             dg_ref, dya_ref, dyb_ref, ds_ref, dt_ref, dm_ref, dga_ref):
        i = pl.program_id(0)
        d = dh_ref[...]
        dm = (ga_ref[0] * d).astype(BF16)
        dm_ref[...] = dm
        part = jnp.sum(d * m_ref[...].astype(F32), axis=0, keepdims=True)

        @pl.when(i % tps == 0)
        def _():
            dga_ref[0] = part

        @pl.when(i % tps != 0)
        def _():
            dga_ref[0] += part

        dmg = _dg(dm, wo_ref[...], NT)
        sa = jax.nn.sigmoid(pg_ref[:, 0:D].astype(F32))
        sb = jax.nn.sigmoid(pg_ref[:, D:2 * D].astype(F32))
        dg_ref[:, 0:D] = (dmg * ya_ref[...].astype(F32) * (sa * (1.0 - sa))).astype(BF16)
        dg_ref[:, D:2 * D] = (dmg * yb_ref[...].astype(F32) * (sb * (1.0 - sb))).astype(BF16)
        dya = (dmg * sa).astype(BF16)
        dyb = (dmg * sb).astype(BF16)
        dya_ref[...] = dya
        dyb_ref[...] = dyb
        ds_ref[...] = _dg(dya, wa_ref[...], NT).astype(BF16)
        dt_ref[...] = _dg(dyb, wb_ref[...], NT).astype(BF16)

    tile = lambda w: pl.BlockSpec((tm, w), lambda i: (i, 0))
    per_seq = pl.BlockSpec((1, 1, D), lambda i: (i // tps, 0, 0))
    dg, dya, dyb, ds, dt, dm, dga = _call(
        body, name=name, grid=(T // tm,),
        in_specs=[tile(D), tile(D), _mod_spec(ga, tps), tile(D), tile(D), tile(2 * D), _resident(wa), _resident(wb),
                  _resident(wo)],
        out_specs=[tile(2 * D), tile(D), tile(D), tile(D_A), tile(D_B), tile(D), per_seq],
        out_shape=[jax.ShapeDtypeStruct((T, 2 * D), BF16), jax.ShapeDtypeStruct((T, D), BF16),
                   jax.ShapeDtypeStruct((T, D), BF16), jax.ShapeDtypeStruct((T, D_A), BF16),
                   jax.ShapeDtypeStruct((T, D_B), BF16), jax.ShapeDtypeStruct((T, D), BF16),
                   jax.ShapeDtypeStruct((B, 1, D), F32)],
        args=[dh, m, ga[0], ya, yb, pg, wa, wb, wo], comms=comms)
    return dg, dya, dyb, ds, dt, dm, dga.reshape(B, D)


def sgu_bwd(puv, dsgu, gln, bln, ws, bs_t, *, cpb, name, comms=()):
    T = puv.shape[0]
    gd = D_A // N_GROUPS
    tm = cpb * CHUNK

    def body(p_ref, ds_ref, gln_ref, bln_ref, ws_ref, bs_ref, d_ref, dws_ref, dz_ref, dgl_ref, dbl_ref):
        i = pl.program_id(0)
        causal = _causal()
        wf = [jnp.where(causal, ws_ref[g], 0.0) for g in range(N_GROUPS)]
        wm = [w.astype(BF16) for w in wf]
        wt = [w.T.astype(BF16) for w in wf]
        dws = [jnp.zeros((CHUNK, CHUNK), F32) for _ in range(N_GROUPS)]
        dzs = jnp.zeros((CHUNK, D_A), F32)
        dgl = jnp.zeros((1, D_A), F32)
        dbl = jnp.zeros((1, D_A), F32)
        for j in range(cpb):
            rows = slice(j * CHUNK, (j + 1) * CHUNK)
            au = p_ref[rows, 0:D_A].astype(F32)
            av = p_ref[rows, D_A:2 * D_A].astype(F32)
            u, tu = _gelu_parts(au)
            vv, tv = _gelu_parts(av)
            vhat, rstd = _layer_norm_parts(vv)
            vn = (vhat * gln_ref[...] + bln_ref[...]).astype(BF16)
            dsg = ds_ref[rows, :].astype(F32)
            dz = dsg * u
            dzb = dz.astype(BF16)
            zs, dvns = [], []
            for g in range(N_GROUPS):
                cols = slice(g * gd, (g + 1) * gd)
                zs.append(_dot(wm[g], vn[:, cols]) + bs_ref[:, g:g + 1])
                dws[g] = dws[g] + _dg(dzb[:, cols], vn[:, cols], NT)
                dvns.append(_dot(wt[g], dzb[:, cols]))
            z = jnp.concatenate(zs, axis=1)
            dvn = jnp.concatenate(dvns, axis=1)
            d_ref[rows, 0:D_A] = (dsg * z * _dgelu(au, tu)).astype(BF16)
            dvh = dvn * gln_ref[...]
            dvv = rstd * (dvh - jnp.mean(dvh, axis=-1, keepdims=True)
                          - vhat * jnp.mean(dvh * vhat, axis=-1, keepdims=True))
            d_ref[rows, D_A:2 * D_A] = (dvv * _dgelu(av, tv)).astype(BF16)
            dzs = dzs + dz
            dgl = dgl + jnp.sum(dvn * vhat, axis=0, keepdims=True)
            dbl = dbl + jnp.sum(dvn, axis=0, keepdims=True)

        @pl.when(i == 0)
        def _():
            for g in range(N_GROUPS):
                dws_ref[g] = jnp.where(causal, dws[g], 0.0)
            dz_ref[...] = dzs
            dgl_ref[...] = dgl
            dbl_ref[...] = dbl

        @pl.when(i != 0)
        def _():
            for g in range(N_GROUPS):
                dws_ref[g] += jnp.where(causal, dws[g], 0.0)
            dz_ref[...] += dzs
            dgl_ref[...] += dgl
            dbl_ref[...] += dbl

    full = lambda a: pl.BlockSpec(a.shape, lambda i: (0,) * a.ndim)
    acc = lambda s: pl.BlockSpec(s, lambda i: (0,) * len(s))
    return _call(
        body, name=name, grid=(T // tm,),
        in_specs=[pl.BlockSpec((tm, 2 * D_A), lambda i: (i, 0)), pl.BlockSpec((tm, D_A), lambda i: (i, 0)),
                  full(gln), full(bln), full(ws), full(bs_t)],
        out_specs=[pl.BlockSpec((tm, 2 * D_A), lambda i: (i, 0)), acc((N_GROUPS, CHUNK, CHUNK)), acc((CHUNK, D_A)),
                   acc((1, D_A)), acc((1, D_A))],
        out_shape=[jax.ShapeDtypeStruct((T, 2 * D_A), BF16), jax.ShapeDtypeStruct((N_GROUPS, CHUNK, CHUNK), F32),
                   jax.ShapeDtypeStruct((CHUNK, D_A), F32), jax.ShapeDtypeStruct((1, D_A), F32),
                   jax.ShapeDtypeStruct((1, D_A), F32)],
        args=[puv, dsgu, gln, bln, ws, bs_t], comms=comms)


def attn_bwd(pqkv, datt, gq_t, gk_t, sinks, *, seq, name, comms=()):
    T = pqkv.shape[0]
    nb = seq // CHUNK

    def body(p_ref, do_ref, gq_ref, gk_ref, sink_ref, d_ref, dgq_ref, dgk_ref, dsk_ref,
             k0, k1, v0, v1, dk0, dk1, dv0, dv1, dgq_s, dsk_s, bias_ref, s_ref, dp_ref, pr_ref, ds_ref):
        b = pl.program_id(0)
        krep, vrep, dkacc, dvacc = [k0, k1], [v0, v1], [dk0, dk1], [dv0, dv1]
        _fill_keys(p_ref, gk_ref, krep, vrep, seq)
        _fill_mask_bias(bias_ref)
        for h in range(N_KV):
            dkacc[h][...] = jnp.zeros_like(dkacc[h])
            dvacc[h][...] = jnp.zeros_like(dvacc[h])
        dgq_s[...] = jnp.zeros_like(dgq_s)
        dsk_s[...] = jnp.zeros_like(dsk_s)
        mean_q = _head_mean_matrix(D_B)
        lane = _iota((1, 128), 1)

        def block(i, carry):
            r0 = pl.multiple_of(i * CHUNK, CHUNK)
            first = jnp.minimum(i, 1)
            nq, rq = _head_rms(p_ref[pl.ds(r0, CHUNK), 0:D_B].astype(F32), mean_q)
            qn = nq * (gq_ref[...] * ATT_SCALE)
            dqn_parts = []
            for h in range(N_KV):
                cols = slice(h * KV_W, (h + 1) * KV_W)
                qs = _stack_heads(qn[:, cols])
                kk = krep[h][pl.ds(r0, 2 * CHUNK), :]
                dos = _stack_heads(do_ref[pl.ds(r0, CHUNK), cols].astype(F32))
                s_ref[...] = _dg(qs, kk, NT)
                dp_ref[...] = _dg(dos, vrep[h][pl.ds(r0, 2 * CHUNK), :], NT)

                def rows_of(c, carry2):
                    rows, probs, psink = _softmax_rows(s_ref, bias_ref, first, sink_ref, h, c)
                    dp = dp_ref[rows, :]
                    dr = jnp.sum(probs * dp, axis=-1, keepdims=True)
                    pr_ref[rows, :] = probs.astype(BF16)
                    ds_ref[rows, :] = (probs * (dp - dr)).astype(BF16)
                    head = h * Q_PER_KV + c // (CHUNK // SOFTMAX_ROWS)
                    dsk_s[...] += jnp.where(lane == head, -jnp.sum(psink * dr), 0.0)
                    return carry2

                lax.fori_loop(0, STACK // SOFTMAX_ROWS, rows_of, 0, unroll=True)
                dqn_parts.append(_unstack_heads(_dot(ds_ref[...], kk)))
                dkacc[h][pl.ds(r0, 2 * CHUNK), :] += _dg(ds_ref[...], qs, TN)
                dvacc[h][pl.ds(r0, 2 * CHUNK), :] += _dg(pr_ref[...], dos, TN)
            dqn = jnp.concatenate(dqn_parts, axis=1) * ATT_SCALE
            dn = dqn * gq_ref[...]
            d_ref[pl.ds(r0, CHUNK), 0:D_B] = (rq * (dn - nq * _dot_split(dn * nq, mean_q))).astype(BF16)
            dgq_s[...] += jnp.sum(dqn * nq, axis=0, keepdims=True)
            return carry

        lax.fori_loop(0, nb, block, 0)

        mean_k = _head_mean_matrix(KV2)
        folds = [_fold_matrix(h) for h in range(N_KV)]
        rows = min(seq, 4 * CHUNK)

        def piece(j, dgk):
            r0 = pl.multiple_of(j * rows, CHUNK)
            r1 = pl.multiple_of(r0 + CHUNK, CHUNK)
            dkn = _dot_split(dkacc[0][pl.ds(r1, rows), :], folds[0]) + _dot_split(dkacc[1][pl.ds(r1, rows), :], folds[1])
            dv = _dot_split(dvacc[0][pl.ds(r1, rows), :], folds[0]) + _dot_split(dvacc[1][pl.ds(r1, rows), :], folds[1])
            nk, rk = _head_rms(p_ref[pl.ds(r0, rows), K_OFF:K_OFF + KV2].astype(F32), mean_k)
            dn = dkn * gk_ref[...]
            d_ref[pl.ds(r0, rows), K_OFF:K_OFF + KV2] = (rk * (dn - nk * _dot_split(dn * nk, mean_k))).astype(BF16)
            d_ref[pl.ds(r0, rows), V_OFF:V_OFF + KV2] = dv.astype(BF16)
            return dgk + jnp.sum(dkn * nk, axis=0, keepdims=True)

        dgk = lax.fori_loop(0, seq // rows, piece, jnp.zeros((1, KV2), F32))

        @pl.when(b == 0)
        def _():
            dgq_ref[...] = dgq_s[...]
            dgk_ref[...] = dgk
            dsk_ref[...] = jnp.broadcast_to(dsk_s[...], dsk_ref.shape)

        @pl.when(b != 0)
        def _():
            dgq_ref[...] += dgq_s[...]
            dgk_ref[...] += dgk
            dsk_ref[...] += jnp.broadcast_to(dsk_s[...], dsk_ref.shape)

    acc = lambda s: pl.BlockSpec(s, lambda b: (0, 0))
    return _call(
        body, name=name, grid=(T // seq,),
        in_specs=[pl.BlockSpec((seq, QKV_W), lambda b: (b, 0)), pl.BlockSpec((seq, D_B), lambda b: (b, 0)),
                  pl.BlockSpec(gq_t.shape, lambda b: (0, 0)), pl.BlockSpec(gk_t.shape, lambda b: (0, 0)),
                  pl.BlockSpec(memory_space=pltpu.SMEM)],
        out_specs=[pl.BlockSpec((seq, QKV_W), lambda b: (b, 0)), acc((1, D_B)), acc((1, KV2)), acc((8, 128))],
        out_shape=[jax.ShapeDtypeStruct((T, QKV_W), BF16), jax.ShapeDtypeStruct((1, D_B), F32),
                   jax.ShapeDtypeStruct((1, KV2), F32), jax.ShapeDtypeStruct((8, 128), F32)],
        scratch_shapes=[pltpu.VMEM((seq + CHUNK, KV_W), BF16)] * 4 + [pltpu.VMEM((seq + CHUNK, KV_W), F32)] * 4
        + [pltpu.VMEM((1, D_B), F32), pltpu.VMEM((1, 128), F32), pltpu.VMEM((2, CHUNK, 2 * CHUNK), F32)]
        + [pltpu.VMEM((STACK, 2 * CHUNK), F32)] * 2 + [pltpu.VMEM((STACK, 2 * CHUNK), BF16)] * 2,
        args=[pqkv, datt, gq_t, gk_t, sinks], comms=comms)


def ada_fwd(c_all, w, b, *, name):
    nb, D = c_all.shape
    N = w.shape[1]
    tn = N // 3

    def body(c_ref, w_ref, b_ref, o_ref):
        c = c_ref[...]
        cond = (c * jax.nn.sigmoid(c)).astype(BF16)
        o_ref[...] = _dot(cond, w_ref[...].astype(BF16)) + b_ref[...]

    return _call(
        body, name=name, grid=(3,),
        in_specs=[pl.BlockSpec((nb, D), lambda j: (0, 0)), pl.BlockSpec((D, tn), lambda j: (0, j)),
                  pl.BlockSpec((1, tn), lambda j: (0, j))],
        out_specs=[pl.BlockSpec((nb, tn), lambda j: (0, j))], out_shape=[jax.ShapeDtypeStruct((nb, N), F32)],
        args=[c_all, w, b])[0]


def ada_bwd(c_all, dmods_all, dmods_mine, gain_rows, *, name, comms=()):
    nb, D = c_all.shape
    NA = dmods_all.shape[1]
    N = dmods_mine.shape[1]
    tn = N // 3

    def body(c_ref, da_ref, dm_ref, gr_ref, db_ref, dw_ref, dgn_ref):
        c = c_ref[...]
        cond = (c * jax.nn.sigmoid(c)).astype(BF16)
        dw_ref[...] = _dg(cond, dm_ref[...].astype(BF16), TN)
        db_ref[...] = jnp.sum(da_ref[...], axis=0, keepdims=True)
        total = gr_ref[0:1, :]
        for d in range(1, N_DEV):
            total = total + gr_ref[d:d + 1, :]
        dgn_ref[...] = total

    return _call(
        body, name=name, grid=(3,),
        in_specs=[pl.BlockSpec((nb, D), lambda j: (0, 0)), pl.BlockSpec((nb, NA), lambda j: (0, 0)),
                  pl.BlockSpec((nb, tn), lambda j: (0, j)), pl.BlockSpec((N_DEV, D), lambda j: (0, 0))],
        out_specs=[pl.BlockSpec((1, NA), lambda j: (0, 0)), pl.BlockSpec((D, tn), lambda j: (0, j)),
                   pl.BlockSpec((1, D), lambda j: (0, 0))],
        out_shape=[jax.ShapeDtypeStruct((1, NA), F32), jax.ShapeDtypeStruct((D, N), F32),
                   jax.ShapeDtypeStruct((1, D), F32)],
        args=[c_all, dmods_all, dmods_mine, gain_rows], comms=comms)


def adamw(items, *, steps, name, comms=()):
    n = len(items)
    c1 = 1.0 / (1.0 - ADAM_B1 ** ADAM_STEP)
    c2 = 1.0 / (1.0 - ADAM_B2 ** ADAM_STEP)

    def body(*refs):
        for j in range(n):
            w_ref, g_ref, m_ref, v_ref = refs[4 * j:4 * j + 4]
            d_ref, mo_ref, vo_ref = refs[4 * n + 3 * j:4 * n + 3 * j + 3]
            gg = g_ref[...]
            mn = ADAM_B1 * m_ref[...] + (1.0 - ADAM_B1) * gg
            vn = ADAM_B2 * v_ref[...] + (1.0 - ADAM_B2) * (gg * gg)
            mo_ref[...] = mn
            vo_ref[...] = vn
            d_ref[...] = -ADAM_LR * ((mn * c1) / (jnp.sqrt(vn * c2) + ADAM_EPS) + ADAM_WD * w_ref[...])

    in_specs, out_specs, out_shape, args = [], [], [], []
    for item in items:
        R, C = item[0].shape
        blk = pl.BlockSpec((R // steps, C), lambda i: (i, 0))
        in_specs += [blk] * 4
        out_specs += [blk] * 3
        out_shape += [jax.ShapeDtypeStruct((R, C), F32)] * 3
        args += list(item)
    res = _call(body, name=name, grid=(steps,), in_specs=in_specs, out_specs=out_specs, out_shape=out_shape, args=args,
                comms=comms)
    return [tuple(res[3 * j:3 * j + 3]) for j in range(n)]


def _place():
    return lax.axis_index("x"), lax.axis_index("y"), lax.axis_index("c")


def _flip(v, bit):
    return 1 - v if bit else v


def _other_chips(mx, my):
    return [(_flip(mx, k & 2), _flip(my, k & 1)) for k in range(1, N_QUAD)]


def _remote(src, dst, send_sem, recv_sem, peer):
    return pltpu.make_async_remote_copy(src_ref=src, dst_ref=dst, send_sem=send_sem, recv_sem=recv_sem,
                                        device_id=peer, device_id_type=MESH)


def ag8_comm(x):
    def copies(srcs, lands, ss, rs, only_sends=False):
        mx, my, mc = _place()
        me = 4 * mx + 2 * my + mc
        local = pltpu.make_async_copy(srcs[0], lands[0].at[me], ss.at[N_DEV - 1])
        sends, recvs = [], []
        for k in range(1, N_DEV):
            peer = (_flip(mx, k & 4), _flip(my, k & 2), _flip(mc, k & 1))
            sends.append(_remote(srcs[0], lands[0].at[me], ss.at[k - 1], rs.at[k - 1], peer))
            if not only_sends:
                recvs.append(_remote(srcs[0], lands[0].at[4 * peer[0] + 2 * peer[1] + peer[2]], ss.at[k - 1], rs.at[k - 1], peer))
        return local, sends, recvs

    def start(srcs, bufs, lands, ss, rs):
        local, sends, _ = copies(srcs, lands, ss, rs, only_sends=True)
        local.start()
        for cp in sends:
            cp.start()

    def finish(srcs, bufs, lands, ss, rs):
        local, sends, recvs = copies(srcs, lands, ss, rs)
        for cp in recvs:
            cp.wait_recv()
        for cp in sends:
            cp.wait_send()
        local.wait()

    return Comm(srcs=[x], land_shapes=[jax.ShapeDtypeStruct((N_DEV,) + x.shape, x.dtype)], n_sems=N_DEV,
                start=start, finish=finish)


def sum8(stacked, *, name):
    _, r, n = stacked.shape

    def body(s_ref, o_ref):
        total = s_ref[0]
        for d in range(1, N_DEV):
            total = total + s_ref[d]
        o_ref[...] = total

    return _call(body, name=name, grid=(), in_specs=[pl.BlockSpec(memory_space=pltpu.VMEM)],
                 out_specs=[pl.BlockSpec(memory_space=pltpu.VMEM)], out_shape=[jax.ShapeDtypeStruct((r, n), F32)],
                 args=[stacked])[0]


def cast_into_stacks(ws, quad, *, name, comms=()):
    steps = 4

    def body(q_ref, *refs):
        for w_ref, o_ref in zip(refs[:len(ws)], refs[len(ws):]):
            o_ref[0] = w_ref[...].astype(BF16)

    return _call(
        body, name=name, grid=(steps,), prefetch=[quad],
        in_specs=[pl.BlockSpec((w.shape[0] // steps, w.shape[1]), lambda i, q: (i, 0)) for w in ws],
        out_specs=[pl.BlockSpec((1, w.shape[0] // steps, w.shape[1]), lambda i, q: (q[0], i, 0)) for w in ws],
        out_shape=[jax.ShapeDtypeStruct((N_QUAD,) + w.shape, BF16) for w in ws], args=list(ws), comms=comms)


def gather_comm(stacks):
    n = len(stacks)

    def copies(bufs, ss, rs, only_sends=False):
        mx, my, mc = _place()
        q = 2 * mx + my
        sibling = (mx, my, 1 - mc)
        ici_send, ici_recv, fwd_send, fwd_recv = [], [], [], []
        for p in range(n):
            hr = stacks[p].shape[1] // 2
            mine, other = pl.ds(mc * hr, hr), pl.ds((1 - mc) * hr, hr)
            for k, chip in enumerate(_other_chips(mx, my)):
                qk = 2 * chip[0] + chip[1]
                peer = (chip[0], chip[1], mc)
                own, landed, theirs = bufs[p].at[q, mine, :], bufs[p].at[qk, mine, :], bufs[p].at[qk, other, :]
                ici_send.append(_remote(own, own, ss.at[6 * p + k], rs.at[6 * p + k], peer))
                if only_sends:
                    continue
                ici_recv.append(_remote(own, landed, ss.at[6 * p + k], rs.at[6 * p + k], peer))
                fwd_send.append(_remote(landed, landed, ss.at[6 * p + 3 + k], rs.at[6 * p + 3 + k], sibling))
                fwd_recv.append(_remote(theirs, theirs, ss.at[6 * p + 3 + k], rs.at[6 * p + 3 + k], sibling))
        return ici_send, ici_recv, fwd_send, fwd_recv

    def start(srcs, bufs, lands, ss, rs):
        for cp in copies(bufs, ss, rs, only_sends=True)[0]:
            cp.start()

    def finish(srcs, bufs, lands, ss, rs):
        ici_send, ici_recv, fwd_send, fwd_recv = copies(bufs, ss, rs)
        for arrived, onward in zip(ici_recv, fwd_send):
            arrived.wait_recv()
            onward.start()
        for cp in fwd_recv:
            cp.wait_recv()
        for cp in ici_send + fwd_send:
            cp.wait_send()

    return Comm(bufs=stacks, n_sems=6 * n, start=start, finish=finish)


def rs_pair_comm(gs):
    n = len(gs)

    def copies(srcs, lands, ss, rs):
        mx, my, mc = _place()
        out = []
        for p in range(n):
            hr = gs[p].shape[1] // 2
            out.append(_remote(srcs[p].at[:, pl.ds((1 - mc) * hr, hr), :], lands[p], ss.at[p], rs.at[p], (mx, my, 1 - mc)))
        return out

    def start(srcs, bufs, lands, ss, rs):
        for cp in copies(srcs, lands, ss, rs):
            cp.start()

    def finish(srcs, bufs, lands, ss, rs):
        for cp in copies(srcs, lands, ss, rs):
            cp.wait()

    return Comm(srcs=gs, land_shapes=[jax.ShapeDtypeStruct((g.shape[0], g.shape[1] // 2, g.shape[2]), g.dtype) for g in gs],
                n_sems=n, start=start, finish=finish)


def rs_chip_comm(ss_):
    n = len(ss_)

    def copies(srcs, lands, ss, rs):
        mx, my, mc = _place()
        out = []
        for p in range(n):
            for k, chip in enumerate(_other_chips(mx, my)):
                out.append(_remote(srcs[p].at[2 * chip[0] + chip[1]], lands[p].at[k], ss.at[3 * p + k], rs.at[3 * p + k],
                                   (chip[0], chip[1], mc)))
        return out

    def start(srcs, bufs, lands, ss, rs):
        for cp in copies(srcs, lands, ss, rs):
            cp.start()

    def finish(srcs, bufs, lands, ss, rs):
        for cp in copies(srcs, lands, ss, rs):
            cp.wait()

    return Comm(srcs=ss_, land_shapes=[jax.ShapeDtypeStruct((N_QUAD - 1,) + s.shape[1:], s.dtype) for s in ss_],
                n_sems=3 * n, start=start, finish=finish)


def rs_share_comm(fulls):
    n = len(fulls)

    def copies(bufs, ss, rs, only_sends=False):
        mx, my, mc = _place()
        send, recv = [], []
        for p in range(n):
            hr = fulls[p].shape[0] // 2
            mine, other = bufs[p].at[pl.ds(mc * hr, hr), :], bufs[p].at[pl.ds((1 - mc) * hr, hr), :]
            send.append(_remote(mine, mine, ss.at[p], rs.at[p], (mx, my, 1 - mc)))
            if not only_sends:
                recv.append(_remote(other, other, ss.at[p], rs.at[p], (mx, my, 1 - mc)))
        return send, recv

    def start(srcs, bufs, lands, ss, rs):
        for cp in copies(bufs, ss, rs, only_sends=True)[0]:
            cp.start()

    def finish(srcs, bufs, lands, ss, rs):
        send, recv = copies(bufs, ss, rs)
        for cp in recv:
            cp.wait_recv()
        for cp in send:
            cp.wait_send()

    return Comm(bufs=fulls, n_sems=n, start=start, finish=finish)


def pair_sum(g, recv, mc, *, name):
    nq, R, C = g.shape
    hr = R // 2
    rb = _row_block(hr, 512)
    nb = hr // rb

    def body(s_ref, g_ref, r_ref, o_ref):
        o_ref[...] = (g_ref[...].astype(F32) + r_ref[...].astype(F32)).astype(BF16)

    return pl.pallas_call(
        body, name=name, out_shape=jax.ShapeDtypeStruct((nq, hr, C), BF16),
        grid_spec=pltpu.PrefetchScalarGridSpec(
            num_scalar_prefetch=1, grid=(nq, nb),
            in_specs=[pl.BlockSpec((1, rb, C), lambda q, i, s: (q, s[0] * nb + i, 0)),
                      pl.BlockSpec((1, rb, C), lambda q, i, s: (q, i, 0))],
            out_specs=pl.BlockSpec((1, rb, C), lambda q, i, s: (q, i, 0))),
        compiler_params=pltpu.CompilerParams(dimension_semantics=("arbitrary", "arbitrary"),
                                             vmem_limit_bytes=VMEM_LIMIT_BYTES),
    )(mc, g, recv)


def chip_sum(s, recv, quad_mc, *, name):
    nq, hr, C = s.shape
    rb = _row_block(hr, 256)
    nb = hr // rb

    def body(q_ref, s_ref, r_ref, o_ref):
        total = s_ref[0].astype(F32)
        for k in range(N_QUAD - 1):
            total = total + r_ref[k].astype(F32)
        o_ref[...] = total

    return pl.pallas_call(
        body, name=name, out_shape=jax.ShapeDtypeStruct((2 * hr, C), F32),
        grid_spec=pltpu.PrefetchScalarGridSpec(
            num_scalar_prefetch=1, grid=(nb,),
            in_specs=[pl.BlockSpec((1, rb, C), lambda i, q: (q[0], i, 0)),
                      pl.BlockSpec((N_QUAD - 1, rb, C), lambda i, q: (0, i, 0))],
            out_specs=pl.BlockSpec((rb, C), lambda i, q: (q[1] * nb + i, 0))),
        compiler_params=pltpu.CompilerParams(dimension_semantics=("arbitrary",), vmem_limit_bytes=VMEM_LIMIT_BYTES),
    )(quad_mc, s, recv)


def _stack_cols(w_full):
    K, N = w_full.shape
    return w_full.reshape(K, N_QUAD, N // N_QUAD).transpose(1, 0, 2)


def _unstack_cols(w4):
    nq, K, n = w4.shape
    return w4.transpose(1, 0, 2).reshape(K, nq * n)


def kernel(x, c, w_ada, b_ada, g_norm1, ffn1_w_gate, ffn1_w_up, ffn1_w_down, g_norm2, w_in, g_sgu_ln, b_sgu_ln, w_spatial, b_spatial, g_q, g_k, attn_sinks, w_branch_a, w_branch_b, w_out, g_norm3, ffn2_w_gate, ffn2_w_up, ffn2_w_down, loss_target, m_w_ada, m_b_ada, m_g_norm1, m_ffn1_w_gate, m_ffn1_w_up, m_ffn1_w_down, m_g_norm2, m_w_in, m_g_sgu_ln, m_b_sgu_ln, m_w_spatial, m_b_spatial, m_g_q, m_g_k, m_attn_sinks, m_w_branch_a, m_w_branch_b, m_w_out, m_g_norm3, m_ffn2_w_gate, m_ffn2_w_up, m_ffn2_w_down, v_w_ada, v_b_ada, v_g_norm1, v_ffn1_w_gate, v_ffn1_w_up, v_ffn1_w_down, v_g_norm2, v_w_in, v_g_sgu_ln, v_b_sgu_ln, v_w_spatial, v_b_spatial, v_g_q, v_g_k, v_attn_sinks, v_w_branch_a, v_w_branch_b, v_w_out, v_g_norm3, v_ffn2_w_gate, v_ffn2_w_up, v_ffn2_w_down):
    B, S, D = x.shape
    T = B * S
    n_mod = b_ada.shape[1] // D
    mx, my, mc = _place()
    quad = 2 * mx + my
    mc_arr = jnp.reshape(mc, (1,)).astype(jnp.int32)
    quad_arr = jnp.reshape(quad, (1,)).astype(jnp.int32)
    quad_mc = jnp.stack([quad, mc]).astype(jnp.int32)
    tm = min(512, S)
    tm_small = min(256, S)
    tm_big = min(1024, S)
    tt_half = min(2048, T)
    cpb = min(4, S // CHUNK)

    xt = x.reshape(T, D)
    tgt = loss_target.reshape(T, D)

    tr = lambda a: jnp.swapaxes(a, 1, 2)
    stack_wg1, stack_wu1 = cast_into_stacks([tr(ffn1_w_gate)[0], tr(ffn1_w_up)[0]], quad_arr, name="stack_gu1")
    gather_wg1, gather_wu1 = gather_comm([stack_wg1]), gather_comm([stack_wu1])
    later = [ffn1_w_down, tr(w_in), w_branch_a, w_branch_b, w_out, tr(ffn2_w_gate), tr(ffn2_w_up), ffn2_w_down]
    gather_cond = ag8_comm(c)
    stacks = cast_into_stacks([w[0] for w in later[0:4]], quad_arr, name="stack_a", comms=[gather_wg1, gather_cond])
    (wg1,) = gather_wg1.bufs_out
    c_all = gather_cond.lands[0].reshape(N_DEV * B, D)
    n_ada = w_ada.shape[2]
    b_mine = lax.dynamic_slice(b_ada, (0, quad * n_ada), (1, n_ada))
    mods_part = ada_fwd(c_all, w_ada[0], b_mine, name="ada_fwd")
    gather_mods = ag8_comm(mods_part)
    stacks += cast_into_stacks([w[0] for w in later[4:8]], quad_arr, name="stack_b", comms=[gather_wu1, gather_mods])
    (wu1,), (mods_parts,) = gather_wu1.bufs_out, gather_mods.lands
    gather_wd1, gather_win = gather_comm([stacks[0]]), gather_comm([stacks[1]])
    gather_abo = gather_comm(stacks[2:5])
    gather_wg3, gather_wu3, gather_wd3 = gather_comm([stacks[5]]), gather_comm([stacks[6]]), gather_comm([stacks[7]])
    mods = jnp.concatenate([mods_parts[2 * j] for j in range(N_QUAD)], axis=1)
    me = 4 * mx + 2 * my + mc
    mods = lax.dynamic_slice(mods, (me * B, 0), (B, n_mod * D))
    mods = mods.reshape(B, n_mod, 1, D)
    sh1, sc1, ga1, sh2, sc2, ga2, sh3, sc3, ga3 = [(mods, j) for j in range(n_mod)]
    bs_t = b_spatial[0].T
    ws = w_spatial[0]

    xn1 = norm_mod_fwd(xt, g_norm1, sc1, sh1, seq=S, tm=tm, name="norm1")
    g1, u1, a1 = ffn_up(xn1, wg1, wu1, tm=tm_big, name="ffn1_up", comms=[gather_wd1, gather_abo])
    (wd1,), (wa4, wb4, wo4) = gather_wd1.bufs_out, gather_abo.bufs_out
    wa, wb = _unstack_cols(wa4), _unstack_cols(wb4)
    wo = wo4.reshape(D, D)
    h1, f1, xn2 = ffn_down(a1, wd1, xt, ga1, norm=(g_norm2, sc2, sh2), seq=S, tm=tm, name="ffn1_down",
                           comms=[gather_win])
    (win4,) = gather_win.bufs_out
    win = win4.reshape(-1, D)
    w_uv, w_qkv, w_gt = win[0:2 * D_A], win[2 * D_A:2 * D_A + QKV_W], win[2 * D_A + QKV_W:]
    puv, pqkv, pgt = proj_fwd(xn2, [w_uv, w_qkv, w_gt], tm=tm_small, name="proj", comms=[gather_wg3])
    (wg3,) = gather_wg3.bufs_out
    sgu = sgu_fwd(puv, g_sgu_ln, b_sgu_ln, ws, bs_t, cpb=cpb, name="sgu_fwd")
    gq_t, gk_t = jnp.tile(g_q, (1, N_Q)), jnp.tile(g_k, (1, N_KV))
    att = attn_fwd(pqkv, gq_t, gk_t, attn_sinks, seq=S, name="attn_fwd", comms=[gather_wu3])
    (wu3,) = gather_wu3.bufs_out
    ya, yb, merged, mm, h2, xn3 = mixer_out_fwd(sgu, att, pgt, h1, ga2, wa, wb, wo, (g_norm3, sc3, sh3), seq=S,
                                                tm=tm_small, name="mixer_out", comms=[gather_wd3])
    (wd3,) = gather_wd3.bufs_out
    g3, u3, a3 = ffn_up(xn3, wg3, wu3, tm=tm_big, name="ffn2_up")
    dy, f3, lparts = ffn_down(a3, wd3, h2, ga3, target=tgt, seq=S, tm=tm, name="ffn2_down_loss")
    loss_part = jnp.sum(lparts[:, 0, 0])

    def sums_of(pair, tag):
        return [pair_sum(g, r, mc_arr, name=f"pair_sum_{tag}_{p}") for p, (g, r) in enumerate(zip(pair.srcs, pair.lands))]

    def halves_of(chip, tag):
        return [chip_sum(s, r, quad_mc, name=f"chip_sum_{tag}_{p}") for p, (s, r) in enumerate(zip(chip.srcs, chip.lands))]

    dg3, du3, df3, dga3 = ffn_bwd_act(dy, f3, ga3, wd3, g3, u3, seq=S, tm=tm, name="ffn2_act_bwd")
    (dwd3,) = mm_tn([(a3, df3)], tt=T, name="ffn2_dwd")
    pair_wd3 = rs_pair_comm([dwd3])
    dwg3, dwu3 = mm_tn([(dg3, xn3), (du3, xn3)], tt=tt_half, name="ffn2_dwgu", comms=[pair_wd3])
    chip_wd3 = rs_chip_comm(sums_of(pair_wd3, "wd3"))
    pair_gu3 = rs_pair_comm([dwg3, dwu3])
    dh2, dsh3, dsc3, dgn3 = dx_norm_bwd([dg3, du3], [wg3, wu3], h2, dy, g_norm3, sc3, seq=S, tm=tm, name="ffn2_dx",
                                        comms=[chip_wd3, pair_gu3])
    sum_wg3, sum_wu3 = sums_of(pair_gu3, "gu3")
    chip_wg3, chip_wu3 = rs_chip_comm([sum_wg3]), rs_chip_comm([sum_wu3])

    dgt, dya, dyb, dsgu, datt, dm, dga2 = mixer_out_bwd(dh2, mm, ga2, ya, yb, pgt, wa, wb, wo, seq=S, tm=tm_small,
                                                        name="mixer_out_bwd", comms=[chip_wg3])
    (dwo,) = mm_tn([(merged, dm)], tt=tt_half, name="mixer_dwo")
    (dwa,) = mm_tn([(sgu, dya)], tt=tt_half, name="mixer_dwa")
    (dwb,) = mm_tn([(att, dyb)], tt=tt_half, name="mixer_dwb")
    pair_abo = rs_pair_comm([_stack_cols(dwa), _stack_cols(dwb), dwo.reshape(N_QUAD, D // N_QUAD, D)])
    duv, dws, dzs, dgln, dbln = sgu_bwd(puv, dsgu, g_sgu_ln, b_sgu_ln, ws, bs_t, cpb=cpb, name="sgu_bwd",
                                        comms=[pair_abo])
    chip_abo = rs_chip_comm(sums_of(pair_abo, "abo"))
    dqkv, dgq_h, dgk_h, dsk = attn_bwd(pqkv, datt, gq_t, gk_t, attn_sinks, seq=S, name="attn_bwd",
                                       comms=[chip_wu3, chip_abo])
    dgq = dgq_h.reshape(N_Q, HEAD_DIM).sum(axis=0, keepdims=True)
    dgk = dgk_h.reshape(N_KV, HEAD_DIM).sum(axis=0, keepdims=True)
    share3 = rs_share_comm(halves_of(chip_wg3, "wg3") + halves_of(chip_wu3, "wu3") + halves_of(chip_wd3, "wd3"))
    dwin_uv, dwin_qkv = mm_tn([(duv, xn2), (dqkv, xn2)], tt=tt_half, name="mixer_dwin_a", comms=[share3])
    (dwin_gt,) = mm_tn([(dgt, xn2)], tt=tt_half, name="mixer_dwin_b")
    dwin_parts = [dwin_uv, dwin_qkv, dwin_gt]
    r_wg3, r_wu3, r_wd3 = share3.bufs_out
    pair_win = rs_pair_comm([jnp.concatenate(dwin_parts, axis=0).reshape(win4.shape)])
    dh1, dsh2, dsc2, dgn2 = dx_norm_bwd([duv, dqkv, dgt], [w_uv, w_qkv, w_gt], h1, dh2, g_norm2, sc2, seq=S,
                                        tm=tm, name="mixer_dx", comms=[pair_win])
    chip_win = rs_chip_comm(sums_of(pair_win, "win"))

    dg1, du1, df1, dga1 = ffn_bwd_act(dh1, f1, ga1, wd1, g1, u1, seq=S, tm=tm, name="ffn1_act_bwd", comms=[chip_win])
    share_mix = rs_share_comm(halves_of(chip_win, "win") + halves_of(chip_abo, "abo"))

    db_s = dzs.reshape(CHUNK, N_GROUPS, D_A // N_GROUPS).sum(axis=2).T.reshape(1, N_GROUPS * CHUNK)
    tail = jnp.concatenate([db_s, dgq, dgk, dsk[0:1, 0:N_Q], loss_part.reshape(1, 1)], axis=1)
    tail = jnp.pad(tail, ((0, 0), (0, (-tail.shape[1]) % D)))
    lnrow = jnp.concatenate([dgln, dbln], axis=1)
    lnrow = jnp.pad(lnrow, ((0, 0), (0, (-lnrow.shape[1]) % D)))
    packed = jnp.concatenate([dgn2, dgn3, lnrow.reshape(-1, D), tail.reshape(-1, D), dws.reshape(-1, D)], axis=0)
    n_rows = packed.shape[0]
    packed = jnp.pad(packed, ((0, (-n_rows) % 8), (0, 0)))
    gather_small = ag8_comm(packed)

    dwg1, dwu1 = mm_tn([(dg1, xn1), (du1, xn1)], tt=tt_half, name="ffn1_dwgu", comms=[share_mix, gather_small])
    r_win, r_wa, r_wb, r_wo = share_mix.bufs_out
    small = sum8(gather_small.lands[0], name="sum_small")
    pair_gu1 = rs_pair_comm([dwg1, dwu1])
    (dwd1,) = mm_tn([(a1, df1)], tt=T, name="ffn1_dwd", comms=[pair_gu1])
    chip_gu1 = rs_chip_comm(sums_of(pair_gu1, "gu1"))
    pair_wd1 = rs_pair_comm([dwd1])
    dx, dsh1, dsc1, dgn1 = dx_norm_bwd([dg1, du1], [wg1, wu1], xt, dh1, g_norm1, sc1, seq=S, tm=tm, name="ffn1_dx",
                                       comms=[chip_gu1, pair_wd1])
    chip_wd1 = rs_chip_comm(sums_of(pair_wd1, "wd1"))
    share_gu1 = rs_share_comm(halves_of(chip_gu1, "gu1"))

    names = ["w_ada", "b_ada", "g_norm1", "ffn1_w_gate", "ffn1_w_up", "ffn1_w_down", "g_norm2", "w_in", "g_sgu_ln",
             "b_sgu_ln", "w_spatial", "b_spatial", "g_q", "g_k", "attn_sinks", "w_branch_a", "w_branch_b", "w_out",
             "g_norm3", "ffn2_w_gate", "ffn2_w_up", "ffn2_w_down"]
    weights = dict(zip(names, [w_ada, b_ada, g_norm1, ffn1_w_gate, ffn1_w_up, ffn1_w_down, g_norm2, w_in, g_sgu_ln,
                               b_sgu_ln, w_spatial, b_spatial, g_q, g_k, attn_sinks, w_branch_a, w_branch_b, w_out,
                               g_norm3, ffn2_w_gate, ffn2_w_up, ffn2_w_down]))
    m_in = dict(zip(names, [m_w_ada, m_b_ada, m_g_norm1, m_ffn1_w_gate, m_ffn1_w_up, m_ffn1_w_down, m_g_norm2, m_w_in,
                            m_g_sgu_ln, m_b_sgu_ln, m_w_spatial, m_b_spatial, m_g_q, m_g_k, m_attn_sinks, m_w_branch_a,
                            m_w_branch_b, m_w_out, m_g_norm3, m_ffn2_w_gate, m_ffn2_w_up, m_ffn2_w_down]))
    v_in = dict(zip(names, [v_w_ada, v_b_ada, v_g_norm1, v_ffn1_w_gate, v_ffn1_w_up, v_ffn1_w_down, v_g_norm2, v_w_in,
                            v_g_sgu_ln, v_b_sgu_ln, v_w_spatial, v_b_spatial, v_g_q, v_g_k, v_attn_sinks, v_w_branch_a,
                            v_w_branch_b, v_w_out, v_g_norm3, v_ffn2_w_gate, v_ffn2_w_up, v_ffn2_w_down]))
    transposed = ("ffn1_w_gate", "ffn1_w_up", "w_in", "ffn2_w_gate", "ffn2_w_up")
    grads, delta, new_m, new_v = {}, {}, {}, {}

    def update(group, steps, name, comms=()):
        form = lambda nm, a: (tr(a) if nm in transposed else a)[0].reshape(group[nm].shape)
        res = adamw([(form(nm, weights[nm]), g, form(nm, m_in[nm]), form(nm, v_in[nm])) for nm, g in group.items()],
                    steps=steps, name=name, comms=comms)
        back = lambda nm, a: tr(a[None]) if nm in transposed else a.reshape(weights[nm].shape)
        for (nm, g), (d, mn, vn) in zip(group.items(), res):
            grads[nm], delta[nm], new_m[nm], new_v[nm] = back(nm, g), back(nm, d), back(nm, mn), back(nm, vn)

    dmods = jnp.concatenate([dsh1, dsc1, dga1, dsh2, dsc2, dga2, dsh3, dsc3, dga3], axis=1)
    dmods = jnp.concatenate([dmods, jnp.pad(dgn1, ((0, 0), (0, (n_mod - 1) * D)))], axis=0)
    gather_dmods = ag8_comm(dmods)
    update({"ffn2_w_gate": r_wg3, "ffn2_w_up": r_wu3, "ffn2_w_down": r_wd3, "w_out": r_wo, "w_branch_a": r_wa,
            "w_branch_b": r_wb}, 8, "adamw_early", comms=[chip_wd1, share_gu1, gather_dmods])
    r_wg1, r_wu1 = share_gu1.bufs_out
    (gathered,) = gather_dmods.lands
    dmods_all = gathered[:, 0:B].reshape(N_DEV * B, n_mod * D)
    dmods_mine = lax.dynamic_slice(dmods_all, (0, quad * n_ada), (N_DEV * B, n_ada))
    share_wd1 = rs_share_comm(halves_of(chip_wd1, "wd1"))
    db_ada, dw_ada, g_gn1 = ada_bwd(c_all, dmods_all, dmods_mine, gathered[:, B, 0:D], name="ada_bwd", comms=[share_wd1])
    (r_wd1,) = share_wd1.bufs_out

    ln_rows = lnrow.size // D
    tail_rows = tail.size // D
    r = 2
    g_gn2, g_gn3 = small[0:1], small[1:2]
    ln_flat = small[r:r + ln_rows].reshape(1, -1)
    r += ln_rows
    tail_flat = small[r:r + tail_rows].reshape(1, -1)
    r += tail_rows
    g_ws = small[r:r + dws.size // D].reshape(w_spatial.shape)
    g_gln, g_bln = ln_flat[:, 0:D_A], ln_flat[:, D_A:2 * D_A]
    o = N_GROUPS * CHUNK
    g_bs = tail_flat[:, 0:o].reshape(b_spatial.shape)
    g_gq, g_gk, g_sk = tail_flat[:, o:o + HEAD_DIM], tail_flat[:, o + HEAD_DIM:o + 2 * HEAD_DIM], tail_flat[:, o + 2 * HEAD_DIM:o + 2 * HEAD_DIM + N_Q]
    loss = tail_flat[0, o + 2 * HEAD_DIM + N_Q]

    update({"w_ada": dw_ada}, 16, "adamw_w_ada")
    update({"ffn1_w_gate": r_wg1, "ffn1_w_up": r_wu1, "ffn1_w_down": r_wd1, "w_in": r_win}, 8, "adamw_late")

    update({"b_ada": db_ada, "g_norm1": g_gn1, "g_norm2": g_gn2, "g_norm3": g_gn3, "g_sgu_ln": g_gln,
            "b_sgu_ln": g_bln, "w_spatial": g_ws.reshape(-1, CHUNK), "b_spatial": g_bs.reshape(N_GROUPS, CHUNK),
            "g_q": g_gq, "g_k": g_gk, "attn_sinks": g_sk}, 1, "adamw_small")

    return (loss, dx.reshape(B, S, D), *[grads[nm] for nm in names], *[delta[nm] for nm in names],
            *[new_m[nm] for nm in names], *[new_v[nm] for nm in names])
```

```python
import functools
import math
import operator

import jax
import jax.numpy as jnp
from jax import lax
from jax.experimental import pallas as pl
from jax.experimental.pallas import tpu as pltpu

F32 = jnp.float32
BF16 = jnp.bfloat16
EPS = 1e-6
NEG = -1e30
N_DEV = 8
N_QUAD = 4
D_A = 512
N_GROUPS = 4
CHUNK = 128
HEAD_DIM = 64
N_KV = 2
Q_PER_KV = 4
N_Q = N_KV * Q_PER_KV
D_B = N_Q * HEAD_DIM
QKV_W = D_B + 2 * N_KV * HEAD_DIM
K_OFF = D_B
V_OFF = D_B + N_KV * HEAD_DIM
ATT_SCALE = HEAD_DIM ** -0.5
GELU_K = math.sqrt(2.0 / math.pi)
GELU_C = 0.044715
ADAM_LR, ADAM_B1, ADAM_B2, ADAM_EPS, ADAM_WD, ADAM_STEP = 0.001, 0.9, 0.999, 1e-08, 0.01, 10
VMEM_LIMIT_BYTES = 56 * 1024 * 1024
MESH = pl.DeviceIdType.MESH
NT = (((1,), (1,)), ((), ()))
TN = (((0,), (0,)), ((), ()))
ANY = pl.BlockSpec(memory_space=pl.ANY)


def _dot(a, b):
    return jnp.dot(a, b, preferred_element_type=F32)


def _dg(a, b, dims):
    return lax.dot_general(a, b, dims, preferred_element_type=F32)


def _gelu_parts(x):
    t = jnp.tanh(GELU_K * (x + GELU_C * x * x * x))
    return 0.5 * x * (1.0 + t), t


def _dgelu(x, t):
    return 0.5 * (1.0 + t) + 0.5 * x * (1.0 - t * t) * (GELU_K * (1.0 + 3.0 * GELU_C * x * x))


def _row_block(rows, target):
    b = min(rows, target)
    while rows % b or b % 8:
        b -= 1
    return b


def _mod_spec(mod, tps):
    mods, j = mod
    return pl.BlockSpec((1, None, 1, mods.shape[3]), lambda i: (i // tps, j, 0, 0))


def _resident(a):
    return pl.BlockSpec(a.shape, lambda *_: (0,) * a.ndim, pipeline_mode=pl.Buffered(1))


class Comm:
    def __init__(self, *, srcs=(), bufs=(), land_shapes=(), n_sems, start, finish):
        self.srcs, self.bufs, self.land_shapes = list(srcs), list(bufs), list(land_shapes)
        self.n_sems, self.start, self.finish = n_sems, start, finish
        self.bufs_out, self.lands = None, None


def _call(body, *, name, grid, in_specs, out_specs, out_shape, args, scratch_shapes=(), comms=(), prefetch=()):
    prefetch = list(prefetch)
    in_specs, out_specs, out_shape = list(in_specs), list(out_specs), list(out_shape)
    args, scratch = list(args), list(scratch_shapes)
    n_in, n_out, n_scr = len(args), len(out_shape), len(scratch)
    aliases, layout = {}, []
    for cm in comms:
        i0, o0, s0 = len(args), len(out_shape), len(scratch)
        args += cm.srcs + cm.bufs
        in_specs += [ANY] * (len(cm.srcs) + len(cm.bufs))
        out_shape += [jax.ShapeDtypeStruct(b.shape, b.dtype) for b in cm.bufs] + cm.land_shapes
        out_specs += [ANY] * (len(cm.bufs) + len(cm.land_shapes))
        for j in range(len(cm.bufs)):
            aliases[len(prefetch) + i0 + len(cm.srcs) + j] = o0 + j
        scratch += [pltpu.SemaphoreType.DMA((cm.n_sems,)), pltpu.SemaphoreType.DMA((cm.n_sems,))]
        layout.append((i0, o0, s0))
    n_in_all, n_out_all = len(args), len(out_shape)

    def wrapped(*refs):
        scalars, refs = refs[:len(prefetch)], refs[len(prefetch):]
        ins, outs, scr = refs[:n_in_all], refs[n_in_all:n_in_all + n_out_all], refs[n_in_all + n_out_all:]
        parts = []
        for cm, (i0, o0, s0) in zip(comms, layout):
            nb = len(cm.bufs)
            parts.append((ins[i0:i0 + len(cm.srcs)], outs[o0:o0 + nb], outs[o0 + nb:o0 + nb + len(cm.land_shapes)],
                          scr[s0], scr[s0 + 1]))
        if comms and grid:
            ids = [pl.program_id(d) for d in range(len(grid))]
            first = functools.reduce(operator.and_, [i == 0 for i in ids])
            last = functools.reduce(operator.and_, [i == g - 1 for i, g in zip(ids, grid)])

            @pl.when(first)
            def _():
                for cm, p in zip(comms, parts):
                    cm.start(*p)
        elif comms:
            for cm, p in zip(comms, parts):
                cm.start(*p)
        if body is not None:
            body(*scalars, *ins[:n_in], *outs[:n_out], *scr[:n_scr])
        if comms and grid:
            @pl.when(last)
            def _():
                for cm, p in zip(comms, parts):
                    cm.finish(*p)
        elif comms:
            for cm, p in zip(comms, parts):
                cm.finish(*p)

    if prefetch:
        kwargs = dict(grid_spec=pltpu.PrefetchScalarGridSpec(
            num_scalar_prefetch=len(prefetch), grid=grid, in_specs=in_specs, out_specs=out_specs, scratch_shapes=scratch))
    else:
        kwargs = dict(in_specs=in_specs, out_specs=out_specs, scratch_shapes=scratch, **(dict(grid=grid) if grid else {}))
    sem = ("arbitrary",) * len(grid)
    res = pl.pallas_call(
        wrapped, name=name, out_shape=out_shape, input_output_aliases=aliases,
        compiler_params=pltpu.CompilerParams(dimension_semantics=sem, vmem_limit_bytes=VMEM_LIMIT_BYTES), **kwargs,
    )(*prefetch, *args)
    for cm, (i0, o0, s0) in zip(comms, layout):
        nb = len(cm.bufs)
        cm.bufs_out = list(res[o0:o0 + nb])
        cm.lands = list(res[o0 + nb:o0 + nb + len(cm.land_shapes)])
    return list(res[:n_out])


def run_comms(comms, *, name):
    _call(None, name=name, grid=(), in_specs=[], out_specs=[], out_shape=[], args=[], comms=comms)


def norm_mod_fwd(h, g, sc, sh, *, seq, tm, name, comms=()):
    T, D = h.shape
    B, tps = T // seq, seq // tm

    def body(h_ref, g_ref, sc_ref, sh_ref, o_ref):
        x = h_ref[...]
        r = lax.rsqrt(jnp.mean(x * x, axis=-1, keepdims=True) + EPS)
        y = x * r * g_ref[...]
        o_ref[...] = (y * (1.0 + sc_ref[0]) + sh_ref[0]).astype(o_ref.dtype)

    return _call(
        body, name=name, grid=(T // tm,),
        in_specs=[pl.BlockSpec((tm, D), lambda i: (i, 0)), pl.BlockSpec((1, D), lambda i: (0, 0)),
                  _mod_spec(sc, tps), _mod_spec(sh, tps)],
        out_specs=[pl.BlockSpec((tm, D), lambda i: (i, 0))], out_shape=[jax.ShapeDtypeStruct((T, D), BF16)],
        args=[h, g, sc[0], sh[0]], comms=comms)[0]


def ffn_up(xn, wg, wu, *, tm, name, comms=()):
    T, D = xn.shape
    nq, fs, _ = wg.shape

    def body(x_ref, wg_ref, wu_ref, s_ref, q_ref, a_ref):
        x = x_ref[...]
        g = _dg(x, wg_ref[0], NT)
        u = _dg(x, wu_ref[0], NT)
        sg = jax.nn.sigmoid(g)
        s = g * sg
        s_ref[0] = s.astype(BF16)
        q_ref[0] = (u * (sg + s * (1.0 - sg))).astype(BF16)
        a_ref[0] = (s * u).astype(BF16)

    w_spec = pl.BlockSpec((1, fs, D), lambda q, i: (q, 0, 0))
    o_spec = pl.BlockSpec((1, tm, fs), lambda q, i: (q, i, 0))
    o_shape = jax.ShapeDtypeStruct((nq, T, fs), BF16)
    return _call(
        body, name=name, grid=(nq, T // tm),
        in_specs=[pl.BlockSpec((tm, D), lambda q, i: (i, 0)), w_spec, w_spec],
        out_specs=[o_spec, o_spec, o_spec], out_shape=[o_shape, o_shape, o_shape], args=[xn, wg, wu], comms=comms)


def _norm_mod(y, g_ref, sc_ref, sh_ref):
    nx, _ = _rms_parts(y)
    return ((nx * g_ref[...]) * (1.0 + sc_ref[0]) + sh_ref[0]).astype(BF16)


def ffn_down(a, wd, hin, ga, *, target=None, norm=None, seq, tm, name, comms=()):
    nq, T, fs = a.shape
    D = wd.shape[-1]
    B, tps, nt = T // seq, seq // tm, T // tm

    def body(a_ref, wd_ref, h_ref, ga_ref, *rest):
        rest = list(rest)
        t_ref = rest.pop(0) if target is not None else None
        norm_refs = [rest.pop(0) for _ in range(3)] if norm is not None else None
        o_ref, f_ref = rest[0], rest[1]
        f = _dot(a_ref[0], wd_ref[0])
        for q in range(1, nq):
            f = f + _dot(a_ref[q], wd_ref[q])
        f_ref[...] = f.astype(BF16)
        y = h_ref[...] + (0.5 * ga_ref[0]) * f
        if target is not None:
            e = y - t_ref[...]
            o_ref[...] = e * (1.0 / D)
            rest[2][...] = jnp.full(rest[2].shape, 0.5 * jnp.sum(e * e) * (1.0 / D), F32)
        else:
            o_ref[...] = y
        if norm is not None:
            rest[2][...] = _norm_mod(y, *norm_refs)

    tile = pl.BlockSpec((tm, D), lambda i: (i, 0))
    in_specs = [pl.BlockSpec((nq, tm, fs), lambda i: (0, i, 0)), _resident(wd), tile, _mod_spec(ga, tps)]
    out_specs = [tile, tile]
    out_shape = [jax.ShapeDtypeStruct((T, D), F32), jax.ShapeDtypeStruct((T, D), BF16)]
    args = [a, wd, hin, ga[0]]
    if target is not None:
        in_specs.append(tile)
        args.append(target)
        out_specs.append(pl.BlockSpec((1, 8, 128), lambda i: (i, 0, 0)))
        out_shape.append(jax.ShapeDtypeStruct((nt, 8, 128), F32))
    if norm is not None:
        in_specs += [pl.BlockSpec((1, D), lambda i: (0, 0)), _mod_spec(norm[1], tps), _mod_spec(norm[2], tps)]
        args += [norm[0], norm[1][0], norm[2][0]]
        out_specs.append(tile)
        out_shape.append(jax.ShapeDtypeStruct((T, D), BF16))
    return _call(body, name=name, grid=(nt,), in_specs=in_specs, out_specs=out_specs, out_shape=out_shape, args=args,
                 comms=comms)


def proj_fwd(xn, ws, *, tm, name, comms=()):
    T, D = xn.shape
    n = len(ws)

    def body(*refs):
        x = refs[0][...]
        for j in range(n):
            refs[1 + n + j][...] = _dg(x, refs[1 + j][...], NT).astype(BF16)

    return _call(
        body, name=name, grid=(T // tm,),
        in_specs=[pl.BlockSpec((tm, D), lambda i: (i, 0))] + [_resident(w) for w in ws],
        out_specs=[pl.BlockSpec((tm, w.shape[0]), lambda i: (i, 0)) for w in ws],
        out_shape=[jax.ShapeDtypeStruct((T, w.shape[0]), BF16) for w in ws], args=[xn, *ws], comms=comms)


def _layer_norm_parts(v):
    mu = jnp.mean(v, axis=-1, keepdims=True)
    d = v - mu
    rstd = lax.rsqrt(jnp.mean(d * d, axis=-1, keepdims=True) + EPS)
    return d * rstd, rstd


def _causal():
    row = lax.broadcasted_iota(jnp.int32, (CHUNK, CHUNK), 0)
    col = lax.broadcasted_iota(jnp.int32, (CHUNK, CHUNK), 1)
    return row >= col


def sgu_fwd(puv, gln, bln, ws, bs_t, *, cpb, name, comms=()):
    T = puv.shape[0]
    gd = D_A // N_GROUPS
    tm = cpb * CHUNK

    def body(p_ref, gln_ref, bln_ref, ws_ref, bs_ref, o_ref):
        causal = _causal()
        wm = [jnp.where(causal, ws_ref[g], 0.0).astype(BF16) for g in range(N_GROUPS)]
        for j in range(cpb):
            rows = slice(j * CHUNK, (j + 1) * CHUNK)
            u, _ = _gelu_parts(p_ref[rows, 0:D_A].astype(F32))
            vv, _ = _gelu_parts(p_ref[rows, D_A:2 * D_A].astype(F32))
            vhat, _ = _layer_norm_parts(vv)
            vn = (vhat * gln_ref[...] + bln_ref[...]).astype(BF16)
            for g in range(N_GROUPS):
                cols = slice(g * gd, (g + 1) * gd)
                z = _dot(wm[g], vn[:, cols]) + bs_ref[:, g:g + 1]
                o_ref[rows, cols] = (u[:, cols] * z).astype(BF16)

    full = lambda a: pl.BlockSpec(a.shape, lambda i: (0,) * a.ndim)
    return _call(
        body, name=name, grid=(T // tm,),
        in_specs=[pl.BlockSpec((tm, 2 * D_A), lambda i: (i, 0)), full(gln), full(bln), full(ws), full(bs_t)],
        out_specs=[pl.BlockSpec((tm, D_A), lambda i: (i, 0))], out_shape=[jax.ShapeDtypeStruct((T, D_A), BF16)],
        args=[puv, gln, bln, ws, bs_t], comms=comms)[0]


def _rms_parts(x):
    r = lax.rsqrt(jnp.mean(x * x, axis=-1, keepdims=True) + EPS)
    return x * r, r


KV_W = Q_PER_KV * HEAD_DIM
KV2 = N_KV * HEAD_DIM
STACK = Q_PER_KV * CHUNK


def _iota(shape, dim):
    return lax.broadcasted_iota(jnp.int32, shape, dim)


def _head_mean_matrix(n):
    return jnp.where((_iota((n, n), 0) >> 6) == (_iota((n, n), 1) >> 6), 1.0 / HEAD_DIM, 0.0).astype(BF16)


def _repeat_matrix(h):
    return jnp.where(_iota((KV2, KV_W), 0) == h * HEAD_DIM + (_iota((KV2, KV_W), 1) & (HEAD_DIM - 1)), 1.0, 0.0).astype(BF16)


def _fold_matrix(h):
    j, l = _iota((KV_W, KV2), 0), _iota((KV_W, KV2), 1)
    return jnp.where(((j & (HEAD_DIM - 1)) == (l & (HEAD_DIM - 1))) & ((l >> 6) == h), 1.0, 0.0).astype(BF16)


def _dot_split(x, m):
    hi = x.astype(BF16)
    lo = (x - hi.astype(F32)).astype(BF16)
    return _dot(hi, m) + _dot(lo, m)


def _head_rms(x, mean_matrix):
    r = lax.rsqrt(_dot_split(x * x, mean_matrix) + EPS)
    return x * r, r


def _stack_heads(x):
    head = _iota(x.shape, 1) >> 6
    return jnp.concatenate([jnp.where(head == g, x, 0.0).astype(BF16) for g in range(Q_PER_KV)], axis=0)


def _unstack_heads(y):
    head = _iota((CHUNK, KV_W), 1) >> 6
    out = jnp.where(head == 0, y[0:CHUNK], 0.0)
    for g in range(1, Q_PER_KV):
        out = out + jnp.where(head == g, y[g * CHUNK:(g + 1) * CHUNK], 0.0)
    return out


SOFTMAX_ROWS = 32


def _fill_mask_bias(bias_ref):
    qi = _iota((CHUNK, 2 * CHUNK), 0)
    kj = _iota((CHUNK, 2 * CHUNK), 1)
    diff = qi + CHUNK - kj
    window = (diff >= 0) & (diff < CHUNK)
    bias_ref[0] = jnp.where(window & (kj >= CHUNK), 0.0, NEG)
    bias_ref[1] = jnp.where(window, 0.0, NEG)


def _softmax_rows(s_ref, bias_ref, first, sink_ref, h, c):
    rows = pl.ds(pl.multiple_of(c * SOFTMAX_ROWS, SOFTMAX_ROWS), SOFTMAX_ROWS)
    in_block = pl.ds(pl.multiple_of((c % (CHUNK // SOFTMAX_ROWS)) * SOFTMAX_ROWS, SOFTMAX_ROWS), SOFTMAX_ROWS)
    sink = sink_ref[0, h * Q_PER_KV + c // (CHUNK // SOFTMAX_ROWS)]
    s = s_ref[rows, :] + bias_ref[first, in_block, :]
    m = jnp.maximum(jnp.max(s, axis=-1, keepdims=True), sink)
    p = jnp.exp(s - m)
    es = jnp.exp(sink - m)
    inv = 1.0 / (jnp.sum(p, axis=-1, keepdims=True) + es)
    return rows, p * inv, es * inv


def _fill_keys(p_ref, gk_ref, krep, vrep, seq):
    mean_k = _head_mean_matrix(KV2)
    reps = [_repeat_matrix(h) for h in range(N_KV)]
    for h in range(N_KV):
        krep[h][0:CHUNK, :] = jnp.zeros((CHUNK, KV_W), BF16)
        vrep[h][0:CHUNK, :] = jnp.zeros((CHUNK, KV_W), BF16)
    rows = min(seq, 4 * CHUNK)

    def piece(j, carry):
        r0 = pl.multiple_of(j * rows, CHUNK)
        nk, _ = _head_rms(p_ref[pl.ds(r0, rows), K_OFF:K_OFF + KV2].astype(F32), mean_k)
        kn = (nk * gk_ref[...]).astype(BF16)
        v = p_ref[pl.ds(r0, rows), V_OFF:V_OFF + KV2].astype(BF16)
        r1 = pl.multiple_of(r0 + CHUNK, CHUNK)
        for h in range(N_KV):
            krep[h][pl.ds(r1, rows), :] = _dot(kn, reps[h]).astype(BF16)
            vrep[h][pl.ds(r1, rows), :] = _dot(v, reps[h]).astype(BF16)
        return carry

    lax.fori_loop(0, seq // rows, piece, 0)


def attn_fwd(pqkv, gq_t, gk_t, sinks, *, seq, name, comms=()):
    T = pqkv.shape[0]
    nb = seq // CHUNK

    def body(p_ref, gq_ref, gk_ref, sink_ref, o_ref, k0, k1, v0, v1, bias_ref, s_ref, pr_ref):
        krep, vrep = [k0, k1], [v0, v1]
        _fill_keys(p_ref, gk_ref, krep, vrep, seq)
        _fill_mask_bias(bias_ref)
        mean_q = _head_mean_matrix(D_B)

        def block(i, carry):
            r0 = pl.multiple_of(i * CHUNK, CHUNK)
            first = jnp.minimum(i, 1)
            nq, _ = _head_rms(p_ref[pl.ds(r0, CHUNK), 0:D_B].astype(F32), mean_q)
            qn = nq * (gq_ref[...] * ATT_SCALE)
            for h in range(N_KV):
                qs = _stack_heads(qn[:, h * KV_W:(h + 1) * KV_W])
                s_ref[...] = _dg(qs, krep[h][pl.ds(r0, 2 * CHUNK), :], NT)

                def rows_of(c, carry2):
                    rows, probs, _ = _softmax_rows(s_ref, bias_ref, first, sink_ref, h, c)
                    pr_ref[rows, :] = probs.astype(BF16)
                    return carry2

                lax.fori_loop(0, STACK // SOFTMAX_ROWS, rows_of, 0, unroll=True)
                out = _unstack_heads(_dot(pr_ref[...], vrep[h][pl.ds(r0, 2 * CHUNK), :]))
                o_ref[pl.ds(r0, CHUNK), h * KV_W:(h + 1) * KV_W] = out.astype(BF16)
            return carry

        lax.fori_loop(0, nb, block, 0)

    return _call(
        body, name=name, grid=(T // seq,),
        in_specs=[pl.BlockSpec((seq, QKV_W), lambda b: (b, 0)), pl.BlockSpec(gq_t.shape, lambda b: (0, 0)),
                  pl.BlockSpec(gk_t.shape, lambda b: (0, 0)), pl.BlockSpec(memory_space=pltpu.SMEM)],
        out_specs=[pl.BlockSpec((seq, D_B), lambda b: (b, 0))], out_shape=[jax.ShapeDtypeStruct((T, D_B), BF16)],
        scratch_shapes=[pltpu.VMEM((seq + CHUNK, KV_W), BF16)] * 4
        + [pltpu.VMEM((2, CHUNK, 2 * CHUNK), F32), pltpu.VMEM((STACK, 2 * CHUNK), F32), pltpu.VMEM((STACK, 2 * CHUNK), BF16)],
        args=[pqkv, gq_t, gk_t, sinks], comms=comms)[0]


def mixer_out_fwd(sgu, att, pg, hin, ga, wa, wb, wo, norm, *, seq, tm, name, comms=()):
    T, D = hin.shape
    B, tps = T // seq, seq // tm

    def body(s_ref, t_ref, pg_ref, h_ref, ga_ref, wa_ref, wb_ref, wo_ref, g_ref, sc_ref, sh_ref,
             ya_ref, yb_ref, mg_ref, m_ref, ho_ref, xn_ref):
        ya = _dot(s_ref[...], wa_ref[...])
        yb = _dot(t_ref[...], wb_ref[...])
        merged = (jax.nn.sigmoid(pg_ref[:, 0:D].astype(F32)) * ya
                  + jax.nn.sigmoid(pg_ref[:, D:2 * D].astype(F32)) * yb)
        mg = merged.astype(BF16)
        m = _dot(mg, wo_ref[...])
        ya_ref[...] = ya.astype(BF16)
        yb_ref[...] = yb.astype(BF16)
        mg_ref[...] = mg
        m_ref[...] = m.astype(BF16)
        y = h_ref[...] + ga_ref[0] * m
        ho_ref[...] = y
        xn_ref[...] = _norm_mod(y, g_ref, sc_ref, sh_ref)

    tile = lambda w: pl.BlockSpec((tm, w), lambda i: (i, 0))
    b16 = jax.ShapeDtypeStruct((T, D), BF16)
    return _call(
        body, name=name, grid=(T // tm,),
        in_specs=[tile(D_A), tile(D_B), tile(2 * D), tile(D), _mod_spec(ga, tps), _resident(wa), _resident(wb),
                  _resident(wo), pl.BlockSpec((1, D), lambda i: (0, 0)), _mod_spec(norm[1], tps), _mod_spec(norm[2], tps)],
        out_specs=[tile(D)] * 6, out_shape=[b16, b16, b16, b16, jax.ShapeDtypeStruct((T, D), F32), b16],
        args=[sgu, att, pg, hin, ga[0], wa, wb, wo, norm[0], norm[1][0], norm[2][0]], comms=comms)


def ffn_bwd_act(dh, f, ga, wd, s, q, *, seq, tm, name, comms=()):
    T, D = dh.shape
    nq, fs, _ = wd.shape
    B, tps = T // seq, seq // tm

    def body(dh_ref, f_ref, ga_ref, wd_ref, s_ref, q_ref, dg_ref, du_ref, df_ref, dga_ref):
        i = pl.program_id(0)
        d = dh_ref[...]
        df = ((0.5 * ga_ref[0]) * d).astype(BF16)
        df_ref[...] = df
        part = 0.5 * jnp.sum(d * f_ref[...].astype(F32), axis=0, keepdims=True)
        for j in range(nq):
            da = _dg(df, wd_ref[j], NT)
            du_ref[j] = (da * s_ref[j].astype(F32)).astype(BF16)
            dg_ref[j] = (da * q_ref[j].astype(F32)).astype(BF16)

        @pl.when(i % tps == 0)
        def _():
            dga_ref[0] = part

        @pl.when(i % tps != 0)
        def _():
            dga_ref[0] += part

    tile = pl.BlockSpec((tm, D), lambda i: (i, 0))
    per_seq = pl.BlockSpec((1, 1, D), lambda i: (i // tps, 0, 0))
    act = pl.BlockSpec((nq, tm, fs), lambda i: (0, i, 0))
    o_shape = jax.ShapeDtypeStruct((nq, T, fs), BF16)
    dg, du, df, dga = _call(
        body, name=name, grid=(T // tm,), in_specs=[tile, tile, _mod_spec(ga, tps), _resident(wd), act, act],
        out_specs=[act, act, tile, per_seq],
        out_shape=[o_shape, o_shape, jax.ShapeDtypeStruct((T, D), BF16), jax.ShapeDtypeStruct((B, 1, D), F32)],
        args=[dh, f, ga[0], wd, s, q], comms=comms)
    return dg, du, df, dga.reshape(B, D)


def mm_tn(pairs, *, tt, name, comms=()):
    n = len(pairs)
    flat = []
    for pair in pairs:
        for a in pair:
            if not any(a is b for b in flat):
                flat.append(a)
    where = lambda a: [k for k, b in enumerate(flat) if a is b][0]
    nq = max([a.shape[0] for a in flat if a.ndim == 3] + [1])
    T = flat[0].shape[-2]
    tt = min(tt, T)
    nt = T // tt
    stacked = [x.ndim == 3 or y.ndim == 3 for x, y in pairs]

    def body(*refs):
        in_refs, o_refs, accs = refs[:len(flat)], refs[len(flat):len(flat) + n], refs[len(flat) + n:]
        t = pl.program_id(1)
        value = lambda a: in_refs[where(a)][0] if a.ndim == 3 else in_refs[where(a)][...]

        def put(j, v):
            if stacked[j]:
                o_refs[j][0] = v.astype(BF16)
            else:
                o_refs[j][...] = v.astype(BF16)

        for j, (x, y) in enumerate(pairs):
            part = _dg(value(x), value(y), TN)
            if nt == 1:
                put(j, part)
                continue

            @pl.when(t == 0)
            def _():
                accs[j][...] = part

            @pl.when(t != 0)
            def _():
                accs[j][...] += part

        if nt > 1:
            @pl.when(t == nt - 1)
            def _():
                for j in range(n):
                    put(j, accs[j][...])

    def spec(a):
        if a.ndim == 3:
            return pl.BlockSpec((1, tt, a.shape[2]), lambda q, t: (q, t, 0))
        return pl.BlockSpec((tt, a.shape[1]), lambda q, t: (t, 0))

    out_specs, out_shape = [], []
    for j, (x, y) in enumerate(pairs):
        K, N = x.shape[-1], y.shape[-1]
        if stacked[j]:
            out_specs.append(pl.BlockSpec((1, K, N), lambda q, t: (q, 0, 0)))
            out_shape.append(jax.ShapeDtypeStruct((nq, K, N), BF16))
        else:
            out_specs.append(pl.BlockSpec((K, N), lambda q, t: (0, 0)))
            out_shape.append(jax.ShapeDtypeStruct((K, N), BF16))
    return _call(
        body, name=name, grid=(nq, nt), in_specs=[spec(a) for a in flat],
        out_specs=out_specs, out_shape=out_shape,
        scratch_shapes=[pltpu.VMEM((x.shape[-1], y.shape[-1]), F32) for x, y in pairs] if nt > 1 else [],
        args=flat, comms=comms)


def dx_norm_bwd(dys, ws, hin, dh_out, g, sc, *, seq, tm, name, comms=()):
    T, D = hin.shape
    B, tps = T // seq, seq // tm
    n = len(dys)

    def body(*refs):
        dy_refs, w_refs = refs[:n], refs[n:2 * n]
        h_ref, dho_ref, g_ref, sc_ref, dhi_ref, dsh_ref, dsc_ref, dgn_ref = refs[2 * n:]
        i = pl.program_id(0)
        dxn = None
        for j in range(n):
            if dys[j].ndim == 3:
                for q in range(dys[j].shape[0]):
                    part = _dot(dy_refs[j][q], w_refs[j][q])
                    dxn = part if dxn is None else dxn + part
            else:
                part = _dot(dy_refs[j][...], w_refs[j][...])
                dxn = part if dxn is None else dxn + part
        x = h_ref[...]
        nx, r = _rms_parts(x)
        gain = g_ref[...]
        one_sc = 1.0 + sc_ref[0]
        dsh = jnp.sum(dxn, axis=0, keepdims=True)
        dsc = jnp.sum(dxn * (nx * gain), axis=0, keepdims=True)
        dgn = jnp.sum(dxn * one_sc * nx, axis=0, keepdims=True)
        dn = dxn * one_sc * gain
        dhi_ref[...] = dho_ref[...] + r * (dn - nx * jnp.mean(dn * nx, axis=-1, keepdims=True))

        @pl.when(i % tps == 0)
        def _():
            dsh_ref[0] = dsh
            dsc_ref[0] = dsc

        @pl.when(i % tps != 0)
        def _():
            dsh_ref[0] += dsh
            dsc_ref[0] += dsc

        @pl.when(i == 0)
        def _():
            dgn_ref[...] = dgn

        @pl.when(i != 0)
        def _():
            dgn_ref[...] += dgn

    def dy_spec(a):
        if a.ndim == 3:
            return pl.BlockSpec((a.shape[0], tm, a.shape[2]), lambda i: (0, i, 0))
        return pl.BlockSpec((tm, a.shape[1]), lambda i: (i, 0))

    tile = pl.BlockSpec((tm, D), lambda i: (i, 0))
    per_seq = pl.BlockSpec((1, 1, D), lambda i: (i // tps, 0, 0))
    row = pl.BlockSpec((1, D), lambda i: (0, 0))
    dhi, dsh, dsc, dgn = _call(
        body, name=name, grid=(T // tm,),
        in_specs=[dy_spec(a) for a in dys] + [_resident(w) for w in ws] + [tile, tile, row, _mod_spec(sc, tps)],
        out_specs=[tile, per_seq, per_seq, row],
        out_shape=[jax.ShapeDtypeStruct((T, D), F32), jax.ShapeDtypeStruct((B, 1, D), F32),
                   jax.ShapeDtypeStruct((B, 1, D), F32), jax.ShapeDtypeStruct((1, D), F32)],
        args=[*dys, *ws, hin, dh_out, g, sc[0]], comms=comms)
    return dhi, dsh.reshape(B, D), dsc.reshape(B, D), dgn


def mixer_out_bwd(dh, m, ga, ya, yb, pg, wa, wb, wo, *, seq, tm, name, comms=()):
    T, D = dh.shape
    B, tps = T // seq, seq // tm

    def body(dh_ref, m_ref, ga_ref, ya_ref, yb_ref, pg_ref, wa_ref, wb_ref, wo_ref,
             dg_ref, dya_ref, dyb_ref, ds_ref, dt_ref, dm_ref, dga_ref):
        i = pl.program_id(0)
        d = dh_ref[...]
        dm = (ga_ref[0] * d).astype(BF16)
        dm_ref[...] = dm
        part = jnp.sum(d * m_ref[...].astype(F32), axis=0, keepdims=True)

        @pl.when(i % tps == 0)
        def _():
            dga_ref[0] = part

        @pl.when(i % tps != 0)
        def _():
            dga_ref[0] += part

        dmg = _dg(dm, wo_ref[...], NT)
        sa = jax.nn.sigmoid(pg_ref[:, 0:D].astype(F32))
        sb = jax.nn.sigmoid(pg_ref[:, D:2 * D].astype(F32))
        dg_ref[:, 0:D] = (dmg * ya_ref[...].astype(F32) * (sa * (1.0 - sa))).astype(BF16)
        dg_ref[:, D:2 * D] = (dmg * yb_ref[...].astype(F32) * (sb * (1.0 - sb))).astype(BF16)
        dya = (dmg * sa).astype(BF16)
        dyb = (dmg * sb).astype(BF16)
        dya_ref[...] = dya
        dyb_ref[...] = dyb
        ds_ref[...] = _dg(dya, wa_ref[...], NT).astype(BF16)
        dt_ref[...] = _dg(dyb, wb_ref[...], NT).astype(BF16)

    tile = lambda w: pl.BlockSpec((tm, w), lambda i: (i, 0))
    per_seq = pl.BlockSpec((1, 1, D), lambda i: (i // tps, 0, 0))
    dg, dya, dyb, ds, dt, dm, dga = _call(
        body, name=name, grid=(T // tm,),
        in_specs=[tile(D), tile(D), _mod_spec(ga, tps), tile(D), tile(D), tile(2 * D), _resident(wa), _resident(wb),
                  _resident(wo)],
        out_specs=[tile(2 * D), tile(D), tile(D), tile(D_A), tile(D_B), tile(D), per_seq],
        out_shape=[jax.ShapeDtypeStruct((T, 2 * D), BF16), jax.ShapeDtypeStruct((T, D), BF16),
                   jax.ShapeDtypeStruct((T, D), BF16), jax.ShapeDtypeStruct((T, D_A), BF16),
                   jax.ShapeDtypeStruct((T, D_B), BF16), jax.ShapeDtypeStruct((T, D), BF16),
                   jax.ShapeDtypeStruct((B, 1, D), F32)],
        args=[dh, m, ga[0], ya, yb, pg, wa, wb, wo], comms=comms)
    return dg, dya, dyb, ds, dt, dm, dga.reshape(B, D)


def sgu_bwd(puv, dsgu, gln, bln, ws, bs_t, *, cpb, name, comms=()):
    T = puv.shape[0]
    gd = D_A // N_GROUPS
    tm = cpb * CHUNK

    def body(p_ref, ds_ref, gln_ref, bln_ref, ws_ref, bs_ref, d_ref, dws_ref, dz_ref, dgl_ref, dbl_ref):
        i = pl.program_id(0)
        causal = _causal()
        wf = [jnp.where(causal, ws_ref[g], 0.0) for g in range(N_GROUPS)]
        wm = [w.astype(BF16) for w in wf]
        wt = [w.T.astype(BF16) for w in wf]
        dws = [jnp.zeros((CHUNK, CHUNK), F32) for _ in range(N_GROUPS)]
        dzs = jnp.zeros((CHUNK, D_A), F32)
        dgl = jnp.zeros((1, D_A), F32)
        dbl = jnp.zeros((1, D_A), F32)
        for j in range(cpb):
            rows = slice(j * CHUNK, (j + 1) * CHUNK)
            au = p_ref[rows, 0:D_A].astype(F32)
            av = p_ref[rows, D_A:2 * D_A].astype(F32)
            u, tu = _gelu_parts(au)
            vv, tv = _gelu_parts(av)
            vhat, rstd = _layer_norm_parts(vv)
            vn = (vhat * gln_ref[...] + bln_ref[...]).astype(BF16)
            dsg = ds_ref[rows, :].astype(F32)
            dz = dsg * u
            dzb = dz.astype(BF16)
            zs, dvns = [], []
            for g in range(N_GROUPS):
                cols = slice(g * gd, (g + 1) * gd)
                zs.append(_dot(wm[g], vn[:, cols]) + bs_ref[:, g:g + 1])
                dws[g] = dws[g] + _dg(dzb[:, cols], vn[:, cols], NT)
                dvns.append(_dot(wt[g], dzb[:, cols]))
            z = jnp.concatenate(zs, axis=1)
            dvn = jnp.concatenate(dvns, axis=1)
            d_ref[rows, 0:D_A] = (dsg * z * _dgelu(au, tu)).astype(BF16)
            dvh = dvn * gln_ref[...]
            dvv = rstd * (dvh - jnp.mean(dvh, axis=-1, keepdims=True)
                          - vhat * jnp.mean(dvh * vhat, axis=-1, keepdims=True))
            d_ref[rows, D_A:2 * D_A] = (dvv * _dgelu(av, tv)).astype(BF16)
            dzs = dzs + dz
            dgl = dgl + jnp.sum(dvn * vhat, axis=0, keepdims=True)
            dbl = dbl + jnp.sum(dvn, axis=0, keepdims=True)

        @pl.when(i == 0)
        def _():
            for g in range(N_GROUPS):
                dws_ref[g] = jnp.where(causal, dws[g], 0.0)
            dz_ref[...] = dzs
            dgl_ref[...] = dgl
            dbl_ref[...] = dbl

        @pl.when(i != 0)
        def _():
            for g in range(N_GROUPS):
                dws_ref[g] += jnp.where(causal, dws[g], 0.0)
            dz_ref[...] += dzs
            dgl_ref[...] += dgl
            dbl_ref[...] += dbl

    full = lambda a: pl.BlockSpec(a.shape, lambda i: (0,) * a.ndim)
    acc = lambda s: pl.BlockSpec(s, lambda i: (0,) * len(s))
    return _call(
        body, name=name, grid=(T // tm,),
        in_specs=[pl.BlockSpec((tm, 2 * D_A), lambda i: (i, 0)), pl.BlockSpec((tm, D_A), lambda i: (i, 0)),
                  full(gln), full(bln), full(ws), full(bs_t)],
        out_specs=[pl.BlockSpec((tm, 2 * D_A), lambda i: (i, 0)), acc((N_GROUPS, CHUNK, CHUNK)), acc((CHUNK, D_A)),
                   acc((1, D_A)), acc((1, D_A))],
        out_shape=[jax.ShapeDtypeStruct((T, 2 * D_A), BF16), jax.ShapeDtypeStruct((N_GROUPS, CHUNK, CHUNK), F32),
                   jax.ShapeDtypeStruct((CHUNK, D_A), F32), jax.ShapeDtypeStruct((1, D_A), F32),
                   jax.ShapeDtypeStruct((1, D_A), F32)],
        args=[puv, dsgu, gln, bln, ws, bs_t], comms=comms)


def attn_bwd(pqkv, datt, gq_t, gk_t, sinks, *, seq, name, comms=()):
    T = pqkv.shape[0]
    nb = seq // CHUNK

    def body(p_ref, do_ref, gq_ref, gk_ref, sink_ref, d_ref, dgq_ref, dgk_ref, dsk_ref,
             k0, k1, v0, v1, dk0, dk1, dv0, dv1, dgq_s, dsk_s, bias_ref, s_ref, dp_ref, pr_ref, ds_ref):
        b = pl.program_id(0)
        krep, vrep, dkacc, dvacc = [k0, k1], [v0, v1], [dk0, dk1], [dv0, dv1]
        _fill_keys(p_ref, gk_ref, krep, vrep, seq)
        _fill_mask_bias(bias_ref)
        for h in range(N_KV):
            dkacc[h][...] = jnp.zeros_like(dkacc[h])
            dvacc[h][...] = jnp.zeros_like(dvacc[h])
        dgq_s[...] = jnp.zeros_like(dgq_s)
        dsk_s[...] = jnp.zeros_like(dsk_s)
        mean_q = _head_mean_matrix(D_B)
        lane = _iota((1, 128), 1)

        def block(i, carry):
            r0 = pl.multiple_of(i * CHUNK, CHUNK)
            first = jnp.minimum(i, 1)
            nq, rq = _head_rms(p_ref[pl.ds(r0, CHUNK), 0:D_B].astype(F32), mean_q)
            qn = nq * (gq_ref[...] * ATT_SCALE)
            dqn_parts = []
            for h in range(N_KV):
                cols = slice(h * KV_W, (h + 1) * KV_W)
                qs = _stack_heads(qn[:, cols])
                kk = krep[h][pl.ds(r0, 2 * CHUNK), :]
                dos = _stack_heads(do_ref[pl.ds(r0, CHUNK), cols].astype(F32))
                s_ref[...] = _dg(qs, kk, NT)
                dp_ref[...] = _dg(dos, vrep[h][pl.ds(r0, 2 * CHUNK), :], NT)

                def rows_of(c, carry2):
                    rows, probs, psink = _softmax_rows(s_ref, bias_ref, first, sink_ref, h, c)
                    dp = dp_ref[rows, :]
                    dr = jnp.sum(probs * dp, axis=-1, keepdims=True)
                    pr_ref[rows, :] = probs.astype(BF16)
                    ds_ref[rows, :] = (probs * (dp - dr)).astype(BF16)
                    head = h * Q_PER_KV + c // (CHUNK // SOFTMAX_ROWS)
                    dsk_s[...] += jnp.where(lane == head, -jnp.sum(psink * dr), 0.0)
                    return carry2

                lax.fori_loop(0, STACK // SOFTMAX_ROWS, rows_of, 0, unroll=True)
                dqn_parts.append(_unstack_heads(_dot(ds_ref[...], kk)))
                dkacc[h][pl.ds(r0, 2 * CHUNK), :] += _dg(ds_ref[...], qs, TN)
                dvacc[h][pl.ds(r0, 2 * CHUNK), :] += _dg(pr_ref[...], dos, TN)
            dqn = jnp.concatenate(dqn_parts, axis=1) * ATT_SCALE
            dn = dqn * gq_ref[...]
            d_ref[pl.ds(r0, CHUNK), 0:D_B] = (rq * (dn - nq * _dot_split(dn * nq, mean_q))).astype(BF16)
            dgq_s[...] += jnp.sum(dqn * nq, axis=0, keepdims=True)
            return carry

        lax.fori_loop(0, nb, block, 0)

        mean_k = _head_mean_matrix(KV2)
        folds = [_fold_matrix(h) for h in range(N_KV)]
        rows = min(seq, 4 * CHUNK)

        def piece(j, dgk):
            r0 = pl.multiple_of(j * rows, CHUNK)
            r1 = pl.multiple_of(r0 + CHUNK, CHUNK)
            dkn = _dot_split(dkacc[0][pl.ds(r1, rows), :], folds[0]) + _dot_split(dkacc[1][pl.ds(r1, rows), :], folds[1])
            dv = _dot_split(dvacc[0][pl.ds(r1, rows), :], folds[0]) + _dot_split(dvacc[1][pl.ds(r1, rows), :], folds[1])
            nk, rk = _head_rms(p_ref[pl.ds(r0, rows), K_OFF:K_OFF + KV2].astype(F32), mean_k)
            dn = dkn * gk_ref[...]
            d_ref[pl.ds(r0, rows), K_OFF:K_OFF + KV2] = (rk * (dn - nk * _dot_split(dn * nk, mean_k))).astype(BF16)
            d_ref[pl.ds(r0, rows), V_OFF:V_OFF + KV2] = dv.astype(BF16)
            return dgk + jnp.sum(dkn * nk, axis=0, keepdims=True)

        dgk = lax.fori_loop(0, seq // rows, piece, jnp.zeros((1, KV2), F32))

        @pl.when(b == 0)
        def _():
            dgq_ref[...] = dgq_s[...]
            dgk_ref[...] = dgk
            dsk_ref[...] = jnp.broadcast_to(dsk_s[...], dsk_ref.shape)

        @pl.when(b != 0)
        def _():
            dgq_ref[...] += dgq_s[...]
            dgk_ref[...] += dgk
            dsk_ref[...] += jnp.broadcast_to(dsk_s[...], dsk_ref.shape)

    acc = lambda s: pl.BlockSpec(s, lambda b: (0, 0))
    return _call(
        body, name=name, grid=(T // seq,),
        in_specs=[pl.BlockSpec((seq, QKV_W), lambda b: (b, 0)), pl.BlockSpec((seq, D_B), lambda b: (b, 0)),
                  pl.BlockSpec(gq_t.shape, lambda b: (0, 0)), pl.BlockSpec(gk_t.shape, lambda b: (0, 0)),
                  pl.BlockSpec(memory_space=pltpu.SMEM)],
        out_specs=[pl.BlockSpec((seq, QKV_W), lambda b: (b, 0)), acc((1, D_B)), acc((1, KV2)), acc((8, 128))],
        out_shape=[jax.ShapeDtypeStruct((T, QKV_W), BF16), jax.ShapeDtypeStruct((1, D_B), F32),
                   jax.ShapeDtypeStruct((1, KV2), F32), jax.ShapeDtypeStruct((8, 128), F32)],
        scratch_shapes=[pltpu.VMEM((seq + CHUNK, KV_W), BF16)] * 4 + [pltpu.VMEM((seq + CHUNK, KV_W), F32)] * 4
        + [pltpu.VMEM((1, D_B), F32), pltpu.VMEM((1, 128), F32), pltpu.VMEM((2, CHUNK, 2 * CHUNK), F32)]
        + [pltpu.VMEM((STACK, 2 * CHUNK), F32)] * 2 + [pltpu.VMEM((STACK, 2 * CHUNK), BF16)] * 2,
        args=[pqkv, datt, gq_t, gk_t, sinks], comms=comms)


def ada_fwd(c_all, w, b, *, name):
    nb, D = c_all.shape
    N = w.shape[1]
    tn = N // 3

    def body(c_ref, w_ref, b_ref, o_ref):
        c = c_ref[...]
        cond = (c * jax.nn.sigmoid(c)).astype(BF16)
        o_ref[...] = _dot(cond, w_ref[...].astype(BF16)) + b_ref[...]

    return _call(
        body, name=name, grid=(3,),
        in_specs=[pl.BlockSpec((nb, D), lambda j: (0, 0)), pl.BlockSpec((D, tn), lambda j: (0, j)),
                  pl.BlockSpec((1, tn), lambda j: (0, j))],
        out_specs=[pl.BlockSpec((nb, tn), lambda j: (0, j))], out_shape=[jax.ShapeDtypeStruct((nb, N), F32)],
        args=[c_all, w, b])[0]


def ada_bwd(c_all, dmods_all, dmods_mine, gain_rows, *, name, comms=()):
    nb, D = c_all.shape
    NA = dmods_all.shape[1]
    N = dmods_mine.shape[1]
    tn = N // 3

    def body(c_ref, da_ref, dm_ref, gr_ref, db_ref, dw_ref, dgn_ref):
        c = c_ref[...]
        cond = (c * jax.nn.sigmoid(c)).astype(BF16)
        dw_ref[...] = _dg(cond, dm_ref[...].astype(BF16), TN)
        db_ref[...] = jnp.sum(da_ref[...], axis=0, keepdims=True)
        total = gr_ref[0:1, :]
        for d in range(1, N_DEV):
            total = total + gr_ref[d:d + 1, :]
        dgn_ref[...] = total

    return _call(
        body, name=name, grid=(3,),
        in_specs=[pl.BlockSpec((nb, D), lambda j: (0, 0)), pl.BlockSpec((nb, NA), lambda j: (0, 0)),
                  pl.BlockSpec((nb, tn), lambda j: (0, j)), pl.BlockSpec((N_DEV, D), lambda j: (0, 0))],
        out_specs=[pl.BlockSpec((1, NA), lambda j: (0, 0)), pl.BlockSpec((D, tn), lambda j: (0, j)),
                   pl.BlockSpec((1, D), lambda j: (0, 0))],
        out_shape=[jax.ShapeDtypeStruct((1, NA), F32), jax.ShapeDtypeStruct((D, N), F32),
                   jax.ShapeDtypeStruct((1, D), F32)],
        args=[c_all, dmods_all, dmods_mine, gain_rows], comms=comms)


def adamw(items, *, steps, name, comms=()):
    n = len(items)
    c1 = 1.0 / (1.0 - ADAM_B1 ** ADAM_STEP)
    c2 = 1.0 / (1.0 - ADAM_B2 ** ADAM_STEP)

    def body(*refs):
        for j in range(n):
            w_ref, g_ref, m_ref, v_ref = refs[4 * j:4 * j + 4]
            d_ref, mo_ref, vo_ref = refs[4 * n + 3 * j:4 * n + 3 * j + 3]
            gg = g_ref[...]
            mn = ADAM_B1 * m_ref[...] + (1.0 - ADAM_B1) * gg
            vn = ADAM_B2 * v_ref[...] + (1.0 - ADAM_B2) * (gg * gg)
            mo_ref[...] = mn
            vo_ref[...] = vn
            d_ref[...] = -ADAM_LR * ((mn * c1) / (jnp.sqrt(vn * c2) + ADAM_EPS) + ADAM_WD * w_ref[...])

    in_specs, out_specs, out_shape, args = [], [], [], []
    for item in items:
        R, C = item[0].shape
        blk = pl.BlockSpec((R // steps, C), lambda i: (i, 0))
        in_specs += [blk] * 4
        out_specs += [blk] * 3
        out_shape += [jax.ShapeDtypeStruct((R, C), F32)] * 3
        args += list(item)
    res = _call(body, name=name, grid=(steps,), in_specs=in_specs, out_specs=out_specs, out_shape=out_shape, args=args,
                comms=comms)
    return [tuple(res[3 * j:3 * j + 3]) for j in range(n)]


def _place():
    return lax.axis_index("x"), lax.axis_index("y"), lax.axis_index("c")


def _flip(v, bit):
    return 1 - v if bit else v


def _other_chips(mx, my):
    return [(_flip(mx, k & 2), _flip(my, k & 1)) for k in range(1, N_QUAD)]


def _remote(src, dst, send_sem, recv_sem, peer):
    return pltpu.make_async_remote_copy(src_ref=src, dst_ref=dst, send_sem=send_sem, recv_sem=recv_sem,
                                        device_id=peer, device_id_type=MESH)


def ag8_comm(x):
    def copies(srcs, lands, ss, rs, only_sends=False):
        mx, my, mc = _place()
        me = 4 * mx + 2 * my + mc
        local = pltpu.make_async_copy(srcs[0], lands[0].at[me], ss.at[N_DEV - 1])
        sends, recvs = [], []
        for k in range(1, N_DEV):
            peer = (_flip(mx, k & 4), _flip(my, k & 2), _flip(mc, k & 1))
            sends.append(_remote(srcs[0], lands[0].at[me], ss.at[k - 1], rs.at[k - 1], peer))
            if not only_sends:
                recvs.append(_remote(srcs[0], lands[0].at[4 * peer[0] + 2 * peer[1] + peer[2]], ss.at[k - 1], rs.at[k - 1], peer))
        return local, sends, recvs

    def start(srcs, bufs, lands, ss, rs):
        local, sends, _ = copies(srcs, lands, ss, rs, only_sends=True)
        local.start()
        for cp in sends:
            cp.start()

    def finish(srcs, bufs, lands, ss, rs):
        local, sends, recvs = copies(srcs, lands, ss, rs)
        for cp in recvs:
            cp.wait_recv()
        for cp in sends:
            cp.wait_send()
        local.wait()

    return Comm(srcs=[x], land_shapes=[jax.ShapeDtypeStruct((N_DEV,) + x.shape, x.dtype)], n_sems=N_DEV,
                start=start, finish=finish)


def sum8(stacked, *, name):
    _, r, n = stacked.shape

    def body(s_ref, o_ref):
        total = s_ref[0]
        for d in range(1, N_DEV):
            total = total + s_ref[d]
        o_ref[...] = total

    return _call(body, name=name, grid=(), in_specs=[pl.BlockSpec(memory_space=pltpu.VMEM)],
                 out_specs=[pl.BlockSpec(memory_space=pltpu.VMEM)], out_shape=[jax.ShapeDtypeStruct((r, n), F32)],
                 args=[stacked])[0]


def cast_into_stacks(ws, quad, *, name, comms=()):
    steps = 4

    def body(q_ref, *refs):
        for w_ref, o_ref in zip(refs[:len(ws)], refs[len(ws):]):
            o_ref[0] = w_ref[...].astype(BF16)

    return _call(
        body, name=name, grid=(steps,), prefetch=[quad],
        in_specs=[pl.BlockSpec((w.shape[0] // steps, w.shape[1]), lambda i, q: (i, 0)) for w in ws],
        out_specs=[pl.BlockSpec((1, w.shape[0] // steps, w.shape[1]), lambda i, q: (q[0], i, 0)) for w in ws],
        out_shape=[jax.ShapeDtypeStruct((N_QUAD,) + w.shape, BF16) for w in ws], args=list(ws), comms=comms)


def gather_comm(stacks):
    n = len(stacks)

    def copies(bufs, ss, rs, only_sends=False):
        mx, my, mc = _place()
        q = 2 * mx + my
        sibling = (mx, my, 1 - mc)
        ici_send, ici_recv, fwd_send, fwd_recv = [], [], [], []
        for p in range(n):
            hr = stacks[p].shape[1] // 2
            mine, other = pl.ds(mc * hr, hr), pl.ds((1 - mc) * hr, hr)
            for k, chip in enumerate(_other_chips(mx, my)):
                qk = 2 * chip[0] + chip[1]
                peer = (chip[0], chip[1], mc)
                own, landed, theirs = bufs[p].at[q, mine, :], bufs[p].at[qk, mine, :], bufs[p].at[qk, other, :]
                ici_send.append(_remote(own, own, ss.at[6 * p + k], rs.at[6 * p + k], peer))
                if only_sends:
                    continue
                ici_recv.append(_remote(own, landed, ss.at[6 * p + k], rs.at[6 * p + k], peer))
                fwd_send.append(_remote(landed, landed, ss.at[6 * p + 3 + k], rs.at[6 * p + 3 + k], sibling))
                fwd_recv.append(_remote(theirs, theirs, ss.at[6 * p + 3 + k], rs.at[6 * p + 3 + k], sibling))
        return ici_send, ici_recv, fwd_send, fwd_recv

    def start(srcs, bufs, lands, ss, rs):
        for cp in copies(bufs, ss, rs, only_sends=True)[0]:
            cp.start()

    def finish(srcs, bufs, lands, ss, rs):
        ici_send, ici_recv, fwd_send, fwd_recv = copies(bufs, ss, rs)
        for arrived, onward in zip(ici_recv, fwd_send):
            arrived.wait_recv()
            onward.start()
        for cp in fwd_recv:
            cp.wait_recv()
        for cp in ici_send + fwd_send:
            cp.wait_send()

    return Comm(bufs=stacks, n_sems=6 * n, start=start, finish=finish)


def rs_pair_comm(gs):
    n = len(gs)

    def copies(srcs, lands, ss, rs):
        mx, my, mc = _place()
        out = []
        for p in range(n):
            hr = gs[p].shape[1] // 2
            out.append(_remote(srcs[p].at[:, pl.ds((1 - mc) * hr, hr), :], lands[p], ss.at[p], rs.at[p], (mx, my, 1 - mc)))
        return out

    def start(srcs, bufs, lands, ss, rs):
        for cp in copies(srcs, lands, ss, rs):
            cp.start()

    def finish(srcs, bufs, lands, ss, rs):
        for cp in copies(srcs, lands, ss, rs):
            cp.wait()

    return Comm(srcs=gs, land_shapes=[jax.ShapeDtypeStruct((g.shape[0], g.shape[1] // 2, g.shape[2]), g.dtype) for g in gs],
                n_sems=n, start=start, finish=finish)


def rs_chip_comm(ss_):
    n = len(ss_)

    def copies(srcs, lands, ss, rs):
        mx, my, mc = _place()
        out = []
        for p in range(n):
            for k, chip in enumerate(_other_chips(mx, my)):
                out.append(_remote(srcs[p].at[2 * chip[0] + chip[1]], lands[p].at[k], ss.at[3 * p + k], rs.at[3 * p + k],
                                   (chip[0], chip[1], mc)))
        return out

    def start(srcs, bufs, lands, ss, rs):
        for cp in copies(srcs, lands, ss, rs):
            cp.start()

    def finish(srcs, bufs, lands, ss, rs):
        for cp in copies(srcs, lands, ss, rs):
            cp.wait()

    return Comm(srcs=ss_, land_shapes=[jax.ShapeDtypeStruct((N_QUAD - 1,) + s.shape[1:], s.dtype) for s in ss_],
                n_sems=3 * n, start=start, finish=finish)


def rs_share_comm(fulls):
    n = len(fulls)

    def copies(bufs, ss, rs, only_sends=False):
        mx, my, mc = _place()
        send, recv = [], []
        for p in range(n):
            hr = fulls[p].shape[0] // 2
            mine, other = bufs[p].at[pl.ds(mc * hr, hr), :], bufs[p].at[pl.ds((1 - mc) * hr, hr), :]
            send.append(_remote(mine, mine, ss.at[p], rs.at[p], (mx, my, 1 - mc)))
            if not only_sends:
                recv.append(_remote(other, other, ss.at[p], rs.at[p], (mx, my, 1 - mc)))
        return send, recv

    def start(srcs, bufs, lands, ss, rs):
        for cp in copies(bufs, ss, rs, only_sends=True)[0]:
            cp.start()

    def finish(srcs, bufs, lands, ss, rs):
        send, recv = copies(bufs, ss, rs)
        for cp in recv:
            cp.wait_recv()
        for cp in send:
            cp.wait_send()

    return Comm(bufs=fulls, n_sems=n, start=start, finish=finish)


def pair_sum(g, recv, mc, *, name):
    nq, R, C = g.shape
    hr = R // 2
    rb = _row_block(hr, 512)
    nb = hr // rb

    def body(s_ref, g_ref, r_ref, o_ref):
        o_ref[...] = (g_ref[...].astype(F32) + r_ref[...].astype(F32)).astype(BF16)

    return pl.pallas_call(
        body, name=name, out_shape=jax.ShapeDtypeStruct((nq, hr, C), BF16),
        grid_spec=pltpu.PrefetchScalarGridSpec(
            num_scalar_prefetch=1, grid=(nq, nb),
            in_specs=[pl.BlockSpec((1, rb, C), lambda q, i, s: (q, s[0] * nb + i, 0)),
                      pl.BlockSpec((1, rb, C), lambda q, i, s: (q, i, 0))],
            out_specs=pl.BlockSpec((1, rb, C), lambda q, i, s: (q, i, 0))),
        compiler_params=pltpu.CompilerParams(dimension_semantics=("arbitrary", "arbitrary"),
                                             vmem_limit_bytes=VMEM_LIMIT_BYTES),
    )(mc, g, recv)


def chip_sum(s, recv, quad_mc, *, name):
    nq, hr, C = s.shape
    rb = _row_block(hr, 256)
    nb = hr // rb

    def body(q_ref, s_ref, r_ref, o_ref):
        total = s_ref[0].astype(F32)
        for k in range(N_QUAD - 1):
            total = total + r_ref[k].astype(F32)
        o_ref[...] = total

    return pl.pallas_call(
        body, name=name, out_shape=jax.ShapeDtypeStruct((2 * hr, C), F32),
        grid_spec=pltpu.PrefetchScalarGridSpec(
            num_scalar_prefetch=1, grid=(nb,),
            in_specs=[pl.BlockSpec((1, rb, C), lambda i, q: (q[0], i, 0)),
                      pl.BlockSpec((N_QUAD - 1, rb, C), lambda i, q: (0, i, 0))],
            out_specs=pl.BlockSpec((rb, C), lambda i, q: (q[1] * nb + i, 0))),
        compiler_params=pltpu.CompilerParams(dimension_semantics=("arbitrary",), vmem_limit_bytes=VMEM_LIMIT_BYTES),
    )(quad_mc, s, recv)


def _stack_cols(w_full):
    K, N = w_full.shape
    return w_full.reshape(K, N_QUAD, N // N_QUAD).transpose(1, 0, 2)


def _unstack_cols(w4):
    nq, K, n = w4.shape
    return w4.transpose(1, 0, 2).reshape(K, nq * n)


def kernel(x, c, w_ada, b_ada, g_norm1, ffn1_w_gate, ffn1_w_up, ffn1_w_down, g_norm2, w_in, g_sgu_ln, b_sgu_ln, w_spatial, b_spatial, g_q, g_k, attn_sinks, w_branch_a, w_branch_b, w_out, g_norm3, ffn2_w_gate, ffn2_w_up, ffn2_w_down, loss_target, m_w_ada, m_b_ada, m_g_norm1, m_ffn1_w_gate, m_ffn1_w_up, m_ffn1_w_down, m_g_norm2, m_w_in, m_g_sgu_ln, m_b_sgu_ln, m_w_spatial, m_b_spatial, m_g_q, m_g_k, m_attn_sinks, m_w_branch_a, m_w_branch_b, m_w_out, m_g_norm3, m_ffn2_w_gate, m_ffn2_w_up, m_ffn2_w_down, v_w_ada, v_b_ada, v_g_norm1, v_ffn1_w_gate, v_ffn1_w_up, v_ffn1_w_down, v_g_norm2, v_w_in, v_g_sgu_ln, v_b_sgu_ln, v_w_spatial, v_b_spatial, v_g_q, v_g_k, v_attn_sinks, v_w_branch_a, v_w_branch_b, v_w_out, v_g_norm3, v_ffn2_w_gate, v_ffn2_w_up, v_ffn2_w_down):
    B, S, D = x.shape
    T = B * S
    n_mod = b_ada.shape[1] // D
    mx, my, mc = _place()
    quad = 2 * mx + my
    mc_arr = jnp.reshape(mc, (1,)).astype(jnp.int32)
    quad_arr = jnp.reshape(quad, (1,)).astype(jnp.int32)
    quad_mc = jnp.stack([quad, mc]).astype(jnp.int32)
    tm = min(512, S)
    tm_big = min(1024, S)
    tt_half = min(2048, T)
    cpb = min(4, S // CHUNK)

    xt = x.reshape(T, D)
    tgt = loss_target.reshape(T, D)

    tr = lambda a: jnp.swapaxes(a, 1, 2)
    stack_wg1, stack_wu1 = cast_into_stacks([tr(ffn1_w_gate)[0], tr(ffn1_w_up)[0]], quad_arr, name="stack_gu1")
    gather_wg1, gather_wu1 = gather_comm([stack_wg1]), gather_comm([stack_wu1])
    later = [ffn1_w_down, tr(w_in), w_branch_a, w_branch_b, w_out, tr(ffn2_w_gate), tr(ffn2_w_up), ffn2_w_down]
    gather_cond = ag8_comm(c)
    stacks = cast_into_stacks([w[0] for w in later[0:4]], quad_arr, name="stack_a", comms=[gather_wg1, gather_cond])
    (wg1,) = gather_wg1.bufs_out
    c_all = gather_cond.lands[0].reshape(N_DEV * B, D)
    n_ada = w_ada.shape[2]
    b_mine = lax.dynamic_slice(b_ada, (0, quad * n_ada), (1, n_ada))
    mods_part = ada_fwd(c_all, w_ada[0], b_mine, name="ada_fwd")
    gather_mods = ag8_comm(mods_part)
    stacks += cast_into_stacks([w[0] for w in later[4:8]], quad_arr, name="stack_b", comms=[gather_wu1, gather_mods])
    (wu1,), (mods_parts,) = gather_wu1.bufs_out, gather_mods.lands
    gather_wd1, gather_win = gather_comm([stacks[0]]), gather_comm([stacks[1]])
    gather_abo = gather_comm(stacks[2:5])
    gather_wg3, gather_wu3, gather_wd3 = gather_comm([stacks[5]]), gather_comm([stacks[6]]), gather_comm([stacks[7]])
    mods = jnp.concatenate([mods_parts[2 * j] for j in range(N_QUAD)], axis=1)
    me = 4 * mx + 2 * my + mc
    mods = lax.dynamic_slice(mods, (me * B, 0), (B, n_mod * D))
    mods = mods.reshape(B, n_mod, 1, D)
    sh1, sc1, ga1, sh2, sc2, ga2, sh3, sc3, ga3 = [(mods, j) for j in range(n_mod)]
    bs_t = b_spatial[0].T
    ws = w_spatial[0]

    xn1 = norm_mod_fwd(xt, g_norm1, sc1, sh1, seq=S, tm=tm, name="norm1")
    g1, u1, a1 = ffn_up(xn1, wg1, wu1, tm=tm_big, name="ffn1_up", comms=[gather_wd1, gather_abo])
    (wd1,), (wa4, wb4, wo4) = gather_wd1.bufs_out, gather_abo.bufs_out
    wa, wb = _unstack_cols(wa4), _unstack_cols(wb4)
    wo = wo4.reshape(D, D)
    h1, f1, xn2 = ffn_down(a1, wd1, xt, ga1, norm=(g_norm2, sc2, sh2), seq=S, tm=tm, name="ffn1_down",
                           comms=[gather_win])
    (win4,) = gather_win.bufs_out
    win = win4.reshape(-1, D)
    w_uv, w_qkv, w_gt = win[0:2 * D_A], win[2 * D_A:2 * D_A + QKV_W], win[2 * D_A + QKV_W:]
    puv, pqkv, pgt = proj_fwd(xn2, [w_uv, w_qkv, w_gt], tm=tm, name="proj", comms=[gather_wg3])
    (wg3,) = gather_wg3.bufs_out
    sgu = sgu_fwd(puv, g_sgu_ln, b_sgu_ln, ws, bs_t, cpb=cpb, name="sgu_fwd")
    gq_t, gk_t = jnp.tile(g_q, (1, N_Q)), jnp.tile(g_k, (1, N_KV))
    att = attn_fwd(pqkv, gq_t, gk_t, attn_sinks, seq=S, name="attn_fwd", comms=[gather_wu3])
    (wu3,) = gather_wu3.bufs_out
    ya, yb, merged, mm, h2, xn3 = mixer_out_fwd(sgu, att, pgt, h1, ga2, wa, wb, wo, (g_norm3, sc3, sh3), seq=S,
                                                tm=tm, name="mixer_out", comms=[gather_wd3])
    (wd3,) = gather_wd3.bufs_out
    g3, u3, a3 = ffn_up(xn3, wg3, wu3, tm=tm_big, name="ffn2_up")
    dy, f3, lparts = ffn_down(a3, wd3, h2, ga3, target=tgt, seq=S, tm=tm, name="ffn2_down_loss")
    loss_part = jnp.sum(lparts[:, 0, 0])

    def sums_of(pair, tag):
        return [pair_sum(g, r, mc_arr, name=f"pair_sum_{tag}_{p}") for p, (g, r) in enumerate(zip(pair.srcs, pair.lands))]

    def halves_of(chip, tag):
        return [chip_sum(s, r, quad_mc, name=f"chip_sum_{tag}_{p}") for p, (s, r) in enumerate(zip(chip.srcs, chip.lands))]

    dg3, du3, df3, dga3 = ffn_bwd_act(dy, f3, ga3, wd3, g3, u3, seq=S, tm=tm, name="ffn2_act_bwd")
    (dwd3,) = mm_tn([(a3, df3)], tt=T, name="ffn2_dwd")
    pair_wd3 = rs_pair_comm([dwd3])
    dwg3, dwu3 = mm_tn([(dg3, xn3), (du3, xn3)], tt=tt_half, name="ffn2_dwgu", comms=[pair_wd3])
    chip_wd3 = rs_chip_comm(sums_of(pair_wd3, "wd3"))
    pair_gu3 = rs_pair_comm([dwg3, dwu3])
    dh2, dsh3, dsc3, dgn3 = dx_norm_bwd([dg3, du3], [wg3, wu3], h2, dy, g_norm3, sc3, seq=S, tm=tm, name="ffn2_dx",
                                        comms=[chip_wd3, pair_gu3])
    sum_wg3, sum_wu3 = sums_of(pair_gu3, "gu3")
    chip_wg3, chip_wu3 = rs_chip_comm([sum_wg3]), rs_chip_comm([sum_wu3])

    dgt, dya, dyb, dsgu, datt, dm, dga2 = mixer_out_bwd(dh2, mm, ga2, ya, yb, pgt, wa, wb, wo, seq=S, tm=tm,
                                                        name="mixer_out_bwd", comms=[chip_wg3])
    dwo, dwa, dwb = mm_tn([(merged, dm), (sgu, dya), (att, dyb)], tt=tm_big, name="mixer_dw")
    pair_abo = rs_pair_comm([_stack_cols(dwa), _stack_cols(dwb), dwo.reshape(N_QUAD, D // N_QUAD, D)])
    duv, dws, dzs, dgln, dbln = sgu_bwd(puv, dsgu, g_sgu_ln, b_sgu_ln, ws, bs_t, cpb=cpb, name="sgu_bwd",
                                        comms=[pair_abo])
    chip_abo = rs_chip_comm(sums_of(pair_abo, "abo"))
    dqkv, dgq_h, dgk_h, dsk = attn_bwd(pqkv, datt, gq_t, gk_t, attn_sinks, seq=S, name="attn_bwd",
                                       comms=[chip_wu3, chip_abo])
    dgq = dgq_h.reshape(N_Q, HEAD_DIM).sum(axis=0, keepdims=True)
    dgk = dgk_h.reshape(N_KV, HEAD_DIM).sum(axis=0, keepdims=True)
    share3 = rs_share_comm(halves_of(chip_wg3, "wg3") + halves_of(chip_wu3, "wu3") + halves_of(chip_wd3, "wd3"))
    dwin_uv, dwin_qkv = mm_tn([(duv, xn2), (dqkv, xn2)], tt=tt_half, name="mixer_dwin_a", comms=[share3])
    (dwin_gt,) = mm_tn([(dgt, xn2)], tt=tt_half, name="mixer_dwin_b")
    dwin_parts = [dwin_uv, dwin_qkv, dwin_gt]
    r_wg3, r_wu3, r_wd3 = share3.bufs_out
    pair_win = rs_pair_comm([jnp.concatenate(dwin_parts, axis=0).reshape(win4.shape)])
    dh1, dsh2, dsc2, dgn2 = dx_norm_bwd([duv, dqkv, dgt], [w_uv, w_qkv, w_gt], h1, dh2, g_norm2, sc2, seq=S,
                                        tm=tm, name="mixer_dx", comms=[pair_win])
    chip_win = rs_chip_comm(sums_of(pair_win, "win"))

    dg1, du1, df1, dga1 = ffn_bwd_act(dh1, f1, ga1, wd1, g1, u1, seq=S, tm=tm, name="ffn1_act_bwd", comms=[chip_win])
    share_mix = rs_share_comm(halves_of(chip_win, "win") + halves_of(chip_abo, "abo"))

    db_s = dzs.reshape(CHUNK, N_GROUPS, D_A // N_GROUPS).sum(axis=2).T.reshape(1, N_GROUPS * CHUNK)
    tail = jnp.concatenate([db_s, dgq, dgk, dsk[0:1, 0:N_Q], loss_part.reshape(1, 1)], axis=1)
    tail = jnp.pad(tail, ((0, 0), (0, (-tail.shape[1]) % D)))
    lnrow = jnp.concatenate([dgln, dbln], axis=1)
    lnrow = jnp.pad(lnrow, ((0, 0), (0, (-lnrow.shape[1]) % D)))
    packed = jnp.concatenate([dgn2, dgn3, lnrow.reshape(-1, D), tail.reshape(-1, D), dws.reshape(-1, D)], axis=0)
    n_rows = packed.shape[0]
    packed = jnp.pad(packed, ((0, (-n_rows) % 8), (0, 0)))
    gather_small = ag8_comm(packed)

    dwg1, dwu1 = mm_tn([(dg1, xn1), (du1, xn1)], tt=tt_half, name="ffn1_dwgu", comms=[share_mix, gather_small])
    r_win, r_wa, r_wb, r_wo = share_mix.bufs_out
    small = sum8(gather_small.lands[0], name="sum_small")
    pair_gu1 = rs_pair_comm([dwg1, dwu1])
    (dwd1,) = mm_tn([(a1, df1)], tt=T, name="ffn1_dwd", comms=[pair_gu1])
    chip_gu1 = rs_chip_comm(sums_of(pair_gu1, "gu1"))
    pair_wd1 = rs_pair_comm([dwd1])
    dx, dsh1, dsc1, dgn1 = dx_norm_bwd([dg1, du1], [wg1, wu1], xt, dh1, g_norm1, sc1, seq=S, tm=tm, name="ffn1_dx",
                                       comms=[chip_gu1, pair_wd1])
    chip_wd1 = rs_chip_comm(sums_of(pair_wd1, "wd1"))
    share_gu1 = rs_share_comm(halves_of(chip_gu1, "gu1"))

    names = ["w_ada", "b_ada", "g_norm1", "ffn1_w_gate", "ffn1_w_up", "ffn1_w_down", "g_norm2", "w_in", "g_sgu_ln",
             "b_sgu_ln", "w_spatial", "b_spatial", "g_q", "g_k", "attn_sinks", "w_branch_a", "w_branch_b", "w_out",
             "g_norm3", "ffn2_w_gate", "ffn2_w_up", "ffn2_w_down"]
    weights = dict(zip(names, [w_ada, b_ada, g_norm1, ffn1_w_gate, ffn1_w_up, ffn1_w_down, g_norm2, w_in, g_sgu_ln,
                               b_sgu_ln, w_spatial, b_spatial, g_q, g_k, attn_sinks, w_branch_a, w_branch_b, w_out,
                               g_norm3, ffn2_w_gate, ffn2_w_up, ffn2_w_down]))
    m_in = dict(zip(names, [m_w_ada, m_b_ada, m_g_norm1, m_ffn1_w_gate, m_ffn1_w_up, m_ffn1_w_down, m_g_norm2, m_w_in,
                            m_g_sgu_ln, m_b_sgu_ln, m_w_spatial, m_b_spatial, m_g_q, m_g_k, m_attn_sinks, m_w_branch_a,
                            m_w_branch_b, m_w_out, m_g_norm3, m_ffn2_w_gate, m_ffn2_w_up, m_ffn2_w_down]))
    v_in = dict(zip(names, [v_w_ada, v_b_ada, v_g_norm1, v_ffn1_w_gate, v_ffn1_w_up, v_ffn1_w_down, v_g_norm2, v_w_in,
                            v_g_sgu_ln, v_b_sgu_ln, v_w_spatial, v_b_spatial, v_g_q, v_g_k, v_attn_sinks, v_w_branch_a,
                            v_w_branch_b, v_w_out, v_g_norm3, v_ffn2_w_gate, v_ffn2_w_up, v_ffn2_w_down]))
    transposed = ("ffn1_w_gate", "ffn1_w_up", "w_in", "ffn2_w_gate", "ffn2_w_up")
    grads, delta, new_m, new_v = {}, {}, {}, {}

    def update(group, steps, name, comms=()):
        form = lambda nm, a: (tr(a) if nm in transposed else a)[0].reshape(group[nm].shape)
        res = adamw([(form(nm, weights[nm]), g, form(nm, m_in[nm]), form(nm, v_in[nm])) for nm, g in group.items()],
                    steps=steps, name=name, comms=comms)
        back = lambda nm, a: tr(a[None]) if nm in transposed else a.reshape(weights[nm].shape)
        for (nm, g), (d, mn, vn) in zip(group.items(), res):
            grads[nm], delta[nm], new_m[nm], new_v[nm] = back(nm, g), back(nm, d), back(nm, mn), back(nm, vn)

    dmods = jnp.concatenate([dsh1, dsc1, dga1, dsh2, dsc2, dga2, dsh3, dsc3, dga3], axis=1)
    dmods = jnp.concatenate([dmods, jnp.pad(dgn1, ((0, 0), (0, (n_mod - 1) * D)))], axis=0)
    gather_dmods = ag8_comm(dmods)
    update({"ffn2_w_gate": r_wg3, "ffn2_w_up": r_wu3, "ffn2_w_down": r_wd3, "w_out": r_wo, "w_branch_a": r_wa,
            "w_branch_b": r_wb}, 8, "adamw_early", comms=[chip_wd1, share_gu1, gather_dmods])
    r_wg1, r_wu1 = share_gu1.bufs_out
    (gathered,) = gather_dmods.lands
    dmods_all = gathered[:, 0:B].reshape(N_DEV * B, n_mod * D)
    dmods_mine = lax.dynamic_slice(dmods_all, (0, quad * n_ada), (N_DEV * B, n_ada))
    share_wd1 = rs_share_comm(halves_of(chip_wd1, "wd1"))
    db_ada, dw_ada, g_gn1 = ada_bwd(c_all, dmods_all, dmods_mine, gathered[:, B, 0:D], name="ada_bwd", comms=[share_wd1])
    (r_wd1,) = share_wd1.bufs_out

    ln_rows = lnrow.size // D
    tail_rows = tail.size // D
    r = 2
    g_gn2, g_gn3 = small[0:1], small[1:2]
    ln_flat = small[r:r + ln_rows].reshape(1, -1)
    r += ln_rows
    tail_flat = small[r:r + tail_rows].reshape(1, -1)
    r += tail_rows
    g_ws = small[r:r + dws.size // D].reshape(w_spatial.shape)
    g_gln, g_bln = ln_flat[:, 0:D_A], ln_flat[:, D_A:2 * D_A]
    o = N_GROUPS * CHUNK
    g_bs = tail_flat[:, 0:o].reshape(b_spatial.shape)
    g_gq, g_gk, g_sk = tail_flat[:, o:o + HEAD_DIM], tail_flat[:, o + HEAD_DIM:o + 2 * HEAD_DIM], tail_flat[:, o + 2 * HEAD_DIM:o + 2 * HEAD_DIM + N_Q]
    loss = tail_flat[0, o + 2 * HEAD_DIM + N_Q]

    update({"w_ada": dw_ada}, 16, "adamw_w_ada")
    update({"ffn1_w_gate": r_wg1, "ffn1_w_up": r_wu1, "ffn1_w_down": r_wd1, "w_in": r_win}, 8, "adamw_late")

    update({"b_ada": db_ada, "g_norm1": g_gn1, "g_norm2": g_gn2, "g_norm3": g_gn3, "g_sgu_ln": g_gln,
            "b_sgu_ln": g_bln, "w_spatial": g_ws.reshape(-1, CHUNK), "b_spatial": g_bs.reshape(N_GROUPS, CHUNK),
            "g_q": g_gq, "g_k": g_gk, "attn_sinks": g_sk}, 1, "adamw_small")

    return (loss, dx.reshape(B, S, D), *[grads[nm] for nm in names], *[delta[nm] for nm in names],
            *[new_m[nm] for nm in names], *[new_v[nm] for nm in names])
```

```python
import functools
import math
import operator

import jax
import jax.numpy as jnp
from jax import lax
from jax.experimental import pallas as pl
from jax.experimental.pallas import tpu as pltpu

F32 = jnp.float32
BF16 = jnp.bfloat16
EPS = 1e-6
NEG = -1e30
N_DEV = 8
N_QUAD = 4
D_A = 512
N_GROUPS = 4
CHUNK = 128
HEAD_DIM = 64
N_KV = 2
Q_PER_KV = 4
N_Q = N_KV * Q_PER_KV
D_B = N_Q * HEAD_DIM
QKV_W = D_B + 2 * N_KV * HEAD_DIM
K_OFF = D_B
V_OFF = D_B + N_KV * HEAD_DIM
ATT_SCALE = HEAD_DIM ** -0.5
GELU_K = math.sqrt(2.0 / math.pi)
GELU_C = 0.044715
ADAM_LR, ADAM_B1, ADAM_B2, ADAM_EPS, ADAM_WD, ADAM_STEP = 0.001, 0.9, 0.999, 1e-08, 0.01, 10
VMEM_LIMIT_BYTES = 56 * 1024 * 1024
MESH = pl.DeviceIdType.MESH
NT = (((1,), (1,)), ((), ()))
TN = (((0,), (0,)), ((), ()))
ANY = pl.BlockSpec(memory_space=pl.ANY)


def _dot(a, b):
    return jnp.dot(a, b, preferred_element_type=F32)


def _dg(a, b, dims):
    return lax.dot_general(a, b, dims, preferred_element_type=F32)


def _gelu_parts(x):
    t = jnp.tanh(GELU_K * (x + GELU_C * x * x * x))
    return 0.5 * x * (1.0 + t), t


def _dgelu(x, t):
    return 0.5 * (1.0 + t) + 0.5 * x * (1.0 - t * t) * (GELU_K * (1.0 + 3.0 * GELU_C * x * x))


def _row_block(rows, target):
    b = min(rows, target)
    while rows % b or b % 8:
        b -= 1
    return b


def _mod_spec(mod, tps):
    mods, j = mod
    return pl.BlockSpec((1, None, 1, mods.shape[3]), lambda i: (i // tps, j, 0, 0))


def _resident(a):
    return pl.BlockSpec(a.shape, lambda *_: (0,) * a.ndim, pipeline_mode=pl.Buffered(1))


class Comm:
    def __init__(self, *, srcs=(), bufs=(), land_shapes=(), n_sems, start, finish):
        self.srcs, self.bufs, self.land_shapes = list(srcs), list(bufs), list(land_shapes)
        self.n_sems, self.start, self.finish = n_sems, start, finish
        self.bufs_out, self.lands = None, None


def _call(body, *, name, grid, in_specs, out_specs, out_shape, args, scratch_shapes=(), comms=(), prefetch=()):
    prefetch = list(prefetch)
    in_specs, out_specs, out_shape = list(in_specs), list(out_specs), list(out_shape)
    args, scratch = list(args), list(scratch_shapes)
    n_in, n_out, n_scr = len(args), len(out_shape), len(scratch)
    aliases, layout = {}, []
    for cm in comms:
        i0, o0, s0 = len(args), len(out_shape), len(scratch)
        args += cm.srcs + cm.bufs
        in_specs += [ANY] * (len(cm.srcs) + len(cm.bufs))
        out_shape += [jax.ShapeDtypeStruct(b.shape, b.dtype) for b in cm.bufs] + cm.land_shapes
        out_specs += [ANY] * (len(cm.bufs) + len(cm.land_shapes))
        for j in range(len(cm.bufs)):
            aliases[len(prefetch) + i0 + len(cm.srcs) + j] = o0 + j
        scratch += [pltpu.SemaphoreType.DMA((cm.n_sems,)), pltpu.SemaphoreType.DMA((cm.n_sems,))]
        layout.append((i0, o0, s0))
    n_in_all, n_out_all = len(args), len(out_shape)

    def wrapped(*refs):
        scalars, refs = refs[:len(prefetch)], refs[len(prefetch):]
        ins, outs, scr = refs[:n_in_all], refs[n_in_all:n_in_all + n_out_all], refs[n_in_all + n_out_all:]
        parts = []
        for cm, (i0, o0, s0) in zip(comms, layout):
            nb = len(cm.bufs)
            parts.append((ins[i0:i0 + len(cm.srcs)], outs[o0:o0 + nb], outs[o0 + nb:o0 + nb + len(cm.land_shapes)],
                          scr[s0], scr[s0 + 1]))
        if comms and grid:
            ids = [pl.program_id(d) for d in range(len(grid))]
            first = functools.reduce(operator.and_, [i == 0 for i in ids])
            last = functools.reduce(operator.and_, [i == g - 1 for i, g in zip(ids, grid)])

            @pl.when(first)
            def _():
                for cm, p in zip(comms, parts):
                    cm.start(*p)
        elif comms:
            for cm, p in zip(comms, parts):
                cm.start(*p)
        if body is not None:
            body(*scalars, *ins[:n_in], *outs[:n_out], *scr[:n_scr])
        if comms and grid:
            @pl.when(last)
            def _():
                for cm, p in zip(comms, parts):
                    cm.finish(*p)
        elif comms:
            for cm, p in zip(comms, parts):
                cm.finish(*p)

    if prefetch:
        kwargs = dict(grid_spec=pltpu.PrefetchScalarGridSpec(
            num_scalar_prefetch=len(prefetch), grid=grid, in_specs=in_specs, out_specs=out_specs, scratch_shapes=scratch))
    else:
        kwargs = dict(in_specs=in_specs, out_specs=out_specs, scratch_shapes=scratch, **(dict(grid=grid) if grid else {}))
    sem = ("arbitrary",) * len(grid)
    res = pl.pallas_call(
        wrapped, name=name, out_shape=out_shape, input_output_aliases=aliases,
        compiler_params=pltpu.CompilerParams(dimension_semantics=sem, vmem_limit_bytes=VMEM_LIMIT_BYTES), **kwargs,
    )(*prefetch, *args)
    for cm, (i0, o0, s0) in zip(comms, layout):
        nb = len(cm.bufs)
        cm.bufs_out = list(res[o0:o0 + nb])
        cm.lands = list(res[o0 + nb:o0 + nb + len(cm.land_shapes)])
    return list(res[:n_out])


def run_comms(comms, *, name):
    _call(None, name=name, grid=(), in_specs=[], out_specs=[], out_shape=[], args=[], comms=comms)


def norm_mod_fwd(h, g, sc, sh, *, seq, tm, name, comms=()):
    T, D = h.shape
    B, tps = T // seq, seq // tm

    def body(h_ref, g_ref, sc_ref, sh_ref, o_ref):
        x = h_ref[...]
        r = lax.rsqrt(jnp.mean(x * x, axis=-1, keepdims=True) + EPS)
        y = x * r * g_ref[...]
        o_ref[...] = (y * (1.0 + sc_ref[0]) + sh_ref[0]).astype(o_ref.dtype)

    return _call(
        body, name=name, grid=(T // tm,),
        in_specs=[pl.BlockSpec((tm, D), lambda i: (i, 0)), pl.BlockSpec((1, D), lambda i: (0, 0)),
                  _mod_spec(sc, tps), _mod_spec(sh, tps)],
        out_specs=[pl.BlockSpec((tm, D), lambda i: (i, 0))], out_shape=[jax.ShapeDtypeStruct((T, D), BF16)],
        args=[h, g, sc[0], sh[0]], comms=comms)[0]


def ffn_up(xn, wg, wu, *, tm, name, comms=()):
    T, D = xn.shape
    nq, fs, _ = wg.shape

    def body(x_ref, wg_ref, wu_ref, s_ref, q_ref, a_ref):
        x = x_ref[...]
        g = _dg(x, wg_ref[0], NT)
        u = _dg(x, wu_ref[0], NT)
        sg = jax.nn.sigmoid(g)
        s = g * sg
        s_ref[0] = s.astype(BF16)
        q_ref[0] = (u * (sg + s * (1.0 - sg))).astype(BF16)
        a_ref[0] = (s * u).astype(BF16)

    w_spec = pl.BlockSpec((1, fs, D), lambda q, i: (q, 0, 0))
    o_spec = pl.BlockSpec((1, tm, fs), lambda q, i: (q, i, 0))
    o_shape = jax.ShapeDtypeStruct((nq, T, fs), BF16)
    return _call(
        body, name=name, grid=(nq, T // tm),
        in_specs=[pl.BlockSpec((tm, D), lambda q, i: (i, 0)), w_spec, w_spec],
        out_specs=[o_spec, o_spec, o_spec], out_shape=[o_shape, o_shape, o_shape], args=[xn, wg, wu], comms=comms)


def _norm_mod(y, g_ref, sc_ref, sh_ref):
    nx, _ = _rms_parts(y)
    return ((nx * g_ref[...]) * (1.0 + sc_ref[0]) + sh_ref[0]).astype(BF16)


def ffn_down(a, wd, hin, ga, *, target=None, norm=None, seq, tm, name, comms=()):
    nq, T, fs = a.shape
    D = wd.shape[-1]
    B, tps, nt = T // seq, seq // tm, T // tm

    def body(a_ref, wd_ref, h_ref, ga_ref, *rest):
        rest = list(rest)
        t_ref = rest.pop(0) if target is not None else None
        norm_refs = [rest.pop(0) for _ in range(3)] if norm is not None else None
        o_ref, f_ref = rest[0], rest[1]
        f = _dot(a_ref[0], wd_ref[0])
        for q in range(1, nq):
            f = f + _dot(a_ref[q], wd_ref[q])
        f_ref[...] = f.astype(BF16)
        y = h_ref[...] + (0.5 * ga_ref[0]) * f
        if target is not None:
            e = y - t_ref[...]
            o_ref[...] = e * (1.0 / D)
            rest[2][...] = jnp.full(rest[2].shape, 0.5 * jnp.sum(e * e) * (1.0 / D), F32)
        else:
            o_ref[...] = y
        if norm is not None:
            rest[2][...] = _norm_mod(y, *norm_refs)

    tile = pl.BlockSpec((tm, D), lambda i: (i, 0))
    in_specs = [pl.BlockSpec((nq, tm, fs), lambda i: (0, i, 0)), _resident(wd), tile, _mod_spec(ga, tps)]
    out_specs = [tile, tile]
    out_shape = [jax.ShapeDtypeStruct((T, D), F32), jax.ShapeDtypeStruct((T, D), BF16)]
    args = [a, wd, hin, ga[0]]
    if target is not None:
        in_specs.append(tile)
        args.append(target)
        out_specs.append(pl.BlockSpec((1, 8, 128), lambda i: (i, 0, 0)))
        out_shape.append(jax.ShapeDtypeStruct((nt, 8, 128), F32))
    if norm is not None:
        in_specs += [pl.BlockSpec((1, D), lambda i: (0, 0)), _mod_spec(norm[1], tps), _mod_spec(norm[2], tps)]
        args += [norm[0], norm[1][0], norm[2][0]]
        out_specs.append(tile)
        out_shape.append(jax.ShapeDtypeStruct((T, D), BF16))
    return _call(body, name=name, grid=(nt,), in_specs=in_specs, out_specs=out_specs, out_shape=out_shape, args=args,
                 comms=comms)


def proj_fwd(xn, ws, *, tm, name, comms=()):
    T, D = xn.shape
    n = len(ws)

    def body(*refs):
        x = refs[0][...]
        for j in range(n):
            refs[1 + n + j][...] = _dg(x, refs[1 + j][...], NT).astype(BF16)

    return _call(
        body, name=name, grid=(T // tm,),
        in_specs=[pl.BlockSpec((tm, D), lambda i: (i, 0))] + [_resident(w) for w in ws],
        out_specs=[pl.BlockSpec((tm, w.shape[0]), lambda i: (i, 0)) for w in ws],
        out_shape=[jax.ShapeDtypeStruct((T, w.shape[0]), BF16) for w in ws], args=[xn, *ws], comms=comms)


def _layer_norm_parts(v):
    mu = jnp.mean(v, axis=-1, keepdims=True)
    d = v - mu
    rstd = lax.rsqrt(jnp.mean(d * d, axis=-1, keepdims=True) + EPS)
    return d * rstd, rstd


def _causal():
    row = lax.broadcasted_iota(jnp.int32, (CHUNK, CHUNK), 0)
    col = lax.broadcasted_iota(jnp.int32, (CHUNK, CHUNK), 1)
    return row >= col


def sgu_fwd(puv, gln, bln, ws, bs_t, *, cpb, name, comms=()):
    T = puv.shape[0]
    gd = D_A // N_GROUPS
    tm = cpb * CHUNK

    def body(p_ref, gln_ref, bln_ref, ws_ref, bs_ref, o_ref):
        causal = _causal()
        wm = [jnp.where(causal, ws_ref[g], 0.0).astype(BF16) for g in range(N_GROUPS)]
        for j in range(cpb):
            rows = slice(j * CHUNK, (j + 1) * CHUNK)
            u, _ = _gelu_parts(p_ref[rows, 0:D_A].astype(F32))
            vv, _ = _gelu_parts(p_ref[rows, D_A:2 * D_A].astype(F32))
            vhat, _ = _layer_norm_parts(vv)
            vn = (vhat * gln_ref[...] + bln_ref[...]).astype(BF16)
            for g in range(N_GROUPS):
                cols = slice(g * gd, (g + 1) * gd)
                z = _dot(wm[g], vn[:, cols]) + bs_ref[:, g:g + 1]
                o_ref[rows, cols] = (u[:, cols] * z).astype(BF16)

    full = lambda a: pl.BlockSpec(a.shape, lambda i: (0,) * a.ndim)
    return _call(
        body, name=name, grid=(T // tm,),
        in_specs=[pl.BlockSpec((tm, 2 * D_A), lambda i: (i, 0)), full(gln), full(bln), full(ws), full(bs_t)],
        out_specs=[pl.BlockSpec((tm, D_A), lambda i: (i, 0))], out_shape=[jax.ShapeDtypeStruct((T, D_A), BF16)],
        args=[puv, gln, bln, ws, bs_t], comms=comms)[0]


def _rms_parts(x):
    r = lax.rsqrt(jnp.mean(x * x, axis=-1, keepdims=True) + EPS)
    return x * r, r


KV_W = Q_PER_KV * HEAD_DIM
KV2 = N_KV * HEAD_DIM
STACK = Q_PER_KV * CHUNK


def _iota(shape, dim):
    return lax.broadcasted_iota(jnp.int32, shape, dim)


def _head_mean_matrix(n):
    return jnp.where((_iota((n, n), 0) >> 6) == (_iota((n, n), 1) >> 6), 1.0 / HEAD_DIM, 0.0).astype(BF16)


def _repeat_matrix(h):
    return jnp.where(_iota((KV2, KV_W), 0) == h * HEAD_DIM + (_iota((KV2, KV_W), 1) & (HEAD_DIM - 1)), 1.0, 0.0).astype(BF16)


def _fold_matrix(h):
    j, l = _iota((KV_W, KV2), 0), _iota((KV_W, KV2), 1)
    return jnp.where(((j & (HEAD_DIM - 1)) == (l & (HEAD_DIM - 1))) & ((l >> 6) == h), 1.0, 0.0).astype(BF16)


def _dot_split(x, m):
    hi = x.astype(BF16)
    lo = (x - hi.astype(F32)).astype(BF16)
    return _dot(hi, m) + _dot(lo, m)


def _head_rms(x, mean_matrix):
    r = lax.rsqrt(_dot_split(x * x, mean_matrix) + EPS)
    return x * r, r


def _stack_heads(x):
    head = _iota(x.shape, 1) >> 6
    return jnp.concatenate([jnp.where(head == g, x, 0.0).astype(BF16) for g in range(Q_PER_KV)], axis=0)


def _unstack_heads(y):
    head = _iota((CHUNK, KV_W), 1) >> 6
    out = jnp.where(head == 0, y[0:CHUNK], 0.0)
    for g in range(1, Q_PER_KV):
        out = out + jnp.where(head == g, y[g * CHUNK:(g + 1) * CHUNK], 0.0)
    return out


SOFTMAX_ROWS = 32


def _fill_mask_bias(bias_ref):
    qi = _iota((CHUNK, 2 * CHUNK), 0)
    kj = _iota((CHUNK, 2 * CHUNK), 1)
    diff = qi + CHUNK - kj
    window = (diff >= 0) & (diff < CHUNK)
    bias_ref[0] = jnp.where(window & (kj >= CHUNK), 0.0, NEG)
    bias_ref[1] = jnp.where(window, 0.0, NEG)


def _softmax_rows(s_ref, bias_ref, first, sink_ref, h, c):
    rows = pl.ds(pl.multiple_of(c * SOFTMAX_ROWS, SOFTMAX_ROWS), SOFTMAX_ROWS)
    in_block = pl.ds(pl.multiple_of((c % (CHUNK // SOFTMAX_ROWS)) * SOFTMAX_ROWS, SOFTMAX_ROWS), SOFTMAX_ROWS)
    sink = sink_ref[0, h * Q_PER_KV + c // (CHUNK // SOFTMAX_ROWS)]
    s = s_ref[rows, :] + bias_ref[first, in_block, :]
    m = jnp.maximum(jnp.max(s, axis=-1, keepdims=True), sink)
    p = jnp.exp(s - m)
    es = jnp.exp(sink - m)
    inv = 1.0 / (jnp.sum(p, axis=-1, keepdims=True) + es)
    return rows, p * inv, es * inv


def _fill_keys(p_ref, gk_ref, krep, vrep, seq):
    mean_k = _head_mean_matrix(KV2)
    reps = [_repeat_matrix(h) for h in range(N_KV)]
    for h in range(N_KV):
        krep[h][0:CHUNK, :] = jnp.zeros((CHUNK, KV_W), BF16)
        vrep[h][0:CHUNK, :] = jnp.zeros((CHUNK, KV_W), BF16)
    rows = min(seq, 4 * CHUNK)

    def piece(j, carry):
        r0 = pl.multiple_of(j * rows, CHUNK)
        nk, _ = _head_rms(p_ref[pl.ds(r0, rows), K_OFF:K_OFF + KV2].astype(F32), mean_k)
        kn = (nk * gk_ref[...]).astype(BF16)
        v = p_ref[pl.ds(r0, rows), V_OFF:V_OFF + KV2].astype(BF16)
        r1 = pl.multiple_of(r0 + CHUNK, CHUNK)
        for h in range(N_KV):
            krep[h][pl.ds(r1, rows), :] = _dot(kn, reps[h]).astype(BF16)
            vrep[h][pl.ds(r1, rows), :] = _dot(v, reps[h]).astype(BF16)
        return carry

    lax.fori_loop(0, seq // rows, piece, 0)


def attn_fwd(pqkv, gq_t, gk_t, sinks, *, seq, name, comms=()):
    T = pqkv.shape[0]
    nb = seq // CHUNK

    def body(p_ref, gq_ref, gk_ref, sink_ref, o_ref, k0, k1, v0, v1, bias_ref, s_ref, pr_ref):
        krep, vrep = [k0, k1], [v0, v1]
        _fill_keys(p_ref, gk_ref, krep, vrep, seq)
        _fill_mask_bias(bias_ref)
        mean_q = _head_mean_matrix(D_B)

        def block(i, carry):
            r0 = pl.multiple_of(i * CHUNK, CHUNK)
            first = jnp.minimum(i, 1)
            nq, _ = _head_rms(p_ref[pl.ds(r0, CHUNK), 0:D_B].astype(F32), mean_q)
            qn = nq * (gq_ref[...] * ATT_SCALE)
            for h in range(N_KV):
                qs = _stack_heads(qn[:, h * KV_W:(h + 1) * KV_W])
                s_ref[...] = _dg(qs, krep[h][pl.ds(r0, 2 * CHUNK), :], NT)

                def rows_of(c, carry2):
                    rows, probs, _ = _softmax_rows(s_ref, bias_ref, first, sink_ref, h, c)
                    pr_ref[rows, :] = probs.astype(BF16)
                    return carry2

                lax.fori_loop(0, STACK // SOFTMAX_ROWS, rows_of, 0, unroll=True)
                out = _unstack_heads(_dot(pr_ref[...], vrep[h][pl.ds(r0, 2 * CHUNK), :]))
                o_ref[pl.ds(r0, CHUNK), h * KV_W:(h + 1) * KV_W] = out.astype(BF16)
            return carry

        lax.fori_loop(0, nb, block, 0)

    return _call(
        body, name=name, grid=(T // seq,),
        in_specs=[pl.BlockSpec((seq, QKV_W), lambda b: (b, 0)), pl.BlockSpec(gq_t.shape, lambda b: (0, 0)),
                  pl.BlockSpec(gk_t.shape, lambda b: (0, 0)), pl.BlockSpec(memory_space=pltpu.SMEM)],
        out_specs=[pl.BlockSpec((seq, D_B), lambda b: (b, 0))], out_shape=[jax.ShapeDtypeStruct((T, D_B), BF16)],
        scratch_shapes=[pltpu.VMEM((seq + CHUNK, KV_W), BF16)] * 4
        + [pltpu.VMEM((2, CHUNK, 2 * CHUNK), F32), pltpu.VMEM((STACK, 2 * CHUNK), F32), pltpu.VMEM((STACK, 2 * CHUNK), BF16)],
        args=[pqkv, gq_t, gk_t, sinks], comms=comms)[0]


def mixer_out_fwd(sgu, att, pg, hin, ga, wa, wb, wo, norm, *, seq, tm, name, comms=()):
    T, D = hin.shape
    B, tps = T // seq, seq // tm

    def body(s_ref, t_ref, pg_ref, h_ref, ga_ref, wa_ref, wb_ref, wo_ref, g_ref, sc_ref, sh_ref,
             ya_ref, yb_ref, mg_ref, m_ref, ho_ref, xn_ref):
        ya = _dot(s_ref[...], wa_ref[...])
        yb = _dot(t_ref[...], wb_ref[...])
        merged = (jax.nn.sigmoid(pg_ref[:, 0:D].astype(F32)) * ya
                  + jax.nn.sigmoid(pg_ref[:, D:2 * D].astype(F32)) * yb)
        mg = merged.astype(BF16)
        m = _dot(mg, wo_ref[...])
        ya_ref[...] = ya.astype(BF16)
        yb_ref[...] = yb.astype(BF16)
        mg_ref[...] = mg
        m_ref[...] = m.astype(BF16)
        y = h_ref[...] + ga_ref[0] * m
        ho_ref[...] = y
        xn_ref[...] = _norm_mod(y, g_ref, sc_ref, sh_ref)

    tile = lambda w: pl.BlockSpec((tm, w), lambda i: (i, 0))
    b16 = jax.ShapeDtypeStruct((T, D), BF16)
    return _call(
        body, name=name, grid=(T // tm,),
        in_specs=[tile(D_A), tile(D_B), tile(2 * D), tile(D), _mod_spec(ga, tps), _resident(wa), _resident(wb),
                  _resident(wo), pl.BlockSpec((1, D), lambda i: (0, 0)), _mod_spec(norm[1], tps), _mod_spec(norm[2], tps)],
        out_specs=[tile(D)] * 6, out_shape=[b16, b16, b16, b16, jax.ShapeDtypeStruct((T, D), F32), b16],
        args=[sgu, att, pg, hin, ga[0], wa, wb, wo, norm[0], norm[1][0], norm[2][0]], comms=comms)


def ffn_bwd_act(dh, f, ga, wd, s, q, *, seq, tm, name, comms=()):
    T, D = dh.shape
    nq, fs, _ = wd.shape
    B, tps = T // seq, seq // tm

    def body(dh_ref, f_ref, ga_ref, wd_ref, s_ref, q_ref, dg_ref, du_ref, df_ref, dga_ref):
        i = pl.program_id(0)
        d = dh_ref[...]
        df = ((0.5 * ga_ref[0]) * d).astype(BF16)
        df_ref[...] = df
        part = 0.5 * jnp.sum(d * f_ref[...].astype(F32), axis=0, keepdims=True)
        for j in range(nq):
            da = _dg(df, wd_ref[j], NT)
            du_ref[j] = (da * s_ref[j].astype(F32)).astype(BF16)
            dg_ref[j] = (da * q_ref[j].astype(F32)).astype(BF16)

        @pl.when(i % tps == 0)
        def _():
            dga_ref[0] = part

        @pl.when(i % tps != 0)
        def _():
            dga_ref[0] += part

    tile = pl.BlockSpec((tm, D), lambda i: (i, 0))
    per_seq = pl.BlockSpec((1, 1, D), lambda i: (i // tps, 0, 0))
    act = pl.BlockSpec((nq, tm, fs), lambda i: (0, i, 0))
    o_shape = jax.ShapeDtypeStruct((nq, T, fs), BF16)
    dg, du, df, dga = _call(
        body, name=name, grid=(T // tm,), in_specs=[tile, tile, _mod_spec(ga, tps), _resident(wd), act, act],
        out_specs=[act, act, tile, per_seq],
        out_shape=[o_shape, o_shape, jax.ShapeDtypeStruct((T, D), BF16), jax.ShapeDtypeStruct((B, 1, D), F32)],
        args=[dh, f, ga[0], wd, s, q], comms=comms)
    return dg, du, df, dga.reshape(B, D)


def mm_tn(pairs, *, tt, name, comms=()):
    n = len(pairs)
    flat = []
    for pair in pairs:
        for a in pair:
            if not any(a is b for b in flat):
                flat.append(a)
    where = lambda a: [k for k, b in enumerate(flat) if a is b][0]
    nq = max([a.shape[0] for a in flat if a.ndim == 3] + [1])
    T = flat[0].shape[-2]
    tt = min(tt, T)
    nt = T // tt
    stacked = [x.ndim == 3 or y.ndim == 3 for x, y in pairs]

    def body(*refs):
        in_refs, o_refs, accs = refs[:len(flat)], refs[len(flat):len(flat) + n], refs[len(flat) + n:]
        t = pl.program_id(1)
        value = lambda a: in_refs[where(a)][0] if a.ndim == 3 else in_refs[where(a)][...]

        def put(j, v):
            if stacked[j]:
                o_refs[j][0] = v.astype(BF16)
            else:
                o_refs[j][...] = v.astype(BF16)

        for j, (x, y) in enumerate(pairs):
            part = _dg(value(x), value(y), TN)
            if nt == 1:
                put(j, part)
                continue

            @pl.when(t == 0)
            def _():
                accs[j][...] = part

            @pl.when(t != 0)
            def _():
                accs[j][...] += part

        if nt > 1:
            @pl.when(t == nt - 1)
            def _():
                for j in range(n):
                    put(j, accs[j][...])

    def spec(a):
        if a.ndim == 3:
            return pl.BlockSpec((1, tt, a.shape[2]), lambda q, t: (q, t, 0))
        return pl.BlockSpec((tt, a.shape[1]), lambda q, t: (t, 0))

    out_specs, out_shape = [], []
    for j, (x, y) in enumerate(pairs):
        K, N = x.shape[-1], y.shape[-1]
        if stacked[j]:
            out_specs.append(pl.BlockSpec((1, K, N), lambda q, t: (q, 0, 0)))
            out_shape.append(jax.ShapeDtypeStruct((nq, K, N), BF16))
        else:
            out_specs.append(pl.BlockSpec((K, N), lambda q, t: (0, 0)))
            out_shape.append(jax.ShapeDtypeStruct((K, N), BF16))
    return _call(
        body, name=name, grid=(nq, nt), in_specs=[spec(a) for a in flat],
        out_specs=out_specs, out_shape=out_shape,
        scratch_shapes=[pltpu.VMEM((x.shape[-1], y.shape[-1]), F32) for x, y in pairs] if nt > 1 else [],
        args=flat, comms=comms)


def dx_norm_bwd(dys, ws, hin, dh_out, g, sc, *, seq, tm, name, comms=()):
    T, D = hin.shape
    B, tps = T // seq, seq // tm
    n = len(dys)

    def body(*refs):
        dy_refs, w_refs = refs[:n], refs[n:2 * n]
        h_ref, dho_ref, g_ref, sc_ref, dhi_ref, dsh_ref, dsc_ref, dgn_ref = refs[2 * n:]
        i = pl.program_id(0)
        dxn = None
        for j in range(n):
            if dys[j].ndim == 3:
                for q in range(dys[j].shape[0]):
                    part = _dot(dy_refs[j][q], w_refs[j][q])
                    dxn = part if dxn is None else dxn + part
            else:
                part = _dot(dy_refs[j][...], w_refs[j][...])
                dxn = part if dxn is None else dxn + part
        x = h_ref[...]
        nx, r = _rms_parts(x)
        gain = g_ref[...]
        one_sc = 1.0 + sc_ref[0]
        dsh = jnp.sum(dxn, axis=0, keepdims=True)
        dsc = jnp.sum(dxn * (nx * gain), axis=0, keepdims=True)
        dgn = jnp.sum(dxn * one_sc * nx, axis=0, keepdims=True)
        dn = dxn * one_sc * gain
        dhi_ref[...] = dho_ref[...] + r * (dn - nx * jnp.mean(dn * nx, axis=-1, keepdims=True))

        @pl.when(i % tps == 0)
        def _():
            dsh_ref[0] = dsh
            dsc_ref[0] = dsc

        @pl.when(i % tps != 0)
        def _():
            dsh_ref[0] += dsh
            dsc_ref[0] += dsc

        @pl.when(i == 0)
        def _():
            dgn_ref[...] = dgn

        @pl.when(i != 0)
        def _():
            dgn_ref[...] += dgn

    def dy_spec(a):
        if a.ndim == 3:
            return pl.BlockSpec((a.shape[0], tm, a.shape[2]), lambda i: (0, i, 0))
        return pl.BlockSpec((tm, a.shape[1]), lambda i: (i, 0))

    tile = pl.BlockSpec((tm, D), lambda i: (i, 0))
    per_seq = pl.BlockSpec((1, 1, D), lambda i: (i // tps, 0, 0))
    row = pl.BlockSpec((1, D), lambda i: (0, 0))
    dhi, dsh, dsc, dgn = _call(
        body, name=name, grid=(T // tm,),
        in_specs=[dy_spec(a) for a in dys] + [_resident(w) for w in ws] + [tile, tile, row, _mod_spec(sc, tps)],
        out_specs=[tile, per_seq, per_seq, row],
        out_shape=[jax.ShapeDtypeStruct((T, D), F32), jax.ShapeDtypeStruct((B, 1, D), F32),
                   jax.ShapeDtypeStruct((B, 1, D), F32), jax.ShapeDtypeStruct((1, D), F32)],
        args=[*dys, *ws, hin, dh_out, g, sc[0]], comms=comms)
    return dhi, dsh.reshape(B, D), dsc.reshape(B, D), dgn


def mixer_out_bwd(dh, m, ga, ya, yb, pg, wa, wb, wo, *, seq, tm, name, comms=()):
    T, D = dh.shape
    B, tps = T // seq, seq // tm

    def body(dh_ref, m_ref, ga_ref, ya_ref, yb_ref, pg_ref, wa_ref, wb_ref, wo_ref,
             dg_ref, dya_ref, dyb_ref, ds_ref, dt_ref, dm_ref, dga_ref):
        i = pl.program_id(0)
        d = dh_ref[...]
        dm = (ga_ref[0] * d).astype(BF16)
        dm_ref[...] = dm
        part = jnp.sum(d * m_ref[...].astype(F32), axis=0, keepdims=True)

        @pl.when(i % tps == 0)
        def _():
            dga_ref[0] = part

        @pl.when(i % tps != 0)
        def _():
            dga_ref[0] += part

        dmg = _dg(dm, wo_ref[...], NT)
        sa = jax.nn.sigmoid(pg_ref[:, 0:D].astype(F32))
        sb = jax.nn.sigmoid(pg_ref[:, D:2 * D].astype(F32))
        dg_ref[:, 0:D] = (dmg * ya_ref[...].astype(F32) * (sa * (1.0 - sa))).astype(BF16)
        dg_ref[:, D:2 * D] = (dmg * yb_ref[...].astype(F32) * (sb * (1.0 - sb))).astype(BF16)
        dya = (dmg * sa).astype(BF16)
        dyb = (dmg * sb).astype(BF16)
        dya_ref[...] = dya
        dyb_ref[...] = dyb
        ds_ref[...] = _dg(dya, wa_ref[...], NT).astype(BF16)
        dt_ref[...] = _dg(dyb, wb_ref[...], NT).astype(BF16)

    tile = lambda w: pl.BlockSpec((tm, w), lambda i: (i, 0))
    per_seq = pl.BlockSpec((1, 1, D), lambda i: (i // tps, 0, 0))
    dg, dya, dyb, ds, dt, dm, dga = _call(
        body, name=name, grid=(T // tm,),
        in_specs=[tile(D), tile(D), _mod_spec(ga, tps), tile(D), tile(D), tile(2 * D), _resident(wa), _resident(wb),
                  _resident(wo)],
        out_specs=[tile(2 * D), tile(D), tile(D), tile(D_A), tile(D_B), tile(D), per_seq],
        out_shape=[jax.ShapeDtypeStruct((T, 2 * D), BF16), jax.ShapeDtypeStruct((T, D), BF16),
                   jax.ShapeDtypeStruct((T, D), BF16), jax.ShapeDtypeStruct((T, D_A), BF16),
                   jax.ShapeDtypeStruct((T, D_B), BF16), jax.ShapeDtypeStruct((T, D), BF16),
                   jax.ShapeDtypeStruct((B, 1, D), F32)],
        args=[dh, m, ga[0], ya, yb, pg, wa, wb, wo], comms=comms)
    return dg, dya, dyb, ds, dt, dm, dga.reshape(B, D)


def sgu_bwd(puv, dsgu, gln, bln, ws, bs_t, *, cpb, name, comms=()):
    T = puv.shape[0]
    gd = D_A // N_GROUPS
    tm = cpb * CHUNK

    def body(p_ref, ds_ref, gln_ref, bln_ref, ws_ref, bs_ref, d_ref, dws_ref, dz_ref, dgl_ref, dbl_ref):
        i = pl.program_id(0)
        causal = _causal()
        wf = [jnp.where(causal, ws_ref[g], 0.0) for g in range(N_GROUPS)]
        wm = [w.astype(BF16) for w in wf]
        wt = [w.T.astype(BF16) for w in wf]
        dws = [jnp.zeros((CHUNK, CHUNK), F32) for _ in range(N_GROUPS)]
        dzs = jnp.zeros((CHUNK, D_A), F32)
        dgl = jnp.zeros((1, D_A), F32)
        dbl = jnp.zeros((1, D_A), F32)
        for j in range(cpb):
            rows = slice(j * CHUNK, (j + 1) * CHUNK)
            au = p_ref[rows, 0:D_A].astype(F32)
            av = p_ref[rows, D_A:2 * D_A].astype(F32)
            u, tu = _gelu_parts(au)
            vv, tv = _gelu_parts(av)
            vhat, rstd = _layer_norm_parts(vv)
            vn = (vhat * gln_ref[...] + bln_ref[...]).astype(BF16)
            dsg = ds_ref[rows, :].astype(F32)
            dz = dsg * u
            dzb = dz.astype(BF16)
            zs, dvns = [], []
            for g in range(N_GROUPS):
                cols = slice(g * gd, (g + 1) * gd)
                zs.append(_dot(wm[g], vn[:, cols]) + bs_ref[:, g:g + 1])
                dws[g] = dws[g] + _dg(dzb[:, cols], vn[:, cols], NT)
                dvns.append(_dot(wt[g], dzb[:, cols]))
            z = jnp.concatenate(zs, axis=1)
            dvn = jnp.concatenate(dvns, axis=1)
            d_ref[rows, 0:D_A] = (dsg * z * _dgelu(au, tu)).astype(BF16)
            dvh = dvn * gln_ref[...]
            dvv = rstd * (dvh - jnp.mean(dvh, axis=-1, keepdims=True)
                          - vhat * jnp.mean(dvh * vhat, axis=-1, keepdims=True))
            d_ref[rows, D_A:2 * D_A] = (dvv * _dgelu(av, tv)).astype(BF16)
            dzs = dzs + dz
            dgl = dgl + jnp.sum(dvn * vhat, axis=0, keepdims=True)
            dbl = dbl + jnp.sum(dvn, axis=0, keepdims=True)

        @pl.when(i == 0)
        def _():
            for g in range(N_GROUPS):
                dws_ref[g] = jnp.where(causal, dws[g], 0.0)
            dz_ref[...] = dzs
            dgl_ref[...] = dgl
            dbl_ref[...] = dbl

        @pl.when(i != 0)
        def _():
            for g in range(N_GROUPS):
                dws_ref[g] += jnp.where(causal, dws[g], 0.0)
            dz_ref[...] += dzs
            dgl_ref[...] += dgl
            dbl_ref[...] += dbl

    full = lambda a: pl.BlockSpec(a.shape, lambda i: (0,) * a.ndim)
    acc = lambda s: pl.BlockSpec(s, lambda i: (0,) * len(s))
    return _call(
        body, name=name, grid=(T // tm,),
        in_specs=[pl.BlockSpec((tm, 2 * D_A), lambda i: (i, 0)), pl.BlockSpec((tm, D_A), lambda i: (i, 0)),
                  full(gln), full(bln), full(ws), full(bs_t)],
        out_specs=[pl.BlockSpec((tm, 2 * D_A), lambda i: (i, 0)), acc((N_GROUPS, CHUNK, CHUNK)), acc((CHUNK, D_A)),
                   acc((1, D_A)), acc((1, D_A))],
        out_shape=[jax.ShapeDtypeStruct((T, 2 * D_A), BF16), jax.ShapeDtypeStruct((N_GROUPS, CHUNK, CHUNK), F32),
                   jax.ShapeDtypeStruct((CHUNK, D_A), F32), jax.ShapeDtypeStruct((1, D_A), F32),
                   jax.ShapeDtypeStruct((1, D_A), F32)],
        args=[puv, dsgu, gln, bln, ws, bs_t], comms=comms)


def attn_bwd(pqkv, datt, gq_t, gk_t, sinks, *, seq, name, comms=()):
    T = pqkv.shape[0]
    nb = seq // CHUNK

    def body(p_ref, do_ref, gq_ref, gk_ref, sink_ref, d_ref, dgq_ref, dgk_ref, dsk_ref,
             k0, k1, v0, v1, dk0, dk1, dv0, dv1, dgq_s, dsk_s, bias_ref, s_ref, dp_ref, pr_ref, ds_ref):
        b = pl.program_id(0)
        krep, vrep, dkacc, dvacc = [k0, k1], [v0, v1], [dk0, dk1], [dv0, dv1]
        _fill_keys(p_ref, gk_ref, krep, vrep, seq)
        _fill_mask_bias(bias_ref)
        for h in range(N_KV):
            dkacc[h][...] = jnp.zeros_like(dkacc[h])
            dvacc[h][...] = jnp.zeros_like(dvacc[h])
        dgq_s[...] = jnp.zeros_like(dgq_s)
        dsk_s[...] = jnp.zeros_like(dsk_s)
        mean_q = _head_mean_matrix(D_B)
        lane = _iota((1, 128), 1)

        def block(i, carry):
            r0 = pl.multiple_of(i * CHUNK, CHUNK)
            first = jnp.minimum(i, 1)
            nq, rq = _head_rms(p_ref[pl.ds(r0, CHUNK), 0:D_B].astype(F32), mean_q)
            qn = nq * (gq_ref[...] * ATT_SCALE)
            dqn_parts = []
            for h in range(N_KV):
                cols = slice(h * KV_W, (h + 1) * KV_W)
                qs = _stack_heads(qn[:, cols])
                kk = krep[h][pl.ds(r0, 2 * CHUNK), :]
                dos = _stack_heads(do_ref[pl.ds(r0, CHUNK), cols].astype(F32))
                s_ref[...] = _dg(qs, kk, NT)
                dp_ref[...] = _dg(dos, vrep[h][pl.ds(r0, 2 * CHUNK), :], NT)

                def rows_of(c, carry2):
                    rows, probs, psink = _softmax_rows(s_ref, bias_ref, first, sink_ref, h, c)
                    dp = dp_ref[rows, :]
                    dr = jnp.sum(probs * dp, axis=-1, keepdims=True)
                    pr_ref[rows, :] = probs.astype(BF16)
                    ds_ref[rows, :] = (probs * (dp - dr)).astype(BF16)
                    head = h * Q_PER_KV + c // (CHUNK // SOFTMAX_ROWS)
                    dsk_s[...] += jnp.where(lane == head, -jnp.sum(psink * dr), 0.0)
                    return carry2

                lax.fori_loop(0, STACK // SOFTMAX_ROWS, rows_of, 0, unroll=True)
                dqn_parts.append(_unstack_heads(_dot(ds_ref[...], kk)))
                dkacc[h][pl.ds(r0, 2 * CHUNK), :] += _dg(ds_ref[...], qs, TN)
                dvacc[h][pl.ds(r0, 2 * CHUNK), :] += _dg(pr_ref[...], dos, TN)
            dqn = jnp.concatenate(dqn_parts, axis=1) * ATT_SCALE
            dn = dqn * gq_ref[...]
            d_ref[pl.ds(r0, CHUNK), 0:D_B] = (rq * (dn - nq * _dot_split(dn * nq, mean_q))).astype(BF16)
            dgq_s[...] += jnp.sum(dqn * nq, axis=0, keepdims=True)
            return carry

        lax.fori_loop(0, nb, block, 0)

        mean_k = _head_mean_matrix(KV2)
        folds = [_fold_matrix(h) for h in range(N_KV)]
        rows = min(seq, 4 * CHUNK)

        def piece(j, dgk):
            r0 = pl.multiple_of(j * rows, CHUNK)
            r1 = pl.multiple_of(r0 + CHUNK, CHUNK)
            dkn = _dot_split(dkacc[0][pl.ds(r1, rows), :], folds[0]) + _dot_split(dkacc[1][pl.ds(r1, rows), :], folds[1])
            dv = _dot_split(dvacc[0][pl.ds(r1, rows), :], folds[0]) + _dot_split(dvacc[1][pl.ds(r1, rows), :], folds[1])
            nk, rk = _head_rms(p_ref[pl.ds(r0, rows), K_OFF:K_OFF + KV2].astype(F32), mean_k)
            dn = dkn * gk_ref[...]
            d_ref[pl.ds(r0, rows), K_OFF:K_OFF + KV2] = (rk * (dn - nk * _dot_split(dn * nk, mean_k))).astype(BF16)
            d_ref[pl.ds(r0, rows), V_OFF:V_OFF + KV2] = dv.astype(BF16)
            return dgk + jnp.sum(dkn * nk, axis=0, keepdims=True)

        dgk = lax.fori_loop(0, seq // rows, piece, jnp.zeros((1, KV2), F32))

        @pl.when(b == 0)
        def _():
            dgq_ref[...] = dgq_s[...]
            dgk_ref[...] = dgk
            dsk_ref[...] = jnp.broadcast_to(dsk_s[...], dsk_ref.shape)

        @pl.when(b != 0)
        def _():
            dgq_ref[...] += dgq_s[...]
            dgk_ref[...] += dgk
            dsk_ref[...] += jnp.broadcast_to(dsk_s[...], dsk_ref.shape)

    acc = lambda s: pl.BlockSpec(s, lambda b: (0, 0))
    return _call(
        body, name=name, grid=(T // seq,),
        in_specs=[pl.BlockSpec((seq, QKV_W), lambda b: (b, 0)), pl.BlockSpec((seq, D_B), lambda b: (b, 0)),
                  pl.BlockSpec(gq_t.shape, lambda b: (0, 0)), pl.BlockSpec(gk_t.shape, lambda b: (0, 0)),
                  pl.BlockSpec(memory_space=pltpu.SMEM)],
        out_specs=[pl.BlockSpec((seq, QKV_W), lambda b: (b, 0)), acc((1, D_B)), acc((1, KV2)), acc((8, 128))],
        out_shape=[jax.ShapeDtypeStruct((T, QKV_W), BF16), jax.ShapeDtypeStruct((1, D_B), F32),
                   jax.ShapeDtypeStruct((1, KV2), F32), jax.ShapeDtypeStruct((8, 128), F32)],
        scratch_shapes=[pltpu.VMEM((seq + CHUNK, KV_W), BF16)] * 4 + [pltpu.VMEM((seq + CHUNK, KV_W), F32)] * 4
        + [pltpu.VMEM((1, D_B), F32), pltpu.VMEM((1, 128), F32), pltpu.VMEM((2, CHUNK, 2 * CHUNK), F32)]
        + [pltpu.VMEM((STACK, 2 * CHUNK), F32)] * 2 + [pltpu.VMEM((STACK, 2 * CHUNK), BF16)] * 2,
        args=[pqkv, datt, gq_t, gk_t, sinks], comms=comms)


def ada_fwd(c_all, w, b, *, name):
    nb, D = c_all.shape
    N = w.shape[1]
    tn = N // 3

    def body(c_ref, w_ref, b_ref, o_ref):
        c = c_ref[...]
        cond = (c * jax.nn.sigmoid(c)).astype(BF16)
        o_ref[...] = _dot(cond, w_ref[...].astype(BF16)) + b_ref[...]

    return _call(
        body, name=name, grid=(3,),
        in_specs=[pl.BlockSpec((nb, D), lambda j: (0, 0)), pl.BlockSpec((D, tn), lambda j: (0, j)),
                  pl.BlockSpec((1, tn), lambda j: (0, j))],
        out_specs=[pl.BlockSpec((nb, tn), lambda j: (0, j))], out_shape=[jax.ShapeDtypeStruct((nb, N), F32)],
        args=[c_all, w, b])[0]


def ada_bwd(c_all, dmods_all, dmods_mine, gain_rows, *, name, comms=()):
    nb, D = c_all.shape
    NA = dmods_all.shape[1]
    N = dmods_mine.shape[1]
    tn = N // 3

    def body(c_ref, da_ref, dm_ref, gr_ref, db_ref, dw_ref, dgn_ref):
        c = c_ref[...]
        cond = (c * jax.nn.sigmoid(c)).astype(BF16)
        dw_ref[...] = _dg(cond, dm_ref[...].astype(BF16), TN)
        db_ref[...] = jnp.sum(da_ref[...], axis=0, keepdims=True)
        total = gr_ref[0:1, :]
        for d in range(1, N_DEV):
            total = total + gr_ref[d:d + 1, :]
        dgn_ref[...] = total

    return _call(
        body, name=name, grid=(3,),
        in_specs=[pl.BlockSpec((nb, D), lambda j: (0, 0)), pl.BlockSpec((nb, NA), lambda j: (0, 0)),
                  pl.BlockSpec((nb, tn), lambda j: (0, j)), pl.BlockSpec((N_DEV, D), lambda j: (0, 0))],
        out_specs=[pl.BlockSpec((1, NA), lambda j: (0, 0)), pl.BlockSpec((D, tn), lambda j: (0, j)),
                   pl.BlockSpec((1, D), lambda j: (0, 0))],
        out_shape=[jax.ShapeDtypeStruct((1, NA), F32), jax.ShapeDtypeStruct((D, N), F32),
                   jax.ShapeDtypeStruct((1, D), F32)],
        args=[c_all, dmods_all, dmods_mine, gain_rows], comms=comms)


def adamw(items, *, steps, name, comms=()):
    n = len(items)
    c1 = 1.0 / (1.0 - ADAM_B1 ** ADAM_STEP)
    c2 = 1.0 / (1.0 - ADAM_B2 ** ADAM_STEP)

    def body(*refs):
        for j in range(n):
            w_ref, g_ref, m_ref, v_ref = refs[4 * j:4 * j + 4]
            d_ref, mo_ref, vo_ref = refs[4 * n + 3 * j:4 * n + 3 * j + 3]
            gg = g_ref[...]
            mn = ADAM_B1 * m_ref[...] + (1.0 - ADAM_B1) * gg
            vn = ADAM_B2 * v_ref[...] + (1.0 - ADAM_B2) * (gg * gg)
            mo_ref[...] = mn
            vo_ref[...] = vn
            d_ref[...] = -ADAM_LR * ((mn * c1) / (jnp.sqrt(vn * c2) + ADAM_EPS) + ADAM_WD * w_ref[...])

    in_specs, out_specs, out_shape, args = [], [], [], []
    for item in items:
        R, C = item[0].shape
        blk = pl.BlockSpec((R // steps, C), lambda i: (i, 0))
        in_specs += [blk] * 4
        out_specs += [blk] * 3
        out_shape += [jax.ShapeDtypeStruct((R, C), F32)] * 3
        args += list(item)
    res = _call(body, name=name, grid=(steps,), in_specs=in_specs, out_specs=out_specs, out_shape=out_shape, args=args,
                comms=comms)
    return [tuple(res[3 * j:3 * j + 3]) for j in range(n)]


def _place():
    return lax.axis_index("x"), lax.axis_index("y"), lax.axis_index("c")


def _flip(v, bit):
    return 1 - v if bit else v


def _other_chips(mx, my):
    return [(_flip(mx, k & 2), _flip(my, k & 1)) for k in range(1, N_QUAD)]


def _remote(src, dst, send_sem, recv_sem, peer):
    return pltpu.make_async_remote_copy(src_ref=src, dst_ref=dst, send_sem=send_sem, recv_sem=recv_sem,
                                        device_id=peer, device_id_type=MESH)


def ag8_comm(x):
    def copies(srcs, lands, ss, rs, only_sends=False):
        mx, my, mc = _place()
        me = 4 * mx + 2 * my + mc
        local = pltpu.make_async_copy(srcs[0], lands[0].at[me], ss.at[N_DEV - 1])
        sends, recvs = [], []
        for k in range(1, N_DEV):
            peer = (_flip(mx, k & 4), _flip(my, k & 2), _flip(mc, k & 1))
            sends.append(_remote(srcs[0], lands[0].at[me], ss.at[k - 1], rs.at[k - 1], peer))
            if not only_sends:
                recvs.append(_remote(srcs[0], lands[0].at[4 * peer[0] + 2 * peer[1] + peer[2]], ss.at[k - 1], rs.at[k - 1], peer))
        return local, sends, recvs

    def start(srcs, bufs, lands, ss, rs):
        local, sends, _ = copies(srcs, lands, ss, rs, only_sends=True)
        local.start()
        for cp in sends:
            cp.start()

    def finish(srcs, bufs, lands, ss, rs):
        local, sends, recvs = copies(srcs, lands, ss, rs)
        for cp in recvs:
            cp.wait_recv()
        for cp in sends:
            cp.wait_send()
        local.wait()

    return Comm(srcs=[x], land_shapes=[jax.ShapeDtypeStruct((N_DEV,) + x.shape, x.dtype)], n_sems=N_DEV,
                start=start, finish=finish)


def sum8(stacked, *, name):
    _, r, n = stacked.shape

    def body(s_ref, o_ref):
        total = s_ref[0]
        for d in range(1, N_DEV):
            total = total + s_ref[d]
        o_ref[...] = total

    return _call(body, name=name, grid=(), in_specs=[pl.BlockSpec(memory_space=pltpu.VMEM)],
                 out_specs=[pl.BlockSpec(memory_space=pltpu.VMEM)], out_shape=[jax.ShapeDtypeStruct((r, n), F32)],
                 args=[stacked])[0]


def cast_into_stacks(ws, quad, *, name, comms=()):
    steps = 4

    def body(q_ref, *refs):
        for w_ref, o_ref in zip(refs[:len(ws)], refs[len(ws):]):
            o_ref[0] = w_ref[...].astype(BF16)

    return _call(
        body, name=name, grid=(steps,), prefetch=[quad],
        in_specs=[pl.BlockSpec((w.shape[0] // steps, w.shape[1]), lambda i, q: (i, 0)) for w in ws],
        out_specs=[pl.BlockSpec((1, w.shape[0] // steps, w.shape[1]), lambda i, q: (q[0], i, 0)) for w in ws],
        out_shape=[jax.ShapeDtypeStruct((N_QUAD,) + w.shape, BF16) for w in ws], args=list(ws), comms=comms)


def gather_comm(stacks):
    n = len(stacks)

    def copies(bufs, ss, rs, only_sends=False):
        mx, my, mc = _place()
        q = 2 * mx + my
        sibling = (mx, my, 1 - mc)
        ici_send, ici_recv, fwd_send, fwd_recv = [], [], [], []
        for p in range(n):
            hr = stacks[p].shape[1] // 2
            mine, other = pl.ds(mc * hr, hr), pl.ds((1 - mc) * hr, hr)
            for k, chip in enumerate(_other_chips(mx, my)):
                qk = 2 * chip[0] + chip[1]
                peer = (chip[0], chip[1], mc)
                own, landed, theirs = bufs[p].at[q, mine, :], bufs[p].at[qk, mine, :], bufs[p].at[qk, other, :]
                ici_send.append(_remote(own, own, ss.at[6 * p + k], rs.at[6 * p + k], peer))
                if only_sends:
                    continue
                ici_recv.append(_remote(own, landed, ss.at[6 * p + k], rs.at[6 * p + k], peer))
                fwd_send.append(_remote(landed, landed, ss.at[6 * p + 3 + k], rs.at[6 * p + 3 + k], sibling))
                fwd_recv.append(_remote(theirs, theirs, ss.at[6 * p + 3 + k], rs.at[6 * p + 3 + k], sibling))
        return ici_send, ici_recv, fwd_send, fwd_recv

    def start(srcs, bufs, lands, ss, rs):
        for cp in copies(bufs, ss, rs, only_sends=True)[0]:
            cp.start()

    def finish(srcs, bufs, lands, ss, rs):
        ici_send, ici_recv, fwd_send, fwd_recv = copies(bufs, ss, rs)
        for arrived, onward in zip(ici_recv, fwd_send):
            arrived.wait_recv()
            onward.start()
        for cp in fwd_recv:
            cp.wait_recv()
        for cp in ici_send + fwd_send:
            cp.wait_send()

    return Comm(bufs=stacks, n_sems=6 * n, start=start, finish=finish)


def rs_pair_comm(gs):
    n = len(gs)

    def copies(srcs, lands, ss, rs):
        mx, my, mc = _place()
        out = []
        for p in range(n):
            hr = gs[p].shape[1] // 2
            out.append(_remote(srcs[p].at[:, pl.ds((1 - mc) * hr, hr), :], lands[p], ss.at[p], rs.at[p], (mx, my, 1 - mc)))
        return out

    def start(srcs, bufs, lands, ss, rs):
        for cp in copies(srcs, lands, ss, rs):
            cp.start()

    def finish(srcs, bufs, lands, ss, rs):
        for cp in copies(srcs, lands, ss, rs):
            cp.wait()

    return Comm(srcs=gs, land_shapes=[jax.ShapeDtypeStruct((g.shape[0], g.shape[1] // 2, g.shape[2]), g.dtype) for g in gs],
                n_sems=n, start=start, finish=finish)


def rs_chip_comm(ss_):
    n = len(ss_)

    def copies(srcs, lands, ss, rs):
        mx, my, mc = _place()
        out = []
        for p in range(n):
            for k, chip in enumerate(_other_chips(mx, my)):
                out.append(_remote(srcs[p].at[2 * chip[0] + chip[1]], lands[p].at[k], ss.at[3 * p + k], rs.at[3 * p + k],
                                   (chip[0], chip[1], mc)))
        return out

    def start(srcs, bufs, lands, ss, rs):
        for cp in copies(srcs, lands, ss, rs):
            cp.start()

    def finish(srcs, bufs, lands, ss, rs):
        for cp in copies(srcs, lands, ss, rs):
            cp.wait()

    return Comm(srcs=ss_, land_shapes=[jax.ShapeDtypeStruct((N_QUAD - 1,) + s.shape[1:], s.dtype) for s in ss_],
                n_sems=3 * n, start=start, finish=finish)


def rs_share_comm(fulls):
    n = len(fulls)

    def copies(bufs, ss, rs, only_sends=False):
        mx, my, mc = _place()
        send, recv = [], []
        for p in range(n):
            hr = fulls[p].shape[0] // 2
            mine, other = bufs[p].at[pl.ds(mc * hr, hr), :], bufs[p].at[pl.ds((1 - mc) * hr, hr), :]
            send.append(_remote(mine, mine, ss.at[p], rs.at[p], (mx, my, 1 - mc)))
            if not only_sends:
                recv.append(_remote(other, other, ss.at[p], rs.at[p], (mx, my, 1 - mc)))
        return send, recv

    def start(srcs, bufs, lands, ss, rs):
        for cp in copies(bufs, ss, rs, only_sends=True)[0]:
            cp.start()

    def finish(srcs, bufs, lands, ss, rs):
        send, recv = copies(bufs, ss, rs)
        for cp in recv:
            cp.wait_recv()
        for cp in send:
            cp.wait_send()

    return Comm(bufs=fulls, n_sems=n, start=start, finish=finish)


def pair_sums(gs, recvs, mc, *, name):
    n = len(gs)

    def body(s_ref, *refs):
        for p in range(n):
            g_ref, r_ref, o_ref = refs[2 * p], refs[2 * p + 1], refs[2 * n + p]
            o_ref[...] = (g_ref[...].astype(F32) + r_ref[...].astype(F32)).astype(BF16)

    in_specs, out_specs, out_shape, args = [], [], [], []
    for g, r in zip(gs, recvs):
        nq, hr, C = r.shape
        in_specs += [pl.BlockSpec((1, hr, C), lambda q, s: (q, s[0], 0)), pl.BlockSpec((1, hr, C), lambda q, s: (q, 0, 0))]
        out_specs.append(pl.BlockSpec((1, hr, C), lambda q, s: (q, 0, 0)))
        out_shape.append(jax.ShapeDtypeStruct((nq, hr, C), BF16))
        args += [g, r]
    return _call(body, name=name, grid=(N_QUAD,), in_specs=in_specs, out_specs=out_specs, out_shape=out_shape,
                 args=args, prefetch=[mc])


def chip_sums(ss, recvs, quad_mc, *, name):
    n = len(ss)
    steps = 2

    def body(q_ref, *refs):
        for p in range(n):
            s_ref, r_ref, o_ref = refs[2 * p], refs[2 * p + 1], refs[2 * n + p]
            total = s_ref[0].astype(F32)
            for k in range(N_QUAD - 1):
                total = total + r_ref[k].astype(F32)
            o_ref[...] = total

    in_specs, out_specs, out_shape, args = [], [], [], []
    for s, r in zip(ss, recvs):
        _, hr, C = s.shape
        rb = hr // steps
        in_specs += [pl.BlockSpec((1, rb, C), lambda i, q: (q[0], i, 0)),
                     pl.BlockSpec((N_QUAD - 1, rb, C), lambda i, q: (0, i, 0))]
        out_specs.append(pl.BlockSpec((rb, C), lambda i, q: (q[1] * steps + i, 0)))
        out_shape.append(jax.ShapeDtypeStruct((2 * hr, C), F32))
        args += [s, r]
    return _call(body, name=name, grid=(steps,), in_specs=in_specs, out_specs=out_specs, out_shape=out_shape,
                 args=args, prefetch=[quad_mc])


def _stack_cols(w_full):
    K, N = w_full.shape
    return w_full.reshape(K, N_QUAD, N // N_QUAD).transpose(1, 0, 2)


def _unstack_cols(w4):
    nq, K, n = w4.shape
    return w4.transpose(1, 0, 2).reshape(K, nq * n)


def kernel(x, c, w_ada, b_ada, g_norm1, ffn1_w_gate, ffn1_w_up, ffn1_w_down, g_norm2, w_in, g_sgu_ln, b_sgu_ln, w_spatial, b_spatial, g_q, g_k, attn_sinks, w_branch_a, w_branch_b, w_out, g_norm3, ffn2_w_gate, ffn2_w_up, ffn2_w_down, loss_target, m_w_ada, m_b_ada, m_g_norm1, m_ffn1_w_gate, m_ffn1_w_up, m_ffn1_w_down, m_g_norm2, m_w_in, m_g_sgu_ln, m_b_sgu_ln, m_w_spatial, m_b_spatial, m_g_q, m_g_k, m_attn_sinks, m_w_branch_a, m_w_branch_b, m_w_out, m_g_norm3, m_ffn2_w_gate, m_ffn2_w_up, m_ffn2_w_down, v_w_ada, v_b_ada, v_g_norm1, v_ffn1_w_gate, v_ffn1_w_up, v_ffn1_w_down, v_g_norm2, v_w_in, v_g_sgu_ln, v_b_sgu_ln, v_w_spatial, v_b_spatial, v_g_q, v_g_k, v_attn_sinks, v_w_branch_a, v_w_branch_b, v_w_out, v_g_norm3, v_ffn2_w_gate, v_ffn2_w_up, v_ffn2_w_down):
    B, S, D = x.shape
    T = B * S
    n_mod = b_ada.shape[1] // D
    mx, my, mc = _place()
    quad = 2 * mx + my
    mc_arr = jnp.reshape(mc, (1,)).astype(jnp.int32)
    quad_arr = jnp.reshape(quad, (1,)).astype(jnp.int32)
    quad_mc = jnp.stack([quad, mc]).astype(jnp.int32)
    tm = min(512, S)
    tm_big = min(1024, S)
    tt_half = min(2048, T)
    cpb = min(4, S // CHUNK)

    xt = x.reshape(T, D)
    tgt = loss_target.reshape(T, D)

    tr = lambda a: jnp.swapaxes(a, 1, 2)
    stack_wg1, stack_wu1 = cast_into_stacks([tr(ffn1_w_gate)[0], tr(ffn1_w_up)[0]], quad_arr, name="stack_gu1")
    gather_wg1, gather_wu1 = gather_comm([stack_wg1]), gather_comm([stack_wu1])
    later = [ffn1_w_down, tr(w_in), w_branch_a, w_branch_b, w_out, tr(ffn2_w_gate), tr(ffn2_w_up), ffn2_w_down]
    gather_cond = ag8_comm(c)
    stacks = cast_into_stacks([w[0] for w in later[0:4]], quad_arr, name="stack_a", comms=[gather_wg1, gather_cond])
    (wg1,) = gather_wg1.bufs_out
    c_all = gather_cond.lands[0].reshape(N_DEV * B, D)
    n_ada = w_ada.shape[2]
    b_mine = lax.dynamic_slice(b_ada, (0, quad * n_ada), (1, n_ada))
    mods_part = ada_fwd(c_all, w_ada[0], b_mine, name="ada_fwd")
    gather_mods = ag8_comm(mods_part)
    stacks += cast_into_stacks([w[0] for w in later[4:8]], quad_arr, name="stack_b", comms=[gather_wu1, gather_mods])
    (wu1,), (mods_parts,) = gather_wu1.bufs_out, gather_mods.lands
    gather_wd1, gather_win = gather_comm([stacks[0]]), gather_comm([stacks[1]])
    gather_abo = gather_comm(stacks[2:5])
    gather_wg3, gather_wu3, gather_wd3 = gather_comm([stacks[5]]), gather_comm([stacks[6]]), gather_comm([stacks[7]])
    mods = jnp.concatenate([mods_parts[2 * j] for j in range(N_QUAD)], axis=1)
    me = 4 * mx + 2 * my + mc
    mods = lax.dynamic_slice(mods, (me * B, 0), (B, n_mod * D))
    mods = mods.reshape(B, n_mod, 1, D)
    sh1, sc1, ga1, sh2, sc2, ga2, sh3, sc3, ga3 = [(mods, j) for j in range(n_mod)]
    bs_t = b_spatial[0].T
    ws = w_spatial[0]

    xn1 = norm_mod_fwd(xt, g_norm1, sc1, sh1, seq=S, tm=tm, name="norm1")
    g1, u1, a1 = ffn_up(xn1, wg1, wu1, tm=tm_big, name="ffn1_up", comms=[gather_wd1, gather_abo])
    (wd1,), (wa4, wb4, wo4) = gather_wd1.bufs_out, gather_abo.bufs_out
    wa, wb = _unstack_cols(wa4), _unstack_cols(wb4)
    wo = wo4.reshape(D, D)
    h1, f1, xn2 = ffn_down(a1, wd1, xt, ga1, norm=(g_norm2, sc2, sh2), seq=S, tm=tm, name="ffn1_down",
                           comms=[gather_win])
    (win4,) = gather_win.bufs_out
    win = win4.reshape(-1, D)
    w_uv, w_qkv, w_gt = win[0:2 * D_A], win[2 * D_A:2 * D_A + QKV_W], win[2 * D_A + QKV_W:]
    puv, pqkv, pgt = proj_fwd(xn2, [w_uv, w_qkv, w_gt], tm=tm, name="proj", comms=[gather_wg3])
    (wg3,) = gather_wg3.bufs_out
    sgu = sgu_fwd(puv, g_sgu_ln, b_sgu_ln, ws, bs_t, cpb=cpb, name="sgu_fwd")
    gq_t, gk_t = jnp.tile(g_q, (1, N_Q)), jnp.tile(g_k, (1, N_KV))
    att = attn_fwd(pqkv, gq_t, gk_t, attn_sinks, seq=S, name="attn_fwd", comms=[gather_wu3])
    (wu3,) = gather_wu3.bufs_out
    ya, yb, merged, mm, h2, xn3 = mixer_out_fwd(sgu, att, pgt, h1, ga2, wa, wb, wo, (g_norm3, sc3, sh3), seq=S,
                                                tm=tm, name="mixer_out", comms=[gather_wd3])
    (wd3,) = gather_wd3.bufs_out
    g3, u3, a3 = ffn_up(xn3, wg3, wu3, tm=tm_big, name="ffn2_up")
    dy, f3, lparts = ffn_down(a3, wd3, h2, ga3, target=tgt, seq=S, tm=tm, name="ffn2_down_loss")
    loss_part = jnp.sum(lparts[:, 0, 0])

    def sums_of(pair, tag):
        return pair_sums(pair.srcs, pair.lands, mc_arr, name=f"pair_sum_{tag}")

    def halves_of(*chips_and_tag):
        *chips, tag = chips_and_tag
        return chip_sums([s for ch in chips for s in ch.srcs], [r for ch in chips for r in ch.lands], quad_mc,
                         name=f"chip_sum_{tag}")

    dg3, du3, df3, dga3 = ffn_bwd_act(dy, f3, ga3, wd3, g3, u3, seq=S, tm=tm, name="ffn2_act_bwd")
    (dwd3,) = mm_tn([(a3, df3)], tt=T, name="ffn2_dwd")
    pair_wd3 = rs_pair_comm([dwd3])
    dwg3, dwu3 = mm_tn([(dg3, xn3), (du3, xn3)], tt=tt_half, name="ffn2_dwgu", comms=[pair_wd3])
    chip_wd3 = rs_chip_comm(sums_of(pair_wd3, "wd3"))
    pair_gu3 = rs_pair_comm([dwg3, dwu3])
    dh2, dsh3, dsc3, dgn3 = dx_norm_bwd([dg3, du3], [wg3, wu3], h2, dy, g_norm3, sc3, seq=S, tm=tm, name="ffn2_dx",
                                        comms=[chip_wd3, pair_gu3])
    sum_wg3, sum_wu3 = sums_of(pair_gu3, "gu3")
    chip_wg3, chip_wu3 = rs_chip_comm([sum_wg3]), rs_chip_comm([sum_wu3])

    dgt, dya, dyb, dsgu, datt, dm, dga2 = mixer_out_bwd(dh2, mm, ga2, ya, yb, pgt, wa, wb, wo, seq=S, tm=tm,
                                                        name="mixer_out_bwd", comms=[chip_wg3])
    dwo, dwa, dwb = mm_tn([(merged, dm), (sgu, dya), (att, dyb)], tt=tm_big, name="mixer_dw")
    pair_abo = rs_pair_comm([_stack_cols(dwa), _stack_cols(dwb), dwo.reshape(N_QUAD, D // N_QUAD, D)])
    duv, dws, dzs, dgln, dbln = sgu_bwd(puv, dsgu, g_sgu_ln, b_sgu_ln, ws, bs_t, cpb=cpb, name="sgu_bwd",
                                        comms=[pair_abo])
    chip_abo = rs_chip_comm(sums_of(pair_abo, "abo"))
    dqkv, dgq_h, dgk_h, dsk = attn_bwd(pqkv, datt, gq_t, gk_t, attn_sinks, seq=S, name="attn_bwd",
                                       comms=[chip_wu3, chip_abo])
    dgq = dgq_h.reshape(N_Q, HEAD_DIM).sum(axis=0, keepdims=True)
    dgk = dgk_h.reshape(N_KV, HEAD_DIM).sum(axis=0, keepdims=True)
    share3 = rs_share_comm(halves_of(chip_wg3, chip_wu3, chip_wd3, "ffn2"))
    dwin_uv, dwin_qkv = mm_tn([(duv, xn2), (dqkv, xn2)], tt=tt_half, name="mixer_dwin_a", comms=[share3])
    (dwin_gt,) = mm_tn([(dgt, xn2)], tt=tt_half, name="mixer_dwin_b")
    dwin_parts = [dwin_uv, dwin_qkv, dwin_gt]
    r_wg3, r_wu3, r_wd3 = share3.bufs_out
    pair_win = rs_pair_comm([jnp.concatenate(dwin_parts, axis=0).reshape(win4.shape)])
    dh1, dsh2, dsc2, dgn2 = dx_norm_bwd([duv, dqkv, dgt], [w_uv, w_qkv, w_gt], h1, dh2, g_norm2, sc2, seq=S,
                                        tm=tm, name="mixer_dx", comms=[pair_win])
    chip_win = rs_chip_comm(sums_of(pair_win, "win"))

    dg1, du1, df1, dga1 = ffn_bwd_act(dh1, f1, ga1, wd1, g1, u1, seq=S, tm=tm, name="ffn1_act_bwd", comms=[chip_win])
    share_mix = rs_share_comm(halves_of(chip_win, chip_abo, "mix"))

    db_s = dzs.reshape(CHUNK, N_GROUPS, D_A // N_GROUPS).sum(axis=2).T.reshape(1, N_GROUPS * CHUNK)
    tail = jnp.concatenate([db_s, dgq, dgk, dsk[0:1, 0:N_Q], loss_part.reshape(1, 1)], axis=1)
    tail = jnp.pad(tail, ((0, 0), (0, (-tail.shape[1]) % D)))
    lnrow = jnp.concatenate([dgln, dbln], axis=1)
    lnrow = jnp.pad(lnrow, ((0, 0), (0, (-lnrow.shape[1]) % D)))
    packed = jnp.concatenate([dgn2, dgn3, lnrow.reshape(-1, D), tail.reshape(-1, D), dws.reshape(-1, D)], axis=0)
    n_rows = packed.shape[0]
    packed = jnp.pad(packed, ((0, (-n_rows) % 8), (0, 0)))
    gather_small = ag8_comm(packed)

    dwg1, dwu1 = mm_tn([(dg1, xn1), (du1, xn1)], tt=tt_half, name="ffn1_dwgu", comms=[share_mix, gather_small])
    r_win, r_wa, r_wb, r_wo = share_mix.bufs_out
    small = sum8(gather_small.lands[0], name="sum_small")
    pair_gu1 = rs_pair_comm([dwg1, dwu1])
    (dwd1,) = mm_tn([(a1, df1)], tt=T, name="ffn1_dwd", comms=[pair_gu1])
    chip_gu1 = rs_chip_comm(sums_of(pair_gu1, "gu1"))
    pair_wd1 = rs_pair_comm([dwd1])
    dx, dsh1, dsc1, dgn1 = dx_norm_bwd([dg1, du1], [wg1, wu1], xt, dh1, g_norm1, sc1, seq=S, tm=tm, name="ffn1_dx",
                                       comms=[chip_gu1, pair_wd1])
    chip_wd1 = rs_chip_comm(sums_of(pair_wd1, "wd1"))
    share_gu1 = rs_share_comm(halves_of(chip_gu1, "gu1"))

    names = ["w_ada", "b_ada", "g_norm1", "ffn1_w_gate", "ffn1_w_up", "ffn1_w_down", "g_norm2", "w_in", "g_sgu_ln",
             "b_sgu_ln", "w_spatial", "b_spatial", "g_q", "g_k", "attn_sinks", "w_branch_a", "w_branch_b", "w_out",
             "g_norm3", "ffn2_w_gate", "ffn2_w_up", "ffn2_w_down"]
    weights = dict(zip(names, [w_ada, b_ada, g_norm1, ffn1_w_gate, ffn1_w_up, ffn1_w_down, g_norm2, w_in, g_sgu_ln,
                               b_sgu_ln, w_spatial, b_spatial, g_q, g_k, attn_sinks, w_branch_a, w_branch_b, w_out,
                               g_norm3, ffn2_w_gate, ffn2_w_up, ffn2_w_down]))
    m_in = dict(zip(names, [m_w_ada, m_b_ada, m_g_norm1, m_ffn1_w_gate, m_ffn1_w_up, m_ffn1_w_down, m_g_norm2, m_w_in,
                            m_g_sgu_ln, m_b_sgu_ln, m_w_spatial, m_b_spatial, m_g_q, m_g_k, m_attn_sinks, m_w_branch_a,
                            m_w_branch_b, m_w_out, m_g_norm3, m_ffn2_w_gate, m_ffn2_w_up, m_ffn2_w_down]))
    v_in = dict(zip(names, [v_w_ada, v_b_ada, v_g_norm1, v_ffn1_w_gate, v_ffn1_w_up, v_ffn1_w_down, v_g_norm2, v_w_in,
                            v_g_sgu_ln, v_b_sgu_ln, v_w_spatial, v_b_spatial, v_g_q, v_g_k, v_attn_sinks, v_w_branch_a,
                            v_w_branch_b, v_w_out, v_g_norm3, v_ffn2_w_gate, v_ffn2_w_up, v_ffn2_w_down]))
    transposed = ("ffn1_w_gate", "ffn1_w_up", "w_in", "ffn2_w_gate", "ffn2_w_up")
    grads, delta, new_m, new_v = {}, {}, {}, {}

    def update(group, steps, name, comms=()):
        form = lambda nm, a: (tr(a) if nm in transposed else a)[0].reshape(group[nm].shape)
        res = adamw([(form(nm, weights[nm]), g, form(nm, m_in[nm]), form(nm, v_in[nm])) for nm, g in group.items()],
                    steps=steps, name=name, comms=comms)
        back = lambda nm, a: tr(a[None]) if nm in transposed else a.reshape(weights[nm].shape)
        for (nm, g), (d, mn, vn) in zip(group.items(), res):
            grads[nm], delta[nm], new_m[nm], new_v[nm] = back(nm, g), back(nm, d), back(nm, mn), back(nm, vn)

    dmods = jnp.concatenate([dsh1, dsc1, dga1, dsh2, dsc2, dga2, dsh3, dsc3, dga3], axis=1)
    dmods = jnp.concatenate([dmods, jnp.pad(dgn1, ((0, 0), (0, (n_mod - 1) * D)))], axis=0)
    gather_dmods = ag8_comm(dmods)
    update({"ffn2_w_gate": r_wg3, "ffn2_w_up": r_wu3, "ffn2_w_down": r_wd3, "w_out": r_wo, "w_branch_a": r_wa,
            "w_branch_b": r_wb}, 8, "adamw_early", comms=[chip_wd1, share_gu1, gather_dmods])
    r_wg1, r_wu1 = share_gu1.bufs_out
    (gathered,) = gather_dmods.lands
    dmods_all = gathered[:, 0:B].reshape(N_DEV * B, n_mod * D)
    dmods_mine = lax.dynamic_slice(dmods_all, (0, quad * n_ada), (N_DEV * B, n_ada))
    share_wd1 = rs_share_comm(halves_of(chip_wd1, "wd1"))
    db_ada, dw_ada, g_gn1 = ada_bwd(c_all, dmods_all, dmods_mine, gathered[:, B, 0:D], name="ada_bwd", comms=[share_wd1])
    (r_wd1,) = share_wd1.bufs_out

    ln_rows = lnrow.size // D
    tail_rows = tail.size // D
    r = 2
    g_gn2, g_gn3 = small[0:1], small[1:2]
    ln_flat = small[r:r + ln_rows].reshape(1, -1)
    r += ln_rows
    tail_flat = small[r:r + tail_rows].reshape(1, -1)
    r += tail_rows
    g_ws = small[r:r + dws.size // D].reshape(w_spatial.shape)
    g_gln, g_bln = ln_flat[:, 0:D_A], ln_flat[:, D_A:2 * D_A]
    o = N_GROUPS * CHUNK
    g_bs = tail_flat[:, 0:o].reshape(b_spatial.shape)
    g_gq, g_gk, g_sk = tail_flat[:, o:o + HEAD_DIM], tail_flat[:, o + HEAD_DIM:o + 2 * HEAD_DIM], tail_flat[:, o + 2 * HEAD_DIM:o + 2 * HEAD_DIM + N_Q]
    loss = tail_flat[0, o + 2 * HEAD_DIM + N_Q]

    update({"w_ada": dw_ada}, 16, "adamw_w_ada")
    update({"ffn1_w_gate": r_wg1, "ffn1_w_up": r_wu1, "ffn1_w_down": r_wd1, "w_in": r_win}, 8, "adamw_late")

    update({"b_ada": db_ada, "g_norm1": g_gn1, "g_norm2": g_gn2, "g_norm3": g_gn3, "g_sgu_ln": g_gln,
            "b_sgu_ln": g_bln, "w_spatial": g_ws.reshape(-1, CHUNK), "b_spatial": g_bs.reshape(N_GROUPS, CHUNK),
            "g_q": g_gq, "g_k": g_gk, "attn_sinks": g_sk}, 1, "adamw_small")

    return (loss, dx.reshape(B, S, D), *[grads[nm] for nm in names], *[delta[nm] for nm in names],
            *[new_m[nm] for nm in names], *[new_v[nm] for nm in names])
```

```python
import functools
import math
import operator

import jax
import jax.numpy as jnp
from jax import lax
from jax.experimental import pallas as pl
from jax.experimental.pallas import tpu as pltpu

F32 = jnp.float32
BF16 = jnp.bfloat16
EPS = 1e-6
NEG = -1e30
N_DEV = 8
N_QUAD = 4
D_A = 512
N_GROUPS = 4
CHUNK = 128
HEAD_DIM = 64
N_KV = 2
Q_PER_KV = 4
N_Q = N_KV * Q_PER_KV
D_B = N_Q * HEAD_DIM
QKV_W = D_B + 2 * N_KV * HEAD_DIM
K_OFF = D_B
V_OFF = D_B + N_KV * HEAD_DIM
ATT_SCALE = HEAD_DIM ** -0.5
GELU_K = math.sqrt(2.0 / math.pi)
GELU_C = 0.044715
ADAM_LR, ADAM_B1, ADAM_B2, ADAM_EPS, ADAM_WD, ADAM_STEP = 0.001, 0.9, 0.999, 1e-08, 0.01, 10
VMEM_LIMIT_BYTES = 56 * 1024 * 1024
MESH = pl.DeviceIdType.MESH
NT = (((1,), (1,)), ((), ()))
TN = (((0,), (0,)), ((), ()))
ANY = pl.BlockSpec(memory_space=pl.ANY)


def _dot(a, b):
    return jnp.dot(a, b, preferred_element_type=F32)


def _dg(a, b, dims):
    return lax.dot_general(a, b, dims, preferred_element_type=F32)


def _gelu_parts(x):
    t = jnp.tanh(GELU_K * (x + GELU_C * x * x * x))
    return 0.5 * x * (1.0 + t), t


def _dgelu(x, t):
    return 0.5 * (1.0 + t) + 0.5 * x * (1.0 - t * t) * (GELU_K * (1.0 + 3.0 * GELU_C * x * x))


def _row_block(rows, target):
    b = min(rows, target)
    while rows % b or b % 8:
        b -= 1
    return b


def _mod_spec(mod, tps):
    mods, j = mod
    return pl.BlockSpec((1, None, 1, mods.shape[3]), lambda i: (i // tps, j, 0, 0))


def _resident(a):
    return pl.BlockSpec(a.shape, lambda *_: (0,) * a.ndim, pipeline_mode=pl.Buffered(1))


class Comm:
    def __init__(self, *, srcs=(), bufs=(), land_shapes=(), n_sems, start, finish):
        self.srcs, self.bufs, self.land_shapes = list(srcs), list(bufs), list(land_shapes)
        self.n_sems, self.start, self.finish = n_sems, start, finish
        self.bufs_out, self.lands = None, None


def _call(body, *, name, grid, in_specs, out_specs, out_shape, args, scratch_shapes=(), comms=(), prefetch=()):
    prefetch = list(prefetch)
    in_specs, out_specs, out_shape = list(in_specs), list(out_specs), list(out_shape)
    args, scratch = list(args), list(scratch_shapes)
    n_in, n_out, n_scr = len(args), len(out_shape), len(scratch)
    aliases, layout = {}, []
    for cm in comms:
        i0, o0, s0 = len(args), len(out_shape), len(scratch)
        args += cm.srcs + cm.bufs
        in_specs += [ANY] * (len(cm.srcs) + len(cm.bufs))
        out_shape += [jax.ShapeDtypeStruct(b.shape, b.dtype) for b in cm.bufs] + cm.land_shapes
        out_specs += [ANY] * (len(cm.bufs) + len(cm.land_shapes))
        for j in range(len(cm.bufs)):
            aliases[len(prefetch) + i0 + len(cm.srcs) + j] = o0 + j
        scratch += [pltpu.SemaphoreType.DMA((cm.n_sems,)), pltpu.SemaphoreType.DMA((cm.n_sems,))]
        layout.append((i0, o0, s0))
    n_in_all, n_out_all = len(args), len(out_shape)

    def wrapped(*refs):
        scalars, refs = refs[:len(prefetch)], refs[len(prefetch):]
        ins, outs, scr = refs[:n_in_all], refs[n_in_all:n_in_all + n_out_all], refs[n_in_all + n_out_all:]
        parts = []
        for cm, (i0, o0, s0) in zip(comms, layout):
            nb = len(cm.bufs)
            parts.append((ins[i0:i0 + len(cm.srcs)], outs[o0:o0 + nb], outs[o0 + nb:o0 + nb + len(cm.land_shapes)],
                          scr[s0], scr[s0 + 1]))
        if comms and grid:
            ids = [pl.program_id(d) for d in range(len(grid))]
            first = functools.reduce(operator.and_, [i == 0 for i in ids])
            last = functools.reduce(operator.and_, [i == g - 1 for i, g in zip(ids, grid)])

            @pl.when(first)
            def _():
                for cm, p in zip(comms, parts):
                    cm.start(*p)
        elif comms:
            for cm, p in zip(comms, parts):
                cm.start(*p)
        if body is not None:
            body(*scalars, *ins[:n_in], *outs[:n_out], *scr[:n_scr])
        if comms and grid:
            @pl.when(last)
            def _():
                for cm, p in zip(comms, parts):
                    cm.finish(*p)
        elif comms:
            for cm, p in zip(comms, parts):
                cm.finish(*p)

    if prefetch:
        kwargs = dict(grid_spec=pltpu.PrefetchScalarGridSpec(
            num_scalar_prefetch=len(prefetch), grid=grid, in_specs=in_specs, out_specs=out_specs, scratch_shapes=scratch))
    else:
        kwargs = dict(in_specs=in_specs, out_specs=out_specs, scratch_shapes=scratch, **(dict(grid=grid) if grid else {}))
    sem = ("arbitrary",) * len(grid)
    res = pl.pallas_call(
        wrapped, name=name, out_shape=out_shape, input_output_aliases=aliases,
        compiler_params=pltpu.CompilerParams(dimension_semantics=sem, vmem_limit_bytes=VMEM_LIMIT_BYTES), **kwargs,
    )(*prefetch, *args)
    for cm, (i0, o0, s0) in zip(comms, layout):
        nb = len(cm.bufs)
        cm.bufs_out = list(res[o0:o0 + nb])
        cm.lands = list(res[o0 + nb:o0 + nb + len(cm.land_shapes)])
    return list(res[:n_out])


def run_comms(comms, *, name):
    _call(None, name=name, grid=(), in_specs=[], out_specs=[], out_shape=[], args=[], comms=comms)


def norm_mod_fwd(h, g, sc, sh, *, seq, tm, name, comms=()):
    T, D = h.shape
    B, tps = T // seq, seq // tm

    def body(h_ref, g_ref, sc_ref, sh_ref, o_ref):
        x = h_ref[...]
        r = lax.rsqrt(jnp.mean(x * x, axis=-1, keepdims=True) + EPS)
        y = x * r * g_ref[...]
        o_ref[...] = (y * (1.0 + sc_ref[0]) + sh_ref[0]).astype(o_ref.dtype)

    return _call(
        body, name=name, grid=(T // tm,),
        in_specs=[pl.BlockSpec((tm, D), lambda i: (i, 0)), pl.BlockSpec((1, D), lambda i: (0, 0)),
                  _mod_spec(sc, tps), _mod_spec(sh, tps)],
        out_specs=[pl.BlockSpec((tm, D), lambda i: (i, 0))], out_shape=[jax.ShapeDtypeStruct((T, D), BF16)],
        args=[h, g, sc[0], sh[0]], comms=comms)[0]


def ffn_up(xn, wg, wu, *, tm, name, comms=()):
    T, D = xn.shape
    nq, fs, _ = wg.shape

    def body(x_ref, wg_ref, wu_ref, s_ref, q_ref, a_ref):
        x = x_ref[...]
        g = _dg(x, wg_ref[0], NT)
        u = _dg(x, wu_ref[0], NT)
        sg = jax.nn.sigmoid(g)
        s = g * sg
        s_ref[0] = s.astype(BF16)
        q_ref[0] = (u * (sg + s * (1.0 - sg))).astype(BF16)
        a_ref[0] = (s * u).astype(BF16)

    w_spec = pl.BlockSpec((1, fs, D), lambda q, i: (q, 0, 0))
    o_spec = pl.BlockSpec((1, tm, fs), lambda q, i: (q, i, 0))
    o_shape = jax.ShapeDtypeStruct((nq, T, fs), BF16)
    return _call(
        body, name=name, grid=(nq, T // tm),
        in_specs=[pl.BlockSpec((tm, D), lambda q, i: (i, 0)), w_spec, w_spec],
        out_specs=[o_spec, o_spec, o_spec], out_shape=[o_shape, o_shape, o_shape], args=[xn, wg, wu], comms=comms)


def _norm_mod(y, g_ref, sc_ref, sh_ref):
    nx, _ = _rms_parts(y)
    return ((nx * g_ref[...]) * (1.0 + sc_ref[0]) + sh_ref[0]).astype(BF16)


def ffn_down(a, wd, hin, ga, *, target=None, norm=None, seq, tm, name, comms=()):
    nq, T, fs = a.shape
    D = wd.shape[-1]
    B, tps, nt = T // seq, seq // tm, T // tm

    def body(a_ref, wd_ref, h_ref, ga_ref, *rest):
        rest = list(rest)
        t_ref = rest.pop(0) if target is not None else None
        norm_refs = [rest.pop(0) for _ in range(3)] if norm is not None else None
        o_ref, f_ref = rest[0], rest[1]
        f = _dot(a_ref[0], wd_ref[0])
        for q in range(1, nq):
            f = f + _dot(a_ref[q], wd_ref[q])
        f_ref[...] = f.astype(BF16)
        y = h_ref[...] + (0.5 * ga_ref[0]) * f
        if target is not None:
            e = y - t_ref[...]
            o_ref[...] = e * (1.0 / D)
            rest[2][...] = jnp.full(rest[2].shape, 0.5 * jnp.sum(e * e) * (1.0 / D), F32)
        else:
            o_ref[...] = y
        if norm is not None:
            rest[2][...] = _norm_mod(y, *norm_refs)

    tile = pl.BlockSpec((tm, D), lambda i: (i, 0))
    in_specs = [pl.BlockSpec((nq, tm, fs), lambda i: (0, i, 0)), _resident(wd), tile, _mod_spec(ga, tps)]
    out_specs = [tile, tile]
    out_shape = [jax.ShapeDtypeStruct((T, D), F32), jax.ShapeDtypeStruct((T, D), BF16)]
    args = [a, wd, hin, ga[0]]
    if target is not None:
        in_specs.append(tile)
        args.append(target)
        out_specs.append(pl.BlockSpec((1, 8, 128), lambda i: (i, 0, 0)))
        out_shape.append(jax.ShapeDtypeStruct((nt, 8, 128), F32))
    if norm is not None:
        in_specs += [pl.BlockSpec((1, D), lambda i: (0, 0)), _mod_spec(norm[1], tps), _mod_spec(norm[2], tps)]
        args += [norm[0], norm[1][0], norm[2][0]]
        out_specs.append(tile)
        out_shape.append(jax.ShapeDtypeStruct((T, D), BF16))
    return _call(body, name=name, grid=(nt,), in_specs=in_specs, out_specs=out_specs, out_shape=out_shape, args=args,
                 comms=comms)


def proj_fwd(xn, w, groups, *, tm, name, comms=()):
    T, D = xn.shape

    def body(x_ref, w_ref, *o_refs):
        x = x_ref[...]
        for (r0, r1), o_ref in zip(groups, o_refs):
            o_ref[...] = _dg(x, w_ref[r0:r1, :], NT).astype(BF16)

    return _call(
        body, name=name, grid=(T // tm,), in_specs=[pl.BlockSpec((tm, D), lambda i: (i, 0)), _resident(w)],
        out_specs=[pl.BlockSpec((tm, r1 - r0), lambda i: (i, 0)) for r0, r1 in groups],
        out_shape=[jax.ShapeDtypeStruct((T, r1 - r0), BF16) for r0, r1 in groups], args=[xn, w], comms=comms)


def _layer_norm_parts(v):
    mu = jnp.mean(v, axis=-1, keepdims=True)
    d = v - mu
    rstd = lax.rsqrt(jnp.mean(d * d, axis=-1, keepdims=True) + EPS)
    return d * rstd, rstd


def _causal():
    row = lax.broadcasted_iota(jnp.int32, (CHUNK, CHUNK), 0)
    col = lax.broadcasted_iota(jnp.int32, (CHUNK, CHUNK), 1)
    return row >= col


def sgu_fwd(puv, gln, bln, ws, bs_t, *, cpb, name, comms=()):
    T = puv.shape[0]
    gd = D_A // N_GROUPS
    tm = cpb * CHUNK

    def body(p_ref, gln_ref, bln_ref, ws_ref, bs_ref, o_ref):
        causal = _causal()
        wm = [jnp.where(causal, ws_ref[g], 0.0).astype(BF16) for g in range(N_GROUPS)]
        for j in range(cpb):
            rows = slice(j * CHUNK, (j + 1) * CHUNK)
            u, _ = _gelu_parts(p_ref[rows, 0:D_A].astype(F32))
            vv, _ = _gelu_parts(p_ref[rows, D_A:2 * D_A].astype(F32))
            vhat, _ = _layer_norm_parts(vv)
            vn = (vhat * gln_ref[...] + bln_ref[...]).astype(BF16)
            for g in range(N_GROUPS):
                cols = slice(g * gd, (g + 1) * gd)
                z = _dot(wm[g], vn[:, cols]) + bs_ref[:, g:g + 1]
                o_ref[rows, cols] = (u[:, cols] * z).astype(BF16)

    full = lambda a: pl.BlockSpec(a.shape, lambda i: (0,) * a.ndim)
    return _call(
        body, name=name, grid=(T // tm,),
        in_specs=[pl.BlockSpec((tm, 2 * D_A), lambda i: (i, 0)), full(gln), full(bln), full(ws), full(bs_t)],
        out_specs=[pl.BlockSpec((tm, D_A), lambda i: (i, 0))], out_shape=[jax.ShapeDtypeStruct((T, D_A), BF16)],
        args=[puv, gln, bln, ws, bs_t], comms=comms)[0]


def _rms_parts(x):
    r = lax.rsqrt(jnp.mean(x * x, axis=-1, keepdims=True) + EPS)
    return x * r, r


KV_W = Q_PER_KV * HEAD_DIM
KV2 = N_KV * HEAD_DIM
STACK = Q_PER_KV * CHUNK


def _iota(shape, dim):
    return lax.broadcasted_iota(jnp.int32, shape, dim)


def _head_mean_matrix(n):
    return jnp.where((_iota((n, n), 0) >> 6) == (_iota((n, n), 1) >> 6), 1.0 / HEAD_DIM, 0.0).astype(BF16)


def _repeat_matrix(h):
    return jnp.where(_iota((KV2, KV_W), 0) == h * HEAD_DIM + (_iota((KV2, KV_W), 1) & (HEAD_DIM - 1)), 1.0, 0.0).astype(BF16)


def _fold_matrix(h):
    j, l = _iota((KV_W, KV2), 0), _iota((KV_W, KV2), 1)
    return jnp.where(((j & (HEAD_DIM - 1)) == (l & (HEAD_DIM - 1))) & ((l >> 6) == h), 1.0, 0.0).astype(BF16)


def _dot_split(x, m):
    hi = x.astype(BF16)
    lo = (x - hi.astype(F32)).astype(BF16)
    return _dot(hi, m) + _dot(lo, m)


def _head_rms(x, mean_matrix):
    r = lax.rsqrt(_dot_split(x * x, mean_matrix) + EPS)
    return x * r, r


def _stack_heads(x):
    head = _iota(x.shape, 1) >> 6
    return jnp.concatenate([jnp.where(head == g, x, 0.0).astype(BF16) for g in range(Q_PER_KV)], axis=0)


def _unstack_heads(y):
    head = _iota((CHUNK, KV_W), 1) >> 6
    out = jnp.where(head == 0, y[0:CHUNK], 0.0)
    for g in range(1, Q_PER_KV):
        out = out + jnp.where(head == g, y[g * CHUNK:(g + 1) * CHUNK], 0.0)
    return out


SOFTMAX_ROWS = 32


def _fill_mask_bias(bias_ref):
    qi = _iota((CHUNK, 2 * CHUNK), 0)
    kj = _iota((CHUNK, 2 * CHUNK), 1)
    diff = qi + CHUNK - kj
    window = (diff >= 0) & (diff < CHUNK)
    bias_ref[0] = jnp.where(window & (kj >= CHUNK), 0.0, NEG)
    bias_ref[1] = jnp.where(window, 0.0, NEG)


def _softmax_rows(s_ref, bias_ref, first, sink_ref, h, c):
    rows = pl.ds(pl.multiple_of(c * SOFTMAX_ROWS, SOFTMAX_ROWS), SOFTMAX_ROWS)
    in_block = pl.ds(pl.multiple_of((c % (CHUNK // SOFTMAX_ROWS)) * SOFTMAX_ROWS, SOFTMAX_ROWS), SOFTMAX_ROWS)
    sink = sink_ref[0, h * Q_PER_KV + c // (CHUNK // SOFTMAX_ROWS)]
    s = s_ref[rows, :] + bias_ref[first, in_block, :]
    m = jnp.maximum(jnp.max(s, axis=-1, keepdims=True), sink)
    p = jnp.exp(s - m)
    es = jnp.exp(sink - m)
    inv = 1.0 / (jnp.sum(p, axis=-1, keepdims=True) + es)
    return rows, p * inv, es * inv


def _fill_keys(p_ref, gk_ref, krep, vrep, seq):
    mean_k = _head_mean_matrix(KV2)
    reps = [_repeat_matrix(h) for h in range(N_KV)]
    for h in range(N_KV):
        krep[h][0:CHUNK, :] = jnp.zeros((CHUNK, KV_W), BF16)
        vrep[h][0:CHUNK, :] = jnp.zeros((CHUNK, KV_W), BF16)
    rows = min(seq, 4 * CHUNK)

    def piece(j, carry):
        r0 = pl.multiple_of(j * rows, CHUNK)
        nk, _ = _head_rms(p_ref[pl.ds(r0, rows), K_OFF:K_OFF + KV2].astype(F32), mean_k)
        kn = (nk * gk_ref[...]).astype(BF16)
        v = p_ref[pl.ds(r0, rows), V_OFF:V_OFF + KV2].astype(BF16)
        r1 = pl.multiple_of(r0 + CHUNK, CHUNK)
        for h in range(N_KV):
            krep[h][pl.ds(r1, rows), :] = _dot(kn, reps[h]).astype(BF16)
            vrep[h][pl.ds(r1, rows), :] = _dot(v, reps[h]).astype(BF16)
        return carry

    lax.fori_loop(0, seq // rows, piece, 0)


def attn_fwd(pqkv, gq_t, gk_t, sinks, *, seq, name, comms=()):
    T = pqkv.shape[0]
    nb = seq // CHUNK

    def body(p_ref, gq_ref, gk_ref, sink_ref, o_ref, k0, k1, v0, v1, bias_ref, s_ref, pr_ref):
        krep, vrep = [k0, k1], [v0, v1]
        _fill_keys(p_ref, gk_ref, krep, vrep, seq)
        _fill_mask_bias(bias_ref)
        mean_q = _head_mean_matrix(D_B)

        def block(i, carry):
            r0 = pl.multiple_of(i * CHUNK, CHUNK)
            first = jnp.minimum(i, 1)
            nq, _ = _head_rms(p_ref[pl.ds(r0, CHUNK), 0:D_B].astype(F32), mean_q)
            qn = nq * (gq_ref[...] * ATT_SCALE)
            for h in range(N_KV):
                qs = _stack_heads(qn[:, h * KV_W:(h + 1) * KV_W])
                s_ref[...] = _dg(qs, krep[h][pl.ds(r0, 2 * CHUNK), :], NT)

                def rows_of(c, carry2):
                    rows, probs, _ = _softmax_rows(s_ref, bias_ref, first, sink_ref, h, c)
                    pr_ref[rows, :] = probs.astype(BF16)
                    return carry2

                lax.fori_loop(0, STACK // SOFTMAX_ROWS, rows_of, 0, unroll=True)
                out = _unstack_heads(_dot(pr_ref[...], vrep[h][pl.ds(r0, 2 * CHUNK), :]))
                o_ref[pl.ds(r0, CHUNK), h * KV_W:(h + 1) * KV_W] = out.astype(BF16)
            return carry

        lax.fori_loop(0, nb, block, 0)

    return _call(
        body, name=name, grid=(T // seq,),
        in_specs=[pl.BlockSpec((seq, QKV_W), lambda b: (b, 0)), pl.BlockSpec(gq_t.shape, lambda b: (0, 0)),
                  pl.BlockSpec(gk_t.shape, lambda b: (0, 0)), pl.BlockSpec(memory_space=pltpu.SMEM)],
        out_specs=[pl.BlockSpec((seq, D_B), lambda b: (b, 0))], out_shape=[jax.ShapeDtypeStruct((T, D_B), BF16)],
        scratch_shapes=[pltpu.VMEM((seq + CHUNK, KV_W), BF16)] * 4
        + [pltpu.VMEM((2, CHUNK, 2 * CHUNK), F32), pltpu.VMEM((STACK, 2 * CHUNK), F32), pltpu.VMEM((STACK, 2 * CHUNK), BF16)],
        args=[pqkv, gq_t, gk_t, sinks], comms=comms)[0]


def mixer_out_fwd(sgu, att, pg, hin, ga, wa, wb, wo, norm, *, seq, tm, name, comms=()):
    T, D = hin.shape
    B, tps = T // seq, seq // tm

    def body(s_ref, t_ref, pg_ref, h_ref, ga_ref, wa_ref, wb_ref, wo_ref, g_ref, sc_ref, sh_ref,
             ya_ref, yb_ref, mg_ref, m_ref, ho_ref, xn_ref):
        ya = _dot(s_ref[...], wa_ref[...])
        yb = _dot(t_ref[...], wb_ref[...])
        merged = (jax.nn.sigmoid(pg_ref[:, 0:D].astype(F32)) * ya
                  + jax.nn.sigmoid(pg_ref[:, D:2 * D].astype(F32)) * yb)
        mg = merged.astype(BF16)
        m = _dot(mg, wo_ref[...])
        ya_ref[...] = ya.astype(BF16)
        yb_ref[...] = yb.astype(BF16)
        mg_ref[...] = mg
        m_ref[...] = m.astype(BF16)
        y = h_ref[...] + ga_ref[0] * m
        ho_ref[...] = y
        xn_ref[...] = _norm_mod(y, g_ref, sc_ref, sh_ref)

    tile = lambda w: pl.BlockSpec((tm, w), lambda i: (i, 0))
    b16 = jax.ShapeDtypeStruct((T, D), BF16)
    return _call(
        body, name=name, grid=(T // tm,),
        in_specs=[tile(D_A), tile(D_B), tile(2 * D), tile(D), _mod_spec(ga, tps), _resident(wa), _resident(wb),
                  _resident(wo), pl.BlockSpec((1, D), lambda i: (0, 0)), _mod_spec(norm[1], tps), _mod_spec(norm[2], tps)],
        out_specs=[tile(D)] * 6, out_shape=[b16, b16, b16, b16, jax.ShapeDtypeStruct((T, D), F32), b16],
        args=[sgu, att, pg, hin, ga[0], wa, wb, wo, norm[0], norm[1][0], norm[2][0]], comms=comms)


def ffn_bwd_act(dh, f, ga, wd, s, q, *, seq, tm, name, comms=()):
    T, D = dh.shape
    nq, fs, _ = wd.shape
    B, tps = T // seq, seq // tm

    def body(dh_ref, f_ref, ga_ref, wd_ref, s_ref, q_ref, dg_ref, du_ref, df_ref, dga_ref):
        i = pl.program_id(0)
        d = dh_ref[...]
        df = ((0.5 * ga_ref[0]) * d).astype(BF16)
        df_ref[...] = df
        part = 0.5 * jnp.sum(d * f_ref[...].astype(F32), axis=0, keepdims=True)
        for j in range(nq):
            da = _dg(df, wd_ref[j], NT)
            du_ref[j] = (da * s_ref[j].astype(F32)).astype(BF16)
            dg_ref[j] = (da * q_ref[j].astype(F32)).astype(BF16)

        @pl.when(i % tps == 0)
        def _():
            dga_ref[0] = part

        @pl.when(i % tps != 0)
        def _():
            dga_ref[0] += part

    tile = pl.BlockSpec((tm, D), lambda i: (i, 0))
    per_seq = pl.BlockSpec((1, 1, D), lambda i: (i // tps, 0, 0))
    act = pl.BlockSpec((nq, tm, fs), lambda i: (0, i, 0))
    o_shape = jax.ShapeDtypeStruct((nq, T, fs), BF16)
    dg, du, df, dga = _call(
        body, name=name, grid=(T // tm,), in_specs=[tile, tile, _mod_spec(ga, tps), _resident(wd), act, act],
        out_specs=[act, act, tile, per_seq],
        out_shape=[o_shape, o_shape, jax.ShapeDtypeStruct((T, D), BF16), jax.ShapeDtypeStruct((B, 1, D), F32)],
        args=[dh, f, ga[0], wd, s, q], comms=comms)
    return dg, du, df, dga.reshape(B, D)


def mm_tn(pairs, *, tt, name, comms=()):
    n = len(pairs)
    flat = []
    for pair in pairs:
        for a in pair:
            if not any(a is b for b in flat):
                flat.append(a)
    where = lambda a: [k for k, b in enumerate(flat) if a is b][0]
    nq = max([a.shape[0] for a in flat if a.ndim == 3] + [1])
    T = flat[0].shape[-2]
    tt = min(tt, T)
    nt = T // tt
    stacked = [x.ndim == 3 or y.ndim == 3 for x, y in pairs]

    def body(*refs):
        in_refs, o_refs, accs = refs[:len(flat)], refs[len(flat):len(flat) + n], refs[len(flat) + n:]
        t = pl.program_id(1)
        value = lambda a: in_refs[where(a)][0] if a.ndim == 3 else in_refs[where(a)][...]

        def put(j, v):
            if stacked[j]:
                o_refs[j][0] = v.astype(BF16)
            else:
                o_refs[j][...] = v.astype(BF16)

        for j, (x, y) in enumerate(pairs):
            part = _dg(value(x), value(y), TN)
            if nt == 1:
                put(j, part)
                continue

            @pl.when(t == 0)
            def _():
                accs[j][...] = part

            @pl.when(t != 0)
            def _():
                accs[j][...] += part

        if nt > 1:
            @pl.when(t == nt - 1)
            def _():
                for j in range(n):
                    put(j, accs[j][...])

    def spec(a):
        if a.ndim == 3:
            return pl.BlockSpec((1, tt, a.shape[2]), lambda q, t: (q, t, 0))
        return pl.BlockSpec((tt, a.shape[1]), lambda q, t: (t, 0))

    out_specs, out_shape = [], []
    for j, (x, y) in enumerate(pairs):
        K, N = x.shape[-1], y.shape[-1]
        if stacked[j]:
            out_specs.append(pl.BlockSpec((1, K, N), lambda q, t: (q, 0, 0)))
            out_shape.append(jax.ShapeDtypeStruct((nq, K, N), BF16))
        else:
            out_specs.append(pl.BlockSpec((K, N), lambda q, t: (0, 0)))
            out_shape.append(jax.ShapeDtypeStruct((K, N), BF16))
    return _call(
        body, name=name, grid=(nq, nt), in_specs=[spec(a) for a in flat],
        out_specs=out_specs, out_shape=out_shape,
        scratch_shapes=[pltpu.VMEM((x.shape[-1], y.shape[-1]), F32) for x, y in pairs] if nt > 1 else [],
        args=flat, comms=comms)


def dx_norm_bwd(dys, ws, hin, dh_out, g, sc, *, seq, tm, name, comms=()):
    T, D = hin.shape
    B, tps = T // seq, seq // tm
    n = len(dys)
    ranged = [w if isinstance(w, tuple) else (w, 0, w.shape[-2]) for w in ws]
    ws = []
    for w, _, _ in ranged:
        if not any(w is u for u in ws):
            ws.append(w)
    where = [[k for k, u in enumerate(ws) if u is w][0] for w, _, _ in ranged]

    def body(*refs):
        dy_refs, w_refs = refs[:n], refs[n:n + len(ws)]
        h_ref, dho_ref, g_ref, sc_ref, dhi_ref, dsh_ref, dsc_ref, dgn_ref = refs[n + len(ws):]
        i = pl.program_id(0)
        dxn = None
        for j in range(n):
            w_ref, (_, r0, r1) = w_refs[where[j]], ranged[j]
            if dys[j].ndim == 3:
                for q in range(dys[j].shape[0]):
                    part = _dot(dy_refs[j][q], w_ref[q])
                    dxn = part if dxn is None else dxn + part
            else:
                part = _dot(dy_refs[j][...], w_ref[r0:r1, :])
                dxn = part if dxn is None else dxn + part
        x = h_ref[...]
        nx, r = _rms_parts(x)
        gain = g_ref[...]
        one_sc = 1.0 + sc_ref[0]
        dsh = jnp.sum(dxn, axis=0, keepdims=True)
        dsc = jnp.sum(dxn * (nx * gain), axis=0, keepdims=True)
        dgn = jnp.sum(dxn * one_sc * nx, axis=0, keepdims=True)
        dn = dxn * one_sc * gain
        dhi_ref[...] = dho_ref[...] + r * (dn - nx * jnp.mean(dn * nx, axis=-1, keepdims=True))

        @pl.when(i % tps == 0)
        def _():
            dsh_ref[0] = dsh
            dsc_ref[0] = dsc

        @pl.when(i % tps != 0)
        def _():
            dsh_ref[0] += dsh
            dsc_ref[0] += dsc

        @pl.when(i == 0)
        def _():
            dgn_ref[...] = dgn

        @pl.when(i != 0)
        def _():
            dgn_ref[...] += dgn

    def dy_spec(a):
        if a.ndim == 3:
            return pl.BlockSpec((a.shape[0], tm, a.shape[2]), lambda i: (0, i, 0))
        return pl.BlockSpec((tm, a.shape[1]), lambda i: (i, 0))

    tile = pl.BlockSpec((tm, D), lambda i: (i, 0))
    per_seq = pl.BlockSpec((1, 1, D), lambda i: (i // tps, 0, 0))
    row = pl.BlockSpec((1, D), lambda i: (0, 0))
    dhi, dsh, dsc, dgn = _call(
        body, name=name, grid=(T // tm,),
        in_specs=[dy_spec(a) for a in dys] + [_resident(w) for w in ws] + [tile, tile, row, _mod_spec(sc, tps)],
        out_specs=[tile, per_seq, per_seq, row],
        out_shape=[jax.ShapeDtypeStruct((T, D), F32), jax.ShapeDtypeStruct((B, 1, D), F32),
                   jax.ShapeDtypeStruct((B, 1, D), F32), jax.ShapeDtypeStruct((1, D), F32)],
        args=[*dys, *ws, hin, dh_out, g, sc[0]], comms=comms)
    return dhi, dsh.reshape(B, D), dsc.reshape(B, D), dgn


def mixer_out_bwd(dh, m, ga, ya, yb, pg, wa, wb, wo, *, seq, tm, name, comms=()):
    T, D = dh.shape
    B, tps = T // seq, seq // tm

    def body(dh_ref, m_ref, ga_ref, ya_ref, yb_ref, pg_ref, wa_ref, wb_ref, wo_ref,
             dg_ref, dya_ref, dyb_ref, ds_ref, dt_ref, dm_ref, dga_ref):
        i = pl.program_id(0)
        d = dh_ref[...]
        dm = (ga_ref[0] * d).astype(BF16)
        dm_ref[...] = dm
        part = jnp.sum(d * m_ref[...].astype(F32), axis=0, keepdims=True)

        @pl.when(i % tps == 0)
        def _():
            dga_ref[0] = part

        @pl.when(i % tps != 0)
        def _():
            dga_ref[0] += part

        dmg = _dg(dm, wo_ref[...], NT)
        sa = jax.nn.sigmoid(pg_ref[:, 0:D].astype(F32))
        sb = jax.nn.sigmoid(pg_ref[:, D:2 * D].astype(F32))
        dg_ref[:, 0:D] = (dmg * ya_ref[...].astype(F32) * (sa * (1.0 - sa))).astype(BF16)
        dg_ref[:, D:2 * D] = (dmg * yb_ref[...].astype(F32) * (sb * (1.0 - sb))).astype(BF16)
        dya = (dmg * sa).astype(BF16)
        dyb = (dmg * sb).astype(BF16)
        dya_ref[...] = dya
        dyb_ref[...] = dyb
        ds_ref[...] = _dg(dya, wa_ref[...], NT).astype(BF16)
        dt_ref[...] = _dg(dyb, wb_ref[...], NT).astype(BF16)

    tile = lambda w: pl.BlockSpec((tm, w), lambda i: (i, 0))
    per_seq = pl.BlockSpec((1, 1, D), lambda i: (i // tps, 0, 0))
    dg, dya, dyb, ds, dt, dm, dga = _call(
        body, name=name, grid=(T // tm,),
        in_specs=[tile(D), tile(D), _mod_spec(ga, tps), tile(D), tile(D), tile(2 * D), _resident(wa), _resident(wb),
                  _resident(wo)],
        out_specs=[tile(2 * D), tile(D), tile(D), tile(D_A), tile(D_B), tile(D), per_seq],
        out_shape=[jax.ShapeDtypeStruct((T, 2 * D), BF16), jax.ShapeDtypeStruct((T, D), BF16),
                   jax.ShapeDtypeStruct((T, D), BF16), jax.ShapeDtypeStruct((T, D_A), BF16),
                   jax.ShapeDtypeStruct((T, D_B), BF16), jax.ShapeDtypeStruct((T, D), BF16),
                   jax.ShapeDtypeStruct((B, 1, D), F32)],
        args=[dh, m, ga[0], ya, yb, pg, wa, wb, wo], comms=comms)
    return dg, dya, dyb, ds, dt, dm, dga.reshape(B, D)


def sgu_bwd(puv, dsgu, gln, bln, ws, bs_t, *, cpb, name, comms=()):
    T = puv.shape[0]
    gd = D_A // N_GROUPS
    tm = cpb * CHUNK

    def body(p_ref, ds_ref, gln_ref, bln_ref, ws_ref, bs_ref, d_ref, dws_ref, dz_ref, dgl_ref, dbl_ref):
        i = pl.program_id(0)
        causal = _causal()
        wf = [jnp.where(causal, ws_ref[g], 0.0) for g in range(N_GROUPS)]
        wm = [w.astype(BF16) for w in wf]
        wt = [w.T.astype(BF16) for w in wf]
        dws = [jnp.zeros((CHUNK, CHUNK), F32) for _ in range(N_GROUPS)]
        dzs = jnp.zeros((CHUNK, D_A), F32)
        dgl = jnp.zeros((1, D_A), F32)
        dbl = jnp.zeros((1, D_A), F32)
        for j in range(cpb):
            rows = slice(j * CHUNK, (j + 1) * CHUNK)
            au = p_ref[rows, 0:D_A].astype(F32)
            av = p_ref[rows, D_A:2 * D_A].astype(F32)
            u, tu = _gelu_parts(au)
            vv, tv = _gelu_parts(av)
            vhat, rstd = _layer_norm_parts(vv)
            vn = (vhat * gln_ref[...] + bln_ref[...]).astype(BF16)
            dsg = ds_ref[rows, :].astype(F32)
            dz = dsg * u
            dzb = dz.astype(BF16)
            zs, dvns = [], []
            for g in range(N_GROUPS):
                cols = slice(g * gd, (g + 1) * gd)
                zs.append(_dot(wm[g], vn[:, cols]) + bs_ref[:, g:g + 1])
                dws[g] = dws[g] + _dg(dzb[:, cols], vn[:, cols], NT)
                dvns.append(_dot(wt[g], dzb[:, cols]))
            z = jnp.concatenate(zs, axis=1)
            dvn = jnp.concatenate(dvns, axis=1)
            d_ref[rows, 0:D_A] = (dsg * z * _dgelu(au, tu)).astype(BF16)
            dvh = dvn * gln_ref[...]
            dvv = rstd * (dvh - jnp.mean(dvh, axis=-1, keepdims=True)
                          - vhat * jnp.mean(dvh * vhat, axis=-1, keepdims=True))
            d_ref[rows, D_A:2 * D_A] = (dvv * _dgelu(av, tv)).astype(BF16)
            dzs = dzs + dz
            dgl = dgl + jnp.sum(dvn * vhat, axis=0, keepdims=True)
            dbl = dbl + jnp.sum(dvn, axis=0, keepdims=True)

        @pl.when(i == 0)
        def _():
            for g in range(N_GROUPS):
                dws_ref[g] = jnp.where(causal, dws[g], 0.0)
            dz_ref[...] = dzs
            dgl_ref[...] = dgl
            dbl_ref[...] = dbl

        @pl.when(i != 0)
        def _():
            for g in range(N_GROUPS):
                dws_ref[g] += jnp.where(causal, dws[g], 0.0)
            dz_ref[...] += dzs
            dgl_ref[...] += dgl
            dbl_ref[...] += dbl

    full = lambda a: pl.BlockSpec(a.shape, lambda i: (0,) * a.ndim)
    acc = lambda s: pl.BlockSpec(s, lambda i: (0,) * len(s))
    return _call(
        body, name=name, grid=(T // tm,),
        in_specs=[pl.BlockSpec((tm, 2 * D_A), lambda i: (i, 0)), pl.BlockSpec((tm, D_A), lambda i: (i, 0)),
                  full(gln), full(bln), full(ws), full(bs_t)],
        out_specs=[pl.BlockSpec((tm, 2 * D_A), lambda i: (i, 0)), acc((N_GROUPS, CHUNK, CHUNK)), acc((CHUNK, D_A)),
                   acc((1, D_A)), acc((1, D_A))],
        out_shape=[jax.ShapeDtypeStruct((T, 2 * D_A), BF16), jax.ShapeDtypeStruct((N_GROUPS, CHUNK, CHUNK), F32),
                   jax.ShapeDtypeStruct((CHUNK, D_A), F32), jax.ShapeDtypeStruct((1, D_A), F32),
                   jax.ShapeDtypeStruct((1, D_A), F32)],
        args=[puv, dsgu, gln, bln, ws, bs_t], comms=comms)


def attn_bwd(pqkv, datt, gq_t, gk_t, sinks, *, seq, name, comms=()):
    T = pqkv.shape[0]
    nb = seq // CHUNK

    def body(p_ref, do_ref, gq_ref, gk_ref, sink_ref, d_ref, dgq_ref, dgk_ref, dsk_ref,
             k0, k1, v0, v1, dk0, dk1, dv0, dv1, dgq_s, dsk_s, bias_ref, s_ref, dp_ref, pr_ref, ds_ref):
        b = pl.program_id(0)
        krep, vrep, dkacc, dvacc = [k0, k1], [v0, v1], [dk0, dk1], [dv0, dv1]
        _fill_keys(p_ref, gk_ref, krep, vrep, seq)
        _fill_mask_bias(bias_ref)
        for h in range(N_KV):
            dkacc[h][...] = jnp.zeros_like(dkacc[h])
            dvacc[h][...] = jnp.zeros_like(dvacc[h])
        dgq_s[...] = jnp.zeros_like(dgq_s)
        dsk_s[...] = jnp.zeros_like(dsk_s)
        mean_q = _head_mean_matrix(D_B)
        lane = _iota((1, 128), 1)

        def block(i, carry):
            r0 = pl.multiple_of(i * CHUNK, CHUNK)
            first = jnp.minimum(i, 1)
            nq, rq = _head_rms(p_ref[pl.ds(r0, CHUNK), 0:D_B].astype(F32), mean_q)
            qn = nq * (gq_ref[...] * ATT_SCALE)
            dqn_parts = []
            for h in range(N_KV):
                cols = slice(h * KV_W, (h + 1) * KV_W)
                qs = _stack_heads(qn[:, cols])
                kk = krep[h][pl.ds(r0, 2 * CHUNK), :]
                dos = _stack_heads(do_ref[pl.ds(r0, CHUNK), cols].astype(F32))
                s_ref[...] = _dg(qs, kk, NT)
                dp_ref[...] = _dg(dos, vrep[h][pl.ds(r0, 2 * CHUNK), :], NT)

                def rows_of(c, carry2):
                    rows, probs, psink = _softmax_rows(s_ref, bias_ref, first, sink_ref, h, c)
                    dp = dp_ref[rows, :]
                    dr = jnp.sum(probs * dp, axis=-1, keepdims=True)
                    pr_ref[rows, :] = probs.astype(BF16)
                    ds_ref[rows, :] = (probs * (dp - dr)).astype(BF16)
                    head = h * Q_PER_KV + c // (CHUNK // SOFTMAX_ROWS)
                    dsk_s[...] += jnp.where(lane == head, -jnp.sum(psink * dr), 0.0)
                    return carry2

                lax.fori_loop(0, STACK // SOFTMAX_ROWS, rows_of, 0, unroll=True)
                dqn_parts.append(_unstack_heads(_dot(ds_ref[...], kk)))
                dkacc[h][pl.ds(r0, 2 * CHUNK), :] += _dg(ds_ref[...], qs, TN)
                dvacc[h][pl.ds(r0, 2 * CHUNK), :] += _dg(pr_ref[...], dos, TN)
            dqn = jnp.concatenate(dqn_parts, axis=1) * ATT_SCALE
            dn = dqn * gq_ref[...]
            d_ref[pl.ds(r0, CHUNK), 0:D_B] = (rq * (dn - nq * _dot_split(dn * nq, mean_q))).astype(BF16)
            dgq_s[...] += jnp.sum(dqn * nq, axis=0, keepdims=True)
            return carry

        lax.fori_loop(0, nb, block, 0)

        mean_k = _head_mean_matrix(KV2)
        folds = [_fold_matrix(h) for h in range(N_KV)]
        rows = min(seq, 4 * CHUNK)

        def piece(j, dgk):
            r0 = pl.multiple_of(j * rows, CHUNK)
            r1 = pl.multiple_of(r0 + CHUNK, CHUNK)
            dkn = _dot_split(dkacc[0][pl.ds(r1, rows), :], folds[0]) + _dot_split(dkacc[1][pl.ds(r1, rows), :], folds[1])
            dv = _dot_split(dvacc[0][pl.ds(r1, rows), :], folds[0]) + _dot_split(dvacc[1][pl.ds(r1, rows), :], folds[1])
            nk, rk = _head_rms(p_ref[pl.ds(r0, rows), K_OFF:K_OFF + KV2].astype(F32), mean_k)
            dn = dkn * gk_ref[...]
            d_ref[pl.ds(r0, rows), K_OFF:K_OFF + KV2] = (rk * (dn - nk * _dot_split(dn * nk, mean_k))).astype(BF16)
            d_ref[pl.ds(r0, rows), V_OFF:V_OFF + KV2] = dv.astype(BF16)
            return dgk + jnp.sum(dkn * nk, axis=0, keepdims=True)

        dgk = lax.fori_loop(0, seq // rows, piece, jnp.zeros((1, KV2), F32))

        @pl.when(b == 0)
        def _():
            dgq_ref[...] = dgq_s[...]
            dgk_ref[...] = dgk
            dsk_ref[...] = jnp.broadcast_to(dsk_s[...], dsk_ref.shape)

        @pl.when(b != 0)
        def _():
            dgq_ref[...] += dgq_s[...]
            dgk_ref[...] += dgk
            dsk_ref[...] += jnp.broadcast_to(dsk_s[...], dsk_ref.shape)

    acc = lambda s: pl.BlockSpec(s, lambda b: (0, 0))
    return _call(
        body, name=name, grid=(T // seq,),
        in_specs=[pl.BlockSpec((seq, QKV_W), lambda b: (b, 0)), pl.BlockSpec((seq, D_B), lambda b: (b, 0)),
                  pl.BlockSpec(gq_t.shape, lambda b: (0, 0)), pl.BlockSpec(gk_t.shape, lambda b: (0, 0)),
                  pl.BlockSpec(memory_space=pltpu.SMEM)],
        out_specs=[pl.BlockSpec((seq, QKV_W), lambda b: (b, 0)), acc((1, D_B)), acc((1, KV2)), acc((8, 128))],
        out_shape=[jax.ShapeDtypeStruct((T, QKV_W), BF16), jax.ShapeDtypeStruct((1, D_B), F32),
                   jax.ShapeDtypeStruct((1, KV2), F32), jax.ShapeDtypeStruct((8, 128), F32)],
        scratch_shapes=[pltpu.VMEM((seq + CHUNK, KV_W), BF16)] * 4 + [pltpu.VMEM((seq + CHUNK, KV_W), F32)] * 4
        + [pltpu.VMEM((1, D_B), F32), pltpu.VMEM((1, 128), F32), pltpu.VMEM((2, CHUNK, 2 * CHUNK), F32)]
        + [pltpu.VMEM((STACK, 2 * CHUNK), F32)] * 2 + [pltpu.VMEM((STACK, 2 * CHUNK), BF16)] * 2,
        args=[pqkv, datt, gq_t, gk_t, sinks], comms=comms)


def ada_fwd(c_all, w, b, *, name):
    nb, D = c_all.shape
    N = w.shape[1]
    tn = N // 3

    def body(c_ref, w_ref, b_ref, o_ref):
        c = c_ref[...]
        cond = (c * jax.nn.sigmoid(c)).astype(BF16)
        o_ref[...] = _dot(cond, w_ref[...].astype(BF16)) + b_ref[...]

    return _call(
        body, name=name, grid=(3,),
        in_specs=[pl.BlockSpec((nb, D), lambda j: (0, 0)), pl.BlockSpec((D, tn), lambda j: (0, j)),
                  pl.BlockSpec((1, tn), lambda j: (0, j))],
        out_specs=[pl.BlockSpec((nb, tn), lambda j: (0, j))], out_shape=[jax.ShapeDtypeStruct((nb, N), F32)],
        args=[c_all, w, b])[0]


def ada_bwd(c_all, dmods_all, dmods_mine, gain_rows, *, name, comms=()):
    nb, D = c_all.shape
    NA = dmods_all.shape[1]
    N = dmods_mine.shape[1]
    tn = N // 3

    def body(c_ref, da_ref, dm_ref, gr_ref, db_ref, dw_ref, dgn_ref):
        c = c_ref[...]
        cond = (c * jax.nn.sigmoid(c)).astype(BF16)
        dw_ref[...] = _dg(cond, dm_ref[...].astype(BF16), TN)
        db_ref[...] = jnp.sum(da_ref[...], axis=0, keepdims=True)
        total = gr_ref[0:1, :]
        for d in range(1, N_DEV):
            total = total + gr_ref[d:d + 1, :]
        dgn_ref[...] = total

    return _call(
        body, name=name, grid=(3,),
        in_specs=[pl.BlockSpec((nb, D), lambda j: (0, 0)), pl.BlockSpec((nb, NA), lambda j: (0, 0)),
                  pl.BlockSpec((nb, tn), lambda j: (0, j)), pl.BlockSpec((N_DEV, D), lambda j: (0, 0))],
        out_specs=[pl.BlockSpec((1, NA), lambda j: (0, 0)), pl.BlockSpec((D, tn), lambda j: (0, j)),
                   pl.BlockSpec((1, D), lambda j: (0, 0))],
        out_shape=[jax.ShapeDtypeStruct((1, NA), F32), jax.ShapeDtypeStruct((D, N), F32),
                   jax.ShapeDtypeStruct((1, D), F32)],
        args=[c_all, dmods_all, dmods_mine, gain_rows], comms=comms)


def adamw(items, *, steps, name, comms=()):
    n = len(items)
    c1 = 1.0 / (1.0 - ADAM_B1 ** ADAM_STEP)
    c2 = 1.0 / (1.0 - ADAM_B2 ** ADAM_STEP)

    def body(*refs):
        for j in range(n):
            w_ref, g_ref, m_ref, v_ref = refs[4 * j:4 * j + 4]
            d_ref, mo_ref, vo_ref = refs[4 * n + 3 * j:4 * n + 3 * j + 3]
            gg = g_ref[...]
            mn = ADAM_B1 * m_ref[...] + (1.0 - ADAM_B1) * gg
            vn = ADAM_B2 * v_ref[...] + (1.0 - ADAM_B2) * (gg * gg)
            mo_ref[...] = mn
            vo_ref[...] = vn
            d_ref[...] = -ADAM_LR * ((mn * c1) / (jnp.sqrt(vn * c2) + ADAM_EPS) + ADAM_WD * w_ref[...])

    in_specs, out_specs, out_shape, args = [], [], [], []
    for item in items:
        R, C = item[0].shape
        blk = pl.BlockSpec((R // steps, C), lambda i: (i, 0))
        in_specs += [blk] * 4
        out_specs += [blk] * 3
        out_shape += [jax.ShapeDtypeStruct((R, C), F32)] * 3
        args += list(item)
    res = _call(body, name=name, grid=(steps,), in_specs=in_specs, out_specs=out_specs, out_shape=out_shape, args=args,
                comms=comms)
    return [tuple(res[3 * j:3 * j + 3]) for j in range(n)]


def _place():
    return lax.axis_index("x"), lax.axis_index("y"), lax.axis_index("c")


def _flip(v, bit):
    return 1 - v if bit else v


def _other_chips(mx, my):
    return [(_flip(mx, k & 2), _flip(my, k & 1)) for k in range(1, N_QUAD)]


def _remote(src, dst, send_sem, recv_sem, peer):
    return pltpu.make_async_remote_copy(src_ref=src, dst_ref=dst, send_sem=send_sem, recv_sem=recv_sem,
                                        device_id=peer, device_id_type=MESH)


def ag8_comm(x):
    def copies(srcs, lands, ss, rs, only_sends=False):
        mx, my, mc = _place()
        me = 4 * mx + 2 * my + mc
        local = pltpu.make_async_copy(srcs[0], lands[0].at[me], ss.at[N_DEV - 1])
        sends, recvs = [], []
        for k in range(1, N_DEV):
            peer = (_flip(mx, k & 4), _flip(my, k & 2), _flip(mc, k & 1))
            sends.append(_remote(srcs[0], lands[0].at[me], ss.at[k - 1], rs.at[k - 1], peer))
            if not only_sends:
                recvs.append(_remote(srcs[0], lands[0].at[4 * peer[0] + 2 * peer[1] + peer[2]], ss.at[k - 1], rs.at[k - 1], peer))
        return local, sends, recvs

    def start(srcs, bufs, lands, ss, rs):
        local, sends, _ = copies(srcs, lands, ss, rs, only_sends=True)
        local.start()
        for cp in sends:
            cp.start()

    def finish(srcs, bufs, lands, ss, rs):
        local, sends, recvs = copies(srcs, lands, ss, rs)
        for cp in recvs:
            cp.wait_recv()
        for cp in sends:
            cp.wait_send()
        local.wait()

    return Comm(srcs=[x], land_shapes=[jax.ShapeDtypeStruct((N_DEV,) + x.shape, x.dtype)], n_sems=N_DEV,
                start=start, finish=finish)


def sum8(stacked, *, name):
    _, r, n = stacked.shape

    def body(s_ref, o_ref):
        total = s_ref[0]
        for d in range(1, N_DEV):
            total = total + s_ref[d]
        o_ref[...] = total

    return _call(body, name=name, grid=(), in_specs=[pl.BlockSpec(memory_space=pltpu.VMEM)],
                 out_specs=[pl.BlockSpec(memory_space=pltpu.VMEM)], out_shape=[jax.ShapeDtypeStruct((r, n), F32)],
                 args=[stacked])[0]


def cast_into_stacks(ws, quad, *, name, comms=()):
    steps = 4

    def body(q_ref, *refs):
        for w_ref, o_ref in zip(refs[:len(ws)], refs[len(ws):]):
            o_ref[0] = w_ref[...].astype(BF16)

    return _call(
        body, name=name, grid=(steps,), prefetch=[quad],
        in_specs=[pl.BlockSpec((w.shape[0] // steps, w.shape[1]), lambda i, q: (i, 0)) for w in ws],
        out_specs=[pl.BlockSpec((1, w.shape[0] // steps, w.shape[1]), lambda i, q: (q[0], i, 0)) for w in ws],
        out_shape=[jax.ShapeDtypeStruct((N_QUAD,) + w.shape, BF16) for w in ws], args=list(ws), comms=comms)


def gather_comm(stacks):
    n = len(stacks)

    def copies(bufs, ss, rs, only_sends=False):
        mx, my, mc = _place()
        q = 2 * mx + my
        sibling = (mx, my, 1 - mc)
        ici_send, ici_recv, fwd_send, fwd_recv = [], [], [], []
        for p in range(n):
            hr = stacks[p].shape[1] // 2
            mine, other = pl.ds(mc * hr, hr), pl.ds((1 - mc) * hr, hr)
            for k, chip in enumerate(_other_chips(mx, my)):
                qk = 2 * chip[0] + chip[1]
                peer = (chip[0], chip[1], mc)
                own, landed, theirs = bufs[p].at[q, mine, :], bufs[p].at[qk, mine, :], bufs[p].at[qk, other, :]
                ici_send.append(_remote(own, own, ss.at[6 * p + k], rs.at[6 * p + k], peer))
                if only_sends:
                    continue
                ici_recv.append(_remote(own, landed, ss.at[6 * p + k], rs.at[6 * p + k], peer))
                fwd_send.append(_remote(landed, landed, ss.at[6 * p + 3 + k], rs.at[6 * p + 3 + k], sibling))
                fwd_recv.append(_remote(theirs, theirs, ss.at[6 * p + 3 + k], rs.at[6 * p + 3 + k], sibling))
        return ici_send, ici_recv, fwd_send, fwd_recv

    def start(srcs, bufs, lands, ss, rs):
        for cp in copies(bufs, ss, rs, only_sends=True)[0]:
            cp.start()

    def finish(srcs, bufs, lands, ss, rs):
        ici_send, ici_recv, fwd_send, fwd_recv = copies(bufs, ss, rs)
        for arrived, onward in zip(ici_recv, fwd_send):
            arrived.wait_recv()
            onward.start()
        for cp in fwd_recv:
            cp.wait_recv()
        for cp in ici_send + fwd_send:
            cp.wait_send()

    return Comm(bufs=stacks, n_sems=6 * n, start=start, finish=finish)


def rs_pair_comm(gs):
    n = len(gs)

    def copies(srcs, lands, ss, rs):
        mx, my, mc = _place()
        out = []
        for p in range(n):
            hr = gs[p].shape[1] // 2
            out.append(_remote(srcs[p].at[:, pl.ds((1 - mc) * hr, hr), :], lands[p], ss.at[p], rs.at[p], (mx, my, 1 - mc)))
        return out

    def start(srcs, bufs, lands, ss, rs):
        for cp in copies(srcs, lands, ss, rs):
            cp.start()

    def finish(srcs, bufs, lands, ss, rs):
        for cp in copies(srcs, lands, ss, rs):
            cp.wait()

    return Comm(srcs=gs, land_shapes=[jax.ShapeDtypeStruct((g.shape[0], g.shape[1] // 2, g.shape[2]), g.dtype) for g in gs],
                n_sems=n, start=start, finish=finish)


def rs_chip_comm(ss_):
    n = len(ss_)

    def copies(srcs, lands, ss, rs):
        mx, my, mc = _place()
        out = []
        for p in range(n):
            for k, chip in enumerate(_other_chips(mx, my)):
                out.append(_remote(srcs[p].at[2 * chip[0] + chip[1]], lands[p].at[k], ss.at[3 * p + k], rs.at[3 * p + k],
                                   (chip[0], chip[1], mc)))
        return out

    def start(srcs, bufs, lands, ss, rs):
        for cp in copies(srcs, lands, ss, rs):
            cp.start()

    def finish(srcs, bufs, lands, ss, rs):
        for cp in copies(srcs, lands, ss, rs):
            cp.wait()

    return Comm(srcs=ss_, land_shapes=[jax.ShapeDtypeStruct((N_QUAD - 1,) + s.shape[1:], s.dtype) for s in ss_],
                n_sems=3 * n, start=start, finish=finish)


def rs_share_comm(fulls):
    n = len(fulls)

    def copies(bufs, ss, rs, only_sends=False):
        mx, my, mc = _place()
        send, recv = [], []
        for p in range(n):
            hr = fulls[p].shape[0] // 2
            mine, other = bufs[p].at[pl.ds(mc * hr, hr), :], bufs[p].at[pl.ds((1 - mc) * hr, hr), :]
            send.append(_remote(mine, mine, ss.at[p], rs.at[p], (mx, my, 1 - mc)))
            if not only_sends:
                recv.append(_remote(other, other, ss.at[p], rs.at[p], (mx, my, 1 - mc)))
        return send, recv

    def start(srcs, bufs, lands, ss, rs):
        for cp in copies(bufs, ss, rs, only_sends=True)[0]:
            cp.start()

    def finish(srcs, bufs, lands, ss, rs):
        send, recv = copies(bufs, ss, rs)
        for cp in recv:
            cp.wait_recv()
        for cp in send:
            cp.wait_send()

    return Comm(bufs=fulls, n_sems=n, start=start, finish=finish)


def pair_sums(gs, recvs, mc, *, name):
    n = len(gs)

    def body(s_ref, *refs):
        for p in range(n):
            g_ref, r_ref, o_ref = refs[2 * p], refs[2 * p + 1], refs[2 * n + p]
            o_ref[...] = (g_ref[...].astype(F32) + r_ref[...].astype(F32)).astype(BF16)

    in_specs, out_specs, out_shape, args = [], [], [], []
    for g, r in zip(gs, recvs):
        nq, hr, C = r.shape
        in_specs += [pl.BlockSpec((1, hr, C), lambda q, s: (q, s[0], 0)), pl.BlockSpec((1, hr, C), lambda q, s: (q, 0, 0))]
        out_specs.append(pl.BlockSpec((1, hr, C), lambda q, s: (q, 0, 0)))
        out_shape.append(jax.ShapeDtypeStruct((nq, hr, C), BF16))
        args += [g, r]
    return _call(body, name=name, grid=(N_QUAD,), in_specs=in_specs, out_specs=out_specs, out_shape=out_shape,
                 args=args, prefetch=[mc])


def chip_sums(ss, recvs, quad_mc, *, name):
    n = len(ss)
    steps = 2

    def body(q_ref, *refs):
        for p in range(n):
            s_ref, r_ref, o_ref = refs[2 * p], refs[2 * p + 1], refs[2 * n + p]
            total = s_ref[0].astype(F32)
            for k in range(N_QUAD - 1):
                total = total + r_ref[k].astype(F32)
            o_ref[...] = total

    in_specs, out_specs, out_shape, args = [], [], [], []
    for s, r in zip(ss, recvs):
        _, hr, C = s.shape
        rb = hr // steps
        in_specs += [pl.BlockSpec((1, rb, C), lambda i, q: (q[0], i, 0)),
                     pl.BlockSpec((N_QUAD - 1, rb, C), lambda i, q: (0, i, 0))]
        out_specs.append(pl.BlockSpec((rb, C), lambda i, q: (q[1] * steps + i, 0)))
        out_shape.append(jax.ShapeDtypeStruct((2 * hr, C), F32))
        args += [s, r]
    return _call(body, name=name, grid=(steps,), in_specs=in_specs, out_specs=out_specs, out_shape=out_shape,
                 args=args, prefetch=[quad_mc])


def _stack_cols(w_full):
    K, N = w_full.shape
    return w_full.reshape(K, N_QUAD, N // N_QUAD).transpose(1, 0, 2)


def _unstack_cols(w4):
    nq, K, n = w4.shape
    return w4.transpose(1, 0, 2).reshape(K, nq * n)


def kernel(x, c, w_ada, b_ada, g_norm1, ffn1_w_gate, ffn1_w_up, ffn1_w_down, g_norm2, w_in, g_sgu_ln, b_sgu_ln, w_spatial, b_spatial, g_q, g_k, attn_sinks, w_branch_a, w_branch_b, w_out, g_norm3, ffn2_w_gate, ffn2_w_up, ffn2_w_down, loss_target, m_w_ada, m_b_ada, m_g_norm1, m_ffn1_w_gate, m_ffn1_w_up, m_ffn1_w_down, m_g_norm2, m_w_in, m_g_sgu_ln, m_b_sgu_ln, m_w_spatial, m_b_spatial, m_g_q, m_g_k, m_attn_sinks, m_w_branch_a, m_w_branch_b, m_w_out, m_g_norm3, m_ffn2_w_gate, m_ffn2_w_up, m_ffn2_w_down, v_w_ada, v_b_ada, v_g_norm1, v_ffn1_w_gate, v_ffn1_w_up, v_ffn1_w_down, v_g_norm2, v_w_in, v_g_sgu_ln, v_b_sgu_ln, v_w_spatial, v_b_spatial, v_g_q, v_g_k, v_attn_sinks, v_w_branch_a, v_w_branch_b, v_w_out, v_g_norm3, v_ffn2_w_gate, v_ffn2_w_up, v_ffn2_w_down):
    B, S, D = x.shape
    T = B * S
    n_mod = b_ada.shape[1] // D
    mx, my, mc = _place()
    quad = 2 * mx + my
    mc_arr = jnp.reshape(mc, (1,)).astype(jnp.int32)
    quad_arr = jnp.reshape(quad, (1,)).astype(jnp.int32)
    quad_mc = jnp.stack([quad, mc]).astype(jnp.int32)
    tm = min(512, S)
    tm_big = min(1024, S)
    tt_half = min(2048, T)
    cpb = min(4, S // CHUNK)

    xt = x.reshape(T, D)
    tgt = loss_target.reshape(T, D)

    tr = lambda a: jnp.swapaxes(a, 1, 2)
    stack_wg1, stack_wu1 = cast_into_stacks([tr(ffn1_w_gate)[0], tr(ffn1_w_up)[0]], quad_arr, name="stack_gu1")
    gather_wg1, gather_wu1 = gather_comm([stack_wg1]), gather_comm([stack_wu1])
    later = [ffn1_w_down, tr(w_in), w_branch_a, w_branch_b, w_out, tr(ffn2_w_gate), tr(ffn2_w_up), ffn2_w_down]
    gather_cond = ag8_comm(c)
    stacks = cast_into_stacks([w[0] for w in later[0:4]], quad_arr, name="stack_a", comms=[gather_wg1, gather_cond])
    (wg1,) = gather_wg1.bufs_out
    c_all = gather_cond.lands[0].reshape(N_DEV * B, D)
    n_ada = w_ada.shape[2]
    b_mine = lax.dynamic_slice(b_ada, (0, quad * n_ada), (1, n_ada))
    mods_part = ada_fwd(c_all, w_ada[0], b_mine, name="ada_fwd")
    gather_mods = ag8_comm(mods_part)
    stacks += cast_into_stacks([w[0] for w in later[4:8]], quad_arr, name="stack_b", comms=[gather_wu1, gather_mods])
    (wu1,), (mods_parts,) = gather_wu1.bufs_out, gather_mods.lands
    gather_wd1, gather_win = gather_comm([stacks[0]]), gather_comm([stacks[1]])
    gather_abo = gather_comm(stacks[2:5])
    gather_wg3, gather_wu3, gather_wd3 = gather_comm([stacks[5]]), gather_comm([stacks[6]]), gather_comm([stacks[7]])
    mods = jnp.concatenate([mods_parts[2 * j] for j in range(N_QUAD)], axis=1)
    me = 4 * mx + 2 * my + mc
    mods = lax.dynamic_slice(mods, (me * B, 0), (B, n_mod * D))
    mods = mods.reshape(B, n_mod, 1, D)
    sh1, sc1, ga1, sh2, sc2, ga2, sh3, sc3, ga3 = [(mods, j) for j in range(n_mod)]
    bs_t = b_spatial[0].T
    ws = w_spatial[0]

    xn1 = norm_mod_fwd(xt, g_norm1, sc1, sh1, seq=S, tm=tm, name="norm1")
    g1, u1, a1 = ffn_up(xn1, wg1, wu1, tm=tm_big, name="ffn1_up", comms=[gather_wd1, gather_abo])
    (wd1,), (wa4, wb4, wo4) = gather_wd1.bufs_out, gather_abo.bufs_out
    wa, wb = _unstack_cols(wa4), _unstack_cols(wb4)
    wo = wo4.reshape(D, D)
    h1, f1, xn2 = ffn_down(a1, wd1, xt, ga1, norm=(g_norm2, sc2, sh2), seq=S, tm=tm, name="ffn1_down",
                           comms=[gather_win])
    (win4,) = gather_win.bufs_out
    win = win4.reshape(-1, D)
    groups = [(0, 2 * D_A), (2 * D_A, 2 * D_A + QKV_W), (2 * D_A + QKV_W, win.shape[0])]
    puv, pqkv, pgt = proj_fwd(xn2, win, groups, tm=tm, name="proj", comms=[gather_wg3])
    (wg3,) = gather_wg3.bufs_out
    sgu = sgu_fwd(puv, g_sgu_ln, b_sgu_ln, ws, bs_t, cpb=cpb, name="sgu_fwd")
    gq_t, gk_t = jnp.tile(g_q, (1, N_Q)), jnp.tile(g_k, (1, N_KV))
    att = attn_fwd(pqkv, gq_t, gk_t, attn_sinks, seq=S, name="attn_fwd", comms=[gather_wu3])
    (wu3,) = gather_wu3.bufs_out
    ya, yb, merged, mm, h2, xn3 = mixer_out_fwd(sgu, att, pgt, h1, ga2, wa, wb, wo, (g_norm3, sc3, sh3), seq=S,
                                                tm=tm, name="mixer_out", comms=[gather_wd3])
    (wd3,) = gather_wd3.bufs_out
    g3, u3, a3 = ffn_up(xn3, wg3, wu3, tm=tm_big, name="ffn2_up")
    dy, f3, lparts = ffn_down(a3, wd3, h2, ga3, target=tgt, seq=S, tm=tm, name="ffn2_down_loss")
    loss_part = jnp.sum(lparts[:, 0, 0])

    def sums_of(pair, tag):
        return pair_sums(pair.srcs, pair.lands, mc_arr, name=f"pair_sum_{tag}")

    def halves_of(*chips_and_tag):
        *chips, tag = chips_and_tag
        return chip_sums([s for ch in chips for s in ch.srcs], [r for ch in chips for r in ch.lands], quad_mc,
                         name=f"chip_sum_{tag}")

    dg3, du3, df3, dga3 = ffn_bwd_act(dy, f3, ga3, wd3, g3, u3, seq=S, tm=tm, name="ffn2_act_bwd")
    (dwd3,) = mm_tn([(a3, df3)], tt=T, name="ffn2_dwd")
    pair_wd3 = rs_pair_comm([dwd3])
    dwg3, dwu3 = mm_tn([(dg3, xn3), (du3, xn3)], tt=tt_half, name="ffn2_dwgu", comms=[pair_wd3])
    chip_wd3 = rs_chip_comm(sums_of(pair_wd3, "wd3"))
    pair_gu3 = rs_pair_comm([dwg3, dwu3])
    dh2, dsh3, dsc3, dgn3 = dx_norm_bwd([dg3, du3], [wg3, wu3], h2, dy, g_norm3, sc3, seq=S, tm=tm, name="ffn2_dx",
                                        comms=[chip_wd3, pair_gu3])
    sum_wg3, sum_wu3 = sums_of(pair_gu3, "gu3")
    chip_wg3, chip_wu3 = rs_chip_comm([sum_wg3]), rs_chip_comm([sum_wu3])

    dgt, dya, dyb, dsgu, datt, dm, dga2 = mixer_out_bwd(dh2, mm, ga2, ya, yb, pgt, wa, wb, wo, seq=S, tm=tm,
                                                        name="mixer_out_bwd", comms=[chip_wg3])
    dwo, dwa, dwb = mm_tn([(merged, dm), (sgu, dya), (att, dyb)], tt=tm_big, name="mixer_dw")
    pair_abo = rs_pair_comm([_stack_cols(dwa), _stack_cols(dwb), dwo.reshape(N_QUAD, D // N_QUAD, D)])
    duv, dws, dzs, dgln, dbln = sgu_bwd(puv, dsgu, g_sgu_ln, b_sgu_ln, ws, bs_t, cpb=cpb, name="sgu_bwd",
                                        comms=[pair_abo])
    chip_abo = rs_chip_comm(sums_of(pair_abo, "abo"))
    dqkv, dgq_h, dgk_h, dsk = attn_bwd(pqkv, datt, gq_t, gk_t, attn_sinks, seq=S, name="attn_bwd",
                                       comms=[chip_wu3, chip_abo])
    dgq = dgq_h.reshape(N_Q, HEAD_DIM).sum(axis=0, keepdims=True)
    dgk = dgk_h.reshape(N_KV, HEAD_DIM).sum(axis=0, keepdims=True)
    share3 = rs_share_comm(halves_of(chip_wg3, chip_wu3, chip_wd3, "ffn2"))
    dwin_uv, dwin_qkv = mm_tn([(duv, xn2), (dqkv, xn2)], tt=tt_half, name="mixer_dwin_a", comms=[share3])
    (dwin_gt,) = mm_tn([(dgt, xn2)], tt=tt_half, name="mixer_dwin_b")
    dwin_parts = [dwin_uv, dwin_qkv, dwin_gt]
    r_wg3, r_wu3, r_wd3 = share3.bufs_out
    pair_win = rs_pair_comm([jnp.concatenate(dwin_parts, axis=0).reshape(win4.shape)])
    dh1, dsh2, dsc2, dgn2 = dx_norm_bwd([duv, dqkv, dgt], [(win, r0, r1) for r0, r1 in groups], h1, dh2, g_norm2, sc2, seq=S,
                                        tm=tm, name="mixer_dx", comms=[pair_win])
    chip_win = rs_chip_comm(sums_of(pair_win, "win"))

    dg1, du1, df1, dga1 = ffn_bwd_act(dh1, f1, ga1, wd1, g1, u1, seq=S, tm=tm, name="ffn1_act_bwd", comms=[chip_win])
    share_mix = rs_share_comm(halves_of(chip_win, chip_abo, "mix"))

    db_s = dzs.reshape(CHUNK, N_GROUPS, D_A // N_GROUPS).sum(axis=2).T.reshape(1, N_GROUPS * CHUNK)
    tail = jnp.concatenate([db_s, dgq, dgk, dsk[0:1, 0:N_Q], loss_part.reshape(1, 1)], axis=1)
    tail = jnp.pad(tail, ((0, 0), (0, (-tail.shape[1]) % D)))
    lnrow = jnp.concatenate([dgln, dbln], axis=1)
    lnrow = jnp.pad(lnrow, ((0, 0), (0, (-lnrow.shape[1]) % D)))
    packed = jnp.concatenate([dgn2, dgn3, lnrow.reshape(-1, D), tail.reshape(-1, D), dws.reshape(-1, D)], axis=0)
    n_rows = packed.shape[0]
    packed = jnp.pad(packed, ((0, (-n_rows) % 8), (0, 0)))
    gather_small = ag8_comm(packed)

    dwg1, dwu1 = mm_tn([(dg1, xn1), (du1, xn1)], tt=tt_half, name="ffn1_dwgu", comms=[share_mix, gather_small])
    r_win, r_wa, r_wb, r_wo = share_mix.bufs_out
    small = sum8(gather_small.lands[0], name="sum_small")
    pair_gu1 = rs_pair_comm([dwg1, dwu1])
    (dwd1,) = mm_tn([(a1, df1)], tt=T, name="ffn1_dwd", comms=[pair_gu1])
    chip_gu1 = rs_chip_comm(sums_of(pair_gu1, "gu1"))
    pair_wd1 = rs_pair_comm([dwd1])
    dx, dsh1, dsc1, dgn1 = dx_norm_bwd([dg1, du1], [wg1, wu1], xt, dh1, g_norm1, sc1, seq=S, tm=tm, name="ffn1_dx",
                                       comms=[chip_gu1, pair_wd1])
    chip_wd1 = rs_chip_comm(sums_of(pair_wd1, "wd1"))
    share_gu1 = rs_share_comm(halves_of(chip_gu1, "gu1"))

    names = ["w_ada", "b_ada", "g_norm1", "ffn1_w_gate", "ffn1_w_up", "ffn1_w_down", "g_norm2", "w_in", "g_sgu_ln",
             "b_sgu_ln", "w_spatial", "b_spatial", "g_q", "g_k", "attn_sinks", "w_branch_a", "w_branch_b", "w_out",
             "g_norm3", "ffn2_w_gate", "ffn2_w_up", "ffn2_w_down"]
    weights = dict(zip(names, [w_ada, b_ada, g_norm1, ffn1_w_gate, ffn1_w_up, ffn1_w_down, g_norm2, w_in, g_sgu_ln,
                               b_sgu_ln, w_spatial, b_spatial, g_q, g_k, attn_sinks, w_branch_a, w_branch_b, w_out,
                               g_norm3, ffn2_w_gate, ffn2_w_up, ffn2_w_down]))
    m_in = dict(zip(names, [m_w_ada, m_b_ada, m_g_norm1, m_ffn1_w_gate, m_ffn1_w_up, m_ffn1_w_down, m_g_norm2, m_w_in,
                            m_g_sgu_ln, m_b_sgu_ln, m_w_spatial, m_b_spatial, m_g_q, m_g_k, m_attn_sinks, m_w_branch_a,
                            m_w_branch_b, m_w_out, m_g_norm3, m_ffn2_w_gate, m_ffn2_w_up, m_ffn2_w_down]))
    v_in = dict(zip(names, [v_w_ada, v_b_ada, v_g_norm1, v_ffn1_w_gate, v_ffn1_w_up, v_ffn1_w_down, v_g_norm2, v_w_in,
                            v_g_sgu_ln, v_b_sgu_ln, v_w_spatial, v_b_spatial, v_g_q, v_g_k, v_attn_sinks, v_w_branch_a,
                            v_w_branch_b, v_w_out, v_g_norm3, v_ffn2_w_gate, v_ffn2_w_up, v_ffn2_w_down]))
    transposed = ("ffn1_w_gate", "ffn1_w_up", "w_in", "ffn2_w_gate", "ffn2_w_up")
    grads, delta, new_m, new_v = {}, {}, {}, {}

    def update(group, steps, name, comms=()):
        form = lambda nm, a: (tr(a) if nm in transposed else a)[0].reshape(group[nm].shape)
        res = adamw([(form(nm, weights[nm]), g, form(nm, m_in[nm]), form(nm, v_in[nm])) for nm, g in group.items()],
                    steps=steps, name=name, comms=comms)
        back = lambda nm, a: tr(a[None]) if nm in transposed else a.reshape(weights[nm].shape)
        for (nm, g), (d, mn, vn) in zip(group.items(), res):
            grads[nm], delta[nm], new_m[nm], new_v[nm] = back(nm, g), back(nm, d), back(nm, mn), back(nm, vn)

    dmods = jnp.concatenate([dsh1, dsc1, dga1, dsh2, dsc2, dga2, dsh3, dsc3, dga3], axis=1)
    dmods = jnp.concatenate([dmods, jnp.pad(dgn1, ((0, 0), (0, (n_mod - 1) * D)))], axis=0)
    gather_dmods = ag8_comm(dmods)
    run_comms([chip_wd1, share_gu1, gather_dmods], name="rs_tail")
    update({"ffn2_w_gate": r_wg3, "ffn2_w_up": r_wu3, "ffn2_w_down": r_wd3, "w_out": r_wo, "w_branch_a": r_wa,
            "w_branch_b": r_wb}, 8, "adamw_early")
    r_wg1, r_wu1 = share_gu1.bufs_out
    (gathered,) = gather_dmods.lands
    dmods_all = gathered[:, 0:B].reshape(N_DEV * B, n_mod * D)
    dmods_mine = lax.dynamic_slice(dmods_all, (0, quad * n_ada), (N_DEV * B, n_ada))
    share_wd1 = rs_share_comm(halves_of(chip_wd1, "wd1"))
    db_ada, dw_ada, g_gn1 = ada_bwd(c_all, dmods_all, dmods_mine, gathered[:, B, 0:D], name="ada_bwd", comms=[share_wd1])
    (r_wd1,) = share_wd1.bufs_out

    ln_rows = lnrow.size // D
    tail_rows = tail.size // D
    r = 2
    g_gn2, g_gn3 = small[0:1], small[1:2]
    ln_flat = small[r:r + ln_rows].reshape(1, -1)
    r += ln_rows
    tail_flat = small[r:r + tail_rows].reshape(1, -1)
    r += tail_rows
    g_ws = small[r:r + dws.size // D].reshape(w_spatial.shape)
    g_gln, g_bln = ln_flat[:, 0:D_A], ln_flat[:, D_A:2 * D_A]
    o = N_GROUPS * CHUNK
    g_bs = tail_flat[:, 0:o].reshape(b_spatial.shape)
    g_gq, g_gk, g_sk = tail_flat[:, o:o + HEAD_DIM], tail_flat[:, o + HEAD_DIM:o + 2 * HEAD_DIM], tail_flat[:, o + 2 * HEAD_DIM:o + 2 * HEAD_DIM + N_Q]
    loss = tail_flat[0, o + 2 * HEAD_DIM + N_Q]

    update({"w_ada": dw_ada}, 16, "adamw_w_ada")
    update({"ffn1_w_gate": r_wg1, "ffn1_w_up": r_wu1, "ffn1_w_down": r_wd1, "w_in": r_win}, 8, "adamw_late")

    update({"b_ada": db_ada, "g_norm1": g_gn1, "g_norm2": g_gn2, "g_norm3": g_gn3, "g_sgu_ln": g_gln,
            "b_sgu_ln": g_bln, "w_spatial": g_ws.reshape(-1, CHUNK), "b_spatial": g_bs.reshape(N_GROUPS, CHUNK),
            "g_q": g_gq, "g_k": g_gk, "attn_sinks": g_sk}, 1, "adamw_small")

    return (loss, dx.reshape(B, S, D), *[grads[nm] for nm in names], *[delta[nm] for nm in names],
            *[new_m[nm] for nm in names], *[new_v[nm] for nm in names])
```

```python
import functools
import math
import operator

import jax
import jax.numpy as jnp
from jax import lax
from jax.experimental import pallas as pl
from jax.experimental.pallas import tpu as pltpu

F32 = jnp.float32
BF16 = jnp.bfloat16
EPS = 1e-6
NEG = -1e30
N_DEV = 8
N_QUAD = 4
D_A = 512
N_GROUPS = 4
CHUNK = 128
HEAD_DIM = 64
N_KV = 2
Q_PER_KV = 4
N_Q = N_KV * Q_PER_KV
D_B = N_Q * HEAD_DIM
QKV_W = D_B + 2 * N_KV * HEAD_DIM
K_OFF = D_B
V_OFF = D_B + N_KV * HEAD_DIM
ATT_SCALE = HEAD_DIM ** -0.5
GELU_K = math.sqrt(2.0 / math.pi)
GELU_C = 0.044715
ADAM_LR, ADAM_B1, ADAM_B2, ADAM_EPS, ADAM_WD, ADAM_STEP = 0.001, 0.9, 0.999, 1e-08, 0.01, 10
VMEM_LIMIT_BYTES = 56 * 1024 * 1024
MESH = pl.DeviceIdType.MESH
NT = (((1,), (1,)), ((), ()))
TN = (((0,), (0,)), ((), ()))
ANY = pl.BlockSpec(memory_space=pl.ANY)


def _dot(a, b):
    return jnp.dot(a, b, preferred_element_type=F32)


def _dg(a, b, dims):
    return lax.dot_general(a, b, dims, preferred_element_type=F32)


def _gelu_parts(x):
    t = jnp.tanh(GELU_K * (x + GELU_C * x * x * x))
    return 0.5 * x * (1.0 + t), t


def _dgelu(x, t):
    return 0.5 * (1.0 + t) + 0.5 * x * (1.0 - t * t) * (GELU_K * (1.0 + 3.0 * GELU_C * x * x))


def _row_block(rows, target):
    b = min(rows, target)
    while rows % b or b % 8:
        b -= 1
    return b


def _mod_spec(mod, tps):
    mods, j = mod
    return pl.BlockSpec((1, None, 1, mods.shape[3]), lambda i: (i // tps, j, 0, 0))


def _resident(a):
    return pl.BlockSpec(a.shape, lambda *_: (0,) * a.ndim, pipeline_mode=pl.Buffered(1))


class Comm:
    def __init__(self, *, srcs=(), bufs=(), land_shapes=(), n_sems, start, finish):
        self.srcs, self.bufs, self.land_shapes = list(srcs), list(bufs), list(land_shapes)
        self.n_sems, self.start, self.finish = n_sems, start, finish
        self.bufs_out, self.lands = None, None


def _call(body, *, name, grid, in_specs, out_specs, out_shape, args, scratch_shapes=(), comms=(), prefetch=()):
    prefetch = list(prefetch)
    in_specs, out_specs, out_shape = list(in_specs), list(out_specs), list(out_shape)
    args, scratch = list(args), list(scratch_shapes)
    n_in, n_out, n_scr = len(args), len(out_shape), len(scratch)
    aliases, layout = {}, []
    for cm in comms:
        i0, o0, s0 = len(args), len(out_shape), len(scratch)
        args += cm.srcs + cm.bufs
        in_specs += [ANY] * (len(cm.srcs) + len(cm.bufs))
        out_shape += [jax.ShapeDtypeStruct(b.shape, b.dtype) for b in cm.bufs] + cm.land_shapes
        out_specs += [ANY] * (len(cm.bufs) + len(cm.land_shapes))
        for j in range(len(cm.bufs)):
            aliases[len(prefetch) + i0 + len(cm.srcs) + j] = o0 + j
        scratch += [pltpu.SemaphoreType.DMA((cm.n_sems,)), pltpu.SemaphoreType.DMA((cm.n_sems,))]
        layout.append((i0, o0, s0))
    n_in_all, n_out_all = len(args), len(out_shape)

    def wrapped(*refs):
        scalars, refs = refs[:len(prefetch)], refs[len(prefetch):]
        ins, outs, scr = refs[:n_in_all], refs[n_in_all:n_in_all + n_out_all], refs[n_in_all + n_out_all:]
        parts = []
        for cm, (i0, o0, s0) in zip(comms, layout):
            nb = len(cm.bufs)
            parts.append((ins[i0:i0 + len(cm.srcs)], outs[o0:o0 + nb], outs[o0 + nb:o0 + nb + len(cm.land_shapes)],
                          scr[s0], scr[s0 + 1]))
        if comms and grid:
            ids = [pl.program_id(d) for d in range(len(grid))]
            first = functools.reduce(operator.and_, [i == 0 for i in ids])
            last = functools.reduce(operator.and_, [i == g - 1 for i, g in zip(ids, grid)])

            @pl.when(first)
            def _():
                for cm, p in zip(comms, parts):
                    cm.start(*p)
        elif comms:
            for cm, p in zip(comms, parts):
                cm.start(*p)
        if body is not None:
            body(*scalars, *ins[:n_in], *outs[:n_out], *scr[:n_scr])
        if comms and grid:
            @pl.when(last)
            def _():
                for cm, p in zip(comms, parts):
                    cm.finish(*p)
        elif comms:
            for cm, p in zip(comms, parts):
                cm.finish(*p)

    if prefetch:
        kwargs = dict(grid_spec=pltpu.PrefetchScalarGridSpec(
            num_scalar_prefetch=len(prefetch), grid=grid, in_specs=in_specs, out_specs=out_specs, scratch_shapes=scratch))
    else:
        kwargs = dict(in_specs=in_specs, out_specs=out_specs, scratch_shapes=scratch, **(dict(grid=grid) if grid else {}))
    sem = ("arbitrary",) * len(grid)
    res = pl.pallas_call(
        wrapped, name=name, out_shape=out_shape, input_output_aliases=aliases,
        compiler_params=pltpu.CompilerParams(dimension_semantics=sem, vmem_limit_bytes=VMEM_LIMIT_BYTES), **kwargs,
    )(*prefetch, *args)
    for cm, (i0, o0, s0) in zip(comms, layout):
        nb = len(cm.bufs)
        cm.bufs_out = list(res[o0:o0 + nb])
        cm.lands = list(res[o0 + nb:o0 + nb + len(cm.land_shapes)])
    return list(res[:n_out])


def run_comms(comms, *, name):
    _call(None, name=name, grid=(), in_specs=[], out_specs=[], out_shape=[], args=[], comms=comms)


def norm_mod_fwd(h, g, sc, sh, *, seq, tm, name, comms=()):
    T, D = h.shape
    B, tps = T // seq, seq // tm

    def body(h_ref, g_ref, sc_ref, sh_ref, o_ref):
        x = h_ref[...]
        r = lax.rsqrt(jnp.mean(x * x, axis=-1, keepdims=True) + EPS)
        y = x * r * g_ref[...]
        o_ref[...] = (y * (1.0 + sc_ref[0]) + sh_ref[0]).astype(o_ref.dtype)

    return _call(
        body, name=name, grid=(T // tm,),
        in_specs=[pl.BlockSpec((tm, D), lambda i: (i, 0)), pl.BlockSpec((1, D), lambda i: (0, 0)),
                  _mod_spec(sc, tps), _mod_spec(sh, tps)],
        out_specs=[pl.BlockSpec((tm, D), lambda i: (i, 0))], out_shape=[jax.ShapeDtypeStruct((T, D), BF16)],
        args=[h, g, sc[0], sh[0]], comms=comms)[0]


def ffn_up(xn, wg, wu, *, tm, name, comms=()):
    T, D = xn.shape
    nq, fs, _ = wg.shape

    def body(x_ref, wg_ref, wu_ref, s_ref, q_ref, a_ref):
        x = x_ref[...]
        g = _dg(x, wg_ref[0], NT)
        u = _dg(x, wu_ref[0], NT)
        sg = jax.nn.sigmoid(g)
        s = g * sg
        s_ref[0] = s.astype(BF16)
        q_ref[0] = (u * (sg + s * (1.0 - sg))).astype(BF16)
        a_ref[0] = (s * u).astype(BF16)

    w_spec = pl.BlockSpec((1, fs, D), lambda q, i: (q, 0, 0))
    o_spec = pl.BlockSpec((1, tm, fs), lambda q, i: (q, i, 0))
    o_shape = jax.ShapeDtypeStruct((nq, T, fs), BF16)
    return _call(
        body, name=name, grid=(nq, T // tm),
        in_specs=[pl.BlockSpec((tm, D), lambda q, i: (i, 0)), w_spec, w_spec],
        out_specs=[o_spec, o_spec, o_spec], out_shape=[o_shape, o_shape, o_shape], args=[xn, wg, wu], comms=comms)


def _norm_mod(y, g_ref, sc_ref, sh_ref):
    nx, _ = _rms_parts(y)
    return ((nx * g_ref[...]) * (1.0 + sc_ref[0]) + sh_ref[0]).astype(BF16)


def ffn_down(a, wd, hin, ga, *, target=None, norm=None, seq, tm, name, comms=()):
    nq, T, fs = a.shape
    D = wd.shape[-1]
    B, tps, nt = T // seq, seq // tm, T // tm

    def body(a_ref, wd_ref, h_ref, ga_ref, *rest):
        rest = list(rest)
        t_ref = rest.pop(0) if target is not None else None
        norm_refs = [rest.pop(0) for _ in range(3)] if norm is not None else None
        o_ref, f_ref = rest[0], rest[1]
        f = _dot(a_ref[0], wd_ref[0])
        for q in range(1, nq):
            f = f + _dot(a_ref[q], wd_ref[q])
        f_ref[...] = f.astype(BF16)
        y = h_ref[...] + (0.5 * ga_ref[0]) * f
        if target is not None:
            e = y - t_ref[...]
            o_ref[...] = e * (1.0 / D)
            rest[2][...] = jnp.full(rest[2].shape, 0.5 * jnp.sum(e * e) * (1.0 / D), F32)
        else:
            o_ref[...] = y
        if norm is not None:
            rest[2][...] = _norm_mod(y, *norm_refs)

    tile = pl.BlockSpec((tm, D), lambda i: (i, 0))
    in_specs = [pl.BlockSpec((nq, tm, fs), lambda i: (0, i, 0)), _resident(wd), tile, _mod_spec(ga, tps)]
    out_specs = [tile, tile]
    out_shape = [jax.ShapeDtypeStruct((T, D), F32), jax.ShapeDtypeStruct((T, D), BF16)]
    args = [a, wd, hin, ga[0]]
    if target is not None:
        in_specs.append(tile)
        args.append(target)
        out_specs.append(pl.BlockSpec((1, 8, 128), lambda i: (i, 0, 0)))
        out_shape.append(jax.ShapeDtypeStruct((nt, 8, 128), F32))
    if norm is not None:
        in_specs += [pl.BlockSpec((1, D), lambda i: (0, 0)), _mod_spec(norm[1], tps), _mod_spec(norm[2], tps)]
        args += [norm[0], norm[1][0], norm[2][0]]
        out_specs.append(tile)
        out_shape.append(jax.ShapeDtypeStruct((T, D), BF16))
    return _call(body, name=name, grid=(nt,), in_specs=in_specs, out_specs=out_specs, out_shape=out_shape, args=args,
                 comms=comms)


def proj_fwd(xn, w, groups, *, tm, name, comms=()):
    T, D = xn.shape

    def body(x_ref, w_ref, *o_refs):
        x = x_ref[...]
        for (r0, r1), o_ref in zip(groups, o_refs):
            o_ref[...] = _dg(x, w_ref[r0:r1, :], NT).astype(BF16)

    return _call(
        body, name=name, grid=(T // tm,), in_specs=[pl.BlockSpec((tm, D), lambda i: (i, 0)), _resident(w)],
        out_specs=[pl.BlockSpec((tm, r1 - r0), lambda i: (i, 0)) for r0, r1 in groups],
        out_shape=[jax.ShapeDtypeStruct((T, r1 - r0), BF16) for r0, r1 in groups], args=[xn, w], comms=comms)


def _layer_norm_parts(v):
    mu = jnp.mean(v, axis=-1, keepdims=True)
    d = v - mu
    rstd = lax.rsqrt(jnp.mean(d * d, axis=-1, keepdims=True) + EPS)
    return d * rstd, rstd


def _causal():
    row = lax.broadcasted_iota(jnp.int32, (CHUNK, CHUNK), 0)
    col = lax.broadcasted_iota(jnp.int32, (CHUNK, CHUNK), 1)
    return row >= col


def sgu_fwd(puv, gln, bln, ws, bs_t, *, cpb, name, comms=()):
    T = puv.shape[0]
    gd = D_A // N_GROUPS
    tm = cpb * CHUNK

    def body(p_ref, gln_ref, bln_ref, ws_ref, bs_ref, o_ref):
        causal = _causal()
        wm = [jnp.where(causal, ws_ref[g], 0.0).astype(BF16) for g in range(N_GROUPS)]
        for j in range(cpb):
            rows = slice(j * CHUNK, (j + 1) * CHUNK)
            u, _ = _gelu_parts(p_ref[rows, 0:D_A].astype(F32))
            vv, _ = _gelu_parts(p_ref[rows, D_A:2 * D_A].astype(F32))
            vhat, _ = _layer_norm_parts(vv)
            vn = (vhat * gln_ref[...] + bln_ref[...]).astype(BF16)
            for g in range(N_GROUPS):
                cols = slice(g * gd, (g + 1) * gd)
                z = _dot(wm[g], vn[:, cols]) + bs_ref[:, g:g + 1]
                o_ref[rows, cols] = (u[:, cols] * z).astype(BF16)

    full = lambda a: pl.BlockSpec(a.shape, lambda i: (0,) * a.ndim)
    return _call(
        body, name=name, grid=(T // tm,),
        in_specs=[pl.BlockSpec((tm, 2 * D_A), lambda i: (i, 0)), full(gln), full(bln), full(ws), full(bs_t)],
        out_specs=[pl.BlockSpec((tm, D_A), lambda i: (i, 0))], out_shape=[jax.ShapeDtypeStruct((T, D_A), BF16)],
        args=[puv, gln, bln, ws, bs_t], comms=comms)[0]


def _rms_parts(x):
    r = lax.rsqrt(jnp.mean(x * x, axis=-1, keepdims=True) + EPS)
    return x * r, r


KV_W = Q_PER_KV * HEAD_DIM
KV2 = N_KV * HEAD_DIM
STACK = Q_PER_KV * CHUNK


def _iota(shape, dim):
    return lax.broadcasted_iota(jnp.int32, shape, dim)


def _head_mean_matrix(n):
    return jnp.where((_iota((n, n), 0) >> 6) == (_iota((n, n), 1) >> 6), 1.0 / HEAD_DIM, 0.0).astype(BF16)


def _repeat_matrix(h):
    return jnp.where(_iota((KV2, KV_W), 0) == h * HEAD_DIM + (_iota((KV2, KV_W), 1) & (HEAD_DIM - 1)), 1.0, 0.0).astype(BF16)


def _fold_matrix(h):
    j, l = _iota((KV_W, KV2), 0), _iota((KV_W, KV2), 1)
    return jnp.where(((j & (HEAD_DIM - 1)) == (l & (HEAD_DIM - 1))) & ((l >> 6) == h), 1.0, 0.0).astype(BF16)


def _dot_split(x, m):
    hi = x.astype(BF16)
    lo = (x - hi.astype(F32)).astype(BF16)
    return _dot(hi, m) + _dot(lo, m)


def _head_rms(x, mean_matrix):
    r = lax.rsqrt(_dot_split(x * x, mean_matrix) + EPS)
    return x * r, r


def _stack_heads(x):
    head = _iota(x.shape, 1) >> 6
    return jnp.concatenate([jnp.where(head == g, x, 0.0).astype(BF16) for g in range(Q_PER_KV)], axis=0)


def _unstack_heads(y):
    head = _iota((CHUNK, KV_W), 1) >> 6
    out = jnp.where(head == 0, y[0:CHUNK], 0.0)
    for g in range(1, Q_PER_KV):
        out = out + jnp.where(head == g, y[g * CHUNK:(g + 1) * CHUNK], 0.0)
    return out


SOFTMAX_ROWS = 32


def _fill_mask_bias(bias_ref):
    qi = _iota((CHUNK, 2 * CHUNK), 0)
    kj = _iota((CHUNK, 2 * CHUNK), 1)
    diff = qi + CHUNK - kj
    window = (diff >= 0) & (diff < CHUNK)
    bias_ref[0] = jnp.where(window & (kj >= CHUNK), 0.0, NEG)
    bias_ref[1] = jnp.where(window, 0.0, NEG)


def _softmax_rows(s_ref, bias_ref, first, sink_ref, h, c):
    rows = pl.ds(pl.multiple_of(c * SOFTMAX_ROWS, SOFTMAX_ROWS), SOFTMAX_ROWS)
    in_block = pl.ds(pl.multiple_of((c % (CHUNK // SOFTMAX_ROWS)) * SOFTMAX_ROWS, SOFTMAX_ROWS), SOFTMAX_ROWS)
    sink = sink_ref[0, h * Q_PER_KV + c // (CHUNK // SOFTMAX_ROWS)]
    s = s_ref[rows, :] + bias_ref[first, in_block, :]
    m = jnp.maximum(jnp.max(s, axis=-1, keepdims=True), sink)
    p = jnp.exp(s - m)
    es = jnp.exp(sink - m)
    inv = 1.0 / (jnp.sum(p, axis=-1, keepdims=True) + es)
    return rows, p * inv, es * inv


def _fill_keys(p_ref, gk_ref, krep, vrep, seq):
    mean_k = _head_mean_matrix(KV2)
    reps = [_repeat_matrix(h) for h in range(N_KV)]
    for h in range(N_KV):
        krep[h][0:CHUNK, :] = jnp.zeros((CHUNK, KV_W), BF16)
        vrep[h][0:CHUNK, :] = jnp.zeros((CHUNK, KV_W), BF16)
    rows = min(seq, 4 * CHUNK)

    def piece(j, carry):
        r0 = pl.multiple_of(j * rows, CHUNK)
        nk, _ = _head_rms(p_ref[pl.ds(r0, rows), K_OFF:K_OFF + KV2].astype(F32), mean_k)
        kn = (nk * gk_ref[...]).astype(BF16)
        v = p_ref[pl.ds(r0, rows), V_OFF:V_OFF + KV2].astype(BF16)
        r1 = pl.multiple_of(r0 + CHUNK, CHUNK)
        for h in range(N_KV):
            krep[h][pl.ds(r1, rows), :] = _dot(kn, reps[h]).astype(BF16)
            vrep[h][pl.ds(r1, rows), :] = _dot(v, reps[h]).astype(BF16)
        return carry

    lax.fori_loop(0, seq // rows, piece, 0)


def attn_fwd(pqkv, gq_t, gk_t, sinks, *, seq, name, comms=()):
    T = pqkv.shape[0]
    nb = seq // CHUNK

    def body(p_ref, gq_ref, gk_ref, sink_ref, o_ref, k0, k1, v0, v1, bias_ref, s_ref, pr_ref):
        krep, vrep = [k0, k1], [v0, v1]
        _fill_keys(p_ref, gk_ref, krep, vrep, seq)
        _fill_mask_bias(bias_ref)
        mean_q = _head_mean_matrix(D_B)

        def block(i, carry):
            r0 = pl.multiple_of(i * CHUNK, CHUNK)
            first = jnp.minimum(i, 1)
            nq, _ = _head_rms(p_ref[pl.ds(r0, CHUNK), 0:D_B].astype(F32), mean_q)
            qn = nq * (gq_ref[...] * ATT_SCALE)
            for h in range(N_KV):
                qs = _stack_heads(qn[:, h * KV_W:(h + 1) * KV_W])
                s_ref[...] = _dg(qs, krep[h][pl.ds(r0, 2 * CHUNK), :], NT)

                def rows_of(c, carry2):
                    rows, probs, _ = _softmax_rows(s_ref, bias_ref, first, sink_ref, h, c)
                    pr_ref[rows, :] = probs.astype(BF16)
                    return carry2

                lax.fori_loop(0, STACK // SOFTMAX_ROWS, rows_of, 0, unroll=True)
                out = _unstack_heads(_dot(pr_ref[...], vrep[h][pl.ds(r0, 2 * CHUNK), :]))
                o_ref[pl.ds(r0, CHUNK), h * KV_W:(h + 1) * KV_W] = out.astype(BF16)
            return carry

        lax.fori_loop(0, nb, block, 0)

    return _call(
        body, name=name, grid=(T // seq,),
        in_specs=[pl.BlockSpec((seq, QKV_W), lambda b: (b, 0)), pl.BlockSpec(gq_t.shape, lambda b: (0, 0)),
                  pl.BlockSpec(gk_t.shape, lambda b: (0, 0)), pl.BlockSpec(memory_space=pltpu.SMEM)],
        out_specs=[pl.BlockSpec((seq, D_B), lambda b: (b, 0))], out_shape=[jax.ShapeDtypeStruct((T, D_B), BF16)],
        scratch_shapes=[pltpu.VMEM((seq + CHUNK, KV_W), BF16)] * 4
        + [pltpu.VMEM((2, CHUNK, 2 * CHUNK), F32), pltpu.VMEM((STACK, 2 * CHUNK), F32), pltpu.VMEM((STACK, 2 * CHUNK), BF16)],
        args=[pqkv, gq_t, gk_t, sinks], comms=comms)[0]


def mixer_out_fwd(sgu, att, pg, hin, ga, wa, wb, wo, norm, *, seq, tm, name, comms=()):
    T, D = hin.shape
    B, tps = T // seq, seq // tm

    def body(s_ref, t_ref, pg_ref, h_ref, ga_ref, wa_ref, wb_ref, wo_ref, g_ref, sc_ref, sh_ref,
             ya_ref, yb_ref, mg_ref, m_ref, ho_ref, xn_ref):
        ya = _dot(s_ref[...], wa_ref[...])
        yb = _dot(t_ref[...], wb_ref[...])
        merged = (jax.nn.sigmoid(pg_ref[:, 0:D].astype(F32)) * ya
                  + jax.nn.sigmoid(pg_ref[:, D:2 * D].astype(F32)) * yb)
        mg = merged.astype(BF16)
        m = _dot(mg, wo_ref[...])
        ya_ref[...] = ya.astype(BF16)
        yb_ref[...] = yb.astype(BF16)
        mg_ref[...] = mg
        m_ref[...] = m.astype(BF16)
        y = h_ref[...] + ga_ref[0] * m
        ho_ref[...] = y
        xn_ref[...] = _norm_mod(y, g_ref, sc_ref, sh_ref)

    tile = lambda w: pl.BlockSpec((tm, w), lambda i: (i, 0))
    b16 = jax.ShapeDtypeStruct((T, D), BF16)
    return _call(
        body, name=name, grid=(T // tm,),
        in_specs=[tile(D_A), tile(D_B), tile(2 * D), tile(D), _mod_spec(ga, tps), _resident(wa), _resident(wb),
                  _resident(wo), pl.BlockSpec((1, D), lambda i: (0, 0)), _mod_spec(norm[1], tps), _mod_spec(norm[2], tps)],
        out_specs=[tile(D)] * 6, out_shape=[b16, b16, b16, b16, jax.ShapeDtypeStruct((T, D), F32), b16],
        args=[sgu, att, pg, hin, ga[0], wa, wb, wo, norm[0], norm[1][0], norm[2][0]], comms=comms)


def ffn_bwd_act(dh, f, ga, wd, s, q, *, seq, tm, name, comms=()):
    T, D = dh.shape
    nq, fs, _ = wd.shape
    B, tps = T // seq, seq // tm

    def body(dh_ref, f_ref, ga_ref, wd_ref, s_ref, q_ref, dg_ref, du_ref, df_ref, dga_ref):
        i = pl.program_id(0)
        d = dh_ref[...]
        df = ((0.5 * ga_ref[0]) * d).astype(BF16)
        df_ref[...] = df
        part = 0.5 * jnp.sum(d * f_ref[...].astype(F32), axis=0, keepdims=True)
        for j in range(nq):
            da = _dg(df, wd_ref[j], NT)
            du_ref[j] = (da * s_ref[j].astype(F32)).astype(BF16)
            dg_ref[j] = (da * q_ref[j].astype(F32)).astype(BF16)

        @pl.when(i % tps == 0)
        def _():
            dga_ref[0] = part

        @pl.when(i % tps != 0)
        def _():
            dga_ref[0] += part

    tile = pl.BlockSpec((tm, D), lambda i: (i, 0))
    per_seq = pl.BlockSpec((1, 1, D), lambda i: (i // tps, 0, 0))
    act = pl.BlockSpec((nq, tm, fs), lambda i: (0, i, 0))
    o_shape = jax.ShapeDtypeStruct((nq, T, fs), BF16)
    dg, du, df, dga = _call(
        body, name=name, grid=(T // tm,), in_specs=[tile, tile, _mod_spec(ga, tps), _resident(wd), act, act],
        out_specs=[act, act, tile, per_seq],
        out_shape=[o_shape, o_shape, jax.ShapeDtypeStruct((T, D), BF16), jax.ShapeDtypeStruct((B, 1, D), F32)],
        args=[dh, f, ga[0], wd, s, q], comms=comms)
    return dg, du, df, dga.reshape(B, D)


def mm_tn(pairs, *, tt, name, comms=()):
    n = len(pairs)
    flat = []
    for pair in pairs:
        for a in pair:
            if not any(a is b for b in flat):
                flat.append(a)
    where = lambda a: [k for k, b in enumerate(flat) if a is b][0]
    nq = max([a.shape[0] for a in flat if a.ndim == 3] + [1])
    T = flat[0].shape[-2]
    tt = min(tt, T)
    nt = T // tt
    stacked = [x.ndim == 3 or y.ndim == 3 for x, y in pairs]

    def body(*refs):
        in_refs, o_refs, accs = refs[:len(flat)], refs[len(flat):len(flat) + n], refs[len(flat) + n:]
        t = pl.program_id(1)
        value = lambda a: in_refs[where(a)][0] if a.ndim == 3 else in_refs[where(a)][...]

        def put(j, v):
            if stacked[j]:
                o_refs[j][0] = v.astype(BF16)
            else:
                o_refs[j][...] = v.astype(BF16)

        for j, (x, y) in enumerate(pairs):
            part = _dg(value(x), value(y), TN)
            if nt == 1:
                put(j, part)
                continue

            @pl.when(t == 0)
            def _():
                accs[j][...] = part

            @pl.when(t != 0)
            def _():
                accs[j][...] += part

        if nt > 1:
            @pl.when(t == nt - 1)
            def _():
                for j in range(n):
                    put(j, accs[j][...])

    def spec(a):
        if a.ndim == 3:
            return pl.BlockSpec((1, tt, a.shape[2]), lambda q, t: (q, t, 0))
        return pl.BlockSpec((tt, a.shape[1]), lambda q, t: (t, 0))

    out_specs, out_shape = [], []
    for j, (x, y) in enumerate(pairs):
        K, N = x.shape[-1], y.shape[-1]
        if stacked[j]:
            out_specs.append(pl.BlockSpec((1, K, N), lambda q, t: (q, 0, 0)))
            out_shape.append(jax.ShapeDtypeStruct((nq, K, N), BF16))
        else:
            out_specs.append(pl.BlockSpec((K, N), lambda q, t: (0, 0)))
            out_shape.append(jax.ShapeDtypeStruct((K, N), BF16))
    return _call(
        body, name=name, grid=(nq, nt), in_specs=[spec(a) for a in flat],
        out_specs=out_specs, out_shape=out_shape,
        scratch_shapes=[pltpu.VMEM((x.shape[-1], y.shape[-1]), F32) for x, y in pairs] if nt > 1 else [],
        args=flat, comms=comms)


def dx_norm_bwd(dys, ws, hin, dh_out, g, sc, *, seq, tm, name, comms=()):
    T, D = hin.shape
    B, tps = T // seq, seq // tm
    n = len(dys)
    ranged = [w if isinstance(w, tuple) else (w, 0, w.shape[-2]) for w in ws]
    ws = []
    for w, _, _ in ranged:
        if not any(w is u for u in ws):
            ws.append(w)
    where = [[k for k, u in enumerate(ws) if u is w][0] for w, _, _ in ranged]

    def body(*refs):
        dy_refs, w_refs = refs[:n], refs[n:n + len(ws)]
        h_ref, dho_ref, g_ref, sc_ref, dhi_ref, dsh_ref, dsc_ref, dgn_ref = refs[n + len(ws):]
        i = pl.program_id(0)
        dxn = None
        for j in range(n):
            w_ref, (_, r0, r1) = w_refs[where[j]], ranged[j]
            if dys[j].ndim == 3:
                for q in range(dys[j].shape[0]):
                    part = _dot(dy_refs[j][q], w_ref[q])
                    dxn = part if dxn is None else dxn + part
            else:
                part = _dot(dy_refs[j][...], w_ref[r0:r1, :])
                dxn = part if dxn is None else dxn + part
        x = h_ref[...]
        nx, r = _rms_parts(x)
        gain = g_ref[...]
        one_sc = 1.0 + sc_ref[0]
        dsh = jnp.sum(dxn, axis=0, keepdims=True)
        dsc = jnp.sum(dxn * (nx * gain), axis=0, keepdims=True)
        dgn = jnp.sum(dxn * one_sc * nx, axis=0, keepdims=True)
        dn = dxn * one_sc * gain
        dhi_ref[...] = dho_ref[...] + r * (dn - nx * jnp.mean(dn * nx, axis=-1, keepdims=True))

        @pl.when(i % tps == 0)
        def _():
            dsh_ref[0] = dsh
            dsc_ref[0] = dsc

        @pl.when(i % tps != 0)
        def _():
            dsh_ref[0] += dsh
            dsc_ref[0] += dsc

        @pl.when(i == 0)
        def _():
            dgn_ref[...] = dgn

        @pl.when(i != 0)
        def _():
            dgn_ref[...] += dgn

    def dy_spec(a):
        if a.ndim == 3:
            return pl.BlockSpec((a.shape[0], tm, a.shape[2]), lambda i: (0, i, 0))
        return pl.BlockSpec((tm, a.shape[1]), lambda i: (i, 0))

    tile = pl.BlockSpec((tm, D), lambda i: (i, 0))
    per_seq = pl.BlockSpec((1, 1, D), lambda i: (i // tps, 0, 0))
    row = pl.BlockSpec((1, D), lambda i: (0, 0))
    dhi, dsh, dsc, dgn = _call(
        body, name=name, grid=(T // tm,),
        in_specs=[dy_spec(a) for a in dys] + [_resident(w) for w in ws] + [tile, tile, row, _mod_spec(sc, tps)],
        out_specs=[tile, per_seq, per_seq, row],
        out_shape=[jax.ShapeDtypeStruct((T, D), F32), jax.ShapeDtypeStruct((B, 1, D), F32),
                   jax.ShapeDtypeStruct((B, 1, D), F32), jax.ShapeDtypeStruct((1, D), F32)],
        args=[*dys, *ws, hin, dh_out, g, sc[0]], comms=comms)
    return dhi, dsh.reshape(B, D), dsc.reshape(B, D), dgn


def mixer_out_bwd(dh, m, ga, ya, yb, pg, wa, wb, wo, *, seq, tm, name, comms=()):
    T, D = dh.shape
    B, tps = T // seq, seq // tm

    def body(dh_ref, m_ref, ga_ref, ya_ref, yb_ref, pg_ref, wa_ref, wb_ref, wo_ref,
             dg_ref, dya_ref, dyb_ref, ds_ref, dt_ref, dm_ref, dga_ref):
        i = pl.program_id(0)
        d = dh_ref[...]
        dm = (ga_ref[0] * d).astype(BF16)
        dm_ref[...] = dm
        part = jnp.sum(d * m_ref[...].astype(F32), axis=0, keepdims=True)

        @pl.when(i % tps == 0)
        def _():
            dga_ref[0] = part

        @pl.when(i % tps != 0)
        def _():
            dga_ref[0] += part

        dmg = _dg(dm, wo_ref[...], NT)
        sa = jax.nn.sigmoid(pg_ref[:, 0:D].astype(F32))
        sb = jax.nn.sigmoid(pg_ref[:, D:2 * D].astype(F32))
        dg_ref[:, 0:D] = (dmg * ya_ref[...].astype(F32) * (sa * (1.0 - sa))).astype(BF16)
        dg_ref[:, D:2 * D] = (dmg * yb_ref[...].astype(F32) * (sb * (1.0 - sb))).astype(BF16)
        dya = (dmg * sa).astype(BF16)
        dyb = (dmg * sb).astype(BF16)
        dya_ref[...] = dya
        dyb_ref[...] = dyb
        ds_ref[...] = _dg(dya, wa_ref[...], NT).astype(BF16)
        dt_ref[...] = _dg(dyb, wb_ref[...], NT).astype(BF16)

    tile = lambda w: pl.BlockSpec((tm, w), lambda i: (i, 0))
    per_seq = pl.BlockSpec((1, 1, D), lambda i: (i // tps, 0, 0))
    dg, dya, dyb, ds, dt, dm, dga = _call(
        body, name=name, grid=(T // tm,),
        in_specs=[tile(D), tile(D), _mod_spec(ga, tps), tile(D), tile(D), tile(2 * D), _resident(wa), _resident(wb),
                  _resident(wo)],
        out_specs=[tile(2 * D), tile(D), tile(D), tile(D_A), tile(D_B), tile(D), per_seq],
        out_shape=[jax.ShapeDtypeStruct((T, 2 * D), BF16), jax.ShapeDtypeStruct((T, D), BF16),
                   jax.ShapeDtypeStruct((T, D), BF16), jax.ShapeDtypeStruct((T, D_A), BF16),
                   jax.ShapeDtypeStruct((T, D_B), BF16), jax.ShapeDtypeStruct((T, D), BF16),
                   jax.ShapeDtypeStruct((B, 1, D), F32)],
        args=[dh, m, ga[0], ya, yb, pg, wa, wb, wo], comms=comms)
    return dg, dya, dyb, ds, dt, dm, dga.reshape(B, D)


def sgu_bwd(puv, dsgu, gln, bln, ws, bs_t, *, cpb, name, comms=()):
    T = puv.shape[0]
    gd = D_A // N_GROUPS
    tm = cpb * CHUNK

    def body(p_ref, ds_ref, gln_ref, bln_ref, ws_ref, bs_ref, d_ref, dws_ref, dz_ref, dgl_ref, dbl_ref):
        i = pl.program_id(0)
        causal = _causal()
        wf = [jnp.where(causal, ws_ref[g], 0.0) for g in range(N_GROUPS)]
        wm = [w.astype(BF16) for w in wf]
        wt = [w.T.astype(BF16) for w in wf]
        dws = [jnp.zeros((CHUNK, CHUNK), F32) for _ in range(N_GROUPS)]
        dzs = jnp.zeros((CHUNK, D_A), F32)
        dgl = jnp.zeros((1, D_A), F32)
        dbl = jnp.zeros((1, D_A), F32)
        for j in range(cpb):
            rows = slice(j * CHUNK, (j + 1) * CHUNK)
            au = p_ref[rows, 0:D_A].astype(F32)
            av = p_ref[rows, D_A:2 * D_A].astype(F32)
            u, tu = _gelu_parts(au)
            vv, tv = _gelu_parts(av)
            vhat, rstd = _layer_norm_parts(vv)
            vn = (vhat * gln_ref[...] + bln_ref[...]).astype(BF16)
            dsg = ds_ref[rows, :].astype(F32)
            dz = dsg * u
            dzb = dz.astype(BF16)
            zs, dvns = [], []
            for g in range(N_GROUPS):
                cols = slice(g * gd, (g + 1) * gd)
                zs.append(_dot(wm[g], vn[:, cols]) + bs_ref[:, g:g + 1])
                dws[g] = dws[g] + _dg(dzb[:, cols], vn[:, cols], NT)
                dvns.append(_dot(wt[g], dzb[:, cols]))
            z = jnp.concatenate(zs, axis=1)
            dvn = jnp.concatenate(dvns, axis=1)
            d_ref[rows, 0:D_A] = (dsg * z * _dgelu(au, tu)).astype(BF16)
            dvh = dvn * gln_ref[...]
            dvv = rstd * (dvh - jnp.mean(dvh, axis=-1, keepdims=True)
                          - vhat * jnp.mean(dvh * vhat, axis=-1, keepdims=True))
            d_ref[rows, D_A:2 * D_A] = (dvv * _dgelu(av, tv)).astype(BF16)
            dzs = dzs + dz
            dgl = dgl + jnp.sum(dvn * vhat, axis=0, keepdims=True)
            dbl = dbl + jnp.sum(dvn, axis=0, keepdims=True)

        @pl.when(i == 0)
        def _():
            for g in range(N_GROUPS):
                dws_ref[g] = jnp.where(causal, dws[g], 0.0)
            dz_ref[...] = dzs
            dgl_ref[...] = dgl
            dbl_ref[...] = dbl

        @pl.when(i != 0)
        def _():
            for g in range(N_GROUPS):
                dws_ref[g] += jnp.where(causal, dws[g], 0.0)
            dz_ref[...] += dzs
            dgl_ref[...] += dgl
            dbl_ref[...] += dbl

    full = lambda a: pl.BlockSpec(a.shape, lambda i: (0,) * a.ndim)
    acc = lambda s: pl.BlockSpec(s, lambda i: (0,) * len(s))
    return _call(
        body, name=name, grid=(T // tm,),
        in_specs=[pl.BlockSpec((tm, 2 * D_A), lambda i: (i, 0)), pl.BlockSpec((tm, D_A), lambda i: (i, 0)),
                  full(gln), full(bln), full(ws), full(bs_t)],
        out_specs=[pl.BlockSpec((tm, 2 * D_A), lambda i: (i, 0)), acc((N_GROUPS, CHUNK, CHUNK)), acc((CHUNK, D_A)),
                   acc((1, D_A)), acc((1, D_A))],
        out_shape=[jax.ShapeDtypeStruct((T, 2 * D_A), BF16), jax.ShapeDtypeStruct((N_GROUPS, CHUNK, CHUNK), F32),
                   jax.ShapeDtypeStruct((CHUNK, D_A), F32), jax.ShapeDtypeStruct((1, D_A), F32),
                   jax.ShapeDtypeStruct((1, D_A), F32)],
        args=[puv, dsgu, gln, bln, ws, bs_t], comms=comms)


def attn_bwd(pqkv, datt, gq_t, gk_t, sinks, *, seq, name, comms=()):
    T = pqkv.shape[0]
    nb = seq // CHUNK

    def body(p_ref, do_ref, gq_ref, gk_ref, sink_ref, d_ref, dgq_ref, dgk_ref, dsk_ref,
             k0, k1, v0, v1, dk0, dk1, dv0, dv1, dgq_s, dsk_s, bias_ref, s_ref, dp_ref, pr_ref, ds_ref):
        b = pl.program_id(0)
        krep, vrep, dkacc, dvacc = [k0, k1], [v0, v1], [dk0, dk1], [dv0, dv1]
        _fill_keys(p_ref, gk_ref, krep, vrep, seq)
        _fill_mask_bias(bias_ref)
        for h in range(N_KV):
            dkacc[h][...] = jnp.zeros_like(dkacc[h])
            dvacc[h][...] = jnp.zeros_like(dvacc[h])
        dgq_s[...] = jnp.zeros_like(dgq_s)
        dsk_s[...] = jnp.zeros_like(dsk_s)
        mean_q = _head_mean_matrix(D_B)
        lane = _iota((1, 128), 1)

        def block(i, carry):
            r0 = pl.multiple_of(i * CHUNK, CHUNK)
            first = jnp.minimum(i, 1)
            nq, rq = _head_rms(p_ref[pl.ds(r0, CHUNK), 0:D_B].astype(F32), mean_q)
            qn = nq * (gq_ref[...] * ATT_SCALE)
            dqn_parts = []
            for h in range(N_KV):
                cols = slice(h * KV_W, (h + 1) * KV_W)
                qs = _stack_heads(qn[:, cols])
                kk = krep[h][pl.ds(r0, 2 * CHUNK), :]
                dos = _stack_heads(do_ref[pl.ds(r0, CHUNK), cols].astype(F32))
                s_ref[...] = _dg(qs, kk, NT)
                dp_ref[...] = _dg(dos, vrep[h][pl.ds(r0, 2 * CHUNK), :], NT)

                def rows_of(c, carry2):
                    rows, probs, psink = _softmax_rows(s_ref, bias_ref, first, sink_ref, h, c)
                    dp = dp_ref[rows, :]
                    dr = jnp.sum(probs * dp, axis=-1, keepdims=True)
                    pr_ref[rows, :] = probs.astype(BF16)
                    ds_ref[rows, :] = (probs * (dp - dr)).astype(BF16)
                    head = h * Q_PER_KV + c // (CHUNK // SOFTMAX_ROWS)
                    dsk_s[...] += jnp.where(lane == head, -jnp.sum(psink * dr), 0.0)
                    return carry2

                lax.fori_loop(0, STACK // SOFTMAX_ROWS, rows_of, 0, unroll=True)
                dqn_parts.append(_unstack_heads(_dot(ds_ref[...], kk)))
                dkacc[h][pl.ds(r0, 2 * CHUNK), :] += _dg(ds_ref[...], qs, TN)
                dvacc[h][pl.ds(r0, 2 * CHUNK), :] += _dg(pr_ref[...], dos, TN)
            dqn = jnp.concatenate(dqn_parts, axis=1) * ATT_SCALE
            dn = dqn * gq_ref[...]
            d_ref[pl.ds(r0, CHUNK), 0:D_B] = (rq * (dn - nq * _dot_split(dn * nq, mean_q))).astype(BF16)
            dgq_s[...] += jnp.sum(dqn * nq, axis=0, keepdims=True)
            return carry

        lax.fori_loop(0, nb, block, 0)

        mean_k = _head_mean_matrix(KV2)
        folds = [_fold_matrix(h) for h in range(N_KV)]
        rows = min(seq, 4 * CHUNK)

        def piece(j, dgk):
            r0 = pl.multiple_of(j * rows, CHUNK)
            r1 = pl.multiple_of(r0 + CHUNK, CHUNK)
            dkn = _dot_split(dkacc[0][pl.ds(r1, rows), :], folds[0]) + _dot_split(dkacc[1][pl.ds(r1, rows), :], folds[1])
            dv = _dot_split(dvacc[0][pl.ds(r1, rows), :], folds[0]) + _dot_split(dvacc[1][pl.ds(r1, rows), :], folds[1])
            nk, rk = _head_rms(p_ref[pl.ds(r0, rows), K_OFF:K_OFF + KV2].astype(F32), mean_k)
            dn = dkn * gk_ref[...]
            d_ref[pl.ds(r0, rows), K_OFF:K_OFF + KV2] = (rk * (dn - nk * _dot_split(dn * nk, mean_k))).astype(BF16)
            d_ref[pl.ds(r0, rows), V_OFF:V_OFF + KV2] = dv.astype(BF16)
            return dgk + jnp.sum(dkn * nk, axis=0, keepdims=True)

        dgk = lax.fori_loop(0, seq // rows, piece, jnp.zeros((1, KV2), F32))

        @pl.when(b == 0)
        def _():
            dgq_ref[...] = dgq_s[...]
            dgk_ref[...] = dgk
            dsk_ref[...] = jnp.broadcast_to(dsk_s[...], dsk_ref.shape)

        @pl.when(b != 0)
        def _():
            dgq_ref[...] += dgq_s[...]
            dgk_ref[...] += dgk
            dsk_ref[...] += jnp.broadcast_to(dsk_s[...], dsk_ref.shape)

    acc = lambda s: pl.BlockSpec(s, lambda b: (0, 0))
    return _call(
        body, name=name, grid=(T // seq,),
        in_specs=[pl.BlockSpec((seq, QKV_W), lambda b: (b, 0)), pl.BlockSpec((seq, D_B), lambda b: (b, 0)),
                  pl.BlockSpec(gq_t.shape, lambda b: (0, 0)), pl.BlockSpec(gk_t.shape, lambda b: (0, 0)),
                  pl.BlockSpec(memory_space=pltpu.SMEM)],
        out_specs=[pl.BlockSpec((seq, QKV_W), lambda b: (b, 0)), acc((1, D_B)), acc((1, KV2)), acc((8, 128))],
        out_shape=[jax.ShapeDtypeStruct((T, QKV_W), BF16), jax.ShapeDtypeStruct((1, D_B), F32),
                   jax.ShapeDtypeStruct((1, KV2), F32), jax.ShapeDtypeStruct((8, 128), F32)],
        scratch_shapes=[pltpu.VMEM((seq + CHUNK, KV_W), BF16)] * 4 + [pltpu.VMEM((seq + CHUNK, KV_W), F32)] * 4
        + [pltpu.VMEM((1, D_B), F32), pltpu.VMEM((1, 128), F32), pltpu.VMEM((2, CHUNK, 2 * CHUNK), F32)]
        + [pltpu.VMEM((STACK, 2 * CHUNK), F32)] * 2 + [pltpu.VMEM((STACK, 2 * CHUNK), BF16)] * 2,
        args=[pqkv, datt, gq_t, gk_t, sinks], comms=comms)


def ada_fwd(c_all, w, b, *, name):
    nb, D = c_all.shape
    N = w.shape[1]
    tn = N // 3

    def body(c_ref, w_ref, b_ref, o_ref):
        c = c_ref[...]
        cond = (c * jax.nn.sigmoid(c)).astype(BF16)
        o_ref[...] = _dot(cond, w_ref[...].astype(BF16)) + b_ref[...]

    return _call(
        body, name=name, grid=(3,),
        in_specs=[pl.BlockSpec((nb, D), lambda j: (0, 0)), pl.BlockSpec((D, tn), lambda j: (0, j)),
                  pl.BlockSpec((1, tn), lambda j: (0, j))],
        out_specs=[pl.BlockSpec((nb, tn), lambda j: (0, j))], out_shape=[jax.ShapeDtypeStruct((nb, N), F32)],
        args=[c_all, w, b])[0]


def ada_bwd(c_all, dmods_all, dmods_mine, gain_rows, *, name, comms=()):
    nb, D = c_all.shape
    NA = dmods_all.shape[1]
    N = dmods_mine.shape[1]
    tn = N // 3

    def body(c_ref, da_ref, dm_ref, gr_ref, db_ref, dw_ref, dgn_ref):
        c = c_ref[...]
        cond = (c * jax.nn.sigmoid(c)).astype(BF16)
        dw_ref[...] = _dg(cond, dm_ref[...].astype(BF16), TN)
        db_ref[...] = jnp.sum(da_ref[...], axis=0, keepdims=True)
        total = gr_ref[0:1, :]
        for d in range(1, N_DEV):
            total = total + gr_ref[d:d + 1, :]
        dgn_ref[...] = total

    return _call(
        body, name=name, grid=(3,),
        in_specs=[pl.BlockSpec((nb, D), lambda j: (0, 0)), pl.BlockSpec((nb, NA), lambda j: (0, 0)),
                  pl.BlockSpec((nb, tn), lambda j: (0, j)), pl.BlockSpec((N_DEV, D), lambda j: (0, 0))],
        out_specs=[pl.BlockSpec((1, NA), lambda j: (0, 0)), pl.BlockSpec((D, tn), lambda j: (0, j)),
                   pl.BlockSpec((1, D), lambda j: (0, 0))],
        out_shape=[jax.ShapeDtypeStruct((1, NA), F32), jax.ShapeDtypeStruct((D, N), F32),
                   jax.ShapeDtypeStruct((1, D), F32)],
        args=[c_all, dmods_all, dmods_mine, gain_rows], comms=comms)


def adamw(items, *, steps, name, comms=()):
    n = len(items)
    c1 = 1.0 / (1.0 - ADAM_B1 ** ADAM_STEP)
    c2 = 1.0 / (1.0 - ADAM_B2 ** ADAM_STEP)

    def body(*refs):
        for j in range(n):
            w_ref, g_ref, m_ref, v_ref = refs[4 * j:4 * j + 4]
            go_ref, d_ref, mo_ref, vo_ref = refs[4 * n + 4 * j:4 * n + 4 * j + 4]
            gg = g_ref[...]
            mn = ADAM_B1 * m_ref[...] + (1.0 - ADAM_B1) * gg
            vn = ADAM_B2 * v_ref[...] + (1.0 - ADAM_B2) * (gg * gg)
            go_ref[...] = gg
            mo_ref[...] = mn
            vo_ref[...] = vn
            d_ref[...] = -ADAM_LR * ((mn * c1) / (jnp.sqrt(vn * c2) + ADAM_EPS) + ADAM_WD * w_ref[...])

    in_specs, out_specs, out_shape, args = [], [], [], []
    for item in items:
        R, C = item[0].shape
        blk = pl.BlockSpec((R // steps, C), lambda i: (i, 0))
        in_specs += [blk] * 4
        out_specs += [blk] * 4
        out_shape += [jax.ShapeDtypeStruct((R, C), F32)] * 4
        args += list(item)
    res = _call(body, name=name, grid=(steps,), in_specs=in_specs, out_specs=out_specs, out_shape=out_shape, args=args,
                comms=comms)
    return [tuple(res[4 * j:4 * j + 4]) for j in range(n)]


def _place():
    return lax.axis_index("x"), lax.axis_index("y"), lax.axis_index("c")


def _flip(v, bit):
    return 1 - v if bit else v


def _other_chips(mx, my):
    return [(_flip(mx, k & 2), _flip(my, k & 1)) for k in range(1, N_QUAD)]


def _remote(src, dst, send_sem, recv_sem, peer):
    return pltpu.make_async_remote_copy(src_ref=src, dst_ref=dst, send_sem=send_sem, recv_sem=recv_sem,
                                        device_id=peer, device_id_type=MESH)


def ag8_comm(x):
    def copies(srcs, lands, ss, rs, only_sends=False):
        mx, my, mc = _place()
        me = 4 * mx + 2 * my + mc
        local = pltpu.make_async_copy(srcs[0], lands[0].at[me], ss.at[N_DEV - 1])
        sends, recvs = [], []
        for k in range(1, N_DEV):
            peer = (_flip(mx, k & 4), _flip(my, k & 2), _flip(mc, k & 1))
            sends.append(_remote(srcs[0], lands[0].at[me], ss.at[k - 1], rs.at[k - 1], peer))
            if not only_sends:
                recvs.append(_remote(srcs[0], lands[0].at[4 * peer[0] + 2 * peer[1] + peer[2]], ss.at[k - 1], rs.at[k - 1], peer))
        return local, sends, recvs

    def start(srcs, bufs, lands, ss, rs):
        local, sends, _ = copies(srcs, lands, ss, rs, only_sends=True)
        local.start()
        for cp in sends:
            cp.start()

    def finish(srcs, bufs, lands, ss, rs):
        local, sends, recvs = copies(srcs, lands, ss, rs)
        for cp in recvs:
            cp.wait_recv()
        for cp in sends:
            cp.wait_send()
        local.wait()

    return Comm(srcs=[x], land_shapes=[jax.ShapeDtypeStruct((N_DEV,) + x.shape, x.dtype)], n_sems=N_DEV,
                start=start, finish=finish)


def sum8(stacked, *, name):
    _, r, n = stacked.shape

    def body(s_ref, o_ref):
        total = s_ref[0]
        for d in range(1, N_DEV):
            total = total + s_ref[d]
        o_ref[...] = total

    return _call(body, name=name, grid=(), in_specs=[pl.BlockSpec(memory_space=pltpu.VMEM)],
                 out_specs=[pl.BlockSpec(memory_space=pltpu.VMEM)], out_shape=[jax.ShapeDtypeStruct((r, n), F32)],
                 args=[stacked])[0]


def cast_into_stacks(ws, quad, *, name, comms=()):
    steps = 4

    def body(q_ref, *refs):
        for w_ref, o_ref in zip(refs[:len(ws)], refs[len(ws):]):
            o_ref[0] = w_ref[...].astype(BF16)

    return _call(
        body, name=name, grid=(steps,), prefetch=[quad],
        in_specs=[pl.BlockSpec((w.shape[0] // steps, w.shape[1]), lambda i, q: (i, 0)) for w in ws],
        out_specs=[pl.BlockSpec((1, w.shape[0] // steps, w.shape[1]), lambda i, q: (q[0], i, 0)) for w in ws],
        out_shape=[jax.ShapeDtypeStruct((N_QUAD,) + w.shape, BF16) for w in ws], args=list(ws), comms=comms)


def gather_comm(stacks):
    n = len(stacks)

    def copies(bufs, ss, rs, only_sends=False):
        mx, my, mc = _place()
        q = 2 * mx + my
        sibling = (mx, my, 1 - mc)
        ici_send, ici_recv, fwd_send, fwd_recv = [], [], [], []
        for p in range(n):
            hr = stacks[p].shape[1] // 2
            mine, other = pl.ds(mc * hr, hr), pl.ds((1 - mc) * hr, hr)
            for k, chip in enumerate(_other_chips(mx, my)):
                qk = 2 * chip[0] + chip[1]
                peer = (chip[0], chip[1], mc)
                own, landed, theirs = bufs[p].at[q, mine, :], bufs[p].at[qk, mine, :], bufs[p].at[qk, other, :]
                ici_send.append(_remote(own, own, ss.at[6 * p + k], rs.at[6 * p + k], peer))
                if only_sends:
                    continue
                ici_recv.append(_remote(own, landed, ss.at[6 * p + k], rs.at[6 * p + k], peer))
                fwd_send.append(_remote(landed, landed, ss.at[6 * p + 3 + k], rs.at[6 * p + 3 + k], sibling))
                fwd_recv.append(_remote(theirs, theirs, ss.at[6 * p + 3 + k], rs.at[6 * p + 3 + k], sibling))
        return ici_send, ici_recv, fwd_send, fwd_recv

    def start(srcs, bufs, lands, ss, rs):
        for cp in copies(bufs, ss, rs, only_sends=True)[0]:
            cp.start()

    def finish(srcs, bufs, lands, ss, rs):
        ici_send, ici_recv, fwd_send, fwd_recv = copies(bufs, ss, rs)
        for arrived, onward in zip(ici_recv, fwd_send):
            arrived.wait_recv()
            onward.start()
        for cp in fwd_recv:
            cp.wait_recv()
        for cp in ici_send + fwd_send:
            cp.wait_send()

    return Comm(bufs=stacks, n_sems=6 * n, start=start, finish=finish)


def rs_pair_comm(gs):
    n = len(gs)

    def copies(srcs, lands, ss, rs):
        mx, my, mc = _place()
        out = []
        for p in range(n):
            hr = gs[p].shape[1] // 2
            out.append(_remote(srcs[p].at[:, pl.ds((1 - mc) * hr, hr), :], lands[p], ss.at[p], rs.at[p], (mx, my, 1 - mc)))
        return out

    def start(srcs, bufs, lands, ss, rs):
        for cp in copies(srcs, lands, ss, rs):
            cp.start()

    def finish(srcs, bufs, lands, ss, rs):
        for cp in copies(srcs, lands, ss, rs):
            cp.wait()

    return Comm(srcs=gs, land_shapes=[jax.ShapeDtypeStruct((g.shape[0], g.shape[1] // 2, g.shape[2]), g.dtype) for g in gs],
                n_sems=n, start=start, finish=finish)


def rs_chip_comm(ss_):
    n = len(ss_)

    def copies(srcs, lands, ss, rs):
        mx, my, mc = _place()
        out = []
        for p in range(n):
            for k, chip in enumerate(_other_chips(mx, my)):
                out.append(_remote(srcs[p].at[2 * chip[0] + chip[1]], lands[p].at[k], ss.at[3 * p + k], rs.at[3 * p + k],
                                   (chip[0], chip[1], mc)))
        return out

    def start(srcs, bufs, lands, ss, rs):
        for cp in copies(srcs, lands, ss, rs):
            cp.start()

    def finish(srcs, bufs, lands, ss, rs):
        for cp in copies(srcs, lands, ss, rs):
            cp.wait()

    return Comm(srcs=ss_, land_shapes=[jax.ShapeDtypeStruct((N_QUAD - 1,) + s.shape[1:], s.dtype) for s in ss_],
                n_sems=3 * n, start=start, finish=finish)


def rs_share_comm(fulls):
    n = len(fulls)

    def copies(bufs, ss, rs, only_sends=False):
        mx, my, mc = _place()
        send, recv = [], []
        for p in range(n):
            hr = fulls[p].shape[0] // 2
            mine, other = bufs[p].at[pl.ds(mc * hr, hr), :], bufs[p].at[pl.ds((1 - mc) * hr, hr), :]
            send.append(_remote(mine, mine, ss.at[p], rs.at[p], (mx, my, 1 - mc)))
            if not only_sends:
                recv.append(_remote(other, other, ss.at[p], rs.at[p], (mx, my, 1 - mc)))
        return send, recv

    def start(srcs, bufs, lands, ss, rs):
        for cp in copies(bufs, ss, rs, only_sends=True)[0]:
            cp.start()

    def finish(srcs, bufs, lands, ss, rs):
        send, recv = copies(bufs, ss, rs)
        for cp in recv:
            cp.wait_recv()
        for cp in send:
            cp.wait_send()

    return Comm(bufs=fulls, n_sems=n, start=start, finish=finish)


def pair_sums(gs, recvs, mc, *, name):
    n = len(gs)

    def body(s_ref, *refs):
        for p in range(n):
            g_ref, r_ref, o_ref = refs[2 * p], refs[2 * p + 1], refs[2 * n + p]
            o_ref[...] = (g_ref[...].astype(F32) + r_ref[...].astype(F32)).astype(BF16)

    in_specs, out_specs, out_shape, args = [], [], [], []
    for g, r in zip(gs, recvs):
        nq, hr, C = r.shape
        in_specs += [pl.BlockSpec((1, hr, C), lambda q, s: (q, s[0], 0)), pl.BlockSpec((1, hr, C), lambda q, s: (q, 0, 0))]
        out_specs.append(pl.BlockSpec((1, hr, C), lambda q, s: (q, 0, 0)))
        out_shape.append(jax.ShapeDtypeStruct((nq, hr, C), BF16))
        args += [g, r]
    return _call(body, name=name, grid=(N_QUAD,), in_specs=in_specs, out_specs=out_specs, out_shape=out_shape,
                 args=args, prefetch=[mc])


def chip_sums(ss, recvs, quad_mc, *, name):
    n = len(ss)
    steps = 2

    def body(q_ref, *refs):
        for p in range(n):
            s_ref, r_ref, o_ref = refs[2 * p], refs[2 * p + 1], refs[2 * n + p]
            total = s_ref[0].astype(F32)
            for k in range(N_QUAD - 1):
                total = total + r_ref[k].astype(F32)
            o_ref[...] = total

    in_specs, out_specs, out_shape, args = [], [], [], []
    for s, r in zip(ss, recvs):
        _, hr, C = s.shape
        rb = hr // steps
        in_specs += [pl.BlockSpec((1, rb, C), lambda i, q: (q[0], i, 0)),
                     pl.BlockSpec((N_QUAD - 1, rb, C), lambda i, q: (0, i, 0))]
        out_specs.append(pl.BlockSpec((rb, C), lambda i, q: (q[1] * steps + i, 0)))
        out_shape.append(jax.ShapeDtypeStruct((2 * hr, C), F32))
        args += [s, r]
    return _call(body, name=name, grid=(steps,), in_specs=in_specs, out_specs=out_specs, out_shape=out_shape,
                 args=args, prefetch=[quad_mc])


def _stack_cols(w_full):
    K, N = w_full.shape
    return w_full.reshape(K, N_QUAD, N // N_QUAD).transpose(1, 0, 2)


def _unstack_cols(w4):
    nq, K, n = w4.shape
    return w4.transpose(1, 0, 2).reshape(K, nq * n)


def kernel(x, c, w_ada, b_ada, g_norm1, ffn1_w_gate, ffn1_w_up, ffn1_w_down, g_norm2, w_in, g_sgu_ln, b_sgu_ln, w_spatial, b_spatial, g_q, g_k, attn_sinks, w_branch_a, w_branch_b, w_out, g_norm3, ffn2_w_gate, ffn2_w_up, ffn2_w_down, loss_target, m_w_ada, m_b_ada, m_g_norm1, m_ffn1_w_gate, m_ffn1_w_up, m_ffn1_w_down, m_g_norm2, m_w_in, m_g_sgu_ln, m_b_sgu_ln, m_w_spatial, m_b_spatial, m_g_q, m_g_k, m_attn_sinks, m_w_branch_a, m_w_branch_b, m_w_out, m_g_norm3, m_ffn2_w_gate, m_ffn2_w_up, m_ffn2_w_down, v_w_ada, v_b_ada, v_g_norm1, v_ffn1_w_gate, v_ffn1_w_up, v_ffn1_w_down, v_g_norm2, v_w_in, v_g_sgu_ln, v_b_sgu_ln, v_w_spatial, v_b_spatial, v_g_q, v_g_k, v_attn_sinks, v_w_branch_a, v_w_branch_b, v_w_out, v_g_norm3, v_ffn2_w_gate, v_ffn2_w_up, v_ffn2_w_down):
    B, S, D = x.shape
    T = B * S
    n_mod = b_ada.shape[1] // D
    mx, my, mc = _place()
    quad = 2 * mx + my
    mc_arr = jnp.reshape(mc, (1,)).astype(jnp.int32)
    quad_arr = jnp.reshape(quad, (1,)).astype(jnp.int32)
    quad_mc = jnp.stack([quad, mc]).astype(jnp.int32)
    tm = min(512, S)
    tm_big = min(1024, S)
    tt_half = min(2048, T)
    cpb = min(4, S // CHUNK)

    xt = x.reshape(T, D)
    tgt = loss_target.reshape(T, D)

    tr = lambda a: jnp.swapaxes(a, 1, 2)
    stack_wg1, stack_wu1 = cast_into_stacks([tr(ffn1_w_gate)[0], tr(ffn1_w_up)[0]], quad_arr, name="stack_gu1")
    gather_wg1, gather_wu1 = gather_comm([stack_wg1]), gather_comm([stack_wu1])
    later = [ffn1_w_down, tr(w_in), w_branch_a, w_branch_b, w_out, tr(ffn2_w_gate), tr(ffn2_w_up), ffn2_w_down]
    gather_cond = ag8_comm(c)
    stacks = cast_into_stacks([w[0] for w in later[0:4]], quad_arr, name="stack_a", comms=[gather_wg1, gather_cond])
    (wg1,) = gather_wg1.bufs_out
    c_all = gather_cond.lands[0].reshape(N_DEV * B, D)
    n_ada = w_ada.shape[2]
    b_mine = lax.dynamic_slice(b_ada, (0, quad * n_ada), (1, n_ada))
    mods_part = ada_fwd(c_all, w_ada[0], b_mine, name="ada_fwd")
    gather_mods = ag8_comm(mods_part)
    stacks += cast_into_stacks([w[0] for w in later[4:8]], quad_arr, name="stack_b", comms=[gather_wu1, gather_mods])
    (wu1,), (mods_parts,) = gather_wu1.bufs_out, gather_mods.lands
    gather_wd1, gather_win = gather_comm([stacks[0]]), gather_comm([stacks[1]])
    gather_abo = gather_comm(stacks[2:5])
    gather_wg3, gather_wu3, gather_wd3 = gather_comm([stacks[5]]), gather_comm([stacks[6]]), gather_comm([stacks[7]])
    mods = jnp.concatenate([mods_parts[2 * j] for j in range(N_QUAD)], axis=1)
    me = 4 * mx + 2 * my + mc
    mods = lax.dynamic_slice(mods, (me * B, 0), (B, n_mod * D))
    mods = mods.reshape(B, n_mod, 1, D)
    sh1, sc1, ga1, sh2, sc2, ga2, sh3, sc3, ga3 = [(mods, j) for j in range(n_mod)]
    bs_t = b_spatial[0].T
    ws = w_spatial[0]

    xn1 = norm_mod_fwd(xt, g_norm1, sc1, sh1, seq=S, tm=tm, name="norm1")
    g1, u1, a1 = ffn_up(xn1, wg1, wu1, tm=tm_big, name="ffn1_up", comms=[gather_wd1, gather_abo])
    (wd1,), (wa4, wb4, wo4) = gather_wd1.bufs_out, gather_abo.bufs_out
    wa, wb = _unstack_cols(wa4), _unstack_cols(wb4)
    wo = wo4.reshape(D, D)
    h1, f1, xn2 = ffn_down(a1, wd1, xt, ga1, norm=(g_norm2, sc2, sh2), seq=S, tm=tm, name="ffn1_down",
                           comms=[gather_win])
    (win4,) = gather_win.bufs_out
    win = win4.reshape(-1, D)
    groups = [(0, 2 * D_A), (2 * D_A, 2 * D_A + QKV_W), (2 * D_A + QKV_W, win.shape[0])]
    puv, pqkv, pgt = proj_fwd(xn2, win, groups, tm=tm, name="proj", comms=[gather_wg3])
    (wg3,) = gather_wg3.bufs_out
    sgu = sgu_fwd(puv, g_sgu_ln, b_sgu_ln, ws, bs_t, cpb=cpb, name="sgu_fwd")
    gq_t, gk_t = jnp.tile(g_q, (1, N_Q)), jnp.tile(g_k, (1, N_KV))
    att = attn_fwd(pqkv, gq_t, gk_t, attn_sinks, seq=S, name="attn_fwd", comms=[gather_wu3])
    (wu3,) = gather_wu3.bufs_out
    ya, yb, merged, mm, h2, xn3 = mixer_out_fwd(sgu, att, pgt, h1, ga2, wa, wb, wo, (g_norm3, sc3, sh3), seq=S,
                                                tm=tm, name="mixer_out", comms=[gather_wd3])
    (wd3,) = gather_wd3.bufs_out
    g3, u3, a3 = ffn_up(xn3, wg3, wu3, tm=tm_big, name="ffn2_up")
    dy, f3, lparts = ffn_down(a3, wd3, h2, ga3, target=tgt, seq=S, tm=tm, name="ffn2_down_loss")
    loss_part = jnp.sum(lparts[:, 0, 0])

    def sums_of(pair, tag):
        return pair_sums(pair.srcs, pair.lands, mc_arr, name=f"pair_sum_{tag}")

    def halves_of(*chips_and_tag):
        *chips, tag = chips_and_tag
        return chip_sums([s for ch in chips for s in ch.srcs], [r for ch in chips for r in ch.lands], quad_mc,
                         name=f"chip_sum_{tag}")

    dg3, du3, df3, dga3 = ffn_bwd_act(dy, f3, ga3, wd3, g3, u3, seq=S, tm=tm, name="ffn2_act_bwd")
    (dwd3,) = mm_tn([(a3, df3)], tt=T, name="ffn2_dwd")
    pair_wd3 = rs_pair_comm([dwd3])
    dwg3, dwu3 = mm_tn([(dg3, xn3), (du3, xn3)], tt=tt_half, name="ffn2_dwgu", comms=[pair_wd3])
    chip_wd3 = rs_chip_comm(sums_of(pair_wd3, "wd3"))
    pair_gu3 = rs_pair_comm([dwg3, dwu3])
    dh2, dsh3, dsc3, dgn3 = dx_norm_bwd([dg3, du3], [wg3, wu3], h2, dy, g_norm3, sc3, seq=S, tm=tm, name="ffn2_dx",
                                        comms=[chip_wd3, pair_gu3])
    sum_wg3, sum_wu3 = sums_of(pair_gu3, "gu3")
    chip_wg3, chip_wu3 = rs_chip_comm([sum_wg3]), rs_chip_comm([sum_wu3])

    dgt, dya, dyb, dsgu, datt, dm, dga2 = mixer_out_bwd(dh2, mm, ga2, ya, yb, pgt, wa, wb, wo, seq=S, tm=tm,
                                                        name="mixer_out_bwd", comms=[chip_wg3])
    dwo, dwa, dwb = mm_tn([(merged, dm), (sgu, dya), (att, dyb)], tt=tm_big, name="mixer_dw")
    pair_abo = rs_pair_comm([_stack_cols(dwa), _stack_cols(dwb), dwo.reshape(N_QUAD, D // N_QUAD, D)])
    duv, dws, dzs, dgln, dbln = sgu_bwd(puv, dsgu, g_sgu_ln, b_sgu_ln, ws, bs_t, cpb=cpb, name="sgu_bwd",
                                        comms=[pair_abo])
    chip_abo = rs_chip_comm(sums_of(pair_abo, "abo"))
    dqkv, dgq_h, dgk_h, dsk = attn_bwd(pqkv, datt, gq_t, gk_t, attn_sinks, seq=S, name="attn_bwd",
                                       comms=[chip_wu3, chip_abo])
    dgq = dgq_h.reshape(N_Q, HEAD_DIM).sum(axis=0, keepdims=True)
    dgk = dgk_h.reshape(N_KV, HEAD_DIM).sum(axis=0, keepdims=True)
    share3 = rs_share_comm(halves_of(chip_wg3, chip_wu3, chip_wd3, "ffn2"))
    dwin_uv, dwin_qkv = mm_tn([(duv, xn2), (dqkv, xn2)], tt=tt_half, name="mixer_dwin_a", comms=[share3])
    (dwin_gt,) = mm_tn([(dgt, xn2)], tt=tt_half, name="mixer_dwin_b")
    dwin_parts = [dwin_uv, dwin_qkv, dwin_gt]
    r_wg3, r_wu3, r_wd3 = share3.bufs_out
    pair_win = rs_pair_comm([jnp.concatenate(dwin_parts, axis=0).reshape(win4.shape)])
    dh1, dsh2, dsc2, dgn2 = dx_norm_bwd([duv, dqkv, dgt], [(win, r0, r1) for r0, r1 in groups], h1, dh2, g_norm2, sc2, seq=S,
                                        tm=tm, name="mixer_dx", comms=[pair_win])
    chip_win = rs_chip_comm(sums_of(pair_win, "win"))

    dg1, du1, df1, dga1 = ffn_bwd_act(dh1, f1, ga1, wd1, g1, u1, seq=S, tm=tm, name="ffn1_act_bwd", comms=[chip_win])
    share_mix = rs_share_comm(halves_of(chip_win, chip_abo, "mix"))

    db_s = dzs.reshape(CHUNK, N_GROUPS, D_A // N_GROUPS).sum(axis=2).T.reshape(1, N_GROUPS * CHUNK)
    tail = jnp.concatenate([db_s, dgq, dgk, dsk[0:1, 0:N_Q], loss_part.reshape(1, 1)], axis=1)
    tail = jnp.pad(tail, ((0, 0), (0, (-tail.shape[1]) % D)))
    lnrow = jnp.concatenate([dgln, dbln], axis=1)
    lnrow = jnp.pad(lnrow, ((0, 0), (0, (-lnrow.shape[1]) % D)))
    packed = jnp.concatenate([dgn2, dgn3, lnrow.reshape(-1, D), tail.reshape(-1, D), dws.reshape(-1, D)], axis=0)
    n_rows = packed.shape[0]
    packed = jnp.pad(packed, ((0, (-n_rows) % 8), (0, 0)))
    gather_small = ag8_comm(packed)

    dwg1, dwu1 = mm_tn([(dg1, xn1), (du1, xn1)], tt=tt_half, name="ffn1_dwgu", comms=[share_mix, gather_small])
    r_win, r_wa, r_wb, r_wo = share_mix.bufs_out
    small = sum8(gather_small.lands[0], name="sum_small")
    pair_gu1 = rs_pair_comm([dwg1, dwu1])
    (dwd1,) = mm_tn([(a1, df1)], tt=T, name="ffn1_dwd", comms=[pair_gu1])
    chip_gu1 = rs_chip_comm(sums_of(pair_gu1, "gu1"))
    pair_wd1 = rs_pair_comm([dwd1])
    dx, dsh1, dsc1, dgn1 = dx_norm_bwd([dg1, du1], [wg1, wu1], xt, dh1, g_norm1, sc1, seq=S, tm=tm, name="ffn1_dx",
                                       comms=[chip_gu1, pair_wd1])
    chip_wd1 = rs_chip_comm(sums_of(pair_wd1, "wd1"))
    share_gu1 = rs_share_comm(halves_of(chip_gu1, "gu1"))

    names = ["w_ada", "b_ada", "g_norm1", "ffn1_w_gate", "ffn1_w_up", "ffn1_w_down", "g_norm2", "w_in", "g_sgu_ln",
             "b_sgu_ln", "w_spatial", "b_spatial", "g_q", "g_k", "attn_sinks", "w_branch_a", "w_branch_b", "w_out",
             "g_norm3", "ffn2_w_gate", "ffn2_w_up", "ffn2_w_down"]
    weights = dict(zip(names, [w_ada, b_ada, g_norm1, ffn1_w_gate, ffn1_w_up, ffn1_w_down, g_norm2, w_in, g_sgu_ln,
                               b_sgu_ln, w_spatial, b_spatial, g_q, g_k, attn_sinks, w_branch_a, w_branch_b, w_out,
                               g_norm3, ffn2_w_gate, ffn2_w_up, ffn2_w_down]))
    m_in = dict(zip(names, [m_w_ada, m_b_ada, m_g_norm1, m_ffn1_w_gate, m_ffn1_w_up, m_ffn1_w_down, m_g_norm2, m_w_in,
                            m_g_sgu_ln, m_b_sgu_ln, m_w_spatial, m_b_spatial, m_g_q, m_g_k, m_attn_sinks, m_w_branch_a,
                            m_w_branch_b, m_w_out, m_g_norm3, m_ffn2_w_gate, m_ffn2_w_up, m_ffn2_w_down]))
    v_in = dict(zip(names, [v_w_ada, v_b_ada, v_g_norm1, v_ffn1_w_gate, v_ffn1_w_up, v_ffn1_w_down, v_g_norm2, v_w_in,
                            v_g_sgu_ln, v_b_sgu_ln, v_w_spatial, v_b_spatial, v_g_q, v_g_k, v_attn_sinks, v_w_branch_a,
                            v_w_branch_b, v_w_out, v_g_norm3, v_ffn2_w_gate, v_ffn2_w_up, v_ffn2_w_down]))
    transposed = ("ffn1_w_gate", "ffn1_w_up", "w_in", "ffn2_w_gate", "ffn2_w_up")
    grads, delta, new_m, new_v = {}, {}, {}, {}

    def update(group, steps, name, comms=()):
        form = lambda nm, a: (tr(a) if nm in transposed else a)[0].reshape(group[nm].shape)
        res = adamw([(form(nm, weights[nm]), g, form(nm, m_in[nm]), form(nm, v_in[nm])) for nm, g in group.items()],
                    steps=steps, name=name, comms=comms)
        back = lambda nm, a: tr(a[None]) if nm in transposed else a.reshape(weights[nm].shape)
        for nm, (g, d, mn, vn) in zip(group, res):
            grads[nm], delta[nm], new_m[nm], new_v[nm] = back(nm, g), back(nm, d), back(nm, mn), back(nm, vn)

    dmods = jnp.concatenate([dsh1, dsc1, dga1, dsh2, dsc2, dga2, dsh3, dsc3, dga3], axis=1)
    dmods = jnp.concatenate([dmods, jnp.pad(dgn1, ((0, 0), (0, (n_mod - 1) * D)))], axis=0)
    gather_dmods = ag8_comm(dmods)
    run_comms([chip_wd1, share_gu1, gather_dmods], name="rs_tail")
    update({"ffn2_w_gate": r_wg3, "ffn2_w_up": r_wu3, "ffn2_w_down": r_wd3, "w_out": r_wo, "w_branch_a": r_wa,
            "w_branch_b": r_wb}, 8, "adamw_early")
    r_wg1, r_wu1 = share_gu1.bufs_out
    (gathered,) = gather_dmods.lands
    dmods_all = gathered[:, 0:B].reshape(N_DEV * B, n_mod * D)
    dmods_mine = lax.dynamic_slice(dmods_all, (0, quad * n_ada), (N_DEV * B, n_ada))
    share_wd1 = rs_share_comm(halves_of(chip_wd1, "wd1"))
    db_ada, dw_ada, g_gn1 = ada_bwd(c_all, dmods_all, dmods_mine, gathered[:, B, 0:D], name="ada_bwd", comms=[share_wd1])
    (r_wd1,) = share_wd1.bufs_out

    ln_rows = lnrow.size // D
    tail_rows = tail.size // D
    r = 2
    g_gn2, g_gn3 = small[0:1], small[1:2]
    ln_flat = small[r:r + ln_rows].reshape(1, -1)
    r += ln_rows
    tail_flat = small[r:r + tail_rows].reshape(1, -1)
    r += tail_rows
    g_ws = small[r:r + dws.size // D].reshape(w_spatial.shape)
    g_gln, g_bln = ln_flat[:, 0:D_A], ln_flat[:, D_A:2 * D_A]
    o = N_GROUPS * CHUNK
    g_bs = tail_flat[:, 0:o].reshape(b_spatial.shape)
    g_gq, g_gk, g_sk = tail_flat[:, o:o + HEAD_DIM], tail_flat[:, o + HEAD_DIM:o + 2 * HEAD_DIM], tail_flat[:, o + 2 * HEAD_DIM:o + 2 * HEAD_DIM + N_Q]
    loss = tail_flat[0, o + 2 * HEAD_DIM + N_Q]

    update({"w_ada": dw_ada}, 16, "adamw_w_ada")
    update({"ffn1_w_gate": r_wg1, "ffn1_w_up": r_wu1, "ffn1_w_down": r_wd1, "w_in": r_win}, 8, "adamw_late")

    update({"b_ada": db_ada, "g_norm1": g_gn1, "g_norm2": g_gn2, "g_norm3": g_gn3, "g_sgu_ln": g_gln,
            "b_sgu_ln": g_bln, "w_spatial": g_ws.reshape(-1, CHUNK), "b_spatial": g_bs.reshape(N_GROUPS, CHUNK),
            "g_q": g_gq, "g_k": g_gk, "attn_sinks": g_sk}, 1, "adamw_small")

    return (loss, dx.reshape(B, S, D), *[grads[nm] for nm in names], *[delta[nm] for nm in names],
            *[new_m[nm] for nm in names], *[new_v[nm] for nm in names])
```

```python
import functools
import math
import operator

import jax
import jax.numpy as jnp
from jax import lax
from jax.experimental import pallas as pl
from jax.experimental.pallas import tpu as pltpu

F32 = jnp.float32
BF16 = jnp.bfloat16
EPS = 1e-6
NEG = -1e30
N_DEV = 8
N_QUAD = 4
D_A = 512
N_GROUPS = 4
CHUNK = 128
HEAD_DIM = 64
N_KV = 2
Q_PER_KV = 4
N_Q = N_KV * Q_PER_KV
D_B = N_Q * HEAD_DIM
QKV_W = D_B + 2 * N_KV * HEAD_DIM
K_OFF = D_B
V_OFF = D_B + N_KV * HEAD_DIM
ATT_SCALE = HEAD_DIM ** -0.5
GELU_K = math.sqrt(2.0 / math.pi)
GELU_C = 0.044715
ADAM_LR, ADAM_B1, ADAM_B2, ADAM_EPS, ADAM_WD, ADAM_STEP = 0.001, 0.9, 0.999, 1e-08, 0.01, 10
VMEM_LIMIT_BYTES = 56 * 1024 * 1024
MESH = pl.DeviceIdType.MESH
NT = (((1,), (1,)), ((), ()))
TN = (((0,), (0,)), ((), ()))
ANY = pl.BlockSpec(memory_space=pl.ANY)


def _dot(a, b):
    return jnp.dot(a, b, preferred_element_type=F32)


def _dg(a, b, dims):
    return lax.dot_general(a, b, dims, preferred_element_type=F32)


def _gelu_parts(x):
    t = jnp.tanh(GELU_K * (x + GELU_C * x * x * x))
    return 0.5 * x * (1.0 + t), t


def _dgelu(x, t):
    return 0.5 * (1.0 + t) + 0.5 * x * (1.0 - t * t) * (GELU_K * (1.0 + 3.0 * GELU_C * x * x))


def _row_block(rows, target):
    b = min(rows, target)
    while rows % b or b % 8:
        b -= 1
    return b


def _mod_spec(mod, tps):
    mods, j = mod
    return pl.BlockSpec((1, None, 1, mods.shape[3]), lambda i: (i // tps, j, 0, 0))


def _resident(a):
    return pl.BlockSpec(a.shape, lambda *_: (0,) * a.ndim, pipeline_mode=pl.Buffered(1))


class Comm:
    def __init__(self, *, srcs=(), bufs=(), land_shapes=(), n_sems, start, finish):
        self.srcs, self.bufs, self.land_shapes = list(srcs), list(bufs), list(land_shapes)
        self.n_sems, self.start, self.finish = n_sems, start, finish
        self.bufs_out, self.lands = None, None


def _call(body, *, name, grid, in_specs, out_specs, out_shape, args, scratch_shapes=(), comms=(), prefetch=()):
    prefetch = list(prefetch)
    in_specs, out_specs, out_shape = list(in_specs), list(out_specs), list(out_shape)
    args, scratch = list(args), list(scratch_shapes)
    n_in, n_out, n_scr = len(args), len(out_shape), len(scratch)
    aliases, layout = {}, []
    for cm in comms:
        i0, o0, s0 = len(args), len(out_shape), len(scratch)
        args += cm.srcs + cm.bufs
        in_specs += [ANY] * (len(cm.srcs) + len(cm.bufs))
        out_shape += [jax.ShapeDtypeStruct(b.shape, b.dtype) for b in cm.bufs] + cm.land_shapes
        out_specs += [ANY] * (len(cm.bufs) + len(cm.land_shapes))
        for j in range(len(cm.bufs)):
            aliases[len(prefetch) + i0 + len(cm.srcs) + j] = o0 + j
        scratch += [pltpu.SemaphoreType.DMA((cm.n_sems,)), pltpu.SemaphoreType.DMA((cm.n_sems,))]
        layout.append((i0, o0, s0))
    n_in_all, n_out_all = len(args), len(out_shape)

    def wrapped(*refs):
        scalars, refs = refs[:len(prefetch)], refs[len(prefetch):]
        ins, outs, scr = refs[:n_in_all], refs[n_in_all:n_in_all + n_out_all], refs[n_in_all + n_out_all:]
        parts = []
        for cm, (i0, o0, s0) in zip(comms, layout):
            nb = len(cm.bufs)
            parts.append((ins[i0:i0 + len(cm.srcs)], outs[o0:o0 + nb], outs[o0 + nb:o0 + nb + len(cm.land_shapes)],
                          scr[s0], scr[s0 + 1]))
        if comms and grid:
            ids = [pl.program_id(d) for d in range(len(grid))]
            first = functools.reduce(operator.and_, [i == 0 for i in ids])
            last = functools.reduce(operator.and_, [i == g - 1 for i, g in zip(ids, grid)])

            @pl.when(first)
            def _():
                for cm, p in zip(comms, parts):
                    cm.start(*p)
        elif comms:
            for cm, p in zip(comms, parts):
                cm.start(*p)
        if body is not None:
            body(*scalars, *ins[:n_in], *outs[:n_out], *scr[:n_scr])
        if comms and grid:
            @pl.when(last)
            def _():
                for cm, p in zip(comms, parts):
                    cm.finish(*p)
        elif comms:
            for cm, p in zip(comms, parts):
                cm.finish(*p)

    if prefetch:
        kwargs = dict(grid_spec=pltpu.PrefetchScalarGridSpec(
            num_scalar_prefetch=len(prefetch), grid=grid, in_specs=in_specs, out_specs=out_specs, scratch_shapes=scratch))
    else:
        kwargs = dict(in_specs=in_specs, out_specs=out_specs, scratch_shapes=scratch, **(dict(grid=grid) if grid else {}))
    sem = ("arbitrary",) * len(grid)
    res = pl.pallas_call(
        wrapped, name=name, out_shape=out_shape, input_output_aliases=aliases,
        compiler_params=pltpu.CompilerParams(dimension_semantics=sem, vmem_limit_bytes=VMEM_LIMIT_BYTES), **kwargs,
    )(*prefetch, *args)
    for cm, (i0, o0, s0) in zip(comms, layout):
        nb = len(cm.bufs)
        cm.bufs_out = list(res[o0:o0 + nb])
        cm.lands = list(res[o0 + nb:o0 + nb + len(cm.land_shapes)])
    return list(res[:n_out])


def run_comms(comms, *, name):
    _call(None, name=name, grid=(), in_specs=[], out_specs=[], out_shape=[], args=[], comms=comms)


def norm_mod_fwd(h, g, sc, sh, *, seq, tm, name, comms=()):
    T, D = h.shape
    B, tps = T // seq, seq // tm

    def body(h_ref, g_ref, sc_ref, sh_ref, o_ref):
        x = h_ref[...]
        r = lax.rsqrt(jnp.mean(x * x, axis=-1, keepdims=True) + EPS)
        y = x * r * g_ref[...]
        o_ref[...] = (y * (1.0 + sc_ref[0]) + sh_ref[0]).astype(o_ref.dtype)

    return _call(
        body, name=name, grid=(T // tm,),
        in_specs=[pl.BlockSpec((tm, D), lambda i: (i, 0)), pl.BlockSpec((1, D), lambda i: (0, 0)),
                  _mod_spec(sc, tps), _mod_spec(sh, tps)],
        out_specs=[pl.BlockSpec((tm, D), lambda i: (i, 0))], out_shape=[jax.ShapeDtypeStruct((T, D), BF16)],
        args=[h, g, sc[0], sh[0]], comms=comms)[0]


def ffn_up(xn, wg, wu, *, tm, name, comms=()):
    T, D = xn.shape
    nq, fs, _ = wg.shape

    def body(x_ref, wg_ref, wu_ref, s_ref, q_ref, a_ref):
        x = x_ref[...]
        g = _dg(x, wg_ref[0], NT)
        u = _dg(x, wu_ref[0], NT)
        sg = jax.nn.sigmoid(g)
        s = g * sg
        s_ref[0] = s.astype(BF16)
        q_ref[0] = (u * (sg + s * (1.0 - sg))).astype(BF16)
        a_ref[0] = (s * u).astype(BF16)

    w_spec = pl.BlockSpec((1, fs, D), lambda q, i: (q, 0, 0))
    o_spec = pl.BlockSpec((1, tm, fs), lambda q, i: (q, i, 0))
    o_shape = jax.ShapeDtypeStruct((nq, T, fs), BF16)
    return _call(
        body, name=name, grid=(nq, T // tm),
        in_specs=[pl.BlockSpec((tm, D), lambda q, i: (i, 0)), w_spec, w_spec],
        out_specs=[o_spec, o_spec, o_spec], out_shape=[o_shape, o_shape, o_shape], args=[xn, wg, wu], comms=comms)


def _norm_mod(y, g_ref, sc_ref, sh_ref):
    nx, _ = _rms_parts(y)
    return ((nx * g_ref[...]) * (1.0 + sc_ref[0]) + sh_ref[0]).astype(BF16)


def ffn_down(a, wd, hin, ga, *, target=None, norm=None, seq, tm, name, comms=()):
    nq, T, fs = a.shape
    D = wd.shape[-1]
    B, tps, nt = T // seq, seq // tm, T // tm

    def body(a_ref, wd_ref, h_ref, ga_ref, *rest):
        rest = list(rest)
        t_ref = rest.pop(0) if target is not None else None
        norm_refs = [rest.pop(0) for _ in range(3)] if norm is not None else None
        o_ref, f_ref = rest[0], rest[1]
        f = _dot(a_ref[0], wd_ref[0])
        for q in range(1, nq):
            f = f + _dot(a_ref[q], wd_ref[q])
        f_ref[...] = f.astype(BF16)
        y = h_ref[...] + (0.5 * ga_ref[0]) * f
        if target is not None:
            e = y - t_ref[...]
            o_ref[...] = e * (1.0 / D)
            rest[2][...] = jnp.full(rest[2].shape, 0.5 * jnp.sum(e * e) * (1.0 / D), F32)
        else:
            o_ref[...] = y
        if norm is not None:
            rest[2][...] = _norm_mod(y, *norm_refs)

    tile = pl.BlockSpec((tm, D), lambda i: (i, 0))
    in_specs = [pl.BlockSpec((nq, tm, fs), lambda i: (0, i, 0)), _resident(wd), tile, _mod_spec(ga, tps)]
    out_specs = [tile, tile]
    out_shape = [jax.ShapeDtypeStruct((T, D), F32), jax.ShapeDtypeStruct((T, D), BF16)]
    args = [a, wd, hin, ga[0]]
    if target is not None:
        in_specs.append(tile)
        args.append(target)
        out_specs.append(pl.BlockSpec((1, 8, 128), lambda i: (i, 0, 0)))
        out_shape.append(jax.ShapeDtypeStruct((nt, 8, 128), F32))
    if norm is not None:
        in_specs += [pl.BlockSpec((1, D), lambda i: (0, 0)), _mod_spec(norm[1], tps), _mod_spec(norm[2], tps)]
        args += [norm[0], norm[1][0], norm[2][0]]
        out_specs.append(tile)
        out_shape.append(jax.ShapeDtypeStruct((T, D), BF16))
    return _call(body, name=name, grid=(nt,), in_specs=in_specs, out_specs=out_specs, out_shape=out_shape, args=args,
                 comms=comms)


def proj_fwd(xn, w, groups, *, tm, name, comms=()):
    T, D = xn.shape

    def body(x_ref, w_ref, *o_refs):
        x = x_ref[...]
        for (r0, r1), o_ref in zip(groups, o_refs):
            o_ref[...] = _dg(x, w_ref[r0:r1, :], NT).astype(BF16)

    return _call(
        body, name=name, grid=(T // tm,), in_specs=[pl.BlockSpec((tm, D), lambda i: (i, 0)), _resident(w)],
        out_specs=[pl.BlockSpec((tm, r1 - r0), lambda i: (i, 0)) for r0, r1 in groups],
        out_shape=[jax.ShapeDtypeStruct((T, r1 - r0), BF16) for r0, r1 in groups], args=[xn, w], comms=comms)


def _layer_norm_parts(v):
    mu = jnp.mean(v, axis=-1, keepdims=True)
    d = v - mu
    rstd = lax.rsqrt(jnp.mean(d * d, axis=-1, keepdims=True) + EPS)
    return d * rstd, rstd


def _causal():
    row = lax.broadcasted_iota(jnp.int32, (CHUNK, CHUNK), 0)
    col = lax.broadcasted_iota(jnp.int32, (CHUNK, CHUNK), 1)
    return row >= col


def sgu_fwd(puv, gln, bln, ws, bs_t, *, cpb, name, comms=()):
    T = puv.shape[0]
    gd = D_A // N_GROUPS
    tm = cpb * CHUNK

    def body(p_ref, gln_ref, bln_ref, ws_ref, bs_ref, o_ref):
        causal = _causal()
        wm = [jnp.where(causal, ws_ref[g], 0.0).astype(BF16) for g in range(N_GROUPS)]
        for j in range(cpb):
            rows = slice(j * CHUNK, (j + 1) * CHUNK)
            u, _ = _gelu_parts(p_ref[rows, 0:D_A].astype(F32))
            vv, _ = _gelu_parts(p_ref[rows, D_A:2 * D_A].astype(F32))
            vhat, _ = _layer_norm_parts(vv)
            vn = (vhat * gln_ref[...] + bln_ref[...]).astype(BF16)
            for g in range(N_GROUPS):
                cols = slice(g * gd, (g + 1) * gd)
                z = _dot(wm[g], vn[:, cols]) + bs_ref[:, g:g + 1]
                o_ref[rows, cols] = (u[:, cols] * z).astype(BF16)

    full = lambda a: pl.BlockSpec(a.shape, lambda i: (0,) * a.ndim)
    return _call(
        body, name=name, grid=(T // tm,),
        in_specs=[pl.BlockSpec((tm, 2 * D_A), lambda i: (i, 0)), full(gln), full(bln), full(ws), full(bs_t)],
        out_specs=[pl.BlockSpec((tm, D_A), lambda i: (i, 0))], out_shape=[jax.ShapeDtypeStruct((T, D_A), BF16)],
        args=[puv, gln, bln, ws, bs_t], comms=comms)[0]


def _rms_parts(x):
    r = lax.rsqrt(jnp.mean(x * x, axis=-1, keepdims=True) + EPS)
    return x * r, r


KV_W = Q_PER_KV * HEAD_DIM
KV2 = N_KV * HEAD_DIM
STACK = Q_PER_KV * CHUNK


def _iota(shape, dim):
    return lax.broadcasted_iota(jnp.int32, shape, dim)


def _head_mean_matrix(n):
    return jnp.where((_iota((n, n), 0) >> 6) == (_iota((n, n), 1) >> 6), 1.0 / HEAD_DIM, 0.0).astype(BF16)


def _repeat_matrix(h):
    return jnp.where(_iota((KV2, KV_W), 0) == h * HEAD_DIM + (_iota((KV2, KV_W), 1) & (HEAD_DIM - 1)), 1.0, 0.0).astype(BF16)


def _fold_matrix(h):
    j, l = _iota((KV_W, KV2), 0), _iota((KV_W, KV2), 1)
    return jnp.where(((j & (HEAD_DIM - 1)) == (l & (HEAD_DIM - 1))) & ((l >> 6) == h), 1.0, 0.0).astype(BF16)


def _dot_split(x, m):
    hi = x.astype(BF16)
    lo = (x - hi.astype(F32)).astype(BF16)
    return _dot(hi, m) + _dot(lo, m)


def _head_rms(x, mean_matrix):
    r = lax.rsqrt(_dot_split(x * x, mean_matrix) + EPS)
    return x * r, r


def _stack_heads(x):
    head = _iota(x.shape, 1) >> 6
    return jnp.concatenate([jnp.where(head == g, x, 0.0).astype(BF16) for g in range(Q_PER_KV)], axis=0)


def _unstack_heads(y):
    head = _iota((CHUNK, KV_W), 1) >> 6
    out = jnp.where(head == 0, y[0:CHUNK], 0.0)
    for g in range(1, Q_PER_KV):
        out = out + jnp.where(head == g, y[g * CHUNK:(g + 1) * CHUNK], 0.0)
    return out


SOFTMAX_ROWS = 32


def _fill_mask_bias(bias_ref):
    qi = _iota((CHUNK, 2 * CHUNK), 0)
    kj = _iota((CHUNK, 2 * CHUNK), 1)
    diff = qi + CHUNK - kj
    window = (diff >= 0) & (diff < CHUNK)
    bias_ref[0] = jnp.where(window & (kj >= CHUNK), 0.0, NEG)
    bias_ref[1] = jnp.where(window, 0.0, NEG)


def _softmax_rows(s_ref, bias_ref, first, sink_ref, h, c):
    rows = pl.ds(pl.multiple_of(c * SOFTMAX_ROWS, SOFTMAX_ROWS), SOFTMAX_ROWS)
    in_block = pl.ds(pl.multiple_of((c % (CHUNK // SOFTMAX_ROWS)) * SOFTMAX_ROWS, SOFTMAX_ROWS), SOFTMAX_ROWS)
    sink = sink_ref[0, h * Q_PER_KV + c // (CHUNK // SOFTMAX_ROWS)]
    s = s_ref[rows, :] + bias_ref[first, in_block, :]
    m = jnp.maximum(jnp.max(s, axis=-1, keepdims=True), sink)
    p = jnp.exp(s - m)
    es = jnp.exp(sink - m)
    inv = 1.0 / (jnp.sum(p, axis=-1, keepdims=True) + es)
    return rows, p * inv, es * inv


def _fill_keys(p_ref, gk_ref, krep, vrep, seq):
    mean_k = _head_mean_matrix(KV2)
    reps = [_repeat_matrix(h) for h in range(N_KV)]
    for h in range(N_KV):
        krep[h][0:CHUNK, :] = jnp.zeros((CHUNK, KV_W), BF16)
        vrep[h][0:CHUNK, :] = jnp.zeros((CHUNK, KV_W), BF16)
    rows = min(seq, 4 * CHUNK)

    def piece(j, carry):
        r0 = pl.multiple_of(j * rows, CHUNK)
        nk, _ = _head_rms(p_ref[pl.ds(r0, rows), K_OFF:K_OFF + KV2].astype(F32), mean_k)
        kn = (nk * gk_ref[...]).astype(BF16)
        v = p_ref[pl.ds(r0, rows), V_OFF:V_OFF + KV2].astype(BF16)
        r1 = pl.multiple_of(r0 + CHUNK, CHUNK)
        for h in range(N_KV):
            krep[h][pl.ds(r1, rows), :] = _dot(kn, reps[h]).astype(BF16)
            vrep[h][pl.ds(r1, rows), :] = _dot(v, reps[h]).astype(BF16)
        return carry

    lax.fori_loop(0, seq // rows, piece, 0)


def attn_fwd(pqkv, gq_t, gk_t, sinks, *, seq, name, comms=()):
    T = pqkv.shape[0]
    nb = seq // CHUNK

    def body(p_ref, gq_ref, gk_ref, sink_ref, o_ref, k0, k1, v0, v1, bias_ref, s_ref, pr_ref):
        krep, vrep = [k0, k1], [v0, v1]
        _fill_keys(p_ref, gk_ref, krep, vrep, seq)
        _fill_mask_bias(bias_ref)
        mean_q = _head_mean_matrix(D_B)

        def block(i, carry):
            r0 = pl.multiple_of(i * CHUNK, CHUNK)
            first = jnp.minimum(i, 1)
            nq, _ = _head_rms(p_ref[pl.ds(r0, CHUNK), 0:D_B].astype(F32), mean_q)
            qn = nq * (gq_ref[...] * ATT_SCALE)
            for h in range(N_KV):
                qs = _stack_heads(qn[:, h * KV_W:(h + 1) * KV_W])
                s_ref[...] = _dg(qs, krep[h][pl.ds(r0, 2 * CHUNK), :], NT)

                def rows_of(c, carry2):
                    rows, probs, _ = _softmax_rows(s_ref, bias_ref, first, sink_ref, h, c)
                    pr_ref[rows, :] = probs.astype(BF16)
                    return carry2

                lax.fori_loop(0, STACK // SOFTMAX_ROWS, rows_of, 0, unroll=True)
                out = _unstack_heads(_dot(pr_ref[...], vrep[h][pl.ds(r0, 2 * CHUNK), :]))
                o_ref[pl.ds(r0, CHUNK), h * KV_W:(h + 1) * KV_W] = out.astype(BF16)
            return carry

        lax.fori_loop(0, nb, block, 0)

    return _call(
        body, name=name, grid=(T // seq,),
        in_specs=[pl.BlockSpec((seq, QKV_W), lambda b: (b, 0)), pl.BlockSpec(gq_t.shape, lambda b: (0, 0)),
                  pl.BlockSpec(gk_t.shape, lambda b: (0, 0)), pl.BlockSpec(memory_space=pltpu.SMEM)],
        out_specs=[pl.BlockSpec((seq, D_B), lambda b: (b, 0))], out_shape=[jax.ShapeDtypeStruct((T, D_B), BF16)],
        scratch_shapes=[pltpu.VMEM((seq + CHUNK, KV_W), BF16)] * 4
        + [pltpu.VMEM((2, CHUNK, 2 * CHUNK), F32), pltpu.VMEM((STACK, 2 * CHUNK), F32), pltpu.VMEM((STACK, 2 * CHUNK), BF16)],
        args=[pqkv, gq_t, gk_t, sinks], comms=comms)[0]


def mixer_out_fwd(sgu, att, pg, hin, ga, wa, wb, wo, norm, *, seq, tm, name, comms=()):
    T, D = hin.shape
    B, tps = T // seq, seq // tm

    def body(s_ref, t_ref, pg_ref, h_ref, ga_ref, wa_ref, wb_ref, wo_ref, g_ref, sc_ref, sh_ref,
             ya_ref, yb_ref, mg_ref, m_ref, ho_ref, xn_ref):
        ya = _dot(s_ref[...], wa_ref[...])
        yb = _dot(t_ref[...], wb_ref[...])
        merged = (jax.nn.sigmoid(pg_ref[:, 0:D].astype(F32)) * ya
                  + jax.nn.sigmoid(pg_ref[:, D:2 * D].astype(F32)) * yb)
        mg = merged.astype(BF16)
        m = _dot(mg, wo_ref[...])
        ya_ref[...] = ya.astype(BF16)
        yb_ref[...] = yb.astype(BF16)
        mg_ref[...] = mg
        m_ref[...] = m.astype(BF16)
        y = h_ref[...] + ga_ref[0] * m
        ho_ref[...] = y
        xn_ref[...] = _norm_mod(y, g_ref, sc_ref, sh_ref)

    tile = lambda w: pl.BlockSpec((tm, w), lambda i: (i, 0))
    b16 = jax.ShapeDtypeStruct((T, D), BF16)
    return _call(
        body, name=name, grid=(T // tm,),
        in_specs=[tile(D_A), tile(D_B), tile(2 * D), tile(D), _mod_spec(ga, tps), _resident(wa), _resident(wb),
                  _resident(wo), pl.BlockSpec((1, D), lambda i: (0, 0)), _mod_spec(norm[1], tps), _mod_spec(norm[2], tps)],
        out_specs=[tile(D)] * 6, out_shape=[b16, b16, b16, b16, jax.ShapeDtypeStruct((T, D), F32), b16],
        args=[sgu, att, pg, hin, ga[0], wa, wb, wo, norm[0], norm[1][0], norm[2][0]], comms=comms)


def ffn_bwd_act(dh, f, ga, wd, s, q, *, seq, tm, name, comms=()):
    T, D = dh.shape
    nq, fs, _ = wd.shape
    B, tps = T // seq, seq // tm

    def body(dh_ref, f_ref, ga_ref, wd_ref, s_ref, q_ref, dg_ref, du_ref, df_ref, dga_ref):
        i = pl.program_id(0)
        d = dh_ref[...]
        df = ((0.5 * ga_ref[0]) * d).astype(BF16)
        df_ref[...] = df
        part = 0.5 * jnp.sum(d * f_ref[...].astype(F32), axis=0, keepdims=True)
        for j in range(nq):
            da = _dg(df, wd_ref[j], NT)
            du_ref[j] = (da * s_ref[j].astype(F32)).astype(BF16)
            dg_ref[j] = (da * q_ref[j].astype(F32)).astype(BF16)

        @pl.when(i % tps == 0)
        def _():
            dga_ref[0] = part

        @pl.when(i % tps != 0)
        def _():
            dga_ref[0] += part

    tile = pl.BlockSpec((tm, D), lambda i: (i, 0))
    per_seq = pl.BlockSpec((1, 1, D), lambda i: (i // tps, 0, 0))
    act = pl.BlockSpec((nq, tm, fs), lambda i: (0, i, 0))
    o_shape = jax.ShapeDtypeStruct((nq, T, fs), BF16)
    dg, du, df, dga = _call(
        body, name=name, grid=(T // tm,), in_specs=[tile, tile, _mod_spec(ga, tps), _resident(wd), act, act],
        out_specs=[act, act, tile, per_seq],
        out_shape=[o_shape, o_shape, jax.ShapeDtypeStruct((T, D), BF16), jax.ShapeDtypeStruct((B, 1, D), F32)],
        args=[dh, f, ga[0], wd, s, q], comms=comms)
    return dg, du, df, dga.reshape(B, D)


def mm_tn(pairs, *, tt, name, comms=()):
    n = len(pairs)
    flat = []
    for pair in pairs:
        for a in pair:
            if not any(a is b for b in flat):
                flat.append(a)
    where = lambda a: [k for k, b in enumerate(flat) if a is b][0]
    nq = max([a.shape[0] for a in flat if a.ndim == 3] + [1])
    T = flat[0].shape[-2]
    tt = min(tt, T)
    nt = T // tt
    stacked = [x.ndim == 3 or y.ndim == 3 for x, y in pairs]

    def body(*refs):
        in_refs, o_refs, accs = refs[:len(flat)], refs[len(flat):len(flat) + n], refs[len(flat) + n:]
        t = pl.program_id(1)
        value = lambda a: in_refs[where(a)][0] if a.ndim == 3 else in_refs[where(a)][...]

        def put(j, v):
            if stacked[j]:
                o_refs[j][0] = v.astype(BF16)
            else:
                o_refs[j][...] = v.astype(BF16)

        for j, (x, y) in enumerate(pairs):
            part = _dg(value(x), value(y), TN)
            if nt == 1:
                put(j, part)
                continue

            @pl.when(t == 0)
            def _():
                accs[j][...] = part

            @pl.when(t != 0)
            def _():
                accs[j][...] += part

        if nt > 1:
            @pl.when(t == nt - 1)
            def _():
                for j in range(n):
                    put(j, accs[j][...])

    def spec(a):
        if a.ndim == 3:
            return pl.BlockSpec((1, tt, a.shape[2]), lambda q, t: (q, t, 0))
        return pl.BlockSpec((tt, a.shape[1]), lambda q, t: (t, 0))

    out_specs, out_shape = [], []
    for j, (x, y) in enumerate(pairs):
        K, N = x.shape[-1], y.shape[-1]
        if stacked[j]:
            out_specs.append(pl.BlockSpec((1, K, N), lambda q, t: (q, 0, 0)))
            out_shape.append(jax.ShapeDtypeStruct((nq, K, N), BF16))
        else:
            out_specs.append(pl.BlockSpec((K, N), lambda q, t: (0, 0)))
            out_shape.append(jax.ShapeDtypeStruct((K, N), BF16))
    return _call(
        body, name=name, grid=(nq, nt), in_specs=[spec(a) for a in flat],
        out_specs=out_specs, out_shape=out_shape,
        scratch_shapes=[pltpu.VMEM((x.shape[-1], y.shape[-1]), F32) for x, y in pairs] if nt > 1 else [],
        args=flat, comms=comms)


def dx_norm_bwd(dys, ws, hin, dh_out, g, sc, *, seq, tm, name, comms=()):
    T, D = hin.shape
    B, tps = T // seq, seq // tm
    n = len(dys)
    ranged = [w if isinstance(w, tuple) else (w, 0, w.shape[-2]) for w in ws]
    ws = []
    for w, _, _ in ranged:
        if not any(w is u for u in ws):
            ws.append(w)
    where = [[k for k, u in enumerate(ws) if u is w][0] for w, _, _ in ranged]

    def body(*refs):
        dy_refs, w_refs = refs[:n], refs[n:n + len(ws)]
        h_ref, dho_ref, g_ref, sc_ref, dhi_ref, dsh_ref, dsc_ref, dgn_ref = refs[n + len(ws):]
        i = pl.program_id(0)
        dxn = None
        for j in range(n):
            w_ref, (_, r0, r1) = w_refs[where[j]], ranged[j]
            if dys[j].ndim == 3:
                for q in range(dys[j].shape[0]):
                    part = _dot(dy_refs[j][q], w_ref[q])
                    dxn = part if dxn is None else dxn + part
            else:
                part = _dot(dy_refs[j][...], w_ref[r0:r1, :])
                dxn = part if dxn is None else dxn + part
        x = h_ref[...]
        nx, r = _rms_parts(x)
        gain = g_ref[...]
        one_sc = 1.0 + sc_ref[0]
        dsh = jnp.sum(dxn, axis=0, keepdims=True)
        dsc = jnp.sum(dxn * (nx * gain), axis=0, keepdims=True)
        dgn = jnp.sum(dxn * one_sc * nx, axis=0, keepdims=True)
        dn = dxn * one_sc * gain
        dhi_ref[...] = dho_ref[...] + r * (dn - nx * jnp.mean(dn * nx, axis=-1, keepdims=True))

        @pl.when(i % tps == 0)
        def _():
            dsh_ref[0] = dsh
            dsc_ref[0] = dsc

        @pl.when(i % tps != 0)
        def _():
            dsh_ref[0] += dsh
            dsc_ref[0] += dsc

        @pl.when(i == 0)
        def _():
            dgn_ref[...] = dgn

        @pl.when(i != 0)
        def _():
            dgn_ref[...] += dgn

    def dy_spec(a):
        if a.ndim == 3:
            return pl.BlockSpec((a.shape[0], tm, a.shape[2]), lambda i: (0, i, 0))
        return pl.BlockSpec((tm, a.shape[1]), lambda i: (i, 0))

    tile = pl.BlockSpec((tm, D), lambda i: (i, 0))
    per_seq = pl.BlockSpec((1, 1, D), lambda i: (i // tps, 0, 0))
    row = pl.BlockSpec((1, D), lambda i: (0, 0))
    dhi, dsh, dsc, dgn = _call(
        body, name=name, grid=(T // tm,),
        in_specs=[dy_spec(a) for a in dys] + [_resident(w) for w in ws] + [tile, tile, row, _mod_spec(sc, tps)],
        out_specs=[tile, per_seq, per_seq, row],
        out_shape=[jax.ShapeDtypeStruct((T, D), F32), jax.ShapeDtypeStruct((B, 1, D), F32),
                   jax.ShapeDtypeStruct((B, 1, D), F32), jax.ShapeDtypeStruct((1, D), F32)],
        args=[*dys, *ws, hin, dh_out, g, sc[0]], comms=comms)
    return dhi, dsh.reshape(B, D), dsc.reshape(B, D), dgn


def mixer_out_bwd(dh, m, ga, ya, yb, pg, wa, wb, wo, *, seq, tm, name, comms=()):
    T, D = dh.shape
    B, tps = T // seq, seq // tm

    def body(dh_ref, m_ref, ga_ref, ya_ref, yb_ref, pg_ref, wa_ref, wb_ref, wo_ref,
             dg_ref, dya_ref, dyb_ref, ds_ref, dt_ref, dm_ref, dga_ref):
        i = pl.program_id(0)
        d = dh_ref[...]
        dm = (ga_ref[0] * d).astype(BF16)
        dm_ref[...] = dm
        part = jnp.sum(d * m_ref[...].astype(F32), axis=0, keepdims=True)

        @pl.when(i % tps == 0)
        def _():
            dga_ref[0] = part

        @pl.when(i % tps != 0)
        def _():
            dga_ref[0] += part

        dmg = _dg(dm, wo_ref[...], NT)
        sa = jax.nn.sigmoid(pg_ref[:, 0:D].astype(F32))
        sb = jax.nn.sigmoid(pg_ref[:, D:2 * D].astype(F32))
        dg_ref[:, 0:D] = (dmg * ya_ref[...].astype(F32) * (sa * (1.0 - sa))).astype(BF16)
        dg_ref[:, D:2 * D] = (dmg * yb_ref[...].astype(F32) * (sb * (1.0 - sb))).astype(BF16)
        dya = (dmg * sa).astype(BF16)
        dyb = (dmg * sb).astype(BF16)
        dya_ref[...] = dya
        dyb_ref[...] = dyb
        ds_ref[...] = _dg(dya, wa_ref[...], NT).astype(BF16)
        dt_ref[...] = _dg(dyb, wb_ref[...], NT).astype(BF16)

    tile = lambda w: pl.BlockSpec((tm, w), lambda i: (i, 0))
    per_seq = pl.BlockSpec((1, 1, D), lambda i: (i // tps, 0, 0))
    dg, dya, dyb, ds, dt, dm, dga = _call(
        body, name=name, grid=(T // tm,),
        in_specs=[tile(D), tile(D), _mod_spec(ga, tps), tile(D), tile(D), tile(2 * D), _resident(wa), _resident(wb),
                  _resident(wo)],
        out_specs=[tile(2 * D), tile(D), tile(D), tile(D_A), tile(D_B), tile(D), per_seq],
        out_shape=[jax.ShapeDtypeStruct((T, 2 * D), BF16), jax.ShapeDtypeStruct((T, D), BF16),
                   jax.ShapeDtypeStruct((T, D), BF16), jax.ShapeDtypeStruct((T, D_A), BF16),
                   jax.ShapeDtypeStruct((T, D_B), BF16), jax.ShapeDtypeStruct((T, D), BF16),
                   jax.ShapeDtypeStruct((B, 1, D), F32)],
        args=[dh, m, ga[0], ya, yb, pg, wa, wb, wo], comms=comms)
    return dg, dya, dyb, ds, dt, dm, dga.reshape(B, D)


def sgu_bwd(puv, dsgu, gln, bln, ws, bs_t, *, cpb, name, comms=()):
    T = puv.shape[0]
    gd = D_A // N_GROUPS
    tm = cpb * CHUNK

    def body(p_ref, ds_ref, gln_ref, bln_ref, ws_ref, bs_ref, d_ref, dws_ref, dz_ref, dgl_ref, dbl_ref):
        i = pl.program_id(0)
        causal = _causal()
        wf = [jnp.where(causal, ws_ref[g], 0.0) for g in range(N_GROUPS)]
        wm = [w.astype(BF16) for w in wf]
        wt = [w.T.astype(BF16) for w in wf]
        dws = [jnp.zeros((CHUNK, CHUNK), F32) for _ in range(N_GROUPS)]
        dzs = jnp.zeros((CHUNK, D_A), F32)
        dgl = jnp.zeros((1, D_A), F32)
        dbl = jnp.zeros((1, D_A), F32)
        for j in range(cpb):
            rows = slice(j * CHUNK, (j + 1) * CHUNK)
            au = p_ref[rows, 0:D_A].astype(F32)
            av = p_ref[rows, D_A:2 * D_A].astype(F32)
            u, tu = _gelu_parts(au)
            vv, tv = _gelu_parts(av)
            vhat, rstd = _layer_norm_parts(vv)
            vn = (vhat * gln_ref[...] + bln_ref[...]).astype(BF16)
            dsg = ds_ref[rows, :].astype(F32)
            dz = dsg * u
            dzb = dz.astype(BF16)
            zs, dvns = [], []
            for g in range(N_GROUPS):
                cols = slice(g * gd, (g + 1) * gd)
                zs.append(_dot(wm[g], vn[:, cols]) + bs_ref[:, g:g + 1])
                dws[g] = dws[g] + _dg(dzb[:, cols], vn[:, cols], NT)
                dvns.append(_dot(wt[g], dzb[:, cols]))
            z = jnp.concatenate(zs, axis=1)
            dvn = jnp.concatenate(dvns, axis=1)
            d_ref[rows, 0:D_A] = (dsg * z * _dgelu(au, tu)).astype(BF16)
            dvh = dvn * gln_ref[...]
            dvv = rstd * (dvh - jnp.mean(dvh, axis=-1, keepdims=True)
                          - vhat * jnp.mean(dvh * vhat, axis=-1, keepdims=True))
            d_ref[rows, D_A:2 * D_A] = (dvv * _dgelu(av, tv)).astype(BF16)
            dzs = dzs + dz
            dgl = dgl + jnp.sum(dvn * vhat, axis=0, keepdims=True)
            dbl = dbl + jnp.sum(dvn, axis=0, keepdims=True)

        @pl.when(i == 0)
        def _():
            for g in range(N_GROUPS):
                dws_ref[g] = jnp.where(causal, dws[g], 0.0)
            dz_ref[...] = dzs
            dgl_ref[...] = dgl
            dbl_ref[...] = dbl

        @pl.when(i != 0)
        def _():
            for g in range(N_GROUPS):
                dws_ref[g] += jnp.where(causal, dws[g], 0.0)
            dz_ref[...] += dzs
            dgl_ref[...] += dgl
            dbl_ref[...] += dbl

    full = lambda a: pl.BlockSpec(a.shape, lambda i: (0,) * a.ndim)
    acc = lambda s: pl.BlockSpec(s, lambda i: (0,) * len(s))
    return _call(
        body, name=name, grid=(T // tm,),
        in_specs=[pl.BlockSpec((tm, 2 * D_A), lambda i: (i, 0)), pl.BlockSpec((tm, D_A), lambda i: (i, 0)),
                  full(gln), full(bln), full(ws), full(bs_t)],
        out_specs=[pl.BlockSpec((tm, 2 * D_A), lambda i: (i, 0)), acc((N_GROUPS, CHUNK, CHUNK)), acc((CHUNK, D_A)),
                   acc((1, D_A)), acc((1, D_A))],
        out_shape=[jax.ShapeDtypeStruct((T, 2 * D_A), BF16), jax.ShapeDtypeStruct((N_GROUPS, CHUNK, CHUNK), F32),
                   jax.ShapeDtypeStruct((CHUNK, D_A), F32), jax.ShapeDtypeStruct((1, D_A), F32),
                   jax.ShapeDtypeStruct((1, D_A), F32)],
        args=[puv, dsgu, gln, bln, ws, bs_t], comms=comms)


def attn_bwd(pqkv, datt, gq_t, gk_t, sinks, *, seq, name, comms=()):
    T = pqkv.shape[0]
    nb = seq // CHUNK

    def body(p_ref, do_ref, gq_ref, gk_ref, sink_ref, d_ref, dgq_ref, dgk_ref, dsk_ref,
             k0, k1, v0, v1, dk0, dk1, dv0, dv1, dgq_s, dsk_s, bias_ref, s_ref, dp_ref, pr_ref, ds_ref):
        b = pl.program_id(0)
        krep, vrep, dkacc, dvacc = [k0, k1], [v0, v1], [dk0, dk1], [dv0, dv1]
        _fill_keys(p_ref, gk_ref, krep, vrep, seq)
        _fill_mask_bias(bias_ref)
        for h in range(N_KV):
            dkacc[h][...] = jnp.zeros_like(dkacc[h])
            dvacc[h][...] = jnp.zeros_like(dvacc[h])
        dgq_s[...] = jnp.zeros_like(dgq_s)
        dsk_s[...] = jnp.zeros_like(dsk_s)
        mean_q = _head_mean_matrix(D_B)
        lane = _iota((1, 128), 1)

        def block(i, carry):
            r0 = pl.multiple_of(i * CHUNK, CHUNK)
            first = jnp.minimum(i, 1)
            nq, rq = _head_rms(p_ref[pl.ds(r0, CHUNK), 0:D_B].astype(F32), mean_q)
            qn = nq * (gq_ref[...] * ATT_SCALE)
            dqn_parts = []
            for h in range(N_KV):
                cols = slice(h * KV_W, (h + 1) * KV_W)
                qs = _stack_heads(qn[:, cols])
                kk = krep[h][pl.ds(r0, 2 * CHUNK), :]
                dos = _stack_heads(do_ref[pl.ds(r0, CHUNK), cols].astype(F32))
                s_ref[...] = _dg(qs, kk, NT)
                dp_ref[...] = _dg(dos, vrep[h][pl.ds(r0, 2 * CHUNK), :], NT)

                def rows_of(c, carry2):
                    rows, probs, psink = _softmax_rows(s_ref, bias_ref, first, sink_ref, h, c)
                    dp = dp_ref[rows, :]
                    dr = jnp.sum(probs * dp, axis=-1, keepdims=True)
                    pr_ref[rows, :] = probs.astype(BF16)
                    ds_ref[rows, :] = (probs * (dp - dr)).astype(BF16)
                    head = h * Q_PER_KV + c // (CHUNK // SOFTMAX_ROWS)
                    dsk_s[...] += jnp.where(lane == head, -jnp.sum(psink * dr), 0.0)
                    return carry2

                lax.fori_loop(0, STACK // SOFTMAX_ROWS, rows_of, 0, unroll=True)
                dqn_parts.append(_unstack_heads(_dot(ds_ref[...], kk)))
                dkacc[h][pl.ds(r0, 2 * CHUNK), :] += _dg(ds_ref[...], qs, TN)
                dvacc[h][pl.ds(r0, 2 * CHUNK), :] += _dg(pr_ref[...], dos, TN)
            dqn = jnp.concatenate(dqn_parts, axis=1) * ATT_SCALE
            dn = dqn * gq_ref[...]
            d_ref[pl.ds(r0, CHUNK), 0:D_B] = (rq * (dn - nq * _dot_split(dn * nq, mean_q))).astype(BF16)
            dgq_s[...] += jnp.sum(dqn * nq, axis=0, keepdims=True)
            return carry

        lax.fori_loop(0, nb, block, 0)

        mean_k = _head_mean_matrix(KV2)
        folds = [_fold_matrix(h) for h in range(N_KV)]
        rows = min(seq, 4 * CHUNK)

        def piece(j, dgk):
            r0 = pl.multiple_of(j * rows, CHUNK)
            r1 = pl.multiple_of(r0 + CHUNK, CHUNK)
            dkn = _dot_split(dkacc[0][pl.ds(r1, rows), :], folds[0]) + _dot_split(dkacc[1][pl.ds(r1, rows), :], folds[1])
            dv = _dot_split(dvacc[0][pl.ds(r1, rows), :], folds[0]) + _dot_split(dvacc[1][pl.ds(r1, rows), :], folds[1])
            nk, rk = _head_rms(p_ref[pl.ds(r0, rows), K_OFF:K_OFF + KV2].astype(F32), mean_k)
            dn = dkn * gk_ref[...]
            d_ref[pl.ds(r0, rows), K_OFF:K_OFF + KV2] = (rk * (dn - nk * _dot_split(dn * nk, mean_k))).astype(BF16)
            d_ref[pl.ds(r0, rows), V_OFF:V_OFF + KV2] = dv.astype(BF16)
            return dgk + jnp.sum(dkn * nk, axis=0, keepdims=True)

        dgk = lax.fori_loop(0, seq // rows, piece, jnp.zeros((1, KV2), F32))

        @pl.when(b == 0)
        def _():
            dgq_ref[...] = dgq_s[...]
            dgk_ref[...] = dgk
            dsk_ref[...] = jnp.broadcast_to(dsk_s[...], dsk_ref.shape)

        @pl.when(b != 0)
        def _():
            dgq_ref[...] += dgq_s[...]
            dgk_ref[...] += dgk
            dsk_ref[...] += jnp.broadcast_to(dsk_s[...], dsk_ref.shape)

    acc = lambda s: pl.BlockSpec(s, lambda b: (0, 0))
    return _call(
        body, name=name, grid=(T // seq,),
        in_specs=[pl.BlockSpec((seq, QKV_W), lambda b: (b, 0)), pl.BlockSpec((seq, D_B), lambda b: (b, 0)),
                  pl.BlockSpec(gq_t.shape, lambda b: (0, 0)), pl.BlockSpec(gk_t.shape, lambda b: (0, 0)),
                  pl.BlockSpec(memory_space=pltpu.SMEM)],
        out_specs=[pl.BlockSpec((seq, QKV_W), lambda b: (b, 0)), acc((1, D_B)), acc((1, KV2)), acc((8, 128))],
        out_shape=[jax.ShapeDtypeStruct((T, QKV_W), BF16), jax.ShapeDtypeStruct((1, D_B), F32),
                   jax.ShapeDtypeStruct((1, KV2), F32), jax.ShapeDtypeStruct((8, 128), F32)],
        scratch_shapes=[pltpu.VMEM((seq + CHUNK, KV_W), BF16)] * 4 + [pltpu.VMEM((seq + CHUNK, KV_W), F32)] * 4
        + [pltpu.VMEM((1, D_B), F32), pltpu.VMEM((1, 128), F32), pltpu.VMEM((2, CHUNK, 2 * CHUNK), F32)]
        + [pltpu.VMEM((STACK, 2 * CHUNK), F32)] * 2 + [pltpu.VMEM((STACK, 2 * CHUNK), BF16)] * 2,
        args=[pqkv, datt, gq_t, gk_t, sinks], comms=comms)


def ada_fwd(c_all, w, b, *, name):
    nb, D = c_all.shape
    N = w.shape[1]
    tn = N // 3

    def body(c_ref, w_ref, b_ref, o_ref):
        c = c_ref[...]
        cond = (c * jax.nn.sigmoid(c)).astype(BF16)
        o_ref[...] = _dot(cond, w_ref[...].astype(BF16)) + b_ref[...]

    return _call(
        body, name=name, grid=(3,),
        in_specs=[pl.BlockSpec((nb, D), lambda j: (0, 0)), pl.BlockSpec((D, tn), lambda j: (0, j)),
                  pl.BlockSpec((1, tn), lambda j: (0, j))],
        out_specs=[pl.BlockSpec((nb, tn), lambda j: (0, j))], out_shape=[jax.ShapeDtypeStruct((nb, N), F32)],
        args=[c_all, w, b])[0]


def ada_bwd(c_all, dmods_all, dmods_mine, gain_rows, *, name, comms=()):
    nb, D = c_all.shape
    NA = dmods_all.shape[1]
    N = dmods_mine.shape[1]
    tn = N // 3

    def body(c_ref, da_ref, dm_ref, gr_ref, db_ref, dw_ref, dgn_ref):
        c = c_ref[...]
        cond = (c * jax.nn.sigmoid(c)).astype(BF16)
        dw_ref[...] = _dg(cond, dm_ref[...].astype(BF16), TN)
        db_ref[...] = jnp.sum(da_ref[...], axis=0, keepdims=True)
        total = gr_ref[0:1, :]
        for d in range(1, N_DEV):
            total = total + gr_ref[d:d + 1, :]
        dgn_ref[...] = total

    return _call(
        body, name=name, grid=(3,),
        in_specs=[pl.BlockSpec((nb, D), lambda j: (0, 0)), pl.BlockSpec((nb, NA), lambda j: (0, 0)),
                  pl.BlockSpec((nb, tn), lambda j: (0, j)), pl.BlockSpec((N_DEV, D), lambda j: (0, 0))],
        out_specs=[pl.BlockSpec((1, NA), lambda j: (0, 0)), pl.BlockSpec((D, tn), lambda j: (0, j)),
                   pl.BlockSpec((1, D), lambda j: (0, 0))],
        out_shape=[jax.ShapeDtypeStruct((1, NA), F32), jax.ShapeDtypeStruct((D, N), F32),
                   jax.ShapeDtypeStruct((1, D), F32)],
        args=[c_all, dmods_all, dmods_mine, gain_rows], comms=comms)


def adamw(items, *, steps, name, comms=()):
    n = len(items)
    c1 = 1.0 / (1.0 - ADAM_B1 ** ADAM_STEP)
    c2 = 1.0 / (1.0 - ADAM_B2 ** ADAM_STEP)

    def body(*refs):
        for j in range(n):
            w_ref, g_ref, m_ref, v_ref = refs[4 * j:4 * j + 4]
            go_ref, d_ref, mo_ref, vo_ref = refs[4 * n + 4 * j:4 * n + 4 * j + 4]
            gg = g_ref[...]
            mn = ADAM_B1 * m_ref[...] + (1.0 - ADAM_B1) * gg
            vn = ADAM_B2 * v_ref[...] + (1.0 - ADAM_B2) * (gg * gg)
            go_ref[...] = gg
            mo_ref[...] = mn
            vo_ref[...] = vn
            d_ref[...] = -ADAM_LR * ((mn * c1) / (jnp.sqrt(vn * c2) + ADAM_EPS) + ADAM_WD * w_ref[...])

    in_specs, out_specs, out_shape, args = [], [], [], []
    for item in items:
        R, C = item[0].shape
        blk = pl.BlockSpec((R // steps, C), lambda i: (i, 0))
        in_specs += [blk] * 4
        out_specs += [blk] * 4
        out_shape += [jax.ShapeDtypeStruct((R, C), F32)] * 4
        args += list(item)
    res = _call(body, name=name, grid=(steps,), in_specs=in_specs, out_specs=out_specs, out_shape=out_shape, args=args,
                comms=comms)
    return [tuple(res[4 * j:4 * j + 4]) for j in range(n)]


def _place():
    return lax.axis_index("x"), lax.axis_index("y"), lax.axis_index("c")


def _flip(v, bit):
    return 1 - v if bit else v


def _other_chips(mx, my):
    return [(_flip(mx, k & 2), _flip(my, k & 1)) for k in range(1, N_QUAD)]


def _remote(src, dst, send_sem, recv_sem, peer):
    return pltpu.make_async_remote_copy(src_ref=src, dst_ref=dst, send_sem=send_sem, recv_sem=recv_sem,
                                        device_id=peer, device_id_type=MESH)


def ag8_comm(x):
    def copies(srcs, lands, ss, rs, only_sends=False):
        mx, my, mc = _place()
        me = 4 * mx + 2 * my + mc
        local = pltpu.make_async_copy(srcs[0], lands[0].at[me], ss.at[N_DEV - 1])
        sends, recvs = [], []
        for k in range(1, N_DEV):
            peer = (_flip(mx, k & 4), _flip(my, k & 2), _flip(mc, k & 1))
            sends.append(_remote(srcs[0], lands[0].at[me], ss.at[k - 1], rs.at[k - 1], peer))
            if not only_sends:
                recvs.append(_remote(srcs[0], lands[0].at[4 * peer[0] + 2 * peer[1] + peer[2]], ss.at[k - 1], rs.at[k - 1], peer))
        return local, sends, recvs

    def start(srcs, bufs, lands, ss, rs):
        local, sends, _ = copies(srcs, lands, ss, rs, only_sends=True)
        local.start()
        for cp in sends:
            cp.start()

    def finish(srcs, bufs, lands, ss, rs):
        local, sends, recvs = copies(srcs, lands, ss, rs)
        for cp in recvs:
            cp.wait_recv()
        for cp in sends:
            cp.wait_send()
        local.wait()

    return Comm(srcs=[x], land_shapes=[jax.ShapeDtypeStruct((N_DEV,) + x.shape, x.dtype)], n_sems=N_DEV,
                start=start, finish=finish)


def sum8(stacked, *, name):
    _, r, n = stacked.shape

    def body(s_ref, o_ref):
        total = s_ref[0]
        for d in range(1, N_DEV):
            total = total + s_ref[d]
        o_ref[...] = total

    return _call(body, name=name, grid=(), in_specs=[pl.BlockSpec(memory_space=pltpu.VMEM)],
                 out_specs=[pl.BlockSpec(memory_space=pltpu.VMEM)], out_shape=[jax.ShapeDtypeStruct((r, n), F32)],
                 args=[stacked])[0]


def cast_into_stacks(ws, quad, *, name, comms=()):
    steps = 4

    def body(q_ref, *refs):
        for w_ref, o_ref in zip(refs[:len(ws)], refs[len(ws):]):
            o_ref[0] = w_ref[...].astype(BF16)

    return _call(
        body, name=name, grid=(steps,), prefetch=[quad],
        in_specs=[pl.BlockSpec((w.shape[0] // steps, w.shape[1]), lambda i, q: (i, 0)) for w in ws],
        out_specs=[pl.BlockSpec((1, w.shape[0] // steps, w.shape[1]), lambda i, q: (q[0], i, 0)) for w in ws],
        out_shape=[jax.ShapeDtypeStruct((N_QUAD,) + w.shape, BF16) for w in ws], args=list(ws), comms=comms)


def gather_comm(stacks, forward=True):
    n = len(stacks)

    def copies(bufs, ss, rs, only_sends=False):
        mx, my, mc = _place()
        q = 2 * mx + my
        sibling = (mx, my, 1 - mc)
        ici_send, ici_recv, fwd_send, fwd_recv = [], [], [], []
        for p in range(n):
            hr = stacks[p].shape[1] // 2
            mine, other = pl.ds(mc * hr, hr), pl.ds((1 - mc) * hr, hr)
            for k, chip in enumerate(_other_chips(mx, my)):
                qk = 2 * chip[0] + chip[1]
                peer = (chip[0], chip[1], mc)
                own, landed, theirs = bufs[p].at[q, mine, :], bufs[p].at[qk, mine, :], bufs[p].at[qk, other, :]
                ici_send.append(_remote(own, own, ss.at[6 * p + k], rs.at[6 * p + k], peer))
                if only_sends:
                    continue
                ici_recv.append(_remote(own, landed, ss.at[6 * p + k], rs.at[6 * p + k], peer))
                if forward:
                    fwd_send.append(_remote(landed, landed, ss.at[6 * p + 3 + k], rs.at[6 * p + 3 + k], sibling))
                    fwd_recv.append(_remote(theirs, theirs, ss.at[6 * p + 3 + k], rs.at[6 * p + 3 + k], sibling))
        return ici_send, ici_recv, fwd_send, fwd_recv

    def start(srcs, bufs, lands, ss, rs):
        for cp in copies(bufs, ss, rs, only_sends=True)[0]:
            cp.start()

    def finish(srcs, bufs, lands, ss, rs):
        ici_send, ici_recv, fwd_send, fwd_recv = copies(bufs, ss, rs)
        for j, arrived in enumerate(ici_recv):
            arrived.wait_recv()
            if forward:
                fwd_send[j].start()
        for cp in fwd_recv:
            cp.wait_recv()
        for cp in ici_send + fwd_send:
            cp.wait_send()

    return Comm(bufs=stacks, n_sems=6 * n, start=start, finish=finish)


def forward_comm(stacks):
    n = len(stacks)

    def copies(bufs, ss, rs, only_sends=False):
        mx, my, mc = _place()
        sibling = (mx, my, 1 - mc)
        send, recv = [], []
        for p in range(n):
            hr = stacks[p].shape[1] // 2
            for k, chip in enumerate(_other_chips(mx, my)):
                qk = 2 * chip[0] + chip[1]
                landed, theirs = bufs[p].at[qk, pl.ds(mc * hr, hr), :], bufs[p].at[qk, pl.ds((1 - mc) * hr, hr), :]
                send.append(_remote(landed, landed, ss.at[3 * p + k], rs.at[3 * p + k], sibling))
                if not only_sends:
                    recv.append(_remote(theirs, theirs, ss.at[3 * p + k], rs.at[3 * p + k], sibling))
        return send, recv

    def start(srcs, bufs, lands, ss, rs):
        for cp in copies(bufs, ss, rs, only_sends=True)[0]:
            cp.start()

    def finish(srcs, bufs, lands, ss, rs):
        send, recv = copies(bufs, ss, rs)
        for cp in recv:
            cp.wait_recv()
        for cp in send:
            cp.wait_send()

    return Comm(bufs=stacks, n_sems=3 * n, start=start, finish=finish)


def rs_pair_comm(gs):
    n = len(gs)

    def copies(srcs, lands, ss, rs):
        mx, my, mc = _place()
        out = []
        for p in range(n):
            hr = gs[p].shape[1] // 2
            out.append(_remote(srcs[p].at[:, pl.ds((1 - mc) * hr, hr), :], lands[p], ss.at[p], rs.at[p], (mx, my, 1 - mc)))
        return out

    def start(srcs, bufs, lands, ss, rs):
        for cp in copies(srcs, lands, ss, rs):
            cp.start()

    def finish(srcs, bufs, lands, ss, rs):
        for cp in copies(srcs, lands, ss, rs):
            cp.wait()

    return Comm(srcs=gs, land_shapes=[jax.ShapeDtypeStruct((g.shape[0], g.shape[1] // 2, g.shape[2]), g.dtype) for g in gs],
                n_sems=n, start=start, finish=finish)


def rs_chip_comm(ss_):
    n = len(ss_)

    def copies(srcs, lands, ss, rs):
        mx, my, mc = _place()
        out = []
        for p in range(n):
            for k, chip in enumerate(_other_chips(mx, my)):
                out.append(_remote(srcs[p].at[2 * chip[0] + chip[1]], lands[p].at[k], ss.at[3 * p + k], rs.at[3 * p + k],
                                   (chip[0], chip[1], mc)))
        return out

    def start(srcs, bufs, lands, ss, rs):
        for cp in copies(srcs, lands, ss, rs):
            cp.start()

    def finish(srcs, bufs, lands, ss, rs):
        for cp in copies(srcs, lands, ss, rs):
            cp.wait()

    return Comm(srcs=ss_, land_shapes=[jax.ShapeDtypeStruct((N_QUAD - 1,) + s.shape[1:], s.dtype) for s in ss_],
                n_sems=3 * n, start=start, finish=finish)


def rs_share_comm(fulls):
    n = len(fulls)

    def copies(bufs, ss, rs, only_sends=False):
        mx, my, mc = _place()
        send, recv = [], []
        for p in range(n):
            hr = fulls[p].shape[0] // 2
            mine, other = bufs[p].at[pl.ds(mc * hr, hr), :], bufs[p].at[pl.ds((1 - mc) * hr, hr), :]
            send.append(_remote(mine, mine, ss.at[p], rs.at[p], (mx, my, 1 - mc)))
            if not only_sends:
                recv.append(_remote(other, other, ss.at[p], rs.at[p], (mx, my, 1 - mc)))
        return send, recv

    def start(srcs, bufs, lands, ss, rs):
        for cp in copies(bufs, ss, rs, only_sends=True)[0]:
            cp.start()

    def finish(srcs, bufs, lands, ss, rs):
        send, recv = copies(bufs, ss, rs)
        for cp in recv:
            cp.wait_recv()
        for cp in send:
            cp.wait_send()

    return Comm(bufs=fulls, n_sems=n, start=start, finish=finish)


def pair_sums(gs, recvs, mc, *, name):
    n = len(gs)

    def body(s_ref, *refs):
        for p in range(n):
            g_ref, r_ref, o_ref = refs[2 * p], refs[2 * p + 1], refs[2 * n + p]
            o_ref[...] = (g_ref[...].astype(F32) + r_ref[...].astype(F32)).astype(BF16)

    in_specs, out_specs, out_shape, args = [], [], [], []
    for g, r in zip(gs, recvs):
        nq, hr, C = r.shape
        in_specs += [pl.BlockSpec((1, hr, C), lambda q, s: (q, s[0], 0)), pl.BlockSpec((1, hr, C), lambda q, s: (q, 0, 0))]
        out_specs.append(pl.BlockSpec((1, hr, C), lambda q, s: (q, 0, 0)))
        out_shape.append(jax.ShapeDtypeStruct((nq, hr, C), BF16))
        args += [g, r]
    return _call(body, name=name, grid=(N_QUAD,), in_specs=in_specs, out_specs=out_specs, out_shape=out_shape,
                 args=args, prefetch=[mc])


def chip_sums(ss, recvs, quad_mc, *, name):
    n = len(ss)
    steps = 2

    def body(q_ref, *refs):
        for p in range(n):
            s_ref, r_ref, o_ref = refs[2 * p], refs[2 * p + 1], refs[2 * n + p]
            total = s_ref[0].astype(F32)
            for k in range(N_QUAD - 1):
                total = total + r_ref[k].astype(F32)
            o_ref[...] = total

    in_specs, out_specs, out_shape, args = [], [], [], []
    for s, r in zip(ss, recvs):
        _, hr, C = s.shape
        rb = hr // steps
        in_specs += [pl.BlockSpec((1, rb, C), lambda i, q: (q[0], i, 0)),
                     pl.BlockSpec((N_QUAD - 1, rb, C), lambda i, q: (0, i, 0))]
        out_specs.append(pl.BlockSpec((rb, C), lambda i, q: (q[1] * steps + i, 0)))
        out_shape.append(jax.ShapeDtypeStruct((2 * hr, C), F32))
        args += [s, r]
    return _call(body, name=name, grid=(steps,), in_specs=in_specs, out_specs=out_specs, out_shape=out_shape,
                 args=args, prefetch=[quad_mc])


def _stack_cols(w_full):
    K, N = w_full.shape
    return w_full.reshape(K, N_QUAD, N // N_QUAD).transpose(1, 0, 2)


def _unstack_cols(w4):
    nq, K, n = w4.shape
    return w4.transpose(1, 0, 2).reshape(K, nq * n)


def kernel(x, c, w_ada, b_ada, g_norm1, ffn1_w_gate, ffn1_w_up, ffn1_w_down, g_norm2, w_in, g_sgu_ln, b_sgu_ln, w_spatial, b_spatial, g_q, g_k, attn_sinks, w_branch_a, w_branch_b, w_out, g_norm3, ffn2_w_gate, ffn2_w_up, ffn2_w_down, loss_target, m_w_ada, m_b_ada, m_g_norm1, m_ffn1_w_gate, m_ffn1_w_up, m_ffn1_w_down, m_g_norm2, m_w_in, m_g_sgu_ln, m_b_sgu_ln, m_w_spatial, m_b_spatial, m_g_q, m_g_k, m_attn_sinks, m_w_branch_a, m_w_branch_b, m_w_out, m_g_norm3, m_ffn2_w_gate, m_ffn2_w_up, m_ffn2_w_down, v_w_ada, v_b_ada, v_g_norm1, v_ffn1_w_gate, v_ffn1_w_up, v_ffn1_w_down, v_g_norm2, v_w_in, v_g_sgu_ln, v_b_sgu_ln, v_w_spatial, v_b_spatial, v_g_q, v_g_k, v_attn_sinks, v_w_branch_a, v_w_branch_b, v_w_out, v_g_norm3, v_ffn2_w_gate, v_ffn2_w_up, v_ffn2_w_down):
    B, S, D = x.shape
    T = B * S
    n_mod = b_ada.shape[1] // D
    mx, my, mc = _place()
    quad = 2 * mx + my
    mc_arr = jnp.reshape(mc, (1,)).astype(jnp.int32)
    quad_arr = jnp.reshape(quad, (1,)).astype(jnp.int32)
    quad_mc = jnp.stack([quad, mc]).astype(jnp.int32)
    tm = min(512, S)
    tm_big = min(1024, S)
    tt_half = min(2048, T)
    cpb = min(4, S // CHUNK)

    xt = x.reshape(T, D)
    tgt = loss_target.reshape(T, D)

    tr = lambda a: jnp.swapaxes(a, 1, 2)
    stack_wg1, stack_wu1 = cast_into_stacks([tr(ffn1_w_gate)[0], tr(ffn1_w_up)[0]], quad_arr, name="stack_gu1")
    gather_wg1, gather_wu1 = gather_comm([stack_wg1], forward=False), gather_comm([stack_wu1])
    later = [ffn1_w_down, tr(w_in), w_branch_a, w_branch_b, w_out, tr(ffn2_w_gate), tr(ffn2_w_up), ffn2_w_down]
    gather_cond = ag8_comm(c)
    stacks = cast_into_stacks([w[0] for w in later[0:4]], quad_arr, name="stack_a", comms=[gather_wg1, gather_cond])
    forward_wg1 = forward_comm(gather_wg1.bufs_out)
    c_all = gather_cond.lands[0].reshape(N_DEV * B, D)
    n_ada = w_ada.shape[2]
    b_mine = lax.dynamic_slice(b_ada, (0, quad * n_ada), (1, n_ada))
    mods_part = ada_fwd(c_all, w_ada[0], b_mine, name="ada_fwd")
    gather_mods = ag8_comm(mods_part)
    stacks += cast_into_stacks([w[0] for w in later[4:8]], quad_arr, name="stack_b",
                               comms=[gather_wu1, gather_mods, forward_wg1])
    (wg1,), (wu1,), (mods_parts,) = forward_wg1.bufs_out, gather_wu1.bufs_out, gather_mods.lands
    gather_wd1, gather_win = gather_comm([stacks[0]]), gather_comm([stacks[1]])
    gather_abo = gather_comm(stacks[2:5], forward=False)
    gather_wg3, gather_wu3 = gather_comm([stacks[5]], forward=False), gather_comm([stacks[6]], forward=False)
    gather_wd3 = gather_comm([stacks[7]])
    mods = jnp.concatenate([mods_parts[2 * j] for j in range(N_QUAD)], axis=1)
    me = 4 * mx + 2 * my + mc
    mods = lax.dynamic_slice(mods, (me * B, 0), (B, n_mod * D))
    mods = mods.reshape(B, n_mod, 1, D)
    sh1, sc1, ga1, sh2, sc2, ga2, sh3, sc3, ga3 = [(mods, j) for j in range(n_mod)]
    bs_t = b_spatial[0].T
    ws = w_spatial[0]

    xn1 = norm_mod_fwd(xt, g_norm1, sc1, sh1, seq=S, tm=tm, name="norm1")
    g1, u1, a1 = ffn_up(xn1, wg1, wu1, tm=tm_big, name="ffn1_up", comms=[gather_wd1, gather_abo])
    (wd1,) = gather_wd1.bufs_out
    forward_abo = forward_comm(gather_abo.bufs_out)
    h1, f1, xn2 = ffn_down(a1, wd1, xt, ga1, norm=(g_norm2, sc2, sh2), seq=S, tm=tm, name="ffn1_down",
                           comms=[gather_win, forward_abo])
    (win4,), (wa4, wb4, wo4) = gather_win.bufs_out, forward_abo.bufs_out
    wa, wb = _unstack_cols(wa4), _unstack_cols(wb4)
    wo = wo4.reshape(D, D)
    win = win4.reshape(-1, D)
    groups = [(0, 2 * D_A), (2 * D_A, 2 * D_A + QKV_W), (2 * D_A + QKV_W, win.shape[0])]
    puv, pqkv, pgt = proj_fwd(xn2, win, groups, tm=tm, name="proj", comms=[gather_wg3])
    forward_wg3 = forward_comm(gather_wg3.bufs_out)
    sgu = sgu_fwd(puv, g_sgu_ln, b_sgu_ln, ws, bs_t, cpb=cpb, name="sgu_fwd")
    gq_t, gk_t = jnp.tile(g_q, (1, N_Q)), jnp.tile(g_k, (1, N_KV))
    att = attn_fwd(pqkv, gq_t, gk_t, attn_sinks, seq=S, name="attn_fwd", comms=[gather_wu3, forward_wg3])
    (wg3,) = forward_wg3.bufs_out
    forward_wu3 = forward_comm(gather_wu3.bufs_out)
    ya, yb, merged, mm, h2, xn3 = mixer_out_fwd(sgu, att, pgt, h1, ga2, wa, wb, wo, (g_norm3, sc3, sh3), seq=S,
                                                tm=tm, name="mixer_out", comms=[gather_wd3, forward_wu3])
    (wd3,), (wu3,) = gather_wd3.bufs_out, forward_wu3.bufs_out
    g3, u3, a3 = ffn_up(xn3, wg3, wu3, tm=tm_big, name="ffn2_up")
    dy, f3, lparts = ffn_down(a3, wd3, h2, ga3, target=tgt, seq=S, tm=tm, name="ffn2_down_loss")
    loss_part = jnp.sum(lparts[:, 0, 0])

    def sums_of(pair, tag):
        return pair_sums(pair.srcs, pair.lands, mc_arr, name=f"pair_sum_{tag}")

    def halves_of(*chips_and_tag):
        *chips, tag = chips_and_tag
        return chip_sums([s for ch in chips for s in ch.srcs], [r for ch in chips for r in ch.lands], quad_mc,
                         name=f"chip_sum_{tag}")

    dg3, du3, df3, dga3 = ffn_bwd_act(dy, f3, ga3, wd3, g3, u3, seq=S, tm=tm, name="ffn2_act_bwd")
    (dwd3,) = mm_tn([(a3, df3)], tt=T, name="ffn2_dwd")
    pair_wd3 = rs_pair_comm([dwd3])
    dwg3, dwu3 = mm_tn([(dg3, xn3), (du3, xn3)], tt=tt_half, name="ffn2_dwgu", comms=[pair_wd3])
    chip_wd3 = rs_chip_comm(sums_of(pair_wd3, "wd3"))
    pair_gu3 = rs_pair_comm([dwg3, dwu3])
    dh2, dsh3, dsc3, dgn3 = dx_norm_bwd([dg3, du3], [wg3, wu3], h2, dy, g_norm3, sc3, seq=S, tm=tm, name="ffn2_dx",
                                        comms=[chip_wd3, pair_gu3])
    sum_wg3, sum_wu3 = sums_of(pair_gu3, "gu3")
    chip_wg3, chip_wu3 = rs_chip_comm([sum_wg3]), rs_chip_comm([sum_wu3])

    dgt, dya, dyb, dsgu, datt, dm, dga2 = mixer_out_bwd(dh2, mm, ga2, ya, yb, pgt, wa, wb, wo, seq=S, tm=tm,
                                                        name="mixer_out_bwd", comms=[chip_wg3])
    dwo, dwa, dwb = mm_tn([(merged, dm), (sgu, dya), (att, dyb)], tt=tm_big, name="mixer_dw")
    pair_abo = rs_pair_comm([_stack_cols(dwa), _stack_cols(dwb), dwo.reshape(N_QUAD, D // N_QUAD, D)])
    duv, dws, dzs, dgln, dbln = sgu_bwd(puv, dsgu, g_sgu_ln, b_sgu_ln, ws, bs_t, cpb=cpb, name="sgu_bwd",
                                        comms=[pair_abo])
    chip_abo = rs_chip_comm(sums_of(pair_abo, "abo"))
    dqkv, dgq_h, dgk_h, dsk = attn_bwd(pqkv, datt, gq_t, gk_t, attn_sinks, seq=S, name="attn_bwd",
                                       comms=[chip_wu3, chip_abo])
    dgq = dgq_h.reshape(N_Q, HEAD_DIM).sum(axis=0, keepdims=True)
    dgk = dgk_h.reshape(N_KV, HEAD_DIM).sum(axis=0, keepdims=True)
    share3 = rs_share_comm(halves_of(chip_wg3, chip_wu3, chip_wd3, "ffn2"))
    dwin_uv, dwin_qkv = mm_tn([(duv, xn2), (dqkv, xn2)], tt=tt_half, name="mixer_dwin_a", comms=[share3])
    (dwin_gt,) = mm_tn([(dgt, xn2)], tt=tt_half, name="mixer_dwin_b")
    dwin_parts = [dwin_uv, dwin_qkv, dwin_gt]
    r_wg3, r_wu3, r_wd3 = share3.bufs_out
    pair_win = rs_pair_comm([jnp.concatenate(dwin_parts, axis=0).reshape(win4.shape)])
    dh1, dsh2, dsc2, dgn2 = dx_norm_bwd([duv, dqkv, dgt], [(win, r0, r1) for r0, r1 in groups], h1, dh2, g_norm2, sc2, seq=S,
                                        tm=tm, name="mixer_dx", comms=[pair_win])
    chip_win = rs_chip_comm(sums_of(pair_win, "win"))

    dg1, du1, df1, dga1 = ffn_bwd_act(dh1, f1, ga1, wd1, g1, u1, seq=S, tm=tm, name="ffn1_act_bwd", comms=[chip_win])
    share_mix = rs_share_comm(halves_of(chip_win, chip_abo, "mix"))

    db_s = dzs.reshape(CHUNK, N_GROUPS, D_A // N_GROUPS).sum(axis=2).T.reshape(1, N_GROUPS * CHUNK)
    tail = jnp.concatenate([db_s, dgq, dgk, dsk[0:1, 0:N_Q], loss_part.reshape(1, 1)], axis=1)
    tail = jnp.pad(tail, ((0, 0), (0, (-tail.shape[1]) % D)))
    lnrow = jnp.concatenate([dgln, dbln], axis=1)
    lnrow = jnp.pad(lnrow, ((0, 0), (0, (-lnrow.shape[1]) % D)))
    packed = jnp.concatenate([dgn2, dgn3, lnrow.reshape(-1, D), tail.reshape(-1, D), dws.reshape(-1, D)], axis=0)
    n_rows = packed.shape[0]
    packed = jnp.pad(packed, ((0, (-n_rows) % 8), (0, 0)))
    gather_small = ag8_comm(packed)

    dwg1, dwu1 = mm_tn([(dg1, xn1), (du1, xn1)], tt=tt_half, name="ffn1_dwgu", comms=[share_mix, gather_small])
    r_win, r_wa, r_wb, r_wo = share_mix.bufs_out
    small = sum8(gather_small.lands[0], name="sum_small")
    pair_gu1 = rs_pair_comm([dwg1, dwu1])
    (dwd1,) = mm_tn([(a1, df1)], tt=T, name="ffn1_dwd", comms=[pair_gu1])
    chip_gu1 = rs_chip_comm(sums_of(pair_gu1, "gu1"))
    pair_wd1 = rs_pair_comm([dwd1])
    dx, dsh1, dsc1, dgn1 = dx_norm_bwd([dg1, du1], [wg1, wu1], xt, dh1, g_norm1, sc1, seq=S, tm=tm, name="ffn1_dx",
                                       comms=[chip_gu1, pair_wd1])
    chip_wd1 = rs_chip_comm(sums_of(pair_wd1, "wd1"))
    share_gu1 = rs_share_comm(halves_of(chip_gu1, "gu1"))

    names = ["w_ada", "b_ada", "g_norm1", "ffn1_w_gate", "ffn1_w_up", "ffn1_w_down", "g_norm2", "w_in", "g_sgu_ln",
             "b_sgu_ln", "w_spatial", "b_spatial", "g_q", "g_k", "attn_sinks", "w_branch_a", "w_branch_b", "w_out",
             "g_norm3", "ffn2_w_gate", "ffn2_w_up", "ffn2_w_down"]
    weights = dict(zip(names, [w_ada, b_ada, g_norm1, ffn1_w_gate, ffn1_w_up, ffn1_w_down, g_norm2, w_in, g_sgu_ln,
                               b_sgu_ln, w_spatial, b_spatial, g_q, g_k, attn_sinks, w_branch_a, w_branch_b, w_out,
                               g_norm3, ffn2_w_gate, ffn2_w_up, ffn2_w_down]))
    m_in = dict(zip(names, [m_w_ada, m_b_ada, m_g_norm1, m_ffn1_w_gate, m_ffn1_w_up, m_ffn1_w_down, m_g_norm2, m_w_in,
                            m_g_sgu_ln, m_b_sgu_ln, m_w_spatial, m_b_spatial, m_g_q, m_g_k, m_attn_sinks, m_w_branch_a,
                            m_w_branch_b, m_w_out, m_g_norm3, m_ffn2_w_gate, m_ffn2_w_up, m_ffn2_w_down]))
    v_in = dict(zip(names, [v_w_ada, v_b_ada, v_g_norm1, v_ffn1_w_gate, v_ffn1_w_up, v_ffn1_w_down, v_g_norm2, v_w_in,
                            v_g_sgu_ln, v_b_sgu_ln, v_w_spatial, v_b_spatial, v_g_q, v_g_k, v_attn_sinks, v_w_branch_a,
                            v_w_branch_b, v_w_out, v_g_norm3, v_ffn2_w_gate, v_ffn2_w_up, v_ffn2_w_down]))
    transposed = ("ffn1_w_gate", "ffn1_w_up", "w_in", "ffn2_w_gate", "ffn2_w_up")
    grads, delta, new_m, new_v = {}, {}, {}, {}

    def update(group, steps, name, comms=()):
        form = lambda nm, a: (tr(a) if nm in transposed else a)[0].reshape(group[nm].shape)
        res = adamw([(form(nm, weights[nm]), g, form(nm, m_in[nm]), form(nm, v_in[nm])) for nm, g in group.items()],
                    steps=steps, name=name, comms=comms)
        back = lambda nm, a: tr(a[None]) if nm in transposed else a.reshape(weights[nm].shape)
        for nm, (g, d, mn, vn) in zip(group, res):
            grads[nm], delta[nm], new_m[nm], new_v[nm] = back(nm, g), back(nm, d), back(nm, mn), back(nm, vn)

    dmods = jnp.concatenate([dsh1, dsc1, dga1, dsh2, dsc2, dga2, dsh3, dsc3, dga3], axis=1)
    dmods = jnp.concatenate([dmods, jnp.pad(dgn1, ((0, 0), (0, (n_mod - 1) * D)))], axis=0)
    gather_dmods = ag8_comm(dmods)
    run_comms([chip_wd1, share_gu1, gather_dmods], name="rs_tail")
    update({"ffn2_w_gate": r_wg3, "ffn2_w_up": r_wu3, "ffn2_w_down": r_wd3, "w_out": r_wo, "w_branch_a": r_wa,
            "w_branch_b": r_wb}, 8, "adamw_early")
    r_wg1, r_wu1 = share_gu1.bufs_out
    (gathered,) = gather_dmods.lands
    dmods_all = gathered[:, 0:B].reshape(N_DEV * B, n_mod * D)
    dmods_mine = lax.dynamic_slice(dmods_all, (0, quad * n_ada), (N_DEV * B, n_ada))
    share_wd1 = rs_share_comm(halves_of(chip_wd1, "wd1"))
    db_ada, dw_ada, g_gn1 = ada_bwd(c_all, dmods_all, dmods_mine, gathered[:, B, 0:D], name="ada_bwd", comms=[share_wd1])
    (r_wd1,) = share_wd1.bufs_out

    ln_rows = lnrow.size // D
    tail_rows = tail.size // D
    r = 2
    g_gn2, g_gn3 = small[0:1], small[1:2]
    ln_flat = small[r:r + ln_rows].reshape(1, -1)
    r += ln_rows
    tail_flat = small[r:r + tail_rows].reshape(1, -1)
    r += tail_rows
    g_ws = small[r:r + dws.size // D].reshape(w_spatial.shape)
    g_gln, g_bln = ln_flat[:, 0:D_A], ln_flat[:, D_A:2 * D_A]
    o = N_GROUPS * CHUNK
    g_bs = tail_flat[:, 0:o].reshape(b_spatial.shape)
    g_gq, g_gk, g_sk = tail_flat[:, o:o + HEAD_DIM], tail_flat[:, o + HEAD_DIM:o + 2 * HEAD_DIM], tail_flat[:, o + 2 * HEAD_DIM:o + 2 * HEAD_DIM + N_Q]
    loss = tail_flat[0, o + 2 * HEAD_DIM + N_Q]

    update({"w_ada": dw_ada}, 16, "adamw_w_ada")
    update({"ffn1_w_gate": r_wg1, "ffn1_w_up": r_wu1, "ffn1_w_down": r_wd1, "w_in": r_win}, 8, "adamw_late")

    update({"b_ada": db_ada, "g_norm1": g_gn1, "g_norm2": g_gn2, "g_norm3": g_gn3, "g_sgu_ln": g_gln,
            "b_sgu_ln": g_bln, "w_spatial": g_ws.reshape(-1, CHUNK), "b_spatial": g_bs.reshape(N_GROUPS, CHUNK),
            "g_q": g_gq, "g_k": g_gk, "attn_sinks": g_sk}, 1, "adamw_small")

    return (loss, dx.reshape(B, S, D), *[grads[nm] for nm in names], *[delta[nm] for nm in names],
            *[new_m[nm] for nm in names], *[new_v[nm] for nm in names])
```

```python
import functools
import math
import operator

import jax
import jax.numpy as jnp
from jax import lax
from jax.experimental import pallas as pl
from jax.experimental.pallas import tpu as pltpu

F32 = jnp.float32
BF16 = jnp.bfloat16
EPS = 1e-6
NEG = -1e30
N_DEV = 8
N_QUAD = 4
D_A = 512
N_GROUPS = 4
CHUNK = 128
HEAD_DIM = 64
N_KV = 2
Q_PER_KV = 4
N_Q = N_KV * Q_PER_KV
D_B = N_Q * HEAD_DIM
QKV_W = D_B + 2 * N_KV * HEAD_DIM
K_OFF = D_B
V_OFF = D_B + N_KV * HEAD_DIM
ATT_SCALE = HEAD_DIM ** -0.5
GELU_K = math.sqrt(2.0 / math.pi)
GELU_C = 0.044715
ADAM_LR, ADAM_B1, ADAM_B2, ADAM_EPS, ADAM_WD, ADAM_STEP = 0.001, 0.9, 0.999, 1e-08, 0.01, 10
VMEM_LIMIT_BYTES = 56 * 1024 * 1024
MESH = pl.DeviceIdType.MESH
NT = (((1,), (1,)), ((), ()))
TN = (((0,), (0,)), ((), ()))
ANY = pl.BlockSpec(memory_space=pl.ANY)


def _dot(a, b):
    return jnp.dot(a, b, preferred_element_type=F32)


def _dg(a, b, dims):
    return lax.dot_general(a, b, dims, preferred_element_type=F32)


def _gelu_parts(x):
    t = jnp.tanh(GELU_K * (x + GELU_C * x * x * x))
    return 0.5 * x * (1.0 + t), t


def _dgelu(x, t):
    return 0.5 * (1.0 + t) + 0.5 * x * (1.0 - t * t) * (GELU_K * (1.0 + 3.0 * GELU_C * x * x))


def _row_block(rows, target):
    b = min(rows, target)
    while rows % b or b % 8:
        b -= 1
    return b


def _mod_spec(mod, tps):
    mods, j = mod
    return pl.BlockSpec((1, None, 1, mods.shape[3]), lambda i: (i // tps, j, 0, 0))


def _resident(a):
    return pl.BlockSpec(a.shape, lambda *_: (0,) * a.ndim, pipeline_mode=pl.Buffered(1))


class Comm:
    def __init__(self, *, srcs=(), bufs=(), land_shapes=(), n_sems, start, finish):
        self.srcs, self.bufs, self.land_shapes = list(srcs), list(bufs), list(land_shapes)
        self.n_sems, self.start, self.finish = n_sems, start, finish
        self.bufs_out, self.lands = None, None


def _call(body, *, name, grid, in_specs, out_specs, out_shape, args, scratch_shapes=(), comms=(), prefetch=()):
    prefetch = list(prefetch)
    in_specs, out_specs, out_shape = list(in_specs), list(out_specs), list(out_shape)
    args, scratch = list(args), list(scratch_shapes)
    n_in, n_out, n_scr = len(args), len(out_shape), len(scratch)
    aliases, layout = {}, []
    for cm in comms:
        i0, o0, s0 = len(args), len(out_shape), len(scratch)
        args += cm.srcs + cm.bufs
        in_specs += [ANY] * (len(cm.srcs) + len(cm.bufs))
        out_shape += [jax.ShapeDtypeStruct(b.shape, b.dtype) for b in cm.bufs] + cm.land_shapes
        out_specs += [ANY] * (len(cm.bufs) + len(cm.land_shapes))
        for j in range(len(cm.bufs)):
            aliases[len(prefetch) + i0 + len(cm.srcs) + j] = o0 + j
        scratch += [pltpu.SemaphoreType.DMA((cm.n_sems,)), pltpu.SemaphoreType.DMA((cm.n_sems,))]
        layout.append((i0, o0, s0))
    n_in_all, n_out_all = len(args), len(out_shape)

    def wrapped(*refs):
        scalars, refs = refs[:len(prefetch)], refs[len(prefetch):]
        ins, outs, scr = refs[:n_in_all], refs[n_in_all:n_in_all + n_out_all], refs[n_in_all + n_out_all:]
        parts = []
        for cm, (i0, o0, s0) in zip(comms, layout):
            nb = len(cm.bufs)
            parts.append((ins[i0:i0 + len(cm.srcs)], outs[o0:o0 + nb], outs[o0 + nb:o0 + nb + len(cm.land_shapes)],
                          scr[s0], scr[s0 + 1]))
        if comms and grid:
            ids = [pl.program_id(d) for d in range(len(grid))]
            first = functools.reduce(operator.and_, [i == 0 for i in ids])
            last = functools.reduce(operator.and_, [i == g - 1 for i, g in zip(ids, grid)])

            @pl.when(first)
            def _():
                for cm, p in zip(comms, parts):
                    cm.start(*p)
        elif comms:
            for cm, p in zip(comms, parts):
                cm.start(*p)
        if body is not None:
            body(*scalars, *ins[:n_in], *outs[:n_out], *scr[:n_scr])
        if comms and grid:
            @pl.when(last)
            def _():
                for cm, p in zip(comms, parts):
                    cm.finish(*p)
        elif comms:
            for cm, p in zip(comms, parts):
                cm.finish(*p)

    if prefetch:
        kwargs = dict(grid_spec=pltpu.PrefetchScalarGridSpec(
            num_scalar_prefetch=len(prefetch), grid=grid, in_specs=in_specs, out_specs=out_specs, scratch_shapes=scratch))
    else:
        kwargs = dict(in_specs=in_specs, out_specs=out_specs, scratch_shapes=scratch, **(dict(grid=grid) if grid else {}))
    sem = ("arbitrary",) * len(grid)
    res = pl.pallas_call(
        wrapped, name=name, out_shape=out_shape, input_output_aliases=aliases,
        compiler_params=pltpu.CompilerParams(dimension_semantics=sem, vmem_limit_bytes=VMEM_LIMIT_BYTES), **kwargs,
    )(*prefetch, *args)
    for cm, (i0, o0, s0) in zip(comms, layout):
        nb = len(cm.bufs)
        cm.bufs_out = list(res[o0:o0 + nb])
        cm.lands = list(res[o0 + nb:o0 + nb + len(cm.land_shapes)])
    return list(res[:n_out])


def run_comms(comms, *, name):
    _call(None, name=name, grid=(), in_specs=[], out_specs=[], out_shape=[], args=[], comms=comms)


def norm_mod_fwd(h, g, sc, sh, *, seq, tm, name, comms=()):
    T, D = h.shape
    B, tps = T // seq, seq // tm

    def body(h_ref, g_ref, sc_ref, sh_ref, o_ref):
        x = h_ref[...]
        r = lax.rsqrt(jnp.mean(x * x, axis=-1, keepdims=True) + EPS)
        y = x * r * g_ref[...]
        o_ref[...] = (y * (1.0 + sc_ref[0]) + sh_ref[0]).astype(o_ref.dtype)

    return _call(
        body, name=name, grid=(T // tm,),
        in_specs=[pl.BlockSpec((tm, D), lambda i: (i, 0)), pl.BlockSpec((1, D), lambda i: (0, 0)),
                  _mod_spec(sc, tps), _mod_spec(sh, tps)],
        out_specs=[pl.BlockSpec((tm, D), lambda i: (i, 0))], out_shape=[jax.ShapeDtypeStruct((T, D), BF16)],
        args=[h, g, sc[0], sh[0]], comms=comms)[0]


def ffn_up(xn, wg, wu, *, tm, name, comms=()):
    T, D = xn.shape
    nq, fs, _ = wg.shape

    def body(x_ref, wg_ref, wu_ref, s_ref, q_ref, a_ref):
        x = x_ref[...]
        g = _dg(x, wg_ref[0], NT)
        u = _dg(x, wu_ref[0], NT)
        sg = jax.nn.sigmoid(g)
        s = g * sg
        s_ref[0] = s.astype(BF16)
        q_ref[0] = (u * (sg + s * (1.0 - sg))).astype(BF16)
        a_ref[0] = (s * u).astype(BF16)

    w_spec = pl.BlockSpec((1, fs, D), lambda q, i: (q, 0, 0))
    o_spec = pl.BlockSpec((1, tm, fs), lambda q, i: (q, i, 0))
    o_shape = jax.ShapeDtypeStruct((nq, T, fs), BF16)
    return _call(
        body, name=name, grid=(nq, T // tm),
        in_specs=[pl.BlockSpec((tm, D), lambda q, i: (i, 0)), w_spec, w_spec],
        out_specs=[o_spec, o_spec, o_spec], out_shape=[o_shape, o_shape, o_shape], args=[xn, wg, wu], comms=comms)


def _norm_mod(y, g_ref, sc_ref, sh_ref):
    nx, _ = _rms_parts(y)
    return ((nx * g_ref[...]) * (1.0 + sc_ref[0]) + sh_ref[0]).astype(BF16)


def ffn_down(a, wd, hin, ga, *, target=None, norm=None, seq, tm, name, comms=()):
    nq, T, fs = a.shape
    D = wd.shape[-1]
    B, tps, nt = T // seq, seq // tm, T // tm

    def body(a_ref, wd_ref, h_ref, ga_ref, *rest):
        rest = list(rest)
        t_ref = rest.pop(0) if target is not None else None
        norm_refs = [rest.pop(0) for _ in range(3)] if norm is not None else None
        o_ref, f_ref = rest[0], rest[1]
        f = _dot(a_ref[0], wd_ref[0])
        for q in range(1, nq):
            f = f + _dot(a_ref[q], wd_ref[q])
        f_ref[...] = f.astype(BF16)
        y = h_ref[...] + (0.5 * ga_ref[0]) * f
        if target is not None:
            e = y - t_ref[...]
            o_ref[...] = e * (1.0 / D)
            rest[2][...] = jnp.full(rest[2].shape, 0.5 * jnp.sum(e * e) * (1.0 / D), F32)
        else:
            o_ref[...] = y
        if norm is not None:
            rest[2][...] = _norm_mod(y, *norm_refs)

    tile = pl.BlockSpec((tm, D), lambda i: (i, 0))
    in_specs = [pl.BlockSpec((nq, tm, fs), lambda i: (0, i, 0)), _resident(wd), tile, _mod_spec(ga, tps)]
    out_specs = [tile, tile]
    out_shape = [jax.ShapeDtypeStruct((T, D), F32), jax.ShapeDtypeStruct((T, D), BF16)]
    args = [a, wd, hin, ga[0]]
    if target is not None:
        in_specs.append(tile)
        args.append(target)
        out_specs.append(pl.BlockSpec((1, 8, 128), lambda i: (i, 0, 0)))
        out_shape.append(jax.ShapeDtypeStruct((nt, 8, 128), F32))
    if norm is not None:
        in_specs += [pl.BlockSpec((1, D), lambda i: (0, 0)), _mod_spec(norm[1], tps), _mod_spec(norm[2], tps)]
        args += [norm[0], norm[1][0], norm[2][0]]
        out_specs.append(tile)
        out_shape.append(jax.ShapeDtypeStruct((T, D), BF16))
    return _call(body, name=name, grid=(nt,), in_specs=in_specs, out_specs=out_specs, out_shape=out_shape, args=args,
                 comms=comms)


def proj_fwd(xn, w, groups, *, tm, name, comms=()):
    T, D = xn.shape

    def body(x_ref, w_ref, *o_refs):
        x = x_ref[...]
        for (r0, r1), o_ref in zip(groups, o_refs):
            o_ref[...] = _dg(x, w_ref[r0:r1, :], NT).astype(BF16)

    return _call(
        body, name=name, grid=(T // tm,), in_specs=[pl.BlockSpec((tm, D), lambda i: (i, 0)), _resident(w)],
        out_specs=[pl.BlockSpec((tm, r1 - r0), lambda i: (i, 0)) for r0, r1 in groups],
        out_shape=[jax.ShapeDtypeStruct((T, r1 - r0), BF16) for r0, r1 in groups], args=[xn, w], comms=comms)


def _layer_norm_parts(v):
    mu = jnp.mean(v, axis=-1, keepdims=True)
    d = v - mu
    rstd = lax.rsqrt(jnp.mean(d * d, axis=-1, keepdims=True) + EPS)
    return d * rstd, rstd


def _causal():
    row = lax.broadcasted_iota(jnp.int32, (CHUNK, CHUNK), 0)
    col = lax.broadcasted_iota(jnp.int32, (CHUNK, CHUNK), 1)
    return row >= col


def sgu_fwd(puv, gln, bln, ws, bs_t, *, cpb, name, comms=()):
    T = puv.shape[0]
    gd = D_A // N_GROUPS
    tm = cpb * CHUNK

    def body(p_ref, gln_ref, bln_ref, ws_ref, bs_ref, o_ref):
        causal = _causal()
        wm = [jnp.where(causal, ws_ref[g], 0.0).astype(BF16) for g in range(N_GROUPS)]
        for j in range(cpb):
            rows = slice(j * CHUNK, (j + 1) * CHUNK)
            u, _ = _gelu_parts(p_ref[rows, 0:D_A].astype(F32))
            vv, _ = _gelu_parts(p_ref[rows, D_A:2 * D_A].astype(F32))
            vhat, _ = _layer_norm_parts(vv)
            vn = (vhat * gln_ref[...] + bln_ref[...]).astype(BF16)
            for g in range(N_GROUPS):
                cols = slice(g * gd, (g + 1) * gd)
                z = _dot(wm[g], vn[:, cols]) + bs_ref[:, g:g + 1]
                o_ref[rows, cols] = (u[:, cols] * z).astype(BF16)

    full = lambda a: pl.BlockSpec(a.shape, lambda i: (0,) * a.ndim)
    return _call(
        body, name=name, grid=(T // tm,),
        in_specs=[pl.BlockSpec((tm, 2 * D_A), lambda i: (i, 0)), full(gln), full(bln), full(ws), full(bs_t)],
        out_specs=[pl.BlockSpec((tm, D_A), lambda i: (i, 0))], out_shape=[jax.ShapeDtypeStruct((T, D_A), BF16)],
        args=[puv, gln, bln, ws, bs_t], comms=comms)[0]


def _rms_parts(x):
    r = lax.rsqrt(jnp.mean(x * x, axis=-1, keepdims=True) + EPS)
    return x * r, r


KV_W = Q_PER_KV * HEAD_DIM
KV2 = N_KV * HEAD_DIM
STACK = Q_PER_KV * CHUNK


def _iota(shape, dim):
    return lax.broadcasted_iota(jnp.int32, shape, dim)


def _head_mean_matrix(n):
    return jnp.where((_iota((n, n), 0) >> 6) == (_iota((n, n), 1) >> 6), 1.0 / HEAD_DIM, 0.0).astype(BF16)


def _repeat_matrix(h):
    return jnp.where(_iota((KV2, KV_W), 0) == h * HEAD_DIM + (_iota((KV2, KV_W), 1) & (HEAD_DIM - 1)), 1.0, 0.0).astype(BF16)


def _fold_matrix(h):
    j, l = _iota((KV_W, KV2), 0), _iota((KV_W, KV2), 1)
    return jnp.where(((j & (HEAD_DIM - 1)) == (l & (HEAD_DIM - 1))) & ((l >> 6) == h), 1.0, 0.0).astype(BF16)


def _dot_split(x, m):
    hi = x.astype(BF16)
    lo = (x - hi.astype(F32)).astype(BF16)
    return _dot(hi, m) + _dot(lo, m)


def _head_rms(x, mean_matrix):
    r = lax.rsqrt(_dot_split(x * x, mean_matrix) + EPS)
    return x * r, r


def _stack_heads(x):
    head = _iota(x.shape, 1) >> 6
    return jnp.concatenate([jnp.where(head == g, x, 0.0).astype(BF16) for g in range(Q_PER_KV)], axis=0)


def _unstack_heads(y):
    head = _iota((CHUNK, KV_W), 1) >> 6
    out = jnp.where(head == 0, y[0:CHUNK], 0.0)
    for g in range(1, Q_PER_KV):
        out = out + jnp.where(head == g, y[g * CHUNK:(g + 1) * CHUNK], 0.0)
    return out


SOFTMAX_ROWS = 32


def _fill_mask_bias(bias_ref):
    qi = _iota((CHUNK, 2 * CHUNK), 0)
    kj = _iota((CHUNK, 2 * CHUNK), 1)
    diff = qi + CHUNK - kj
    window = (diff >= 0) & (diff < CHUNK)
    bias_ref[0] = jnp.where(window & (kj >= CHUNK), 0.0, NEG)
    bias_ref[1] = jnp.where(window, 0.0, NEG)


def _softmax_rows(s_ref, bias_ref, first, sink_ref, h, c):
    rows = pl.ds(pl.multiple_of(c * SOFTMAX_ROWS, SOFTMAX_ROWS), SOFTMAX_ROWS)
    in_block = pl.ds(pl.multiple_of((c % (CHUNK // SOFTMAX_ROWS)) * SOFTMAX_ROWS, SOFTMAX_ROWS), SOFTMAX_ROWS)
    sink = sink_ref[0, h * Q_PER_KV + c // (CHUNK // SOFTMAX_ROWS)]
    s = s_ref[rows, :] + bias_ref[first, in_block, :]
    m = jnp.maximum(jnp.max(s, axis=-1, keepdims=True), sink)
    p = jnp.exp(s - m)
    es = jnp.exp(sink - m)
    inv = 1.0 / (jnp.sum(p, axis=-1, keepdims=True) + es)
    return rows, p * inv, es * inv


def _fill_keys(p_ref, gk_ref, krep, vrep, seq):
    mean_k = _head_mean_matrix(KV2)
    reps = [_repeat_matrix(h) for h in range(N_KV)]
    for h in range(N_KV):
        krep[h][0:CHUNK, :] = jnp.zeros((CHUNK, KV_W), BF16)
        vrep[h][0:CHUNK, :] = jnp.zeros((CHUNK, KV_W), BF16)
    rows = min(seq, 4 * CHUNK)

    def piece(j, carry):
        r0 = pl.multiple_of(j * rows, CHUNK)
        nk, _ = _head_rms(p_ref[pl.ds(r0, rows), K_OFF:K_OFF + KV2].astype(F32), mean_k)
        kn = (nk * gk_ref[...]).astype(BF16)
        v = p_ref[pl.ds(r0, rows), V_OFF:V_OFF + KV2].astype(BF16)
        r1 = pl.multiple_of(r0 + CHUNK, CHUNK)
        for h in range(N_KV):
            krep[h][pl.ds(r1, rows), :] = _dot(kn, reps[h]).astype(BF16)
            vrep[h][pl.ds(r1, rows), :] = _dot(v, reps[h]).astype(BF16)
        return carry

    lax.fori_loop(0, seq // rows, piece, 0)


def attn_fwd(pqkv, gq_t, gk_t, sinks, *, seq, name, comms=()):
    T = pqkv.shape[0]
    nb = seq // CHUNK

    def body(p_ref, gq_ref, gk_ref, sink_ref, o_ref, k0, k1, v0, v1, bias_ref, s_ref, pr_ref):
        krep, vrep = [k0, k1], [v0, v1]
        _fill_keys(p_ref, gk_ref, krep, vrep, seq)
        _fill_mask_bias(bias_ref)
        mean_q = _head_mean_matrix(D_B)

        def block(i, carry):
            r0 = pl.multiple_of(i * CHUNK, CHUNK)
            first = jnp.minimum(i, 1)
            nq, _ = _head_rms(p_ref[pl.ds(r0, CHUNK), 0:D_B].astype(F32), mean_q)
            qn = nq * (gq_ref[...] * ATT_SCALE)
            for h in range(N_KV):
                qs = _stack_heads(qn[:, h * KV_W:(h + 1) * KV_W])
                s_ref[...] = _dg(qs, krep[h][pl.ds(r0, 2 * CHUNK), :], NT)

                def rows_of(c, carry2):
                    rows, probs, _ = _softmax_rows(s_ref, bias_ref, first, sink_ref, h, c)
                    pr_ref[rows, :] = probs.astype(BF16)
                    return carry2

                lax.fori_loop(0, STACK // SOFTMAX_ROWS, rows_of, 0, unroll=True)
                out = _unstack_heads(_dot(pr_ref[...], vrep[h][pl.ds(r0, 2 * CHUNK), :]))
                o_ref[pl.ds(r0, CHUNK), h * KV_W:(h + 1) * KV_W] = out.astype(BF16)
            return carry

        lax.fori_loop(0, nb, block, 0)

    return _call(
        body, name=name, grid=(T // seq,),
        in_specs=[pl.BlockSpec((seq, QKV_W), lambda b: (b, 0)), pl.BlockSpec(gq_t.shape, lambda b: (0, 0)),
                  pl.BlockSpec(gk_t.shape, lambda b: (0, 0)), pl.BlockSpec(memory_space=pltpu.SMEM)],
        out_specs=[pl.BlockSpec((seq, D_B), lambda b: (b, 0))], out_shape=[jax.ShapeDtypeStruct((T, D_B), BF16)],
        scratch_shapes=[pltpu.VMEM((seq + CHUNK, KV_W), BF16)] * 4
        + [pltpu.VMEM((2, CHUNK, 2 * CHUNK), F32), pltpu.VMEM((STACK, 2 * CHUNK), F32), pltpu.VMEM((STACK, 2 * CHUNK), BF16)],
        args=[pqkv, gq_t, gk_t, sinks], comms=comms)[0]


def mixer_out_fwd(sgu, att, pg, hin, ga, wa, wb, wo, norm, *, seq, tm, name, comms=()):
    T, D = hin.shape
    B, tps = T // seq, seq // tm

    def body(s_ref, t_ref, pg_ref, h_ref, ga_ref, wa_ref, wb_ref, wo_ref, g_ref, sc_ref, sh_ref,
             ya_ref, yb_ref, mg_ref, m_ref, ho_ref, xn_ref):
        ya = _dot(s_ref[...], wa_ref[...])
        yb = _dot(t_ref[...], wb_ref[...])
        merged = (jax.nn.sigmoid(pg_ref[:, 0:D].astype(F32)) * ya
                  + jax.nn.sigmoid(pg_ref[:, D:2 * D].astype(F32)) * yb)
        mg = merged.astype(BF16)
        m = _dot(mg, wo_ref[...])
        ya_ref[...] = ya.astype(BF16)
        yb_ref[...] = yb.astype(BF16)
        mg_ref[...] = mg
        m_ref[...] = m.astype(BF16)
        y = h_ref[...] + ga_ref[0] * m
        ho_ref[...] = y
        xn_ref[...] = _norm_mod(y, g_ref, sc_ref, sh_ref)

    tile = lambda w: pl.BlockSpec((tm, w), lambda i: (i, 0))
    b16 = jax.ShapeDtypeStruct((T, D), BF16)
    return _call(
        body, name=name, grid=(T // tm,),
        in_specs=[tile(D_A), tile(D_B), tile(2 * D), tile(D), _mod_spec(ga, tps), _resident(wa), _resident(wb),
                  _resident(wo), pl.BlockSpec((1, D), lambda i: (0, 0)), _mod_spec(norm[1], tps), _mod_spec(norm[2], tps)],
        out_specs=[tile(D)] * 6, out_shape=[b16, b16, b16, b16, jax.ShapeDtypeStruct((T, D), F32), b16],
        args=[sgu, att, pg, hin, ga[0], wa, wb, wo, norm[0], norm[1][0], norm[2][0]], comms=comms)


def ffn_bwd_act(dh, f, ga, wd, s, q, *, seq, tm, name, comms=()):
    T, D = dh.shape
    nq, fs, _ = wd.shape
    B, tps = T // seq, seq // tm

    def body(dh_ref, f_ref, ga_ref, wd_ref, s_ref, q_ref, dg_ref, du_ref, df_ref, dga_ref):
        i = pl.program_id(0)
        d = dh_ref[...]
        df = ((0.5 * ga_ref[0]) * d).astype(BF16)
        df_ref[...] = df
        part = 0.5 * jnp.sum(d * f_ref[...].astype(F32), axis=0, keepdims=True)
        for j in range(nq):
            da = _dg(df, wd_ref[j], NT)
            du_ref[j] = (da * s_ref[j].astype(F32)).astype(BF16)
            dg_ref[j] = (da * q_ref[j].astype(F32)).astype(BF16)

        @pl.when(i % tps == 0)
        def _():
            dga_ref[0] = part

        @pl.when(i % tps != 0)
        def _():
            dga_ref[0] += part

    tile = pl.BlockSpec((tm, D), lambda i: (i, 0))
    per_seq = pl.BlockSpec((1, 1, D), lambda i: (i // tps, 0, 0))
    act = pl.BlockSpec((nq, tm, fs), lambda i: (0, i, 0))
    o_shape = jax.ShapeDtypeStruct((nq, T, fs), BF16)
    dg, du, df, dga = _call(
        body, name=name, grid=(T // tm,), in_specs=[tile, tile, _mod_spec(ga, tps), _resident(wd), act, act],
        out_specs=[act, act, tile, per_seq],
        out_shape=[o_shape, o_shape, jax.ShapeDtypeStruct((T, D), BF16), jax.ShapeDtypeStruct((B, 1, D), F32)],
        args=[dh, f, ga[0], wd, s, q], comms=comms)
    return dg, du, df, dga.reshape(B, D)


def mm_tn(pairs, *, tt, name, comms=()):
    n = len(pairs)
    flat = []
    for pair in pairs:
        for a in pair:
            if not any(a is b for b in flat):
                flat.append(a)
    where = lambda a: [k for k, b in enumerate(flat) if a is b][0]
    nq = max([a.shape[0] for a in flat if a.ndim == 3] + [1])
    T = flat[0].shape[-2]
    tt = min(tt, T)
    nt = T // tt
    stacked = [x.ndim == 3 or y.ndim == 3 for x, y in pairs]

    def body(*refs):
        in_refs, o_refs, accs = refs[:len(flat)], refs[len(flat):len(flat) + n], refs[len(flat) + n:]
        t = pl.program_id(1)
        value = lambda a: in_refs[where(a)][0] if a.ndim == 3 else in_refs[where(a)][...]

        def put(j, v):
            if stacked[j]:
                o_refs[j][0] = v.astype(BF16)
            else:
                o_refs[j][...] = v.astype(BF16)

        for j, (x, y) in enumerate(pairs):
            part = _dg(value(x), value(y), TN)
            if nt == 1:
                put(j, part)
                continue

            @pl.when(t == 0)
            def _():
                accs[j][...] = part

            @pl.when(t != 0)
            def _():
                accs[j][...] += part

        if nt > 1:
            @pl.when(t == nt - 1)
            def _():
                for j in range(n):
                    put(j, accs[j][...])

    def spec(a):
        if a.ndim == 3:
            return pl.BlockSpec((1, tt, a.shape[2]), lambda q, t: (q, t, 0))
        return pl.BlockSpec((tt, a.shape[1]), lambda q, t: (t, 0))

    out_specs, out_shape = [], []
    for j, (x, y) in enumerate(pairs):
        K, N = x.shape[-1], y.shape[-1]
        if stacked[j]:
            out_specs.append(pl.BlockSpec((1, K, N), lambda q, t: (q, 0, 0)))
            out_shape.append(jax.ShapeDtypeStruct((nq, K, N), BF16))
        else:
            out_specs.append(pl.BlockSpec((K, N), lambda q, t: (0, 0)))
            out_shape.append(jax.ShapeDtypeStruct((K, N), BF16))
    return _call(
        body, name=name, grid=(nq, nt), in_specs=[spec(a) for a in flat],
        out_specs=out_specs, out_shape=out_shape,
        scratch_shapes=[pltpu.VMEM((x.shape[-1], y.shape[-1]), F32) for x, y in pairs] if nt > 1 else [],
        args=flat, comms=comms)


def dx_norm_bwd(dys, ws, hin, dh_out, g, sc, *, seq, tm, name, comms=()):
    T, D = hin.shape
    B, tps = T // seq, seq // tm
    n = len(dys)
    ranged = [w if isinstance(w, tuple) else (w, 0, w.shape[-2]) for w in ws]
    ws = []
    for w, _, _ in ranged:
        if not any(w is u for u in ws):
            ws.append(w)
    where = [[k for k, u in enumerate(ws) if u is w][0] for w, _, _ in ranged]

    def body(*refs):
        dy_refs, w_refs = refs[:n], refs[n:n + len(ws)]
        h_ref, dho_ref, g_ref, sc_ref, dhi_ref, dsh_ref, dsc_ref, dgn_ref = refs[n + len(ws):]
        i = pl.program_id(0)
        dxn = None
        for j in range(n):
            w_ref, (_, r0, r1) = w_refs[where[j]], ranged[j]
            if dys[j].ndim == 3:
                for q in range(dys[j].shape[0]):
                    part = _dot(dy_refs[j][q], w_ref[q])
                    dxn = part if dxn is None else dxn + part
            else:
                part = _dot(dy_refs[j][...], w_ref[r0:r1, :])
                dxn = part if dxn is None else dxn + part
        x = h_ref[...]
        nx, r = _rms_parts(x)
        gain = g_ref[...]
        one_sc = 1.0 + sc_ref[0]
        dsh = jnp.sum(dxn, axis=0, keepdims=True)
        dsc = jnp.sum(dxn * (nx * gain), axis=0, keepdims=True)
        dgn = jnp.sum(dxn * one_sc * nx, axis=0, keepdims=True)
        dn = dxn * one_sc * gain
        dhi_ref[...] = dho_ref[...] + r * (dn - nx * jnp.mean(dn * nx, axis=-1, keepdims=True))

        @pl.when(i % tps == 0)
        def _():
            dsh_ref[0] = dsh
            dsc_ref[0] = dsc

        @pl.when(i % tps != 0)
        def _():
            dsh_ref[0] += dsh
            dsc_ref[0] += dsc

        @pl.when(i == 0)
        def _():
            dgn_ref[...] = dgn

        @pl.when(i != 0)
        def _():
            dgn_ref[...] += dgn

    def dy_spec(a):
        if a.ndim == 3:
            return pl.BlockSpec((a.shape[0], tm, a.shape[2]), lambda i: (0, i, 0))
        return pl.BlockSpec((tm, a.shape[1]), lambda i: (i, 0))

    tile = pl.BlockSpec((tm, D), lambda i: (i, 0))
    per_seq = pl.BlockSpec((1, 1, D), lambda i: (i // tps, 0, 0))
    row = pl.BlockSpec((1, D), lambda i: (0, 0))
    dhi, dsh, dsc, dgn = _call(
        body, name=name, grid=(T // tm,),
        in_specs=[dy_spec(a) for a in dys] + [_resident(w) for w in ws] + [tile, tile, row, _mod_spec(sc, tps)],
        out_specs=[tile, per_seq, per_seq, row],
        out_shape=[jax.ShapeDtypeStruct((T, D), F32), jax.ShapeDtypeStruct((B, 1, D), F32),
                   jax.ShapeDtypeStruct((B, 1, D), F32), jax.ShapeDtypeStruct((1, D), F32)],
        args=[*dys, *ws, hin, dh_out, g, sc[0]], comms=comms)
    return dhi, dsh.reshape(B, D), dsc.reshape(B, D), dgn


def mixer_out_bwd(dh, m, ga, ya, yb, pg, wa, wb, wo, *, seq, tm, name, comms=()):
    T, D = dh.shape
    B, tps = T // seq, seq // tm

    def body(dh_ref, m_ref, ga_ref, ya_ref, yb_ref, pg_ref, wa_ref, wb_ref, wo_ref,
             dg_ref, dya_ref, dyb_ref, ds_ref, dt_ref, dm_ref, dga_ref):
        i = pl.program_id(0)
        d = dh_ref[...]
        dm = (ga_ref[0] * d).astype(BF16)
        dm_ref[...] = dm
        part = jnp.sum(d * m_ref[...].astype(F32), axis=0, keepdims=True)

        @pl.when(i % tps == 0)
        def _():
            dga_ref[0] = part

        @pl.when(i % tps != 0)
        def _():
            dga_ref[0] += part

        dmg = _dg(dm, wo_ref[...], NT)
        sa = jax.nn.sigmoid(pg_ref[:, 0:D].astype(F32))
        sb = jax.nn.sigmoid(pg_ref[:, D:2 * D].astype(F32))
        dg_ref[:, 0:D] = (dmg * ya_ref[...].astype(F32) * (sa * (1.0 - sa))).astype(BF16)
        dg_ref[:, D:2 * D] = (dmg * yb_ref[...].astype(F32) * (sb * (1.0 - sb))).astype(BF16)
        dya = (dmg * sa).astype(BF16)
        dyb = (dmg * sb).astype(BF16)
        dya_ref[...] = dya
        dyb_ref[...] = dyb
        ds_ref[...] = _dg(dya, wa_ref[...], NT).astype(BF16)
        dt_ref[...] = _dg(dyb, wb_ref[...], NT).astype(BF16)

    tile = lambda w: pl.BlockSpec((tm, w), lambda i: (i, 0))
    per_seq = pl.BlockSpec((1, 1, D), lambda i: (i // tps, 0, 0))
    dg, dya, dyb, ds, dt, dm, dga = _call(
        body, name=name, grid=(T // tm,),
        in_specs=[tile(D), tile(D), _mod_spec(ga, tps), tile(D), tile(D), tile(2 * D), _resident(wa), _resident(wb),
                  _resident(wo)],
        out_specs=[tile(2 * D), tile(D), tile(D), tile(D_A), tile(D_B), tile(D), per_seq],
        out_shape=[jax.ShapeDtypeStruct((T, 2 * D), BF16), jax.ShapeDtypeStruct((T, D), BF16),
                   jax.ShapeDtypeStruct((T, D), BF16), jax.ShapeDtypeStruct((T, D_A), BF16),
                   jax.ShapeDtypeStruct((T, D_B), BF16), jax.ShapeDtypeStruct((T, D), BF16),
                   jax.ShapeDtypeStruct((B, 1, D), F32)],
        args=[dh, m, ga[0], ya, yb, pg, wa, wb, wo], comms=comms)
    return dg, dya, dyb, ds, dt, dm, dga.reshape(B, D)


def sgu_bwd(puv, dsgu, gln, bln, ws, bs_t, *, cpb, name, comms=()):
    T = puv.shape[0]
    gd = D_A // N_GROUPS
    tm = cpb * CHUNK

    def body(p_ref, ds_ref, gln_ref, bln_ref, ws_ref, bs_ref, d_ref, dws_ref, dz_ref, dgl_ref, dbl_ref):
        i = pl.program_id(0)
        causal = _causal()
        wf = [jnp.where(causal, ws_ref[g], 0.0) for g in range(N_GROUPS)]
        wm = [w.astype(BF16) for w in wf]
        wt = [w.T.astype(BF16) for w in wf]
        dws = [jnp.zeros((CHUNK, CHUNK), F32) for _ in range(N_GROUPS)]
        dzs = jnp.zeros((CHUNK, D_A), F32)
        dgl = jnp.zeros((1, D_A), F32)
        dbl = jnp.zeros((1, D_A), F32)
        for j in range(cpb):
            rows = slice(j * CHUNK, (j + 1) * CHUNK)
            au = p_ref[rows, 0:D_A].astype(F32)
            av = p_ref[rows, D_A:2 * D_A].astype(F32)
            u, tu = _gelu_parts(au)
            vv, tv = _gelu_parts(av)
            vhat, rstd = _layer_norm_parts(vv)
            vn = (vhat * gln_ref[...] + bln_ref[...]).astype(BF16)
            dsg = ds_ref[rows, :].astype(F32)
            dz = dsg * u
            dzb = dz.astype(BF16)
            zs, dvns = [], []
            for g in range(N_GROUPS):
                cols = slice(g * gd, (g + 1) * gd)
                zs.append(_dot(wm[g], vn[:, cols]) + bs_ref[:, g:g + 1])
                dws[g] = dws[g] + _dg(dzb[:, cols], vn[:, cols], NT)
                dvns.append(_dot(wt[g], dzb[:, cols]))
            z = jnp.concatenate(zs, axis=1)
            dvn = jnp.concatenate(dvns, axis=1)
            d_ref[rows, 0:D_A] = (dsg * z * _dgelu(au, tu)).astype(BF16)
            dvh = dvn * gln_ref[...]
            dvv = rstd * (dvh - jnp.mean(dvh, axis=-1, keepdims=True)
                          - vhat * jnp.mean(dvh * vhat, axis=-1, keepdims=True))
            d_ref[rows, D_A:2 * D_A] = (dvv * _dgelu(av, tv)).astype(BF16)
            dzs = dzs + dz
            dgl = dgl + jnp.sum(dvn * vhat, axis=0, keepdims=True)
            dbl = dbl + jnp.sum(dvn, axis=0, keepdims=True)

        @pl.when(i == 0)
        def _():
            for g in range(N_GROUPS):
                dws_ref[g] = jnp.where(causal, dws[g], 0.0)
            dz_ref[...] = dzs
            dgl_ref[...] = dgl
            dbl_ref[...] = dbl

        @pl.when(i != 0)
        def _():
            for g in range(N_GROUPS):
                dws_ref[g] += jnp.where(causal, dws[g], 0.0)
            dz_ref[...] += dzs
            dgl_ref[...] += dgl
            dbl_ref[...] += dbl

    full = lambda a: pl.BlockSpec(a.shape, lambda i: (0,) * a.ndim)
    acc = lambda s: pl.BlockSpec(s, lambda i: (0,) * len(s))
    return _call(
        body, name=name, grid=(T // tm,),
        in_specs=[pl.BlockSpec((tm, 2 * D_A), lambda i: (i, 0)), pl.BlockSpec((tm, D_A), lambda i: (i, 0)),
                  full(gln), full(bln), full(ws), full(bs_t)],
        out_specs=[pl.BlockSpec((tm, 2 * D_A), lambda i: (i, 0)), acc((N_GROUPS, CHUNK, CHUNK)), acc((CHUNK, D_A)),
                   acc((1, D_A)), acc((1, D_A))],
        out_shape=[jax.ShapeDtypeStruct((T, 2 * D_A), BF16), jax.ShapeDtypeStruct((N_GROUPS, CHUNK, CHUNK), F32),
                   jax.ShapeDtypeStruct((CHUNK, D_A), F32), jax.ShapeDtypeStruct((1, D_A), F32),
                   jax.ShapeDtypeStruct((1, D_A), F32)],
        args=[puv, dsgu, gln, bln, ws, bs_t], comms=comms)


def attn_bwd(pqkv, datt, gq_t, gk_t, sinks, *, seq, name, comms=()):
    T = pqkv.shape[0]
    nb = seq // CHUNK

    def body(p_ref, do_ref, gq_ref, gk_ref, sink_ref, d_ref, dgq_ref, dgk_ref, dsk_ref,
             k0, k1, v0, v1, dk0, dk1, dv0, dv1, dgq_s, dsk_s, bias_ref, s_ref, dp_ref, pr_ref, ds_ref):
        b = pl.program_id(0)
        krep, vrep, dkacc, dvacc = [k0, k1], [v0, v1], [dk0, dk1], [dv0, dv1]
        _fill_keys(p_ref, gk_ref, krep, vrep, seq)
        _fill_mask_bias(bias_ref)
        for h in range(N_KV):
            dkacc[h][...] = jnp.zeros_like(dkacc[h])
            dvacc[h][...] = jnp.zeros_like(dvacc[h])
        dgq_s[...] = jnp.zeros_like(dgq_s)
        dsk_s[...] = jnp.zeros_like(dsk_s)
        mean_q = _head_mean_matrix(D_B)
        lane = _iota((1, 128), 1)

        def block(i, carry):
            r0 = pl.multiple_of(i * CHUNK, CHUNK)
            first = jnp.minimum(i, 1)
            nq, rq = _head_rms(p_ref[pl.ds(r0, CHUNK), 0:D_B].astype(F32), mean_q)
            qn = nq * (gq_ref[...] * ATT_SCALE)
            dqn_parts = []
            for h in range(N_KV):
                cols = slice(h * KV_W, (h + 1) * KV_W)
                qs = _stack_heads(qn[:, cols])
                kk = krep[h][pl.ds(r0, 2 * CHUNK), :]
                dos = _stack_heads(do_ref[pl.ds(r0, CHUNK), cols].astype(F32))
                s_ref[...] = _dg(qs, kk, NT)
                dp_ref[...] = _dg(dos, vrep[h][pl.ds(r0, 2 * CHUNK), :], NT)

                def rows_of(c, carry2):
                    rows, probs, psink = _softmax_rows(s_ref, bias_ref, first, sink_ref, h, c)
                    dp = dp_ref[rows, :]
                    dr = jnp.sum(probs * dp, axis=-1, keepdims=True)
                    pr_ref[rows, :] = probs.astype(BF16)
                    ds_ref[rows, :] = (probs * (dp - dr)).astype(BF16)
                    head = h * Q_PER_KV + c // (CHUNK // SOFTMAX_ROWS)
                    dsk_s[...] += jnp.where(lane == head, -jnp.sum(psink * dr), 0.0)
                    return carry2

                lax.fori_loop(0, STACK // SOFTMAX_ROWS, rows_of, 0, unroll=True)
                dqn_parts.append(_unstack_heads(_dot(ds_ref[...], kk)))
                dkacc[h][pl.ds(r0, 2 * CHUNK), :] += _dg(ds_ref[...], qs, TN)
                dvacc[h][pl.ds(r0, 2 * CHUNK), :] += _dg(pr_ref[...], dos, TN)
            dqn = jnp.concatenate(dqn_parts, axis=1) * ATT_SCALE
            dn = dqn * gq_ref[...]
            d_ref[pl.ds(r0, CHUNK), 0:D_B] = (rq * (dn - nq * _dot_split(dn * nq, mean_q))).astype(BF16)
            dgq_s[...] += jnp.sum(dqn * nq, axis=0, keepdims=True)
            return carry

        lax.fori_loop(0, nb, block, 0)

        mean_k = _head_mean_matrix(KV2)
        folds = [_fold_matrix(h) for h in range(N_KV)]
        rows = min(seq, 4 * CHUNK)

        def piece(j, dgk):
            r0 = pl.multiple_of(j * rows, CHUNK)
            r1 = pl.multiple_of(r0 + CHUNK, CHUNK)
            dkn = _dot_split(dkacc[0][pl.ds(r1, rows), :], folds[0]) + _dot_split(dkacc[1][pl.ds(r1, rows), :], folds[1])
            dv = _dot_split(dvacc[0][pl.ds(r1, rows), :], folds[0]) + _dot_split(dvacc[1][pl.ds(r1, rows), :], folds[1])
            nk, rk = _head_rms(p_ref[pl.ds(r0, rows), K_OFF:K_OFF + KV2].astype(F32), mean_k)
            dn = dkn * gk_ref[...]
            d_ref[pl.ds(r0, rows), K_OFF:K_OFF + KV2] = (rk * (dn - nk * _dot_split(dn * nk, mean_k))).astype(BF16)
            d_ref[pl.ds(r0, rows), V_OFF:V_OFF + KV2] = dv.astype(BF16)
            return dgk + jnp.sum(dkn * nk, axis=0, keepdims=True)

        dgk = lax.fori_loop(0, seq // rows, piece, jnp.zeros((1, KV2), F32))

        @pl.when(b == 0)
        def _():
            dgq_ref[...] = dgq_s[...]
            dgk_ref[...] = dgk
            dsk_ref[...] = jnp.broadcast_to(dsk_s[...], dsk_ref.shape)

        @pl.when(b != 0)
        def _():
            dgq_ref[...] += dgq_s[...]
            dgk_ref[...] += dgk
            dsk_ref[...] += jnp.broadcast_to(dsk_s[...], dsk_ref.shape)

    acc = lambda s: pl.BlockSpec(s, lambda b: (0, 0))
    return _call(
        body, name=name, grid=(T // seq,),
        in_specs=[pl.BlockSpec((seq, QKV_W), lambda b: (b, 0)), pl.BlockSpec((seq, D_B), lambda b: (b, 0)),
                  pl.BlockSpec(gq_t.shape, lambda b: (0, 0)), pl.BlockSpec(gk_t.shape, lambda b: (0, 0)),
                  pl.BlockSpec(memory_space=pltpu.SMEM)],
        out_specs=[pl.BlockSpec((seq, QKV_W), lambda b: (b, 0)), acc((1, D_B)), acc((1, KV2)), acc((8, 128))],
        out_shape=[jax.ShapeDtypeStruct((T, QKV_W), BF16), jax.ShapeDtypeStruct((1, D_B), F32),
                   jax.ShapeDtypeStruct((1, KV2), F32), jax.ShapeDtypeStruct((8, 128), F32)],
        scratch_shapes=[pltpu.VMEM((seq + CHUNK, KV_W), BF16)] * 4 + [pltpu.VMEM((seq + CHUNK, KV_W), F32)] * 4
        + [pltpu.VMEM((1, D_B), F32), pltpu.VMEM((1, 128), F32), pltpu.VMEM((2, CHUNK, 2 * CHUNK), F32)]
        + [pltpu.VMEM((STACK, 2 * CHUNK), F32)] * 2 + [pltpu.VMEM((STACK, 2 * CHUNK), BF16)] * 2,
        args=[pqkv, datt, gq_t, gk_t, sinks], comms=comms)


def ada_fwd(c_all, w, b, *, name):
    nb, D = c_all.shape
    N = w.shape[1]
    tn = N // 3

    def body(c_ref, w_ref, b_ref, o_ref):
        c = c_ref[...]
        cond = (c * jax.nn.sigmoid(c)).astype(BF16)
        o_ref[...] = _dot(cond, w_ref[...].astype(BF16)) + b_ref[...]

    return _call(
        body, name=name, grid=(3,),
        in_specs=[pl.BlockSpec((nb, D), lambda j: (0, 0)), pl.BlockSpec((D, tn), lambda j: (0, j)),
                  pl.BlockSpec((1, tn), lambda j: (0, j))],
        out_specs=[pl.BlockSpec((nb, tn), lambda j: (0, j))], out_shape=[jax.ShapeDtypeStruct((nb, N), F32)],
        args=[c_all, w, b])[0]


def ada_bwd(c_all, dmods_all, dmods_mine, gain_rows, *, name, comms=()):
    nb, D = c_all.shape
    NA = dmods_all.shape[1]
    N = dmods_mine.shape[1]
    tn = N // 3

    def body(c_ref, da_ref, dm_ref, gr_ref, db_ref, dw_ref, dgn_ref):
        c = c_ref[...]
        cond = (c * jax.nn.sigmoid(c)).astype(BF16)
        dw_ref[...] = _dg(cond, dm_ref[...].astype(BF16), TN)
        db_ref[...] = jnp.sum(da_ref[...], axis=0, keepdims=True)
        total = gr_ref[0:1, :]
        for d in range(1, N_DEV):
            total = total + gr_ref[d:d + 1, :]
        dgn_ref[...] = total

    return _call(
        body, name=name, grid=(3,),
        in_specs=[pl.BlockSpec((nb, D), lambda j: (0, 0)), pl.BlockSpec((nb, NA), lambda j: (0, 0)),
                  pl.BlockSpec((nb, tn), lambda j: (0, j)), pl.BlockSpec((N_DEV, D), lambda j: (0, 0))],
        out_specs=[pl.BlockSpec((1, NA), lambda j: (0, 0)), pl.BlockSpec((D, tn), lambda j: (0, j)),
                   pl.BlockSpec((1, D), lambda j: (0, 0))],
        out_shape=[jax.ShapeDtypeStruct((1, NA), F32), jax.ShapeDtypeStruct((D, N), F32),
                   jax.ShapeDtypeStruct((1, D), F32)],
        args=[c_all, dmods_all, dmods_mine, gain_rows], comms=comms)


def adamw(items, *, steps, name, comms=()):
    n = len(items)
    c1 = 1.0 / (1.0 - ADAM_B1 ** ADAM_STEP)
    c2 = 1.0 / (1.0 - ADAM_B2 ** ADAM_STEP)

    def body(*refs):
        for j in range(n):
            w_ref, g_ref, m_ref, v_ref = refs[4 * j:4 * j + 4]
            go_ref, d_ref, mo_ref, vo_ref = refs[4 * n + 4 * j:4 * n + 4 * j + 4]
            gg = g_ref[...]
            mn = ADAM_B1 * m_ref[...] + (1.0 - ADAM_B1) * gg
            vn = ADAM_B2 * v_ref[...] + (1.0 - ADAM_B2) * (gg * gg)
            go_ref[...] = gg
            mo_ref[...] = mn
            vo_ref[...] = vn
            d_ref[...] = -ADAM_LR * ((mn * c1) / (jnp.sqrt(vn * c2) + ADAM_EPS) + ADAM_WD * w_ref[...])

    in_specs, out_specs, out_shape, args = [], [], [], []
    for item in items:
        R, C = item[0].shape
        blk = pl.BlockSpec((R // steps, C), lambda i: (i, 0))
        in_specs += [blk] * 4
        out_specs += [blk] * 4
        out_shape += [jax.ShapeDtypeStruct((R, C), F32)] * 4
        args += list(item)
    res = _call(body, name=name, grid=(steps,), in_specs=in_specs, out_specs=out_specs, out_shape=out_shape, args=args,
                comms=comms)
    return [tuple(res[4 * j:4 * j + 4]) for j in range(n)]


def _place():
    return lax.axis_index("x"), lax.axis_index("y"), lax.axis_index("c")


def _flip(v, bit):
    return 1 - v if bit else v


def _other_chips(mx, my):
    return [(_flip(mx, k & 2), _flip(my, k & 1)) for k in range(1, N_QUAD)]


def _remote(src, dst, send_sem, recv_sem, peer):
    return pltpu.make_async_remote_copy(src_ref=src, dst_ref=dst, send_sem=send_sem, recv_sem=recv_sem,
                                        device_id=peer, device_id_type=MESH)


def ag8_comm(x):
    def copies(srcs, lands, ss, rs, only_sends=False):
        mx, my, mc = _place()
        me = 4 * mx + 2 * my + mc
        local = pltpu.make_async_copy(srcs[0], lands[0].at[me], ss.at[N_DEV - 1])
        sends, recvs = [], []
        for k in range(1, N_DEV):
            peer = (_flip(mx, k & 4), _flip(my, k & 2), _flip(mc, k & 1))
            sends.append(_remote(srcs[0], lands[0].at[me], ss.at[k - 1], rs.at[k - 1], peer))
            if not only_sends:
                recvs.append(_remote(srcs[0], lands[0].at[4 * peer[0] + 2 * peer[1] + peer[2]], ss.at[k - 1], rs.at[k - 1], peer))
        return local, sends, recvs

    def start(srcs, bufs, lands, ss, rs):
        local, sends, _ = copies(srcs, lands, ss, rs, only_sends=True)
        local.start()
        for cp in sends:
            cp.start()

    def finish(srcs, bufs, lands, ss, rs):
        local, sends, recvs = copies(srcs, lands, ss, rs)
        for cp in recvs:
            cp.wait_recv()
        for cp in sends:
            cp.wait_send()
        local.wait()

    return Comm(srcs=[x], land_shapes=[jax.ShapeDtypeStruct((N_DEV,) + x.shape, x.dtype)], n_sems=N_DEV,
                start=start, finish=finish)


def sum8(stacked, *, name):
    _, r, n = stacked.shape

    def body(s_ref, o_ref):
        total = s_ref[0]
        for d in range(1, N_DEV):
            total = total + s_ref[d]
        o_ref[...] = total

    return _call(body, name=name, grid=(), in_specs=[pl.BlockSpec(memory_space=pltpu.VMEM)],
                 out_specs=[pl.BlockSpec(memory_space=pltpu.VMEM)], out_shape=[jax.ShapeDtypeStruct((r, n), F32)],
                 args=[stacked])[0]


def cast_into_stacks(ws, quad, *, name, comms=()):
    steps = 4

    def body(q_ref, *refs):
        for w_ref, o_ref in zip(refs[:len(ws)], refs[len(ws):]):
            o_ref[0] = w_ref[...].astype(BF16)

    return _call(
        body, name=name, grid=(steps,), prefetch=[quad],
        in_specs=[pl.BlockSpec((w.shape[0] // steps, w.shape[1]), lambda i, q: (i, 0)) for w in ws],
        out_specs=[pl.BlockSpec((1, w.shape[0] // steps, w.shape[1]), lambda i, q: (q[0], i, 0)) for w in ws],
        out_shape=[jax.ShapeDtypeStruct((N_QUAD,) + w.shape, BF16) for w in ws], args=list(ws), comms=comms)


def gather_comm(stacks, forward=True):
    n = len(stacks)

    def copies(bufs, ss, rs, only_sends=False):
        mx, my, mc = _place()
        q = 2 * mx + my
        sibling = (mx, my, 1 - mc)
        ici_send, ici_recv, fwd_send, fwd_recv = [], [], [], []
        for p in range(n):
            hr = stacks[p].shape[1] // 2
            mine, other = pl.ds(mc * hr, hr), pl.ds((1 - mc) * hr, hr)
            for k, chip in enumerate(_other_chips(mx, my)):
                qk = 2 * chip[0] + chip[1]
                peer = (chip[0], chip[1], mc)
                own, landed, theirs = bufs[p].at[q, mine, :], bufs[p].at[qk, mine, :], bufs[p].at[qk, other, :]
                ici_send.append(_remote(own, own, ss.at[6 * p + k], rs.at[6 * p + k], peer))
                if only_sends:
                    continue
                ici_recv.append(_remote(own, landed, ss.at[6 * p + k], rs.at[6 * p + k], peer))
                if forward:
                    fwd_send.append(_remote(landed, landed, ss.at[6 * p + 3 + k], rs.at[6 * p + 3 + k], sibling))
                    fwd_recv.append(_remote(theirs, theirs, ss.at[6 * p + 3 + k], rs.at[6 * p + 3 + k], sibling))
        return ici_send, ici_recv, fwd_send, fwd_recv

    def start(srcs, bufs, lands, ss, rs):
        for cp in copies(bufs, ss, rs, only_sends=True)[0]:
            cp.start()

    def finish(srcs, bufs, lands, ss, rs):
        ici_send, ici_recv, fwd_send, fwd_recv = copies(bufs, ss, rs)
        for j, arrived in enumerate(ici_recv):
            arrived.wait_recv()
            if forward:
                fwd_send[j].start()
        for cp in fwd_recv:
            cp.wait_recv()
        for cp in ici_send + fwd_send:
            cp.wait_send()

    return Comm(bufs=stacks, n_sems=6 * n, start=start, finish=finish)


def forward_comm(stacks):
    n = len(stacks)

    def copies(bufs, ss, rs, only_sends=False):
        mx, my, mc = _place()
        sibling = (mx, my, 1 - mc)
        send, recv = [], []
        for p in range(n):
            hr = stacks[p].shape[1] // 2
            for k, chip in enumerate(_other_chips(mx, my)):
                qk = 2 * chip[0] + chip[1]
                landed, theirs = bufs[p].at[qk, pl.ds(mc * hr, hr), :], bufs[p].at[qk, pl.ds((1 - mc) * hr, hr), :]
                send.append(_remote(landed, landed, ss.at[3 * p + k], rs.at[3 * p + k], sibling))
                if not only_sends:
                    recv.append(_remote(theirs, theirs, ss.at[3 * p + k], rs.at[3 * p + k], sibling))
        return send, recv

    def start(srcs, bufs, lands, ss, rs):
        for cp in copies(bufs, ss, rs, only_sends=True)[0]:
            cp.start()

    def finish(srcs, bufs, lands, ss, rs):
        send, recv = copies(bufs, ss, rs)
        for cp in recv:
            cp.wait_recv()
        for cp in send:
            cp.wait_send()

    return Comm(bufs=stacks, n_sems=3 * n, start=start, finish=finish)


def rs_pair_comm(gs):
    n = len(gs)

    def copies(srcs, lands, ss, rs):
        mx, my, mc = _place()
        out = []
        for p in range(n):
            hr = gs[p].shape[1] // 2
            out.append(_remote(srcs[p].at[:, pl.ds((1 - mc) * hr, hr), :], lands[p], ss.at[p], rs.at[p], (mx, my, 1 - mc)))
        return out

    def start(srcs, bufs, lands, ss, rs):
        for cp in copies(srcs, lands, ss, rs):
            cp.start()

    def finish(srcs, bufs, lands, ss, rs):
        for cp in copies(srcs, lands, ss, rs):
            cp.wait()

    return Comm(srcs=gs, land_shapes=[jax.ShapeDtypeStruct((g.shape[0], g.shape[1] // 2, g.shape[2]), g.dtype) for g in gs],
                n_sems=n, start=start, finish=finish)


def rs_chip_comm(ss_):
    n = len(ss_)

    def copies(srcs, lands, ss, rs):
        mx, my, mc = _place()
        out = []
        for p in range(n):
            for k, chip in enumerate(_other_chips(mx, my)):
                out.append(_remote(srcs[p].at[2 * chip[0] + chip[1]], lands[p].at[k], ss.at[3 * p + k], rs.at[3 * p + k],
                                   (chip[0], chip[1], mc)))
        return out

    def start(srcs, bufs, lands, ss, rs):
        for cp in copies(srcs, lands, ss, rs):
            cp.start()

    def finish(srcs, bufs, lands, ss, rs):
        for cp in copies(srcs, lands, ss, rs):
            cp.wait()

    return Comm(srcs=ss_, land_shapes=[jax.ShapeDtypeStruct((N_QUAD - 1,) + s.shape[1:], s.dtype) for s in ss_],
                n_sems=3 * n, start=start, finish=finish)


def rs_share_comm(fulls):
    n = len(fulls)

    def copies(bufs, ss, rs, only_sends=False):
        mx, my, mc = _place()
        send, recv = [], []
        for p in range(n):
            hr = fulls[p].shape[0] // 2
            mine, other = bufs[p].at[pl.ds(mc * hr, hr), :], bufs[p].at[pl.ds((1 - mc) * hr, hr), :]
            send.append(_remote(mine, mine, ss.at[p], rs.at[p], (mx, my, 1 - mc)))
            if not only_sends:
                recv.append(_remote(other, other, ss.at[p], rs.at[p], (mx, my, 1 - mc)))
        return send, recv

    def start(srcs, bufs, lands, ss, rs):
        for cp in copies(bufs, ss, rs, only_sends=True)[0]:
            cp.start()

    def finish(srcs, bufs, lands, ss, rs):
        send, recv = copies(bufs, ss, rs)
        for cp in recv:
            cp.wait_recv()
        for cp in send:
            cp.wait_send()

    return Comm(bufs=fulls, n_sems=n, start=start, finish=finish)


def pair_sums(gs, recvs, mc, *, name):
    n = len(gs)

    def body(s_ref, *refs):
        for p in range(n):
            g_ref, r_ref, o_ref = refs[2 * p], refs[2 * p + 1], refs[2 * n + p]
            o_ref[...] = (g_ref[...].astype(F32) + r_ref[...].astype(F32)).astype(BF16)

    in_specs, out_specs, out_shape, args = [], [], [], []
    for g, r in zip(gs, recvs):
        nq, hr, C = r.shape
        in_specs += [pl.BlockSpec((1, hr, C), lambda q, s: (q, s[0], 0)), pl.BlockSpec((1, hr, C), lambda q, s: (q, 0, 0))]
        out_specs.append(pl.BlockSpec((1, hr, C), lambda q, s: (q, 0, 0)))
        out_shape.append(jax.ShapeDtypeStruct((nq, hr, C), BF16))
        args += [g, r]
    return _call(body, name=name, grid=(N_QUAD,), in_specs=in_specs, out_specs=out_specs, out_shape=out_shape,
                 args=args, prefetch=[mc])


def chip_sums(ss, recvs, quad_mc, *, name):
    n = len(ss)
    steps = 2

    def body(q_ref, *refs):
        for p in range(n):
            s_ref, r_ref, o_ref = refs[2 * p], refs[2 * p + 1], refs[2 * n + p]
            total = s_ref[0].astype(F32)
            for k in range(N_QUAD - 1):
                total = total + r_ref[k].astype(F32)
            o_ref[...] = total

    in_specs, out_specs, out_shape, args = [], [], [], []
    for s, r in zip(ss, recvs):
        _, hr, C = s.shape
        rb = hr // steps
        in_specs += [pl.BlockSpec((1, rb, C), lambda i, q: (q[0], i, 0)),
                     pl.BlockSpec((N_QUAD - 1, rb, C), lambda i, q: (0, i, 0))]
        out_specs.append(pl.BlockSpec((rb, C), lambda i, q: (q[1] * steps + i, 0)))
        out_shape.append(jax.ShapeDtypeStruct((2 * hr, C), F32))
        args += [s, r]
    return _call(body, name=name, grid=(steps,), in_specs=in_specs, out_specs=out_specs, out_shape=out_shape,
                 args=args, prefetch=[quad_mc])


def _stack_cols(w_full):
    K, N = w_full.shape
    return w_full.reshape(K, N_QUAD, N // N_QUAD).transpose(1, 0, 2)


def _unstack_cols(w4):
    nq, K, n = w4.shape
    return w4.transpose(1, 0, 2).reshape(K, nq * n)


def kernel(x, c, w_ada, b_ada, g_norm1, ffn1_w_gate, ffn1_w_up, ffn1_w_down, g_norm2, w_in, g_sgu_ln, b_sgu_ln, w_spatial, b_spatial, g_q, g_k, attn_sinks, w_branch_a, w_branch_b, w_out, g_norm3, ffn2_w_gate, ffn2_w_up, ffn2_w_down, loss_target, m_w_ada, m_b_ada, m_g_norm1, m_ffn1_w_gate, m_ffn1_w_up, m_ffn1_w_down, m_g_norm2, m_w_in, m_g_sgu_ln, m_b_sgu_ln, m_w_spatial, m_b_spatial, m_g_q, m_g_k, m_attn_sinks, m_w_branch_a, m_w_branch_b, m_w_out, m_g_norm3, m_ffn2_w_gate, m_ffn2_w_up, m_ffn2_w_down, v_w_ada, v_b_ada, v_g_norm1, v_ffn1_w_gate, v_ffn1_w_up, v_ffn1_w_down, v_g_norm2, v_w_in, v_g_sgu_ln, v_b_sgu_ln, v_w_spatial, v_b_spatial, v_g_q, v_g_k, v_attn_sinks, v_w_branch_a, v_w_branch_b, v_w_out, v_g_norm3, v_ffn2_w_gate, v_ffn2_w_up, v_ffn2_w_down):
    B, S, D = x.shape
    T = B * S
    n_mod = b_ada.shape[1] // D
    mx, my, mc = _place()
    quad = 2 * mx + my
    mc_arr = jnp.reshape(mc, (1,)).astype(jnp.int32)
    quad_arr = jnp.reshape(quad, (1,)).astype(jnp.int32)
    quad_mc = jnp.stack([quad, mc]).astype(jnp.int32)
    tm = min(512, S)
    tm_big = min(1024, S)
    tt_half = min(2048, T)
    cpb = min(4, S // CHUNK)

    xt = x.reshape(T, D)
    tgt = loss_target.reshape(T, D)

    tr = lambda a: jnp.swapaxes(a, 1, 2)
    gather_cond = ag8_comm(c)
    stack_wg1, stack_wu1 = cast_into_stacks([tr(ffn1_w_gate)[0], tr(ffn1_w_up)[0]], quad_arr, name="stack_gu1",
                                            comms=[gather_cond])
    c_all = gather_cond.lands[0].reshape(N_DEV * B, D)
    n_ada = w_ada.shape[2]
    b_mine = lax.dynamic_slice(b_ada, (0, quad * n_ada), (1, n_ada))
    mods_part = ada_fwd(c_all, w_ada[0], b_mine, name="ada_fwd")
    gather_mods = ag8_comm(mods_part)
    gather_wg1, gather_wu1 = gather_comm([stack_wg1], forward=False), gather_comm([stack_wu1])
    later = [ffn1_w_down, tr(w_in), w_branch_a, w_branch_b, w_out, tr(ffn2_w_gate), tr(ffn2_w_up), ffn2_w_down]
    stacks = cast_into_stacks([w[0] for w in later], quad_arr, name="stack_rest", comms=[gather_wg1, gather_mods])
    forward_wg1 = forward_comm(gather_wg1.bufs_out)
    (mods_parts,) = gather_mods.lands
    gather_wd1, gather_win = gather_comm([stacks[0]]), gather_comm([stacks[1]])
    gather_abo = gather_comm(stacks[2:5], forward=False)
    gather_wg3, gather_wu3 = gather_comm([stacks[5]], forward=False), gather_comm([stacks[6]], forward=False)
    gather_wd3 = gather_comm([stacks[7]])
    mods = jnp.concatenate([mods_parts[2 * j] for j in range(N_QUAD)], axis=1)
    me = 4 * mx + 2 * my + mc
    mods = lax.dynamic_slice(mods, (me * B, 0), (B, n_mod * D))
    mods = mods.reshape(B, n_mod, 1, D)
    sh1, sc1, ga1, sh2, sc2, ga2, sh3, sc3, ga3 = [(mods, j) for j in range(n_mod)]
    bs_t = b_spatial[0].T
    ws = w_spatial[0]

    xn1 = norm_mod_fwd(xt, g_norm1, sc1, sh1, seq=S, tm=tm, name="norm1", comms=[gather_wu1, forward_wg1])
    (wu1,), (wg1,) = gather_wu1.bufs_out, forward_wg1.bufs_out
    g1, u1, a1 = ffn_up(xn1, wg1, wu1, tm=tm_big, name="ffn1_up", comms=[gather_wd1, gather_abo])
    (wd1,) = gather_wd1.bufs_out
    forward_abo = forward_comm(gather_abo.bufs_out)
    h1, f1, xn2 = ffn_down(a1, wd1, xt, ga1, norm=(g_norm2, sc2, sh2), seq=S, tm=tm, name="ffn1_down",
                           comms=[gather_win, forward_abo])
    (win4,), (wa4, wb4, wo4) = gather_win.bufs_out, forward_abo.bufs_out
    wa, wb = _unstack_cols(wa4), _unstack_cols(wb4)
    wo = wo4.reshape(D, D)
    win = win4.reshape(-1, D)
    groups = [(0, 2 * D_A), (2 * D_A, 2 * D_A + QKV_W), (2 * D_A + QKV_W, win.shape[0])]
    puv, pqkv, pgt = proj_fwd(xn2, win, groups, tm=tm, name="proj", comms=[gather_wg3])
    forward_wg3 = forward_comm(gather_wg3.bufs_out)
    sgu = sgu_fwd(puv, g_sgu_ln, b_sgu_ln, ws, bs_t, cpb=cpb, name="sgu_fwd")
    gq_t, gk_t = jnp.tile(g_q, (1, N_Q)), jnp.tile(g_k, (1, N_KV))
    att = attn_fwd(pqkv, gq_t, gk_t, attn_sinks, seq=S, name="attn_fwd", comms=[gather_wu3, forward_wg3])
    (wg3,) = forward_wg3.bufs_out
    forward_wu3 = forward_comm(gather_wu3.bufs_out)
    ya, yb, merged, mm, h2, xn3 = mixer_out_fwd(sgu, att, pgt, h1, ga2, wa, wb, wo, (g_norm3, sc3, sh3), seq=S,
                                                tm=tm, name="mixer_out", comms=[gather_wd3, forward_wu3])
    (wd3,), (wu3,) = gather_wd3.bufs_out, forward_wu3.bufs_out
    g3, u3, a3 = ffn_up(xn3, wg3, wu3, tm=tm_big, name="ffn2_up")
    dy, f3, lparts = ffn_down(a3, wd3, h2, ga3, target=tgt, seq=S, tm=tm, name="ffn2_down_loss")
    loss_part = jnp.sum(lparts[:, 0, 0])

    def sums_of(pair, tag):
        return pair_sums(pair.srcs, pair.lands, mc_arr, name=f"pair_sum_{tag}")

    def halves_of(*chips_and_tag):
        *chips, tag = chips_and_tag
        return chip_sums([s for ch in chips for s in ch.srcs], [r for ch in chips for r in ch.lands], quad_mc,
                         name=f"chip_sum_{tag}")

    dg3, du3, df3, dga3 = ffn_bwd_act(dy, f3, ga3, wd3, g3, u3, seq=S, tm=tm, name="ffn2_act_bwd")
    (dwd3,) = mm_tn([(a3, df3)], tt=T, name="ffn2_dwd")
    pair_wd3 = rs_pair_comm([dwd3])
    dwg3, dwu3 = mm_tn([(dg3, xn3), (du3, xn3)], tt=tt_half, name="ffn2_dwgu", comms=[pair_wd3])
    chip_wd3 = rs_chip_comm(sums_of(pair_wd3, "wd3"))
    pair_gu3 = rs_pair_comm([dwg3, dwu3])
    dh2, dsh3, dsc3, dgn3 = dx_norm_bwd([dg3, du3], [wg3, wu3], h2, dy, g_norm3, sc3, seq=S, tm=tm, name="ffn2_dx",
                                        comms=[chip_wd3, pair_gu3])
    sum_wg3, sum_wu3 = sums_of(pair_gu3, "gu3")
    chip_wg3, chip_wu3 = rs_chip_comm([sum_wg3]), rs_chip_comm([sum_wu3])

    dgt, dya, dyb, dsgu, datt, dm, dga2 = mixer_out_bwd(dh2, mm, ga2, ya, yb, pgt, wa, wb, wo, seq=S, tm=tm,
                                                        name="mixer_out_bwd", comms=[chip_wg3])
    dwo, dwa, dwb = mm_tn([(merged, dm), (sgu, dya), (att, dyb)], tt=tm_big, name="mixer_dw")
    pair_abo = rs_pair_comm([_stack_cols(dwa), _stack_cols(dwb), dwo.reshape(N_QUAD, D // N_QUAD, D)])
    duv, dws, dzs, dgln, dbln = sgu_bwd(puv, dsgu, g_sgu_ln, b_sgu_ln, ws, bs_t, cpb=cpb, name="sgu_bwd",
                                        comms=[pair_abo])
    chip_abo = rs_chip_comm(sums_of(pair_abo, "abo"))
    dqkv, dgq_h, dgk_h, dsk = attn_bwd(pqkv, datt, gq_t, gk_t, attn_sinks, seq=S, name="attn_bwd",
                                       comms=[chip_wu3, chip_abo])
    dgq = dgq_h.reshape(N_Q, HEAD_DIM).sum(axis=0, keepdims=True)
    dgk = dgk_h.reshape(N_KV, HEAD_DIM).sum(axis=0, keepdims=True)
    share3 = rs_share_comm(halves_of(chip_wg3, chip_wu3, chip_wd3, "ffn2"))
    dwin_uv, dwin_qkv = mm_tn([(duv, xn2), (dqkv, xn2)], tt=tt_half, name="mixer_dwin_a", comms=[share3])
    (dwin_gt,) = mm_tn([(dgt, xn2)], tt=tt_half, name="mixer_dwin_b")
    dwin_parts = [dwin_uv, dwin_qkv, dwin_gt]
    r_wg3, r_wu3, r_wd3 = share3.bufs_out
    pair_win = rs_pair_comm([jnp.concatenate(dwin_parts, axis=0).reshape(win4.shape)])
    dh1, dsh2, dsc2, dgn2 = dx_norm_bwd([duv, dqkv, dgt], [(win, r0, r1) for r0, r1 in groups], h1, dh2, g_norm2, sc2, seq=S,
                                        tm=tm, name="mixer_dx", comms=[pair_win])
    chip_win = rs_chip_comm(sums_of(pair_win, "win"))

    dg1, du1, df1, dga1 = ffn_bwd_act(dh1, f1, ga1, wd1, g1, u1, seq=S, tm=tm, name="ffn1_act_bwd", comms=[chip_win])
    share_mix = rs_share_comm(halves_of(chip_win, chip_abo, "mix"))

    db_s = dzs.reshape(CHUNK, N_GROUPS, D_A // N_GROUPS).sum(axis=2).T.reshape(1, N_GROUPS * CHUNK)
    tail = jnp.concatenate([db_s, dgq, dgk, dsk[0:1, 0:N_Q], loss_part.reshape(1, 1)], axis=1)
    tail = jnp.pad(tail, ((0, 0), (0, (-tail.shape[1]) % D)))
    lnrow = jnp.concatenate([dgln, dbln], axis=1)
    lnrow = jnp.pad(lnrow, ((0, 0), (0, (-lnrow.shape[1]) % D)))
    packed = jnp.concatenate([dgn2, dgn3, lnrow.reshape(-1, D), tail.reshape(-1, D), dws.reshape(-1, D)], axis=0)
    n_rows = packed.shape[0]
    packed = jnp.pad(packed, ((0, (-n_rows) % 8), (0, 0)))
    gather_small = ag8_comm(packed)

    dwg1, dwu1 = mm_tn([(dg1, xn1), (du1, xn1)], tt=tt_half, name="ffn1_dwgu", comms=[share_mix, gather_small])
    r_win, r_wa, r_wb, r_wo = share_mix.bufs_out
    small = sum8(gather_small.lands[0], name="sum_small")
    pair_gu1 = rs_pair_comm([dwg1, dwu1])
    (dwd1,) = mm_tn([(a1, df1)], tt=T, name="ffn1_dwd", comms=[pair_gu1])
    chip_gu1 = rs_chip_comm(sums_of(pair_gu1, "gu1"))
    pair_wd1 = rs_pair_comm([dwd1])
    dx, dsh1, dsc1, dgn1 = dx_norm_bwd([dg1, du1], [wg1, wu1], xt, dh1, g_norm1, sc1, seq=S, tm=tm, name="ffn1_dx",
                                       comms=[chip_gu1, pair_wd1])
    chip_wd1 = rs_chip_comm(sums_of(pair_wd1, "wd1"))
    share_gu1 = rs_share_comm(halves_of(chip_gu1, "gu1"))

    names = ["w_ada", "b_ada", "g_norm1", "ffn1_w_gate", "ffn1_w_up", "ffn1_w_down", "g_norm2", "w_in", "g_sgu_ln",
             "b_sgu_ln", "w_spatial", "b_spatial", "g_q", "g_k", "attn_sinks", "w_branch_a", "w_branch_b", "w_out",
             "g_norm3", "ffn2_w_gate", "ffn2_w_up", "ffn2_w_down"]
    weights = dict(zip(names, [w_ada, b_ada, g_norm1, ffn1_w_gate, ffn1_w_up, ffn1_w_down, g_norm2, w_in, g_sgu_ln,
                               b_sgu_ln, w_spatial, b_spatial, g_q, g_k, attn_sinks, w_branch_a, w_branch_b, w_out,
                               g_norm3, ffn2_w_gate, ffn2_w_up, ffn2_w_down]))
    m_in = dict(zip(names, [m_w_ada, m_b_ada, m_g_norm1, m_ffn1_w_gate, m_ffn1_w_up, m_ffn1_w_down, m_g_norm2, m_w_in,
                            m_g_sgu_ln, m_b_sgu_ln, m_w_spatial, m_b_spatial, m_g_q, m_g_k, m_attn_sinks, m_w_branch_a,
                            m_w_branch_b, m_w_out, m_g_norm3, m_ffn2_w_gate, m_ffn2_w_up, m_ffn2_w_down]))
    v_in = dict(zip(names, [v_w_ada, v_b_ada, v_g_norm1, v_ffn1_w_gate, v_ffn1_w_up, v_ffn1_w_down, v_g_norm2, v_w_in,
                            v_g_sgu_ln, v_b_sgu_ln, v_w_spatial, v_b_spatial, v_g_q, v_g_k, v_attn_sinks, v_w_branch_a,
                            v_w_branch_b, v_w_out, v_g_norm3, v_ffn2_w_gate, v_ffn2_w_up, v_ffn2_w_down]))
    transposed = ("ffn1_w_gate", "ffn1_w_up", "w_in", "ffn2_w_gate", "ffn2_w_up")
    grads, delta, new_m, new_v = {}, {}, {}, {}

    def update(group, steps, name, comms=()):
        form = lambda nm, a: (tr(a) if nm in transposed else a)[0].reshape(group[nm].shape)
        res = adamw([(form(nm, weights[nm]), g, form(nm, m_in[nm]), form(nm, v_in[nm])) for nm, g in group.items()],
                    steps=steps, name=name, comms=comms)
        back = lambda nm, a: tr(a[None]) if nm in transposed else a.reshape(weights[nm].shape)
        for nm, (g, d, mn, vn) in zip(group, res):
            grads[nm], delta[nm], new_m[nm], new_v[nm] = back(nm, g), back(nm, d), back(nm, mn), back(nm, vn)

    dmods = jnp.concatenate([dsh1, dsc1, dga1, dsh2, dsc2, dga2, dsh3, dsc3, dga3], axis=1)
    dmods = jnp.concatenate([dmods, jnp.pad(dgn1, ((0, 0), (0, (n_mod - 1) * D)))], axis=0)
    gather_dmods = ag8_comm(dmods)
    run_comms([chip_wd1, share_gu1, gather_dmods], name="rs_tail")
    update({"ffn2_w_gate": r_wg3, "ffn2_w_up": r_wu3, "ffn2_w_down": r_wd3, "w_out": r_wo, "w_branch_a": r_wa,
            "w_branch_b": r_wb}, 8, "adamw_early")
    r_wg1, r_wu1 = share_gu1.bufs_out
    (gathered,) = gather_dmods.lands
    dmods_all = gathered[:, 0:B].reshape(N_DEV * B, n_mod * D)
    dmods_mine = lax.dynamic_slice(dmods_all, (0, quad * n_ada), (N_DEV * B, n_ada))
    share_wd1 = rs_share_comm(halves_of(chip_wd1, "wd1"))
    db_ada, dw_ada, g_gn1 = ada_bwd(c_all, dmods_all, dmods_mine, gathered[:, B, 0:D], name="ada_bwd", comms=[share_wd1])
    (r_wd1,) = share_wd1.bufs_out

    ln_rows = lnrow.size // D
    tail_rows = tail.size // D
    r = 2
    g_gn2, g_gn3 = small[0:1], small[1:2]
    ln_flat = small[r:r + ln_rows].reshape(1, -1)
    r += ln_rows
    tail_flat = small[r:r + tail_rows].reshape(1, -1)
    r += tail_rows
    g_ws = small[r:r + dws.size // D].reshape(w_spatial.shape)
    g_gln, g_bln = ln_flat[:, 0:D_A], ln_flat[:, D_A:2 * D_A]
    o = N_GROUPS * CHUNK
    g_bs = tail_flat[:, 0:o].reshape(b_spatial.shape)
    g_gq, g_gk, g_sk = tail_flat[:, o:o + HEAD_DIM], tail_flat[:, o + HEAD_DIM:o + 2 * HEAD_DIM], tail_flat[:, o + 2 * HEAD_DIM:o + 2 * HEAD_DIM + N_Q]
    loss = tail_flat[0, o + 2 * HEAD_DIM + N_Q]

    update({"w_ada": dw_ada}, 16, "adamw_w_ada")
    update({"ffn1_w_gate": r_wg1, "ffn1_w_up": r_wu1, "ffn1_w_down": r_wd1, "w_in": r_win}, 8, "adamw_late")

    update({"b_ada": db_ada, "g_norm1": g_gn1, "g_norm2": g_gn2, "g_norm3": g_gn3, "g_sgu_ln": g_gln,
            "b_sgu_ln": g_bln, "w_spatial": g_ws.reshape(-1, CHUNK), "b_spatial": g_bs.reshape(N_GROUPS, CHUNK),
            "g_q": g_gq, "g_k": g_gk, "attn_sinks": g_sk}, 1, "adamw_small")

    return (loss, dx.reshape(B, S, D), *[grads[nm] for nm in names], *[delta[nm] for nm in names],
            *[new_m[nm] for nm in names], *[new_v[nm] for nm in names])
```

```python
import functools
import math
import operator

import jax
import jax.numpy as jnp
from jax import lax
from jax.experimental import pallas as pl
from jax.experimental.pallas import tpu as pltpu

F32 = jnp.float32
BF16 = jnp.bfloat16
EPS = 1e-6
NEG = -1e30
N_DEV = 8
N_QUAD = 4
D_A = 512
N_GROUPS = 4
CHUNK = 128
HEAD_DIM = 64
N_KV = 2
Q_PER_KV = 4
N_Q = N_KV * Q_PER_KV
D_B = N_Q * HEAD_DIM
QKV_W = D_B + 2 * N_KV * HEAD_DIM
K_OFF = D_B
V_OFF = D_B + N_KV * HEAD_DIM
ATT_SCALE = HEAD_DIM ** -0.5
GELU_K = math.sqrt(2.0 / math.pi)
GELU_C = 0.044715
ADAM_LR, ADAM_B1, ADAM_B2, ADAM_EPS, ADAM_WD, ADAM_STEP = 0.001, 0.9, 0.999, 1e-08, 0.01, 10
VMEM_LIMIT_BYTES = 56 * 1024 * 1024
MESH = pl.DeviceIdType.MESH
NT = (((1,), (1,)), ((), ()))
TN = (((0,), (0,)), ((), ()))
ANY = pl.BlockSpec(memory_space=pl.ANY)


def _dot(a, b):
    return jnp.dot(a, b, preferred_element_type=F32)


def _dg(a, b, dims):
    return lax.dot_general(a, b, dims, preferred_element_type=F32)


def _gelu_parts(x):
    t = jnp.tanh(GELU_K * (x + GELU_C * x * x * x))
    return 0.5 * x * (1.0 + t), t


def _dgelu(x, t):
    return 0.5 * (1.0 + t) + 0.5 * x * (1.0 - t * t) * (GELU_K * (1.0 + 3.0 * GELU_C * x * x))


def _mod_spec(mod, tps):
    mods, j = mod
    return pl.BlockSpec((1, None, 1, mods.shape[3]), lambda i: (i // tps, j, 0, 0))


def _resident(a):
    return pl.BlockSpec(a.shape, lambda *_: (0,) * a.ndim, pipeline_mode=pl.Buffered(1))


class Comm:
    def __init__(self, *, srcs=(), bufs=(), land_shapes=(), n_sems, start, finish):
        self.srcs, self.bufs, self.land_shapes = list(srcs), list(bufs), list(land_shapes)
        self.n_sems, self.start, self.finish = n_sems, start, finish
        self.bufs_out, self.lands = None, None


def _call(body, *, name, grid, in_specs, out_specs, out_shape, args, scratch_shapes=(), comms=(), prefetch=()):
    prefetch = list(prefetch)
    in_specs, out_specs, out_shape = list(in_specs), list(out_specs), list(out_shape)
    args, scratch = list(args), list(scratch_shapes)
    n_in, n_out, n_scr = len(args), len(out_shape), len(scratch)
    aliases, layout = {}, []
    for cm in comms:
        i0, o0, s0 = len(args), len(out_shape), len(scratch)
        args += cm.srcs + cm.bufs
        in_specs += [ANY] * (len(cm.srcs) + len(cm.bufs))
        out_shape += [jax.ShapeDtypeStruct(b.shape, b.dtype) for b in cm.bufs] + cm.land_shapes
        out_specs += [ANY] * (len(cm.bufs) + len(cm.land_shapes))
        for j in range(len(cm.bufs)):
            aliases[len(prefetch) + i0 + len(cm.srcs) + j] = o0 + j
        scratch += [pltpu.SemaphoreType.DMA((cm.n_sems,)), pltpu.SemaphoreType.DMA((cm.n_sems,))]
        layout.append((i0, o0, s0))
    n_in_all, n_out_all = len(args), len(out_shape)

    def wrapped(*refs):
        scalars, refs = refs[:len(prefetch)], refs[len(prefetch):]
        ins, outs, scr = refs[:n_in_all], refs[n_in_all:n_in_all + n_out_all], refs[n_in_all + n_out_all:]
        parts = []
        for cm, (i0, o0, s0) in zip(comms, layout):
            nb = len(cm.bufs)
            parts.append((ins[i0:i0 + len(cm.srcs)], outs[o0:o0 + nb], outs[o0 + nb:o0 + nb + len(cm.land_shapes)],
                          scr[s0], scr[s0 + 1]))
        if comms and grid:
            ids = [pl.program_id(d) for d in range(len(grid))]
            first = functools.reduce(operator.and_, [i == 0 for i in ids])
            last = functools.reduce(operator.and_, [i == g - 1 for i, g in zip(ids, grid)])

            @pl.when(first)
            def _():
                for cm, p in zip(comms, parts):
                    cm.start(*p)
        elif comms:
            for cm, p in zip(comms, parts):
                cm.start(*p)
        if body is not None:
            body(*scalars, *ins[:n_in], *outs[:n_out], *scr[:n_scr])
        if comms and grid:
            @pl.when(last)
            def _():
                for cm, p in zip(comms, parts):
                    cm.finish(*p)
        elif comms:
            for cm, p in zip(comms, parts):
                cm.finish(*p)

    if prefetch:
        kwargs = dict(grid_spec=pltpu.PrefetchScalarGridSpec(
            num_scalar_prefetch=len(prefetch), grid=grid, in_specs=in_specs, out_specs=out_specs, scratch_shapes=scratch))
    else:
        kwargs = dict(in_specs=in_specs, out_specs=out_specs, scratch_shapes=scratch, **(dict(grid=grid) if grid else {}))
    sem = ("arbitrary",) * len(grid)
    res = pl.pallas_call(
        wrapped, name=name, out_shape=out_shape, input_output_aliases=aliases,
        compiler_params=pltpu.CompilerParams(dimension_semantics=sem, vmem_limit_bytes=VMEM_LIMIT_BYTES), **kwargs,
    )(*prefetch, *args)
    for cm, (i0, o0, s0) in zip(comms, layout):
        nb = len(cm.bufs)
        cm.bufs_out = list(res[o0:o0 + nb])
        cm.lands = list(res[o0 + nb:o0 + nb + len(cm.land_shapes)])
    return list(res[:n_out])


def run_comms(comms, *, name):
    _call(None, name=name, grid=(), in_specs=[], out_specs=[], out_shape=[], args=[], comms=comms)


def norm_mod_fwd(h, g, sc, sh, *, seq, tm, name, comms=()):
    T, D = h.shape
    B, tps = T // seq, seq // tm

    def body(h_ref, g_ref, sc_ref, sh_ref, o_ref):
        x = h_ref[...]
        r = lax.rsqrt(jnp.mean(x * x, axis=-1, keepdims=True) + EPS)
        y = x * r * g_ref[...]
        o_ref[...] = (y * (1.0 + sc_ref[0]) + sh_ref[0]).astype(o_ref.dtype)

    return _call(
        body, name=name, grid=(T // tm,),
        in_specs=[pl.BlockSpec((tm, D), lambda i: (i, 0)), pl.BlockSpec((1, D), lambda i: (0, 0)),
                  _mod_spec(sc, tps), _mod_spec(sh, tps)],
        out_specs=[pl.BlockSpec((tm, D), lambda i: (i, 0))], out_shape=[jax.ShapeDtypeStruct((T, D), BF16)],
        args=[h, g, sc[0], sh[0]], comms=comms)[0]


def ffn_up(xn, wg, wu, *, tm, name, comms=()):
    T, D = xn.shape
    nq, fs, _ = wg.shape

    def body(x_ref, wg_ref, wu_ref, s_ref, q_ref, a_ref):
        x = x_ref[...]
        g = _dg(x, wg_ref[0], NT)
        u = _dg(x, wu_ref[0], NT)
        sg = jax.nn.sigmoid(g)
        s = g * sg
        s_ref[0] = s.astype(BF16)
        q_ref[0] = (u * (sg + s * (1.0 - sg))).astype(BF16)
        a_ref[0] = (s * u).astype(BF16)

    w_spec = pl.BlockSpec((1, fs, D), lambda q, i: (q, 0, 0))
    o_spec = pl.BlockSpec((1, tm, fs), lambda q, i: (q, i, 0))
    o_shape = jax.ShapeDtypeStruct((nq, T, fs), BF16)
    return _call(
        body, name=name, grid=(nq, T // tm),
        in_specs=[pl.BlockSpec((tm, D), lambda q, i: (i, 0)), w_spec, w_spec],
        out_specs=[o_spec, o_spec, o_spec], out_shape=[o_shape, o_shape, o_shape], args=[xn, wg, wu], comms=comms)


def _norm_mod(y, g_ref, sc_ref, sh_ref):
    nx, _ = _rms_parts(y)
    return ((nx * g_ref[...]) * (1.0 + sc_ref[0]) + sh_ref[0]).astype(BF16)


def ffn_down(a, wd, hin, ga, *, target=None, norm=None, seq, tm, name, comms=()):
    nq, T, fs = a.shape
    D = wd.shape[-1]
    B, tps, nt = T // seq, seq // tm, T // tm

    def body(a_ref, wd_ref, h_ref, ga_ref, *rest):
        rest = list(rest)
        t_ref = rest.pop(0) if target is not None else None
        norm_refs = [rest.pop(0) for _ in range(3)] if norm is not None else None
        o_ref, f_ref = rest[0], rest[1]
        f = _dot(a_ref[0], wd_ref[0])
        for q in range(1, nq):
            f = f + _dot(a_ref[q], wd_ref[q])
        f_ref[...] = f.astype(BF16)
        y = h_ref[...] + (0.5 * ga_ref[0]) * f
        if target is not None:
            e = y - t_ref[...]
            o_ref[...] = e * (1.0 / D)
            rest[2][...] = jnp.full(rest[2].shape, 0.5 * jnp.sum(e * e) * (1.0 / D), F32)
        else:
            o_ref[...] = y
        if norm is not None:
            rest[2][...] = _norm_mod(y, *norm_refs)

    tile = pl.BlockSpec((tm, D), lambda i: (i, 0))
    in_specs = [pl.BlockSpec((nq, tm, fs), lambda i: (0, i, 0)), _resident(wd), tile, _mod_spec(ga, tps)]
    out_specs = [tile, tile]
    out_shape = [jax.ShapeDtypeStruct((T, D), F32), jax.ShapeDtypeStruct((T, D), BF16)]
    args = [a, wd, hin, ga[0]]
    if target is not None:
        in_specs.append(tile)
        args.append(target)
        out_specs.append(pl.BlockSpec((1, 8, 128), lambda i: (i, 0, 0)))
        out_shape.append(jax.ShapeDtypeStruct((nt, 8, 128), F32))
    if norm is not None:
        in_specs += [pl.BlockSpec((1, D), lambda i: (0, 0)), _mod_spec(norm[1], tps), _mod_spec(norm[2], tps)]
        args += [norm[0], norm[1][0], norm[2][0]]
        out_specs.append(tile)
        out_shape.append(jax.ShapeDtypeStruct((T, D), BF16))
    return _call(body, name=name, grid=(nt,), in_specs=in_specs, out_specs=out_specs, out_shape=out_shape, args=args,
                 comms=comms)


def proj_fwd(xn, w, groups, *, tm, name, comms=()):
    T, D = xn.shape

    def body(x_ref, w_ref, *o_refs):
        x = x_ref[...]
        for (r0, r1), o_ref in zip(groups, o_refs):
            o_ref[...] = _dg(x, w_ref[r0:r1, :], NT).astype(BF16)

    return _call(
        body, name=name, grid=(T // tm,), in_specs=[pl.BlockSpec((tm, D), lambda i: (i, 0)), _resident(w)],
        out_specs=[pl.BlockSpec((tm, r1 - r0), lambda i: (i, 0)) for r0, r1 in groups],
        out_shape=[jax.ShapeDtypeStruct((T, r1 - r0), BF16) for r0, r1 in groups], args=[xn, w], comms=comms)


def _layer_norm_parts(v):
    mu = jnp.mean(v, axis=-1, keepdims=True)
    d = v - mu
    rstd = lax.rsqrt(jnp.mean(d * d, axis=-1, keepdims=True) + EPS)
    return d * rstd, rstd


def _causal():
    row = lax.broadcasted_iota(jnp.int32, (CHUNK, CHUNK), 0)
    col = lax.broadcasted_iota(jnp.int32, (CHUNK, CHUNK), 1)
    return row >= col


def sgu_fwd(puv, gln, bln, ws, bs_t, *, cpb, name, comms=()):
    T = puv.shape[0]
    gd = D_A // N_GROUPS
    tm = cpb * CHUNK

    def body(p_ref, gln_ref, bln_ref, ws_ref, bs_ref, o_ref):
        causal = _causal()
        wm = [jnp.where(causal, ws_ref[g], 0.0).astype(BF16) for g in range(N_GROUPS)]
        for j in range(cpb):
            rows = slice(j * CHUNK, (j + 1) * CHUNK)
            u, _ = _gelu_parts(p_ref[rows, 0:D_A].astype(F32))
            vv, _ = _gelu_parts(p_ref[rows, D_A:2 * D_A].astype(F32))
            vhat, _ = _layer_norm_parts(vv)
            vn = (vhat * gln_ref[...] + bln_ref[...]).astype(BF16)
            for g in range(N_GROUPS):
                cols = slice(g * gd, (g + 1) * gd)
                z = _dot(wm[g], vn[:, cols]) + bs_ref[:, g:g + 1]
                o_ref[rows, cols] = (u[:, cols] * z).astype(BF16)

    full = lambda a: pl.BlockSpec(a.shape, lambda i: (0,) * a.ndim)
    return _call(
        body, name=name, grid=(T // tm,),
        in_specs=[pl.BlockSpec((tm, 2 * D_A), lambda i: (i, 0)), full(gln), full(bln), full(ws), full(bs_t)],
        out_specs=[pl.BlockSpec((tm, D_A), lambda i: (i, 0))], out_shape=[jax.ShapeDtypeStruct((T, D_A), BF16)],
        args=[puv, gln, bln, ws, bs_t], comms=comms)[0]


def _rms_parts(x):
    r = lax.rsqrt(jnp.mean(x * x, axis=-1, keepdims=True) + EPS)
    return x * r, r


KV_W = Q_PER_KV * HEAD_DIM
KV2 = N_KV * HEAD_DIM
STACK = Q_PER_KV * CHUNK


def _iota(shape, dim):
    return lax.broadcasted_iota(jnp.int32, shape, dim)


def _head_mean_matrix(n):
    return jnp.where((_iota((n, n), 0) >> 6) == (_iota((n, n), 1) >> 6), 1.0 / HEAD_DIM, 0.0).astype(BF16)


def _repeat_matrix(h):
    return jnp.where(_iota((KV2, KV_W), 0) == h * HEAD_DIM + (_iota((KV2, KV_W), 1) & (HEAD_DIM - 1)), 1.0, 0.0).astype(BF16)


def _fold_matrix(h):
    j, l = _iota((KV_W, KV2), 0), _iota((KV_W, KV2), 1)
    return jnp.where(((j & (HEAD_DIM - 1)) == (l & (HEAD_DIM - 1))) & ((l >> 6) == h), 1.0, 0.0).astype(BF16)


def _dot_split(x, m):
    hi = x.astype(BF16)
    lo = (x - hi.astype(F32)).astype(BF16)
    return _dot(hi, m) + _dot(lo, m)


def _head_rms(x, mean_matrix):
    r = lax.rsqrt(_dot_split(x * x, mean_matrix) + EPS)
    return x * r, r


def _stack_heads(x):
    head = _iota(x.shape, 1) >> 6
    return jnp.concatenate([jnp.where(head == g, x, 0.0).astype(BF16) for g in range(Q_PER_KV)], axis=0)


def _unstack_heads(y):
    head = _iota((CHUNK, KV_W), 1) >> 6
    out = jnp.where(head == 0, y[0:CHUNK], 0.0)
    for g in range(1, Q_PER_KV):
        out = out + jnp.where(head == g, y[g * CHUNK:(g + 1) * CHUNK], 0.0)
    return out


SOFTMAX_ROWS = 32


def _fill_mask_bias(bias_ref):
    qi = _iota((CHUNK, 2 * CHUNK), 0)
    kj = _iota((CHUNK, 2 * CHUNK), 1)
    diff = qi + CHUNK - kj
    window = (diff >= 0) & (diff < CHUNK)
    bias_ref[0] = jnp.where(window & (kj >= CHUNK), 0.0, NEG)
    bias_ref[1] = jnp.where(window, 0.0, NEG)


def _softmax_rows(s_ref, bias_ref, first, sink_ref, h, c):
    rows = pl.ds(pl.multiple_of(c * SOFTMAX_ROWS, SOFTMAX_ROWS), SOFTMAX_ROWS)
    in_block = pl.ds(pl.multiple_of((c % (CHUNK // SOFTMAX_ROWS)) * SOFTMAX_ROWS, SOFTMAX_ROWS), SOFTMAX_ROWS)
    sink = sink_ref[0, h * Q_PER_KV + c // (CHUNK // SOFTMAX_ROWS)]
    s = s_ref[rows, :] + bias_ref[first, in_block, :]
    m = jnp.maximum(jnp.max(s, axis=-1, keepdims=True), sink)
    p = jnp.exp(s - m)
    es = jnp.exp(sink - m)
    inv = 1.0 / (jnp.sum(p, axis=-1, keepdims=True) + es)
    return rows, p * inv, es * inv


def _fill_keys(p_ref, gk_ref, krep, vrep, seq):
    mean_k = _head_mean_matrix(KV2)
    reps = [_repeat_matrix(h) for h in range(N_KV)]
    for h in range(N_KV):
        krep[h][0:CHUNK, :] = jnp.zeros((CHUNK, KV_W), BF16)
        vrep[h][0:CHUNK, :] = jnp.zeros((CHUNK, KV_W), BF16)
    rows = min(seq, 4 * CHUNK)

    def piece(j, carry):
        r0 = pl.multiple_of(j * rows, CHUNK)
        nk, _ = _head_rms(p_ref[pl.ds(r0, rows), K_OFF:K_OFF + KV2].astype(F32), mean_k)
        kn = (nk * gk_ref[...]).astype(BF16)
        v = p_ref[pl.ds(r0, rows), V_OFF:V_OFF + KV2].astype(BF16)
        r1 = pl.multiple_of(r0 + CHUNK, CHUNK)
        for h in range(N_KV):
            krep[h][pl.ds(r1, rows), :] = _dot(kn, reps[h]).astype(BF16)
            vrep[h][pl.ds(r1, rows), :] = _dot(v, reps[h]).astype(BF16)
        return carry

    lax.fori_loop(0, seq // rows, piece, 0)


def attn_fwd(pqkv, gq_t, gk_t, sinks, *, seq, name, comms=()):
    T = pqkv.shape[0]
    nb = seq // CHUNK

    def body(p_ref, gq_ref, gk_ref, sink_ref, o_ref, k0, k1, v0, v1, bias_ref, s_ref, pr_ref):
        krep, vrep = [k0, k1], [v0, v1]
        _fill_keys(p_ref, gk_ref, krep, vrep, seq)
        _fill_mask_bias(bias_ref)
        mean_q = _head_mean_matrix(D_B)

        def block(i, carry):
            r0 = pl.multiple_of(i * CHUNK, CHUNK)
            first = jnp.minimum(i, 1)
            nq, _ = _head_rms(p_ref[pl.ds(r0, CHUNK), 0:D_B].astype(F32), mean_q)
            qn = nq * (gq_ref[...] * ATT_SCALE)
            for h in range(N_KV):
                qs = _stack_heads(qn[:, h * KV_W:(h + 1) * KV_W])
                s_ref[...] = _dg(qs, krep[h][pl.ds(r0, 2 * CHUNK), :], NT)

                def rows_of(c, carry2):
                    rows, probs, _ = _softmax_rows(s_ref, bias_ref, first, sink_ref, h, c)
                    pr_ref[rows, :] = probs.astype(BF16)
                    return carry2

                lax.fori_loop(0, STACK // SOFTMAX_ROWS, rows_of, 0, unroll=True)
                out = _unstack_heads(_dot(pr_ref[...], vrep[h][pl.ds(r0, 2 * CHUNK), :]))
                o_ref[pl.ds(r0, CHUNK), h * KV_W:(h + 1) * KV_W] = out.astype(BF16)
            return carry

        lax.fori_loop(0, nb, block, 0)

    return _call(
        body, name=name, grid=(T // seq,),
        in_specs=[pl.BlockSpec((seq, QKV_W), lambda b: (b, 0)), pl.BlockSpec(gq_t.shape, lambda b: (0, 0)),
                  pl.BlockSpec(gk_t.shape, lambda b: (0, 0)), pl.BlockSpec(memory_space=pltpu.SMEM)],
        out_specs=[pl.BlockSpec((seq, D_B), lambda b: (b, 0))], out_shape=[jax.ShapeDtypeStruct((T, D_B), BF16)],
        scratch_shapes=[pltpu.VMEM((seq + CHUNK, KV_W), BF16)] * 4
        + [pltpu.VMEM((2, CHUNK, 2 * CHUNK), F32), pltpu.VMEM((STACK, 2 * CHUNK), F32), pltpu.VMEM((STACK, 2 * CHUNK), BF16)],
        args=[pqkv, gq_t, gk_t, sinks], comms=comms)[0]


def mixer_out_fwd(sgu, att, pg, hin, ga, wa, wb, wo, norm, *, seq, tm, name, comms=()):
    T, D = hin.shape
    B, tps = T // seq, seq // tm

    def body(s_ref, t_ref, pg_ref, h_ref, ga_ref, wa_ref, wb_ref, wo_ref, g_ref, sc_ref, sh_ref,
             ya_ref, yb_ref, mg_ref, m_ref, ho_ref, xn_ref):
        ya = _dot(s_ref[...], wa_ref[...])
        yb = _dot(t_ref[...], wb_ref[...])
        merged = (jax.nn.sigmoid(pg_ref[:, 0:D].astype(F32)) * ya
                  + jax.nn.sigmoid(pg_ref[:, D:2 * D].astype(F32)) * yb)
        mg = merged.astype(BF16)
        m = _dot(mg, wo_ref[...])
        ya_ref[...] = ya.astype(BF16)
        yb_ref[...] = yb.astype(BF16)
        mg_ref[...] = mg
        m_ref[...] = m.astype(BF16)
        y = h_ref[...] + ga_ref[0] * m
        ho_ref[...] = y
        xn_ref[...] = _norm_mod(y, g_ref, sc_ref, sh_ref)

    tile = lambda w: pl.BlockSpec((tm, w), lambda i: (i, 0))
    b16 = jax.ShapeDtypeStruct((T, D), BF16)
    return _call(
        body, name=name, grid=(T // tm,),
        in_specs=[tile(D_A), tile(D_B), tile(2 * D), tile(D), _mod_spec(ga, tps), _resident(wa), _resident(wb),
                  _resident(wo), pl.BlockSpec((1, D), lambda i: (0, 0)), _mod_spec(norm[1], tps), _mod_spec(norm[2], tps)],
        out_specs=[tile(D)] * 6, out_shape=[b16, b16, b16, b16, jax.ShapeDtypeStruct((T, D), F32), b16],
        args=[sgu, att, pg, hin, ga[0], wa, wb, wo, norm[0], norm[1][0], norm[2][0]], comms=comms)


def ffn_bwd_act(dh, f, ga, wd, s, q, *, seq, tm, name, comms=()):
    T, D = dh.shape
    nq, fs, _ = wd.shape
    B, tps = T // seq, seq // tm

    def body(dh_ref, f_ref, ga_ref, wd_ref, s_ref, q_ref, dg_ref, du_ref, df_ref, dga_ref):
        i = pl.program_id(0)
        d = dh_ref[...]
        df = ((0.5 * ga_ref[0]) * d).astype(BF16)
        df_ref[...] = df
        part = 0.5 * jnp.sum(d * f_ref[...].astype(F32), axis=0, keepdims=True)
        for j in range(nq):
            da = _dg(df, wd_ref[j], NT)
            du_ref[j] = (da * s_ref[j].astype(F32)).astype(BF16)
            dg_ref[j] = (da * q_ref[j].astype(F32)).astype(BF16)

        @pl.when(i % tps == 0)
        def _():
            dga_ref[0] = part

        @pl.when(i % tps != 0)
        def _():
            dga_ref[0] += part

    tile = pl.BlockSpec((tm, D), lambda i: (i, 0))
    per_seq = pl.BlockSpec((1, 1, D), lambda i: (i // tps, 0, 0))
    act = pl.BlockSpec((nq, tm, fs), lambda i: (0, i, 0))
    o_shape = jax.ShapeDtypeStruct((nq, T, fs), BF16)
    dg, du, df, dga = _call(
        body, name=name, grid=(T // tm,), in_specs=[tile, tile, _mod_spec(ga, tps), _resident(wd), act, act],
        out_specs=[act, act, tile, per_seq],
        out_shape=[o_shape, o_shape, jax.ShapeDtypeStruct((T, D), BF16), jax.ShapeDtypeStruct((B, 1, D), F32)],
        args=[dh, f, ga[0], wd, s, q], comms=comms)
    return dg, du, df, dga.reshape(B, D)


def mm_tn(pairs, *, tt, name, comms=()):
    n = len(pairs)
    flat = []
    for pair in pairs:
        for a in pair:
            if not any(a is b for b in flat):
                flat.append(a)
    where = lambda a: [k for k, b in enumerate(flat) if a is b][0]
    nq = max([a.shape[0] for a in flat if a.ndim == 3] + [1])
    T = flat[0].shape[-2]
    tt = min(tt, T)
    nt = T // tt
    stacked = [x.ndim == 3 or y.ndim == 3 for x, y in pairs]

    def body(*refs):
        in_refs, o_refs, accs = refs[:len(flat)], refs[len(flat):len(flat) + n], refs[len(flat) + n:]
        t = pl.program_id(1)
        value = lambda a: in_refs[where(a)][0] if a.ndim == 3 else in_refs[where(a)][...]

        def put(j, v):
            if stacked[j]:
                o_refs[j][0] = v.astype(BF16)
            else:
                o_refs[j][...] = v.astype(BF16)

        for j, (x, y) in enumerate(pairs):
            part = _dg(value(x), value(y), TN)
            if nt == 1:
                put(j, part)
                continue

            @pl.when(t == 0)
            def _():
                accs[j][...] = part

            @pl.when(t != 0)
            def _():
                accs[j][...] += part

        if nt > 1:
            @pl.when(t == nt - 1)
            def _():
                for j in range(n):
                    put(j, accs[j][...])

    def spec(a):
        if a.ndim == 3:
            return pl.BlockSpec((1, tt, a.shape[2]), lambda q, t: (q, t, 0))
        return pl.BlockSpec((tt, a.shape[1]), lambda q, t: (t, 0))

    out_specs, out_shape = [], []
    for j, (x, y) in enumerate(pairs):
        K, N = x.shape[-1], y.shape[-1]
        if stacked[j]:
            out_specs.append(pl.BlockSpec((1, K, N), lambda q, t: (q, 0, 0)))
            out_shape.append(jax.ShapeDtypeStruct((nq, K, N), BF16))
        else:
            out_specs.append(pl.BlockSpec((K, N), lambda q, t: (0, 0)))
            out_shape.append(jax.ShapeDtypeStruct((K, N), BF16))
    return _call(
        body, name=name, grid=(nq, nt), in_specs=[spec(a) for a in flat],
        out_specs=out_specs, out_shape=out_shape,
        scratch_shapes=[pltpu.VMEM((x.shape[-1], y.shape[-1]), F32) for x, y in pairs] if nt > 1 else [],
        args=flat, comms=comms)


def dx_norm_bwd(dys, ws, hin, dh_out, g, sc, *, seq, tm, name, comms=()):
    T, D = hin.shape
    B, tps = T // seq, seq // tm
    n = len(dys)
    ranged = [w if isinstance(w, tuple) else (w, 0, w.shape[-2]) for w in ws]
    ws = []
    for w, _, _ in ranged:
        if not any(w is u for u in ws):
            ws.append(w)
    where = [[k for k, u in enumerate(ws) if u is w][0] for w, _, _ in ranged]

    def body(*refs):
        dy_refs, w_refs = refs[:n], refs[n:n + len(ws)]
        h_ref, dho_ref, g_ref, sc_ref, dhi_ref, dsh_ref, dsc_ref, dgn_ref = refs[n + len(ws):]
        i = pl.program_id(0)
        dxn = None
        for j in range(n):
            w_ref, (_, r0, r1) = w_refs[where[j]], ranged[j]
            if dys[j].ndim == 3:
                for q in range(dys[j].shape[0]):
                    part = _dot(dy_refs[j][q], w_ref[q])
                    dxn = part if dxn is None else dxn + part
            else:
                part = _dot(dy_refs[j][...], w_ref[r0:r1, :])
                dxn = part if dxn is None else dxn + part
        x = h_ref[...]
        nx, r = _rms_parts(x)
        gain = g_ref[...]
        one_sc = 1.0 + sc_ref[0]
        dsh = jnp.sum(dxn, axis=0, keepdims=True)
        dsc = jnp.sum(dxn * (nx * gain), axis=0, keepdims=True)
        dgn = jnp.sum(dxn * one_sc * nx, axis=0, keepdims=True)
        dn = dxn * one_sc * gain
        dhi_ref[...] = dho_ref[...] + r * (dn - nx * jnp.mean(dn * nx, axis=-1, keepdims=True))

        @pl.when(i % tps == 0)
        def _():
            dsh_ref[0] = dsh
            dsc_ref[0] = dsc

        @pl.when(i % tps != 0)
        def _():
            dsh_ref[0] += dsh
            dsc_ref[0] += dsc

        @pl.when(i == 0)
        def _():
            dgn_ref[...] = dgn

        @pl.when(i != 0)
        def _():
            dgn_ref[...] += dgn

    def dy_spec(a):
        if a.ndim == 3:
            return pl.BlockSpec((a.shape[0], tm, a.shape[2]), lambda i: (0, i, 0))
        return pl.BlockSpec((tm, a.shape[1]), lambda i: (i, 0))

    tile = pl.BlockSpec((tm, D), lambda i: (i, 0))
    per_seq = pl.BlockSpec((1, 1, D), lambda i: (i // tps, 0, 0))
    row = pl.BlockSpec((1, D), lambda i: (0, 0))
    dhi, dsh, dsc, dgn = _call(
        body, name=name, grid=(T // tm,),
        in_specs=[dy_spec(a) for a in dys] + [_resident(w) for w in ws] + [tile, tile, row, _mod_spec(sc, tps)],
        out_specs=[tile, per_seq, per_seq, row],
        out_shape=[jax.ShapeDtypeStruct((T, D), F32), jax.ShapeDtypeStruct((B, 1, D), F32),
                   jax.ShapeDtypeStruct((B, 1, D), F32), jax.ShapeDtypeStruct((1, D), F32)],
        args=[*dys, *ws, hin, dh_out, g, sc[0]], comms=comms)
    return dhi, dsh.reshape(B, D), dsc.reshape(B, D), dgn


def mixer_out_bwd(dh, m, ga, ya, yb, pg, wa, wb, wo, *, seq, tm, name, comms=()):
    T, D = dh.shape
    B, tps = T // seq, seq // tm

    def body(dh_ref, m_ref, ga_ref, ya_ref, yb_ref, pg_ref, wa_ref, wb_ref, wo_ref,
             dg_ref, dya_ref, dyb_ref, ds_ref, dt_ref, dm_ref, dga_ref):
        i = pl.program_id(0)
        d = dh_ref[...]
        dm = (ga_ref[0] * d).astype(BF16)
        dm_ref[...] = dm
        part = jnp.sum(d * m_ref[...].astype(F32), axis=0, keepdims=True)

        @pl.when(i % tps == 0)
        def _():
            dga_ref[0] = part

        @pl.when(i % tps != 0)
        def _():
            dga_ref[0] += part

        dmg = _dg(dm, wo_ref[...], NT)
        sa = jax.nn.sigmoid(pg_ref[:, 0:D].astype(F32))
        sb = jax.nn.sigmoid(pg_ref[:, D:2 * D].astype(F32))
        dg_ref[:, 0:D] = (dmg * ya_ref[...].astype(F32) * (sa * (1.0 - sa))).astype(BF16)
        dg_ref[:, D:2 * D] = (dmg * yb_ref[...].astype(F32) * (sb * (1.0 - sb))).astype(BF16)
        dya = (dmg * sa).astype(BF16)
        dyb = (dmg * sb).astype(BF16)
        dya_ref[...] = dya
        dyb_ref[...] = dyb
        ds_ref[...] = _dg(dya, wa_ref[...], NT).astype(BF16)
        dt_ref[...] = _dg(dyb, wb_ref[...], NT).astype(BF16)

    tile = lambda w: pl.BlockSpec((tm, w), lambda i: (i, 0))
    per_seq = pl.BlockSpec((1, 1, D), lambda i: (i // tps, 0, 0))
    dg, dya, dyb, ds, dt, dm, dga = _call(
        body, name=name, grid=(T // tm,),
        in_specs=[tile(D), tile(D), _mod_spec(ga, tps), tile(D), tile(D), tile(2 * D), _resident(wa), _resident(wb),
                  _resident(wo)],
        out_specs=[tile(2 * D), tile(D), tile(D), tile(D_A), tile(D_B), tile(D), per_seq],
        out_shape=[jax.ShapeDtypeStruct((T, 2 * D), BF16), jax.ShapeDtypeStruct((T, D), BF16),
                   jax.ShapeDtypeStruct((T, D), BF16), jax.ShapeDtypeStruct((T, D_A), BF16),
                   jax.ShapeDtypeStruct((T, D_B), BF16), jax.ShapeDtypeStruct((T, D), BF16),
                   jax.ShapeDtypeStruct((B, 1, D), F32)],
        args=[dh, m, ga[0], ya, yb, pg, wa, wb, wo], comms=comms)
    return dg, dya, dyb, ds, dt, dm, dga.reshape(B, D)


def sgu_bwd(puv, dsgu, gln, bln, ws, bs_t, *, cpb, name, comms=()):
    T = puv.shape[0]
    gd = D_A // N_GROUPS
    tm = cpb * CHUNK

    def body(p_ref, ds_ref, gln_ref, bln_ref, ws_ref, bs_ref, d_ref, dws_ref, dz_ref, dgl_ref, dbl_ref):
        i = pl.program_id(0)
        causal = _causal()
        wf = [jnp.where(causal, ws_ref[g], 0.0) for g in range(N_GROUPS)]
        wm = [w.astype(BF16) for w in wf]
        wt = [w.T.astype(BF16) for w in wf]
        dws = [jnp.zeros((CHUNK, CHUNK), F32) for _ in range(N_GROUPS)]
        dzs = jnp.zeros((CHUNK, D_A), F32)
        dgl = jnp.zeros((1, D_A), F32)
        dbl = jnp.zeros((1, D_A), F32)
        for j in range(cpb):
            rows = slice(j * CHUNK, (j + 1) * CHUNK)
            au = p_ref[rows, 0:D_A].astype(F32)
            av = p_ref[rows, D_A:2 * D_A].astype(F32)
            u, tu = _gelu_parts(au)
            vv, tv = _gelu_parts(av)
            vhat, rstd = _layer_norm_parts(vv)
            vn = (vhat * gln_ref[...] + bln_ref[...]).astype(BF16)
            dsg = ds_ref[rows, :].astype(F32)
            dz = dsg * u
            dzb = dz.astype(BF16)
            zs, dvns = [], []
            for g in range(N_GROUPS):
                cols = slice(g * gd, (g + 1) * gd)
                zs.append(_dot(wm[g], vn[:, cols]) + bs_ref[:, g:g + 1])
                dws[g] = dws[g] + _dg(dzb[:, cols], vn[:, cols], NT)
                dvns.append(_dot(wt[g], dzb[:, cols]))
            z = jnp.concatenate(zs, axis=1)
            dvn = jnp.concatenate(dvns, axis=1)
            d_ref[rows, 0:D_A] = (dsg * z * _dgelu(au, tu)).astype(BF16)
            dvh = dvn * gln_ref[...]
            dvv = rstd * (dvh - jnp.mean(dvh, axis=-1, keepdims=True)
                          - vhat * jnp.mean(dvh * vhat, axis=-1, keepdims=True))
            d_ref[rows, D_A:2 * D_A] = (dvv * _dgelu(av, tv)).astype(BF16)
            dzs = dzs + dz
            dgl = dgl + jnp.sum(dvn * vhat, axis=0, keepdims=True)
            dbl = dbl + jnp.sum(dvn, axis=0, keepdims=True)

        @pl.when(i == 0)
        def _():
            for g in range(N_GROUPS):
                dws_ref[g] = jnp.where(causal, dws[g], 0.0)
            dz_ref[...] = dzs
            dgl_ref[...] = dgl
            dbl_ref[...] = dbl

        @pl.when(i != 0)
        def _():
            for g in range(N_GROUPS):
                dws_ref[g] += jnp.where(causal, dws[g], 0.0)
            dz_ref[...] += dzs
            dgl_ref[...] += dgl
            dbl_ref[...] += dbl

    full = lambda a: pl.BlockSpec(a.shape, lambda i: (0,) * a.ndim)
    acc = lambda s: pl.BlockSpec(s, lambda i: (0,) * len(s))
    return _call(
        body, name=name, grid=(T // tm,),
        in_specs=[pl.BlockSpec((tm, 2 * D_A), lambda i: (i, 0)), pl.BlockSpec((tm, D_A), lambda i: (i, 0)),
                  full(gln), full(bln), full(ws), full(bs_t)],
        out_specs=[pl.BlockSpec((tm, 2 * D_A), lambda i: (i, 0)), acc((N_GROUPS, CHUNK, CHUNK)), acc((CHUNK, D_A)),
                   acc((1, D_A)), acc((1, D_A))],
        out_shape=[jax.ShapeDtypeStruct((T, 2 * D_A), BF16), jax.ShapeDtypeStruct((N_GROUPS, CHUNK, CHUNK), F32),
                   jax.ShapeDtypeStruct((CHUNK, D_A), F32), jax.ShapeDtypeStruct((1, D_A), F32),
                   jax.ShapeDtypeStruct((1, D_A), F32)],
        args=[puv, dsgu, gln, bln, ws, bs_t], comms=comms)


def attn_bwd(pqkv, datt, gq_t, gk_t, sinks, *, seq, name, comms=()):
    T = pqkv.shape[0]
    nb = seq // CHUNK

    def body(p_ref, do_ref, gq_ref, gk_ref, sink_ref, d_ref, dgq_ref, dgk_ref, dsk_ref,
             k0, k1, v0, v1, dk0, dk1, dv0, dv1, dgq_s, dsk_s, bias_ref, s_ref, dp_ref, pr_ref, ds_ref):
        b = pl.program_id(0)
        krep, vrep, dkacc, dvacc = [k0, k1], [v0, v1], [dk0, dk1], [dv0, dv1]
        _fill_keys(p_ref, gk_ref, krep, vrep, seq)
        _fill_mask_bias(bias_ref)
        for h in range(N_KV):
            dkacc[h][...] = jnp.zeros_like(dkacc[h])
            dvacc[h][...] = jnp.zeros_like(dvacc[h])
        dgq_s[...] = jnp.zeros_like(dgq_s)
        dsk_s[...] = jnp.zeros_like(dsk_s)
        mean_q = _head_mean_matrix(D_B)
        lane = _iota((1, 128), 1)

        def block(i, carry):
            r0 = pl.multiple_of(i * CHUNK, CHUNK)
            first = jnp.minimum(i, 1)
            nq, rq = _head_rms(p_ref[pl.ds(r0, CHUNK), 0:D_B].astype(F32), mean_q)
            qn = nq * (gq_ref[...] * ATT_SCALE)
            dqn_parts = []
            for h in range(N_KV):
                cols = slice(h * KV_W, (h + 1) * KV_W)
                qs = _stack_heads(qn[:, cols])
                kk = krep[h][pl.ds(r0, 2 * CHUNK), :]
                dos = _stack_heads(do_ref[pl.ds(r0, CHUNK), cols].astype(F32))
                s_ref[...] = _dg(qs, kk, NT)
                dp_ref[...] = _dg(dos, vrep[h][pl.ds(r0, 2 * CHUNK), :], NT)

                def rows_of(c, carry2):
                    rows, probs, psink = _softmax_rows(s_ref, bias_ref, first, sink_ref, h, c)
                    dp = dp_ref[rows, :]
                    dr = jnp.sum(probs * dp, axis=-1, keepdims=True)
                    pr_ref[rows, :] = probs.astype(BF16)
                    ds_ref[rows, :] = (probs * (dp - dr)).astype(BF16)
                    head = h * Q_PER_KV + c // (CHUNK // SOFTMAX_ROWS)
                    dsk_s[...] += jnp.where(lane == head, -jnp.sum(psink * dr), 0.0)
                    return carry2

                lax.fori_loop(0, STACK // SOFTMAX_ROWS, rows_of, 0, unroll=True)
                dqn_parts.append(_unstack_heads(_dot(ds_ref[...], kk)))
                dkacc[h][pl.ds(r0, 2 * CHUNK), :] += _dg(ds_ref[...], qs, TN)
                dvacc[h][pl.ds(r0, 2 * CHUNK), :] += _dg(pr_ref[...], dos, TN)
            dqn = jnp.concatenate(dqn_parts, axis=1) * ATT_SCALE
            dn = dqn * gq_ref[...]
            d_ref[pl.ds(r0, CHUNK), 0:D_B] = (rq * (dn - nq * _dot_split(dn * nq, mean_q))).astype(BF16)
            dgq_s[...] += jnp.sum(dqn * nq, axis=0, keepdims=True)
            return carry

        lax.fori_loop(0, nb, block, 0)

        mean_k = _head_mean_matrix(KV2)
        folds = [_fold_matrix(h) for h in range(N_KV)]
        rows = min(seq, 4 * CHUNK)

        def piece(j, dgk):
            r0 = pl.multiple_of(j * rows, CHUNK)
            r1 = pl.multiple_of(r0 + CHUNK, CHUNK)
            dkn = _dot_split(dkacc[0][pl.ds(r1, rows), :], folds[0]) + _dot_split(dkacc[1][pl.ds(r1, rows), :], folds[1])
            dv = _dot_split(dvacc[0][pl.ds(r1, rows), :], folds[0]) + _dot_split(dvacc[1][pl.ds(r1, rows), :], folds[1])
            nk, rk = _head_rms(p_ref[pl.ds(r0, rows), K_OFF:K_OFF + KV2].astype(F32), mean_k)
            dn = dkn * gk_ref[...]
            d_ref[pl.ds(r0, rows), K_OFF:K_OFF + KV2] = (rk * (dn - nk * _dot_split(dn * nk, mean_k))).astype(BF16)
            d_ref[pl.ds(r0, rows), V_OFF:V_OFF + KV2] = dv.astype(BF16)
            return dgk + jnp.sum(dkn * nk, axis=0, keepdims=True)

        dgk = lax.fori_loop(0, seq // rows, piece, jnp.zeros((1, KV2), F32))

        @pl.when(b == 0)
        def _():
            dgq_ref[...] = dgq_s[...]
            dgk_ref[...] = dgk
            dsk_ref[...] = jnp.broadcast_to(dsk_s[...], dsk_ref.shape)

        @pl.when(b != 0)
        def _():
            dgq_ref[...] += dgq_s[...]
            dgk_ref[...] += dgk
            dsk_ref[...] += jnp.broadcast_to(dsk_s[...], dsk_ref.shape)

    acc = lambda s: pl.BlockSpec(s, lambda b: (0, 0))
    return _call(
        body, name=name, grid=(T // seq,),
        in_specs=[pl.BlockSpec((seq, QKV_W), lambda b: (b, 0)), pl.BlockSpec((seq, D_B), lambda b: (b, 0)),
                  pl.BlockSpec(gq_t.shape, lambda b: (0, 0)), pl.BlockSpec(gk_t.shape, lambda b: (0, 0)),
                  pl.BlockSpec(memory_space=pltpu.SMEM)],
        out_specs=[pl.BlockSpec((seq, QKV_W), lambda b: (b, 0)), acc((1, D_B)), acc((1, KV2)), acc((8, 128))],
        out_shape=[jax.ShapeDtypeStruct((T, QKV_W), BF16), jax.ShapeDtypeStruct((1, D_B), F32),
                   jax.ShapeDtypeStruct((1, KV2), F32), jax.ShapeDtypeStruct((8, 128), F32)],
        scratch_shapes=[pltpu.VMEM((seq + CHUNK, KV_W), BF16)] * 4 + [pltpu.VMEM((seq + CHUNK, KV_W), F32)] * 4
        + [pltpu.VMEM((1, D_B), F32), pltpu.VMEM((1, 128), F32), pltpu.VMEM((2, CHUNK, 2 * CHUNK), F32)]
        + [pltpu.VMEM((STACK, 2 * CHUNK), F32)] * 2 + [pltpu.VMEM((STACK, 2 * CHUNK), BF16)] * 2,
        args=[pqkv, datt, gq_t, gk_t, sinks], comms=comms)


def ada_fwd(c_all, w, b, *, name):
    nb, D = c_all.shape
    N = w.shape[1]
    tn = N // 3

    def body(c_ref, w_ref, b_ref, o_ref):
        c = c_ref[...]
        cond = (c * jax.nn.sigmoid(c)).astype(BF16)
        o_ref[...] = _dot(cond, w_ref[...].astype(BF16)) + b_ref[...]

    return _call(
        body, name=name, grid=(3,),
        in_specs=[pl.BlockSpec((nb, D), lambda j: (0, 0)), pl.BlockSpec((D, tn), lambda j: (0, j)),
                  pl.BlockSpec((1, tn), lambda j: (0, j))],
        out_specs=[pl.BlockSpec((nb, tn), lambda j: (0, j))], out_shape=[jax.ShapeDtypeStruct((nb, N), F32)],
        args=[c_all, w, b])[0]


def ada_bwd(c_all, dmods_all, dmods_mine, gain_rows, *, name, comms=()):
    nb, D = c_all.shape
    NA = dmods_all.shape[1]
    N = dmods_mine.shape[1]
    tn = N // 3

    def body(c_ref, da_ref, dm_ref, gr_ref, db_ref, dw_ref, dgn_ref):
        c = c_ref[...]
        cond = (c * jax.nn.sigmoid(c)).astype(BF16)
        dw_ref[...] = _dg(cond, dm_ref[...].astype(BF16), TN)
        db_ref[...] = jnp.sum(da_ref[...], axis=0, keepdims=True)
        total = gr_ref[0:1, :]
        for d in range(1, N_DEV):
            total = total + gr_ref[d:d + 1, :]
        dgn_ref[...] = total

    return _call(
        body, name=name, grid=(3,),
        in_specs=[pl.BlockSpec((nb, D), lambda j: (0, 0)), pl.BlockSpec((nb, NA), lambda j: (0, 0)),
                  pl.BlockSpec((nb, tn), lambda j: (0, j)), pl.BlockSpec((N_DEV, D), lambda j: (0, 0))],
        out_specs=[pl.BlockSpec((1, NA), lambda j: (0, 0)), pl.BlockSpec((D, tn), lambda j: (0, j)),
                   pl.BlockSpec((1, D), lambda j: (0, 0))],
        out_shape=[jax.ShapeDtypeStruct((1, NA), F32), jax.ShapeDtypeStruct((D, N), F32),
                   jax.ShapeDtypeStruct((1, D), F32)],
        args=[c_all, dmods_all, dmods_mine, gain_rows], comms=comms)


def adamw(items, *, steps, name, comms=()):
    n = len(items)
    c1 = 1.0 / (1.0 - ADAM_B1 ** ADAM_STEP)
    c2 = 1.0 / (1.0 - ADAM_B2 ** ADAM_STEP)

    def body(*refs):
        for j in range(n):
            w_ref, g_ref, m_ref, v_ref = refs[4 * j:4 * j + 4]
            go_ref, d_ref, mo_ref, vo_ref = refs[4 * n + 4 * j:4 * n + 4 * j + 4]
            gg = g_ref[...]
            mn = ADAM_B1 * m_ref[...] + (1.0 - ADAM_B1) * gg
            vn = ADAM_B2 * v_ref[...] + (1.0 - ADAM_B2) * (gg * gg)
            go_ref[...] = gg
            mo_ref[...] = mn
            vo_ref[...] = vn
            d_ref[...] = -ADAM_LR * ((mn * c1) / (jnp.sqrt(vn * c2) + ADAM_EPS) + ADAM_WD * w_ref[...])

    in_specs, out_specs, out_shape, args = [], [], [], []
    for item in items:
        R, C = item[0].shape
        blk = pl.BlockSpec((R // steps, C), lambda i: (i, 0))
        in_specs += [blk] * 4
        out_specs += [blk] * 4
        out_shape += [jax.ShapeDtypeStruct((R, C), F32)] * 4
        args += list(item)
    res = _call(body, name=name, grid=(steps,), in_specs=in_specs, out_specs=out_specs, out_shape=out_shape, args=args,
                comms=comms)
    return [tuple(res[4 * j:4 * j + 4]) for j in range(n)]


def _place():
    return lax.axis_index("x"), lax.axis_index("y"), lax.axis_index("c")


def _flip(v, bit):
    return 1 - v if bit else v


def _other_chips(mx, my):
    return [(_flip(mx, k & 2), _flip(my, k & 1)) for k in range(1, N_QUAD)]


def _remote(src, dst, send_sem, recv_sem, peer):
    return pltpu.make_async_remote_copy(src_ref=src, dst_ref=dst, send_sem=send_sem, recv_sem=recv_sem,
                                        device_id=peer, device_id_type=MESH)


def ag8_comm(x):
    def copies(srcs, lands, ss, rs, only_sends=False):
        mx, my, mc = _place()
        me = 4 * mx + 2 * my + mc
        local = pltpu.make_async_copy(srcs[0], lands[0].at[me], ss.at[N_DEV - 1])
        sends, recvs = [], []
        for k in range(1, N_DEV):
            peer = (_flip(mx, k & 4), _flip(my, k & 2), _flip(mc, k & 1))
            sends.append(_remote(srcs[0], lands[0].at[me], ss.at[k - 1], rs.at[k - 1], peer))
            if not only_sends:
                recvs.append(_remote(srcs[0], lands[0].at[4 * peer[0] + 2 * peer[1] + peer[2]], ss.at[k - 1], rs.at[k - 1], peer))
        return local, sends, recvs

    def start(srcs, bufs, lands, ss, rs):
        local, sends, _ = copies(srcs, lands, ss, rs, only_sends=True)
        local.start()
        for cp in sends:
            cp.start()

    def finish(srcs, bufs, lands, ss, rs):
        local, sends, recvs = copies(srcs, lands, ss, rs)
        for cp in recvs:
            cp.wait_recv()
        for cp in sends:
            cp.wait_send()
        local.wait()

    return Comm(srcs=[x], land_shapes=[jax.ShapeDtypeStruct((N_DEV,) + x.shape, x.dtype)], n_sems=N_DEV,
                start=start, finish=finish)


def sum8(stacked, *, name):
    _, r, n = stacked.shape

    def body(s_ref, o_ref):
        total = s_ref[0]
        for d in range(1, N_DEV):
            total = total + s_ref[d]
        o_ref[...] = total

    return _call(body, name=name, grid=(), in_specs=[pl.BlockSpec(memory_space=pltpu.VMEM)],
                 out_specs=[pl.BlockSpec(memory_space=pltpu.VMEM)], out_shape=[jax.ShapeDtypeStruct((r, n), F32)],
                 args=[stacked])[0]


def cast_into_stacks(ws, quad, *, name, comms=()):
    steps = 4

    def body(q_ref, *refs):
        for w_ref, o_ref in zip(refs[:len(ws)], refs[len(ws):]):
            o_ref[0] = w_ref[...].astype(BF16)

    return _call(
        body, name=name, grid=(steps,), prefetch=[quad],
        in_specs=[pl.BlockSpec((w.shape[0] // steps, w.shape[1]), lambda i, q: (i, 0)) for w in ws],
        out_specs=[pl.BlockSpec((1, w.shape[0] // steps, w.shape[1]), lambda i, q: (q[0], i, 0)) for w in ws],
        out_shape=[jax.ShapeDtypeStruct((N_QUAD,) + w.shape, BF16) for w in ws], args=list(ws), comms=comms)


def gather_comm(stacks, forward=True):
    n = len(stacks)

    def copies(bufs, ss, rs, only_sends=False):
        mx, my, mc = _place()
        q = 2 * mx + my
        sibling = (mx, my, 1 - mc)
        ici_send, ici_recv, fwd_send, fwd_recv = [], [], [], []
        for p in range(n):
            hr = stacks[p].shape[1] // 2
            mine, other = pl.ds(mc * hr, hr), pl.ds((1 - mc) * hr, hr)
            for k, chip in enumerate(_other_chips(mx, my)):
                qk = 2 * chip[0] + chip[1]
                peer = (chip[0], chip[1], mc)
                own, landed, theirs = bufs[p].at[q, mine, :], bufs[p].at[qk, mine, :], bufs[p].at[qk, other, :]
                ici_send.append(_remote(own, own, ss.at[6 * p + k], rs.at[6 * p + k], peer))
                if only_sends:
                    continue
                ici_recv.append(_remote(own, landed, ss.at[6 * p + k], rs.at[6 * p + k], peer))
                if forward:
                    fwd_send.append(_remote(landed, landed, ss.at[6 * p + 3 + k], rs.at[6 * p + 3 + k], sibling))
                    fwd_recv.append(_remote(theirs, theirs, ss.at[6 * p + 3 + k], rs.at[6 * p + 3 + k], sibling))
        return ici_send, ici_recv, fwd_send, fwd_recv

    def start(srcs, bufs, lands, ss, rs):
        for cp in copies(bufs, ss, rs, only_sends=True)[0]:
            cp.start()

    def finish(srcs, bufs, lands, ss, rs):
        ici_send, ici_recv, fwd_send, fwd_recv = copies(bufs, ss, rs)
        for j, arrived in enumerate(ici_recv):
            arrived.wait_recv()
            if forward:
                fwd_send[j].start()
        for cp in fwd_recv:
            cp.wait_recv()
        for cp in ici_send + fwd_send:
            cp.wait_send()

    return Comm(bufs=stacks, n_sems=6 * n, start=start, finish=finish)


def forward_comm(stacks):
    n = len(stacks)

    def copies(bufs, ss, rs, only_sends=False):
        mx, my, mc = _place()
        sibling = (mx, my, 1 - mc)
        send, recv = [], []
        for p in range(n):
            hr = stacks[p].shape[1] // 2
            for k, chip in enumerate(_other_chips(mx, my)):
                qk = 2 * chip[0] + chip[1]
                landed, theirs = bufs[p].at[qk, pl.ds(mc * hr, hr), :], bufs[p].at[qk, pl.ds((1 - mc) * hr, hr), :]
                send.append(_remote(landed, landed, ss.at[3 * p + k], rs.at[3 * p + k], sibling))
                if not only_sends:
                    recv.append(_remote(theirs, theirs, ss.at[3 * p + k], rs.at[3 * p + k], sibling))
        return send, recv

    def start(srcs, bufs, lands, ss, rs):
        for cp in copies(bufs, ss, rs, only_sends=True)[0]:
            cp.start()

    def finish(srcs, bufs, lands, ss, rs):
        send, recv = copies(bufs, ss, rs)
        for cp in recv:
            cp.wait_recv()
        for cp in send:
            cp.wait_send()

    return Comm(bufs=stacks, n_sems=3 * n, start=start, finish=finish)


def rs_pair_comm(gs):
    n = len(gs)

    def copies(srcs, lands, ss, rs):
        mx, my, mc = _place()
        out = []
        for p in range(n):
            hr = gs[p].shape[1] // 2
            out.append(_remote(srcs[p].at[:, pl.ds((1 - mc) * hr, hr), :], lands[p], ss.at[p], rs.at[p], (mx, my, 1 - mc)))
        return out

    def start(srcs, bufs, lands, ss, rs):
        for cp in copies(srcs, lands, ss, rs):
            cp.start()

    def finish(srcs, bufs, lands, ss, rs):
        for cp in copies(srcs, lands, ss, rs):
            cp.wait()

    return Comm(srcs=gs, land_shapes=[jax.ShapeDtypeStruct((g.shape[0], g.shape[1] // 2, g.shape[2]), g.dtype) for g in gs],
                n_sems=n, start=start, finish=finish)


def rs_chip_comm(ss_):
    n = len(ss_)

    def copies(srcs, lands, ss, rs):
        mx, my, mc = _place()
        out = []
        for p in range(n):
            for k, chip in enumerate(_other_chips(mx, my)):
                out.append(_remote(srcs[p].at[2 * chip[0] + chip[1]], lands[p].at[k], ss.at[3 * p + k], rs.at[3 * p + k],
                                   (chip[0], chip[1], mc)))
        return out

    def start(srcs, bufs, lands, ss, rs):
        for cp in copies(srcs, lands, ss, rs):
            cp.start()

    def finish(srcs, bufs, lands, ss, rs):
        for cp in copies(srcs, lands, ss, rs):
            cp.wait()

    return Comm(srcs=ss_, land_shapes=[jax.ShapeDtypeStruct((N_QUAD - 1,) + s.shape[1:], s.dtype) for s in ss_],
                n_sems=3 * n, start=start, finish=finish)


def rs_share_comm(fulls):
    n = len(fulls)

    def copies(bufs, ss, rs, only_sends=False):
        mx, my, mc = _place()
        send, recv = [], []
        for p in range(n):
            hr = fulls[p].shape[0] // 2
            mine, other = bufs[p].at[pl.ds(mc * hr, hr), :], bufs[p].at[pl.ds((1 - mc) * hr, hr), :]
            send.append(_remote(mine, mine, ss.at[p], rs.at[p], (mx, my, 1 - mc)))
            if not only_sends:
                recv.append(_remote(other, other, ss.at[p], rs.at[p], (mx, my, 1 - mc)))
        return send, recv

    def start(srcs, bufs, lands, ss, rs):
        for cp in copies(bufs, ss, rs, only_sends=True)[0]:
            cp.start()

    def finish(srcs, bufs, lands, ss, rs):
        send, recv = copies(bufs, ss, rs)
        for cp in recv:
            cp.wait_recv()
        for cp in send:
            cp.wait_send()

    return Comm(bufs=fulls, n_sems=n, start=start, finish=finish)


def pair_sums(gs, recvs, mc, *, name):
    n = len(gs)

    def body(s_ref, *refs):
        for p in range(n):
            g_ref, r_ref, o_ref = refs[2 * p], refs[2 * p + 1], refs[2 * n + p]
            o_ref[...] = (g_ref[...].astype(F32) + r_ref[...].astype(F32)).astype(BF16)

    in_specs, out_specs, out_shape, args = [], [], [], []
    for g, r in zip(gs, recvs):
        nq, hr, C = r.shape
        in_specs += [pl.BlockSpec((1, hr, C), lambda q, s: (q, s[0], 0)), pl.BlockSpec((1, hr, C), lambda q, s: (q, 0, 0))]
        out_specs.append(pl.BlockSpec((1, hr, C), lambda q, s: (q, 0, 0)))
        out_shape.append(jax.ShapeDtypeStruct((nq, hr, C), BF16))
        args += [g, r]
    return _call(body, name=name, grid=(N_QUAD,), in_specs=in_specs, out_specs=out_specs, out_shape=out_shape,
                 args=args, prefetch=[mc])


def chip_sums(ss, recvs, quad_mc, *, name):
    n = len(ss)
    steps = 2

    def body(q_ref, *refs):
        for p in range(n):
            s_ref, r_ref, o_ref = refs[2 * p], refs[2 * p + 1], refs[2 * n + p]
            total = s_ref[0].astype(F32)
            for k in range(N_QUAD - 1):
                total = total + r_ref[k].astype(F32)
            o_ref[...] = total

    in_specs, out_specs, out_shape, args = [], [], [], []
    for s, r in zip(ss, recvs):
        _, hr, C = s.shape
        rb = hr // steps
        in_specs += [pl.BlockSpec((1, rb, C), lambda i, q: (q[0], i, 0)),
                     pl.BlockSpec((N_QUAD - 1, rb, C), lambda i, q: (0, i, 0))]
        out_specs.append(pl.BlockSpec((rb, C), lambda i, q: (q[1] * steps + i, 0)))
        out_shape.append(jax.ShapeDtypeStruct((2 * hr, C), F32))
        args += [s, r]
    return _call(body, name=name, grid=(steps,), in_specs=in_specs, out_specs=out_specs, out_shape=out_shape,
                 args=args, prefetch=[quad_mc])


def _stack_cols(w_full):
    K, N = w_full.shape
    return w_full.reshape(K, N_QUAD, N // N_QUAD).transpose(1, 0, 2)


def _unstack_cols(w4):
    nq, K, n = w4.shape
    return w4.transpose(1, 0, 2).reshape(K, nq * n)


def kernel(x, c, w_ada, b_ada, g_norm1, ffn1_w_gate, ffn1_w_up, ffn1_w_down, g_norm2, w_in, g_sgu_ln, b_sgu_ln, w_spatial, b_spatial, g_q, g_k, attn_sinks, w_branch_a, w_branch_b, w_out, g_norm3, ffn2_w_gate, ffn2_w_up, ffn2_w_down, loss_target, m_w_ada, m_b_ada, m_g_norm1, m_ffn1_w_gate, m_ffn1_w_up, m_ffn1_w_down, m_g_norm2, m_w_in, m_g_sgu_ln, m_b_sgu_ln, m_w_spatial, m_b_spatial, m_g_q, m_g_k, m_attn_sinks, m_w_branch_a, m_w_branch_b, m_w_out, m_g_norm3, m_ffn2_w_gate, m_ffn2_w_up, m_ffn2_w_down, v_w_ada, v_b_ada, v_g_norm1, v_ffn1_w_gate, v_ffn1_w_up, v_ffn1_w_down, v_g_norm2, v_w_in, v_g_sgu_ln, v_b_sgu_ln, v_w_spatial, v_b_spatial, v_g_q, v_g_k, v_attn_sinks, v_w_branch_a, v_w_branch_b, v_w_out, v_g_norm3, v_ffn2_w_gate, v_ffn2_w_up, v_ffn2_w_down):
    B, S, D = x.shape
    T = B * S
    n_mod = b_ada.shape[1] // D
    mx, my, mc = _place()
    quad = 2 * mx + my
    mc_arr = jnp.reshape(mc, (1,)).astype(jnp.int32)
    quad_arr = jnp.reshape(quad, (1,)).astype(jnp.int32)
    quad_mc = jnp.stack([quad, mc]).astype(jnp.int32)
    tm = min(512, S)
    tm_big = min(1024, S)
    tt_half = min(2048, T)
    cpb = min(4, S // CHUNK)

    xt = x.reshape(T, D)
    tgt = loss_target.reshape(T, D)

    tr = lambda a: jnp.swapaxes(a, 1, 2)
    gather_cond = ag8_comm(c)
    stack_wg1, stack_wu1 = cast_into_stacks([tr(ffn1_w_gate)[0], tr(ffn1_w_up)[0]], quad_arr, name="stack_gu1",
                                            comms=[gather_cond])
    c_all = gather_cond.lands[0].reshape(N_DEV * B, D)
    n_ada = w_ada.shape[2]
    b_mine = lax.dynamic_slice(b_ada, (0, quad * n_ada), (1, n_ada))
    mods_part = ada_fwd(c_all, w_ada[0], b_mine, name="ada_fwd")
    gather_mods = ag8_comm(mods_part)
    gather_wg1, gather_wu1 = gather_comm([stack_wg1], forward=False), gather_comm([stack_wu1])
    later = [ffn1_w_down, tr(w_in), w_branch_a, w_branch_b, w_out, tr(ffn2_w_gate), tr(ffn2_w_up), ffn2_w_down]
    stacks = cast_into_stacks([w[0] for w in later], quad_arr, name="stack_rest", comms=[gather_wg1, gather_mods])
    forward_wg1 = forward_comm(gather_wg1.bufs_out)
    (mods_parts,) = gather_mods.lands
    gather_wd1, gather_win = gather_comm([stacks[0]]), gather_comm([stacks[1]])
    gather_abo = gather_comm(stacks[2:5], forward=False)
    gather_wg3, gather_wu3 = gather_comm([stacks[5]], forward=False), gather_comm([stacks[6]], forward=False)
    gather_wd3 = gather_comm([stacks[7]])
    mods = jnp.concatenate([mods_parts[2 * j] for j in range(N_QUAD)], axis=1)
    me = 4 * mx + 2 * my + mc
    mods = lax.dynamic_slice(mods, (me * B, 0), (B, n_mod * D))
    mods = mods.reshape(B, n_mod, 1, D)
    sh1, sc1, ga1, sh2, sc2, ga2, sh3, sc3, ga3 = [(mods, j) for j in range(n_mod)]
    bs_t = b_spatial[0].T
    ws = w_spatial[0]

    xn1 = norm_mod_fwd(xt, g_norm1, sc1, sh1, seq=S, tm=tm, name="norm1", comms=[gather_wu1, forward_wg1])
    (wu1,), (wg1,) = gather_wu1.bufs_out, forward_wg1.bufs_out
    g1, u1, a1 = ffn_up(xn1, wg1, wu1, tm=tm_big, name="ffn1_up", comms=[gather_wd1, gather_abo])
    (wd1,) = gather_wd1.bufs_out
    forward_abo = forward_comm(gather_abo.bufs_out)
    h1, f1, xn2 = ffn_down(a1, wd1, xt, ga1, norm=(g_norm2, sc2, sh2), seq=S, tm=tm, name="ffn1_down",
                           comms=[gather_win, forward_abo])
    (win4,), (wa4, wb4, wo4) = gather_win.bufs_out, forward_abo.bufs_out
    wa, wb = _unstack_cols(wa4), _unstack_cols(wb4)
    wo = wo4.reshape(D, D)
    win = win4.reshape(-1, D)
    groups = [(0, 2 * D_A), (2 * D_A, 2 * D_A + QKV_W), (2 * D_A + QKV_W, win.shape[0])]
    puv, pqkv, pgt = proj_fwd(xn2, win, groups, tm=tm, name="proj", comms=[gather_wg3])
    forward_wg3 = forward_comm(gather_wg3.bufs_out)
    sgu = sgu_fwd(puv, g_sgu_ln, b_sgu_ln, ws, bs_t, cpb=cpb, name="sgu_fwd")
    gq_t, gk_t = jnp.tile(g_q, (1, N_Q)), jnp.tile(g_k, (1, N_KV))
    att = attn_fwd(pqkv, gq_t, gk_t, attn_sinks, seq=S, name="attn_fwd", comms=[gather_wu3, forward_wg3])
    (wg3,) = forward_wg3.bufs_out
    forward_wu3 = forward_comm(gather_wu3.bufs_out)
    ya, yb, merged, mm, h2, xn3 = mixer_out_fwd(sgu, att, pgt, h1, ga2, wa, wb, wo, (g_norm3, sc3, sh3), seq=S,
                                                tm=tm, name="mixer_out", comms=[gather_wd3, forward_wu3])
    (wd3,), (wu3,) = gather_wd3.bufs_out, forward_wu3.bufs_out
    g3, u3, a3 = ffn_up(xn3, wg3, wu3, tm=tm_big, name="ffn2_up")
    dy, f3, lparts = ffn_down(a3, wd3, h2, ga3, target=tgt, seq=S, tm=tm, name="ffn2_down_loss")
    loss_part = jnp.sum(lparts[:, 0, 0])

    def sums_of(pair, tag):
        return pair_sums(pair.srcs, pair.lands, mc_arr, name=f"pair_sum_{tag}")

    def halves_of(*chips_and_tag):
        *chips, tag = chips_and_tag
        return chip_sums([s for ch in chips for s in ch.srcs], [r for ch in chips for r in ch.lands], quad_mc,
                         name=f"chip_sum_{tag}")

    dg3, du3, df3, dga3 = ffn_bwd_act(dy, f3, ga3, wd3, g3, u3, seq=S, tm=tm, name="ffn2_act_bwd")
    (dwd3,) = mm_tn([(a3, df3)], tt=T, name="ffn2_dwd")
    pair_wd3 = rs_pair_comm([dwd3])
    dwg3, dwu3 = mm_tn([(dg3, xn3), (du3, xn3)], tt=tt_half, name="ffn2_dwgu", comms=[pair_wd3])
    chip_wd3 = rs_chip_comm(sums_of(pair_wd3, "wd3"))
    pair_gu3 = rs_pair_comm([dwg3, dwu3])
    dh2, dsh3, dsc3, dgn3 = dx_norm_bwd([dg3, du3], [wg3, wu3], h2, dy, g_norm3, sc3, seq=S, tm=tm, name="ffn2_dx",
                                        comms=[chip_wd3, pair_gu3])
    sum_wg3, sum_wu3 = sums_of(pair_gu3, "gu3")
    chip_wg3, chip_wu3 = rs_chip_comm([sum_wg3]), rs_chip_comm([sum_wu3])

    dgt, dya, dyb, dsgu, datt, dm, dga2 = mixer_out_bwd(dh2, mm, ga2, ya, yb, pgt, wa, wb, wo, seq=S, tm=tm,
                                                        name="mixer_out_bwd", comms=[chip_wg3])
    dwo, dwa, dwb = mm_tn([(merged, dm), (sgu, dya), (att, dyb)], tt=tm_big, name="mixer_dw")
    pair_abo = rs_pair_comm([_stack_cols(dwa), _stack_cols(dwb), dwo.reshape(N_QUAD, D // N_QUAD, D)])
    duv, dws, dzs, dgln, dbln = sgu_bwd(puv, dsgu, g_sgu_ln, b_sgu_ln, ws, bs_t, cpb=cpb, name="sgu_bwd")
    dqkv, dgq_h, dgk_h, dsk = attn_bwd(pqkv, datt, gq_t, gk_t, attn_sinks, seq=S, name="attn_bwd",
                                       comms=[chip_wu3, pair_abo])
    chip_abo = rs_chip_comm(sums_of(pair_abo, "abo"))
    dgq = dgq_h.reshape(N_Q, HEAD_DIM).sum(axis=0, keepdims=True)
    dgk = dgk_h.reshape(N_KV, HEAD_DIM).sum(axis=0, keepdims=True)
    share3 = rs_share_comm(halves_of(chip_wg3, chip_wu3, chip_wd3, "ffn2"))
    dwin_uv, dwin_qkv = mm_tn([(duv, xn2), (dqkv, xn2)], tt=tt_half, name="mixer_dwin_a", comms=[share3, chip_abo])
    (dwin_gt,) = mm_tn([(dgt, xn2)], tt=tt_half, name="mixer_dwin_b")
    dwin_parts = [dwin_uv, dwin_qkv, dwin_gt]
    r_wg3, r_wu3, r_wd3 = share3.bufs_out
    pair_win = rs_pair_comm([jnp.concatenate(dwin_parts, axis=0).reshape(win4.shape)])
    dh1, dsh2, dsc2, dgn2 = dx_norm_bwd([duv, dqkv, dgt], [(win, r0, r1) for r0, r1 in groups], h1, dh2, g_norm2, sc2, seq=S,
                                        tm=tm, name="mixer_dx", comms=[pair_win])
    chip_win = rs_chip_comm(sums_of(pair_win, "win"))

    dg1, du1, df1, dga1 = ffn_bwd_act(dh1, f1, ga1, wd1, g1, u1, seq=S, tm=tm, name="ffn1_act_bwd", comms=[chip_win])
    share_mix = rs_share_comm(halves_of(chip_win, chip_abo, "mix"))

    db_s = dzs.reshape(CHUNK, N_GROUPS, D_A // N_GROUPS).sum(axis=2).T.reshape(1, N_GROUPS * CHUNK)
    tail = jnp.concatenate([db_s, dgq, dgk, dsk[0:1, 0:N_Q], loss_part.reshape(1, 1)], axis=1)
    tail = jnp.pad(tail, ((0, 0), (0, (-tail.shape[1]) % D)))
    lnrow = jnp.concatenate([dgln, dbln], axis=1)
    lnrow = jnp.pad(lnrow, ((0, 0), (0, (-lnrow.shape[1]) % D)))
    packed = jnp.concatenate([dgn2, dgn3, lnrow.reshape(-1, D), tail.reshape(-1, D), dws.reshape(-1, D)], axis=0)
    n_rows = packed.shape[0]
    packed = jnp.pad(packed, ((0, (-n_rows) % 8), (0, 0)))
    gather_small = ag8_comm(packed)

    dwg1, dwu1 = mm_tn([(dg1, xn1), (du1, xn1)], tt=tt_half, name="ffn1_dwgu", comms=[share_mix, gather_small])
    r_win, r_wa, r_wb, r_wo = share_mix.bufs_out
    small = sum8(gather_small.lands[0], name="sum_small")
    pair_gu1 = rs_pair_comm([dwg1, dwu1])
    (dwd1,) = mm_tn([(a1, df1)], tt=T, name="ffn1_dwd", comms=[pair_gu1])
    chip_gu1 = rs_chip_comm(sums_of(pair_gu1, "gu1"))
    pair_wd1 = rs_pair_comm([dwd1])
    dx, dsh1, dsc1, dgn1 = dx_norm_bwd([dg1, du1], [wg1, wu1], xt, dh1, g_norm1, sc1, seq=S, tm=tm, name="ffn1_dx",
                                       comms=[chip_gu1, pair_wd1])
    chip_wd1 = rs_chip_comm(sums_of(pair_wd1, "wd1"))
    share_gu1 = rs_share_comm(halves_of(chip_gu1, "gu1"))

    names = ["w_ada", "b_ada", "g_norm1", "ffn1_w_gate", "ffn1_w_up", "ffn1_w_down", "g_norm2", "w_in", "g_sgu_ln",
             "b_sgu_ln", "w_spatial", "b_spatial", "g_q", "g_k", "attn_sinks", "w_branch_a", "w_branch_b", "w_out",
             "g_norm3", "ffn2_w_gate", "ffn2_w_up", "ffn2_w_down"]
    weights = dict(zip(names, [w_ada, b_ada, g_norm1, ffn1_w_gate, ffn1_w_up, ffn1_w_down, g_norm2, w_in, g_sgu_ln,
                               b_sgu_ln, w_spatial, b_spatial, g_q, g_k, attn_sinks, w_branch_a, w_branch_b, w_out,
                               g_norm3, ffn2_w_gate, ffn2_w_up, ffn2_w_down]))
    m_in = dict(zip(names, [m_w_ada, m_b_ada, m_g_norm1, m_ffn1_w_gate, m_ffn1_w_up, m_ffn1_w_down, m_g_norm2, m_w_in,
                            m_g_sgu_ln, m_b_sgu_ln, m_w_spatial, m_b_spatial, m_g_q, m_g_k, m_attn_sinks, m_w_branch_a,
                            m_w_branch_b, m_w_out, m_g_norm3, m_ffn2_w_gate, m_ffn2_w_up, m_ffn2_w_down]))
    v_in = dict(zip(names, [v_w_ada, v_b_ada, v_g_norm1, v_ffn1_w_gate, v_ffn1_w_up, v_ffn1_w_down, v_g_norm2, v_w_in,
                            v_g_sgu_ln, v_b_sgu_ln, v_w_spatial, v_b_spatial, v_g_q, v_g_k, v_attn_sinks, v_w_branch_a,
                            v_w_branch_b, v_w_out, v_g_norm3, v_ffn2_w_gate, v_ffn2_w_up, v_ffn2_w_down]))
    transposed = ("ffn1_w_gate", "ffn1_w_up", "w_in", "ffn2_w_gate", "ffn2_w_up")
    grads, delta, new_m, new_v = {}, {}, {}, {}

    def update(group, steps, name, comms=()):
        form = lambda nm, a: (tr(a) if nm in transposed else a)[0].reshape(group[nm].shape)
        res = adamw([(form(nm, weights[nm]), g, form(nm, m_in[nm]), form(nm, v_in[nm])) for nm, g in group.items()],
                    steps=steps, name=name, comms=comms)
        back = lambda nm, a: tr(a[None]) if nm in transposed else a.reshape(weights[nm].shape)
        for nm, (g, d, mn, vn) in zip(group, res):
            grads[nm], delta[nm], new_m[nm], new_v[nm] = back(nm, g), back(nm, d), back(nm, mn), back(nm, vn)

    dmods = jnp.concatenate([dsh1, dsc1, dga1, dsh2, dsc2, dga2, dsh3, dsc3, dga3], axis=1)
    dmods = jnp.concatenate([dmods, jnp.pad(dgn1, ((0, 0), (0, (n_mod - 1) * D)))], axis=0)
    gather_dmods = ag8_comm(dmods)
    run_comms([chip_wd1, share_gu1, gather_dmods], name="rs_tail")
    update({"ffn2_w_gate": r_wg3, "ffn2_w_up": r_wu3, "ffn2_w_down": r_wd3, "w_out": r_wo, "w_branch_a": r_wa,
            "w_branch_b": r_wb}, 8, "adamw_early")
    r_wg1, r_wu1 = share_gu1.bufs_out
    (gathered,) = gather_dmods.lands
    dmods_all = gathered[:, 0:B].reshape(N_DEV * B, n_mod * D)
    dmods_mine = lax.dynamic_slice(dmods_all, (0, quad * n_ada), (N_DEV * B, n_ada))
    share_wd1 = rs_share_comm(halves_of(chip_wd1, "wd1"))
    db_ada, dw_ada, g_gn1 = ada_bwd(c_all, dmods_all, dmods_mine, gathered[:, B, 0:D], name="ada_bwd", comms=[share_wd1])
    (r_wd1,) = share_wd1.bufs_out

    ln_rows = lnrow.size // D
    tail_rows = tail.size // D
    r = 2
    g_gn2, g_gn3 = small[0:1], small[1:2]
    ln_flat = small[r:r + ln_rows].reshape(1, -1)
    r += ln_rows
    tail_flat = small[r:r + tail_rows].reshape(1, -1)
    r += tail_rows
    g_ws = small[r:r + dws.size // D].reshape(w_spatial.shape)
    g_gln, g_bln = ln_flat[:, 0:D_A], ln_flat[:, D_A:2 * D_A]
    o = N_GROUPS * CHUNK
    g_bs = tail_flat[:, 0:o].reshape(b_spatial.shape)
    g_gq, g_gk, g_sk = tail_flat[:, o:o + HEAD_DIM], tail_flat[:, o + HEAD_DIM:o + 2 * HEAD_DIM], tail_flat[:, o + 2 * HEAD_DIM:o + 2 * HEAD_DIM + N_Q]
    loss = tail_flat[0, o + 2 * HEAD_DIM + N_Q]

    update({"w_ada": dw_ada}, 16, "adamw_w_ada")
    update({"ffn1_w_gate": r_wg1, "ffn1_w_up": r_wu1, "ffn1_w_down": r_wd1, "w_in": r_win}, 8, "adamw_late")

    update({"b_ada": db_ada, "g_norm1": g_gn1, "g_norm2": g_gn2, "g_norm3": g_gn3, "g_sgu_ln": g_gln,
            "b_sgu_ln": g_bln, "w_spatial": g_ws.reshape(-1, CHUNK), "b_spatial": g_bs.reshape(N_GROUPS, CHUNK),
            "g_q": g_gq, "g_k": g_gk, "attn_sinks": g_sk}, 1, "adamw_small")

    return (loss, dx.reshape(B, S, D), *[grads[nm] for nm in names], *[delta[nm] for nm in names],
            *[new_m[nm] for nm in names], *[new_v[nm] for nm in names])
```

```python
import functools
import math
import operator

import jax
import jax.numpy as jnp
from jax import lax
from jax.experimental import pallas as pl
from jax.experimental.pallas import tpu as pltpu

F32 = jnp.float32
BF16 = jnp.bfloat16
EPS = 1e-6
NEG = -1e30
N_DEV = 8
N_QUAD = 4
D_A = 512
N_GROUPS = 4
CHUNK = 128
HEAD_DIM = 64
N_KV = 2
Q_PER_KV = 4
N_Q = N_KV * Q_PER_KV
D_B = N_Q * HEAD_DIM
QKV_W = D_B + 2 * N_KV * HEAD_DIM
K_OFF = D_B
V_OFF = D_B + N_KV * HEAD_DIM
ATT_SCALE = HEAD_DIM ** -0.5
GELU_K = math.sqrt(2.0 / math.pi)
GELU_C = 0.044715
ADAM_LR, ADAM_B1, ADAM_B2, ADAM_EPS, ADAM_WD, ADAM_STEP = 0.001, 0.9, 0.999, 1e-08, 0.01, 10
VMEM_LIMIT_BYTES = 56 * 1024 * 1024
MESH = pl.DeviceIdType.MESH
NT = (((1,), (1,)), ((), ()))
TN = (((0,), (0,)), ((), ()))
ANY = pl.BlockSpec(memory_space=pl.ANY)


def _dot(a, b):
    return jnp.dot(a, b, preferred_element_type=F32)


def _dg(a, b, dims):
    return lax.dot_general(a, b, dims, preferred_element_type=F32)


def _gelu_parts(x):
    t = jnp.tanh(GELU_K * (x + GELU_C * x * x * x))
    return 0.5 * x * (1.0 + t), t


def _dgelu(x, t):
    return 0.5 * (1.0 + t) + 0.5 * x * (1.0 - t * t) * (GELU_K * (1.0 + 3.0 * GELU_C * x * x))


def _mod_spec(mod, tps):
    mods, j = mod
    return pl.BlockSpec((1, None, 1, mods.shape[3]), lambda i: (i // tps, j, 0, 0))


def _resident(a):
    return pl.BlockSpec(a.shape, lambda *_: (0,) * a.ndim, pipeline_mode=pl.Buffered(1))


class Comm:
    def __init__(self, *, srcs=(), bufs=(), land_shapes=(), n_sems, start, finish):
        self.srcs, self.bufs, self.land_shapes = list(srcs), list(bufs), list(land_shapes)
        self.n_sems, self.start, self.finish = n_sems, start, finish
        self.bufs_out, self.lands = None, None


def _call(body, *, name, grid, in_specs, out_specs, out_shape, args, scratch_shapes=(), comms=(), prefetch=()):
    prefetch = list(prefetch)
    in_specs, out_specs, out_shape = list(in_specs), list(out_specs), list(out_shape)
    args, scratch = list(args), list(scratch_shapes)
    n_in, n_out, n_scr = len(args), len(out_shape), len(scratch)
    aliases, layout = {}, []
    for cm in comms:
        i0, o0, s0 = len(args), len(out_shape), len(scratch)
        args += cm.srcs + cm.bufs
        in_specs += [ANY] * (len(cm.srcs) + len(cm.bufs))
        out_shape += [jax.ShapeDtypeStruct(b.shape, b.dtype) for b in cm.bufs] + cm.land_shapes
        out_specs += [ANY] * (len(cm.bufs) + len(cm.land_shapes))
        for j in range(len(cm.bufs)):
            aliases[len(prefetch) + i0 + len(cm.srcs) + j] = o0 + j
        scratch += [pltpu.SemaphoreType.DMA((cm.n_sems,)), pltpu.SemaphoreType.DMA((cm.n_sems,))]
        layout.append((i0, o0, s0))
    n_in_all, n_out_all = len(args), len(out_shape)

    def wrapped(*refs):
        scalars, refs = refs[:len(prefetch)], refs[len(prefetch):]
        ins, outs, scr = refs[:n_in_all], refs[n_in_all:n_in_all + n_out_all], refs[n_in_all + n_out_all:]
        parts = []
        for cm, (i0, o0, s0) in zip(comms, layout):
            nb = len(cm.bufs)
            parts.append((ins[i0:i0 + len(cm.srcs)], outs[o0:o0 + nb], outs[o0 + nb:o0 + nb + len(cm.land_shapes)],
                          scr[s0], scr[s0 + 1]))
        if comms and grid:
            ids = [pl.program_id(d) for d in range(len(grid))]
            first = functools.reduce(operator.and_, [i == 0 for i in ids])
            last = functools.reduce(operator.and_, [i == g - 1 for i, g in zip(ids, grid)])

            @pl.when(first)
            def _():
                for cm, p in zip(comms, parts):
                    cm.start(*p)
        elif comms:
            for cm, p in zip(comms, parts):
                cm.start(*p)
        if body is not None:
            body(*scalars, *ins[:n_in], *outs[:n_out], *scr[:n_scr])
        if comms and grid:
            @pl.when(last)
            def _():
                for cm, p in zip(comms, parts):
                    cm.finish(*p)
        elif comms:
            for cm, p in zip(comms, parts):
                cm.finish(*p)

    if prefetch:
        kwargs = dict(grid_spec=pltpu.PrefetchScalarGridSpec(
            num_scalar_prefetch=len(prefetch), grid=grid, in_specs=in_specs, out_specs=out_specs, scratch_shapes=scratch))
    else:
        kwargs = dict(in_specs=in_specs, out_specs=out_specs, scratch_shapes=scratch, **(dict(grid=grid) if grid else {}))
    sem = ("arbitrary",) * len(grid)
    res = pl.pallas_call(
        wrapped, name=name, out_shape=out_shape, input_output_aliases=aliases,
        compiler_params=pltpu.CompilerParams(dimension_semantics=sem, vmem_limit_bytes=VMEM_LIMIT_BYTES), **kwargs,
    )(*prefetch, *args)
    for cm, (i0, o0, s0) in zip(comms, layout):
        nb = len(cm.bufs)
        cm.bufs_out = list(res[o0:o0 + nb])
        cm.lands = list(res[o0 + nb:o0 + nb + len(cm.land_shapes)])
    return list(res[:n_out])


def run_comms(comms, *, name):
    _call(None, name=name, grid=(), in_specs=[], out_specs=[], out_shape=[], args=[], comms=comms)


def norm_mod_fwd(h, g, sc, sh, *, seq, tm, name, comms=()):
    T, D = h.shape
    B, tps = T // seq, seq // tm

    def body(h_ref, g_ref, sc_ref, sh_ref, o_ref):
        x = h_ref[...]
        r = lax.rsqrt(jnp.mean(x * x, axis=-1, keepdims=True) + EPS)
        y = x * r * g_ref[...]
        o_ref[...] = (y * (1.0 + sc_ref[0]) + sh_ref[0]).astype(o_ref.dtype)

    return _call(
        body, name=name, grid=(T // tm,),
        in_specs=[pl.BlockSpec((tm, D), lambda i: (i, 0)), pl.BlockSpec((1, D), lambda i: (0, 0)),
                  _mod_spec(sc, tps), _mod_spec(sh, tps)],
        out_specs=[pl.BlockSpec((tm, D), lambda i: (i, 0))], out_shape=[jax.ShapeDtypeStruct((T, D), BF16)],
        args=[h, g, sc[0], sh[0]], comms=comms)[0]


def ffn_up(xn, wg, wu, *, tm, name, comms=()):
    T, D = xn.shape
    nq, fs, _ = wg.shape

    def body(x_ref, wg_ref, wu_ref, s_ref, q_ref, a_ref):
        x = x_ref[...]
        g = _dg(x, wg_ref[0], NT)
        u = _dg(x, wu_ref[0], NT)
        sg = jax.nn.sigmoid(g)
        s = g * sg
        s_ref[0] = s.astype(BF16)
        q_ref[0] = (u * (sg + s * (1.0 - sg))).astype(BF16)
        a_ref[0] = (s * u).astype(BF16)

    w_spec = pl.BlockSpec((1, fs, D), lambda q, i: (q, 0, 0))
    o_spec = pl.BlockSpec((1, tm, fs), lambda q, i: (q, i, 0))
    o_shape = jax.ShapeDtypeStruct((nq, T, fs), BF16)
    return _call(
        body, name=name, grid=(nq, T // tm),
        in_specs=[pl.BlockSpec((tm, D), lambda q, i: (i, 0)), w_spec, w_spec],
        out_specs=[o_spec, o_spec, o_spec], out_shape=[o_shape, o_shape, o_shape], args=[xn, wg, wu], comms=comms)


def _norm_mod(y, g_ref, sc_ref, sh_ref):
    nx, _ = _rms_parts(y)
    return ((nx * g_ref[...]) * (1.0 + sc_ref[0]) + sh_ref[0]).astype(BF16)


def ffn_down(a, wd, hin, ga, *, target=None, norm=None, seq, tm, name, comms=()):
    nq, T, fs = a.shape
    D = wd.shape[-1]
    B, tps, nt = T // seq, seq // tm, T // tm

    def body(a_ref, wd_ref, h_ref, ga_ref, *rest):
        rest = list(rest)
        t_ref = rest.pop(0) if target is not None else None
        norm_refs = [rest.pop(0) for _ in range(3)] if norm is not None else None
        o_ref, f_ref = rest[0], rest[1]
        f = _dot(a_ref[0], wd_ref[0])
        for q in range(1, nq):
            f = f + _dot(a_ref[q], wd_ref[q])
        f_ref[...] = f.astype(BF16)
        y = h_ref[...] + (0.5 * ga_ref[0]) * f
        if target is not None:
            e = y - t_ref[...]
            o_ref[...] = e * (1.0 / D)
            rest[2][...] = jnp.full(rest[2].shape, 0.5 * jnp.sum(e * e) * (1.0 / D), F32)
        else:
            o_ref[...] = y
        if norm is not None:
            rest[2][...] = _norm_mod(y, *norm_refs)

    tile = pl.BlockSpec((tm, D), lambda i: (i, 0))
    in_specs = [pl.BlockSpec((nq, tm, fs), lambda i: (0, i, 0)), _resident(wd), tile, _mod_spec(ga, tps)]
    out_specs = [tile, tile]
    out_shape = [jax.ShapeDtypeStruct((T, D), F32), jax.ShapeDtypeStruct((T, D), BF16)]
    args = [a, wd, hin, ga[0]]
    if target is not None:
        in_specs.append(tile)
        args.append(target)
        out_specs.append(pl.BlockSpec((1, 8, 128), lambda i: (i, 0, 0)))
        out_shape.append(jax.ShapeDtypeStruct((nt, 8, 128), F32))
    if norm is not None:
        in_specs += [pl.BlockSpec((1, D), lambda i: (0, 0)), _mod_spec(norm[1], tps), _mod_spec(norm[2], tps)]
        args += [norm[0], norm[1][0], norm[2][0]]
        out_specs.append(tile)
        out_shape.append(jax.ShapeDtypeStruct((T, D), BF16))
    return _call(body, name=name, grid=(nt,), in_specs=in_specs, out_specs=out_specs, out_shape=out_shape, args=args,
                 comms=comms)


def proj_fwd(xn, w, groups, *, tm, name, comms=()):
    T, D = xn.shape

    def body(x_ref, w_ref, *o_refs):
        x = x_ref[...]
        for (r0, r1), o_ref in zip(groups, o_refs):
            o_ref[...] = _dg(x, w_ref[r0:r1, :], NT).astype(BF16)

    return _call(
        body, name=name, grid=(T // tm,), in_specs=[pl.BlockSpec((tm, D), lambda i: (i, 0)), _resident(w)],
        out_specs=[pl.BlockSpec((tm, r1 - r0), lambda i: (i, 0)) for r0, r1 in groups],
        out_shape=[jax.ShapeDtypeStruct((T, r1 - r0), BF16) for r0, r1 in groups], args=[xn, w], comms=comms)


def _layer_norm_parts(v):
    mu = jnp.mean(v, axis=-1, keepdims=True)
    d = v - mu
    rstd = lax.rsqrt(jnp.mean(d * d, axis=-1, keepdims=True) + EPS)
    return d * rstd, rstd


def _causal():
    row = lax.broadcasted_iota(jnp.int32, (CHUNK, CHUNK), 0)
    col = lax.broadcasted_iota(jnp.int32, (CHUNK, CHUNK), 1)
    return row >= col


def sgu_fwd(puv, gln, bln, ws, bs_t, *, cpb, name, comms=()):
    T = puv.shape[0]
    gd = D_A // N_GROUPS
    tm = cpb * CHUNK

    def body(p_ref, gln_ref, bln_ref, ws_ref, bs_ref, o_ref):
        causal = _causal()
        wm = [jnp.where(causal, ws_ref[g], 0.0).astype(BF16) for g in range(N_GROUPS)]
        for j in range(cpb):
            rows = slice(j * CHUNK, (j + 1) * CHUNK)
            u, _ = _gelu_parts(p_ref[rows, 0:D_A].astype(F32))
            vv, _ = _gelu_parts(p_ref[rows, D_A:2 * D_A].astype(F32))
            vhat, _ = _layer_norm_parts(vv)
            vn = (vhat * gln_ref[...] + bln_ref[...]).astype(BF16)
            for g in range(N_GROUPS):
                cols = slice(g * gd, (g + 1) * gd)
                z = _dot(wm[g], vn[:, cols]) + bs_ref[:, g:g + 1]
                o_ref[rows, cols] = (u[:, cols] * z).astype(BF16)

    full = lambda a: pl.BlockSpec(a.shape, lambda i: (0,) * a.ndim)
    return _call(
        body, name=name, grid=(T // tm,),
        in_specs=[pl.BlockSpec((tm, 2 * D_A), lambda i: (i, 0)), full(gln), full(bln), full(ws), full(bs_t)],
        out_specs=[pl.BlockSpec((tm, D_A), lambda i: (i, 0))], out_shape=[jax.ShapeDtypeStruct((T, D_A), BF16)],
        args=[puv, gln, bln, ws, bs_t], comms=comms)[0]


def _rms_parts(x):
    r = lax.rsqrt(jnp.mean(x * x, axis=-1, keepdims=True) + EPS)
    return x * r, r


KV_W = Q_PER_KV * HEAD_DIM
KV2 = N_KV * HEAD_DIM
STACK = Q_PER_KV * CHUNK


def _iota(shape, dim):
    return lax.broadcasted_iota(jnp.int32, shape, dim)


def _head_mean_matrix(n):
    return jnp.where((_iota((n, n), 0) >> 6) == (_iota((n, n), 1) >> 6), 1.0 / HEAD_DIM, 0.0).astype(BF16)


def _repeat_matrix(h):
    return jnp.where(_iota((KV2, KV_W), 0) == h * HEAD_DIM + (_iota((KV2, KV_W), 1) & (HEAD_DIM - 1)), 1.0, 0.0).astype(BF16)


def _fold_matrix(h):
    j, l = _iota((KV_W, KV2), 0), _iota((KV_W, KV2), 1)
    return jnp.where(((j & (HEAD_DIM - 1)) == (l & (HEAD_DIM - 1))) & ((l >> 6) == h), 1.0, 0.0).astype(BF16)


def _dot_split(x, m):
    hi = x.astype(BF16)
    lo = (x - hi.astype(F32)).astype(BF16)
    return _dot(hi, m) + _dot(lo, m)


def _head_rms(x, mean_matrix):
    r = lax.rsqrt(_dot_split(x * x, mean_matrix) + EPS)
    return x * r, r


def _stack_heads(x):
    head = _iota(x.shape, 1) >> 6
    return jnp.concatenate([jnp.where(head == g, x, 0.0).astype(BF16) for g in range(Q_PER_KV)], axis=0)


def _unstack_heads(y):
    head = _iota((CHUNK, KV_W), 1) >> 6
    out = jnp.where(head == 0, y[0:CHUNK], 0.0)
    for g in range(1, Q_PER_KV):
        out = out + jnp.where(head == g, y[g * CHUNK:(g + 1) * CHUNK], 0.0)
    return out


SOFTMAX_ROWS = 32


def _fill_mask_bias(bias_ref):
    qi = _iota((CHUNK, 2 * CHUNK), 0)
    kj = _iota((CHUNK, 2 * CHUNK), 1)
    diff = qi + CHUNK - kj
    window = (diff >= 0) & (diff < CHUNK)
    bias_ref[0] = jnp.where(window & (kj >= CHUNK), 0.0, NEG)
    bias_ref[1] = jnp.where(window, 0.0, NEG)


def _softmax_rows(s_ref, bias_ref, first, sink_ref, h, c):
    rows = pl.ds(pl.multiple_of(c * SOFTMAX_ROWS, SOFTMAX_ROWS), SOFTMAX_ROWS)
    in_block = pl.ds(pl.multiple_of((c % (CHUNK // SOFTMAX_ROWS)) * SOFTMAX_ROWS, SOFTMAX_ROWS), SOFTMAX_ROWS)
    sink = sink_ref[0, h * Q_PER_KV + c // (CHUNK // SOFTMAX_ROWS)]
    s = s_ref[rows, :] + bias_ref[first, in_block, :]
    m = jnp.maximum(jnp.max(s, axis=-1, keepdims=True), sink)
    p = jnp.exp(s - m)
    es = jnp.exp(sink - m)
    inv = 1.0 / (jnp.sum(p, axis=-1, keepdims=True) + es)
    return rows, p * inv, es * inv


def _fill_keys(p_ref, gk_ref, krep, vrep, seq):
    mean_k = _head_mean_matrix(KV2)
    reps = [_repeat_matrix(h) for h in range(N_KV)]
    for h in range(N_KV):
        krep[h][0:CHUNK, :] = jnp.zeros((CHUNK, KV_W), BF16)
        vrep[h][0:CHUNK, :] = jnp.zeros((CHUNK, KV_W), BF16)
    rows = min(seq, 4 * CHUNK)

    def piece(j, carry):
        r0 = pl.multiple_of(j * rows, CHUNK)
        nk, _ = _head_rms(p_ref[pl.ds(r0, rows), K_OFF:K_OFF + KV2].astype(F32), mean_k)
        kn = (nk * gk_ref[...]).astype(BF16)
        v = p_ref[pl.ds(r0, rows), V_OFF:V_OFF + KV2].astype(BF16)
        r1 = pl.multiple_of(r0 + CHUNK, CHUNK)
        for h in range(N_KV):
            krep[h][pl.ds(r1, rows), :] = _dot(kn, reps[h]).astype(BF16)
            vrep[h][pl.ds(r1, rows), :] = _dot(v, reps[h]).astype(BF16)
        return carry

    lax.fori_loop(0, seq // rows, piece, 0)


def attn_fwd(pqkv, gq_t, gk_t, sinks, *, seq, name, comms=()):
    T = pqkv.shape[0]
    nb = seq // CHUNK

    def body(p_ref, gq_ref, gk_ref, sink_ref, o_ref, k0, k1, v0, v1, bias_ref, s_ref, pr_ref):
        krep, vrep = [k0, k1], [v0, v1]
        _fill_keys(p_ref, gk_ref, krep, vrep, seq)
        _fill_mask_bias(bias_ref)
        mean_q = _head_mean_matrix(D_B)

        def block(i, carry):
            r0 = pl.multiple_of(i * CHUNK, CHUNK)
            first = jnp.minimum(i, 1)
            nq, _ = _head_rms(p_ref[pl.ds(r0, CHUNK), 0:D_B].astype(F32), mean_q)
            qn = nq * (gq_ref[...] * ATT_SCALE)
            for h in range(N_KV):
                qs = _stack_heads(qn[:, h * KV_W:(h + 1) * KV_W])
                s_ref[...] = _dg(qs, krep[h][pl.ds(r0, 2 * CHUNK), :], NT)

                def rows_of(c, carry2):
                    rows, probs, _ = _softmax_rows(s_ref, bias_ref, first, sink_ref, h, c)
                    pr_ref[rows, :] = probs.astype(BF16)
                    return carry2

                lax.fori_loop(0, STACK // SOFTMAX_ROWS, rows_of, 0, unroll=True)
                out = _unstack_heads(_dot(pr_ref[...], vrep[h][pl.ds(r0, 2 * CHUNK), :]))
                o_ref[pl.ds(r0, CHUNK), h * KV_W:(h + 1) * KV_W] = out.astype(BF16)
            return carry

        lax.fori_loop(0, nb, block, 0)

    return _call(
        body, name=name, grid=(T // seq,),
        in_specs=[pl.BlockSpec((seq, QKV_W), lambda b: (b, 0)), pl.BlockSpec(gq_t.shape, lambda b: (0, 0)),
                  pl.BlockSpec(gk_t.shape, lambda b: (0, 0)), pl.BlockSpec(memory_space=pltpu.SMEM)],
        out_specs=[pl.BlockSpec((seq, D_B), lambda b: (b, 0))], out_shape=[jax.ShapeDtypeStruct((T, D_B), BF16)],
        scratch_shapes=[pltpu.VMEM((seq + CHUNK, KV_W), BF16)] * 4
        + [pltpu.VMEM((2, CHUNK, 2 * CHUNK), F32), pltpu.VMEM((STACK, 2 * CHUNK), F32), pltpu.VMEM((STACK, 2 * CHUNK), BF16)],
        args=[pqkv, gq_t, gk_t, sinks], comms=comms)[0]


def mixer_out_fwd(sgu, att, pg, hin, ga, wa, wb, wo, norm, *, seq, tm, name, comms=()):
    T, D = hin.shape
    B, tps = T // seq, seq // tm

    def body(s_ref, t_ref, pg_ref, h_ref, ga_ref, wa_ref, wb_ref, wo_ref, g_ref, sc_ref, sh_ref,
             ya_ref, yb_ref, mg_ref, m_ref, ho_ref, xn_ref):
        ya = _dot(s_ref[...], wa_ref[...])
        yb = _dot(t_ref[...], wb_ref[...])
        merged = (jax.nn.sigmoid(pg_ref[:, 0:D].astype(F32)) * ya
                  + jax.nn.sigmoid(pg_ref[:, D:2 * D].astype(F32)) * yb)
        mg = merged.astype(BF16)
        m = _dot(mg, wo_ref[...])
        ya_ref[...] = ya.astype(BF16)
        yb_ref[...] = yb.astype(BF16)
        mg_ref[...] = mg
        m_ref[...] = m.astype(BF16)
        y = h_ref[...] + ga_ref[0] * m
        ho_ref[...] = y
        xn_ref[...] = _norm_mod(y, g_ref, sc_ref, sh_ref)

    tile = lambda w: pl.BlockSpec((tm, w), lambda i: (i, 0))
    b16 = jax.ShapeDtypeStruct((T, D), BF16)
    return _call(
        body, name=name, grid=(T // tm,),
        in_specs=[tile(D_A), tile(D_B), tile(2 * D), tile(D), _mod_spec(ga, tps), _resident(wa), _resident(wb),
                  _resident(wo), pl.BlockSpec((1, D), lambda i: (0, 0)), _mod_spec(norm[1], tps), _mod_spec(norm[2], tps)],
        out_specs=[tile(D)] * 6, out_shape=[b16, b16, b16, b16, jax.ShapeDtypeStruct((T, D), F32), b16],
        args=[sgu, att, pg, hin, ga[0], wa, wb, wo, norm[0], norm[1][0], norm[2][0]], comms=comms)


def ffn_bwd_act(dh, f, ga, wd, s, q, *, seq, tm, name, comms=()):
    T, D = dh.shape
    nq, fs, _ = wd.shape
    B, tps = T // seq, seq // tm

    def body(dh_ref, f_ref, ga_ref, wd_ref, s_ref, q_ref, dg_ref, du_ref, df_ref, dga_ref):
        i = pl.program_id(0)
        d = dh_ref[...]
        df = ((0.5 * ga_ref[0]) * d).astype(BF16)
        df_ref[...] = df
        part = 0.5 * jnp.sum(d * f_ref[...].astype(F32), axis=0, keepdims=True)
        for j in range(nq):
            da = _dg(df, wd_ref[j], NT)
            du_ref[j] = (da * s_ref[j].astype(F32)).astype(BF16)
            dg_ref[j] = (da * q_ref[j].astype(F32)).astype(BF16)

        @pl.when(i % tps == 0)
        def _():
            dga_ref[0] = part

        @pl.when(i % tps != 0)
        def _():
            dga_ref[0] += part

    tile = pl.BlockSpec((tm, D), lambda i: (i, 0))
    per_seq = pl.BlockSpec((1, 1, D), lambda i: (i // tps, 0, 0))
    act = pl.BlockSpec((nq, tm, fs), lambda i: (0, i, 0))
    o_shape = jax.ShapeDtypeStruct((nq, T, fs), BF16)
    dg, du, df, dga = _call(
        body, name=name, grid=(T // tm,), in_specs=[tile, tile, _mod_spec(ga, tps), _resident(wd), act, act],
        out_specs=[act, act, tile, per_seq],
        out_shape=[o_shape, o_shape, jax.ShapeDtypeStruct((T, D), BF16), jax.ShapeDtypeStruct((B, 1, D), F32)],
        args=[dh, f, ga[0], wd, s, q], comms=comms)
    return dg, du, df, dga.reshape(B, D)


def mm_tn(pairs, *, tt, name, comms=()):
    n = len(pairs)
    flat = []
    for pair in pairs:
        for a in pair:
            if not any(a is b for b in flat):
                flat.append(a)
    where = lambda a: [k for k, b in enumerate(flat) if a is b][0]
    nq = max([a.shape[0] for a in flat if a.ndim == 3] + [1])
    T = flat[0].shape[-2]
    tt = min(tt, T)
    nt = T // tt
    stacked = [x.ndim == 3 or y.ndim == 3 for x, y in pairs]

    def body(*refs):
        in_refs, o_refs, accs = refs[:len(flat)], refs[len(flat):len(flat) + n], refs[len(flat) + n:]
        t = pl.program_id(1)
        value = lambda a: in_refs[where(a)][0] if a.ndim == 3 else in_refs[where(a)][...]

        def put(j, v):
            if stacked[j]:
                o_refs[j][0] = v.astype(BF16)
            else:
                o_refs[j][...] = v.astype(BF16)

        for j, (x, y) in enumerate(pairs):
            part = _dg(value(x), value(y), TN)
            if nt == 1:
                put(j, part)
                continue

            @pl.when(t == 0)
            def _():
                accs[j][...] = part

            @pl.when(t != 0)
            def _():
                accs[j][...] += part

        if nt > 1:
            @pl.when(t == nt - 1)
            def _():
                for j in range(n):
                    put(j, accs[j][...])

    def spec(a):
        if a.ndim == 3:
            return pl.BlockSpec((1, tt, a.shape[2]), lambda q, t: (q, t, 0))
        return pl.BlockSpec((tt, a.shape[1]), lambda q, t: (t, 0))

    out_specs, out_shape = [], []
    for j, (x, y) in enumerate(pairs):
        K, N = x.shape[-1], y.shape[-1]
        if stacked[j]:
            out_specs.append(pl.BlockSpec((1, K, N), lambda q, t: (q, 0, 0)))
            out_shape.append(jax.ShapeDtypeStruct((nq, K, N), BF16))
        else:
            out_specs.append(pl.BlockSpec((K, N), lambda q, t: (0, 0)))
            out_shape.append(jax.ShapeDtypeStruct((K, N), BF16))
    return _call(
        body, name=name, grid=(nq, nt), in_specs=[spec(a) for a in flat],
        out_specs=out_specs, out_shape=out_shape,
        scratch_shapes=[pltpu.VMEM((x.shape[-1], y.shape[-1]), F32) for x, y in pairs] if nt > 1 else [],
        args=flat, comms=comms)


def dx_norm_bwd(dys, ws, hin, dh_out, g, sc, *, seq, tm, name, comms=()):
    T, D = hin.shape
    B, tps = T // seq, seq // tm
    n = len(dys)
    ranged = [w if isinstance(w, tuple) else (w, 0, w.shape[-2]) for w in ws]
    ws = []
    for w, _, _ in ranged:
        if not any(w is u for u in ws):
            ws.append(w)
    where = [[k for k, u in enumerate(ws) if u is w][0] for w, _, _ in ranged]

    def body(*refs):
        dy_refs, w_refs = refs[:n], refs[n:n + len(ws)]
        h_ref, dho_ref, g_ref, sc_ref, dhi_ref, dsh_ref, dsc_ref, dgn_ref = refs[n + len(ws):]
        i = pl.program_id(0)
        dxn = None
        for j in range(n):
            w_ref, (_, r0, r1) = w_refs[where[j]], ranged[j]
            if dys[j].ndim == 3:
                for q in range(dys[j].shape[0]):
                    part = _dot(dy_refs[j][q], w_ref[q])
                    dxn = part if dxn is None else dxn + part
            else:
                part = _dot(dy_refs[j][...], w_ref[r0:r1, :])
                dxn = part if dxn is None else dxn + part
        x = h_ref[...]
        nx, r = _rms_parts(x)
        gain = g_ref[...]
        one_sc = 1.0 + sc_ref[0]
        dsh = jnp.sum(dxn, axis=0, keepdims=True)
        dsc = jnp.sum(dxn * (nx * gain), axis=0, keepdims=True)
        dgn = jnp.sum(dxn * one_sc * nx, axis=0, keepdims=True)
        dn = dxn * one_sc * gain
        dhi_ref[...] = dho_ref[...] + r * (dn - nx * jnp.mean(dn * nx, axis=-1, keepdims=True))

        @pl.when(i % tps == 0)
        def _():
            dsh_ref[0] = dsh
            dsc_ref[0] = dsc

        @pl.when(i % tps != 0)
        def _():
            dsh_ref[0] += dsh
            dsc_ref[0] += dsc

        @pl.when(i == 0)
        def _():
            dgn_ref[...] = dgn

        @pl.when(i != 0)
        def _():
            dgn_ref[...] += dgn

    def dy_spec(a):
        if a.ndim == 3:
            return pl.BlockSpec((a.shape[0], tm, a.shape[2]), lambda i: (0, i, 0))
        return pl.BlockSpec((tm, a.shape[1]), lambda i: (i, 0))

    tile = pl.BlockSpec((tm, D), lambda i: (i, 0))
    per_seq = pl.BlockSpec((1, 1, D), lambda i: (i // tps, 0, 0))
    row = pl.BlockSpec((1, D), lambda i: (0, 0))
    dhi, dsh, dsc, dgn = _call(
        body, name=name, grid=(T // tm,),
        in_specs=[dy_spec(a) for a in dys] + [_resident(w) for w in ws] + [tile, tile, row, _mod_spec(sc, tps)],
        out_specs=[tile, per_seq, per_seq, row],
        out_shape=[jax.ShapeDtypeStruct((T, D), F32), jax.ShapeDtypeStruct((B, 1, D), F32),
                   jax.ShapeDtypeStruct((B, 1, D), F32), jax.ShapeDtypeStruct((1, D), F32)],
        args=[*dys, *ws, hin, dh_out, g, sc[0]], comms=comms)
    return dhi, dsh.reshape(B, D), dsc.reshape(B, D), dgn


def mixer_out_bwd(dh, m, ga, ya, yb, pg, wa, wb, wo, *, seq, tm, name, comms=()):
    T, D = dh.shape
    B, tps = T // seq, seq // tm

    def body(dh_ref, m_ref, ga_ref, ya_ref, yb_ref, pg_ref, wa_ref, wb_ref, wo_ref,
             dg_ref, dya_ref, dyb_ref, ds_ref, dt_ref, dm_ref, dga_ref):
        i = pl.program_id(0)
        d = dh_ref[...]
        dm = (ga_ref[0] * d).astype(BF16)
        dm_ref[...] = dm
        part = jnp.sum(d * m_ref[...].astype(F32), axis=0, keepdims=True)

        @pl.when(i % tps == 0)
        def _():
            dga_ref[0] = part

        @pl.when(i % tps != 0)
        def _():
            dga_ref[0] += part

        dmg = _dg(dm, wo_ref[...], NT)
        sa = jax.nn.sigmoid(pg_ref[:, 0:D].astype(F32))
        sb = jax.nn.sigmoid(pg_ref[:, D:2 * D].astype(F32))
        dg_ref[:, 0:D] = (dmg * ya_ref[...].astype(F32) * (sa * (1.0 - sa))).astype(BF16)
        dg_ref[:, D:2 * D] = (dmg * yb_ref[...].astype(F32) * (sb * (1.0 - sb))).astype(BF16)
        dya = (dmg * sa).astype(BF16)
        dyb = (dmg * sb).astype(BF16)
        dya_ref[...] = dya
        dyb_ref[...] = dyb
        ds_ref[...] = _dg(dya, wa_ref[...], NT).astype(BF16)
        dt_ref[...] = _dg(dyb, wb_ref[...], NT).astype(BF16)

    tile = lambda w: pl.BlockSpec((tm, w), lambda i: (i, 0))
    per_seq = pl.BlockSpec((1, 1, D), lambda i: (i // tps, 0, 0))
    dg, dya, dyb, ds, dt, dm, dga = _call(
        body, name=name, grid=(T // tm,),
        in_specs=[tile(D), tile(D), _mod_spec(ga, tps), tile(D), tile(D), tile(2 * D), _resident(wa), _resident(wb),
                  _resident(wo)],
        out_specs=[tile(2 * D), tile(D), tile(D), tile(D_A), tile(D_B), tile(D), per_seq],
        out_shape=[jax.ShapeDtypeStruct((T, 2 * D), BF16), jax.ShapeDtypeStruct((T, D), BF16),
                   jax.ShapeDtypeStruct((T, D), BF16), jax.ShapeDtypeStruct((T, D_A), BF16),
                   jax.ShapeDtypeStruct((T, D_B), BF16), jax.ShapeDtypeStruct((T, D), BF16),
                   jax.ShapeDtypeStruct((B, 1, D), F32)],
        args=[dh, m, ga[0], ya, yb, pg, wa, wb, wo], comms=comms)
    return dg, dya, dyb, ds, dt, dm, dga.reshape(B, D)


def sgu_bwd(puv, dsgu, gln, bln, ws, bs_t, *, cpb, name, comms=()):
    T = puv.shape[0]
    gd = D_A // N_GROUPS
    tm = cpb * CHUNK

    def body(p_ref, ds_ref, gln_ref, bln_ref, ws_ref, bs_ref, d_ref, dws_ref, dz_ref, dgl_ref, dbl_ref):
        i = pl.program_id(0)
        causal = _causal()
        wf = [jnp.where(causal, ws_ref[g], 0.0) for g in range(N_GROUPS)]
        wm = [w.astype(BF16) for w in wf]
        wt = [w.T.astype(BF16) for w in wf]
        dws = [jnp.zeros((CHUNK, CHUNK), F32) for _ in range(N_GROUPS)]
        dzs = jnp.zeros((CHUNK, D_A), F32)
        dgl = jnp.zeros((1, D_A), F32)
        dbl = jnp.zeros((1, D_A), F32)
        for j in range(cpb):
            rows = slice(j * CHUNK, (j + 1) * CHUNK)
            au = p_ref[rows, 0:D_A].astype(F32)
            av = p_ref[rows, D_A:2 * D_A].astype(F32)
            u, tu = _gelu_parts(au)
            vv, tv = _gelu_parts(av)
            vhat, rstd = _layer_norm_parts(vv)
            vn = (vhat * gln_ref[...] + bln_ref[...]).astype(BF16)
            dsg = ds_ref[rows, :].astype(F32)
            dz = dsg * u
            dzb = dz.astype(BF16)
            zs, dvns = [], []
            for g in range(N_GROUPS):
                cols = slice(g * gd, (g + 1) * gd)
                zs.append(_dot(wm[g], vn[:, cols]) + bs_ref[:, g:g + 1])
                dws[g] = dws[g] + _dg(dzb[:, cols], vn[:, cols], NT)
                dvns.append(_dot(wt[g], dzb[:, cols]))
            z = jnp.concatenate(zs, axis=1)
            dvn = jnp.concatenate(dvns, axis=1)
            d_ref[rows, 0:D_A] = (dsg * z * _dgelu(au, tu)).astype(BF16)
            dvh = dvn * gln_ref[...]
            dvv = rstd * (dvh - jnp.mean(dvh, axis=-1, keepdims=True)
                          - vhat * jnp.mean(dvh * vhat, axis=-1, keepdims=True))
            d_ref[rows, D_A:2 * D_A] = (dvv * _dgelu(av, tv)).astype(BF16)
            dzs = dzs + dz
            dgl = dgl + jnp.sum(dvn * vhat, axis=0, keepdims=True)
            dbl = dbl + jnp.sum(dvn, axis=0, keepdims=True)

        @pl.when(i == 0)
        def _():
            for g in range(N_GROUPS):
                dws_ref[g] = jnp.where(causal, dws[g], 0.0)
            dz_ref[...] = dzs
            dgl_ref[...] = dgl
            dbl_ref[...] = dbl

        @pl.when(i != 0)
        def _():
            for g in range(N_GROUPS):
                dws_ref[g] += jnp.where(causal, dws[g], 0.0)
            dz_ref[...] += dzs
            dgl_ref[...] += dgl
            dbl_ref[...] += dbl

    full = lambda a: pl.BlockSpec(a.shape, lambda i: (0,) * a.ndim)
    acc = lambda s: pl.BlockSpec(s, lambda i: (0,) * len(s))
    return _call(
        body, name=name, grid=(T // tm,),
        in_specs=[pl.BlockSpec((tm, 2 * D_A), lambda i: (i, 0)), pl.BlockSpec((tm, D_A), lambda i: (i, 0)),
                  full(gln), full(bln), full(ws), full(bs_t)],
        out_specs=[pl.BlockSpec((tm, 2 * D_A), lambda i: (i, 0)), acc((N_GROUPS, CHUNK, CHUNK)), acc((CHUNK, D_A)),
                   acc((1, D_A)), acc((1, D_A))],
        out_shape=[jax.ShapeDtypeStruct((T, 2 * D_A), BF16), jax.ShapeDtypeStruct((N_GROUPS, CHUNK, CHUNK), F32),
                   jax.ShapeDtypeStruct((CHUNK, D_A), F32), jax.ShapeDtypeStruct((1, D_A), F32),
                   jax.ShapeDtypeStruct((1, D_A), F32)],
        args=[puv, dsgu, gln, bln, ws, bs_t], comms=comms)


def attn_bwd(pqkv, datt, gq_t, gk_t, sinks, *, seq, name, comms=()):
    T = pqkv.shape[0]
    nb = seq // CHUNK

    def body(p_ref, do_ref, gq_ref, gk_ref, sink_ref, d_ref, dgq_ref, dgk_ref, dsk_ref,
             k0, k1, v0, v1, dk0, dk1, dv0, dv1, dgq_s, dsk_s, bias_ref, s_ref, dp_ref, pr_ref, ds_ref):
        b = pl.program_id(0)
        krep, vrep, dkacc, dvacc = [k0, k1], [v0, v1], [dk0, dk1], [dv0, dv1]
        _fill_keys(p_ref, gk_ref, krep, vrep, seq)
        _fill_mask_bias(bias_ref)
        for h in range(N_KV):
            dkacc[h][...] = jnp.zeros_like(dkacc[h])
            dvacc[h][...] = jnp.zeros_like(dvacc[h])
        dgq_s[...] = jnp.zeros_like(dgq_s)
        dsk_s[...] = jnp.zeros_like(dsk_s)
        mean_q = _head_mean_matrix(D_B)
        lane = _iota((1, 128), 1)

        def block(i, carry):
            r0 = pl.multiple_of(i * CHUNK, CHUNK)
            first = jnp.minimum(i, 1)
            nq, rq = _head_rms(p_ref[pl.ds(r0, CHUNK), 0:D_B].astype(F32), mean_q)
            qn = nq * (gq_ref[...] * ATT_SCALE)
            dqn_parts = []
            for h in range(N_KV):
                cols = slice(h * KV_W, (h + 1) * KV_W)
                qs = _stack_heads(qn[:, cols])
                kk = krep[h][pl.ds(r0, 2 * CHUNK), :]
                dos = _stack_heads(do_ref[pl.ds(r0, CHUNK), cols].astype(F32))
                s_ref[...] = _dg(qs, kk, NT)
                dp_ref[...] = _dg(dos, vrep[h][pl.ds(r0, 2 * CHUNK), :], NT)

                def rows_of(c, carry2):
                    rows, probs, psink = _softmax_rows(s_ref, bias_ref, first, sink_ref, h, c)
                    dp = dp_ref[rows, :]
                    dr = jnp.sum(probs * dp, axis=-1, keepdims=True)
                    pr_ref[rows, :] = probs.astype(BF16)
                    ds_ref[rows, :] = (probs * (dp - dr)).astype(BF16)
                    head = h * Q_PER_KV + c // (CHUNK // SOFTMAX_ROWS)
                    dsk_s[...] += jnp.where(lane == head, -jnp.sum(psink * dr), 0.0)
                    return carry2

                lax.fori_loop(0, STACK // SOFTMAX_ROWS, rows_of, 0, unroll=True)
                dqn_parts.append(_unstack_heads(_dot(ds_ref[...], kk)))
                dkacc[h][pl.ds(r0, 2 * CHUNK), :] += _dg(ds_ref[...], qs, TN)
                dvacc[h][pl.ds(r0, 2 * CHUNK), :] += _dg(pr_ref[...], dos, TN)
            dqn = jnp.concatenate(dqn_parts, axis=1) * ATT_SCALE
            dn = dqn * gq_ref[...]
            d_ref[pl.ds(r0, CHUNK), 0:D_B] = (rq * (dn - nq * _dot_split(dn * nq, mean_q))).astype(BF16)
            dgq_s[...] += jnp.sum(dqn * nq, axis=0, keepdims=True)
            return carry

        lax.fori_loop(0, nb, block, 0)

        mean_k = _head_mean_matrix(KV2)
        folds = [_fold_matrix(h) for h in range(N_KV)]
        rows = min(seq, 4 * CHUNK)

        def piece(j, dgk):
            r0 = pl.multiple_of(j * rows, CHUNK)
            r1 = pl.multiple_of(r0 + CHUNK, CHUNK)
            dkn = _dot_split(dkacc[0][pl.ds(r1, rows), :], folds[0]) + _dot_split(dkacc[1][pl.ds(r1, rows), :], folds[1])
            dv = _dot_split(dvacc[0][pl.ds(r1, rows), :], folds[0]) + _dot_split(dvacc[1][pl.ds(r1, rows), :], folds[1])
            nk, rk = _head_rms(p_ref[pl.ds(r0, rows), K_OFF:K_OFF + KV2].astype(F32), mean_k)
            dn = dkn * gk_ref[...]
            d_ref[pl.ds(r0, rows), K_OFF:K_OFF + KV2] = (rk * (dn - nk * _dot_split(dn * nk, mean_k))).astype(BF16)
            d_ref[pl.ds(r0, rows), V_OFF:V_OFF + KV2] = dv.astype(BF16)
            return dgk + jnp.sum(dkn * nk, axis=0, keepdims=True)

        dgk = lax.fori_loop(0, seq // rows, piece, jnp.zeros((1, KV2), F32))

        @pl.when(b == 0)
        def _():
            dgq_ref[...] = dgq_s[...]
            dgk_ref[...] = dgk
            dsk_ref[...] = jnp.broadcast_to(dsk_s[...], dsk_ref.shape)

        @pl.when(b != 0)
        def _():
            dgq_ref[...] += dgq_s[...]
            dgk_ref[...] += dgk
            dsk_ref[...] += jnp.broadcast_to(dsk_s[...], dsk_ref.shape)

    acc = lambda s: pl.BlockSpec(s, lambda b: (0, 0))
    return _call(
        body, name=name, grid=(T // seq,),
        in_specs=[pl.BlockSpec((seq, QKV_W), lambda b: (b, 0)), pl.BlockSpec((seq, D_B), lambda b: (b, 0)),
                  pl.BlockSpec(gq_t.shape, lambda b: (0, 0)), pl.BlockSpec(gk_t.shape, lambda b: (0, 0)),
                  pl.BlockSpec(memory_space=pltpu.SMEM)],
        out_specs=[pl.BlockSpec((seq, QKV_W), lambda b: (b, 0)), acc((1, D_B)), acc((1, KV2)), acc((8, 128))],
        out_shape=[jax.ShapeDtypeStruct((T, QKV_W), BF16), jax.ShapeDtypeStruct((1, D_B), F32),
                   jax.ShapeDtypeStruct((1, KV2), F32), jax.ShapeDtypeStruct((8, 128), F32)],
        scratch_shapes=[pltpu.VMEM((seq + CHUNK, KV_W), BF16)] * 4 + [pltpu.VMEM((seq + CHUNK, KV_W), F32)] * 4
        + [pltpu.VMEM((1, D_B), F32), pltpu.VMEM((1, 128), F32), pltpu.VMEM((2, CHUNK, 2 * CHUNK), F32)]
        + [pltpu.VMEM((STACK, 2 * CHUNK), F32)] * 2 + [pltpu.VMEM((STACK, 2 * CHUNK), BF16)] * 2,
        args=[pqkv, datt, gq_t, gk_t, sinks], comms=comms)


def ada_fwd(c_all, w, b, *, name):
    nb, D = c_all.shape
    N = w.shape[1]
    tn = N // 3

    def body(c_ref, w_ref, b_ref, o_ref):
        c = c_ref[...]
        cond = (c * jax.nn.sigmoid(c)).astype(BF16)
        o_ref[...] = _dot(cond, w_ref[...].astype(BF16)) + b_ref[...]

    return _call(
        body, name=name, grid=(3,),
        in_specs=[pl.BlockSpec((nb, D), lambda j: (0, 0)), pl.BlockSpec((D, tn), lambda j: (0, j)),
                  pl.BlockSpec((1, tn), lambda j: (0, j))],
        out_specs=[pl.BlockSpec((nb, tn), lambda j: (0, j))], out_shape=[jax.ShapeDtypeStruct((nb, N), F32)],
        args=[c_all, w, b])[0]


def ada_bwd(c_all, dmods_all, dmods_mine, gain_rows, *, name, comms=()):
    nb, D = c_all.shape
    NA = dmods_all.shape[1]
    N = dmods_mine.shape[1]
    tn = N // 3

    def body(c_ref, da_ref, dm_ref, gr_ref, db_ref, dw_ref, dgn_ref):
        c = c_ref[...]
        cond = (c * jax.nn.sigmoid(c)).astype(BF16)
        dw_ref[...] = _dg(cond, dm_ref[...].astype(BF16), TN)
        db_ref[...] = jnp.sum(da_ref[...], axis=0, keepdims=True)
        total = gr_ref[0:1, :]
        for d in range(1, N_DEV):
            total = total + gr_ref[d:d + 1, :]
        dgn_ref[...] = total

    return _call(
        body, name=name, grid=(3,),
        in_specs=[pl.BlockSpec((nb, D), lambda j: (0, 0)), pl.BlockSpec((nb, NA), lambda j: (0, 0)),
                  pl.BlockSpec((nb, tn), lambda j: (0, j)), pl.BlockSpec((N_DEV, D), lambda j: (0, 0))],
        out_specs=[pl.BlockSpec((1, NA), lambda j: (0, 0)), pl.BlockSpec((D, tn), lambda j: (0, j)),
                   pl.BlockSpec((1, D), lambda j: (0, 0))],
        out_shape=[jax.ShapeDtypeStruct((1, NA), F32), jax.ShapeDtypeStruct((D, N), F32),
                   jax.ShapeDtypeStruct((1, D), F32)],
        args=[c_all, dmods_all, dmods_mine, gain_rows], comms=comms)


def adamw(items, *, steps, name, comms=()):
    n = len(items)
    c1 = 1.0 / (1.0 - ADAM_B1 ** ADAM_STEP)
    c2 = 1.0 / (1.0 - ADAM_B2 ** ADAM_STEP)

    def body(*refs):
        for j in range(n):
            w_ref, g_ref, m_ref, v_ref = refs[4 * j:4 * j + 4]
            go_ref, d_ref, mo_ref, vo_ref = refs[4 * n + 4 * j:4 * n + 4 * j + 4]
            gg = g_ref[...]
            mn = ADAM_B1 * m_ref[...] + (1.0 - ADAM_B1) * gg
            vn = ADAM_B2 * v_ref[...] + (1.0 - ADAM_B2) * (gg * gg)
            go_ref[...] = gg
            mo_ref[...] = mn
            vo_ref[...] = vn
            d_ref[...] = -ADAM_LR * ((mn * c1) / (jnp.sqrt(vn * c2) + ADAM_EPS) + ADAM_WD * w_ref[...])

    in_specs, out_specs, out_shape, args = [], [], [], []
    for item in items:
        R, C = item[0].shape
        blk = pl.BlockSpec((R // steps, C), lambda i: (i, 0))
        in_specs += [blk] * 4
        out_specs += [blk] * 4
        out_shape += [jax.ShapeDtypeStruct((R, C), F32)] * 4
        args += list(item)
    res = _call(body, name=name, grid=(steps,), in_specs=in_specs, out_specs=out_specs, out_shape=out_shape, args=args,
                comms=comms)
    return [tuple(res[4 * j:4 * j + 4]) for j in range(n)]


def _place():
    return lax.axis_index("x"), lax.axis_index("y"), lax.axis_index("c")


def _flip(v, bit):
    return 1 - v if bit else v


def _other_chips(mx, my):
    return [(_flip(mx, k & 2), _flip(my, k & 1)) for k in range(1, N_QUAD)]


def _remote(src, dst, send_sem, recv_sem, peer):
    return pltpu.make_async_remote_copy(src_ref=src, dst_ref=dst, send_sem=send_sem, recv_sem=recv_sem,
                                        device_id=peer, device_id_type=MESH)


def ag8_comm(x):
    def copies(srcs, lands, ss, rs, only_sends=False):
        mx, my, mc = _place()
        me = 4 * mx + 2 * my + mc
        local = pltpu.make_async_copy(srcs[0], lands[0].at[me], ss.at[N_DEV - 1])
        sends, recvs = [], []
        for k in range(1, N_DEV):
            peer = (_flip(mx, k & 4), _flip(my, k & 2), _flip(mc, k & 1))
            sends.append(_remote(srcs[0], lands[0].at[me], ss.at[k - 1], rs.at[k - 1], peer))
            if not only_sends:
                recvs.append(_remote(srcs[0], lands[0].at[4 * peer[0] + 2 * peer[1] + peer[2]], ss.at[k - 1], rs.at[k - 1], peer))
        return local, sends, recvs

    def start(srcs, bufs, lands, ss, rs):
        local, sends, _ = copies(srcs, lands, ss, rs, only_sends=True)
        local.start()
        for cp in sends:
            cp.start()

    def finish(srcs, bufs, lands, ss, rs):
        local, sends, recvs = copies(srcs, lands, ss, rs)
        for cp in recvs:
            cp.wait_recv()
        for cp in sends:
            cp.wait_send()
        local.wait()

    return Comm(srcs=[x], land_shapes=[jax.ShapeDtypeStruct((N_DEV,) + x.shape, x.dtype)], n_sems=N_DEV,
                start=start, finish=finish)


def sum8(stacked, *, name):
    _, r, n = stacked.shape

    def body(s_ref, o_ref):
        total = s_ref[0]
        for d in range(1, N_DEV):
            total = total + s_ref[d]
        o_ref[...] = total

    return _call(body, name=name, grid=(), in_specs=[pl.BlockSpec(memory_space=pltpu.VMEM)],
                 out_specs=[pl.BlockSpec(memory_space=pltpu.VMEM)], out_shape=[jax.ShapeDtypeStruct((r, n), F32)],
                 args=[stacked])[0]


def cast_into_stacks(ws, quad, *, name, comms=()):
    steps = 4

    def body(q_ref, *refs):
        for w_ref, o_ref in zip(refs[:len(ws)], refs[len(ws):]):
            o_ref[0] = w_ref[...].astype(BF16)

    return _call(
        body, name=name, grid=(steps,), prefetch=[quad],
        in_specs=[pl.BlockSpec((w.shape[0] // steps, w.shape[1]), lambda i, q: (i, 0)) for w in ws],
        out_specs=[pl.BlockSpec((1, w.shape[0] // steps, w.shape[1]), lambda i, q: (q[0], i, 0)) for w in ws],
        out_shape=[jax.ShapeDtypeStruct((N_QUAD,) + w.shape, BF16) for w in ws], args=list(ws), comms=comms)


def gather_comm(stacks, forward=True):
    n = len(stacks)

    def copies(bufs, ss, rs, only_sends=False):
        mx, my, mc = _place()
        q = 2 * mx + my
        sibling = (mx, my, 1 - mc)
        ici_send, ici_recv, fwd_send, fwd_recv = [], [], [], []
        for p in range(n):
            hr = stacks[p].shape[1] // 2
            mine, other = pl.ds(mc * hr, hr), pl.ds((1 - mc) * hr, hr)
            for k, chip in enumerate(_other_chips(mx, my)):
                qk = 2 * chip[0] + chip[1]
                peer = (chip[0], chip[1], mc)
                own, landed, theirs = bufs[p].at[q, mine, :], bufs[p].at[qk, mine, :], bufs[p].at[qk, other, :]
                ici_send.append(_remote(own, own, ss.at[6 * p + k], rs.at[6 * p + k], peer))
                if only_sends:
                    continue
                ici_recv.append(_remote(own, landed, ss.at[6 * p + k], rs.at[6 * p + k], peer))
                if forward:
                    fwd_send.append(_remote(landed, landed, ss.at[6 * p + 3 + k], rs.at[6 * p + 3 + k], sibling))
                    fwd_recv.append(_remote(theirs, theirs, ss.at[6 * p + 3 + k], rs.at[6 * p + 3 + k], sibling))
        return ici_send, ici_recv, fwd_send, fwd_recv

    def start(srcs, bufs, lands, ss, rs):
        for cp in copies(bufs, ss, rs, only_sends=True)[0]:
            cp.start()

    def finish(srcs, bufs, lands, ss, rs):
        ici_send, ici_recv, fwd_send, fwd_recv = copies(bufs, ss, rs)
        for j, arrived in enumerate(ici_recv):
            arrived.wait_recv()
            if forward:
                fwd_send[j].start()
        for cp in fwd_recv:
            cp.wait_recv()
        for cp in ici_send + fwd_send:
            cp.wait_send()

    return Comm(bufs=stacks, n_sems=6 * n, start=start, finish=finish)


def forward_comm(stacks):
    n = len(stacks)

    def copies(bufs, ss, rs, only_sends=False):
        mx, my, mc = _place()
        sibling = (mx, my, 1 - mc)
        send, recv = [], []
        for p in range(n):
            hr = stacks[p].shape[1] // 2
            for k, chip in enumerate(_other_chips(mx, my)):
                qk = 2 * chip[0] + chip[1]
                landed, theirs = bufs[p].at[qk, pl.ds(mc * hr, hr), :], bufs[p].at[qk, pl.ds((1 - mc) * hr, hr), :]
                send.append(_remote(landed, landed, ss.at[3 * p + k], rs.at[3 * p + k], sibling))
                if not only_sends:
                    recv.append(_remote(theirs, theirs, ss.at[3 * p + k], rs.at[3 * p + k], sibling))
        return send, recv

    def start(srcs, bufs, lands, ss, rs):
        for cp in copies(bufs, ss, rs, only_sends=True)[0]:
            cp.start()

    def finish(srcs, bufs, lands, ss, rs):
        send, recv = copies(bufs, ss, rs)
        for cp in recv:
            cp.wait_recv()
        for cp in send:
            cp.wait_send()

    return Comm(bufs=stacks, n_sems=3 * n, start=start, finish=finish)


def rs_pair_comm(gs):
    n = len(gs)

    def copies(srcs, lands, ss, rs):
        mx, my, mc = _place()
        out = []
        for p in range(n):
            hr = gs[p].shape[1] // 2
            out.append(_remote(srcs[p].at[:, pl.ds((1 - mc) * hr, hr), :], lands[p], ss.at[p], rs.at[p], (mx, my, 1 - mc)))
        return out

    def start(srcs, bufs, lands, ss, rs):
        for cp in copies(srcs, lands, ss, rs):
            cp.start()

    def finish(srcs, bufs, lands, ss, rs):
        for cp in copies(srcs, lands, ss, rs):
            cp.wait()

    return Comm(srcs=gs, land_shapes=[jax.ShapeDtypeStruct((g.shape[0], g.shape[1] // 2, g.shape[2]), g.dtype) for g in gs],
                n_sems=n, start=start, finish=finish)


def rs_chip_comm(ss_):
    n = len(ss_)

    def copies(srcs, lands, ss, rs):
        mx, my, mc = _place()
        out = []
        for p in range(n):
            for k, chip in enumerate(_other_chips(mx, my)):
                out.append(_remote(srcs[p].at[2 * chip[0] + chip[1]], lands[p].at[k], ss.at[3 * p + k], rs.at[3 * p + k],
                                   (chip[0], chip[1], mc)))
        return out

    def start(srcs, bufs, lands, ss, rs):
        for cp in copies(srcs, lands, ss, rs):
            cp.start()

    def finish(srcs, bufs, lands, ss, rs):
        for cp in copies(srcs, lands, ss, rs):
            cp.wait()

    return Comm(srcs=ss_, land_shapes=[jax.ShapeDtypeStruct((N_QUAD - 1,) + s.shape[1:], s.dtype) for s in ss_],
                n_sems=3 * n, start=start, finish=finish)


def _chip_exchange_copies(s_ref, land_ref, sems):
    mx, my, mc = _place()
    return [_remote(s_ref.at[2 * chip[0] + chip[1]], land_ref.at[k], sems[k], sems[N_QUAD - 1 + k], (chip[0], chip[1], mc))
            for k, chip in enumerate(_other_chips(mx, my))]


def chip_exchange_start(s, *, name):
    hbm = pl.BlockSpec(memory_space=pltpu.HBM)
    sem = pl.BlockSpec(memory_space=pltpu.SEMAPHORE)
    n_sem = 2 * (N_QUAD - 1)
    land = jax.ShapeDtypeStruct((N_QUAD - 1,) + s.shape[1:], s.dtype)

    def body(s_ref, land_ref, *rest):
        sems, token = rest[:n_sem], rest[n_sem + 2]
        for cp in _chip_exchange_copies(s_ref, land_ref, sems):
            cp.start()
        token[...] = jnp.zeros_like(token)

    res = pl.pallas_call(
        body, name=name,
        out_shape=(*[pltpu.SemaphoreType.DMA(())] * n_sem, pltpu.HBM(s.shape, s.dtype), pltpu.HBM(land.shape, land.dtype),
                   jax.ShapeDtypeStruct((8, 128), F32)),
        in_specs=(hbm, hbm), out_specs=(*[sem] * n_sem, hbm, hbm, pl.BlockSpec(memory_space=pltpu.VMEM)),
        input_output_aliases={0: n_sem, 1: n_sem + 1},
        compiler_params=pltpu.CompilerParams(has_side_effects=pltpu.SideEffectType.DATAFLOW_SIDE_EFFECTING),
    )(pltpu.with_memory_space_constraint(s, pltpu.HBM),
      pltpu.with_memory_space_constraint(lax.empty(land.shape, land.dtype), pltpu.HBM))
    return res[:n_sem], res[n_sem], res[n_sem + 1], res[n_sem + 2]


def chip_exchange_wait(sems, s_thru, land_thru, after, *, name):
    hbm = pl.BlockSpec(memory_space=pltpu.HBM)
    sem = pl.BlockSpec(memory_space=pltpu.SEMAPHORE)
    n_sem = len(sems)

    def body(s_ref, land_ref, *rest):
        for cp in _chip_exchange_copies(s_ref, land_ref, rest[:n_sem]):
            cp.wait_send()
            cp.wait_recv()

    return pl.pallas_call(
        body, name=name, out_shape=(pltpu.HBM(s_thru.shape, s_thru.dtype), pltpu.HBM(land_thru.shape, land_thru.dtype)),
        in_specs=(hbm, hbm, *[sem] * n_sem, pl.BlockSpec(memory_space=pl.ANY)), out_specs=(hbm, hbm),
        input_output_aliases={0: 0, 1: 1},
        compiler_params=pltpu.CompilerParams(has_side_effects=pltpu.SideEffectType.DATAFLOW_SIDE_EFFECTING),
    )(s_thru, land_thru, *sems, after)


def rs_share_comm(fulls):
    n = len(fulls)

    def copies(bufs, ss, rs, only_sends=False):
        mx, my, mc = _place()
        send, recv = [], []
        for p in range(n):
            hr = fulls[p].shape[0] // 2
            mine, other = bufs[p].at[pl.ds(mc * hr, hr), :], bufs[p].at[pl.ds((1 - mc) * hr, hr), :]
            send.append(_remote(mine, mine, ss.at[p], rs.at[p], (mx, my, 1 - mc)))
            if not only_sends:
                recv.append(_remote(other, other, ss.at[p], rs.at[p], (mx, my, 1 - mc)))
        return send, recv

    def start(srcs, bufs, lands, ss, rs):
        for cp in copies(bufs, ss, rs, only_sends=True)[0]:
            cp.start()

    def finish(srcs, bufs, lands, ss, rs):
        send, recv = copies(bufs, ss, rs)
        for cp in recv:
            cp.wait_recv()
        for cp in send:
            cp.wait_send()

    return Comm(bufs=fulls, n_sems=n, start=start, finish=finish)


def pair_sums(gs, recvs, mc, *, name):
    n = len(gs)

    def body(s_ref, *refs):
        for p in range(n):
            g_ref, r_ref, o_ref = refs[2 * p], refs[2 * p + 1], refs[2 * n + p]
            o_ref[...] = (g_ref[...].astype(F32) + r_ref[...].astype(F32)).astype(BF16)

    in_specs, out_specs, out_shape, args = [], [], [], []
    for g, r in zip(gs, recvs):
        nq, hr, C = r.shape
        in_specs += [pl.BlockSpec((1, hr, C), lambda q, s: (q, s[0], 0)), pl.BlockSpec((1, hr, C), lambda q, s: (q, 0, 0))]
        out_specs.append(pl.BlockSpec((1, hr, C), lambda q, s: (q, 0, 0)))
        out_shape.append(jax.ShapeDtypeStruct((nq, hr, C), BF16))
        args += [g, r]
    return _call(body, name=name, grid=(N_QUAD,), in_specs=in_specs, out_specs=out_specs, out_shape=out_shape,
                 args=args, prefetch=[mc])


def chip_sums(ss, recvs, quad_mc, *, name):
    n = len(ss)
    steps = 2

    def body(q_ref, *refs):
        for p in range(n):
            s_ref, r_ref, o_ref = refs[2 * p], refs[2 * p + 1], refs[2 * n + p]
            total = s_ref[0].astype(F32)
            for k in range(N_QUAD - 1):
                total = total + r_ref[k].astype(F32)
            o_ref[...] = total

    in_specs, out_specs, out_shape, args = [], [], [], []
    for s, r in zip(ss, recvs):
        _, hr, C = s.shape
        rb = hr // steps
        in_specs += [pl.BlockSpec((1, rb, C), lambda i, q: (q[0], i, 0)),
                     pl.BlockSpec((N_QUAD - 1, rb, C), lambda i, q: (0, i, 0))]
        out_specs.append(pl.BlockSpec((rb, C), lambda i, q: (q[1] * steps + i, 0)))
        out_shape.append(jax.ShapeDtypeStruct((2 * hr, C), F32))
        args += [s, r]
    return _call(body, name=name, grid=(steps,), in_specs=in_specs, out_specs=out_specs, out_shape=out_shape,
                 args=args, prefetch=[quad_mc])


def _stack_cols(w_full):
    K, N = w_full.shape
    return w_full.reshape(K, N_QUAD, N // N_QUAD).transpose(1, 0, 2)


def _unstack_cols(w4):
    nq, K, n = w4.shape
    return w4.transpose(1, 0, 2).reshape(K, nq * n)


def kernel(x, c, w_ada, b_ada, g_norm1, ffn1_w_gate, ffn1_w_up, ffn1_w_down, g_norm2, w_in, g_sgu_ln, b_sgu_ln, w_spatial, b_spatial, g_q, g_k, attn_sinks, w_branch_a, w_branch_b, w_out, g_norm3, ffn2_w_gate, ffn2_w_up, ffn2_w_down, loss_target, m_w_ada, m_b_ada, m_g_norm1, m_ffn1_w_gate, m_ffn1_w_up, m_ffn1_w_down, m_g_norm2, m_w_in, m_g_sgu_ln, m_b_sgu_ln, m_w_spatial, m_b_spatial, m_g_q, m_g_k, m_attn_sinks, m_w_branch_a, m_w_branch_b, m_w_out, m_g_norm3, m_ffn2_w_gate, m_ffn2_w_up, m_ffn2_w_down, v_w_ada, v_b_ada, v_g_norm1, v_ffn1_w_gate, v_ffn1_w_up, v_ffn1_w_down, v_g_norm2, v_w_in, v_g_sgu_ln, v_b_sgu_ln, v_w_spatial, v_b_spatial, v_g_q, v_g_k, v_attn_sinks, v_w_branch_a, v_w_branch_b, v_w_out, v_g_norm3, v_ffn2_w_gate, v_ffn2_w_up, v_ffn2_w_down):
    B, S, D = x.shape
    T = B * S
    n_mod = b_ada.shape[1] // D
    mx, my, mc = _place()
    quad = 2 * mx + my
    mc_arr = jnp.reshape(mc, (1,)).astype(jnp.int32)
    quad_arr = jnp.reshape(quad, (1,)).astype(jnp.int32)
    quad_mc = jnp.stack([quad, mc]).astype(jnp.int32)
    tm = min(512, S)
    tm_big = min(1024, S)
    tt_half = min(2048, T)
    cpb = min(4, S // CHUNK)

    xt = x.reshape(T, D)
    tgt = loss_target.reshape(T, D)

    tr = lambda a: jnp.swapaxes(a, 1, 2)
    gather_cond = ag8_comm(c)
    stack_wg1, stack_wu1 = cast_into_stacks([tr(ffn1_w_gate)[0], tr(ffn1_w_up)[0]], quad_arr, name="stack_gu1",
                                            comms=[gather_cond])
    c_all = gather_cond.lands[0].reshape(N_DEV * B, D)
    n_ada = w_ada.shape[2]
    b_mine = lax.dynamic_slice(b_ada, (0, quad * n_ada), (1, n_ada))
    mods_part = ada_fwd(c_all, w_ada[0], b_mine, name="ada_fwd")
    gather_mods = ag8_comm(mods_part)
    gather_wg1, gather_wu1 = gather_comm([stack_wg1], forward=False), gather_comm([stack_wu1])
    later = [ffn1_w_down, tr(w_in), w_branch_a, w_branch_b, w_out, tr(ffn2_w_gate), tr(ffn2_w_up), ffn2_w_down]
    stacks = cast_into_stacks([w[0] for w in later], quad_arr, name="stack_rest", comms=[gather_wg1, gather_mods])
    forward_wg1 = forward_comm(gather_wg1.bufs_out)
    (mods_parts,) = gather_mods.lands
    gather_wd1, gather_win = gather_comm([stacks[0]]), gather_comm([stacks[1]])
    gather_abo = gather_comm(stacks[2:5], forward=False)
    gather_wg3, gather_wu3 = gather_comm([stacks[5]], forward=False), gather_comm([stacks[6]], forward=False)
    gather_wd3 = gather_comm([stacks[7]])
    mods = jnp.concatenate([mods_parts[2 * j] for j in range(N_QUAD)], axis=1)
    me = 4 * mx + 2 * my + mc
    mods = lax.dynamic_slice(mods, (me * B, 0), (B, n_mod * D))
    mods = mods.reshape(B, n_mod, 1, D)
    sh1, sc1, ga1, sh2, sc2, ga2, sh3, sc3, ga3 = [(mods, j) for j in range(n_mod)]
    bs_t = b_spatial[0].T
    ws = w_spatial[0]

    xn1 = norm_mod_fwd(xt, g_norm1, sc1, sh1, seq=S, tm=tm, name="norm1", comms=[gather_wu1, forward_wg1])
    (wu1,), (wg1,) = gather_wu1.bufs_out, forward_wg1.bufs_out
    g1, u1, a1 = ffn_up(xn1, wg1, wu1, tm=tm_big, name="ffn1_up", comms=[gather_wd1, gather_abo])
    (wd1,) = gather_wd1.bufs_out
    forward_abo = forward_comm(gather_abo.bufs_out)
    h1, f1, xn2 = ffn_down(a1, wd1, xt, ga1, norm=(g_norm2, sc2, sh2), seq=S, tm=tm, name="ffn1_down",
                           comms=[gather_win, forward_abo])
    (win4,), (wa4, wb4, wo4) = gather_win.bufs_out, forward_abo.bufs_out
    wa, wb = _unstack_cols(wa4), _unstack_cols(wb4)
    wo = wo4.reshape(D, D)
    win = win4.reshape(-1, D)
    groups = [(0, 2 * D_A), (2 * D_A, 2 * D_A + QKV_W), (2 * D_A + QKV_W, win.shape[0])]
    puv, pqkv, pgt = proj_fwd(xn2, win, groups, tm=tm, name="proj", comms=[gather_wg3])
    forward_wg3 = forward_comm(gather_wg3.bufs_out)
    sgu = sgu_fwd(puv, g_sgu_ln, b_sgu_ln, ws, bs_t, cpb=cpb, name="sgu_fwd")
    gq_t, gk_t = jnp.tile(g_q, (1, N_Q)), jnp.tile(g_k, (1, N_KV))
    att = attn_fwd(pqkv, gq_t, gk_t, attn_sinks, seq=S, name="attn_fwd", comms=[gather_wu3, forward_wg3])
    (wg3,) = forward_wg3.bufs_out
    forward_wu3 = forward_comm(gather_wu3.bufs_out)
    ya, yb, merged, mm, h2, xn3 = mixer_out_fwd(sgu, att, pgt, h1, ga2, wa, wb, wo, (g_norm3, sc3, sh3), seq=S,
                                                tm=tm, name="mixer_out", comms=[gather_wd3, forward_wu3])
    (wd3,), (wu3,) = gather_wd3.bufs_out, forward_wu3.bufs_out
    g3, u3, a3 = ffn_up(xn3, wg3, wu3, tm=tm_big, name="ffn2_up")
    dy, f3, lparts = ffn_down(a3, wd3, h2, ga3, target=tgt, seq=S, tm=tm, name="ffn2_down_loss")
    loss_part = jnp.sum(lparts[:, 0, 0])

    def sums_of(pair, tag):
        return pair_sums(pair.srcs, pair.lands, mc_arr, name=f"pair_sum_{tag}")

    def halves_of(*chips_and_tag):
        *chips, tag = chips_and_tag
        return chip_sums([s for ch in chips for s in ch.srcs], [r for ch in chips for r in ch.lands], quad_mc,
                         name=f"chip_sum_{tag}")

    dg3, du3, df3, dga3 = ffn_bwd_act(dy, f3, ga3, wd3, g3, u3, seq=S, tm=tm, name="ffn2_act_bwd")
    (dwd3,) = mm_tn([(a3, df3)], tt=T, name="ffn2_dwd")
    pair_wd3 = rs_pair_comm([dwd3])
    dwg3, dwu3 = mm_tn([(dg3, xn3), (du3, xn3)], tt=tt_half, name="ffn2_dwgu", comms=[pair_wd3])
    chip_wd3 = rs_chip_comm(sums_of(pair_wd3, "wd3"))
    pair_gu3 = rs_pair_comm([dwg3, dwu3])
    dh2, dsh3, dsc3, dgn3 = dx_norm_bwd([dg3, du3], [wg3, wu3], h2, dy, g_norm3, sc3, seq=S, tm=tm, name="ffn2_dx",
                                        comms=[chip_wd3, pair_gu3])
    sum_wg3, sum_wu3 = sums_of(pair_gu3, "gu3")
    chip_wg3, chip_wu3 = rs_chip_comm([sum_wg3]), rs_chip_comm([sum_wu3])

    dgt, dya, dyb, dsgu, datt, dm, dga2 = mixer_out_bwd(dh2, mm, ga2, ya, yb, pgt, wa, wb, wo, seq=S, tm=tm,
                                                        name="mixer_out_bwd", comms=[chip_wg3])
    dwo, dwa, dwb = mm_tn([(merged, dm), (sgu, dya), (att, dyb)], tt=tm_big, name="mixer_dw")
    pair_abo = rs_pair_comm([_stack_cols(dwa), _stack_cols(dwb), dwo.reshape(N_QUAD, D // N_QUAD, D)])
    duv, dws, dzs, dgln, dbln = sgu_bwd(puv, dsgu, g_sgu_ln, b_sgu_ln, ws, bs_t, cpb=cpb, name="sgu_bwd",
                                        comms=[pair_abo])
    chip_abo = rs_chip_comm(sums_of(pair_abo, "abo"))
    dqkv, dgq_h, dgk_h, dsk = attn_bwd(pqkv, datt, gq_t, gk_t, attn_sinks, seq=S, name="attn_bwd",
                                       comms=[chip_wu3, chip_abo])
    dgq = dgq_h.reshape(N_Q, HEAD_DIM).sum(axis=0, keepdims=True)
    dgk = dgk_h.reshape(N_KV, HEAD_DIM).sum(axis=0, keepdims=True)
    share3 = rs_share_comm(halves_of(chip_wg3, chip_wu3, chip_wd3, "ffn2"))
    dwin_uv, dwin_qkv = mm_tn([(duv, xn2), (dqkv, xn2)], tt=tt_half, name="mixer_dwin_a", comms=[share3])
    (dwin_gt,) = mm_tn([(dgt, xn2)], tt=tt_half, name="mixer_dwin_b")
    dwin_parts = [dwin_uv, dwin_qkv, dwin_gt]
    r_wg3, r_wu3, r_wd3 = share3.bufs_out
    pair_win = rs_pair_comm([jnp.concatenate(dwin_parts, axis=0).reshape(win4.shape)])
    dh1, dsh2, dsc2, dgn2 = dx_norm_bwd([duv, dqkv, dgt], [(win, r0, r1) for r0, r1 in groups], h1, dh2, g_norm2, sc2, seq=S,
                                        tm=tm, name="mixer_dx", comms=[pair_win])
    chip_win = rs_chip_comm(sums_of(pair_win, "win"))

    dg1, du1, df1, dga1 = ffn_bwd_act(dh1, f1, ga1, wd1, g1, u1, seq=S, tm=tm, name="ffn1_act_bwd", comms=[chip_win])
    share_mix = rs_share_comm(halves_of(chip_win, chip_abo, "mix"))

    db_s = dzs.reshape(CHUNK, N_GROUPS, D_A // N_GROUPS).sum(axis=2).T.reshape(1, N_GROUPS * CHUNK)
    tail = jnp.concatenate([db_s, dgq, dgk, dsk[0:1, 0:N_Q], loss_part.reshape(1, 1)], axis=1)
    tail = jnp.pad(tail, ((0, 0), (0, (-tail.shape[1]) % D)))
    lnrow = jnp.concatenate([dgln, dbln], axis=1)
    lnrow = jnp.pad(lnrow, ((0, 0), (0, (-lnrow.shape[1]) % D)))
    packed = jnp.concatenate([dgn2, dgn3, lnrow.reshape(-1, D), tail.reshape(-1, D), dws.reshape(-1, D)], axis=0)
    n_rows = packed.shape[0]
    packed = jnp.pad(packed, ((0, (-n_rows) % 8), (0, 0)))
    gather_small = ag8_comm(packed)

    dwg1, dwu1 = mm_tn([(dg1, xn1), (du1, xn1)], tt=tt_half, name="ffn1_dwgu", comms=[share_mix, gather_small])
    r_win, r_wa, r_wb, r_wo = share_mix.bufs_out
    small = sum8(gather_small.lands[0], name="sum_small")
    pair_gu1 = rs_pair_comm([dwg1, dwu1])
    (dwd1,) = mm_tn([(a1, df1)], tt=T, name="ffn1_dwd", comms=[pair_gu1])
    chip_gu1 = rs_chip_comm(sums_of(pair_gu1, "gu1"))
    pair_wd1 = rs_pair_comm([dwd1])
    dx, dsh1, dsc1, dgn1 = dx_norm_bwd([dg1, du1], [wg1, wu1], xt, dh1, g_norm1, sc1, seq=S, tm=tm, name="ffn1_dx",
                                       comms=[chip_gu1, pair_wd1])
    (sum_wd1,) = sums_of(pair_wd1, "wd1")
    wd1_sems, wd1_sums, wd1_land, wd1_token = chip_exchange_start(sum_wd1, name="rs_wd1_start")
    share_gu1 = rs_share_comm(halves_of(chip_gu1, "gu1"))

    names = ["w_ada", "b_ada", "g_norm1", "ffn1_w_gate", "ffn1_w_up", "ffn1_w_down", "g_norm2", "w_in", "g_sgu_ln",
             "b_sgu_ln", "w_spatial", "b_spatial", "g_q", "g_k", "attn_sinks", "w_branch_a", "w_branch_b", "w_out",
             "g_norm3", "ffn2_w_gate", "ffn2_w_up", "ffn2_w_down"]
    weights = dict(zip(names, [w_ada, b_ada, g_norm1, ffn1_w_gate, ffn1_w_up, ffn1_w_down, g_norm2, w_in, g_sgu_ln,
                               b_sgu_ln, w_spatial, b_spatial, g_q, g_k, attn_sinks, w_branch_a, w_branch_b, w_out,
                               g_norm3, ffn2_w_gate, ffn2_w_up, ffn2_w_down]))
    m_in = dict(zip(names, [m_w_ada, m_b_ada, m_g_norm1, m_ffn1_w_gate, m_ffn1_w_up, m_ffn1_w_down, m_g_norm2, m_w_in,
                            m_g_sgu_ln, m_b_sgu_ln, m_w_spatial, m_b_spatial, m_g_q, m_g_k, m_attn_sinks, m_w_branch_a,
                            m_w_branch_b, m_w_out, m_g_norm3, m_ffn2_w_gate, m_ffn2_w_up, m_ffn2_w_down]))
    v_in = dict(zip(names, [v_w_ada, v_b_ada, v_g_norm1, v_ffn1_w_gate, v_ffn1_w_up, v_ffn1_w_down, v_g_norm2, v_w_in,
                            v_g_sgu_ln, v_b_sgu_ln, v_w_spatial, v_b_spatial, v_g_q, v_g_k, v_attn_sinks, v_w_branch_a,
                            v_w_branch_b, v_w_out, v_g_norm3, v_ffn2_w_gate, v_ffn2_w_up, v_ffn2_w_down]))
    transposed = ("ffn1_w_gate", "ffn1_w_up", "w_in", "ffn2_w_gate", "ffn2_w_up")
    grads, delta, new_m, new_v = {}, {}, {}, {}

    def update(group, steps, name, comms=()):
        form = lambda nm, a: (tr(a) if nm in transposed else a)[0].reshape(group[nm].shape)
        res = adamw([(form(nm, weights[nm]), g, form(nm, m_in[nm]), form(nm, v_in[nm])) for nm, g in group.items()],
                    steps=steps, name=name, comms=comms)
        back = lambda nm, a: tr(a[None]) if nm in transposed else a.reshape(weights[nm].shape)
        for nm, (g, d, mn, vn) in zip(group, res):
            grads[nm], delta[nm], new_m[nm], new_v[nm] = back(nm, g), back(nm, d), back(nm, mn), back(nm, vn)

    dmods = jnp.concatenate([dsh1, dsc1, dga1, dsh2, dsc2, dga2, dsh3, dsc3, dga3], axis=1)
    dmods = jnp.concatenate([dmods, jnp.pad(dgn1, ((0, 0), (0, (n_mod - 1) * D)))], axis=0)
    gather_dmods = ag8_comm(dmods)
    run_comms([share_gu1, gather_dmods], name="rs_tail")
    update({"ffn2_w_gate": r_wg3, "ffn2_w_up": r_wu3, "ffn2_w_down": r_wd3, "w_out": r_wo + wd1_token[0, 0],
            "w_branch_a": r_wa, "w_branch_b": r_wb}, 8, "adamw_early")
    wd1_sums, wd1_land = chip_exchange_wait(wd1_sems, wd1_sums, wd1_land, delta["w_out"], name="rs_wd1_wait")
    r_wg1, r_wu1 = share_gu1.bufs_out
    (gathered,) = gather_dmods.lands
    dmods_all = gathered[:, 0:B].reshape(N_DEV * B, n_mod * D)
    dmods_mine = lax.dynamic_slice(dmods_all, (0, quad * n_ada), (N_DEV * B, n_ada))
    share_wd1 = rs_share_comm(chip_sums([wd1_sums], [wd1_land], quad_mc, name="chip_sum_wd1"))
    db_ada, dw_ada, g_gn1 = ada_bwd(c_all, dmods_all, dmods_mine, gathered[:, B, 0:D], name="ada_bwd", comms=[share_wd1])
    (r_wd1,) = share_wd1.bufs_out

    ln_rows = lnrow.size // D
    tail_rows = tail.size // D
    r = 2
    g_gn2, g_gn3 = small[0:1], small[1:2]
    ln_flat = small[r:r + ln_rows].reshape(1, -1)
    r += ln_rows
    tail_flat = small[r:r + tail_rows].reshape(1, -1)
    r += tail_rows
    g_ws = small[r:r + dws.size // D].reshape(w_spatial.shape)
    g_gln, g_bln = ln_flat[:, 0:D_A], ln_flat[:, D_A:2 * D_A]
    o = N_GROUPS * CHUNK
    g_bs = tail_flat[:, 0:o].reshape(b_spatial.shape)
    g_gq, g_gk, g_sk = tail_flat[:, o:o + HEAD_DIM], tail_flat[:, o + HEAD_DIM:o + 2 * HEAD_DIM], tail_flat[:, o + 2 * HEAD_DIM:o + 2 * HEAD_DIM + N_Q]
    loss = tail_flat[0, o + 2 * HEAD_DIM + N_Q]

    update({"w_ada": dw_ada}, 16, "adamw_w_ada")
    update({"ffn1_w_gate": r_wg1, "ffn1_w_up": r_wu1, "ffn1_w_down": r_wd1, "w_in": r_win}, 8, "adamw_late")

    update({"b_ada": db_ada, "g_norm1": g_gn1, "g_norm2": g_gn2, "g_norm3": g_gn3, "g_sgu_ln": g_gln,
            "b_sgu_ln": g_bln, "w_spatial": g_ws.reshape(-1, CHUNK), "b_spatial": g_bs.reshape(N_GROUPS, CHUNK),
            "g_q": g_gq, "g_k": g_gk, "attn_sinks": g_sk}, 1, "adamw_small")

    return (loss, dx.reshape(B, S, D), *[grads[nm] for nm in names], *[delta[nm] for nm in names],
            *[new_m[nm] for nm in names], *[new_v[nm] for nm in names])
```
